```python
import math
import jax
import jax.numpy as jnp
from jax import lax
import numpy as np

D_MODEL = 1024
BATCH = 16
SEQ = 2048
DEPTH = 2

HEAD_DIM = 64
SB_HEADS = 4
MLA_HEADS = 6
MLA_Q_RANK = 256
MLA_KV_RANK = 128
MLA_NOPE = 64
MLA_ROPE = 32
MLA_V = 64
MLA_QK = MLA_NOPE + MLA_ROPE
ROPE_THETA = 10000.0
SW_HEADS = 6
SW_KV_HEADS = 2
WINDOW = 128
REL_BUCKETS = 32
REL_MAX_DIST = 128
BLOCK = 128
D_FF = 2816
CONV_W = 3
EPS = 1e-6
NEG = -1e30

D_MIX = SB_HEADS * HEAD_DIM + MLA_HEADS * MLA_V + SW_HEADS * HEAD_DIM
IN_SPLITS = (SB_HEADS * HEAD_DIM, SB_HEADS * HEAD_DIM, SB_HEADS * HEAD_DIM,
             MLA_Q_RANK, MLA_KV_RANK, MLA_ROPE,
             SW_HEADS * HEAD_DIM, SW_KV_HEADS * HEAD_DIM, SW_KV_HEADS * HEAD_DIM)
D_IN = 3 * SB_HEADS * HEAD_DIM + MLA_Q_RANK + MLA_KV_RANK + MLA_ROPE + (SW_HEADS + 2 * SW_KV_HEADS) * HEAD_DIM

kernel_name = 'hybrid_sb_mla_swa_convffn_block'


def rms_norm(x, g):
    xf = x.astype(jnp.float32)
    y = xf * lax.rsqrt(jnp.mean(xf * xf, axis=-1, keepdims=True) + EPS)
    return (y * g.astype(jnp.float32)).astype(x.dtype)


def split_cols(t, sizes):
    out = []
    start = 0
    for n in sizes:
        out.append(t[..., start:start + n])
        start += n
    return out


def apply_rope(x, positions):
    half = x.shape[-1] // 2
    inv_freq = jnp.power(ROPE_THETA, -jnp.arange(half, dtype=jnp.float32) / half)
    ang = positions.astype(jnp.float32)[..., None] * inv_freq
    cos = jnp.cos(ang)[:, :, None, :]
    sin = jnp.sin(ang)[:, :, None, :]
    x1 = x[..., :half].astype(jnp.float32)
    x2 = x[..., half:].astype(jnp.float32)
    out = jnp.concatenate([x1 * cos - x2 * sin, x1 * sin + x2 * cos], axis=-1)
    return out.astype(x.dtype)


def t5_causal_bucket(dist):
    max_exact = REL_BUCKETS // 2
    n = jnp.maximum(dist, 0)
    nf = jnp.maximum(n, 1).astype(jnp.float32)
    large = max_exact + (jnp.log(nf / max_exact) / math.log(REL_MAX_DIST / max_exact)
                         * (REL_BUCKETS - max_exact)).astype(jnp.int32)
    large = jnp.minimum(large, REL_BUCKETS - 1)
    return jnp.where(n < max_exact, n, large)


def window_rel_bias(rel_table):
    a = jnp.arange(BLOCK)[:, None]
    b = jnp.arange(2 * BLOCK)[None, :]
    bucket = t5_causal_bucket(BLOCK + a - b)
    return jnp.transpose(rel_table[bucket], (2, 0, 1))


def stick_breaking_attention(q, k, v):
    B, S, H, D = q.shape
    scale = D ** -0.5
    outs = []
    for i in range(S // BLOCK):
        t0 = i * BLOCK
        end = t0 + BLOCK
        z = jnp.einsum('bqhd,bkhd->bhqk', q[:, t0:end], k[:, :end]).astype(jnp.float32) * scale
        strict = jnp.arange(end)[None, :] < (t0 + jnp.arange(BLOCK))[:, None]
        log_keep = jnp.where(strict, -jax.nn.softplus(z), 0.0)
        suffix = lax.cumsum(log_keep, axis=log_keep.ndim - 1, reverse=True) - log_keep
        weights = jnp.where(strict, jnp.exp(jax.nn.log_sigmoid(z) + suffix), 0.0)
        outs.append(jnp.einsum('bhqk,bkhd->bqhd', weights.astype(v.dtype), v[:, :end]))
    return jnp.concatenate(outs, axis=1)


def causal_softmax_attention(q, k, v):
    B, S, H, Dk = q.shape
    scale = Dk ** -0.5
    outs = []
    for i in range(S // BLOCK):
        t0 = i * BLOCK
        end = t0 + BLOCK
        s = jnp.einsum('bqhd,bkhd->bhqk', q[:, t0:end], k[:, :end]).astype(jnp.float32) * scale
        causal = jnp.arange(end)[None, :] <= (t0 + jnp.arange(BLOCK))[:, None]
        p = jax.nn.softmax(jnp.where(causal, s, NEG), axis=-1)
        outs.append(jnp.einsum('bhqk,bkhd->bqhd', p.astype(v.dtype), v[:, :end]))
    return jnp.concatenate(outs, axis=1)


def sliding_window_sink_attention(q, k, v, sinks, rel_bias):
    B, S, H, D = q.shape
    G = k.shape[2]
    R = H // G
    nb = S // BLOCK
    qb = q.reshape(B, nb, BLOCK, G, R, D)

    def band(t):
        tp = jnp.concatenate([jnp.zeros((B, BLOCK, G, D), t.dtype), t], axis=1)
        tp = tp.reshape(B, nb + 1, BLOCK, G, D)
        return jnp.concatenate([tp[:, :-1], tp[:, 1:]], axis=2)

    kb = band(k)
    vb = band(v)
    s = jnp.einsum('bnqgrd,bnkgd->bngrqk', qb, kb).astype(jnp.float32) * (D ** -0.5)
    s = s + rel_bias.astype(jnp.float32).reshape(G, R, BLOCK, 2 * BLOCK)[None, None]
    dist = BLOCK + jnp.arange(BLOCK)[:, None] - jnp.arange(2 * BLOCK)[None, :]
    in_window = (dist >= 0) & (dist < WINDOW)
    key_pos = (jnp.arange(nb)[:, None] - 1) * BLOCK + jnp.arange(2 * BLOCK)[None, :]
    valid = in_window[None] & (key_pos >= 0)[:, None, :]
    s = jnp.where(valid[None, :, None, None], s, NEG)
    sink = jnp.broadcast_to(sinks.astype(jnp.float32).reshape(1, 1, G, R, 1, 1), s.shape[:-1] + (1,))
    p = jax.nn.softmax(jnp.concatenate([s, sink], axis=-1), axis=-1)[..., :-1]
    o = jnp.einsum('bngrqk,bnkgd->bnqgrd', p.astype(v.dtype), vb)
    return o.reshape(B, S, H, D)


def causal_depthwise_conv(u, w, b):
    C = u.shape[-1]
    y = lax.conv_general_dilated(u, w[:, None, :].astype(u.dtype), window_strides=(1,),
                                 padding=[(CONV_W - 1, 0)],
                                 dimension_numbers=('NWC', 'WIO', 'NWC'),
                                 feature_group_count=C)
    return y + b


def hybrid_layer(x, cond, positions, rel_bias, norm1_g, norm2_g, w_ada, b_ada, w_in,
                 mla_cq_g, w_uq, mla_ckv_g, w_ukv, mla_qn_g, mla_kn_g, sw_qn_g, sw_kn_g,
                 sw_sinks, w_out, w_up, conv_w, conv_b, w_down):
    B, S, _ = x.shape
    mods = jnp.einsum('bd,de->be', jax.nn.silu(cond), w_ada) + b_ada
    shift1, scale1, gate1, shift2, scale2, gate2 = jnp.split(mods[:, None, :], 6, axis=-1)

    h = rms_norm(x, norm1_g) * (1.0 + scale1) + shift1
    proj = jnp.einsum('bsd,de->bse', h, w_in)
    sb_q, sb_k, sb_v, cq, ckv, k_rope, sw_q, sw_k, sw_v = split_cols(proj, IN_SPLITS)

    sb_shape = (B, S, SB_HEADS, HEAD_DIM)
    o_a = stick_breaking_attention(sb_q.reshape(sb_shape), sb_k.reshape(sb_shape), sb_v.reshape(sb_shape))

    q_b = jnp.einsum('bsr,re->bse', rms_norm(cq, mla_cq_g), w_uq).reshape(B, S, MLA_HEADS, MLA_QK)
    kv_b = jnp.einsum('bsr,re->bse', rms_norm(ckv, mla_ckv_g), w_ukv).reshape(B, S, MLA_HEADS, MLA_NOPE + MLA_V)
    k_nope = kv_b[..., :MLA_NOPE]
    v_b = kv_b[..., MLA_NOPE:]
    k_rope_h = jnp.broadcast_to(k_rope[:, :, None, :], (B, S, MLA_HEADS, MLA_ROPE))
    k_b = jnp.concatenate([k_nope, k_rope_h], axis=-1)
    q_b = rms_norm(q_b, mla_qn_g)
    k_b = rms_norm(k_b, mla_kn_g)
    q_b = jnp.concatenate([q_b[..., :MLA_NOPE], apply_rope(q_b[..., MLA_NOPE:], positions)], axis=-1)
    k_b = jnp.concatenate([k_b[..., :MLA_NOPE], apply_rope(k_b[..., MLA_NOPE:], positions)], axis=-1)
    o_b = causal_softmax_attention(q_b, k_b, v_b)

    q_c = rms_norm(sw_q.reshape(B, S, SW_HEADS, HEAD_DIM), sw_qn_g)
    k_c = rms_norm(sw_k.reshape(B, S, SW_KV_HEADS, HEAD_DIM), sw_kn_g)
    v_c = sw_v.reshape(B, S, SW_KV_HEADS, HEAD_DIM)
    o_c = sliding_window_sink_attention(q_c, k_c, v_c, sw_sinks, rel_bias)

    mix = jnp.concatenate([o_a.reshape(B, S, -1), o_b.reshape(B, S, -1), o_c.reshape(B, S, -1)], axis=-1)
    x = x + gate1 * jnp.einsum('bse,ed->bsd', mix, w_out)

    h2 = rms_norm(x, norm2_g) * (1.0 + scale2) + shift2
    u = causal_depthwise_conv(jnp.einsum('bsd,df->bsf', h2, w_up), conv_w, conv_b)
    g = u[..., :D_FF]
    val = u[..., D_FF:]
    y = jnp.einsum('bsf,fd->bsd', jax.nn.silu(g) * val, w_down)
    return x + gate2 * y


def _fwd_setup_inputs(seed: int = 0) -> dict:
    key = jax.random.key(seed)
    ks = jax.random.split(key, 24)
    f32 = jnp.float32
    L = DEPTH
    D = D_MODEL

    def nrm(k, shape, scale):
        return jax.random.normal(k, shape, f32) * scale

    def gain(k, shape):
        return 1.0 + 0.02 * jax.random.normal(k, shape, f32)

    x = nrm(ks[0], (BATCH, SEQ, D), 1.0)
    c = nrm(ks[1], (BATCH, D), 1.0)
    offsets = jax.random.randint(ks[2], (BATCH, 1), 0, SEQ, dtype=jnp.int32)
    positions = offsets + jnp.arange(SEQ, dtype=jnp.int32)[None, :]
    rel_table = nrm(ks[3], (REL_BUCKETS, SW_HEADS), 0.5)
    norm1_g = gain(ks[4], (L, D))
    norm2_g = gain(ks[5], (L, D))
    w_ada = nrm(ks[6], (L, D, 6 * D), 0.5 * D ** -0.5)
    b_ada = nrm(ks[7], (L, 6 * D), 0.02)
    w_in = nrm(ks[8], (L, D, D_IN), D ** -0.5)
    mla_cq_g = gain(ks[9], (L, MLA_Q_RANK))
    w_uq = nrm(ks[10], (L, MLA_Q_RANK, MLA_HEADS * MLA_QK), MLA_Q_RANK ** -0.5)
    mla_ckv_g = gain(ks[11], (L, MLA_KV_RANK))
    w_ukv = nrm(ks[12], (L, MLA_KV_RANK, MLA_HEADS * (MLA_NOPE + MLA_V)), MLA_KV_RANK ** -0.5)
    mla_qn_g = gain(ks[13], (L, MLA_QK))
    mla_kn_g = gain(ks[14], (L, MLA_QK))
    sw_qn_g = gain(ks[15], (L, HEAD_DIM))
    sw_kn_g = gain(ks[16], (L, HEAD_DIM))
    sw_sinks = nrm(ks[17], (L, SW_HEADS), 1.0)
    w_out = nrm(ks[18], (L, D_MIX, D), D_MIX ** -0.5)
    w_up = nrm(ks[19], (L, D, 2 * D_FF), D ** -0.5)
    conv_w = nrm(ks[20], (L, CONV_W, 2 * D_FF), CONV_W ** -0.5)
    conv_b = nrm(ks[21], (L, 2 * D_FF), 0.02)
    w_down = nrm(ks[22], (L, D_FF, D), D_FF ** -0.5)
    return {'x': x, 'c': c, 'positions': positions, 'rel_table': rel_table,
            'norm1_g': norm1_g, 'norm2_g': norm2_g, 'w_ada': w_ada, 'b_ada': b_ada,
            'w_in': w_in, 'mla_cq_g': mla_cq_g, 'w_uq': w_uq, 'mla_ckv_g': mla_ckv_g,
            'w_ukv': w_ukv, 'mla_qn_g': mla_qn_g, 'mla_kn_g': mla_kn_g,
            'sw_qn_g': sw_qn_g, 'sw_kn_g': sw_kn_g, 'sw_sinks': sw_sinks,
            'w_out': w_out, 'w_up': w_up, 'conv_w': conv_w, 'conv_b': conv_b,
            'w_down': w_down}


def _fwd_reference(x, c, positions, rel_table, norm1_g, norm2_g, w_ada, b_ada, w_in,
              mla_cq_g, w_uq, mla_ckv_g, w_ukv, mla_qn_g, mla_kn_g, sw_qn_g, sw_kn_g,
              sw_sinks, w_out, w_up, conv_w, conv_b, w_down):
    rel_bias = window_rel_bias(rel_table)
    for l in range(DEPTH):
        x = hybrid_layer(x, c, positions, rel_bias, norm1_g[l], norm2_g[l], w_ada[l], b_ada[l],
                         w_in[l], mla_cq_g[l], w_uq[l], mla_ckv_g[l], w_ukv[l], mla_qn_g[l],
                         mla_kn_g[l], sw_qn_g[l], sw_kn_g[l], sw_sinks[l], w_out[l], w_up[l],
                         conv_w[l], conv_b[l], w_down[l])
    return x


import jax as _jax
import jax.numpy as _jnp

TWIN_FORMAT = 'train_step'
FWD_PARAMS = ['x', 'c', 'positions', 'rel_table', 'norm1_g', 'norm2_g', 'w_ada', 'b_ada', 'w_in', 'mla_cq_g', 'w_uq', 'mla_ckv_g', 'w_ukv', 'mla_qn_g', 'mla_kn_g', 'sw_qn_g', 'sw_kn_g', 'sw_sinks', 'w_out', 'w_up', 'conv_w', 'conv_b', 'w_down']
TWIN_WEIGHTS = ['rel_table', 'norm1_g', 'norm2_g', 'w_ada', 'b_ada', 'w_in', 'mla_cq_g', 'w_uq', 'mla_ckv_g', 'w_ukv', 'mla_qn_g', 'mla_kn_g', 'sw_qn_g', 'sw_kn_g', 'sw_sinks', 'w_out', 'w_up', 'conv_w', 'conv_b', 'w_down']
TWIN_DIFF_INPUT = 'x'
TWIN_INPUTS = ['x', 'c', 'positions', 'rel_table', 'norm1_g', 'norm2_g', 'w_ada', 'b_ada', 'w_in', 'mla_cq_g', 'w_uq', 'mla_ckv_g', 'w_ukv', 'mla_qn_g', 'mla_kn_g', 'sw_qn_g', 'sw_kn_g', 'sw_sinks', 'w_out', 'w_up', 'conv_w', 'conv_b', 'w_down', 'loss_target', 'm_rel_table', 'm_norm1_g', 'm_norm2_g', 'm_w_ada', 'm_b_ada', 'm_w_in', 'm_mla_cq_g', 'm_w_uq', 'm_mla_ckv_g', 'm_w_ukv', 'm_mla_qn_g', 'm_mla_kn_g', 'm_sw_qn_g', 'm_sw_kn_g', 'm_sw_sinks', 'm_w_out', 'm_w_up', 'm_conv_w', 'm_conv_b', 'm_w_down', 'v_rel_table', 'v_norm1_g', 'v_norm2_g', 'v_w_ada', 'v_b_ada', 'v_w_in', 'v_mla_cq_g', 'v_w_uq', 'v_mla_ckv_g', 'v_w_ukv', 'v_mla_qn_g', 'v_mla_kn_g', 'v_sw_qn_g', 'v_sw_kn_g', 'v_sw_sinks', 'v_w_out', 'v_w_up', 'v_conv_w', 'v_conv_b', 'v_w_down']
TWIN_OUTPUTS = ['loss', 'grad_x', 'grad_rel_table', 'grad_norm1_g', 'grad_norm2_g', 'grad_w_ada', 'grad_b_ada', 'grad_w_in', 'grad_mla_cq_g', 'grad_w_uq', 'grad_mla_ckv_g', 'grad_w_ukv', 'grad_mla_qn_g', 'grad_mla_kn_g', 'grad_sw_qn_g', 'grad_sw_kn_g', 'grad_sw_sinks', 'grad_w_out', 'grad_w_up', 'grad_conv_w', 'grad_conv_b', 'grad_w_down', 'delta_rel_table', 'delta_norm1_g', 'delta_norm2_g', 'delta_w_ada', 'delta_b_ada', 'delta_w_in', 'delta_mla_cq_g', 'delta_w_uq', 'delta_mla_ckv_g', 'delta_w_ukv', 'delta_mla_qn_g', 'delta_mla_kn_g', 'delta_sw_qn_g', 'delta_sw_kn_g', 'delta_sw_sinks', 'delta_w_out', 'delta_w_up', 'delta_conv_w', 'delta_conv_b', 'delta_w_down', 'new_m_rel_table', 'new_m_norm1_g', 'new_m_norm2_g', 'new_m_w_ada', 'new_m_b_ada', 'new_m_w_in', 'new_m_mla_cq_g', 'new_m_w_uq', 'new_m_mla_ckv_g', 'new_m_w_ukv', 'new_m_mla_qn_g', 'new_m_mla_kn_g', 'new_m_sw_qn_g', 'new_m_sw_kn_g', 'new_m_sw_sinks', 'new_m_w_out', 'new_m_w_up', 'new_m_conv_w', 'new_m_conv_b', 'new_m_w_down', 'new_v_rel_table', 'new_v_norm1_g', 'new_v_norm2_g', 'new_v_w_ada', 'new_v_b_ada', 'new_v_w_in', 'new_v_mla_cq_g', 'new_v_w_uq', 'new_v_mla_ckv_g', 'new_v_w_ukv', 'new_v_mla_qn_g', 'new_v_mla_kn_g', 'new_v_sw_qn_g', 'new_v_sw_kn_g', 'new_v_sw_sinks', 'new_v_w_out', 'new_v_w_up', 'new_v_conv_w', 'new_v_conv_b', 'new_v_w_down']
TWIN_LEAF_KINDS = {'loss': 'loss', 'grad_x': 'grad_x', 'grad_rel_table': 'grad_w', 'grad_norm1_g': 'grad_w', 'grad_norm2_g': 'grad_w', 'grad_w_ada': 'grad_w', 'grad_b_ada': 'grad_w', 'grad_w_in': 'grad_w', 'grad_mla_cq_g': 'grad_w', 'grad_w_uq': 'grad_w', 'grad_mla_ckv_g': 'grad_w', 'grad_w_ukv': 'grad_w', 'grad_mla_qn_g': 'grad_w', 'grad_mla_kn_g': 'grad_w', 'grad_sw_qn_g': 'grad_w', 'grad_sw_kn_g': 'grad_w', 'grad_sw_sinks': 'grad_w', 'grad_w_out': 'grad_w', 'grad_w_up': 'grad_w', 'grad_conv_w': 'grad_w', 'grad_conv_b': 'grad_w', 'grad_w_down': 'grad_w', 'delta_rel_table': 'delta_w', 'delta_norm1_g': 'delta_w', 'delta_norm2_g': 'delta_w', 'delta_w_ada': 'delta_w', 'delta_b_ada': 'delta_w', 'delta_w_in': 'delta_w', 'delta_mla_cq_g': 'delta_w', 'delta_w_uq': 'delta_w', 'delta_mla_ckv_g': 'delta_w', 'delta_w_ukv': 'delta_w', 'delta_mla_qn_g': 'delta_w', 'delta_mla_kn_g': 'delta_w', 'delta_sw_qn_g': 'delta_w', 'delta_sw_kn_g': 'delta_w', 'delta_sw_sinks': 'delta_w', 'delta_w_out': 'delta_w', 'delta_w_up': 'delta_w', 'delta_conv_w': 'delta_w', 'delta_conv_b': 'delta_w', 'delta_w_down': 'delta_w', 'new_m_rel_table': 'new_m', 'new_m_norm1_g': 'new_m', 'new_m_norm2_g': 'new_m', 'new_m_w_ada': 'new_m', 'new_m_b_ada': 'new_m', 'new_m_w_in': 'new_m', 'new_m_mla_cq_g': 'new_m', 'new_m_w_uq': 'new_m', 'new_m_mla_ckv_g': 'new_m', 'new_m_w_ukv': 'new_m', 'new_m_mla_qn_g': 'new_m', 'new_m_mla_kn_g': 'new_m', 'new_m_sw_qn_g': 'new_m', 'new_m_sw_kn_g': 'new_m', 'new_m_sw_sinks': 'new_m', 'new_m_w_out': 'new_m', 'new_m_w_up': 'new_m', 'new_m_conv_w': 'new_m', 'new_m_conv_b': 'new_m', 'new_m_w_down': 'new_m', 'new_v_rel_table': 'new_v', 'new_v_norm1_g': 'new_v', 'new_v_norm2_g': 'new_v', 'new_v_w_ada': 'new_v', 'new_v_b_ada': 'new_v', 'new_v_w_in': 'new_v', 'new_v_mla_cq_g': 'new_v', 'new_v_w_uq': 'new_v', 'new_v_mla_ckv_g': 'new_v', 'new_v_w_ukv': 'new_v', 'new_v_mla_qn_g': 'new_v', 'new_v_mla_kn_g': 'new_v', 'new_v_sw_qn_g': 'new_v', 'new_v_sw_kn_g': 'new_v', 'new_v_sw_sinks': 'new_v', 'new_v_w_out': 'new_v', 'new_v_w_up': 'new_v', 'new_v_conv_w': 'new_v', 'new_v_conv_b': 'new_v', 'new_v_w_down': 'new_v'}


def _forward(args):
    return _fwd_reference(*[args[k] for k in FWD_PARAMS])


def _output_shape():
    out = _jax.eval_shape(lambda: _forward(_fwd_setup_inputs(0)))
    return out.shape, out.dtype

N_MICROBATCH = 1
ADAM_LR = 0.001
ADAM_B1 = 0.9
ADAM_B2 = 0.999
ADAM_EPS = 1e-08
ADAM_WD = 0.01
ADAM_STEP = 10
PER_EXAMPLE_BATCH_AXIS = {'x': 0, 'c': 0, 'positions': 0, 'loss_target': 0}
SHARED_INPUTS = []
_WEIGHT_DTYPES = {'rel_table': _jnp.float32, 'norm1_g': _jnp.float32, 'norm2_g': _jnp.float32, 'w_ada': _jnp.float32, 'b_ada': _jnp.float32, 'w_in': _jnp.float32, 'mla_cq_g': _jnp.float32, 'w_uq': _jnp.float32, 'mla_ckv_g': _jnp.float32, 'w_ukv': _jnp.float32, 'mla_qn_g': _jnp.float32, 'mla_kn_g': _jnp.float32, 'sw_qn_g': _jnp.float32, 'sw_kn_g': _jnp.float32, 'sw_sinks': _jnp.float32, 'w_out': _jnp.float32, 'w_up': _jnp.float32, 'conv_w': _jnp.float32, 'conv_b': _jnp.float32, 'w_down': _jnp.float32}
MOMENT_SCALE = {'rel_table': 7.238685e-02, 'norm1_g': 3.780905e-01, 'norm2_g': 3.604513e+00, 'w_ada': 9.961432e-01, 'b_ada': 2.124496e+00, 'w_in': 2.619752e-01, 'mla_cq_g': 2.084689e-02, 'w_uq': 1.386595e-02, 'mla_ckv_g': 1.097873e+00, 'w_ukv': 2.247195e-01, 'mla_qn_g': 6.144140e-02, 'mla_kn_g': 6.150689e-02, 'sw_qn_g': 2.222110e-01, 'sw_kn_g': 2.214988e-01, 'sw_sinks': 1.378860e-01, 'w_out': 2.862037e-01, 'w_up': 1.217979e-01, 'conv_w': 5.338452e-01, 'conv_b': 4.508579e-01, 'w_down': 9.809632e-02}


def _to_microbatches(a, axis):
    t = _jnp.moveaxis(a, axis, 0)
    t = t.reshape((N_MICROBATCH, t.shape[0] // N_MICROBATCH) + t.shape[1:])
    return _jnp.moveaxis(t, 1, axis + 1)


def setup_inputs(seed: int = 0) -> dict:
    inp = _fwd_setup_inputs(seed)
    key = _jax.random.fold_in(_jax.random.key(seed), 7919)
    shape, _ = _output_shape()
    out = dict(inp)
    out["loss_target"] = _jax.random.normal(_jax.random.fold_in(key, 0), shape, _jnp.float32)
    for i, name in enumerate(TWIN_WEIGHTS):
        w = inp[name].astype(_jnp.float32)
        if MOMENT_SCALE is None:
            s = _jnp.sqrt(_jnp.mean(_jnp.square(w)) + 1e-30)
        else:
            s = MOMENT_SCALE[name]
        km, kv = _jax.random.split(_jax.random.fold_in(key, i + 1))
        out[name] = w
        out["m_" + name] = s * _jax.random.normal(km, w.shape, _jnp.float32)
        out["v_" + name] = (s * s) * _jax.random.uniform(kv, w.shape, _jnp.float32, 0.5, 1.5)
    if N_MICROBATCH > 1:
        for name, axis in PER_EXAMPLE_BATCH_AXIS.items():
            out[name] = _to_microbatches(out[name], axis)
    return {'x': out['x'], 'c': out['c'], 'positions': out['positions'], 'rel_table': out['rel_table'], 'norm1_g': out['norm1_g'], 'norm2_g': out['norm2_g'], 'w_ada': out['w_ada'], 'b_ada': out['b_ada'], 'w_in': out['w_in'], 'mla_cq_g': out['mla_cq_g'], 'w_uq': out['w_uq'], 'mla_ckv_g': out['mla_ckv_g'], 'w_ukv': out['w_ukv'], 'mla_qn_g': out['mla_qn_g'], 'mla_kn_g': out['mla_kn_g'], 'sw_qn_g': out['sw_qn_g'], 'sw_kn_g': out['sw_kn_g'], 'sw_sinks': out['sw_sinks'], 'w_out': out['w_out'], 'w_up': out['w_up'], 'conv_w': out['conv_w'], 'conv_b': out['conv_b'], 'w_down': out['w_down'], 'loss_target': out['loss_target'], 'm_rel_table': out['m_rel_table'], 'm_norm1_g': out['m_norm1_g'], 'm_norm2_g': out['m_norm2_g'], 'm_w_ada': out['m_w_ada'], 'm_b_ada': out['m_b_ada'], 'm_w_in': out['m_w_in'], 'm_mla_cq_g': out['m_mla_cq_g'], 'm_w_uq': out['m_w_uq'], 'm_mla_ckv_g': out['m_mla_ckv_g'], 'm_w_ukv': out['m_w_ukv'], 'm_mla_qn_g': out['m_mla_qn_g'], 'm_mla_kn_g': out['m_mla_kn_g'], 'm_sw_qn_g': out['m_sw_qn_g'], 'm_sw_kn_g': out['m_sw_kn_g'], 'm_sw_sinks': out['m_sw_sinks'], 'm_w_out': out['m_w_out'], 'm_w_up': out['m_w_up'], 'm_conv_w': out['m_conv_w'], 'm_conv_b': out['m_conv_b'], 'm_w_down': out['m_w_down'], 'v_rel_table': out['v_rel_table'], 'v_norm1_g': out['v_norm1_g'], 'v_norm2_g': out['v_norm2_g'], 'v_w_ada': out['v_w_ada'], 'v_b_ada': out['v_b_ada'], 'v_w_in': out['v_w_in'], 'v_mla_cq_g': out['v_mla_cq_g'], 'v_w_uq': out['v_w_uq'], 'v_mla_ckv_g': out['v_mla_ckv_g'], 'v_w_ukv': out['v_w_ukv'], 'v_mla_qn_g': out['v_mla_qn_g'], 'v_mla_kn_g': out['v_mla_kn_g'], 'v_sw_qn_g': out['v_sw_qn_g'], 'v_sw_kn_g': out['v_sw_kn_g'], 'v_sw_sinks': out['v_sw_sinks'], 'v_w_out': out['v_w_out'], 'v_w_up': out['v_w_up'], 'v_conv_w': out['v_conv_w'], 'v_conv_b': out['v_conv_b'], 'v_w_down': out['v_w_down']}


def _loss(weights, diff, rest, loss_target):
    with _jax.named_scope("forward"):
        args = {**rest, TWIN_DIFF_INPUT: diff, **{k: w.astype(_WEIGHT_DTYPES[k]) for k, w in weights.items()}}
        y = _forward(args)
    with _jax.named_scope("loss_head"):
        err = _jnp.square(y.astype(_jnp.float32) - loss_target)
        return 0.5 * _jnp.sum(_jnp.mean(err, axis=-1)) if err.ndim else 0.5 * err


def _adamw(w, g, m, v):
    m = ADAM_B1 * m + (1.0 - ADAM_B1) * g
    v = ADAM_B2 * v + (1.0 - ADAM_B2) * _jnp.square(g)
    m_hat = m / (1.0 - ADAM_B1 ** ADAM_STEP)
    v_hat = v / (1.0 - ADAM_B2 ** ADAM_STEP)
    delta = -ADAM_LR * (m_hat / (_jnp.sqrt(v_hat) + ADAM_EPS) + ADAM_WD * w)
    return delta, m, v


def reference(x, c, positions, rel_table, norm1_g, norm2_g, w_ada, b_ada, w_in, mla_cq_g, w_uq, mla_ckv_g, w_ukv, mla_qn_g, mla_kn_g, sw_qn_g, sw_kn_g, sw_sinks, w_out, w_up, conv_w, conv_b, w_down, loss_target, m_rel_table, m_norm1_g, m_norm2_g, m_w_ada, m_b_ada, m_w_in, m_mla_cq_g, m_w_uq, m_mla_ckv_g, m_w_ukv, m_mla_qn_g, m_mla_kn_g, m_sw_qn_g, m_sw_kn_g, m_sw_sinks, m_w_out, m_w_up, m_conv_w, m_conv_b, m_w_down, v_rel_table, v_norm1_g, v_norm2_g, v_w_ada, v_b_ada, v_w_in, v_mla_cq_g, v_w_uq, v_mla_ckv_g, v_w_ukv, v_mla_qn_g, v_mla_kn_g, v_sw_qn_g, v_sw_kn_g, v_sw_sinks, v_w_out, v_w_up, v_conv_w, v_conv_b, v_w_down):
    given = dict(x=x, c=c, positions=positions, rel_table=rel_table, norm1_g=norm1_g, norm2_g=norm2_g, w_ada=w_ada, b_ada=b_ada, w_in=w_in, mla_cq_g=mla_cq_g, w_uq=w_uq, mla_ckv_g=mla_ckv_g, w_ukv=w_ukv, mla_qn_g=mla_qn_g, mla_kn_g=mla_kn_g, sw_qn_g=sw_qn_g, sw_kn_g=sw_kn_g, sw_sinks=sw_sinks, w_out=w_out, w_up=w_up, conv_w=conv_w, conv_b=conv_b, w_down=w_down, loss_target=loss_target, m_rel_table=m_rel_table, m_norm1_g=m_norm1_g, m_norm2_g=m_norm2_g, m_w_ada=m_w_ada, m_b_ada=m_b_ada, m_w_in=m_w_in, m_mla_cq_g=m_mla_cq_g, m_w_uq=m_w_uq, m_mla_ckv_g=m_mla_ckv_g, m_w_ukv=m_w_ukv, m_mla_qn_g=m_mla_qn_g, m_mla_kn_g=m_mla_kn_g, m_sw_qn_g=m_sw_qn_g, m_sw_kn_g=m_sw_kn_g, m_sw_sinks=m_sw_sinks, m_w_out=m_w_out, m_w_up=m_w_up, m_conv_w=m_conv_w, m_conv_b=m_conv_b, m_w_down=m_w_down, v_rel_table=v_rel_table, v_norm1_g=v_norm1_g, v_norm2_g=v_norm2_g, v_w_ada=v_w_ada, v_b_ada=v_b_ada, v_w_in=v_w_in, v_mla_cq_g=v_mla_cq_g, v_w_uq=v_w_uq, v_mla_ckv_g=v_mla_ckv_g, v_w_ukv=v_w_ukv, v_mla_qn_g=v_mla_qn_g, v_mla_kn_g=v_mla_kn_g, v_sw_qn_g=v_sw_qn_g, v_sw_kn_g=v_sw_kn_g, v_sw_sinks=v_sw_sinks, v_w_out=v_w_out, v_w_up=v_w_up, v_conv_w=v_conv_w, v_conv_b=v_conv_b, v_w_down=v_w_down)
    weights = {n: given[n] for n in TWIN_WEIGHTS}
    shared = {n: given[n] for n in SHARED_INPUTS}
    per_example = {n: given[n] for n in ['x', 'c', 'positions']}
    grad_fn = _jax.value_and_grad(_loss, argnums=(0, 1))

    def one_microbatch(ex, loss_target):
        ex = dict(ex)
        diff = ex.pop(TWIN_DIFF_INPUT)
        return grad_fn(weights, diff, {**shared, **ex}, loss_target)

    if N_MICROBATCH == 1:
        loss, (grad_w, grad_x) = one_microbatch(per_example, given["loss_target"])
    else:
        def body(carry, xs):
            loss_sum, grad_sum = carry
            l_k, (gw_k, gx_k) = one_microbatch(xs[0], xs[1])
            with _jax.named_scope("update"):
                return (loss_sum + l_k, _jax.tree.map(_jnp.add, grad_sum, gw_k)), gx_k

        init = (_jnp.zeros((), _jnp.float32), _jax.tree.map(_jnp.zeros_like, weights))
        (loss, grad_w), grad_x = _jax.lax.scan(body, init, (per_example, given["loss_target"]))
    with _jax.named_scope("update"):
        delta_w, new_m, new_v = {}, {}, {}
        for n in TWIN_WEIGHTS:
            delta_w[n], new_m[n], new_v[n] = _adamw(weights[n], grad_w[n], given["m_" + n], given["v_" + n])
    return (loss, grad_x, *[grad_w[n] for n in TWIN_WEIGHTS], *[delta_w[n] for n in TWIN_WEIGHTS],
            *[new_m[n] for n in TWIN_WEIGHTS], *[new_v[n] for n in TWIN_WEIGHTS])
```

```python
import functools
import math

import jax
import jax.numpy as jnp
from jax import lax
from jax.experimental import pallas as pl
from jax.experimental.pallas import tpu as pltpu

F32 = jnp.float32
BF16 = jnp.bfloat16
MESH = pl.DeviceIdType.MESH

EPS = 1e-6
NEG = -1e30
HEAD = 64
LANES = 128
MLA_QK = 96
ROPE_THETA = 10000.0
REL_BUCKETS = 32
REL_MAX_DIST = 128
WINDOW = 128
D_FF = 2816
ADAM_LR, ADAM_B1, ADAM_B2, ADAM_EPS, ADAM_WD, ADAM_STEP = 0.001, 0.9, 0.999, 1e-08, 0.01, 10

VMEM_LIMIT = 56 * 1024 * 1024

P_SBQ, P_SBK, P_SBV, P_CQ, P_CKV, P_SLAB, P_SWQ, P_SWK, P_SWV, P_END = 0, 256, 512, 768, 1024, 1152, 1280, 1664, 1792, 1920
SW_PERM = (0, 3, 1, 4, 2, 5)


def _cp(*sem):
    return pltpu.CompilerParams(dimension_semantics=sem, vmem_limit_bytes=VMEM_LIMIT)


def _dot(a, b):
    return jnp.dot(a, b, preferred_element_type=F32)


def _dot_nt(a, b):
    return lax.dot_general(a, b, (((1,), (1,)), ((), ())), preferred_element_type=F32)


def _dot_tn(a, b):
    return lax.dot_general(a, b, (((0,), (0,)), ((), ())), preferred_element_type=F32)


def _split_dot(x, u):
    hi = x.astype(BF16)
    lo = (x - hi.astype(F32)).astype(BF16)
    return _dot(hi, u) + _dot(lo, u)


def _lane_masks():
    lane = lax.broadcasted_iota(jnp.int32, (1, LANES), 1)
    return (lane < HEAD, lane >= HEAD)


def _tile(n, cap, align=128):
    if n <= cap:
        return n
    t = cap - cap % align
    while t >= align:
        if n % t == 0:
            return t
        t -= align
    return n


def matmul(a, b, *, ta=False, tb=False, out_dtype=F32, tm=512, tn=512, tk=1024, name):
    M, K = (a.shape[1], a.shape[0]) if ta else a.shape
    N = b.shape[0] if tb else b.shape[1]
    tm, tn, tk = _tile(M, tm), _tile(N, tn), _tile(K, tk)
    nk = K // tk

    def body(a_ref, b_ref, o_ref, *scratch):
        av = a_ref[...].astype(BF16)
        bv = b_ref[...].astype(BF16)
        if ta:
            part = _dot_tn(av, bv)
        elif tb:
            part = _dot_nt(av, bv)
        else:
            part = _dot(av, bv)
        if nk == 1:
            o_ref[...] = part.astype(out_dtype)
        else:
            acc_ref, = scratch
            k = pl.program_id(2)

            @pl.when(k == 0)
            def _():
                acc_ref[...] = part

            @pl.when(k > 0)
            def _():
                acc_ref[...] += part

            @pl.when(k == nk - 1)
            def _():
                o_ref[...] = acc_ref[...].astype(out_dtype)

    a_spec = pl.BlockSpec((tk, tm), lambda i, j, k: (k, i)) if ta else pl.BlockSpec((tm, tk), lambda i, j, k: (i, k))
    b_spec = pl.BlockSpec((tn, tk), lambda i, j, k: (j, k)) if tb else pl.BlockSpec((tk, tn), lambda i, j, k: (k, j))
    return pl.pallas_call(
        body, name=name, grid=(M // tm, N // tn, nk),
        in_specs=[a_spec, b_spec], out_specs=pl.BlockSpec((tm, tn), lambda i, j, k: (i, j)),
        out_shape=jax.ShapeDtypeStruct((M, N), out_dtype),
        scratch_shapes=[] if nk == 1 else [pltpu.VMEM((tm, tn), F32)],
        compiler_params=_cp("parallel", "parallel", "arbitrary"),
    )(a, b)


def matmul_res(a, b, res, gate, seq, *, tm=512, tn=512, tk=1024, name):
    M, K = a.shape
    N = b.shape[1]
    tm, tn, tk = _tile(min(M, seq), tm), _tile(N, tn), _tile(K, tk)
    nk = K // tk
    per_seq = seq // tm

    def body(a_ref, b_ref, r_ref, g_ref, y_ref, x_ref, acc_ref):
        k = pl.program_id(2)
        part = _dot(a_ref[...].astype(BF16), b_ref[...].astype(BF16))

        @pl.when(k == 0)
        def _():
            acc_ref[...] = part

        @pl.when(k > 0)
        def _():
            acc_ref[...] += part

        @pl.when(k == nk - 1)
        def _():
            y = acc_ref[...]
            y_ref[...] = y
            x_ref[...] = r_ref[...] + g_ref[...] * y

    out = jax.ShapeDtypeStruct((M, N), F32)
    return pl.pallas_call(
        body, name=name, grid=(M // tm, N // tn, nk),
        in_specs=[pl.BlockSpec((tm, tk), lambda i, j, k: (i, k)), pl.BlockSpec((tk, tn), lambda i, j, k: (k, j)),
                  pl.BlockSpec((tm, tn), lambda i, j, k: (i, j)), pl.BlockSpec((None, 1, tn), lambda i, j, k: (lax.div(i, jnp.int32(per_seq)), 0, j))],
        out_specs=[pl.BlockSpec((tm, tn), lambda i, j, k: (i, j))] * 2,
        out_shape=[out, out], scratch_shapes=[pltpu.VMEM((tm, tn), F32)],
        compiler_params=_cp("parallel", "parallel", "arbitrary"),
    )(a, b, res, gate)


def rms_fwd(x3, blk, W, g, sc=None, sh=None, *, tm=512, name):
    Bl, S, _ = x3.shape
    tm = min(tm, S)
    mod = sc is not None

    def body(x_ref, g_ref, *rest):
        o_ref = rest[-1]
        x = x_ref[...]
        r = lax.rsqrt(jnp.mean(x * x, axis=-1, keepdims=True) + EPS)
        y = x * r * g_ref[...]
        if mod:
            y = y * (1.0 + rest[0][...]) + rest[1][...]
        o_ref[...] = y.astype(BF16)

    vec = pl.BlockSpec((None, 1, W), lambda b, s: (b, 0, 0))
    return pl.pallas_call(
        body, name=name, grid=(Bl, S // tm),
        in_specs=[pl.BlockSpec((None, tm, W), lambda b, s: (b, s, blk)), pl.BlockSpec((1, W), lambda b, s: (0, 0))] + ([vec, vec] if mod else []),
        out_specs=pl.BlockSpec((None, tm, W), lambda b, s: (b, s, 0)),
        out_shape=jax.ShapeDtypeStruct((Bl, S, W), BF16),
        compiler_params=_cp("parallel", "parallel"),
    )(x3, g, *([sc, sh] if mod else []))


def rms_bwd(x3, blk, W, dy3, g, sc=None, dres3=None, *, tm=256, name):
    Bl, S, _ = x3.shape
    tm = min(tm, S)
    mod = sc is not None
    res = dres3 is not None

    def body(*refs):
        x_ref, dy_ref, g_ref = refs[:3]
        k = 3
        sc_ref = dr_ref = None
        if mod:
            sc_ref = refs[k]
            k += 1
        if res:
            dr_ref = refs[k]
            k += 1
        dx_ref, dg_ref = refs[k], refs[k + 1]
        b, s = pl.program_id(0), pl.program_id(1)
        x = x_ref[...]
        dy = dy_ref[...].astype(F32)
        g = g_ref[...]
        r = lax.rsqrt(jnp.mean(x * x, axis=-1, keepdims=True) + EPS)
        n = x * r
        if mod:
            dsc_ref, dsh_ref = refs[k + 2], refs[k + 3]
            one_sc = 1.0 + sc_ref[...]

            @pl.when(s == 0)
            def _():
                dsc_ref[...] = jnp.zeros_like(dsc_ref)
                dsh_ref[...] = jnp.zeros_like(dsh_ref)

            dsh_ref[...] += jnp.sum(dy, axis=0, keepdims=True)
            dsc_ref[...] += jnp.sum(dy * n * g, axis=0, keepdims=True)
            dyn = dy * one_sc
        else:
            dyn = dy

        @pl.when((b == 0) & (s == 0))
        def _():
            dg_ref[...] = jnp.zeros_like(dg_ref)

        dg_ref[...] += jnp.sum(dyn * n, axis=0, keepdims=True)
        dn = dyn * g
        dx = r * (dn - n * jnp.mean(dn * n, axis=-1, keepdims=True))
        if res:
            dx = dx + dr_ref[...]
        dx_ref[...] = dx

    blkspec = pl.BlockSpec((None, tm, W), lambda b, s: (b, s, 0))
    vec = pl.BlockSpec((None, 1, W), lambda b, s: (b, 0, 0))
    row = pl.BlockSpec((1, W), lambda b, s: (0, 0))
    in_specs = [pl.BlockSpec((None, tm, W), lambda b, s: (b, s, blk)), blkspec, row] + ([vec] if mod else []) + ([blkspec] if res else [])
    out_specs = [blkspec, row] + ([vec, vec] if mod else [])
    out_shape = [jax.ShapeDtypeStruct((Bl, S, W), F32), jax.ShapeDtypeStruct((1, W), F32)]
    if mod:
        out_shape += [jax.ShapeDtypeStruct((Bl, 1, W), F32)] * 2
    args = [x3, dy3, g] + ([sc] if mod else []) + ([dres3] if res else [])
    return pl.pallas_call(
        body, name=name, grid=(Bl, S // tm), in_specs=in_specs, out_specs=out_specs, out_shape=out_shape,
        compiler_params=_cp("arbitrary", "arbitrary"),
    )(*args)


def pair_rms_fwd(x3, blk0, npairs, g2, *, tm=512, name):
    Bl, S, _ = x3.shape
    tm = min(tm, S)

    def body(x_ref, g_ref, o_ref):
        lo, hi = _lane_masks()
        x = x_ref[...]
        xx = x * x
        s0 = jnp.sum(jnp.where(lo, xx, 0.0), axis=-1, keepdims=True)
        s1 = jnp.sum(jnp.where(hi, xx, 0.0), axis=-1, keepdims=True)
        r = jnp.where(lo, lax.rsqrt(s0 / HEAD + EPS), lax.rsqrt(s1 / HEAD + EPS))
        o_ref[...] = (x * r * g_ref[...]).astype(BF16)

    return pl.pallas_call(
        body, name=name, grid=(Bl, S // tm, npairs),
        in_specs=[pl.BlockSpec((None, tm, LANES), lambda b, s, p: (b, s, blk0 + p)), pl.BlockSpec((1, LANES), lambda b, s, p: (0, 0))],
        out_specs=pl.BlockSpec((None, tm, LANES), lambda b, s, p: (b, s, p)),
        out_shape=jax.ShapeDtypeStruct((Bl, S, LANES * npairs), BF16),
        compiler_params=_cp("parallel", "parallel", "parallel"),
    )(x3, g2)


def pair_rms_bwd(x3, blk0, npairs, dy3, g2, *, tm=512, name):
    Bl, S, _ = x3.shape
    tm = min(tm, S)

    def body(x_ref, dy_ref, g_ref, dx_ref, dg_ref):
        lo, hi = _lane_masks()
        first = (pl.program_id(0) == 0) & (pl.program_id(1) == 0) & (pl.program_id(2) == 0)
        x = x_ref[...]
        dy = dy_ref[...]
        xx = x * x
        s0 = jnp.sum(jnp.where(lo, xx, 0.0), axis=-1, keepdims=True)
        s1 = jnp.sum(jnp.where(hi, xx, 0.0), axis=-1, keepdims=True)
        r = jnp.where(lo, lax.rsqrt(s0 / HEAD + EPS), lax.rsqrt(s1 / HEAD + EPS))
        n = x * r

        @pl.when(first)
        def _():
            dg_ref[...] = jnp.zeros_like(dg_ref)

        part = jnp.sum(dy * n, axis=0, keepdims=True)
        dg_ref[...] += part + pltpu.roll(part, HEAD, 1)
        dn = dy * g_ref[...]
        t = dn * n
        m0 = jnp.sum(jnp.where(lo, t, 0.0), axis=-1, keepdims=True)
        m1 = jnp.sum(jnp.where(hi, t, 0.0), axis=-1, keepdims=True)
        dx_ref[...] = r * (dn - n * (jnp.where(lo, m0, m1) / HEAD))

    return pl.pallas_call(
        body, name=name, grid=(Bl, S // tm, npairs),
        in_specs=[pl.BlockSpec((None, tm, LANES), lambda b, s, p: (b, s, blk0 + p)), pl.BlockSpec((None, tm, LANES), lambda b, s, p: (b, s, p)),
                  pl.BlockSpec((1, LANES), lambda b, s, p: (0, 0))],
        out_specs=[pl.BlockSpec((None, tm, LANES), lambda b, s, p: (b, s, p)), pl.BlockSpec((1, LANES), lambda b, s, p: (0, 0))],
        out_shape=[jax.ShapeDtypeStruct((Bl, S, LANES * npairs), F32), jax.ShapeDtypeStruct((1, LANES), F32)],
        compiler_params=_cp("arbitrary", "arbitrary", "arbitrary"),
    )(x3, dy3, g2)


def _rot(y, cos_t, sin_a, sin_b):
    return y * cos_t + pltpu.roll(y, LANES - 16, 1) * sin_a + pltpu.roll(y, 16, 1) * sin_b


def _rot_t(d, cos_t, sin_a, sin_b):
    return d * cos_t + pltpu.roll(d * sin_a, 16, 1) + pltpu.roll(d * sin_b, LANES - 16, 1)


def rope_norm_fwd(x3, nheads, g, tabs, slab=None, *, tm=512, name):
    Bl, S, _ = x3.shape
    tm = min(tm, S)
    has_slab = slab is not None

    def body(*refs):
        x_ref, g_ref, c_ref, sa_ref, sb_ref = refs[:5]
        o_ref = refs[-1]
        x = x_ref[...]
        if has_slab:
            x = x + refs[5][...]
        r = lax.rsqrt(jnp.sum(x * x, axis=-1, keepdims=True) / MLA_QK + EPS)
        o_ref[...] = _rot(x * r * g_ref[...], c_ref[...], sa_ref[...], sb_ref[...]).astype(BF16)

    head = pl.BlockSpec((None, tm, LANES), lambda b, s, h: (b, s, h))
    tab = pl.BlockSpec((None, tm, LANES), lambda b, s, h: (b, s, 0))
    in_specs = [head, pl.BlockSpec((1, LANES), lambda b, s, h: (0, 0)), tab, tab, tab]
    args = [x3, g, *tabs]
    if has_slab:
        sblk = slab[1]
        in_specs.append(pl.BlockSpec((None, tm, LANES), lambda b, s, h: (b, s, sblk)))
        args.append(slab[0])
    return pl.pallas_call(
        body, name=name, grid=(Bl, S // tm, nheads), in_specs=in_specs, out_specs=head,
        out_shape=jax.ShapeDtypeStruct((Bl, S, LANES * nheads), BF16),
        compiler_params=_cp("parallel", "parallel", "parallel"),
    )(*args)


def rope_norm_bwd(x3, nheads, dy3, g, tabs, slab=None, *, tm=512, name):
    Bl, S, _ = x3.shape
    tm = min(tm, S)
    has_slab = slab is not None

    def body(*refs):
        x_ref, dy_ref, g_ref, c_ref, sa_ref, sb_ref = refs[:6]
        k = 7 if has_slab else 6
        dx_ref, dg_ref = refs[k], refs[k + 1]
        h = pl.program_id(2)
        first = (pl.program_id(0) == 0) & (pl.program_id(1) == 0) & (h == 0)
        x = x_ref[...]
        if has_slab:
            x = x + refs[6][...]
        g = g_ref[...]
        r = lax.rsqrt(jnp.sum(x * x, axis=-1, keepdims=True) / MLA_QK + EPS)
        n = x * r
        d = _rot_t(dy_ref[...], c_ref[...], sa_ref[...], sb_ref[...])

        @pl.when(first)
        def _():
            dg_ref[...] = jnp.zeros_like(dg_ref)

        dg_ref[...] += jnp.sum(d * n, axis=0, keepdims=True)
        dn = d * g
        dx = r * (dn - n * (jnp.sum(dn * n, axis=-1, keepdims=True) / MLA_QK))
        dx_ref[...] = dx
        if has_slab:
            ds_ref = refs[k + 2]

            @pl.when(h == 0)
            def _():
                ds_ref[...] = dx

            @pl.when(h > 0)
            def _():
                ds_ref[...] += dx

    head = pl.BlockSpec((None, tm, LANES), lambda b, s, h: (b, s, h))
    tab = pl.BlockSpec((None, tm, LANES), lambda b, s, h: (b, s, 0))
    row = pl.BlockSpec((1, LANES), lambda b, s, h: (0, 0))
    in_specs = [head, head, row, tab, tab, tab]
    args = [x3, dy3, g, *tabs]
    out_specs = [head, row]
    out_shape = [jax.ShapeDtypeStruct((Bl, S, LANES * nheads), F32), jax.ShapeDtypeStruct((1, LANES), F32)]
    if has_slab:
        sblk = slab[1]
        in_specs.append(pl.BlockSpec((None, tm, LANES), lambda b, s, h: (b, s, sblk)))
        args.append(slab[0])
        out_specs.append(tab)
        out_shape.append(jax.ShapeDtypeStruct((Bl, S, LANES), F32))
    return pl.pallas_call(
        body, name=name, grid=(Bl, S // tm, nheads), in_specs=in_specs, out_specs=out_specs, out_shape=out_shape,
        compiler_params=_cp("arbitrary", "arbitrary", "arbitrary"),
    )(*args)


def _sb_tile(z, strict, u, carry_r):
    sp = jnp.maximum(z, 0.0) + jnp.log(1.0 + jnp.exp(-jnp.abs(z)))
    keep = jnp.where(strict, -sp, 0.0)
    logw = (z - sp) + _split_dot(keep, u) + carry_r
    return jnp.where(strict, jnp.exp(logw), 0.0), keep, sp


SB_BLOCK = 256


def sb_attn_fwd(proj3, *, name):
    Bl, S, _ = proj3.shape
    tq = min(SB_BLOCK, S)
    scale = HEAD ** -0.5
    qb, kb0, vb0 = P_SBQ // LANES, P_SBK // LANES, P_SBV // LANES

    def body(q_ref, k_ref, v_ref, o_ref, rt_ref):
        i = pl.program_id(2)
        masks = _lane_masks()
        lane = lax.broadcasted_iota(jnp.int32, (1, LANES), 1)
        q = q_ref[...]
        qh = [jnp.where(m, q, 0.0).astype(BF16) for m in masks]
        rr = lax.broadcasted_iota(jnp.int32, (tq, tq), 0)
        cc = lax.broadcasted_iota(jnp.int32, (tq, tq), 1)
        u = (rr > cc).astype(BF16)

        rt_ref[...] = jnp.zeros_like(rt_ref)

        def step(t, carry):
            r0, r1, acc = carry
            j = i - t
            off = pl.multiple_of(j * tq, tq)
            kb = k_ref[pl.ds(off, tq), :].astype(BF16)
            vb = v_ref[pl.ds(off, tq), :]
            strict = (cc + j * tq) < (rr + i * tq)
            rt_ref[...] = jnp.where(lane == j, r0, jnp.where(lane == j + HEAD, r1, rt_ref[...]))
            rs = [r0, r1]
            for h in range(2):
                z = _dot_nt(qh[h], kb) * scale
                w, keep, _ = _sb_tile(z, strict, u, rs[h])
                acc = acc + _dot(w.astype(BF16), jnp.where(masks[h], vb, 0.0).astype(BF16))
                rs[h] = rs[h] + jnp.sum(keep, axis=1, keepdims=True)
            return rs[0], rs[1], acc

        zero = jnp.zeros((tq, 1), F32)
        _, _, acc = lax.fori_loop(0, i + 1, step, (zero, zero, jnp.zeros((tq, LANES), F32)))
        o_ref[...] = acc

    seq = lambda blk0: pl.BlockSpec((None, S, LANES), lambda b, p, i: (b, 0, blk0 + p))
    out = pl.BlockSpec((None, tq, LANES), lambda b, p, i: (b, i, p))
    shp = jax.ShapeDtypeStruct((Bl, S, 2 * LANES), F32)
    return pl.pallas_call(
        body, name=name, grid=(Bl, 2, S // tq),
        in_specs=[pl.BlockSpec((None, tq, LANES), lambda b, p, i: (b, i, qb + p)), seq(kb0), seq(vb0)],
        out_specs=[out, out], out_shape=[shp, shp],
        compiler_params=_cp("parallel", "parallel", "arbitrary"),
    )(proj3, proj3, proj3)


def sb_attn_bwd(proj3, rt3, do3, *, name):
    Bl, S, _ = proj3.shape
    tq = min(SB_BLOCK, S)
    scale = HEAD ** -0.5
    qb, kb0, vb0 = P_SBQ // LANES, P_SBK // LANES, P_SBV // LANES

    def body(q_ref, k_ref, v_ref, rt_ref, do_ref, dq_ref, dk_ref, dv_ref):
        i = pl.program_id(2)

        @pl.when(i == 0)
        def _():
            dk_ref[...] = jnp.zeros_like(dk_ref)
            dv_ref[...] = jnp.zeros_like(dv_ref)

        masks = _lane_masks()
        lane = lax.broadcasted_iota(jnp.int32, (1, LANES), 1)
        q = q_ref[...]
        qh = [jnp.where(m, q, 0.0).astype(BF16) for m in masks]
        do_b = do_ref[...].astype(BF16)
        doh = [jnp.where(m, do_b, jnp.zeros_like(do_b)) for m in masks]
        rt = rt_ref[...]
        rr = lax.broadcasted_iota(jnp.int32, (tq, tq), 0)
        cc = lax.broadcasted_iota(jnp.int32, (tq, tq), 1)
        u_suffix = (rr > cc).astype(BF16)
        u_prefix = (rr < cc).astype(BF16)

        def step(j, carry):
            p0, p1, dq = carry
            off = pl.multiple_of(j * tq, tq)
            kf = k_ref[pl.ds(off, tq), :]
            kb = kf.astype(BF16)
            vb = v_ref[pl.ds(off, tq), :]
            strict = (cc + j * tq) < (rr + i * tq)
            ps = [p0, p1]
            dk_acc = jnp.zeros((tq, LANES), F32)
            dv_acc = jnp.zeros((tq, LANES), F32)
            for h in range(2):
                r_j = jnp.sum(jnp.where(lane == j + h * HEAD, rt, 0.0), axis=1, keepdims=True)
                z = _dot_nt(qh[h], kb) * scale
                w, _, sp = _sb_tile(z, strict, u_suffix, r_j)
                vh = jnp.where(masks[h], vb, 0.0).astype(BF16)
                g = _dot_nt(doh[h], vh) * w
                pre = _split_dot(g, u_prefix) + ps[h]
                dz = jnp.where(strict, g * jnp.exp(-sp) - jnp.exp(z - sp) * pre, 0.0) * scale
                dzb = dz.astype(BF16)
                dq = dq + _dot(dzb, jnp.where(masks[h], kf, 0.0).astype(BF16))
                dk_acc = dk_acc + _dot_tn(dzb, qh[h])
                dv_acc = dv_acc + _dot_tn(w.astype(BF16), doh[h])
                ps[h] = ps[h] + jnp.sum(g, axis=1, keepdims=True)
            dk_ref[pl.ds(off, tq), :] += dk_acc
            dv_ref[pl.ds(off, tq), :] += dv_acc
            return ps[0], ps[1], dq

        zero = jnp.zeros((tq, 1), F32)
        out = lax.fori_loop(0, i + 1, step, (zero, zero, jnp.zeros((tq, LANES), F32)))
        dq_ref[...] = out[2]

    seq_in = lambda blk0: pl.BlockSpec((None, S, LANES), lambda b, p, i: (b, 0, blk0 + p))
    blk = pl.BlockSpec((None, tq, LANES), lambda b, p, i: (b, i, p))
    seq_out = pl.BlockSpec((None, S, LANES), lambda b, p, i: (b, 0, p))
    shp = jax.ShapeDtypeStruct((Bl, S, 2 * LANES), F32)
    return pl.pallas_call(
        body, name=name, grid=(Bl, 2, S // tq),
        in_specs=[pl.BlockSpec((None, tq, LANES), lambda b, p, i: (b, i, qb + p)), seq_in(kb0), seq_in(vb0), blk, blk],
        out_specs=[blk, seq_out, seq_out], out_shape=[shp, shp, shp],
        compiler_params=_cp("parallel", "parallel", "arbitrary"),
    )(proj3, proj3, proj3, rt3, do3)


def mla_attn_fwd(q3, k3, kv3, vblk0, *, tq=256, name):
    Bl, S, _ = q3.shape
    tq = min(tq, S)
    scale = MLA_QK ** -0.5

    def body(q_ref, k_ref, v_ref, o_ref, lse_ref):
        i = pl.program_id(2)
        masks = _lane_masks()
        rr = lax.broadcasted_iota(jnp.int32, (tq, tq), 0)
        cc = lax.broadcasted_iota(jnp.int32, (tq, tq), 1)
        qh = [q_ref[:, h * LANES:(h + 1) * LANES] for h in range(2)]

        def step(j, carry):
            m0, l0, m1, l1, acc = carry
            off = pl.multiple_of(j * tq, tq)
            vb = v_ref[pl.ds(off, tq), :]
            causal = (cc + j * tq) <= (rr + i * tq)
            ms, ls, alphas = [m0, m1], [l0, l1], []
            add = jnp.zeros((tq, LANES), F32)
            for h in range(2):
                kh = k_ref[pl.ds(off, tq), h * LANES:(h + 1) * LANES]
                s = jnp.where(causal, _dot_nt(qh[h], kh) * scale, NEG)
                m_new = jnp.maximum(ms[h], jnp.max(s, axis=1, keepdims=True))
                p = jnp.exp(s - m_new)
                alpha = jnp.exp(ms[h] - m_new)
                ls[h] = alpha * ls[h] + jnp.sum(p, axis=1, keepdims=True)
                ms[h] = m_new
                alphas.append(alpha)
                add = add + _dot(p.astype(BF16), jnp.where(masks[h], vb, 0.0).astype(BF16))
            acc = acc * jnp.where(masks[0], alphas[0], alphas[1]) + add
            return ms[0], ls[0], ms[1], ls[1], acc

        neg = jnp.full((tq, 1), NEG, F32)
        zero = jnp.zeros((tq, 1), F32)
        m0, l0, m1, l1, acc = lax.fori_loop(0, i + 1, step, (neg, zero, neg, zero, jnp.zeros((tq, LANES), F32)))
        o_ref[...] = acc / jnp.where(masks[0], l0, l1)
        lse_ref[...] = jnp.where(masks[0], m0 + jnp.log(l0), m1 + jnp.log(l1))

    out = pl.BlockSpec((None, tq, LANES), lambda b, p, i: (b, i, p))
    shp = jax.ShapeDtypeStruct((Bl, S, 3 * LANES), F32)
    return pl.pallas_call(
        body, name=name, grid=(Bl, 3, S // tq),
        in_specs=[pl.BlockSpec((None, tq, 2 * LANES), lambda b, p, i: (b, i, p)), pl.BlockSpec((None, S, 2 * LANES), lambda b, p, i: (b, 0, p)),
                  pl.BlockSpec((None, S, LANES), lambda b, p, i: (b, 0, vblk0 + p))],
        out_specs=[out, out], out_shape=[shp, shp],
        compiler_params=_cp("parallel", "parallel", "arbitrary"),
    )(q3, k3, kv3)


def mla_attn_bwd(q3, k3, kv3, vblk0, o3, lse3, do3, *, tq=256, name):
    Bl, S, _ = q3.shape
    tq = min(tq, S)
    nq = S // tq
    scale = MLA_QK ** -0.5

    def body(q_ref, k_ref, v_ref, o_ref, lse_ref, do_ref, dq_ref, dk_ref, dv_ref):
        j = pl.program_id(2)

        @pl.when(j == 0)
        def _():
            dq_ref[...] = jnp.zeros_like(dq_ref)

        masks = _lane_masks()
        rr = lax.broadcasted_iota(jnp.int32, (tq, tq), 0)
        cc = lax.broadcasted_iota(jnp.int32, (tq, tq), 1)
        vb = v_ref[...]
        vh = [jnp.where(m, vb, 0.0).astype(BF16) for m in masks]
        kh = [k_ref[:, h * LANES:(h + 1) * LANES] for h in range(2)]

        def step(t, carry):
            dk0, dk1, dv = carry
            i = j + t
            off = pl.multiple_of(i * tq, tq)
            causal = (cc + j * tq) <= (rr + i * tq)
            do_b = do_ref[pl.ds(off, tq), :].astype(BF16)
            prod = do_b.astype(F32) * o_ref[pl.ds(off, tq), :]
            lse = lse_ref[pl.ds(off, tq), :]
            dks = [dk0, dk1]
            for h in range(2):
                qh = q_ref[pl.ds(off, tq), h * LANES:(h + 1) * LANES]
                doh = jnp.where(masks[h], do_b, jnp.zeros_like(do_b))
                delta = jnp.sum(jnp.where(masks[h], prod, 0.0), axis=1, keepdims=True)
                lse_h = lse[:, h * HEAD:h * HEAD + 1]
                s = jnp.where(causal, _dot_nt(qh, kh[h]) * scale, NEG)
                p = jnp.exp(s - lse_h)
                ds = (p * (_dot_nt(doh, vh[h]) - delta) * scale).astype(BF16)
                dq_ref[pl.ds(off, tq), h * LANES:(h + 1) * LANES] += _dot(ds, kh[h])
                dks[h] = dks[h] + _dot_tn(ds, qh)
                dv = dv + _dot_tn(p.astype(BF16), doh)
            return dks[0], dks[1], dv

        zero = jnp.zeros((tq, LANES), F32)
        dk0, dk1, dv = lax.fori_loop(0, nq - j, step, (zero, zero, zero))
        dk_ref[:, 0:LANES] = dk0
        dk_ref[:, LANES:2 * LANES] = dk1
        dv_ref[...] = dv

    seq1 = pl.BlockSpec((None, S, LANES), lambda b, p, j: (b, 0, p))
    seq2 = pl.BlockSpec((None, S, 2 * LANES), lambda b, p, j: (b, 0, p))
    return pl.pallas_call(
        body, name=name, grid=(Bl, 3, nq),
        in_specs=[seq2, pl.BlockSpec((None, tq, 2 * LANES), lambda b, p, j: (b, j, p)),
                  pl.BlockSpec((None, tq, LANES), lambda b, p, j: (b, j, vblk0 + p)), seq1, seq1, seq1],
        out_specs=[seq2, pl.BlockSpec((None, tq, 2 * LANES), lambda b, p, j: (b, j, p)), pl.BlockSpec((None, tq, LANES), lambda b, p, j: (b, j, p))],
        out_shape=[jax.ShapeDtypeStruct((Bl, S, 6 * LANES), F32), jax.ShapeDtypeStruct((Bl, S, 6 * LANES), F32), jax.ShapeDtypeStruct((Bl, S, 3 * LANES), F32)],
        compiler_params=_cp("parallel", "parallel", "arbitrary"),
    )(q3, k3, kv3, o3, lse3, do3)


def _bucket_table():
    a = jnp.arange(WINDOW)[:, None]
    b = jnp.arange(2 * WINDOW)[None, :]
    dist = WINDOW + a - b
    max_exact = REL_BUCKETS // 2
    n = jnp.maximum(dist, 0)
    nf = jnp.maximum(n, 1).astype(F32)
    large = max_exact + (jnp.log(nf / max_exact) / math.log(REL_MAX_DIST / max_exact) * (REL_BUCKETS - max_exact)).astype(jnp.int32)
    large = jnp.minimum(large, REL_BUCKETS - 1)
    bucket = jnp.where(n < max_exact, n, large)
    return jnp.where((dist >= 0) & (dist < WINDOW), bucket, -1).astype(jnp.int32)


def swa_bias(rel_flat, bucket, *, name):
    def body(t_ref, b_ref, o_ref):
        bk = b_ref[...]
        for p in range(3):
            for hh in range(2):
                h = hh * 3 + p
                acc = jnp.full(bk.shape, NEG, F32)
                for b in range(REL_BUCKETS):
                    acc = jnp.where(bk == b, t_ref[b * 6 + h], acc)
                o_ref[p, hh] = acc

    return pl.pallas_call(
        body, name=name,
        in_specs=[pl.BlockSpec(memory_space=pltpu.SMEM), pl.BlockSpec(memory_space=pltpu.VMEM)],
        out_specs=pl.BlockSpec(memory_space=pltpu.VMEM),
        out_shape=jax.ShapeDtypeStruct((3, 2, WINDOW, 2 * WINDOW), F32),
    )(rel_flat, bucket)


def swa_bias_bwd(dbias, bucket, *, name):
    Bl = dbias.shape[0]

    def body(d_ref, b_ref, o_ref):
        bk = b_ref[...]
        lane = lax.broadcasted_iota(jnp.int32, (1, LANES), 1)
        rows = []
        for h in range(6):
            hh, p = divmod(h, 3)
            d = d_ref[0, p, hh]
            for bl in range(1, Bl):
                d = d + d_ref[bl, p, hh]
            row = jnp.zeros((1, LANES), F32)
            for b in range(REL_BUCKETS):
                s = jnp.sum(jnp.sum(jnp.where(bk == b, d, 0.0), axis=1, keepdims=True), axis=0, keepdims=True)
                row = row + jnp.where(lane == b, s, 0.0)
            rows.append(row)
        rows += [jnp.zeros((1, LANES), F32)] * 2
        o_ref[...] = jnp.concatenate(rows, axis=0)

    return pl.pallas_call(
        body, name=name,
        in_specs=[pl.BlockSpec(memory_space=pltpu.VMEM)] * 2, out_specs=pl.BlockSpec(memory_space=pltpu.VMEM),
        out_shape=jax.ShapeDtypeStruct((8, LANES), F32),
    )(dbias, bucket)


def _swa_specs(vblk):
    cur = lambda blk: pl.BlockSpec((None, WINDOW, LANES), lambda b, p, n: (b, n, blk))
    prev = lambda blk: pl.BlockSpec((None, WINDOW, LANES), lambda b, p, n: (b, jnp.maximum(n - 1, 0), blk))
    return [pl.BlockSpec((None, WINDOW, LANES), lambda b, p, n: (b, n, p)), cur(0), prev(0), cur(vblk), prev(vblk),
            pl.BlockSpec((None, 2, WINDOW, 2 * WINDOW), lambda b, p, n: (p, 0, 0, 0)), pl.BlockSpec((None, 2, LANES), lambda b, p, n: (p, 0, 0))]


def _swa_logits(qh, kp, kc, bias_h, first, scale):
    sp = jnp.where(first, NEG, _dot_nt(qh, kp) * scale + bias_h[:, :WINDOW])
    sc = _dot_nt(qh, kc) * scale + bias_h[:, WINDOW:]
    return sp, sc


def swa_attn_fwd(qn3, kn3, proj3, bias, sinks, *, name):
    Bl, S, _ = qn3.shape
    scale = HEAD ** -0.5

    def body(q_ref, kc_ref, kp_ref, vc_ref, vp_ref, b_ref, s_ref, o_ref, lse_ref):
        first = pl.program_id(2) == 0
        masks = _lane_masks()
        q = q_ref[...]
        o = jnp.zeros((WINDOW, LANES), F32)
        lses = []
        for h in range(2):
            qh = jnp.where(masks[h], q, jnp.zeros_like(q))
            sp, sc = _swa_logits(qh, kp_ref[...], kc_ref[...], b_ref[h], first, scale)
            sink = s_ref[h:h + 1, 0:1]
            m = jnp.maximum(jnp.maximum(jnp.max(sp, axis=1, keepdims=True), jnp.max(sc, axis=1, keepdims=True)), sink)
            ep, ec = jnp.exp(sp - m), jnp.exp(sc - m)
            l = jnp.sum(ep, axis=1, keepdims=True) + jnp.sum(ec, axis=1, keepdims=True) + jnp.exp(sink - m)
            inv = 1.0 / l
            o = o + _dot((ep * inv).astype(BF16), jnp.where(masks[h], vp_ref[...], 0.0).astype(BF16))
            o = o + _dot((ec * inv).astype(BF16), jnp.where(masks[h], vc_ref[...], 0.0).astype(BF16))
            lses.append(m + jnp.log(l))
        o_ref[...] = o
        lse_ref[...] = jnp.where(masks[0], lses[0], lses[1])

    out = pl.BlockSpec((None, WINDOW, LANES), lambda b, p, n: (b, n, p))
    shp = jax.ShapeDtypeStruct((Bl, S, 3 * LANES), F32)
    return pl.pallas_call(
        body, name=name, grid=(Bl, 3, S // WINDOW), in_specs=_swa_specs(P_SWV // LANES),
        out_specs=[out, out], out_shape=[shp, shp], compiler_params=_cp("parallel", "parallel", "arbitrary"),
    )(qn3, kn3, kn3, proj3, proj3, bias, sinks)


def swa_attn_bwd(qn3, kn3, proj3, bias, sinks, o3, lse3, do3, *, name):
    Bl, S, _ = qn3.shape
    scale = HEAD ** -0.5

    def body(q_ref, kc_ref, kp_ref, vc_ref, vp_ref, b_ref, s_ref, o_ref, lse_ref, do_ref,
             dq_ref, dk_ref, dv_ref, db_ref, dsk_ref):
        p_id, n = pl.program_id(1), pl.program_id(2)
        first = n == 0

        @pl.when((p_id == 0) & first)
        def _():
            dk_ref[...] = jnp.zeros_like(dk_ref)
            dv_ref[...] = jnp.zeros_like(dv_ref)

        @pl.when(first)
        def _():
            db_ref[...] = jnp.zeros_like(db_ref)
            dsk_ref[...] = jnp.zeros_like(dsk_ref)

        masks = _lane_masks()
        q = q_ref[...]
        kc, kp = kc_ref[...], kp_ref[...]
        do_b = do_ref[...].astype(BF16)
        prod = do_b.astype(F32) * o_ref[...]
        lse = lse_ref[...]
        dq = jnp.zeros((WINDOW, LANES), F32)
        dkp = jnp.zeros((WINDOW, LANES), F32)
        dkc = jnp.zeros((WINDOW, LANES), F32)
        dvp = jnp.zeros((WINDOW, LANES), F32)
        dvc = jnp.zeros((WINDOW, LANES), F32)
        for h in range(2):
            qh = jnp.where(masks[h], q, jnp.zeros_like(q))
            doh = jnp.where(masks[h], do_b, jnp.zeros_like(do_b))
            sp, sc = _swa_logits(qh, kp, kc, b_ref[h], first, scale)
            lse_h = lse[:, h * HEAD:h * HEAD + 1]
            pp, pc = jnp.exp(sp - lse_h), jnp.exp(sc - lse_h)
            delta = jnp.sum(jnp.where(masks[h], prod, 0.0), axis=1, keepdims=True)
            dsp = pp * (_dot_nt(doh, jnp.where(masks[h], vp_ref[...], 0.0).astype(BF16)) - delta)
            dsc = pc * (_dot_nt(doh, jnp.where(masks[h], vc_ref[...], 0.0).astype(BF16)) - delta)
            db_ref[h, :, 0:WINDOW] += dsp
            db_ref[h, :, WINDOW:2 * WINDOW] += dsc
            psink = jnp.exp(s_ref[h:h + 1, 0:1] - lse_h)
            dsk_ref[h:h + 1, :] += jnp.broadcast_to(-jnp.sum(psink * delta, axis=0, keepdims=True), (1, LANES))
            dspb, dscb = (dsp * scale).astype(BF16), (dsc * scale).astype(BF16)
            dq = dq + _dot(dspb, jnp.where(masks[h], kp, jnp.zeros_like(kp))) + _dot(dscb, jnp.where(masks[h], kc, jnp.zeros_like(kc)))
            dkp = dkp + _dot_tn(dspb, qh)
            dkc = dkc + _dot_tn(dscb, qh)
            dvp = dvp + _dot_tn(pp.astype(BF16), doh)
            dvc = dvc + _dot_tn(pc.astype(BF16), doh)
        dq_ref[...] = dq
        offp = pl.multiple_of(jnp.maximum(n - 1, 0) * WINDOW, WINDOW)
        offc = pl.multiple_of(n * WINDOW, WINDOW)
        dk_ref[pl.ds(offp, WINDOW), :] += dkp
        dv_ref[pl.ds(offp, WINDOW), :] += dvp
        dk_ref[pl.ds(offc, WINDOW), :] += dkc
        dv_ref[pl.ds(offc, WINDOW), :] += dvc

    blk = pl.BlockSpec((None, WINDOW, LANES), lambda b, p, n: (b, n, p))
    seq = pl.BlockSpec((None, S, LANES), lambda b, p, n: (b, 0, 0))
    return pl.pallas_call(
        body, name=name, grid=(Bl, 3, S // WINDOW), in_specs=_swa_specs(P_SWV // LANES) + [blk, blk, blk],
        out_specs=[blk, seq, seq, pl.BlockSpec((None, None, 2, WINDOW, 2 * WINDOW), lambda b, p, n: (b, p, 0, 0, 0)),
                   pl.BlockSpec((None, None, 2, LANES), lambda b, p, n: (b, p, 0, 0))],
        out_shape=[jax.ShapeDtypeStruct((Bl, S, 3 * LANES), F32), jax.ShapeDtypeStruct((Bl, S, LANES), F32), jax.ShapeDtypeStruct((Bl, S, LANES), F32),
                   jax.ShapeDtypeStruct((Bl, 3, 2, WINDOW, 2 * WINDOW), F32), jax.ShapeDtypeStruct((Bl, 3, 2, LANES), F32)],
        compiler_params=_cp("arbitrary", "arbitrary", "arbitrary"),
    )(qn3, kn3, kn3, proj3, proj3, bias, sinks, o3, lse3, do3)


def _conv_rows(x, halo, w_ref, b_ref, first_blk):
    rows = lax.broadcasted_iota(jnp.int32, x.shape, 0)
    h6 = jnp.where(first_blk, 0.0, halo[6:7, :])
    h7 = jnp.where(first_blk, 0.0, halo[7:8, :])
    x1 = jnp.where(rows == 0, h7, pltpu.roll(x, 1, 0))
    x2 = jnp.where(rows == 0, h6, jnp.where(rows == 1, h7, pltpu.roll(x, 2, 0)))
    return w_ref[0:1, :] * x2 + w_ref[1:2, :] * x1 + w_ref[2:3, :] * x + b_ref[...], x1, x2


def conv_gate_fwd(up3, cw, cb, *, tm=256, tn=256, name):
    Bl, S, _ = up3.shape
    tm = min(tm, S)
    nc = D_FF // tn

    def body(xg_ref, xv_ref, hg_ref, hv_ref, wg_ref, wv_ref, bg_ref, bv_ref, o_ref):
        first = pl.program_id(1) == 0
        ug, _, _ = _conv_rows(xg_ref[...], hg_ref[...], wg_ref, bg_ref, first)
        uv, _, _ = _conv_rows(xv_ref[...], hv_ref[...], wv_ref, bv_ref, first)
        o_ref[...] = (ug * jax.nn.sigmoid(ug) * uv).astype(BF16)

    hb = tm // 8
    main = lambda o: pl.BlockSpec((None, tm, tn), lambda b, s, c: (b, s, c + o))
    halo = lambda o: pl.BlockSpec((None, 8, tn), lambda b, s, c: (b, jnp.maximum(s * hb - 1, 0), c + o))
    wsp = lambda o, r: pl.BlockSpec((r, tn), lambda b, s, c: (0, c + o))
    return pl.pallas_call(
        body, name=name, grid=(Bl, S // tm, nc),
        in_specs=[main(0), main(nc), halo(0), halo(nc), wsp(0, 3), wsp(nc, 3), wsp(0, 1), wsp(nc, 1)],
        out_specs=pl.BlockSpec((None, tm, tn), lambda b, s, c: (b, s, c)),
        out_shape=jax.ShapeDtypeStruct((Bl, S, D_FF), BF16),
        compiler_params=_cp("parallel", "parallel", "parallel"),
    )(up3, up3, up3, up3, cw, cw, cb, cb)


def conv_gate_bwd(up3, cw, cb, da3, *, tm=256, tn=256, name):
    Bl, S, _ = up3.shape
    tm = min(tm, S)
    nc = D_FF // tn

    def body(xg_ref, xv_ref, hg_ref, hv_ref, wg_ref, wv_ref, bg_ref, bv_ref, da_ref, du_ref, dw_ref):
        c, b, s = pl.program_id(0), pl.program_id(1), pl.program_id(2)
        first = s == 0

        @pl.when((b == 0) & first)
        def _():
            dw_ref[...] = jnp.zeros_like(dw_ref)

        ug, g1, g2 = _conv_rows(xg_ref[...], hg_ref[...], wg_ref, bg_ref, first)
        uv, v1, v2 = _conv_rows(xv_ref[...], hv_ref[...], wv_ref, bv_ref, first)
        da = da_ref[...].astype(F32)
        sg = jax.nn.sigmoid(ug)

        def emit(du, x0, x1, x2):
            du_ref[...] = du
            dw_ref[0:1, :] += jnp.sum(du * x2, axis=0, keepdims=True)
            dw_ref[1:2, :] += jnp.sum(du * x1, axis=0, keepdims=True)
            dw_ref[2:3, :] += jnp.sum(du * x0, axis=0, keepdims=True)
            dw_ref[3:4, :] += jnp.sum(du, axis=0, keepdims=True)

        @pl.when(c < nc)
        def _():
            emit(da * uv * sg * (1.0 + ug * (1.0 - sg)), xg_ref[...], g1, g2)

        @pl.when(c >= nc)
        def _():
            emit(da * ug * sg, xv_ref[...], v1, v2)

    hb = tm // 8
    main = lambda o: pl.BlockSpec((None, tm, tn), lambda c, b, s: (b, s, c % nc + o))
    halo = lambda o: pl.BlockSpec((None, 8, tn), lambda c, b, s: (b, jnp.maximum(s * hb - 1, 0), c % nc + o))
    wsp = lambda o, r: pl.BlockSpec((r, tn), lambda c, b, s: (0, c % nc + o))
    return pl.pallas_call(
        body, name=name, grid=(2 * nc, Bl, S // tm),
        in_specs=[main(0), main(nc), halo(0), halo(nc), wsp(0, 3), wsp(nc, 3), wsp(0, 1), wsp(nc, 1),
                  pl.BlockSpec((None, tm, tn), lambda c, b, s: (b, s, c % nc))],
        out_specs=[pl.BlockSpec((None, tm, tn), lambda c, b, s: (b, s, c)), pl.BlockSpec((8, tn), lambda c, b, s: (0, c))],
        out_shape=[jax.ShapeDtypeStruct((Bl, S, 2 * D_FF), F32), jax.ShapeDtypeStruct((8, 2 * D_FF), F32)],
        compiler_params=_cp("arbitrary", "arbitrary", "arbitrary"),
    )(up3, up3, up3, up3, cw, cw, cb, cb, da3)


def conv_t(du3, cw, *, tm=256, tn=512, name):
    Bl, S, C = du3.shape
    tm = min(tm, S)
    ns = S // tm

    def body(x_ref, h_ref, w_ref, o_ref):
        last = pl.program_id(1) == ns - 1
        x = x_ref[...]
        rows = lax.broadcasted_iota(jnp.int32, x.shape, 0)
        h0 = jnp.where(last, 0.0, h_ref[0:1, :])
        h1 = jnp.where(last, 0.0, h_ref[1:2, :])
        x1 = jnp.where(rows == tm - 1, h0, pltpu.roll(x, tm - 1, 0))
        x2 = jnp.where(rows == tm - 1, h1, jnp.where(rows == tm - 2, h0, pltpu.roll(x, tm - 2, 0)))
        o_ref[...] = (w_ref[2:3, :] * x + w_ref[1:2, :] * x1 + w_ref[0:1, :] * x2).astype(BF16)

    hb = tm // 8
    return pl.pallas_call(
        body, name=name, grid=(Bl, ns, C // tn),
        in_specs=[pl.BlockSpec((None, tm, tn), lambda b, s, c: (b, s, c)),
                  pl.BlockSpec((None, 8, tn), lambda b, s, c: (b, jnp.minimum((s + 1) * hb, S // 8 - 1), c)),
                  pl.BlockSpec((3, tn), lambda b, s, c: (0, c))],
        out_specs=pl.BlockSpec((None, tm, tn), lambda b, s, c: (b, s, c)),
        out_shape=jax.ShapeDtypeStruct((Bl, S, C), BF16),
        compiler_params=_cp("parallel", "parallel", "parallel"),
    )(du3, du3, cw)


def gate_bwd(dx3, y3, gate, *, tm=512, name):
    Bl, S, D = dx3.shape
    tm = min(tm, S)

    def body(dx_ref, y_ref, g_ref, o_ref, dg_ref):
        @pl.when(pl.program_id(1) == 0)
        def _():
            dg_ref[...] = jnp.zeros_like(dg_ref)

        dx = dx_ref[...]
        dg_ref[...] += jnp.sum(dx * y_ref[...], axis=0, keepdims=True)
        o_ref[...] = (dx * g_ref[...]).astype(BF16)

    blk = pl.BlockSpec((None, tm, D), lambda b, s: (b, s, 0))
    vec = pl.BlockSpec((None, 1, D), lambda b, s: (b, 0, 0))
    return pl.pallas_call(
        body, name=name, grid=(Bl, S // tm), in_specs=[blk, blk, vec], out_specs=[blk, vec],
        out_shape=[jax.ShapeDtypeStruct((Bl, S, D), BF16), jax.ShapeDtypeStruct((Bl, 1, D), F32)],
        compiler_params=_cp("parallel", "arbitrary"),
    )(dx3, y3, gate)


def loss_grad(y3, t3, *, tm=512, name):
    Bl, S, D = y3.shape
    tm = min(tm, S)
    last = (Bl - 1, S // tm - 1)

    def body(y_ref, t_ref, dy_ref, l_ref, acc_ref):
        b, s = pl.program_id(0), pl.program_id(1)

        @pl.when((b == 0) & (s == 0))
        def _():
            acc_ref[...] = jnp.zeros_like(acc_ref)

        e = y_ref[...] - t_ref[...]
        dy_ref[...] = e * (1.0 / D)
        acc_ref[...] += jnp.sum(e * e, axis=0, keepdims=True)

        @pl.when((b == last[0]) & (s == last[1]))
        def _():
            l_ref[...] = jnp.broadcast_to(jnp.sum(acc_ref[...], axis=1, keepdims=True) * (0.5 / D), (1, LANES))

    blk = pl.BlockSpec((None, tm, D), lambda b, s: (b, s, 0))
    return pl.pallas_call(
        body, name=name, grid=(Bl, S // tm), in_specs=[blk, blk],
        out_specs=[blk, pl.BlockSpec((1, LANES), lambda b, s: (0, 0))],
        out_shape=[jax.ShapeDtypeStruct((Bl, S, D), F32), jax.ShapeDtypeStruct((1, LANES), F32)],
        scratch_shapes=[pltpu.VMEM((1, D), F32)], compiler_params=_cp("arbitrary", "arbitrary"),
    )(y3, t3)


def adamw(w, g, m, v, *, name):
    R, C = w.shape
    tr = R
    for cand in (512, 256, 128, 64, 32, 16, 8):
        if R > cand and R % cand == 0:
            tr = cand
            break
    c1 = 1.0 / (1.0 - ADAM_B1 ** ADAM_STEP)
    c2 = 1.0 / (1.0 - ADAM_B2 ** ADAM_STEP)

    def body(w_ref, g_ref, m_ref, v_ref, d_ref, m2_ref, v2_ref):
        gg = g_ref[...]
        m2 = ADAM_B1 * m_ref[...] + (1.0 - ADAM_B1) * gg
        v2 = ADAM_B2 * v_ref[...] + (1.0 - ADAM_B2) * (gg * gg)
        m2_ref[...] = m2
        v2_ref[...] = v2
        d_ref[...] = -ADAM_LR * ((m2 * c1) / (jnp.sqrt(v2 * c2) + ADAM_EPS) + ADAM_WD * w_ref[...])

    blk = pl.BlockSpec((tr, C), lambda i: (i, 0))
    shp = jax.ShapeDtypeStruct((R, C), F32)
    return pl.pallas_call(
        body, name=name, grid=(R // tr,), in_specs=[blk] * 4, out_specs=[blk] * 3, out_shape=[shp] * 3,
        compiler_params=_cp("parallel"),
    )(w, g, m, v)


def sum_leading(x, *, out_dtype=F32, tr=256, name):
    n, R, C = x.shape
    tr = _tile(R, tr, 16)

    def body(x_ref, o_ref):
        acc = x_ref[0].astype(F32)
        for k in range(1, n):
            acc = acc + x_ref[k].astype(F32)
        o_ref[...] = acc.astype(out_dtype)

    return pl.pallas_call(
        body, name=name, grid=(R // tr,), in_specs=[pl.BlockSpec((n, tr, C), lambda i: (0, i, 0))],
        out_specs=pl.BlockSpec((tr, C), lambda i: (i, 0)), out_shape=jax.ShapeDtypeStruct((R, C), out_dtype),
        compiler_params=_cp("parallel"),
    )(x)


def pair_add_half(g4, recv, c_arr, *, tr=128, name):
    _, R, C = g4.shape
    H = R // 2
    tr = _tile(H, tr, 16)
    nb = H // tr

    def body(c_ref, g_ref, r_ref, o_ref):
        o_ref[...] = (g_ref[...] + r_ref[...]).astype(BF16)

    grid_spec = pltpu.PrefetchScalarGridSpec(
        num_scalar_prefetch=1, grid=(4, nb),
        in_specs=[pl.BlockSpec((None, tr, C), lambda k, i, c_ref: (k, c_ref[0] * nb + i, 0)),
                  pl.BlockSpec((None, tr, C), lambda k, i, c_ref: (k, i, 0))],
        out_specs=pl.BlockSpec((None, tr, C), lambda k, i, c_ref: (k, i, 0)),
    )
    return pl.pallas_call(
        body, name=name, grid_spec=grid_spec, out_shape=jax.ShapeDtypeStruct((4, H, C), BF16),
        compiler_params=_cp("parallel", "parallel"),
    )(c_arr, g4, recv)


def mods_matmul(c_all, w_ada, b_ada_cols, *, tn=512, name):
    L, D, E = w_ada.shape
    nb = c_all.shape[0]
    tn = _tile(E, tn)

    def body(c_ref, w_ref, b_ref, o_ref):
        c = c_ref[...]
        a = c * jax.nn.sigmoid(c)
        o_ref[...] = jnp.dot(a, w_ref[...], preferred_element_type=F32, precision=lax.Precision.HIGHEST) + b_ref[...]

    return pl.pallas_call(
        body, name=name, grid=(L, E // tn),
        in_specs=[pl.BlockSpec((nb, D), lambda l, j: (0, 0)), pl.BlockSpec((None, D, tn), lambda l, j: (l, 0, j)),
                  pl.BlockSpec((None, 1, tn), lambda l, j: (l, 0, j))],
        out_specs=pl.BlockSpec((None, nb, tn), lambda l, j: (l, 0, j)),
        out_shape=jax.ShapeDtypeStruct((L, nb, E), F32), compiler_params=_cp("parallel", "parallel"),
    )(c_all, w_ada, b_ada_cols)


def ada_grad(c_all, dmods, *, tn=512, name):
    L, nb, E = dmods.shape
    D = c_all.shape[1]
    tn = _tile(E, tn)

    def body(c_ref, d_ref, o_ref):
        c = c_ref[...]
        a = c * jax.nn.sigmoid(c)
        o_ref[...] = lax.dot_general(a, d_ref[...], (((0,), (0,)), ((), ())), preferred_element_type=F32, precision=lax.Precision.HIGHEST)

    return pl.pallas_call(
        body, name=name, grid=(L, E // tn),
        in_specs=[pl.BlockSpec((nb, D), lambda l, j: (0, 0)), pl.BlockSpec((None, nb, tn), lambda l, j: (l, 0, j))],
        out_specs=pl.BlockSpec((None, D, tn), lambda l, j: (l, 0, j)),
        out_shape=jax.ShapeDtypeStruct((L, D, E), F32), compiler_params=_cp("parallel", "parallel"),
    )(c_all, dmods)


HBM = pl.BlockSpec(memory_space=pltpu.HBM)


def _me():
    return lax.axis_index("x"), lax.axis_index("y"), lax.axis_index("c")


def _flip(v, bit):
    return 1 - v if bit else v


def allgather8(xs, *, name):
    n, w = xs.shape

    def body(x_ref, out_ref, send_sems, recv_sems, local_sem):
        x, y, c = _me()
        me = 4 * x + 2 * y + c
        mine = pltpu.make_async_copy(x_ref, out_ref.at[me], local_sem)
        mine.start()
        sends = []
        for k in range(1, 8):
            peer = (_flip(x, k & 4), _flip(y, k & 2), _flip(c, k & 1))
            cp = pltpu.make_async_remote_copy(src_ref=x_ref, dst_ref=out_ref.at[me], send_sem=send_sems.at[k - 1],
                                              recv_sem=recv_sems.at[k - 1], device_id=peer, device_id_type=MESH)
            cp.start()
            sends.append(cp)
        for k in range(1, 8):
            peer = (_flip(x, k & 4), _flip(y, k & 2), _flip(c, k & 1))
            src = 4 * peer[0] + 2 * peer[1] + peer[2]
            pltpu.make_async_remote_copy(src_ref=x_ref, dst_ref=out_ref.at[src], send_sem=send_sems.at[k - 1],
                                         recv_sem=recv_sems.at[k - 1], device_id=peer, device_id_type=MESH).wait_recv()
        for cp in sends:
            cp.wait_send()
        mine.wait()

    return pl.pallas_call(
        body, name=name, in_specs=[pl.BlockSpec(memory_space=pltpu.VMEM)], out_specs=pl.BlockSpec(memory_space=pltpu.VMEM),
        out_shape=jax.ShapeDtypeStruct((8, n, w), xs.dtype),
        scratch_shapes=[pltpu.SemaphoreType.DMA((7,)), pltpu.SemaphoreType.DMA((7,)), pltpu.SemaphoreType.DMA],
    )(xs)


def gather_weights(wflat, *, name):
    R, C = wflat.shape
    H = R // 2

    def body(x_ref, out_ref, send_sems, recv_sems, local_sem):
        x, y, c = _me()
        j = 2 * x + y
        mine = pltpu.make_async_copy(x_ref, out_ref.at[j], local_sem)
        mine.start()
        my_half = pl.ds(c * H, H)
        sib_half = pl.ds((1 - c) * H, H)
        chips = [(_flip(x, k & 2), _flip(y, k & 1)) for k in range(1, 4)]
        sends = []
        for k, (px, py) in enumerate(chips):
            cp = pltpu.make_async_remote_copy(src_ref=x_ref.at[my_half], dst_ref=out_ref.at[j, my_half], send_sem=send_sems.at[k],
                                              recv_sem=recv_sems.at[k], device_id=(px, py, c), device_id_type=MESH)
            cp.start()
            sends.append(cp)
        for k, (px, py) in enumerate(chips):
            slot = out_ref.at[2 * px + py, my_half]
            pltpu.make_async_remote_copy(src_ref=slot, dst_ref=slot, send_sem=send_sems.at[k], recv_sem=recv_sems.at[k],
                                         device_id=(px, py, c), device_id_type=MESH).wait_recv()
            cp = pltpu.make_async_remote_copy(src_ref=slot, dst_ref=slot, send_sem=send_sems.at[3 + k], recv_sem=recv_sems.at[3 + k],
                                              device_id=(x, y, 1 - c), device_id_type=MESH)
            cp.start()
            sends.append(cp)
        for k, (px, py) in enumerate(chips):
            slot = out_ref.at[2 * px + py, sib_half]
            pltpu.make_async_remote_copy(src_ref=slot, dst_ref=slot, send_sem=send_sems.at[3 + k], recv_sem=recv_sems.at[3 + k],
                                         device_id=(x, y, 1 - c), device_id_type=MESH).wait_recv()
        for cp in sends:
            cp.wait_send()
        mine.wait()

    return pl.pallas_call(
        body, name=name, in_specs=[HBM], out_specs=HBM, out_shape=jax.ShapeDtypeStruct((4, R, C), wflat.dtype),
        scratch_shapes=[pltpu.SemaphoreType.DMA((6,)), pltpu.SemaphoreType.DMA((6,)), pltpu.SemaphoreType.DMA],
    )(wflat)


def swap_halves(g4, *, name):
    n, R, C = g4.shape
    H = R // 2

    def body(g_ref, out_ref, send_sems, recv_sems):
        x, y, c = _me()
        sib = (x, y, 1 - c)
        sends = []
        for k in range(n):
            cp = pltpu.make_async_remote_copy(src_ref=g_ref.at[k, pl.ds((1 - c) * H, H)], dst_ref=out_ref.at[k], send_sem=send_sems.at[k],
                                              recv_sem=recv_sems.at[k], device_id=sib, device_id_type=MESH)
            cp.start()
            sends.append(cp)
        for k in range(n):
            pltpu.make_async_remote_copy(src_ref=g_ref.at[k, pl.ds(c * H, H)], dst_ref=out_ref.at[k], send_sem=send_sems.at[k],
                                         recv_sem=recv_sems.at[k], device_id=sib, device_id_type=MESH).wait_recv()
        for cp in sends:
            cp.wait_send()

    return pl.pallas_call(
        body, name=name, in_specs=[HBM], out_specs=HBM, out_shape=jax.ShapeDtypeStruct((n, H, C), g4.dtype),
        scratch_shapes=[pltpu.SemaphoreType.DMA((n,)), pltpu.SemaphoreType.DMA((n,))],
    )(g4)


def scatter_chips(p4, *, name):
    n, H, C = p4.shape

    def body(p_ref, out_ref, send_sems, recv_sems, local_sem):
        x, y, c = _me()
        j = 2 * x + y
        mine = pltpu.make_async_copy(p_ref.at[j], out_ref.at[j], local_sem)
        mine.start()
        chips = [(_flip(x, k & 2), _flip(y, k & 1)) for k in range(1, 4)]
        sends = []
        for k, (px, py) in enumerate(chips):
            cp = pltpu.make_async_remote_copy(src_ref=p_ref.at[2 * px + py], dst_ref=out_ref.at[j], send_sem=send_sems.at[k],
                                              recv_sem=recv_sems.at[k], device_id=(px, py, c), device_id_type=MESH)
            cp.start()
            sends.append(cp)
        for k, (px, py) in enumerate(chips):
            slot = out_ref.at[2 * px + py]
            pltpu.make_async_remote_copy(src_ref=slot, dst_ref=slot, send_sem=send_sems.at[k], recv_sem=recv_sems.at[k],
                                         device_id=(px, py, c), device_id_type=MESH).wait_recv()
        for cp in sends:
            cp.wait_send()
        mine.wait()

    return pl.pallas_call(
        body, name=name, in_specs=[HBM], out_specs=HBM, out_shape=jax.ShapeDtypeStruct((n, H, C), p4.dtype),
        scratch_shapes=[pltpu.SemaphoreType.DMA((3,)), pltpu.SemaphoreType.DMA((3,)), pltpu.SemaphoreType.DMA],
    )(p4)


def join_halves(half, *, name):
    H, C = half.shape

    def body(h_ref, out_ref, send_sem, recv_sem, local_sem):
        x, y, c = _me()
        sib = (x, y, 1 - c)
        mine = pltpu.make_async_copy(h_ref, out_ref.at[pl.ds(c * H, H)], local_sem)
        mine.start()
        cp = pltpu.make_async_remote_copy(src_ref=h_ref, dst_ref=out_ref.at[pl.ds(c * H, H)], send_sem=send_sem, recv_sem=recv_sem,
                                          device_id=sib, device_id_type=MESH)
        cp.start()
        pltpu.make_async_remote_copy(src_ref=h_ref, dst_ref=out_ref.at[pl.ds((1 - c) * H, H)], send_sem=send_sem, recv_sem=recv_sem,
                                     device_id=sib, device_id_type=MESH).wait_recv()
        cp.wait_send()
        mine.wait()

    return pl.pallas_call(
        body, name=name, in_specs=[HBM], out_specs=HBM, out_shape=jax.ShapeDtypeStruct((2 * H, C), half.dtype),
        scratch_shapes=[pltpu.SemaphoreType.DMA, pltpu.SemaphoreType.DMA, pltpu.SemaphoreType.DMA],
    )(half)


def _cat(parts, axis=-1):
    return jnp.concatenate(parts, axis=axis)


def _prep_w_in(w):
    z = lambda n: jnp.zeros((w.shape[0], n), w.dtype)
    swq = w[:, 1184:1568]
    return _cat([w[:, 0:1152], z(64), w[:, 1152:1184], z(32)] + [swq[:, HEAD * h:HEAD * (h + 1)] for h in SW_PERM] + [w[:, 1568:1824]])


def _unprep_w_in(g):
    swq = g[:, P_SWQ:P_SWK]
    return _cat([g[:, 0:1152], g[:, 1216:1248]] + [swq[:, HEAD * SW_PERM.index(h):HEAD * (SW_PERM.index(h) + 1)] for h in range(6)] + [g[:, P_SWK:P_END]])


def _prep_w_uq(w):
    z = jnp.zeros((w.shape[0], 32), w.dtype)
    return _cat([p for h in range(6) for p in (w[:, MLA_QK * h:MLA_QK * (h + 1)], z)])


def _unprep_w_uq(g):
    return _cat([g[:, LANES * h:LANES * h + MLA_QK] for h in range(6)])


def _prep_w_ukv(w):
    z = jnp.zeros((w.shape[0], HEAD), w.dtype)
    return _cat([p for h in range(6) for p in (w[:, LANES * h:LANES * h + HEAD], z)] + [w[:, LANES * h + HEAD:LANES * (h + 1)] for h in range(6)])


def _unprep_w_ukv(g):
    return _cat([p for h in range(6) for p in (g[:, LANES * h:LANES * h + HEAD], g[:, 768 + HEAD * h:768 + HEAD * (h + 1)])])


def _prep_w_out(w):
    return _cat([w[0:640]] + [w[640 + HEAD * h:640 + HEAD * (h + 1)] for h in SW_PERM], axis=0)


def _unprep_w_out(g):
    return _cat([g[0:640]] + [g[640 + HEAD * SW_PERM.index(h):640 + HEAD * (SW_PERM.index(h) + 1)] for h in range(6)], axis=0)


def _rope_tables(positions):
    half = 16
    inv_freq = jnp.power(ROPE_THETA, -jnp.arange(half, dtype=F32) / half)
    ang = positions.astype(F32)[..., None] * inv_freq
    cos, sin = jnp.cos(ang), jnp.sin(ang)
    z = lambda n: jnp.zeros(ang.shape[:-1] + (n,), F32)
    return (_cat([jnp.ones(ang.shape[:-1] + (HEAD,), F32), cos, cos, z(32)]), _cat([z(HEAD), -sin, z(16), z(32)]), _cat([z(HEAD), z(16), sin, z(32)]))


def _small_params(p):
    pad96 = lambda g: _cat([g, jnp.zeros((32,), F32)]).reshape(1, LANES)
    two = lambda g: _cat([g, g]).reshape(1, LANES)
    sinks = jnp.broadcast_to(p["sw_sinks"].reshape(2, 3).T[:, :, None], (3, 2, LANES))
    return dict(n1=p["norm1_g"].reshape(1, -1), n2=p["norm2_g"].reshape(1, -1), cq_g=p["mla_cq_g"].reshape(1, -1),
                ckv_g=p["mla_ckv_g"].reshape(1, -1), qn_g=pad96(p["mla_qn_g"]), kn_g=pad96(p["mla_kn_g"]),
                swq_g=two(p["sw_qn_g"]), swk_g=two(p["sw_kn_g"]), sinks=sinks, conv_b=p["conv_b"].reshape(1, -1))


def _layer_fwd(x3, md, W, tabs, bias, tag):
    Bl, S, D = x3.shape
    T = Bl * S
    n = lambda s: f"{s}_{tag}"
    two = lambda a: a.reshape(T, a.shape[-1])
    three = lambda a: a.reshape(Bl, S, a.shape[-1])
    h = rms_fwd(x3, 0, D, W["n1"], md["scale1"], md["shift1"], name=n("norm1"))
    proj = three(matmul(two(h), W["w_in"], tn=640, name=n("in_proj")))
    o_a, rt_a = sb_attn_fwd(proj, name=n("sb_fwd"))
    cqn = rms_fwd(proj, P_CQ // 256, 256, W["cq_g"], name=n("cq_norm"))
    ckvn = rms_fwd(proj, P_CKV // LANES, LANES, W["ckv_g"], name=n("ckv_norm"))
    qb = three(matmul(two(cqn), W["w_uq"], tn=768, name=n("uq")))
    kvb = three(matmul(two(ckvn), W["w_ukv"], tn=384, name=n("ukv")))
    q_m = rope_norm_fwd(qb, 6, W["qn_g"], tabs, name=n("q_rope"))
    k_m = rope_norm_fwd(kvb, 6, W["kn_g"], tabs, (proj, P_SLAB // LANES), name=n("k_rope"))
    o_b, lse_b = mla_attn_fwd(q_m, k_m, kvb, 6, name=n("mla_fwd"))
    q_c = pair_rms_fwd(proj, P_SWQ // LANES, 3, W["swq_g"], name=n("swq_norm"))
    k_c = pair_rms_fwd(proj, P_SWK // LANES, 1, W["swk_g"], name=n("swk_norm"))
    o_c, lse_c = swa_attn_fwd(q_c, k_c, proj, bias, W["sinks"], name=n("swa_fwd"))
    mix = _cat([o_a, o_b, o_c])
    att, x1 = matmul_res(two(mix), W["w_out"], two(x3), md["gate1"], S, name=n("out_proj"))
    x1 = three(x1)
    h2 = rms_fwd(x1, 0, D, W["n2"], md["scale2"], md["shift2"], name=n("norm2"))
    up = three(matmul(two(h2), W["w_up"], name=n("up_proj")))
    a = conv_gate_fwd(up, W["conv_w"], W["conv_b"], name=n("conv_gate"))
    yd, x2 = matmul_res(two(a), W["w_down"], two(x1), md["gate2"], S, name=n("down_proj"))
    saved = dict(x=x3, h=h, proj=proj, rt_a=rt_a, cqn=cqn, ckvn=ckvn, qb=qb, kvb=kvb, q_m=q_m, k_m=k_m, o_b=o_b, lse_b=lse_b,
                 q_c=q_c, k_c=k_c, o_c=o_c, lse_c=lse_c, mix=mix, att=three(att), x1=x1, h2=h2, up=up, a=a, yd=three(yd))
    return three(x2), saved


def _layer_bwd(dx2, sv, md, W, tabs, bias, tag):
    Bl, S, D = dx2.shape
    T = Bl * S
    n = lambda s: f"{s}_{tag}"
    two = lambda a: a.reshape(T, a.shape[-1])
    three = lambda a: a.reshape(Bl, S, a.shape[-1])
    g = {}
    dyb, dgate2 = gate_bwd(dx2, sv["yd"], md["gate2"], name=n("gate2_bwd"))
    da = three(matmul(two(dyb), W["w_down"], tb=True, name=n("down_dx")))
    g["w_down"] = matmul(two(sv["a"]), two(dyb), ta=True, name=n("down_dw"))
    du, dcw = conv_gate_bwd(sv["up"], W["conv_w"], W["conv_b"], da, name=n("conv_gate_bwd"))
    dup = conv_t(du, W["conv_w"], name=n("conv_t"))
    dh2 = three(matmul(two(dup), W["w_up"], tb=True, name=n("up_dx")))
    g["w_up"] = matmul(two(sv["h2"]), two(dup), ta=True, name=n("up_dw"))
    dx1, dn2, dsc2, dsh2 = rms_bwd(sv["x1"], 0, D, dh2, W["n2"], md["scale2"], dx2, name=n("norm2_bwd"))
    dmo, dgate1 = gate_bwd(dx1, sv["att"], md["gate1"], name=n("gate1_bwd"))
    dmix = three(matmul(two(dmo), W["w_out"], tb=True, name=n("out_dx")))
    g["w_out"] = matmul(two(sv["mix"]), two(dmo), ta=True, name=n("out_dw"))
    proj = sv["proj"]
    dq_a, dk_a, dv_a = sb_attn_bwd(proj, sv["rt_a"], dmix[:, :, 0:256], name=n("sb_bwd"))
    dq_m, dk_m, dv_b = mla_attn_bwd(sv["q_m"], sv["k_m"], sv["kvb"], 6, sv["o_b"], sv["lse_b"], dmix[:, :, 256:640], name=n("mla_bwd"))
    dqb, dqn = rope_norm_bwd(sv["qb"], 6, dq_m, W["qn_g"], tabs, name=n("q_rope_bwd"))
    dkn_x, dkn, dslab = rope_norm_bwd(sv["kvb"], 6, dk_m, W["kn_g"], tabs, (proj, P_SLAB // LANES), name=n("k_rope_bwd"))
    dkvb = _cat([dkn_x, dv_b])
    dckvn = three(matmul(two(dkvb), W["w_ukv"], tb=True, name=n("ukv_dx")))
    g["w_ukv"] = matmul(two(sv["ckvn"]), two(dkvb), ta=True, tn=384, name=n("ukv_dw"))
    dcqn = three(matmul(two(dqb), W["w_uq"], tb=True, name=n("uq_dx")))
    g["w_uq"] = matmul(two(sv["cqn"]), two(dqb), ta=True, tn=768, name=n("uq_dw"))
    dcq, dcq_g = rms_bwd(proj, P_CQ // 256, 256, dcqn, W["cq_g"], name=n("cq_norm_bwd"))
    dckv, dckv_g = rms_bwd(proj, P_CKV // LANES, LANES, dckvn, W["ckv_g"], name=n("ckv_norm_bwd"))
    dq_c, dk_c, dv_c, dbias, dsink = swa_attn_bwd(sv["q_c"], sv["k_c"], proj, bias, W["sinks"], sv["o_c"], sv["lse_c"], dmix[:, :, 640:1024], name=n("swa_bwd"))
    dswq, dswq_g = pair_rms_bwd(proj, P_SWQ // LANES, 3, dq_c, W["swq_g"], name=n("swq_norm_bwd"))
    dswk, dswk_g = pair_rms_bwd(proj, P_SWK // LANES, 1, dk_c, W["swk_g"], name=n("swk_norm_bwd"))
    dproj = _cat([dq_a, dk_a, dv_a, dcq, dckv, dslab, dswq, dswk, dv_c])
    dh = three(matmul(two(dproj), W["w_in"], tb=True, name=n("in_dx")))
    g["w_in"] = matmul(two(sv["h"]), two(dproj), ta=True, tn=640, name=n("in_dw"))
    dx, dn1, dsc1, dsh1 = rms_bwd(sv["x"], 0, D, dh, W["n1"], md["scale1"], dx1, name=n("norm1_bwd"))
    small = dict(n1=dn1, n2=dn2, cq_g=dcq_g, ckv_g=dckv_g, qn_g=dqn, kn_g=dkn, swq_g=dswq_g, swk_g=dswk_g, conv=dcw)
    dmods = _cat([dsh1, dsc1, dgate1, dsh2, dsc2, dgate2]).reshape(Bl, 6 * D)
    return dx, g, small, dmods, dbias, dsink


BIG = ("w_in", "w_uq", "w_ukv", "w_out", "w_up", "w_down")
ROW_SHARDED = ("w_out", "w_down")
PREP = dict(w_in=_prep_w_in, w_uq=_prep_w_uq, w_ukv=_prep_w_ukv, w_out=_prep_w_out, w_up=lambda w: w, w_down=lambda w: w)
UNPREP = dict(w_in=_unprep_w_in, w_uq=_unprep_w_uq, w_ukv=_unprep_w_ukv, w_out=_unprep_w_out, w_up=lambda w: w, w_down=lambda w: w)
PACK_COLS = 1024
NCHIPS = 4


def _packed_rows(shapes, L):
    total = L * sum(r * c for r, c in shapes.values())
    rows = -(-total // PACK_COLS)
    return -(-rows // 256) * 256


def _local_step(x, target, positions, mods, Wl, rel_flat):
    Bl, S, D = x.shape
    L = len(Wl)
    tabs = _rope_tables(positions)
    bucket = _bucket_table()
    bias = swa_bias(rel_flat, bucket, name="swa_bias")
    mds = []
    for l in range(L):
        parts = [mods[l, :, D * k:D * (k + 1)].reshape(Bl, 1, D) for k in range(6)]
        mds.append(dict(zip(("shift1", "scale1", "gate1", "shift2", "scale2", "gate2"), parts)))
    saved = []
    h = x
    for l in range(L):
        h, sv = _layer_fwd(h, mds[l], Wl[l], tabs, bias, f"l{l}")
        saved.append(sv)
    dy, loss = loss_grad(h, target, name="loss")
    grads, smalls, dmods, dbiases, dsinks = [None] * L, [None] * L, [None] * L, [None] * L, [None] * L
    for l in reversed(range(L)):
        dy, grads[l], smalls[l], dmods[l], dbiases[l], dsinks[l] = _layer_bwd(dy, saved[l], mds[l], Wl[l], tabs, bias, f"l{l}")
    drel = swa_bias_bwd(_cat(dbiases, axis=0), bucket, name="swa_bias_bwd")
    return loss, dy, grads, smalls, dmods, dsinks, drel


def _rows(a):
    flat = a.reshape(-1)
    pad = (-flat.shape[0]) % LANES
    if pad:
        flat = _cat([flat, jnp.zeros((pad,), flat.dtype)])
    return flat.reshape(-1, LANES)


class _Packer:
    def __init__(self):
        self.items, self.n = [], 0

    def add(self, name, a):
        r = _rows(a)
        pad = (-r.shape[0]) % 8
        if pad:
            r = _cat([r, jnp.zeros((pad, LANES), r.dtype)], axis=0)
        self.items.append((name, self.n, a.shape, r))
        self.n += r.shape[0]

    def pack(self):
        return _cat([it[3] for it in self.items], axis=0)

    def cut(self, buf, lead=()):
        out = {}
        for name, off, shape, r in self.items:
            size = math.prod(shape)
            seg = buf[..., off:off + r.shape[0], :].reshape(lead + (-1,))[..., :size]
            out[name] = seg.reshape(lead + tuple(shape))
        return out


WEIGHTS = ("rel_table", "norm1_g", "norm2_g", "w_ada", "b_ada", "w_in", "mla_cq_g", "w_uq", "mla_ckv_g", "w_ukv", "mla_qn_g", "mla_kn_g",
           "sw_qn_g", "sw_kn_g", "sw_sinks", "w_out", "w_up", "conv_w", "conv_b", "w_down")
SMALL = tuple(n for n in WEIGHTS if n not in BIG + ("w_ada",))


def kernel(x, c, positions, rel_table, norm1_g, norm2_g, w_ada, b_ada, w_in, mla_cq_g, w_uq, mla_ckv_g, w_ukv, mla_qn_g, mla_kn_g, sw_qn_g, sw_kn_g, sw_sinks, w_out, w_up, conv_w, conv_b, w_down, loss_target, m_rel_table, m_norm1_g, m_norm2_g, m_w_ada, m_b_ada, m_w_in, m_mla_cq_g, m_w_uq, m_mla_ckv_g, m_w_ukv, m_mla_qn_g, m_mla_kn_g, m_sw_qn_g, m_sw_kn_g, m_sw_sinks, m_w_out, m_w_up, m_conv_w, m_conv_b, m_w_down, v_rel_table, v_norm1_g, v_norm2_g, v_w_ada, v_b_ada, v_w_in, v_mla_cq_g, v_w_uq, v_mla_ckv_g, v_w_ukv, v_mla_qn_g, v_mla_kn_g, v_sw_qn_g, v_sw_kn_g, v_sw_sinks, v_w_out, v_w_up, v_conv_w, v_conv_b, v_w_down):
    w = dict(rel_table=rel_table, norm1_g=norm1_g, norm2_g=norm2_g, w_ada=w_ada, b_ada=b_ada, w_in=w_in, mla_cq_g=mla_cq_g, w_uq=w_uq,
             mla_ckv_g=mla_ckv_g, w_ukv=w_ukv, mla_qn_g=mla_qn_g, mla_kn_g=mla_kn_g, sw_qn_g=sw_qn_g, sw_kn_g=sw_kn_g, sw_sinks=sw_sinks,
             w_out=w_out, w_up=w_up, conv_w=conv_w, conv_b=conv_b, w_down=w_down)
    m = dict(rel_table=m_rel_table, norm1_g=m_norm1_g, norm2_g=m_norm2_g, w_ada=m_w_ada, b_ada=m_b_ada, w_in=m_w_in, mla_cq_g=m_mla_cq_g,
             w_uq=m_w_uq, mla_ckv_g=m_mla_ckv_g, w_ukv=m_w_ukv, mla_qn_g=m_mla_qn_g, mla_kn_g=m_mla_kn_g, sw_qn_g=m_sw_qn_g,
             sw_kn_g=m_sw_kn_g, sw_sinks=m_sw_sinks, w_out=m_w_out, w_up=m_w_up, conv_w=m_conv_w, conv_b=m_conv_b, w_down=m_w_down)
    v = dict(rel_table=v_rel_table, norm1_g=v_norm1_g, norm2_g=v_norm2_g, w_ada=v_w_ada, b_ada=v_b_ada, w_in=v_w_in, mla_cq_g=v_mla_cq_g,
             w_uq=v_w_uq, mla_ckv_g=v_mla_ckv_g, w_ukv=v_w_ukv, mla_qn_g=v_mla_qn_g, mla_kn_g=v_mla_kn_g, sw_qn_g=v_sw_qn_g,
             sw_kn_g=v_sw_kn_g, sw_sinks=v_sw_sinks, w_out=v_w_out, w_up=v_w_up, conv_w=v_conv_w, conv_b=v_conv_b, w_down=v_w_down)
    Bl, S, D = x.shape
    L = norm1_g.shape[0]
    xi, yi, ci = _me()
    chip = 2 * xi + yi
    dev = 4 * xi + 2 * yi + ci
    ndev = 2 * NCHIPS

    shapes = {k: w[k].shape[1:] for k in BIG}
    R = _packed_rows(shapes, L)
    flat = _cat([w[k][l].reshape(-1) for l in range(L) for k in BIG])
    flat = _cat([flat, jnp.zeros((R * PACK_COLS - flat.shape[0],), F32)]).astype(BF16)
    w4 = gather_weights(flat.reshape(R, PACK_COLS), name="gather_weights").reshape(NCHIPS, R * PACK_COLS)
    full = [dict() for _ in range(L)]
    off = 0
    for l in range(L):
        for k in BIG:
            r, cc = shapes[k]
            seg = w4[:, off:off + r * cc].reshape(NCHIPS, r, cc)
            off += r * cc
            fw = seg.reshape(NCHIPS * r, cc) if k in ROW_SHARDED else jnp.transpose(seg, (1, 0, 2)).reshape(r, NCHIPS * cc)
            full[l][k] = PREP[k](fw)

    pk = _Packer()
    pk.add("c", c)
    pk.add("conv_w", conv_w)
    got = pk.cut(allgather8(pk.pack(), name="gather_cond"), (ndev,))
    c_all = got["c"].reshape(ndev * Bl, D)
    conv_full = jnp.transpose(got["conv_w"][0::2], (1, 2, 0, 3)).reshape(L, 3, -1)
    E = w_ada.shape[2]
    b_cols = lax.dynamic_slice(b_ada, (0, chip * E), (L, E)).reshape(L, 1, E)
    mods_cols = mods_matmul(c_all, w_ada, b_cols, name="mods")
    mods_all = allgather8(_rows(mods_cols), name="gather_mods")[0::2].reshape(NCHIPS, L, ndev * Bl, E)
    mods_all = jnp.transpose(mods_all, (1, 2, 0, 3)).reshape(L, ndev * Bl, NCHIPS * E)
    mods = lax.dynamic_slice(mods_all, (0, dev * Bl, 0), (L, Bl, NCHIPS * E))

    Wl = []
    for l in range(L):
        Wd = _small_params({k: w[k][l] for k in SMALL if k not in ("rel_table", "b_ada", "conv_w")})
        Wd.update(full[l])
        Wd["conv_w"] = conv_full[l]
        Wl.append(Wd)

    loss, dx, grads, smalls, dmods, dsinks, drel = _local_step(x, loss_target, positions, mods, Wl, rel_table.reshape(-1))

    parts = []
    for l in range(L):
        for k in BIG:
            gk = UNPREP[k](grads[l][k])
            r, cc = shapes[k]
            parts.append(gk.reshape(NCHIPS, r * cc) if k in ROW_SHARDED else jnp.transpose(gk.reshape(r, NCHIPS, cc), (1, 0, 2)).reshape(NCHIPS, r * cc))
    used = sum(p.shape[1] for p in parts)
    g4 = _cat(parts + [jnp.zeros((NCHIPS, R * PACK_COLS - used), F32)], axis=1).reshape(NCHIPS, R, PACK_COLS)
    theirs = swap_halves(g4, name="rs_swap_halves")
    pair = pair_add_half(g4, theirs, ci.reshape(1).astype(jnp.int32), name="rs_pair_add")
    landed = scatter_chips(pair, name="rs_scatter_chips")
    half = sum_leading(landed, name="rs_chip_sum")
    gshard = join_halves(half, name="rs_join_halves").reshape(-1)
    grad = {}
    off = 0
    for l in range(L):
        for k in BIG:
            r, cc = shapes[k]
            grad.setdefault(k, []).append(gshard[off:off + r * cc].reshape(r, cc))
            off += r * cc
    grad = {k: jnp.stack(vv) for k, vv in grad.items()}

    pa = _Packer()
    for l in range(L):
        for k, a in smalls[l].items():
            pa.add(f"{k}{l}", a)
    pa.add("rel", drel)
    pa.add("loss", loss)
    seq_rows = []
    for b in range(Bl):
        sp = _Packer()
        sp.add("dmods", jnp.stack([dmods[l][b] for l in range(L)], axis=0))
        sp.add("dsink", jnp.stack([dsinks[l][b] for l in range(L)], axis=0))
        seq_rows.append(sp)
    nseq = seq_rows[0].n
    buf = _cat([pa.pack()] + [sp.pack() for sp in seq_rows], axis=0)
    got = allgather8(buf, name="gather_small_grads")
    tot_a = pa.cut(sum_leading(got[:, :pa.n], name="sum_small_grads"))
    per_seq_all = got[:, pa.n:].reshape(ndev * Bl, nseq, LANES)
    tot_b = seq_rows[0].cut(sum_leading(per_seq_all, name="sum_seq_grads"))
    dm_all = jnp.transpose(seq_rows[0].cut(per_seq_all, (ndev * Bl,))["dmods"], (1, 0, 2))
    grad["w_ada"] = ada_grad(c_all, lax.dynamic_slice(dm_all, (0, 0, chip * E), (L, ndev * Bl, E)), name="ada_grad")
    grad["b_ada"] = tot_b["dmods"]
    grad["sw_sinks"] = jnp.transpose(tot_b["dsink"][:, :, :, 0], (0, 2, 1)).reshape(L, 6)
    grad["rel_table"] = tot_a["rel"][:6, :REL_BUCKETS].T
    st = lambda k: jnp.stack([tot_a[f"{k}{l}"] for l in range(L)])
    grad["norm1_g"], grad["norm2_g"] = st("n1")[:, 0], st("n2")[:, 0]
    grad["mla_cq_g"], grad["mla_ckv_g"] = st("cq_g")[:, 0], st("ckv_g")[:, 0]
    grad["mla_qn_g"], grad["mla_kn_g"] = st("qn_g")[:, 0, :MLA_QK], st("kn_g")[:, 0, :MLA_QK]
    grad["sw_qn_g"], grad["sw_kn_g"] = st("swq_g")[:, 0, :HEAD], st("swk_g")[:, 0, :HEAD]
    conv = st("conv")
    cw_cols = conv_w.shape[2]
    grad["conv_w"] = lax.dynamic_slice(conv[:, 0:3], (0, 0, chip * cw_cols), (L, 3, cw_cols))
    grad["conv_b"] = conv[:, 3]
    loss_out = tot_a["loss"][0, 0]

    delta, new_m, new_v = {}, {}, {}
    for k in BIG + ("w_ada",):
        shp = w[k].shape
        to2 = lambda a: a.reshape(-1, shp[-1])
        d_, m_, v_ = adamw(to2(w[k]), to2(grad[k]), to2(m[k]), to2(v[k]), name=f"adamw_{k}")
        delta[k], new_m[k], new_v[k] = d_.reshape(shp), m_.reshape(shp), v_.reshape(shp)
    packs = [_Packer() for _ in range(4)]
    for k in SMALL:
        for pkr, src in zip(packs, (w, grad, m, v)):
            pkr.add(k, src[k])
    outs = adamw(*[pkr.pack() for pkr in packs], name="adamw_small")
    for dst, o in zip((delta, new_m, new_v), outs):
        dst.update(packs[0].cut(o))
    return (loss_out, dx, *[grad[k] for k in WEIGHTS], *[delta[k] for k in WEIGHTS], *[new_m[k] for k in WEIGHTS], *[new_v[k] for k in WEIGHTS])
```

```python
import functools
import math

import jax
import jax.numpy as jnp
from jax import lax
from jax.experimental import pallas as pl
from jax.experimental.pallas import tpu as pltpu

F32 = jnp.float32
BF16 = jnp.bfloat16
MESH = pl.DeviceIdType.MESH

EPS = 1e-6
NEG = -1e30
HEAD = 64
LANES = 128
MLA_QK = 96
ROPE_THETA = 10000.0
REL_BUCKETS = 32
REL_MAX_DIST = 128
WINDOW = 128
D_FF = 2816
ADAM_LR, ADAM_B1, ADAM_B2, ADAM_EPS, ADAM_WD, ADAM_STEP = 0.001, 0.9, 0.999, 1e-08, 0.01, 10

VMEM_LIMIT = 56 * 1024 * 1024

P_SBQ, P_SBK, P_SBV, P_CQ, P_CKV, P_SLAB, P_SWQ, P_SWK, P_SWV, P_END = 0, 256, 512, 768, 1024, 1152, 1280, 1664, 1792, 1920
SW_PERM = (0, 3, 1, 4, 2, 5)


def _cp(*sem):
    return pltpu.CompilerParams(dimension_semantics=sem, vmem_limit_bytes=VMEM_LIMIT)


def _dot(a, b):
    return jnp.dot(a, b, preferred_element_type=F32)


def _dot_nt(a, b):
    return lax.dot_general(a, b, (((1,), (1,)), ((), ())), preferred_element_type=F32)


def _dot_tn(a, b):
    return lax.dot_general(a, b, (((0,), (0,)), ((), ())), preferred_element_type=F32)


def _split_dot(x, u):
    hi = x.astype(BF16)
    lo = (x - hi.astype(F32)).astype(BF16)
    return _dot(hi, u) + _dot(lo, u)


def _lane_masks():
    lane = lax.broadcasted_iota(jnp.int32, (1, LANES), 1)
    return (lane < HEAD, lane >= HEAD)


def _tile(n, cap, align=128):
    if n <= cap:
        return n
    t = cap - cap % align
    while t >= align:
        if n % t == 0:
            return t
        t -= align
    return n


def matmul(a, b, *, ta=False, tb=False, out_dtype=F32, tm=512, tn=512, tk=8192, name):
    M, K = (a.shape[1], a.shape[0]) if ta else a.shape
    N = b.shape[0] if tb else b.shape[1]
    tm, tn, tk = _tile(M, tm), _tile(N, tn), _tile(K, tk)
    nk = K // tk

    def body(a_ref, b_ref, o_ref, *scratch):
        av = a_ref[...].astype(BF16)
        bv = b_ref[...].astype(BF16)
        if ta:
            part = _dot_tn(av, bv)
        elif tb:
            part = _dot_nt(av, bv)
        else:
            part = _dot(av, bv)
        if nk == 1:
            o_ref[...] = part.astype(out_dtype)
        else:
            acc_ref, = scratch
            k = pl.program_id(2)

            @pl.when(k == 0)
            def _():
                acc_ref[...] = part

            @pl.when(k > 0)
            def _():
                acc_ref[...] += part

            @pl.when(k == nk - 1)
            def _():
                o_ref[...] = acc_ref[...].astype(out_dtype)

    a_spec = pl.BlockSpec((tk, tm), lambda i, j, k: (k, i)) if ta else pl.BlockSpec((tm, tk), lambda i, j, k: (i, k))
    b_spec = pl.BlockSpec((tn, tk), lambda i, j, k: (j, k)) if tb else pl.BlockSpec((tk, tn), lambda i, j, k: (k, j))
    return pl.pallas_call(
        body, name=name, grid=(M // tm, N // tn, nk),
        in_specs=[a_spec, b_spec], out_specs=pl.BlockSpec((tm, tn), lambda i, j, k: (i, j)),
        out_shape=jax.ShapeDtypeStruct((M, N), out_dtype),
        scratch_shapes=[] if nk == 1 else [pltpu.VMEM((tm, tn), F32)],
        compiler_params=_cp("parallel", "parallel", "arbitrary"),
    )(a, b)


def matmul_res(a, b, res, gate, seq, *, tm=512, tn=1024, name):
    M, K = a.shape
    N = b.shape[1]
    tm, tn = _tile(min(M, seq), tm), _tile(N, tn)
    per_seq = seq // tm

    def body(a_ref, b_ref, r_ref, g_ref, y_ref, x_ref):
        y = _dot(a_ref[...].astype(BF16), b_ref[...].astype(BF16))
        y_ref[...] = y
        x_ref[...] = r_ref[...] + g_ref[...] * y

    out = jax.ShapeDtypeStruct((M, N), F32)
    return pl.pallas_call(
        body, name=name, grid=(M // tm, N // tn),
        in_specs=[pl.BlockSpec((tm, K), lambda i, j: (i, 0)), pl.BlockSpec((K, tn), lambda i, j: (0, j)),
                  pl.BlockSpec((tm, tn), lambda i, j: (i, j)), pl.BlockSpec((None, 1, tn), lambda i, j: (lax.div(i, jnp.int32(per_seq)), 0, j))],
        out_specs=[pl.BlockSpec((tm, tn), lambda i, j: (i, j))] * 2,
        out_shape=[out, out], compiler_params=_cp("parallel", "parallel"),
    )(a, b, res, gate)


def rms_fwd(x3, blk, W, g, sc=None, sh=None, *, tm=512, name):
    Bl, S, _ = x3.shape
    tm = min(tm, S)
    mod = sc is not None

    def body(x_ref, g_ref, *rest):
        o_ref = rest[-1]
        x = x_ref[...]
        r = lax.rsqrt(jnp.mean(x * x, axis=-1, keepdims=True) + EPS)
        y = x * r * g_ref[...]
        if mod:
            y = y * (1.0 + rest[0][...]) + rest[1][...]
        o_ref[...] = y.astype(BF16)

    vec = pl.BlockSpec((None, 1, W), lambda b, s: (b, 0, 0))
    return pl.pallas_call(
        body, name=name, grid=(Bl, S // tm),
        in_specs=[pl.BlockSpec((None, tm, W), lambda b, s: (b, s, blk)), pl.BlockSpec((1, W), lambda b, s: (0, 0))] + ([vec, vec] if mod else []),
        out_specs=pl.BlockSpec((None, tm, W), lambda b, s: (b, s, 0)),
        out_shape=jax.ShapeDtypeStruct((Bl, S, W), BF16),
        compiler_params=_cp("parallel", "parallel"),
    )(x3, g, *([sc, sh] if mod else []))


def rms_bwd(x3, blk, W, dy3, g, sc=None, dres3=None, *, tm=256, name):
    Bl, S, _ = x3.shape
    tm = min(tm, S)
    mod = sc is not None
    res = dres3 is not None

    def body(*refs):
        x_ref, dy_ref, g_ref = refs[:3]
        k = 3
        sc_ref = dr_ref = None
        if mod:
            sc_ref = refs[k]
            k += 1
        if res:
            dr_ref = refs[k]
            k += 1
        dx_ref, dg_ref = refs[k], refs[k + 1]
        b, s = pl.program_id(0), pl.program_id(1)
        x = x_ref[...]
        dy = dy_ref[...].astype(F32)
        g = g_ref[...]
        r = lax.rsqrt(jnp.mean(x * x, axis=-1, keepdims=True) + EPS)
        n = x * r
        if mod:
            dsc_ref, dsh_ref = refs[k + 2], refs[k + 3]
            one_sc = 1.0 + sc_ref[...]

            @pl.when(s == 0)
            def _():
                dsc_ref[...] = jnp.zeros_like(dsc_ref)
                dsh_ref[...] = jnp.zeros_like(dsh_ref)

            dsh_ref[...] += jnp.sum(dy, axis=0, keepdims=True)
            dsc_ref[...] += jnp.sum(dy * n * g, axis=0, keepdims=True)
            dyn = dy * one_sc
        else:
            dyn = dy

        @pl.when((b == 0) & (s == 0))
        def _():
            dg_ref[...] = jnp.zeros_like(dg_ref)

        dg_ref[...] += jnp.sum(dyn * n, axis=0, keepdims=True)
        dn = dyn * g
        dx = r * (dn - n * jnp.mean(dn * n, axis=-1, keepdims=True))
        if res:
            dx = dx + dr_ref[...]
        dx_ref[...] = dx

    blkspec = pl.BlockSpec((None, tm, W), lambda b, s: (b, s, 0))
    vec = pl.BlockSpec((None, 1, W), lambda b, s: (b, 0, 0))
    row = pl.BlockSpec((1, W), lambda b, s: (0, 0))
    in_specs = [pl.BlockSpec((None, tm, W), lambda b, s: (b, s, blk)), blkspec, row] + ([vec] if mod else []) + ([blkspec] if res else [])
    out_specs = [blkspec, row] + ([vec, vec] if mod else [])
    out_shape = [jax.ShapeDtypeStruct((Bl, S, W), F32), jax.ShapeDtypeStruct((1, W), F32)]
    if mod:
        out_shape += [jax.ShapeDtypeStruct((Bl, 1, W), F32)] * 2
    args = [x3, dy3, g] + ([sc] if mod else []) + ([dres3] if res else [])
    return pl.pallas_call(
        body, name=name, grid=(Bl, S // tm), in_specs=in_specs, out_specs=out_specs, out_shape=out_shape,
        compiler_params=_cp("arbitrary", "arbitrary"),
    )(*args)


def pair_rms_fwd(x3, blk0, npairs, g2, *, tm=512, name):
    Bl, S, _ = x3.shape
    tm = min(tm, S)

    def body(x_ref, g_ref, o_ref):
        lo, hi = _lane_masks()
        x = x_ref[...]
        xx = x * x
        s0 = jnp.sum(jnp.where(lo, xx, 0.0), axis=-1, keepdims=True)
        s1 = jnp.sum(jnp.where(hi, xx, 0.0), axis=-1, keepdims=True)
        r = jnp.where(lo, lax.rsqrt(s0 / HEAD + EPS), lax.rsqrt(s1 / HEAD + EPS))
        o_ref[...] = (x * r * g_ref[...]).astype(BF16)

    return pl.pallas_call(
        body, name=name, grid=(Bl, S // tm, npairs),
        in_specs=[pl.BlockSpec((None, tm, LANES), lambda b, s, p: (b, s, blk0 + p)), pl.BlockSpec((1, LANES), lambda b, s, p: (0, 0))],
        out_specs=pl.BlockSpec((None, tm, LANES), lambda b, s, p: (b, s, p)),
        out_shape=jax.ShapeDtypeStruct((Bl, S, LANES * npairs), BF16),
        compiler_params=_cp("parallel", "parallel", "parallel"),
    )(x3, g2)


def pair_rms_bwd(x3, blk0, npairs, dy3, g2, *, tm=512, name):
    Bl, S, _ = x3.shape
    tm = min(tm, S)

    def body(x_ref, dy_ref, g_ref, dx_ref, dg_ref):
        lo, hi = _lane_masks()
        first = (pl.program_id(0) == 0) & (pl.program_id(1) == 0) & (pl.program_id(2) == 0)
        x = x_ref[...]
        dy = dy_ref[...]
        xx = x * x
        s0 = jnp.sum(jnp.where(lo, xx, 0.0), axis=-1, keepdims=True)
        s1 = jnp.sum(jnp.where(hi, xx, 0.0), axis=-1, keepdims=True)
        r = jnp.where(lo, lax.rsqrt(s0 / HEAD + EPS), lax.rsqrt(s1 / HEAD + EPS))
        n = x * r

        @pl.when(first)
        def _():
            dg_ref[...] = jnp.zeros_like(dg_ref)

        part = jnp.sum(dy * n, axis=0, keepdims=True)
        dg_ref[...] += part + pltpu.roll(part, HEAD, 1)
        dn = dy * g_ref[...]
        t = dn * n
        m0 = jnp.sum(jnp.where(lo, t, 0.0), axis=-1, keepdims=True)
        m1 = jnp.sum(jnp.where(hi, t, 0.0), axis=-1, keepdims=True)
        dx_ref[...] = r * (dn - n * (jnp.where(lo, m0, m1) / HEAD))

    return pl.pallas_call(
        body, name=name, grid=(Bl, S // tm, npairs),
        in_specs=[pl.BlockSpec((None, tm, LANES), lambda b, s, p: (b, s, blk0 + p)), pl.BlockSpec((None, tm, LANES), lambda b, s, p: (b, s, p)),
                  pl.BlockSpec((1, LANES), lambda b, s, p: (0, 0))],
        out_specs=[pl.BlockSpec((None, tm, LANES), lambda b, s, p: (b, s, p)), pl.BlockSpec((1, LANES), lambda b, s, p: (0, 0))],
        out_shape=[jax.ShapeDtypeStruct((Bl, S, LANES * npairs), F32), jax.ShapeDtypeStruct((1, LANES), F32)],
        compiler_params=_cp("arbitrary", "arbitrary", "arbitrary"),
    )(x3, dy3, g2)


def _rot(y, cos_t, sin_a, sin_b):
    return y * cos_t + pltpu.roll(y, LANES - 16, 1) * sin_a + pltpu.roll(y, 16, 1) * sin_b


def _rot_t(d, cos_t, sin_a, sin_b):
    return d * cos_t + pltpu.roll(d * sin_a, 16, 1) + pltpu.roll(d * sin_b, LANES - 16, 1)


def rope_norm_fwd(x3, nheads, g, tabs, slab=None, *, tm=512, name):
    Bl, S, _ = x3.shape
    tm = min(tm, S)
    has_slab = slab is not None

    def body(*refs):
        x_ref, g_ref, c_ref, sa_ref, sb_ref = refs[:5]
        o_ref = refs[-1]
        x = x_ref[...]
        if has_slab:
            x = x + refs[5][...]
        r = lax.rsqrt(jnp.sum(x * x, axis=-1, keepdims=True) / MLA_QK + EPS)
        o_ref[...] = _rot(x * r * g_ref[...], c_ref[...], sa_ref[...], sb_ref[...]).astype(BF16)

    head = pl.BlockSpec((None, tm, LANES), lambda b, s, h: (b, s, h))
    tab = pl.BlockSpec((None, tm, LANES), lambda b, s, h: (b, s, 0))
    in_specs = [head, pl.BlockSpec((1, LANES), lambda b, s, h: (0, 0)), tab, tab, tab]
    args = [x3, g, *tabs]
    if has_slab:
        sblk = slab[1]
        in_specs.append(pl.BlockSpec((None, tm, LANES), lambda b, s, h: (b, s, sblk)))
        args.append(slab[0])
    return pl.pallas_call(
        body, name=name, grid=(Bl, S // tm, nheads), in_specs=in_specs, out_specs=head,
        out_shape=jax.ShapeDtypeStruct((Bl, S, LANES * nheads), BF16),
        compiler_params=_cp("parallel", "parallel", "parallel"),
    )(*args)


def rope_norm_bwd(x3, nheads, dy3, g, tabs, slab=None, *, tm=512, name):
    Bl, S, _ = x3.shape
    tm = min(tm, S)
    has_slab = slab is not None

    def body(*refs):
        x_ref, dy_ref, g_ref, c_ref, sa_ref, sb_ref = refs[:6]
        k = 7 if has_slab else 6
        dx_ref, dg_ref = refs[k], refs[k + 1]
        h = pl.program_id(2)
        first = (pl.program_id(0) == 0) & (pl.program_id(1) == 0) & (h == 0)
        x = x_ref[...]
        if has_slab:
            x = x + refs[6][...]
        g = g_ref[...]
        r = lax.rsqrt(jnp.sum(x * x, axis=-1, keepdims=True) / MLA_QK + EPS)
        n = x * r
        d = _rot_t(dy_ref[...], c_ref[...], sa_ref[...], sb_ref[...])

        @pl.when(first)
        def _():
            dg_ref[...] = jnp.zeros_like(dg_ref)

        dg_ref[...] += jnp.sum(d * n, axis=0, keepdims=True)
        dn = d * g
        dx = r * (dn - n * (jnp.sum(dn * n, axis=-1, keepdims=True) / MLA_QK))
        dx_ref[...] = dx
        if has_slab:
            ds_ref = refs[k + 2]

            @pl.when(h == 0)
            def _():
                ds_ref[...] = dx

            @pl.when(h > 0)
            def _():
                ds_ref[...] += dx

    head = pl.BlockSpec((None, tm, LANES), lambda b, s, h: (b, s, h))
    tab = pl.BlockSpec((None, tm, LANES), lambda b, s, h: (b, s, 0))
    row = pl.BlockSpec((1, LANES), lambda b, s, h: (0, 0))
    in_specs = [head, head, row, tab, tab, tab]
    args = [x3, dy3, g, *tabs]
    out_specs = [head, row]
    out_shape = [jax.ShapeDtypeStruct((Bl, S, LANES * nheads), F32), jax.ShapeDtypeStruct((1, LANES), F32)]
    if has_slab:
        sblk = slab[1]
        in_specs.append(pl.BlockSpec((None, tm, LANES), lambda b, s, h: (b, s, sblk)))
        args.append(slab[0])
        out_specs.append(tab)
        out_shape.append(jax.ShapeDtypeStruct((Bl, S, LANES), F32))
    return pl.pallas_call(
        body, name=name, grid=(Bl, S // tm, nheads), in_specs=in_specs, out_specs=out_specs, out_shape=out_shape,
        compiler_params=_cp("arbitrary", "arbitrary", "arbitrary"),
    )(*args)


def _sb_tile(z, strict, u, carry_r):
    sp = jnp.maximum(z, 0.0) + jnp.log(1.0 + jnp.exp(-jnp.abs(z)))
    keep = jnp.where(strict, -sp, 0.0)
    logw = (z - sp) + _split_dot(keep, u) + carry_r
    return jnp.where(strict, jnp.exp(logw), 0.0), keep, sp


SB_BLOCK = 256


def sb_attn_fwd(proj3, *, name):
    Bl, S, _ = proj3.shape
    tq = min(SB_BLOCK, S)
    scale = HEAD ** -0.5
    qb, kb0, vb0 = P_SBQ // LANES, P_SBK // LANES, P_SBV // LANES

    def body(q_ref, k_ref, v_ref, o_ref, rt_ref):
        i = pl.program_id(2)
        masks = _lane_masks()
        lane = lax.broadcasted_iota(jnp.int32, (1, LANES), 1)
        q = q_ref[...]
        qh = [jnp.where(m, q, 0.0).astype(BF16) for m in masks]
        rr = lax.broadcasted_iota(jnp.int32, (tq, tq), 0)
        cc = lax.broadcasted_iota(jnp.int32, (tq, tq), 1)
        u = (rr > cc).astype(BF16)

        rt_ref[...] = jnp.zeros_like(rt_ref)

        def step(t, carry):
            r0, r1, acc = carry
            j = i - t
            off = pl.multiple_of(j * tq, tq)
            kb = k_ref[pl.ds(off, tq), :].astype(BF16)
            vb = v_ref[pl.ds(off, tq), :]
            strict = (cc + j * tq) < (rr + i * tq)
            rt_ref[...] = jnp.where(lane == j, r0, jnp.where(lane == j + HEAD, r1, rt_ref[...]))
            rs = [r0, r1]
            for h in range(2):
                z = _dot_nt(qh[h], kb) * scale
                w, keep, _ = _sb_tile(z, strict, u, rs[h])
                acc = acc + _dot(w.astype(BF16), jnp.where(masks[h], vb, 0.0).astype(BF16))
                rs[h] = rs[h] + jnp.sum(keep, axis=1, keepdims=True)
            return rs[0], rs[1], acc

        zero = jnp.zeros((tq, 1), F32)
        _, _, acc = lax.fori_loop(0, i + 1, step, (zero, zero, jnp.zeros((tq, LANES), F32)))
        o_ref[...] = acc

    seq = lambda blk0: pl.BlockSpec((None, S, LANES), lambda b, p, i: (b, 0, blk0 + p))
    out = pl.BlockSpec((None, tq, LANES), lambda b, p, i: (b, i, p))
    shp = jax.ShapeDtypeStruct((Bl, S, 2 * LANES), F32)
    return pl.pallas_call(
        body, name=name, grid=(Bl, 2, S // tq),
        in_specs=[pl.BlockSpec((None, tq, LANES), lambda b, p, i: (b, i, qb + p)), seq(kb0), seq(vb0)],
        out_specs=[out, out], out_shape=[shp, shp],
        compiler_params=_cp("parallel", "parallel", "arbitrary"),
    )(proj3, proj3, proj3)


def sb_attn_bwd(proj3, rt3, do3, *, name):
    Bl, S, _ = proj3.shape
    tq = min(SB_BLOCK, S)
    scale = HEAD ** -0.5
    qb, kb0, vb0 = P_SBQ // LANES, P_SBK // LANES, P_SBV // LANES

    def body(q_ref, k_ref, v_ref, rt_ref, do_ref, dq_ref, dk_ref, dv_ref):
        i = pl.program_id(2)

        @pl.when(i == 0)
        def _():
            dk_ref[...] = jnp.zeros_like(dk_ref)
            dv_ref[...] = jnp.zeros_like(dv_ref)

        masks = _lane_masks()
        lane = lax.broadcasted_iota(jnp.int32, (1, LANES), 1)
        q = q_ref[...]
        qh = [jnp.where(m, q, 0.0).astype(BF16) for m in masks]
        do_b = do_ref[...].astype(BF16)
        doh = [jnp.where(m, do_b, jnp.zeros_like(do_b)) for m in masks]
        rt = rt_ref[...]
        rr = lax.broadcasted_iota(jnp.int32, (tq, tq), 0)
        cc = lax.broadcasted_iota(jnp.int32, (tq, tq), 1)
        u_suffix = (rr > cc).astype(BF16)
        u_prefix = (rr < cc).astype(BF16)

        def step(j, carry):
            p0, p1, dq = carry
            off = pl.multiple_of(j * tq, tq)
            kf = k_ref[pl.ds(off, tq), :]
            kb = kf.astype(BF16)
            vb = v_ref[pl.ds(off, tq), :]
            strict = (cc + j * tq) < (rr + i * tq)
            ps = [p0, p1]
            dk_acc = jnp.zeros((tq, LANES), F32)
            dv_acc = jnp.zeros((tq, LANES), F32)
            for h in range(2):
                r_j = jnp.sum(jnp.where(lane == j + h * HEAD, rt, 0.0), axis=1, keepdims=True)
                z = _dot_nt(qh[h], kb) * scale
                w, _, sp = _sb_tile(z, strict, u_suffix, r_j)
                vh = jnp.where(masks[h], vb, 0.0).astype(BF16)
                g = _dot_nt(doh[h], vh) * w
                pre = _split_dot(g, u_prefix) + ps[h]
                dz = jnp.where(strict, g * jnp.exp(-sp) - jnp.exp(z - sp) * pre, 0.0) * scale
                dzb = dz.astype(BF16)
                dq = dq + _dot(dzb, jnp.where(masks[h], kf, 0.0).astype(BF16))
                dk_acc = dk_acc + _dot_tn(dzb, qh[h])
                dv_acc = dv_acc + _dot_tn(w.astype(BF16), doh[h])
                ps[h] = ps[h] + jnp.sum(g, axis=1, keepdims=True)
            dk_ref[pl.ds(off, tq), :] += dk_acc
            dv_ref[pl.ds(off, tq), :] += dv_acc
            return ps[0], ps[1], dq

        zero = jnp.zeros((tq, 1), F32)
        out = lax.fori_loop(0, i + 1, step, (zero, zero, jnp.zeros((tq, LANES), F32)))
        dq_ref[...] = out[2]

    seq_in = lambda blk0: pl.BlockSpec((None, S, LANES), lambda b, p, i: (b, 0, blk0 + p))
    blk = pl.BlockSpec((None, tq, LANES), lambda b, p, i: (b, i, p))
    seq_out = pl.BlockSpec((None, S, LANES), lambda b, p, i: (b, 0, p))
    shp = jax.ShapeDtypeStruct((Bl, S, 2 * LANES), F32)
    return pl.pallas_call(
        body, name=name, grid=(Bl, 2, S // tq),
        in_specs=[pl.BlockSpec((None, tq, LANES), lambda b, p, i: (b, i, qb + p)), seq_in(kb0), seq_in(vb0), blk, blk],
        out_specs=[blk, seq_out, seq_out], out_shape=[shp, shp, shp],
        compiler_params=_cp("parallel", "parallel", "arbitrary"),
    )(proj3, proj3, proj3, rt3, do3)


def mla_attn_fwd(q3, k3, kv3, vblk0, *, tq=256, name):
    Bl, S, _ = q3.shape
    tq = min(tq, S)
    scale = MLA_QK ** -0.5

    def body(q_ref, k_ref, v_ref, o_ref, lse_ref):
        i = pl.program_id(2)
        masks = _lane_masks()
        rr = lax.broadcasted_iota(jnp.int32, (tq, tq), 0)
        cc = lax.broadcasted_iota(jnp.int32, (tq, tq), 1)
        qh = [q_ref[:, h * LANES:(h + 1) * LANES] for h in range(2)]

        def step(j, carry):
            m0, l0, m1, l1, acc = carry
            off = pl.multiple_of(j * tq, tq)
            vb = v_ref[pl.ds(off, tq), :]
            causal = (cc + j * tq) <= (rr + i * tq)
            ms, ls, alphas = [m0, m1], [l0, l1], []
            add = jnp.zeros((tq, LANES), F32)
            for h in range(2):
                kh = k_ref[pl.ds(off, tq), h * LANES:(h + 1) * LANES]
                s = jnp.where(causal, _dot_nt(qh[h], kh) * scale, NEG)
                m_new = jnp.maximum(ms[h], jnp.max(s, axis=1, keepdims=True))
                p = jnp.exp(s - m_new)
                alpha = jnp.exp(ms[h] - m_new)
                ls[h] = alpha * ls[h] + jnp.sum(p, axis=1, keepdims=True)
                ms[h] = m_new
                alphas.append(alpha)
                add = add + _dot(p.astype(BF16), jnp.where(masks[h], vb, 0.0).astype(BF16))
            acc = acc * jnp.where(masks[0], alphas[0], alphas[1]) + add
            return ms[0], ls[0], ms[1], ls[1], acc

        neg = jnp.full((tq, 1), NEG, F32)
        zero = jnp.zeros((tq, 1), F32)
        m0, l0, m1, l1, acc = lax.fori_loop(0, i + 1, step, (neg, zero, neg, zero, jnp.zeros((tq, LANES), F32)))
        o_ref[...] = acc / jnp.where(masks[0], l0, l1)
        lse_ref[...] = jnp.where(masks[0], m0 + jnp.log(l0), m1 + jnp.log(l1))

    out = pl.BlockSpec((None, tq, LANES), lambda b, p, i: (b, i, p))
    shp = jax.ShapeDtypeStruct((Bl, S, 3 * LANES), F32)
    return pl.pallas_call(
        body, name=name, grid=(Bl, 3, S // tq),
        in_specs=[pl.BlockSpec((None, tq, 2 * LANES), lambda b, p, i: (b, i, p)), pl.BlockSpec((None, S, 2 * LANES), lambda b, p, i: (b, 0, p)),
                  pl.BlockSpec((None, S, LANES), lambda b, p, i: (b, 0, vblk0 + p))],
        out_specs=[out, out], out_shape=[shp, shp],
        compiler_params=_cp("parallel", "parallel", "arbitrary"),
    )(q3, k3, kv3)


def mla_attn_bwd(q3, k3, kv3, vblk0, o3, lse3, do3, *, tq=256, name):
    Bl, S, _ = q3.shape
    tq = min(tq, S)
    nq = S // tq
    scale = MLA_QK ** -0.5

    def body(q_ref, k_ref, v_ref, o_ref, lse_ref, do_ref, dq_ref, dk_ref, dv_ref):
        j = pl.program_id(2)

        @pl.when(j == 0)
        def _():
            dq_ref[...] = jnp.zeros_like(dq_ref)

        masks = _lane_masks()
        rr = lax.broadcasted_iota(jnp.int32, (tq, tq), 0)
        cc = lax.broadcasted_iota(jnp.int32, (tq, tq), 1)
        vb = v_ref[...]
        vh = [jnp.where(m, vb, 0.0).astype(BF16) for m in masks]
        kh = [k_ref[:, h * LANES:(h + 1) * LANES] for h in range(2)]

        def step(t, carry):
            dk0, dk1, dv = carry
            i = j + t
            off = pl.multiple_of(i * tq, tq)
            causal = (cc + j * tq) <= (rr + i * tq)
            do_b = do_ref[pl.ds(off, tq), :].astype(BF16)
            prod = do_b.astype(F32) * o_ref[pl.ds(off, tq), :]
            lse = lse_ref[pl.ds(off, tq), :]
            dks = [dk0, dk1]
            for h in range(2):
                qh = q_ref[pl.ds(off, tq), h * LANES:(h + 1) * LANES]
                doh = jnp.where(masks[h], do_b, jnp.zeros_like(do_b))
                delta = jnp.sum(jnp.where(masks[h], prod, 0.0), axis=1, keepdims=True)
                lse_h = lse[:, h * HEAD:h * HEAD + 1]
                s = jnp.where(causal, _dot_nt(qh, kh[h]) * scale, NEG)
                p = jnp.exp(s - lse_h)
                ds = (p * (_dot_nt(doh, vh[h]) - delta) * scale).astype(BF16)
                dq_ref[pl.ds(off, tq), h * LANES:(h + 1) * LANES] += _dot(ds, kh[h])
                dks[h] = dks[h] + _dot_tn(ds, qh)
                dv = dv + _dot_tn(p.astype(BF16), doh)
            return dks[0], dks[1], dv

        zero = jnp.zeros((tq, LANES), F32)
        dk0, dk1, dv = lax.fori_loop(0, nq - j, step, (zero, zero, zero))
        dk_ref[:, 0:LANES] = dk0
        dk_ref[:, LANES:2 * LANES] = dk1
        dv_ref[...] = dv

    seq1 = pl.BlockSpec((None, S, LANES), lambda b, p, j: (b, 0, p))
    seq2 = pl.BlockSpec((None, S, 2 * LANES), lambda b, p, j: (b, 0, p))
    return pl.pallas_call(
        body, name=name, grid=(Bl, 3, nq),
        in_specs=[seq2, pl.BlockSpec((None, tq, 2 * LANES), lambda b, p, j: (b, j, p)),
                  pl.BlockSpec((None, tq, LANES), lambda b, p, j: (b, j, vblk0 + p)), seq1, seq1, seq1],
        out_specs=[seq2, pl.BlockSpec((None, tq, 2 * LANES), lambda b, p, j: (b, j, p)), pl.BlockSpec((None, tq, LANES), lambda b, p, j: (b, j, p))],
        out_shape=[jax.ShapeDtypeStruct((Bl, S, 6 * LANES), F32), jax.ShapeDtypeStruct((Bl, S, 6 * LANES), F32), jax.ShapeDtypeStruct((Bl, S, 3 * LANES), F32)],
        compiler_params=_cp("parallel", "parallel", "arbitrary"),
    )(q3, k3, kv3, o3, lse3, do3)


def _bucket_table():
    a = jnp.arange(WINDOW)[:, None]
    b = jnp.arange(2 * WINDOW)[None, :]
    dist = WINDOW + a - b
    max_exact = REL_BUCKETS // 2
    n = jnp.maximum(dist, 0)
    nf = jnp.maximum(n, 1).astype(F32)
    large = max_exact + (jnp.log(nf / max_exact) / math.log(REL_MAX_DIST / max_exact) * (REL_BUCKETS - max_exact)).astype(jnp.int32)
    large = jnp.minimum(large, REL_BUCKETS - 1)
    bucket = jnp.where(n < max_exact, n, large)
    return jnp.where((dist >= 0) & (dist < WINDOW), bucket, -1).astype(jnp.int32)


def swa_bias(rel_flat, bucket, *, name):
    def body(t_ref, b_ref, o_ref):
        bk = b_ref[...]
        for p in range(3):
            for hh in range(2):
                h = hh * 3 + p
                acc = jnp.full(bk.shape, NEG, F32)
                for b in range(REL_BUCKETS):
                    acc = jnp.where(bk == b, t_ref[b * 6 + h], acc)
                o_ref[p, hh] = acc

    return pl.pallas_call(
        body, name=name,
        in_specs=[pl.BlockSpec(memory_space=pltpu.SMEM), pl.BlockSpec(memory_space=pltpu.VMEM)],
        out_specs=pl.BlockSpec(memory_space=pltpu.VMEM),
        out_shape=jax.ShapeDtypeStruct((3, 2, WINDOW, 2 * WINDOW), F32),
    )(rel_flat, bucket)


def swa_bias_bwd(dbias, bucket, *, name):
    Bl = dbias.shape[0]

    def body(d_ref, b_ref, o_ref):
        bk = b_ref[...]
        lane = lax.broadcasted_iota(jnp.int32, (1, LANES), 1)
        rows = []
        for h in range(6):
            hh, p = divmod(h, 3)
            d = d_ref[0, p, hh]
            for bl in range(1, Bl):
                d = d + d_ref[bl, p, hh]
            row = jnp.zeros((1, LANES), F32)
            for b in range(REL_BUCKETS):
                s = jnp.sum(jnp.sum(jnp.where(bk == b, d, 0.0), axis=1, keepdims=True), axis=0, keepdims=True)
                row = row + jnp.where(lane == b, s, 0.0)
            rows.append(row)
        rows += [jnp.zeros((1, LANES), F32)] * 2
        o_ref[...] = jnp.concatenate(rows, axis=0)

    return pl.pallas_call(
        body, name=name,
        in_specs=[pl.BlockSpec(memory_space=pltpu.VMEM)] * 2, out_specs=pl.BlockSpec(memory_space=pltpu.VMEM),
        out_shape=jax.ShapeDtypeStruct((8, LANES), F32),
    )(dbias, bucket)


def _swa_specs(vblk):
    cur = lambda blk: pl.BlockSpec((None, WINDOW, LANES), lambda b, p, n: (b, n, blk))
    prev = lambda blk: pl.BlockSpec((None, WINDOW, LANES), lambda b, p, n: (b, jnp.maximum(n - 1, 0), blk))
    return [pl.BlockSpec((None, WINDOW, LANES), lambda b, p, n: (b, n, p)), cur(0), prev(0), cur(vblk), prev(vblk),
            pl.BlockSpec((None, 2, WINDOW, 2 * WINDOW), lambda b, p, n: (p, 0, 0, 0)), pl.BlockSpec((None, 2, LANES), lambda b, p, n: (p, 0, 0))]


def _swa_logits(qh, kp, kc, bias_h, first, scale):
    sp = jnp.where(first, NEG, _dot_nt(qh, kp) * scale + bias_h[:, :WINDOW])
    sc = _dot_nt(qh, kc) * scale + bias_h[:, WINDOW:]
    return sp, sc


def swa_attn_fwd(qn3, kn3, proj3, bias, sinks, *, name):
    Bl, S, _ = qn3.shape
    scale = HEAD ** -0.5

    def body(q_ref, kc_ref, kp_ref, vc_ref, vp_ref, b_ref, s_ref, o_ref, lse_ref):
        first = pl.program_id(2) == 0
        masks = _lane_masks()
        q = q_ref[...]
        o = jnp.zeros((WINDOW, LANES), F32)
        lses = []
        for h in range(2):
            qh = jnp.where(masks[h], q, jnp.zeros_like(q))
            sp, sc = _swa_logits(qh, kp_ref[...], kc_ref[...], b_ref[h], first, scale)
            sink = s_ref[h:h + 1, 0:1]
            m = jnp.maximum(jnp.maximum(jnp.max(sp, axis=1, keepdims=True), jnp.max(sc, axis=1, keepdims=True)), sink)
            ep, ec = jnp.exp(sp - m), jnp.exp(sc - m)
            l = jnp.sum(ep, axis=1, keepdims=True) + jnp.sum(ec, axis=1, keepdims=True) + jnp.exp(sink - m)
            inv = 1.0 / l
            o = o + _dot((ep * inv).astype(BF16), jnp.where(masks[h], vp_ref[...], 0.0).astype(BF16))
            o = o + _dot((ec * inv).astype(BF16), jnp.where(masks[h], vc_ref[...], 0.0).astype(BF16))
            lses.append(m + jnp.log(l))
        o_ref[...] = o
        lse_ref[...] = jnp.where(masks[0], lses[0], lses[1])

    out = pl.BlockSpec((None, WINDOW, LANES), lambda b, p, n: (b, n, p))
    shp = jax.ShapeDtypeStruct((Bl, S, 3 * LANES), F32)
    return pl.pallas_call(
        body, name=name, grid=(Bl, 3, S // WINDOW), in_specs=_swa_specs(P_SWV // LANES),
        out_specs=[out, out], out_shape=[shp, shp], compiler_params=_cp("parallel", "parallel", "arbitrary"),
    )(qn3, kn3, kn3, proj3, proj3, bias, sinks)


def swa_attn_bwd(qn3, kn3, proj3, bias, sinks, o3, lse3, do3, *, name):
    Bl, S, _ = qn3.shape
    scale = HEAD ** -0.5

    def body(q_ref, kc_ref, kp_ref, vc_ref, vp_ref, b_ref, s_ref, o_ref, lse_ref, do_ref,
             dq_ref, dk_ref, dv_ref, db_ref, dsk_ref):
        p_id, n = pl.program_id(1), pl.program_id(2)
        first = n == 0

        @pl.when((p_id == 0) & first)
        def _():
            dk_ref[...] = jnp.zeros_like(dk_ref)
            dv_ref[...] = jnp.zeros_like(dv_ref)

        @pl.when(first)
        def _():
            db_ref[...] = jnp.zeros_like(db_ref)
            dsk_ref[...] = jnp.zeros_like(dsk_ref)

        masks = _lane_masks()
        q = q_ref[...]
        kc, kp = kc_ref[...], kp_ref[...]
        do_b = do_ref[...].astype(BF16)
        prod = do_b.astype(F32) * o_ref[...]
        lse = lse_ref[...]
        dq = jnp.zeros((WINDOW, LANES), F32)
        dkp = jnp.zeros((WINDOW, LANES), F32)
        dkc = jnp.zeros((WINDOW, LANES), F32)
        dvp = jnp.zeros((WINDOW, LANES), F32)
        dvc = jnp.zeros((WINDOW, LANES), F32)
        for h in range(2):
            qh = jnp.where(masks[h], q, jnp.zeros_like(q))
            doh = jnp.where(masks[h], do_b, jnp.zeros_like(do_b))
            sp, sc = _swa_logits(qh, kp, kc, b_ref[h], first, scale)
            lse_h = lse[:, h * HEAD:h * HEAD + 1]
            pp, pc = jnp.exp(sp - lse_h), jnp.exp(sc - lse_h)
            delta = jnp.sum(jnp.where(masks[h], prod, 0.0), axis=1, keepdims=True)
            dsp = pp * (_dot_nt(doh, jnp.where(masks[h], vp_ref[...], 0.0).astype(BF16)) - delta)
            dsc = pc * (_dot_nt(doh, jnp.where(masks[h], vc_ref[...], 0.0).astype(BF16)) - delta)
            db_ref[h, :, 0:WINDOW] += dsp
            db_ref[h, :, WINDOW:2 * WINDOW] += dsc
            psink = jnp.exp(s_ref[h:h + 1, 0:1] - lse_h)
            dsk_ref[h:h + 1, :] += jnp.broadcast_to(-jnp.sum(psink * delta, axis=0, keepdims=True), (1, LANES))
            dspb, dscb = (dsp * scale).astype(BF16), (dsc * scale).astype(BF16)
            dq = dq + _dot(dspb, jnp.where(masks[h], kp, jnp.zeros_like(kp))) + _dot(dscb, jnp.where(masks[h], kc, jnp.zeros_like(kc)))
            dkp = dkp + _dot_tn(dspb, qh)
            dkc = dkc + _dot_tn(dscb, qh)
            dvp = dvp + _dot_tn(pp.astype(BF16), doh)
            dvc = dvc + _dot_tn(pc.astype(BF16), doh)
        dq_ref[...] = dq
        offp = pl.multiple_of(jnp.maximum(n - 1, 0) * WINDOW, WINDOW)
        offc = pl.multiple_of(n * WINDOW, WINDOW)
        dk_ref[pl.ds(offp, WINDOW), :] += dkp
        dv_ref[pl.ds(offp, WINDOW), :] += dvp
        dk_ref[pl.ds(offc, WINDOW), :] += dkc
        dv_ref[pl.ds(offc, WINDOW), :] += dvc

    blk = pl.BlockSpec((None, WINDOW, LANES), lambda b, p, n: (b, n, p))
    seq = pl.BlockSpec((None, S, LANES), lambda b, p, n: (b, 0, 0))
    return pl.pallas_call(
        body, name=name, grid=(Bl, 3, S // WINDOW), in_specs=_swa_specs(P_SWV // LANES) + [blk, blk, blk],
        out_specs=[blk, seq, seq, pl.BlockSpec((None, None, 2, WINDOW, 2 * WINDOW), lambda b, p, n: (b, p, 0, 0, 0)),
                   pl.BlockSpec((None, None, 2, LANES), lambda b, p, n: (b, p, 0, 0))],
        out_shape=[jax.ShapeDtypeStruct((Bl, S, 3 * LANES), F32), jax.ShapeDtypeStruct((Bl, S, LANES), F32), jax.ShapeDtypeStruct((Bl, S, LANES), F32),
                   jax.ShapeDtypeStruct((Bl, 3, 2, WINDOW, 2 * WINDOW), F32), jax.ShapeDtypeStruct((Bl, 3, 2, LANES), F32)],
        compiler_params=_cp("arbitrary", "arbitrary", "arbitrary"),
    )(qn3, kn3, kn3, proj3, proj3, bias, sinks, o3, lse3, do3)


def _conv_rows(x, halo, w_ref, b_ref, first_blk):
    rows = lax.broadcasted_iota(jnp.int32, x.shape, 0)
    h6 = jnp.where(first_blk, 0.0, halo[6:7, :])
    h7 = jnp.where(first_blk, 0.0, halo[7:8, :])
    x1 = jnp.where(rows == 0, h7, pltpu.roll(x, 1, 0))
    x2 = jnp.where(rows == 0, h6, jnp.where(rows == 1, h7, pltpu.roll(x, 2, 0)))
    return w_ref[0:1, :] * x2 + w_ref[1:2, :] * x1 + w_ref[2:3, :] * x + b_ref[...], x1, x2


FF_BLK = D_FF // 2


def _up_perm(a):
    q = FF_BLK
    return _cat([a[..., 0:q], a[..., 2 * q:3 * q], a[..., q:2 * q], a[..., 3 * q:4 * q]])


def conv_gate_fwd(up3, cw, cb, *, tm=256, name):
    Bl, S, _ = up3.shape
    tm = min(tm, S)
    W = 2 * FF_BLK

    def body(x_ref, h_ref, w_ref, b_ref, o_ref):
        u, _, _ = _conv_rows(x_ref[...], h_ref[...], w_ref, b_ref, pl.program_id(1) == 0)
        ug, uv = u[:, :FF_BLK], u[:, FF_BLK:]
        o_ref[...] = (ug * jax.nn.sigmoid(ug) * uv).astype(BF16)

    hb = tm // 8
    return pl.pallas_call(
        body, name=name, grid=(Bl, S // tm, 2),
        in_specs=[pl.BlockSpec((None, tm, W), lambda b, s, c: (b, s, c)),
                  pl.BlockSpec((None, 8, W), lambda b, s, c: (b, jnp.maximum(s * hb - 1, 0), c)),
                  pl.BlockSpec((3, W), lambda b, s, c: (0, c)), pl.BlockSpec((1, W), lambda b, s, c: (0, c))],
        out_specs=pl.BlockSpec((None, tm, FF_BLK), lambda b, s, c: (b, s, c)),
        out_shape=jax.ShapeDtypeStruct((Bl, S, D_FF), BF16),
        compiler_params=_cp("parallel", "parallel", "parallel"),
    )(up3, up3, cw, cb)


def conv_gate_bwd(up3, cw, cb, da3, *, tm=256, name):
    Bl, S, _ = up3.shape
    tm = min(tm, S)
    ns = S // tm
    W = 2 * FF_BLK

    def body(x_ref, h_ref, w_ref, b_ref, da_ref, dup_ref, dw_ref, nxt_ref):
        b, s = pl.program_id(1), pl.program_id(2)
        seq_end = s == 0

        @pl.when((b == 0) & seq_end)
        def _():
            dw_ref[...] = jnp.zeros_like(dw_ref)

        x = x_ref[...]
        u, x1, x2 = _conv_rows(x, h_ref[...], w_ref, b_ref, s == ns - 1)
        ug, uv = u[:, :FF_BLK], u[:, FF_BLK:]
        da = da_ref[...].astype(F32)
        sg = jax.nn.sigmoid(ug)
        du = _cat([da * uv * sg * (1.0 + ug * (1.0 - sg)), da * ug * sg])
        dw_ref[0:1, :] += jnp.sum(du * x2, axis=0, keepdims=True)
        dw_ref[1:2, :] += jnp.sum(du * x1, axis=0, keepdims=True)
        dw_ref[2:3, :] += jnp.sum(du * x, axis=0, keepdims=True)
        dw_ref[3:4, :] += jnp.sum(du, axis=0, keepdims=True)
        rows = lax.broadcasted_iota(jnp.int32, du.shape, 0)
        n0 = jnp.where(seq_end, 0.0, nxt_ref[0:1, :])
        n1 = jnp.where(seq_end, 0.0, nxt_ref[1:2, :])
        d1 = jnp.where(rows == tm - 1, n0, pltpu.roll(du, tm - 1, 0))
        d2 = jnp.where(rows == tm - 1, n1, jnp.where(rows == tm - 2, n0, pltpu.roll(du, tm - 2, 0)))
        dup_ref[...] = (w_ref[2:3, :] * du + w_ref[1:2, :] * d1 + w_ref[0:1, :] * d2).astype(BF16)
        nxt_ref[...] = du[0:8, :]

    hb = tm // 8
    rb = lambda s: ns - 1 - s
    return pl.pallas_call(
        body, name=name, grid=(2, Bl, ns),
        in_specs=[pl.BlockSpec((None, tm, W), lambda c, b, s: (b, rb(s), c)),
                  pl.BlockSpec((None, 8, W), lambda c, b, s: (b, jnp.maximum(rb(s) * hb - 1, 0), c)),
                  pl.BlockSpec((3, W), lambda c, b, s: (0, c)), pl.BlockSpec((1, W), lambda c, b, s: (0, c)),
                  pl.BlockSpec((None, tm, FF_BLK), lambda c, b, s: (b, rb(s), c))],
        out_specs=[pl.BlockSpec((None, tm, W), lambda c, b, s: (b, rb(s), c)), pl.BlockSpec((8, W), lambda c, b, s: (0, c))],
        out_shape=[jax.ShapeDtypeStruct((Bl, S, 2 * D_FF), BF16), jax.ShapeDtypeStruct((8, 2 * D_FF), F32)],
        scratch_shapes=[pltpu.VMEM((8, W), F32)],
        compiler_params=_cp("arbitrary", "arbitrary", "arbitrary"),
    )(up3, up3, cw, cb, da3)


def gate_bwd(dx3, y3, gate, *, tm=512, name):
    Bl, S, D = dx3.shape
    tm = min(tm, S)

    def body(dx_ref, y_ref, g_ref, o_ref, dg_ref):
        @pl.when(pl.program_id(1) == 0)
        def _():
            dg_ref[...] = jnp.zeros_like(dg_ref)

        dx = dx_ref[...]
        dg_ref[...] += jnp.sum(dx * y_ref[...], axis=0, keepdims=True)
        o_ref[...] = (dx * g_ref[...]).astype(BF16)

    blk = pl.BlockSpec((None, tm, D), lambda b, s: (b, s, 0))
    vec = pl.BlockSpec((None, 1, D), lambda b, s: (b, 0, 0))
    return pl.pallas_call(
        body, name=name, grid=(Bl, S // tm), in_specs=[blk, blk, vec], out_specs=[blk, vec],
        out_shape=[jax.ShapeDtypeStruct((Bl, S, D), BF16), jax.ShapeDtypeStruct((Bl, 1, D), F32)],
        compiler_params=_cp("parallel", "arbitrary"),
    )(dx3, y3, gate)


def loss_grad(y3, t3, *, tm=512, name):
    Bl, S, D = y3.shape
    tm = min(tm, S)
    last = (Bl - 1, S // tm - 1)

    def body(y_ref, t_ref, dy_ref, l_ref, acc_ref):
        b, s = pl.program_id(0), pl.program_id(1)

        @pl.when((b == 0) & (s == 0))
        def _():
            acc_ref[...] = jnp.zeros_like(acc_ref)

        e = y_ref[...] - t_ref[...]
        dy_ref[...] = e * (1.0 / D)
        acc_ref[...] += jnp.sum(e * e, axis=0, keepdims=True)

        @pl.when((b == last[0]) & (s == last[1]))
        def _():
            l_ref[...] = jnp.broadcast_to(jnp.sum(acc_ref[...], axis=1, keepdims=True) * (0.5 / D), (1, LANES))

    blk = pl.BlockSpec((None, tm, D), lambda b, s: (b, s, 0))
    return pl.pallas_call(
        body, name=name, grid=(Bl, S // tm), in_specs=[blk, blk],
        out_specs=[blk, pl.BlockSpec((1, LANES), lambda b, s: (0, 0))],
        out_shape=[jax.ShapeDtypeStruct((Bl, S, D), F32), jax.ShapeDtypeStruct((1, LANES), F32)],
        scratch_shapes=[pltpu.VMEM((1, D), F32)], compiler_params=_cp("arbitrary", "arbitrary"),
    )(y3, t3)


def adamw(w, g, m, v, *, name):
    R, C = w.shape
    tr = R
    for cand in (512, 256, 128, 64, 32, 16, 8):
        if R > cand and R % cand == 0:
            tr = cand
            break
    c1 = 1.0 / (1.0 - ADAM_B1 ** ADAM_STEP)
    c2 = 1.0 / (1.0 - ADAM_B2 ** ADAM_STEP)

    def body(w_ref, g_ref, m_ref, v_ref, d_ref, m2_ref, v2_ref):
        gg = g_ref[...]
        m2 = ADAM_B1 * m_ref[...] + (1.0 - ADAM_B1) * gg
        v2 = ADAM_B2 * v_ref[...] + (1.0 - ADAM_B2) * (gg * gg)
        m2_ref[...] = m2
        v2_ref[...] = v2
        d_ref[...] = -ADAM_LR * ((m2 * c1) / (jnp.sqrt(v2 * c2) + ADAM_EPS) + ADAM_WD * w_ref[...])

    blk = pl.BlockSpec((tr, C), lambda i: (i, 0))
    shp = jax.ShapeDtypeStruct((R, C), F32)
    return pl.pallas_call(
        body, name=name, grid=(R // tr,), in_specs=[blk] * 4, out_specs=[blk] * 3, out_shape=[shp] * 3,
        compiler_params=_cp("parallel"),
    )(w, g, m, v)


def sum_leading(x, *, out_dtype=F32, tr=256, name):
    n, R, C = x.shape
    tr = _tile(R, tr, 16)

    def body(x_ref, o_ref):
        acc = x_ref[0].astype(F32)
        for k in range(1, n):
            acc = acc + x_ref[k].astype(F32)
        o_ref[...] = acc.astype(out_dtype)

    return pl.pallas_call(
        body, name=name, grid=(R // tr,), in_specs=[pl.BlockSpec((n, tr, C), lambda i: (0, i, 0))],
        out_specs=pl.BlockSpec((tr, C), lambda i: (i, 0)), out_shape=jax.ShapeDtypeStruct((R, C), out_dtype),
        compiler_params=_cp("parallel"),
    )(x)


def pair_add_half(g4, recv, c_arr, *, tr=128, name):
    _, R, C = g4.shape
    H = R // 2
    tr = _tile(H, tr, 16)
    nb = H // tr

    def body(c_ref, g_ref, r_ref, o_ref):
        o_ref[...] = (g_ref[...] + r_ref[...]).astype(BF16)

    grid_spec = pltpu.PrefetchScalarGridSpec(
        num_scalar_prefetch=1, grid=(4, nb),
        in_specs=[pl.BlockSpec((None, tr, C), lambda k, i, c_ref: (k, c_ref[0] * nb + i, 0)),
                  pl.BlockSpec((None, tr, C), lambda k, i, c_ref: (k, i, 0))],
        out_specs=pl.BlockSpec((None, tr, C), lambda k, i, c_ref: (k, i, 0)),
    )
    return pl.pallas_call(
        body, name=name, grid_spec=grid_spec, out_shape=jax.ShapeDtypeStruct((4, H, C), BF16),
        compiler_params=_cp("parallel", "parallel"),
    )(c_arr, g4, recv)


def mods_matmul(c_all, w_ada, b_ada_cols, *, tn=512, name):
    L, D, E = w_ada.shape
    nb = c_all.shape[0]
    tn = _tile(E, tn)

    def body(c_ref, w_ref, b_ref, o_ref):
        c = c_ref[...]
        a = c * jax.nn.sigmoid(c)
        o_ref[...] = jnp.dot(a, w_ref[...], preferred_element_type=F32, precision=lax.Precision.HIGHEST) + b_ref[...]

    return pl.pallas_call(
        body, name=name, grid=(L, E // tn),
        in_specs=[pl.BlockSpec((nb, D), lambda l, j: (0, 0)), pl.BlockSpec((None, D, tn), lambda l, j: (l, 0, j)),
                  pl.BlockSpec((None, 1, tn), lambda l, j: (l, 0, j))],
        out_specs=pl.BlockSpec((None, nb, tn), lambda l, j: (l, 0, j)),
        out_shape=jax.ShapeDtypeStruct((L, nb, E), F32), compiler_params=_cp("parallel", "parallel"),
    )(c_all, w_ada, b_ada_cols)


def ada_grad(c_all, dmods, *, tn=512, name):
    L, nb, E = dmods.shape
    D = c_all.shape[1]
    tn = _tile(E, tn)

    def body(c_ref, d_ref, o_ref):
        c = c_ref[...]
        a = c * jax.nn.sigmoid(c)
        o_ref[...] = lax.dot_general(a, d_ref[...], (((0,), (0,)), ((), ())), preferred_element_type=F32, precision=lax.Precision.HIGHEST)

    return pl.pallas_call(
        body, name=name, grid=(L, E // tn),
        in_specs=[pl.BlockSpec((nb, D), lambda l, j: (0, 0)), pl.BlockSpec((None, nb, tn), lambda l, j: (l, 0, j))],
        out_specs=pl.BlockSpec((None, D, tn), lambda l, j: (l, 0, j)),
        out_shape=jax.ShapeDtypeStruct((L, D, E), F32), compiler_params=_cp("parallel", "parallel"),
    )(c_all, dmods)


HBM = pl.BlockSpec(memory_space=pltpu.HBM)


def _me():
    return lax.axis_index("x"), lax.axis_index("y"), lax.axis_index("c")


def _flip(v, bit):
    return 1 - v if bit else v


def allgather8(xs, *, name):
    n, w = xs.shape

    def body(x_ref, out_ref, send_sems, recv_sems, local_sem):
        x, y, c = _me()
        me = 4 * x + 2 * y + c
        mine = pltpu.make_async_copy(x_ref, out_ref.at[me], local_sem)
        mine.start()
        sends = []
        for k in range(1, 8):
            peer = (_flip(x, k & 4), _flip(y, k & 2), _flip(c, k & 1))
            cp = pltpu.make_async_remote_copy(src_ref=x_ref, dst_ref=out_ref.at[me], send_sem=send_sems.at[k - 1],
                                              recv_sem=recv_sems.at[k - 1], device_id=peer, device_id_type=MESH)
            cp.start()
            sends.append(cp)
        for k in range(1, 8):
            peer = (_flip(x, k & 4), _flip(y, k & 2), _flip(c, k & 1))
            src = 4 * peer[0] + 2 * peer[1] + peer[2]
            pltpu.make_async_remote_copy(src_ref=x_ref, dst_ref=out_ref.at[src], send_sem=send_sems.at[k - 1],
                                         recv_sem=recv_sems.at[k - 1], device_id=peer, device_id_type=MESH).wait_recv()
        for cp in sends:
            cp.wait_send()
        mine.wait()

    return pl.pallas_call(
        body, name=name, in_specs=[pl.BlockSpec(memory_space=pltpu.VMEM)], out_specs=pl.BlockSpec(memory_space=pltpu.VMEM),
        out_shape=jax.ShapeDtypeStruct((8, n, w), xs.dtype),
        scratch_shapes=[pltpu.SemaphoreType.DMA((7,)), pltpu.SemaphoreType.DMA((7,)), pltpu.SemaphoreType.DMA],
    )(xs)


LOCAL_CHUNKS = 8


def _copy_via_vmem(src, dst_at, rows, buf, sem):
    ch = buf.shape[0]
    for i in range(rows // ch):
        load = pltpu.make_async_copy(src.at[pl.ds(i * ch, ch)], buf, sem)
        load.start()
        load.wait()
        store = pltpu.make_async_copy(buf, dst_at(i * ch, ch), sem)
        store.start()
        store.wait()


def gather_weights(wflat, *, name):
    R, C = wflat.shape
    H = R // 2

    def body(x_ref, out_ref, send_sems, recv_sems, buf, local_sem):
        x, y, c = _me()
        j = 2 * x + y
        my_half = pl.ds(c * H, H)
        sib_half = pl.ds((1 - c) * H, H)
        chips = [(_flip(x, k & 2), _flip(y, k & 1)) for k in range(1, 4)]
        sends = []
        for k, (px, py) in enumerate(chips):
            cp = pltpu.make_async_remote_copy(src_ref=x_ref.at[my_half], dst_ref=out_ref.at[j, my_half], send_sem=send_sems.at[k],
                                              recv_sem=recv_sems.at[k], device_id=(px, py, c), device_id_type=MESH)
            cp.start()
            sends.append(cp)
        _copy_via_vmem(x_ref, lambda o, n: out_ref.at[j, pl.ds(o, n)], R, buf, local_sem)
        for k, (px, py) in enumerate(chips):
            slot = out_ref.at[2 * px + py, my_half]
            pltpu.make_async_remote_copy(src_ref=slot, dst_ref=slot, send_sem=send_sems.at[k], recv_sem=recv_sems.at[k],
                                         device_id=(px, py, c), device_id_type=MESH).wait_recv()
            cp = pltpu.make_async_remote_copy(src_ref=slot, dst_ref=slot, send_sem=send_sems.at[3 + k], recv_sem=recv_sems.at[3 + k],
                                              device_id=(x, y, 1 - c), device_id_type=MESH)
            cp.start()
            sends.append(cp)
        for k, (px, py) in enumerate(chips):
            slot = out_ref.at[2 * px + py, sib_half]
            pltpu.make_async_remote_copy(src_ref=slot, dst_ref=slot, send_sem=send_sems.at[3 + k], recv_sem=recv_sems.at[3 + k],
                                         device_id=(x, y, 1 - c), device_id_type=MESH).wait_recv()
        for cp in sends:
            cp.wait_send()

    return pl.pallas_call(
        body, name=name, in_specs=[HBM], out_specs=HBM, out_shape=jax.ShapeDtypeStruct((4, R, C), wflat.dtype),
        scratch_shapes=[pltpu.SemaphoreType.DMA((6,)), pltpu.SemaphoreType.DMA((6,)), pltpu.VMEM((R // LOCAL_CHUNKS, C), wflat.dtype),
                        pltpu.SemaphoreType.DMA],
    )(wflat)


def swap_halves(g4, *, name):
    n, R, C = g4.shape
    H = R // 2

    def body(g_ref, out_ref, send_sems, recv_sems):
        x, y, c = _me()
        sib = (x, y, 1 - c)
        sends = []
        for k in range(n):
            cp = pltpu.make_async_remote_copy(src_ref=g_ref.at[k, pl.ds((1 - c) * H, H)], dst_ref=out_ref.at[k], send_sem=send_sems.at[k],
                                              recv_sem=recv_sems.at[k], device_id=sib, device_id_type=MESH)
            cp.start()
            sends.append(cp)
        for k in range(n):
            pltpu.make_async_remote_copy(src_ref=g_ref.at[k, pl.ds(c * H, H)], dst_ref=out_ref.at[k], send_sem=send_sems.at[k],
                                         recv_sem=recv_sems.at[k], device_id=sib, device_id_type=MESH).wait_recv()
        for cp in sends:
            cp.wait_send()

    return pl.pallas_call(
        body, name=name, in_specs=[HBM], out_specs=HBM, out_shape=jax.ShapeDtypeStruct((n, H, C), g4.dtype),
        scratch_shapes=[pltpu.SemaphoreType.DMA((n,)), pltpu.SemaphoreType.DMA((n,))],
    )(g4)


def scatter_chips(p4, *, name):
    n, H, C = p4.shape

    def body(p_ref, out_ref, send_sems, recv_sems, buf, local_sem):
        x, y, c = _me()
        j = 2 * x + y
        chips = [(_flip(x, k & 2), _flip(y, k & 1)) for k in range(1, 4)]
        sends = []
        for k, (px, py) in enumerate(chips):
            cp = pltpu.make_async_remote_copy(src_ref=p_ref.at[2 * px + py], dst_ref=out_ref.at[j], send_sem=send_sems.at[k],
                                              recv_sem=recv_sems.at[k], device_id=(px, py, c), device_id_type=MESH)
            cp.start()
            sends.append(cp)
        _copy_via_vmem(p_ref.at[j], lambda o, n_: out_ref.at[j, pl.ds(o, n_)], H, buf, local_sem)
        for k, (px, py) in enumerate(chips):
            slot = out_ref.at[2 * px + py]
            pltpu.make_async_remote_copy(src_ref=slot, dst_ref=slot, send_sem=send_sems.at[k], recv_sem=recv_sems.at[k],
                                         device_id=(px, py, c), device_id_type=MESH).wait_recv()
        for cp in sends:
            cp.wait_send()

    return pl.pallas_call(
        body, name=name, in_specs=[HBM], out_specs=HBM, out_shape=jax.ShapeDtypeStruct((n, H, C), p4.dtype),
        scratch_shapes=[pltpu.SemaphoreType.DMA((3,)), pltpu.SemaphoreType.DMA((3,)), pltpu.VMEM((H // LOCAL_CHUNKS, C), p4.dtype),
                        pltpu.SemaphoreType.DMA],
    )(p4)


def join_halves(half, *, name):
    H, C = half.shape

    def body(h_ref, out_ref, send_sem, recv_sem, buf, local_sem):
        x, y, c = _me()
        sib = (x, y, 1 - c)
        cp = pltpu.make_async_remote_copy(src_ref=h_ref, dst_ref=out_ref.at[pl.ds(c * H, H)], send_sem=send_sem, recv_sem=recv_sem,
                                          device_id=sib, device_id_type=MESH)
        cp.start()
        _copy_via_vmem(h_ref, lambda o, n: out_ref.at[pl.ds(c * H + o, n)], H, buf, local_sem)
        pltpu.make_async_remote_copy(src_ref=h_ref, dst_ref=out_ref.at[pl.ds((1 - c) * H, H)], send_sem=send_sem, recv_sem=recv_sem,
                                     device_id=sib, device_id_type=MESH).wait_recv()
        cp.wait_send()

    return pl.pallas_call(
        body, name=name, in_specs=[HBM], out_specs=HBM, out_shape=jax.ShapeDtypeStruct((2 * H, C), half.dtype),
        scratch_shapes=[pltpu.SemaphoreType.DMA, pltpu.SemaphoreType.DMA, pltpu.VMEM((H // LOCAL_CHUNKS, C), half.dtype),
                        pltpu.SemaphoreType.DMA],
    )(half)


def _cat(parts, axis=-1):
    return jnp.concatenate(parts, axis=axis)


def _prep_w_in(w):
    z = lambda n: jnp.zeros((w.shape[0], n), w.dtype)
    swq = w[:, 1184:1568]
    return _cat([w[:, 0:1152], z(64), w[:, 1152:1184], z(32)] + [swq[:, HEAD * h:HEAD * (h + 1)] for h in SW_PERM] + [w[:, 1568:1824]])


def _unprep_w_in(g):
    swq = g[:, P_SWQ:P_SWK]
    return _cat([g[:, 0:1152], g[:, 1216:1248]] + [swq[:, HEAD * SW_PERM.index(h):HEAD * (SW_PERM.index(h) + 1)] for h in range(6)] + [g[:, P_SWK:P_END]])


def _prep_w_uq(w):
    z = jnp.zeros((w.shape[0], 32), w.dtype)
    return _cat([p for h in range(6) for p in (w[:, MLA_QK * h:MLA_QK * (h + 1)], z)])


def _unprep_w_uq(g):
    return _cat([g[:, LANES * h:LANES * h + MLA_QK] for h in range(6)])


def _prep_w_ukv(w):
    z = jnp.zeros((w.shape[0], HEAD), w.dtype)
    return _cat([p for h in range(6) for p in (w[:, LANES * h:LANES * h + HEAD], z)] + [w[:, LANES * h + HEAD:LANES * (h + 1)] for h in range(6)])


def _unprep_w_ukv(g):
    return _cat([p for h in range(6) for p in (g[:, LANES * h:LANES * h + HEAD], g[:, 768 + HEAD * h:768 + HEAD * (h + 1)])])


def _prep_w_out(w):
    return _cat([w[0:640]] + [w[640 + HEAD * h:640 + HEAD * (h + 1)] for h in SW_PERM], axis=0)


def _unprep_w_out(g):
    return _cat([g[0:640]] + [g[640 + HEAD * SW_PERM.index(h):640 + HEAD * (SW_PERM.index(h) + 1)] for h in range(6)], axis=0)


def _rope_tables(positions):
    half = 16
    inv_freq = jnp.power(ROPE_THETA, -jnp.arange(half, dtype=F32) / half)
    ang = positions.astype(F32)[..., None] * inv_freq
    cos, sin = jnp.cos(ang), jnp.sin(ang)
    z = lambda n: jnp.zeros(ang.shape[:-1] + (n,), F32)
    return (_cat([jnp.ones(ang.shape[:-1] + (HEAD,), F32), cos, cos, z(32)]), _cat([z(HEAD), -sin, z(16), z(32)]), _cat([z(HEAD), z(16), sin, z(32)]))


def _small_params(p):
    pad96 = lambda g: _cat([g, jnp.zeros((32,), F32)]).reshape(1, LANES)
    two = lambda g: _cat([g, g]).reshape(1, LANES)
    sinks = jnp.broadcast_to(p["sw_sinks"].reshape(2, 3).T[:, :, None], (3, 2, LANES))
    return dict(n1=p["norm1_g"].reshape(1, -1), n2=p["norm2_g"].reshape(1, -1), cq_g=p["mla_cq_g"].reshape(1, -1),
                ckv_g=p["mla_ckv_g"].reshape(1, -1), qn_g=pad96(p["mla_qn_g"]), kn_g=pad96(p["mla_kn_g"]),
                swq_g=two(p["sw_qn_g"]), swk_g=two(p["sw_kn_g"]), sinks=sinks, conv_b=_up_perm(p["conv_b"]).reshape(1, -1))


def _layer_fwd(x3, md, W, tabs, bias, tag):
    Bl, S, D = x3.shape
    T = Bl * S
    n = lambda s: f"{s}_{tag}"
    two = lambda a: a.reshape(T, a.shape[-1])
    three = lambda a: a.reshape(Bl, S, a.shape[-1])
    h = rms_fwd(x3, 0, D, W["n1"], md["scale1"], md["shift1"], name=n("norm1"))
    proj = three(matmul(two(h), W["w_in"], tn=1920, name=n("in_proj")))
    o_a, rt_a = sb_attn_fwd(proj, name=n("sb_fwd"))
    cqn = rms_fwd(proj, P_CQ // 256, 256, W["cq_g"], name=n("cq_norm"))
    ckvn = rms_fwd(proj, P_CKV // LANES, LANES, W["ckv_g"], name=n("ckv_norm"))
    qb = three(matmul(two(cqn), W["w_uq"], tm=1024, tn=768, name=n("uq")))
    kvb = three(matmul(two(ckvn), W["w_ukv"], tm=1024, tn=1152, name=n("ukv")))
    q_m = rope_norm_fwd(qb, 6, W["qn_g"], tabs, name=n("q_rope"))
    k_m = rope_norm_fwd(kvb, 6, W["kn_g"], tabs, (proj, P_SLAB // LANES), name=n("k_rope"))
    o_b, lse_b = mla_attn_fwd(q_m, k_m, kvb, 6, name=n("mla_fwd"))
    q_c = pair_rms_fwd(proj, P_SWQ // LANES, 3, W["swq_g"], name=n("swq_norm"))
    k_c = pair_rms_fwd(proj, P_SWK // LANES, 1, W["swk_g"], name=n("swk_norm"))
    o_c, lse_c = swa_attn_fwd(q_c, k_c, proj, bias, W["sinks"], name=n("swa_fwd"))
    mix = _cat([o_a, o_b, o_c]).astype(BF16)
    att, x1 = matmul_res(two(mix), W["w_out"], two(x3), md["gate1"], S, name=n("out_proj"))
    x1 = three(x1)
    h2 = rms_fwd(x1, 0, D, W["n2"], md["scale2"], md["shift2"], name=n("norm2"))
    up = three(matmul(two(h2), W["w_up"], tn=1408, name=n("up_proj")))
    a = conv_gate_fwd(up, W["conv_w"], W["conv_b"], name=n("conv_gate"))
    yd, x2 = matmul_res(two(a), W["w_down"], two(x1), md["gate2"], S, name=n("down_proj"))
    saved = dict(x=x3, h=h, proj=proj, rt_a=rt_a, cqn=cqn, ckvn=ckvn, qb=qb, kvb=kvb, q_m=q_m, k_m=k_m, o_b=o_b, lse_b=lse_b,
                 q_c=q_c, k_c=k_c, o_c=o_c, lse_c=lse_c, mix=mix, att=three(att), x1=x1, h2=h2, up=up, a=a, yd=three(yd))
    return three(x2), saved


def _layer_bwd(dx2, sv, md, W, tabs, bias, tag):
    Bl, S, D = dx2.shape
    T = Bl * S
    n = lambda s: f"{s}_{tag}"
    two = lambda a: a.reshape(T, a.shape[-1])
    three = lambda a: a.reshape(Bl, S, a.shape[-1])
    g = {}
    dyb, dgate2 = gate_bwd(dx2, sv["yd"], md["gate2"], name=n("gate2_bwd"))
    da = three(matmul(two(dyb), W["w_down"], tb=True, tn=1408, name=n("down_dx")))
    g["w_down"] = matmul(two(sv["a"]), two(dyb), ta=True, tm=256, tn=1024, name=n("down_dw"))
    dup, dcw = conv_gate_bwd(sv["up"], W["conv_w"], W["conv_b"], da, name=n("conv_gate_bwd"))
    dh2 = three(matmul(two(dup), W["w_up"], tb=True, tn=1024, name=n("up_dx")))
    g["w_up"] = matmul(two(sv["h2"]), two(dup), ta=True, tn=1408, name=n("up_dw"))
    dx1, dn2, dsc2, dsh2 = rms_bwd(sv["x1"], 0, D, dh2, W["n2"], md["scale2"], dx2, name=n("norm2_bwd"))
    dmo, dgate1 = gate_bwd(dx1, sv["att"], md["gate1"], name=n("gate1_bwd"))
    dmix = three(matmul(two(dmo), W["w_out"], tb=True, tn=1024, out_dtype=BF16, name=n("out_dx")))
    g["w_out"] = matmul(two(sv["mix"]), two(dmo), ta=True, tn=1024, name=n("out_dw"))
    proj = sv["proj"]
    dq_a, dk_a, dv_a = sb_attn_bwd(proj, sv["rt_a"], dmix[:, :, 0:256], name=n("sb_bwd"))
    dq_m, dk_m, dv_b = mla_attn_bwd(sv["q_m"], sv["k_m"], sv["kvb"], 6, sv["o_b"], sv["lse_b"], dmix[:, :, 256:640], name=n("mla_bwd"))
    dqb, dqn = rope_norm_bwd(sv["qb"], 6, dq_m, W["qn_g"], tabs, name=n("q_rope_bwd"))
    dkn_x, dkn, dslab = rope_norm_bwd(sv["kvb"], 6, dk_m, W["kn_g"], tabs, (proj, P_SLAB // LANES), name=n("k_rope_bwd"))
    dkvb = _cat([dkn_x, dv_b]).astype(BF16)
    dckvn = three(matmul(two(dkvb), W["w_ukv"], tb=True, tm=1024, name=n("ukv_dx")))
    g["w_ukv"] = matmul(two(sv["ckvn"]), two(dkvb), ta=True, tn=1152, name=n("ukv_dw"))
    dcqn = three(matmul(two(dqb), W["w_uq"], tb=True, tm=1024, name=n("uq_dx")))
    g["w_uq"] = matmul(two(sv["cqn"]), two(dqb), ta=True, tn=768, name=n("uq_dw"))
    dcq, dcq_g = rms_bwd(proj, P_CQ // 256, 256, dcqn, W["cq_g"], name=n("cq_norm_bwd"))
    dckv, dckv_g = rms_bwd(proj, P_CKV // LANES, LANES, dckvn, W["ckv_g"], name=n("ckv_norm_bwd"))
    dq_c, dk_c, dv_c, dbias, dsink = swa_attn_bwd(sv["q_c"], sv["k_c"], proj, bias, W["sinks"], sv["o_c"], sv["lse_c"], dmix[:, :, 640:1024], name=n("swa_bwd"))
    dswq, dswq_g = pair_rms_bwd(proj, P_SWQ // LANES, 3, dq_c, W["swq_g"], name=n("swq_norm_bwd"))
    dswk, dswk_g = pair_rms_bwd(proj, P_SWK // LANES, 1, dk_c, W["swk_g"], name=n("swk_norm_bwd"))
    dproj = _cat([dq_a, dk_a, dv_a, dcq, dckv, dslab, dswq, dswk, dv_c]).astype(BF16)
    dh = three(matmul(two(dproj), W["w_in"], tb=True, tn=1024, name=n("in_dx")))
    g["w_in"] = matmul(two(sv["h"]), two(dproj), ta=True, tn=1920, tk=2048, name=n("in_dw"))
    dx, dn1, dsc1, dsh1 = rms_bwd(sv["x"], 0, D, dh, W["n1"], md["scale1"], dx1, name=n("norm1_bwd"))
    small = dict(n1=dn1, n2=dn2, cq_g=dcq_g, ckv_g=dckv_g, qn_g=dqn, kn_g=dkn, swq_g=dswq_g, swk_g=dswk_g, conv=dcw)
    dmods = _cat([dsh1, dsc1, dgate1, dsh2, dsc2, dgate2]).reshape(Bl, 6 * D)
    return dx, g, small, dmods, dbias, dsink


BIG = ("w_in", "w_uq", "w_ukv", "w_out", "w_up", "w_down")
ROW_SHARDED = ("w_out", "w_down")
PREP = dict(w_in=_prep_w_in, w_uq=_prep_w_uq, w_ukv=_prep_w_ukv, w_out=_prep_w_out, w_up=_up_perm, w_down=lambda w: w)
UNPREP = dict(w_in=_unprep_w_in, w_uq=_unprep_w_uq, w_ukv=_unprep_w_ukv, w_out=_unprep_w_out, w_up=_up_perm, w_down=lambda w: w)
PACK_COLS = 1024
NCHIPS = 4


def _packed_rows(shapes, L):
    total = L * sum(r * c for r, c in shapes.values())
    rows = -(-total // PACK_COLS)
    return -(-rows // 256) * 256


def _local_step(x, target, positions, mods, Wl, rel_flat):
    Bl, S, D = x.shape
    L = len(Wl)
    tabs = _rope_tables(positions)
    bucket = _bucket_table()
    bias = swa_bias(rel_flat, bucket, name="swa_bias")
    mds = []
    for l in range(L):
        parts = [mods[l, :, D * k:D * (k + 1)].reshape(Bl, 1, D) for k in range(6)]
        mds.append(dict(zip(("shift1", "scale1", "gate1", "shift2", "scale2", "gate2"), parts)))
    saved = []
    h = x
    for l in range(L):
        h, sv = _layer_fwd(h, mds[l], Wl[l], tabs, bias, f"l{l}")
        saved.append(sv)
    dy, loss = loss_grad(h, target, name="loss")
    grads, smalls, dmods, dbiases, dsinks = [None] * L, [None] * L, [None] * L, [None] * L, [None] * L
    for l in reversed(range(L)):
        dy, grads[l], smalls[l], dmods[l], dbiases[l], dsinks[l] = _layer_bwd(dy, saved[l], mds[l], Wl[l], tabs, bias, f"l{l}")
    drel = swa_bias_bwd(_cat(dbiases, axis=0), bucket, name="swa_bias_bwd")
    return loss, dy, grads, smalls, dmods, dsinks, drel


def _rows(a):
    flat = a.reshape(-1)
    pad = (-flat.shape[0]) % LANES
    if pad:
        flat = _cat([flat, jnp.zeros((pad,), flat.dtype)])
    return flat.reshape(-1, LANES)


class _Packer:
    def __init__(self):
        self.items, self.n = [], 0

    def add(self, name, a):
        r = _rows(a)
        pad = (-r.shape[0]) % 8
        if pad:
            r = _cat([r, jnp.zeros((pad, LANES), r.dtype)], axis=0)
        self.items.append((name, self.n, a.shape, r))
        self.n += r.shape[0]

    def pack(self):
        return _cat([it[3] for it in self.items], axis=0)

    def cut(self, buf, lead=()):
        out = {}
        for name, off, shape, r in self.items:
            size = math.prod(shape)
            seg = buf[..., off:off + r.shape[0], :].reshape(lead + (-1,))[..., :size]
            out[name] = seg.reshape(lead + tuple(shape))
        return out


WEIGHTS = ("rel_table", "norm1_g", "norm2_g", "w_ada", "b_ada", "w_in", "mla_cq_g", "w_uq", "mla_ckv_g", "w_ukv", "mla_qn_g", "mla_kn_g",
           "sw_qn_g", "sw_kn_g", "sw_sinks", "w_out", "w_up", "conv_w", "conv_b", "w_down")
SMALL = tuple(n for n in WEIGHTS if n not in BIG + ("w_ada",))


def kernel(x, c, positions, rel_table, norm1_g, norm2_g, w_ada, b_ada, w_in, mla_cq_g, w_uq, mla_ckv_g, w_ukv, mla_qn_g, mla_kn_g, sw_qn_g, sw_kn_g, sw_sinks, w_out, w_up, conv_w, conv_b, w_down, loss_target, m_rel_table, m_norm1_g, m_norm2_g, m_w_ada, m_b_ada, m_w_in, m_mla_cq_g, m_w_uq, m_mla_ckv_g, m_w_ukv, m_mla_qn_g, m_mla_kn_g, m_sw_qn_g, m_sw_kn_g, m_sw_sinks, m_w_out, m_w_up, m_conv_w, m_conv_b, m_w_down, v_rel_table, v_norm1_g, v_norm2_g, v_w_ada, v_b_ada, v_w_in, v_mla_cq_g, v_w_uq, v_mla_ckv_g, v_w_ukv, v_mla_qn_g, v_mla_kn_g, v_sw_qn_g, v_sw_kn_g, v_sw_sinks, v_w_out, v_w_up, v_conv_w, v_conv_b, v_w_down):
    w = dict(rel_table=rel_table, norm1_g=norm1_g, norm2_g=norm2_g, w_ada=w_ada, b_ada=b_ada, w_in=w_in, mla_cq_g=mla_cq_g, w_uq=w_uq,
             mla_ckv_g=mla_ckv_g, w_ukv=w_ukv, mla_qn_g=mla_qn_g, mla_kn_g=mla_kn_g, sw_qn_g=sw_qn_g, sw_kn_g=sw_kn_g, sw_sinks=sw_sinks,
             w_out=w_out, w_up=w_up, conv_w=conv_w, conv_b=conv_b, w_down=w_down)
    m = dict(rel_table=m_rel_table, norm1_g=m_norm1_g, norm2_g=m_norm2_g, w_ada=m_w_ada, b_ada=m_b_ada, w_in=m_w_in, mla_cq_g=m_mla_cq_g,
             w_uq=m_w_uq, mla_ckv_g=m_mla_ckv_g, w_ukv=m_w_ukv, mla_qn_g=m_mla_qn_g, mla_kn_g=m_mla_kn_g, sw_qn_g=m_sw_qn_g,
             sw_kn_g=m_sw_kn_g, sw_sinks=m_sw_sinks, w_out=m_w_out, w_up=m_w_up, conv_w=m_conv_w, conv_b=m_conv_b, w_down=m_w_down)
    v = dict(rel_table=v_rel_table, norm1_g=v_norm1_g, norm2_g=v_norm2_g, w_ada=v_w_ada, b_ada=v_b_ada, w_in=v_w_in, mla_cq_g=v_mla_cq_g,
             w_uq=v_w_uq, mla_ckv_g=v_mla_ckv_g, w_ukv=v_w_ukv, mla_qn_g=v_mla_qn_g, mla_kn_g=v_mla_kn_g, sw_qn_g=v_sw_qn_g,
             sw_kn_g=v_sw_kn_g, sw_sinks=v_sw_sinks, w_out=v_w_out, w_up=v_w_up, conv_w=v_conv_w, conv_b=v_conv_b, w_down=v_w_down)
    Bl, S, D = x.shape
    L = norm1_g.shape[0]
    xi, yi, ci = _me()
    chip = 2 * xi + yi
    dev = 4 * xi + 2 * yi + ci
    ndev = 2 * NCHIPS

    shapes = {k: w[k].shape[1:] for k in BIG}
    R = _packed_rows(shapes, L)
    flat = _cat([w[k][l].reshape(-1) for l in range(L) for k in BIG])
    flat = _cat([flat, jnp.zeros((R * PACK_COLS - flat.shape[0],), F32)]).astype(BF16)
    w4 = gather_weights(flat.reshape(R, PACK_COLS), name="gather_weights").reshape(NCHIPS, R * PACK_COLS)
    full = [dict() for _ in range(L)]
    off = 0
    for l in range(L):
        for k in BIG:
            r, cc = shapes[k]
            seg = w4[:, off:off + r * cc].reshape(NCHIPS, r, cc)
            off += r * cc
            fw = seg.reshape(NCHIPS * r, cc) if k in ROW_SHARDED else jnp.transpose(seg, (1, 0, 2)).reshape(r, NCHIPS * cc)
            full[l][k] = PREP[k](fw)

    pk = _Packer()
    pk.add("c", c)
    pk.add("conv_w", conv_w)
    got = pk.cut(allgather8(pk.pack(), name="gather_cond"), (ndev,))
    c_all = got["c"].reshape(ndev * Bl, D)
    conv_full = jnp.transpose(got["conv_w"][0::2], (1, 2, 0, 3)).reshape(L, 3, -1)
    E = w_ada.shape[2]
    b_cols = lax.dynamic_slice(b_ada, (0, chip * E), (L, E)).reshape(L, 1, E)
    mods_cols = mods_matmul(c_all, w_ada, b_cols, name="mods")
    mods_all = allgather8(_rows(mods_cols), name="gather_mods")[0::2].reshape(NCHIPS, L, ndev * Bl, E)
    mods_all = jnp.transpose(mods_all, (1, 2, 0, 3)).reshape(L, ndev * Bl, NCHIPS * E)
    mods = lax.dynamic_slice(mods_all, (0, dev * Bl, 0), (L, Bl, NCHIPS * E))

    Wl = []
    for l in range(L):
        Wd = _small_params({k: w[k][l] for k in SMALL if k not in ("rel_table", "b_ada", "conv_w")})
        Wd.update(full[l])
        Wd["conv_w"] = _up_perm(conv_full[l])
        Wl.append(Wd)

    loss, dx, grads, smalls, dmods, dsinks, drel = _local_step(x, loss_target, positions, mods, Wl, rel_table.reshape(-1))

    parts = []
    for l in range(L):
        for k in BIG:
            gk = UNPREP[k](grads[l][k])
            r, cc = shapes[k]
            parts.append(gk.reshape(NCHIPS, r * cc) if k in ROW_SHARDED else jnp.transpose(gk.reshape(r, NCHIPS, cc), (1, 0, 2)).reshape(NCHIPS, r * cc))
    used = sum(p.shape[1] for p in parts)
    g4 = _cat(parts + [jnp.zeros((NCHIPS, R * PACK_COLS - used), F32)], axis=1).reshape(NCHIPS, R, PACK_COLS)
    theirs = swap_halves(g4, name="rs_swap_halves")
    pair = pair_add_half(g4, theirs, ci.reshape(1).astype(jnp.int32), name="rs_pair_add")
    landed = scatter_chips(pair, name="rs_scatter_chips")
    half = sum_leading(landed, name="rs_chip_sum")
    gshard = join_halves(half, name="rs_join_halves").reshape(-1)
    grad = {}
    off = 0
    for l in range(L):
        for k in BIG:
            r, cc = shapes[k]
            grad.setdefault(k, []).append(gshard[off:off + r * cc].reshape(r, cc))
            off += r * cc
    grad = {k: jnp.stack(vv) for k, vv in grad.items()}

    pa = _Packer()
    for l in range(L):
        for k, a in smalls[l].items():
            pa.add(f"{k}{l}", a)
    pa.add("rel", drel)
    pa.add("loss", loss)
    seq_rows = []
    for b in range(Bl):
        sp = _Packer()
        sp.add("dmods", jnp.stack([dmods[l][b] for l in range(L)], axis=0))
        sp.add("dsink", jnp.stack([dsinks[l][b] for l in range(L)], axis=0))
        seq_rows.append(sp)
    nseq = seq_rows[0].n
    buf = _cat([pa.pack()] + [sp.pack() for sp in seq_rows], axis=0)
    got = allgather8(buf, name="gather_small_grads")
    tot_a = pa.cut(sum_leading(got[:, :pa.n], name="sum_small_grads"))
    per_seq_all = got[:, pa.n:].reshape(ndev * Bl, nseq, LANES)
    tot_b = seq_rows[0].cut(sum_leading(per_seq_all, name="sum_seq_grads"))
    dm_all = jnp.transpose(seq_rows[0].cut(per_seq_all, (ndev * Bl,))["dmods"], (1, 0, 2))
    grad["w_ada"] = ada_grad(c_all, lax.dynamic_slice(dm_all, (0, 0, chip * E), (L, ndev * Bl, E)), name="ada_grad")
    grad["b_ada"] = tot_b["dmods"]
    grad["sw_sinks"] = jnp.transpose(tot_b["dsink"][:, :, :, 0], (0, 2, 1)).reshape(L, 6)
    grad["rel_table"] = tot_a["rel"][:6, :REL_BUCKETS].T
    st = lambda k: jnp.stack([tot_a[f"{k}{l}"] for l in range(L)])
    grad["norm1_g"], grad["norm2_g"] = st("n1")[:, 0], st("n2")[:, 0]
    grad["mla_cq_g"], grad["mla_ckv_g"] = st("cq_g")[:, 0], st("ckv_g")[:, 0]
    grad["mla_qn_g"], grad["mla_kn_g"] = st("qn_g")[:, 0, :MLA_QK], st("kn_g")[:, 0, :MLA_QK]
    grad["sw_qn_g"], grad["sw_kn_g"] = st("swq_g")[:, 0, :HEAD], st("swk_g")[:, 0, :HEAD]
    conv = _up_perm(st("conv"))
    cw_cols = conv_w.shape[2]
    grad["conv_w"] = lax.dynamic_slice(conv[:, 0:3], (0, 0, chip * cw_cols), (L, 3, cw_cols))
    grad["conv_b"] = conv[:, 3]
    loss_out = tot_a["loss"][0, 0]

    delta, new_m, new_v = {}, {}, {}
    for k in BIG + ("w_ada",):
        shp = w[k].shape
        to2 = lambda a: a.reshape(-1, shp[-1])
        d_, m_, v_ = adamw(to2(w[k]), to2(grad[k]), to2(m[k]), to2(v[k]), name=f"adamw_{k}")
        delta[k], new_m[k], new_v[k] = d_.reshape(shp), m_.reshape(shp), v_.reshape(shp)
    packs = [_Packer() for _ in range(4)]
    for k in SMALL:
        for pkr, src in zip(packs, (w, grad, m, v)):
            pkr.add(k, src[k])
    outs = adamw(*[pkr.pack() for pkr in packs], name="adamw_small")
    for dst, o in zip((delta, new_m, new_v), outs):
        dst.update(packs[0].cut(o))
    return (loss_out, dx, *[grad[k] for k in WEIGHTS], *[delta[k] for k in WEIGHTS], *[new_m[k] for k in WEIGHTS], *[new_v[k] for k in WEIGHTS])
```

```python
import functools
import math

import jax
import jax.numpy as jnp
from jax import lax
from jax.experimental import pallas as pl
from jax.experimental.pallas import tpu as pltpu

F32 = jnp.float32
BF16 = jnp.bfloat16
MESH = pl.DeviceIdType.MESH

EPS = 1e-6
NEG = -1e30
HEAD = 64
LANES = 128
MLA_QK = 96
ROPE_THETA = 10000.0
REL_BUCKETS = 32
REL_MAX_DIST = 128
WINDOW = 128
D_FF = 2816
ADAM_LR, ADAM_B1, ADAM_B2, ADAM_EPS, ADAM_WD, ADAM_STEP = 0.001, 0.9, 0.999, 1e-08, 0.01, 10

VMEM_LIMIT = 56 * 1024 * 1024

P_SBQ, P_SBK, P_SBV, P_CQ, P_CKV, P_SLAB, P_SWQ, P_SWK, P_SWV, P_END = 0, 256, 512, 768, 1024, 1152, 1280, 1664, 1792, 1920
SW_PERM = (0, 3, 1, 4, 2, 5)


def _cp(*sem):
    return pltpu.CompilerParams(dimension_semantics=sem, vmem_limit_bytes=VMEM_LIMIT)


def _dot(a, b):
    return jnp.dot(a, b, preferred_element_type=F32)


def _dot_nt(a, b):
    return lax.dot_general(a, b, (((1,), (1,)), ((), ())), preferred_element_type=F32)


def _dot_tn(a, b):
    return lax.dot_general(a, b, (((0,), (0,)), ((), ())), preferred_element_type=F32)


def _split_dot(x, u):
    hi = x.astype(BF16)
    lo = (x - hi.astype(F32)).astype(BF16)
    return _dot(hi, u) + _dot(lo, u)


def _lane_masks():
    lane = lax.broadcasted_iota(jnp.int32, (1, LANES), 1)
    return (lane < HEAD, lane >= HEAD)


def _tile(n, cap, align=128):
    if n <= cap:
        return n
    t = cap - cap % align
    while t >= align:
        if n % t == 0:
            return t
        t -= align
    return n


def matmul(a, b, *, ta=False, tb=False, out_dtype=F32, tm=512, tn=512, tk=8192, name):
    M, K = (a.shape[1], a.shape[0]) if ta else a.shape
    N = b.shape[0] if tb else b.shape[1]
    tm, tn, tk = _tile(M, tm), _tile(N, tn), _tile(K, tk)
    nk = K // tk

    def body(a_ref, b_ref, o_ref, *scratch):
        av = a_ref[...].astype(BF16)
        bv = b_ref[...].astype(BF16)
        if ta:
            part = _dot_tn(av, bv)
        elif tb:
            part = _dot_nt(av, bv)
        else:
            part = _dot(av, bv)
        if nk == 1:
            o_ref[...] = part.astype(out_dtype)
        else:
            acc_ref, = scratch
            k = pl.program_id(2)

            @pl.when(k == 0)
            def _():
                acc_ref[...] = part

            @pl.when(k > 0)
            def _():
                acc_ref[...] += part

            @pl.when(k == nk - 1)
            def _():
                o_ref[...] = acc_ref[...].astype(out_dtype)

    a_spec = pl.BlockSpec((tk, tm), lambda i, j, k: (k, i)) if ta else pl.BlockSpec((tm, tk), lambda i, j, k: (i, k))
    b_spec = pl.BlockSpec((tn, tk), lambda i, j, k: (j, k)) if tb else pl.BlockSpec((tk, tn), lambda i, j, k: (k, j))
    return pl.pallas_call(
        body, name=name, grid=(M // tm, N // tn, nk),
        in_specs=[a_spec, b_spec], out_specs=pl.BlockSpec((tm, tn), lambda i, j, k: (i, j)),
        out_shape=jax.ShapeDtypeStruct((M, N), out_dtype),
        scratch_shapes=[] if nk == 1 else [pltpu.VMEM((tm, tn), F32)],
        compiler_params=_cp("parallel", "parallel", "arbitrary"),
    )(a, b)


def matmul_res(a, b, res, gate, seq, *, tm=512, tn=1024, name):
    M, K = a.shape
    N = b.shape[1]
    tm, tn = _tile(min(M, seq), tm), _tile(N, tn)
    per_seq = seq // tm

    def body(a_ref, b_ref, r_ref, g_ref, y_ref, x_ref):
        y = _dot(a_ref[...].astype(BF16), b_ref[...].astype(BF16))
        y_ref[...] = y
        x_ref[...] = r_ref[...] + g_ref[...] * y

    out = jax.ShapeDtypeStruct((M, N), F32)
    return pl.pallas_call(
        body, name=name, grid=(M // tm, N // tn),
        in_specs=[pl.BlockSpec((tm, K), lambda i, j: (i, 0)), pl.BlockSpec((K, tn), lambda i, j: (0, j)),
                  pl.BlockSpec((tm, tn), lambda i, j: (i, j)), pl.BlockSpec((None, 1, tn), lambda i, j: (lax.div(i, jnp.int32(per_seq)), 0, j))],
        out_specs=[pl.BlockSpec((tm, tn), lambda i, j: (i, j))] * 2,
        out_shape=[out, out], compiler_params=_cp("parallel", "parallel"),
    )(a, b, res, gate)


def rms_fwd(x3, blk, W, g, sc=None, sh=None, *, tm=512, name):
    Bl, S, _ = x3.shape
    tm = min(tm, S)
    mod = sc is not None

    def body(x_ref, g_ref, *rest):
        o_ref = rest[-1]
        x = x_ref[...]
        r = lax.rsqrt(jnp.mean(x * x, axis=-1, keepdims=True) + EPS)
        y = x * r * g_ref[...]
        if mod:
            y = y * (1.0 + rest[0][...]) + rest[1][...]
        o_ref[...] = y.astype(BF16)

    vec = pl.BlockSpec((None, 1, W), lambda b, s: (b, 0, 0))
    return pl.pallas_call(
        body, name=name, grid=(Bl, S // tm),
        in_specs=[pl.BlockSpec((None, tm, W), lambda b, s: (b, s, blk)), pl.BlockSpec((1, W), lambda b, s: (0, 0))] + ([vec, vec] if mod else []),
        out_specs=pl.BlockSpec((None, tm, W), lambda b, s: (b, s, 0)),
        out_shape=jax.ShapeDtypeStruct((Bl, S, W), BF16),
        compiler_params=_cp("parallel", "parallel"),
    )(x3, g, *([sc, sh] if mod else []))


def rms_bwd(x3, blk, W, dy3, g, sc=None, dres3=None, *, tm=256, name):
    Bl, S, _ = x3.shape
    tm = min(tm, S)
    mod = sc is not None
    res = dres3 is not None

    def body(*refs):
        x_ref, dy_ref, g_ref = refs[:3]
        k = 3
        sc_ref = dr_ref = None
        if mod:
            sc_ref = refs[k]
            k += 1
        if res:
            dr_ref = refs[k]
            k += 1
        dx_ref, dg_ref = refs[k], refs[k + 1]
        b, s = pl.program_id(0), pl.program_id(1)
        x = x_ref[...]
        dy = dy_ref[...].astype(F32)
        g = g_ref[...]
        r = lax.rsqrt(jnp.mean(x * x, axis=-1, keepdims=True) + EPS)
        n = x * r
        if mod:
            dsc_ref, dsh_ref = refs[k + 2], refs[k + 3]
            one_sc = 1.0 + sc_ref[...]

            @pl.when(s == 0)
            def _():
                dsc_ref[...] = jnp.zeros_like(dsc_ref)
                dsh_ref[...] = jnp.zeros_like(dsh_ref)

            dsh_ref[...] += jnp.sum(dy, axis=0, keepdims=True)
            dsc_ref[...] += jnp.sum(dy * n * g, axis=0, keepdims=True)
            dyn = dy * one_sc
        else:
            dyn = dy

        @pl.when((b == 0) & (s == 0))
        def _():
            dg_ref[...] = jnp.zeros_like(dg_ref)

        dg_ref[...] += jnp.sum(dyn * n, axis=0, keepdims=True)
        dn = dyn * g
        dx = r * (dn - n * jnp.mean(dn * n, axis=-1, keepdims=True))
        if res:
            dx = dx + dr_ref[...]
        dx_ref[...] = dx

    blkspec = pl.BlockSpec((None, tm, W), lambda b, s: (b, s, 0))
    vec = pl.BlockSpec((None, 1, W), lambda b, s: (b, 0, 0))
    row = pl.BlockSpec((1, W), lambda b, s: (0, 0))
    in_specs = [pl.BlockSpec((None, tm, W), lambda b, s: (b, s, blk)), blkspec, row] + ([vec] if mod else []) + ([blkspec] if res else [])
    out_specs = [blkspec, row] + ([vec, vec] if mod else [])
    out_shape = [jax.ShapeDtypeStruct((Bl, S, W), F32), jax.ShapeDtypeStruct((1, W), F32)]
    if mod:
        out_shape += [jax.ShapeDtypeStruct((Bl, 1, W), F32)] * 2
    args = [x3, dy3, g] + ([sc] if mod else []) + ([dres3] if res else [])
    return pl.pallas_call(
        body, name=name, grid=(Bl, S // tm), in_specs=in_specs, out_specs=out_specs, out_shape=out_shape,
        compiler_params=_cp("arbitrary", "arbitrary"),
    )(*args)


def pair_rms_fwd(x3, blk0, npairs, g2, *, tm=1024, name):
    Bl, S, _ = x3.shape
    tm = min(tm, S)

    def body(x_ref, g_ref, o_ref):
        lo, hi = _lane_masks()
        x = x_ref[...]
        xx = x * x
        s0 = jnp.sum(jnp.where(lo, xx, 0.0), axis=-1, keepdims=True)
        s1 = jnp.sum(jnp.where(hi, xx, 0.0), axis=-1, keepdims=True)
        r = jnp.where(lo, lax.rsqrt(s0 / HEAD + EPS), lax.rsqrt(s1 / HEAD + EPS))
        o_ref[...] = (x * r * g_ref[...]).astype(BF16)

    return pl.pallas_call(
        body, name=name, grid=(Bl, S // tm, npairs),
        in_specs=[pl.BlockSpec((None, tm, LANES), lambda b, s, p: (b, s, blk0 + p)), pl.BlockSpec((1, LANES), lambda b, s, p: (0, 0))],
        out_specs=pl.BlockSpec((None, tm, LANES), lambda b, s, p: (b, s, p)),
        out_shape=jax.ShapeDtypeStruct((Bl, S, LANES * npairs), BF16),
        compiler_params=_cp("parallel", "parallel", "parallel"),
    )(x3, g2)


def pair_rms_bwd(x3, blk0, npairs, dy3, g2, *, tm=1024, name):
    Bl, S, _ = x3.shape
    tm = min(tm, S)

    def body(x_ref, dy_ref, g_ref, dx_ref, dg_ref):
        lo, hi = _lane_masks()
        first = (pl.program_id(0) == 0) & (pl.program_id(1) == 0) & (pl.program_id(2) == 0)
        x = x_ref[...]
        dy = dy_ref[...]
        xx = x * x
        s0 = jnp.sum(jnp.where(lo, xx, 0.0), axis=-1, keepdims=True)
        s1 = jnp.sum(jnp.where(hi, xx, 0.0), axis=-1, keepdims=True)
        r = jnp.where(lo, lax.rsqrt(s0 / HEAD + EPS), lax.rsqrt(s1 / HEAD + EPS))
        n = x * r

        @pl.when(first)
        def _():
            dg_ref[...] = jnp.zeros_like(dg_ref)

        part = jnp.sum(dy * n, axis=0, keepdims=True)
        dg_ref[...] += part + pltpu.roll(part, HEAD, 1)
        dn = dy * g_ref[...]
        t = dn * n
        m0 = jnp.sum(jnp.where(lo, t, 0.0), axis=-1, keepdims=True)
        m1 = jnp.sum(jnp.where(hi, t, 0.0), axis=-1, keepdims=True)
        dx_ref[...] = r * (dn - n * (jnp.where(lo, m0, m1) / HEAD))

    return pl.pallas_call(
        body, name=name, grid=(Bl, S // tm, npairs),
        in_specs=[pl.BlockSpec((None, tm, LANES), lambda b, s, p: (b, s, blk0 + p)), pl.BlockSpec((None, tm, LANES), lambda b, s, p: (b, s, p)),
                  pl.BlockSpec((1, LANES), lambda b, s, p: (0, 0))],
        out_specs=[pl.BlockSpec((None, tm, LANES), lambda b, s, p: (b, s, p)), pl.BlockSpec((1, LANES), lambda b, s, p: (0, 0))],
        out_shape=[jax.ShapeDtypeStruct((Bl, S, LANES * npairs), F32), jax.ShapeDtypeStruct((1, LANES), F32)],
        compiler_params=_cp("arbitrary", "arbitrary", "arbitrary"),
    )(x3, dy3, g2)


def _rot(y, cos_t, sin_a, sin_b):
    return y * cos_t + pltpu.roll(y, LANES - 16, 1) * sin_a + pltpu.roll(y, 16, 1) * sin_b


def _rot_t(d, cos_t, sin_a, sin_b):
    return d * cos_t + pltpu.roll(d * sin_a, 16, 1) + pltpu.roll(d * sin_b, LANES - 16, 1)


def rope_norm_fwd(x3, nheads, g, tabs, slab=None, *, tm=1024, name):
    Bl, S, _ = x3.shape
    tm = min(tm, S)
    has_slab = slab is not None

    def body(*refs):
        x_ref, g_ref, c_ref, sa_ref, sb_ref = refs[:5]
        o_ref = refs[-1]
        x = x_ref[...]
        if has_slab:
            x = x + refs[5][...]
        r = lax.rsqrt(jnp.sum(x * x, axis=-1, keepdims=True) / MLA_QK + EPS)
        o_ref[...] = _rot(x * r * g_ref[...], c_ref[...], sa_ref[...], sb_ref[...]).astype(BF16)

    head = pl.BlockSpec((None, tm, LANES), lambda b, s, h: (b, s, h))
    tab = pl.BlockSpec((None, tm, LANES), lambda b, s, h: (b, s, 0))
    in_specs = [head, pl.BlockSpec((1, LANES), lambda b, s, h: (0, 0)), tab, tab, tab]
    args = [x3, g, *tabs]
    if has_slab:
        sblk = slab[1]
        in_specs.append(pl.BlockSpec((None, tm, LANES), lambda b, s, h: (b, s, sblk)))
        args.append(slab[0])
    return pl.pallas_call(
        body, name=name, grid=(Bl, S // tm, nheads), in_specs=in_specs, out_specs=head,
        out_shape=jax.ShapeDtypeStruct((Bl, S, LANES * nheads), BF16),
        compiler_params=_cp("parallel", "parallel", "parallel"),
    )(*args)


def rope_norm_bwd(x3, nheads, dy3, g, tabs, slab=None, *, tm=1024, name):
    Bl, S, _ = x3.shape
    tm = min(tm, S)
    has_slab = slab is not None

    def body(*refs):
        x_ref, dy_ref, g_ref, c_ref, sa_ref, sb_ref = refs[:6]
        k = 7 if has_slab else 6
        dx_ref, dg_ref = refs[k], refs[k + 1]
        h = pl.program_id(2)
        first = (pl.program_id(0) == 0) & (pl.program_id(1) == 0) & (h == 0)
        x = x_ref[...]
        if has_slab:
            x = x + refs[6][...]
        g = g_ref[...]
        r = lax.rsqrt(jnp.sum(x * x, axis=-1, keepdims=True) / MLA_QK + EPS)
        n = x * r
        d = _rot_t(dy_ref[...], c_ref[...], sa_ref[...], sb_ref[...])

        @pl.when(first)
        def _():
            dg_ref[...] = jnp.zeros_like(dg_ref)

        dg_ref[...] += jnp.sum(d * n, axis=0, keepdims=True)
        dn = d * g
        dx = r * (dn - n * (jnp.sum(dn * n, axis=-1, keepdims=True) / MLA_QK))
        dx_ref[...] = dx
        if has_slab:
            ds_ref = refs[k + 2]

            @pl.when(h == 0)
            def _():
                ds_ref[...] = dx

            @pl.when(h > 0)
            def _():
                ds_ref[...] += dx

    head = pl.BlockSpec((None, tm, LANES), lambda b, s, h: (b, s, h))
    tab = pl.BlockSpec((None, tm, LANES), lambda b, s, h: (b, s, 0))
    row = pl.BlockSpec((1, LANES), lambda b, s, h: (0, 0))
    in_specs = [head, head, row, tab, tab, tab]
    args = [x3, dy3, g, *tabs]
    out_specs = [head, row]
    out_shape = [jax.ShapeDtypeStruct((Bl, S, LANES * nheads), F32), jax.ShapeDtypeStruct((1, LANES), F32)]
    if has_slab:
        sblk = slab[1]
        in_specs.append(pl.BlockSpec((None, tm, LANES), lambda b, s, h: (b, s, sblk)))
        args.append(slab[0])
        out_specs.append(tab)
        out_shape.append(jax.ShapeDtypeStruct((Bl, S, LANES), F32))
    return pl.pallas_call(
        body, name=name, grid=(Bl, S // tm, nheads), in_specs=in_specs, out_specs=out_specs, out_shape=out_shape,
        compiler_params=_cp("arbitrary", "arbitrary", "arbitrary"),
    )(*args)


def _sb_tile(z, strict, u, carry_r):
    sp = jnp.maximum(z, 0.0) + jnp.log(1.0 + jnp.exp(-jnp.abs(z)))
    keep = jnp.where(strict, -sp, 0.0)
    logw = (z - sp) + _split_dot(keep, u) + carry_r
    return jnp.where(strict, jnp.exp(logw), 0.0), keep, sp


SB_BLOCK = 256


def sb_attn_fwd(proj3, *, name):
    Bl, S, _ = proj3.shape
    tq = min(SB_BLOCK, S)
    scale = HEAD ** -0.5
    qb, kb0, vb0 = P_SBQ // LANES, P_SBK // LANES, P_SBV // LANES

    def body(q_ref, k_ref, v_ref, o_ref, rt_ref):
        i = pl.program_id(2)
        masks = _lane_masks()
        lane = lax.broadcasted_iota(jnp.int32, (1, LANES), 1)
        q = q_ref[...]
        qh = [jnp.where(m, q, 0.0).astype(BF16) for m in masks]
        rr = lax.broadcasted_iota(jnp.int32, (tq, tq), 0)
        cc = lax.broadcasted_iota(jnp.int32, (tq, tq), 1)
        u = (rr > cc).astype(BF16)

        rt_ref[...] = jnp.zeros_like(rt_ref)

        def step(t, carry):
            r0, r1, acc = carry
            j = i - t
            off = pl.multiple_of(j * tq, tq)
            kb = k_ref[pl.ds(off, tq), :].astype(BF16)
            vb = v_ref[pl.ds(off, tq), :]
            strict = (cc + j * tq) < (rr + i * tq)
            rt_ref[...] = jnp.where(lane == j, r0, jnp.where(lane == j + HEAD, r1, rt_ref[...]))
            rs = [r0, r1]
            for h in range(2):
                z = _dot_nt(qh[h], kb) * scale
                w, keep, _ = _sb_tile(z, strict, u, rs[h])
                acc = acc + _dot(w.astype(BF16), jnp.where(masks[h], vb, 0.0).astype(BF16))
                rs[h] = rs[h] + jnp.sum(keep, axis=1, keepdims=True)
            return rs[0], rs[1], acc

        zero = jnp.zeros((tq, 1), F32)
        _, _, acc = lax.fori_loop(0, i + 1, step, (zero, zero, jnp.zeros((tq, LANES), F32)))
        o_ref[...] = acc

    seq = lambda blk0: pl.BlockSpec((None, S, LANES), lambda b, p, i: (b, 0, blk0 + p))
    out = pl.BlockSpec((None, tq, LANES), lambda b, p, i: (b, i, p))
    shp = jax.ShapeDtypeStruct((Bl, S, 2 * LANES), F32)
    return pl.pallas_call(
        body, name=name, grid=(Bl, 2, S // tq),
        in_specs=[pl.BlockSpec((None, tq, LANES), lambda b, p, i: (b, i, qb + p)), seq(kb0), seq(vb0)],
        out_specs=[out, out], out_shape=[shp, shp],
        compiler_params=_cp("parallel", "parallel", "arbitrary"),
    )(proj3, proj3, proj3)


def sb_attn_bwd(proj3, rt3, do3, *, name):
    Bl, S, _ = proj3.shape
    tq = min(SB_BLOCK, S)
    scale = HEAD ** -0.5
    qb, kb0, vb0 = P_SBQ // LANES, P_SBK // LANES, P_SBV // LANES

    def body(q_ref, k_ref, v_ref, rt_ref, do_ref, dq_ref, dk_ref, dv_ref):
        i = pl.program_id(2)

        @pl.when(i == 0)
        def _():
            dk_ref[...] = jnp.zeros_like(dk_ref)
            dv_ref[...] = jnp.zeros_like(dv_ref)

        masks = _lane_masks()
        lane = lax.broadcasted_iota(jnp.int32, (1, LANES), 1)
        q = q_ref[...]
        qh = [jnp.where(m, q, 0.0).astype(BF16) for m in masks]
        do_b = do_ref[...].astype(BF16)
        doh = [jnp.where(m, do_b, jnp.zeros_like(do_b)) for m in masks]
        rt = rt_ref[...]
        rr = lax.broadcasted_iota(jnp.int32, (tq, tq), 0)
        cc = lax.broadcasted_iota(jnp.int32, (tq, tq), 1)
        u_suffix = (rr > cc).astype(BF16)
        u_prefix = (rr < cc).astype(BF16)

        def step(j, carry):
            p0, p1, dq = carry
            off = pl.multiple_of(j * tq, tq)
            kf = k_ref[pl.ds(off, tq), :]
            kb = kf.astype(BF16)
            vb = v_ref[pl.ds(off, tq), :]
            strict = (cc + j * tq) < (rr + i * tq)
            ps = [p0, p1]
            dk_acc = jnp.zeros((tq, LANES), F32)
            dv_acc = jnp.zeros((tq, LANES), F32)
            for h in range(2):
                r_j = jnp.sum(jnp.where(lane == j + h * HEAD, rt, 0.0), axis=1, keepdims=True)
                z = _dot_nt(qh[h], kb) * scale
                w, _, sp = _sb_tile(z, strict, u_suffix, r_j)
                vh = jnp.where(masks[h], vb, 0.0).astype(BF16)
                g = _dot_nt(doh[h], vh) * w
                pre = _split_dot(g, u_prefix) + ps[h]
                dz = jnp.where(strict, g * jnp.exp(-sp) - jnp.exp(z - sp) * pre, 0.0) * scale
                dzb = dz.astype(BF16)
                dq = dq + _dot(dzb, jnp.where(masks[h], kf, 0.0).astype(BF16))
                dk_acc = dk_acc + _dot_tn(dzb, qh[h])
                dv_acc = dv_acc + _dot_tn(w.astype(BF16), doh[h])
                ps[h] = ps[h] + jnp.sum(g, axis=1, keepdims=True)
            dk_ref[pl.ds(off, tq), :] += dk_acc
            dv_ref[pl.ds(off, tq), :] += dv_acc
            return ps[0], ps[1], dq

        zero = jnp.zeros((tq, 1), F32)
        out = lax.fori_loop(0, i + 1, step, (zero, zero, jnp.zeros((tq, LANES), F32)))
        dq_ref[...] = out[2]

    seq_in = lambda blk0: pl.BlockSpec((None, S, LANES), lambda b, p, i: (b, 0, blk0 + p))
    blk = pl.BlockSpec((None, tq, LANES), lambda b, p, i: (b, i, p))
    seq_out = pl.BlockSpec((None, S, LANES), lambda b, p, i: (b, 0, p))
    shp = jax.ShapeDtypeStruct((Bl, S, 2 * LANES), F32)
    return pl.pallas_call(
        body, name=name, grid=(Bl, 2, S // tq),
        in_specs=[pl.BlockSpec((None, tq, LANES), lambda b, p, i: (b, i, qb + p)), seq_in(kb0), seq_in(vb0), blk, blk],
        out_specs=[blk, seq_out, seq_out], out_shape=[shp, shp, shp],
        compiler_params=_cp("parallel", "parallel", "arbitrary"),
    )(proj3, proj3, proj3, rt3, do3)


def mla_attn_fwd(q3, k3, kv3, vblk0, *, tq=256, name):
    Bl, S, _ = q3.shape
    tq = min(tq, S)
    scale = MLA_QK ** -0.5

    def body(q_ref, k_ref, v_ref, o_ref, lse_ref):
        i = pl.program_id(2)
        masks = _lane_masks()
        rr = lax.broadcasted_iota(jnp.int32, (tq, tq), 0)
        cc = lax.broadcasted_iota(jnp.int32, (tq, tq), 1)
        qh = [q_ref[:, h * LANES:(h + 1) * LANES] for h in range(2)]

        def step(j, carry):
            m0, l0, m1, l1, acc = carry
            off = pl.multiple_of(j * tq, tq)
            vb = v_ref[pl.ds(off, tq), :]
            causal = (cc + j * tq) <= (rr + i * tq)
            ms, ls, alphas = [m0, m1], [l0, l1], []
            add = jnp.zeros((tq, LANES), F32)
            for h in range(2):
                kh = k_ref[pl.ds(off, tq), h * LANES:(h + 1) * LANES]
                s = jnp.where(causal, _dot_nt(qh[h], kh) * scale, NEG)
                m_new = jnp.maximum(ms[h], jnp.max(s, axis=1, keepdims=True))
                p = jnp.exp(s - m_new)
                alpha = jnp.exp(ms[h] - m_new)
                ls[h] = alpha * ls[h] + jnp.sum(p, axis=1, keepdims=True)
                ms[h] = m_new
                alphas.append(alpha)
                add = add + _dot(p.astype(BF16), jnp.where(masks[h], vb, 0.0).astype(BF16))
            acc = acc * jnp.where(masks[0], alphas[0], alphas[1]) + add
            return ms[0], ls[0], ms[1], ls[1], acc

        neg = jnp.full((tq, 1), NEG, F32)
        zero = jnp.zeros((tq, 1), F32)
        m0, l0, m1, l1, acc = lax.fori_loop(0, i + 1, step, (neg, zero, neg, zero, jnp.zeros((tq, LANES), F32)))
        o_ref[...] = acc / jnp.where(masks[0], l0, l1)
        lse_ref[...] = jnp.where(masks[0], m0 + jnp.log(l0), m1 + jnp.log(l1))

    out = pl.BlockSpec((None, tq, LANES), lambda b, p, i: (b, i, p))
    shp = jax.ShapeDtypeStruct((Bl, S, 3 * LANES), F32)
    return pl.pallas_call(
        body, name=name, grid=(Bl, 3, S // tq),
        in_specs=[pl.BlockSpec((None, tq, 2 * LANES), lambda b, p, i: (b, i, p)), pl.BlockSpec((None, S, 2 * LANES), lambda b, p, i: (b, 0, p)),
                  pl.BlockSpec((None, S, LANES), lambda b, p, i: (b, 0, vblk0 + p))],
        out_specs=[out, out], out_shape=[shp, shp],
        compiler_params=_cp("parallel", "parallel", "arbitrary"),
    )(q3, k3, kv3)


def mla_attn_bwd(q3, k3, kv3, vblk0, o3, lse3, do3, *, tq=256, name):
    Bl, S, _ = q3.shape
    tq = min(tq, S)
    nq = S // tq
    scale = MLA_QK ** -0.5

    def body(q_ref, k_ref, v_ref, o_ref, lse_ref, do_ref, dq_ref, dk_ref, dv_ref):
        j = pl.program_id(2)

        @pl.when(j == 0)
        def _():
            dq_ref[...] = jnp.zeros_like(dq_ref)

        masks = _lane_masks()
        rr = lax.broadcasted_iota(jnp.int32, (tq, tq), 0)
        cc = lax.broadcasted_iota(jnp.int32, (tq, tq), 1)
        vb = v_ref[...]
        vh = [jnp.where(m, vb, 0.0).astype(BF16) for m in masks]
        kh = [k_ref[:, h * LANES:(h + 1) * LANES] for h in range(2)]

        def step(t, carry):
            dk0, dk1, dv = carry
            i = j + t
            off = pl.multiple_of(i * tq, tq)
            causal = (cc + j * tq) <= (rr + i * tq)
            do_b = do_ref[pl.ds(off, tq), :].astype(BF16)
            prod = do_b.astype(F32) * o_ref[pl.ds(off, tq), :]
            lse = lse_ref[pl.ds(off, tq), :]
            dks = [dk0, dk1]
            for h in range(2):
                qh = q_ref[pl.ds(off, tq), h * LANES:(h + 1) * LANES]
                doh = jnp.where(masks[h], do_b, jnp.zeros_like(do_b))
                delta = jnp.sum(jnp.where(masks[h], prod, 0.0), axis=1, keepdims=True)
                lse_h = lse[:, h * HEAD:h * HEAD + 1]
                s = jnp.where(causal, _dot_nt(qh, kh[h]) * scale, NEG)
                p = jnp.exp(s - lse_h)
                ds = (p * (_dot_nt(doh, vh[h]) - delta) * scale).astype(BF16)
                dq_ref[pl.ds(off, tq), h * LANES:(h + 1) * LANES] += _dot(ds, kh[h])
                dks[h] = dks[h] + _dot_tn(ds, qh)
                dv = dv + _dot_tn(p.astype(BF16), doh)
            return dks[0], dks[1], dv

        zero = jnp.zeros((tq, LANES), F32)
        dk0, dk1, dv = lax.fori_loop(0, nq - j, step, (zero, zero, zero))
        dk_ref[:, 0:LANES] = dk0
        dk_ref[:, LANES:2 * LANES] = dk1
        dv_ref[...] = dv

    seq1 = pl.BlockSpec((None, S, LANES), lambda b, p, j: (b, 0, p))
    seq2 = pl.BlockSpec((None, S, 2 * LANES), lambda b, p, j: (b, 0, p))
    return pl.pallas_call(
        body, name=name, grid=(Bl, 3, nq),
        in_specs=[seq2, pl.BlockSpec((None, tq, 2 * LANES), lambda b, p, j: (b, j, p)),
                  pl.BlockSpec((None, tq, LANES), lambda b, p, j: (b, j, vblk0 + p)), seq1, seq1, seq1],
        out_specs=[seq2, pl.BlockSpec((None, tq, 2 * LANES), lambda b, p, j: (b, j, p)), pl.BlockSpec((None, tq, LANES), lambda b, p, j: (b, j, p))],
        out_shape=[jax.ShapeDtypeStruct((Bl, S, 6 * LANES), F32), jax.ShapeDtypeStruct((Bl, S, 6 * LANES), F32), jax.ShapeDtypeStruct((Bl, S, 3 * LANES), F32)],
        compiler_params=_cp("parallel", "parallel", "arbitrary"),
    )(q3, k3, kv3, o3, lse3, do3)


def _bucket_table():
    a = jnp.arange(WINDOW)[:, None]
    b = jnp.arange(2 * WINDOW)[None, :]
    dist = WINDOW + a - b
    max_exact = REL_BUCKETS // 2
    n = jnp.maximum(dist, 0)
    nf = jnp.maximum(n, 1).astype(F32)
    large = max_exact + (jnp.log(nf / max_exact) / math.log(REL_MAX_DIST / max_exact) * (REL_BUCKETS - max_exact)).astype(jnp.int32)
    large = jnp.minimum(large, REL_BUCKETS - 1)
    bucket = jnp.where(n < max_exact, n, large)
    return jnp.where((dist >= 0) & (dist < WINDOW), bucket, -1).astype(jnp.int32)


def swa_bias(rel_flat, bucket, *, name):
    def body(t_ref, b_ref, o_ref):
        bk = b_ref[...]
        for p in range(3):
            for hh in range(2):
                h = hh * 3 + p
                acc = jnp.full(bk.shape, NEG, F32)
                for b in range(REL_BUCKETS):
                    acc = jnp.where(bk == b, t_ref[b * 6 + h], acc)
                o_ref[p, hh] = acc

    return pl.pallas_call(
        body, name=name,
        in_specs=[pl.BlockSpec(memory_space=pltpu.SMEM), pl.BlockSpec(memory_space=pltpu.VMEM)],
        out_specs=pl.BlockSpec(memory_space=pltpu.VMEM),
        out_shape=jax.ShapeDtypeStruct((3, 2, WINDOW, 2 * WINDOW), F32),
    )(rel_flat, bucket)


def swa_bias_bwd(dbias, bucket, *, name):
    Bl = dbias.shape[0]

    def body(d_ref, b_ref, o_ref):
        bk = b_ref[...]
        lane = lax.broadcasted_iota(jnp.int32, (1, LANES), 1)
        rows = []
        for h in range(6):
            hh, p = divmod(h, 3)
            d = d_ref[0, p, hh]
            for bl in range(1, Bl):
                d = d + d_ref[bl, p, hh]
            row = jnp.zeros((1, LANES), F32)
            for b in range(REL_BUCKETS):
                s = jnp.sum(jnp.sum(jnp.where(bk == b, d, 0.0), axis=1, keepdims=True), axis=0, keepdims=True)
                row = row + jnp.where(lane == b, s, 0.0)
            rows.append(row)
        rows += [jnp.zeros((1, LANES), F32)] * 2
        o_ref[...] = jnp.concatenate(rows, axis=0)

    return pl.pallas_call(
        body, name=name,
        in_specs=[pl.BlockSpec(memory_space=pltpu.VMEM)] * 2, out_specs=pl.BlockSpec(memory_space=pltpu.VMEM),
        out_shape=jax.ShapeDtypeStruct((8, LANES), F32),
    )(dbias, bucket)


def _swa_specs(vblk):
    cur = lambda blk: pl.BlockSpec((None, WINDOW, LANES), lambda b, p, n: (b, n, blk))
    prev = lambda blk: pl.BlockSpec((None, WINDOW, LANES), lambda b, p, n: (b, jnp.maximum(n - 1, 0), blk))
    return [pl.BlockSpec((None, WINDOW, LANES), lambda b, p, n: (b, n, p)), cur(0), prev(0), cur(vblk), prev(vblk),
            pl.BlockSpec((None, 2, WINDOW, 2 * WINDOW), lambda b, p, n: (p, 0, 0, 0)), pl.BlockSpec((None, 2, LANES), lambda b, p, n: (p, 0, 0))]


def _swa_logits(qh, kp, kc, bias_h, first, scale):
    sp = jnp.where(first, NEG, _dot_nt(qh, kp) * scale + bias_h[:, :WINDOW])
    sc = _dot_nt(qh, kc) * scale + bias_h[:, WINDOW:]
    return sp, sc


def swa_attn_fwd(qn3, kn3, proj3, bias, sinks, *, name):
    Bl, S, _ = qn3.shape
    scale = HEAD ** -0.5

    def body(q_ref, kc_ref, kp_ref, vc_ref, vp_ref, b_ref, s_ref, o_ref, lse_ref):
        first = pl.program_id(2) == 0
        masks = _lane_masks()
        q = q_ref[...]
        o = jnp.zeros((WINDOW, LANES), F32)
        lses = []
        for h in range(2):
            qh = jnp.where(masks[h], q, jnp.zeros_like(q))
            sp, sc = _swa_logits(qh, kp_ref[...], kc_ref[...], b_ref[h], first, scale)
            sink = s_ref[h:h + 1, 0:1]
            m = jnp.maximum(jnp.maximum(jnp.max(sp, axis=1, keepdims=True), jnp.max(sc, axis=1, keepdims=True)), sink)
            ep, ec = jnp.exp(sp - m), jnp.exp(sc - m)
            l = jnp.sum(ep, axis=1, keepdims=True) + jnp.sum(ec, axis=1, keepdims=True) + jnp.exp(sink - m)
            inv = 1.0 / l
            o = o + _dot((ep * inv).astype(BF16), jnp.where(masks[h], vp_ref[...], 0.0).astype(BF16))
            o = o + _dot((ec * inv).astype(BF16), jnp.where(masks[h], vc_ref[...], 0.0).astype(BF16))
            lses.append(m + jnp.log(l))
        o_ref[...] = o
        lse_ref[...] = jnp.where(masks[0], lses[0], lses[1])

    out = pl.BlockSpec((None, WINDOW, LANES), lambda b, p, n: (b, n, p))
    shp = jax.ShapeDtypeStruct((Bl, S, 3 * LANES), F32)
    return pl.pallas_call(
        body, name=name, grid=(Bl, 3, S // WINDOW), in_specs=_swa_specs(P_SWV // LANES),
        out_specs=[out, out], out_shape=[shp, shp], compiler_params=_cp("parallel", "parallel", "arbitrary"),
    )(qn3, kn3, kn3, proj3, proj3, bias, sinks)


def swa_attn_bwd(qn3, kn3, proj3, bias, sinks, o3, lse3, do3, *, name):
    Bl, S, _ = qn3.shape
    scale = HEAD ** -0.5

    def body(q_ref, kc_ref, kp_ref, vc_ref, vp_ref, b_ref, s_ref, o_ref, lse_ref, do_ref,
             dq_ref, dk_ref, dv_ref, db_ref, dsk_ref):
        p_id, n = pl.program_id(1), pl.program_id(2)
        first = n == 0

        @pl.when((p_id == 0) & first)
        def _():
            dk_ref[...] = jnp.zeros_like(dk_ref)
            dv_ref[...] = jnp.zeros_like(dv_ref)

        @pl.when(first)
        def _():
            db_ref[...] = jnp.zeros_like(db_ref)
            dsk_ref[...] = jnp.zeros_like(dsk_ref)

        masks = _lane_masks()
        q = q_ref[...]
        kc, kp = kc_ref[...], kp_ref[...]
        do_b = do_ref[...].astype(BF16)
        prod = do_b.astype(F32) * o_ref[...]
        lse = lse_ref[...]
        dq = jnp.zeros((WINDOW, LANES), F32)
        dkp = jnp.zeros((WINDOW, LANES), F32)
        dkc = jnp.zeros((WINDOW, LANES), F32)
        dvp = jnp.zeros((WINDOW, LANES), F32)
        dvc = jnp.zeros((WINDOW, LANES), F32)
        for h in range(2):
            qh = jnp.where(masks[h], q, jnp.zeros_like(q))
            doh = jnp.where(masks[h], do_b, jnp.zeros_like(do_b))
            sp, sc = _swa_logits(qh, kp, kc, b_ref[h], first, scale)
            lse_h = lse[:, h * HEAD:h * HEAD + 1]
            pp, pc = jnp.exp(sp - lse_h), jnp.exp(sc - lse_h)
            delta = jnp.sum(jnp.where(masks[h], prod, 0.0), axis=1, keepdims=True)
            dsp = pp * (_dot_nt(doh, jnp.where(masks[h], vp_ref[...], 0.0).astype(BF16)) - delta)
            dsc = pc * (_dot_nt(doh, jnp.where(masks[h], vc_ref[...], 0.0).astype(BF16)) - delta)
            db_ref[h, :, 0:WINDOW] += dsp
            db_ref[h, :, WINDOW:2 * WINDOW] += dsc
            psink = jnp.exp(s_ref[h:h + 1, 0:1] - lse_h)
            dsk_ref[h:h + 1, :] += jnp.broadcast_to(-jnp.sum(psink * delta, axis=0, keepdims=True), (1, LANES))
            dspb, dscb = (dsp * scale).astype(BF16), (dsc * scale).astype(BF16)
            dq = dq + _dot(dspb, jnp.where(masks[h], kp, jnp.zeros_like(kp))) + _dot(dscb, jnp.where(masks[h], kc, jnp.zeros_like(kc)))
            dkp = dkp + _dot_tn(dspb, qh)
            dkc = dkc + _dot_tn(dscb, qh)
            dvp = dvp + _dot_tn(pp.astype(BF16), doh)
            dvc = dvc + _dot_tn(pc.astype(BF16), doh)
        dq_ref[...] = dq
        offp = pl.multiple_of(jnp.maximum(n - 1, 0) * WINDOW, WINDOW)
        offc = pl.multiple_of(n * WINDOW, WINDOW)
        dk_ref[pl.ds(offp, WINDOW), :] += dkp
        dv_ref[pl.ds(offp, WINDOW), :] += dvp
        dk_ref[pl.ds(offc, WINDOW), :] += dkc
        dv_ref[pl.ds(offc, WINDOW), :] += dvc

    blk = pl.BlockSpec((None, WINDOW, LANES), lambda b, p, n: (b, n, p))
    seq = pl.BlockSpec((None, S, LANES), lambda b, p, n: (b, 0, 0))
    return pl.pallas_call(
        body, name=name, grid=(Bl, 3, S // WINDOW), in_specs=_swa_specs(P_SWV // LANES) + [blk, blk, blk],
        out_specs=[blk, seq, seq, pl.BlockSpec((None, None, 2, WINDOW, 2 * WINDOW), lambda b, p, n: (b, p, 0, 0, 0)),
                   pl.BlockSpec((None, None, 2, LANES), lambda b, p, n: (b, p, 0, 0))],
        out_shape=[jax.ShapeDtypeStruct((Bl, S, 3 * LANES), F32), jax.ShapeDtypeStruct((Bl, S, LANES), F32), jax.ShapeDtypeStruct((Bl, S, LANES), F32),
                   jax.ShapeDtypeStruct((Bl, 3, 2, WINDOW, 2 * WINDOW), F32), jax.ShapeDtypeStruct((Bl, 3, 2, LANES), F32)],
        compiler_params=_cp("arbitrary", "arbitrary", "arbitrary"),
    )(qn3, kn3, kn3, proj3, proj3, bias, sinks, o3, lse3, do3)


def _conv_rows(x, halo, w_ref, b_ref, first_blk):
    rows = lax.broadcasted_iota(jnp.int32, x.shape, 0)
    h6 = jnp.where(first_blk, 0.0, halo[6:7, :])
    h7 = jnp.where(first_blk, 0.0, halo[7:8, :])
    x1 = jnp.where(rows == 0, h7, pltpu.roll(x, 1, 0))
    x2 = jnp.where(rows == 0, h6, jnp.where(rows == 1, h7, pltpu.roll(x, 2, 0)))
    return w_ref[0:1, :] * x2 + w_ref[1:2, :] * x1 + w_ref[2:3, :] * x + b_ref[...], x1, x2


FF_BLK = D_FF // 2


def _up_perm(a):
    q = FF_BLK
    return _cat([a[..., 0:q], a[..., 2 * q:3 * q], a[..., q:2 * q], a[..., 3 * q:4 * q]])


def conv_gate_fwd(up3, cw, cb, *, tm=256, name):
    Bl, S, _ = up3.shape
    tm = min(tm, S)
    W = 2 * FF_BLK

    def body(x_ref, h_ref, w_ref, b_ref, o_ref):
        u, _, _ = _conv_rows(x_ref[...], h_ref[...], w_ref, b_ref, pl.program_id(1) == 0)
        ug, uv = u[:, :FF_BLK], u[:, FF_BLK:]
        o_ref[...] = (ug * jax.nn.sigmoid(ug) * uv).astype(BF16)

    hb = tm // 8
    return pl.pallas_call(
        body, name=name, grid=(Bl, S // tm, 2),
        in_specs=[pl.BlockSpec((None, tm, W), lambda b, s, c: (b, s, c)),
                  pl.BlockSpec((None, 8, W), lambda b, s, c: (b, jnp.maximum(s * hb - 1, 0), c)),
                  pl.BlockSpec((3, W), lambda b, s, c: (0, c)), pl.BlockSpec((1, W), lambda b, s, c: (0, c))],
        out_specs=pl.BlockSpec((None, tm, FF_BLK), lambda b, s, c: (b, s, c)),
        out_shape=jax.ShapeDtypeStruct((Bl, S, D_FF), BF16),
        compiler_params=_cp("parallel", "parallel", "parallel"),
    )(up3, up3, cw, cb)


def conv_gate_bwd(up3, cw, cb, da3, *, tm=256, name):
    Bl, S, _ = up3.shape
    tm = min(tm, S)
    ns = S // tm
    W = 2 * FF_BLK

    def body(x_ref, h_ref, w_ref, b_ref, da_ref, dup_ref, dw_ref, nxt_ref):
        b, s = pl.program_id(1), pl.program_id(2)
        seq_end = s == 0

        @pl.when((b == 0) & seq_end)
        def _():
            dw_ref[...] = jnp.zeros_like(dw_ref)

        x = x_ref[...]
        u, x1, x2 = _conv_rows(x, h_ref[...], w_ref, b_ref, s == ns - 1)
        ug, uv = u[:, :FF_BLK], u[:, FF_BLK:]
        da = da_ref[...].astype(F32)
        sg = jax.nn.sigmoid(ug)
        du = _cat([da * uv * sg * (1.0 + ug * (1.0 - sg)), da * ug * sg])
        dw_ref[0:1, :] += jnp.sum(du * x2, axis=0, keepdims=True)
        dw_ref[1:2, :] += jnp.sum(du * x1, axis=0, keepdims=True)
        dw_ref[2:3, :] += jnp.sum(du * x, axis=0, keepdims=True)
        dw_ref[3:4, :] += jnp.sum(du, axis=0, keepdims=True)
        rows = lax.broadcasted_iota(jnp.int32, du.shape, 0)
        n0 = jnp.where(seq_end, 0.0, nxt_ref[0:1, :])
        n1 = jnp.where(seq_end, 0.0, nxt_ref[1:2, :])
        d1 = jnp.where(rows == tm - 1, n0, pltpu.roll(du, tm - 1, 0))
        d2 = jnp.where(rows == tm - 1, n1, jnp.where(rows == tm - 2, n0, pltpu.roll(du, tm - 2, 0)))
        dup_ref[...] = (w_ref[2:3, :] * du + w_ref[1:2, :] * d1 + w_ref[0:1, :] * d2).astype(BF16)
        nxt_ref[...] = du[0:8, :]

    hb = tm // 8
    rb = lambda s: ns - 1 - s
    return pl.pallas_call(
        body, name=name, grid=(2, Bl, ns),
        in_specs=[pl.BlockSpec((None, tm, W), lambda c, b, s: (b, rb(s), c)),
                  pl.BlockSpec((None, 8, W), lambda c, b, s: (b, jnp.maximum(rb(s) * hb - 1, 0), c)),
                  pl.BlockSpec((3, W), lambda c, b, s: (0, c)), pl.BlockSpec((1, W), lambda c, b, s: (0, c)),
                  pl.BlockSpec((None, tm, FF_BLK), lambda c, b, s: (b, rb(s), c))],
        out_specs=[pl.BlockSpec((None, tm, W), lambda c, b, s: (b, rb(s), c)), pl.BlockSpec((8, W), lambda c, b, s: (0, c))],
        out_shape=[jax.ShapeDtypeStruct((Bl, S, 2 * D_FF), BF16), jax.ShapeDtypeStruct((8, 2 * D_FF), F32)],
        scratch_shapes=[pltpu.VMEM((8, W), F32)],
        compiler_params=_cp("arbitrary", "arbitrary", "arbitrary"),
    )(up3, up3, cw, cb, da3)


def gate_bwd(dx3, y3, gate, *, tm=512, name):
    Bl, S, D = dx3.shape
    tm = min(tm, S)

    def body(dx_ref, y_ref, g_ref, o_ref, dg_ref):
        @pl.when(pl.program_id(1) == 0)
        def _():
            dg_ref[...] = jnp.zeros_like(dg_ref)

        dx = dx_ref[...]
        dg_ref[...] += jnp.sum(dx * y_ref[...], axis=0, keepdims=True)
        o_ref[...] = (dx * g_ref[...]).astype(BF16)

    blk = pl.BlockSpec((None, tm, D), lambda b, s: (b, s, 0))
    vec = pl.BlockSpec((None, 1, D), lambda b, s: (b, 0, 0))
    return pl.pallas_call(
        body, name=name, grid=(Bl, S // tm), in_specs=[blk, blk, vec], out_specs=[blk, vec],
        out_shape=[jax.ShapeDtypeStruct((Bl, S, D), BF16), jax.ShapeDtypeStruct((Bl, 1, D), F32)],
        compiler_params=_cp("parallel", "arbitrary"),
    )(dx3, y3, gate)


def loss_grad(y3, t3, *, tm=512, name):
    Bl, S, D = y3.shape
    tm = min(tm, S)
    last = (Bl - 1, S // tm - 1)

    def body(y_ref, t_ref, dy_ref, l_ref, acc_ref):
        b, s = pl.program_id(0), pl.program_id(1)

        @pl.when((b == 0) & (s == 0))
        def _():
            acc_ref[...] = jnp.zeros_like(acc_ref)

        e = y_ref[...] - t_ref[...]
        dy_ref[...] = e * (1.0 / D)
        acc_ref[...] += jnp.sum(e * e, axis=0, keepdims=True)

        @pl.when((b == last[0]) & (s == last[1]))
        def _():
            l_ref[...] = jnp.broadcast_to(jnp.sum(acc_ref[...], axis=1, keepdims=True) * (0.5 / D), (1, LANES))

    blk = pl.BlockSpec((None, tm, D), lambda b, s: (b, s, 0))
    return pl.pallas_call(
        body, name=name, grid=(Bl, S // tm), in_specs=[blk, blk],
        out_specs=[blk, pl.BlockSpec((1, LANES), lambda b, s: (0, 0))],
        out_shape=[jax.ShapeDtypeStruct((Bl, S, D), F32), jax.ShapeDtypeStruct((1, LANES), F32)],
        scratch_shapes=[pltpu.VMEM((1, D), F32)], compiler_params=_cp("arbitrary", "arbitrary"),
    )(y3, t3)


def adamw(w, g, m, v, *, name):
    R, C = w.shape
    tr = R
    for cand in (512, 256, 128, 64, 32, 16, 8):
        if R > cand and R % cand == 0:
            tr = cand
            break
    c1 = 1.0 / (1.0 - ADAM_B1 ** ADAM_STEP)
    c2 = 1.0 / (1.0 - ADAM_B2 ** ADAM_STEP)

    def body(w_ref, g_ref, m_ref, v_ref, d_ref, m2_ref, v2_ref):
        gg = g_ref[...]
        m2 = ADAM_B1 * m_ref[...] + (1.0 - ADAM_B1) * gg
        v2 = ADAM_B2 * v_ref[...] + (1.0 - ADAM_B2) * (gg * gg)
        m2_ref[...] = m2
        v2_ref[...] = v2
        d_ref[...] = -ADAM_LR * ((m2 * c1) / (jnp.sqrt(v2 * c2) + ADAM_EPS) + ADAM_WD * w_ref[...])

    blk = pl.BlockSpec((tr, C), lambda i: (i, 0))
    shp = jax.ShapeDtypeStruct((R, C), F32)
    return pl.pallas_call(
        body, name=name, grid=(R // tr,), in_specs=[blk] * 4, out_specs=[blk] * 3, out_shape=[shp] * 3,
        compiler_params=_cp("parallel"),
    )(w, g, m, v)


def sum_leading(x, *, out_dtype=F32, tr=256, name):
    n, R, C = x.shape
    tr = _tile(R, tr, 16)

    def body(x_ref, o_ref):
        acc = x_ref[0].astype(F32)
        for k in range(1, n):
            acc = acc + x_ref[k].astype(F32)
        o_ref[...] = acc.astype(out_dtype)

    return pl.pallas_call(
        body, name=name, grid=(R // tr,), in_specs=[pl.BlockSpec((n, tr, C), lambda i: (0, i, 0))],
        out_specs=pl.BlockSpec((tr, C), lambda i: (i, 0)), out_shape=jax.ShapeDtypeStruct((R, C), out_dtype),
        compiler_params=_cp("parallel"),
    )(x)


def _adam_update(w, g, m, v):
    c1 = 1.0 / (1.0 - ADAM_B1 ** ADAM_STEP)
    c2 = 1.0 / (1.0 - ADAM_B2 ** ADAM_STEP)
    m2 = ADAM_B1 * m + (1.0 - ADAM_B1) * g
    v2 = ADAM_B2 * v + (1.0 - ADAM_B2) * (g * g)
    return -ADAM_LR * ((m2 * c1) / (jnp.sqrt(v2 * c2) + ADAM_EPS) + ADAM_WD * w), m2, v2


def adamw_small(ws, gs, ms, vs, *, name):
    na = len(ws)

    def body(*refs):
        w_r, g_r, m_r, v_r = (refs[i * na:(i + 1) * na] for i in range(4))
        d_r, m2_r, v2_r = (refs[(4 + i) * na:(5 + i) * na] for i in range(3))
        for a in range(na):
            d_r[a][...], m2_r[a][...], v2_r[a][...] = _adam_update(w_r[a][...], g_r[a][...], m_r[a][...], v_r[a][...])

    vm = pl.BlockSpec(memory_space=pltpu.VMEM)
    shp = [jax.ShapeDtypeStruct(w.shape, F32) for w in ws]
    out = pl.pallas_call(body, name=name, in_specs=[vm] * (4 * na), out_specs=[vm] * (3 * na), out_shape=shp * 3)(*ws, *gs, *ms, *vs)
    return out[:na], out[na:2 * na], out[2 * na:]


def sum_small(xs, *, name):
    na = len(xs)

    def body(*refs):
        for x_ref, o_ref in zip(refs[:na], refs[na:]):
            acc = x_ref[0]
            for k in range(1, x_ref.shape[0]):
                acc = acc + x_ref[k]
            o_ref[...] = acc

    vm = pl.BlockSpec(memory_space=pltpu.VMEM)
    return pl.pallas_call(body, name=name, in_specs=[vm] * na, out_specs=[vm] * na,
                          out_shape=[jax.ShapeDtypeStruct(x.shape[1:], x.dtype) for x in xs])(*xs)


def pair_add_half(g4, recv, c_arr, *, tr=128, name):
    _, R, C = g4.shape
    H = R // 2
    tr = _tile(H, tr, 16)
    nb = H // tr

    def body(c_ref, g_ref, r_ref, o_ref):
        o_ref[...] = (g_ref[...] + r_ref[...]).astype(BF16)

    grid_spec = pltpu.PrefetchScalarGridSpec(
        num_scalar_prefetch=1, grid=(4, nb),
        in_specs=[pl.BlockSpec((None, tr, C), lambda k, i, c_ref: (k, c_ref[0] * nb + i, 0)),
                  pl.BlockSpec((None, tr, C), lambda k, i, c_ref: (k, i, 0))],
        out_specs=pl.BlockSpec((None, tr, C), lambda k, i, c_ref: (k, i, 0)),
    )
    return pl.pallas_call(
        body, name=name, grid_spec=grid_spec, out_shape=jax.ShapeDtypeStruct((4, H, C), BF16),
        compiler_params=_cp("parallel", "parallel"),
    )(c_arr, g4, recv)


def mods_matmul(c_all, w_ada, b_ada_cols, *, tn=512, name):
    L, D, E = w_ada.shape
    nb = c_all.shape[0]
    tn = _tile(E, tn)

    def body(c_ref, w_ref, b_ref, o_ref):
        c = c_ref[...]
        a = c * jax.nn.sigmoid(c)
        o_ref[...] = jnp.dot(a, w_ref[...], preferred_element_type=F32, precision=lax.Precision.HIGHEST) + b_ref[...]

    return pl.pallas_call(
        body, name=name, grid=(L, E // tn),
        in_specs=[pl.BlockSpec((nb, D), lambda l, j: (0, 0)), pl.BlockSpec((None, D, tn), lambda l, j: (l, 0, j)),
                  pl.BlockSpec((None, 1, tn), lambda l, j: (l, 0, j))],
        out_specs=pl.BlockSpec((None, nb, tn), lambda l, j: (l, 0, j)),
        out_shape=jax.ShapeDtypeStruct((L, nb, E), F32), compiler_params=_cp("parallel", "parallel"),
    )(c_all, w_ada, b_ada_cols)


def ada_grad(c_all, dmods, *, tn=512, name):
    L, nb, E = dmods.shape
    D = c_all.shape[1]
    tn = _tile(E, tn)

    def body(c_ref, d_ref, o_ref):
        c = c_ref[...]
        a = c * jax.nn.sigmoid(c)
        o_ref[...] = lax.dot_general(a, d_ref[...], (((0,), (0,)), ((), ())), preferred_element_type=F32, precision=lax.Precision.HIGHEST)

    return pl.pallas_call(
        body, name=name, grid=(L, E // tn),
        in_specs=[pl.BlockSpec((nb, D), lambda l, j: (0, 0)), pl.BlockSpec((None, nb, tn), lambda l, j: (l, 0, j))],
        out_specs=pl.BlockSpec((None, D, tn), lambda l, j: (l, 0, j)),
        out_shape=jax.ShapeDtypeStruct((L, D, E), F32), compiler_params=_cp("parallel", "parallel"),
    )(c_all, dmods)


HBM = pl.BlockSpec(memory_space=pltpu.HBM)


def _me():
    return lax.axis_index("x"), lax.axis_index("y"), lax.axis_index("c")


def _flip(v, bit):
    return 1 - v if bit else v


def allgather8(xs, *, name):
    na = len(xs)

    def body(*refs):
        x_refs, out_refs = refs[:na], refs[na:2 * na]
        send_sems, recv_sems = refs[2 * na], refs[2 * na + 1]
        x, y, c = _me()
        me = 4 * x + 2 * y + c
        for x_ref, out_ref in zip(x_refs, out_refs):
            out_ref[me] = x_ref[...]
        sends = []
        for a, (x_ref, out_ref) in enumerate(zip(x_refs, out_refs)):
            for k in range(1, 8):
                peer = (_flip(x, k & 4), _flip(y, k & 2), _flip(c, k & 1))
                cp = pltpu.make_async_remote_copy(src_ref=x_ref, dst_ref=out_ref.at[me], send_sem=send_sems.at[a, k - 1],
                                                  recv_sem=recv_sems.at[a, k - 1], device_id=peer, device_id_type=MESH)
                cp.start()
                sends.append(cp)
        for a, (x_ref, out_ref) in enumerate(zip(x_refs, out_refs)):
            for k in range(1, 8):
                peer = (_flip(x, k & 4), _flip(y, k & 2), _flip(c, k & 1))
                src = 4 * peer[0] + 2 * peer[1] + peer[2]
                pltpu.make_async_remote_copy(src_ref=x_ref, dst_ref=out_ref.at[src], send_sem=send_sems.at[a, k - 1],
                                             recv_sem=recv_sems.at[a, k - 1], device_id=peer, device_id_type=MESH).wait_recv()
        for cp in sends:
            cp.wait_send()

    vm = pl.BlockSpec(memory_space=pltpu.VMEM)
    return pl.pallas_call(
        body, name=name, in_specs=[vm] * na, out_specs=[vm] * na,
        out_shape=[jax.ShapeDtypeStruct((8,) + a.shape, a.dtype) for a in xs],
        scratch_shapes=[pltpu.SemaphoreType.DMA((na, 7)), pltpu.SemaphoreType.DMA((na, 7))],
    )(*xs)


LOCAL_CHUNKS = 8


def _copy_via_vmem(src, dst_at, rows, buf, sem):
    ch = buf.shape[0]
    for i in range(rows // ch):
        load = pltpu.make_async_copy(src.at[pl.ds(i * ch, ch)], buf, sem)
        load.start()
        load.wait()
        store = pltpu.make_async_copy(buf, dst_at(i * ch, ch), sem)
        store.start()
        store.wait()


def _chunk_buf(rows, cols, dtype):
    align = 16 if dtype == BF16 else 8
    for n in range(LOCAL_CHUNKS, 0, -1):
        if rows % n == 0 and (rows // n) % align == 0:
            return pltpu.VMEM((rows // n, cols), dtype)
    return pltpu.VMEM((rows, cols), dtype)


def gather_weights(ws, *, name):
    na = len(ws)

    def body(*refs):
        x_refs, out_refs = refs[:na], refs[na:2 * na]
        send_sems, recv_sems, local_sem = refs[2 * na:2 * na + 3]
        bufs = refs[2 * na + 3:]
        x, y, c = _me()
        j = 2 * x + y
        chips = [(_flip(x, k & 2), _flip(y, k & 1)) for k in range(1, 4)]
        sends = []
        for a, (x_ref, out_ref) in enumerate(zip(x_refs, out_refs)):
            H = x_ref.shape[0] // 2
            for k, (px, py) in enumerate(chips):
                cp = pltpu.make_async_remote_copy(src_ref=x_ref.at[pl.ds(c * H, H)], dst_ref=out_ref.at[j, pl.ds(c * H, H)],
                                                  send_sem=send_sems.at[a, k], recv_sem=recv_sems.at[a, k],
                                                  device_id=(px, py, c), device_id_type=MESH)
                cp.start()
                sends.append(cp)
        for x_ref, out_ref, buf in zip(x_refs, out_refs, bufs):
            _copy_via_vmem(x_ref, lambda o, n, out_ref=out_ref: out_ref.at[j, pl.ds(o, n)], x_ref.shape[0], buf, local_sem)
        for a, out_ref in enumerate(out_refs):
            H = out_ref.shape[1] // 2
            for k, (px, py) in enumerate(chips):
                slot = out_ref.at[2 * px + py, pl.ds(c * H, H)]
                pltpu.make_async_remote_copy(src_ref=slot, dst_ref=slot, send_sem=send_sems.at[a, k], recv_sem=recv_sems.at[a, k],
                                             device_id=(px, py, c), device_id_type=MESH).wait_recv()
                cp = pltpu.make_async_remote_copy(src_ref=slot, dst_ref=slot, send_sem=send_sems.at[a, 3 + k],
                                                  recv_sem=recv_sems.at[a, 3 + k], device_id=(x, y, 1 - c), device_id_type=MESH)
                cp.start()
                sends.append(cp)
        for a, out_ref in enumerate(out_refs):
            H = out_ref.shape[1] // 2
            for k, (px, py) in enumerate(chips):
                slot = out_ref.at[2 * px + py, pl.ds((1 - c) * H, H)]
                pltpu.make_async_remote_copy(src_ref=slot, dst_ref=slot, send_sem=send_sems.at[a, 3 + k], recv_sem=recv_sems.at[a, 3 + k],
                                             device_id=(x, y, 1 - c), device_id_type=MESH).wait_recv()
        for cp in sends:
            cp.wait_send()

    return pl.pallas_call(
        body, name=name, in_specs=[HBM] * na, out_specs=[HBM] * na,
        out_shape=[jax.ShapeDtypeStruct((4,) + w.shape, w.dtype) for w in ws],
        scratch_shapes=[pltpu.SemaphoreType.DMA((na, 6)), pltpu.SemaphoreType.DMA((na, 6)), pltpu.SemaphoreType.DMA]
        + [_chunk_buf(w.shape[0], w.shape[1], w.dtype) for w in ws],
    )(*ws)


def swap_halves(gs, *, name):
    na = len(gs)

    def body(*refs):
        g_refs, out_refs = refs[:na], refs[na:2 * na]
        send_sems, recv_sems = refs[2 * na:]
        x, y, c = _me()
        sib = (x, y, 1 - c)
        sends = []
        for a, (g_ref, out_ref) in enumerate(zip(g_refs, out_refs)):
            H = g_ref.shape[1] // 2
            for k in range(4):
                cp = pltpu.make_async_remote_copy(src_ref=g_ref.at[k, pl.ds((1 - c) * H, H)], dst_ref=out_ref.at[k],
                                                  send_sem=send_sems.at[a, k], recv_sem=recv_sems.at[a, k], device_id=sib, device_id_type=MESH)
                cp.start()
                sends.append(cp)
        for a, (g_ref, out_ref) in enumerate(zip(g_refs, out_refs)):
            H = g_ref.shape[1] // 2
            for k in range(4):
                pltpu.make_async_remote_copy(src_ref=g_ref.at[k, pl.ds(c * H, H)], dst_ref=out_ref.at[k], send_sem=send_sems.at[a, k],
                                             recv_sem=recv_sems.at[a, k], device_id=sib, device_id_type=MESH).wait_recv()
        for cp in sends:
            cp.wait_send()

    return pl.pallas_call(
        body, name=name, in_specs=[HBM] * na, out_specs=[HBM] * na,
        out_shape=[jax.ShapeDtypeStruct((4, g.shape[1] // 2, g.shape[2]), g.dtype) for g in gs],
        scratch_shapes=[pltpu.SemaphoreType.DMA((na, 4)), pltpu.SemaphoreType.DMA((na, 4))],
    )(*gs)


def scatter_chips(ps, *, name):
    na = len(ps)

    def body(*refs):
        p_refs, out_refs = refs[:na], refs[na:2 * na]
        send_sems, recv_sems, local_sem = refs[2 * na:2 * na + 3]
        bufs = refs[2 * na + 3:]
        x, y, c = _me()
        j = 2 * x + y
        chips = [(_flip(x, k & 2), _flip(y, k & 1)) for k in range(1, 4)]
        sends = []
        for a, (p_ref, out_ref) in enumerate(zip(p_refs, out_refs)):
            for k, (px, py) in enumerate(chips):
                cp = pltpu.make_async_remote_copy(src_ref=p_ref.at[2 * px + py], dst_ref=out_ref.at[j], send_sem=send_sems.at[a, k],
                                                  recv_sem=recv_sems.at[a, k], device_id=(px, py, c), device_id_type=MESH)
                cp.start()
                sends.append(cp)
        for p_ref, out_ref, buf in zip(p_refs, out_refs, bufs):
            _copy_via_vmem(p_ref.at[j], lambda o, n, out_ref=out_ref: out_ref.at[j, pl.ds(o, n)], p_ref.shape[1], buf, local_sem)
        for a, out_ref in enumerate(out_refs):
            for k, (px, py) in enumerate(chips):
                slot = out_ref.at[2 * px + py]
                pltpu.make_async_remote_copy(src_ref=slot, dst_ref=slot, send_sem=send_sems.at[a, k], recv_sem=recv_sems.at[a, k],
                                             device_id=(px, py, c), device_id_type=MESH).wait_recv()
        for cp in sends:
            cp.wait_send()

    return pl.pallas_call(
        body, name=name, in_specs=[HBM] * na, out_specs=[HBM] * na, out_shape=[jax.ShapeDtypeStruct(p.shape, p.dtype) for p in ps],
        scratch_shapes=[pltpu.SemaphoreType.DMA((na, 3)), pltpu.SemaphoreType.DMA((na, 3)), pltpu.SemaphoreType.DMA]
        + [_chunk_buf(p.shape[1], p.shape[2], p.dtype) for p in ps],
    )(*ps)


def join_halves(halves, *, name):
    na = len(halves)

    def body(*refs):
        h_refs, out_refs = refs[:na], refs[na:2 * na]
        send_sems, recv_sems, local_sem = refs[2 * na:2 * na + 3]
        bufs = refs[2 * na + 3:]
        x, y, c = _me()
        sib = (x, y, 1 - c)
        sends = []
        for a, (h_ref, out_ref) in enumerate(zip(h_refs, out_refs)):
            H = h_ref.shape[0]
            cp = pltpu.make_async_remote_copy(src_ref=h_ref, dst_ref=out_ref.at[pl.ds(c * H, H)], send_sem=send_sems.at[a],
                                              recv_sem=recv_sems.at[a], device_id=sib, device_id_type=MESH)
            cp.start()
            sends.append(cp)
        for h_ref, out_ref, buf in zip(h_refs, out_refs, bufs):
            H = h_ref.shape[0]
            _copy_via_vmem(h_ref, lambda o, n, out_ref=out_ref, H=H: out_ref.at[pl.ds(c * H + o, n)], H, buf, local_sem)
        for a, (h_ref, out_ref) in enumerate(zip(h_refs, out_refs)):
            H = h_ref.shape[0]
            pltpu.make_async_remote_copy(src_ref=h_ref, dst_ref=out_ref.at[pl.ds((1 - c) * H, H)], send_sem=send_sems.at[a],
                                         recv_sem=recv_sems.at[a], device_id=sib, device_id_type=MESH).wait_recv()
        for cp in sends:
            cp.wait_send()

    return pl.pallas_call(
        body, name=name, in_specs=[HBM] * na, out_specs=[HBM] * na,
        out_shape=[jax.ShapeDtypeStruct((2 * h.shape[0], h.shape[1]), h.dtype) for h in halves],
        scratch_shapes=[pltpu.SemaphoreType.DMA((na,)), pltpu.SemaphoreType.DMA((na,)), pltpu.SemaphoreType.DMA]
        + [_chunk_buf(h.shape[0], h.shape[1], h.dtype) for h in halves],
    )(*halves)


def _cat(parts, axis=-1):
    return jnp.concatenate(parts, axis=axis)


def _prep_w_in(w):
    z = lambda n: jnp.zeros((w.shape[0], n), w.dtype)
    swq = w[:, 1184:1568]
    return _cat([w[:, 0:1152], z(64), w[:, 1152:1184], z(32)] + [swq[:, HEAD * h:HEAD * (h + 1)] for h in SW_PERM] + [w[:, 1568:1824]])


def _unprep_w_in(g):
    swq = g[:, P_SWQ:P_SWK]
    return _cat([g[:, 0:1152], g[:, 1216:1248]] + [swq[:, HEAD * SW_PERM.index(h):HEAD * (SW_PERM.index(h) + 1)] for h in range(6)] + [g[:, P_SWK:P_END]])


def _prep_w_uq(w):
    z = jnp.zeros((w.shape[0], 32), w.dtype)
    return _cat([p for h in range(6) for p in (w[:, MLA_QK * h:MLA_QK * (h + 1)], z)])


def _unprep_w_uq(g):
    return _cat([g[:, LANES * h:LANES * h + MLA_QK] for h in range(6)])


def _prep_w_ukv(w):
    z = jnp.zeros((w.shape[0], HEAD), w.dtype)
    return _cat([p for h in range(6) for p in (w[:, LANES * h:LANES * h + HEAD], z)] + [w[:, LANES * h + HEAD:LANES * (h + 1)] for h in range(6)])


def _unprep_w_ukv(g):
    return _cat([p for h in range(6) for p in (g[:, LANES * h:LANES * h + HEAD], g[:, 768 + HEAD * h:768 + HEAD * (h + 1)])])


def _prep_w_out(w):
    return _cat([w[0:640]] + [w[640 + HEAD * h:640 + HEAD * (h + 1)] for h in SW_PERM], axis=0)


def _unprep_w_out(g):
    return _cat([g[0:640]] + [g[640 + HEAD * SW_PERM.index(h):640 + HEAD * (SW_PERM.index(h) + 1)] for h in range(6)], axis=0)


def _rope_tables(positions):
    half = 16
    inv_freq = jnp.power(ROPE_THETA, -jnp.arange(half, dtype=F32) / half)
    ang = positions.astype(F32)[..., None] * inv_freq
    cos, sin = jnp.cos(ang), jnp.sin(ang)
    z = lambda n: jnp.zeros(ang.shape[:-1] + (n,), F32)
    return (_cat([jnp.ones(ang.shape[:-1] + (HEAD,), F32), cos, cos, z(32)]), _cat([z(HEAD), -sin, z(16), z(32)]), _cat([z(HEAD), z(16), sin, z(32)]))


def _small_params(p):
    pad96 = lambda g: _cat([g, jnp.zeros((32,), F32)]).reshape(1, LANES)
    two = lambda g: _cat([g, g]).reshape(1, LANES)
    sinks = jnp.broadcast_to(p["sw_sinks"].reshape(2, 3).T[:, :, None], (3, 2, LANES))
    return dict(n1=p["norm1_g"].reshape(1, -1), n2=p["norm2_g"].reshape(1, -1), cq_g=p["mla_cq_g"].reshape(1, -1),
                ckv_g=p["mla_ckv_g"].reshape(1, -1), qn_g=pad96(p["mla_qn_g"]), kn_g=pad96(p["mla_kn_g"]),
                swq_g=two(p["sw_qn_g"]), swk_g=two(p["sw_kn_g"]), sinks=sinks, conv_b=_up_perm(p["conv_b"]).reshape(1, -1))


def _layer_fwd(x3, md, W, tabs, bias, tag):
    Bl, S, D = x3.shape
    T = Bl * S
    n = lambda s: f"{s}_{tag}"
    two = lambda a: a.reshape(T, a.shape[-1])
    three = lambda a: a.reshape(Bl, S, a.shape[-1])
    h = rms_fwd(x3, 0, D, W["n1"], md["scale1"], md["shift1"], name=n("norm1"))
    proj = three(matmul(two(h), W["w_in"], tn=1920, name=n("in_proj")))
    o_a, rt_a = sb_attn_fwd(proj, name=n("sb_fwd"))
    cqn = rms_fwd(proj, P_CQ // 256, 256, W["cq_g"], name=n("cq_norm"))
    ckvn = rms_fwd(proj, P_CKV // LANES, LANES, W["ckv_g"], name=n("ckv_norm"))
    qb = three(matmul(two(cqn), W["w_uq"], tm=1024, tn=768, name=n("uq")))
    kvb = three(matmul(two(ckvn), W["w_ukv"], tm=1024, tn=1152, name=n("ukv")))
    q_m = rope_norm_fwd(qb, 6, W["qn_g"], tabs, name=n("q_rope"))
    k_m = rope_norm_fwd(kvb, 6, W["kn_g"], tabs, (proj, P_SLAB // LANES), name=n("k_rope"))
    o_b, lse_b = mla_attn_fwd(q_m, k_m, kvb, 6, name=n("mla_fwd"))
    q_c = pair_rms_fwd(proj, P_SWQ // LANES, 3, W["swq_g"], name=n("swq_norm"))
    k_c = pair_rms_fwd(proj, P_SWK // LANES, 1, W["swk_g"], name=n("swk_norm"))
    o_c, lse_c = swa_attn_fwd(q_c, k_c, proj, bias, W["sinks"], name=n("swa_fwd"))
    mix = _cat([o_a, o_b, o_c]).astype(BF16)
    att, x1 = matmul_res(two(mix), W["w_out"], two(x3), md["gate1"], S, name=n("out_proj"))
    x1 = three(x1)
    h2 = rms_fwd(x1, 0, D, W["n2"], md["scale2"], md["shift2"], name=n("norm2"))
    up = three(matmul(two(h2), W["w_up"], tn=1408, name=n("up_proj")))
    a = conv_gate_fwd(up, W["conv_w"], W["conv_b"], name=n("conv_gate"))
    yd, x2 = matmul_res(two(a), W["w_down"], two(x1), md["gate2"], S, name=n("down_proj"))
    saved = dict(x=x3, h=h, proj=proj, rt_a=rt_a, cqn=cqn, ckvn=ckvn, qb=qb, kvb=kvb, q_m=q_m, k_m=k_m, o_b=o_b, lse_b=lse_b,
                 q_c=q_c, k_c=k_c, o_c=o_c, lse_c=lse_c, mix=mix, att=three(att), x1=x1, h2=h2, up=up, a=a, yd=three(yd))
    return three(x2), saved


def _layer_bwd(dx2, sv, md, W, tabs, bias, tag):
    Bl, S, D = dx2.shape
    T = Bl * S
    n = lambda s: f"{s}_{tag}"
    two = lambda a: a.reshape(T, a.shape[-1])
    three = lambda a: a.reshape(Bl, S, a.shape[-1])
    g = {}
    dyb, dgate2 = gate_bwd(dx2, sv["yd"], md["gate2"], name=n("gate2_bwd"))
    da = three(matmul(two(dyb), W["w_down"], tb=True, tn=1408, name=n("down_dx")))
    g["w_down"] = matmul(two(sv["a"]), two(dyb), ta=True, tm=256, tn=1024, name=n("down_dw"))
    dup, dcw = conv_gate_bwd(sv["up"], W["conv_w"], W["conv_b"], da, name=n("conv_gate_bwd"))
    dh2 = three(matmul(two(dup), W["w_up"], tb=True, tn=1024, name=n("up_dx")))
    g["w_up"] = matmul(two(sv["h2"]), two(dup), ta=True, tn=1408, name=n("up_dw"))
    dx1, dn2, dsc2, dsh2 = rms_bwd(sv["x1"], 0, D, dh2, W["n2"], md["scale2"], dx2, name=n("norm2_bwd"))
    dmo, dgate1 = gate_bwd(dx1, sv["att"], md["gate1"], name=n("gate1_bwd"))
    dmix = three(matmul(two(dmo), W["w_out"], tb=True, tn=1024, out_dtype=BF16, name=n("out_dx")))
    g["w_out"] = matmul(two(sv["mix"]), two(dmo), ta=True, tn=1024, name=n("out_dw"))
    proj = sv["proj"]
    dq_a, dk_a, dv_a = sb_attn_bwd(proj, sv["rt_a"], dmix[:, :, 0:256], name=n("sb_bwd"))
    dq_m, dk_m, dv_b = mla_attn_bwd(sv["q_m"], sv["k_m"], sv["kvb"], 6, sv["o_b"], sv["lse_b"], dmix[:, :, 256:640], name=n("mla_bwd"))
    dqb, dqn = rope_norm_bwd(sv["qb"], 6, dq_m, W["qn_g"], tabs, name=n("q_rope_bwd"))
    dkn_x, dkn, dslab = rope_norm_bwd(sv["kvb"], 6, dk_m, W["kn_g"], tabs, (proj, P_SLAB // LANES), name=n("k_rope_bwd"))
    dkvb = _cat([dkn_x, dv_b]).astype(BF16)
    dckvn = three(matmul(two(dkvb), W["w_ukv"], tb=True, tm=1024, name=n("ukv_dx")))
    g["w_ukv"] = matmul(two(sv["ckvn"]), two(dkvb), ta=True, tn=1152, name=n("ukv_dw"))
    dcqn = three(matmul(two(dqb), W["w_uq"], tb=True, tm=1024, name=n("uq_dx")))
    g["w_uq"] = matmul(two(sv["cqn"]), two(dqb), ta=True, tn=768, name=n("uq_dw"))
    dcq, dcq_g = rms_bwd(proj, P_CQ // 256, 256, dcqn, W["cq_g"], name=n("cq_norm_bwd"))
    dckv, dckv_g = rms_bwd(proj, P_CKV // LANES, LANES, dckvn, W["ckv_g"], name=n("ckv_norm_bwd"))
    dq_c, dk_c, dv_c, dbias, dsink = swa_attn_bwd(sv["q_c"], sv["k_c"], proj, bias, W["sinks"], sv["o_c"], sv["lse_c"], dmix[:, :, 640:1024], name=n("swa_bwd"))
    dswq, dswq_g = pair_rms_bwd(proj, P_SWQ // LANES, 3, dq_c, W["swq_g"], name=n("swq_norm_bwd"))
    dswk, dswk_g = pair_rms_bwd(proj, P_SWK // LANES, 1, dk_c, W["swk_g"], name=n("swk_norm_bwd"))
    dproj = _cat([dq_a, dk_a, dv_a, dcq, dckv, dslab, dswq, dswk, dv_c]).astype(BF16)
    dh = three(matmul(two(dproj), W["w_in"], tb=True, tn=1024, name=n("in_dx")))
    g["w_in"] = matmul(two(sv["h"]), two(dproj), ta=True, tn=1920, tk=2048, name=n("in_dw"))
    dx, dn1, dsc1, dsh1 = rms_bwd(sv["x"], 0, D, dh, W["n1"], md["scale1"], dx1, name=n("norm1_bwd"))
    small = dict(n1=dn1, n2=dn2, cq_g=dcq_g, ckv_g=dckv_g, qn_g=dqn, kn_g=dkn, swq_g=dswq_g, swk_g=dswk_g, conv=dcw)
    dmods = _cat([dsh1, dsc1, dgate1, dsh2, dsc2, dgate2]).reshape(Bl, 6 * D)
    return dx, g, small, dmods, dbias, dsink


BIG = ("w_in", "w_uq", "w_ukv", "w_out", "w_up", "w_down")
ROW_SHARDED = ("w_out", "w_down")
PREP = dict(w_in=_prep_w_in, w_uq=_prep_w_uq, w_ukv=_prep_w_ukv, w_out=_prep_w_out, w_up=_up_perm, w_down=lambda w: w)
UNPREP = dict(w_in=_unprep_w_in, w_uq=_unprep_w_uq, w_ukv=_unprep_w_ukv, w_out=_unprep_w_out, w_up=_up_perm, w_down=lambda w: w)
NCHIPS = 4


def _local_step(x, target, positions, mods, Wl, rel_flat):
    Bl, S, D = x.shape
    L = len(Wl)
    tabs = _rope_tables(positions)
    bucket = _bucket_table()
    bias = swa_bias(rel_flat, bucket, name="swa_bias")
    mds = []
    for l in range(L):
        parts = [mods[l, :, D * k:D * (k + 1)].reshape(Bl, 1, D) for k in range(6)]
        mds.append(dict(zip(("shift1", "scale1", "gate1", "shift2", "scale2", "gate2"), parts)))
    saved = []
    h = x
    for l in range(L):
        h, sv = _layer_fwd(h, mds[l], Wl[l], tabs, bias, f"l{l}")
        saved.append(sv)
    dy, loss = loss_grad(h, target, name="loss")
    grads, smalls, dmods, dbiases, dsinks = [None] * L, [None] * L, [None] * L, [None] * L, [None] * L
    for l in reversed(range(L)):
        dy, grads[l], smalls[l], dmods[l], dbiases[l], dsinks[l] = _layer_bwd(dy, saved[l], mds[l], Wl[l], tabs, bias, f"l{l}")
    drel = swa_bias_bwd(_cat(dbiases, axis=0), bucket, name="swa_bias_bwd")
    return loss, dy, grads, smalls, dmods, dsinks, drel


WEIGHTS =("rel_table", "norm1_g", "norm2_g", "w_ada", "b_ada", "w_in", "mla_cq_g", "w_uq", "mla_ckv_g", "w_ukv", "mla_qn_g", "mla_kn_g",
           "sw_qn_g", "sw_kn_g", "sw_sinks", "w_out", "w_up", "conv_w", "conv_b", "w_down")
SMALL = tuple(n for n in WEIGHTS if n not in BIG + ("w_ada",))


def kernel(x, c, positions, rel_table, norm1_g, norm2_g, w_ada, b_ada, w_in, mla_cq_g, w_uq, mla_ckv_g, w_ukv, mla_qn_g, mla_kn_g, sw_qn_g, sw_kn_g, sw_sinks, w_out, w_up, conv_w, conv_b, w_down, loss_target, m_rel_table, m_norm1_g, m_norm2_g, m_w_ada, m_b_ada, m_w_in, m_mla_cq_g, m_w_uq, m_mla_ckv_g, m_w_ukv, m_mla_qn_g, m_mla_kn_g, m_sw_qn_g, m_sw_kn_g, m_sw_sinks, m_w_out, m_w_up, m_conv_w, m_conv_b, m_w_down, v_rel_table, v_norm1_g, v_norm2_g, v_w_ada, v_b_ada, v_w_in, v_mla_cq_g, v_w_uq, v_mla_ckv_g, v_w_ukv, v_mla_qn_g, v_mla_kn_g, v_sw_qn_g, v_sw_kn_g, v_sw_sinks, v_w_out, v_w_up, v_conv_w, v_conv_b, v_w_down):
    w = dict(rel_table=rel_table, norm1_g=norm1_g, norm2_g=norm2_g, w_ada=w_ada, b_ada=b_ada, w_in=w_in, mla_cq_g=mla_cq_g, w_uq=w_uq,
             mla_ckv_g=mla_ckv_g, w_ukv=w_ukv, mla_qn_g=mla_qn_g, mla_kn_g=mla_kn_g, sw_qn_g=sw_qn_g, sw_kn_g=sw_kn_g, sw_sinks=sw_sinks,
             w_out=w_out, w_up=w_up, conv_w=conv_w, conv_b=conv_b, w_down=w_down)
    m = dict(rel_table=m_rel_table, norm1_g=m_norm1_g, norm2_g=m_norm2_g, w_ada=m_w_ada, b_ada=m_b_ada, w_in=m_w_in, mla_cq_g=m_mla_cq_g,
             w_uq=m_w_uq, mla_ckv_g=m_mla_ckv_g, w_ukv=m_w_ukv, mla_qn_g=m_mla_qn_g, mla_kn_g=m_mla_kn_g, sw_qn_g=m_sw_qn_g,
             sw_kn_g=m_sw_kn_g, sw_sinks=m_sw_sinks, w_out=m_w_out, w_up=m_w_up, conv_w=m_conv_w, conv_b=m_conv_b, w_down=m_w_down)
    v = dict(rel_table=v_rel_table, norm1_g=v_norm1_g, norm2_g=v_norm2_g, w_ada=v_w_ada, b_ada=v_b_ada, w_in=v_w_in, mla_cq_g=v_mla_cq_g,
             w_uq=v_w_uq, mla_ckv_g=v_mla_ckv_g, w_ukv=v_w_ukv, mla_qn_g=v_mla_qn_g, mla_kn_g=v_mla_kn_g, sw_qn_g=v_sw_qn_g,
             sw_kn_g=v_sw_kn_g, sw_sinks=v_sw_sinks, w_out=v_w_out, w_up=v_w_up, conv_w=v_conv_w, conv_b=v_conv_b, w_down=v_w_down)
    Bl, S, D = x.shape
    L = norm1_g.shape[0]
    xi, yi, ci = _me()
    chip = 2 * xi + yi
    dev = 4 * xi + 2 * yi + ci
    ndev = 2 * NCHIPS

    shapes = {k: w[k].shape[1:] for k in BIG}
    got_w = gather_weights([w[k].astype(BF16).reshape(L * shapes[k][0], shapes[k][1]) for k in BIG], name="gather_weights")
    full = [dict() for _ in range(L)]
    for k, w4 in zip(BIG, got_w):
        r, cc = shapes[k]
        for l in range(L):
            seg = w4[:, l * r:(l + 1) * r, :]
            fw = seg.reshape(NCHIPS * r, cc) if k in ROW_SHARDED else jnp.transpose(seg, (1, 0, 2)).reshape(r, NCHIPS * cc)
            full[l][k] = PREP[k](fw)

    cw_cols = conv_w.shape[2]
    c_got, cw_got = allgather8([c, conv_w.reshape(L * 3, cw_cols)], name="gather_cond")
    c_all = c_got.reshape(ndev * Bl, D)
    conv_full = jnp.transpose(cw_got[0::2].reshape(NCHIPS, L, 3, cw_cols), (1, 2, 0, 3)).reshape(L, 3, NCHIPS * cw_cols)
    E = w_ada.shape[2]
    b_cols = lax.dynamic_slice(b_ada, (0, chip * E), (L, E)).reshape(L, 1, E)
    mods_cols = mods_matmul(c_all, w_ada, b_cols, name="mods")
    mods_all, = allgather8([mods_cols.reshape(L * ndev * Bl, E)], name="gather_mods")
    mods_all = jnp.transpose(mods_all[0::2].reshape(NCHIPS, L, ndev * Bl, E), (1, 2, 0, 3)).reshape(L, ndev * Bl, NCHIPS * E)
    mods = lax.dynamic_slice(mods_all, (0, dev * Bl, 0), (L, Bl, NCHIPS * E))

    Wl = []
    for l in range(L):
        Wd = _small_params({k: w[k][l] for k in SMALL if k not in ("rel_table", "b_ada", "conv_w")})
        Wd.update(full[l])
        Wd["conv_w"] = _up_perm(conv_full[l])
        Wl.append(Wd)

    loss, dx, grads, smalls, dmods, dsinks, drel = _local_step(x, loss_target, positions, mods, Wl, rel_table.reshape(-1))

    g4s = []
    for k in BIG:
        r, cc = shapes[k]
        per_layer = []
        for l in range(L):
            gk = UNPREP[k](grads[l][k])
            per_layer.append(gk.reshape(NCHIPS, r, cc) if k in ROW_SHARDED else jnp.transpose(gk.reshape(r, NCHIPS, cc), (1, 0, 2)))
        g4s.append(_cat(per_layer, axis=1))
    theirs = swap_halves(g4s, name="rs_swap_halves")
    c_arr = ci.reshape(1).astype(jnp.int32)
    pairs = [pair_add_half(g4, th, c_arr, name=f"rs_pair_add_{k}") for k, g4, th in zip(BIG, g4s, theirs)]
    landed = scatter_chips(pairs, name="rs_scatter_chips")
    halves = [sum_leading(ld, name=f"rs_chip_sum_{k}") for k, ld in zip(BIG, landed)]
    joined = join_halves(halves, name="rs_join_halves")
    grad = {k: j.reshape((L,) + tuple(shapes[k])) for k, j in zip(BIG, joined)}

    vec_names = ("n1", "n2", "cq_g", "ckv_g", "qn_g", "kn_g", "swq_g", "swk_g")
    vecs = _cat([_cat([smalls[l][k] for k in vec_names], axis=1) for l in range(L)], axis=0)
    convs = _cat([smalls[l]["conv"] for l in range(L)], axis=0)
    dm = jnp.stack(dmods, axis=1).reshape(Bl * L, 6 * D)
    dsk = jnp.stack(dsinks, axis=1).reshape(Bl * L * 6, LANES)
    got = allgather8([vecs, convs, drel, loss, dm, dsk], name="gather_small_grads")
    seq = lambda a, rows: a.reshape(ndev * Bl, rows, a.shape[-1])
    vec_s, conv_s, rel_s, loss_s, dm_s, dsk_s = sum_small(list(got[:4]) + [seq(got[4], L), seq(got[5], L * 6)], name="sum_small_grads")
    dm_all = jnp.transpose(seq(got[4], L), (1, 0, 2))
    grad["w_ada"] = ada_grad(c_all, lax.dynamic_slice(dm_all, (0, 0, chip * E), (L, ndev * Bl, E)), name="ada_grad")
    grad["b_ada"] = dm_s
    grad["sw_sinks"] = jnp.transpose(dsk_s.reshape(L, 3, 2, LANES)[:, :, :, 0], (0, 2, 1)).reshape(L, 6)
    grad["rel_table"] = rel_s[:6, :REL_BUCKETS].T
    off = 0
    for k, name_, keep in zip(vec_names, ("norm1_g", "norm2_g", "mla_cq_g", "mla_ckv_g", "mla_qn_g", "mla_kn_g", "sw_qn_g", "sw_kn_g"),
                              (D, D, 256, LANES, MLA_QK, MLA_QK, HEAD, HEAD)):
        grad[name_] = vec_s[:, off:off + keep]
        off += smalls[0][k].shape[1]
    conv = _up_perm(conv_s.reshape(L, 8, 2 * D_FF))
    grad["conv_w"] = lax.dynamic_slice(conv[:, 0:3], (0, 0, chip * cw_cols), (L, 3, cw_cols))
    grad["conv_b"] = conv[:, 3]
    loss_out = loss_s[0, 0]

    delta, new_m, new_v = {}, {}, {}
    for k in BIG + ("w_ada",):
        shp = w[k].shape
        to2 = lambda a: a.reshape(-1, shp[-1])
        d_, m_, v_ = adamw(to2(w[k]), to2(grad[k]), to2(m[k]), to2(v[k]), name=f"adamw_{k}")
        delta[k], new_m[k], new_v[k] = d_.reshape(shp), m_.reshape(shp), v_.reshape(shp)
    outs = adamw_small(*[[src[k] for k in SMALL] for src in (w, grad, m, v)], name="adamw_small")
    for dst, o in zip((delta, new_m, new_v), outs):
        dst.update(dict(zip(SMALL, o)))
    return (loss_out, dx, *[grad[k] for k in WEIGHTS], *[delta[k] for k in WEIGHTS], *[new_m[k] for k in WEIGHTS], *[new_v[k] for k in WEIGHTS])
```

```python
import functools
import math

import jax
import jax.numpy as jnp
from jax import lax
from jax.experimental import pallas as pl
from jax.experimental.pallas import tpu as pltpu

F32 = jnp.float32
BF16 = jnp.bfloat16
MESH = pl.DeviceIdType.MESH

EPS = 1e-6
NEG = -1e30
HEAD = 64
LANES = 128
MLA_QK = 96
ROPE_THETA = 10000.0
REL_BUCKETS = 32
REL_MAX_DIST = 128
WINDOW = 128
D_FF = 2816
ADAM_LR, ADAM_B1, ADAM_B2, ADAM_EPS, ADAM_WD, ADAM_STEP = 0.001, 0.9, 0.999, 1e-08, 0.01, 10

VMEM_LIMIT = 56 * 1024 * 1024

P_SBQ, P_SBK, P_SBV, P_CQ, P_CKV, P_SLAB, P_SWQ, P_SWK, P_SWV, P_END = 0, 256, 512, 768, 1024, 1152, 1280, 1664, 1792, 1920
SW_PERM = (0, 3, 1, 4, 2, 5)


def _cp(*sem):
    return pltpu.CompilerParams(dimension_semantics=sem, vmem_limit_bytes=VMEM_LIMIT)


def _dot(a, b):
    return jnp.dot(a, b, preferred_element_type=F32)


def _dot_nt(a, b):
    return lax.dot_general(a, b, (((1,), (1,)), ((), ())), preferred_element_type=F32)


def _dot_tn(a, b):
    return lax.dot_general(a, b, (((0,), (0,)), ((), ())), preferred_element_type=F32)


def _split_dot(x, u):
    hi = x.astype(BF16)
    lo = (x - hi.astype(F32)).astype(BF16)
    return _dot(hi, u) + _dot(lo, u)


def _lane_masks():
    lane = lax.broadcasted_iota(jnp.int32, (1, LANES), 1)
    return (lane < HEAD, lane >= HEAD)


def _tile(n, cap, align=128):
    if n <= cap:
        return n
    t = cap - cap % align
    while t >= align:
        if n % t == 0:
            return t
        t -= align
    return n


def matmul(a, b, *, ta=False, tb=False, out_dtype=F32, tm=512, tn=512, tk=8192, name):
    M, K = (a.shape[1], a.shape[0]) if ta else a.shape
    N = b.shape[0] if tb else b.shape[1]
    tm, tn, tk = _tile(M, tm), _tile(N, tn), _tile(K, tk)
    nk = K // tk

    def body(a_ref, b_ref, o_ref, *scratch):
        av = a_ref[...].astype(BF16)
        bv = b_ref[...].astype(BF16)
        if ta:
            part = _dot_tn(av, bv)
        elif tb:
            part = _dot_nt(av, bv)
        else:
            part = _dot(av, bv)
        if nk == 1:
            o_ref[...] = part.astype(out_dtype)
        else:
            acc_ref, = scratch
            k = pl.program_id(2)

            @pl.when(k == 0)
            def _():
                acc_ref[...] = part

            @pl.when(k > 0)
            def _():
                acc_ref[...] += part

            @pl.when(k == nk - 1)
            def _():
                o_ref[...] = acc_ref[...].astype(out_dtype)

    a_spec = pl.BlockSpec((tk, tm), lambda i, j, k: (k, i)) if ta else pl.BlockSpec((tm, tk), lambda i, j, k: (i, k))
    b_spec = pl.BlockSpec((tn, tk), lambda i, j, k: (j, k)) if tb else pl.BlockSpec((tk, tn), lambda i, j, k: (k, j))
    return pl.pallas_call(
        body, name=name, grid=(M // tm, N // tn, nk),
        in_specs=[a_spec, b_spec], out_specs=pl.BlockSpec((tm, tn), lambda i, j, k: (i, j)),
        out_shape=jax.ShapeDtypeStruct((M, N), out_dtype),
        scratch_shapes=[] if nk == 1 else [pltpu.VMEM((tm, tn), F32)],
        compiler_params=_cp("parallel", "parallel", "arbitrary"),
    )(a, b)


def matmul_res(a, b, res, gate, seq, *, tm=512, tn=1024, name):
    M, K = a.shape
    N = b.shape[1]
    tm, tn = _tile(min(M, seq), tm), _tile(N, tn)
    per_seq = seq // tm

    def body(a_ref, b_ref, r_ref, g_ref, y_ref, x_ref):
        y = _dot(a_ref[...].astype(BF16), b_ref[...].astype(BF16))
        y_ref[...] = y
        x_ref[...] = r_ref[...] + g_ref[...] * y

    out = jax.ShapeDtypeStruct((M, N), F32)
    return pl.pallas_call(
        body, name=name, grid=(M // tm, N // tn),
        in_specs=[pl.BlockSpec((tm, K), lambda i, j: (i, 0)), pl.BlockSpec((K, tn), lambda i, j: (0, j)),
                  pl.BlockSpec((tm, tn), lambda i, j: (i, j)), pl.BlockSpec((None, 1, tn), lambda i, j: (lax.div(i, jnp.int32(per_seq)), 0, j))],
        out_specs=[pl.BlockSpec((tm, tn), lambda i, j: (i, j))] * 2,
        out_shape=[out, out], compiler_params=_cp("parallel", "parallel"),
    )(a, b, res, gate)


def rms_fwd(x3, blk, W, g, sc=None, sh=None, *, tm=512, name):
    Bl, S, _ = x3.shape
    tm = min(tm, S)
    mod = sc is not None

    def body(x_ref, g_ref, *rest):
        o_ref = rest[-1]
        x = x_ref[...]
        r = lax.rsqrt(jnp.mean(x * x, axis=-1, keepdims=True) + EPS)
        y = x * r * g_ref[...]
        if mod:
            y = y * (1.0 + rest[0][...]) + rest[1][...]
        o_ref[...] = y.astype(BF16)

    vec = pl.BlockSpec((None, 1, W), lambda b, s: (b, 0, 0))
    return pl.pallas_call(
        body, name=name, grid=(Bl, S // tm),
        in_specs=[pl.BlockSpec((None, tm, W), lambda b, s: (b, s, blk)), pl.BlockSpec((1, W), lambda b, s: (0, 0))] + ([vec, vec] if mod else []),
        out_specs=pl.BlockSpec((None, tm, W), lambda b, s: (b, s, 0)),
        out_shape=jax.ShapeDtypeStruct((Bl, S, W), BF16),
        compiler_params=_cp("parallel", "parallel"),
    )(x3, g, *([sc, sh] if mod else []))


def rms_bwd(x3, blk, W, dy3, g, sc=None, dres3=None, *, tm=256, name):
    Bl, S, _ = x3.shape
    tm = min(tm, S)
    mod = sc is not None
    res = dres3 is not None

    def body(*refs):
        x_ref, dy_ref, g_ref = refs[:3]
        k = 3
        sc_ref = dr_ref = None
        if mod:
            sc_ref = refs[k]
            k += 1
        if res:
            dr_ref = refs[k]
            k += 1
        dx_ref, dg_ref = refs[k], refs[k + 1]
        b, s = pl.program_id(0), pl.program_id(1)
        x = x_ref[...]
        dy = dy_ref[...].astype(F32)
        g = g_ref[...]
        r = lax.rsqrt(jnp.mean(x * x, axis=-1, keepdims=True) + EPS)
        n = x * r
        if mod:
            dsc_ref, dsh_ref = refs[k + 2], refs[k + 3]
            one_sc = 1.0 + sc_ref[...]

            @pl.when(s == 0)
            def _():
                dsc_ref[...] = jnp.zeros_like(dsc_ref)
                dsh_ref[...] = jnp.zeros_like(dsh_ref)

            dsh_ref[...] += jnp.sum(dy, axis=0, keepdims=True)
            dsc_ref[...] += jnp.sum(dy * n * g, axis=0, keepdims=True)
            dyn = dy * one_sc
        else:
            dyn = dy

        @pl.when((b == 0) & (s == 0))
        def _():
            dg_ref[...] = jnp.zeros_like(dg_ref)

        dg_ref[...] += jnp.sum(dyn * n, axis=0, keepdims=True)
        dn = dyn * g
        dx = r * (dn - n * jnp.mean(dn * n, axis=-1, keepdims=True))
        if res:
            dx = dx + dr_ref[...]
        dx_ref[...] = dx

    blkspec = pl.BlockSpec((None, tm, W), lambda b, s: (b, s, 0))
    vec = pl.BlockSpec((None, 1, W), lambda b, s: (b, 0, 0))
    row = pl.BlockSpec((1, W), lambda b, s: (0, 0))
    in_specs = [pl.BlockSpec((None, tm, W), lambda b, s: (b, s, blk)), blkspec, row] + ([vec] if mod else []) + ([blkspec] if res else [])
    out_specs = [blkspec, row] + ([vec, vec] if mod else [])
    out_shape = [jax.ShapeDtypeStruct((Bl, S, W), F32), jax.ShapeDtypeStruct((1, W), F32)]
    if mod:
        out_shape += [jax.ShapeDtypeStruct((Bl, 1, W), F32)] * 2
    args = [x3, dy3, g] + ([sc] if mod else []) + ([dres3] if res else [])
    return pl.pallas_call(
        body, name=name, grid=(Bl, S // tm), in_specs=in_specs, out_specs=out_specs, out_shape=out_shape,
        compiler_params=_cp("arbitrary", "arbitrary"),
    )(*args)


def pair_rms_fwd(x3, blk0, npairs, g2, *, tm=1024, name):
    Bl, S, _ = x3.shape
    tm = min(tm, S)

    def body(x_ref, g_ref, o_ref):
        lo, hi = _lane_masks()
        x = x_ref[...]
        xx = x * x
        s0 = jnp.sum(jnp.where(lo, xx, 0.0), axis=-1, keepdims=True)
        s1 = jnp.sum(jnp.where(hi, xx, 0.0), axis=-1, keepdims=True)
        r = jnp.where(lo, lax.rsqrt(s0 / HEAD + EPS), lax.rsqrt(s1 / HEAD + EPS))
        o_ref[...] = (x * r * g_ref[...]).astype(BF16)

    return pl.pallas_call(
        body, name=name, grid=(Bl, S // tm, npairs),
        in_specs=[pl.BlockSpec((None, tm, LANES), lambda b, s, p: (b, s, blk0 + p)), pl.BlockSpec((1, LANES), lambda b, s, p: (0, 0))],
        out_specs=pl.BlockSpec((None, tm, LANES), lambda b, s, p: (b, s, p)),
        out_shape=jax.ShapeDtypeStruct((Bl, S, LANES * npairs), BF16),
        compiler_params=_cp("parallel", "parallel", "parallel"),
    )(x3, g2)


def pair_rms_bwd(x3, blk0, npairs, dy3, g2, *, tm=1024, name):
    Bl, S, _ = x3.shape
    tm = min(tm, S)

    def body(x_ref, dy_ref, g_ref, dx_ref, dg_ref):
        lo, hi = _lane_masks()
        first = (pl.program_id(0) == 0) & (pl.program_id(1) == 0) & (pl.program_id(2) == 0)
        x = x_ref[...]
        dy = dy_ref[...]
        xx = x * x
        s0 = jnp.sum(jnp.where(lo, xx, 0.0), axis=-1, keepdims=True)
        s1 = jnp.sum(jnp.where(hi, xx, 0.0), axis=-1, keepdims=True)
        r = jnp.where(lo, lax.rsqrt(s0 / HEAD + EPS), lax.rsqrt(s1 / HEAD + EPS))
        n = x * r

        @pl.when(first)
        def _():
            dg_ref[...] = jnp.zeros_like(dg_ref)

        part = jnp.sum(dy * n, axis=0, keepdims=True)
        dg_ref[...] += part + pltpu.roll(part, HEAD, 1)
        dn = dy * g_ref[...]
        t = dn * n
        m0 = jnp.sum(jnp.where(lo, t, 0.0), axis=-1, keepdims=True)
        m1 = jnp.sum(jnp.where(hi, t, 0.0), axis=-1, keepdims=True)
        dx_ref[...] = r * (dn - n * (jnp.where(lo, m0, m1) / HEAD))

    return pl.pallas_call(
        body, name=name, grid=(Bl, S // tm, npairs),
        in_specs=[pl.BlockSpec((None, tm, LANES), lambda b, s, p: (b, s, blk0 + p)), pl.BlockSpec((None, tm, LANES), lambda b, s, p: (b, s, p)),
                  pl.BlockSpec((1, LANES), lambda b, s, p: (0, 0))],
        out_specs=[pl.BlockSpec((None, tm, LANES), lambda b, s, p: (b, s, p)), pl.BlockSpec((1, LANES), lambda b, s, p: (0, 0))],
        out_shape=[jax.ShapeDtypeStruct((Bl, S, LANES * npairs), F32), jax.ShapeDtypeStruct((1, LANES), F32)],
        compiler_params=_cp("arbitrary", "arbitrary", "arbitrary"),
    )(x3, dy3, g2)


def _rot(y, cos_t, sin_a, sin_b):
    return y * cos_t + pltpu.roll(y, LANES - 16, 1) * sin_a + pltpu.roll(y, 16, 1) * sin_b


def _rot_t(d, cos_t, sin_a, sin_b):
    return d * cos_t + pltpu.roll(d * sin_a, 16, 1) + pltpu.roll(d * sin_b, LANES - 16, 1)


def rope_norm_fwd(x3, nheads, g, tabs, slab=None, *, tm=1024, name):
    Bl, S, _ = x3.shape
    tm = min(tm, S)
    has_slab = slab is not None

    def body(*refs):
        x_ref, g_ref, c_ref, sa_ref, sb_ref = refs[:5]
        o_ref = refs[-1]
        x = x_ref[...]
        if has_slab:
            x = x + refs[5][...]
        r = lax.rsqrt(jnp.sum(x * x, axis=-1, keepdims=True) / MLA_QK + EPS)
        o_ref[...] = _rot(x * r * g_ref[...], c_ref[...], sa_ref[...], sb_ref[...]).astype(BF16)

    head = pl.BlockSpec((None, tm, LANES), lambda b, s, h: (b, s, h))
    tab = pl.BlockSpec((None, tm, LANES), lambda b, s, h: (b, s, 0))
    in_specs = [head, pl.BlockSpec((1, LANES), lambda b, s, h: (0, 0)), tab, tab, tab]
    args = [x3, g, *tabs]
    if has_slab:
        sblk = slab[1]
        in_specs.append(pl.BlockSpec((None, tm, LANES), lambda b, s, h: (b, s, sblk)))
        args.append(slab[0])
    return pl.pallas_call(
        body, name=name, grid=(Bl, S // tm, nheads), in_specs=in_specs, out_specs=head,
        out_shape=jax.ShapeDtypeStruct((Bl, S, LANES * nheads), BF16),
        compiler_params=_cp("parallel", "parallel", "parallel"),
    )(*args)


def rope_norm_bwd(x3, nheads, dy3, g, tabs, slab=None, *, tm=1024, name):
    Bl, S, _ = x3.shape
    tm = min(tm, S)
    has_slab = slab is not None

    def body(*refs):
        x_ref, dy_ref, g_ref, c_ref, sa_ref, sb_ref = refs[:6]
        k = 7 if has_slab else 6
        dx_ref, dg_ref = refs[k], refs[k + 1]
        h = pl.program_id(2)
        first = (pl.program_id(0) == 0) & (pl.program_id(1) == 0) & (h == 0)
        x = x_ref[...]
        if has_slab:
            x = x + refs[6][...]
        g = g_ref[...]
        r = lax.rsqrt(jnp.sum(x * x, axis=-1, keepdims=True) / MLA_QK + EPS)
        n = x * r
        d = _rot_t(dy_ref[...], c_ref[...], sa_ref[...], sb_ref[...])

        @pl.when(first)
        def _():
            dg_ref[...] = jnp.zeros_like(dg_ref)

        dg_ref[...] += jnp.sum(d * n, axis=0, keepdims=True)
        dn = d * g
        dx = r * (dn - n * (jnp.sum(dn * n, axis=-1, keepdims=True) / MLA_QK))
        dx_ref[...] = dx
        if has_slab:
            ds_ref = refs[k + 2]

            @pl.when(h == 0)
            def _():
                ds_ref[...] = dx

            @pl.when(h > 0)
            def _():
                ds_ref[...] += dx

    head = pl.BlockSpec((None, tm, LANES), lambda b, s, h: (b, s, h))
    tab = pl.BlockSpec((None, tm, LANES), lambda b, s, h: (b, s, 0))
    row = pl.BlockSpec((1, LANES), lambda b, s, h: (0, 0))
    in_specs = [head, head, row, tab, tab, tab]
    args = [x3, dy3, g, *tabs]
    out_specs = [head, row]
    out_shape = [jax.ShapeDtypeStruct((Bl, S, LANES * nheads), F32), jax.ShapeDtypeStruct((1, LANES), F32)]
    if has_slab:
        sblk = slab[1]
        in_specs.append(pl.BlockSpec((None, tm, LANES), lambda b, s, h: (b, s, sblk)))
        args.append(slab[0])
        out_specs.append(tab)
        out_shape.append(jax.ShapeDtypeStruct((Bl, S, LANES), F32))
    return pl.pallas_call(
        body, name=name, grid=(Bl, S // tm, nheads), in_specs=in_specs, out_specs=out_specs, out_shape=out_shape,
        compiler_params=_cp("arbitrary", "arbitrary", "arbitrary"),
    )(*args)


def _sb_tile(z, strict, u, carry_r):
    sp = jnp.maximum(z, 0.0) + jnp.log(1.0 + jnp.exp(-jnp.abs(z)))
    keep = jnp.where(strict, -sp, 0.0)
    logw = (z - sp) + _split_dot(keep, u) + carry_r
    return jnp.where(strict, jnp.exp(logw), 0.0), keep, sp


SB_BLOCK = 256
SB_QBLOCK = 512


def sb_attn_fwd(proj3, *, name):
    Bl, S, _ = proj3.shape
    tk = min(SB_BLOCK, S)
    tq = min(SB_QBLOCK, S)
    per_q = tq // tk
    scale = HEAD ** -0.5
    qb, kb0, vb0 = P_SBQ // LANES, P_SBK // LANES, P_SBV // LANES

    def body(q_ref, k_ref, v_ref, o_ref, rt_ref):
        i = pl.program_id(2)
        masks = _lane_masks()
        lane = lax.broadcasted_iota(jnp.int32, (1, LANES), 1)
        q = q_ref[...]
        qh = [jnp.where(m, q, 0.0).astype(BF16) for m in masks]
        rr = lax.broadcasted_iota(jnp.int32, (tq, tk), 0)
        cc = lax.broadcasted_iota(jnp.int32, (tq, tk), 1)
        u = (lax.broadcasted_iota(jnp.int32, (tk, tk), 0) > lax.broadcasted_iota(jnp.int32, (tk, tk), 1)).astype(BF16)

        rt_ref[...] = jnp.zeros_like(rt_ref)

        def step(t, carry):
            r0, r1, acc = carry
            j = (i + 1) * per_q - 1 - t
            off = pl.multiple_of(j * tk, tk)
            kb = k_ref[pl.ds(off, tk), :].astype(BF16)
            vb = v_ref[pl.ds(off, tk), :]
            strict = (cc + j * tk) < (rr + i * tq)
            rt_ref[...] = jnp.where(lane == j, r0, jnp.where(lane == j + HEAD, r1, rt_ref[...]))
            rs = [r0, r1]
            for h in range(2):
                z = _dot_nt(qh[h], kb) * scale
                w, keep, _ = _sb_tile(z, strict, u, rs[h])
                acc = acc + _dot(w.astype(BF16), jnp.where(masks[h], vb, 0.0).astype(BF16))
                rs[h] = rs[h] + jnp.sum(keep, axis=1, keepdims=True)
            return rs[0], rs[1], acc

        zero = jnp.zeros((tq, 1), F32)
        _, _, acc = lax.fori_loop(0, (i + 1) * per_q, step, (zero, zero, jnp.zeros((tq, LANES), F32)))
        o_ref[...] = acc

    seq = lambda blk0: pl.BlockSpec((None, S, LANES), lambda b, p, i: (b, 0, blk0 + p))
    out = pl.BlockSpec((None, tq, LANES), lambda b, p, i: (b, i, p))
    shp = jax.ShapeDtypeStruct((Bl, S, 2 * LANES), F32)
    return pl.pallas_call(
        body, name=name, grid=(Bl, 2, S // tq),
        in_specs=[pl.BlockSpec((None, tq, LANES), lambda b, p, i: (b, i, qb + p)), seq(kb0), seq(vb0)],
        out_specs=[out, out], out_shape=[shp, shp],
        compiler_params=_cp("parallel", "parallel", "arbitrary"),
    )(proj3, proj3, proj3)


def sb_attn_bwd(proj3, rt3, do3, *, name):
    Bl, S, _ = proj3.shape
    tk = min(SB_BLOCK, S)
    tq = min(SB_QBLOCK, S)
    per_q = tq // tk
    scale = HEAD ** -0.5
    qb, kb0, vb0 = P_SBQ // LANES, P_SBK // LANES, P_SBV // LANES

    def body(q_ref, k_ref, v_ref, rt_ref, do_ref, dq_ref, dk_ref, dv_ref):
        i = pl.program_id(2)

        @pl.when(i == 0)
        def _():
            dk_ref[...] = jnp.zeros_like(dk_ref)
            dv_ref[...] = jnp.zeros_like(dv_ref)

        masks = _lane_masks()
        lane = lax.broadcasted_iota(jnp.int32, (1, LANES), 1)
        q = q_ref[...]
        qh = [jnp.where(m, q, 0.0).astype(BF16) for m in masks]
        do_b = do_ref[...].astype(BF16)
        doh = [jnp.where(m, do_b, jnp.zeros_like(do_b)) for m in masks]
        rt = rt_ref[...]
        rr = lax.broadcasted_iota(jnp.int32, (tq, tk), 0)
        cc = lax.broadcasted_iota(jnp.int32, (tq, tk), 1)
        ur = lax.broadcasted_iota(jnp.int32, (tk, tk), 0)
        uc = lax.broadcasted_iota(jnp.int32, (tk, tk), 1)
        u_suffix = (ur > uc).astype(BF16)
        u_prefix = (ur < uc).astype(BF16)

        def step(j, carry):
            p0, p1, dq = carry
            off = pl.multiple_of(j * tk, tk)
            kf = k_ref[pl.ds(off, tk), :]
            kb = kf.astype(BF16)
            vb = v_ref[pl.ds(off, tk), :]
            strict = (cc + j * tk) < (rr + i * tq)
            ps = [p0, p1]
            dk_acc = jnp.zeros((tk, LANES), F32)
            dv_acc = jnp.zeros((tk, LANES), F32)
            for h in range(2):
                r_j = jnp.sum(jnp.where(lane == j + h * HEAD, rt, 0.0), axis=1, keepdims=True)
                z = _dot_nt(qh[h], kb) * scale
                w, _, sp = _sb_tile(z, strict, u_suffix, r_j)
                vh = jnp.where(masks[h], vb, 0.0).astype(BF16)
                g = _dot_nt(doh[h], vh) * w
                pre = _split_dot(g, u_prefix) + ps[h]
                dz = jnp.where(strict, g * jnp.exp(-sp) - jnp.exp(z - sp) * pre, 0.0) * scale
                dzb = dz.astype(BF16)
                dq = dq + _dot(dzb, jnp.where(masks[h], kf, 0.0).astype(BF16))
                dk_acc = dk_acc + _dot_tn(dzb, qh[h])
                dv_acc = dv_acc + _dot_tn(w.astype(BF16), doh[h])
                ps[h] = ps[h] + jnp.sum(g, axis=1, keepdims=True)
            dk_ref[pl.ds(off, tk), :] += dk_acc
            dv_ref[pl.ds(off, tk), :] += dv_acc
            return ps[0], ps[1], dq

        zero = jnp.zeros((tq, 1), F32)
        out = lax.fori_loop(0, (i + 1) * per_q, step, (zero, zero, jnp.zeros((tq, LANES), F32)))
        dq_ref[...] = out[2]

    seq_in = lambda blk0: pl.BlockSpec((None, S, LANES), lambda b, p, i: (b, 0, blk0 + p))
    blk = pl.BlockSpec((None, tq, LANES), lambda b, p, i: (b, i, p))
    seq_out = pl.BlockSpec((None, S, LANES), lambda b, p, i: (b, 0, p))
    shp = jax.ShapeDtypeStruct((Bl, S, 2 * LANES), F32)
    return pl.pallas_call(
        body, name=name, grid=(Bl, 2, S // tq),
        in_specs=[pl.BlockSpec((None, tq, LANES), lambda b, p, i: (b, i, qb + p)), seq_in(kb0), seq_in(vb0), blk, blk],
        out_specs=[blk, seq_out, seq_out], out_shape=[shp, shp, shp],
        compiler_params=_cp("parallel", "parallel", "arbitrary"),
    )(proj3, proj3, proj3, rt3, do3)


def mla_attn_fwd(q3, k3, kv3, vblk0, *, tq=512, tk=256, name):
    Bl, S, _ = q3.shape
    tq = min(tq, S)
    tk = min(tk, tq)
    per_q = tq // tk
    scale = MLA_QK ** -0.5

    def body(q_ref, k_ref, v_ref, o_ref, lse_ref):
        i = pl.program_id(2)
        masks = _lane_masks()
        rr = lax.broadcasted_iota(jnp.int32, (tq, tk), 0)
        cc = lax.broadcasted_iota(jnp.int32, (tq, tk), 1)
        qh = [q_ref[:, h * LANES:(h + 1) * LANES] for h in range(2)]

        def step(j, carry):
            m0, l0, m1, l1, acc = carry
            off = pl.multiple_of(j * tk, tk)
            vb = v_ref[pl.ds(off, tk), :]
            causal = (cc + j * tk) <= (rr + i * tq)
            ms, ls, alphas = [m0, m1], [l0, l1], []
            add = jnp.zeros((tq, LANES), F32)
            for h in range(2):
                kh = k_ref[pl.ds(off, tk), h * LANES:(h + 1) * LANES]
                s = jnp.where(causal, _dot_nt(qh[h], kh) * scale, NEG)
                m_new = jnp.maximum(ms[h], jnp.max(s, axis=1, keepdims=True))
                p = jnp.exp(s - m_new)
                alpha = jnp.exp(ms[h] - m_new)
                ls[h] = alpha * ls[h] + jnp.sum(p, axis=1, keepdims=True)
                ms[h] = m_new
                alphas.append(alpha)
                add = add + _dot(p.astype(BF16), jnp.where(masks[h], vb, 0.0).astype(BF16))
            acc = acc * jnp.where(masks[0], alphas[0], alphas[1]) + add
            return ms[0], ls[0], ms[1], ls[1], acc

        neg = jnp.full((tq, 1), NEG, F32)
        zero = jnp.zeros((tq, 1), F32)
        m0, l0, m1, l1, acc = lax.fori_loop(0, (i + 1) * per_q, step, (neg, zero, neg, zero, jnp.zeros((tq, LANES), F32)))
        o_ref[...] = acc / jnp.where(masks[0], l0, l1)
        lse_ref[...] = jnp.where(masks[0], m0 + jnp.log(l0), m1 + jnp.log(l1))

    out = pl.BlockSpec((None, tq, LANES), lambda b, p, i: (b, i, p))
    shp = jax.ShapeDtypeStruct((Bl, S, 3 * LANES), F32)
    return pl.pallas_call(
        body, name=name, grid=(Bl, 3, S // tq),
        in_specs=[pl.BlockSpec((None, tq, 2 * LANES), lambda b, p, i: (b, i, p)), pl.BlockSpec((None, S, 2 * LANES), lambda b, p, i: (b, 0, p)),
                  pl.BlockSpec((None, S, LANES), lambda b, p, i: (b, 0, vblk0 + p))],
        out_specs=[out, out], out_shape=[shp, shp],
        compiler_params=_cp("parallel", "parallel", "arbitrary"),
    )(q3, k3, kv3)


def mla_attn_bwd(q3, k3, kv3, vblk0, o3, lse3, do3, *, tq=512, tk=256, name):
    Bl, S, _ = q3.shape
    tq = min(tq, S)
    tk = min(tk, tq)
    per_q = tq // tk
    nq = S // tq
    scale = MLA_QK ** -0.5

    def body(q_ref, k_ref, v_ref, o_ref, lse_ref, do_ref, dq_ref, dk_ref, dv_ref):
        j = pl.program_id(2)

        @pl.when(j == 0)
        def _():
            dq_ref[...] = jnp.zeros_like(dq_ref)

        masks = _lane_masks()
        rr = lax.broadcasted_iota(jnp.int32, (tq, tk), 0)
        cc = lax.broadcasted_iota(jnp.int32, (tq, tk), 1)
        vb = v_ref[...]
        vh = [jnp.where(m, vb, 0.0).astype(BF16) for m in masks]
        kh = [k_ref[:, h * LANES:(h + 1) * LANES] for h in range(2)]
        i0 = lax.div(j, jnp.int32(per_q))

        def step(t, carry):
            dk0, dk1, dv = carry
            i = i0 + t
            off = pl.multiple_of(i * tq, tq)
            causal = (cc + j * tk) <= (rr + i * tq)
            do_b = do_ref[pl.ds(off, tq), :].astype(BF16)
            prod = do_b.astype(F32) * o_ref[pl.ds(off, tq), :]
            lse = lse_ref[pl.ds(off, tq), :]
            dks = [dk0, dk1]
            for h in range(2):
                qh = q_ref[pl.ds(off, tq), h * LANES:(h + 1) * LANES]
                doh = jnp.where(masks[h], do_b, jnp.zeros_like(do_b))
                delta = jnp.sum(jnp.where(masks[h], prod, 0.0), axis=1, keepdims=True)
                lse_h = lse[:, h * HEAD:h * HEAD + 1]
                s = jnp.where(causal, _dot_nt(qh, kh[h]) * scale, NEG)
                p = jnp.exp(s - lse_h)
                ds = (p * (_dot_nt(doh, vh[h]) - delta) * scale).astype(BF16)
                dq_ref[pl.ds(off, tq), h * LANES:(h + 1) * LANES] += _dot(ds, kh[h])
                dks[h] = dks[h] + _dot_tn(ds, qh)
                dv = dv + _dot_tn(p.astype(BF16), doh)
            return dks[0], dks[1], dv

        zero = jnp.zeros((tk, LANES), F32)
        dk0, dk1, dv = lax.fori_loop(0, nq - i0, step, (zero, zero, zero))
        dk_ref[:, 0:LANES] = dk0
        dk_ref[:, LANES:2 * LANES] = dk1
        dv_ref[...] = dv

    seq1 = pl.BlockSpec((None, S, LANES), lambda b, p, j: (b, 0, p))
    seq2 = pl.BlockSpec((None, S, 2 * LANES), lambda b, p, j: (b, 0, p))
    return pl.pallas_call(
        body, name=name, grid=(Bl, 3, S // tk),
        in_specs=[seq2, pl.BlockSpec((None, tk, 2 * LANES), lambda b, p, j: (b, j, p)),
                  pl.BlockSpec((None, tk, LANES), lambda b, p, j: (b, j, vblk0 + p)), seq1, seq1, seq1],
        out_specs=[seq2, pl.BlockSpec((None, tk, 2 * LANES), lambda b, p, j: (b, j, p)), pl.BlockSpec((None, tk, LANES), lambda b, p, j: (b, j, p))],
        out_shape=[jax.ShapeDtypeStruct((Bl, S, 6 * LANES), F32), jax.ShapeDtypeStruct((Bl, S, 6 * LANES), F32), jax.ShapeDtypeStruct((Bl, S, 3 * LANES), F32)],
        compiler_params=_cp("parallel", "parallel", "arbitrary"),
    )(q3, k3, kv3, o3, lse3, do3)


def _bucket_table():
    a = jnp.arange(WINDOW)[:, None]
    b = jnp.arange(2 * WINDOW)[None, :]
    dist = WINDOW + a - b
    max_exact = REL_BUCKETS // 2
    n = jnp.maximum(dist, 0)
    nf = jnp.maximum(n, 1).astype(F32)
    large = max_exact + (jnp.log(nf / max_exact) / math.log(REL_MAX_DIST / max_exact) * (REL_BUCKETS - max_exact)).astype(jnp.int32)
    large = jnp.minimum(large, REL_BUCKETS - 1)
    bucket = jnp.where(n < max_exact, n, large)
    return jnp.where((dist >= 0) & (dist < WINDOW), bucket, -1).astype(jnp.int32)


def swa_bias(rel_flat, bucket, *, name):
    def body(t_ref, b_ref, o_ref):
        bk = b_ref[...]
        for p in range(3):
            for hh in range(2):
                h = hh * 3 + p
                acc = jnp.full(bk.shape, NEG, F32)
                for b in range(REL_BUCKETS):
                    acc = jnp.where(bk == b, t_ref[b * 6 + h], acc)
                o_ref[p, hh] = acc

    return pl.pallas_call(
        body, name=name,
        in_specs=[pl.BlockSpec(memory_space=pltpu.SMEM), pl.BlockSpec(memory_space=pltpu.VMEM)],
        out_specs=pl.BlockSpec(memory_space=pltpu.VMEM),
        out_shape=jax.ShapeDtypeStruct((3, 2, WINDOW, 2 * WINDOW), F32),
    )(rel_flat, bucket)


def swa_bias_bwd(dbias, bucket, *, name):
    Bl = dbias.shape[0]

    def body(d_ref, b_ref, o_ref):
        bk = b_ref[...]
        lane = lax.broadcasted_iota(jnp.int32, (1, LANES), 1)
        rows = []
        for h in range(6):
            hh, p = divmod(h, 3)
            d = d_ref[0, p, hh]
            for bl in range(1, Bl):
                d = d + d_ref[bl, p, hh]
            row = jnp.zeros((1, LANES), F32)
            for b in range(REL_BUCKETS):
                s = jnp.sum(jnp.sum(jnp.where(bk == b, d, 0.0), axis=1, keepdims=True), axis=0, keepdims=True)
                row = row + jnp.where(lane == b, s, 0.0)
            rows.append(row)
        rows += [jnp.zeros((1, LANES), F32)] * 2
        o_ref[...] = jnp.concatenate(rows, axis=0)

    return pl.pallas_call(
        body, name=name,
        in_specs=[pl.BlockSpec(memory_space=pltpu.VMEM)] * 2, out_specs=pl.BlockSpec(memory_space=pltpu.VMEM),
        out_shape=jax.ShapeDtypeStruct((8, LANES), F32),
    )(dbias, bucket)


SWA_QBLOCKS = 4


def _swa_specs(vblk, nqb):
    rows = nqb * WINDOW
    cur = lambda blk: pl.BlockSpec((None, rows, LANES), lambda b, p, n: (b, n, blk))
    prev = lambda blk: pl.BlockSpec((None, WINDOW, LANES), lambda b, p, n: (b, jnp.maximum(n * nqb - 1, 0), blk))
    return [pl.BlockSpec((None, rows, LANES), lambda b, p, n: (b, n, p)), cur(0), prev(0), cur(vblk), prev(vblk),
            pl.BlockSpec((None, 2, WINDOW, 2 * WINDOW), lambda b, p, n: (p, 0, 0, 0)), pl.BlockSpec((None, 2, LANES), lambda b, p, n: (p, 0, 0))]


def _rows128(ref, m):
    return ref[m * WINDOW:(m + 1) * WINDOW, :]


def _swa_logits(qh, kp, kc, bias_h, first, scale):
    sp = jnp.where(first, NEG, _dot_nt(qh, kp) * scale + bias_h[:, :WINDOW])
    sc = _dot_nt(qh, kc) * scale + bias_h[:, WINDOW:]
    return sp, sc


def swa_attn_fwd(qn3, kn3, proj3, bias, sinks, *, name):
    Bl, S, _ = qn3.shape
    scale = HEAD ** -0.5
    nqb = min(SWA_QBLOCKS, S // WINDOW)

    def body(q_ref, kc_ref, kp_ref, vc_ref, vp_ref, b_ref, s_ref, o_ref, lse_ref):
        seq_start = pl.program_id(2) == 0
        masks = _lane_masks()
        for m_ in range(nqb):
            first = seq_start if m_ == 0 else False
            q = _rows128(q_ref, m_)
            kp = kp_ref[...] if m_ == 0 else _rows128(kc_ref, m_ - 1)
            vp = vp_ref[...] if m_ == 0 else _rows128(vc_ref, m_ - 1)
            kc, vc = _rows128(kc_ref, m_), _rows128(vc_ref, m_)
            o = jnp.zeros((WINDOW, LANES), F32)
            lses = []
            for h in range(2):
                qh = jnp.where(masks[h], q, jnp.zeros_like(q))
                sp, sc = _swa_logits(qh, kp, kc, b_ref[h], first, scale)
                sink = s_ref[h:h + 1, 0:1]
                m = jnp.maximum(jnp.maximum(jnp.max(sp, axis=1, keepdims=True), jnp.max(sc, axis=1, keepdims=True)), sink)
                ep, ec = jnp.exp(sp - m), jnp.exp(sc - m)
                l = jnp.sum(ep, axis=1, keepdims=True) + jnp.sum(ec, axis=1, keepdims=True) + jnp.exp(sink - m)
                inv = 1.0 / l
                o = o + _dot((ep * inv).astype(BF16), jnp.where(masks[h], vp, 0.0).astype(BF16))
                o = o + _dot((ec * inv).astype(BF16), jnp.where(masks[h], vc, 0.0).astype(BF16))
                lses.append(m + jnp.log(l))
            o_ref[m_ * WINDOW:(m_ + 1) * WINDOW, :] = o
            lse_ref[m_ * WINDOW:(m_ + 1) * WINDOW, :] = jnp.where(masks[0], lses[0], lses[1])

    out = pl.BlockSpec((None, nqb * WINDOW, LANES), lambda b, p, n: (b, n, p))
    shp = jax.ShapeDtypeStruct((Bl, S, 3 * LANES), F32)
    return pl.pallas_call(
        body, name=name, grid=(Bl, 3, S // (nqb * WINDOW)), in_specs=_swa_specs(P_SWV // LANES, nqb),
        out_specs=[out, out], out_shape=[shp, shp], compiler_params=_cp("parallel", "parallel", "arbitrary"),
    )(qn3, kn3, kn3, proj3, proj3, bias, sinks)


def swa_attn_bwd(qn3, kn3, proj3, bias, sinks, o3, lse3, do3, *, name):
    Bl, S, _ = qn3.shape
    scale = HEAD ** -0.5
    nqb = min(SWA_QBLOCKS, S // WINDOW)
    rows = nqb * WINDOW

    def body(q_ref, kc_ref, kp_ref, vc_ref, vp_ref, b_ref, s_ref, o_ref, lse_ref, do_ref,
             dq_ref, dk_ref, dv_ref, db_ref, dsk_ref):
        p_id, n = pl.program_id(1), pl.program_id(2)
        seq_start = n == 0

        @pl.when((p_id == 0) & seq_start)
        def _():
            dk_ref[...] = jnp.zeros_like(dk_ref)
            dv_ref[...] = jnp.zeros_like(dv_ref)

        @pl.when(seq_start)
        def _():
            db_ref[...] = jnp.zeros_like(db_ref)
            dsk_ref[...] = jnp.zeros_like(dsk_ref)

        masks = _lane_masks()
        zero = jnp.zeros((WINDOW, LANES), F32)
        dk_acc = [zero] * (nqb + 1)
        dv_acc = [zero] * (nqb + 1)
        db_acc = [[jnp.zeros((WINDOW, WINDOW), F32)] * 2 for _ in range(2)]
        dsk_acc = [jnp.zeros((1, 1), F32)] * 2
        for m_ in range(nqb):
            first = seq_start if m_ == 0 else False
            q = _rows128(q_ref, m_)
            kp = kp_ref[...] if m_ == 0 else _rows128(kc_ref, m_ - 1)
            vp = vp_ref[...] if m_ == 0 else _rows128(vc_ref, m_ - 1)
            kc, vc = _rows128(kc_ref, m_), _rows128(vc_ref, m_)
            do_b = _rows128(do_ref, m_).astype(BF16)
            prod = do_b.astype(F32) * _rows128(o_ref, m_)
            lse = _rows128(lse_ref, m_)
            dq = zero
            for h in range(2):
                qh = jnp.where(masks[h], q, jnp.zeros_like(q))
                doh = jnp.where(masks[h], do_b, jnp.zeros_like(do_b))
                sp, sc = _swa_logits(qh, kp, kc, b_ref[h], first, scale)
                lse_h = lse[:, h * HEAD:h * HEAD + 1]
                pp, pc = jnp.exp(sp - lse_h), jnp.exp(sc - lse_h)
                delta = jnp.sum(jnp.where(masks[h], prod, 0.0), axis=1, keepdims=True)
                dsp = pp * (_dot_nt(doh, jnp.where(masks[h], vp, 0.0).astype(BF16)) - delta)
                dsc = pc * (_dot_nt(doh, jnp.where(masks[h], vc, 0.0).astype(BF16)) - delta)
                db_acc[h] = [db_acc[h][0] + dsp, db_acc[h][1] + dsc]
                psink = jnp.exp(s_ref[h:h + 1, 0:1] - lse_h)
                dsk_acc[h] = dsk_acc[h] - jnp.sum(psink * delta, axis=0, keepdims=True)
                dspb, dscb = (dsp * scale).astype(BF16), (dsc * scale).astype(BF16)
                dq = dq + _dot(dspb, jnp.where(masks[h], kp, jnp.zeros_like(kp))) + _dot(dscb, jnp.where(masks[h], kc, jnp.zeros_like(kc)))
                dk_acc[m_] = dk_acc[m_] + _dot_tn(dspb, qh)
                dk_acc[m_ + 1] = dk_acc[m_ + 1] + _dot_tn(dscb, qh)
                dv_acc[m_] = dv_acc[m_] + _dot_tn(pp.astype(BF16), doh)
                dv_acc[m_ + 1] = dv_acc[m_ + 1] + _dot_tn(pc.astype(BF16), doh)
            dq_ref[m_ * WINDOW:(m_ + 1) * WINDOW, :] = dq
        for h in range(2):
            db_ref[h, :, 0:WINDOW] += db_acc[h][0]
            db_ref[h, :, WINDOW:2 * WINDOW] += db_acc[h][1]
            dsk_ref[h:h + 1, :] += jnp.broadcast_to(dsk_acc[h], (1, LANES))
        offp = pl.multiple_of(jnp.maximum(n * nqb - 1, 0) * WINDOW, WINDOW)
        dk_ref[pl.ds(offp, WINDOW), :] += dk_acc[0]
        dv_ref[pl.ds(offp, WINDOW), :] += dv_acc[0]
        for m_ in range(nqb):
            off = pl.multiple_of(n * rows + m_ * WINDOW, WINDOW)
            dk_ref[pl.ds(off, WINDOW), :] += dk_acc[m_ + 1]
            dv_ref[pl.ds(off, WINDOW), :] += dv_acc[m_ + 1]

    blk = pl.BlockSpec((None, rows, LANES), lambda b, p, n: (b, n, p))
    seq = pl.BlockSpec((None, S, LANES), lambda b, p, n: (b, 0, 0))
    return pl.pallas_call(
        body, name=name, grid=(Bl, 3, S // rows), in_specs=_swa_specs(P_SWV // LANES, nqb) + [blk, blk, blk],
        out_specs=[blk, seq, seq, pl.BlockSpec((None, None, 2, WINDOW, 2 * WINDOW), lambda b, p, n: (b, p, 0, 0, 0)),
                   pl.BlockSpec((None, None, 2, LANES), lambda b, p, n: (b, p, 0, 0))],
        out_shape=[jax.ShapeDtypeStruct((Bl, S, 3 * LANES), F32), jax.ShapeDtypeStruct((Bl, S, LANES), F32), jax.ShapeDtypeStruct((Bl, S, LANES), F32),
                   jax.ShapeDtypeStruct((Bl, 3, 2, WINDOW, 2 * WINDOW), F32), jax.ShapeDtypeStruct((Bl, 3, 2, LANES), F32)],
        compiler_params=_cp("arbitrary", "arbitrary", "arbitrary"),
    )(qn3, kn3, kn3, proj3, proj3, bias, sinks, o3, lse3, do3)


def _conv_rows(x, halo, w_ref, b_ref, first_blk):
    rows = lax.broadcasted_iota(jnp.int32, x.shape, 0)
    h6 = jnp.where(first_blk, 0.0, halo[6:7, :])
    h7 = jnp.where(first_blk, 0.0, halo[7:8, :])
    x1 = jnp.where(rows == 0, h7, pltpu.roll(x, 1, 0))
    x2 = jnp.where(rows == 0, h6, jnp.where(rows == 1, h7, pltpu.roll(x, 2, 0)))
    return w_ref[0:1, :] * x2 + w_ref[1:2, :] * x1 + w_ref[2:3, :] * x + b_ref[...], x1, x2


FF_BLK = D_FF // 2


def _up_perm(a):
    q = FF_BLK
    return _cat([a[..., 0:q], a[..., 2 * q:3 * q], a[..., q:2 * q], a[..., 3 * q:4 * q]])


def conv_gate_fwd(up3, cw, cb, *, tm=256, name):
    Bl, S, _ = up3.shape
    tm = min(tm, S)
    W = 2 * FF_BLK

    def body(x_ref, h_ref, w_ref, b_ref, o_ref):
        u, _, _ = _conv_rows(x_ref[...], h_ref[...], w_ref, b_ref, pl.program_id(1) == 0)
        ug, uv = u[:, :FF_BLK], u[:, FF_BLK:]
        o_ref[...] = (ug * jax.nn.sigmoid(ug) * uv).astype(BF16)

    hb = tm // 8
    return pl.pallas_call(
        body, name=name, grid=(Bl, S // tm, 2),
        in_specs=[pl.BlockSpec((None, tm, W), lambda b, s, c: (b, s, c)),
                  pl.BlockSpec((None, 8, W), lambda b, s, c: (b, jnp.maximum(s * hb - 1, 0), c)),
                  pl.BlockSpec((3, W), lambda b, s, c: (0, c)), pl.BlockSpec((1, W), lambda b, s, c: (0, c))],
        out_specs=pl.BlockSpec((None, tm, FF_BLK), lambda b, s, c: (b, s, c)),
        out_shape=jax.ShapeDtypeStruct((Bl, S, D_FF), BF16),
        compiler_params=_cp("parallel", "parallel", "parallel"),
    )(up3, up3, cw, cb)


def conv_gate_bwd(up3, cw, cb, da3, *, tm=256, name):
    Bl, S, _ = up3.shape
    tm = min(tm, S)
    ns = S // tm
    W = 2 * FF_BLK

    def body(x_ref, h_ref, w_ref, b_ref, da_ref, dup_ref, dw_ref, nxt_ref):
        b, s = pl.program_id(1), pl.program_id(2)
        seq_end = s == 0

        @pl.when((b == 0) & seq_end)
        def _():
            dw_ref[...] = jnp.zeros_like(dw_ref)

        x = x_ref[...]
        u, x1, x2 = _conv_rows(x, h_ref[...], w_ref, b_ref, s == ns - 1)
        ug, uv = u[:, :FF_BLK], u[:, FF_BLK:]
        da = da_ref[...].astype(F32)
        sg = jax.nn.sigmoid(ug)
        du = _cat([da * uv * sg * (1.0 + ug * (1.0 - sg)), da * ug * sg])
        dw_ref[0:1, :] += jnp.sum(du * x2, axis=0, keepdims=True)
        dw_ref[1:2, :] += jnp.sum(du * x1, axis=0, keepdims=True)
        dw_ref[2:3, :] += jnp.sum(du * x, axis=0, keepdims=True)
        dw_ref[3:4, :] += jnp.sum(du, axis=0, keepdims=True)
        rows = lax.broadcasted_iota(jnp.int32, du.shape, 0)
        n0 = jnp.where(seq_end, 0.0, nxt_ref[0:1, :])
        n1 = jnp.where(seq_end, 0.0, nxt_ref[1:2, :])
        d1 = jnp.where(rows == tm - 1, n0, pltpu.roll(du, tm - 1, 0))
        d2 = jnp.where(rows == tm - 1, n1, jnp.where(rows == tm - 2, n0, pltpu.roll(du, tm - 2, 0)))
        dup_ref[...] = (w_ref[2:3, :] * du + w_ref[1:2, :] * d1 + w_ref[0:1, :] * d2).astype(BF16)
        nxt_ref[...] = du[0:8, :]

    hb = tm // 8
    rb = lambda s: ns - 1 - s
    return pl.pallas_call(
        body, name=name, grid=(2, Bl, ns),
        in_specs=[pl.BlockSpec((None, tm, W), lambda c, b, s: (b, rb(s), c)),
                  pl.BlockSpec((None, 8, W), lambda c, b, s: (b, jnp.maximum(rb(s) * hb - 1, 0), c)),
                  pl.BlockSpec((3, W), lambda c, b, s: (0, c)), pl.BlockSpec((1, W), lambda c, b, s: (0, c)),
                  pl.BlockSpec((None, tm, FF_BLK), lambda c, b, s: (b, rb(s), c))],
        out_specs=[pl.BlockSpec((None, tm, W), lambda c, b, s: (b, rb(s), c)), pl.BlockSpec((8, W), lambda c, b, s: (0, c))],
        out_shape=[jax.ShapeDtypeStruct((Bl, S, 2 * D_FF), BF16), jax.ShapeDtypeStruct((8, 2 * D_FF), F32)],
        scratch_shapes=[pltpu.VMEM((8, W), F32)],
        compiler_params=_cp("arbitrary", "arbitrary", "arbitrary"),
    )(up3, up3, cw, cb, da3)


def gate_bwd(dx3, y3, gate, *, tm=512, name):
    Bl, S, D = dx3.shape
    tm = min(tm, S)

    def body(dx_ref, y_ref, g_ref, o_ref, dg_ref):
        @pl.when(pl.program_id(1) == 0)
        def _():
            dg_ref[...] = jnp.zeros_like(dg_ref)

        dx = dx_ref[...]
        dg_ref[...] += jnp.sum(dx * y_ref[...], axis=0, keepdims=True)
        o_ref[...] = (dx * g_ref[...]).astype(BF16)

    blk = pl.BlockSpec((None, tm, D), lambda b, s: (b, s, 0))
    vec = pl.BlockSpec((None, 1, D), lambda b, s: (b, 0, 0))
    return pl.pallas_call(
        body, name=name, grid=(Bl, S // tm), in_specs=[blk, blk, vec], out_specs=[blk, vec],
        out_shape=[jax.ShapeDtypeStruct((Bl, S, D), BF16), jax.ShapeDtypeStruct((Bl, 1, D), F32)],
        compiler_params=_cp("parallel", "arbitrary"),
    )(dx3, y3, gate)


def loss_grad(y3, t3, *, tm=512, name):
    Bl, S, D = y3.shape
    tm = min(tm, S)
    last = (Bl - 1, S // tm - 1)

    def body(y_ref, t_ref, dy_ref, l_ref, acc_ref):
        b, s = pl.program_id(0), pl.program_id(1)

        @pl.when((b == 0) & (s == 0))
        def _():
            acc_ref[...] = jnp.zeros_like(acc_ref)

        e = y_ref[...] - t_ref[...]
        dy_ref[...] = e * (1.0 / D)
        acc_ref[...] += jnp.sum(e * e, axis=0, keepdims=True)

        @pl.when((b == last[0]) & (s == last[1]))
        def _():
            l_ref[...] = jnp.broadcast_to(jnp.sum(acc_ref[...], axis=1, keepdims=True) * (0.5 / D), (1, LANES))

    blk = pl.BlockSpec((None, tm, D), lambda b, s: (b, s, 0))
    return pl.pallas_call(
        body, name=name, grid=(Bl, S // tm), in_specs=[blk, blk],
        out_specs=[blk, pl.BlockSpec((1, LANES), lambda b, s: (0, 0))],
        out_shape=[jax.ShapeDtypeStruct((Bl, S, D), F32), jax.ShapeDtypeStruct((1, LANES), F32)],
        scratch_shapes=[pltpu.VMEM((1, D), F32)], compiler_params=_cp("arbitrary", "arbitrary"),
    )(y3, t3)


def adamw(w, g, m, v, *, name):
    R, C = w.shape
    tr = R
    for cand in (512, 256, 128, 64, 32, 16, 8):
        if R > cand and R % cand == 0:
            tr = cand
            break
    c1 = 1.0 / (1.0 - ADAM_B1 ** ADAM_STEP)
    c2 = 1.0 / (1.0 - ADAM_B2 ** ADAM_STEP)

    def body(w_ref, g_ref, m_ref, v_ref, d_ref, m2_ref, v2_ref):
        gg = g_ref[...]
        m2 = ADAM_B1 * m_ref[...] + (1.0 - ADAM_B1) * gg
        v2 = ADAM_B2 * v_ref[...] + (1.0 - ADAM_B2) * (gg * gg)
        m2_ref[...] = m2
        v2_ref[...] = v2
        d_ref[...] = -ADAM_LR * ((m2 * c1) / (jnp.sqrt(v2 * c2) + ADAM_EPS) + ADAM_WD * w_ref[...])

    blk = pl.BlockSpec((tr, C), lambda i: (i, 0))
    shp = jax.ShapeDtypeStruct((R, C), F32)
    return pl.pallas_call(
        body, name=name, grid=(R // tr,), in_specs=[blk] * 4, out_specs=[blk] * 3, out_shape=[shp] * 3,
        compiler_params=_cp("parallel"),
    )(w, g, m, v)


def sum_leading(x, *, out_dtype=F32, tr=256, name):
    n, R, C = x.shape
    tr = _tile(R, tr, 16)

    def body(x_ref, o_ref):
        acc = x_ref[0].astype(F32)
        for k in range(1, n):
            acc = acc + x_ref[k].astype(F32)
        o_ref[...] = acc.astype(out_dtype)

    return pl.pallas_call(
        body, name=name, grid=(R // tr,), in_specs=[pl.BlockSpec((n, tr, C), lambda i: (0, i, 0))],
        out_specs=pl.BlockSpec((tr, C), lambda i: (i, 0)), out_shape=jax.ShapeDtypeStruct((R, C), out_dtype),
        compiler_params=_cp("parallel"),
    )(x)


def _adam_update(w, g, m, v):
    c1 = 1.0 / (1.0 - ADAM_B1 ** ADAM_STEP)
    c2 = 1.0 / (1.0 - ADAM_B2 ** ADAM_STEP)
    m2 = ADAM_B1 * m + (1.0 - ADAM_B1) * g
    v2 = ADAM_B2 * v + (1.0 - ADAM_B2) * (g * g)
    return -ADAM_LR * ((m2 * c1) / (jnp.sqrt(v2 * c2) + ADAM_EPS) + ADAM_WD * w), m2, v2


def adamw_small(ws, gs, ms, vs, *, name):
    na = len(ws)

    def body(*refs):
        w_r, g_r, m_r, v_r = (refs[i * na:(i + 1) * na] for i in range(4))
        d_r, m2_r, v2_r = (refs[(4 + i) * na:(5 + i) * na] for i in range(3))
        for a in range(na):
            d_r[a][...], m2_r[a][...], v2_r[a][...] = _adam_update(w_r[a][...], g_r[a][...], m_r[a][...], v_r[a][...])

    vm = pl.BlockSpec(memory_space=pltpu.VMEM)
    shp = [jax.ShapeDtypeStruct(w.shape, F32) for w in ws]
    out = pl.pallas_call(body, name=name, in_specs=[vm] * (4 * na), out_specs=[vm] * (3 * na), out_shape=shp * 3)(*ws, *gs, *ms, *vs)
    return out[:na], out[na:2 * na], out[2 * na:]


def sum_small(xs, *, name):
    na = len(xs)

    def body(*refs):
        for x_ref, o_ref in zip(refs[:na], refs[na:]):
            acc = x_ref[0]
            for k in range(1, x_ref.shape[0]):
                acc = acc + x_ref[k]
            o_ref[...] = acc

    vm = pl.BlockSpec(memory_space=pltpu.VMEM)
    return pl.pallas_call(body, name=name, in_specs=[vm] * na, out_specs=[vm] * na,
                          out_shape=[jax.ShapeDtypeStruct(x.shape[1:], x.dtype) for x in xs])(*xs)


def pair_add_half(g4, recv, c_arr, *, tr=128, name):
    _, R, C = g4.shape
    H = R // 2
    tr = _tile(H, tr, 16)
    nb = H // tr

    def body(c_ref, g_ref, r_ref, o_ref):
        o_ref[...] = (g_ref[...] + r_ref[...]).astype(BF16)

    grid_spec = pltpu.PrefetchScalarGridSpec(
        num_scalar_prefetch=1, grid=(4, nb),
        in_specs=[pl.BlockSpec((None, tr, C), lambda k, i, c_ref: (k, c_ref[0] * nb + i, 0)),
                  pl.BlockSpec((None, tr, C), lambda k, i, c_ref: (k, i, 0))],
        out_specs=pl.BlockSpec((None, tr, C), lambda k, i, c_ref: (k, i, 0)),
    )
    return pl.pallas_call(
        body, name=name, grid_spec=grid_spec, out_shape=jax.ShapeDtypeStruct((4, H, C), BF16),
        compiler_params=_cp("parallel", "parallel"),
    )(c_arr, g4, recv)


def mods_matmul(c_all, w_ada, b_ada_cols, *, tn=512, name):
    L, D, E = w_ada.shape
    nb = c_all.shape[0]
    tn = _tile(E, tn)

    def body(c_ref, w_ref, b_ref, o_ref):
        c = c_ref[...]
        a = c * jax.nn.sigmoid(c)
        o_ref[...] = jnp.dot(a, w_ref[...], preferred_element_type=F32, precision=lax.Precision.HIGHEST) + b_ref[...]

    return pl.pallas_call(
        body, name=name, grid=(L, E // tn),
        in_specs=[pl.BlockSpec((nb, D), lambda l, j: (0, 0)), pl.BlockSpec((None, D, tn), lambda l, j: (l, 0, j)),
                  pl.BlockSpec((None, 1, tn), lambda l, j: (l, 0, j))],
        out_specs=pl.BlockSpec((None, nb, tn), lambda l, j: (l, 0, j)),
        out_shape=jax.ShapeDtypeStruct((L, nb, E), F32), compiler_params=_cp("parallel", "parallel"),
    )(c_all, w_ada, b_ada_cols)


def ada_grad(c_all, dmods, *, tn=512, name):
    L, nb, E = dmods.shape
    D = c_all.shape[1]
    tn = _tile(E, tn)

    def body(c_ref, d_ref, o_ref):
        c = c_ref[...]
        a = c * jax.nn.sigmoid(c)
        o_ref[...] = lax.dot_general(a, d_ref[...], (((0,), (0,)), ((), ())), preferred_element_type=F32, precision=lax.Precision.HIGHEST)

    return pl.pallas_call(
        body, name=name, grid=(L, E // tn),
        in_specs=[pl.BlockSpec((nb, D), lambda l, j: (0, 0)), pl.BlockSpec((None, nb, tn), lambda l, j: (l, 0, j))],
        out_specs=pl.BlockSpec((None, D, tn), lambda l, j: (l, 0, j)),
        out_shape=jax.ShapeDtypeStruct((L, D, E), F32), compiler_params=_cp("parallel", "parallel"),
    )(c_all, dmods)


HBM = pl.BlockSpec(memory_space=pltpu.HBM)


def _me():
    return lax.axis_index("x"), lax.axis_index("y"), lax.axis_index("c")


def _flip(v, bit):
    return 1 - v if bit else v


def allgather8(xs, *, name):
    na = len(xs)

    def body(*refs):
        x_refs, out_refs = refs[:na], refs[na:2 * na]
        send_sems, recv_sems = refs[2 * na], refs[2 * na + 1]
        x, y, c = _me()
        me = 4 * x + 2 * y + c
        for x_ref, out_ref in zip(x_refs, out_refs):
            out_ref[me] = x_ref[...]
        sends = []
        for a, (x_ref, out_ref) in enumerate(zip(x_refs, out_refs)):
            for k in range(1, 8):
                peer = (_flip(x, k & 4), _flip(y, k & 2), _flip(c, k & 1))
                cp = pltpu.make_async_remote_copy(src_ref=x_ref, dst_ref=out_ref.at[me], send_sem=send_sems.at[a, k - 1],
                                                  recv_sem=recv_sems.at[a, k - 1], device_id=peer, device_id_type=MESH)
                cp.start()
                sends.append(cp)
        for a, (x_ref, out_ref) in enumerate(zip(x_refs, out_refs)):
            for k in range(1, 8):
                peer = (_flip(x, k & 4), _flip(y, k & 2), _flip(c, k & 1))
                src = 4 * peer[0] + 2 * peer[1] + peer[2]
                pltpu.make_async_remote_copy(src_ref=x_ref, dst_ref=out_ref.at[src], send_sem=send_sems.at[a, k - 1],
                                             recv_sem=recv_sems.at[a, k - 1], device_id=peer, device_id_type=MESH).wait_recv()
        for cp in sends:
            cp.wait_send()

    vm = pl.BlockSpec(memory_space=pltpu.VMEM)
    return pl.pallas_call(
        body, name=name, in_specs=[vm] * na, out_specs=[vm] * na,
        out_shape=[jax.ShapeDtypeStruct((8,) + a.shape, a.dtype) for a in xs],
        scratch_shapes=[pltpu.SemaphoreType.DMA((na, 7)), pltpu.SemaphoreType.DMA((na, 7))],
    )(*xs)


LOCAL_CHUNKS = 8


def _copy_via_vmem(src, dst_at, rows, buf, sem):
    ch = buf.shape[0]
    for i in range(rows // ch):
        load = pltpu.make_async_copy(src.at[pl.ds(i * ch, ch)], buf, sem)
        load.start()
        load.wait()
        store = pltpu.make_async_copy(buf, dst_at(i * ch, ch), sem)
        store.start()
        store.wait()


def _chunk_buf(rows, cols, dtype):
    align = 16 if dtype == BF16 else 8
    for n in range(LOCAL_CHUNKS, 0, -1):
        if rows % n == 0 and (rows // n) % align == 0:
            return pltpu.VMEM((rows // n, cols), dtype)
    return pltpu.VMEM((rows, cols), dtype)


def gather_weights(ws, *, name):
    na = len(ws)

    def body(*refs):
        x_refs, out_refs = refs[:na], refs[na:2 * na]
        send_sems, recv_sems, local_sem = refs[2 * na:2 * na + 3]
        bufs = refs[2 * na + 3:]
        x, y, c = _me()
        j = 2 * x + y
        chips = [(_flip(x, k & 2), _flip(y, k & 1)) for k in range(1, 4)]
        sends = []
        for a, (x_ref, out_ref) in enumerate(zip(x_refs, out_refs)):
            H = x_ref.shape[0] // 2
            for k, (px, py) in enumerate(chips):
                cp = pltpu.make_async_remote_copy(src_ref=x_ref.at[pl.ds(c * H, H)], dst_ref=out_ref.at[j, pl.ds(c * H, H)],
                                                  send_sem=send_sems.at[a, k], recv_sem=recv_sems.at[a, k],
                                                  device_id=(px, py, c), device_id_type=MESH)
                cp.start()
                sends.append(cp)
        for x_ref, out_ref, buf in zip(x_refs, out_refs, bufs):
            _copy_via_vmem(x_ref, lambda o, n, out_ref=out_ref: out_ref.at[j, pl.ds(o, n)], x_ref.shape[0], buf, local_sem)
        for a, out_ref in enumerate(out_refs):
            H = out_ref.shape[1] // 2
            for k, (px, py) in enumerate(chips):
                slot = out_ref.at[2 * px + py, pl.ds(c * H, H)]
                pltpu.make_async_remote_copy(src_ref=slot, dst_ref=slot, send_sem=send_sems.at[a, k], recv_sem=recv_sems.at[a, k],
                                             device_id=(px, py, c), device_id_type=MESH).wait_recv()
                cp = pltpu.make_async_remote_copy(src_ref=slot, dst_ref=slot, send_sem=send_sems.at[a, 3 + k],
                                                  recv_sem=recv_sems.at[a, 3 + k], device_id=(x, y, 1 - c), device_id_type=MESH)
                cp.start()
                sends.append(cp)
        for a, out_ref in enumerate(out_refs):
            H = out_ref.shape[1] // 2
            for k, (px, py) in enumerate(chips):
                slot = out_ref.at[2 * px + py, pl.ds((1 - c) * H, H)]
                pltpu.make_async_remote_copy(src_ref=slot, dst_ref=slot, send_sem=send_sems.at[a, 3 + k], recv_sem=recv_sems.at[a, 3 + k],
                                             device_id=(x, y, 1 - c), device_id_type=MESH).wait_recv()
        for cp in sends:
            cp.wait_send()

    return pl.pallas_call(
        body, name=name, in_specs=[HBM] * na, out_specs=[HBM] * na,
        out_shape=[jax.ShapeDtypeStruct((4,) + w.shape, w.dtype) for w in ws],
        scratch_shapes=[pltpu.SemaphoreType.DMA((na, 6)), pltpu.SemaphoreType.DMA((na, 6)), pltpu.SemaphoreType.DMA]
        + [_chunk_buf(w.shape[0], w.shape[1], w.dtype) for w in ws],
    )(*ws)


def swap_halves(gs, *, name):
    na = len(gs)

    def body(*refs):
        g_refs, out_refs = refs[:na], refs[na:2 * na]
        send_sems, recv_sems = refs[2 * na:]
        x, y, c = _me()
        sib = (x, y, 1 - c)
        sends = []
        for a, (g_ref, out_ref) in enumerate(zip(g_refs, out_refs)):
            H = g_ref.shape[1] // 2
            for k in range(4):
                cp = pltpu.make_async_remote_copy(src_ref=g_ref.at[k, pl.ds((1 - c) * H, H)], dst_ref=out_ref.at[k],
                                                  send_sem=send_sems.at[a, k], recv_sem=recv_sems.at[a, k], device_id=sib, device_id_type=MESH)
                cp.start()
                sends.append(cp)
        for a, (g_ref, out_ref) in enumerate(zip(g_refs, out_refs)):
            H = g_ref.shape[1] // 2
            for k in range(4):
                pltpu.make_async_remote_copy(src_ref=g_ref.at[k, pl.ds(c * H, H)], dst_ref=out_ref.at[k], send_sem=send_sems.at[a, k],
                                             recv_sem=recv_sems.at[a, k], device_id=sib, device_id_type=MESH).wait_recv()
        for cp in sends:
            cp.wait_send()

    return pl.pallas_call(
        body, name=name, in_specs=[HBM] * na, out_specs=[HBM] * na,
        out_shape=[jax.ShapeDtypeStruct((4, g.shape[1] // 2, g.shape[2]), g.dtype) for g in gs],
        scratch_shapes=[pltpu.SemaphoreType.DMA((na, 4)), pltpu.SemaphoreType.DMA((na, 4))],
    )(*gs)


def scatter_chips(ps, *, name):
    na = len(ps)

    def body(*refs):
        p_refs, out_refs = refs[:na], refs[na:2 * na]
        send_sems, recv_sems, local_sem = refs[2 * na:2 * na + 3]
        bufs = refs[2 * na + 3:]
        x, y, c = _me()
        j = 2 * x + y
        chips = [(_flip(x, k & 2), _flip(y, k & 1)) for k in range(1, 4)]
        sends = []
        for a, (p_ref, out_ref) in enumerate(zip(p_refs, out_refs)):
            for k, (px, py) in enumerate(chips):
                cp = pltpu.make_async_remote_copy(src_ref=p_ref.at[2 * px + py], dst_ref=out_ref.at[j], send_sem=send_sems.at[a, k],
                                                  recv_sem=recv_sems.at[a, k], device_id=(px, py, c), device_id_type=MESH)
                cp.start()
                sends.append(cp)
        for p_ref, out_ref, buf in zip(p_refs, out_refs, bufs):
            _copy_via_vmem(p_ref.at[j], lambda o, n, out_ref=out_ref: out_ref.at[j, pl.ds(o, n)], p_ref.shape[1], buf, local_sem)
        for a, out_ref in enumerate(out_refs):
            for k, (px, py) in enumerate(chips):
                slot = out_ref.at[2 * px + py]
                pltpu.make_async_remote_copy(src_ref=slot, dst_ref=slot, send_sem=send_sems.at[a, k], recv_sem=recv_sems.at[a, k],
                                             device_id=(px, py, c), device_id_type=MESH).wait_recv()
        for cp in sends:
            cp.wait_send()

    return pl.pallas_call(
        body, name=name, in_specs=[HBM] * na, out_specs=[HBM] * na, out_shape=[jax.ShapeDtypeStruct(p.shape, p.dtype) for p in ps],
        scratch_shapes=[pltpu.SemaphoreType.DMA((na, 3)), pltpu.SemaphoreType.DMA((na, 3)), pltpu.SemaphoreType.DMA]
        + [_chunk_buf(p.shape[1], p.shape[2], p.dtype) for p in ps],
    )(*ps)


def join_halves(halves, *, name):
    na = len(halves)

    def body(*refs):
        h_refs, out_refs = refs[:na], refs[na:2 * na]
        send_sems, recv_sems, local_sem = refs[2 * na:2 * na + 3]
        bufs = refs[2 * na + 3:]
        x, y, c = _me()
        sib = (x, y, 1 - c)
        sends = []
        for a, (h_ref, out_ref) in enumerate(zip(h_refs, out_refs)):
            H = h_ref.shape[0]
            cp = pltpu.make_async_remote_copy(src_ref=h_ref, dst_ref=out_ref.at[pl.ds(c * H, H)], send_sem=send_sems.at[a],
                                              recv_sem=recv_sems.at[a], device_id=sib, device_id_type=MESH)
            cp.start()
            sends.append(cp)
        for h_ref, out_ref, buf in zip(h_refs, out_refs, bufs):
            H = h_ref.shape[0]
            _copy_via_vmem(h_ref, lambda o, n, out_ref=out_ref, H=H: out_ref.at[pl.ds(c * H + o, n)], H, buf, local_sem)
        for a, (h_ref, out_ref) in enumerate(zip(h_refs, out_refs)):
            H = h_ref.shape[0]
            pltpu.make_async_remote_copy(src_ref=h_ref, dst_ref=out_ref.at[pl.ds((1 - c) * H, H)], send_sem=send_sems.at[a],
                                         recv_sem=recv_sems.at[a], device_id=sib, device_id_type=MESH).wait_recv()
        for cp in sends:
            cp.wait_send()

    return pl.pallas_call(
        body, name=name, in_specs=[HBM] * na, out_specs=[HBM] * na,
        out_shape=[jax.ShapeDtypeStruct((2 * h.shape[0], h.shape[1]), h.dtype) for h in halves],
        scratch_shapes=[pltpu.SemaphoreType.DMA((na,)), pltpu.SemaphoreType.DMA((na,)), pltpu.SemaphoreType.DMA]
        + [_chunk_buf(h.shape[0], h.shape[1], h.dtype) for h in halves],
    )(*halves)


def _cat(parts, axis=-1):
    return jnp.concatenate(parts, axis=axis)


def _prep_w_in(w):
    z = lambda n: jnp.zeros((w.shape[0], n), w.dtype)
    swq = w[:, 1184:1568]
    return _cat([w[:, 0:1152], z(64), w[:, 1152:1184], z(32)] + [swq[:, HEAD * h:HEAD * (h + 1)] for h in SW_PERM] + [w[:, 1568:1824]])


def _unprep_w_in(g):
    swq = g[:, P_SWQ:P_SWK]
    return _cat([g[:, 0:1152], g[:, 1216:1248]] + [swq[:, HEAD * SW_PERM.index(h):HEAD * (SW_PERM.index(h) + 1)] for h in range(6)] + [g[:, P_SWK:P_END]])


def _prep_w_uq(w):
    z = jnp.zeros((w.shape[0], 32), w.dtype)
    return _cat([p for h in range(6) for p in (w[:, MLA_QK * h:MLA_QK * (h + 1)], z)])


def _unprep_w_uq(g):
    return _cat([g[:, LANES * h:LANES * h + MLA_QK] for h in range(6)])


def _prep_w_ukv(w):
    z = jnp.zeros((w.shape[0], HEAD), w.dtype)
    return _cat([p for h in range(6) for p in (w[:, LANES * h:LANES * h + HEAD], z)] + [w[:, LANES * h + HEAD:LANES * (h + 1)] for h in range(6)])


def _unprep_w_ukv(g):
    return _cat([p for h in range(6) for p in (g[:, LANES * h:LANES * h + HEAD], g[:, 768 + HEAD * h:768 + HEAD * (h + 1)])])


def _prep_w_out(w):
    return _cat([w[0:640]] + [w[640 + HEAD * h:640 + HEAD * (h + 1)] for h in SW_PERM], axis=0)


def _unprep_w_out(g):
    return _cat([g[0:640]] + [g[640 + HEAD * SW_PERM.index(h):640 + HEAD * (SW_PERM.index(h) + 1)] for h in range(6)], axis=0)


def _rope_tables(positions):
    half = 16
    inv_freq = jnp.power(ROPE_THETA, -jnp.arange(half, dtype=F32) / half)
    ang = positions.astype(F32)[..., None] * inv_freq
    cos, sin = jnp.cos(ang), jnp.sin(ang)
    z = lambda n: jnp.zeros(ang.shape[:-1] + (n,), F32)
    return (_cat([jnp.ones(ang.shape[:-1] + (HEAD,), F32), cos, cos, z(32)]), _cat([z(HEAD), -sin, z(16), z(32)]), _cat([z(HEAD), z(16), sin, z(32)]))


def _small_params(p):
    pad96 = lambda g: _cat([g, jnp.zeros((32,), F32)]).reshape(1, LANES)
    two = lambda g: _cat([g, g]).reshape(1, LANES)
    sinks = jnp.broadcast_to(p["sw_sinks"].reshape(2, 3).T[:, :, None], (3, 2, LANES))
    return dict(n1=p["norm1_g"].reshape(1, -1), n2=p["norm2_g"].reshape(1, -1), cq_g=p["mla_cq_g"].reshape(1, -1),
                ckv_g=p["mla_ckv_g"].reshape(1, -1), qn_g=pad96(p["mla_qn_g"]), kn_g=pad96(p["mla_kn_g"]),
                swq_g=two(p["sw_qn_g"]), swk_g=two(p["sw_kn_g"]), sinks=sinks, conv_b=_up_perm(p["conv_b"]).reshape(1, -1))


def _layer_fwd(x3, md, W, tabs, bias, tag):
    Bl, S, D = x3.shape
    T = Bl * S
    n = lambda s: f"{s}_{tag}"
    two = lambda a: a.reshape(T, a.shape[-1])
    three = lambda a: a.reshape(Bl, S, a.shape[-1])
    h = rms_fwd(x3, 0, D, W["n1"], md["scale1"], md["shift1"], name=n("norm1"))
    proj = three(matmul(two(h), W["w_in"], tn=1920, name=n("in_proj")))
    o_a, rt_a = sb_attn_fwd(proj, name=n("sb_fwd"))
    cqn = rms_fwd(proj, P_CQ // 256, 256, W["cq_g"], name=n("cq_norm"))
    ckvn = rms_fwd(proj, P_CKV // LANES, LANES, W["ckv_g"], name=n("ckv_norm"))
    qb = three(matmul(two(cqn), W["w_uq"], tm=1024, tn=768, name=n("uq")))
    kvb = three(matmul(two(ckvn), W["w_ukv"], tm=1024, tn=1152, name=n("ukv")))
    q_m = rope_norm_fwd(qb, 6, W["qn_g"], tabs, name=n("q_rope"))
    k_m = rope_norm_fwd(kvb, 6, W["kn_g"], tabs, (proj, P_SLAB // LANES), name=n("k_rope"))
    o_b, lse_b = mla_attn_fwd(q_m, k_m, kvb, 6, name=n("mla_fwd"))
    q_c = pair_rms_fwd(proj, P_SWQ // LANES, 3, W["swq_g"], name=n("swq_norm"))
    k_c = pair_rms_fwd(proj, P_SWK // LANES, 1, W["swk_g"], name=n("swk_norm"))
    o_c, lse_c = swa_attn_fwd(q_c, k_c, proj, bias, W["sinks"], name=n("swa_fwd"))
    mix = _cat([o_a, o_b, o_c]).astype(BF16)
    att, x1 = matmul_res(two(mix), W["w_out"], two(x3), md["gate1"], S, name=n("out_proj"))
    x1 = three(x1)
    h2 = rms_fwd(x1, 0, D, W["n2"], md["scale2"], md["shift2"], name=n("norm2"))
    up = three(matmul(two(h2), W["w_up"], tn=1408, name=n("up_proj")))
    a = conv_gate_fwd(up, W["conv_w"], W["conv_b"], name=n("conv_gate"))
    yd, x2 = matmul_res(two(a), W["w_down"], two(x1), md["gate2"], S, name=n("down_proj"))
    saved = dict(x=x3, h=h, proj=proj, rt_a=rt_a, cqn=cqn, ckvn=ckvn, qb=qb, kvb=kvb, q_m=q_m, k_m=k_m, o_b=o_b, lse_b=lse_b,
                 q_c=q_c, k_c=k_c, o_c=o_c, lse_c=lse_c, mix=mix, att=three(att), x1=x1, h2=h2, up=up, a=a, yd=three(yd))
    return three(x2), saved


def _layer_bwd(dx2, sv, md, W, tabs, bias, tag):
    Bl, S, D = dx2.shape
    T = Bl * S
    n = lambda s: f"{s}_{tag}"
    two = lambda a: a.reshape(T, a.shape[-1])
    three = lambda a: a.reshape(Bl, S, a.shape[-1])
    g = {}
    dyb, dgate2 = gate_bwd(dx2, sv["yd"], md["gate2"], name=n("gate2_bwd"))
    da = three(matmul(two(dyb), W["w_down"], tb=True, tn=1408, name=n("down_dx")))
    g["w_down"] = matmul(two(sv["a"]), two(dyb), ta=True, tm=256, tn=1024, name=n("down_dw"))
    dup, dcw = conv_gate_bwd(sv["up"], W["conv_w"], W["conv_b"], da, name=n("conv_gate_bwd"))
    dh2 = three(matmul(two(dup), W["w_up"], tb=True, tn=1024, name=n("up_dx")))
    g["w_up"] = matmul(two(sv["h2"]), two(dup), ta=True, tn=1408, name=n("up_dw"))
    dx1, dn2, dsc2, dsh2 = rms_bwd(sv["x1"], 0, D, dh2, W["n2"], md["scale2"], dx2, name=n("norm2_bwd"))
    dmo, dgate1 = gate_bwd(dx1, sv["att"], md["gate1"], name=n("gate1_bwd"))
    dmix = three(matmul(two(dmo), W["w_out"], tb=True, tn=1024, out_dtype=BF16, name=n("out_dx")))
    g["w_out"] = matmul(two(sv["mix"]), two(dmo), ta=True, tn=1024, name=n("out_dw"))
    proj = sv["proj"]
    dq_a, dk_a, dv_a = sb_attn_bwd(proj, sv["rt_a"], dmix[:, :, 0:256], name=n("sb_bwd"))
    dq_m, dk_m, dv_b = mla_attn_bwd(sv["q_m"], sv["k_m"], sv["kvb"], 6, sv["o_b"], sv["lse_b"], dmix[:, :, 256:640], name=n("mla_bwd"))
    dqb, dqn = rope_norm_bwd(sv["qb"], 6, dq_m, W["qn_g"], tabs, name=n("q_rope_bwd"))
    dkn_x, dkn, dslab = rope_norm_bwd(sv["kvb"], 6, dk_m, W["kn_g"], tabs, (proj, P_SLAB // LANES), name=n("k_rope_bwd"))
    dkvb = _cat([dkn_x, dv_b]).astype(BF16)
    dckvn = three(matmul(two(dkvb), W["w_ukv"], tb=True, tm=1024, name=n("ukv_dx")))
    g["w_ukv"] = matmul(two(sv["ckvn"]), two(dkvb), ta=True, tn=1152, name=n("ukv_dw"))
    dcqn = three(matmul(two(dqb), W["w_uq"], tb=True, tm=1024, name=n("uq_dx")))
    g["w_uq"] = matmul(two(sv["cqn"]), two(dqb), ta=True, tn=768, name=n("uq_dw"))
    dcq, dcq_g = rms_bwd(proj, P_CQ // 256, 256, dcqn, W["cq_g"], name=n("cq_norm_bwd"))
    dckv, dckv_g = rms_bwd(proj, P_CKV // LANES, LANES, dckvn, W["ckv_g"], name=n("ckv_norm_bwd"))
    dq_c, dk_c, dv_c, dbias, dsink = swa_attn_bwd(sv["q_c"], sv["k_c"], proj, bias, W["sinks"], sv["o_c"], sv["lse_c"], dmix[:, :, 640:1024], name=n("swa_bwd"))
    dswq, dswq_g = pair_rms_bwd(proj, P_SWQ // LANES, 3, dq_c, W["swq_g"], name=n("swq_norm_bwd"))
    dswk, dswk_g = pair_rms_bwd(proj, P_SWK // LANES, 1, dk_c, W["swk_g"], name=n("swk_norm_bwd"))
    dproj = _cat([dq_a, dk_a, dv_a, dcq, dckv, dslab, dswq, dswk, dv_c]).astype(BF16)
    dh = three(matmul(two(dproj), W["w_in"], tb=True, tn=1024, name=n("in_dx")))
    g["w_in"] = matmul(two(sv["h"]), two(dproj), ta=True, tn=1920, tk=2048, name=n("in_dw"))
    dx, dn1, dsc1, dsh1 = rms_bwd(sv["x"], 0, D, dh, W["n1"], md["scale1"], dx1, name=n("norm1_bwd"))
    small = dict(n1=dn1, n2=dn2, cq_g=dcq_g, ckv_g=dckv_g, qn_g=dqn, kn_g=dkn, swq_g=dswq_g, swk_g=dswk_g, conv=dcw)
    dmods = _cat([dsh1, dsc1, dgate1, dsh2, dsc2, dgate2]).reshape(Bl, 6 * D)
    return dx, g, small, dmods, dbias, dsink


BIG = ("w_in", "w_uq", "w_ukv", "w_out", "w_up", "w_down")
ROW_SHARDED = ("w_out", "w_down")
PREP = dict(w_in=_prep_w_in, w_uq=_prep_w_uq, w_ukv=_prep_w_ukv, w_out=_prep_w_out, w_up=_up_perm, w_down=lambda w: w)
UNPREP = dict(w_in=_unprep_w_in, w_uq=_unprep_w_uq, w_ukv=_unprep_w_ukv, w_out=_unprep_w_out, w_up=_up_perm, w_down=lambda w: w)
NCHIPS = 4


def _local_step(x, target, positions, mods, Wl, rel_flat):
    Bl, S, D = x.shape
    L = len(Wl)
    tabs = _rope_tables(positions)
    bucket = _bucket_table()
    bias = swa_bias(rel_flat, bucket, name="swa_bias")
    mds = []
    for l in range(L):
        parts = [mods[l, :, D * k:D * (k + 1)].reshape(Bl, 1, D) for k in range(6)]
        mds.append(dict(zip(("shift1", "scale1", "gate1", "shift2", "scale2", "gate2"), parts)))
    saved = []
    h = x
    for l in range(L):
        h, sv = _layer_fwd(h, mds[l], Wl[l], tabs, bias, f"l{l}")
        saved.append(sv)
    dy, loss = loss_grad(h, target, name="loss")
    grads, smalls, dmods, dbiases, dsinks = [None] * L, [None] * L, [None] * L, [None] * L, [None] * L
    for l in reversed(range(L)):
        dy, grads[l], smalls[l], dmods[l], dbiases[l], dsinks[l] = _layer_bwd(dy, saved[l], mds[l], Wl[l], tabs, bias, f"l{l}")
    drel = swa_bias_bwd(_cat(dbiases, axis=0), bucket, name="swa_bias_bwd")
    return loss, dy, grads, smalls, dmods, dsinks, drel


WEIGHTS =("rel_table", "norm1_g", "norm2_g", "w_ada", "b_ada", "w_in", "mla_cq_g", "w_uq", "mla_ckv_g", "w_ukv", "mla_qn_g", "mla_kn_g",
           "sw_qn_g", "sw_kn_g", "sw_sinks", "w_out", "w_up", "conv_w", "conv_b", "w_down")
SMALL = tuple(n for n in WEIGHTS if n not in BIG + ("w_ada",))


def kernel(x, c, positions, rel_table, norm1_g, norm2_g, w_ada, b_ada, w_in, mla_cq_g, w_uq, mla_ckv_g, w_ukv, mla_qn_g, mla_kn_g, sw_qn_g, sw_kn_g, sw_sinks, w_out, w_up, conv_w, conv_b, w_down, loss_target, m_rel_table, m_norm1_g, m_norm2_g, m_w_ada, m_b_ada, m_w_in, m_mla_cq_g, m_w_uq, m_mla_ckv_g, m_w_ukv, m_mla_qn_g, m_mla_kn_g, m_sw_qn_g, m_sw_kn_g, m_sw_sinks, m_w_out, m_w_up, m_conv_w, m_conv_b, m_w_down, v_rel_table, v_norm1_g, v_norm2_g, v_w_ada, v_b_ada, v_w_in, v_mla_cq_g, v_w_uq, v_mla_ckv_g, v_w_ukv, v_mla_qn_g, v_mla_kn_g, v_sw_qn_g, v_sw_kn_g, v_sw_sinks, v_w_out, v_w_up, v_conv_w, v_conv_b, v_w_down):
    w = dict(rel_table=rel_table, norm1_g=norm1_g, norm2_g=norm2_g, w_ada=w_ada, b_ada=b_ada, w_in=w_in, mla_cq_g=mla_cq_g, w_uq=w_uq,
             mla_ckv_g=mla_ckv_g, w_ukv=w_ukv, mla_qn_g=mla_qn_g, mla_kn_g=mla_kn_g, sw_qn_g=sw_qn_g, sw_kn_g=sw_kn_g, sw_sinks=sw_sinks,
             w_out=w_out, w_up=w_up, conv_w=conv_w, conv_b=conv_b, w_down=w_down)
    m = dict(rel_table=m_rel_table, norm1_g=m_norm1_g, norm2_g=m_norm2_g, w_ada=m_w_ada, b_ada=m_b_ada, w_in=m_w_in, mla_cq_g=m_mla_cq_g,
             w_uq=m_w_uq, mla_ckv_g=m_mla_ckv_g, w_ukv=m_w_ukv, mla_qn_g=m_mla_qn_g, mla_kn_g=m_mla_kn_g, sw_qn_g=m_sw_qn_g,
             sw_kn_g=m_sw_kn_g, sw_sinks=m_sw_sinks, w_out=m_w_out, w_up=m_w_up, conv_w=m_conv_w, conv_b=m_conv_b, w_down=m_w_down)
    v = dict(rel_table=v_rel_table, norm1_g=v_norm1_g, norm2_g=v_norm2_g, w_ada=v_w_ada, b_ada=v_b_ada, w_in=v_w_in, mla_cq_g=v_mla_cq_g,
             w_uq=v_w_uq, mla_ckv_g=v_mla_ckv_g, w_ukv=v_w_ukv, mla_qn_g=v_mla_qn_g, mla_kn_g=v_mla_kn_g, sw_qn_g=v_sw_qn_g,
             sw_kn_g=v_sw_kn_g, sw_sinks=v_sw_sinks, w_out=v_w_out, w_up=v_w_up, conv_w=v_conv_w, conv_b=v_conv_b, w_down=v_w_down)
    Bl, S, D = x.shape
    L = norm1_g.shape[0]
    xi, yi, ci = _me()
    chip = 2 * xi + yi
    dev = 4 * xi + 2 * yi + ci
    ndev = 2 * NCHIPS

    shapes = {k: w[k].shape[1:] for k in BIG}
    got_w = gather_weights([w[k].astype(BF16).reshape(L * shapes[k][0], shapes[k][1]) for k in BIG], name="gather_weights")
    full = [dict() for _ in range(L)]
    for k, w4 in zip(BIG, got_w):
        r, cc = shapes[k]
        for l in range(L):
            seg = w4[:, l * r:(l + 1) * r, :]
            fw = seg.reshape(NCHIPS * r, cc) if k in ROW_SHARDED else jnp.transpose(seg, (1, 0, 2)).reshape(r, NCHIPS * cc)
            full[l][k] = PREP[k](fw)

    cw_cols = conv_w.shape[2]
    c_got, cw_got = allgather8([c, conv_w.reshape(L * 3, cw_cols)], name="gather_cond")
    c_all = c_got.reshape(ndev * Bl, D)
    conv_full = jnp.transpose(cw_got[0::2].reshape(NCHIPS, L, 3, cw_cols), (1, 2, 0, 3)).reshape(L, 3, NCHIPS * cw_cols)
    E = w_ada.shape[2]
    b_cols = lax.dynamic_slice(b_ada, (0, chip * E), (L, E)).reshape(L, 1, E)
    mods_cols = mods_matmul(c_all, w_ada, b_cols, name="mods")
    mods_all, = allgather8([mods_cols.reshape(L * ndev * Bl, E)], name="gather_mods")
    mods_all = jnp.transpose(mods_all[0::2].reshape(NCHIPS, L, ndev * Bl, E), (1, 2, 0, 3)).reshape(L, ndev * Bl, NCHIPS * E)
    mods = lax.dynamic_slice(mods_all, (0, dev * Bl, 0), (L, Bl, NCHIPS * E))

    Wl = []
    for l in range(L):
        Wd = _small_params({k: w[k][l] for k in SMALL if k not in ("rel_table", "b_ada", "conv_w")})
        Wd.update(full[l])
        Wd["conv_w"] = _up_perm(conv_full[l])
        Wl.append(Wd)

    loss, dx, grads, smalls, dmods, dsinks, drel = _local_step(x, loss_target, positions, mods, Wl, rel_table.reshape(-1))

    g4s = []
    for k in BIG:
        r, cc = shapes[k]
        per_layer = []
        for l in range(L):
            gk = UNPREP[k](grads[l][k])
            per_layer.append(gk.reshape(NCHIPS, r, cc) if k in ROW_SHARDED else jnp.transpose(gk.reshape(r, NCHIPS, cc), (1, 0, 2)))
        g4s.append(_cat(per_layer, axis=1))
    theirs = swap_halves(g4s, name="rs_swap_halves")
    c_arr = ci.reshape(1).astype(jnp.int32)
    pairs = [pair_add_half(g4, th, c_arr, name=f"rs_pair_add_{k}") for k, g4, th in zip(BIG, g4s, theirs)]
    landed = scatter_chips(pairs, name="rs_scatter_chips")
    halves = [sum_leading(ld, name=f"rs_chip_sum_{k}") for k, ld in zip(BIG, landed)]
    joined = join_halves(halves, name="rs_join_halves")
    grad = {k: j.reshape((L,) + tuple(shapes[k])) for k, j in zip(BIG, joined)}

    vec_names = ("n1", "n2", "cq_g", "ckv_g", "qn_g", "kn_g", "swq_g", "swk_g")
    vecs = _cat([_cat([smalls[l][k] for k in vec_names], axis=1) for l in range(L)], axis=0)
    convs = _cat([smalls[l]["conv"] for l in range(L)], axis=0)
    dm = jnp.stack(dmods, axis=1).reshape(Bl * L, 6 * D)
    dsk = jnp.stack(dsinks, axis=1).reshape(Bl * L * 6, LANES)
    got = allgather8([vecs, convs, drel, loss, dm, dsk], name="gather_small_grads")
    seq = lambda a, rows: a.reshape(ndev * Bl, rows, a.shape[-1])
    vec_s, conv_s, rel_s, loss_s, dm_s, dsk_s = sum_small(list(got[:4]) + [seq(got[4], L), seq(got[5], L * 6)], name="sum_small_grads")
    dm_all = jnp.transpose(seq(got[4], L), (1, 0, 2))
    grad["w_ada"] = ada_grad(c_all, lax.dynamic_slice(dm_all, (0, 0, chip * E), (L, ndev * Bl, E)), name="ada_grad")
    grad["b_ada"] = dm_s
    grad["sw_sinks"] = jnp.transpose(dsk_s.reshape(L, 3, 2, LANES)[:, :, :, 0], (0, 2, 1)).reshape(L, 6)
    grad["rel_table"] = rel_s[:6, :REL_BUCKETS].T
    off = 0
    for k, name_, keep in zip(vec_names, ("norm1_g", "norm2_g", "mla_cq_g", "mla_ckv_g", "mla_qn_g", "mla_kn_g", "sw_qn_g", "sw_kn_g"),
                              (D, D, 256, LANES, MLA_QK, MLA_QK, HEAD, HEAD)):
        grad[name_] = vec_s[:, off:off + keep]
        off += smalls[0][k].shape[1]
    conv = _up_perm(conv_s.reshape(L, 8, 2 * D_FF))
    grad["conv_w"] = lax.dynamic_slice(conv[:, 0:3], (0, 0, chip * cw_cols), (L, 3, cw_cols))
    grad["conv_b"] = conv[:, 3]
    loss_out = loss_s[0, 0]

    delta, new_m, new_v = {}, {}, {}
    for k in BIG + ("w_ada",):
        shp = w[k].shape
        to2 = lambda a: a.reshape(-1, shp[-1])
        d_, m_, v_ = adamw(to2(w[k]), to2(grad[k]), to2(m[k]), to2(v[k]), name=f"adamw_{k}")
        delta[k], new_m[k], new_v[k] = d_.reshape(shp), m_.reshape(shp), v_.reshape(shp)
    outs = adamw_small(*[[src[k] for k in SMALL] for src in (w, grad, m, v)], name="adamw_small")
    for dst, o in zip((delta, new_m, new_v), outs):
        dst.update(dict(zip(SMALL, o)))
    return (loss_out, dx, *[grad[k] for k in WEIGHTS], *[delta[k] for k in WEIGHTS], *[new_m[k] for k in WEIGHTS], *[new_v[k] for k in WEIGHTS])
```

```python
import functools
import math

import jax
import jax.numpy as jnp
from jax import lax
from jax.experimental import pallas as pl
from jax.experimental.pallas import tpu as pltpu

F32 = jnp.float32
BF16 = jnp.bfloat16
MESH = pl.DeviceIdType.MESH

EPS = 1e-6
NEG = -1e30
HEAD = 64
LANES = 128
MLA_QK = 96
ROPE_THETA = 10000.0
REL_BUCKETS = 32
REL_MAX_DIST = 128
WINDOW = 128
D_FF = 2816
ADAM_LR, ADAM_B1, ADAM_B2, ADAM_EPS, ADAM_WD, ADAM_STEP = 0.001, 0.9, 0.999, 1e-08, 0.01, 10

VMEM_LIMIT = 56 * 1024 * 1024

P_SBQ, P_SBK, P_SBV, P_CQ, P_CKV, P_SLAB, P_SWQ, P_SWK, P_SWV, P_END = 0, 256, 512, 768, 1024, 1152, 1280, 1664, 1792, 1920
SW_PERM = (0, 3, 1, 4, 2, 5)


def _cp(*sem):
    return pltpu.CompilerParams(dimension_semantics=sem, vmem_limit_bytes=VMEM_LIMIT)


def _dot(a, b):
    return jnp.dot(a, b, preferred_element_type=F32)


def _dot_nt(a, b):
    return lax.dot_general(a, b, (((1,), (1,)), ((), ())), preferred_element_type=F32)


def _dot_tn(a, b):
    return lax.dot_general(a, b, (((0,), (0,)), ((), ())), preferred_element_type=F32)


def _split_dot(x, u):
    hi = x.astype(BF16)
    lo = (x - hi.astype(F32)).astype(BF16)
    return _dot(hi, u) + _dot(lo, u)


def _lane_masks():
    lane = lax.broadcasted_iota(jnp.int32, (1, LANES), 1)
    return (lane < HEAD, lane >= HEAD)


def _tile(n, cap, align=128):
    if n <= cap:
        return n
    t = cap - cap % align
    while t >= align:
        if n % t == 0:
            return t
        t -= align
    return n


def matmul(a, b, *, ta=False, tb=False, out_dtype=F32, tm=512, tn=512, tk=8192, name):
    M, K = (a.shape[1], a.shape[0]) if ta else a.shape
    N = b.shape[0] if tb else b.shape[1]
    tm, tn, tk = _tile(M, tm), _tile(N, tn), _tile(K, tk)
    nk = K // tk

    def body(a_ref, b_ref, o_ref, *scratch):
        av = a_ref[...].astype(BF16)
        bv = b_ref[...].astype(BF16)
        if ta:
            part = _dot_tn(av, bv)
        elif tb:
            part = _dot_nt(av, bv)
        else:
            part = _dot(av, bv)
        if nk == 1:
            o_ref[...] = part.astype(out_dtype)
        else:
            acc_ref, = scratch
            k = pl.program_id(2)

            @pl.when(k == 0)
            def _():
                acc_ref[...] = part

            @pl.when(k > 0)
            def _():
                acc_ref[...] += part

            @pl.when(k == nk - 1)
            def _():
                o_ref[...] = acc_ref[...].astype(out_dtype)

    a_spec = pl.BlockSpec((tk, tm), lambda i, j, k: (k, i)) if ta else pl.BlockSpec((tm, tk), lambda i, j, k: (i, k))
    b_spec = pl.BlockSpec((tn, tk), lambda i, j, k: (j, k)) if tb else pl.BlockSpec((tk, tn), lambda i, j, k: (k, j))
    return pl.pallas_call(
        body, name=name, grid=(M // tm, N // tn, nk),
        in_specs=[a_spec, b_spec], out_specs=pl.BlockSpec((tm, tn), lambda i, j, k: (i, j)),
        out_shape=jax.ShapeDtypeStruct((M, N), out_dtype),
        scratch_shapes=[] if nk == 1 else [pltpu.VMEM((tm, tn), F32)],
        compiler_params=_cp("parallel", "parallel", "arbitrary"),
    )(a, b)


def matmul_res(a, b, res, gate, seq, *, tm=512, tn=1024, name):
    M, K = a.shape
    N = b.shape[1]
    tm, tn = _tile(min(M, seq), tm), _tile(N, tn)
    per_seq = seq // tm

    def body(a_ref, b_ref, r_ref, g_ref, y_ref, x_ref):
        y = _dot(a_ref[...].astype(BF16), b_ref[...].astype(BF16))
        y_ref[...] = y
        x_ref[...] = r_ref[...] + g_ref[...] * y

    out = jax.ShapeDtypeStruct((M, N), F32)
    return pl.pallas_call(
        body, name=name, grid=(M // tm, N // tn),
        in_specs=[pl.BlockSpec((tm, K), lambda i, j: (i, 0)), pl.BlockSpec((K, tn), lambda i, j: (0, j)),
                  pl.BlockSpec((tm, tn), lambda i, j: (i, j)), pl.BlockSpec((None, 1, tn), lambda i, j: (lax.div(i, jnp.int32(per_seq)), 0, j))],
        out_specs=[pl.BlockSpec((tm, tn), lambda i, j: (i, j))] * 2,
        out_shape=[out, out], compiler_params=_cp("parallel", "parallel"),
    )(a, b, res, gate)


def rms_fwd(x3, blk, W, g, sc=None, sh=None, *, tm=512, name):
    Bl, S, _ = x3.shape
    tm = min(tm, S)
    mod = sc is not None

    def body(x_ref, g_ref, *rest):
        o_ref = rest[-1]
        x = x_ref[...]
        r = lax.rsqrt(jnp.mean(x * x, axis=-1, keepdims=True) + EPS)
        y = x * r * g_ref[...]
        if mod:
            y = y * (1.0 + rest[0][...]) + rest[1][...]
        o_ref[...] = y.astype(BF16)

    vec = pl.BlockSpec((None, 1, W), lambda b, s: (b, 0, 0))
    return pl.pallas_call(
        body, name=name, grid=(Bl, S // tm),
        in_specs=[pl.BlockSpec((None, tm, W), lambda b, s: (b, s, blk)), pl.BlockSpec((1, W), lambda b, s: (0, 0))] + ([vec, vec] if mod else []),
        out_specs=pl.BlockSpec((None, tm, W), lambda b, s: (b, s, 0)),
        out_shape=jax.ShapeDtypeStruct((Bl, S, W), BF16),
        compiler_params=_cp("parallel", "parallel"),
    )(x3, g, *([sc, sh] if mod else []))


def rms_bwd(x3, blk, W, dy3, g, sc=None, dres3=None, *, tm=256, name):
    Bl, S, _ = x3.shape
    tm = min(tm, S)
    mod = sc is not None
    res = dres3 is not None

    def body(*refs):
        x_ref, dy_ref, g_ref = refs[:3]
        k = 3
        sc_ref = dr_ref = None
        if mod:
            sc_ref = refs[k]
            k += 1
        if res:
            dr_ref = refs[k]
            k += 1
        dx_ref, dg_ref = refs[k], refs[k + 1]
        b, s = pl.program_id(0), pl.program_id(1)
        x = x_ref[...]
        dy = dy_ref[...].astype(F32)
        g = g_ref[...]
        r = lax.rsqrt(jnp.mean(x * x, axis=-1, keepdims=True) + EPS)
        n = x * r
        if mod:
            dsc_ref, dsh_ref = refs[k + 2], refs[k + 3]
            one_sc = 1.0 + sc_ref[...]

            @pl.when(s == 0)
            def _():
                dsc_ref[...] = jnp.zeros_like(dsc_ref)
                dsh_ref[...] = jnp.zeros_like(dsh_ref)

            dsh_ref[...] += jnp.sum(dy, axis=0, keepdims=True)
            dsc_ref[...] += jnp.sum(dy * n * g, axis=0, keepdims=True)
            dyn = dy * one_sc
        else:
            dyn = dy

        @pl.when((b == 0) & (s == 0))
        def _():
            dg_ref[...] = jnp.zeros_like(dg_ref)

        dg_ref[...] += jnp.sum(dyn * n, axis=0, keepdims=True)
        dn = dyn * g
        dx = r * (dn - n * jnp.mean(dn * n, axis=-1, keepdims=True))
        if res:
            dx = dx + dr_ref[...]
        dx_ref[...] = dx

    blkspec = pl.BlockSpec((None, tm, W), lambda b, s: (b, s, 0))
    vec = pl.BlockSpec((None, 1, W), lambda b, s: (b, 0, 0))
    row = pl.BlockSpec((1, W), lambda b, s: (0, 0))
    in_specs = [pl.BlockSpec((None, tm, W), lambda b, s: (b, s, blk)), blkspec, row] + ([vec] if mod else []) + ([blkspec] if res else [])
    out_specs = [blkspec, row] + ([vec, vec] if mod else [])
    out_shape = [jax.ShapeDtypeStruct((Bl, S, W), F32), jax.ShapeDtypeStruct((1, W), F32)]
    if mod:
        out_shape += [jax.ShapeDtypeStruct((Bl, 1, W), F32)] * 2
    args = [x3, dy3, g] + ([sc] if mod else []) + ([dres3] if res else [])
    return pl.pallas_call(
        body, name=name, grid=(Bl, S // tm), in_specs=in_specs, out_specs=out_specs, out_shape=out_shape,
        compiler_params=_cp("arbitrary", "arbitrary"),
    )(*args)


def pair_rms_fwd(x3, blk0, npairs, g2, *, tm=1024, name):
    Bl, S, _ = x3.shape
    tm = min(tm, S)

    def body(x_ref, g_ref, o_ref):
        lo, hi = _lane_masks()
        x = x_ref[...]
        xx = x * x
        s0 = jnp.sum(jnp.where(lo, xx, 0.0), axis=-1, keepdims=True)
        s1 = jnp.sum(jnp.where(hi, xx, 0.0), axis=-1, keepdims=True)
        r = jnp.where(lo, lax.rsqrt(s0 / HEAD + EPS), lax.rsqrt(s1 / HEAD + EPS))
        o_ref[...] = (x * r * g_ref[...]).astype(BF16)

    return pl.pallas_call(
        body, name=name, grid=(Bl, S // tm, npairs),
        in_specs=[pl.BlockSpec((None, tm, LANES), lambda b, s, p: (b, s, blk0 + p)), pl.BlockSpec((1, LANES), lambda b, s, p: (0, 0))],
        out_specs=pl.BlockSpec((None, tm, LANES), lambda b, s, p: (b, s, p)),
        out_shape=jax.ShapeDtypeStruct((Bl, S, LANES * npairs), BF16),
        compiler_params=_cp("parallel", "parallel", "parallel"),
    )(x3, g2)


def pair_rms_bwd(x3, blk0, npairs, dy3, g2, *, tm=1024, name):
    Bl, S, _ = x3.shape
    tm = min(tm, S)

    def body(x_ref, dy_ref, g_ref, dx_ref, dg_ref):
        lo, hi = _lane_masks()
        first = (pl.program_id(0) == 0) & (pl.program_id(1) == 0) & (pl.program_id(2) == 0)
        x = x_ref[...]
        dy = dy_ref[...]
        xx = x * x
        s0 = jnp.sum(jnp.where(lo, xx, 0.0), axis=-1, keepdims=True)
        s1 = jnp.sum(jnp.where(hi, xx, 0.0), axis=-1, keepdims=True)
        r = jnp.where(lo, lax.rsqrt(s0 / HEAD + EPS), lax.rsqrt(s1 / HEAD + EPS))
        n = x * r

        @pl.when(first)
        def _():
            dg_ref[...] = jnp.zeros_like(dg_ref)

        part = jnp.sum(dy * n, axis=0, keepdims=True)
        dg_ref[...] += part + pltpu.roll(part, HEAD, 1)
        dn = dy * g_ref[...]
        t = dn * n
        m0 = jnp.sum(jnp.where(lo, t, 0.0), axis=-1, keepdims=True)
        m1 = jnp.sum(jnp.where(hi, t, 0.0), axis=-1, keepdims=True)
        dx_ref[...] = r * (dn - n * (jnp.where(lo, m0, m1) / HEAD))

    return pl.pallas_call(
        body, name=name, grid=(Bl, S // tm, npairs),
        in_specs=[pl.BlockSpec((None, tm, LANES), lambda b, s, p: (b, s, blk0 + p)), pl.BlockSpec((None, tm, LANES), lambda b, s, p: (b, s, p)),
                  pl.BlockSpec((1, LANES), lambda b, s, p: (0, 0))],
        out_specs=[pl.BlockSpec((None, tm, LANES), lambda b, s, p: (b, s, p)), pl.BlockSpec((1, LANES), lambda b, s, p: (0, 0))],
        out_shape=[jax.ShapeDtypeStruct((Bl, S, LANES * npairs), F32), jax.ShapeDtypeStruct((1, LANES), F32)],
        compiler_params=_cp("arbitrary", "arbitrary", "arbitrary"),
    )(x3, dy3, g2)


def _rot(y, cos_t, sin_a, sin_b):
    return y * cos_t + pltpu.roll(y, LANES - 16, 1) * sin_a + pltpu.roll(y, 16, 1) * sin_b


def _rot_t(d, cos_t, sin_a, sin_b):
    return d * cos_t + pltpu.roll(d * sin_a, 16, 1) + pltpu.roll(d * sin_b, LANES - 16, 1)


def rope_norm_fwd(x3, nheads, g, tabs, slab=None, *, tm=1024, name):
    Bl, S, _ = x3.shape
    tm = min(tm, S)
    has_slab = slab is not None

    def body(*refs):
        x_ref, g_ref, c_ref, sa_ref, sb_ref = refs[:5]
        o_ref = refs[-1]
        x = x_ref[...]
        if has_slab:
            x = x + refs[5][...]
        r = lax.rsqrt(jnp.sum(x * x, axis=-1, keepdims=True) / MLA_QK + EPS)
        o_ref[...] = _rot(x * r * g_ref[...], c_ref[...], sa_ref[...], sb_ref[...]).astype(BF16)

    head = pl.BlockSpec((None, tm, LANES), lambda b, s, h: (b, s, h))
    tab = pl.BlockSpec((None, tm, LANES), lambda b, s, h: (b, s, 0))
    in_specs = [head, pl.BlockSpec((1, LANES), lambda b, s, h: (0, 0)), tab, tab, tab]
    args = [x3, g, *tabs]
    if has_slab:
        sblk = slab[1]
        in_specs.append(pl.BlockSpec((None, tm, LANES), lambda b, s, h: (b, s, sblk)))
        args.append(slab[0])
    return pl.pallas_call(
        body, name=name, grid=(Bl, S // tm, nheads), in_specs=in_specs, out_specs=head,
        out_shape=jax.ShapeDtypeStruct((Bl, S, LANES * nheads), BF16),
        compiler_params=_cp("parallel", "parallel", "parallel"),
    )(*args)


def rope_norm_bwd(x3, nheads, dy3, g, tabs, slab=None, *, tm=1024, name):
    Bl, S, _ = x3.shape
    tm = min(tm, S)
    has_slab = slab is not None

    def body(*refs):
        x_ref, dy_ref, g_ref, c_ref, sa_ref, sb_ref = refs[:6]
        k = 7 if has_slab else 6
        dx_ref, dg_ref = refs[k], refs[k + 1]
        h = pl.program_id(2)
        first = (pl.program_id(0) == 0) & (pl.program_id(1) == 0) & (h == 0)
        x = x_ref[...]
        if has_slab:
            x = x + refs[6][...]
        g = g_ref[...]
        r = lax.rsqrt(jnp.sum(x * x, axis=-1, keepdims=True) / MLA_QK + EPS)
        n = x * r
        d = _rot_t(dy_ref[...], c_ref[...], sa_ref[...], sb_ref[...])

        @pl.when(first)
        def _():
            dg_ref[...] = jnp.zeros_like(dg_ref)

        dg_ref[...] += jnp.sum(d * n, axis=0, keepdims=True)
        dn = d * g
        dx = r * (dn - n * (jnp.sum(dn * n, axis=-1, keepdims=True) / MLA_QK))
        dx_ref[...] = dx
        if has_slab:
            ds_ref = refs[k + 2]

            @pl.when(h == 0)
            def _():
                ds_ref[...] = dx

            @pl.when(h > 0)
            def _():
                ds_ref[...] += dx

    head = pl.BlockSpec((None, tm, LANES), lambda b, s, h: (b, s, h))
    tab = pl.BlockSpec((None, tm, LANES), lambda b, s, h: (b, s, 0))
    row = pl.BlockSpec((1, LANES), lambda b, s, h: (0, 0))
    in_specs = [head, head, row, tab, tab, tab]
    args = [x3, dy3, g, *tabs]
    out_specs = [head, row]
    out_shape = [jax.ShapeDtypeStruct((Bl, S, LANES * nheads), F32), jax.ShapeDtypeStruct((1, LANES), F32)]
    if has_slab:
        sblk = slab[1]
        in_specs.append(pl.BlockSpec((None, tm, LANES), lambda b, s, h: (b, s, sblk)))
        args.append(slab[0])
        out_specs.append(tab)
        out_shape.append(jax.ShapeDtypeStruct((Bl, S, LANES), F32))
    return pl.pallas_call(
        body, name=name, grid=(Bl, S // tm, nheads), in_specs=in_specs, out_specs=out_specs, out_shape=out_shape,
        compiler_params=_cp("arbitrary", "arbitrary", "arbitrary"),
    )(*args)


def _sb_tile(z, strict, u, carry_r):
    sp = jnp.maximum(z, 0.0) + jnp.log(1.0 + jnp.exp(-jnp.abs(z)))
    keep = jnp.where(strict, -sp, 0.0)
    logw = (z - sp) + _split_dot(keep, u) + carry_r
    return jnp.where(strict, jnp.exp(logw), 0.0), keep, sp


SB_BLOCK = 256
SB_QBLOCK = 512


def sb_attn_fwd(proj3, *, plans=None, name):
    Bl, S, _ = proj3.shape
    tk = min(SB_BLOCK, S)
    tq = min(SB_QBLOCK, S)
    per_q = tq // tk
    scale = HEAD ** -0.5
    qb, kb0, vb0 = P_SBQ // LANES, P_SBK // LANES, P_SBV // LANES

    def body(q_ref, k_ref, v_ref, o_ref, rt_ref):
        i = pl.program_id(2)
        masks = _lane_masks()
        lane = lax.broadcasted_iota(jnp.int32, (1, LANES), 1)
        q = q_ref[...]
        qh = [jnp.where(m, q, 0.0).astype(BF16) for m in masks]
        rr = lax.broadcasted_iota(jnp.int32, (tq, tk), 0)
        cc = lax.broadcasted_iota(jnp.int32, (tq, tk), 1)
        u = (lax.broadcasted_iota(jnp.int32, (tk, tk), 0) > lax.broadcasted_iota(jnp.int32, (tk, tk), 1)).astype(BF16)

        rt_ref[...] = jnp.zeros_like(rt_ref)

        def step(t, carry):
            r0, r1, acc = carry
            j = (i + 1) * per_q - 1 - t
            off = pl.multiple_of(j * tk, tk)
            kb = k_ref[pl.ds(off, tk), :].astype(BF16)
            vb = v_ref[pl.ds(off, tk), :]
            strict = (cc + j * tk) < (rr + i * tq)
            rt_ref[...] = jnp.where(lane == j, r0, jnp.where(lane == j + HEAD, r1, rt_ref[...]))
            rs = [r0, r1]
            for h in range(2):
                z = _dot_nt(qh[h], kb) * scale
                w, keep, _ = _sb_tile(z, strict, u, rs[h])
                acc = acc + _dot(w.astype(BF16), jnp.where(masks[h], vb, 0.0).astype(BF16))
                rs[h] = rs[h] + jnp.sum(keep, axis=1, keepdims=True)
            return rs[0], rs[1], acc

        zero = jnp.zeros((tq, 1), F32)
        _, _, acc = lax.fori_loop(0, (i + 1) * per_q, step, (zero, zero, jnp.zeros((tq, LANES), F32)))
        o_ref[...] = acc

    seq = lambda blk0: pl.BlockSpec((None, S, LANES), lambda b, p, i: (b, 0, blk0 + p))
    out = pl.BlockSpec((None, tq, LANES), lambda b, p, i: (b, i, p))
    shp = jax.ShapeDtypeStruct((Bl, S, 2 * LANES), F32)
    return call_with_plans(
        body, plans, name=name, grid=(Bl, 2, S // tq),
        in_specs=[pl.BlockSpec((None, tq, LANES), lambda b, p, i: (b, i, qb + p)), seq(kb0), seq(vb0)],
        out_specs=[out, out], out_shape=[shp, shp], scratch_shapes=[], args=[proj3, proj3, proj3],
        sem=("arbitrary",) * 3 if plans else ("parallel", "parallel", "arbitrary"))


def sb_attn_bwd(proj3, rt3, do3, *, plans=None, name):
    Bl, S, _ = proj3.shape
    tk = min(SB_BLOCK, S)
    tq = min(SB_QBLOCK, S)
    per_q = tq // tk
    scale = HEAD ** -0.5
    qb, kb0, vb0 = P_SBQ // LANES, P_SBK // LANES, P_SBV // LANES

    def body(q_ref, k_ref, v_ref, rt_ref, do_ref, dq_ref, dk_ref, dv_ref):
        i = pl.program_id(2)

        @pl.when(i == 0)
        def _():
            dk_ref[...] = jnp.zeros_like(dk_ref)
            dv_ref[...] = jnp.zeros_like(dv_ref)

        masks = _lane_masks()
        lane = lax.broadcasted_iota(jnp.int32, (1, LANES), 1)
        q = q_ref[...]
        qh = [jnp.where(m, q, 0.0).astype(BF16) for m in masks]
        do_b = do_ref[...].astype(BF16)
        doh = [jnp.where(m, do_b, jnp.zeros_like(do_b)) for m in masks]
        rt = rt_ref[...]
        rr = lax.broadcasted_iota(jnp.int32, (tq, tk), 0)
        cc = lax.broadcasted_iota(jnp.int32, (tq, tk), 1)
        ur = lax.broadcasted_iota(jnp.int32, (tk, tk), 0)
        uc = lax.broadcasted_iota(jnp.int32, (tk, tk), 1)
        u_suffix = (ur > uc).astype(BF16)
        u_prefix = (ur < uc).astype(BF16)

        def step(j, carry):
            p0, p1, dq = carry
            off = pl.multiple_of(j * tk, tk)
            kf = k_ref[pl.ds(off, tk), :]
            kb = kf.astype(BF16)
            vb = v_ref[pl.ds(off, tk), :]
            strict = (cc + j * tk) < (rr + i * tq)
            ps = [p0, p1]
            dk_acc = jnp.zeros((tk, LANES), F32)
            dv_acc = jnp.zeros((tk, LANES), F32)
            for h in range(2):
                r_j = jnp.sum(jnp.where(lane == j + h * HEAD, rt, 0.0), axis=1, keepdims=True)
                z = _dot_nt(qh[h], kb) * scale
                w, _, sp = _sb_tile(z, strict, u_suffix, r_j)
                vh = jnp.where(masks[h], vb, 0.0).astype(BF16)
                g = _dot_nt(doh[h], vh) * w
                pre = _split_dot(g, u_prefix) + ps[h]
                dz = jnp.where(strict, g * jnp.exp(-sp) - jnp.exp(z - sp) * pre, 0.0) * scale
                dzb = dz.astype(BF16)
                dq = dq + _dot(dzb, jnp.where(masks[h], kf, 0.0).astype(BF16))
                dk_acc = dk_acc + _dot_tn(dzb, qh[h])
                dv_acc = dv_acc + _dot_tn(w.astype(BF16), doh[h])
                ps[h] = ps[h] + jnp.sum(g, axis=1, keepdims=True)
            dk_ref[pl.ds(off, tk), :] += dk_acc
            dv_ref[pl.ds(off, tk), :] += dv_acc
            return ps[0], ps[1], dq

        zero = jnp.zeros((tq, 1), F32)
        out = lax.fori_loop(0, (i + 1) * per_q, step, (zero, zero, jnp.zeros((tq, LANES), F32)))
        dq_ref[...] = out[2]

    seq_in = lambda blk0: pl.BlockSpec((None, S, LANES), lambda b, p, i: (b, 0, blk0 + p))
    blk = pl.BlockSpec((None, tq, LANES), lambda b, p, i: (b, i, p))
    seq_out = pl.BlockSpec((None, S, LANES), lambda b, p, i: (b, 0, p))
    shp = jax.ShapeDtypeStruct((Bl, S, 2 * LANES), F32)
    return call_with_plans(
        body, plans, name=name, grid=(Bl, 2, S // tq),
        in_specs=[pl.BlockSpec((None, tq, LANES), lambda b, p, i: (b, i, qb + p)), seq_in(kb0), seq_in(vb0), blk, blk],
        out_specs=[blk, seq_out, seq_out], out_shape=[shp, shp, shp], scratch_shapes=[], args=[proj3, proj3, proj3, rt3, do3],
        sem=("arbitrary",) * 3 if plans else ("parallel", "parallel", "arbitrary"))


def mla_attn_fwd(q3, k3, kv3, vblk0, *, tq=512, tk=256, plans=None, name):
    Bl, S, _ = q3.shape
    tq = min(tq, S)
    tk = min(tk, tq)
    per_q = tq // tk
    scale = MLA_QK ** -0.5

    def body(q_ref, k_ref, v_ref, o_ref, lse_ref):
        i = pl.program_id(2)
        masks = _lane_masks()
        rr = lax.broadcasted_iota(jnp.int32, (tq, tk), 0)
        cc = lax.broadcasted_iota(jnp.int32, (tq, tk), 1)
        qh = [q_ref[:, h * LANES:(h + 1) * LANES] for h in range(2)]

        def step(j, carry):
            m0, l0, m1, l1, acc = carry
            off = pl.multiple_of(j * tk, tk)
            vb = v_ref[pl.ds(off, tk), :]
            causal = (cc + j * tk) <= (rr + i * tq)
            ms, ls, alphas = [m0, m1], [l0, l1], []
            add = jnp.zeros((tq, LANES), F32)
            for h in range(2):
                kh = k_ref[pl.ds(off, tk), h * LANES:(h + 1) * LANES]
                s = jnp.where(causal, _dot_nt(qh[h], kh) * scale, NEG)
                m_new = jnp.maximum(ms[h], jnp.max(s, axis=1, keepdims=True))
                p = jnp.exp(s - m_new)
                alpha = jnp.exp(ms[h] - m_new)
                ls[h] = alpha * ls[h] + jnp.sum(p, axis=1, keepdims=True)
                ms[h] = m_new
                alphas.append(alpha)
                add = add + _dot(p.astype(BF16), jnp.where(masks[h], vb, 0.0).astype(BF16))
            acc = acc * jnp.where(masks[0], alphas[0], alphas[1]) + add
            return ms[0], ls[0], ms[1], ls[1], acc

        neg = jnp.full((tq, 1), NEG, F32)
        zero = jnp.zeros((tq, 1), F32)
        m0, l0, m1, l1, acc = lax.fori_loop(0, (i + 1) * per_q, step, (neg, zero, neg, zero, jnp.zeros((tq, LANES), F32)))
        o_ref[...] = acc / jnp.where(masks[0], l0, l1)
        lse_ref[...] = jnp.where(masks[0], m0 + jnp.log(l0), m1 + jnp.log(l1))

    out = pl.BlockSpec((None, tq, LANES), lambda b, p, i: (b, i, p))
    shp = jax.ShapeDtypeStruct((Bl, S, 3 * LANES), F32)
    return call_with_plans(
        body, plans, name=name, grid=(Bl, 3, S // tq),
        in_specs=[pl.BlockSpec((None, tq, 2 * LANES), lambda b, p, i: (b, i, p)), pl.BlockSpec((None, S, 2 * LANES), lambda b, p, i: (b, 0, p)),
                  pl.BlockSpec((None, S, LANES), lambda b, p, i: (b, 0, vblk0 + p))],
        out_specs=[out, out], out_shape=[shp, shp], scratch_shapes=[], args=[q3, k3, kv3],
        sem=("arbitrary",) * 3 if plans else ("parallel", "parallel", "arbitrary"))


def mla_attn_bwd(q3, k3, kv3, vblk0, o3, lse3, do3, *, tq=512, tk=256, name):
    Bl, S, _ = q3.shape
    tq = min(tq, S)
    tk = min(tk, tq)
    per_q = tq // tk
    nq = S // tq
    scale = MLA_QK ** -0.5

    def body(q_ref, k_ref, v_ref, o_ref, lse_ref, do_ref, dq_ref, dk_ref, dv_ref):
        j = pl.program_id(2)

        @pl.when(j == 0)
        def _():
            dq_ref[...] = jnp.zeros_like(dq_ref)

        masks = _lane_masks()
        rr = lax.broadcasted_iota(jnp.int32, (tq, tk), 0)
        cc = lax.broadcasted_iota(jnp.int32, (tq, tk), 1)
        vb = v_ref[...]
        vh = [jnp.where(m, vb, 0.0).astype(BF16) for m in masks]
        kh = [k_ref[:, h * LANES:(h + 1) * LANES] for h in range(2)]
        i0 = lax.div(j, jnp.int32(per_q))

        def step(t, carry):
            dk0, dk1, dv = carry
            i = i0 + t
            off = pl.multiple_of(i * tq, tq)
            causal = (cc + j * tk) <= (rr + i * tq)
            do_b = do_ref[pl.ds(off, tq), :].astype(BF16)
            prod = do_b.astype(F32) * o_ref[pl.ds(off, tq), :]
            lse = lse_ref[pl.ds(off, tq), :]
            dks = [dk0, dk1]
            for h in range(2):
                qh = q_ref[pl.ds(off, tq), h * LANES:(h + 1) * LANES]
                doh = jnp.where(masks[h], do_b, jnp.zeros_like(do_b))
                delta = jnp.sum(jnp.where(masks[h], prod, 0.0), axis=1, keepdims=True)
                lse_h = lse[:, h * HEAD:h * HEAD + 1]
                s = jnp.where(causal, _dot_nt(qh, kh[h]) * scale, NEG)
                p = jnp.exp(s - lse_h)
                ds = (p * (_dot_nt(doh, vh[h]) - delta) * scale).astype(BF16)
                dq_ref[pl.ds(off, tq), h * LANES:(h + 1) * LANES] += _dot(ds, kh[h])
                dks[h] = dks[h] + _dot_tn(ds, qh)
                dv = dv + _dot_tn(p.astype(BF16), doh)
            return dks[0], dks[1], dv

        zero = jnp.zeros((tk, LANES), F32)
        dk0, dk1, dv = lax.fori_loop(0, nq - i0, step, (zero, zero, zero))
        dk_ref[:, 0:LANES] = dk0
        dk_ref[:, LANES:2 * LANES] = dk1
        dv_ref[...] = dv

    seq1 = pl.BlockSpec((None, S, LANES), lambda b, p, j: (b, 0, p))
    seq2 = pl.BlockSpec((None, S, 2 * LANES), lambda b, p, j: (b, 0, p))
    return pl.pallas_call(
        body, name=name, grid=(Bl, 3, S // tk),
        in_specs=[seq2, pl.BlockSpec((None, tk, 2 * LANES), lambda b, p, j: (b, j, p)),
                  pl.BlockSpec((None, tk, LANES), lambda b, p, j: (b, j, vblk0 + p)), seq1, seq1, seq1],
        out_specs=[seq2, pl.BlockSpec((None, tk, 2 * LANES), lambda b, p, j: (b, j, p)), pl.BlockSpec((None, tk, LANES), lambda b, p, j: (b, j, p))],
        out_shape=[jax.ShapeDtypeStruct((Bl, S, 6 * LANES), F32), jax.ShapeDtypeStruct((Bl, S, 6 * LANES), F32), jax.ShapeDtypeStruct((Bl, S, 3 * LANES), F32)],
        compiler_params=_cp("parallel", "parallel", "arbitrary"),
    )(q3, k3, kv3, o3, lse3, do3)


def _bucket_table():
    a = jnp.arange(WINDOW)[:, None]
    b = jnp.arange(2 * WINDOW)[None, :]
    dist = WINDOW + a - b
    max_exact = REL_BUCKETS // 2
    n = jnp.maximum(dist, 0)
    nf = jnp.maximum(n, 1).astype(F32)
    large = max_exact + (jnp.log(nf / max_exact) / math.log(REL_MAX_DIST / max_exact) * (REL_BUCKETS - max_exact)).astype(jnp.int32)
    large = jnp.minimum(large, REL_BUCKETS - 1)
    bucket = jnp.where(n < max_exact, n, large)
    return jnp.where((dist >= 0) & (dist < WINDOW), bucket, -1).astype(jnp.int32)


def swa_bias(rel_flat, bucket, *, name):
    def body(t_ref, b_ref, o_ref):
        bk = b_ref[...]
        for p in range(3):
            for hh in range(2):
                h = hh * 3 + p
                acc = jnp.full(bk.shape, NEG, F32)
                for b in range(REL_BUCKETS):
                    acc = jnp.where(bk == b, t_ref[b * 6 + h], acc)
                o_ref[p, hh] = acc

    return pl.pallas_call(
        body, name=name,
        in_specs=[pl.BlockSpec(memory_space=pltpu.SMEM), pl.BlockSpec(memory_space=pltpu.VMEM)],
        out_specs=pl.BlockSpec(memory_space=pltpu.VMEM),
        out_shape=jax.ShapeDtypeStruct((3, 2, WINDOW, 2 * WINDOW), F32),
    )(rel_flat, bucket)


def swa_bias_bwd(dbias, bucket, *, name):
    Bl = dbias.shape[0]

    def body(d_ref, b_ref, o_ref):
        bk = b_ref[...]
        lane = lax.broadcasted_iota(jnp.int32, (1, LANES), 1)
        rows = []
        for h in range(6):
            hh, p = divmod(h, 3)
            d = d_ref[0, p, hh]
            for bl in range(1, Bl):
                d = d + d_ref[bl, p, hh]
            row = jnp.zeros((1, LANES), F32)
            for b in range(REL_BUCKETS):
                s = jnp.sum(jnp.sum(jnp.where(bk == b, d, 0.0), axis=1, keepdims=True), axis=0, keepdims=True)
                row = row + jnp.where(lane == b, s, 0.0)
            rows.append(row)
        rows += [jnp.zeros((1, LANES), F32)] * 2
        o_ref[...] = jnp.concatenate(rows, axis=0)

    return pl.pallas_call(
        body, name=name,
        in_specs=[pl.BlockSpec(memory_space=pltpu.VMEM)] * 2, out_specs=pl.BlockSpec(memory_space=pltpu.VMEM),
        out_shape=jax.ShapeDtypeStruct((8, LANES), F32),
    )(dbias, bucket)


SWA_QBLOCKS = 4


def _swa_specs(vblk, nqb):
    rows = nqb * WINDOW
    cur = lambda blk: pl.BlockSpec((None, rows, LANES), lambda b, p, n: (b, n, blk))
    prev = lambda blk: pl.BlockSpec((None, WINDOW, LANES), lambda b, p, n: (b, jnp.maximum(n * nqb - 1, 0), blk))
    return [pl.BlockSpec((None, rows, LANES), lambda b, p, n: (b, n, p)), cur(0), prev(0), cur(vblk), prev(vblk),
            pl.BlockSpec((None, 2, WINDOW, 2 * WINDOW), lambda b, p, n: (p, 0, 0, 0)), pl.BlockSpec((None, 2, LANES), lambda b, p, n: (p, 0, 0))]


def _rows128(ref, m):
    return ref[m * WINDOW:(m + 1) * WINDOW, :]


def _swa_logits(qh, kp, kc, bias_h, first, scale):
    sp = jnp.where(first, NEG, _dot_nt(qh, kp) * scale + bias_h[:, :WINDOW])
    sc = _dot_nt(qh, kc) * scale + bias_h[:, WINDOW:]
    return sp, sc


def swa_attn_fwd(qn3, kn3, proj3, bias, sinks, *, plans=None, name):
    Bl, S, _ = qn3.shape
    scale = HEAD ** -0.5
    nqb = min(SWA_QBLOCKS, S // WINDOW)

    def body(q_ref, kc_ref, kp_ref, vc_ref, vp_ref, b_ref, s_ref, o_ref, lse_ref):
        seq_start = pl.program_id(2) == 0
        masks = _lane_masks()
        for m_ in range(nqb):
            first = seq_start if m_ == 0 else False
            q = _rows128(q_ref, m_)
            kp = kp_ref[...] if m_ == 0 else _rows128(kc_ref, m_ - 1)
            vp = vp_ref[...] if m_ == 0 else _rows128(vc_ref, m_ - 1)
            kc, vc = _rows128(kc_ref, m_), _rows128(vc_ref, m_)
            o = jnp.zeros((WINDOW, LANES), F32)
            lses = []
            for h in range(2):
                qh = jnp.where(masks[h], q, jnp.zeros_like(q))
                sp, sc = _swa_logits(qh, kp, kc, b_ref[h], first, scale)
                sink = s_ref[h:h + 1, 0:1]
                m = jnp.maximum(jnp.maximum(jnp.max(sp, axis=1, keepdims=True), jnp.max(sc, axis=1, keepdims=True)), sink)
                ep, ec = jnp.exp(sp - m), jnp.exp(sc - m)
                l = jnp.sum(ep, axis=1, keepdims=True) + jnp.sum(ec, axis=1, keepdims=True) + jnp.exp(sink - m)
                inv = 1.0 / l
                o = o + _dot((ep * inv).astype(BF16), jnp.where(masks[h], vp, 0.0).astype(BF16))
                o = o + _dot((ec * inv).astype(BF16), jnp.where(masks[h], vc, 0.0).astype(BF16))
                lses.append(m + jnp.log(l))
            o_ref[m_ * WINDOW:(m_ + 1) * WINDOW, :] = o
            lse_ref[m_ * WINDOW:(m_ + 1) * WINDOW, :] = jnp.where(masks[0], lses[0], lses[1])

    out = pl.BlockSpec((None, nqb * WINDOW, LANES), lambda b, p, n: (b, n, p))
    shp = jax.ShapeDtypeStruct((Bl, S, 3 * LANES), F32)
    return call_with_plans(
        body, plans, name=name, grid=(Bl, 3, S // (nqb * WINDOW)), in_specs=_swa_specs(P_SWV // LANES, nqb),
        out_specs=[out, out], out_shape=[shp, shp], scratch_shapes=[], args=[qn3, kn3, kn3, proj3, proj3, bias, sinks],
        sem=("arbitrary",) * 3 if plans else ("parallel", "parallel", "arbitrary"))


def swa_attn_bwd(qn3, kn3, proj3, bias, sinks, o3, lse3, do3, *, name):
    Bl, S, _ = qn3.shape
    scale = HEAD ** -0.5
    nqb = min(SWA_QBLOCKS, S // WINDOW)
    rows = nqb * WINDOW

    def body(q_ref, kc_ref, kp_ref, vc_ref, vp_ref, b_ref, s_ref, o_ref, lse_ref, do_ref,
             dq_ref, dk_ref, dv_ref, db_ref, dsk_ref):
        p_id, n = pl.program_id(1), pl.program_id(2)
        seq_start = n == 0

        @pl.when((p_id == 0) & seq_start)
        def _():
            dk_ref[...] = jnp.zeros_like(dk_ref)
            dv_ref[...] = jnp.zeros_like(dv_ref)

        @pl.when(seq_start)
        def _():
            db_ref[...] = jnp.zeros_like(db_ref)
            dsk_ref[...] = jnp.zeros_like(dsk_ref)

        masks = _lane_masks()
        zero = jnp.zeros((WINDOW, LANES), F32)
        dk_acc = [zero] * (nqb + 1)
        dv_acc = [zero] * (nqb + 1)
        db_acc = [[jnp.zeros((WINDOW, WINDOW), F32)] * 2 for _ in range(2)]
        dsk_acc = [jnp.zeros((1, 1), F32)] * 2
        for m_ in range(nqb):
            first = seq_start if m_ == 0 else False
            q = _rows128(q_ref, m_)
            kp = kp_ref[...] if m_ == 0 else _rows128(kc_ref, m_ - 1)
            vp = vp_ref[...] if m_ == 0 else _rows128(vc_ref, m_ - 1)
            kc, vc = _rows128(kc_ref, m_), _rows128(vc_ref, m_)
            do_b = _rows128(do_ref, m_).astype(BF16)
            prod = do_b.astype(F32) * _rows128(o_ref, m_)
            lse = _rows128(lse_ref, m_)
            dq = zero
            for h in range(2):
                qh = jnp.where(masks[h], q, jnp.zeros_like(q))
                doh = jnp.where(masks[h], do_b, jnp.zeros_like(do_b))
                sp, sc = _swa_logits(qh, kp, kc, b_ref[h], first, scale)
                lse_h = lse[:, h * HEAD:h * HEAD + 1]
                pp, pc = jnp.exp(sp - lse_h), jnp.exp(sc - lse_h)
                delta = jnp.sum(jnp.where(masks[h], prod, 0.0), axis=1, keepdims=True)
                dsp = pp * (_dot_nt(doh, jnp.where(masks[h], vp, 0.0).astype(BF16)) - delta)
                dsc = pc * (_dot_nt(doh, jnp.where(masks[h], vc, 0.0).astype(BF16)) - delta)
                db_acc[h] = [db_acc[h][0] + dsp, db_acc[h][1] + dsc]
                psink = jnp.exp(s_ref[h:h + 1, 0:1] - lse_h)
                dsk_acc[h] = dsk_acc[h] - jnp.sum(psink * delta, axis=0, keepdims=True)
                dspb, dscb = (dsp * scale).astype(BF16), (dsc * scale).astype(BF16)
                dq = dq + _dot(dspb, jnp.where(masks[h], kp, jnp.zeros_like(kp))) + _dot(dscb, jnp.where(masks[h], kc, jnp.zeros_like(kc)))
                dk_acc[m_] = dk_acc[m_] + _dot_tn(dspb, qh)
                dk_acc[m_ + 1] = dk_acc[m_ + 1] + _dot_tn(dscb, qh)
                dv_acc[m_] = dv_acc[m_] + _dot_tn(pp.astype(BF16), doh)
                dv_acc[m_ + 1] = dv_acc[m_ + 1] + _dot_tn(pc.astype(BF16), doh)
            dq_ref[m_ * WINDOW:(m_ + 1) * WINDOW, :] = dq
        for h in range(2):
            db_ref[h, :, 0:WINDOW] += db_acc[h][0]
            db_ref[h, :, WINDOW:2 * WINDOW] += db_acc[h][1]
            dsk_ref[h:h + 1, :] += jnp.broadcast_to(dsk_acc[h], (1, LANES))
        offp = pl.multiple_of(jnp.maximum(n * nqb - 1, 0) * WINDOW, WINDOW)
        dk_ref[pl.ds(offp, WINDOW), :] += dk_acc[0]
        dv_ref[pl.ds(offp, WINDOW), :] += dv_acc[0]
        for m_ in range(nqb):
            off = pl.multiple_of(n * rows + m_ * WINDOW, WINDOW)
            dk_ref[pl.ds(off, WINDOW), :] += dk_acc[m_ + 1]
            dv_ref[pl.ds(off, WINDOW), :] += dv_acc[m_ + 1]

    blk = pl.BlockSpec((None, rows, LANES), lambda b, p, n: (b, n, p))
    seq = pl.BlockSpec((None, S, LANES), lambda b, p, n: (b, 0, 0))
    return pl.pallas_call(
        body, name=name, grid=(Bl, 3, S // rows), in_specs=_swa_specs(P_SWV // LANES, nqb) + [blk, blk, blk],
        out_specs=[blk, seq, seq, pl.BlockSpec((None, None, 2, WINDOW, 2 * WINDOW), lambda b, p, n: (b, p, 0, 0, 0)),
                   pl.BlockSpec((None, None, 2, LANES), lambda b, p, n: (b, p, 0, 0))],
        out_shape=[jax.ShapeDtypeStruct((Bl, S, 3 * LANES), F32), jax.ShapeDtypeStruct((Bl, S, LANES), F32), jax.ShapeDtypeStruct((Bl, S, LANES), F32),
                   jax.ShapeDtypeStruct((Bl, 3, 2, WINDOW, 2 * WINDOW), F32), jax.ShapeDtypeStruct((Bl, 3, 2, LANES), F32)],
        compiler_params=_cp("arbitrary", "arbitrary", "arbitrary"),
    )(qn3, kn3, kn3, proj3, proj3, bias, sinks, o3, lse3, do3)


def _conv_rows(x, halo, w_ref, b_ref, first_blk):
    rows = lax.broadcasted_iota(jnp.int32, x.shape, 0)
    h6 = jnp.where(first_blk, 0.0, halo[6:7, :])
    h7 = jnp.where(first_blk, 0.0, halo[7:8, :])
    x1 = jnp.where(rows == 0, h7, pltpu.roll(x, 1, 0))
    x2 = jnp.where(rows == 0, h6, jnp.where(rows == 1, h7, pltpu.roll(x, 2, 0)))
    return w_ref[0:1, :] * x2 + w_ref[1:2, :] * x1 + w_ref[2:3, :] * x + b_ref[...], x1, x2


FF_BLK = D_FF // 2


def _up_perm(a):
    q = FF_BLK
    return _cat([a[..., 0:q], a[..., 2 * q:3 * q], a[..., q:2 * q], a[..., 3 * q:4 * q]])


def conv_gate_fwd(up3, cw, cb, *, tm=256, name):
    Bl, S, _ = up3.shape
    tm = min(tm, S)
    W = 2 * FF_BLK

    def body(x_ref, h_ref, w_ref, b_ref, o_ref):
        u, _, _ = _conv_rows(x_ref[...], h_ref[...], w_ref, b_ref, pl.program_id(1) == 0)
        ug, uv = u[:, :FF_BLK], u[:, FF_BLK:]
        o_ref[...] = (ug * jax.nn.sigmoid(ug) * uv).astype(BF16)

    hb = tm // 8
    return pl.pallas_call(
        body, name=name, grid=(Bl, S // tm, 2),
        in_specs=[pl.BlockSpec((None, tm, W), lambda b, s, c: (b, s, c)),
                  pl.BlockSpec((None, 8, W), lambda b, s, c: (b, jnp.maximum(s * hb - 1, 0), c)),
                  pl.BlockSpec((3, W), lambda b, s, c: (0, c)), pl.BlockSpec((1, W), lambda b, s, c: (0, c))],
        out_specs=pl.BlockSpec((None, tm, FF_BLK), lambda b, s, c: (b, s, c)),
        out_shape=jax.ShapeDtypeStruct((Bl, S, D_FF), BF16),
        compiler_params=_cp("parallel", "parallel", "parallel"),
    )(up3, up3, cw, cb)


def conv_gate_bwd(up3, cw, cb, da3, *, tm=256, name):
    Bl, S, _ = up3.shape
    tm = min(tm, S)
    ns = S // tm
    W = 2 * FF_BLK

    def body(x_ref, h_ref, w_ref, b_ref, da_ref, dup_ref, dw_ref, nxt_ref):
        b, s = pl.program_id(1), pl.program_id(2)
        seq_end = s == 0

        @pl.when((b == 0) & seq_end)
        def _():
            dw_ref[...] = jnp.zeros_like(dw_ref)

        x = x_ref[...]
        u, x1, x2 = _conv_rows(x, h_ref[...], w_ref, b_ref, s == ns - 1)
        ug, uv = u[:, :FF_BLK], u[:, FF_BLK:]
        da = da_ref[...].astype(F32)
        sg = jax.nn.sigmoid(ug)
        du = _cat([da * uv * sg * (1.0 + ug * (1.0 - sg)), da * ug * sg])
        dw_ref[0:1, :] += jnp.sum(du * x2, axis=0, keepdims=True)
        dw_ref[1:2, :] += jnp.sum(du * x1, axis=0, keepdims=True)
        dw_ref[2:3, :] += jnp.sum(du * x, axis=0, keepdims=True)
        dw_ref[3:4, :] += jnp.sum(du, axis=0, keepdims=True)
        rows = lax.broadcasted_iota(jnp.int32, du.shape, 0)
        n0 = jnp.where(seq_end, 0.0, nxt_ref[0:1, :])
        n1 = jnp.where(seq_end, 0.0, nxt_ref[1:2, :])
        d1 = jnp.where(rows == tm - 1, n0, pltpu.roll(du, tm - 1, 0))
        d2 = jnp.where(rows == tm - 1, n1, jnp.where(rows == tm - 2, n0, pltpu.roll(du, tm - 2, 0)))
        dup_ref[...] = (w_ref[2:3, :] * du + w_ref[1:2, :] * d1 + w_ref[0:1, :] * d2).astype(BF16)
        nxt_ref[...] = du[0:8, :]

    hb = tm // 8
    rb = lambda s: ns - 1 - s
    return pl.pallas_call(
        body, name=name, grid=(2, Bl, ns),
        in_specs=[pl.BlockSpec((None, tm, W), lambda c, b, s: (b, rb(s), c)),
                  pl.BlockSpec((None, 8, W), lambda c, b, s: (b, jnp.maximum(rb(s) * hb - 1, 0), c)),
                  pl.BlockSpec((3, W), lambda c, b, s: (0, c)), pl.BlockSpec((1, W), lambda c, b, s: (0, c)),
                  pl.BlockSpec((None, tm, FF_BLK), lambda c, b, s: (b, rb(s), c))],
        out_specs=[pl.BlockSpec((None, tm, W), lambda c, b, s: (b, rb(s), c)), pl.BlockSpec((8, W), lambda c, b, s: (0, c))],
        out_shape=[jax.ShapeDtypeStruct((Bl, S, 2 * D_FF), BF16), jax.ShapeDtypeStruct((8, 2 * D_FF), F32)],
        scratch_shapes=[pltpu.VMEM((8, W), F32)],
        compiler_params=_cp("arbitrary", "arbitrary", "arbitrary"),
    )(up3, up3, cw, cb, da3)


def gate_bwd(dx3, y3, gate, *, tm=512, name):
    Bl, S, D = dx3.shape
    tm = min(tm, S)

    def body(dx_ref, y_ref, g_ref, o_ref, dg_ref):
        @pl.when(pl.program_id(1) == 0)
        def _():
            dg_ref[...] = jnp.zeros_like(dg_ref)

        dx = dx_ref[...]
        dg_ref[...] += jnp.sum(dx * y_ref[...], axis=0, keepdims=True)
        o_ref[...] = (dx * g_ref[...]).astype(BF16)

    blk = pl.BlockSpec((None, tm, D), lambda b, s: (b, s, 0))
    vec = pl.BlockSpec((None, 1, D), lambda b, s: (b, 0, 0))
    return pl.pallas_call(
        body, name=name, grid=(Bl, S // tm), in_specs=[blk, blk, vec], out_specs=[blk, vec],
        out_shape=[jax.ShapeDtypeStruct((Bl, S, D), BF16), jax.ShapeDtypeStruct((Bl, 1, D), F32)],
        compiler_params=_cp("parallel", "arbitrary"),
    )(dx3, y3, gate)


def loss_grad(y3, t3, *, tm=512, name):
    Bl, S, D = y3.shape
    tm = min(tm, S)
    last = (Bl - 1, S // tm - 1)

    def body(y_ref, t_ref, dy_ref, l_ref, acc_ref):
        b, s = pl.program_id(0), pl.program_id(1)

        @pl.when((b == 0) & (s == 0))
        def _():
            acc_ref[...] = jnp.zeros_like(acc_ref)

        e = y_ref[...] - t_ref[...]
        dy_ref[...] = e * (1.0 / D)
        acc_ref[...] += jnp.sum(e * e, axis=0, keepdims=True)

        @pl.when((b == last[0]) & (s == last[1]))
        def _():
            l_ref[...] = jnp.broadcast_to(jnp.sum(acc_ref[...], axis=1, keepdims=True) * (0.5 / D), (1, LANES))

    blk = pl.BlockSpec((None, tm, D), lambda b, s: (b, s, 0))
    return pl.pallas_call(
        body, name=name, grid=(Bl, S // tm), in_specs=[blk, blk],
        out_specs=[blk, pl.BlockSpec((1, LANES), lambda b, s: (0, 0))],
        out_shape=[jax.ShapeDtypeStruct((Bl, S, D), F32), jax.ShapeDtypeStruct((1, LANES), F32)],
        scratch_shapes=[pltpu.VMEM((1, D), F32)], compiler_params=_cp("arbitrary", "arbitrary"),
    )(y3, t3)


def adamw(w, g, m, v, *, name):
    R, C = w.shape
    tr = R
    for cand in (512, 256, 128, 64, 32, 16, 8):
        if R > cand and R % cand == 0:
            tr = cand
            break
    c1 = 1.0 / (1.0 - ADAM_B1 ** ADAM_STEP)
    c2 = 1.0 / (1.0 - ADAM_B2 ** ADAM_STEP)

    def body(w_ref, g_ref, m_ref, v_ref, d_ref, m2_ref, v2_ref):
        gg = g_ref[...]
        m2 = ADAM_B1 * m_ref[...] + (1.0 - ADAM_B1) * gg
        v2 = ADAM_B2 * v_ref[...] + (1.0 - ADAM_B2) * (gg * gg)
        m2_ref[...] = m2
        v2_ref[...] = v2
        d_ref[...] = -ADAM_LR * ((m2 * c1) / (jnp.sqrt(v2 * c2) + ADAM_EPS) + ADAM_WD * w_ref[...])

    blk = pl.BlockSpec((tr, C), lambda i: (i, 0))
    shp = jax.ShapeDtypeStruct((R, C), F32)
    return pl.pallas_call(
        body, name=name, grid=(R // tr,), in_specs=[blk] * 4, out_specs=[blk] * 3, out_shape=[shp] * 3,
        compiler_params=_cp("parallel"),
    )(w, g, m, v)


def sum_leading(x, *, out_dtype=F32, tr=256, name):
    n, R, C = x.shape
    tr = _tile(R, tr, 16)

    def body(x_ref, o_ref):
        acc = x_ref[0].astype(F32)
        for k in range(1, n):
            acc = acc + x_ref[k].astype(F32)
        o_ref[...] = acc.astype(out_dtype)

    return pl.pallas_call(
        body, name=name, grid=(R // tr,), in_specs=[pl.BlockSpec((n, tr, C), lambda i: (0, i, 0))],
        out_specs=pl.BlockSpec((tr, C), lambda i: (i, 0)), out_shape=jax.ShapeDtypeStruct((R, C), out_dtype),
        compiler_params=_cp("parallel"),
    )(x)


def _adam_update(w, g, m, v):
    c1 = 1.0 / (1.0 - ADAM_B1 ** ADAM_STEP)
    c2 = 1.0 / (1.0 - ADAM_B2 ** ADAM_STEP)
    m2 = ADAM_B1 * m + (1.0 - ADAM_B1) * g
    v2 = ADAM_B2 * v + (1.0 - ADAM_B2) * (g * g)
    return -ADAM_LR * ((m2 * c1) / (jnp.sqrt(v2 * c2) + ADAM_EPS) + ADAM_WD * w), m2, v2


def adamw_small(ws, gs, ms, vs, *, name):
    na = len(ws)

    def body(*refs):
        w_r, g_r, m_r, v_r = (refs[i * na:(i + 1) * na] for i in range(4))
        d_r, m2_r, v2_r = (refs[(4 + i) * na:(5 + i) * na] for i in range(3))
        for a in range(na):
            d_r[a][...], m2_r[a][...], v2_r[a][...] = _adam_update(w_r[a][...], g_r[a][...], m_r[a][...], v_r[a][...])

    vm = pl.BlockSpec(memory_space=pltpu.VMEM)
    shp = [jax.ShapeDtypeStruct(w.shape, F32) for w in ws]
    out = pl.pallas_call(body, name=name, in_specs=[vm] * (4 * na), out_specs=[vm] * (3 * na), out_shape=shp * 3)(*ws, *gs, *ms, *vs)
    return out[:na], out[na:2 * na], out[2 * na:]


def sum_small(xs, *, name):
    na = len(xs)

    def body(*refs):
        for x_ref, o_ref in zip(refs[:na], refs[na:]):
            acc = x_ref[0]
            for k in range(1, x_ref.shape[0]):
                acc = acc + x_ref[k]
            o_ref[...] = acc

    vm = pl.BlockSpec(memory_space=pltpu.VMEM)
    return pl.pallas_call(body, name=name, in_specs=[vm] * na, out_specs=[vm] * na,
                          out_shape=[jax.ShapeDtypeStruct(x.shape[1:], x.dtype) for x in xs])(*xs)


def pair_add_half(g4, recv, c_arr, *, tr=512, name):
    _, R, C = g4.shape
    H = R // 2
    tr = _tile(H, tr, 16)
    nb = H // tr

    def body(c_ref, g_ref, r_ref, o_ref):
        o_ref[...] = (g_ref[...].astype(F32) + r_ref[...].astype(F32)).astype(BF16)

    grid_spec = pltpu.PrefetchScalarGridSpec(
        num_scalar_prefetch=1, grid=(4, nb),
        in_specs=[pl.BlockSpec((None, tr, C), lambda k, i, c_ref: (k, c_ref[0] * nb + i, 0)),
                  pl.BlockSpec((None, tr, C), lambda k, i, c_ref: (k, i, 0))],
        out_specs=pl.BlockSpec((None, tr, C), lambda k, i, c_ref: (k, i, 0)),
    )
    return pl.pallas_call(
        body, name=name, grid_spec=grid_spec, out_shape=jax.ShapeDtypeStruct((4, H, C), BF16),
        compiler_params=_cp("parallel", "parallel"),
    )(c_arr, g4, recv)


def chip_sum_into(landed, pair, sel, *, tr=512, name):
    _, H, C = landed.shape
    tr = _tile(H, tr, 16)
    nb = H // tr

    def body(s_ref, l0, l1, l2, l3, p_ref, o_ref):
        own = p_ref[...].astype(F32)
        acc = None
        for k, l_ref in enumerate((l0, l1, l2, l3)):
            part = jnp.where(s_ref[0] == k, own, l_ref[...].astype(F32))
            acc = part if acc is None else acc + part
        o_ref[...] = acc

    def slot(k):
        return pl.BlockSpec((None, tr, C), lambda i, s: (jnp.where(s[0] == k, (k + 1) % 4, k), i, 0))

    grid_spec = pltpu.PrefetchScalarGridSpec(
        num_scalar_prefetch=1, grid=(nb,),
        in_specs=[slot(0), slot(1), slot(2), slot(3), pl.BlockSpec((None, tr, C), lambda i, s: (s[0], i, 0))],
        out_specs=pl.BlockSpec((tr, C), lambda i, s: (s[1] * nb + i, 0)),
    )
    return pl.pallas_call(
        body, name=name, grid_spec=grid_spec, out_shape=jax.ShapeDtypeStruct((2 * H, C), F32), compiler_params=_cp("parallel"),
    )(sel, landed, landed, landed, landed, pair)


def mods_matmul(c_all, w_ada, b_ada_cols, *, tn=512, name):
    L, D, E = w_ada.shape
    nb = c_all.shape[0]
    tn = _tile(E, tn)

    def body(c_ref, w_ref, b_ref, o_ref):
        c = c_ref[...]
        a = c * jax.nn.sigmoid(c)
        o_ref[...] = jnp.dot(a, w_ref[...], preferred_element_type=F32, precision=lax.Precision.HIGHEST) + b_ref[...]

    return pl.pallas_call(
        body, name=name, grid=(L, E // tn),
        in_specs=[pl.BlockSpec((nb, D), lambda l, j: (0, 0)), pl.BlockSpec((None, D, tn), lambda l, j: (l, 0, j)),
                  pl.BlockSpec((None, 1, tn), lambda l, j: (l, 0, j))],
        out_specs=pl.BlockSpec((None, nb, tn), lambda l, j: (l, 0, j)),
        out_shape=jax.ShapeDtypeStruct((L, nb, E), F32), compiler_params=_cp("parallel", "parallel"),
    )(c_all, w_ada, b_ada_cols)


def ada_grad(c_all, dmods, *, tn=512, name):
    L, nb, E = dmods.shape
    D = c_all.shape[1]
    tn = _tile(E, tn)

    def body(c_ref, d_ref, o_ref):
        c = c_ref[...]
        a = c * jax.nn.sigmoid(c)
        o_ref[...] = lax.dot_general(a, d_ref[...], (((0,), (0,)), ((), ())), preferred_element_type=F32, precision=lax.Precision.HIGHEST)

    return pl.pallas_call(
        body, name=name, grid=(L, E // tn),
        in_specs=[pl.BlockSpec((nb, D), lambda l, j: (0, 0)), pl.BlockSpec((None, nb, tn), lambda l, j: (l, 0, j))],
        out_specs=pl.BlockSpec((None, D, tn), lambda l, j: (l, 0, j)),
        out_shape=jax.ShapeDtypeStruct((L, D, E), F32), compiler_params=_cp("parallel", "parallel"),
    )(c_all, dmods)


HBM = pl.BlockSpec(memory_space=pltpu.HBM)


def _me():
    return lax.axis_index("x"), lax.axis_index("y"), lax.axis_index("c")


def _flip(v, bit):
    return 1 - v if bit else v


def allgather8(xs, *, name):
    na = len(xs)

    def body(*refs):
        x_refs, out_refs = refs[:na], refs[na:2 * na]
        send_sems, recv_sems = refs[2 * na], refs[2 * na + 1]
        x, y, c = _me()
        me = 4 * x + 2 * y + c
        for x_ref, out_ref in zip(x_refs, out_refs):
            out_ref[me] = x_ref[...]
        sends = []
        for a, (x_ref, out_ref) in enumerate(zip(x_refs, out_refs)):
            for k in range(1, 8):
                peer = (_flip(x, k & 4), _flip(y, k & 2), _flip(c, k & 1))
                cp = pltpu.make_async_remote_copy(src_ref=x_ref, dst_ref=out_ref.at[me], send_sem=send_sems.at[a, k - 1],
                                                  recv_sem=recv_sems.at[a, k - 1], device_id=peer, device_id_type=MESH)
                cp.start()
                sends.append(cp)
        for a, (x_ref, out_ref) in enumerate(zip(x_refs, out_refs)):
            for k in range(1, 8):
                peer = (_flip(x, k & 4), _flip(y, k & 2), _flip(c, k & 1))
                src = 4 * peer[0] + 2 * peer[1] + peer[2]
                pltpu.make_async_remote_copy(src_ref=x_ref, dst_ref=out_ref.at[src], send_sem=send_sems.at[a, k - 1],
                                             recv_sem=recv_sems.at[a, k - 1], device_id=peer, device_id_type=MESH).wait_recv()
        for cp in sends:
            cp.wait_send()

    vm = pl.BlockSpec(memory_space=pltpu.VMEM)
    return pl.pallas_call(
        body, name=name, in_specs=[vm] * na, out_specs=[vm] * na,
        out_shape=[jax.ShapeDtypeStruct((8,) + a.shape, a.dtype) for a in xs],
        scratch_shapes=[pltpu.SemaphoreType.DMA((na, 7)), pltpu.SemaphoreType.DMA((na, 7))],
    )(*xs)


LOCAL_CHUNKS = 8


def _copy_via_vmem(src, dst_at, rows, buf, sem):
    ch = buf.shape[0]
    for i in range(rows // ch):
        load = pltpu.make_async_copy(src.at[pl.ds(i * ch, ch)], buf, sem)
        load.start()
        load.wait()
        store = pltpu.make_async_copy(buf, dst_at(i * ch, ch), sem)
        store.start()
        store.wait()


def _chunk_buf(rows, cols, dtype):
    align = 16 if dtype == BF16 else 8
    for n in range(LOCAL_CHUNKS, 0, -1):
        if rows % n == 0 and (rows // n) % align == 0:
            return pltpu.VMEM((rows // n, cols), dtype)
    return pltpu.VMEM((rows, cols), dtype)


def gather_weights(ws, *, name):
    na = len(ws)

    def body(*refs):
        x_refs, out_refs = refs[:na], refs[na:2 * na]
        send_sems, recv_sems, local_sem = refs[2 * na:2 * na + 3]
        bufs = refs[2 * na + 3:]
        x, y, c = _me()
        j = 2 * x + y
        chips = [(_flip(x, k & 2), _flip(y, k & 1)) for k in range(1, 4)]
        sends = []
        for a, (x_ref, out_ref) in enumerate(zip(x_refs, out_refs)):
            H = x_ref.shape[0] // 2
            for k, (px, py) in enumerate(chips):
                cp = pltpu.make_async_remote_copy(src_ref=x_ref.at[pl.ds(c * H, H)], dst_ref=out_ref.at[j, pl.ds(c * H, H)],
                                                  send_sem=send_sems.at[a, k], recv_sem=recv_sems.at[a, k],
                                                  device_id=(px, py, c), device_id_type=MESH)
                cp.start()
                sends.append(cp)
        for x_ref, out_ref, buf in zip(x_refs, out_refs, bufs):
            _copy_via_vmem(x_ref, lambda o, n, out_ref=out_ref: out_ref.at[j, pl.ds(o, n)], x_ref.shape[0], buf, local_sem)
        for a, out_ref in enumerate(out_refs):
            H = out_ref.shape[1] // 2
            for k, (px, py) in enumerate(chips):
                slot = out_ref.at[2 * px + py, pl.ds(c * H, H)]
                pltpu.make_async_remote_copy(src_ref=slot, dst_ref=slot, send_sem=send_sems.at[a, k], recv_sem=recv_sems.at[a, k],
                                             device_id=(px, py, c), device_id_type=MESH).wait_recv()
                cp = pltpu.make_async_remote_copy(src_ref=slot, dst_ref=slot, send_sem=send_sems.at[a, 3 + k],
                                                  recv_sem=recv_sems.at[a, 3 + k], device_id=(x, y, 1 - c), device_id_type=MESH)
                cp.start()
                sends.append(cp)
        for a, out_ref in enumerate(out_refs):
            H = out_ref.shape[1] // 2
            for k, (px, py) in enumerate(chips):
                slot = out_ref.at[2 * px + py, pl.ds((1 - c) * H, H)]
                pltpu.make_async_remote_copy(src_ref=slot, dst_ref=slot, send_sem=send_sems.at[a, 3 + k], recv_sem=recv_sems.at[a, 3 + k],
                                             device_id=(x, y, 1 - c), device_id_type=MESH).wait_recv()
        for cp in sends:
            cp.wait_send()

    return pl.pallas_call(
        body, name=name, in_specs=[HBM] * na, out_specs=[HBM] * na,
        out_shape=[jax.ShapeDtypeStruct((4,) + w.shape, w.dtype) for w in ws],
        scratch_shapes=[pltpu.SemaphoreType.DMA((na, 6)), pltpu.SemaphoreType.DMA((na, 6)), pltpu.SemaphoreType.DMA]
        + [_chunk_buf(w.shape[0], w.shape[1], w.dtype) for w in ws],
    )(*ws)


def swap_halves(gs, *, name):
    na = len(gs)

    def body(*refs):
        g_refs, out_refs = refs[:na], refs[na:2 * na]
        send_sems, recv_sems = refs[2 * na:]
        x, y, c = _me()
        sib = (x, y, 1 - c)
        sends = []
        for a, (g_ref, out_ref) in enumerate(zip(g_refs, out_refs)):
            H = g_ref.shape[1] // 2
            for k in range(4):
                cp = pltpu.make_async_remote_copy(src_ref=g_ref.at[k, pl.ds((1 - c) * H, H)], dst_ref=out_ref.at[k],
                                                  send_sem=send_sems.at[a, k], recv_sem=recv_sems.at[a, k], device_id=sib, device_id_type=MESH)
                cp.start()
                sends.append(cp)
        for a, (g_ref, out_ref) in enumerate(zip(g_refs, out_refs)):
            H = g_ref.shape[1] // 2
            for k in range(4):
                pltpu.make_async_remote_copy(src_ref=g_ref.at[k, pl.ds(c * H, H)], dst_ref=out_ref.at[k], send_sem=send_sems.at[a, k],
                                             recv_sem=recv_sems.at[a, k], device_id=sib, device_id_type=MESH).wait_recv()
        for cp in sends:
            cp.wait_send()

    return pl.pallas_call(
        body, name=name, in_specs=[HBM] * na, out_specs=[HBM] * na,
        out_shape=[jax.ShapeDtypeStruct((4, g.shape[1] // 2, g.shape[2]), g.dtype) for g in gs],
        scratch_shapes=[pltpu.SemaphoreType.DMA((na, 4)), pltpu.SemaphoreType.DMA((na, 4))],
    )(*gs)


def scatter_chips(ps, *, name):
    na = len(ps)

    def body(*refs):
        p_refs, out_refs = refs[:na], refs[na:2 * na]
        send_sems, recv_sems, local_sem = refs[2 * na:2 * na + 3]
        bufs = refs[2 * na + 3:]
        x, y, c = _me()
        j = 2 * x + y
        chips = [(_flip(x, k & 2), _flip(y, k & 1)) for k in range(1, 4)]
        sends = []
        for a, (p_ref, out_ref) in enumerate(zip(p_refs, out_refs)):
            for k, (px, py) in enumerate(chips):
                cp = pltpu.make_async_remote_copy(src_ref=p_ref.at[2 * px + py], dst_ref=out_ref.at[j], send_sem=send_sems.at[a, k],
                                                  recv_sem=recv_sems.at[a, k], device_id=(px, py, c), device_id_type=MESH)
                cp.start()
                sends.append(cp)
        for p_ref, out_ref, buf in zip(p_refs, out_refs, bufs):
            _copy_via_vmem(p_ref.at[j], lambda o, n, out_ref=out_ref: out_ref.at[j, pl.ds(o, n)], p_ref.shape[1], buf, local_sem)
        for a, out_ref in enumerate(out_refs):
            for k, (px, py) in enumerate(chips):
                slot = out_ref.at[2 * px + py]
                pltpu.make_async_remote_copy(src_ref=slot, dst_ref=slot, send_sem=send_sems.at[a, k], recv_sem=recv_sems.at[a, k],
                                             device_id=(px, py, c), device_id_type=MESH).wait_recv()
        for cp in sends:
            cp.wait_send()

    return pl.pallas_call(
        body, name=name, in_specs=[HBM] * na, out_specs=[HBM] * na, out_shape=[jax.ShapeDtypeStruct(p.shape, p.dtype) for p in ps],
        scratch_shapes=[pltpu.SemaphoreType.DMA((na, 3)), pltpu.SemaphoreType.DMA((na, 3)), pltpu.SemaphoreType.DMA]
        + [_chunk_buf(p.shape[1], p.shape[2], p.dtype) for p in ps],
    )(*ps)


def join_halves(halves, *, name):
    na = len(halves)

    def body(*refs):
        h_refs, out_refs = refs[:na], refs[na:2 * na]
        send_sems, recv_sems, local_sem = refs[2 * na:2 * na + 3]
        bufs = refs[2 * na + 3:]
        x, y, c = _me()
        sib = (x, y, 1 - c)
        sends = []
        for a, (h_ref, out_ref) in enumerate(zip(h_refs, out_refs)):
            H = h_ref.shape[0]
            cp = pltpu.make_async_remote_copy(src_ref=h_ref, dst_ref=out_ref.at[pl.ds(c * H, H)], send_sem=send_sems.at[a],
                                              recv_sem=recv_sems.at[a], device_id=sib, device_id_type=MESH)
            cp.start()
            sends.append(cp)
        for h_ref, out_ref, buf in zip(h_refs, out_refs, bufs):
            H = h_ref.shape[0]
            _copy_via_vmem(h_ref, lambda o, n, out_ref=out_ref, H=H: out_ref.at[pl.ds(c * H + o, n)], H, buf, local_sem)
        for a, (h_ref, out_ref) in enumerate(zip(h_refs, out_refs)):
            H = h_ref.shape[0]
            pltpu.make_async_remote_copy(src_ref=h_ref, dst_ref=out_ref.at[pl.ds((1 - c) * H, H)], send_sem=send_sems.at[a],
                                         recv_sem=recv_sems.at[a], device_id=sib, device_id_type=MESH).wait_recv()
        for cp in sends:
            cp.wait_send()

    return pl.pallas_call(
        body, name=name, in_specs=[HBM] * na, out_specs=[HBM] * na,
        out_shape=[jax.ShapeDtypeStruct((2 * h.shape[0], h.shape[1]), h.dtype) for h in halves],
        scratch_shapes=[pltpu.SemaphoreType.DMA((na,)), pltpu.SemaphoreType.DMA((na,)), pltpu.SemaphoreType.DMA]
        + [_chunk_buf(h.shape[0], h.shape[1], h.dtype) for h in halves],
    )(*halves)


class _Plan:
    def __init__(self, ins, out_shapes, ncopies, copies, aliased=False):
        self.ins, self.out_shapes, self.ncopies, self.copies, self.aliased = list(ins), list(out_shapes), ncopies, copies, aliased

    def start(self, in_refs, out_refs, send_sems, recv_sems):
        sends, _ = self.copies(in_refs, out_refs, send_sems, recv_sems)
        for cp in sends:
            cp.start()

    def finish(self, in_refs, out_refs, send_sems, recv_sems):
        sends, recvs = self.copies(in_refs, out_refs, send_sems, recv_sems)
        for cp in recvs:
            cp.wait_recv()
        for cp in sends:
            cp.wait_send()


def _rcopy(src, dst, send_sems, recv_sems, idx, dev):
    return pltpu.make_async_remote_copy(src_ref=src, dst_ref=dst, send_sem=send_sems.at[idx], recv_sem=recv_sems.at[idx],
                                        device_id=dev, device_id_type=MESH)


def _other_chips(x, y):
    return [(_flip(x, k & 2), _flip(y, k & 1)) for k in range(1, 4)]


def plan_gather_ici(ws):
    def copies(in_refs, out_refs, ss, rs):
        x, y, c = _me()
        j = 2 * x + y
        sends, recvs = [], []
        for a, (x_ref, out_ref) in enumerate(zip(in_refs, out_refs)):
            H = x_ref.shape[0] // 2
            for k, (px, py) in enumerate(_other_chips(x, y)):
                sends.append(_rcopy(x_ref.at[pl.ds(c * H, H)], out_ref.at[j, pl.ds(c * H, H)], ss, rs, 3 * a + k, (px, py, c)))
                slot = out_ref.at[2 * px + py, pl.ds(c * H, H)]
                recvs.append(_rcopy(slot, slot, ss, rs, 3 * a + k, (px, py, c)))
        return sends, recvs

    return _Plan(ws, [jax.ShapeDtypeStruct((4,) + w.shape, w.dtype) for w in ws], 3 * len(ws), copies)


def plan_gather_d2d(w4s):
    def copies(in_refs, out_refs, ss, rs):
        x, y, c = _me()
        sends, recvs = [], []
        for a, out_ref in enumerate(out_refs):
            H = out_ref.shape[1] // 2
            for k, (px, py) in enumerate(_other_chips(x, y)):
                mine = out_ref.at[2 * px + py, pl.ds(c * H, H)]
                theirs = out_ref.at[2 * px + py, pl.ds((1 - c) * H, H)]
                sends.append(_rcopy(mine, mine, ss, rs, 3 * a + k, (x, y, 1 - c)))
                recvs.append(_rcopy(theirs, theirs, ss, rs, 3 * a + k, (x, y, 1 - c)))
        return sends, recvs

    return _Plan(w4s, [jax.ShapeDtypeStruct(w.shape, w.dtype) for w in w4s], 3 * len(w4s), copies, aliased=True)


def plan_swap_halves(gs):
    def copies(in_refs, out_refs, ss, rs):
        x, y, c = _me()
        sends, recvs = [], []
        for a, (g_ref, out_ref) in enumerate(zip(in_refs, out_refs)):
            H = g_ref.shape[1] // 2
            for k in range(4):
                sends.append(_rcopy(g_ref.at[k, pl.ds((1 - c) * H, H)], out_ref.at[k], ss, rs, 4 * a + k, (x, y, 1 - c)))
                recvs.append(_rcopy(g_ref.at[k, pl.ds(c * H, H)], out_ref.at[k], ss, rs, 4 * a + k, (x, y, 1 - c)))
        return sends, recvs

    return _Plan(gs, [jax.ShapeDtypeStruct((4, g.shape[1] // 2, g.shape[2]), g.dtype) for g in gs], 4 * len(gs), copies)


def plan_scatter_ici(ps):
    def copies(in_refs, out_refs, ss, rs):
        x, y, c = _me()
        j = 2 * x + y
        sends, recvs = [], []
        for a, (p_ref, out_ref) in enumerate(zip(in_refs, out_refs)):
            for k, (px, py) in enumerate(_other_chips(x, y)):
                sends.append(_rcopy(p_ref.at[2 * px + py], out_ref.at[j], ss, rs, 3 * a + k, (px, py, c)))
                slot = out_ref.at[2 * px + py]
                recvs.append(_rcopy(slot, slot, ss, rs, 3 * a + k, (px, py, c)))
        return sends, recvs

    return _Plan(ps, [jax.ShapeDtypeStruct(p.shape, p.dtype) for p in ps], 3 * len(ps), copies)


def plan_join_halves(fulls):
    def copies(in_refs, out_refs, ss, rs):
        x, y, c = _me()
        sends, recvs = [], []
        for a, out_ref in enumerate(out_refs):
            H = out_ref.shape[0] // 2
            mine, theirs = out_ref.at[pl.ds(c * H, H)], out_ref.at[pl.ds((1 - c) * H, H)]
            sends.append(_rcopy(mine, mine, ss, rs, a, (x, y, 1 - c)))
            recvs.append(_rcopy(theirs, theirs, ss, rs, a, (x, y, 1 - c)))
        return sends, recvs

    return _Plan(fulls, [jax.ShapeDtypeStruct(f.shape, f.dtype) for f in fulls], len(fulls), copies, aliased=True)


def call_with_plans(body, plans, *, grid, in_specs, out_specs, out_shape, scratch_shapes, args, sem, name):
    plans = list(plans or [])
    n_in, n_out, n_scr = len(in_specs), len(out_specs), len(scratch_shapes)
    c_in = [len(p.ins) for p in plans]
    c_out = [len(p.out_shapes) for p in plans]
    steps = math.prod(grid) if grid else 1

    def wrapped(*refs):
        pos = 0

        def take(n):
            nonlocal pos
            out = refs[pos:pos + n]
            pos += n
            return out

        ins = take(n_in)
        cins = [take(n) for n in c_in]
        outs = take(n_out)
        couts = [take(n) for n in c_out]
        scr = take(n_scr)
        sems = [take(2) for _ in plans]
        def start_all():
            for p, ci, co, (ss, rs) in zip(plans, cins, couts, sems):
                p.start(ci, co, ss, rs)

        def finish_all():
            for p, ci, co, (ss, rs) in zip(plans, cins, couts, sems):
                p.finish(ci, co, ss, rs)

        if plans and grid:
            idx = 0
            for ax, g in enumerate(grid):
                idx = idx * g + pl.program_id(ax)
            pl.when(idx == 0)(start_all)
        elif plans:
            start_all()
        if body is not None:
            body(*ins, *outs, *scr)
        if plans and grid:
            pl.when(idx == steps - 1)(finish_all)
        elif plans:
            finish_all()

    aliases = {}
    i_pos, o_pos = n_in, n_out
    for p, ni, no in zip(plans, c_in, c_out):
        if p.aliased:
            aliases.update({i_pos + t: o_pos + t for t in range(ni)})
        i_pos += ni
        o_pos += no
    kwargs = dict(grid=grid) if grid else {}
    if aliases:
        kwargs["input_output_aliases"] = aliases
    res = pl.pallas_call(
        wrapped, name=name, in_specs=list(in_specs) + [HBM] * sum(c_in), out_specs=list(out_specs) + [HBM] * sum(c_out),
        out_shape=list(out_shape) + [s for p in plans for s in p.out_shapes],
        scratch_shapes=list(scratch_shapes) + [pltpu.SemaphoreType.DMA((p.ncopies,)) for p in plans for _ in range(2)],
        compiler_params=_cp(*sem) if grid else pltpu.CompilerParams(vmem_limit_bytes=VMEM_LIMIT), **kwargs,
    )(*args, *[a for p in plans for a in p.ins])
    res = list(res)
    comp, rest = res[:n_out], res[n_out:]
    pouts = []
    for no in c_out:
        pouts.append(rest[:no])
        rest = rest[no:]
    return comp, pouts


def run_plans(plans, *, name):
    return call_with_plans(None, plans, grid=(), in_specs=[], out_specs=[], out_shape=[], scratch_shapes=[], args=[], sem=(), name=name)[1]


def _cat(parts, axis=-1):
    return jnp.concatenate(parts, axis=axis)


def _prep_w_in(w):
    z = lambda n: jnp.zeros((w.shape[0], n), w.dtype)
    swq = w[:, 1184:1568]
    return _cat([w[:, 0:1152], z(64), w[:, 1152:1184], z(32)] + [swq[:, HEAD * h:HEAD * (h + 1)] for h in SW_PERM] + [w[:, 1568:1824]])


def _unprep_w_in(g):
    swq = g[:, P_SWQ:P_SWK]
    return _cat([g[:, 0:1152], g[:, 1216:1248]] + [swq[:, HEAD * SW_PERM.index(h):HEAD * (SW_PERM.index(h) + 1)] for h in range(6)] + [g[:, P_SWK:P_END]])


def _prep_w_uq(w):
    z = jnp.zeros((w.shape[0], 32), w.dtype)
    return _cat([p for h in range(6) for p in (w[:, MLA_QK * h:MLA_QK * (h + 1)], z)])


def _unprep_w_uq(g):
    return _cat([g[:, LANES * h:LANES * h + MLA_QK] for h in range(6)])


def _prep_w_ukv(w):
    z = jnp.zeros((w.shape[0], HEAD), w.dtype)
    return _cat([p for h in range(6) for p in (w[:, LANES * h:LANES * h + HEAD], z)] + [w[:, LANES * h + HEAD:LANES * (h + 1)] for h in range(6)])


def _unprep_w_ukv(g):
    return _cat([p for h in range(6) for p in (g[:, LANES * h:LANES * h + HEAD], g[:, 768 + HEAD * h:768 + HEAD * (h + 1)])])


def _prep_w_out(w):
    return _cat([w[0:640]] + [w[640 + HEAD * h:640 + HEAD * (h + 1)] for h in SW_PERM], axis=0)


def _unprep_w_out(g):
    return _cat([g[0:640]] + [g[640 + HEAD * SW_PERM.index(h):640 + HEAD * (SW_PERM.index(h) + 1)] for h in range(6)], axis=0)


def _rope_tables(positions):
    half = 16
    inv_freq = jnp.power(ROPE_THETA, -jnp.arange(half, dtype=F32) / half)
    ang = positions.astype(F32)[..., None] * inv_freq
    cos, sin = jnp.cos(ang), jnp.sin(ang)
    z = lambda n: jnp.zeros(ang.shape[:-1] + (n,), F32)
    return (_cat([jnp.ones(ang.shape[:-1] + (HEAD,), F32), cos, cos, z(32)]), _cat([z(HEAD), -sin, z(16), z(32)]), _cat([z(HEAD), z(16), sin, z(32)]))


def _small_params(p):
    pad96 = lambda g: _cat([g, jnp.zeros((32,), F32)]).reshape(1, LANES)
    two = lambda g: _cat([g, g]).reshape(1, LANES)
    sinks = jnp.broadcast_to(p["sw_sinks"].reshape(2, 3).T[:, :, None], (3, 2, LANES))
    return dict(n1=p["norm1_g"].reshape(1, -1), n2=p["norm2_g"].reshape(1, -1), cq_g=p["mla_cq_g"].reshape(1, -1),
                ckv_g=p["mla_ckv_g"].reshape(1, -1), qn_g=pad96(p["mla_qn_g"]), kn_g=pad96(p["mla_kn_g"]),
                swq_g=two(p["sw_qn_g"]), swk_g=two(p["sw_kn_g"]), sinks=sinks, conv_b=_up_perm(p["conv_b"]).reshape(1, -1))


class _NoFlow:
    def plans(self, tag):
        return []

    def done(self, tag, outs):
        pass

    def add(self, key, g):
        pass


def _layer_fwd(x3, md, W, tabs, bias, tag, flow=_NoFlow()):
    Bl, S, D = x3.shape
    T = Bl * S
    n = lambda s: f"{s}_{tag}"
    two = lambda a: a.reshape(T, a.shape[-1])
    three = lambda a: a.reshape(Bl, S, a.shape[-1])
    h = rms_fwd(x3, 0, D, W["n1"], md["scale1"], md["shift1"], name=n("norm1"))
    proj = three(matmul(two(h), W["w_in"], tn=1920, name=n("in_proj")))
    (o_a, rt_a), got = sb_attn_fwd(proj, plans=flow.plans(n("sb_fwd")), name=n("sb_fwd"))
    flow.done(n("sb_fwd"), got)
    cqn = rms_fwd(proj, P_CQ // 256, 256, W["cq_g"], name=n("cq_norm"))
    ckvn = rms_fwd(proj, P_CKV // LANES, LANES, W["ckv_g"], name=n("ckv_norm"))
    qb = three(matmul(two(cqn), W["w_uq"], tm=1024, tn=768, name=n("uq")))
    kvb = three(matmul(two(ckvn), W["w_ukv"], tm=1024, tn=1152, name=n("ukv")))
    q_m = rope_norm_fwd(qb, 6, W["qn_g"], tabs, name=n("q_rope"))
    k_m = rope_norm_fwd(kvb, 6, W["kn_g"], tabs, (proj, P_SLAB // LANES), name=n("k_rope"))
    (o_b, lse_b), got = mla_attn_fwd(q_m, k_m, kvb, 6, plans=flow.plans(n("mla_fwd")), name=n("mla_fwd"))
    flow.done(n("mla_fwd"), got)
    q_c = pair_rms_fwd(proj, P_SWQ // LANES, 3, W["swq_g"], name=n("swq_norm"))
    k_c = pair_rms_fwd(proj, P_SWK // LANES, 1, W["swk_g"], name=n("swk_norm"))
    (o_c, lse_c), got = swa_attn_fwd(q_c, k_c, proj, bias, W["sinks"], plans=flow.plans(n("swa_fwd")), name=n("swa_fwd"))
    flow.done(n("swa_fwd"), got)
    mix = _cat([o_a, o_b, o_c]).astype(BF16)
    att, x1 = matmul_res(two(mix), W["w_out"], two(x3), md["gate1"], S, name=n("out_proj"))
    x1 = three(x1)
    h2 = rms_fwd(x1, 0, D, W["n2"], md["scale2"], md["shift2"], name=n("norm2"))
    up = three(matmul(two(h2), W["w_up"], tn=1408, name=n("up_proj")))
    a = conv_gate_fwd(up, W["conv_w"], W["conv_b"], name=n("conv_gate"))
    yd, x2 = matmul_res(two(a), W["w_down"], two(x1), md["gate2"], S, name=n("down_proj"))
    saved = dict(x=x3, h=h, proj=proj, rt_a=rt_a, cqn=cqn, ckvn=ckvn, qb=qb, kvb=kvb, q_m=q_m, k_m=k_m, o_b=o_b, lse_b=lse_b,
                 q_c=q_c, k_c=k_c, o_c=o_c, lse_c=lse_c, mix=mix, att=three(att), x1=x1, h2=h2, up=up, a=a, yd=three(yd))
    return three(x2), saved


def _layer_bwd(dx2, sv, md, W, tabs, bias, tag, flow=_NoFlow()):
    Bl, S, D = dx2.shape
    T = Bl * S
    n = lambda s: f"{s}_{tag}"
    two = lambda a: a.reshape(T, a.shape[-1])
    three = lambda a: a.reshape(Bl, S, a.shape[-1])
    g = {}
    dyb, dgate2 = gate_bwd(dx2, sv["yd"], md["gate2"], name=n("gate2_bwd"))
    da = three(matmul(two(dyb), W["w_down"], tb=True, tn=1408, name=n("down_dx")))
    g["w_down"] = matmul(two(sv["a"]), two(dyb), ta=True, tm=256, tn=1024, name=n("down_dw"))
    dup, dcw = conv_gate_bwd(sv["up"], W["conv_w"], W["conv_b"], da, name=n("conv_gate_bwd"))
    dh2 = three(matmul(two(dup), W["w_up"], tb=True, tn=1024, name=n("up_dx")))
    g["w_up"] = matmul(two(sv["h2"]), two(dup), ta=True, tn=1408, name=n("up_dw"))
    dx1, dn2, dsc2, dsh2 = rms_bwd(sv["x1"], 0, D, dh2, W["n2"], md["scale2"], dx2, name=n("norm2_bwd"))
    dmo, dgate1 = gate_bwd(dx1, sv["att"], md["gate1"], name=n("gate1_bwd"))
    dmix = three(matmul(two(dmo), W["w_out"], tb=True, tn=1024, out_dtype=BF16, name=n("out_dx")))
    g["w_out"] = matmul(two(sv["mix"]), two(dmo), ta=True, tn=1024, name=n("out_dw"))
    proj = sv["proj"]
    for k in ("w_down", "w_up", "w_out"):
        flow.add((tag, k), g[k])
    (dq_a, dk_a, dv_a), got = sb_attn_bwd(proj, sv["rt_a"], dmix[:, :, 0:256], plans=flow.plans(n("sb_bwd")), name=n("sb_bwd"))
    flow.done(n("sb_bwd"), got)
    dq_m, dk_m, dv_b = mla_attn_bwd(sv["q_m"], sv["k_m"], sv["kvb"], 6, sv["o_b"], sv["lse_b"], dmix[:, :, 256:640], name=n("mla_bwd"))
    dqb, dqn = rope_norm_bwd(sv["qb"], 6, dq_m, W["qn_g"], tabs, name=n("q_rope_bwd"))
    dkn_x, dkn, dslab = rope_norm_bwd(sv["kvb"], 6, dk_m, W["kn_g"], tabs, (proj, P_SLAB // LANES), name=n("k_rope_bwd"))
    dkvb = _cat([dkn_x, dv_b]).astype(BF16)
    dckvn = three(matmul(two(dkvb), W["w_ukv"], tb=True, tm=1024, name=n("ukv_dx")))
    g["w_ukv"] = matmul(two(sv["ckvn"]), two(dkvb), ta=True, tn=1152, name=n("ukv_dw"))
    dcqn = three(matmul(two(dqb), W["w_uq"], tb=True, tm=1024, name=n("uq_dx")))
    g["w_uq"] = matmul(two(sv["cqn"]), two(dqb), ta=True, tn=768, name=n("uq_dw"))
    dcq, dcq_g = rms_bwd(proj, P_CQ // 256, 256, dcqn, W["cq_g"], name=n("cq_norm_bwd"))
    dckv, dckv_g = rms_bwd(proj, P_CKV // LANES, LANES, dckvn, W["ckv_g"], name=n("ckv_norm_bwd"))
    dq_c, dk_c, dv_c, dbias, dsink = swa_attn_bwd(sv["q_c"], sv["k_c"], proj, bias, W["sinks"], sv["o_c"], sv["lse_c"], dmix[:, :, 640:1024], name=n("swa_bwd"))
    dswq, dswq_g = pair_rms_bwd(proj, P_SWQ // LANES, 3, dq_c, W["swq_g"], name=n("swq_norm_bwd"))
    dswk, dswk_g = pair_rms_bwd(proj, P_SWK // LANES, 1, dk_c, W["swk_g"], name=n("swk_norm_bwd"))
    dproj = _cat([dq_a, dk_a, dv_a, dcq, dckv, dslab, dswq, dswk, dv_c]).astype(BF16)
    dh = three(matmul(two(dproj), W["w_in"], tb=True, tn=1024, name=n("in_dx")))
    g["w_in"] = matmul(two(sv["h"]), two(dproj), ta=True, tn=1920, tk=2048, name=n("in_dw"))
    dx, dn1, dsc1, dsh1 = rms_bwd(sv["x"], 0, D, dh, W["n1"], md["scale1"], dx1, name=n("norm1_bwd"))
    small = dict(n1=dn1, n2=dn2, cq_g=dcq_g, ckv_g=dckv_g, qn_g=dqn, kn_g=dkn, swq_g=dswq_g, swk_g=dswk_g, conv=dcw)
    dmods = _cat([dsh1, dsc1, dgate1, dsh2, dsc2, dgate2]).reshape(Bl, 6 * D)
    for k in ("w_ukv", "w_uq", "w_in"):
        flow.add((tag, k), g[k])
    return dx, g, small, dmods, dbias, dsink


BIG = ("w_in", "w_uq", "w_ukv", "w_out", "w_up", "w_down")
ROW_SHARDED = ("w_out", "w_down")
PREP = dict(w_in=_prep_w_in, w_uq=_prep_w_uq, w_ukv=_prep_w_ukv, w_out=_prep_w_out, w_up=_up_perm, w_down=lambda w: w)
UNPREP = dict(w_in=_unprep_w_in, w_uq=_unprep_w_uq, w_ukv=_unprep_w_ukv, w_out=_unprep_w_out, w_up=_up_perm, w_down=lambda w: w)
NCHIPS = 4


def _local_step(x, target, positions, mods, Wl, rel_flat, fwd_flow=_NoFlow(), bwd_flow=_NoFlow()):
    Bl, S, D = x.shape
    L = len(Wl)
    tabs = _rope_tables(positions)
    bucket = _bucket_table()
    bias = swa_bias(rel_flat, bucket, name="swa_bias")
    mds = []
    for l in range(L):
        parts = [mods[l, :, D * k:D * (k + 1)].reshape(Bl, 1, D) for k in range(6)]
        mds.append(dict(zip(("shift1", "scale1", "gate1", "shift2", "scale2", "gate2"), parts)))
    saved = []
    h = x
    for l in range(L):
        h, sv = _layer_fwd(h, mds[l], Wl[l], tabs, bias, f"l{l}", fwd_flow)
        saved.append(sv)
    dy, loss = loss_grad(h, target, name="loss")
    grads, smalls, dmods, dbiases, dsinks = [None] * L, [None] * L, [None] * L, [None] * L, [None] * L
    for l in reversed(range(L)):
        dy, grads[l], smalls[l], dmods[l], dbiases[l], dsinks[l] = _layer_bwd(dy, saved[l], mds[l], Wl[l], tabs, bias, f"l{l}", bwd_flow)
    drel = swa_bias_bwd(_cat(dbiases, axis=0), bucket, name="swa_bias_bwd")
    return loss, dy, grads, smalls, dmods, dsinks, drel


ATT = ("w_in", "w_uq", "w_ukv", "w_out")
FFN = ("w_up", "w_down")
GATHER_STAGES = {
    "sb_fwd_l0": ([("l0", k) for k in FFN], []),
    "mla_fwd_l0": ([("l1", k) for k in ATT + ("w_up",)], [("l0", k) for k in FFN]),
    "swa_fwd_l0": ([("l1", "w_down")], [("l1", k) for k in ATT + ("w_up",)]),
    "sb_fwd_l1": ([], [("l1", "w_down")]),
}
SCATTER_STAGES = {
    "sb_bwd_l1": [("l1", k) for k in FFN],
    "sb_bwd_l0": [("l1", k) for k in ATT] + [("l0", k) for k in FFN],
}


class _GatherFlow:
    def __init__(self, shards, chip):
        self.shards, self.chip, self.ici, self.d2d, self.pending = shards, chip, {}, {}, {}

    def early(self, keys):
        ici, = run_plans([plan_gather_ici([self.shards[k] for k in keys])], name="gather_early_ici")
        d2d, = run_plans([plan_gather_d2d(ici)], name="gather_early_d2d")
        self.d2d.update(zip(keys, d2d))

    def plans(self, tag):
        ici_keys, d2d_keys = GATHER_STAGES.get(tag, ([], []))
        plans = []
        if d2d_keys:
            plans.append(plan_gather_d2d([self.ici[k] for k in d2d_keys]))
        if ici_keys:
            plans.append(plan_gather_ici([self.shards[k] for k in ici_keys]))
        self.pending[tag] = (ici_keys, d2d_keys)
        return plans

    def done(self, tag, outs):
        ici_keys, d2d_keys = self.pending.pop(tag, ([], []))
        outs = list(outs)
        if d2d_keys:
            self.d2d.update(zip(d2d_keys, outs.pop(0)))
        if ici_keys:
            self.ici.update(zip(ici_keys, outs.pop(0)))

    def weight(self, key):
        k = key[1]
        own = self.shards[key]
        r, cc = own.shape
        w4 = lax.dynamic_update_slice(self.d2d[key], own[None], (self.chip, 0, 0))
        fw = w4.reshape(NCHIPS * r, cc) if k in ROW_SHARDED else jnp.transpose(w4, (1, 0, 2)).reshape(r, NCHIPS * cc)
        return PREP[k](fw)


class _LayerWeights(dict):
    def __init__(self, small, flow, tag):
        super().__init__(small)
        self.flow, self.tag = flow, tag

    def __missing__(self, k):
        self[k] = self.flow.weight((self.tag, k))
        return self[k]


class _ScatterFlow:
    def __init__(self, shapes, sel, c_arr):
        self.shapes, self.sel, self.c_arr = shapes, sel, c_arr
        self.g, self.pairs, self.landed, self.pending = {}, {}, {}, {}

    def add(self, key, g):
        self.g[key] = g

    def _pairs(self, keys, label):
        g4s = []
        for key in keys:
            k = key[1]
            r, cc = self.shapes[k]
            gk = UNPREP[k](self.g[key])
            g4 = gk.reshape(NCHIPS, r, cc) if k in ROW_SHARDED else jnp.transpose(gk.reshape(r, NCHIPS, cc), (1, 0, 2))
            g4s.append(g4.astype(BF16))
        theirs, = run_plans([plan_swap_halves(g4s)], name=f"rs_swap_{label}")
        pairs = [pair_add_half(g4, th, self.c_arr, name=f"rs_pair_add_{key[1]}_{key[0]}") for key, g4, th in zip(keys, g4s, theirs)]
        self.pairs.update(zip(keys, pairs))
        return pairs

    def plans(self, tag):
        keys = SCATTER_STAGES.get(tag, [])
        self.pending[tag] = keys
        return [plan_scatter_ici(self._pairs(keys, tag))] if keys else []

    def done(self, tag, outs):
        keys = self.pending.pop(tag, [])
        if keys:
            self.landed.update(zip(keys, outs[0]))

    def finish(self):
        rest = [key for key in self.g if key not in self.pairs]
        if rest:
            landed, = run_plans([plan_scatter_ici(self._pairs(rest, "rest"))], name="rs_scatter_rest")
            self.landed.update(zip(rest, landed))
        keys = list(self.pairs)
        fulls = [chip_sum_into(self.landed[key], self.pairs[key], self.sel, name=f"rs_chip_sum_{key[1]}_{key[0]}") for key in keys]
        joined, = run_plans([plan_join_halves(fulls)], name="rs_join_halves")
        return dict(zip(keys, joined))


WEIGHTS = ("rel_table", "norm1_g", "norm2_g", "w_ada", "b_ada", "w_in", "mla_cq_g", "w_uq", "mla_ckv_g", "w_ukv", "mla_qn_g", "mla_kn_g",
           "sw_qn_g", "sw_kn_g", "sw_sinks", "w_out", "w_up", "conv_w", "conv_b", "w_down")
SMALL = tuple(n for n in WEIGHTS if n not in BIG + ("w_ada",))


def kernel(x, c, positions, rel_table, norm1_g, norm2_g, w_ada, b_ada, w_in, mla_cq_g, w_uq, mla_ckv_g, w_ukv, mla_qn_g, mla_kn_g, sw_qn_g, sw_kn_g, sw_sinks, w_out, w_up, conv_w, conv_b, w_down, loss_target, m_rel_table, m_norm1_g, m_norm2_g, m_w_ada, m_b_ada, m_w_in, m_mla_cq_g, m_w_uq, m_mla_ckv_g, m_w_ukv, m_mla_qn_g, m_mla_kn_g, m_sw_qn_g, m_sw_kn_g, m_sw_sinks, m_w_out, m_w_up, m_conv_w, m_conv_b, m_w_down, v_rel_table, v_norm1_g, v_norm2_g, v_w_ada, v_b_ada, v_w_in, v_mla_cq_g, v_w_uq, v_mla_ckv_g, v_w_ukv, v_mla_qn_g, v_mla_kn_g, v_sw_qn_g, v_sw_kn_g, v_sw_sinks, v_w_out, v_w_up, v_conv_w, v_conv_b, v_w_down):
    w = dict(rel_table=rel_table, norm1_g=norm1_g, norm2_g=norm2_g, w_ada=w_ada, b_ada=b_ada, w_in=w_in, mla_cq_g=mla_cq_g, w_uq=w_uq,
             mla_ckv_g=mla_ckv_g, w_ukv=w_ukv, mla_qn_g=mla_qn_g, mla_kn_g=mla_kn_g, sw_qn_g=sw_qn_g, sw_kn_g=sw_kn_g, sw_sinks=sw_sinks,
             w_out=w_out, w_up=w_up, conv_w=conv_w, conv_b=conv_b, w_down=w_down)
    m = dict(rel_table=m_rel_table, norm1_g=m_norm1_g, norm2_g=m_norm2_g, w_ada=m_w_ada, b_ada=m_b_ada, w_in=m_w_in, mla_cq_g=m_mla_cq_g,
             w_uq=m_w_uq, mla_ckv_g=m_mla_ckv_g, w_ukv=m_w_ukv, mla_qn_g=m_mla_qn_g, mla_kn_g=m_mla_kn_g, sw_qn_g=m_sw_qn_g,
             sw_kn_g=m_sw_kn_g, sw_sinks=m_sw_sinks, w_out=m_w_out, w_up=m_w_up, conv_w=m_conv_w, conv_b=m_conv_b, w_down=m_w_down)
    v = dict(rel_table=v_rel_table, norm1_g=v_norm1_g, norm2_g=v_norm2_g, w_ada=v_w_ada, b_ada=v_b_ada, w_in=v_w_in, mla_cq_g=v_mla_cq_g,
             w_uq=v_w_uq, mla_ckv_g=v_mla_ckv_g, w_ukv=v_w_ukv, mla_qn_g=v_mla_qn_g, mla_kn_g=v_mla_kn_g, sw_qn_g=v_sw_qn_g,
             sw_kn_g=v_sw_kn_g, sw_sinks=v_sw_sinks, w_out=v_w_out, w_up=v_w_up, conv_w=v_conv_w, conv_b=v_conv_b, w_down=v_w_down)
    Bl, S, D = x.shape
    L = norm1_g.shape[0]
    xi, yi, ci = _me()
    chip = 2 * xi + yi
    dev = 4 * xi + 2 * yi + ci
    ndev = 2 * NCHIPS

    shapes = {k: w[k].shape[1:] for k in BIG}
    shards = {(f"l{l}", k): w[k][l].astype(BF16) for l in range(L) for k in BIG}
    gflow = _GatherFlow(shards, chip)
    gflow.early([("l0", k) for k in ATT])

    cw_cols = conv_w.shape[2]
    c_got, cw_got = allgather8([c, conv_w.reshape(L * 3, cw_cols)], name="gather_cond")
    c_all = c_got.reshape(ndev * Bl, D)
    conv_full = jnp.transpose(cw_got[0::2].reshape(NCHIPS, L, 3, cw_cols), (1, 2, 0, 3)).reshape(L, 3, NCHIPS * cw_cols)
    E = w_ada.shape[2]
    b_cols = lax.dynamic_slice(b_ada, (0, chip * E), (L, E)).reshape(L, 1, E)
    mods_cols = mods_matmul(c_all, w_ada, b_cols, name="mods")
    mods_all, = allgather8([mods_cols.reshape(L * ndev * Bl, E)], name="gather_mods")
    mods_all = jnp.transpose(mods_all[0::2].reshape(NCHIPS, L, ndev * Bl, E), (1, 2, 0, 3)).reshape(L, ndev * Bl, NCHIPS * E)
    mods = lax.dynamic_slice(mods_all, (0, dev * Bl, 0), (L, Bl, NCHIPS * E))

    Wl = []
    for l in range(L):
        Wd = _small_params({k: w[k][l] for k in SMALL if k not in ("rel_table", "b_ada", "conv_w")})
        Wd["conv_w"] = _up_perm(conv_full[l])
        Wl.append(_LayerWeights(Wd, gflow, f"l{l}"))

    sflow = _ScatterFlow(shapes, jnp.stack([chip, ci]).astype(jnp.int32), ci.reshape(1).astype(jnp.int32))
    loss, dx, _, smalls, dmods, dsinks, drel = _local_step(x, loss_target, positions, mods, Wl, rel_table.reshape(-1), gflow, sflow)
    reduced = sflow.finish()
    grad = {k: jnp.stack([reduced[(f"l{l}", k)] for l in range(L)]) for k in BIG}

    vec_names = ("n1", "n2", "cq_g", "ckv_g", "qn_g", "kn_g", "swq_g", "swk_g")
    vecs = _cat([_cat([smalls[l][k] for k in vec_names], axis=1) for l in range(L)], axis=0)
    convs = _cat([smalls[l]["conv"] for l in range(L)], axis=0)
    dm = jnp.stack(dmods, axis=1).reshape(Bl * L, 6 * D)
    dsk = jnp.stack(dsinks, axis=1).reshape(Bl * L * 6, LANES)
    got = allgather8([vecs, convs, drel, loss, dm, dsk], name="gather_small_grads")
    seq = lambda a, rows: a.reshape(ndev * Bl, rows, a.shape[-1])
    vec_s, conv_s, rel_s, loss_s, dm_s, dsk_s = sum_small(list(got[:4]) + [seq(got[4], L), seq(got[5], L * 6)], name="sum_small_grads")
    dm_all = jnp.transpose(seq(got[4], L), (1, 0, 2))
    grad["w_ada"] = ada_grad(c_all, lax.dynamic_slice(dm_all, (0, 0, chip * E), (L, ndev * Bl, E)), name="ada_grad")
    grad["b_ada"] = dm_s
    grad["sw_sinks"] = jnp.transpose(dsk_s.reshape(L, 3, 2, LANES)[:, :, :, 0], (0, 2, 1)).reshape(L, 6)
    grad["rel_table"] = rel_s[:6, :REL_BUCKETS].T
    off = 0
    for k, name_, keep in zip(vec_names, ("norm1_g", "norm2_g", "mla_cq_g", "mla_ckv_g", "mla_qn_g", "mla_kn_g", "sw_qn_g", "sw_kn_g"),
                              (D, D, 256, LANES, MLA_QK, MLA_QK, HEAD, HEAD)):
        grad[name_] = vec_s[:, off:off + keep]
        off += smalls[0][k].shape[1]
    conv = _up_perm(conv_s.reshape(L, 8, 2 * D_FF))
    grad["conv_w"] = lax.dynamic_slice(conv[:, 0:3], (0, 0, chip * cw_cols), (L, 3, cw_cols))
    grad["conv_b"] = conv[:, 3]
    loss_out = loss_s[0, 0]

    delta, new_m, new_v = {}, {}, {}
    for k in BIG + ("w_ada",):
        shp = w[k].shape
        to2 = lambda a: a.reshape(-1, shp[-1])
        d_, m_, v_ = adamw(to2(w[k]), to2(grad[k]), to2(m[k]), to2(v[k]), name=f"adamw_{k}")
        delta[k], new_m[k], new_v[k] = d_.reshape(shp), m_.reshape(shp), v_.reshape(shp)
    outs = adamw_small(*[[src[k] for k in SMALL] for src in (w, grad, m, v)], name="adamw_small")
    for dst, o in zip((delta, new_m, new_v), outs):
        dst.update(dict(zip(SMALL, o)))
    return (loss_out, dx, *[grad[k] for k in WEIGHTS], *[delta[k] for k in WEIGHTS], *[new_m[k] for k in WEIGHTS], *[new_v[k] for k in WEIGHTS])
```

```python
import functools
import math

import jax
import jax.numpy as jnp
from jax import lax
from jax.experimental import pallas as pl
from jax.experimental.pallas import tpu as pltpu

F32 = jnp.float32
BF16 = jnp.bfloat16
MESH = pl.DeviceIdType.MESH

EPS = 1e-6
NEG = -1e30
HEAD = 64
LANES = 128
MLA_QK = 96
ROPE_THETA = 10000.0
REL_BUCKETS = 32
REL_MAX_DIST = 128
WINDOW = 128
D_FF = 2816
ADAM_LR, ADAM_B1, ADAM_B2, ADAM_EPS, ADAM_WD, ADAM_STEP = 0.001, 0.9, 0.999, 1e-08, 0.01, 10

VMEM_LIMIT = 56 * 1024 * 1024
STRIP = 32

P_SBQ, P_SBK, P_SBV, P_CQ, P_CKV, P_SLAB, P_SWQ, P_SWK, P_SWV, P_END = 0, 256, 512, 768, 1024, 1152, 1280, 1664, 1792, 1920
SW_PERM = (0, 3, 1, 4, 2, 5)


def _cp(*sem):
    return pltpu.CompilerParams(dimension_semantics=sem, vmem_limit_bytes=VMEM_LIMIT)


def _dot(a, b):
    return jnp.dot(a, b, preferred_element_type=F32)


def _dot_nt(a, b):
    return lax.dot_general(a, b, (((1,), (1,)), ((), ())), preferred_element_type=F32)


def _dot_tn(a, b):
    return lax.dot_general(a, b, (((0,), (0,)), ((), ())), preferred_element_type=F32)


def _split_dot(x, u):
    hi = x.astype(BF16)
    lo = (x - hi.astype(F32)).astype(BF16)
    return _dot(hi, u) + _dot(lo, u)


def _lane_masks():
    lane = lax.broadcasted_iota(jnp.int32, (1, LANES), 1)
    return (lane < HEAD, lane >= HEAD)


def _tile(n, cap, align=128):
    if n <= cap:
        return n
    t = cap - cap % align
    while t >= align:
        if n % t == 0:
            return t
        t -= align
    return n


def matmul(a, b, *, ta=False, tb=False, out_dtype=F32, tm=512, tn=512, tk=8192, name):
    M, K = (a.shape[1], a.shape[0]) if ta else a.shape
    N = b.shape[0] if tb else b.shape[1]
    tm, tn, tk = _tile(M, tm), _tile(N, tn), _tile(K, tk)
    nk = K // tk

    def body(a_ref, b_ref, o_ref, *scratch):
        av = a_ref[...].astype(BF16)
        bv = b_ref[...].astype(BF16)
        if ta:
            part = _dot_tn(av, bv)
        elif tb:
            part = _dot_nt(av, bv)
        else:
            part = _dot(av, bv)
        if nk == 1:
            o_ref[...] = part.astype(out_dtype)
        else:
            acc_ref, = scratch
            k = pl.program_id(2)

            @pl.when(k == 0)
            def _():
                acc_ref[...] = part

            @pl.when(k > 0)
            def _():
                acc_ref[...] += part

            @pl.when(k == nk - 1)
            def _():
                o_ref[...] = acc_ref[...].astype(out_dtype)

    a_spec = pl.BlockSpec((tk, tm), lambda i, j, k: (k, i)) if ta else pl.BlockSpec((tm, tk), lambda i, j, k: (i, k))
    b_spec = pl.BlockSpec((tn, tk), lambda i, j, k: (j, k)) if tb else pl.BlockSpec((tk, tn), lambda i, j, k: (k, j))
    return pl.pallas_call(
        body, name=name, grid=(M // tm, N // tn, nk),
        in_specs=[a_spec, b_spec], out_specs=pl.BlockSpec((tm, tn), lambda i, j, k: (i, j)),
        out_shape=jax.ShapeDtypeStruct((M, N), out_dtype),
        scratch_shapes=[] if nk == 1 else [pltpu.VMEM((tm, tn), F32)],
        compiler_params=_cp("parallel", "parallel", "arbitrary"),
    )(a, b)


def matmul_res(a, b, res, gate, seq, *, tm=512, tn=1024, name):
    M, K = a.shape
    N = b.shape[1]
    tm, tn = _tile(min(M, seq), tm), _tile(N, tn)
    per_seq = seq // tm

    def body(a_ref, b_ref, r_ref, g_ref, y_ref, x_ref):
        y = _dot(a_ref[...].astype(BF16), b_ref[...].astype(BF16))
        y_ref[...] = y
        x_ref[...] = r_ref[...] + g_ref[...] * y

    out = jax.ShapeDtypeStruct((M, N), F32)
    return pl.pallas_call(
        body, name=name, grid=(M // tm, N // tn),
        in_specs=[pl.BlockSpec((tm, K), lambda i, j: (i, 0)), pl.BlockSpec((K, tn), lambda i, j: (0, j)),
                  pl.BlockSpec((tm, tn), lambda i, j: (i, j)), pl.BlockSpec((None, 1, tn), lambda i, j: (lax.div(i, jnp.int32(per_seq)), 0, j))],
        out_specs=[pl.BlockSpec((tm, tn), lambda i, j: (i, j))] * 2,
        out_shape=[out, out], compiler_params=_cp("parallel", "parallel"),
    )(a, b, res, gate)


def rms_fwd(x3, blk, W, g, sc=None, sh=None, *, tm=512, name):
    Bl, S, _ = x3.shape
    tm = min(tm, S)
    mod = sc is not None

    def body(x_ref, g_ref, *rest):
        o_ref = rest[-1]
        x = x_ref[...]
        r = lax.rsqrt(jnp.mean(x * x, axis=-1, keepdims=True) + EPS)
        y = x * r * g_ref[...]
        if mod:
            y = y * (1.0 + rest[0][...]) + rest[1][...]
        o_ref[...] = y.astype(BF16)

    vec = pl.BlockSpec((None, 1, W), lambda b, s: (b, 0, 0))
    return pl.pallas_call(
        body, name=name, grid=(Bl, S // tm),
        in_specs=[pl.BlockSpec((None, tm, W), lambda b, s: (b, s, blk)), pl.BlockSpec((1, W), lambda b, s: (0, 0))] + ([vec, vec] if mod else []),
        out_specs=pl.BlockSpec((None, tm, W), lambda b, s: (b, s, 0)),
        out_shape=jax.ShapeDtypeStruct((Bl, S, W), BF16),
        compiler_params=_cp("parallel", "parallel"),
    )(x3, g, *([sc, sh] if mod else []))


def rms_bwd(x3, blk, W, dy3, g, sc=None, dres3=None, *, tm=256, name):
    Bl, S, _ = x3.shape
    tm = min(tm, S)
    mod = sc is not None
    res = dres3 is not None

    def body(*refs):
        x_ref, dy_ref, g_ref = refs[:3]
        k = 3
        sc_ref = dr_ref = None
        if mod:
            sc_ref = refs[k]
            k += 1
        if res:
            dr_ref = refs[k]
            k += 1
        dx_ref, dg_ref = refs[k], refs[k + 1]
        b, s = pl.program_id(0), pl.program_id(1)
        x = x_ref[...]
        dy = dy_ref[...].astype(F32)
        g = g_ref[...]
        r = lax.rsqrt(jnp.mean(x * x, axis=-1, keepdims=True) + EPS)
        n = x * r
        if mod:
            dsc_ref, dsh_ref = refs[k + 2], refs[k + 3]
            one_sc = 1.0 + sc_ref[...]

            @pl.when(s == 0)
            def _():
                dsc_ref[...] = jnp.zeros_like(dsc_ref)
                dsh_ref[...] = jnp.zeros_like(dsh_ref)

            dsh_ref[...] += jnp.sum(dy, axis=0, keepdims=True)
            dsc_ref[...] += jnp.sum(dy * n * g, axis=0, keepdims=True)
            dyn = dy * one_sc
        else:
            dyn = dy

        @pl.when((b == 0) & (s == 0))
        def _():
            dg_ref[...] = jnp.zeros_like(dg_ref)

        dg_ref[...] += jnp.sum(dyn * n, axis=0, keepdims=True)
        dn = dyn * g
        dx = r * (dn - n * jnp.mean(dn * n, axis=-1, keepdims=True))
        if res:
            dx = dx + dr_ref[...]
        dx_ref[...] = dx

    blkspec = pl.BlockSpec((None, tm, W), lambda b, s: (b, s, 0))
    vec = pl.BlockSpec((None, 1, W), lambda b, s: (b, 0, 0))
    row = pl.BlockSpec((1, W), lambda b, s: (0, 0))
    in_specs = [pl.BlockSpec((None, tm, W), lambda b, s: (b, s, blk)), blkspec, row] + ([vec] if mod else []) + ([blkspec] if res else [])
    out_specs = [blkspec, row] + ([vec, vec] if mod else [])
    out_shape = [jax.ShapeDtypeStruct((Bl, S, W), F32), jax.ShapeDtypeStruct((1, W), F32)]
    if mod:
        out_shape += [jax.ShapeDtypeStruct((Bl, 1, W), F32)] * 2
    args = [x3, dy3, g] + ([sc] if mod else []) + ([dres3] if res else [])
    return pl.pallas_call(
        body, name=name, grid=(Bl, S // tm), in_specs=in_specs, out_specs=out_specs, out_shape=out_shape,
        compiler_params=_cp("arbitrary", "arbitrary"),
    )(*args)


def pair_rms_fwd(x3, blk0, npairs, g2, *, tm=1024, name):
    Bl, S, _ = x3.shape
    tm = min(tm, S)

    def body(x_ref, g_ref, o_ref):
        lo, hi = _lane_masks()
        x = x_ref[...]
        xx = x * x
        s0 = jnp.sum(jnp.where(lo, xx, 0.0), axis=-1, keepdims=True)
        s1 = jnp.sum(jnp.where(hi, xx, 0.0), axis=-1, keepdims=True)
        r = jnp.where(lo, lax.rsqrt(s0 / HEAD + EPS), lax.rsqrt(s1 / HEAD + EPS))
        o_ref[...] = (x * r * g_ref[...]).astype(BF16)

    return pl.pallas_call(
        body, name=name, grid=(Bl, S // tm, npairs),
        in_specs=[pl.BlockSpec((None, tm, LANES), lambda b, s, p: (b, s, blk0 + p)), pl.BlockSpec((1, LANES), lambda b, s, p: (0, 0))],
        out_specs=pl.BlockSpec((None, tm, LANES), lambda b, s, p: (b, s, p)),
        out_shape=jax.ShapeDtypeStruct((Bl, S, LANES * npairs), BF16),
        compiler_params=_cp("parallel", "parallel", "parallel"),
    )(x3, g2)


def pair_rms_bwd(x3, blk0, npairs, dy3, g2, *, tm=1024, name):
    Bl, S, _ = x3.shape
    tm = min(tm, S)

    def body(x_ref, dy_ref, g_ref, dx_ref, dg_ref):
        lo, hi = _lane_masks()
        first = (pl.program_id(0) == 0) & (pl.program_id(1) == 0) & (pl.program_id(2) == 0)
        x = x_ref[...]
        dy = dy_ref[...]
        xx = x * x
        s0 = jnp.sum(jnp.where(lo, xx, 0.0), axis=-1, keepdims=True)
        s1 = jnp.sum(jnp.where(hi, xx, 0.0), axis=-1, keepdims=True)
        r = jnp.where(lo, lax.rsqrt(s0 / HEAD + EPS), lax.rsqrt(s1 / HEAD + EPS))
        n = x * r

        @pl.when(first)
        def _():
            dg_ref[...] = jnp.zeros_like(dg_ref)

        part = jnp.sum(dy * n, axis=0, keepdims=True)
        dg_ref[...] += part + pltpu.roll(part, HEAD, 1)
        dn = dy * g_ref[...]
        t = dn * n
        m0 = jnp.sum(jnp.where(lo, t, 0.0), axis=-1, keepdims=True)
        m1 = jnp.sum(jnp.where(hi, t, 0.0), axis=-1, keepdims=True)
        dx_ref[...] = r * (dn - n * (jnp.where(lo, m0, m1) / HEAD))

    return pl.pallas_call(
        body, name=name, grid=(Bl, S // tm, npairs),
        in_specs=[pl.BlockSpec((None, tm, LANES), lambda b, s, p: (b, s, blk0 + p)), pl.BlockSpec((None, tm, LANES), lambda b, s, p: (b, s, p)),
                  pl.BlockSpec((1, LANES), lambda b, s, p: (0, 0))],
        out_specs=[pl.BlockSpec((None, tm, LANES), lambda b, s, p: (b, s, p)), pl.BlockSpec((1, LANES), lambda b, s, p: (0, 0))],
        out_shape=[jax.ShapeDtypeStruct((Bl, S, LANES * npairs), F32), jax.ShapeDtypeStruct((1, LANES), F32)],
        compiler_params=_cp("arbitrary", "arbitrary", "arbitrary"),
    )(x3, dy3, g2)


def _rot(y, cos_t, sin_a, sin_b):
    return y * cos_t + pltpu.roll(y, LANES - 16, 1) * sin_a + pltpu.roll(y, 16, 1) * sin_b


def _rot_t(d, cos_t, sin_a, sin_b):
    return d * cos_t + pltpu.roll(d * sin_a, 16, 1) + pltpu.roll(d * sin_b, LANES - 16, 1)


def rope_norm_fwd(x3, nheads, g, tabs, slab=None, *, tm=1024, name):
    Bl, S, _ = x3.shape
    tm = min(tm, S)
    has_slab = slab is not None

    def body(*refs):
        x_ref, g_ref, c_ref, sa_ref, sb_ref = refs[:5]
        o_ref = refs[-1]
        x = x_ref[...]
        if has_slab:
            x = x + refs[5][...]
        r = lax.rsqrt(jnp.sum(x * x, axis=-1, keepdims=True) / MLA_QK + EPS)
        o_ref[...] = _rot(x * r * g_ref[...], c_ref[...], sa_ref[...], sb_ref[...]).astype(BF16)

    head = pl.BlockSpec((None, tm, LANES), lambda b, s, h: (b, s, h))
    tab = pl.BlockSpec((None, tm, LANES), lambda b, s, h: (b, s, 0))
    in_specs = [head, pl.BlockSpec((1, LANES), lambda b, s, h: (0, 0)), tab, tab, tab]
    args = [x3, g, *tabs]
    if has_slab:
        sblk = slab[1]
        in_specs.append(pl.BlockSpec((None, tm, LANES), lambda b, s, h: (b, s, sblk)))
        args.append(slab[0])
    return pl.pallas_call(
        body, name=name, grid=(Bl, S // tm, nheads), in_specs=in_specs, out_specs=head,
        out_shape=jax.ShapeDtypeStruct((Bl, S, LANES * nheads), BF16),
        compiler_params=_cp("parallel", "parallel", "parallel"),
    )(*args)


def rope_norm_bwd(x3, nheads, dy3, g, tabs, slab=None, *, tm=1024, name):
    Bl, S, _ = x3.shape
    tm = min(tm, S)
    has_slab = slab is not None

    def body(*refs):
        x_ref, dy_ref, g_ref, c_ref, sa_ref, sb_ref = refs[:6]
        k = 7 if has_slab else 6
        dx_ref, dg_ref = refs[k], refs[k + 1]
        h = pl.program_id(2)
        first = (pl.program_id(0) == 0) & (pl.program_id(1) == 0) & (h == 0)
        x = x_ref[...]
        if has_slab:
            x = x + refs[6][...]
        g = g_ref[...]
        r = lax.rsqrt(jnp.sum(x * x, axis=-1, keepdims=True) / MLA_QK + EPS)
        n = x * r
        d = _rot_t(dy_ref[...], c_ref[...], sa_ref[...], sb_ref[...])

        @pl.when(first)
        def _():
            dg_ref[...] = jnp.zeros_like(dg_ref)

        dg_ref[...] += jnp.sum(d * n, axis=0, keepdims=True)
        dn = d * g
        dx = r * (dn - n * (jnp.sum(dn * n, axis=-1, keepdims=True) / MLA_QK))
        dx_ref[...] = dx
        if has_slab:
            ds_ref = refs[k + 2]

            @pl.when(h == 0)
            def _():
                ds_ref[...] = dx

            @pl.when(h > 0)
            def _():
                ds_ref[...] += dx

    head = pl.BlockSpec((None, tm, LANES), lambda b, s, h: (b, s, h))
    tab = pl.BlockSpec((None, tm, LANES), lambda b, s, h: (b, s, 0))
    row = pl.BlockSpec((1, LANES), lambda b, s, h: (0, 0))
    in_specs = [head, head, row, tab, tab, tab]
    args = [x3, dy3, g, *tabs]
    out_specs = [head, row]
    out_shape = [jax.ShapeDtypeStruct((Bl, S, LANES * nheads), F32), jax.ShapeDtypeStruct((1, LANES), F32)]
    if has_slab:
        sblk = slab[1]
        in_specs.append(pl.BlockSpec((None, tm, LANES), lambda b, s, h: (b, s, sblk)))
        args.append(slab[0])
        out_specs.append(tab)
        out_shape.append(jax.ShapeDtypeStruct((Bl, S, LANES), F32))
    return pl.pallas_call(
        body, name=name, grid=(Bl, S // tm, nheads), in_specs=in_specs, out_specs=out_specs, out_shape=out_shape,
        compiler_params=_cp("arbitrary", "arbitrary", "arbitrary"),
    )(*args)


def _sb_strict(row0, col0, tk):
    rr = lax.broadcasted_iota(jnp.int32, (STRIP, tk), 0) + row0
    cc = lax.broadcasted_iota(jnp.int32, (STRIP, tk), 1) + col0
    return cc < rr


def _lanes(x, n):
    return jnp.concatenate([x] * (n // LANES), axis=1) if n > LANES else x


SB_BLOCK = 256
SB_QBLOCK = 512


def sb_attn_fwd(proj3, *, plans=None, name):
    Bl, S, _ = proj3.shape
    tk = min(SB_BLOCK, S)
    tq = min(SB_QBLOCK, S)
    per_q = tq // tk
    scale = HEAD ** -0.5
    qb, kb0, vb0 = P_SBQ // LANES, P_SBK // LANES, P_SBV // LANES

    def body(q_ref, k_ref, v_ref, o_ref, rt_ref, z_scr, a_scr, hi_scr, lo_scr, suf_scr, w_scr):
        i = pl.program_id(2)
        masks = _lane_masks()
        lane = lax.broadcasted_iota(jnp.int32, (1, LANES), 1)
        q = q_ref[...]
        qh = [jnp.where(m, q, 0.0).astype(BF16) for m in masks]
        u = (lax.broadcasted_iota(jnp.int32, (tk, tk), 0) > lax.broadcasted_iota(jnp.int32, (tk, tk), 1)).astype(BF16)
        ones = jnp.ones((tk, LANES), BF16)

        rt_ref[...] = jnp.zeros_like(rt_ref)

        def step(j, carry, masked):
            r0c, r1c, acc = carry
            off = pl.multiple_of(j * tk, tk)
            kb = k_ref[pl.ds(off, tk), :].astype(BF16)
            vb = v_ref[pl.ds(off, tk), :]
            rt_ref[...] = jnp.where(lane == j, r0c, jnp.where(lane == j + HEAD, r1c, rt_ref[...]))
            rs = [r0c, r1c]
            for h in range(2):
                z_scr[...] = _dot_nt(qh[h], kb)
                for r0 in range(0, tq, STRIP):
                    rows = slice(r0, r0 + STRIP)
                    z = z_scr[rows, :] * scale
                    sp = jnp.maximum(z, 0.0) + jnp.log(1.0 + jnp.exp(-jnp.abs(z)))
                    keep = -sp
                    if masked:
                        keep = jnp.where(_sb_strict(i * tq + r0, j * tk, tk), keep, 0.0)
                    a_scr[rows, :] = z - sp
                    hi = keep.astype(BF16)
                    hi_scr[rows, :] = hi
                    lo_scr[rows, :] = (keep - hi.astype(F32)).astype(BF16)
                suf_scr[...] = _dot(hi_scr[...], u) + _dot(lo_scr[...], u)
                for r0 in range(0, tq, STRIP):
                    rows = slice(r0, r0 + STRIP)
                    w = jnp.exp(a_scr[rows, :] + suf_scr[rows, :] + _lanes(rs[h][rows, :], tk))
                    if masked:
                        w = jnp.where(_sb_strict(i * tq + r0, j * tk, tk), w, 0.0)
                    w_scr[rows, :] = w.astype(BF16)
                acc = acc + _dot(w_scr[...], jnp.where(masks[h], vb, 0.0).astype(BF16))
                rs[h] = rs[h] + (_dot(hi_scr[...], ones) + _dot(lo_scr[...], ones))
            return rs[0], rs[1], acc

        zero = jnp.zeros((tq, LANES), F32)
        carry = (zero, zero, zero)
        for t in range(per_q):
            carry = step((i + 1) * per_q - 1 - t, carry, True)
        _, _, acc = lax.fori_loop(0, i * per_q, lambda t, c: step(i * per_q - 1 - t, c, False), carry)
        o_ref[...] = acc

    seq = lambda blk0: pl.BlockSpec((None, S, LANES), lambda b, p, i: (b, 0, blk0 + p))
    out = pl.BlockSpec((None, tq, LANES), lambda b, p, i: (b, i, p))
    shp = jax.ShapeDtypeStruct((Bl, S, 2 * LANES), F32)
    f32_tile, bf_tile = pltpu.VMEM((tq, tk), F32), pltpu.VMEM((tq, tk), BF16)
    return call_with_plans(
        body, plans, name=name, grid=(Bl, 2, S // tq),
        in_specs=[pl.BlockSpec((None, tq, LANES), lambda b, p, i: (b, i, qb + p)), seq(kb0), seq(vb0)],
        out_specs=[out, out], out_shape=[shp, shp], scratch_shapes=[f32_tile, f32_tile, bf_tile, bf_tile, f32_tile, bf_tile],
        args=[proj3, proj3, proj3], sem=("arbitrary",) * 3 if plans else ("parallel", "parallel", "arbitrary"))


def sb_attn_bwd(proj3, rt3, do3, *, plans=None, name):
    Bl, S, _ = proj3.shape
    tk = min(SB_BLOCK, S)
    tq = min(SB_QBLOCK, S)
    per_q = tq // tk
    scale = HEAD ** -0.5
    qb, kb0, vb0 = P_SBQ // LANES, P_SBK // LANES, P_SBV // LANES

    def body(q_ref, k_ref, v_ref, rt_ref, do_ref, dq_ref, dk_ref, dv_ref, z_scr, sp_scr, hi_scr, lo_scr, sum_scr, dw_scr, w_scr, dz_scr):
        i = pl.program_id(2)

        @pl.when(i == 0)
        def _():
            dk_ref[...] = jnp.zeros_like(dk_ref)
            dv_ref[...] = jnp.zeros_like(dv_ref)

        masks = _lane_masks()
        q = q_ref[...]
        qh = [jnp.where(m, q, 0.0).astype(BF16) for m in masks]
        do_b = do_ref[...].astype(BF16)
        doh = [jnp.where(m, do_b, jnp.zeros_like(do_b)) for m in masks]
        rt = rt_ref[...]
        ur = lax.broadcasted_iota(jnp.int32, (tk, tk), 0)
        uc = lax.broadcasted_iota(jnp.int32, (tk, tk), 1)
        u_suffix = (ur > uc).astype(BF16)
        u_prefix = (ur < uc).astype(BF16)
        ones = jnp.ones((tk, LANES), BF16)
        sel_row = lax.broadcasted_iota(jnp.int32, (LANES, LANES), 0)

        def step(j, carry, masked):
            p0, p1, dq = carry
            off = pl.multiple_of(j * tk, tk)
            kf = k_ref[pl.ds(off, tk), :]
            kb = kf.astype(BF16)
            vb = v_ref[pl.ds(off, tk), :]
            ps = [p0, p1]
            dk_acc = jnp.zeros((tk, LANES), F32)
            dv_acc = jnp.zeros((tk, LANES), F32)
            for h in range(2):
                r_j = _split_dot(rt, (sel_row == j + h * HEAD).astype(BF16))
                z_scr[...] = _dot_nt(qh[h], kb)
                for r0 in range(0, tq, STRIP):
                    rows = slice(r0, r0 + STRIP)
                    z = z_scr[rows, :] * scale
                    sp = jnp.maximum(z, 0.0) + jnp.log(1.0 + jnp.exp(-jnp.abs(z)))
                    keep = -sp
                    if masked:
                        keep = jnp.where(_sb_strict(i * tq + r0, j * tk, tk), keep, 0.0)
                    z_scr[rows, :] = z
                    sp_scr[rows, :] = sp
                    hi = keep.astype(BF16)
                    hi_scr[rows, :] = hi
                    lo_scr[rows, :] = (keep - hi.astype(F32)).astype(BF16)
                sum_scr[...] = _dot(hi_scr[...], u_suffix) + _dot(lo_scr[...], u_suffix)
                dw_scr[...] = _dot_nt(doh[h], jnp.where(masks[h], vb, 0.0).astype(BF16))
                for r0 in range(0, tq, STRIP):
                    rows = slice(r0, r0 + STRIP)
                    w = jnp.exp((z_scr[rows, :] - sp_scr[rows, :]) + sum_scr[rows, :] + _lanes(r_j[rows, :], tk))
                    if masked:
                        w = jnp.where(_sb_strict(i * tq + r0, j * tk, tk), w, 0.0)
                    w_scr[rows, :] = w.astype(BF16)
                    g = dw_scr[rows, :] * w
                    hi = g.astype(BF16)
                    hi_scr[rows, :] = hi
                    lo_scr[rows, :] = (g - hi.astype(F32)).astype(BF16)
                sum_scr[...] = _dot(hi_scr[...], u_prefix) + _dot(lo_scr[...], u_prefix)
                for r0 in range(0, tq, STRIP):
                    rows = slice(r0, r0 + STRIP)
                    g = hi_scr[rows, :].astype(F32) + lo_scr[rows, :].astype(F32)
                    sp, z = sp_scr[rows, :], z_scr[rows, :]
                    pre = sum_scr[rows, :] + _lanes(ps[h][rows, :], tk)
                    dz = g * jnp.exp(-sp) - jnp.exp(z - sp) * pre
                    if masked:
                        dz = jnp.where(_sb_strict(i * tq + r0, j * tk, tk), dz, 0.0)
                    dz_scr[rows, :] = (dz * scale).astype(BF16)
                dzb = dz_scr[...]
                dq = dq + _dot(dzb, jnp.where(masks[h], kf, 0.0).astype(BF16))
                dk_acc = dk_acc + _dot_tn(dzb, qh[h])
                dv_acc = dv_acc + _dot_tn(w_scr[...], doh[h])
                ps[h] = ps[h] + (_dot(hi_scr[...], ones) + _dot(lo_scr[...], ones))
            dk_ref[pl.ds(off, tk), :] += dk_acc
            dv_ref[pl.ds(off, tk), :] += dv_acc
            return ps[0], ps[1], dq

        zero = jnp.zeros((tq, LANES), F32)
        carry = lax.fori_loop(0, i * per_q, lambda j, c: step(j, c, False), (zero, zero, zero))
        for t in range(per_q):
            carry = step(i * per_q + t, carry, True)
        dq_ref[...] = carry[2]

    seq_in = lambda blk0: pl.BlockSpec((None, S, LANES), lambda b, p, i: (b, 0, blk0 + p))
    blk = pl.BlockSpec((None, tq, LANES), lambda b, p, i: (b, i, p))
    seq_out = pl.BlockSpec((None, S, LANES), lambda b, p, i: (b, 0, p))
    shp = jax.ShapeDtypeStruct((Bl, S, 2 * LANES), F32)
    f32_tile, bf_tile = pltpu.VMEM((tq, tk), F32), pltpu.VMEM((tq, tk), BF16)
    return call_with_plans(
        body, plans, name=name, grid=(Bl, 2, S // tq),
        in_specs=[pl.BlockSpec((None, tq, LANES), lambda b, p, i: (b, i, qb + p)), seq_in(kb0), seq_in(vb0), blk, blk],
        out_specs=[blk, seq_out, seq_out], out_shape=[shp, shp, shp],
        scratch_shapes=[f32_tile, f32_tile, bf_tile, bf_tile, f32_tile, f32_tile, bf_tile, bf_tile],
        args=[proj3, proj3, proj3, rt3, do3], sem=("arbitrary",) * 3 if plans else ("parallel", "parallel", "arbitrary"))


def mla_attn_fwd(q3, k3, kv3, vblk0, *, tq=512, tk=256, plans=None, name):
    Bl, S, _ = q3.shape
    tq = min(tq, S)
    tk = min(tk, tq)
    per_q = tq // tk
    scale = MLA_QK ** -0.5

    def body(q_ref, k_ref, v_ref, o_ref, lse_ref):
        i = pl.program_id(2)
        masks = _lane_masks()
        rr = lax.broadcasted_iota(jnp.int32, (tq, tk), 0)
        cc = lax.broadcasted_iota(jnp.int32, (tq, tk), 1)
        qh = [q_ref[:, h * LANES:(h + 1) * LANES] for h in range(2)]

        def step(j, carry):
            m0, l0, m1, l1, acc = carry
            off = pl.multiple_of(j * tk, tk)
            vb = v_ref[pl.ds(off, tk), :]
            causal = (cc + j * tk) <= (rr + i * tq)
            ms, ls, alphas = [m0, m1], [l0, l1], []
            add = jnp.zeros((tq, LANES), F32)
            for h in range(2):
                kh = k_ref[pl.ds(off, tk), h * LANES:(h + 1) * LANES]
                s = jnp.where(causal, _dot_nt(qh[h], kh) * scale, NEG)
                m_new = jnp.maximum(ms[h], jnp.max(s, axis=1, keepdims=True))
                p = jnp.exp(s - m_new)
                alpha = jnp.exp(ms[h] - m_new)
                ls[h] = alpha * ls[h] + jnp.sum(p, axis=1, keepdims=True)
                ms[h] = m_new
                alphas.append(alpha)
                add = add + _dot(p.astype(BF16), jnp.where(masks[h], vb, 0.0).astype(BF16))
            acc = acc * jnp.where(masks[0], alphas[0], alphas[1]) + add
            return ms[0], ls[0], ms[1], ls[1], acc

        neg = jnp.full((tq, 1), NEG, F32)
        zero = jnp.zeros((tq, 1), F32)
        m0, l0, m1, l1, acc = lax.fori_loop(0, (i + 1) * per_q, step, (neg, zero, neg, zero, jnp.zeros((tq, LANES), F32)))
        o_ref[...] = acc / jnp.where(masks[0], l0, l1)
        lse_ref[...] = jnp.where(masks[0], m0 + jnp.log(l0), m1 + jnp.log(l1))

    out = pl.BlockSpec((None, tq, LANES), lambda b, p, i: (b, i, p))
    shp = jax.ShapeDtypeStruct((Bl, S, 3 * LANES), F32)
    return call_with_plans(
        body, plans, name=name, grid=(Bl, 3, S // tq),
        in_specs=[pl.BlockSpec((None, tq, 2 * LANES), lambda b, p, i: (b, i, p)), pl.BlockSpec((None, S, 2 * LANES), lambda b, p, i: (b, 0, p)),
                  pl.BlockSpec((None, S, LANES), lambda b, p, i: (b, 0, vblk0 + p))],
        out_specs=[out, out], out_shape=[shp, shp], scratch_shapes=[], args=[q3, k3, kv3],
        sem=("arbitrary",) * 3 if plans else ("parallel", "parallel", "arbitrary"))


def mla_attn_bwd(q3, k3, kv3, vblk0, o3, lse3, do3, *, tq=512, tk=256, name):
    Bl, S, _ = q3.shape
    tq = min(tq, S)
    tk = min(tk, tq)
    per_q = tq // tk
    nq = S // tq
    scale = MLA_QK ** -0.5

    def body(q_ref, k_ref, v_ref, o_ref, lse_ref, do_ref, dq_ref, dk_ref, dv_ref, s_scr, dp_scr, p_scr, ds_scr):
        j = pl.program_id(2)

        @pl.when(j == 0)
        def _():
            dq_ref[...] = jnp.zeros_like(dq_ref)

        masks = _lane_masks()
        vb = v_ref[...]
        vh = [jnp.where(m, vb, 0.0).astype(BF16) for m in masks]
        kh = [k_ref[:, h * LANES:(h + 1) * LANES] for h in range(2)]
        i0 = lax.div(j, jnp.int32(per_q))

        def step(i, carry, masked):
            dk0, dk1, dv = carry
            off = pl.multiple_of(i * tq, tq)
            do_b = do_ref[pl.ds(off, tq), :].astype(BF16)
            prod = do_b.astype(F32) * o_ref[pl.ds(off, tq), :]
            lse = lse_ref[pl.ds(off, tq), :]
            dks = [dk0, dk1]
            for h in range(2):
                qh = q_ref[pl.ds(off, tq), h * LANES:(h + 1) * LANES]
                doh = jnp.where(masks[h], do_b, jnp.zeros_like(do_b))
                delta = jnp.sum(jnp.where(masks[h], prod, 0.0), axis=1, keepdims=True)
                lse_h = lse[:, h * HEAD:h * HEAD + 1]
                s_scr[...] = _dot_nt(qh, kh[h])
                dp_scr[...] = _dot_nt(doh, vh[h])
                for r0 in range(0, tq, STRIP):
                    rows = slice(r0, r0 + STRIP)
                    s = s_scr[rows, :] * scale
                    if masked:
                        rr = lax.broadcasted_iota(jnp.int32, (STRIP, tk), 0) + (i * tq + r0)
                        cc = lax.broadcasted_iota(jnp.int32, (STRIP, tk), 1) + j * tk
                        s = jnp.where(cc <= rr, s, NEG)
                    p = jnp.exp(s - lse_h[rows])
                    p_scr[rows, :] = p.astype(BF16)
                    ds_scr[rows, :] = (p * (dp_scr[rows, :] - delta[rows])).astype(BF16)
                ds = ds_scr[...]
                dq_ref[pl.ds(off, tq), h * LANES:(h + 1) * LANES] += _dot(ds, kh[h]) * scale
                dks[h] = dks[h] + _dot_tn(ds, qh)
                dv = dv + _dot_tn(p_scr[...], doh)
            return dks[0], dks[1], dv

        zero = jnp.zeros((tk, LANES), F32)
        carry = step(i0, (zero, zero, zero), True)
        dk0, dk1, dv = lax.fori_loop(i0 + 1, nq, lambda i, c: step(i, c, False), carry)
        dk_ref[:, 0:LANES] = dk0 * scale
        dk_ref[:, LANES:2 * LANES] = dk1 * scale
        dv_ref[...] = dv

    seq1 = pl.BlockSpec((None, S, LANES), lambda b, p, j: (b, 0, p))
    seq2 = pl.BlockSpec((None, S, 2 * LANES), lambda b, p, j: (b, 0, p))
    return pl.pallas_call(
        body, name=name, grid=(Bl, 3, S // tk),
        in_specs=[seq2, pl.BlockSpec((None, tk, 2 * LANES), lambda b, p, j: (b, j, p)),
                  pl.BlockSpec((None, tk, LANES), lambda b, p, j: (b, j, vblk0 + p)), seq1, seq1, seq1],
        out_specs=[seq2, pl.BlockSpec((None, tk, 2 * LANES), lambda b, p, j: (b, j, p)), pl.BlockSpec((None, tk, LANES), lambda b, p, j: (b, j, p))],
        out_shape=[jax.ShapeDtypeStruct((Bl, S, 6 * LANES), F32), jax.ShapeDtypeStruct((Bl, S, 6 * LANES), F32), jax.ShapeDtypeStruct((Bl, S, 3 * LANES), F32)],
        scratch_shapes=[pltpu.VMEM((tq, tk), F32), pltpu.VMEM((tq, tk), F32), pltpu.VMEM((tq, tk), BF16), pltpu.VMEM((tq, tk), BF16)],
        compiler_params=_cp("parallel", "parallel", "arbitrary"),
    )(q3, k3, kv3, o3, lse3, do3)


def _bucket_table():
    a = jnp.arange(WINDOW)[:, None]
    b = jnp.arange(2 * WINDOW)[None, :]
    dist = WINDOW + a - b
    max_exact = REL_BUCKETS // 2
    n = jnp.maximum(dist, 0)
    nf = jnp.maximum(n, 1).astype(F32)
    large = max_exact + (jnp.log(nf / max_exact) / math.log(REL_MAX_DIST / max_exact) * (REL_BUCKETS - max_exact)).astype(jnp.int32)
    large = jnp.minimum(large, REL_BUCKETS - 1)
    bucket = jnp.where(n < max_exact, n, large)
    return jnp.where((dist >= 0) & (dist < WINDOW), bucket, -1).astype(jnp.int32)


def swa_bias(rel_flat, bucket, *, name):
    def body(t_ref, b_ref, o_ref):
        bk = b_ref[...]
        for p in range(3):
            for hh in range(2):
                h = hh * 3 + p
                acc = jnp.full(bk.shape, NEG, F32)
                for b in range(REL_BUCKETS):
                    acc = jnp.where(bk == b, t_ref[b * 6 + h], acc)
                o_ref[p, hh] = acc

    return pl.pallas_call(
        body, name=name,
        in_specs=[pl.BlockSpec(memory_space=pltpu.SMEM), pl.BlockSpec(memory_space=pltpu.VMEM)],
        out_specs=pl.BlockSpec(memory_space=pltpu.VMEM),
        out_shape=jax.ShapeDtypeStruct((3, 2, WINDOW, 2 * WINDOW), F32),
    )(rel_flat, bucket)


def swa_bias_bwd(dbias, bucket, *, name):
    Bl = dbias.shape[0]

    def body(d_ref, b_ref, o_ref):
        bk = b_ref[...]
        lane = lax.broadcasted_iota(jnp.int32, (1, LANES), 1)
        rows = []
        for h in range(6):
            hh, p = divmod(h, 3)
            d = d_ref[0, p, hh]
            for bl in range(1, Bl):
                d = d + d_ref[bl, p, hh]
            row = jnp.zeros((1, LANES), F32)
            for b in range(REL_BUCKETS):
                s = jnp.sum(jnp.sum(jnp.where(bk == b, d, 0.0), axis=1, keepdims=True), axis=0, keepdims=True)
                row = row + jnp.where(lane == b, s, 0.0)
            rows.append(row)
        rows += [jnp.zeros((1, LANES), F32)] * 2
        o_ref[...] = jnp.concatenate(rows, axis=0)

    return pl.pallas_call(
        body, name=name,
        in_specs=[pl.BlockSpec(memory_space=pltpu.VMEM)] * 2, out_specs=pl.BlockSpec(memory_space=pltpu.VMEM),
        out_shape=jax.ShapeDtypeStruct((8, LANES), F32),
    )(dbias, bucket)


SWA_QBLOCKS = 4


def _swa_specs(vblk, nqb):
    rows = nqb * WINDOW
    cur = lambda blk: pl.BlockSpec((None, rows, LANES), lambda b, p, n: (b, n, blk))
    prev = lambda blk: pl.BlockSpec((None, WINDOW, LANES), lambda b, p, n: (b, jnp.maximum(n * nqb - 1, 0), blk))
    return [pl.BlockSpec((None, rows, LANES), lambda b, p, n: (b, n, p)), cur(0), prev(0), cur(vblk), prev(vblk),
            pl.BlockSpec((None, 2, WINDOW, 2 * WINDOW), lambda b, p, n: (p, 0, 0, 0)), pl.BlockSpec((None, 2, LANES), lambda b, p, n: (p, 0, 0))]


def _rows128(ref, m):
    return ref[m * WINDOW:(m + 1) * WINDOW, :]


def _swa_logits(qh, kp, kc, bias_h, first, scale):
    sp = jnp.where(first, NEG, _dot_nt(qh, kp) * scale + bias_h[:, :WINDOW])
    sc = _dot_nt(qh, kc) * scale + bias_h[:, WINDOW:]
    return sp, sc


def swa_attn_fwd(qn3, kn3, proj3, bias, sinks, *, plans=None, name):
    Bl, S, _ = qn3.shape
    scale = HEAD ** -0.5
    nqb = min(SWA_QBLOCKS, S // WINDOW)

    def body(q_ref, kc_ref, kp_ref, vc_ref, vp_ref, b_ref, s_ref, o_ref, lse_ref):
        seq_start = pl.program_id(2) == 0
        masks = _lane_masks()
        for m_ in range(nqb):
            first = seq_start if m_ == 0 else False
            q = _rows128(q_ref, m_)
            kp = kp_ref[...] if m_ == 0 else _rows128(kc_ref, m_ - 1)
            vp = vp_ref[...] if m_ == 0 else _rows128(vc_ref, m_ - 1)
            kc, vc = _rows128(kc_ref, m_), _rows128(vc_ref, m_)
            o = jnp.zeros((WINDOW, LANES), F32)
            lses = []
            for h in range(2):
                qh = jnp.where(masks[h], q, jnp.zeros_like(q))
                sp, sc = _swa_logits(qh, kp, kc, b_ref[h], first, scale)
                sink = s_ref[h:h + 1, 0:1]
                m = jnp.maximum(jnp.maximum(jnp.max(sp, axis=1, keepdims=True), jnp.max(sc, axis=1, keepdims=True)), sink)
                ep, ec = jnp.exp(sp - m), jnp.exp(sc - m)
                l = jnp.sum(ep, axis=1, keepdims=True) + jnp.sum(ec, axis=1, keepdims=True) + jnp.exp(sink - m)
                inv = 1.0 / l
                o = o + _dot((ep * inv).astype(BF16), jnp.where(masks[h], vp, 0.0).astype(BF16))
                o = o + _dot((ec * inv).astype(BF16), jnp.where(masks[h], vc, 0.0).astype(BF16))
                lses.append(m + jnp.log(l))
            o_ref[m_ * WINDOW:(m_ + 1) * WINDOW, :] = o
            lse_ref[m_ * WINDOW:(m_ + 1) * WINDOW, :] = jnp.where(masks[0], lses[0], lses[1])

    out = pl.BlockSpec((None, nqb * WINDOW, LANES), lambda b, p, n: (b, n, p))
    shp = jax.ShapeDtypeStruct((Bl, S, 3 * LANES), F32)
    return call_with_plans(
        body, plans, name=name, grid=(Bl, 3, S // (nqb * WINDOW)), in_specs=_swa_specs(P_SWV // LANES, nqb),
        out_specs=[out, out], out_shape=[shp, shp], scratch_shapes=[], args=[qn3, kn3, kn3, proj3, proj3, bias, sinks],
        sem=("arbitrary",) * 3 if plans else ("parallel", "parallel", "arbitrary"))


def swa_attn_bwd(qn3, kn3, proj3, bias, sinks, o3, lse3, do3, *, name):
    Bl, S, _ = qn3.shape
    scale = HEAD ** -0.5
    nqb = min(SWA_QBLOCKS, S // WINDOW)
    rows = nqb * WINDOW

    def body(q_ref, kc_ref, kp_ref, vc_ref, vp_ref, b_ref, s_ref, o_ref, lse_ref, do_ref,
             dq_ref, dk_ref, dv_ref, db_ref, dsk_ref):
        p_id, n = pl.program_id(1), pl.program_id(2)
        seq_start = n == 0

        @pl.when((p_id == 0) & seq_start)
        def _():
            dk_ref[...] = jnp.zeros_like(dk_ref)
            dv_ref[...] = jnp.zeros_like(dv_ref)

        @pl.when(seq_start)
        def _():
            db_ref[...] = jnp.zeros_like(db_ref)
            dsk_ref[...] = jnp.zeros_like(dsk_ref)

        masks = _lane_masks()
        zero = jnp.zeros((WINDOW, LANES), F32)
        dk_acc = [zero] * (nqb + 1)
        dv_acc = [zero] * (nqb + 1)
        db_acc = [[jnp.zeros((WINDOW, WINDOW), F32)] * 2 for _ in range(2)]
        dsk_acc = [jnp.zeros((1, 1), F32)] * 2
        for m_ in range(nqb):
            first = seq_start if m_ == 0 else False
            q = _rows128(q_ref, m_)
            kp = kp_ref[...] if m_ == 0 else _rows128(kc_ref, m_ - 1)
            vp = vp_ref[...] if m_ == 0 else _rows128(vc_ref, m_ - 1)
            kc, vc = _rows128(kc_ref, m_), _rows128(vc_ref, m_)
            do_b = _rows128(do_ref, m_).astype(BF16)
            prod = do_b.astype(F32) * _rows128(o_ref, m_)
            lse = _rows128(lse_ref, m_)
            dq = zero
            for h in range(2):
                qh = jnp.where(masks[h], q, jnp.zeros_like(q))
                doh = jnp.where(masks[h], do_b, jnp.zeros_like(do_b))
                sp, sc = _swa_logits(qh, kp, kc, b_ref[h], first, scale)
                lse_h = lse[:, h * HEAD:h * HEAD + 1]
                pp, pc = jnp.exp(sp - lse_h), jnp.exp(sc - lse_h)
                delta = jnp.sum(jnp.where(masks[h], prod, 0.0), axis=1, keepdims=True)
                dsp = pp * (_dot_nt(doh, jnp.where(masks[h], vp, 0.0).astype(BF16)) - delta)
                dsc = pc * (_dot_nt(doh, jnp.where(masks[h], vc, 0.0).astype(BF16)) - delta)
                db_acc[h] = [db_acc[h][0] + dsp, db_acc[h][1] + dsc]
                psink = jnp.exp(s_ref[h:h + 1, 0:1] - lse_h)
                dsk_acc[h] = dsk_acc[h] - jnp.sum(psink * delta, axis=0, keepdims=True)
                dspb, dscb = (dsp * scale).astype(BF16), (dsc * scale).astype(BF16)
                dq = dq + _dot(dspb, jnp.where(masks[h], kp, jnp.zeros_like(kp))) + _dot(dscb, jnp.where(masks[h], kc, jnp.zeros_like(kc)))
                dk_acc[m_] = dk_acc[m_] + _dot_tn(dspb, qh)
                dk_acc[m_ + 1] = dk_acc[m_ + 1] + _dot_tn(dscb, qh)
                dv_acc[m_] = dv_acc[m_] + _dot_tn(pp.astype(BF16), doh)
                dv_acc[m_ + 1] = dv_acc[m_ + 1] + _dot_tn(pc.astype(BF16), doh)
            dq_ref[m_ * WINDOW:(m_ + 1) * WINDOW, :] = dq
        for h in range(2):
            db_ref[h, :, 0:WINDOW] += db_acc[h][0]
            db_ref[h, :, WINDOW:2 * WINDOW] += db_acc[h][1]
            dsk_ref[h:h + 1, :] += jnp.broadcast_to(dsk_acc[h], (1, LANES))
        offp = pl.multiple_of(jnp.maximum(n * nqb - 1, 0) * WINDOW, WINDOW)
        dk_ref[pl.ds(offp, WINDOW), :] += dk_acc[0]
        dv_ref[pl.ds(offp, WINDOW), :] += dv_acc[0]
        for m_ in range(nqb):
            off = pl.multiple_of(n * rows + m_ * WINDOW, WINDOW)
            dk_ref[pl.ds(off, WINDOW), :] += dk_acc[m_ + 1]
            dv_ref[pl.ds(off, WINDOW), :] += dv_acc[m_ + 1]

    blk = pl.BlockSpec((None, rows, LANES), lambda b, p, n: (b, n, p))
    seq = pl.BlockSpec((None, S, LANES), lambda b, p, n: (b, 0, 0))
    return pl.pallas_call(
        body, name=name, grid=(Bl, 3, S // rows), in_specs=_swa_specs(P_SWV // LANES, nqb) + [blk, blk, blk],
        out_specs=[blk, seq, seq, pl.BlockSpec((None, None, 2, WINDOW, 2 * WINDOW), lambda b, p, n: (b, p, 0, 0, 0)),
                   pl.BlockSpec((None, None, 2, LANES), lambda b, p, n: (b, p, 0, 0))],
        out_shape=[jax.ShapeDtypeStruct((Bl, S, 3 * LANES), F32), jax.ShapeDtypeStruct((Bl, S, LANES), F32), jax.ShapeDtypeStruct((Bl, S, LANES), F32),
                   jax.ShapeDtypeStruct((Bl, 3, 2, WINDOW, 2 * WINDOW), F32), jax.ShapeDtypeStruct((Bl, 3, 2, LANES), F32)],
        compiler_params=_cp("arbitrary", "arbitrary", "arbitrary"),
    )(qn3, kn3, kn3, proj3, proj3, bias, sinks, o3, lse3, do3)


CONV_ROWS = 64
CONV_LANES = 128


def _conv_strip(x_ref, h_ref, w, b, r0, cols, first_blk):
    x = x_ref[r0:r0 + CONV_ROWS, cols]
    if r0 == 0:
        rows = lax.broadcasted_iota(jnp.int32, x.shape, 0)
        h6 = jnp.where(first_blk, 0.0, h_ref[6:7, cols])
        h7 = jnp.where(first_blk, 0.0, h_ref[7:8, cols])
        x1 = jnp.where(rows == 0, h7, pltpu.roll(x, 1, 0))
        x2 = jnp.where(rows == 0, h6, jnp.where(rows == 1, h7, pltpu.roll(x, 2, 0)))
    else:
        x1 = x_ref[r0 - 1:r0 - 1 + CONV_ROWS, cols]
        x2 = x_ref[r0 - 2:r0 - 2 + CONV_ROWS, cols]
    return w[0:1] * x2 + w[1:2] * x1 + w[2:3] * x + b, x, x1, x2


FF_BLK = D_FF // 2


def _up_perm(a):
    q = FF_BLK
    return _cat([a[..., 0:q], a[..., 2 * q:3 * q], a[..., q:2 * q], a[..., 3 * q:4 * q]])


def conv_gate_fwd(up3, cw, cb, *, tm=256, name):
    Bl, S, _ = up3.shape
    tm = min(tm, S)
    W = 2 * FF_BLK

    def body(x_ref, h_ref, w_ref, b_ref, o_ref):
        first = pl.program_id(1) == 0

        def chunk(c, carry):
            cg = pl.ds(pl.multiple_of(c * CONV_LANES, CONV_LANES), CONV_LANES)
            cv = pl.ds(pl.multiple_of(FF_BLK + c * CONV_LANES, CONV_LANES), CONV_LANES)
            wg, wv, bg, bv = w_ref[:, cg], w_ref[:, cv], b_ref[:, cg], b_ref[:, cv]
            for r0 in range(0, tm, CONV_ROWS):
                ug = _conv_strip(x_ref, h_ref, wg, bg, r0, cg, first)[0]
                uv = _conv_strip(x_ref, h_ref, wv, bv, r0, cv, first)[0]
                o_ref[r0:r0 + CONV_ROWS, cg] = (ug * jax.nn.sigmoid(ug) * uv).astype(BF16)
            return carry

        lax.fori_loop(0, FF_BLK // CONV_LANES, chunk, 0)

    hb = tm // 8
    return pl.pallas_call(
        body, name=name, grid=(Bl, S // tm, 2),
        in_specs=[pl.BlockSpec((None, tm, W), lambda b, s, c: (b, s, c)),
                  pl.BlockSpec((None, 8, W), lambda b, s, c: (b, jnp.maximum(s * hb - 1, 0), c)),
                  pl.BlockSpec((3, W), lambda b, s, c: (0, c)), pl.BlockSpec((1, W), lambda b, s, c: (0, c))],
        out_specs=pl.BlockSpec((None, tm, FF_BLK), lambda b, s, c: (b, s, c)),
        out_shape=jax.ShapeDtypeStruct((Bl, S, D_FF), BF16),
        compiler_params=_cp("parallel", "parallel", "parallel"),
    )(up3, up3, cw, cb)


def conv_gate_bwd(up3, cw, cb, da3, *, tm=256, name):
    Bl, S, _ = up3.shape
    tm = min(tm, S)
    ns = S // tm
    W = 2 * FF_BLK

    def body(x_ref, h_ref, w_ref, b_ref, da_ref, dup_ref, dw_ref, nxt_ref, du_scr):
        b, s = pl.program_id(1), pl.program_id(2)
        seq_end = s == 0
        first = s == ns - 1

        @pl.when((b == 0) & seq_end)
        def _():
            dw_ref[...] = jnp.zeros_like(dw_ref)

        def du_chunk(c, carry):
            cg = pl.ds(pl.multiple_of(c * CONV_LANES, CONV_LANES), CONV_LANES)
            cv = pl.ds(pl.multiple_of(FF_BLK + c * CONV_LANES, CONV_LANES), CONV_LANES)
            wg, wv, bg, bv = w_ref[:, cg], w_ref[:, cv], b_ref[:, cg], b_ref[:, cv]
            acc_g = [jnp.zeros((1, CONV_LANES), F32)] * 4
            acc_v = [jnp.zeros((1, CONV_LANES), F32)] * 4
            for r0 in range(0, tm, CONV_ROWS):
                ug, xg, xg1, xg2 = _conv_strip(x_ref, h_ref, wg, bg, r0, cg, first)
                uv, xv, xv1, xv2 = _conv_strip(x_ref, h_ref, wv, bv, r0, cv, first)
                da = da_ref[r0:r0 + CONV_ROWS, cg].astype(F32)
                sg = jax.nn.sigmoid(ug)
                dug = da * uv * sg * (1.0 + ug * (1.0 - sg))
                duv = da * ug * sg
                du_scr[r0:r0 + CONV_ROWS, cg] = dug
                du_scr[r0:r0 + CONV_ROWS, cv] = duv
                col = lambda t: jnp.sum(t, axis=0, keepdims=True)
                acc_g = [acc_g[0] + col(dug * xg2), acc_g[1] + col(dug * xg1), acc_g[2] + col(dug * xg), acc_g[3] + col(dug)]
                acc_v = [acc_v[0] + col(duv * xv2), acc_v[1] + col(duv * xv1), acc_v[2] + col(duv * xv), acc_v[3] + col(duv)]
            for t in range(4):
                dw_ref[t:t + 1, cg] += acc_g[t]
                dw_ref[t:t + 1, cv] += acc_v[t]
            return carry

        lax.fori_loop(0, FF_BLK // CONV_LANES, du_chunk, 0)
        du_scr[tm:tm + 8, :] = jnp.where(seq_end, 0.0, nxt_ref[...])

        def dup_chunk(c, carry):
            cols = pl.ds(pl.multiple_of(c * CONV_LANES, CONV_LANES), CONV_LANES)
            w = w_ref[:, cols]
            for r0 in range(0, tm, CONV_ROWS):
                d0 = du_scr[r0:r0 + CONV_ROWS, cols]
                d1 = du_scr[r0 + 1:r0 + 1 + CONV_ROWS, cols]
                d2 = du_scr[r0 + 2:r0 + 2 + CONV_ROWS, cols]
                dup_ref[r0:r0 + CONV_ROWS, cols] = (w[2:3] * d0 + w[1:2] * d1 + w[0:1] * d2).astype(BF16)
            return carry

        lax.fori_loop(0, W // CONV_LANES, dup_chunk, 0)
        nxt_ref[...] = du_scr[0:8, :]

    hb = tm // 8
    rb = lambda s: ns - 1 - s
    return pl.pallas_call(
        body, name=name, grid=(2, Bl, ns),
        in_specs=[pl.BlockSpec((None, tm, W), lambda c, b, s: (b, rb(s), c)),
                  pl.BlockSpec((None, 8, W), lambda c, b, s: (b, jnp.maximum(rb(s) * hb - 1, 0), c)),
                  pl.BlockSpec((3, W), lambda c, b, s: (0, c)), pl.BlockSpec((1, W), lambda c, b, s: (0, c)),
                  pl.BlockSpec((None, tm, FF_BLK), lambda c, b, s: (b, rb(s), c))],
        out_specs=[pl.BlockSpec((None, tm, W), lambda c, b, s: (b, rb(s), c)), pl.BlockSpec((8, W), lambda c, b, s: (0, c))],
        out_shape=[jax.ShapeDtypeStruct((Bl, S, 2 * D_FF), BF16), jax.ShapeDtypeStruct((8, 2 * D_FF), F32)],
        scratch_shapes=[pltpu.VMEM((8, W), F32), pltpu.VMEM((tm + 8, W), F32)],
        compiler_params=_cp("arbitrary", "arbitrary", "arbitrary"),
    )(up3, up3, cw, cb, da3)


def gate_bwd(dx3, y3, gate, *, tm=512, name):
    Bl, S, D = dx3.shape
    tm = min(tm, S)

    def body(dx_ref, y_ref, g_ref, o_ref, dg_ref):
        @pl.when(pl.program_id(1) == 0)
        def _():
            dg_ref[...] = jnp.zeros_like(dg_ref)

        dx = dx_ref[...]
        dg_ref[...] += jnp.sum(dx * y_ref[...], axis=0, keepdims=True)
        o_ref[...] = (dx * g_ref[...]).astype(BF16)

    blk = pl.BlockSpec((None, tm, D), lambda b, s: (b, s, 0))
    vec = pl.BlockSpec((None, 1, D), lambda b, s: (b, 0, 0))
    return pl.pallas_call(
        body, name=name, grid=(Bl, S // tm), in_specs=[blk, blk, vec], out_specs=[blk, vec],
        out_shape=[jax.ShapeDtypeStruct((Bl, S, D), BF16), jax.ShapeDtypeStruct((Bl, 1, D), F32)],
        compiler_params=_cp("parallel", "arbitrary"),
    )(dx3, y3, gate)


def loss_grad(y3, t3, *, tm=512, name):
    Bl, S, D = y3.shape
    tm = min(tm, S)
    last = (Bl - 1, S // tm - 1)

    def body(y_ref, t_ref, dy_ref, l_ref, acc_ref):
        b, s = pl.program_id(0), pl.program_id(1)

        @pl.when((b == 0) & (s == 0))
        def _():
            acc_ref[...] = jnp.zeros_like(acc_ref)

        e = y_ref[...] - t_ref[...]
        dy_ref[...] = e * (1.0 / D)
        acc_ref[...] += jnp.sum(e * e, axis=0, keepdims=True)

        @pl.when((b == last[0]) & (s == last[1]))
        def _():
            l_ref[...] = jnp.broadcast_to(jnp.sum(acc_ref[...], axis=1, keepdims=True) * (0.5 / D), (1, LANES))

    blk = pl.BlockSpec((None, tm, D), lambda b, s: (b, s, 0))
    return pl.pallas_call(
        body, name=name, grid=(Bl, S // tm), in_specs=[blk, blk],
        out_specs=[blk, pl.BlockSpec((1, LANES), lambda b, s: (0, 0))],
        out_shape=[jax.ShapeDtypeStruct((Bl, S, D), F32), jax.ShapeDtypeStruct((1, LANES), F32)],
        scratch_shapes=[pltpu.VMEM((1, D), F32)], compiler_params=_cp("arbitrary", "arbitrary"),
    )(y3, t3)


def adamw(w, g, m, v, *, name):
    R, C = w.shape
    tr = R
    for cand in (512, 256, 128, 64, 32, 16, 8):
        if R > cand and R % cand == 0:
            tr = cand
            break
    c1 = 1.0 / (1.0 - ADAM_B1 ** ADAM_STEP)
    c2 = 1.0 / (1.0 - ADAM_B2 ** ADAM_STEP)

    def body(w_ref, g_ref, m_ref, v_ref, d_ref, m2_ref, v2_ref):
        gg = g_ref[...]
        m2 = ADAM_B1 * m_ref[...] + (1.0 - ADAM_B1) * gg
        v2 = ADAM_B2 * v_ref[...] + (1.0 - ADAM_B2) * (gg * gg)
        m2_ref[...] = m2
        v2_ref[...] = v2
        d_ref[...] = -ADAM_LR * ((m2 * c1) / (jnp.sqrt(v2 * c2) + ADAM_EPS) + ADAM_WD * w_ref[...])

    blk = pl.BlockSpec((tr, C), lambda i: (i, 0))
    shp = jax.ShapeDtypeStruct((R, C), F32)
    return pl.pallas_call(
        body, name=name, grid=(R // tr,), in_specs=[blk] * 4, out_specs=[blk] * 3, out_shape=[shp] * 3,
        compiler_params=_cp("parallel"),
    )(w, g, m, v)


def sum_leading(x, *, out_dtype=F32, tr=256, name):
    n, R, C = x.shape
    tr = _tile(R, tr, 16)

    def body(x_ref, o_ref):
        acc = x_ref[0].astype(F32)
        for k in range(1, n):
            acc = acc + x_ref[k].astype(F32)
        o_ref[...] = acc.astype(out_dtype)

    return pl.pallas_call(
        body, name=name, grid=(R // tr,), in_specs=[pl.BlockSpec((n, tr, C), lambda i: (0, i, 0))],
        out_specs=pl.BlockSpec((tr, C), lambda i: (i, 0)), out_shape=jax.ShapeDtypeStruct((R, C), out_dtype),
        compiler_params=_cp("parallel"),
    )(x)


def _adam_update(w, g, m, v):
    c1 = 1.0 / (1.0 - ADAM_B1 ** ADAM_STEP)
    c2 = 1.0 / (1.0 - ADAM_B2 ** ADAM_STEP)
    m2 = ADAM_B1 * m + (1.0 - ADAM_B1) * g
    v2 = ADAM_B2 * v + (1.0 - ADAM_B2) * (g * g)
    return -ADAM_LR * ((m2 * c1) / (jnp.sqrt(v2 * c2) + ADAM_EPS) + ADAM_WD * w), m2, v2


def adamw_small(ws, gs, ms, vs, *, name):
    na = len(ws)

    def body(*refs):
        w_r, g_r, m_r, v_r = (refs[i * na:(i + 1) * na] for i in range(4))
        d_r, m2_r, v2_r = (refs[(4 + i) * na:(5 + i) * na] for i in range(3))
        for a in range(na):
            d_r[a][...], m2_r[a][...], v2_r[a][...] = _adam_update(w_r[a][...], g_r[a][...], m_r[a][...], v_r[a][...])

    vm = pl.BlockSpec(memory_space=pltpu.VMEM)
    shp = [jax.ShapeDtypeStruct(w.shape, F32) for w in ws]
    out = pl.pallas_call(body, name=name, in_specs=[vm] * (4 * na), out_specs=[vm] * (3 * na), out_shape=shp * 3)(*ws, *gs, *ms, *vs)
    return out[:na], out[na:2 * na], out[2 * na:]


def sum_small(xs, *, name):
    na = len(xs)

    def body(*refs):
        for x_ref, o_ref in zip(refs[:na], refs[na:]):
            acc = x_ref[0]
            for k in range(1, x_ref.shape[0]):
                acc = acc + x_ref[k]
            o_ref[...] = acc

    vm = pl.BlockSpec(memory_space=pltpu.VMEM)
    return pl.pallas_call(body, name=name, in_specs=[vm] * na, out_specs=[vm] * na,
                          out_shape=[jax.ShapeDtypeStruct(x.shape[1:], x.dtype) for x in xs])(*xs)


def pair_add_half(g4, recv, c_arr, *, tr=512, name):
    _, R, C = g4.shape
    H = R // 2
    tr = _tile(H, tr, 16)
    nb = H // tr

    def body(c_ref, g_ref, r_ref, o_ref):
        o_ref[...] = (g_ref[...].astype(F32) + r_ref[...].astype(F32)).astype(BF16)

    grid_spec = pltpu.PrefetchScalarGridSpec(
        num_scalar_prefetch=1, grid=(4, nb),
        in_specs=[pl.BlockSpec((None, tr, C), lambda k, i, c_ref: (k, c_ref[0] * nb + i, 0)),
                  pl.BlockSpec((None, tr, C), lambda k, i, c_ref: (k, i, 0))],
        out_specs=pl.BlockSpec((None, tr, C), lambda k, i, c_ref: (k, i, 0)),
    )
    return pl.pallas_call(
        body, name=name, grid_spec=grid_spec, out_shape=jax.ShapeDtypeStruct((4, H, C), BF16),
        compiler_params=_cp("parallel", "parallel"),
    )(c_arr, g4, recv)


def chip_sum_into(landed, pair, sel, *, tr=512, name):
    _, H, C = landed.shape
    tr = _tile(H, tr, 16)
    nb = H // tr

    def body(s_ref, l0, l1, l2, l3, p_ref, o_ref):
        own = p_ref[...].astype(F32)
        acc = None
        for k, l_ref in enumerate((l0, l1, l2, l3)):
            part = jnp.where(s_ref[0] == k, own, l_ref[...].astype(F32))
            acc = part if acc is None else acc + part
        o_ref[...] = acc

    def slot(k):
        return pl.BlockSpec((None, tr, C), lambda i, s: (jnp.where(s[0] == k, (k + 1) % 4, k), i, 0))

    grid_spec = pltpu.PrefetchScalarGridSpec(
        num_scalar_prefetch=1, grid=(nb,),
        in_specs=[slot(0), slot(1), slot(2), slot(3), pl.BlockSpec((None, tr, C), lambda i, s: (s[0], i, 0))],
        out_specs=pl.BlockSpec((tr, C), lambda i, s: (s[1] * nb + i, 0)),
    )
    return pl.pallas_call(
        body, name=name, grid_spec=grid_spec, out_shape=jax.ShapeDtypeStruct((2 * H, C), F32), compiler_params=_cp("parallel"),
    )(sel, landed, landed, landed, landed, pair)


def mods_matmul(c_all, w_ada, b_ada_cols, *, tn=512, name):
    L, D, E = w_ada.shape
    nb = c_all.shape[0]
    tn = _tile(E, tn)

    def body(c_ref, w_ref, b_ref, o_ref):
        c = c_ref[...]
        a = c * jax.nn.sigmoid(c)
        o_ref[...] = jnp.dot(a, w_ref[...], preferred_element_type=F32, precision=lax.Precision.HIGHEST) + b_ref[...]

    return pl.pallas_call(
        body, name=name, grid=(L, E // tn),
        in_specs=[pl.BlockSpec((nb, D), lambda l, j: (0, 0)), pl.BlockSpec((None, D, tn), lambda l, j: (l, 0, j)),
                  pl.BlockSpec((None, 1, tn), lambda l, j: (l, 0, j))],
        out_specs=pl.BlockSpec((None, nb, tn), lambda l, j: (l, 0, j)),
        out_shape=jax.ShapeDtypeStruct((L, nb, E), F32), compiler_params=_cp("parallel", "parallel"),
    )(c_all, w_ada, b_ada_cols)


def ada_grad(c_all, dmods, *, tn=512, name):
    L, nb, E = dmods.shape
    D = c_all.shape[1]
    tn = _tile(E, tn)

    def body(c_ref, d_ref, o_ref):
        c = c_ref[...]
        a = c * jax.nn.sigmoid(c)
        o_ref[...] = lax.dot_general(a, d_ref[...], (((0,), (0,)), ((), ())), preferred_element_type=F32, precision=lax.Precision.HIGHEST)

    return pl.pallas_call(
        body, name=name, grid=(L, E // tn),
        in_specs=[pl.BlockSpec((nb, D), lambda l, j: (0, 0)), pl.BlockSpec((None, nb, tn), lambda l, j: (l, 0, j))],
        out_specs=pl.BlockSpec((None, D, tn), lambda l, j: (l, 0, j)),
        out_shape=jax.ShapeDtypeStruct((L, D, E), F32), compiler_params=_cp("parallel", "parallel"),
    )(c_all, dmods)


HBM = pl.BlockSpec(memory_space=pltpu.HBM)


def _me():
    return lax.axis_index("x"), lax.axis_index("y"), lax.axis_index("c")


def _flip(v, bit):
    return 1 - v if bit else v


def allgather8(xs, *, name):
    na = len(xs)

    def body(*refs):
        x_refs, out_refs = refs[:na], refs[na:2 * na]
        send_sems, recv_sems = refs[2 * na], refs[2 * na + 1]
        x, y, c = _me()
        me = 4 * x + 2 * y + c
        for x_ref, out_ref in zip(x_refs, out_refs):
            out_ref[me] = x_ref[...]
        sends = []
        for a, (x_ref, out_ref) in enumerate(zip(x_refs, out_refs)):
            for k in range(1, 8):
                peer = (_flip(x, k & 4), _flip(y, k & 2), _flip(c, k & 1))
                cp = pltpu.make_async_remote_copy(src_ref=x_ref, dst_ref=out_ref.at[me], send_sem=send_sems.at[a, k - 1],
                                                  recv_sem=recv_sems.at[a, k - 1], device_id=peer, device_id_type=MESH)
                cp.start()
                sends.append(cp)
        for a, (x_ref, out_ref) in enumerate(zip(x_refs, out_refs)):
            for k in range(1, 8):
                peer = (_flip(x, k & 4), _flip(y, k & 2), _flip(c, k & 1))
                src = 4 * peer[0] + 2 * peer[1] + peer[2]
                pltpu.make_async_remote_copy(src_ref=x_ref, dst_ref=out_ref.at[src], send_sem=send_sems.at[a, k - 1],
                                             recv_sem=recv_sems.at[a, k - 1], device_id=peer, device_id_type=MESH).wait_recv()
        for cp in sends:
            cp.wait_send()

    vm = pl.BlockSpec(memory_space=pltpu.VMEM)
    return pl.pallas_call(
        body, name=name, in_specs=[vm] * na, out_specs=[vm] * na,
        out_shape=[jax.ShapeDtypeStruct((8,) + a.shape, a.dtype) for a in xs],
        scratch_shapes=[pltpu.SemaphoreType.DMA((na, 7)), pltpu.SemaphoreType.DMA((na, 7))],
    )(*xs)


LOCAL_CHUNKS = 8


def _copy_via_vmem(src, dst_at, rows, buf, sem):
    ch = buf.shape[0]
    for i in range(rows // ch):
        load = pltpu.make_async_copy(src.at[pl.ds(i * ch, ch)], buf, sem)
        load.start()
        load.wait()
        store = pltpu.make_async_copy(buf, dst_at(i * ch, ch), sem)
        store.start()
        store.wait()


def _chunk_buf(rows, cols, dtype):
    align = 16 if dtype == BF16 else 8
    for n in range(LOCAL_CHUNKS, 0, -1):
        if rows % n == 0 and (rows // n) % align == 0:
            return pltpu.VMEM((rows // n, cols), dtype)
    return pltpu.VMEM((rows, cols), dtype)


def gather_weights(ws, *, name):
    na = len(ws)

    def body(*refs):
        x_refs, out_refs = refs[:na], refs[na:2 * na]
        send_sems, recv_sems, local_sem = refs[2 * na:2 * na + 3]
        bufs = refs[2 * na + 3:]
        x, y, c = _me()
        j = 2 * x + y
        chips = [(_flip(x, k & 2), _flip(y, k & 1)) for k in range(1, 4)]
        sends = []
        for a, (x_ref, out_ref) in enumerate(zip(x_refs, out_refs)):
            H = x_ref.shape[0] // 2
            for k, (px, py) in enumerate(chips):
                cp = pltpu.make_async_remote_copy(src_ref=x_ref.at[pl.ds(c * H, H)], dst_ref=out_ref.at[j, pl.ds(c * H, H)],
                                                  send_sem=send_sems.at[a, k], recv_sem=recv_sems.at[a, k],
                                                  device_id=(px, py, c), device_id_type=MESH)
                cp.start()
                sends.append(cp)
        for x_ref, out_ref, buf in zip(x_refs, out_refs, bufs):
            _copy_via_vmem(x_ref, lambda o, n, out_ref=out_ref: out_ref.at[j, pl.ds(o, n)], x_ref.shape[0], buf, local_sem)
        for a, out_ref in enumerate(out_refs):
            H = out_ref.shape[1] // 2
            for k, (px, py) in enumerate(chips):
                slot = out_ref.at[2 * px + py, pl.ds(c * H, H)]
                pltpu.make_async_remote_copy(src_ref=slot, dst_ref=slot, send_sem=send_sems.at[a, k], recv_sem=recv_sems.at[a, k],
                                             device_id=(px, py, c), device_id_type=MESH).wait_recv()
                cp = pltpu.make_async_remote_copy(src_ref=slot, dst_ref=slot, send_sem=send_sems.at[a, 3 + k],
                                                  recv_sem=recv_sems.at[a, 3 + k], device_id=(x, y, 1 - c), device_id_type=MESH)
                cp.start()
                sends.append(cp)
        for a, out_ref in enumerate(out_refs):
            H = out_ref.shape[1] // 2
            for k, (px, py) in enumerate(chips):
                slot = out_ref.at[2 * px + py, pl.ds((1 - c) * H, H)]
                pltpu.make_async_remote_copy(src_ref=slot, dst_ref=slot, send_sem=send_sems.at[a, 3 + k], recv_sem=recv_sems.at[a, 3 + k],
                                             device_id=(x, y, 1 - c), device_id_type=MESH).wait_recv()
        for cp in sends:
            cp.wait_send()

    return pl.pallas_call(
        body, name=name, in_specs=[HBM] * na, out_specs=[HBM] * na,
        out_shape=[jax.ShapeDtypeStruct((4,) + w.shape, w.dtype) for w in ws],
        scratch_shapes=[pltpu.SemaphoreType.DMA((na, 6)), pltpu.SemaphoreType.DMA((na, 6)), pltpu.SemaphoreType.DMA]
        + [_chunk_buf(w.shape[0], w.shape[1], w.dtype) for w in ws],
    )(*ws)


def swap_halves(gs, *, name):
    na = len(gs)

    def body(*refs):
        g_refs, out_refs = refs[:na], refs[na:2 * na]
        send_sems, recv_sems = refs[2 * na:]
        x, y, c = _me()
        sib = (x, y, 1 - c)
        sends = []
        for a, (g_ref, out_ref) in enumerate(zip(g_refs, out_refs)):
            H = g_ref.shape[1] // 2
            for k in range(4):
                cp = pltpu.make_async_remote_copy(src_ref=g_ref.at[k, pl.ds((1 - c) * H, H)], dst_ref=out_ref.at[k],
                                                  send_sem=send_sems.at[a, k], recv_sem=recv_sems.at[a, k], device_id=sib, device_id_type=MESH)
                cp.start()
                sends.append(cp)
        for a, (g_ref, out_ref) in enumerate(zip(g_refs, out_refs)):
            H = g_ref.shape[1] // 2
            for k in range(4):
                pltpu.make_async_remote_copy(src_ref=g_ref.at[k, pl.ds(c * H, H)], dst_ref=out_ref.at[k], send_sem=send_sems.at[a, k],
                                             recv_sem=recv_sems.at[a, k], device_id=sib, device_id_type=MESH).wait_recv()
        for cp in sends:
            cp.wait_send()

    return pl.pallas_call(
        body, name=name, in_specs=[HBM] * na, out_specs=[HBM] * na,
        out_shape=[jax.ShapeDtypeStruct((4, g.shape[1] // 2, g.shape[2]), g.dtype) for g in gs],
        scratch_shapes=[pltpu.SemaphoreType.DMA((na, 4)), pltpu.SemaphoreType.DMA((na, 4))],
    )(*gs)


def scatter_chips(ps, *, name):
    na = len(ps)

    def body(*refs):
        p_refs, out_refs = refs[:na], refs[na:2 * na]
        send_sems, recv_sems, local_sem = refs[2 * na:2 * na + 3]
        bufs = refs[2 * na + 3:]
        x, y, c = _me()
        j = 2 * x + y
        chips = [(_flip(x, k & 2), _flip(y, k & 1)) for k in range(1, 4)]
        sends = []
        for a, (p_ref, out_ref) in enumerate(zip(p_refs, out_refs)):
            for k, (px, py) in enumerate(chips):
                cp = pltpu.make_async_remote_copy(src_ref=p_ref.at[2 * px + py], dst_ref=out_ref.at[j], send_sem=send_sems.at[a, k],
                                                  recv_sem=recv_sems.at[a, k], device_id=(px, py, c), device_id_type=MESH)
                cp.start()
                sends.append(cp)
        for p_ref, out_ref, buf in zip(p_refs, out_refs, bufs):
            _copy_via_vmem(p_ref.at[j], lambda o, n, out_ref=out_ref: out_ref.at[j, pl.ds(o, n)], p_ref.shape[1], buf, local_sem)
        for a, out_ref in enumerate(out_refs):
            for k, (px, py) in enumerate(chips):
                slot = out_ref.at[2 * px + py]
                pltpu.make_async_remote_copy(src_ref=slot, dst_ref=slot, send_sem=send_sems.at[a, k], recv_sem=recv_sems.at[a, k],
                                             device_id=(px, py, c), device_id_type=MESH).wait_recv()
        for cp in sends:
            cp.wait_send()

    return pl.pallas_call(
        body, name=name, in_specs=[HBM] * na, out_specs=[HBM] * na, out_shape=[jax.ShapeDtypeStruct(p.shape, p.dtype) for p in ps],
        scratch_shapes=[pltpu.SemaphoreType.DMA((na, 3)), pltpu.SemaphoreType.DMA((na, 3)), pltpu.SemaphoreType.DMA]
        + [_chunk_buf(p.shape[1], p.shape[2], p.dtype) for p in ps],
    )(*ps)


def join_halves(halves, *, name):
    na = len(halves)

    def body(*refs):
        h_refs, out_refs = refs[:na], refs[na:2 * na]
        send_sems, recv_sems, local_sem = refs[2 * na:2 * na + 3]
        bufs = refs[2 * na + 3:]
        x, y, c = _me()
        sib = (x, y, 1 - c)
        sends = []
        for a, (h_ref, out_ref) in enumerate(zip(h_refs, out_refs)):
            H = h_ref.shape[0]
            cp = pltpu.make_async_remote_copy(src_ref=h_ref, dst_ref=out_ref.at[pl.ds(c * H, H)], send_sem=send_sems.at[a],
                                              recv_sem=recv_sems.at[a], device_id=sib, device_id_type=MESH)
            cp.start()
            sends.append(cp)
        for h_ref, out_ref, buf in zip(h_refs, out_refs, bufs):
            H = h_ref.shape[0]
            _copy_via_vmem(h_ref, lambda o, n, out_ref=out_ref, H=H: out_ref.at[pl.ds(c * H + o, n)], H, buf, local_sem)
        for a, (h_ref, out_ref) in enumerate(zip(h_refs, out_refs)):
            H = h_ref.shape[0]
            pltpu.make_async_remote_copy(src_ref=h_ref, dst_ref=out_ref.at[pl.ds((1 - c) * H, H)], send_sem=send_sems.at[a],
                                         recv_sem=recv_sems.at[a], device_id=sib, device_id_type=MESH).wait_recv()
        for cp in sends:
            cp.wait_send()

    return pl.pallas_call(
        body, name=name, in_specs=[HBM] * na, out_specs=[HBM] * na,
        out_shape=[jax.ShapeDtypeStruct((2 * h.shape[0], h.shape[1]), h.dtype) for h in halves],
        scratch_shapes=[pltpu.SemaphoreType.DMA((na,)), pltpu.SemaphoreType.DMA((na,)), pltpu.SemaphoreType.DMA]
        + [_chunk_buf(h.shape[0], h.shape[1], h.dtype) for h in halves],
    )(*halves)


class _Plan:
    def __init__(self, ins, out_shapes, ncopies, copies, aliased=False):
        self.ins, self.out_shapes, self.ncopies, self.copies, self.aliased = list(ins), list(out_shapes), ncopies, copies, aliased

    def start(self, in_refs, out_refs, send_sems, recv_sems):
        sends, _ = self.copies(in_refs, out_refs, send_sems, recv_sems)
        for cp in sends:
            cp.start()

    def finish(self, in_refs, out_refs, send_sems, recv_sems):
        sends, recvs = self.copies(in_refs, out_refs, send_sems, recv_sems)
        for cp in recvs:
            cp.wait_recv()
        for cp in sends:
            cp.wait_send()


def _rcopy(src, dst, send_sems, recv_sems, idx, dev):
    return pltpu.make_async_remote_copy(src_ref=src, dst_ref=dst, send_sem=send_sems.at[idx], recv_sem=recv_sems.at[idx],
                                        device_id=dev, device_id_type=MESH)


def _other_chips(x, y):
    return [(_flip(x, k & 2), _flip(y, k & 1)) for k in range(1, 4)]


def plan_gather_ici(ws):
    def copies(in_refs, out_refs, ss, rs):
        x, y, c = _me()
        j = 2 * x + y
        sends, recvs = [], []
        for a, (x_ref, out_ref) in enumerate(zip(in_refs, out_refs)):
            H = x_ref.shape[0] // 2
            for k, (px, py) in enumerate(_other_chips(x, y)):
                sends.append(_rcopy(x_ref.at[pl.ds(c * H, H)], out_ref.at[j, pl.ds(c * H, H)], ss, rs, 3 * a + k, (px, py, c)))
                slot = out_ref.at[2 * px + py, pl.ds(c * H, H)]
                recvs.append(_rcopy(slot, slot, ss, rs, 3 * a + k, (px, py, c)))
        return sends, recvs

    return _Plan(ws, [jax.ShapeDtypeStruct((4,) + w.shape, w.dtype) for w in ws], 3 * len(ws), copies)


def plan_gather_d2d(w4s):
    def copies(in_refs, out_refs, ss, rs):
        x, y, c = _me()
        sends, recvs = [], []
        for a, out_ref in enumerate(out_refs):
            H = out_ref.shape[1] // 2
            for k, (px, py) in enumerate(_other_chips(x, y)):
                mine = out_ref.at[2 * px + py, pl.ds(c * H, H)]
                theirs = out_ref.at[2 * px + py, pl.ds((1 - c) * H, H)]
                sends.append(_rcopy(mine, mine, ss, rs, 3 * a + k, (x, y, 1 - c)))
                recvs.append(_rcopy(theirs, theirs, ss, rs, 3 * a + k, (x, y, 1 - c)))
        return sends, recvs

    return _Plan(w4s, [jax.ShapeDtypeStruct(w.shape, w.dtype) for w in w4s], 3 * len(w4s), copies, aliased=True)


def plan_swap_halves(gs):
    def copies(in_refs, out_refs, ss, rs):
        x, y, c = _me()
        sends, recvs = [], []
        for a, (g_ref, out_ref) in enumerate(zip(in_refs, out_refs)):
            H = g_ref.shape[1] // 2
            for k in range(4):
                sends.append(_rcopy(g_ref.at[k, pl.ds((1 - c) * H, H)], out_ref.at[k], ss, rs, 4 * a + k, (x, y, 1 - c)))
                recvs.append(_rcopy(g_ref.at[k, pl.ds(c * H, H)], out_ref.at[k], ss, rs, 4 * a + k, (x, y, 1 - c)))
        return sends, recvs

    return _Plan(gs, [jax.ShapeDtypeStruct((4, g.shape[1] // 2, g.shape[2]), g.dtype) for g in gs], 4 * len(gs), copies)


def plan_scatter_ici(ps):
    def copies(in_refs, out_refs, ss, rs):
        x, y, c = _me()
        j = 2 * x + y
        sends, recvs = [], []
        for a, (p_ref, out_ref) in enumerate(zip(in_refs, out_refs)):
            for k, (px, py) in enumerate(_other_chips(x, y)):
                sends.append(_rcopy(p_ref.at[2 * px + py], out_ref.at[j], ss, rs, 3 * a + k, (px, py, c)))
                slot = out_ref.at[2 * px + py]
                recvs.append(_rcopy(slot, slot, ss, rs, 3 * a + k, (px, py, c)))
        return sends, recvs

    return _Plan(ps, [jax.ShapeDtypeStruct(p.shape, p.dtype) for p in ps], 3 * len(ps), copies)


def plan_join_halves(fulls):
    def copies(in_refs, out_refs, ss, rs):
        x, y, c = _me()
        sends, recvs = [], []
        for a, out_ref in enumerate(out_refs):
            H = out_ref.shape[0] // 2
            mine, theirs = out_ref.at[pl.ds(c * H, H)], out_ref.at[pl.ds((1 - c) * H, H)]
            sends.append(_rcopy(mine, mine, ss, rs, a, (x, y, 1 - c)))
            recvs.append(_rcopy(theirs, theirs, ss, rs, a, (x, y, 1 - c)))
        return sends, recvs

    return _Plan(fulls, [jax.ShapeDtypeStruct(f.shape, f.dtype) for f in fulls], len(fulls), copies, aliased=True)


def call_with_plans(body, plans, *, grid, in_specs, out_specs, out_shape, scratch_shapes, args, sem, name):
    plans = list(plans or [])
    n_in, n_out, n_scr = len(in_specs), len(out_specs), len(scratch_shapes)
    c_in = [len(p.ins) for p in plans]
    c_out = [len(p.out_shapes) for p in plans]
    steps = math.prod(grid) if grid else 1

    def wrapped(*refs):
        pos = 0

        def take(n):
            nonlocal pos
            out = refs[pos:pos + n]
            pos += n
            return out

        ins = take(n_in)
        cins = [take(n) for n in c_in]
        outs = take(n_out)
        couts = [take(n) for n in c_out]
        scr = take(n_scr)
        sems = [take(2) for _ in plans]
        def start_all():
            for p, ci, co, (ss, rs) in zip(plans, cins, couts, sems):
                p.start(ci, co, ss, rs)

        def finish_all():
            for p, ci, co, (ss, rs) in zip(plans, cins, couts, sems):
                p.finish(ci, co, ss, rs)

        if plans and grid:
            idx = 0
            for ax, g in enumerate(grid):
                idx = idx * g + pl.program_id(ax)
            pl.when(idx == 0)(start_all)
        elif plans:
            start_all()
        if body is not None:
            body(*ins, *outs, *scr)
        if plans and grid:
            pl.when(idx == steps - 1)(finish_all)
        elif plans:
            finish_all()

    aliases = {}
    i_pos, o_pos = n_in, n_out
    for p, ni, no in zip(plans, c_in, c_out):
        if p.aliased:
            aliases.update({i_pos + t: o_pos + t for t in range(ni)})
        i_pos += ni
        o_pos += no
    kwargs = dict(grid=grid) if grid else {}
    if aliases:
        kwargs["input_output_aliases"] = aliases
    res = pl.pallas_call(
        wrapped, name=name, in_specs=list(in_specs) + [HBM] * sum(c_in), out_specs=list(out_specs) + [HBM] * sum(c_out),
        out_shape=list(out_shape) + [s for p in plans for s in p.out_shapes],
        scratch_shapes=list(scratch_shapes) + [pltpu.SemaphoreType.DMA((p.ncopies,)) for p in plans for _ in range(2)],
        compiler_params=_cp(*sem) if grid else pltpu.CompilerParams(vmem_limit_bytes=VMEM_LIMIT), **kwargs,
    )(*args, *[a for p in plans for a in p.ins])
    res = list(res)
    comp, rest = res[:n_out], res[n_out:]
    pouts = []
    for no in c_out:
        pouts.append(rest[:no])
        rest = rest[no:]
    return comp, pouts


def run_plans(plans, *, name):
    return call_with_plans(None, plans, grid=(), in_specs=[], out_specs=[], out_shape=[], scratch_shapes=[], args=[], sem=(), name=name)[1]


def _cat(parts, axis=-1):
    return jnp.concatenate(parts, axis=axis)


def _prep_w_in(w):
    z = lambda n: jnp.zeros((w.shape[0], n), w.dtype)
    swq = w[:, 1184:1568]
    return _cat([w[:, 0:1152], z(64), w[:, 1152:1184], z(32)] + [swq[:, HEAD * h:HEAD * (h + 1)] for h in SW_PERM] + [w[:, 1568:1824]])


def _unprep_w_in(g):
    swq = g[:, P_SWQ:P_SWK]
    return _cat([g[:, 0:1152], g[:, 1216:1248]] + [swq[:, HEAD * SW_PERM.index(h):HEAD * (SW_PERM.index(h) + 1)] for h in range(6)] + [g[:, P_SWK:P_END]])


def _prep_w_uq(w):
    z = jnp.zeros((w.shape[0], 32), w.dtype)
    return _cat([p for h in range(6) for p in (w[:, MLA_QK * h:MLA_QK * (h + 1)], z)])


def _unprep_w_uq(g):
    return _cat([g[:, LANES * h:LANES * h + MLA_QK] for h in range(6)])


def _prep_w_ukv(w):
    z = jnp.zeros((w.shape[0], HEAD), w.dtype)
    return _cat([p for h in range(6) for p in (w[:, LANES * h:LANES * h + HEAD], z)] + [w[:, LANES * h + HEAD:LANES * (h + 1)] for h in range(6)])


def _unprep_w_ukv(g):
    return _cat([p for h in range(6) for p in (g[:, LANES * h:LANES * h + HEAD], g[:, 768 + HEAD * h:768 + HEAD * (h + 1)])])


def _prep_w_out(w):
    return _cat([w[0:640]] + [w[640 + HEAD * h:640 + HEAD * (h + 1)] for h in SW_PERM], axis=0)


def _unprep_w_out(g):
    return _cat([g[0:640]] + [g[640 + HEAD * SW_PERM.index(h):640 + HEAD * (SW_PERM.index(h) + 1)] for h in range(6)], axis=0)


def _rope_tables(positions):
    half = 16
    inv_freq = jnp.power(ROPE_THETA, -jnp.arange(half, dtype=F32) / half)
    ang = positions.astype(F32)[..., None] * inv_freq
    cos, sin = jnp.cos(ang), jnp.sin(ang)
    z = lambda n: jnp.zeros(ang.shape[:-1] + (n,), F32)
    return (_cat([jnp.ones(ang.shape[:-1] + (HEAD,), F32), cos, cos, z(32)]), _cat([z(HEAD), -sin, z(16), z(32)]), _cat([z(HEAD), z(16), sin, z(32)]))


def _small_params(p):
    pad96 = lambda g: _cat([g, jnp.zeros((32,), F32)]).reshape(1, LANES)
    two = lambda g: _cat([g, g]).reshape(1, LANES)
    sinks = jnp.broadcast_to(p["sw_sinks"].reshape(2, 3).T[:, :, None], (3, 2, LANES))
    return dict(n1=p["norm1_g"].reshape(1, -1), n2=p["norm2_g"].reshape(1, -1), cq_g=p["mla_cq_g"].reshape(1, -1),
                ckv_g=p["mla_ckv_g"].reshape(1, -1), qn_g=pad96(p["mla_qn_g"]), kn_g=pad96(p["mla_kn_g"]),
                swq_g=two(p["sw_qn_g"]), swk_g=two(p["sw_kn_g"]), sinks=sinks, conv_b=_up_perm(p["conv_b"]).reshape(1, -1))


class _NoFlow:
    def plans(self, tag):
        return []

    def done(self, tag, outs):
        pass

    def add(self, key, g):
        pass


def _layer_fwd(x3, md, W, tabs, bias, tag, flow=_NoFlow()):
    Bl, S, D = x3.shape
    T = Bl * S
    n = lambda s: f"{s}_{tag}"
    two = lambda a: a.reshape(T, a.shape[-1])
    three = lambda a: a.reshape(Bl, S, a.shape[-1])
    h = rms_fwd(x3, 0, D, W["n1"], md["scale1"], md["shift1"], name=n("norm1"))
    proj = three(matmul(two(h), W["w_in"], tn=1920, name=n("in_proj")))
    (o_a, rt_a), got = sb_attn_fwd(proj, plans=flow.plans(n("sb_fwd")), name=n("sb_fwd"))
    flow.done(n("sb_fwd"), got)
    cqn = rms_fwd(proj, P_CQ // 256, 256, W["cq_g"], name=n("cq_norm"))
    ckvn = rms_fwd(proj, P_CKV // LANES, LANES, W["ckv_g"], name=n("ckv_norm"))
    qb = three(matmul(two(cqn), W["w_uq"], tm=1024, tn=768, name=n("uq")))
    kvb = three(matmul(two(ckvn), W["w_ukv"], tm=1024, tn=1152, name=n("ukv")))
    q_m = rope_norm_fwd(qb, 6, W["qn_g"], tabs, name=n("q_rope"))
    k_m = rope_norm_fwd(kvb, 6, W["kn_g"], tabs, (proj, P_SLAB // LANES), name=n("k_rope"))
    (o_b, lse_b), got = mla_attn_fwd(q_m, k_m, kvb, 6, plans=flow.plans(n("mla_fwd")), name=n("mla_fwd"))
    flow.done(n("mla_fwd"), got)
    q_c = pair_rms_fwd(proj, P_SWQ // LANES, 3, W["swq_g"], name=n("swq_norm"))
    k_c = pair_rms_fwd(proj, P_SWK // LANES, 1, W["swk_g"], name=n("swk_norm"))
    (o_c, lse_c), got = swa_attn_fwd(q_c, k_c, proj, bias, W["sinks"], plans=flow.plans(n("swa_fwd")), name=n("swa_fwd"))
    flow.done(n("swa_fwd"), got)
    mix = _cat([o_a, o_b, o_c]).astype(BF16)
    att, x1 = matmul_res(two(mix), W["w_out"], two(x3), md["gate1"], S, name=n("out_proj"))
    x1 = three(x1)
    h2 = rms_fwd(x1, 0, D, W["n2"], md["scale2"], md["shift2"], name=n("norm2"))
    up = three(matmul(two(h2), W["w_up"], tn=1408, name=n("up_proj")))
    a = conv_gate_fwd(up, W["conv_w"], W["conv_b"], name=n("conv_gate"))
    yd, x2 = matmul_res(two(a), W["w_down"], two(x1), md["gate2"], S, name=n("down_proj"))
    saved = dict(x=x3, h=h, proj=proj, rt_a=rt_a, cqn=cqn, ckvn=ckvn, qb=qb, kvb=kvb, q_m=q_m, k_m=k_m, o_b=o_b, lse_b=lse_b,
                 q_c=q_c, k_c=k_c, o_c=o_c, lse_c=lse_c, mix=mix, att=three(att), x1=x1, h2=h2, up=up, a=a, yd=three(yd))
    return three(x2), saved


def _layer_bwd(dx2, sv, md, W, tabs, bias, tag, flow=_NoFlow()):
    Bl, S, D = dx2.shape
    T = Bl * S
    n = lambda s: f"{s}_{tag}"
    two = lambda a: a.reshape(T, a.shape[-1])
    three = lambda a: a.reshape(Bl, S, a.shape[-1])
    g = {}
    dyb, dgate2 = gate_bwd(dx2, sv["yd"], md["gate2"], name=n("gate2_bwd"))
    da = three(matmul(two(dyb), W["w_down"], tb=True, tn=1408, name=n("down_dx")))
    g["w_down"] = matmul(two(sv["a"]), two(dyb), ta=True, tm=256, tn=1024, name=n("down_dw"))
    dup, dcw = conv_gate_bwd(sv["up"], W["conv_w"], W["conv_b"], da, name=n("conv_gate_bwd"))
    dh2 = three(matmul(two(dup), W["w_up"], tb=True, tn=1024, name=n("up_dx")))
    g["w_up"] = matmul(two(sv["h2"]), two(dup), ta=True, tn=1408, name=n("up_dw"))
    dx1, dn2, dsc2, dsh2 = rms_bwd(sv["x1"], 0, D, dh2, W["n2"], md["scale2"], dx2, name=n("norm2_bwd"))
    dmo, dgate1 = gate_bwd(dx1, sv["att"], md["gate1"], name=n("gate1_bwd"))
    dmix = three(matmul(two(dmo), W["w_out"], tb=True, tn=1024, out_dtype=BF16, name=n("out_dx")))
    g["w_out"] = matmul(two(sv["mix"]), two(dmo), ta=True, tn=1024, name=n("out_dw"))
    proj = sv["proj"]
    for k in ("w_down", "w_up", "w_out"):
        flow.add((tag, k), g[k])
    (dq_a, dk_a, dv_a), got = sb_attn_bwd(proj, sv["rt_a"], dmix[:, :, 0:256], plans=flow.plans(n("sb_bwd")), name=n("sb_bwd"))
    flow.done(n("sb_bwd"), got)
    dq_m, dk_m, dv_b = mla_attn_bwd(sv["q_m"], sv["k_m"], sv["kvb"], 6, sv["o_b"], sv["lse_b"], dmix[:, :, 256:640], name=n("mla_bwd"))
    dqb, dqn = rope_norm_bwd(sv["qb"], 6, dq_m, W["qn_g"], tabs, name=n("q_rope_bwd"))
    dkn_x, dkn, dslab = rope_norm_bwd(sv["kvb"], 6, dk_m, W["kn_g"], tabs, (proj, P_SLAB // LANES), name=n("k_rope_bwd"))
    dkvb = _cat([dkn_x, dv_b]).astype(BF16)
    dckvn = three(matmul(two(dkvb), W["w_ukv"], tb=True, tm=1024, name=n("ukv_dx")))
    g["w_ukv"] = matmul(two(sv["ckvn"]), two(dkvb), ta=True, tn=1152, name=n("ukv_dw"))
    dcqn = three(matmul(two(dqb), W["w_uq"], tb=True, tm=1024, name=n("uq_dx")))
    g["w_uq"] = matmul(two(sv["cqn"]), two(dqb), ta=True, tn=768, name=n("uq_dw"))
    dcq, dcq_g = rms_bwd(proj, P_CQ // 256, 256, dcqn, W["cq_g"], name=n("cq_norm_bwd"))
    dckv, dckv_g = rms_bwd(proj, P_CKV // LANES, LANES, dckvn, W["ckv_g"], name=n("ckv_norm_bwd"))
    dq_c, dk_c, dv_c, dbias, dsink = swa_attn_bwd(sv["q_c"], sv["k_c"], proj, bias, W["sinks"], sv["o_c"], sv["lse_c"], dmix[:, :, 640:1024], name=n("swa_bwd"))
    dswq, dswq_g = pair_rms_bwd(proj, P_SWQ // LANES, 3, dq_c, W["swq_g"], name=n("swq_norm_bwd"))
    dswk, dswk_g = pair_rms_bwd(proj, P_SWK // LANES, 1, dk_c, W["swk_g"], name=n("swk_norm_bwd"))
    dproj = _cat([dq_a, dk_a, dv_a, dcq, dckv, dslab, dswq, dswk, dv_c]).astype(BF16)
    dh = three(matmul(two(dproj), W["w_in"], tb=True, tn=1024, name=n("in_dx")))
    g["w_in"] = matmul(two(sv["h"]), two(dproj), ta=True, tn=1920, tk=2048, name=n("in_dw"))
    dx, dn1, dsc1, dsh1 = rms_bwd(sv["x"], 0, D, dh, W["n1"], md["scale1"], dx1, name=n("norm1_bwd"))
    small = dict(n1=dn1, n2=dn2, cq_g=dcq_g, ckv_g=dckv_g, qn_g=dqn, kn_g=dkn, swq_g=dswq_g, swk_g=dswk_g, conv=dcw)
    dmods = _cat([dsh1, dsc1, dgate1, dsh2, dsc2, dgate2]).reshape(Bl, 6 * D)
    for k in ("w_ukv", "w_uq", "w_in"):
        flow.add((tag, k), g[k])
    return dx, g, small, dmods, dbias, dsink


BIG = ("w_in", "w_uq", "w_ukv", "w_out", "w_up", "w_down")
ROW_SHARDED = ("w_out", "w_down")
PREP = dict(w_in=_prep_w_in, w_uq=_prep_w_uq, w_ukv=_prep_w_ukv, w_out=_prep_w_out, w_up=_up_perm, w_down=lambda w: w)
UNPREP = dict(w_in=_unprep_w_in, w_uq=_unprep_w_uq, w_ukv=_unprep_w_ukv, w_out=_unprep_w_out, w_up=_up_perm, w_down=lambda w: w)
NCHIPS = 4


def _local_step(x, target, positions, mods, Wl, rel_flat, fwd_flow=_NoFlow(), bwd_flow=_NoFlow()):
    Bl, S, D = x.shape
    L = len(Wl)
    tabs = _rope_tables(positions)
    bucket = _bucket_table()
    bias = swa_bias(rel_flat, bucket, name="swa_bias")
    mds = []
    for l in range(L):
        parts = [mods[l, :, D * k:D * (k + 1)].reshape(Bl, 1, D) for k in range(6)]
        mds.append(dict(zip(("shift1", "scale1", "gate1", "shift2", "scale2", "gate2"), parts)))
    saved = []
    h = x
    for l in range(L):
        h, sv = _layer_fwd(h, mds[l], Wl[l], tabs, bias, f"l{l}", fwd_flow)
        saved.append(sv)
    dy, loss = loss_grad(h, target, name="loss")
    grads, smalls, dmods, dbiases, dsinks = [None] * L, [None] * L, [None] * L, [None] * L, [None] * L
    for l in reversed(range(L)):
        dy, grads[l], smalls[l], dmods[l], dbiases[l], dsinks[l] = _layer_bwd(dy, saved[l], mds[l], Wl[l], tabs, bias, f"l{l}", bwd_flow)
    drel = swa_bias_bwd(_cat(dbiases, axis=0), bucket, name="swa_bias_bwd")
    return loss, dy, grads, smalls, dmods, dsinks, drel


ATT = ("w_in", "w_uq", "w_ukv", "w_out")
FFN = ("w_up", "w_down")
GATHER_STAGES = {
    "sb_fwd_l0": ([("l0", k) for k in FFN], []),
    "mla_fwd_l0": ([("l1", k) for k in ATT + ("w_up",)], [("l0", k) for k in FFN]),
    "swa_fwd_l0": ([("l1", "w_down")], [("l1", k) for k in ATT + ("w_up",)]),
    "sb_fwd_l1": ([], [("l1", "w_down")]),
}
SCATTER_STAGES = {
    "sb_bwd_l1": [("l1", k) for k in FFN],
    "sb_bwd_l0": [("l1", k) for k in ATT] + [("l0", k) for k in FFN],
}


class _GatherFlow:
    def __init__(self, shards, chip):
        self.shards, self.chip, self.ici, self.d2d, self.pending = shards, chip, {}, {}, {}

    def early(self, keys):
        ici, = run_plans([plan_gather_ici([self.shards[k] for k in keys])], name="gather_early_ici")
        d2d, = run_plans([plan_gather_d2d(ici)], name="gather_early_d2d")
        self.d2d.update(zip(keys, d2d))

    def plans(self, tag):
        ici_keys, d2d_keys = GATHER_STAGES.get(tag, ([], []))
        plans = []
        if d2d_keys:
            plans.append(plan_gather_d2d([self.ici[k] for k in d2d_keys]))
        if ici_keys:
            plans.append(plan_gather_ici([self.shards[k] for k in ici_keys]))
        self.pending[tag] = (ici_keys, d2d_keys)
        return plans

    def done(self, tag, outs):
        ici_keys, d2d_keys = self.pending.pop(tag, ([], []))
        outs = list(outs)
        if d2d_keys:
            self.d2d.update(zip(d2d_keys, outs.pop(0)))
        if ici_keys:
            self.ici.update(zip(ici_keys, outs.pop(0)))

    def weight(self, key):
        k = key[1]
        own = self.shards[key]
        r, cc = own.shape
        w4 = lax.dynamic_update_slice(self.d2d[key], own[None], (self.chip, 0, 0))
        fw = w4.reshape(NCHIPS * r, cc) if k in ROW_SHARDED else jnp.transpose(w4, (1, 0, 2)).reshape(r, NCHIPS * cc)
        return PREP[k](fw)


class _LayerWeights(dict):
    def __init__(self, small, flow, tag):
        super().__init__(small)
        self.flow, self.tag = flow, tag

    def __missing__(self, k):
        self[k] = self.flow.weight((self.tag, k))
        return self[k]


class _ScatterFlow:
    def __init__(self, shapes, sel, c_arr):
        self.shapes, self.sel, self.c_arr = shapes, sel, c_arr
        self.g, self.pairs, self.landed, self.pending = {}, {}, {}, {}

    def add(self, key, g):
        self.g[key] = g

    def _pairs(self, keys, label):
        g4s = []
        for key in keys:
            k = key[1]
            r, cc = self.shapes[k]
            gk = UNPREP[k](self.g[key])
            g4 = gk.reshape(NCHIPS, r, cc) if k in ROW_SHARDED else jnp.transpose(gk.reshape(r, NCHIPS, cc), (1, 0, 2))
            g4s.append(g4.astype(BF16))
        theirs, = run_plans([plan_swap_halves(g4s)], name=f"rs_swap_{label}")
        pairs = [pair_add_half(g4, th, self.c_arr, name=f"rs_pair_add_{key[1]}_{key[0]}") for key, g4, th in zip(keys, g4s, theirs)]
        self.pairs.update(zip(keys, pairs))
        return pairs

    def plans(self, tag):
        keys = SCATTER_STAGES.get(tag, [])
        self.pending[tag] = keys
        return [plan_scatter_ici(self._pairs(keys, tag))] if keys else []

    def done(self, tag, outs):
        keys = self.pending.pop(tag, [])
        if keys:
            self.landed.update(zip(keys, outs[0]))

    def finish(self):
        rest = [key for key in self.g if key not in self.pairs]
        if rest:
            landed, = run_plans([plan_scatter_ici(self._pairs(rest, "rest"))], name="rs_scatter_rest")
            self.landed.update(zip(rest, landed))
        keys = list(self.pairs)
        fulls = [chip_sum_into(self.landed[key], self.pairs[key], self.sel, name=f"rs_chip_sum_{key[1]}_{key[0]}") for key in keys]
        joined, = run_plans([plan_join_halves(fulls)], name="rs_join_halves")
        return dict(zip(keys, joined))


WEIGHTS = ("rel_table", "norm1_g", "norm2_g", "w_ada", "b_ada", "w_in", "mla_cq_g", "w_uq", "mla_ckv_g", "w_ukv", "mla_qn_g", "mla_kn_g",
           "sw_qn_g", "sw_kn_g", "sw_sinks", "w_out", "w_up", "conv_w", "conv_b", "w_down")
SMALL = tuple(n for n in WEIGHTS if n not in BIG + ("w_ada",))


def kernel(x, c, positions, rel_table, norm1_g, norm2_g, w_ada, b_ada, w_in, mla_cq_g, w_uq, mla_ckv_g, w_ukv, mla_qn_g, mla_kn_g, sw_qn_g, sw_kn_g, sw_sinks, w_out, w_up, conv_w, conv_b, w_down, loss_target, m_rel_table, m_norm1_g, m_norm2_g, m_w_ada, m_b_ada, m_w_in, m_mla_cq_g, m_w_uq, m_mla_ckv_g, m_w_ukv, m_mla_qn_g, m_mla_kn_g, m_sw_qn_g, m_sw_kn_g, m_sw_sinks, m_w_out, m_w_up, m_conv_w, m_conv_b, m_w_down, v_rel_table, v_norm1_g, v_norm2_g, v_w_ada, v_b_ada, v_w_in, v_mla_cq_g, v_w_uq, v_mla_ckv_g, v_w_ukv, v_mla_qn_g, v_mla_kn_g, v_sw_qn_g, v_sw_kn_g, v_sw_sinks, v_w_out, v_w_up, v_conv_w, v_conv_b, v_w_down):
    w = dict(rel_table=rel_table, norm1_g=norm1_g, norm2_g=norm2_g, w_ada=w_ada, b_ada=b_ada, w_in=w_in, mla_cq_g=mla_cq_g, w_uq=w_uq,
             mla_ckv_g=mla_ckv_g, w_ukv=w_ukv, mla_qn_g=mla_qn_g, mla_kn_g=mla_kn_g, sw_qn_g=sw_qn_g, sw_kn_g=sw_kn_g, sw_sinks=sw_sinks,
             w_out=w_out, w_up=w_up, conv_w=conv_w, conv_b=conv_b, w_down=w_down)
    m = dict(rel_table=m_rel_table, norm1_g=m_norm1_g, norm2_g=m_norm2_g, w_ada=m_w_ada, b_ada=m_b_ada, w_in=m_w_in, mla_cq_g=m_mla_cq_g,
             w_uq=m_w_uq, mla_ckv_g=m_mla_ckv_g, w_ukv=m_w_ukv, mla_qn_g=m_mla_qn_g, mla_kn_g=m_mla_kn_g, sw_qn_g=m_sw_qn_g,
             sw_kn_g=m_sw_kn_g, sw_sinks=m_sw_sinks, w_out=m_w_out, w_up=m_w_up, conv_w=m_conv_w, conv_b=m_conv_b, w_down=m_w_down)
    v = dict(rel_table=v_rel_table, norm1_g=v_norm1_g, norm2_g=v_norm2_g, w_ada=v_w_ada, b_ada=v_b_ada, w_in=v_w_in, mla_cq_g=v_mla_cq_g,
             w_uq=v_w_uq, mla_ckv_g=v_mla_ckv_g, w_ukv=v_w_ukv, mla_qn_g=v_mla_qn_g, mla_kn_g=v_mla_kn_g, sw_qn_g=v_sw_qn_g,
             sw_kn_g=v_sw_kn_g, sw_sinks=v_sw_sinks, w_out=v_w_out, w_up=v_w_up, conv_w=v_conv_w, conv_b=v_conv_b, w_down=v_w_down)
    Bl, S, D = x.shape
    L = norm1_g.shape[0]
    xi, yi, ci = _me()
    chip = 2 * xi + yi
    dev = 4 * xi + 2 * yi + ci
    ndev = 2 * NCHIPS

    shapes = {k: w[k].shape[1:] for k in BIG}
    shards = {(f"l{l}", k): w[k][l].astype(BF16) for l in range(L) for k in BIG}
    gflow = _GatherFlow(shards, chip)
    gflow.early([("l0", k) for k in ATT])

    cw_cols = conv_w.shape[2]
    c_got, cw_got = allgather8([c, conv_w.reshape(L * 3, cw_cols)], name="gather_cond")
    c_all = c_got.reshape(ndev * Bl, D)
    conv_full = jnp.transpose(cw_got[0::2].reshape(NCHIPS, L, 3, cw_cols), (1, 2, 0, 3)).reshape(L, 3, NCHIPS * cw_cols)
    E = w_ada.shape[2]
    b_cols = lax.dynamic_slice(b_ada, (0, chip * E), (L, E)).reshape(L, 1, E)
    mods_cols = mods_matmul(c_all, w_ada, b_cols, name="mods")
    mods_all, = allgather8([mods_cols.reshape(L * ndev * Bl, E)], name="gather_mods")
    mods_all = jnp.transpose(mods_all[0::2].reshape(NCHIPS, L, ndev * Bl, E), (1, 2, 0, 3)).reshape(L, ndev * Bl, NCHIPS * E)
    mods = lax.dynamic_slice(mods_all, (0, dev * Bl, 0), (L, Bl, NCHIPS * E))

    Wl = []
    for l in range(L):
        Wd = _small_params({k: w[k][l] for k in SMALL if k not in ("rel_table", "b_ada", "conv_w")})
        Wd["conv_w"] = _up_perm(conv_full[l])
        Wl.append(_LayerWeights(Wd, gflow, f"l{l}"))

    sflow = _ScatterFlow(shapes, jnp.stack([chip, ci]).astype(jnp.int32), ci.reshape(1).astype(jnp.int32))
    loss, dx, _, smalls, dmods, dsinks, drel = _local_step(x, loss_target, positions, mods, Wl, rel_table.reshape(-1), gflow, sflow)
    reduced = sflow.finish()
    grad = {k: jnp.stack([reduced[(f"l{l}", k)] for l in range(L)]) for k in BIG}

    vec_names = ("n1", "n2", "cq_g", "ckv_g", "qn_g", "kn_g", "swq_g", "swk_g")
    vecs = _cat([_cat([smalls[l][k] for k in vec_names], axis=1) for l in range(L)], axis=0)
    convs = _cat([smalls[l]["conv"] for l in range(L)], axis=0)
    dm = jnp.stack(dmods, axis=1).reshape(Bl * L, 6 * D)
    dsk = jnp.stack(dsinks, axis=1).reshape(Bl * L * 6, LANES)
    got = allgather8([vecs, convs, drel, loss, dm, dsk], name="gather_small_grads")
    seq = lambda a, rows: a.reshape(ndev * Bl, rows, a.shape[-1])
    vec_s, conv_s, rel_s, loss_s, dm_s, dsk_s = sum_small(list(got[:4]) + [seq(got[4], L), seq(got[5], L * 6)], name="sum_small_grads")
    dm_all = jnp.transpose(seq(got[4], L), (1, 0, 2))
    grad["w_ada"] = ada_grad(c_all, lax.dynamic_slice(dm_all, (0, 0, chip * E), (L, ndev * Bl, E)), name="ada_grad")
    grad["b_ada"] = dm_s
    grad["sw_sinks"] = jnp.transpose(dsk_s.reshape(L, 3, 2, LANES)[:, :, :, 0], (0, 2, 1)).reshape(L, 6)
    grad["rel_table"] = rel_s[:6, :REL_BUCKETS].T
    off = 0
    for k, name_, keep in zip(vec_names, ("norm1_g", "norm2_g", "mla_cq_g", "mla_ckv_g", "mla_qn_g", "mla_kn_g", "sw_qn_g", "sw_kn_g"),
                              (D, D, 256, LANES, MLA_QK, MLA_QK, HEAD, HEAD)):
        grad[name_] = vec_s[:, off:off + keep]
        off += smalls[0][k].shape[1]
    conv = _up_perm(conv_s.reshape(L, 8, 2 * D_FF))
    grad["conv_w"] = lax.dynamic_slice(conv[:, 0:3], (0, 0, chip * cw_cols), (L, 3, cw_cols))
    grad["conv_b"] = conv[:, 3]
    loss_out = loss_s[0, 0]

    delta, new_m, new_v = {}, {}, {}
    for k in BIG + ("w_ada",):
        shp = w[k].shape
        to2 = lambda a: a.reshape(-1, shp[-1])
        d_, m_, v_ = adamw(to2(w[k]), to2(grad[k]), to2(m[k]), to2(v[k]), name=f"adamw_{k}")
        delta[k], new_m[k], new_v[k] = d_.reshape(shp), m_.reshape(shp), v_.reshape(shp)
    outs = adamw_small(*[[src[k] for k in SMALL] for src in (w, grad, m, v)], name="adamw_small")
    for dst, o in zip((delta, new_m, new_v), outs):
        dst.update(dict(zip(SMALL, o)))
    return (loss_out, dx, *[grad[k] for k in WEIGHTS], *[delta[k] for k in WEIGHTS], *[new_m[k] for k in WEIGHTS], *[new_v[k] for k in WEIGHTS])
```

```python
import functools
import math

import jax
import jax.numpy as jnp
from jax import lax
from jax.experimental import pallas as pl
from jax.experimental.pallas import tpu as pltpu

F32 = jnp.float32
BF16 = jnp.bfloat16
MESH = pl.DeviceIdType.MESH

EPS = 1e-6
NEG = -1e30
HEAD = 64
LANES = 128
MLA_QK = 96
ROPE_THETA = 10000.0
REL_BUCKETS = 32
REL_MAX_DIST = 128
WINDOW = 128
D_FF = 2816
ADAM_LR, ADAM_B1, ADAM_B2, ADAM_EPS, ADAM_WD, ADAM_STEP = 0.001, 0.9, 0.999, 1e-08, 0.01, 10

VMEM_LIMIT = 56 * 1024 * 1024
STRIP = 32
ROW_STRIP = 64

P_SBQ, P_SBK, P_SBV, P_CQ, P_CKV, P_SLAB, P_SWQ, P_SWK, P_SWV, P_END = 0, 256, 512, 768, 1024, 1152, 1280, 1664, 1792, 1920
SW_PERM = (0, 3, 1, 4, 2, 5)


def _cp(*sem):
    return pltpu.CompilerParams(dimension_semantics=sem, vmem_limit_bytes=VMEM_LIMIT)


def _dot(a, b):
    return jnp.dot(a, b, preferred_element_type=F32)


def _dot_nt(a, b):
    return lax.dot_general(a, b, (((1,), (1,)), ((), ())), preferred_element_type=F32)


def _dot_tn(a, b):
    return lax.dot_general(a, b, (((0,), (0,)), ((), ())), preferred_element_type=F32)


def _split_dot(x, u):
    hi = x.astype(BF16)
    lo = (x - hi.astype(F32)).astype(BF16)
    return _dot(hi, u) + _dot(lo, u)


def _lane_masks():
    lane = lax.broadcasted_iota(jnp.int32, (1, LANES), 1)
    return (lane < HEAD, lane >= HEAD)


def _tile(n, cap, align=128):
    if n <= cap:
        return n
    t = cap - cap % align
    while t >= align:
        if n % t == 0:
            return t
        t -= align
    return n


def matmul(a, b, *, ta=False, tb=False, out_dtype=F32, tm=512, tn=512, tk=8192, name):
    M, K = (a.shape[1], a.shape[0]) if ta else a.shape
    N = b.shape[0] if tb else b.shape[1]
    tm, tn, tk = _tile(M, tm), _tile(N, tn), _tile(K, tk)
    nk = K // tk

    def body(a_ref, b_ref, o_ref, *scratch):
        av = a_ref[...].astype(BF16)
        bv = b_ref[...].astype(BF16)
        if ta:
            part = _dot_tn(av, bv)
        elif tb:
            part = _dot_nt(av, bv)
        else:
            part = _dot(av, bv)
        if nk == 1:
            o_ref[...] = part.astype(out_dtype)
        else:
            acc_ref, = scratch
            k = pl.program_id(2)

            @pl.when(k == 0)
            def _():
                acc_ref[...] = part

            @pl.when(k > 0)
            def _():
                acc_ref[...] += part

            @pl.when(k == nk - 1)
            def _():
                o_ref[...] = acc_ref[...].astype(out_dtype)

    a_spec = pl.BlockSpec((tk, tm), lambda i, j, k: (k, i)) if ta else pl.BlockSpec((tm, tk), lambda i, j, k: (i, k))
    b_spec = pl.BlockSpec((tn, tk), lambda i, j, k: (j, k)) if tb else pl.BlockSpec((tk, tn), lambda i, j, k: (k, j))
    return pl.pallas_call(
        body, name=name, grid=(M // tm, N // tn, nk),
        in_specs=[a_spec, b_spec], out_specs=pl.BlockSpec((tm, tn), lambda i, j, k: (i, j)),
        out_shape=jax.ShapeDtypeStruct((M, N), out_dtype),
        scratch_shapes=[] if nk == 1 else [pltpu.VMEM((tm, tn), F32)],
        compiler_params=_cp("parallel", "parallel", "arbitrary"),
    )(a, b)


def matmul_res(a, b, res, gate, seq, *, tm=512, tn=1024, name):
    M, K = a.shape
    N = b.shape[1]
    tm, tn = _tile(min(M, seq), tm), _tile(N, tn)
    per_seq = seq // tm

    def body(a_ref, b_ref, r_ref, g_ref, y_ref, x_ref):
        y = _dot(a_ref[...].astype(BF16), b_ref[...].astype(BF16))
        y_ref[...] = y
        x_ref[...] = r_ref[...] + g_ref[...] * y

    out = jax.ShapeDtypeStruct((M, N), F32)
    return pl.pallas_call(
        body, name=name, grid=(M // tm, N // tn),
        in_specs=[pl.BlockSpec((tm, K), lambda i, j: (i, 0)), pl.BlockSpec((K, tn), lambda i, j: (0, j)),
                  pl.BlockSpec((tm, tn), lambda i, j: (i, j)), pl.BlockSpec((None, 1, tn), lambda i, j: (lax.div(i, jnp.int32(per_seq)), 0, j))],
        out_specs=[pl.BlockSpec((tm, tn), lambda i, j: (i, j))] * 2,
        out_shape=[out, out], compiler_params=_cp("parallel", "parallel"),
    )(a, b, res, gate)


def rms_fwd(x3, blk, W, g, sc=None, sh=None, *, tm=512, name):
    Bl, S, _ = x3.shape
    tm = min(tm, S)
    mod = sc is not None

    def body(x_ref, g_ref, *rest):
        o_ref = rest[-1]
        x = x_ref[...]
        r = lax.rsqrt(jnp.mean(x * x, axis=-1, keepdims=True) + EPS)
        y = x * r * g_ref[...]
        if mod:
            y = y * (1.0 + rest[0][...]) + rest[1][...]
        o_ref[...] = y.astype(BF16)

    vec = pl.BlockSpec((None, 1, W), lambda b, s: (b, 0, 0))
    return pl.pallas_call(
        body, name=name, grid=(Bl, S // tm),
        in_specs=[pl.BlockSpec((None, tm, W), lambda b, s: (b, s, blk)), pl.BlockSpec((1, W), lambda b, s: (0, 0))] + ([vec, vec] if mod else []),
        out_specs=pl.BlockSpec((None, tm, W), lambda b, s: (b, s, 0)),
        out_shape=jax.ShapeDtypeStruct((Bl, S, W), BF16),
        compiler_params=_cp("parallel", "parallel"),
    )(x3, g, *([sc, sh] if mod else []))


def rms_bwd(x3, blk, W, dy3, g, sc=None, dres3=None, *, tm=256, name):
    Bl, S, _ = x3.shape
    tm = min(tm, S)
    mod = sc is not None
    res = dres3 is not None

    def body(*refs):
        x_ref, dy_ref, g_ref = refs[:3]
        k = 3
        sc_ref = dr_ref = None
        if mod:
            sc_ref = refs[k]
            k += 1
        if res:
            dr_ref = refs[k]
            k += 1
        dx_ref, dg_ref = refs[k], refs[k + 1]
        b, s = pl.program_id(0), pl.program_id(1)
        g = g_ref[...]
        one_sc = 1.0 + sc_ref[...] if mod else None
        rs = max(8, 8 * LANES // W * 8)

        def strip(t, carry):
            dg_acc, dsc_acc, dsh_acc = carry
            rows = pl.ds(pl.multiple_of(t * rs, rs), rs)
            x = x_ref[rows, :]
            dy = dy_ref[rows, :].astype(F32)
            r = lax.rsqrt(jnp.mean(x * x, axis=-1, keepdims=True) + EPS)
            n = x * r
            if mod:
                dsh_acc = dsh_acc + jnp.sum(dy, axis=0, keepdims=True)
                dsc_acc = dsc_acc + jnp.sum(dy * n * g, axis=0, keepdims=True)
                dy = dy * one_sc
            dn = dy * g
            dx = r * (dn - n * jnp.mean(dn * n, axis=-1, keepdims=True))
            if res:
                dx = dx + dr_ref[rows, :]
            dx_ref[rows, :] = dx
            return dg_acc + jnp.sum(dy * n, axis=0, keepdims=True), dsc_acc, dsh_acc

        zero = jnp.zeros((1, W), F32)
        dg_acc, dsc_acc, dsh_acc = lax.fori_loop(0, tm // rs, strip, (zero, zero, zero))
        if mod:
            dsc_ref, dsh_ref = refs[k + 2], refs[k + 3]

            @pl.when(s == 0)
            def _():
                dsc_ref[...] = jnp.zeros_like(dsc_ref)
                dsh_ref[...] = jnp.zeros_like(dsh_ref)

            dsh_ref[...] += dsh_acc
            dsc_ref[...] += dsc_acc

        @pl.when((b == 0) & (s == 0))
        def _():
            dg_ref[...] = jnp.zeros_like(dg_ref)

        dg_ref[...] += dg_acc

    blkspec = pl.BlockSpec((None, tm, W), lambda b, s: (b, s, 0))
    vec = pl.BlockSpec((None, 1, W), lambda b, s: (b, 0, 0))
    row = pl.BlockSpec((1, W), lambda b, s: (0, 0))
    in_specs = [pl.BlockSpec((None, tm, W), lambda b, s: (b, s, blk)), blkspec, row] + ([vec] if mod else []) + ([blkspec] if res else [])
    out_specs = [blkspec, row] + ([vec, vec] if mod else [])
    out_shape = [jax.ShapeDtypeStruct((Bl, S, W), F32), jax.ShapeDtypeStruct((1, W), F32)]
    if mod:
        out_shape += [jax.ShapeDtypeStruct((Bl, 1, W), F32)] * 2
    args = [x3, dy3, g] + ([sc] if mod else []) + ([dres3] if res else [])
    return pl.pallas_call(
        body, name=name, grid=(Bl, S // tm), in_specs=in_specs, out_specs=out_specs, out_shape=out_shape,
        compiler_params=_cp("arbitrary", "arbitrary"),
    )(*args)


def pair_rms_fwd(x3, blk0, npairs, g2, *, tm=1024, name):
    Bl, S, _ = x3.shape
    tm = min(tm, S)

    def body(x_ref, g_ref, o_ref):
        lo, hi = _lane_masks()
        x = x_ref[...]
        xx = x * x
        s0 = jnp.sum(jnp.where(lo, xx, 0.0), axis=-1, keepdims=True)
        s1 = jnp.sum(jnp.where(hi, xx, 0.0), axis=-1, keepdims=True)
        r = jnp.where(lo, lax.rsqrt(s0 / HEAD + EPS), lax.rsqrt(s1 / HEAD + EPS))
        o_ref[...] = (x * r * g_ref[...]).astype(BF16)

    return pl.pallas_call(
        body, name=name, grid=(Bl, S // tm, npairs),
        in_specs=[pl.BlockSpec((None, tm, LANES), lambda b, s, p: (b, s, blk0 + p)), pl.BlockSpec((1, LANES), lambda b, s, p: (0, 0))],
        out_specs=pl.BlockSpec((None, tm, LANES), lambda b, s, p: (b, s, p)),
        out_shape=jax.ShapeDtypeStruct((Bl, S, LANES * npairs), BF16),
        compiler_params=_cp("parallel", "parallel", "parallel"),
    )(x3, g2)


def pair_rms_bwd(x3, blk0, npairs, dy3, g2, *, tm=1024, name):
    Bl, S, _ = x3.shape
    tm = min(tm, S)

    def body(x_ref, dy_ref, g_ref, dx_ref, dg_ref):
        lo, hi = _lane_masks()
        first = (pl.program_id(0) == 0) & (pl.program_id(1) == 0) & (pl.program_id(2) == 0)
        x = x_ref[...]
        dy = dy_ref[...]
        xx = x * x
        s0 = jnp.sum(jnp.where(lo, xx, 0.0), axis=-1, keepdims=True)
        s1 = jnp.sum(jnp.where(hi, xx, 0.0), axis=-1, keepdims=True)
        r = jnp.where(lo, lax.rsqrt(s0 / HEAD + EPS), lax.rsqrt(s1 / HEAD + EPS))
        n = x * r

        @pl.when(first)
        def _():
            dg_ref[...] = jnp.zeros_like(dg_ref)

        part = jnp.sum(dy * n, axis=0, keepdims=True)
        dg_ref[...] += part + pltpu.roll(part, HEAD, 1)
        dn = dy * g_ref[...]
        t = dn * n
        m0 = jnp.sum(jnp.where(lo, t, 0.0), axis=-1, keepdims=True)
        m1 = jnp.sum(jnp.where(hi, t, 0.0), axis=-1, keepdims=True)
        dx_ref[...] = r * (dn - n * (jnp.where(lo, m0, m1) / HEAD))

    return pl.pallas_call(
        body, name=name, grid=(Bl, S // tm, npairs),
        in_specs=[pl.BlockSpec((None, tm, LANES), lambda b, s, p: (b, s, blk0 + p)), pl.BlockSpec((None, tm, LANES), lambda b, s, p: (b, s, p)),
                  pl.BlockSpec((1, LANES), lambda b, s, p: (0, 0))],
        out_specs=[pl.BlockSpec((None, tm, LANES), lambda b, s, p: (b, s, p)), pl.BlockSpec((1, LANES), lambda b, s, p: (0, 0))],
        out_shape=[jax.ShapeDtypeStruct((Bl, S, LANES * npairs), F32), jax.ShapeDtypeStruct((1, LANES), F32)],
        compiler_params=_cp("arbitrary", "arbitrary", "arbitrary"),
    )(x3, dy3, g2)


def _rot(y, cos_t, sin_a, sin_b):
    return y * cos_t + pltpu.roll(y, LANES - 16, 1) * sin_a + pltpu.roll(y, 16, 1) * sin_b


def _rot_t(d, cos_t, sin_a, sin_b):
    return d * cos_t + pltpu.roll(d * sin_a, 16, 1) + pltpu.roll(d * sin_b, LANES - 16, 1)


def rope_norm_fwd(x3, nheads, g, tabs, slab=None, *, tm=1024, name):
    Bl, S, _ = x3.shape
    tm = min(tm, S)
    has_slab = slab is not None

    def body(*refs):
        x_ref, g_ref, c_ref, sa_ref, sb_ref = refs[:5]
        o_ref = refs[-1]
        g = g_ref[...]

        def strip(s, carry):
            rows = pl.ds(pl.multiple_of(s * ROW_STRIP, ROW_STRIP), ROW_STRIP)
            x = x_ref[rows, :]
            if has_slab:
                x = x + refs[5][rows, :]
            r = lax.rsqrt(jnp.sum(x * x, axis=-1, keepdims=True) / MLA_QK + EPS)
            o_ref[rows, :] = _rot(x * r * g, c_ref[rows, :], sa_ref[rows, :], sb_ref[rows, :]).astype(BF16)
            return carry

        lax.fori_loop(0, tm // ROW_STRIP, strip, 0)

    head = pl.BlockSpec((None, tm, LANES), lambda b, s, h: (b, s, h))
    tab = pl.BlockSpec((None, tm, LANES), lambda b, s, h: (b, s, 0))
    in_specs = [head, pl.BlockSpec((1, LANES), lambda b, s, h: (0, 0)), tab, tab, tab]
    args = [x3, g, *tabs]
    if has_slab:
        sblk = slab[1]
        in_specs.append(pl.BlockSpec((None, tm, LANES), lambda b, s, h: (b, s, sblk)))
        args.append(slab[0])
    return pl.pallas_call(
        body, name=name, grid=(Bl, S // tm, nheads), in_specs=in_specs, out_specs=head,
        out_shape=jax.ShapeDtypeStruct((Bl, S, LANES * nheads), BF16),
        compiler_params=_cp("parallel", "parallel", "parallel"),
    )(*args)


def rope_norm_bwd(x3, nheads, dy3, g, tabs, slab=None, *, tm=1024, name):
    Bl, S, _ = x3.shape
    tm = min(tm, S)
    has_slab = slab is not None

    def body(*refs):
        x_ref, dy_ref, g_ref, c_ref, sa_ref, sb_ref = refs[:6]
        k = 7 if has_slab else 6
        dx_ref, dg_ref = refs[k], refs[k + 1]
        h = pl.program_id(2)
        first = (pl.program_id(0) == 0) & (pl.program_id(1) == 0) & (h == 0)
        g = g_ref[...]
        ds_ref = refs[k + 2] if has_slab else None

        @pl.when(first)
        def _():
            dg_ref[...] = jnp.zeros_like(dg_ref)

        if has_slab:
            @pl.when(h == 0)
            def _():
                ds_ref[...] = jnp.zeros_like(ds_ref)

        def strip(s, dg_acc):
            rows = pl.ds(pl.multiple_of(s * ROW_STRIP, ROW_STRIP), ROW_STRIP)
            x = x_ref[rows, :]
            if has_slab:
                x = x + refs[6][rows, :]
            r = lax.rsqrt(jnp.sum(x * x, axis=-1, keepdims=True) / MLA_QK + EPS)
            n = x * r
            d = _rot_t(dy_ref[rows, :], c_ref[rows, :], sa_ref[rows, :], sb_ref[rows, :])
            dn = d * g
            dx = r * (dn - n * (jnp.sum(dn * n, axis=-1, keepdims=True) / MLA_QK))
            dx_ref[rows, :] = dx
            if has_slab:
                ds_ref[rows, :] += dx
            return dg_acc + jnp.sum(d * n, axis=0, keepdims=True)

        dg_ref[...] += lax.fori_loop(0, tm // ROW_STRIP, strip, jnp.zeros((1, LANES), F32))

    head = pl.BlockSpec((None, tm, LANES), lambda b, s, h: (b, s, h))
    tab = pl.BlockSpec((None, tm, LANES), lambda b, s, h: (b, s, 0))
    row = pl.BlockSpec((1, LANES), lambda b, s, h: (0, 0))
    in_specs = [head, head, row, tab, tab, tab]
    args = [x3, dy3, g, *tabs]
    out_specs = [head, row]
    out_shape = [jax.ShapeDtypeStruct((Bl, S, LANES * nheads), F32), jax.ShapeDtypeStruct((1, LANES), F32)]
    if has_slab:
        sblk = slab[1]
        in_specs.append(pl.BlockSpec((None, tm, LANES), lambda b, s, h: (b, s, sblk)))
        args.append(slab[0])
        out_specs.append(tab)
        out_shape.append(jax.ShapeDtypeStruct((Bl, S, LANES), F32))
    return pl.pallas_call(
        body, name=name, grid=(Bl, S // tm, nheads), in_specs=in_specs, out_specs=out_specs, out_shape=out_shape,
        compiler_params=_cp("arbitrary", "arbitrary", "arbitrary"),
    )(*args)


def _sb_tile(z, strict, u, carry_r):
    sp = jnp.maximum(z, 0.0) + jnp.log(1.0 + jnp.exp(-jnp.abs(z)))
    keep = jnp.where(strict, -sp, 0.0)
    logw = (z - sp) + _split_dot(keep, u) + carry_r
    return jnp.where(strict, jnp.exp(logw), 0.0), keep, sp


SB_BLOCK = 256
SB_QBLOCK = 512


def sb_attn_fwd(proj3, *, plans=None, name):
    Bl, S, _ = proj3.shape
    tk = min(SB_BLOCK, S)
    tq = min(SB_QBLOCK, S)
    per_q = tq // tk
    scale = HEAD ** -0.5
    qb, kb0, vb0 = P_SBQ // LANES, P_SBK // LANES, P_SBV // LANES

    def body(q_ref, k_ref, v_ref, o_ref, rt_ref):
        i = pl.program_id(2)
        masks = _lane_masks()
        lane = lax.broadcasted_iota(jnp.int32, (1, LANES), 1)
        q = q_ref[...]
        qh = [jnp.where(m, q, 0.0).astype(BF16) for m in masks]
        rr = lax.broadcasted_iota(jnp.int32, (tq, tk), 0)
        cc = lax.broadcasted_iota(jnp.int32, (tq, tk), 1)
        u = (lax.broadcasted_iota(jnp.int32, (tk, tk), 0) > lax.broadcasted_iota(jnp.int32, (tk, tk), 1)).astype(BF16)

        rt_ref[...] = jnp.zeros_like(rt_ref)

        def step(t, carry):
            r0, r1, acc = carry
            j = (i + 1) * per_q - 1 - t
            off = pl.multiple_of(j * tk, tk)
            kb = k_ref[pl.ds(off, tk), :].astype(BF16)
            vb = v_ref[pl.ds(off, tk), :]
            strict = (cc + j * tk) < (rr + i * tq)
            rt_ref[...] = jnp.where(lane == j, r0, jnp.where(lane == j + HEAD, r1, rt_ref[...]))
            rs = [r0, r1]
            for h in range(2):
                z = _dot_nt(qh[h], kb) * scale
                w, keep, _ = _sb_tile(z, strict, u, rs[h])
                acc = acc + _dot(w.astype(BF16), jnp.where(masks[h], vb, 0.0).astype(BF16))
                rs[h] = rs[h] + jnp.sum(keep, axis=1, keepdims=True)
            return rs[0], rs[1], acc

        zero = jnp.zeros((tq, 1), F32)
        _, _, acc = lax.fori_loop(0, (i + 1) * per_q, step, (zero, zero, jnp.zeros((tq, LANES), F32)))
        o_ref[...] = acc

    seq = lambda blk0: pl.BlockSpec((None, S, LANES), lambda b, p, i: (b, 0, blk0 + p))
    out = pl.BlockSpec((None, tq, LANES), lambda b, p, i: (b, i, p))
    shp = jax.ShapeDtypeStruct((Bl, S, 2 * LANES), F32)
    return call_with_plans(
        body, plans, name=name, grid=(Bl, 2, S // tq),
        in_specs=[pl.BlockSpec((None, tq, LANES), lambda b, p, i: (b, i, qb + p)), seq(kb0), seq(vb0)],
        out_specs=[out, out], out_shape=[shp, shp], scratch_shapes=[], args=[proj3, proj3, proj3],
        sem=("arbitrary",) * 3 if plans else ("parallel", "parallel", "arbitrary"))


def sb_attn_bwd(proj3, rt3, do3, *, plans=None, name):
    Bl, S, _ = proj3.shape
    tk = min(SB_BLOCK, S)
    tq = min(SB_QBLOCK, S)
    per_q = tq // tk
    scale = HEAD ** -0.5
    qb, kb0, vb0 = P_SBQ // LANES, P_SBK // LANES, P_SBV // LANES

    def body(q_ref, k_ref, v_ref, rt_ref, do_ref, dq_ref, dk_ref, dv_ref):
        i = pl.program_id(2)

        @pl.when(i == 0)
        def _():
            dk_ref[...] = jnp.zeros_like(dk_ref)
            dv_ref[...] = jnp.zeros_like(dv_ref)

        masks = _lane_masks()
        lane = lax.broadcasted_iota(jnp.int32, (1, LANES), 1)
        q = q_ref[...]
        qh = [jnp.where(m, q, 0.0).astype(BF16) for m in masks]
        do_b = do_ref[...].astype(BF16)
        doh = [jnp.where(m, do_b, jnp.zeros_like(do_b)) for m in masks]
        rt = rt_ref[...]
        rr = lax.broadcasted_iota(jnp.int32, (tq, tk), 0)
        cc = lax.broadcasted_iota(jnp.int32, (tq, tk), 1)
        ur = lax.broadcasted_iota(jnp.int32, (tk, tk), 0)
        uc = lax.broadcasted_iota(jnp.int32, (tk, tk), 1)
        u_suffix = (ur > uc).astype(BF16)
        u_prefix = (ur < uc).astype(BF16)

        def step(j, carry):
            p0, p1, dq = carry
            off = pl.multiple_of(j * tk, tk)
            kf = k_ref[pl.ds(off, tk), :]
            kb = kf.astype(BF16)
            vb = v_ref[pl.ds(off, tk), :]
            strict = (cc + j * tk) < (rr + i * tq)
            ps = [p0, p1]
            dk_acc = jnp.zeros((tk, LANES), F32)
            dv_acc = jnp.zeros((tk, LANES), F32)
            for h in range(2):
                r_j = jnp.sum(jnp.where(lane == j + h * HEAD, rt, 0.0), axis=1, keepdims=True)
                z = _dot_nt(qh[h], kb) * scale
                w, _, sp = _sb_tile(z, strict, u_suffix, r_j)
                vh = jnp.where(masks[h], vb, 0.0).astype(BF16)
                g = _dot_nt(doh[h], vh) * w
                pre = _split_dot(g, u_prefix) + ps[h]
                dz = jnp.where(strict, g * jnp.exp(-sp) - jnp.exp(z - sp) * pre, 0.0) * scale
                dzb = dz.astype(BF16)
                dq = dq + _dot(dzb, jnp.where(masks[h], kf, 0.0).astype(BF16))
                dk_acc = dk_acc + _dot_tn(dzb, qh[h])
                dv_acc = dv_acc + _dot_tn(w.astype(BF16), doh[h])
                ps[h] = ps[h] + jnp.sum(g, axis=1, keepdims=True)
            dk_ref[pl.ds(off, tk), :] += dk_acc
            dv_ref[pl.ds(off, tk), :] += dv_acc
            return ps[0], ps[1], dq

        zero = jnp.zeros((tq, 1), F32)
        out = lax.fori_loop(0, (i + 1) * per_q, step, (zero, zero, jnp.zeros((tq, LANES), F32)))
        dq_ref[...] = out[2]

    seq_in = lambda blk0: pl.BlockSpec((None, S, LANES), lambda b, p, i: (b, 0, blk0 + p))
    blk = pl.BlockSpec((None, tq, LANES), lambda b, p, i: (b, i, p))
    seq_out = pl.BlockSpec((None, S, LANES), lambda b, p, i: (b, 0, p))
    shp = jax.ShapeDtypeStruct((Bl, S, 2 * LANES), F32)
    return call_with_plans(
        body, plans, name=name, grid=(Bl, 2, S // tq),
        in_specs=[pl.BlockSpec((None, tq, LANES), lambda b, p, i: (b, i, qb + p)), seq_in(kb0), seq_in(vb0), blk, blk],
        out_specs=[blk, seq_out, seq_out], out_shape=[shp, shp, shp], scratch_shapes=[], args=[proj3, proj3, proj3, rt3, do3],
        sem=("arbitrary",) * 3 if plans else ("parallel", "parallel", "arbitrary"))


def mla_attn_fwd(q3, k3, kv3, vblk0, *, tq=512, tk=256, plans=None, name):
    Bl, S, _ = q3.shape
    tq = min(tq, S)
    tk = min(tk, tq)
    per_q = tq // tk
    scale = MLA_QK ** -0.5

    def body(q_ref, k_ref, v_ref, o_ref, lse_ref):
        i = pl.program_id(2)
        masks = _lane_masks()
        rr = lax.broadcasted_iota(jnp.int32, (tq, tk), 0)
        cc = lax.broadcasted_iota(jnp.int32, (tq, tk), 1)
        qh = [q_ref[:, h * LANES:(h + 1) * LANES] for h in range(2)]

        def step(j, carry):
            m0, l0, m1, l1, acc = carry
            off = pl.multiple_of(j * tk, tk)
            vb = v_ref[pl.ds(off, tk), :]
            causal = (cc + j * tk) <= (rr + i * tq)
            ms, ls, alphas = [m0, m1], [l0, l1], []
            add = jnp.zeros((tq, LANES), F32)
            for h in range(2):
                kh = k_ref[pl.ds(off, tk), h * LANES:(h + 1) * LANES]
                s = jnp.where(causal, _dot_nt(qh[h], kh) * scale, NEG)
                m_new = jnp.maximum(ms[h], jnp.max(s, axis=1, keepdims=True))
                p = jnp.exp(s - m_new)
                alpha = jnp.exp(ms[h] - m_new)
                ls[h] = alpha * ls[h] + jnp.sum(p, axis=1, keepdims=True)
                ms[h] = m_new
                alphas.append(alpha)
                add = add + _dot(p.astype(BF16), jnp.where(masks[h], vb, 0.0).astype(BF16))
            acc = acc * jnp.where(masks[0], alphas[0], alphas[1]) + add
            return ms[0], ls[0], ms[1], ls[1], acc

        neg = jnp.full((tq, 1), NEG, F32)
        zero = jnp.zeros((tq, 1), F32)
        m0, l0, m1, l1, acc = lax.fori_loop(0, (i + 1) * per_q, step, (neg, zero, neg, zero, jnp.zeros((tq, LANES), F32)))
        o_ref[...] = acc / jnp.where(masks[0], l0, l1)
        lse_ref[...] = jnp.where(masks[0], m0 + jnp.log(l0), m1 + jnp.log(l1))

    out = pl.BlockSpec((None, tq, LANES), lambda b, p, i: (b, i, p))
    shp = jax.ShapeDtypeStruct((Bl, S, 3 * LANES), F32)
    return call_with_plans(
        body, plans, name=name, grid=(Bl, 3, S // tq),
        in_specs=[pl.BlockSpec((None, tq, 2 * LANES), lambda b, p, i: (b, i, p)), pl.BlockSpec((None, S, 2 * LANES), lambda b, p, i: (b, 0, p)),
                  pl.BlockSpec((None, S, LANES), lambda b, p, i: (b, 0, vblk0 + p))],
        out_specs=[out, out], out_shape=[shp, shp], scratch_shapes=[], args=[q3, k3, kv3],
        sem=("arbitrary",) * 3 if plans else ("parallel", "parallel", "arbitrary"))


def mla_attn_bwd(q3, k3, kv3, vblk0, o3, lse3, do3, *, tq=512, tk=256, name):
    Bl, S, _ = q3.shape
    tq = min(tq, S)
    tk = min(tk, tq)
    per_q = tq // tk
    nq = S // tq
    scale = MLA_QK ** -0.5

    def body(q_ref, k_ref, v_ref, o_ref, lse_ref, do_ref, dq_ref, dk_ref, dv_ref, s_scr, dp_scr, p_scr, ds_scr):
        j = pl.program_id(2)

        @pl.when(j == 0)
        def _():
            dq_ref[...] = jnp.zeros_like(dq_ref)

        masks = _lane_masks()
        vb = v_ref[...]
        vh = [jnp.where(m, vb, 0.0).astype(BF16) for m in masks]
        kh = [k_ref[:, h * LANES:(h + 1) * LANES] for h in range(2)]
        i0 = lax.div(j, jnp.int32(per_q))

        def step(i, carry, masked):
            dk0, dk1, dv = carry
            off = pl.multiple_of(i * tq, tq)
            do_b = do_ref[pl.ds(off, tq), :].astype(BF16)
            prod = do_b.astype(F32) * o_ref[pl.ds(off, tq), :]
            lse = lse_ref[pl.ds(off, tq), :]
            dks = [dk0, dk1]
            for h in range(2):
                qh = q_ref[pl.ds(off, tq), h * LANES:(h + 1) * LANES]
                doh = jnp.where(masks[h], do_b, jnp.zeros_like(do_b))
                delta = jnp.sum(jnp.where(masks[h], prod, 0.0), axis=1, keepdims=True)
                lse_h = lse[:, h * HEAD:h * HEAD + 1]
                s_scr[...] = _dot_nt(qh, kh[h])
                dp_scr[...] = _dot_nt(doh, vh[h])
                for r0 in range(0, tq, STRIP):
                    rows = slice(r0, r0 + STRIP)
                    s = s_scr[rows, :] * scale
                    if masked:
                        rr = lax.broadcasted_iota(jnp.int32, (STRIP, tk), 0) + (i * tq + r0)
                        cc = lax.broadcasted_iota(jnp.int32, (STRIP, tk), 1) + j * tk
                        s = jnp.where(cc <= rr, s, NEG)
                    p = jnp.exp(s - lse_h[rows])
                    p_scr[rows, :] = p.astype(BF16)
                    ds_scr[rows, :] = (p * (dp_scr[rows, :] - delta[rows])).astype(BF16)
                ds = ds_scr[...]
                dq_ref[pl.ds(off, tq), h * LANES:(h + 1) * LANES] += _dot(ds, kh[h]) * scale
                dks[h] = dks[h] + _dot_tn(ds, qh)
                dv = dv + _dot_tn(p_scr[...], doh)
            return dks[0], dks[1], dv

        zero = jnp.zeros((tk, LANES), F32)
        carry = step(i0, (zero, zero, zero), True)
        dk0, dk1, dv = lax.fori_loop(i0 + 1, nq, lambda i, c: step(i, c, False), carry)
        dk_ref[:, 0:LANES] = dk0 * scale
        dk_ref[:, LANES:2 * LANES] = dk1 * scale
        dv_ref[...] = dv

    seq1 = pl.BlockSpec((None, S, LANES), lambda b, p, j: (b, 0, p))
    seq2 = pl.BlockSpec((None, S, 2 * LANES), lambda b, p, j: (b, 0, p))
    return pl.pallas_call(
        body, name=name, grid=(Bl, 3, S // tk),
        in_specs=[seq2, pl.BlockSpec((None, tk, 2 * LANES), lambda b, p, j: (b, j, p)),
                  pl.BlockSpec((None, tk, LANES), lambda b, p, j: (b, j, vblk0 + p)), seq1, seq1, seq1],
        out_specs=[seq2, pl.BlockSpec((None, tk, 2 * LANES), lambda b, p, j: (b, j, p)), pl.BlockSpec((None, tk, LANES), lambda b, p, j: (b, j, p))],
        out_shape=[jax.ShapeDtypeStruct((Bl, S, 6 * LANES), F32), jax.ShapeDtypeStruct((Bl, S, 6 * LANES), F32), jax.ShapeDtypeStruct((Bl, S, 3 * LANES), F32)],
        scratch_shapes=[pltpu.VMEM((tq, tk), F32), pltpu.VMEM((tq, tk), F32), pltpu.VMEM((tq, tk), BF16), pltpu.VMEM((tq, tk), BF16)],
        compiler_params=_cp("parallel", "parallel", "arbitrary"),
    )(q3, k3, kv3, o3, lse3, do3)


def _bucket_table():
    a = jnp.arange(WINDOW)[:, None]
    b = jnp.arange(2 * WINDOW)[None, :]
    dist = WINDOW + a - b
    max_exact = REL_BUCKETS // 2
    n = jnp.maximum(dist, 0)
    nf = jnp.maximum(n, 1).astype(F32)
    large = max_exact + (jnp.log(nf / max_exact) / math.log(REL_MAX_DIST / max_exact) * (REL_BUCKETS - max_exact)).astype(jnp.int32)
    large = jnp.minimum(large, REL_BUCKETS - 1)
    bucket = jnp.where(n < max_exact, n, large)
    return jnp.where((dist >= 0) & (dist < WINDOW), bucket, -1).astype(jnp.int32)


def swa_bias(rel_flat, bucket, *, name):
    def body(t_ref, b_ref, o_ref):
        bk = b_ref[...]
        for p in range(3):
            for hh in range(2):
                h = hh * 3 + p
                acc = jnp.full(bk.shape, NEG, F32)
                for b in range(REL_BUCKETS):
                    acc = jnp.where(bk == b, t_ref[b * 6 + h], acc)
                o_ref[p, hh] = acc

    return pl.pallas_call(
        body, name=name,
        in_specs=[pl.BlockSpec(memory_space=pltpu.SMEM), pl.BlockSpec(memory_space=pltpu.VMEM)],
        out_specs=pl.BlockSpec(memory_space=pltpu.VMEM),
        out_shape=jax.ShapeDtypeStruct((3, 2, WINDOW, 2 * WINDOW), F32),
    )(rel_flat, bucket)


def swa_bias_bwd(dbias, bucket, *, name):
    Bl = dbias.shape[0]

    def body(d_ref, b_ref, o_ref):
        bk = b_ref[...]
        lane = lax.broadcasted_iota(jnp.int32, (1, LANES), 1)
        rows = []
        for h in range(6):
            hh, p = divmod(h, 3)
            d = d_ref[0, p, hh]
            for bl in range(1, Bl):
                d = d + d_ref[bl, p, hh]
            row = jnp.zeros((1, LANES), F32)
            for b in range(REL_BUCKETS):
                s = jnp.sum(jnp.sum(jnp.where(bk == b, d, 0.0), axis=1, keepdims=True), axis=0, keepdims=True)
                row = row + jnp.where(lane == b, s, 0.0)
            rows.append(row)
        rows += [jnp.zeros((1, LANES), F32)] * 2
        o_ref[...] = jnp.concatenate(rows, axis=0)

    return pl.pallas_call(
        body, name=name,
        in_specs=[pl.BlockSpec(memory_space=pltpu.VMEM)] * 2, out_specs=pl.BlockSpec(memory_space=pltpu.VMEM),
        out_shape=jax.ShapeDtypeStruct((8, LANES), F32),
    )(dbias, bucket)


SWA_QBLOCKS = 4


def _swa_specs(vblk, nqb):
    rows = nqb * WINDOW
    cur = lambda blk: pl.BlockSpec((None, rows, LANES), lambda b, p, n: (b, n, blk))
    prev = lambda blk: pl.BlockSpec((None, WINDOW, LANES), lambda b, p, n: (b, jnp.maximum(n * nqb - 1, 0), blk))
    return [pl.BlockSpec((None, rows, LANES), lambda b, p, n: (b, n, p)), cur(0), prev(0), cur(vblk), prev(vblk),
            pl.BlockSpec((None, 2, WINDOW, 2 * WINDOW), lambda b, p, n: (p, 0, 0, 0)), pl.BlockSpec((None, 2, LANES), lambda b, p, n: (p, 0, 0))]


def _rows128(ref, m):
    return ref[m * WINDOW:(m + 1) * WINDOW, :]


def _swa_logits(qh, kp, kc, bias_h, first, scale):
    sp = jnp.where(first, NEG, _dot_nt(qh, kp) * scale + bias_h[:, :WINDOW])
    sc = _dot_nt(qh, kc) * scale + bias_h[:, WINDOW:]
    return sp, sc


def swa_attn_fwd(qn3, kn3, proj3, bias, sinks, *, plans=None, name):
    Bl, S, _ = qn3.shape
    scale = HEAD ** -0.5
    nqb = min(SWA_QBLOCKS, S // WINDOW)

    def body(q_ref, kc_ref, kp_ref, vc_ref, vp_ref, b_ref, s_ref, o_ref, lse_ref):
        seq_start = pl.program_id(2) == 0
        masks = _lane_masks()
        for m_ in range(nqb):
            first = seq_start if m_ == 0 else False
            q = _rows128(q_ref, m_)
            kp = kp_ref[...] if m_ == 0 else _rows128(kc_ref, m_ - 1)
            vp = vp_ref[...] if m_ == 0 else _rows128(vc_ref, m_ - 1)
            kc, vc = _rows128(kc_ref, m_), _rows128(vc_ref, m_)
            o = jnp.zeros((WINDOW, LANES), F32)
            lses = []
            for h in range(2):
                qh = jnp.where(masks[h], q, jnp.zeros_like(q))
                sp, sc = _swa_logits(qh, kp, kc, b_ref[h], first, scale)
                sink = s_ref[h:h + 1, 0:1]
                m = jnp.maximum(jnp.maximum(jnp.max(sp, axis=1, keepdims=True), jnp.max(sc, axis=1, keepdims=True)), sink)
                ep, ec = jnp.exp(sp - m), jnp.exp(sc - m)
                l = jnp.sum(ep, axis=1, keepdims=True) + jnp.sum(ec, axis=1, keepdims=True) + jnp.exp(sink - m)
                inv = 1.0 / l
                o = o + _dot((ep * inv).astype(BF16), jnp.where(masks[h], vp, 0.0).astype(BF16))
                o = o + _dot((ec * inv).astype(BF16), jnp.where(masks[h], vc, 0.0).astype(BF16))
                lses.append(m + jnp.log(l))
            o_ref[m_ * WINDOW:(m_ + 1) * WINDOW, :] = o
            lse_ref[m_ * WINDOW:(m_ + 1) * WINDOW, :] = jnp.where(masks[0], lses[0], lses[1])

    out = pl.BlockSpec((None, nqb * WINDOW, LANES), lambda b, p, n: (b, n, p))
    shp = jax.ShapeDtypeStruct((Bl, S, 3 * LANES), F32)
    return call_with_plans(
        body, plans, name=name, grid=(Bl, 3, S // (nqb * WINDOW)), in_specs=_swa_specs(P_SWV // LANES, nqb),
        out_specs=[out, out], out_shape=[shp, shp], scratch_shapes=[], args=[qn3, kn3, kn3, proj3, proj3, bias, sinks],
        sem=("arbitrary",) * 3 if plans else ("parallel", "parallel", "arbitrary"))


def swa_attn_bwd(qn3, kn3, proj3, bias, sinks, o3, lse3, do3, *, name):
    Bl, S, _ = qn3.shape
    scale = HEAD ** -0.5
    nqb = min(SWA_QBLOCKS, S // WINDOW)
    rows = nqb * WINDOW

    def body(q_ref, kc_ref, kp_ref, vc_ref, vp_ref, b_ref, s_ref, o_ref, lse_ref, do_ref,
             dq_ref, dk_ref, dv_ref, db_ref, dsk_ref):
        p_id, n = pl.program_id(1), pl.program_id(2)
        seq_start = n == 0

        @pl.when((p_id == 0) & seq_start)
        def _():
            dk_ref[...] = jnp.zeros_like(dk_ref)
            dv_ref[...] = jnp.zeros_like(dv_ref)

        @pl.when(seq_start)
        def _():
            db_ref[...] = jnp.zeros_like(db_ref)
            dsk_ref[...] = jnp.zeros_like(dsk_ref)

        masks = _lane_masks()
        zero = jnp.zeros((WINDOW, LANES), F32)
        dk_acc = [zero] * (nqb + 1)
        dv_acc = [zero] * (nqb + 1)
        db_acc = [[jnp.zeros((WINDOW, WINDOW), F32)] * 2 for _ in range(2)]
        dsk_acc = [jnp.zeros((1, 1), F32)] * 2
        for m_ in range(nqb):
            first = seq_start if m_ == 0 else False
            q = _rows128(q_ref, m_)
            kp = kp_ref[...] if m_ == 0 else _rows128(kc_ref, m_ - 1)
            vp = vp_ref[...] if m_ == 0 else _rows128(vc_ref, m_ - 1)
            kc, vc = _rows128(kc_ref, m_), _rows128(vc_ref, m_)
            do_b = _rows128(do_ref, m_).astype(BF16)
            prod = do_b.astype(F32) * _rows128(o_ref, m_)
            lse = _rows128(lse_ref, m_)
            dq = zero
            for h in range(2):
                qh = jnp.where(masks[h], q, jnp.zeros_like(q))
                doh = jnp.where(masks[h], do_b, jnp.zeros_like(do_b))
                sp, sc = _swa_logits(qh, kp, kc, b_ref[h], first, scale)
                lse_h = lse[:, h * HEAD:h * HEAD + 1]
                pp, pc = jnp.exp(sp - lse_h), jnp.exp(sc - lse_h)
                delta = jnp.sum(jnp.where(masks[h], prod, 0.0), axis=1, keepdims=True)
                dsp = pp * (_dot_nt(doh, jnp.where(masks[h], vp, 0.0).astype(BF16)) - delta)
                dsc = pc * (_dot_nt(doh, jnp.where(masks[h], vc, 0.0).astype(BF16)) - delta)
                db_acc[h] = [db_acc[h][0] + dsp, db_acc[h][1] + dsc]
                psink = jnp.exp(s_ref[h:h + 1, 0:1] - lse_h)
                dsk_acc[h] = dsk_acc[h] - jnp.sum(psink * delta, axis=0, keepdims=True)
                dspb, dscb = (dsp * scale).astype(BF16), (dsc * scale).astype(BF16)
                dq = dq + _dot(dspb, jnp.where(masks[h], kp, jnp.zeros_like(kp))) + _dot(dscb, jnp.where(masks[h], kc, jnp.zeros_like(kc)))
                dk_acc[m_] = dk_acc[m_] + _dot_tn(dspb, qh)
                dk_acc[m_ + 1] = dk_acc[m_ + 1] + _dot_tn(dscb, qh)
                dv_acc[m_] = dv_acc[m_] + _dot_tn(pp.astype(BF16), doh)
                dv_acc[m_ + 1] = dv_acc[m_ + 1] + _dot_tn(pc.astype(BF16), doh)
            dq_ref[m_ * WINDOW:(m_ + 1) * WINDOW, :] = dq
        for h in range(2):
            db_ref[h, :, 0:WINDOW] += db_acc[h][0]
            db_ref[h, :, WINDOW:2 * WINDOW] += db_acc[h][1]
            dsk_ref[h:h + 1, :] += jnp.broadcast_to(dsk_acc[h], (1, LANES))
        offp = pl.multiple_of(jnp.maximum(n * nqb - 1, 0) * WINDOW, WINDOW)
        dk_ref[pl.ds(offp, WINDOW), :] += dk_acc[0]
        dv_ref[pl.ds(offp, WINDOW), :] += dv_acc[0]
        for m_ in range(nqb):
            off = pl.multiple_of(n * rows + m_ * WINDOW, WINDOW)
            dk_ref[pl.ds(off, WINDOW), :] += dk_acc[m_ + 1]
            dv_ref[pl.ds(off, WINDOW), :] += dv_acc[m_ + 1]

    blk = pl.BlockSpec((None, rows, LANES), lambda b, p, n: (b, n, p))
    seq = pl.BlockSpec((None, S, LANES), lambda b, p, n: (b, 0, 0))
    return pl.pallas_call(
        body, name=name, grid=(Bl, 3, S // rows), in_specs=_swa_specs(P_SWV // LANES, nqb) + [blk, blk, blk],
        out_specs=[blk, seq, seq, pl.BlockSpec((None, None, 2, WINDOW, 2 * WINDOW), lambda b, p, n: (b, p, 0, 0, 0)),
                   pl.BlockSpec((None, None, 2, LANES), lambda b, p, n: (b, p, 0, 0))],
        out_shape=[jax.ShapeDtypeStruct((Bl, S, 3 * LANES), F32), jax.ShapeDtypeStruct((Bl, S, LANES), F32), jax.ShapeDtypeStruct((Bl, S, LANES), F32),
                   jax.ShapeDtypeStruct((Bl, 3, 2, WINDOW, 2 * WINDOW), F32), jax.ShapeDtypeStruct((Bl, 3, 2, LANES), F32)],
        compiler_params=_cp("arbitrary", "arbitrary", "arbitrary"),
    )(qn3, kn3, kn3, proj3, proj3, bias, sinks, o3, lse3, do3)


CONV_ROWS = 64
CONV_LANES = 128


def _conv_strip(x_ref, h_ref, w, b, r0, cols, first_blk):
    x = x_ref[r0:r0 + CONV_ROWS, cols]
    if r0 == 0:
        rows = lax.broadcasted_iota(jnp.int32, x.shape, 0)
        h6 = jnp.where(first_blk, 0.0, h_ref[6:7, cols])
        h7 = jnp.where(first_blk, 0.0, h_ref[7:8, cols])
        x1 = jnp.where(rows == 0, h7, pltpu.roll(x, 1, 0))
        x2 = jnp.where(rows == 0, h6, jnp.where(rows == 1, h7, pltpu.roll(x, 2, 0)))
    else:
        x1 = x_ref[r0 - 1:r0 - 1 + CONV_ROWS, cols]
        x2 = x_ref[r0 - 2:r0 - 2 + CONV_ROWS, cols]
    return w[0:1] * x2 + w[1:2] * x1 + w[2:3] * x + b, x, x1, x2


FF_BLK = D_FF // 2


def _up_perm(a):
    q = FF_BLK
    return _cat([a[..., 0:q], a[..., 2 * q:3 * q], a[..., q:2 * q], a[..., 3 * q:4 * q]])


def conv_gate_fwd(up3, cw, cb, *, tm=256, name):
    Bl, S, _ = up3.shape
    tm = min(tm, S)
    W = 2 * FF_BLK

    def body(x_ref, h_ref, w_ref, b_ref, o_ref):
        first = pl.program_id(1) == 0

        def chunk(c, carry):
            cg = pl.ds(pl.multiple_of(c * CONV_LANES, CONV_LANES), CONV_LANES)
            cv = pl.ds(pl.multiple_of(FF_BLK + c * CONV_LANES, CONV_LANES), CONV_LANES)
            wg, wv, bg, bv = w_ref[:, cg], w_ref[:, cv], b_ref[:, cg], b_ref[:, cv]
            for r0 in range(0, tm, CONV_ROWS):
                ug = _conv_strip(x_ref, h_ref, wg, bg, r0, cg, first)[0]
                uv = _conv_strip(x_ref, h_ref, wv, bv, r0, cv, first)[0]
                o_ref[r0:r0 + CONV_ROWS, cg] = (ug * jax.nn.sigmoid(ug) * uv).astype(BF16)
            return carry

        lax.fori_loop(0, FF_BLK // CONV_LANES, chunk, 0)

    hb = tm // 8
    return pl.pallas_call(
        body, name=name, grid=(Bl, S // tm, 2),
        in_specs=[pl.BlockSpec((None, tm, W), lambda b, s, c: (b, s, c)),
                  pl.BlockSpec((None, 8, W), lambda b, s, c: (b, jnp.maximum(s * hb - 1, 0), c)),
                  pl.BlockSpec((3, W), lambda b, s, c: (0, c)), pl.BlockSpec((1, W), lambda b, s, c: (0, c))],
        out_specs=pl.BlockSpec((None, tm, FF_BLK), lambda b, s, c: (b, s, c)),
        out_shape=jax.ShapeDtypeStruct((Bl, S, D_FF), BF16),
        compiler_params=_cp("parallel", "parallel", "parallel"),
    )(up3, up3, cw, cb)


def conv_gate_bwd(up3, cw, cb, da3, *, tm=256, name):
    Bl, S, _ = up3.shape
    tm = min(tm, S)
    ns = S // tm
    W = 2 * FF_BLK

    def body(x_ref, h_ref, w_ref, b_ref, da_ref, dup_ref, dw_ref, nxt_ref, du_scr):
        b, s = pl.program_id(1), pl.program_id(2)
        seq_end = s == 0
        first = s == ns - 1

        @pl.when((b == 0) & seq_end)
        def _():
            dw_ref[...] = jnp.zeros_like(dw_ref)

        def du_chunk(c, carry):
            cg = pl.ds(pl.multiple_of(c * CONV_LANES, CONV_LANES), CONV_LANES)
            cv = pl.ds(pl.multiple_of(FF_BLK + c * CONV_LANES, CONV_LANES), CONV_LANES)
            wg, wv, bg, bv = w_ref[:, cg], w_ref[:, cv], b_ref[:, cg], b_ref[:, cv]
            acc_g = [jnp.zeros((1, CONV_LANES), F32)] * 4
            acc_v = [jnp.zeros((1, CONV_LANES), F32)] * 4
            for r0 in range(0, tm, CONV_ROWS):
                ug, xg, xg1, xg2 = _conv_strip(x_ref, h_ref, wg, bg, r0, cg, first)
                uv, xv, xv1, xv2 = _conv_strip(x_ref, h_ref, wv, bv, r0, cv, first)
                da = da_ref[r0:r0 + CONV_ROWS, cg].astype(F32)
                sg = jax.nn.sigmoid(ug)
                dug = da * uv * sg * (1.0 + ug * (1.0 - sg))
                duv = da * ug * sg
                du_scr[r0:r0 + CONV_ROWS, cg] = dug
                du_scr[r0:r0 + CONV_ROWS, cv] = duv
                col = lambda t: jnp.sum(t, axis=0, keepdims=True)
                acc_g = [acc_g[0] + col(dug * xg2), acc_g[1] + col(dug * xg1), acc_g[2] + col(dug * xg), acc_g[3] + col(dug)]
                acc_v = [acc_v[0] + col(duv * xv2), acc_v[1] + col(duv * xv1), acc_v[2] + col(duv * xv), acc_v[3] + col(duv)]
            for t in range(4):
                dw_ref[t:t + 1, cg] += acc_g[t]
                dw_ref[t:t + 1, cv] += acc_v[t]
            return carry

        lax.fori_loop(0, FF_BLK // CONV_LANES, du_chunk, 0)
        du_scr[tm:tm + 8, :] = jnp.where(seq_end, 0.0, nxt_ref[...])

        def dup_chunk(c, carry):
            cols = pl.ds(pl.multiple_of(c * CONV_LANES, CONV_LANES), CONV_LANES)
            w = w_ref[:, cols]
            for r0 in range(0, tm, CONV_ROWS):
                d0 = du_scr[r0:r0 + CONV_ROWS, cols]
                d1 = du_scr[r0 + 1:r0 + 1 + CONV_ROWS, cols]
                d2 = du_scr[r0 + 2:r0 + 2 + CONV_ROWS, cols]
                dup_ref[r0:r0 + CONV_ROWS, cols] = (w[2:3] * d0 + w[1:2] * d1 + w[0:1] * d2).astype(BF16)
            return carry

        lax.fori_loop(0, W // CONV_LANES, dup_chunk, 0)
        nxt_ref[...] = du_scr[0:8, :]

    hb = tm // 8
    rb = lambda s: ns - 1 - s
    return pl.pallas_call(
        body, name=name, grid=(2, Bl, ns),
        in_specs=[pl.BlockSpec((None, tm, W), lambda c, b, s: (b, rb(s), c)),
                  pl.BlockSpec((None, 8, W), lambda c, b, s: (b, jnp.maximum(rb(s) * hb - 1, 0), c)),
                  pl.BlockSpec((3, W), lambda c, b, s: (0, c)), pl.BlockSpec((1, W), lambda c, b, s: (0, c)),
                  pl.BlockSpec((None, tm, FF_BLK), lambda c, b, s: (b, rb(s), c))],
        out_specs=[pl.BlockSpec((None, tm, W), lambda c, b, s: (b, rb(s), c)), pl.BlockSpec((8, W), lambda c, b, s: (0, c))],
        out_shape=[jax.ShapeDtypeStruct((Bl, S, 2 * D_FF), BF16), jax.ShapeDtypeStruct((8, 2 * D_FF), F32)],
        scratch_shapes=[pltpu.VMEM((8, W), F32), pltpu.VMEM((tm + 8, W), F32)],
        compiler_params=_cp("arbitrary", "arbitrary", "arbitrary"),
    )(up3, up3, cw, cb, da3)


def gate_bwd(dx3, y3, gate, *, tm=512, name):
    Bl, S, D = dx3.shape
    tm = min(tm, S)

    def body(dx_ref, y_ref, g_ref, o_ref, dg_ref):
        @pl.when(pl.program_id(1) == 0)
        def _():
            dg_ref[...] = jnp.zeros_like(dg_ref)

        dx = dx_ref[...]
        dg_ref[...] += jnp.sum(dx * y_ref[...], axis=0, keepdims=True)
        o_ref[...] = (dx * g_ref[...]).astype(BF16)

    blk = pl.BlockSpec((None, tm, D), lambda b, s: (b, s, 0))
    vec = pl.BlockSpec((None, 1, D), lambda b, s: (b, 0, 0))
    return pl.pallas_call(
        body, name=name, grid=(Bl, S // tm), in_specs=[blk, blk, vec], out_specs=[blk, vec],
        out_shape=[jax.ShapeDtypeStruct((Bl, S, D), BF16), jax.ShapeDtypeStruct((Bl, 1, D), F32)],
        compiler_params=_cp("parallel", "arbitrary"),
    )(dx3, y3, gate)


def loss_grad(y3, t3, *, tm=512, name):
    Bl, S, D = y3.shape
    tm = min(tm, S)
    last = (Bl - 1, S // tm - 1)

    def body(y_ref, t_ref, dy_ref, l_ref, acc_ref):
        b, s = pl.program_id(0), pl.program_id(1)

        @pl.when((b == 0) & (s == 0))
        def _():
            acc_ref[...] = jnp.zeros_like(acc_ref)

        e = y_ref[...] - t_ref[...]
        dy_ref[...] = e * (1.0 / D)
        acc_ref[...] += jnp.sum(e * e, axis=0, keepdims=True)

        @pl.when((b == last[0]) & (s == last[1]))
        def _():
            l_ref[...] = jnp.broadcast_to(jnp.sum(acc_ref[...], axis=1, keepdims=True) * (0.5 / D), (1, LANES))

    blk = pl.BlockSpec((None, tm, D), lambda b, s: (b, s, 0))
    return pl.pallas_call(
        body, name=name, grid=(Bl, S // tm), in_specs=[blk, blk],
        out_specs=[blk, pl.BlockSpec((1, LANES), lambda b, s: (0, 0))],
        out_shape=[jax.ShapeDtypeStruct((Bl, S, D), F32), jax.ShapeDtypeStruct((1, LANES), F32)],
        scratch_shapes=[pltpu.VMEM((1, D), F32)], compiler_params=_cp("arbitrary", "arbitrary"),
    )(y3, t3)


def adamw(w, g, m, v, *, name):
    L, R, C = w.shape
    tr = _tile(R, 512, 8)

    def body(w_ref, g_ref, m_ref, v_ref, d_ref, m2_ref, v2_ref):
        d_ref[...], m2_ref[...], v2_ref[...] = _adam_update(w_ref[...], g_ref[...], m_ref[...], v_ref[...])

    blk = pl.BlockSpec((None, tr, C), lambda l, i: (l, i, 0))
    shp = jax.ShapeDtypeStruct((L, R, C), F32)
    return pl.pallas_call(
        body, name=name, grid=(L, R // tr), in_specs=[blk] * 4, out_specs=[blk] * 3, out_shape=[shp] * 3,
        compiler_params=_cp("parallel", "parallel"),
    )(w, g, m, v)


def sum_leading(x, *, out_dtype=F32, tr=256, name):
    n, R, C = x.shape
    tr = _tile(R, tr, 16)

    def body(x_ref, o_ref):
        acc = x_ref[0].astype(F32)
        for k in range(1, n):
            acc = acc + x_ref[k].astype(F32)
        o_ref[...] = acc.astype(out_dtype)

    return pl.pallas_call(
        body, name=name, grid=(R // tr,), in_specs=[pl.BlockSpec((n, tr, C), lambda i: (0, i, 0))],
        out_specs=pl.BlockSpec((tr, C), lambda i: (i, 0)), out_shape=jax.ShapeDtypeStruct((R, C), out_dtype),
        compiler_params=_cp("parallel"),
    )(x)


def _adam_update(w, g, m, v):
    c1 = 1.0 / (1.0 - ADAM_B1 ** ADAM_STEP)
    c2 = 1.0 / (1.0 - ADAM_B2 ** ADAM_STEP)
    m2 = ADAM_B1 * m + (1.0 - ADAM_B1) * g
    v2 = ADAM_B2 * v + (1.0 - ADAM_B2) * (g * g)
    return -ADAM_LR * ((m2 * c1) / (jnp.sqrt(v2 * c2) + ADAM_EPS) + ADAM_WD * w), m2, v2


def adamw_small(ws, gs, ms, vs, *, name):
    na = len(ws)

    def body(*refs):
        w_r, g_r, m_r, v_r = (refs[i * na:(i + 1) * na] for i in range(4))
        d_r, m2_r, v2_r = (refs[(4 + i) * na:(5 + i) * na] for i in range(3))
        for a in range(na):
            d_r[a][...], m2_r[a][...], v2_r[a][...] = _adam_update(w_r[a][...], g_r[a][...], m_r[a][...], v_r[a][...])

    vm = pl.BlockSpec(memory_space=pltpu.VMEM)
    shp = [jax.ShapeDtypeStruct(w.shape, F32) for w in ws]
    out = pl.pallas_call(body, name=name, in_specs=[vm] * (4 * na), out_specs=[vm] * (3 * na), out_shape=shp * 3)(*ws, *gs, *ms, *vs)
    return out[:na], out[na:2 * na], out[2 * na:]


def sum_small(xs, *, name):
    na = len(xs)

    def body(*refs):
        for x_ref, o_ref in zip(refs[:na], refs[na:]):
            acc = x_ref[0]
            for k in range(1, x_ref.shape[0]):
                acc = acc + x_ref[k]
            o_ref[...] = acc

    vm = pl.BlockSpec(memory_space=pltpu.VMEM)
    return pl.pallas_call(body, name=name, in_specs=[vm] * na, out_specs=[vm] * na,
                          out_shape=[jax.ShapeDtypeStruct(x.shape[1:], x.dtype) for x in xs])(*xs)


def pair_add_half(g4, recv, c_arr, *, tr=512, name):
    _, R, C = g4.shape
    H = R // 2
    tr = _tile(H, tr, 16)
    nb = H // tr

    def body(c_ref, g_ref, r_ref, o_ref):
        o_ref[...] = (g_ref[...].astype(F32) + r_ref[...].astype(F32)).astype(BF16)

    grid_spec = pltpu.PrefetchScalarGridSpec(
        num_scalar_prefetch=1, grid=(4, nb),
        in_specs=[pl.BlockSpec((None, tr, C), lambda k, i, c_ref: (k, c_ref[0] * nb + i, 0)),
                  pl.BlockSpec((None, tr, C), lambda k, i, c_ref: (k, i, 0))],
        out_specs=pl.BlockSpec((None, tr, C), lambda k, i, c_ref: (k, i, 0)),
    )
    return pl.pallas_call(
        body, name=name, grid_spec=grid_spec, out_shape=jax.ShapeDtypeStruct((4, H, C), BF16),
        compiler_params=_cp("parallel", "parallel"),
    )(c_arr, g4, recv)


def chip_sum_into(landed, pair, sel, *, tr=512, name):
    _, H, C = landed.shape
    tr = _tile(H, tr, 16)
    nb = H // tr

    def body(s_ref, l0, l1, l2, l3, p_ref, o_ref):
        own = p_ref[...].astype(F32)
        acc = None
        for k, l_ref in enumerate((l0, l1, l2, l3)):
            part = jnp.where(s_ref[0] == k, own, l_ref[...].astype(F32))
            acc = part if acc is None else acc + part
        o_ref[...] = acc

    def slot(k):
        return pl.BlockSpec((None, tr, C), lambda i, s: (jnp.where(s[0] == k, (k + 1) % 4, k), i, 0))

    grid_spec = pltpu.PrefetchScalarGridSpec(
        num_scalar_prefetch=1, grid=(nb,),
        in_specs=[slot(0), slot(1), slot(2), slot(3), pl.BlockSpec((None, tr, C), lambda i, s: (s[0], i, 0))],
        out_specs=pl.BlockSpec((tr, C), lambda i, s: (s[1] * nb + i, 0)),
    )
    return pl.pallas_call(
        body, name=name, grid_spec=grid_spec, out_shape=jax.ShapeDtypeStruct((2 * H, C), F32), compiler_params=_cp("parallel"),
    )(sel, landed, landed, landed, landed, pair)


def mods_matmul(c_all, w_ada, b_ada_cols, *, tn=512, name):
    L, D, E = w_ada.shape
    nb = c_all.shape[0]
    tn = _tile(E, tn)

    def body(c_ref, w_ref, b_ref, o_ref):
        c = c_ref[...]
        a = c * jax.nn.sigmoid(c)
        o_ref[...] = jnp.dot(a, w_ref[...], preferred_element_type=F32, precision=lax.Precision.HIGHEST) + b_ref[...]

    return pl.pallas_call(
        body, name=name, grid=(L, E // tn),
        in_specs=[pl.BlockSpec((nb, D), lambda l, j: (0, 0)), pl.BlockSpec((None, D, tn), lambda l, j: (l, 0, j)),
                  pl.BlockSpec((None, 1, tn), lambda l, j: (l, 0, j))],
        out_specs=pl.BlockSpec((None, nb, tn), lambda l, j: (l, 0, j)),
        out_shape=jax.ShapeDtypeStruct((L, nb, E), F32), compiler_params=_cp("parallel", "parallel"),
    )(c_all, w_ada, b_ada_cols)


def ada_grad(c_all, dmods, *, tn=512, name):
    L, nb, E = dmods.shape
    D = c_all.shape[1]
    tn = _tile(E, tn)

    def body(c_ref, d_ref, o_ref):
        c = c_ref[...]
        a = c * jax.nn.sigmoid(c)
        o_ref[...] = lax.dot_general(a, d_ref[...], (((0,), (0,)), ((), ())), preferred_element_type=F32, precision=lax.Precision.HIGHEST)

    return pl.pallas_call(
        body, name=name, grid=(L, E // tn),
        in_specs=[pl.BlockSpec((nb, D), lambda l, j: (0, 0)), pl.BlockSpec((None, nb, tn), lambda l, j: (l, 0, j))],
        out_specs=pl.BlockSpec((None, D, tn), lambda l, j: (l, 0, j)),
        out_shape=jax.ShapeDtypeStruct((L, D, E), F32), compiler_params=_cp("parallel", "parallel"),
    )(c_all, dmods)


HBM = pl.BlockSpec(memory_space=pltpu.HBM)


def _me():
    return lax.axis_index("x"), lax.axis_index("y"), lax.axis_index("c")


def _flip(v, bit):
    return 1 - v if bit else v


def allgather8(xs, *, name):
    na = len(xs)

    def body(*refs):
        x_refs, out_refs = refs[:na], refs[na:2 * na]
        send_sems, recv_sems = refs[2 * na], refs[2 * na + 1]
        x, y, c = _me()
        me = 4 * x + 2 * y + c
        for x_ref, out_ref in zip(x_refs, out_refs):
            out_ref[me] = x_ref[...]
        sends = []
        for a, (x_ref, out_ref) in enumerate(zip(x_refs, out_refs)):
            for k in range(1, 8):
                peer = (_flip(x, k & 4), _flip(y, k & 2), _flip(c, k & 1))
                cp = pltpu.make_async_remote_copy(src_ref=x_ref, dst_ref=out_ref.at[me], send_sem=send_sems.at[a, k - 1],
                                                  recv_sem=recv_sems.at[a, k - 1], device_id=peer, device_id_type=MESH)
                cp.start()
                sends.append(cp)
        for a, (x_ref, out_ref) in enumerate(zip(x_refs, out_refs)):
            for k in range(1, 8):
                peer = (_flip(x, k & 4), _flip(y, k & 2), _flip(c, k & 1))
                src = 4 * peer[0] + 2 * peer[1] + peer[2]
                pltpu.make_async_remote_copy(src_ref=x_ref, dst_ref=out_ref.at[src], send_sem=send_sems.at[a, k - 1],
                                             recv_sem=recv_sems.at[a, k - 1], device_id=peer, device_id_type=MESH).wait_recv()
        for cp in sends:
            cp.wait_send()

    vm = pl.BlockSpec(memory_space=pltpu.VMEM)
    return pl.pallas_call(
        body, name=name, in_specs=[vm] * na, out_specs=[vm] * na,
        out_shape=[jax.ShapeDtypeStruct((8,) + a.shape, a.dtype) for a in xs],
        scratch_shapes=[pltpu.SemaphoreType.DMA((na, 7)), pltpu.SemaphoreType.DMA((na, 7))],
    )(*xs)


LOCAL_CHUNKS = 8


def _copy_via_vmem(src, dst_at, rows, buf, sem):
    ch = buf.shape[0]
    for i in range(rows // ch):
        load = pltpu.make_async_copy(src.at[pl.ds(i * ch, ch)], buf, sem)
        load.start()
        load.wait()
        store = pltpu.make_async_copy(buf, dst_at(i * ch, ch), sem)
        store.start()
        store.wait()


def _chunk_buf(rows, cols, dtype):
    align = 16 if dtype == BF16 else 8
    for n in range(LOCAL_CHUNKS, 0, -1):
        if rows % n == 0 and (rows // n) % align == 0:
            return pltpu.VMEM((rows // n, cols), dtype)
    return pltpu.VMEM((rows, cols), dtype)


def gather_weights(ws, *, name):
    na = len(ws)

    def body(*refs):
        x_refs, out_refs = refs[:na], refs[na:2 * na]
        send_sems, recv_sems, local_sem = refs[2 * na:2 * na + 3]
        bufs = refs[2 * na + 3:]
        x, y, c = _me()
        j = 2 * x + y
        chips = [(_flip(x, k & 2), _flip(y, k & 1)) for k in range(1, 4)]
        sends = []
        for a, (x_ref, out_ref) in enumerate(zip(x_refs, out_refs)):
            H = x_ref.shape[0] // 2
            for k, (px, py) in enumerate(chips):
                cp = pltpu.make_async_remote_copy(src_ref=x_ref.at[pl.ds(c * H, H)], dst_ref=out_ref.at[j, pl.ds(c * H, H)],
                                                  send_sem=send_sems.at[a, k], recv_sem=recv_sems.at[a, k],
                                                  device_id=(px, py, c), device_id_type=MESH)
                cp.start()
                sends.append(cp)
        for x_ref, out_ref, buf in zip(x_refs, out_refs, bufs):
            _copy_via_vmem(x_ref, lambda o, n, out_ref=out_ref: out_ref.at[j, pl.ds(o, n)], x_ref.shape[0], buf, local_sem)
        for a, out_ref in enumerate(out_refs):
            H = out_ref.shape[1] // 2
            for k, (px, py) in enumerate(chips):
                slot = out_ref.at[2 * px + py, pl.ds(c * H, H)]
                pltpu.make_async_remote_copy(src_ref=slot, dst_ref=slot, send_sem=send_sems.at[a, k], recv_sem=recv_sems.at[a, k],
                                             device_id=(px, py, c), device_id_type=MESH).wait_recv()
                cp = pltpu.make_async_remote_copy(src_ref=slot, dst_ref=slot, send_sem=send_sems.at[a, 3 + k],
                                                  recv_sem=recv_sems.at[a, 3 + k], device_id=(x, y, 1 - c), device_id_type=MESH)
                cp.start()
                sends.append(cp)
        for a, out_ref in enumerate(out_refs):
            H = out_ref.shape[1] // 2
            for k, (px, py) in enumerate(chips):
                slot = out_ref.at[2 * px + py, pl.ds((1 - c) * H, H)]
                pltpu.make_async_remote_copy(src_ref=slot, dst_ref=slot, send_sem=send_sems.at[a, 3 + k], recv_sem=recv_sems.at[a, 3 + k],
                                             device_id=(x, y, 1 - c), device_id_type=MESH).wait_recv()
        for cp in sends:
            cp.wait_send()

    return pl.pallas_call(
        body, name=name, in_specs=[HBM] * na, out_specs=[HBM] * na,
        out_shape=[jax.ShapeDtypeStruct((4,) + w.shape, w.dtype) for w in ws],
        scratch_shapes=[pltpu.SemaphoreType.DMA((na, 6)), pltpu.SemaphoreType.DMA((na, 6)), pltpu.SemaphoreType.DMA]
        + [_chunk_buf(w.shape[0], w.shape[1], w.dtype) for w in ws],
    )(*ws)


def swap_halves(gs, *, name):
    na = len(gs)

    def body(*refs):
        g_refs, out_refs = refs[:na], refs[na:2 * na]
        send_sems, recv_sems = refs[2 * na:]
        x, y, c = _me()
        sib = (x, y, 1 - c)
        sends = []
        for a, (g_ref, out_ref) in enumerate(zip(g_refs, out_refs)):
            H = g_ref.shape[1] // 2
            for k in range(4):
                cp = pltpu.make_async_remote_copy(src_ref=g_ref.at[k, pl.ds((1 - c) * H, H)], dst_ref=out_ref.at[k],
                                                  send_sem=send_sems.at[a, k], recv_sem=recv_sems.at[a, k], device_id=sib, device_id_type=MESH)
                cp.start()
                sends.append(cp)
        for a, (g_ref, out_ref) in enumerate(zip(g_refs, out_refs)):
            H = g_ref.shape[1] // 2
            for k in range(4):
                pltpu.make_async_remote_copy(src_ref=g_ref.at[k, pl.ds(c * H, H)], dst_ref=out_ref.at[k], send_sem=send_sems.at[a, k],
                                             recv_sem=recv_sems.at[a, k], device_id=sib, device_id_type=MESH).wait_recv()
        for cp in sends:
            cp.wait_send()

    return pl.pallas_call(
        body, name=name, in_specs=[HBM] * na, out_specs=[HBM] * na,
        out_shape=[jax.ShapeDtypeStruct((4, g.shape[1] // 2, g.shape[2]), g.dtype) for g in gs],
        scratch_shapes=[pltpu.SemaphoreType.DMA((na, 4)), pltpu.SemaphoreType.DMA((na, 4))],
    )(*gs)


def scatter_chips(ps, *, name):
    na = len(ps)

    def body(*refs):
        p_refs, out_refs = refs[:na], refs[na:2 * na]
        send_sems, recv_sems, local_sem = refs[2 * na:2 * na + 3]
        bufs = refs[2 * na + 3:]
        x, y, c = _me()
        j = 2 * x + y
        chips = [(_flip(x, k & 2), _flip(y, k & 1)) for k in range(1, 4)]
        sends = []
        for a, (p_ref, out_ref) in enumerate(zip(p_refs, out_refs)):
            for k, (px, py) in enumerate(chips):
                cp = pltpu.make_async_remote_copy(src_ref=p_ref.at[2 * px + py], dst_ref=out_ref.at[j], send_sem=send_sems.at[a, k],
                                                  recv_sem=recv_sems.at[a, k], device_id=(px, py, c), device_id_type=MESH)
                cp.start()
                sends.append(cp)
        for p_ref, out_ref, buf in zip(p_refs, out_refs, bufs):
            _copy_via_vmem(p_ref.at[j], lambda o, n, out_ref=out_ref: out_ref.at[j, pl.ds(o, n)], p_ref.shape[1], buf, local_sem)
        for a, out_ref in enumerate(out_refs):
            for k, (px, py) in enumerate(chips):
                slot = out_ref.at[2 * px + py]
                pltpu.make_async_remote_copy(src_ref=slot, dst_ref=slot, send_sem=send_sems.at[a, k], recv_sem=recv_sems.at[a, k],
                                             device_id=(px, py, c), device_id_type=MESH).wait_recv()
        for cp in sends:
            cp.wait_send()

    return pl.pallas_call(
        body, name=name, in_specs=[HBM] * na, out_specs=[HBM] * na, out_shape=[jax.ShapeDtypeStruct(p.shape, p.dtype) for p in ps],
        scratch_shapes=[pltpu.SemaphoreType.DMA((na, 3)), pltpu.SemaphoreType.DMA((na, 3)), pltpu.SemaphoreType.DMA]
        + [_chunk_buf(p.shape[1], p.shape[2], p.dtype) for p in ps],
    )(*ps)


def join_halves(halves, *, name):
    na = len(halves)

    def body(*refs):
        h_refs, out_refs = refs[:na], refs[na:2 * na]
        send_sems, recv_sems, local_sem = refs[2 * na:2 * na + 3]
        bufs = refs[2 * na + 3:]
        x, y, c = _me()
        sib = (x, y, 1 - c)
        sends = []
        for a, (h_ref, out_ref) in enumerate(zip(h_refs, out_refs)):
            H = h_ref.shape[0]
            cp = pltpu.make_async_remote_copy(src_ref=h_ref, dst_ref=out_ref.at[pl.ds(c * H, H)], send_sem=send_sems.at[a],
                                              recv_sem=recv_sems.at[a], device_id=sib, device_id_type=MESH)
            cp.start()
            sends.append(cp)
        for h_ref, out_ref, buf in zip(h_refs, out_refs, bufs):
            H = h_ref.shape[0]
            _copy_via_vmem(h_ref, lambda o, n, out_ref=out_ref, H=H: out_ref.at[pl.ds(c * H + o, n)], H, buf, local_sem)
        for a, (h_ref, out_ref) in enumerate(zip(h_refs, out_refs)):
            H = h_ref.shape[0]
            pltpu.make_async_remote_copy(src_ref=h_ref, dst_ref=out_ref.at[pl.ds((1 - c) * H, H)], send_sem=send_sems.at[a],
                                         recv_sem=recv_sems.at[a], device_id=sib, device_id_type=MESH).wait_recv()
        for cp in sends:
            cp.wait_send()

    return pl.pallas_call(
        body, name=name, in_specs=[HBM] * na, out_specs=[HBM] * na,
        out_shape=[jax.ShapeDtypeStruct((2 * h.shape[0], h.shape[1]), h.dtype) for h in halves],
        scratch_shapes=[pltpu.SemaphoreType.DMA((na,)), pltpu.SemaphoreType.DMA((na,)), pltpu.SemaphoreType.DMA]
        + [_chunk_buf(h.shape[0], h.shape[1], h.dtype) for h in halves],
    )(*halves)


class _Plan:
    def __init__(self, ins, out_shapes, ncopies, copies, aliased=False):
        self.ins, self.out_shapes, self.ncopies, self.copies, self.aliased = list(ins), list(out_shapes), ncopies, copies, aliased

    def start(self, in_refs, out_refs, send_sems, recv_sems):
        sends, _ = self.copies(in_refs, out_refs, send_sems, recv_sems)
        for cp in sends:
            cp.start()

    def finish(self, in_refs, out_refs, send_sems, recv_sems):
        sends, recvs = self.copies(in_refs, out_refs, send_sems, recv_sems)
        for cp in recvs:
            cp.wait_recv()
        for cp in sends:
            cp.wait_send()


def _rcopy(src, dst, send_sems, recv_sems, idx, dev):
    return pltpu.make_async_remote_copy(src_ref=src, dst_ref=dst, send_sem=send_sems.at[idx], recv_sem=recv_sems.at[idx],
                                        device_id=dev, device_id_type=MESH)


def _other_chips(x, y):
    return [(_flip(x, k & 2), _flip(y, k & 1)) for k in range(1, 4)]


def plan_gather_ici(ws):
    def copies(in_refs, out_refs, ss, rs):
        x, y, c = _me()
        j = 2 * x + y
        sends, recvs = [], []
        for a, (x_ref, out_ref) in enumerate(zip(in_refs, out_refs)):
            H = x_ref.shape[0] // 2
            for k, (px, py) in enumerate(_other_chips(x, y)):
                sends.append(_rcopy(x_ref.at[pl.ds(c * H, H)], out_ref.at[j, pl.ds(c * H, H)], ss, rs, 3 * a + k, (px, py, c)))
                slot = out_ref.at[2 * px + py, pl.ds(c * H, H)]
                recvs.append(_rcopy(slot, slot, ss, rs, 3 * a + k, (px, py, c)))
        return sends, recvs

    return _Plan(ws, [jax.ShapeDtypeStruct((4,) + w.shape, w.dtype) for w in ws], 3 * len(ws), copies)


def plan_gather_d2d(w4s):
    def copies(in_refs, out_refs, ss, rs):
        x, y, c = _me()
        sends, recvs = [], []
        for a, out_ref in enumerate(out_refs):
            H = out_ref.shape[1] // 2
            for k, (px, py) in enumerate(_other_chips(x, y)):
                mine = out_ref.at[2 * px + py, pl.ds(c * H, H)]
                theirs = out_ref.at[2 * px + py, pl.ds((1 - c) * H, H)]
                sends.append(_rcopy(mine, mine, ss, rs, 3 * a + k, (x, y, 1 - c)))
                recvs.append(_rcopy(theirs, theirs, ss, rs, 3 * a + k, (x, y, 1 - c)))
        return sends, recvs

    return _Plan(w4s, [jax.ShapeDtypeStruct(w.shape, w.dtype) for w in w4s], 3 * len(w4s), copies, aliased=True)


def plan_swap_halves(gs):
    def copies(in_refs, out_refs, ss, rs):
        x, y, c = _me()
        sends, recvs = [], []
        for a, (g_ref, out_ref) in enumerate(zip(in_refs, out_refs)):
            H = g_ref.shape[1] // 2
            for k in range(4):
                sends.append(_rcopy(g_ref.at[k, pl.ds((1 - c) * H, H)], out_ref.at[k], ss, rs, 4 * a + k, (x, y, 1 - c)))
                recvs.append(_rcopy(g_ref.at[k, pl.ds(c * H, H)], out_ref.at[k], ss, rs, 4 * a + k, (x, y, 1 - c)))
        return sends, recvs

    return _Plan(gs, [jax.ShapeDtypeStruct((4, g.shape[1] // 2, g.shape[2]), g.dtype) for g in gs], 4 * len(gs), copies)


def plan_scatter_ici(ps):
    def copies(in_refs, out_refs, ss, rs):
        x, y, c = _me()
        j = 2 * x + y
        sends, recvs = [], []
        for a, (p_ref, out_ref) in enumerate(zip(in_refs, out_refs)):
            for k, (px, py) in enumerate(_other_chips(x, y)):
                sends.append(_rcopy(p_ref.at[2 * px + py], out_ref.at[j], ss, rs, 3 * a + k, (px, py, c)))
                slot = out_ref.at[2 * px + py]
                recvs.append(_rcopy(slot, slot, ss, rs, 3 * a + k, (px, py, c)))
        return sends, recvs

    return _Plan(ps, [jax.ShapeDtypeStruct(p.shape, p.dtype) for p in ps], 3 * len(ps), copies)


def plan_join_halves(fulls):
    def copies(in_refs, out_refs, ss, rs):
        x, y, c = _me()
        sends, recvs = [], []
        for a, out_ref in enumerate(out_refs):
            H = out_ref.shape[0] // 2
            mine, theirs = out_ref.at[pl.ds(c * H, H)], out_ref.at[pl.ds((1 - c) * H, H)]
            sends.append(_rcopy(mine, mine, ss, rs, a, (x, y, 1 - c)))
            recvs.append(_rcopy(theirs, theirs, ss, rs, a, (x, y, 1 - c)))
        return sends, recvs

    return _Plan(fulls, [jax.ShapeDtypeStruct(f.shape, f.dtype) for f in fulls], len(fulls), copies, aliased=True)


def call_with_plans(body, plans, *, grid, in_specs, out_specs, out_shape, scratch_shapes, args, sem, name):
    plans = list(plans or [])
    n_in, n_out, n_scr = len(in_specs), len(out_specs), len(scratch_shapes)
    c_in = [len(p.ins) for p in plans]
    c_out = [len(p.out_shapes) for p in plans]
    steps = math.prod(grid) if grid else 1

    def wrapped(*refs):
        pos = 0

        def take(n):
            nonlocal pos
            out = refs[pos:pos + n]
            pos += n
            return out

        ins = take(n_in)
        cins = [take(n) for n in c_in]
        outs = take(n_out)
        couts = [take(n) for n in c_out]
        scr = take(n_scr)
        sems = [take(2) for _ in plans]
        def start_all():
            for p, ci, co, (ss, rs) in zip(plans, cins, couts, sems):
                p.start(ci, co, ss, rs)

        def finish_all():
            for p, ci, co, (ss, rs) in zip(plans, cins, couts, sems):
                p.finish(ci, co, ss, rs)

        if plans and grid:
            idx = 0
            for ax, g in enumerate(grid):
                idx = idx * g + pl.program_id(ax)
            pl.when(idx == 0)(start_all)
        elif plans:
            start_all()
        if body is not None:
            body(*ins, *outs, *scr)
        if plans and grid:
            pl.when(idx == steps - 1)(finish_all)
        elif plans:
            finish_all()

    aliases = {}
    i_pos, o_pos = n_in, n_out
    for p, ni, no in zip(plans, c_in, c_out):
        if p.aliased:
            aliases.update({i_pos + t: o_pos + t for t in range(ni)})
        i_pos += ni
        o_pos += no
    kwargs = dict(grid=grid) if grid else {}
    if aliases:
        kwargs["input_output_aliases"] = aliases
    res = pl.pallas_call(
        wrapped, name=name, in_specs=list(in_specs) + [HBM] * sum(c_in), out_specs=list(out_specs) + [HBM] * sum(c_out),
        out_shape=list(out_shape) + [s for p in plans for s in p.out_shapes],
        scratch_shapes=list(scratch_shapes) + [pltpu.SemaphoreType.DMA((p.ncopies,)) for p in plans for _ in range(2)],
        compiler_params=_cp(*sem) if grid else pltpu.CompilerParams(vmem_limit_bytes=VMEM_LIMIT), **kwargs,
    )(*args, *[a for p in plans for a in p.ins])
    res = list(res)
    comp, rest = res[:n_out], res[n_out:]
    pouts = []
    for no in c_out:
        pouts.append(rest[:no])
        rest = rest[no:]
    return comp, pouts


def run_plans(plans, *, name):
    return call_with_plans(None, plans, grid=(), in_specs=[], out_specs=[], out_shape=[], scratch_shapes=[], args=[], sem=(), name=name)[1]


def _cat(parts, axis=-1):
    return jnp.concatenate(parts, axis=axis)


def _pairs_of_heads(a, axis, inverse=False):
    lead, tail = a.shape[:axis], a.shape[axis + 1:]
    split = (3, 2) if inverse else (2, 3)
    a = a.reshape(lead + split + (HEAD,) + tail)
    return jnp.swapaxes(a, axis, axis + 1).reshape(lead + (6 * HEAD,) + tail)


def _prep_w_in(w):
    z = lambda n: jnp.zeros((w.shape[0], n), w.dtype)
    return _cat([w[:, 0:1152], z(64), w[:, 1152:1184], z(32), _pairs_of_heads(w[:, 1184:1568], 1), w[:, 1568:1824]])


def _unprep_w_in(g):
    return _cat([g[:, 0:1152], g[:, 1216:1248], _pairs_of_heads(g[:, P_SWQ:P_SWK], 1, inverse=True), g[:, P_SWK:P_END]])


def _prep_w_uq(w):
    r = w.shape[0]
    return jnp.pad(w.reshape(r, 6, MLA_QK), ((0, 0), (0, 0), (0, LANES - MLA_QK))).reshape(r, 6 * LANES)


def _unprep_w_uq(g):
    r = g.shape[0]
    return g.reshape(r, 6, LANES)[:, :, :MLA_QK].reshape(r, 6 * MLA_QK)


def _prep_w_ukv(w):
    r = w.shape[0]
    w3 = w.reshape(r, 6, LANES)
    k = jnp.pad(w3[:, :, :HEAD], ((0, 0), (0, 0), (0, LANES - HEAD))).reshape(r, 6 * LANES)
    return _cat([k, w3[:, :, HEAD:].reshape(r, 6 * HEAD)])


def _unprep_w_ukv(g):
    r = g.shape[0]
    k = g[:, :6 * LANES].reshape(r, 6, LANES)[:, :, :HEAD]
    return _cat([k, g[:, 6 * LANES:].reshape(r, 6, HEAD)], axis=2).reshape(r, 6 * LANES)


def _prep_w_out(w):
    return _cat([w[0:640], _pairs_of_heads(w[640:], 0)], axis=0)


def _unprep_w_out(g):
    return _cat([g[0:640], _pairs_of_heads(g[640:], 0, inverse=True)], axis=0)


def _rope_tables(positions):
    half = 16
    inv_freq = jnp.power(ROPE_THETA, -jnp.arange(half, dtype=F32) / half)
    ang = positions.astype(F32)[..., None] * inv_freq
    cos, sin = jnp.cos(ang), jnp.sin(ang)
    z = lambda n: jnp.zeros(ang.shape[:-1] + (n,), F32)
    return (_cat([jnp.ones(ang.shape[:-1] + (HEAD,), F32), cos, cos, z(32)]), _cat([z(HEAD), -sin, z(16), z(32)]), _cat([z(HEAD), z(16), sin, z(32)]))


def _small_params(p):
    pad96 = lambda g: _cat([g, jnp.zeros((32,), F32)]).reshape(1, LANES)
    two = lambda g: _cat([g, g]).reshape(1, LANES)
    sinks = jnp.broadcast_to(p["sw_sinks"].reshape(2, 3).T[:, :, None], (3, 2, LANES))
    return dict(n1=p["norm1_g"].reshape(1, -1), n2=p["norm2_g"].reshape(1, -1), cq_g=p["mla_cq_g"].reshape(1, -1),
                ckv_g=p["mla_ckv_g"].reshape(1, -1), qn_g=pad96(p["mla_qn_g"]), kn_g=pad96(p["mla_kn_g"]),
                swq_g=two(p["sw_qn_g"]), swk_g=two(p["sw_kn_g"]), sinks=sinks, conv_b=_up_perm(p["conv_b"]).reshape(1, -1))


class _NoFlow:
    def plans(self, tag):
        return []

    def done(self, tag, outs):
        pass

    def add(self, key, g):
        pass


def _layer_fwd(x3, md, W, tabs, bias, tag, flow=_NoFlow()):
    Bl, S, D = x3.shape
    T = Bl * S
    n = lambda s: f"{s}_{tag}"
    two = lambda a: a.reshape(T, a.shape[-1])
    three = lambda a: a.reshape(Bl, S, a.shape[-1])
    h = rms_fwd(x3, 0, D, W["n1"], md["scale1"], md["shift1"], name=n("norm1"))
    proj = three(matmul(two(h), W["w_in"], tn=1920, name=n("in_proj")))
    (o_a, rt_a), got = sb_attn_fwd(proj, plans=flow.plans(n("sb_fwd")), name=n("sb_fwd"))
    flow.done(n("sb_fwd"), got)
    cqn = rms_fwd(proj, P_CQ // 256, 256, W["cq_g"], name=n("cq_norm"))
    ckvn = rms_fwd(proj, P_CKV // LANES, LANES, W["ckv_g"], name=n("ckv_norm"))
    qb = three(matmul(two(cqn), W["w_uq"], tm=1024, tn=768, name=n("uq")))
    kvb = three(matmul(two(ckvn), W["w_ukv"], tm=1024, tn=1152, name=n("ukv")))
    q_m = rope_norm_fwd(qb, 6, W["qn_g"], tabs, name=n("q_rope"))
    k_m = rope_norm_fwd(kvb, 6, W["kn_g"], tabs, (proj, P_SLAB // LANES), name=n("k_rope"))
    (o_b, lse_b), got = mla_attn_fwd(q_m, k_m, kvb, 6, plans=flow.plans(n("mla_fwd")), name=n("mla_fwd"))
    flow.done(n("mla_fwd"), got)
    q_c = pair_rms_fwd(proj, P_SWQ // LANES, 3, W["swq_g"], name=n("swq_norm"))
    k_c = pair_rms_fwd(proj, P_SWK // LANES, 1, W["swk_g"], name=n("swk_norm"))
    (o_c, lse_c), got = swa_attn_fwd(q_c, k_c, proj, bias, W["sinks"], plans=flow.plans(n("swa_fwd")), name=n("swa_fwd"))
    flow.done(n("swa_fwd"), got)
    mix = _cat([o_a, o_b, o_c]).astype(BF16)
    att, x1 = matmul_res(two(mix), W["w_out"], two(x3), md["gate1"], S, name=n("out_proj"))
    x1 = three(x1)
    h2 = rms_fwd(x1, 0, D, W["n2"], md["scale2"], md["shift2"], name=n("norm2"))
    up = three(matmul(two(h2), W["w_up"], tn=1408, name=n("up_proj")))
    a = conv_gate_fwd(up, W["conv_w"], W["conv_b"], name=n("conv_gate"))
    yd, x2 = matmul_res(two(a), W["w_down"], two(x1), md["gate2"], S, name=n("down_proj"))
    saved = dict(x=x3, h=h, proj=proj, rt_a=rt_a, cqn=cqn, ckvn=ckvn, qb=qb, kvb=kvb, q_m=q_m, k_m=k_m, o_b=o_b, lse_b=lse_b,
                 q_c=q_c, k_c=k_c, o_c=o_c, lse_c=lse_c, mix=mix, att=three(att), x1=x1, h2=h2, up=up, a=a, yd=three(yd))
    return three(x2), saved


def _layer_bwd(dx2, sv, md, W, tabs, bias, tag, flow=_NoFlow()):
    Bl, S, D = dx2.shape
    T = Bl * S
    n = lambda s: f"{s}_{tag}"
    two = lambda a: a.reshape(T, a.shape[-1])
    three = lambda a: a.reshape(Bl, S, a.shape[-1])
    g = {}
    dyb, dgate2 = gate_bwd(dx2, sv["yd"], md["gate2"], name=n("gate2_bwd"))
    da = three(matmul(two(dyb), W["w_down"], tb=True, tn=1408, name=n("down_dx")))
    g["w_down"] = matmul(two(sv["a"]), two(dyb), ta=True, tm=256, tn=1024, name=n("down_dw"))
    dup, dcw = conv_gate_bwd(sv["up"], W["conv_w"], W["conv_b"], da, name=n("conv_gate_bwd"))
    dh2 = three(matmul(two(dup), W["w_up"], tb=True, tn=1024, name=n("up_dx")))
    g["w_up"] = matmul(two(sv["h2"]), two(dup), ta=True, tn=1408, name=n("up_dw"))
    dx1, dn2, dsc2, dsh2 = rms_bwd(sv["x1"], 0, D, dh2, W["n2"], md["scale2"], dx2, name=n("norm2_bwd"))
    dmo, dgate1 = gate_bwd(dx1, sv["att"], md["gate1"], name=n("gate1_bwd"))
    dmix = three(matmul(two(dmo), W["w_out"], tb=True, tn=1024, out_dtype=BF16, name=n("out_dx")))
    g["w_out"] = matmul(two(sv["mix"]), two(dmo), ta=True, tn=1024, name=n("out_dw"))
    proj = sv["proj"]
    for k in ("w_down", "w_up", "w_out"):
        flow.add((tag, k), g[k])
    (dq_a, dk_a, dv_a), got = sb_attn_bwd(proj, sv["rt_a"], dmix[:, :, 0:256], plans=flow.plans(n("sb_bwd")), name=n("sb_bwd"))
    flow.done(n("sb_bwd"), got)
    dq_m, dk_m, dv_b = mla_attn_bwd(sv["q_m"], sv["k_m"], sv["kvb"], 6, sv["o_b"], sv["lse_b"], dmix[:, :, 256:640], name=n("mla_bwd"))
    dqb, dqn = rope_norm_bwd(sv["qb"], 6, dq_m, W["qn_g"], tabs, name=n("q_rope_bwd"))
    dkn_x, dkn, dslab = rope_norm_bwd(sv["kvb"], 6, dk_m, W["kn_g"], tabs, (proj, P_SLAB // LANES), name=n("k_rope_bwd"))
    dkvb = _cat([dkn_x, dv_b]).astype(BF16)
    dckvn = three(matmul(two(dkvb), W["w_ukv"], tb=True, tm=1024, name=n("ukv_dx")))
    g["w_ukv"] = matmul(two(sv["ckvn"]), two(dkvb), ta=True, tn=1152, name=n("ukv_dw"))
    dcqn = three(matmul(two(dqb), W["w_uq"], tb=True, tm=1024, name=n("uq_dx")))
    g["w_uq"] = matmul(two(sv["cqn"]), two(dqb), ta=True, tn=768, name=n("uq_dw"))
    dcq, dcq_g = rms_bwd(proj, P_CQ // 256, 256, dcqn, W["cq_g"], name=n("cq_norm_bwd"))
    dckv, dckv_g = rms_bwd(proj, P_CKV // LANES, LANES, dckvn, W["ckv_g"], name=n("ckv_norm_bwd"))
    dq_c, dk_c, dv_c, dbias, dsink = swa_attn_bwd(sv["q_c"], sv["k_c"], proj, bias, W["sinks"], sv["o_c"], sv["lse_c"], dmix[:, :, 640:1024], name=n("swa_bwd"))
    dswq, dswq_g = pair_rms_bwd(proj, P_SWQ // LANES, 3, dq_c, W["swq_g"], name=n("swq_norm_bwd"))
    dswk, dswk_g = pair_rms_bwd(proj, P_SWK // LANES, 1, dk_c, W["swk_g"], name=n("swk_norm_bwd"))
    dproj = _cat([dq_a, dk_a, dv_a, dcq, dckv, dslab, dswq, dswk, dv_c]).astype(BF16)
    dh = three(matmul(two(dproj), W["w_in"], tb=True, tn=1024, name=n("in_dx")))
    g["w_in"] = matmul(two(sv["h"]), two(dproj), ta=True, tn=1920, tk=2048, name=n("in_dw"))
    dx, dn1, dsc1, dsh1 = rms_bwd(sv["x"], 0, D, dh, W["n1"], md["scale1"], dx1, name=n("norm1_bwd"))
    small = dict(n1=dn1, n2=dn2, cq_g=dcq_g, ckv_g=dckv_g, qn_g=dqn, kn_g=dkn, swq_g=dswq_g, swk_g=dswk_g, conv=dcw)
    dmods = _cat([dsh1, dsc1, dgate1, dsh2, dsc2, dgate2]).reshape(Bl, 6 * D)
    for k in ("w_ukv", "w_uq", "w_in"):
        flow.add((tag, k), g[k])
    return dx, g, small, dmods, dbias, dsink


BIG = ("w_in", "w_uq", "w_ukv", "w_out", "w_up", "w_down")
ROW_SHARDED = ("w_out", "w_down")
PREP = dict(w_in=_prep_w_in, w_uq=_prep_w_uq, w_ukv=_prep_w_ukv, w_out=_prep_w_out, w_up=_up_perm, w_down=lambda w: w)
UNPREP = dict(w_in=_unprep_w_in, w_uq=_unprep_w_uq, w_ukv=_unprep_w_ukv, w_out=_unprep_w_out, w_up=_up_perm, w_down=lambda w: w)
NCHIPS = 4


def _local_step(x, target, positions, mods, Wl, rel_flat, fwd_flow=_NoFlow(), bwd_flow=_NoFlow()):
    Bl, S, D = x.shape
    L = len(Wl)
    tabs = _rope_tables(positions)
    bucket = _bucket_table()
    bias = swa_bias(rel_flat, bucket, name="swa_bias")
    mds = []
    for l in range(L):
        parts = [mods[l, :, D * k:D * (k + 1)].reshape(Bl, 1, D) for k in range(6)]
        mds.append(dict(zip(("shift1", "scale1", "gate1", "shift2", "scale2", "gate2"), parts)))
    saved = []
    h = x
    for l in range(L):
        h, sv = _layer_fwd(h, mds[l], Wl[l], tabs, bias, f"l{l}", fwd_flow)
        saved.append(sv)
    dy, loss = loss_grad(h, target, name="loss")
    grads, smalls, dmods, dbiases, dsinks = [None] * L, [None] * L, [None] * L, [None] * L, [None] * L
    for l in reversed(range(L)):
        dy, grads[l], smalls[l], dmods[l], dbiases[l], dsinks[l] = _layer_bwd(dy, saved[l], mds[l], Wl[l], tabs, bias, f"l{l}", bwd_flow)
    drel = swa_bias_bwd(_cat(dbiases, axis=0), bucket, name="swa_bias_bwd")
    return loss, dy, grads, smalls, dmods, dsinks, drel


ATT = ("w_in", "w_uq", "w_ukv", "w_out")
FFN = ("w_up", "w_down")
GATHER_STAGES = {
    "sb_fwd_l0": ([("l0", k) for k in FFN], []),
    "mla_fwd_l0": ([("l1", k) for k in ATT + ("w_up",)], [("l0", k) for k in FFN]),
    "swa_fwd_l0": ([("l1", "w_down")], [("l1", k) for k in ATT + ("w_up",)]),
    "sb_fwd_l1": ([], [("l1", "w_down")]),
}
SCATTER_STAGES = {
    "sb_bwd_l1": [("l1", k) for k in FFN],
    "sb_bwd_l0": [("l1", k) for k in ATT] + [("l0", k) for k in FFN],
}


class _GatherFlow:
    def __init__(self, shards, chip):
        self.shards, self.chip, self.ici, self.d2d, self.pending = shards, chip, {}, {}, {}

    def early(self, keys):
        ici, = run_plans([plan_gather_ici([self.shards[k] for k in keys])], name="gather_early_ici")
        d2d, = run_plans([plan_gather_d2d(ici)], name="gather_early_d2d")
        self.d2d.update(zip(keys, d2d))

    def plans(self, tag):
        ici_keys, d2d_keys = GATHER_STAGES.get(tag, ([], []))
        plans = []
        if d2d_keys:
            plans.append(plan_gather_d2d([self.ici[k] for k in d2d_keys]))
        if ici_keys:
            plans.append(plan_gather_ici([self.shards[k] for k in ici_keys]))
        self.pending[tag] = (ici_keys, d2d_keys)
        return plans

    def done(self, tag, outs):
        ici_keys, d2d_keys = self.pending.pop(tag, ([], []))
        outs = list(outs)
        if d2d_keys:
            self.d2d.update(zip(d2d_keys, outs.pop(0)))
        if ici_keys:
            self.ici.update(zip(ici_keys, outs.pop(0)))

    def weight(self, key):
        k = key[1]
        own = self.shards[key]
        r, cc = own.shape
        w4 = lax.dynamic_update_slice(self.d2d[key], own[None], (self.chip, 0, 0))
        fw = w4.reshape(NCHIPS * r, cc) if k in ROW_SHARDED else jnp.transpose(w4, (1, 0, 2)).reshape(r, NCHIPS * cc)
        return PREP[k](fw)


class _LayerWeights(dict):
    def __init__(self, small, flow, tag):
        super().__init__(small)
        self.flow, self.tag = flow, tag

    def __missing__(self, k):
        self[k] = self.flow.weight((self.tag, k))
        return self[k]


class _ScatterFlow:
    def __init__(self, shapes, sel, c_arr):
        self.shapes, self.sel, self.c_arr = shapes, sel, c_arr
        self.g, self.pairs, self.landed, self.pending = {}, {}, {}, {}

    def add(self, key, g):
        self.g[key] = g

    def _pairs(self, keys, label):
        g4s = []
        for key in keys:
            k = key[1]
            r, cc = self.shapes[k]
            gk = UNPREP[k](self.g[key])
            g4 = gk.reshape(NCHIPS, r, cc) if k in ROW_SHARDED else jnp.transpose(gk.reshape(r, NCHIPS, cc), (1, 0, 2))
            g4s.append(g4.astype(BF16))
        theirs, = run_plans([plan_swap_halves(g4s)], name=f"rs_swap_{label}")
        pairs = [pair_add_half(g4, th, self.c_arr, name=f"rs_pair_add_{key[1]}_{key[0]}") for key, g4, th in zip(keys, g4s, theirs)]
        self.pairs.update(zip(keys, pairs))
        return pairs

    def plans(self, tag):
        keys = SCATTER_STAGES.get(tag, [])
        self.pending[tag] = keys
        return [plan_scatter_ici(self._pairs(keys, tag))] if keys else []

    def done(self, tag, outs):
        keys = self.pending.pop(tag, [])
        if keys:
            self.landed.update(zip(keys, outs[0]))

    def finish(self):
        rest = [key for key in self.g if key not in self.pairs]
        if rest:
            landed, = run_plans([plan_scatter_ici(self._pairs(rest, "rest"))], name="rs_scatter_rest")
            self.landed.update(zip(rest, landed))
        keys = list(self.pairs)
        fulls = [chip_sum_into(self.landed[key], self.pairs[key], self.sel, name=f"rs_chip_sum_{key[1]}_{key[0]}") for key in keys]
        joined, = run_plans([plan_join_halves(fulls)], name="rs_join_halves")
        return dict(zip(keys, joined))


WEIGHTS = ("rel_table", "norm1_g", "norm2_g", "w_ada", "b_ada", "w_in", "mla_cq_g", "w_uq", "mla_ckv_g", "w_ukv", "mla_qn_g", "mla_kn_g",
           "sw_qn_g", "sw_kn_g", "sw_sinks", "w_out", "w_up", "conv_w", "conv_b", "w_down")
SMALL = tuple(n for n in WEIGHTS if n not in BIG + ("w_ada",))


def kernel(x, c, positions, rel_table, norm1_g, norm2_g, w_ada, b_ada, w_in, mla_cq_g, w_uq, mla_ckv_g, w_ukv, mla_qn_g, mla_kn_g, sw_qn_g, sw_kn_g, sw_sinks, w_out, w_up, conv_w, conv_b, w_down, loss_target, m_rel_table, m_norm1_g, m_norm2_g, m_w_ada, m_b_ada, m_w_in, m_mla_cq_g, m_w_uq, m_mla_ckv_g, m_w_ukv, m_mla_qn_g, m_mla_kn_g, m_sw_qn_g, m_sw_kn_g, m_sw_sinks, m_w_out, m_w_up, m_conv_w, m_conv_b, m_w_down, v_rel_table, v_norm1_g, v_norm2_g, v_w_ada, v_b_ada, v_w_in, v_mla_cq_g, v_w_uq, v_mla_ckv_g, v_w_ukv, v_mla_qn_g, v_mla_kn_g, v_sw_qn_g, v_sw_kn_g, v_sw_sinks, v_w_out, v_w_up, v_conv_w, v_conv_b, v_w_down):
    w = dict(rel_table=rel_table, norm1_g=norm1_g, norm2_g=norm2_g, w_ada=w_ada, b_ada=b_ada, w_in=w_in, mla_cq_g=mla_cq_g, w_uq=w_uq,
             mla_ckv_g=mla_ckv_g, w_ukv=w_ukv, mla_qn_g=mla_qn_g, mla_kn_g=mla_kn_g, sw_qn_g=sw_qn_g, sw_kn_g=sw_kn_g, sw_sinks=sw_sinks,
             w_out=w_out, w_up=w_up, conv_w=conv_w, conv_b=conv_b, w_down=w_down)
    m = dict(rel_table=m_rel_table, norm1_g=m_norm1_g, norm2_g=m_norm2_g, w_ada=m_w_ada, b_ada=m_b_ada, w_in=m_w_in, mla_cq_g=m_mla_cq_g,
             w_uq=m_w_uq, mla_ckv_g=m_mla_ckv_g, w_ukv=m_w_ukv, mla_qn_g=m_mla_qn_g, mla_kn_g=m_mla_kn_g, sw_qn_g=m_sw_qn_g,
             sw_kn_g=m_sw_kn_g, sw_sinks=m_sw_sinks, w_out=m_w_out, w_up=m_w_up, conv_w=m_conv_w, conv_b=m_conv_b, w_down=m_w_down)
    v = dict(rel_table=v_rel_table, norm1_g=v_norm1_g, norm2_g=v_norm2_g, w_ada=v_w_ada, b_ada=v_b_ada, w_in=v_w_in, mla_cq_g=v_mla_cq_g,
             w_uq=v_w_uq, mla_ckv_g=v_mla_ckv_g, w_ukv=v_w_ukv, mla_qn_g=v_mla_qn_g, mla_kn_g=v_mla_kn_g, sw_qn_g=v_sw_qn_g,
             sw_kn_g=v_sw_kn_g, sw_sinks=v_sw_sinks, w_out=v_w_out, w_up=v_w_up, conv_w=v_conv_w, conv_b=v_conv_b, w_down=v_w_down)
    Bl, S, D = x.shape
    L = norm1_g.shape[0]
    xi, yi, ci = _me()
    chip = 2 * xi + yi
    dev = 4 * xi + 2 * yi + ci
    ndev = 2 * NCHIPS

    shapes = {k: w[k].shape[1:] for k in BIG}
    shards = {(f"l{l}", k): w[k][l].astype(BF16) for l in range(L) for k in BIG}
    gflow = _GatherFlow(shards, chip)
    gflow.early([("l0", k) for k in ATT])

    cw_cols = conv_w.shape[2]
    c_got, cw_got = allgather8([c, conv_w.reshape(L * 3, cw_cols)], name="gather_cond")
    c_all = c_got.reshape(ndev * Bl, D)
    conv_full = jnp.transpose(cw_got[0::2].reshape(NCHIPS, L, 3, cw_cols), (1, 2, 0, 3)).reshape(L, 3, NCHIPS * cw_cols)
    E = w_ada.shape[2]
    b_cols = lax.dynamic_slice(b_ada, (0, chip * E), (L, E)).reshape(L, 1, E)
    mods_cols = mods_matmul(c_all, w_ada, b_cols, name="mods")
    mods_all, = allgather8([mods_cols.reshape(L * ndev * Bl, E)], name="gather_mods")
    mods_all = jnp.transpose(mods_all[0::2].reshape(NCHIPS, L, ndev * Bl, E), (1, 2, 0, 3)).reshape(L, ndev * Bl, NCHIPS * E)
    mods = lax.dynamic_slice(mods_all, (0, dev * Bl, 0), (L, Bl, NCHIPS * E))

    Wl = []
    for l in range(L):
        Wd = _small_params({k: w[k][l] for k in SMALL if k not in ("rel_table", "b_ada", "conv_w")})
        Wd["conv_w"] = _up_perm(conv_full[l])
        Wl.append(_LayerWeights(Wd, gflow, f"l{l}"))

    sflow = _ScatterFlow(shapes, jnp.stack([chip, ci]).astype(jnp.int32), ci.reshape(1).astype(jnp.int32))
    loss, dx, _, smalls, dmods, dsinks, drel = _local_step(x, loss_target, positions, mods, Wl, rel_table.reshape(-1), gflow, sflow)
    reduced = sflow.finish()
    grad = {k: jnp.stack([reduced[(f"l{l}", k)] for l in range(L)]) for k in BIG}

    vec_names = ("n1", "n2", "cq_g", "ckv_g", "qn_g", "kn_g", "swq_g", "swk_g")
    vecs = _cat([_cat([smalls[l][k] for k in vec_names], axis=1) for l in range(L)], axis=0)
    convs = _cat([smalls[l]["conv"][0:4] for l in range(L)], axis=0)
    dm = jnp.stack(dmods, axis=1).reshape(Bl * L, 6 * D)
    dsk = jnp.stack(dsinks, axis=1).reshape(Bl * L * 6, LANES)
    got = allgather8([vecs, convs, drel, loss, dm, dsk], name="gather_small_grads")
    seq = lambda a, rows: a.reshape(ndev * Bl, rows, a.shape[-1])
    vec_s, conv_s, rel_s, loss_s, dm_s, dsk_s = sum_small(list(got[:4]) + [seq(got[4], L), seq(got[5], L * 6)], name="sum_small_grads")
    dm_all = jnp.transpose(seq(got[4], L), (1, 0, 2))
    grad["w_ada"] = ada_grad(c_all, lax.dynamic_slice(dm_all, (0, 0, chip * E), (L, ndev * Bl, E)), name="ada_grad")
    grad["b_ada"] = dm_s
    grad["sw_sinks"] = jnp.transpose(dsk_s.reshape(L, 3, 2, LANES)[:, :, :, 0], (0, 2, 1)).reshape(L, 6)
    grad["rel_table"] = rel_s[:6, :REL_BUCKETS].T
    off = 0
    for k, name_, keep in zip(vec_names, ("norm1_g", "norm2_g", "mla_cq_g", "mla_ckv_g", "mla_qn_g", "mla_kn_g", "sw_qn_g", "sw_kn_g"),
                              (D, D, 256, LANES, MLA_QK, MLA_QK, HEAD, HEAD)):
        grad[name_] = vec_s[:, off:off + keep]
        off += smalls[0][k].shape[1]
    conv = _up_perm(conv_s.reshape(L, 4, 2 * D_FF))
    grad["conv_w"] = lax.dynamic_slice(conv[:, 0:3], (0, 0, chip * cw_cols), (L, 3, cw_cols))
    grad["conv_b"] = conv[:, 3]
    loss_out = loss_s[0, 0]

    delta, new_m, new_v = {}, {}, {}
    for k in BIG + ("w_ada",):
        delta[k], new_m[k], new_v[k] = adamw(w[k], grad[k], m[k], v[k], name=f"adamw_{k}")
    outs = adamw_small(*[[src[k] for k in SMALL] for src in (w, grad, m, v)], name="adamw_small")
    for dst, o in zip((delta, new_m, new_v), outs):
        dst.update(dict(zip(SMALL, o)))
    return (loss_out, dx, *[grad[k] for k in WEIGHTS], *[delta[k] for k in WEIGHTS], *[new_m[k] for k in WEIGHTS], *[new_v[k] for k in WEIGHTS])
```

```python
import functools
import math

import jax
import jax.numpy as jnp
from jax import lax
from jax.experimental import pallas as pl
from jax.experimental.pallas import tpu as pltpu

F32 = jnp.float32
BF16 = jnp.bfloat16
MESH = pl.DeviceIdType.MESH

EPS = 1e-6
NEG = -1e30
HEAD = 64
LANES = 128
MLA_QK = 96
ROPE_THETA = 10000.0
REL_BUCKETS = 32
REL_MAX_DIST = 128
WINDOW = 128
D_FF = 2816
ADAM_LR, ADAM_B1, ADAM_B2, ADAM_EPS, ADAM_WD, ADAM_STEP = 0.001, 0.9, 0.999, 1e-08, 0.01, 10

VMEM_LIMIT = 56 * 1024 * 1024
STRIP = 32
ROW_STRIP = 64
P_SBQ, P_SBK, P_SBV, P_CQ, P_CKV, P_SLAB, P_SWQ, P_SWK, P_SWV, P_END = 0, 256, 512, 768, 1024, 1152, 1280, 1664, 1792, 1920
SW_PERM = (0, 3, 1, 4, 2, 5)


def _cp(*sem):
    return pltpu.CompilerParams(dimension_semantics=sem, vmem_limit_bytes=VMEM_LIMIT)


def _dot(a, b):
    return jnp.dot(a, b, preferred_element_type=F32)


def _dot_nt(a, b):
    return lax.dot_general(a, b, (((1,), (1,)), ((), ())), preferred_element_type=F32)


def _dot_tn(a, b):
    return lax.dot_general(a, b, (((0,), (0,)), ((), ())), preferred_element_type=F32)


def _split_dot(x, u):
    hi = x.astype(BF16)
    lo = (x - hi.astype(F32)).astype(BF16)
    return _dot(hi, u) + _dot(lo, u)


def _lane_masks():
    lane = lax.broadcasted_iota(jnp.int32, (1, LANES), 1)
    return (lane < HEAD, lane >= HEAD)


def _tile(n, cap, align=128):
    if n <= cap:
        return n
    t = cap - cap % align
    while t >= align:
        if n % t == 0:
            return t
        t -= align
    return n


def matmul(a, b, *, ta=False, tb=False, out_dtype=F32, tm=512, tn=512, tk=8192, name):
    M, K = (a.shape[1], a.shape[0]) if ta else a.shape
    N = b.shape[0] if tb else b.shape[1]
    tm, tn, tk = _tile(M, tm), _tile(N, tn), _tile(K, tk)
    nk = K // tk

    def body(a_ref, b_ref, o_ref, *scratch):
        av = a_ref[...].astype(BF16)
        bv = b_ref[...].astype(BF16)
        if ta:
            part = _dot_tn(av, bv)
        elif tb:
            part = _dot_nt(av, bv)
        else:
            part = _dot(av, bv)
        if nk == 1:
            o_ref[...] = part.astype(out_dtype)
        else:
            acc_ref, = scratch
            k = pl.program_id(2)

            @pl.when(k == 0)
            def _():
                acc_ref[...] = part

            @pl.when(k > 0)
            def _():
                acc_ref[...] += part

            @pl.when(k == nk - 1)
            def _():
                o_ref[...] = acc_ref[...].astype(out_dtype)

    n_outer = nk == 1 and tn * b.dtype.itemsize > tm * a.dtype.itemsize
    ij = (lambda p, q: (q, p)) if n_outer else (lambda p, q: (p, q))
    a_map = (lambda p, q, k: (k, ij(p, q)[0])) if ta else (lambda p, q, k: (ij(p, q)[0], k))
    b_map = (lambda p, q, k: (ij(p, q)[1], k)) if tb else (lambda p, q, k: (k, ij(p, q)[1]))
    grid = (N // tn, M // tm, nk) if n_outer else (M // tm, N // tn, nk)
    return pl.pallas_call(
        body, name=name, grid=grid,
        in_specs=[pl.BlockSpec((tk, tm) if ta else (tm, tk), a_map), pl.BlockSpec((tn, tk) if tb else (tk, tn), b_map)],
        out_specs=pl.BlockSpec((tm, tn), lambda p, q, k: ij(p, q)),
        out_shape=jax.ShapeDtypeStruct((M, N), out_dtype),
        scratch_shapes=[] if nk == 1 else [pltpu.VMEM((tm, tn), F32)],
        compiler_params=_cp("parallel", "parallel", "arbitrary"),
    )(a, b)


def matmul_res(a, b, res, gate, seq, *, tm=512, tn=1024, name):
    M, K = a.shape
    N = b.shape[1]
    tm, tn = _tile(min(M, seq), tm), _tile(N, tn)
    per_seq = seq // tm

    def body(a_ref, b_ref, r_ref, g_ref, y_ref, x_ref):
        y = _dot(a_ref[...].astype(BF16), b_ref[...].astype(BF16))
        y_ref[...] = y
        x_ref[...] = r_ref[...] + g_ref[...] * y

    out = jax.ShapeDtypeStruct((M, N), F32)
    return pl.pallas_call(
        body, name=name, grid=(M // tm, N // tn),
        in_specs=[pl.BlockSpec((tm, K), lambda i, j: (i, 0)), pl.BlockSpec((K, tn), lambda i, j: (0, j)),
                  pl.BlockSpec((tm, tn), lambda i, j: (i, j)), pl.BlockSpec((None, 1, tn), lambda i, j: (lax.div(i, jnp.int32(per_seq)), 0, j))],
        out_specs=[pl.BlockSpec((tm, tn), lambda i, j: (i, j))] * 2,
        out_shape=[out, out], compiler_params=_cp("parallel", "parallel"),
    )(a, b, res, gate)


def rms_fwd(x3, blk, W, g, sc=None, sh=None, *, tm=512, name):
    Bl, S, _ = x3.shape
    tm = min(tm, S)
    mod = sc is not None

    def body(x_ref, g_ref, *rest):
        o_ref = rest[-1]
        x = x_ref[...]
        r = lax.rsqrt(jnp.mean(x * x, axis=-1, keepdims=True) + EPS)
        y = x * r * g_ref[...]
        if mod:
            y = y * (1.0 + rest[0][...]) + rest[1][...]
        o_ref[...] = y.astype(BF16)

    vec = pl.BlockSpec((None, 1, W), lambda b, s: (b, 0, 0))
    return pl.pallas_call(
        body, name=name, grid=(Bl, S // tm),
        in_specs=[pl.BlockSpec((None, tm, W), lambda b, s: (b, s, blk)), pl.BlockSpec((1, W), lambda b, s: (0, 0))] + ([vec, vec] if mod else []),
        out_specs=pl.BlockSpec((None, tm, W), lambda b, s: (b, s, 0)),
        out_shape=jax.ShapeDtypeStruct((Bl, S, W), BF16),
        compiler_params=_cp("parallel", "parallel"),
    )(x3, g, *([sc, sh] if mod else []))


def rms_bwd(x3, blk, W, dy3, g, sc=None, dres3=None, *, tm=256, name):
    Bl, S, _ = x3.shape
    tm = min(tm, S)
    mod = sc is not None
    res = dres3 is not None

    def body(*refs):
        x_ref, dy_ref, g_ref = refs[:3]
        k = 3
        sc_ref = dr_ref = None
        if mod:
            sc_ref = refs[k]
            k += 1
        if res:
            dr_ref = refs[k]
            k += 1
        dx_ref, dg_ref = refs[k], refs[k + 1]
        b, s = pl.program_id(0), pl.program_id(1)
        x = x_ref[...]
        dy = dy_ref[...].astype(F32)
        g = g_ref[...]
        r = lax.rsqrt(jnp.mean(x * x, axis=-1, keepdims=True) + EPS)
        n = x * r
        if mod:
            dsc_ref, dsh_ref = refs[k + 2], refs[k + 3]
            one_sc = 1.0 + sc_ref[...]

            @pl.when(s == 0)
            def _():
                dsc_ref[...] = jnp.zeros_like(dsc_ref)
                dsh_ref[...] = jnp.zeros_like(dsh_ref)

            dsh_ref[...] += jnp.sum(dy, axis=0, keepdims=True)
            dsc_ref[...] += jnp.sum(dy * n * g, axis=0, keepdims=True)
            dyn = dy * one_sc
        else:
            dyn = dy

        @pl.when((b == 0) & (s == 0))
        def _():
            dg_ref[...] = jnp.zeros_like(dg_ref)

        dg_ref[...] += jnp.sum(dyn * n, axis=0, keepdims=True)
        dn = dyn * g
        dx = r * (dn - n * jnp.mean(dn * n, axis=-1, keepdims=True))
        if res:
            dx = dx + dr_ref[...]
        dx_ref[...] = dx

    blkspec = pl.BlockSpec((None, tm, W), lambda b, s: (b, s, 0))
    vec = pl.BlockSpec((None, 1, W), lambda b, s: (b, 0, 0))
    row = pl.BlockSpec((1, W), lambda b, s: (0, 0))
    in_specs = [pl.BlockSpec((None, tm, W), lambda b, s: (b, s, blk)), blkspec, row] + ([vec] if mod else []) + ([blkspec] if res else [])
    out_specs = [blkspec, row] + ([vec, vec] if mod else [])
    out_shape = [jax.ShapeDtypeStruct((Bl, S, W), F32), jax.ShapeDtypeStruct((1, W), F32)]
    if mod:
        out_shape += [jax.ShapeDtypeStruct((Bl, 1, W), F32)] * 2
    args = [x3, dy3, g] + ([sc] if mod else []) + ([dres3] if res else [])
    return pl.pallas_call(
        body, name=name, grid=(Bl, S // tm), in_specs=in_specs, out_specs=out_specs, out_shape=out_shape,
        compiler_params=_cp("arbitrary", "arbitrary"),
    )(*args)


def pair_rms_fwd(x3, blk0, npairs, g2, *, tm=1024, name):
    Bl, S, _ = x3.shape
    tm = min(tm, S)

    def body(x_ref, g_ref, o_ref):
        lo, hi = _lane_masks()
        x = x_ref[...]
        xx = x * x
        s0 = jnp.sum(jnp.where(lo, xx, 0.0), axis=-1, keepdims=True)
        s1 = jnp.sum(jnp.where(hi, xx, 0.0), axis=-1, keepdims=True)
        r = jnp.where(lo, lax.rsqrt(s0 / HEAD + EPS), lax.rsqrt(s1 / HEAD + EPS))
        o_ref[...] = (x * r * g_ref[...]).astype(BF16)

    return pl.pallas_call(
        body, name=name, grid=(Bl, S // tm, npairs),
        in_specs=[pl.BlockSpec((None, tm, LANES), lambda b, s, p: (b, s, blk0 + p)), pl.BlockSpec((1, LANES), lambda b, s, p: (0, 0))],
        out_specs=pl.BlockSpec((None, tm, LANES), lambda b, s, p: (b, s, p)),
        out_shape=jax.ShapeDtypeStruct((Bl, S, LANES * npairs), BF16),
        compiler_params=_cp("parallel", "parallel", "parallel"),
    )(x3, g2)


def pair_rms_bwd(x3, blk0, npairs, dy3, g2, *, tm=1024, name):
    Bl, S, _ = x3.shape
    tm = min(tm, S)

    def body(x_ref, dy_ref, g_ref, dx_ref, dg_ref):
        lo, hi = _lane_masks()
        first = (pl.program_id(0) == 0) & (pl.program_id(1) == 0) & (pl.program_id(2) == 0)
        x = x_ref[...]
        dy = dy_ref[...]
        xx = x * x
        s0 = jnp.sum(jnp.where(lo, xx, 0.0), axis=-1, keepdims=True)
        s1 = jnp.sum(jnp.where(hi, xx, 0.0), axis=-1, keepdims=True)
        r = jnp.where(lo, lax.rsqrt(s0 / HEAD + EPS), lax.rsqrt(s1 / HEAD + EPS))
        n = x * r

        @pl.when(first)
        def _():
            dg_ref[...] = jnp.zeros_like(dg_ref)

        part = jnp.sum(dy * n, axis=0, keepdims=True)
        dg_ref[...] += part + pltpu.roll(part, HEAD, 1)
        dn = dy * g_ref[...]
        t = dn * n
        m0 = jnp.sum(jnp.where(lo, t, 0.0), axis=-1, keepdims=True)
        m1 = jnp.sum(jnp.where(hi, t, 0.0), axis=-1, keepdims=True)
        dx_ref[...] = r * (dn - n * (jnp.where(lo, m0, m1) / HEAD))

    return pl.pallas_call(
        body, name=name, grid=(Bl, S // tm, npairs),
        in_specs=[pl.BlockSpec((None, tm, LANES), lambda b, s, p: (b, s, blk0 + p)), pl.BlockSpec((None, tm, LANES), lambda b, s, p: (b, s, p)),
                  pl.BlockSpec((1, LANES), lambda b, s, p: (0, 0))],
        out_specs=[pl.BlockSpec((None, tm, LANES), lambda b, s, p: (b, s, p)), pl.BlockSpec((1, LANES), lambda b, s, p: (0, 0))],
        out_shape=[jax.ShapeDtypeStruct((Bl, S, LANES * npairs), F32), jax.ShapeDtypeStruct((1, LANES), F32)],
        compiler_params=_cp("arbitrary", "arbitrary", "arbitrary"),
    )(x3, dy3, g2)


def _rot(y, cos_t, sin_a, sin_b):
    return y * cos_t + pltpu.roll(y, LANES - 16, 1) * sin_a + pltpu.roll(y, 16, 1) * sin_b


def _rot_t(d, cos_t, sin_a, sin_b):
    return d * cos_t + pltpu.roll(d * sin_a, 16, 1) + pltpu.roll(d * sin_b, LANES - 16, 1)


def rope_norm_fwd(x3, nheads, g, tabs, slab=None, *, tm=1024, name):
    Bl, S, _ = x3.shape
    tm = min(tm, S)
    has_slab = slab is not None

    def body(*refs):
        x_ref, g_ref, c_ref, sa_ref, sb_ref = refs[:5]
        o_ref = refs[-1]
        g = g_ref[...]
        for r0 in range(0, tm, ROW_STRIP):
            rows = slice(r0, r0 + ROW_STRIP)
            x = x_ref[rows, :]
            if has_slab:
                x = x + refs[5][rows, :]
            r = lax.rsqrt(jnp.sum(x * x, axis=-1, keepdims=True) / MLA_QK + EPS)
            o_ref[rows, :] = _rot(x * r * g, c_ref[rows, :], sa_ref[rows, :], sb_ref[rows, :]).astype(BF16)

    head = pl.BlockSpec((None, tm, LANES), lambda b, s, h: (b, s, h))
    tab = pl.BlockSpec((None, tm, LANES), lambda b, s, h: (b, s, 0))
    in_specs = [head, pl.BlockSpec((1, LANES), lambda b, s, h: (0, 0)), tab, tab, tab]
    args = [x3, g, *tabs]
    if has_slab:
        sblk = slab[1]
        in_specs.append(pl.BlockSpec((None, tm, LANES), lambda b, s, h: (b, s, sblk)))
        args.append(slab[0])
    return pl.pallas_call(
        body, name=name, grid=(Bl, S // tm, nheads), in_specs=in_specs, out_specs=head,
        out_shape=jax.ShapeDtypeStruct((Bl, S, LANES * nheads), BF16),
        compiler_params=_cp("parallel", "parallel", "parallel"),
    )(*args)


def rope_norm_bwd(x3, nheads, dy3, g, tabs, slab=None, *, tm=1024, name):
    Bl, S, _ = x3.shape
    tm = min(tm, S)
    has_slab = slab is not None

    def body(*refs):
        x_ref, dy_ref, g_ref, c_ref, sa_ref, sb_ref = refs[:6]
        k = 7 if has_slab else 6
        dx_ref, dg_ref = refs[k], refs[k + 1]
        h = pl.program_id(2)
        first = (pl.program_id(0) == 0) & (pl.program_id(1) == 0) & (h == 0)
        g = g_ref[...]
        ds_ref = refs[k + 2] if has_slab else None

        @pl.when(first)
        def _():
            dg_ref[...] = jnp.zeros_like(dg_ref)

        if has_slab:
            @pl.when(h == 0)
            def _():
                ds_ref[...] = jnp.zeros_like(ds_ref)

        dg_acc = jnp.zeros((1, LANES), F32)
        for r0 in range(0, tm, ROW_STRIP):
            rows = slice(r0, r0 + ROW_STRIP)
            x = x_ref[rows, :]
            if has_slab:
                x = x + refs[6][rows, :]
            r = lax.rsqrt(jnp.sum(x * x, axis=-1, keepdims=True) / MLA_QK + EPS)
            n = x * r
            d = _rot_t(dy_ref[rows, :], c_ref[rows, :], sa_ref[rows, :], sb_ref[rows, :])
            dg_acc = dg_acc + jnp.sum(d * n, axis=0, keepdims=True)
            dn = d * g
            dx = r * (dn - n * (jnp.sum(dn * n, axis=-1, keepdims=True) / MLA_QK))
            dx_ref[rows, :] = dx
            if has_slab:
                ds_ref[rows, :] += dx
        dg_ref[...] += dg_acc

    head = pl.BlockSpec((None, tm, LANES), lambda b, s, h: (b, s, h))
    tab = pl.BlockSpec((None, tm, LANES), lambda b, s, h: (b, s, 0))
    row = pl.BlockSpec((1, LANES), lambda b, s, h: (0, 0))
    in_specs = [head, head, row, tab, tab, tab]
    args = [x3, dy3, g, *tabs]
    out_specs = [head, row]
    out_shape = [jax.ShapeDtypeStruct((Bl, S, LANES * nheads), F32), jax.ShapeDtypeStruct((1, LANES), F32)]
    if has_slab:
        sblk = slab[1]
        in_specs.append(pl.BlockSpec((None, tm, LANES), lambda b, s, h: (b, s, sblk)))
        args.append(slab[0])
        out_specs.append(tab)
        out_shape.append(jax.ShapeDtypeStruct((Bl, S, LANES), F32))
    return pl.pallas_call(
        body, name=name, grid=(Bl, S // tm, nheads), in_specs=in_specs, out_specs=out_specs, out_shape=out_shape,
        compiler_params=_cp("arbitrary", "arbitrary", "arbitrary"),
    )(*args)


def _sb_tile(z, strict, u, carry_r):
    sp = jnp.maximum(z, 0.0) + jnp.log(1.0 + jnp.exp(-jnp.abs(z)))
    keep = jnp.where(strict, -sp, 0.0)
    logw = (z - sp) + _split_dot(keep, u) + carry_r
    return jnp.where(strict, jnp.exp(logw), 0.0), keep, sp


SB_BLOCK = 256
SB_QBLOCK = 512


def sb_attn_fwd(proj3, *, plans=None, name):
    Bl, S, _ = proj3.shape
    tk = min(SB_BLOCK, S)
    tq = min(SB_QBLOCK, S)
    per_q = tq // tk
    scale = HEAD ** -0.5
    qb, kb0, vb0 = P_SBQ // LANES, P_SBK // LANES, P_SBV // LANES

    def body(q_ref, k_ref, v_ref, o_ref, rt_ref):
        i = pl.program_id(2)
        masks = _lane_masks()
        lane = lax.broadcasted_iota(jnp.int32, (1, LANES), 1)
        q = q_ref[...]
        qh = [jnp.where(m, q, 0.0).astype(BF16) for m in masks]
        rr = lax.broadcasted_iota(jnp.int32, (tq, tk), 0)
        cc = lax.broadcasted_iota(jnp.int32, (tq, tk), 1)
        u = (lax.broadcasted_iota(jnp.int32, (tk, tk), 0) > lax.broadcasted_iota(jnp.int32, (tk, tk), 1)).astype(BF16)

        rt_ref[...] = jnp.zeros_like(rt_ref)

        def step(t, carry):
            r0, r1, acc = carry
            j = (i + 1) * per_q - 1 - t
            off = pl.multiple_of(j * tk, tk)
            kb = k_ref[pl.ds(off, tk), :].astype(BF16)
            vb = v_ref[pl.ds(off, tk), :]
            strict = (cc + j * tk) < (rr + i * tq)
            rt_ref[...] = jnp.where(lane == j, r0, jnp.where(lane == j + HEAD, r1, rt_ref[...]))
            rs = [r0, r1]
            for h in range(2):
                z = _dot_nt(qh[h], kb) * scale
                w, keep, _ = _sb_tile(z, strict, u, rs[h])
                acc = acc + _dot(w.astype(BF16), jnp.where(masks[h], vb, 0.0).astype(BF16))
                rs[h] = rs[h] + jnp.sum(keep, axis=1, keepdims=True)
            return rs[0], rs[1], acc

        zero = jnp.zeros((tq, 1), F32)
        _, _, acc = lax.fori_loop(0, (i + 1) * per_q, step, (zero, zero, jnp.zeros((tq, LANES), F32)))
        o_ref[...] = acc

    seq = lambda blk0: pl.BlockSpec((None, S, LANES), lambda b, p, i: (b, 0, blk0 + p))
    out = pl.BlockSpec((None, tq, LANES), lambda b, p, i: (b, i, p))
    shp = jax.ShapeDtypeStruct((Bl, S, 2 * LANES), F32)
    return call_with_plans(
        body, plans, name=name, grid=(Bl, 2, S // tq),
        in_specs=[pl.BlockSpec((None, tq, LANES), lambda b, p, i: (b, i, qb + p)), seq(kb0), seq(vb0)],
        out_specs=[out, out], out_shape=[shp, shp], scratch_shapes=[], args=[proj3, proj3, proj3],
        sem=("arbitrary",) * 3 if plans else ("parallel", "parallel", "arbitrary"))


def sb_attn_bwd(proj3, rt3, do3, *, plans=None, name):
    Bl, S, _ = proj3.shape
    tk = min(SB_BLOCK, S)
    tq = min(SB_QBLOCK, S)
    per_q = tq // tk
    scale = HEAD ** -0.5
    qb, kb0, vb0 = P_SBQ // LANES, P_SBK // LANES, P_SBV // LANES

    def body(q_ref, k_ref, v_ref, rt_ref, do_ref, dq_ref, dk_ref, dv_ref):
        i = pl.program_id(2)

        @pl.when(i == 0)
        def _():
            dk_ref[...] = jnp.zeros_like(dk_ref)
            dv_ref[...] = jnp.zeros_like(dv_ref)

        masks = _lane_masks()
        lane = lax.broadcasted_iota(jnp.int32, (1, LANES), 1)
        q = q_ref[...]
        qh = [jnp.where(m, q, 0.0).astype(BF16) for m in masks]
        do_b = do_ref[...].astype(BF16)
        doh = [jnp.where(m, do_b, jnp.zeros_like(do_b)) for m in masks]
        rt = rt_ref[...]
        rr = lax.broadcasted_iota(jnp.int32, (tq, tk), 0)
        cc = lax.broadcasted_iota(jnp.int32, (tq, tk), 1)
        ur = lax.broadcasted_iota(jnp.int32, (tk, tk), 0)
        uc = lax.broadcasted_iota(jnp.int32, (tk, tk), 1)
        u_suffix = (ur > uc).astype(BF16)
        u_prefix = (ur < uc).astype(BF16)

        def step(j, carry):
            p0, p1, dq = carry
            off = pl.multiple_of(j * tk, tk)
            kf = k_ref[pl.ds(off, tk), :]
            kb = kf.astype(BF16)
            vb = v_ref[pl.ds(off, tk), :]
            strict = (cc + j * tk) < (rr + i * tq)
            ps = [p0, p1]
            dk_acc = jnp.zeros((tk, LANES), F32)
            dv_acc = jnp.zeros((tk, LANES), F32)
            for h in range(2):
                r_j = jnp.sum(jnp.where(lane == j + h * HEAD, rt, 0.0), axis=1, keepdims=True)
                z = _dot_nt(qh[h], kb) * scale
                w, _, sp = _sb_tile(z, strict, u_suffix, r_j)
                vh = jnp.where(masks[h], vb, 0.0).astype(BF16)
                g = _dot_nt(doh[h], vh) * w
                pre = _split_dot(g, u_prefix) + ps[h]
                dz = jnp.where(strict, g * jnp.exp(-sp) - jnp.exp(z - sp) * pre, 0.0) * scale
                dzb = dz.astype(BF16)
                dq = dq + _dot(dzb, jnp.where(masks[h], kf, 0.0).astype(BF16))
                dk_acc = dk_acc + _dot_tn(dzb, qh[h])
                dv_acc = dv_acc + _dot_tn(w.astype(BF16), doh[h])
                ps[h] = ps[h] + jnp.sum(g, axis=1, keepdims=True)
            dk_ref[pl.ds(off, tk), :] += dk_acc
            dv_ref[pl.ds(off, tk), :] += dv_acc
            return ps[0], ps[1], dq

        zero = jnp.zeros((tq, 1), F32)
        out = lax.fori_loop(0, (i + 1) * per_q, step, (zero, zero, jnp.zeros((tq, LANES), F32)))
        dq_ref[...] = out[2]

    seq_in = lambda blk0: pl.BlockSpec((None, S, LANES), lambda b, p, i: (b, 0, blk0 + p))
    blk = pl.BlockSpec((None, tq, LANES), lambda b, p, i: (b, i, p))
    seq_out = pl.BlockSpec((None, S, LANES), lambda b, p, i: (b, 0, p))
    shp = jax.ShapeDtypeStruct((Bl, S, 2 * LANES), F32)
    return call_with_plans(
        body, plans, name=name, grid=(Bl, 2, S // tq),
        in_specs=[pl.BlockSpec((None, tq, LANES), lambda b, p, i: (b, i, qb + p)), seq_in(kb0), seq_in(vb0), blk, blk],
        out_specs=[blk, seq_out, seq_out], out_shape=[shp, shp, shp], scratch_shapes=[], args=[proj3, proj3, proj3, rt3, do3],
        sem=("arbitrary",) * 3 if plans else ("parallel", "parallel", "arbitrary"))


def mla_attn_fwd(q3, k3, kv3, vblk0, *, tq=512, tk=256, plans=None, name):
    Bl, S, _ = q3.shape
    tq = min(tq, S)
    tk = min(tk, tq)
    per_q = tq // tk
    scale = MLA_QK ** -0.5

    def body(q_ref, k_ref, v_ref, o_ref, lse_ref):
        i = pl.program_id(2)
        masks = _lane_masks()
        rr = lax.broadcasted_iota(jnp.int32, (tq, tk), 0)
        cc = lax.broadcasted_iota(jnp.int32, (tq, tk), 1)
        qh = [q_ref[:, h * LANES:(h + 1) * LANES] for h in range(2)]

        def step(j, carry):
            m0, l0, m1, l1, acc = carry
            off = pl.multiple_of(j * tk, tk)
            vb = v_ref[pl.ds(off, tk), :]
            causal = (cc + j * tk) <= (rr + i * tq)
            ms, ls, alphas = [m0, m1], [l0, l1], []
            add = jnp.zeros((tq, LANES), F32)
            for h in range(2):
                kh = k_ref[pl.ds(off, tk), h * LANES:(h + 1) * LANES]
                s = jnp.where(causal, _dot_nt(qh[h], kh) * scale, NEG)
                m_new = jnp.maximum(ms[h], jnp.max(s, axis=1, keepdims=True))
                p = jnp.exp(s - m_new)
                alpha = jnp.exp(ms[h] - m_new)
                ls[h] = alpha * ls[h] + jnp.sum(p, axis=1, keepdims=True)
                ms[h] = m_new
                alphas.append(alpha)
                add = add + _dot(p.astype(BF16), jnp.where(masks[h], vb, 0.0).astype(BF16))
            acc = acc * jnp.where(masks[0], alphas[0], alphas[1]) + add
            return ms[0], ls[0], ms[1], ls[1], acc

        neg = jnp.full((tq, 1), NEG, F32)
        zero = jnp.zeros((tq, 1), F32)
        m0, l0, m1, l1, acc = lax.fori_loop(0, (i + 1) * per_q, step, (neg, zero, neg, zero, jnp.zeros((tq, LANES), F32)))
        o_ref[...] = acc / jnp.where(masks[0], l0, l1)
        lse_ref[...] = jnp.where(masks[0], m0 + jnp.log(l0), m1 + jnp.log(l1))

    out = pl.BlockSpec((None, tq, LANES), lambda b, p, i: (b, i, p))
    shp = jax.ShapeDtypeStruct((Bl, S, 3 * LANES), F32)
    return call_with_plans(
        body, plans, name=name, grid=(Bl, 3, S // tq),
        in_specs=[pl.BlockSpec((None, tq, 2 * LANES), lambda b, p, i: (b, i, p)), pl.BlockSpec((None, S, 2 * LANES), lambda b, p, i: (b, 0, p)),
                  pl.BlockSpec((None, S, LANES), lambda b, p, i: (b, 0, vblk0 + p))],
        out_specs=[out, out], out_shape=[shp, shp], scratch_shapes=[], args=[q3, k3, kv3],
        sem=("arbitrary",) * 3 if plans else ("parallel", "parallel", "arbitrary"))


def mla_attn_bwd(q3, k3, kv3, vblk0, o3, lse3, do3, *, tq=512, tk=256, name):
    Bl, S, _ = q3.shape
    tq = min(tq, S)
    tk = min(tk, tq)
    per_q = tq // tk
    nq = S // tq
    scale = MLA_QK ** -0.5

    def body(q_ref, k_ref, v_ref, o_ref, lse_ref, do_ref, dq_ref, dk_ref, dv_ref, s_scr, dp_scr, p_scr, ds_scr):
        j = pl.program_id(2)

        @pl.when(j == 0)
        def _():
            dq_ref[...] = jnp.zeros_like(dq_ref)

        masks = _lane_masks()
        vb = v_ref[...]
        vh = [jnp.where(m, vb, 0.0).astype(BF16) for m in masks]
        kh = [k_ref[:, h * LANES:(h + 1) * LANES] for h in range(2)]
        i0 = lax.div(j, jnp.int32(per_q))

        def step(i, carry, masked):
            dk0, dk1, dv = carry
            off = pl.multiple_of(i * tq, tq)
            do_b = do_ref[pl.ds(off, tq), :].astype(BF16)
            prod = do_b.astype(F32) * o_ref[pl.ds(off, tq), :]
            lse = lse_ref[pl.ds(off, tq), :]
            dks = [dk0, dk1]
            for h in range(2):
                qh = q_ref[pl.ds(off, tq), h * LANES:(h + 1) * LANES]
                doh = jnp.where(masks[h], do_b, jnp.zeros_like(do_b))
                delta = jnp.sum(jnp.where(masks[h], prod, 0.0), axis=1, keepdims=True)
                lse_h = lse[:, h * HEAD:h * HEAD + 1]
                s_scr[...] = _dot_nt(qh, kh[h])
                dp_scr[...] = _dot_nt(doh, vh[h])
                for r0 in range(0, tq, STRIP):
                    rows = slice(r0, r0 + STRIP)
                    s = s_scr[rows, :] * scale
                    if masked:
                        rr = lax.broadcasted_iota(jnp.int32, (STRIP, tk), 0) + (i * tq + r0)
                        cc = lax.broadcasted_iota(jnp.int32, (STRIP, tk), 1) + j * tk
                        s = jnp.where(cc <= rr, s, NEG)
                    p = jnp.exp(s - lse_h[rows])
                    p_scr[rows, :] = p.astype(BF16)
                    ds_scr[rows, :] = (p * (dp_scr[rows, :] - delta[rows])).astype(BF16)
                ds = ds_scr[...]
                dq_ref[pl.ds(off, tq), h * LANES:(h + 1) * LANES] += _dot(ds, kh[h]) * scale
                dks[h] = dks[h] + _dot_tn(ds, qh)
                dv = dv + _dot_tn(p_scr[...], doh)
            return dks[0], dks[1], dv

        zero = jnp.zeros((tk, LANES), F32)
        carry = step(i0, (zero, zero, zero), True)
        dk0, dk1, dv = lax.fori_loop(i0 + 1, nq, lambda i, c: step(i, c, False), carry)
        dk_ref[:, 0:LANES] = dk0 * scale
        dk_ref[:, LANES:2 * LANES] = dk1 * scale
        dv_ref[...] = dv

    seq1 = pl.BlockSpec((None, S, LANES), lambda b, p, j: (b, 0, p))
    seq2 = pl.BlockSpec((None, S, 2 * LANES), lambda b, p, j: (b, 0, p))
    return pl.pallas_call(
        body, name=name, grid=(Bl, 3, S // tk),
        in_specs=[seq2, pl.BlockSpec((None, tk, 2 * LANES), lambda b, p, j: (b, j, p)),
                  pl.BlockSpec((None, tk, LANES), lambda b, p, j: (b, j, vblk0 + p)), seq1, seq1, seq1],
        out_specs=[seq2, pl.BlockSpec((None, tk, 2 * LANES), lambda b, p, j: (b, j, p)), pl.BlockSpec((None, tk, LANES), lambda b, p, j: (b, j, p))],
        out_shape=[jax.ShapeDtypeStruct((Bl, S, 6 * LANES), F32), jax.ShapeDtypeStruct((Bl, S, 6 * LANES), F32), jax.ShapeDtypeStruct((Bl, S, 3 * LANES), F32)],
        scratch_shapes=[pltpu.VMEM((tq, tk), F32), pltpu.VMEM((tq, tk), F32), pltpu.VMEM((tq, tk), BF16), pltpu.VMEM((tq, tk), BF16)],
        compiler_params=_cp("parallel", "parallel", "arbitrary"),
    )(q3, k3, kv3, o3, lse3, do3)


def _bucket_table():
    a = jnp.arange(WINDOW)[:, None]
    b = jnp.arange(2 * WINDOW)[None, :]
    dist = WINDOW + a - b
    max_exact = REL_BUCKETS // 2
    n = jnp.maximum(dist, 0)
    nf = jnp.maximum(n, 1).astype(F32)
    large = max_exact + (jnp.log(nf / max_exact) / math.log(REL_MAX_DIST / max_exact) * (REL_BUCKETS - max_exact)).astype(jnp.int32)
    large = jnp.minimum(large, REL_BUCKETS - 1)
    bucket = jnp.where(n < max_exact, n, large)
    return jnp.where((dist >= 0) & (dist < WINDOW), bucket, -1).astype(jnp.int32)


def swa_bias(rel_flat, bucket, *, name):
    def body(t_ref, b_ref, o_ref):
        bk = b_ref[...]
        for p in range(3):
            for hh in range(2):
                h = hh * 3 + p
                acc = jnp.full(bk.shape, NEG, F32)
                for b in range(REL_BUCKETS):
                    acc = jnp.where(bk == b, t_ref[b * 6 + h], acc)
                o_ref[p, hh] = acc

    return pl.pallas_call(
        body, name=name,
        in_specs=[pl.BlockSpec(memory_space=pltpu.SMEM), pl.BlockSpec(memory_space=pltpu.VMEM)],
        out_specs=pl.BlockSpec(memory_space=pltpu.VMEM),
        out_shape=jax.ShapeDtypeStruct((3, 2, WINDOW, 2 * WINDOW), F32),
    )(rel_flat, bucket)


def swa_bias_bwd(dbias, bucket, *, name):
    Bl = dbias.shape[0]

    def body(d_ref, b_ref, o_ref):
        bk = b_ref[...]
        lane = lax.broadcasted_iota(jnp.int32, (1, LANES), 1)
        rows = []
        for h in range(6):
            hh, p = divmod(h, 3)
            d = d_ref[0, p, hh]
            for bl in range(1, Bl):
                d = d + d_ref[bl, p, hh]
            row = jnp.zeros((1, LANES), F32)
            for b in range(REL_BUCKETS):
                s = jnp.sum(jnp.sum(jnp.where(bk == b, d, 0.0), axis=1, keepdims=True), axis=0, keepdims=True)
                row = row + jnp.where(lane == b, s, 0.0)
            rows.append(row)
        rows += [jnp.zeros((1, LANES), F32)] * 2
        o_ref[...] = jnp.concatenate(rows, axis=0)

    return pl.pallas_call(
        body, name=name,
        in_specs=[pl.BlockSpec(memory_space=pltpu.VMEM)] * 2, out_specs=pl.BlockSpec(memory_space=pltpu.VMEM),
        out_shape=jax.ShapeDtypeStruct((8, LANES), F32),
    )(dbias, bucket)


SWA_QBLOCKS = 4


def _swa_specs(vblk, nqb):
    rows = nqb * WINDOW
    cur = lambda blk: pl.BlockSpec((None, rows, LANES), lambda b, p, n: (b, n, blk))
    prev = lambda blk: pl.BlockSpec((None, WINDOW, LANES), lambda b, p, n: (b, jnp.maximum(n * nqb - 1, 0), blk))
    return [pl.BlockSpec((None, rows, LANES), lambda b, p, n: (b, n, p)), cur(0), prev(0), cur(vblk), prev(vblk),
            pl.BlockSpec((None, 2, WINDOW, 2 * WINDOW), lambda b, p, n: (p, 0, 0, 0)), pl.BlockSpec((None, 2, LANES), lambda b, p, n: (p, 0, 0))]


def _rows128(ref, m):
    return ref[m * WINDOW:(m + 1) * WINDOW, :]


def _swa_logits(qh, kp, kc, bias_h, first, scale):
    sp = jnp.where(first, NEG, _dot_nt(qh, kp) * scale + bias_h[:, :WINDOW])
    sc = _dot_nt(qh, kc) * scale + bias_h[:, WINDOW:]
    return sp, sc


def swa_attn_fwd(qn3, kn3, proj3, bias, sinks, *, plans=None, name):
    Bl, S, _ = qn3.shape
    scale = HEAD ** -0.5
    nqb = min(SWA_QBLOCKS, S // WINDOW)

    def body(q_ref, kc_ref, kp_ref, vc_ref, vp_ref, b_ref, s_ref, o_ref, lse_ref):
        seq_start = pl.program_id(2) == 0
        masks = _lane_masks()
        for m_ in range(nqb):
            first = seq_start if m_ == 0 else False
            q = _rows128(q_ref, m_)
            kp = kp_ref[...] if m_ == 0 else _rows128(kc_ref, m_ - 1)
            vp = vp_ref[...] if m_ == 0 else _rows128(vc_ref, m_ - 1)
            kc, vc = _rows128(kc_ref, m_), _rows128(vc_ref, m_)
            o = jnp.zeros((WINDOW, LANES), F32)
            lses = []
            for h in range(2):
                qh = jnp.where(masks[h], q, jnp.zeros_like(q))
                sp, sc = _swa_logits(qh, kp, kc, b_ref[h], first, scale)
                sink = s_ref[h:h + 1, 0:1]
                m = jnp.maximum(jnp.maximum(jnp.max(sp, axis=1, keepdims=True), jnp.max(sc, axis=1, keepdims=True)), sink)
                ep, ec = jnp.exp(sp - m), jnp.exp(sc - m)
                l = jnp.sum(ep, axis=1, keepdims=True) + jnp.sum(ec, axis=1, keepdims=True) + jnp.exp(sink - m)
                inv = 1.0 / l
                o = o + _dot((ep * inv).astype(BF16), jnp.where(masks[h], vp, 0.0).astype(BF16))
                o = o + _dot((ec * inv).astype(BF16), jnp.where(masks[h], vc, 0.0).astype(BF16))
                lses.append(m + jnp.log(l))
            o_ref[m_ * WINDOW:(m_ + 1) * WINDOW, :] = o
            lse_ref[m_ * WINDOW:(m_ + 1) * WINDOW, :] = jnp.where(masks[0], lses[0], lses[1])

    out = pl.BlockSpec((None, nqb * WINDOW, LANES), lambda b, p, n: (b, n, p))
    shp = jax.ShapeDtypeStruct((Bl, S, 3 * LANES), F32)
    return call_with_plans(
        body, plans, name=name, grid=(Bl, 3, S // (nqb * WINDOW)), in_specs=_swa_specs(P_SWV // LANES, nqb),
        out_specs=[out, out], out_shape=[shp, shp], scratch_shapes=[], args=[qn3, kn3, kn3, proj3, proj3, bias, sinks],
        sem=("arbitrary",) * 3 if plans else ("parallel", "parallel", "arbitrary"))


def swa_attn_bwd(qn3, kn3, proj3, bias, sinks, o3, lse3, do3, *, name):
    Bl, S, _ = qn3.shape
    scale = HEAD ** -0.5
    nqb = min(SWA_QBLOCKS, S // WINDOW)
    rows = nqb * WINDOW

    def body(q_ref, kc_ref, kp_ref, vc_ref, vp_ref, b_ref, s_ref, o_ref, lse_ref, do_ref,
             dq_ref, dk_ref, dv_ref, db_ref, dsk_ref):
        p_id, n = pl.program_id(1), pl.program_id(2)
        seq_start = n == 0

        @pl.when((p_id == 0) & seq_start)
        def _():
            dk_ref[...] = jnp.zeros_like(dk_ref)
            dv_ref[...] = jnp.zeros_like(dv_ref)

        @pl.when(seq_start)
        def _():
            db_ref[...] = jnp.zeros_like(db_ref)
            dsk_ref[...] = jnp.zeros_like(dsk_ref)

        masks = _lane_masks()
        zero = jnp.zeros((WINDOW, LANES), F32)
        dk_acc = [zero] * (nqb + 1)
        dv_acc = [zero] * (nqb + 1)
        db_acc = [[jnp.zeros((WINDOW, WINDOW), F32)] * 2 for _ in range(2)]
        dsk_acc = [jnp.zeros((1, 1), F32)] * 2
        for m_ in range(nqb):
            first = seq_start if m_ == 0 else False
            q = _rows128(q_ref, m_)
            kp = kp_ref[...] if m_ == 0 else _rows128(kc_ref, m_ - 1)
            vp = vp_ref[...] if m_ == 0 else _rows128(vc_ref, m_ - 1)
            kc, vc = _rows128(kc_ref, m_), _rows128(vc_ref, m_)
            do_b = _rows128(do_ref, m_).astype(BF16)
            prod = do_b.astype(F32) * _rows128(o_ref, m_)
            lse = _rows128(lse_ref, m_)
            dq = zero
            for h in range(2):
                qh = jnp.where(masks[h], q, jnp.zeros_like(q))
                doh = jnp.where(masks[h], do_b, jnp.zeros_like(do_b))
                sp, sc = _swa_logits(qh, kp, kc, b_ref[h], first, scale)
                lse_h = lse[:, h * HEAD:h * HEAD + 1]
                pp, pc = jnp.exp(sp - lse_h), jnp.exp(sc - lse_h)
                delta = jnp.sum(jnp.where(masks[h], prod, 0.0), axis=1, keepdims=True)
                dsp = pp * (_dot_nt(doh, jnp.where(masks[h], vp, 0.0).astype(BF16)) - delta)
                dsc = pc * (_dot_nt(doh, jnp.where(masks[h], vc, 0.0).astype(BF16)) - delta)
                db_acc[h] = [db_acc[h][0] + dsp, db_acc[h][1] + dsc]
                psink = jnp.exp(s_ref[h:h + 1, 0:1] - lse_h)
                dsk_acc[h] = dsk_acc[h] - jnp.sum(psink * delta, axis=0, keepdims=True)
                dspb, dscb = (dsp * scale).astype(BF16), (dsc * scale).astype(BF16)
                dq = dq + _dot(dspb, jnp.where(masks[h], kp, jnp.zeros_like(kp))) + _dot(dscb, jnp.where(masks[h], kc, jnp.zeros_like(kc)))
                dk_acc[m_] = dk_acc[m_] + _dot_tn(dspb, qh)
                dk_acc[m_ + 1] = dk_acc[m_ + 1] + _dot_tn(dscb, qh)
                dv_acc[m_] = dv_acc[m_] + _dot_tn(pp.astype(BF16), doh)
                dv_acc[m_ + 1] = dv_acc[m_ + 1] + _dot_tn(pc.astype(BF16), doh)
            dq_ref[m_ * WINDOW:(m_ + 1) * WINDOW, :] = dq
        for h in range(2):
            db_ref[h, :, 0:WINDOW] += db_acc[h][0]
            db_ref[h, :, WINDOW:2 * WINDOW] += db_acc[h][1]
            dsk_ref[h:h + 1, :] += jnp.broadcast_to(dsk_acc[h], (1, LANES))
        offp = pl.multiple_of(jnp.maximum(n * nqb - 1, 0) * WINDOW, WINDOW)
        dk_ref[pl.ds(offp, WINDOW), :] += dk_acc[0]
        dv_ref[pl.ds(offp, WINDOW), :] += dv_acc[0]
        for m_ in range(nqb):
            off = pl.multiple_of(n * rows + m_ * WINDOW, WINDOW)
            dk_ref[pl.ds(off, WINDOW), :] += dk_acc[m_ + 1]
            dv_ref[pl.ds(off, WINDOW), :] += dv_acc[m_ + 1]

    blk = pl.BlockSpec((None, rows, LANES), lambda b, p, n: (b, n, p))
    seq = pl.BlockSpec((None, S, LANES), lambda b, p, n: (b, 0, 0))
    return pl.pallas_call(
        body, name=name, grid=(Bl, 3, S // rows), in_specs=_swa_specs(P_SWV // LANES, nqb) + [blk, blk, blk],
        out_specs=[blk, seq, seq, pl.BlockSpec((None, None, 2, WINDOW, 2 * WINDOW), lambda b, p, n: (b, p, 0, 0, 0)),
                   pl.BlockSpec((None, None, 2, LANES), lambda b, p, n: (b, p, 0, 0))],
        out_shape=[jax.ShapeDtypeStruct((Bl, S, 3 * LANES), F32), jax.ShapeDtypeStruct((Bl, S, LANES), F32), jax.ShapeDtypeStruct((Bl, S, LANES), F32),
                   jax.ShapeDtypeStruct((Bl, 3, 2, WINDOW, 2 * WINDOW), F32), jax.ShapeDtypeStruct((Bl, 3, 2, LANES), F32)],
        compiler_params=_cp("arbitrary", "arbitrary", "arbitrary"),
    )(qn3, kn3, kn3, proj3, proj3, bias, sinks, o3, lse3, do3)


CONV_ROWS = 64
CONV_LANES = 128


def _conv_strip(x_ref, h_ref, w, b, r0, cols, first_blk):
    x = x_ref[r0:r0 + CONV_ROWS, cols]
    if r0 == 0:
        rows = lax.broadcasted_iota(jnp.int32, x.shape, 0)
        h6 = jnp.where(first_blk, 0.0, h_ref[6:7, cols])
        h7 = jnp.where(first_blk, 0.0, h_ref[7:8, cols])
        x1 = jnp.where(rows == 0, h7, pltpu.roll(x, 1, 0))
        x2 = jnp.where(rows == 0, h6, jnp.where(rows == 1, h7, pltpu.roll(x, 2, 0)))
    else:
        x1 = x_ref[r0 - 1:r0 - 1 + CONV_ROWS, cols]
        x2 = x_ref[r0 - 2:r0 - 2 + CONV_ROWS, cols]
    return w[0:1] * x2 + w[1:2] * x1 + w[2:3] * x + b, x, x1, x2


FF_BLK = D_FF // 2


def _up_perm(a):
    q = FF_BLK
    return _cat([a[..., 0:q], a[..., 2 * q:3 * q], a[..., q:2 * q], a[..., 3 * q:4 * q]])


def conv_gate_fwd(up3, cw, cb, *, tm=256, name):
    Bl, S, _ = up3.shape
    tm = min(tm, S)
    W = 2 * FF_BLK

    def body(x_ref, h_ref, w_ref, b_ref, o_ref):
        first = pl.program_id(1) == 0

        def chunk(c, carry):
            cg = pl.ds(pl.multiple_of(c * CONV_LANES, CONV_LANES), CONV_LANES)
            cv = pl.ds(pl.multiple_of(FF_BLK + c * CONV_LANES, CONV_LANES), CONV_LANES)
            wg, wv, bg, bv = w_ref[:, cg], w_ref[:, cv], b_ref[:, cg], b_ref[:, cv]
            for r0 in range(0, tm, CONV_ROWS):
                ug = _conv_strip(x_ref, h_ref, wg, bg, r0, cg, first)[0]
                uv = _conv_strip(x_ref, h_ref, wv, bv, r0, cv, first)[0]
                o_ref[r0:r0 + CONV_ROWS, cg] = (ug * jax.nn.sigmoid(ug) * uv).astype(BF16)
            return carry

        lax.fori_loop(0, FF_BLK // CONV_LANES, chunk, 0)

    hb = tm // 8
    return pl.pallas_call(
        body, name=name, grid=(Bl, S // tm, 2),
        in_specs=[pl.BlockSpec((None, tm, W), lambda b, s, c: (b, s, c)),
                  pl.BlockSpec((None, 8, W), lambda b, s, c: (b, jnp.maximum(s * hb - 1, 0), c)),
                  pl.BlockSpec((3, W), lambda b, s, c: (0, c)), pl.BlockSpec((1, W), lambda b, s, c: (0, c))],
        out_specs=pl.BlockSpec((None, tm, FF_BLK), lambda b, s, c: (b, s, c)),
        out_shape=jax.ShapeDtypeStruct((Bl, S, D_FF), BF16),
        compiler_params=_cp("parallel", "parallel", "parallel"),
    )(up3, up3, cw, cb)


def conv_gate_bwd(up3, cw, cb, da3, *, tm=256, name):
    Bl, S, _ = up3.shape
    tm = min(tm, S)
    ns = S // tm
    W = 2 * FF_BLK

    def body(x_ref, h_ref, w_ref, b_ref, da_ref, dup_ref, dw_ref, nxt_ref, du_scr):
        b, s = pl.program_id(1), pl.program_id(2)
        seq_end = s == 0
        first = s == ns - 1

        @pl.when((b == 0) & seq_end)
        def _():
            dw_ref[...] = jnp.zeros_like(dw_ref)

        def du_chunk(c, carry):
            cg = pl.ds(pl.multiple_of(c * CONV_LANES, CONV_LANES), CONV_LANES)
            cv = pl.ds(pl.multiple_of(FF_BLK + c * CONV_LANES, CONV_LANES), CONV_LANES)
            wg, wv, bg, bv = w_ref[:, cg], w_ref[:, cv], b_ref[:, cg], b_ref[:, cv]
            acc_g = [jnp.zeros((1, CONV_LANES), F32)] * 4
            acc_v = [jnp.zeros((1, CONV_LANES), F32)] * 4
            for r0 in range(0, tm, CONV_ROWS):
                ug, xg, xg1, xg2 = _conv_strip(x_ref, h_ref, wg, bg, r0, cg, first)
                uv, xv, xv1, xv2 = _conv_strip(x_ref, h_ref, wv, bv, r0, cv, first)
                da = da_ref[r0:r0 + CONV_ROWS, cg].astype(F32)
                sg = jax.nn.sigmoid(ug)
                dug = da * uv * sg * (1.0 + ug * (1.0 - sg))
                duv = da * ug * sg
                du_scr[r0:r0 + CONV_ROWS, cg] = dug
                du_scr[r0:r0 + CONV_ROWS, cv] = duv
                col = lambda t: jnp.sum(t, axis=0, keepdims=True)
                acc_g = [acc_g[0] + col(dug * xg2), acc_g[1] + col(dug * xg1), acc_g[2] + col(dug * xg), acc_g[3] + col(dug)]
                acc_v = [acc_v[0] + col(duv * xv2), acc_v[1] + col(duv * xv1), acc_v[2] + col(duv * xv), acc_v[3] + col(duv)]
            for t in range(4):
                dw_ref[t:t + 1, cg] += acc_g[t]
                dw_ref[t:t + 1, cv] += acc_v[t]
            return carry

        lax.fori_loop(0, FF_BLK // CONV_LANES, du_chunk, 0)
        du_scr[tm:tm + 8, :] = jnp.where(seq_end, 0.0, nxt_ref[...])

        def dup_chunk(c, carry):
            cols = pl.ds(pl.multiple_of(c * CONV_LANES, CONV_LANES), CONV_LANES)
            w = w_ref[:, cols]
            for r0 in range(0, tm, CONV_ROWS):
                d0 = du_scr[r0:r0 + CONV_ROWS, cols]
                d1 = du_scr[r0 + 1:r0 + 1 + CONV_ROWS, cols]
                d2 = du_scr[r0 + 2:r0 + 2 + CONV_ROWS, cols]
                dup_ref[r0:r0 + CONV_ROWS, cols] = (w[2:3] * d0 + w[1:2] * d1 + w[0:1] * d2).astype(BF16)
            return carry

        lax.fori_loop(0, W // CONV_LANES, dup_chunk, 0)
        nxt_ref[...] = du_scr[0:8, :]

    hb = tm // 8
    rb = lambda s: ns - 1 - s
    return pl.pallas_call(
        body, name=name, grid=(2, Bl, ns),
        in_specs=[pl.BlockSpec((None, tm, W), lambda c, b, s: (b, rb(s), c)),
                  pl.BlockSpec((None, 8, W), lambda c, b, s: (b, jnp.maximum(rb(s) * hb - 1, 0), c)),
                  pl.BlockSpec((3, W), lambda c, b, s: (0, c)), pl.BlockSpec((1, W), lambda c, b, s: (0, c)),
                  pl.BlockSpec((None, tm, FF_BLK), lambda c, b, s: (b, rb(s), c))],
        out_specs=[pl.BlockSpec((None, tm, W), lambda c, b, s: (b, rb(s), c)), pl.BlockSpec((8, W), lambda c, b, s: (0, c))],
        out_shape=[jax.ShapeDtypeStruct((Bl, S, 2 * D_FF), BF16), jax.ShapeDtypeStruct((8, 2 * D_FF), F32)],
        scratch_shapes=[pltpu.VMEM((8, W), F32), pltpu.VMEM((tm + 8, W), F32)],
        compiler_params=_cp("arbitrary", "arbitrary", "arbitrary"),
    )(up3, up3, cw, cb, da3)


def gate_bwd(dx3, y3, gate, *, tm=512, name):
    Bl, S, D = dx3.shape
    tm = min(tm, S)

    def body(dx_ref, y_ref, g_ref, o_ref, dg_ref):
        @pl.when(pl.program_id(1) == 0)
        def _():
            dg_ref[...] = jnp.zeros_like(dg_ref)

        dx = dx_ref[...]
        dg_ref[...] += jnp.sum(dx * y_ref[...], axis=0, keepdims=True)
        o_ref[...] = (dx * g_ref[...]).astype(BF16)

    blk = pl.BlockSpec((None, tm, D), lambda b, s: (b, s, 0))
    vec = pl.BlockSpec((None, 1, D), lambda b, s: (b, 0, 0))
    return pl.pallas_call(
        body, name=name, grid=(Bl, S // tm), in_specs=[blk, blk, vec], out_specs=[blk, vec],
        out_shape=[jax.ShapeDtypeStruct((Bl, S, D), BF16), jax.ShapeDtypeStruct((Bl, 1, D), F32)],
        compiler_params=_cp("parallel", "arbitrary"),
    )(dx3, y3, gate)


def loss_grad(y3, t3, *, tm=512, name):
    Bl, S, D = y3.shape
    tm = min(tm, S)
    last = (Bl - 1, S // tm - 1)

    def body(y_ref, t_ref, dy_ref, l_ref, acc_ref):
        b, s = pl.program_id(0), pl.program_id(1)

        @pl.when((b == 0) & (s == 0))
        def _():
            acc_ref[...] = jnp.zeros_like(acc_ref)

        e = y_ref[...] - t_ref[...]
        dy_ref[...] = e * (1.0 / D)
        acc_ref[...] += jnp.sum(e * e, axis=0, keepdims=True)

        @pl.when((b == last[0]) & (s == last[1]))
        def _():
            l_ref[...] = jnp.broadcast_to(jnp.sum(acc_ref[...], axis=1, keepdims=True) * (0.5 / D), (1, LANES))

    blk = pl.BlockSpec((None, tm, D), lambda b, s: (b, s, 0))
    return pl.pallas_call(
        body, name=name, grid=(Bl, S // tm), in_specs=[blk, blk],
        out_specs=[blk, pl.BlockSpec((1, LANES), lambda b, s: (0, 0))],
        out_shape=[jax.ShapeDtypeStruct((Bl, S, D), F32), jax.ShapeDtypeStruct((1, LANES), F32)],
        scratch_shapes=[pltpu.VMEM((1, D), F32)], compiler_params=_cp("arbitrary", "arbitrary"),
    )(y3, t3)


def adamw(w, g, m, v, *, name):
    L, R, C = w.shape
    tr = _tile(R, 512, 8)

    def body(w_ref, g_ref, m_ref, v_ref, d_ref, m2_ref, v2_ref):
        d_ref[...], m2_ref[...], v2_ref[...] = _adam_update(w_ref[...], g_ref[...], m_ref[...], v_ref[...])

    blk = pl.BlockSpec((None, tr, C), lambda l, i: (l, i, 0))
    shp = jax.ShapeDtypeStruct((L, R, C), F32)
    return pl.pallas_call(
        body, name=name, grid=(L, R // tr), in_specs=[blk] * 4, out_specs=[blk] * 3, out_shape=[shp] * 3,
        compiler_params=_cp("parallel", "parallel"),
    )(w, g, m, v)


def sum_leading(x, *, out_dtype=F32, tr=256, name):
    n, R, C = x.shape
    tr = _tile(R, tr, 16)

    def body(x_ref, o_ref):
        acc = x_ref[0].astype(F32)
        for k in range(1, n):
            acc = acc + x_ref[k].astype(F32)
        o_ref[...] = acc.astype(out_dtype)

    return pl.pallas_call(
        body, name=name, grid=(R // tr,), in_specs=[pl.BlockSpec((n, tr, C), lambda i: (0, i, 0))],
        out_specs=pl.BlockSpec((tr, C), lambda i: (i, 0)), out_shape=jax.ShapeDtypeStruct((R, C), out_dtype),
        compiler_params=_cp("parallel"),
    )(x)


def _adam_update(w, g, m, v):
    c1 = 1.0 / (1.0 - ADAM_B1 ** ADAM_STEP)
    c2 = 1.0 / (1.0 - ADAM_B2 ** ADAM_STEP)
    m2 = ADAM_B1 * m + (1.0 - ADAM_B1) * g
    v2 = ADAM_B2 * v + (1.0 - ADAM_B2) * (g * g)
    return -ADAM_LR * ((m2 * c1) / (jnp.sqrt(v2 * c2) + ADAM_EPS) + ADAM_WD * w), m2, v2


def adamw_small(ws, gs, ms, vs, *, name):
    na = len(ws)

    def body(*refs):
        w_r, g_r, m_r, v_r = (refs[i * na:(i + 1) * na] for i in range(4))
        d_r, m2_r, v2_r = (refs[(4 + i) * na:(5 + i) * na] for i in range(3))
        for a in range(na):
            d_r[a][...], m2_r[a][...], v2_r[a][...] = _adam_update(w_r[a][...], g_r[a][...], m_r[a][...], v_r[a][...])

    vm = pl.BlockSpec(memory_space=pltpu.VMEM)
    shp = [jax.ShapeDtypeStruct(w.shape, F32) for w in ws]
    out = pl.pallas_call(body, name=name, in_specs=[vm] * (4 * na), out_specs=[vm] * (3 * na), out_shape=shp * 3)(*ws, *gs, *ms, *vs)
    return out[:na], out[na:2 * na], out[2 * na:]


def sum_small(xs, *, name):
    na = len(xs)

    def body(*refs):
        for x_ref, o_ref in zip(refs[:na], refs[na:]):
            acc = x_ref[0]
            for k in range(1, x_ref.shape[0]):
                acc = acc + x_ref[k]
            o_ref[...] = acc

    vm = pl.BlockSpec(memory_space=pltpu.VMEM)
    return pl.pallas_call(body, name=name, in_specs=[vm] * na, out_specs=[vm] * na,
                          out_shape=[jax.ShapeDtypeStruct(x.shape[1:], x.dtype) for x in xs])(*xs)


def pair_add_half(g4, recv, c_arr, *, tr=512, name):
    _, R, C = g4.shape
    H = R // 2
    tr = _tile(H, tr, 16)
    nb = H // tr

    def body(c_ref, g_ref, r_ref, o_ref):
        o_ref[...] = (g_ref[...].astype(F32) + r_ref[...].astype(F32)).astype(BF16)

    grid_spec = pltpu.PrefetchScalarGridSpec(
        num_scalar_prefetch=1, grid=(4, nb),
        in_specs=[pl.BlockSpec((None, tr, C), lambda k, i, c_ref: (k, c_ref[0] * nb + i, 0)),
                  pl.BlockSpec((None, tr, C), lambda k, i, c_ref: (k, i, 0))],
        out_specs=pl.BlockSpec((None, tr, C), lambda k, i, c_ref: (k, i, 0)),
    )
    return pl.pallas_call(
        body, name=name, grid_spec=grid_spec, out_shape=jax.ShapeDtypeStruct((4, H, C), BF16),
        compiler_params=_cp("parallel", "parallel"),
    )(c_arr, g4, recv)


def chip_sum_into(landed, pair, sel, *, tr=512, name):
    _, H, C = landed.shape
    tr = _tile(H, tr, 16)
    nb = H // tr

    def body(s_ref, l0, l1, l2, l3, p_ref, o_ref):
        own = p_ref[...].astype(F32)
        acc = None
        for k, l_ref in enumerate((l0, l1, l2, l3)):
            part = jnp.where(s_ref[0] == k, own, l_ref[...].astype(F32))
            acc = part if acc is None else acc + part
        o_ref[...] = acc

    def slot(k):
        return pl.BlockSpec((None, tr, C), lambda i, s: (jnp.where(s[0] == k, (k + 1) % 4, k), i, 0))

    grid_spec = pltpu.PrefetchScalarGridSpec(
        num_scalar_prefetch=1, grid=(nb,),
        in_specs=[slot(0), slot(1), slot(2), slot(3), pl.BlockSpec((None, tr, C), lambda i, s: (s[0], i, 0))],
        out_specs=pl.BlockSpec((tr, C), lambda i, s: (s[1] * nb + i, 0)),
    )
    return pl.pallas_call(
        body, name=name, grid_spec=grid_spec, out_shape=jax.ShapeDtypeStruct((2 * H, C), F32), compiler_params=_cp("parallel"),
    )(sel, landed, landed, landed, landed, pair)


def mods_matmul(c_all, w_ada, b_ada_cols, *, tn=512, name):
    L, D, E = w_ada.shape
    nb = c_all.shape[0]
    tn = _tile(E, tn)

    def body(c_ref, w_ref, b_ref, o_ref):
        c = c_ref[...]
        a = c * jax.nn.sigmoid(c)
        o_ref[...] = jnp.dot(a, w_ref[...], preferred_element_type=F32, precision=lax.Precision.HIGHEST) + b_ref[...]

    return pl.pallas_call(
        body, name=name, grid=(L, E // tn),
        in_specs=[pl.BlockSpec((nb, D), lambda l, j: (0, 0)), pl.BlockSpec((None, D, tn), lambda l, j: (l, 0, j)),
                  pl.BlockSpec((None, 1, tn), lambda l, j: (l, 0, j))],
        out_specs=pl.BlockSpec((None, nb, tn), lambda l, j: (l, 0, j)),
        out_shape=jax.ShapeDtypeStruct((L, nb, E), F32), compiler_params=_cp("parallel", "parallel"),
    )(c_all, w_ada, b_ada_cols)


def ada_grad(c_all, dmods, *, tn=512, name):
    L, nb, E = dmods.shape
    D = c_all.shape[1]
    tn = _tile(E, tn)

    def body(c_ref, d_ref, o_ref):
        c = c_ref[...]
        a = c * jax.nn.sigmoid(c)
        o_ref[...] = lax.dot_general(a, d_ref[...], (((0,), (0,)), ((), ())), preferred_element_type=F32, precision=lax.Precision.HIGHEST)

    return pl.pallas_call(
        body, name=name, grid=(L, E // tn),
        in_specs=[pl.BlockSpec((nb, D), lambda l, j: (0, 0)), pl.BlockSpec((None, nb, tn), lambda l, j: (l, 0, j))],
        out_specs=pl.BlockSpec((None, D, tn), lambda l, j: (l, 0, j)),
        out_shape=jax.ShapeDtypeStruct((L, D, E), F32), compiler_params=_cp("parallel", "parallel"),
    )(c_all, dmods)


HBM = pl.BlockSpec(memory_space=pltpu.HBM)


def _me():
    return lax.axis_index("x"), lax.axis_index("y"), lax.axis_index("c")


def _flip(v, bit):
    return 1 - v if bit else v


def allgather8(xs, *, name):
    na = len(xs)

    def body(*refs):
        x_refs, out_refs = refs[:na], refs[na:2 * na]
        send_sems, recv_sems = refs[2 * na], refs[2 * na + 1]
        x, y, c = _me()
        me = 4 * x + 2 * y + c
        for x_ref, out_ref in zip(x_refs, out_refs):
            out_ref[me] = x_ref[...]
        sends = []
        for a, (x_ref, out_ref) in enumerate(zip(x_refs, out_refs)):
            for k in range(1, 8):
                peer = (_flip(x, k & 4), _flip(y, k & 2), _flip(c, k & 1))
                cp = pltpu.make_async_remote_copy(src_ref=x_ref, dst_ref=out_ref.at[me], send_sem=send_sems.at[a, k - 1],
                                                  recv_sem=recv_sems.at[a, k - 1], device_id=peer, device_id_type=MESH)
                cp.start()
                sends.append(cp)
        for a, (x_ref, out_ref) in enumerate(zip(x_refs, out_refs)):
            for k in range(1, 8):
                peer = (_flip(x, k & 4), _flip(y, k & 2), _flip(c, k & 1))
                src = 4 * peer[0] + 2 * peer[1] + peer[2]
                pltpu.make_async_remote_copy(src_ref=x_ref, dst_ref=out_ref.at[src], send_sem=send_sems.at[a, k - 1],
                                             recv_sem=recv_sems.at[a, k - 1], device_id=peer, device_id_type=MESH).wait_recv()
        for cp in sends:
            cp.wait_send()

    vm = pl.BlockSpec(memory_space=pltpu.VMEM)
    return pl.pallas_call(
        body, name=name, in_specs=[vm] * na, out_specs=[vm] * na,
        out_shape=[jax.ShapeDtypeStruct((8,) + a.shape, a.dtype) for a in xs],
        scratch_shapes=[pltpu.SemaphoreType.DMA((na, 7)), pltpu.SemaphoreType.DMA((na, 7))],
    )(*xs)


LOCAL_CHUNKS = 8


def _copy_via_vmem(src, dst_at, rows, buf, sem):
    ch = buf.shape[0]
    for i in range(rows // ch):
        load = pltpu.make_async_copy(src.at[pl.ds(i * ch, ch)], buf, sem)
        load.start()
        load.wait()
        store = pltpu.make_async_copy(buf, dst_at(i * ch, ch), sem)
        store.start()
        store.wait()


def _chunk_buf(rows, cols, dtype):
    align = 16 if dtype == BF16 else 8
    for n in range(LOCAL_CHUNKS, 0, -1):
        if rows % n == 0 and (rows // n) % align == 0:
            return pltpu.VMEM((rows // n, cols), dtype)
    return pltpu.VMEM((rows, cols), dtype)


def gather_weights(ws, *, name):
    na = len(ws)

    def body(*refs):
        x_refs, out_refs = refs[:na], refs[na:2 * na]
        send_sems, recv_sems, local_sem = refs[2 * na:2 * na + 3]
        bufs = refs[2 * na + 3:]
        x, y, c = _me()
        j = 2 * x + y
        chips = [(_flip(x, k & 2), _flip(y, k & 1)) for k in range(1, 4)]
        sends = []
        for a, (x_ref, out_ref) in enumerate(zip(x_refs, out_refs)):
            H = x_ref.shape[0] // 2
            for k, (px, py) in enumerate(chips):
                cp = pltpu.make_async_remote_copy(src_ref=x_ref.at[pl.ds(c * H, H)], dst_ref=out_ref.at[j, pl.ds(c * H, H)],
                                                  send_sem=send_sems.at[a, k], recv_sem=recv_sems.at[a, k],
                                                  device_id=(px, py, c), device_id_type=MESH)
                cp.start()
                sends.append(cp)
        for x_ref, out_ref, buf in zip(x_refs, out_refs, bufs):
            _copy_via_vmem(x_ref, lambda o, n, out_ref=out_ref: out_ref.at[j, pl.ds(o, n)], x_ref.shape[0], buf, local_sem)
        for a, out_ref in enumerate(out_refs):
            H = out_ref.shape[1] // 2
            for k, (px, py) in enumerate(chips):
                slot = out_ref.at[2 * px + py, pl.ds(c * H, H)]
                pltpu.make_async_remote_copy(src_ref=slot, dst_ref=slot, send_sem=send_sems.at[a, k], recv_sem=recv_sems.at[a, k],
                                             device_id=(px, py, c), device_id_type=MESH).wait_recv()
                cp = pltpu.make_async_remote_copy(src_ref=slot, dst_ref=slot, send_sem=send_sems.at[a, 3 + k],
                                                  recv_sem=recv_sems.at[a, 3 + k], device_id=(x, y, 1 - c), device_id_type=MESH)
                cp.start()
                sends.append(cp)
        for a, out_ref in enumerate(out_refs):
            H = out_ref.shape[1] // 2
            for k, (px, py) in enumerate(chips):
                slot = out_ref.at[2 * px + py, pl.ds((1 - c) * H, H)]
                pltpu.make_async_remote_copy(src_ref=slot, dst_ref=slot, send_sem=send_sems.at[a, 3 + k], recv_sem=recv_sems.at[a, 3 + k],
                                             device_id=(x, y, 1 - c), device_id_type=MESH).wait_recv()
        for cp in sends:
            cp.wait_send()

    return pl.pallas_call(
        body, name=name, in_specs=[HBM] * na, out_specs=[HBM] * na,
        out_shape=[jax.ShapeDtypeStruct((4,) + w.shape, w.dtype) for w in ws],
        scratch_shapes=[pltpu.SemaphoreType.DMA((na, 6)), pltpu.SemaphoreType.DMA((na, 6)), pltpu.SemaphoreType.DMA]
        + [_chunk_buf(w.shape[0], w.shape[1], w.dtype) for w in ws],
    )(*ws)


def swap_halves(gs, *, name):
    na = len(gs)

    def body(*refs):
        g_refs, out_refs = refs[:na], refs[na:2 * na]
        send_sems, recv_sems = refs[2 * na:]
        x, y, c = _me()
        sib = (x, y, 1 - c)
        sends = []
        for a, (g_ref, out_ref) in enumerate(zip(g_refs, out_refs)):
            H = g_ref.shape[1] // 2
            for k in range(4):
                cp = pltpu.make_async_remote_copy(src_ref=g_ref.at[k, pl.ds((1 - c) * H, H)], dst_ref=out_ref.at[k],
                                                  send_sem=send_sems.at[a, k], recv_sem=recv_sems.at[a, k], device_id=sib, device_id_type=MESH)
                cp.start()
                sends.append(cp)
        for a, (g_ref, out_ref) in enumerate(zip(g_refs, out_refs)):
            H = g_ref.shape[1] // 2
            for k in range(4):
                pltpu.make_async_remote_copy(src_ref=g_ref.at[k, pl.ds(c * H, H)], dst_ref=out_ref.at[k], send_sem=send_sems.at[a, k],
                                             recv_sem=recv_sems.at[a, k], device_id=sib, device_id_type=MESH).wait_recv()
        for cp in sends:
            cp.wait_send()

    return pl.pallas_call(
        body, name=name, in_specs=[HBM] * na, out_specs=[HBM] * na,
        out_shape=[jax.ShapeDtypeStruct((4, g.shape[1] // 2, g.shape[2]), g.dtype) for g in gs],
        scratch_shapes=[pltpu.SemaphoreType.DMA((na, 4)), pltpu.SemaphoreType.DMA((na, 4))],
    )(*gs)


def scatter_chips(ps, *, name):
    na = len(ps)

    def body(*refs):
        p_refs, out_refs = refs[:na], refs[na:2 * na]
        send_sems, recv_sems, local_sem = refs[2 * na:2 * na + 3]
        bufs = refs[2 * na + 3:]
        x, y, c = _me()
        j = 2 * x + y
        chips = [(_flip(x, k & 2), _flip(y, k & 1)) for k in range(1, 4)]
        sends = []
        for a, (p_ref, out_ref) in enumerate(zip(p_refs, out_refs)):
            for k, (px, py) in enumerate(chips):
                cp = pltpu.make_async_remote_copy(src_ref=p_ref.at[2 * px + py], dst_ref=out_ref.at[j], send_sem=send_sems.at[a, k],
                                                  recv_sem=recv_sems.at[a, k], device_id=(px, py, c), device_id_type=MESH)
                cp.start()
                sends.append(cp)
        for p_ref, out_ref, buf in zip(p_refs, out_refs, bufs):
            _copy_via_vmem(p_ref.at[j], lambda o, n, out_ref=out_ref: out_ref.at[j, pl.ds(o, n)], p_ref.shape[1], buf, local_sem)
        for a, out_ref in enumerate(out_refs):
            for k, (px, py) in enumerate(chips):
                slot = out_ref.at[2 * px + py]
                pltpu.make_async_remote_copy(src_ref=slot, dst_ref=slot, send_sem=send_sems.at[a, k], recv_sem=recv_sems.at[a, k],
                                             device_id=(px, py, c), device_id_type=MESH).wait_recv()
        for cp in sends:
            cp.wait_send()

    return pl.pallas_call(
        body, name=name, in_specs=[HBM] * na, out_specs=[HBM] * na, out_shape=[jax.ShapeDtypeStruct(p.shape, p.dtype) for p in ps],
        scratch_shapes=[pltpu.SemaphoreType.DMA((na, 3)), pltpu.SemaphoreType.DMA((na, 3)), pltpu.SemaphoreType.DMA]
        + [_chunk_buf(p.shape[1], p.shape[2], p.dtype) for p in ps],
    )(*ps)


def join_halves(halves, *, name):
    na = len(halves)

    def body(*refs):
        h_refs, out_refs = refs[:na], refs[na:2 * na]
        send_sems, recv_sems, local_sem = refs[2 * na:2 * na + 3]
        bufs = refs[2 * na + 3:]
        x, y, c = _me()
        sib = (x, y, 1 - c)
        sends = []
        for a, (h_ref, out_ref) in enumerate(zip(h_refs, out_refs)):
            H = h_ref.shape[0]
            cp = pltpu.make_async_remote_copy(src_ref=h_ref, dst_ref=out_ref.at[pl.ds(c * H, H)], send_sem=send_sems.at[a],
                                              recv_sem=recv_sems.at[a], device_id=sib, device_id_type=MESH)
            cp.start()
            sends.append(cp)
        for h_ref, out_ref, buf in zip(h_refs, out_refs, bufs):
            H = h_ref.shape[0]
            _copy_via_vmem(h_ref, lambda o, n, out_ref=out_ref, H=H: out_ref.at[pl.ds(c * H + o, n)], H, buf, local_sem)
        for a, (h_ref, out_ref) in enumerate(zip(h_refs, out_refs)):
            H = h_ref.shape[0]
            pltpu.make_async_remote_copy(src_ref=h_ref, dst_ref=out_ref.at[pl.ds((1 - c) * H, H)], send_sem=send_sems.at[a],
                                         recv_sem=recv_sems.at[a], device_id=sib, device_id_type=MESH).wait_recv()
        for cp in sends:
            cp.wait_send()

    return pl.pallas_call(
        body, name=name, in_specs=[HBM] * na, out_specs=[HBM] * na,
        out_shape=[jax.ShapeDtypeStruct((2 * h.shape[0], h.shape[1]), h.dtype) for h in halves],
        scratch_shapes=[pltpu.SemaphoreType.DMA((na,)), pltpu.SemaphoreType.DMA((na,)), pltpu.SemaphoreType.DMA]
        + [_chunk_buf(h.shape[0], h.shape[1], h.dtype) for h in halves],
    )(*halves)


class _Plan:
    def __init__(self, ins, out_shapes, ncopies, copies, aliased=False):
        self.ins, self.out_shapes, self.ncopies, self.copies, self.aliased = list(ins), list(out_shapes), ncopies, copies, aliased

    def start(self, in_refs, out_refs, send_sems, recv_sems):
        sends, _ = self.copies(in_refs, out_refs, send_sems, recv_sems)
        for cp in sends:
            cp.start()

    def finish(self, in_refs, out_refs, send_sems, recv_sems):
        sends, recvs = self.copies(in_refs, out_refs, send_sems, recv_sems)
        for cp in recvs:
            cp.wait_recv()
        for cp in sends:
            cp.wait_send()


def _rcopy(src, dst, send_sems, recv_sems, idx, dev):
    return pltpu.make_async_remote_copy(src_ref=src, dst_ref=dst, send_sem=send_sems.at[idx], recv_sem=recv_sems.at[idx],
                                        device_id=dev, device_id_type=MESH)


def _other_chips(x, y):
    return [(_flip(x, k & 2), _flip(y, k & 1)) for k in range(1, 4)]


def plan_gather_ici(ws):
    def copies(in_refs, out_refs, ss, rs):
        x, y, c = _me()
        j = 2 * x + y
        sends, recvs = [], []
        for a, (x_ref, out_ref) in enumerate(zip(in_refs, out_refs)):
            H = x_ref.shape[0] // 2
            for k, (px, py) in enumerate(_other_chips(x, y)):
                sends.append(_rcopy(x_ref.at[pl.ds(c * H, H)], out_ref.at[j, pl.ds(c * H, H)], ss, rs, 3 * a + k, (px, py, c)))
                slot = out_ref.at[2 * px + py, pl.ds(c * H, H)]
                recvs.append(_rcopy(slot, slot, ss, rs, 3 * a + k, (px, py, c)))
        return sends, recvs

    return _Plan(ws, [jax.ShapeDtypeStruct((4,) + w.shape, w.dtype) for w in ws], 3 * len(ws), copies)


def plan_gather_d2d(w4s):
    def copies(in_refs, out_refs, ss, rs):
        x, y, c = _me()
        sends, recvs = [], []
        for a, out_ref in enumerate(out_refs):
            H = out_ref.shape[1] // 2
            for k, (px, py) in enumerate(_other_chips(x, y)):
                mine = out_ref.at[2 * px + py, pl.ds(c * H, H)]
                theirs = out_ref.at[2 * px + py, pl.ds((1 - c) * H, H)]
                sends.append(_rcopy(mine, mine, ss, rs, 3 * a + k, (x, y, 1 - c)))
                recvs.append(_rcopy(theirs, theirs, ss, rs, 3 * a + k, (x, y, 1 - c)))
        return sends, recvs

    return _Plan(w4s, [jax.ShapeDtypeStruct(w.shape, w.dtype) for w in w4s], 3 * len(w4s), copies, aliased=True)


def plan_swap_halves(gs):
    def copies(in_refs, out_refs, ss, rs):
        x, y, c = _me()
        sends, recvs = [], []
        for a, (g_ref, out_ref) in enumerate(zip(in_refs, out_refs)):
            H = g_ref.shape[1] // 2
            for k in range(4):
                sends.append(_rcopy(g_ref.at[k, pl.ds((1 - c) * H, H)], out_ref.at[k], ss, rs, 4 * a + k, (x, y, 1 - c)))
                recvs.append(_rcopy(g_ref.at[k, pl.ds(c * H, H)], out_ref.at[k], ss, rs, 4 * a + k, (x, y, 1 - c)))
        return sends, recvs

    return _Plan(gs, [jax.ShapeDtypeStruct((4, g.shape[1] // 2, g.shape[2]), g.dtype) for g in gs], 4 * len(gs), copies)


def plan_scatter_ici(ps):
    def copies(in_refs, out_refs, ss, rs):
        x, y, c = _me()
        j = 2 * x + y
        sends, recvs = [], []
        for a, (p_ref, out_ref) in enumerate(zip(in_refs, out_refs)):
            for k, (px, py) in enumerate(_other_chips(x, y)):
                sends.append(_rcopy(p_ref.at[2 * px + py], out_ref.at[j], ss, rs, 3 * a + k, (px, py, c)))
                slot = out_ref.at[2 * px + py]
                recvs.append(_rcopy(slot, slot, ss, rs, 3 * a + k, (px, py, c)))
        return sends, recvs

    return _Plan(ps, [jax.ShapeDtypeStruct(p.shape, p.dtype) for p in ps], 3 * len(ps), copies)


def plan_join_halves(fulls):
    def copies(in_refs, out_refs, ss, rs):
        x, y, c = _me()
        sends, recvs = [], []
        for a, out_ref in enumerate(out_refs):
            H = out_ref.shape[0] // 2
            mine, theirs = out_ref.at[pl.ds(c * H, H)], out_ref.at[pl.ds((1 - c) * H, H)]
            sends.append(_rcopy(mine, mine, ss, rs, a, (x, y, 1 - c)))
            recvs.append(_rcopy(theirs, theirs, ss, rs, a, (x, y, 1 - c)))
        return sends, recvs

    return _Plan(fulls, [jax.ShapeDtypeStruct(f.shape, f.dtype) for f in fulls], len(fulls), copies, aliased=True)


def call_with_plans(body, plans, *, grid, in_specs, out_specs, out_shape, scratch_shapes, args, sem, name):
    plans = list(plans or [])
    n_in, n_out, n_scr = len(in_specs), len(out_specs), len(scratch_shapes)
    c_in = [len(p.ins) for p in plans]
    c_out = [len(p.out_shapes) for p in plans]
    steps = math.prod(grid) if grid else 1

    def wrapped(*refs):
        pos = 0

        def take(n):
            nonlocal pos
            out = refs[pos:pos + n]
            pos += n
            return out

        ins = take(n_in)
        cins = [take(n) for n in c_in]
        outs = take(n_out)
        couts = [take(n) for n in c_out]
        scr = take(n_scr)
        sems = [take(2) for _ in plans]
        def start_all():
            for p, ci, co, (ss, rs) in zip(plans, cins, couts, sems):
                p.start(ci, co, ss, rs)

        def finish_all():
            for p, ci, co, (ss, rs) in zip(plans, cins, couts, sems):
                p.finish(ci, co, ss, rs)

        if plans and grid:
            idx = 0
            for ax, g in enumerate(grid):
                idx = idx * g + pl.program_id(ax)
            pl.when(idx == 0)(start_all)
        elif plans:
            start_all()
        if body is not None:
            body(*ins, *outs, *scr)
        if plans and grid:
            pl.when(idx == steps - 1)(finish_all)
        elif plans:
            finish_all()

    aliases = {}
    i_pos, o_pos = n_in, n_out
    for p, ni, no in zip(plans, c_in, c_out):
        if p.aliased:
            aliases.update({i_pos + t: o_pos + t for t in range(ni)})
        i_pos += ni
        o_pos += no
    kwargs = dict(grid=grid) if grid else {}
    if aliases:
        kwargs["input_output_aliases"] = aliases
    res = pl.pallas_call(
        wrapped, name=name, in_specs=list(in_specs) + [HBM] * sum(c_in), out_specs=list(out_specs) + [HBM] * sum(c_out),
        out_shape=list(out_shape) + [s for p in plans for s in p.out_shapes],
        scratch_shapes=list(scratch_shapes) + [pltpu.SemaphoreType.DMA((p.ncopies,)) for p in plans for _ in range(2)],
        compiler_params=_cp(*sem) if grid else pltpu.CompilerParams(vmem_limit_bytes=VMEM_LIMIT), **kwargs,
    )(*args, *[a for p in plans for a in p.ins])
    res = list(res)
    comp, rest = res[:n_out], res[n_out:]
    pouts = []
    for no in c_out:
        pouts.append(rest[:no])
        rest = rest[no:]
    return comp, pouts


def run_plans(plans, *, name):
    return call_with_plans(None, plans, grid=(), in_specs=[], out_specs=[], out_shape=[], scratch_shapes=[], args=[], sem=(), name=name)[1]


def _cat(parts, axis=-1):
    return jnp.concatenate(parts, axis=axis)


def _pairs_of_heads(a, axis, inverse=False):
    lead, tail = a.shape[:axis], a.shape[axis + 1:]
    split = (3, 2) if inverse else (2, 3)
    a = a.reshape(lead + split + (HEAD,) + tail)
    return jnp.swapaxes(a, axis, axis + 1).reshape(lead + (6 * HEAD,) + tail)


def _prep_w_in(w):
    z = lambda n: jnp.zeros((w.shape[0], n), w.dtype)
    return _cat([w[:, 0:1152], z(64), w[:, 1152:1184], z(32), _pairs_of_heads(w[:, 1184:1568], 1), w[:, 1568:1824]])


def _unprep_w_in(g):
    return _cat([g[:, 0:1152], g[:, 1216:1248], _pairs_of_heads(g[:, P_SWQ:P_SWK], 1, inverse=True), g[:, P_SWK:P_END]])


def _prep_w_uq(w):
    r = w.shape[0]
    return jnp.pad(w.reshape(r, 6, MLA_QK), ((0, 0), (0, 0), (0, LANES - MLA_QK))).reshape(r, 6 * LANES)


def _unprep_w_uq(g):
    r = g.shape[0]
    return g.reshape(r, 6, LANES)[:, :, :MLA_QK].reshape(r, 6 * MLA_QK)


def _prep_w_ukv(w):
    r = w.shape[0]
    w3 = w.reshape(r, 6, LANES)
    k = jnp.pad(w3[:, :, :HEAD], ((0, 0), (0, 0), (0, LANES - HEAD))).reshape(r, 6 * LANES)
    return _cat([k, w3[:, :, HEAD:].reshape(r, 6 * HEAD)])


def _unprep_w_ukv(g):
    r = g.shape[0]
    k = g[:, :6 * LANES].reshape(r, 6, LANES)[:, :, :HEAD]
    return _cat([k, g[:, 6 * LANES:].reshape(r, 6, HEAD)], axis=2).reshape(r, 6 * LANES)


def _prep_w_out(w):
    return _cat([w[0:640], _pairs_of_heads(w[640:], 0)], axis=0)


def _unprep_w_out(g):
    return _cat([g[0:640], _pairs_of_heads(g[640:], 0, inverse=True)], axis=0)


def _rope_tables(positions):
    half = 16
    inv_freq = jnp.power(ROPE_THETA, -jnp.arange(half, dtype=F32) / half)
    ang = positions.astype(F32)[..., None] * inv_freq
    cos, sin = jnp.cos(ang), jnp.sin(ang)
    z = lambda n: jnp.zeros(ang.shape[:-1] + (n,), F32)
    return (_cat([jnp.ones(ang.shape[:-1] + (HEAD,), F32), cos, cos, z(32)]), _cat([z(HEAD), -sin, z(16), z(32)]), _cat([z(HEAD), z(16), sin, z(32)]))


def _small_params(p):
    pad96 = lambda g: _cat([g, jnp.zeros((32,), F32)]).reshape(1, LANES)
    two = lambda g: _cat([g, g]).reshape(1, LANES)
    sinks = jnp.broadcast_to(p["sw_sinks"].reshape(2, 3).T[:, :, None], (3, 2, LANES))
    return dict(n1=p["norm1_g"].reshape(1, -1), n2=p["norm2_g"].reshape(1, -1), cq_g=p["mla_cq_g"].reshape(1, -1),
                ckv_g=p["mla_ckv_g"].reshape(1, -1), qn_g=pad96(p["mla_qn_g"]), kn_g=pad96(p["mla_kn_g"]),
                swq_g=two(p["sw_qn_g"]), swk_g=two(p["sw_kn_g"]), sinks=sinks, conv_b=_up_perm(p["conv_b"]).reshape(1, -1))


class _NoFlow:
    def plans(self, tag):
        return []

    def done(self, tag, outs):
        pass

    def add(self, key, g):
        pass


def _layer_fwd(x3, md, W, tabs, bias, tag, flow=_NoFlow()):
    Bl, S, D = x3.shape
    T = Bl * S
    n = lambda s: f"{s}_{tag}"
    two = lambda a: a.reshape(T, a.shape[-1])
    three = lambda a: a.reshape(Bl, S, a.shape[-1])
    h = rms_fwd(x3, 0, D, W["n1"], md["scale1"], md["shift1"], name=n("norm1"))
    proj = three(matmul(two(h), W["w_in"], tn=1920, name=n("in_proj")))
    (o_a, rt_a), got = sb_attn_fwd(proj, plans=flow.plans(n("sb_fwd")), name=n("sb_fwd"))
    flow.done(n("sb_fwd"), got)
    cqn = rms_fwd(proj, P_CQ // 256, 256, W["cq_g"], name=n("cq_norm"))
    ckvn = rms_fwd(proj, P_CKV // LANES, LANES, W["ckv_g"], name=n("ckv_norm"))
    qb = three(matmul(two(cqn), W["w_uq"], tm=1024, tn=768, name=n("uq")))
    kvb = three(matmul(two(ckvn), W["w_ukv"], tm=1024, tn=1152, name=n("ukv")))
    q_m = rope_norm_fwd(qb, 6, W["qn_g"], tabs, name=n("q_rope"))
    k_m = rope_norm_fwd(kvb, 6, W["kn_g"], tabs, (proj, P_SLAB // LANES), name=n("k_rope"))
    (o_b, lse_b), got = mla_attn_fwd(q_m, k_m, kvb, 6, plans=flow.plans(n("mla_fwd")), name=n("mla_fwd"))
    flow.done(n("mla_fwd"), got)
    q_c = pair_rms_fwd(proj, P_SWQ // LANES, 3, W["swq_g"], name=n("swq_norm"))
    k_c = pair_rms_fwd(proj, P_SWK // LANES, 1, W["swk_g"], name=n("swk_norm"))
    (o_c, lse_c), got = swa_attn_fwd(q_c, k_c, proj, bias, W["sinks"], plans=flow.plans(n("swa_fwd")), name=n("swa_fwd"))
    flow.done(n("swa_fwd"), got)
    mix = _cat([o_a, o_b, o_c]).astype(BF16)
    att, x1 = matmul_res(two(mix), W["w_out"], two(x3), md["gate1"], S, name=n("out_proj"))
    x1 = three(x1)
    h2 = rms_fwd(x1, 0, D, W["n2"], md["scale2"], md["shift2"], name=n("norm2"))
    up = three(matmul(two(h2), W["w_up"], tn=1408, name=n("up_proj")))
    a = conv_gate_fwd(up, W["conv_w"], W["conv_b"], name=n("conv_gate"))
    yd, x2 = matmul_res(two(a), W["w_down"], two(x1), md["gate2"], S, name=n("down_proj"))
    saved = dict(x=x3, h=h, proj=proj, rt_a=rt_a, cqn=cqn, ckvn=ckvn, qb=qb, kvb=kvb, q_m=q_m, k_m=k_m, o_b=o_b, lse_b=lse_b,
                 q_c=q_c, k_c=k_c, o_c=o_c, lse_c=lse_c, mix=mix, att=three(att), x1=x1, h2=h2, up=up, a=a, yd=three(yd))
    return three(x2), saved


def _layer_bwd(dx2, sv, md, W, tabs, bias, tag, flow=_NoFlow()):
    Bl, S, D = dx2.shape
    T = Bl * S
    n = lambda s: f"{s}_{tag}"
    two = lambda a: a.reshape(T, a.shape[-1])
    three = lambda a: a.reshape(Bl, S, a.shape[-1])
    g = {}
    dyb, dgate2 = gate_bwd(dx2, sv["yd"], md["gate2"], name=n("gate2_bwd"))
    da = three(matmul(two(dyb), W["w_down"], tb=True, tn=1408, name=n("down_dx")))
    g["w_down"] = matmul(two(sv["a"]), two(dyb), ta=True, tm=256, tn=1024, name=n("down_dw"))
    dup, dcw = conv_gate_bwd(sv["up"], W["conv_w"], W["conv_b"], da, name=n("conv_gate_bwd"))
    dh2 = three(matmul(two(dup), W["w_up"], tb=True, tn=1024, name=n("up_dx")))
    g["w_up"] = matmul(two(sv["h2"]), two(dup), ta=True, tn=1408, name=n("up_dw"))
    dx1, dn2, dsc2, dsh2 = rms_bwd(sv["x1"], 0, D, dh2, W["n2"], md["scale2"], dx2, name=n("norm2_bwd"))
    dmo, dgate1 = gate_bwd(dx1, sv["att"], md["gate1"], name=n("gate1_bwd"))
    dmix = three(matmul(two(dmo), W["w_out"], tb=True, tn=1024, out_dtype=BF16, name=n("out_dx")))
    g["w_out"] = matmul(two(sv["mix"]), two(dmo), ta=True, tn=1024, name=n("out_dw"))
    proj = sv["proj"]
    for k in ("w_down", "w_up", "w_out"):
        flow.add((tag, k), g[k])
    (dq_a, dk_a, dv_a), got = sb_attn_bwd(proj, sv["rt_a"], dmix[:, :, 0:256], plans=flow.plans(n("sb_bwd")), name=n("sb_bwd"))
    flow.done(n("sb_bwd"), got)
    dq_m, dk_m, dv_b = mla_attn_bwd(sv["q_m"], sv["k_m"], sv["kvb"], 6, sv["o_b"], sv["lse_b"], dmix[:, :, 256:640], name=n("mla_bwd"))
    dqb, dqn = rope_norm_bwd(sv["qb"], 6, dq_m, W["qn_g"], tabs, name=n("q_rope_bwd"))
    dkn_x, dkn, dslab = rope_norm_bwd(sv["kvb"], 6, dk_m, W["kn_g"], tabs, (proj, P_SLAB // LANES), name=n("k_rope_bwd"))
    dkvb = _cat([dkn_x, dv_b]).astype(BF16)
    dckvn = three(matmul(two(dkvb), W["w_ukv"], tb=True, tm=1024, name=n("ukv_dx")))
    g["w_ukv"] = matmul(two(sv["ckvn"]), two(dkvb), ta=True, tn=1152, name=n("ukv_dw"))
    dcqn = three(matmul(two(dqb), W["w_uq"], tb=True, tm=1024, name=n("uq_dx")))
    g["w_uq"] = matmul(two(sv["cqn"]), two(dqb), ta=True, tn=768, name=n("uq_dw"))
    dcq, dcq_g = rms_bwd(proj, P_CQ // 256, 256, dcqn, W["cq_g"], name=n("cq_norm_bwd"))
    dckv, dckv_g = rms_bwd(proj, P_CKV // LANES, LANES, dckvn, W["ckv_g"], name=n("ckv_norm_bwd"))
    dq_c, dk_c, dv_c, dbias, dsink = swa_attn_bwd(sv["q_c"], sv["k_c"], proj, bias, W["sinks"], sv["o_c"], sv["lse_c"], dmix[:, :, 640:1024], name=n("swa_bwd"))
    dswq, dswq_g = pair_rms_bwd(proj, P_SWQ // LANES, 3, dq_c, W["swq_g"], name=n("swq_norm_bwd"))
    dswk, dswk_g = pair_rms_bwd(proj, P_SWK // LANES, 1, dk_c, W["swk_g"], name=n("swk_norm_bwd"))
    dproj = _cat([dq_a, dk_a, dv_a, dcq, dckv, dslab, dswq, dswk, dv_c]).astype(BF16)
    dh = three(matmul(two(dproj), W["w_in"], tb=True, tn=1024, name=n("in_dx")))
    g["w_in"] = matmul(two(sv["h"]), two(dproj), ta=True, tn=1920, tk=2048, name=n("in_dw"))
    dx, dn1, dsc1, dsh1 = rms_bwd(sv["x"], 0, D, dh, W["n1"], md["scale1"], dx1, name=n("norm1_bwd"))
    small = dict(n1=dn1, n2=dn2, cq_g=dcq_g, ckv_g=dckv_g, qn_g=dqn, kn_g=dkn, swq_g=dswq_g, swk_g=dswk_g, conv=dcw)
    dmods = _cat([dsh1, dsc1, dgate1, dsh2, dsc2, dgate2]).reshape(Bl, 6 * D)
    for k in ("w_ukv", "w_uq", "w_in"):
        flow.add((tag, k), g[k])
    return dx, g, small, dmods, dbias, dsink


BIG = ("w_in", "w_uq", "w_ukv", "w_out", "w_up", "w_down")
ROW_SHARDED = ("w_out", "w_down")
PREP = dict(w_in=_prep_w_in, w_uq=_prep_w_uq, w_ukv=_prep_w_ukv, w_out=_prep_w_out, w_up=_up_perm, w_down=lambda w: w)
UNPREP = dict(w_in=_unprep_w_in, w_uq=_unprep_w_uq, w_ukv=_unprep_w_ukv, w_out=_unprep_w_out, w_up=_up_perm, w_down=lambda w: w)
NCHIPS = 4


def _local_step(x, target, positions, mods, Wl, rel_flat, fwd_flow=_NoFlow(), bwd_flow=_NoFlow()):
    Bl, S, D = x.shape
    L = len(Wl)
    tabs = _rope_tables(positions)
    bucket = _bucket_table()
    bias = swa_bias(rel_flat, bucket, name="swa_bias")
    mds = []
    for l in range(L):
        parts = [mods[l, :, D * k:D * (k + 1)].reshape(Bl, 1, D) for k in range(6)]
        mds.append(dict(zip(("shift1", "scale1", "gate1", "shift2", "scale2", "gate2"), parts)))
    saved = []
    h = x
    for l in range(L):
        h, sv = _layer_fwd(h, mds[l], Wl[l], tabs, bias, f"l{l}", fwd_flow)
        saved.append(sv)
    dy, loss = loss_grad(h, target, name="loss")
    grads, smalls, dmods, dbiases, dsinks = [None] * L, [None] * L, [None] * L, [None] * L, [None] * L
    for l in reversed(range(L)):
        dy, grads[l], smalls[l], dmods[l], dbiases[l], dsinks[l] = _layer_bwd(dy, saved[l], mds[l], Wl[l], tabs, bias, f"l{l}", bwd_flow)
    drel = swa_bias_bwd(_cat(dbiases, axis=0), bucket, name="swa_bias_bwd")
    return loss, dy, grads, smalls, dmods, dsinks, drel


ATT = ("w_in", "w_uq", "w_ukv", "w_out")
FFN = ("w_up", "w_down")
GATHER_STAGES = {
    "sb_fwd_l0": ([("l0", k) for k in FFN], []),
    "mla_fwd_l0": ([("l1", k) for k in ATT + ("w_up",)], [("l0", k) for k in FFN]),
    "swa_fwd_l0": ([("l1", "w_down")], [("l1", k) for k in ATT + ("w_up",)]),
    "sb_fwd_l1": ([], [("l1", "w_down")]),
}
SCATTER_STAGES = {
    "sb_bwd_l1": [("l1", k) for k in FFN],
    "sb_bwd_l0": [("l1", k) for k in ATT] + [("l0", k) for k in FFN],
}


class _GatherFlow:
    def __init__(self, shards, chip):
        self.shards, self.chip, self.ici, self.d2d, self.pending = shards, chip, {}, {}, {}

    def early(self, keys):
        ici, = run_plans([plan_gather_ici([self.shards[k] for k in keys])], name="gather_early_ici")
        d2d, = run_plans([plan_gather_d2d(ici)], name="gather_early_d2d")
        self.d2d.update(zip(keys, d2d))

    def plans(self, tag):
        ici_keys, d2d_keys = GATHER_STAGES.get(tag, ([], []))
        plans = []
        if d2d_keys:
            plans.append(plan_gather_d2d([self.ici[k] for k in d2d_keys]))
        if ici_keys:
            plans.append(plan_gather_ici([self.shards[k] for k in ici_keys]))
        self.pending[tag] = (ici_keys, d2d_keys)
        return plans

    def done(self, tag, outs):
        ici_keys, d2d_keys = self.pending.pop(tag, ([], []))
        outs = list(outs)
        if d2d_keys:
            self.d2d.update(zip(d2d_keys, outs.pop(0)))
        if ici_keys:
            self.ici.update(zip(ici_keys, outs.pop(0)))

    def weight(self, key):
        k = key[1]
        own = self.shards[key]
        r, cc = own.shape
        w4 = lax.dynamic_update_slice(self.d2d[key], own[None], (self.chip, 0, 0))
        fw = w4.reshape(NCHIPS * r, cc) if k in ROW_SHARDED else jnp.transpose(w4, (1, 0, 2)).reshape(r, NCHIPS * cc)
        return PREP[k](fw)


class _LayerWeights(dict):
    def __init__(self, small, flow, tag):
        super().__init__(small)
        self.flow, self.tag = flow, tag

    def __missing__(self, k):
        self[k] = self.flow.weight((self.tag, k))
        return self[k]


class _ScatterFlow:
    def __init__(self, shapes, sel, c_arr):
        self.shapes, self.sel, self.c_arr = shapes, sel, c_arr
        self.g, self.pairs, self.landed, self.pending = {}, {}, {}, {}

    def add(self, key, g):
        self.g[key] = g

    def _pairs(self, keys, label):
        g4s = []
        for key in keys:
            k = key[1]
            r, cc = self.shapes[k]
            gk = UNPREP[k](self.g[key])
            g4 = gk.reshape(NCHIPS, r, cc) if k in ROW_SHARDED else jnp.transpose(gk.reshape(r, NCHIPS, cc), (1, 0, 2))
            g4s.append(g4.astype(BF16))
        theirs, = run_plans([plan_swap_halves(g4s)], name=f"rs_swap_{label}")
        pairs = [pair_add_half(g4, th, self.c_arr, name=f"rs_pair_add_{key[1]}_{key[0]}") for key, g4, th in zip(keys, g4s, theirs)]
        self.pairs.update(zip(keys, pairs))
        return pairs

    def plans(self, tag):
        keys = SCATTER_STAGES.get(tag, [])
        self.pending[tag] = keys
        return [plan_scatter_ici(self._pairs(keys, tag))] if keys else []

    def done(self, tag, outs):
        keys = self.pending.pop(tag, [])
        if keys:
            self.landed.update(zip(keys, outs[0]))

    def finish(self):
        rest = [key for key in self.g if key not in self.pairs]
        if rest:
            landed, = run_plans([plan_scatter_ici(self._pairs(rest, "rest"))], name="rs_scatter_rest")
            self.landed.update(zip(rest, landed))
        keys = list(self.pairs)
        fulls = [chip_sum_into(self.landed[key], self.pairs[key], self.sel, name=f"rs_chip_sum_{key[1]}_{key[0]}") for key in keys]
        joined, = run_plans([plan_join_halves(fulls)], name="rs_join_halves")
        return dict(zip(keys, joined))


WEIGHTS = ("rel_table", "norm1_g", "norm2_g", "w_ada", "b_ada", "w_in", "mla_cq_g", "w_uq", "mla_ckv_g", "w_ukv", "mla_qn_g", "mla_kn_g",
           "sw_qn_g", "sw_kn_g", "sw_sinks", "w_out", "w_up", "conv_w", "conv_b", "w_down")
SMALL = tuple(n for n in WEIGHTS if n not in BIG + ("w_ada",))


def kernel(x, c, positions, rel_table, norm1_g, norm2_g, w_ada, b_ada, w_in, mla_cq_g, w_uq, mla_ckv_g, w_ukv, mla_qn_g, mla_kn_g, sw_qn_g, sw_kn_g, sw_sinks, w_out, w_up, conv_w, conv_b, w_down, loss_target, m_rel_table, m_norm1_g, m_norm2_g, m_w_ada, m_b_ada, m_w_in, m_mla_cq_g, m_w_uq, m_mla_ckv_g, m_w_ukv, m_mla_qn_g, m_mla_kn_g, m_sw_qn_g, m_sw_kn_g, m_sw_sinks, m_w_out, m_w_up, m_conv_w, m_conv_b, m_w_down, v_rel_table, v_norm1_g, v_norm2_g, v_w_ada, v_b_ada, v_w_in, v_mla_cq_g, v_w_uq, v_mla_ckv_g, v_w_ukv, v_mla_qn_g, v_mla_kn_g, v_sw_qn_g, v_sw_kn_g, v_sw_sinks, v_w_out, v_w_up, v_conv_w, v_conv_b, v_w_down):
    w = dict(rel_table=rel_table, norm1_g=norm1_g, norm2_g=norm2_g, w_ada=w_ada, b_ada=b_ada, w_in=w_in, mla_cq_g=mla_cq_g, w_uq=w_uq,
             mla_ckv_g=mla_ckv_g, w_ukv=w_ukv, mla_qn_g=mla_qn_g, mla_kn_g=mla_kn_g, sw_qn_g=sw_qn_g, sw_kn_g=sw_kn_g, sw_sinks=sw_sinks,
             w_out=w_out, w_up=w_up, conv_w=conv_w, conv_b=conv_b, w_down=w_down)
    m = dict(rel_table=m_rel_table, norm1_g=m_norm1_g, norm2_g=m_norm2_g, w_ada=m_w_ada, b_ada=m_b_ada, w_in=m_w_in, mla_cq_g=m_mla_cq_g,
             w_uq=m_w_uq, mla_ckv_g=m_mla_ckv_g, w_ukv=m_w_ukv, mla_qn_g=m_mla_qn_g, mla_kn_g=m_mla_kn_g, sw_qn_g=m_sw_qn_g,
             sw_kn_g=m_sw_kn_g, sw_sinks=m_sw_sinks, w_out=m_w_out, w_up=m_w_up, conv_w=m_conv_w, conv_b=m_conv_b, w_down=m_w_down)
    v = dict(rel_table=v_rel_table, norm1_g=v_norm1_g, norm2_g=v_norm2_g, w_ada=v_w_ada, b_ada=v_b_ada, w_in=v_w_in, mla_cq_g=v_mla_cq_g,
             w_uq=v_w_uq, mla_ckv_g=v_mla_ckv_g, w_ukv=v_w_ukv, mla_qn_g=v_mla_qn_g, mla_kn_g=v_mla_kn_g, sw_qn_g=v_sw_qn_g,
             sw_kn_g=v_sw_kn_g, sw_sinks=v_sw_sinks, w_out=v_w_out, w_up=v_w_up, conv_w=v_conv_w, conv_b=v_conv_b, w_down=v_w_down)
    Bl, S, D = x.shape
    L = norm1_g.shape[0]
    xi, yi, ci = _me()
    chip = 2 * xi + yi
    dev = 4 * xi + 2 * yi + ci
    ndev = 2 * NCHIPS

    shapes = {k: w[k].shape[1:] for k in BIG}
    shards = {(f"l{l}", k): w[k][l].astype(BF16) for l in range(L) for k in BIG}
    gflow = _GatherFlow(shards, chip)
    gflow.early([("l0", k) for k in ATT])

    cw_cols = conv_w.shape[2]
    c_got, cw_got = allgather8([c, conv_w.reshape(L * 3, cw_cols)], name="gather_cond")
    c_all = c_got.reshape(ndev * Bl, D)
    conv_full = jnp.transpose(cw_got[0::2].reshape(NCHIPS, L, 3, cw_cols), (1, 2, 0, 3)).reshape(L, 3, NCHIPS * cw_cols)
    E = w_ada.shape[2]
    b_cols = lax.dynamic_slice(b_ada, (0, chip * E), (L, E)).reshape(L, 1, E)
    mods_cols = mods_matmul(c_all, w_ada, b_cols, name="mods")
    mods_all, = allgather8([mods_cols.reshape(L * ndev * Bl, E)], name="gather_mods")
    mods_all = jnp.transpose(mods_all[0::2].reshape(NCHIPS, L, ndev * Bl, E), (1, 2, 0, 3)).reshape(L, ndev * Bl, NCHIPS * E)
    mods = lax.dynamic_slice(mods_all, (0, dev * Bl, 0), (L, Bl, NCHIPS * E))

    Wl = []
    for l in range(L):
        Wd = _small_params({k: w[k][l] for k in SMALL if k not in ("rel_table", "b_ada", "conv_w")})
        Wd["conv_w"] = _up_perm(conv_full[l])
        Wl.append(_LayerWeights(Wd, gflow, f"l{l}"))

    sflow = _ScatterFlow(shapes, jnp.stack([chip, ci]).astype(jnp.int32), ci.reshape(1).astype(jnp.int32))
    loss, dx, _, smalls, dmods, dsinks, drel = _local_step(x, loss_target, positions, mods, Wl, rel_table.reshape(-1), gflow, sflow)
    reduced = sflow.finish()
    grad = {k: jnp.stack([reduced[(f"l{l}", k)] for l in range(L)]) for k in BIG}

    vec_names = ("n1", "n2", "cq_g", "ckv_g", "qn_g", "kn_g", "swq_g", "swk_g")
    vecs = _cat([_cat([smalls[l][k] for k in vec_names], axis=1) for l in range(L)], axis=0)
    convs = _cat([smalls[l]["conv"][0:4] for l in range(L)], axis=0)
    dm = jnp.stack(dmods, axis=1).reshape(Bl * L, 6 * D)
    dsk = jnp.stack(dsinks, axis=1).reshape(Bl * L * 6, LANES)
    got = allgather8([vecs, convs, drel, loss, dm, dsk], name="gather_small_grads")
    seq = lambda a, rows: a.reshape(ndev * Bl, rows, a.shape[-1])
    vec_s, conv_s, rel_s, loss_s, dm_s, dsk_s = sum_small(list(got[:4]) + [seq(got[4], L), seq(got[5], L * 6)], name="sum_small_grads")
    dm_all = jnp.transpose(seq(got[4], L), (1, 0, 2))
    grad["w_ada"] = ada_grad(c_all, lax.dynamic_slice(dm_all, (0, 0, chip * E), (L, ndev * Bl, E)), name="ada_grad")
    grad["b_ada"] = dm_s
    grad["sw_sinks"] = jnp.transpose(dsk_s.reshape(L, 3, 2, LANES)[:, :, :, 0], (0, 2, 1)).reshape(L, 6)
    grad["rel_table"] = rel_s[:6, :REL_BUCKETS].T
    off = 0
    for k, name_, keep in zip(vec_names, ("norm1_g", "norm2_g", "mla_cq_g", "mla_ckv_g", "mla_qn_g", "mla_kn_g", "sw_qn_g", "sw_kn_g"),
                              (D, D, 256, LANES, MLA_QK, MLA_QK, HEAD, HEAD)):
        grad[name_] = vec_s[:, off:off + keep]
        off += smalls[0][k].shape[1]
    conv = _up_perm(conv_s.reshape(L, 4, 2 * D_FF))
    grad["conv_w"] = lax.dynamic_slice(conv[:, 0:3], (0, 0, chip * cw_cols), (L, 3, cw_cols))
    grad["conv_b"] = conv[:, 3]
    loss_out = loss_s[0, 0]

    delta, new_m, new_v = {}, {}, {}
    for k in BIG + ("w_ada",):
        delta[k], new_m[k], new_v[k] = adamw(w[k], grad[k], m[k], v[k], name=f"adamw_{k}")
    outs = adamw_small(*[[src[k] for k in SMALL] for src in (w, grad, m, v)], name="adamw_small")
    for dst, o in zip((delta, new_m, new_v), outs):
        dst.update(dict(zip(SMALL, o)))
    return (loss_out, dx, *[grad[k] for k in WEIGHTS], *[delta[k] for k in WEIGHTS], *[new_m[k] for k in WEIGHTS], *[new_v[k] for k in WEIGHTS])
```

```python
import functools
import math

import jax
import jax.numpy as jnp
from jax import lax
from jax.experimental import pallas as pl
from jax.experimental.pallas import tpu as pltpu

F32 = jnp.float32
BF16 = jnp.bfloat16
MESH = pl.DeviceIdType.MESH

EPS = 1e-6
NEG = -1e30
HEAD = 64
LANES = 128
MLA_QK = 96
ROPE_THETA = 10000.0
REL_BUCKETS = 32
REL_MAX_DIST = 128
WINDOW = 128
D_FF = 2816
ADAM_LR, ADAM_B1, ADAM_B2, ADAM_EPS, ADAM_WD, ADAM_STEP = 0.001, 0.9, 0.999, 1e-08, 0.01, 10

VMEM_LIMIT = 56 * 1024 * 1024
STRIP = 32
P_SBQ, P_SBK, P_SBV, P_CQ, P_CKV, P_SLAB, P_SWQ, P_SWK, P_SWV, P_END = 0, 256, 512, 768, 1024, 1152, 1280, 1664, 1792, 1920
SW_PERM = (0, 3, 1, 4, 2, 5)


def _cp(*sem):
    return pltpu.CompilerParams(dimension_semantics=sem, vmem_limit_bytes=VMEM_LIMIT)


def _dot(a, b):
    return jnp.dot(a, b, preferred_element_type=F32)


def _dot_nt(a, b):
    return lax.dot_general(a, b, (((1,), (1,)), ((), ())), preferred_element_type=F32)


def _dot_tn(a, b):
    return lax.dot_general(a, b, (((0,), (0,)), ((), ())), preferred_element_type=F32)


def _split_dot(x, u):
    hi = x.astype(BF16)
    lo = (x - hi.astype(F32)).astype(BF16)
    return _dot(hi, u) + _dot(lo, u)


def _lane_masks():
    lane = lax.broadcasted_iota(jnp.int32, (1, LANES), 1)
    return (lane < HEAD, lane >= HEAD)


def _tile(n, cap, align=128):
    if n <= cap:
        return n
    t = cap - cap % align
    while t >= align:
        if n % t == 0:
            return t
        t -= align
    return n


def matmul(a, b, *, ta=False, tb=False, out_dtype=F32, tm=512, tn=512, tk=8192, name):
    M, K = (a.shape[1], a.shape[0]) if ta else a.shape
    N = b.shape[0] if tb else b.shape[1]
    tm, tn, tk = _tile(M, tm), _tile(N, tn), _tile(K, tk)
    nk = K // tk

    def body(a_ref, b_ref, o_ref, *scratch):
        av = a_ref[...].astype(BF16)
        bv = b_ref[...].astype(BF16)
        if ta:
            part = _dot_tn(av, bv)
        elif tb:
            part = _dot_nt(av, bv)
        else:
            part = _dot(av, bv)
        if nk == 1:
            o_ref[...] = part.astype(out_dtype)
        else:
            acc_ref, = scratch
            k = pl.program_id(2)

            @pl.when(k == 0)
            def _():
                acc_ref[...] = part

            @pl.when(k > 0)
            def _():
                acc_ref[...] += part

            @pl.when(k == nk - 1)
            def _():
                o_ref[...] = acc_ref[...].astype(out_dtype)

    n_outer = nk == 1 and tn * b.dtype.itemsize > tm * a.dtype.itemsize
    ij = (lambda p, q: (q, p)) if n_outer else (lambda p, q: (p, q))
    a_map = (lambda p, q, k: (k, ij(p, q)[0])) if ta else (lambda p, q, k: (ij(p, q)[0], k))
    b_map = (lambda p, q, k: (ij(p, q)[1], k)) if tb else (lambda p, q, k: (k, ij(p, q)[1]))
    grid = (N // tn, M // tm, nk) if n_outer else (M // tm, N // tn, nk)
    return pl.pallas_call(
        body, name=name, grid=grid,
        in_specs=[pl.BlockSpec((tk, tm) if ta else (tm, tk), a_map), pl.BlockSpec((tn, tk) if tb else (tk, tn), b_map)],
        out_specs=pl.BlockSpec((tm, tn), lambda p, q, k: ij(p, q)),
        out_shape=jax.ShapeDtypeStruct((M, N), out_dtype),
        scratch_shapes=[] if nk == 1 else [pltpu.VMEM((tm, tn), F32)],
        compiler_params=_cp("parallel", "parallel", "arbitrary"),
    )(a, b)


def matmul_res(a, b, res, gate, seq, *, tm=512, tn=1024, name):
    M, K = a.shape
    N = b.shape[1]
    tm, tn = _tile(min(M, seq), tm), _tile(N, tn)
    per_seq = seq // tm

    def body(a_ref, b_ref, r_ref, g_ref, y_ref, x_ref):
        y = _dot(a_ref[...].astype(BF16), b_ref[...].astype(BF16))
        y_ref[...] = y
        x_ref[...] = r_ref[...] + g_ref[...] * y

    out = jax.ShapeDtypeStruct((M, N), F32)
    return pl.pallas_call(
        body, name=name, grid=(M // tm, N // tn),
        in_specs=[pl.BlockSpec((tm, K), lambda i, j: (i, 0)), pl.BlockSpec((K, tn), lambda i, j: (0, j)),
                  pl.BlockSpec((tm, tn), lambda i, j: (i, j)), pl.BlockSpec((None, 1, tn), lambda i, j: (lax.div(i, jnp.int32(per_seq)), 0, j))],
        out_specs=[pl.BlockSpec((tm, tn), lambda i, j: (i, j))] * 2,
        out_shape=[out, out], compiler_params=_cp("parallel", "parallel"),
    )(a, b, res, gate)


def rms_fwd(x3, blk, W, g, sc=None, sh=None, *, tm=512, name):
    Bl, S, _ = x3.shape
    tm = min(tm, S)
    mod = sc is not None

    def body(x_ref, g_ref, *rest):
        o_ref = rest[-1]
        x = x_ref[...]
        r = lax.rsqrt(jnp.mean(x * x, axis=-1, keepdims=True) + EPS)
        y = x * r * g_ref[...]
        if mod:
            y = y * (1.0 + rest[0][...]) + rest[1][...]
        o_ref[...] = y.astype(BF16)

    vec = pl.BlockSpec((None, 1, W), lambda b, s: (b, 0, 0))
    return pl.pallas_call(
        body, name=name, grid=(Bl, S // tm),
        in_specs=[pl.BlockSpec((None, tm, W), lambda b, s: (b, s, blk)), pl.BlockSpec((1, W), lambda b, s: (0, 0))] + ([vec, vec] if mod else []),
        out_specs=pl.BlockSpec((None, tm, W), lambda b, s: (b, s, 0)),
        out_shape=jax.ShapeDtypeStruct((Bl, S, W), BF16),
        compiler_params=_cp("parallel", "parallel"),
    )(x3, g, *([sc, sh] if mod else []))


def rms_bwd(x3, blk, W, dy3, g, sc=None, dres3=None, *, tm=256, name):
    Bl, S, _ = x3.shape
    tm = min(tm, S)
    mod = sc is not None
    res = dres3 is not None

    def body(*refs):
        x_ref, dy_ref, g_ref = refs[:3]
        k = 3
        sc_ref = dr_ref = None
        if mod:
            sc_ref = refs[k]
            k += 1
        if res:
            dr_ref = refs[k]
            k += 1
        dx_ref, dg_ref = refs[k], refs[k + 1]
        b, s = pl.program_id(0), pl.program_id(1)
        x = x_ref[...]
        dy = dy_ref[...].astype(F32)
        g = g_ref[...]
        r = lax.rsqrt(jnp.mean(x * x, axis=-1, keepdims=True) + EPS)
        n = x * r
        if mod:
            dsc_ref, dsh_ref = refs[k + 2], refs[k + 3]
            one_sc = 1.0 + sc_ref[...]

            @pl.when(s == 0)
            def _():
                dsc_ref[...] = jnp.zeros_like(dsc_ref)
                dsh_ref[...] = jnp.zeros_like(dsh_ref)

            dsh_ref[...] += jnp.sum(dy, axis=0, keepdims=True)
            dsc_ref[...] += jnp.sum(dy * n * g, axis=0, keepdims=True)
            dyn = dy * one_sc
        else:
            dyn = dy

        @pl.when((b == 0) & (s == 0))
        def _():
            dg_ref[...] = jnp.zeros_like(dg_ref)

        dg_ref[...] += jnp.sum(dyn * n, axis=0, keepdims=True)
        dn = dyn * g
        dx = r * (dn - n * jnp.mean(dn * n, axis=-1, keepdims=True))
        if res:
            dx = dx + dr_ref[...]
        dx_ref[...] = dx

    blkspec = pl.BlockSpec((None, tm, W), lambda b, s: (b, s, 0))
    vec = pl.BlockSpec((None, 1, W), lambda b, s: (b, 0, 0))
    row = pl.BlockSpec((1, W), lambda b, s: (0, 0))
    in_specs = [pl.BlockSpec((None, tm, W), lambda b, s: (b, s, blk)), blkspec, row] + ([vec] if mod else []) + ([blkspec] if res else [])
    out_specs = [blkspec, row] + ([vec, vec] if mod else [])
    out_shape = [jax.ShapeDtypeStruct((Bl, S, W), F32), jax.ShapeDtypeStruct((1, W), F32)]
    if mod:
        out_shape += [jax.ShapeDtypeStruct((Bl, 1, W), F32)] * 2
    args = [x3, dy3, g] + ([sc] if mod else []) + ([dres3] if res else [])
    return pl.pallas_call(
        body, name=name, grid=(Bl, S // tm), in_specs=in_specs, out_specs=out_specs, out_shape=out_shape,
        compiler_params=_cp("arbitrary", "arbitrary"),
    )(*args)


def pair_rms_fwd(x3, blk0, npairs, g2, *, tm=1024, name):
    Bl, S, _ = x3.shape
    tm = min(tm, S)

    def body(x_ref, g_ref, o_ref):
        lo, hi = _lane_masks()
        x = x_ref[...]
        xx = x * x
        s0 = jnp.sum(jnp.where(lo, xx, 0.0), axis=-1, keepdims=True)
        s1 = jnp.sum(jnp.where(hi, xx, 0.0), axis=-1, keepdims=True)
        r = jnp.where(lo, lax.rsqrt(s0 / HEAD + EPS), lax.rsqrt(s1 / HEAD + EPS))
        o_ref[...] = (x * r * g_ref[...]).astype(BF16)

    return pl.pallas_call(
        body, name=name, grid=(Bl, S // tm, npairs),
        in_specs=[pl.BlockSpec((None, tm, LANES), lambda b, s, p: (b, s, blk0 + p)), pl.BlockSpec((1, LANES), lambda b, s, p: (0, 0))],
        out_specs=pl.BlockSpec((None, tm, LANES), lambda b, s, p: (b, s, p)),
        out_shape=jax.ShapeDtypeStruct((Bl, S, LANES * npairs), BF16),
        compiler_params=_cp("parallel", "parallel", "parallel"),
    )(x3, g2)


def pair_rms_bwd(x3, blk0, npairs, dy3, g2, *, tm=1024, name):
    Bl, S, _ = x3.shape
    tm = min(tm, S)

    def body(x_ref, dy_ref, g_ref, dx_ref, dg_ref):
        lo, hi = _lane_masks()
        first = (pl.program_id(0) == 0) & (pl.program_id(1) == 0) & (pl.program_id(2) == 0)
        x = x_ref[...]
        dy = dy_ref[...]
        xx = x * x
        s0 = jnp.sum(jnp.where(lo, xx, 0.0), axis=-1, keepdims=True)
        s1 = jnp.sum(jnp.where(hi, xx, 0.0), axis=-1, keepdims=True)
        r = jnp.where(lo, lax.rsqrt(s0 / HEAD + EPS), lax.rsqrt(s1 / HEAD + EPS))
        n = x * r

        @pl.when(first)
        def _():
            dg_ref[...] = jnp.zeros_like(dg_ref)

        part = jnp.sum(dy * n, axis=0, keepdims=True)
        dg_ref[...] += part + pltpu.roll(part, HEAD, 1)
        dn = dy * g_ref[...]
        t = dn * n
        m0 = jnp.sum(jnp.where(lo, t, 0.0), axis=-1, keepdims=True)
        m1 = jnp.sum(jnp.where(hi, t, 0.0), axis=-1, keepdims=True)
        dx_ref[...] = r * (dn - n * (jnp.where(lo, m0, m1) / HEAD))

    return pl.pallas_call(
        body, name=name, grid=(Bl, S // tm, npairs),
        in_specs=[pl.BlockSpec((None, tm, LANES), lambda b, s, p: (b, s, blk0 + p)), pl.BlockSpec((None, tm, LANES), lambda b, s, p: (b, s, p)),
                  pl.BlockSpec((1, LANES), lambda b, s, p: (0, 0))],
        out_specs=[pl.BlockSpec((None, tm, LANES), lambda b, s, p: (b, s, p)), pl.BlockSpec((1, LANES), lambda b, s, p: (0, 0))],
        out_shape=[jax.ShapeDtypeStruct((Bl, S, LANES * npairs), F32), jax.ShapeDtypeStruct((1, LANES), F32)],
        compiler_params=_cp("arbitrary", "arbitrary", "arbitrary"),
    )(x3, dy3, g2)


def _rot(y, cos_t, sin_a, sin_b):
    return y * cos_t + pltpu.roll(y, LANES - 16, 1) * sin_a + pltpu.roll(y, 16, 1) * sin_b


def _rot_t(d, cos_t, sin_a, sin_b):
    return d * cos_t + pltpu.roll(d * sin_a, 16, 1) + pltpu.roll(d * sin_b, LANES - 16, 1)


def rope_norm_fwd(x3, nheads, g, tabs, slab=None, *, tm=1024, name):
    Bl, S, _ = x3.shape
    tm = min(tm, S)
    has_slab = slab is not None

    def body(*refs):
        x_ref, g_ref, c_ref, sa_ref, sb_ref = refs[:5]
        o_ref = refs[-1]
        x = x_ref[...]
        if has_slab:
            x = x + refs[5][...]
        r = lax.rsqrt(jnp.sum(x * x, axis=-1, keepdims=True) / MLA_QK + EPS)
        o_ref[...] = _rot(x * r * g_ref[...], c_ref[...], sa_ref[...], sb_ref[...]).astype(BF16)

    head = pl.BlockSpec((None, tm, LANES), lambda b, s, h: (b, s, h))
    tab = pl.BlockSpec((None, tm, LANES), lambda b, s, h: (b, s, 0))
    in_specs = [head, pl.BlockSpec((1, LANES), lambda b, s, h: (0, 0)), tab, tab, tab]
    args = [x3, g, *tabs]
    if has_slab:
        sblk = slab[1]
        in_specs.append(pl.BlockSpec((None, tm, LANES), lambda b, s, h: (b, s, sblk)))
        args.append(slab[0])
    return pl.pallas_call(
        body, name=name, grid=(Bl, S // tm, nheads), in_specs=in_specs, out_specs=head,
        out_shape=jax.ShapeDtypeStruct((Bl, S, LANES * nheads), BF16),
        compiler_params=_cp("parallel", "parallel", "parallel"),
    )(*args)


def rope_norm_bwd(x3, nheads, dy3, g, tabs, slab=None, *, tm=1024, name):
    Bl, S, _ = x3.shape
    tm = min(tm, S)
    has_slab = slab is not None

    def body(*refs):
        x_ref, dy_ref, g_ref, c_ref, sa_ref, sb_ref = refs[:6]
        k = 7 if has_slab else 6
        dx_ref, dg_ref = refs[k], refs[k + 1]
        h = pl.program_id(2)
        first = (pl.program_id(0) == 0) & (pl.program_id(1) == 0) & (h == 0)
        x = x_ref[...]
        if has_slab:
            x = x + refs[6][...]
        g = g_ref[...]
        r = lax.rsqrt(jnp.sum(x * x, axis=-1, keepdims=True) / MLA_QK + EPS)
        n = x * r
        d = _rot_t(dy_ref[...], c_ref[...], sa_ref[...], sb_ref[...])

        @pl.when(first)
        def _():
            dg_ref[...] = jnp.zeros_like(dg_ref)

        dg_ref[...] += jnp.sum(d * n, axis=0, keepdims=True)
        dn = d * g
        dx = r * (dn - n * (jnp.sum(dn * n, axis=-1, keepdims=True) / MLA_QK))
        dx_ref[...] = dx
        if has_slab:
            ds_ref = refs[k + 2]

            @pl.when(h == 0)
            def _():
                ds_ref[...] = dx

            @pl.when(h > 0)
            def _():
                ds_ref[...] += dx

    head = pl.BlockSpec((None, tm, LANES), lambda b, s, h: (b, s, h))
    tab = pl.BlockSpec((None, tm, LANES), lambda b, s, h: (b, s, 0))
    row = pl.BlockSpec((1, LANES), lambda b, s, h: (0, 0))
    in_specs = [head, head, row, tab, tab, tab]
    args = [x3, dy3, g, *tabs]
    out_specs = [head, row]
    out_shape = [jax.ShapeDtypeStruct((Bl, S, LANES * nheads), F32), jax.ShapeDtypeStruct((1, LANES), F32)]
    if has_slab:
        sblk = slab[1]
        in_specs.append(pl.BlockSpec((None, tm, LANES), lambda b, s, h: (b, s, sblk)))
        args.append(slab[0])
        out_specs.append(tab)
        out_shape.append(jax.ShapeDtypeStruct((Bl, S, LANES), F32))
    return pl.pallas_call(
        body, name=name, grid=(Bl, S // tm, nheads), in_specs=in_specs, out_specs=out_specs, out_shape=out_shape,
        compiler_params=_cp("arbitrary", "arbitrary", "arbitrary"),
    )(*args)


def _sb_tile(z, strict, u, carry_r):
    sp = jnp.maximum(z, 0.0) + jnp.log(1.0 + jnp.exp(-jnp.abs(z)))
    keep = jnp.where(strict, -sp, 0.0)
    logw = (z - sp) + _split_dot(keep, u) + carry_r
    return jnp.where(strict, jnp.exp(logw), 0.0), keep, sp


SB_BLOCK = 256
SB_QBLOCK = 512


def sb_attn_fwd(proj3, *, plans=None, name):
    Bl, S, _ = proj3.shape
    tk = min(SB_BLOCK, S)
    tq = min(SB_QBLOCK, S)
    per_q = tq // tk
    scale = HEAD ** -0.5
    qb, kb0, vb0 = P_SBQ // LANES, P_SBK // LANES, P_SBV // LANES

    def body(q_ref, k_ref, v_ref, o_ref, rt_ref):
        i = pl.program_id(2)
        masks = _lane_masks()
        lane = lax.broadcasted_iota(jnp.int32, (1, LANES), 1)
        q = q_ref[...]
        qh = [jnp.where(m, q, 0.0).astype(BF16) for m in masks]
        rr = lax.broadcasted_iota(jnp.int32, (tq, tk), 0)
        cc = lax.broadcasted_iota(jnp.int32, (tq, tk), 1)
        u = (lax.broadcasted_iota(jnp.int32, (tk, tk), 0) > lax.broadcasted_iota(jnp.int32, (tk, tk), 1)).astype(BF16)

        rt_ref[...] = jnp.zeros_like(rt_ref)

        def step(t, carry):
            r0, r1, acc = carry
            j = (i + 1) * per_q - 1 - t
            off = pl.multiple_of(j * tk, tk)
            kb = k_ref[pl.ds(off, tk), :].astype(BF16)
            vb = v_ref[pl.ds(off, tk), :]
            strict = (cc + j * tk) < (rr + i * tq)
            rt_ref[...] = jnp.where(lane == j, r0, jnp.where(lane == j + HEAD, r1, rt_ref[...]))
            rs = [r0, r1]
            for h in range(2):
                z = _dot_nt(qh[h], kb) * scale
                w, keep, _ = _sb_tile(z, strict, u, rs[h])
                acc = acc + _dot(w.astype(BF16), jnp.where(masks[h], vb, 0.0).astype(BF16))
                rs[h] = rs[h] + jnp.sum(keep, axis=1, keepdims=True)
            return rs[0], rs[1], acc

        zero = jnp.zeros((tq, 1), F32)
        _, _, acc = lax.fori_loop(0, (i + 1) * per_q, step, (zero, zero, jnp.zeros((tq, LANES), F32)))
        o_ref[...] = acc

    seq = lambda blk0: pl.BlockSpec((None, S, LANES), lambda b, p, i: (b, 0, blk0 + p))
    out = pl.BlockSpec((None, tq, LANES), lambda b, p, i: (b, i, p))
    shp = jax.ShapeDtypeStruct((Bl, S, 2 * LANES), F32)
    return call_with_plans(
        body, plans, name=name, grid=(Bl, 2, S // tq),
        in_specs=[pl.BlockSpec((None, tq, LANES), lambda b, p, i: (b, i, qb + p)), seq(kb0), seq(vb0)],
        out_specs=[out, out], out_shape=[shp, shp], scratch_shapes=[], args=[proj3, proj3, proj3],
        sem=("arbitrary",) * 3 if plans else ("parallel", "parallel", "arbitrary"))


def sb_attn_bwd(proj3, rt3, do3, *, plans=None, name):
    Bl, S, _ = proj3.shape
    tk = min(SB_BLOCK, S)
    tq = min(SB_QBLOCK, S)
    per_q = tq // tk
    scale = HEAD ** -0.5
    qb, kb0, vb0 = P_SBQ // LANES, P_SBK // LANES, P_SBV // LANES

    def body(q_ref, k_ref, v_ref, rt_ref, do_ref, dq_ref, dk_ref, dv_ref):
        i = pl.program_id(2)

        @pl.when(i == 0)
        def _():
            dk_ref[...] = jnp.zeros_like(dk_ref)
            dv_ref[...] = jnp.zeros_like(dv_ref)

        masks = _lane_masks()
        lane = lax.broadcasted_iota(jnp.int32, (1, LANES), 1)
        q = q_ref[...]
        qh = [jnp.where(m, q, 0.0).astype(BF16) for m in masks]
        do_b = do_ref[...].astype(BF16)
        doh = [jnp.where(m, do_b, jnp.zeros_like(do_b)) for m in masks]
        rt = rt_ref[...]
        rr = lax.broadcasted_iota(jnp.int32, (tq, tk), 0)
        cc = lax.broadcasted_iota(jnp.int32, (tq, tk), 1)
        ur = lax.broadcasted_iota(jnp.int32, (tk, tk), 0)
        uc = lax.broadcasted_iota(jnp.int32, (tk, tk), 1)
        u_suffix = (ur > uc).astype(BF16)
        u_prefix = (ur < uc).astype(BF16)

        def step(j, carry):
            p0, p1, dq = carry
            off = pl.multiple_of(j * tk, tk)
            kf = k_ref[pl.ds(off, tk), :]
            kb = kf.astype(BF16)
            vb = v_ref[pl.ds(off, tk), :]
            strict = (cc + j * tk) < (rr + i * tq)
            ps = [p0, p1]
            dk_acc = jnp.zeros((tk, LANES), F32)
            dv_acc = jnp.zeros((tk, LANES), F32)
            for h in range(2):
                r_j = jnp.sum(jnp.where(lane == j + h * HEAD, rt, 0.0), axis=1, keepdims=True)
                z = _dot_nt(qh[h], kb) * scale
                w, _, sp = _sb_tile(z, strict, u_suffix, r_j)
                vh = jnp.where(masks[h], vb, 0.0).astype(BF16)
                g = _dot_nt(doh[h], vh) * w
                pre = _split_dot(g, u_prefix) + ps[h]
                dz = jnp.where(strict, g * jnp.exp(-sp) - jnp.exp(z - sp) * pre, 0.0) * scale
                dzb = dz.astype(BF16)
                dq = dq + _dot(dzb, jnp.where(masks[h], kf, 0.0).astype(BF16))
                dk_acc = dk_acc + _dot_tn(dzb, qh[h])
                dv_acc = dv_acc + _dot_tn(w.astype(BF16), doh[h])
                ps[h] = ps[h] + jnp.sum(g, axis=1, keepdims=True)
            dk_ref[pl.ds(off, tk), :] += dk_acc
            dv_ref[pl.ds(off, tk), :] += dv_acc
            return ps[0], ps[1], dq

        zero = jnp.zeros((tq, 1), F32)
        out = lax.fori_loop(0, (i + 1) * per_q, step, (zero, zero, jnp.zeros((tq, LANES), F32)))
        dq_ref[...] = out[2]

    seq_in = lambda blk0: pl.BlockSpec((None, S, LANES), lambda b, p, i: (b, 0, blk0 + p))
    blk = pl.BlockSpec((None, tq, LANES), lambda b, p, i: (b, i, p))
    seq_out = pl.BlockSpec((None, S, LANES), lambda b, p, i: (b, 0, p))
    shp = jax.ShapeDtypeStruct((Bl, S, 2 * LANES), F32)
    return call_with_plans(
        body, plans, name=name, grid=(Bl, 2, S // tq),
        in_specs=[pl.BlockSpec((None, tq, LANES), lambda b, p, i: (b, i, qb + p)), seq_in(kb0), seq_in(vb0), blk, blk],
        out_specs=[blk, seq_out, seq_out], out_shape=[shp, shp, shp], scratch_shapes=[], args=[proj3, proj3, proj3, rt3, do3],
        sem=("arbitrary",) * 3 if plans else ("parallel", "parallel", "arbitrary"))


def mla_attn_fwd(q3, k3, kv3, vblk0, *, tq=512, tk=256, plans=None, name):
    Bl, S, _ = q3.shape
    tq = min(tq, S)
    tk = min(tk, tq)
    per_q = tq // tk
    scale = MLA_QK ** -0.5

    def body(q_ref, k_ref, v_ref, o_ref, lse_ref):
        i = pl.program_id(2)
        masks = _lane_masks()
        rr = lax.broadcasted_iota(jnp.int32, (tq, tk), 0)
        cc = lax.broadcasted_iota(jnp.int32, (tq, tk), 1)
        qh = [q_ref[:, h * LANES:(h + 1) * LANES] for h in range(2)]

        def step(j, carry):
            m0, l0, m1, l1, acc = carry
            off = pl.multiple_of(j * tk, tk)
            vb = v_ref[pl.ds(off, tk), :]
            causal = (cc + j * tk) <= (rr + i * tq)
            ms, ls, alphas = [m0, m1], [l0, l1], []
            add = jnp.zeros((tq, LANES), F32)
            for h in range(2):
                kh = k_ref[pl.ds(off, tk), h * LANES:(h + 1) * LANES]
                s = jnp.where(causal, _dot_nt(qh[h], kh) * scale, NEG)
                m_new = jnp.maximum(ms[h], jnp.max(s, axis=1, keepdims=True))
                p = jnp.exp(s - m_new)
                alpha = jnp.exp(ms[h] - m_new)
                ls[h] = alpha * ls[h] + jnp.sum(p, axis=1, keepdims=True)
                ms[h] = m_new
                alphas.append(alpha)
                add = add + _dot(p.astype(BF16), jnp.where(masks[h], vb, 0.0).astype(BF16))
            acc = acc * jnp.where(masks[0], alphas[0], alphas[1]) + add
            return ms[0], ls[0], ms[1], ls[1], acc

        neg = jnp.full((tq, 1), NEG, F32)
        zero = jnp.zeros((tq, 1), F32)
        m0, l0, m1, l1, acc = lax.fori_loop(0, (i + 1) * per_q, step, (neg, zero, neg, zero, jnp.zeros((tq, LANES), F32)))
        o_ref[...] = acc / jnp.where(masks[0], l0, l1)
        lse_ref[...] = jnp.where(masks[0], m0 + jnp.log(l0), m1 + jnp.log(l1))

    out = pl.BlockSpec((None, tq, LANES), lambda b, p, i: (b, i, p))
    shp = jax.ShapeDtypeStruct((Bl, S, 3 * LANES), F32)
    return call_with_plans(
        body, plans, name=name, grid=(Bl, 3, S // tq),
        in_specs=[pl.BlockSpec((None, tq, 2 * LANES), lambda b, p, i: (b, i, p)), pl.BlockSpec((None, S, 2 * LANES), lambda b, p, i: (b, 0, p)),
                  pl.BlockSpec((None, S, LANES), lambda b, p, i: (b, 0, vblk0 + p))],
        out_specs=[out, out], out_shape=[shp, shp], scratch_shapes=[], args=[q3, k3, kv3],
        sem=("arbitrary",) * 3 if plans else ("parallel", "parallel", "arbitrary"))


def mla_attn_bwd(q3, k3, kv3, vblk0, o3, lse3, do3, *, tq=512, tk=256, name):
    Bl, S, _ = q3.shape
    tq = min(tq, S)
    tk = min(tk, tq)
    per_q = tq // tk
    nq = S // tq
    scale = MLA_QK ** -0.5

    def body(q_ref, k_ref, v_ref, o_ref, lse_ref, do_ref, dq_ref, dk_ref, dv_ref, s_scr, dp_scr, p_scr, ds_scr):
        j = pl.program_id(2)

        @pl.when(j == 0)
        def _():
            dq_ref[...] = jnp.zeros_like(dq_ref)

        masks = _lane_masks()
        vb = v_ref[...]
        vh = [jnp.where(m, vb, 0.0).astype(BF16) for m in masks]
        kh = [k_ref[:, h * LANES:(h + 1) * LANES] for h in range(2)]
        i0 = lax.div(j, jnp.int32(per_q))

        def step(i, carry, masked):
            dk0, dk1, dv = carry
            off = pl.multiple_of(i * tq, tq)
            do_b = do_ref[pl.ds(off, tq), :].astype(BF16)
            prod = do_b.astype(F32) * o_ref[pl.ds(off, tq), :]
            lse = lse_ref[pl.ds(off, tq), :]
            dks = [dk0, dk1]
            for h in range(2):
                qh = q_ref[pl.ds(off, tq), h * LANES:(h + 1) * LANES]
                doh = jnp.where(masks[h], do_b, jnp.zeros_like(do_b))
                delta = jnp.sum(jnp.where(masks[h], prod, 0.0), axis=1, keepdims=True)
                lse_h = lse[:, h * HEAD:h * HEAD + 1]
                s_scr[...] = _dot_nt(qh, kh[h])
                dp_scr[...] = _dot_nt(doh, vh[h])
                for r0 in range(0, tq, STRIP):
                    rows = slice(r0, r0 + STRIP)
                    s = s_scr[rows, :] * scale
                    if masked:
                        rr = lax.broadcasted_iota(jnp.int32, (STRIP, tk), 0) + (i * tq + r0)
                        cc = lax.broadcasted_iota(jnp.int32, (STRIP, tk), 1) + j * tk
                        s = jnp.where(cc <= rr, s, NEG)
                    p = jnp.exp(s - lse_h[rows])
                    p_scr[rows, :] = p.astype(BF16)
                    ds_scr[rows, :] = (p * (dp_scr[rows, :] - delta[rows])).astype(BF16)
                ds = ds_scr[...]
                dq_ref[pl.ds(off, tq), h * LANES:(h + 1) * LANES] += _dot(ds, kh[h]) * scale
                dks[h] = dks[h] + _dot_tn(ds, qh)
                dv = dv + _dot_tn(p_scr[...], doh)
            return dks[0], dks[1], dv

        zero = jnp.zeros((tk, LANES), F32)
        carry = step(i0, (zero, zero, zero), True)
        dk0, dk1, dv = lax.fori_loop(i0 + 1, nq, lambda i, c: step(i, c, False), carry)
        dk_ref[:, 0:LANES] = dk0 * scale
        dk_ref[:, LANES:2 * LANES] = dk1 * scale
        dv_ref[...] = dv

    seq1 = pl.BlockSpec((None, S, LANES), lambda b, p, j: (b, 0, p))
    seq2 = pl.BlockSpec((None, S, 2 * LANES), lambda b, p, j: (b, 0, p))
    return pl.pallas_call(
        body, name=name, grid=(Bl, 3, S // tk),
        in_specs=[seq2, pl.BlockSpec((None, tk, 2 * LANES), lambda b, p, j: (b, j, p)),
                  pl.BlockSpec((None, tk, LANES), lambda b, p, j: (b, j, vblk0 + p)), seq1, seq1, seq1],
        out_specs=[seq2, pl.BlockSpec((None, tk, 2 * LANES), lambda b, p, j: (b, j, p)), pl.BlockSpec((None, tk, LANES), lambda b, p, j: (b, j, p))],
        out_shape=[jax.ShapeDtypeStruct((Bl, S, 6 * LANES), F32), jax.ShapeDtypeStruct((Bl, S, 6 * LANES), F32), jax.ShapeDtypeStruct((Bl, S, 3 * LANES), F32)],
        scratch_shapes=[pltpu.VMEM((tq, tk), F32), pltpu.VMEM((tq, tk), F32), pltpu.VMEM((tq, tk), BF16), pltpu.VMEM((tq, tk), BF16)],
        compiler_params=_cp("parallel", "parallel", "arbitrary"),
    )(q3, k3, kv3, o3, lse3, do3)


def _bucket_table():
    a = jnp.arange(WINDOW)[:, None]
    b = jnp.arange(2 * WINDOW)[None, :]
    dist = WINDOW + a - b
    max_exact = REL_BUCKETS // 2
    n = jnp.maximum(dist, 0)
    nf = jnp.maximum(n, 1).astype(F32)
    large = max_exact + (jnp.log(nf / max_exact) / math.log(REL_MAX_DIST / max_exact) * (REL_BUCKETS - max_exact)).astype(jnp.int32)
    large = jnp.minimum(large, REL_BUCKETS - 1)
    bucket = jnp.where(n < max_exact, n, large)
    return jnp.where((dist >= 0) & (dist < WINDOW), bucket, -1).astype(jnp.int32)


def swa_bias(rel_flat, bucket, *, name):
    def body(t_ref, b_ref, o_ref):
        bk = b_ref[...]
        for p in range(3):
            for hh in range(2):
                h = hh * 3 + p
                acc = jnp.full(bk.shape, NEG, F32)
                for b in range(REL_BUCKETS):
                    acc = jnp.where(bk == b, t_ref[b * 6 + h], acc)
                o_ref[p, hh] = acc

    return pl.pallas_call(
        body, name=name,
        in_specs=[pl.BlockSpec(memory_space=pltpu.SMEM), pl.BlockSpec(memory_space=pltpu.VMEM)],
        out_specs=pl.BlockSpec(memory_space=pltpu.VMEM),
        out_shape=jax.ShapeDtypeStruct((3, 2, WINDOW, 2 * WINDOW), F32),
    )(rel_flat, bucket)


def swa_bias_bwd(dbias, bucket, *, name):
    Bl = dbias.shape[0]

    def body(d_ref, b_ref, o_ref):
        bk = b_ref[...]
        lane = lax.broadcasted_iota(jnp.int32, (1, LANES), 1)
        rows = []
        for h in range(6):
            hh, p = divmod(h, 3)
            d = d_ref[0, p, hh]
            for bl in range(1, Bl):
                d = d + d_ref[bl, p, hh]
            row = jnp.zeros((1, LANES), F32)
            for b in range(REL_BUCKETS):
                s = jnp.sum(jnp.sum(jnp.where(bk == b, d, 0.0), axis=1, keepdims=True), axis=0, keepdims=True)
                row = row + jnp.where(lane == b, s, 0.0)
            rows.append(row)
        rows += [jnp.zeros((1, LANES), F32)] * 2
        o_ref[...] = jnp.concatenate(rows, axis=0)

    return pl.pallas_call(
        body, name=name,
        in_specs=[pl.BlockSpec(memory_space=pltpu.VMEM)] * 2, out_specs=pl.BlockSpec(memory_space=pltpu.VMEM),
        out_shape=jax.ShapeDtypeStruct((8, LANES), F32),
    )(dbias, bucket)


SWA_QBLOCKS = 4


def _swa_specs(vblk, nqb):
    rows = nqb * WINDOW
    cur = lambda blk: pl.BlockSpec((None, rows, LANES), lambda b, p, n: (b, n, blk))
    prev = lambda blk: pl.BlockSpec((None, WINDOW, LANES), lambda b, p, n: (b, jnp.maximum(n * nqb - 1, 0), blk))
    return [pl.BlockSpec((None, rows, LANES), lambda b, p, n: (b, n, p)), cur(0), prev(0), cur(vblk), prev(vblk),
            pl.BlockSpec((None, 2, WINDOW, 2 * WINDOW), lambda b, p, n: (p, 0, 0, 0)), pl.BlockSpec((None, 2, LANES), lambda b, p, n: (p, 0, 0))]


def _rows128(ref, m):
    return ref[m * WINDOW:(m + 1) * WINDOW, :]


def _swa_logits(qh, kp, kc, bias_h, first, scale):
    sp = jnp.where(first, NEG, _dot_nt(qh, kp) * scale + bias_h[:, :WINDOW])
    sc = _dot_nt(qh, kc) * scale + bias_h[:, WINDOW:]
    return sp, sc


def swa_attn_fwd(qn3, kn3, proj3, bias, sinks, *, plans=None, name):
    Bl, S, _ = qn3.shape
    scale = HEAD ** -0.5
    nqb = min(SWA_QBLOCKS, S // WINDOW)

    def body(q_ref, kc_ref, kp_ref, vc_ref, vp_ref, b_ref, s_ref, o_ref, lse_ref):
        seq_start = pl.program_id(2) == 0
        masks = _lane_masks()
        chains = [(m_, h) for m_ in range(nqb) for h in range(2)]
        kp = [kp_ref[...] if m_ == 0 else _rows128(kc_ref, m_ - 1) for m_ in range(nqb)]
        vp = [vp_ref[...] if m_ == 0 else _rows128(vc_ref, m_ - 1) for m_ in range(nqb)]
        kc = [_rows128(kc_ref, m_) for m_ in range(nqb)]
        vc = [_rows128(vc_ref, m_) for m_ in range(nqb)]
        sink = [s_ref[h:h + 1, 0:1] for h in range(2)]
        logits = {}
        for m_, h in chains:
            q = _rows128(q_ref, m_)
            qh = jnp.where(masks[h], q, jnp.zeros_like(q))
            logits[m_, h] = _swa_logits(qh, kp[m_], kc[m_], b_ref[h], seq_start if m_ == 0 else False, scale)
        mx = {c: jnp.maximum(jnp.maximum(jnp.max(logits[c][0], axis=1, keepdims=True), jnp.max(logits[c][1], axis=1, keepdims=True)), sink[c[1]])
              for c in chains}
        ex = {c: (jnp.exp(logits[c][0] - mx[c]), jnp.exp(logits[c][1] - mx[c])) for c in chains}
        den = {c: jnp.sum(ex[c][0], axis=1, keepdims=True) + jnp.sum(ex[c][1], axis=1, keepdims=True) + jnp.exp(sink[c[1]] - mx[c]) for c in chains}
        inv = {c: 1.0 / den[c] for c in chains}
        out = {}
        for m_, h in chains:
            c = (m_, h)
            out[c] = (_dot((ex[c][0] * inv[c]).astype(BF16), jnp.where(masks[h], vp[m_], 0.0).astype(BF16))
                      + _dot((ex[c][1] * inv[c]).astype(BF16), jnp.where(masks[h], vc[m_], 0.0).astype(BF16)))
        for m_ in range(nqb):
            o_ref[m_ * WINDOW:(m_ + 1) * WINDOW, :] = out[m_, 0] + out[m_, 1]
            lse_ref[m_ * WINDOW:(m_ + 1) * WINDOW, :] = jnp.where(masks[0], mx[m_, 0] + jnp.log(den[m_, 0]), mx[m_, 1] + jnp.log(den[m_, 1]))

    out = pl.BlockSpec((None, nqb * WINDOW, LANES), lambda b, p, n: (b, n, p))
    shp = jax.ShapeDtypeStruct((Bl, S, 3 * LANES), F32)
    return call_with_plans(
        body, plans, name=name, grid=(Bl, 3, S // (nqb * WINDOW)), in_specs=_swa_specs(P_SWV // LANES, nqb),
        out_specs=[out, out], out_shape=[shp, shp], scratch_shapes=[], args=[qn3, kn3, kn3, proj3, proj3, bias, sinks],
        sem=("arbitrary",) * 3 if plans else ("parallel", "parallel", "arbitrary"))


def swa_attn_bwd(qn3, kn3, proj3, bias, sinks, o3, lse3, do3, *, name):
    Bl, S, _ = qn3.shape
    scale = HEAD ** -0.5
    nqb = min(SWA_QBLOCKS, S // WINDOW)
    rows = nqb * WINDOW

    def body(q_ref, kc_ref, kp_ref, vc_ref, vp_ref, b_ref, s_ref, o_ref, lse_ref, do_ref,
             dq_ref, dk_ref, dv_ref, db_ref, dsk_ref):
        p_id, n = pl.program_id(1), pl.program_id(2)
        seq_start = n == 0

        @pl.when((p_id == 0) & seq_start)
        def _():
            dk_ref[...] = jnp.zeros_like(dk_ref)
            dv_ref[...] = jnp.zeros_like(dv_ref)

        @pl.when(seq_start)
        def _():
            db_ref[...] = jnp.zeros_like(db_ref)
            dsk_ref[...] = jnp.zeros_like(dsk_ref)

        masks = _lane_masks()
        zero = jnp.zeros((WINDOW, LANES), F32)
        chains = [(m_, h) for m_ in range(nqb) for h in range(2)]
        kp = [kp_ref[...] if m_ == 0 else _rows128(kc_ref, m_ - 1) for m_ in range(nqb)]
        vp = [vp_ref[...] if m_ == 0 else _rows128(vc_ref, m_ - 1) for m_ in range(nqb)]
        kc = [_rows128(kc_ref, m_) for m_ in range(nqb)]
        vc = [_rows128(vc_ref, m_) for m_ in range(nqb)]
        do_b = [_rows128(do_ref, m_).astype(BF16) for m_ in range(nqb)]
        prod = [do_b[m_].astype(F32) * _rows128(o_ref, m_) for m_ in range(nqb)]
        lse = [_rows128(lse_ref, m_) for m_ in range(nqb)]
        qh, doh, logits, lse_h, delta = {}, {}, {}, {}, {}
        for m_, h in chains:
            q = _rows128(q_ref, m_)
            qh[m_, h] = jnp.where(masks[h], q, jnp.zeros_like(q))
            doh[m_, h] = jnp.where(masks[h], do_b[m_], jnp.zeros_like(do_b[m_]))
            logits[m_, h] = _swa_logits(qh[m_, h], kp[m_], kc[m_], b_ref[h], seq_start if m_ == 0 else False, scale)
            lse_h[m_, h] = lse[m_][:, h * HEAD:h * HEAD + 1]
            delta[m_, h] = jnp.sum(jnp.where(masks[h], prod[m_], 0.0), axis=1, keepdims=True)
        pr = {c: (jnp.exp(logits[c][0] - lse_h[c]), jnp.exp(logits[c][1] - lse_h[c])) for c in chains}
        dp = {(m_, h): (_dot_nt(doh[m_, h], jnp.where(masks[h], vp[m_], 0.0).astype(BF16)),
                        _dot_nt(doh[m_, h], jnp.where(masks[h], vc[m_], 0.0).astype(BF16))) for m_, h in chains}
        ds = {c: (pr[c][0] * (dp[c][0] - delta[c]), pr[c][1] * (dp[c][1] - delta[c])) for c in chains}
        dsb = {c: ((ds[c][0] * scale).astype(BF16), (ds[c][1] * scale).astype(BF16)) for c in chains}
        dk_acc = [zero] * (nqb + 1)
        dv_acc = [zero] * (nqb + 1)
        db_acc = [[jnp.zeros((WINDOW, WINDOW), F32)] * 2 for _ in range(2)]
        dsk_acc = [jnp.zeros((1, 1), F32)] * 2
        dq = [zero] * nqb
        for m_, h in chains:
            c = (m_, h)
            db_acc[h] = [db_acc[h][0] + ds[c][0], db_acc[h][1] + ds[c][1]]
            dsk_acc[h] = dsk_acc[h] - jnp.sum(jnp.exp(s_ref[h:h + 1, 0:1] - lse_h[c]) * delta[c], axis=0, keepdims=True)
            dq[m_] = (dq[m_] + _dot(dsb[c][0], jnp.where(masks[h], kp[m_], jnp.zeros_like(kp[m_])))
                      + _dot(dsb[c][1], jnp.where(masks[h], kc[m_], jnp.zeros_like(kc[m_]))))
            dk_acc[m_] = dk_acc[m_] + _dot_tn(dsb[c][0], qh[c])
            dk_acc[m_ + 1] = dk_acc[m_ + 1] + _dot_tn(dsb[c][1], qh[c])
            dv_acc[m_] = dv_acc[m_] + _dot_tn(pr[c][0].astype(BF16), doh[c])
            dv_acc[m_ + 1] = dv_acc[m_ + 1] + _dot_tn(pr[c][1].astype(BF16), doh[c])
        for m_ in range(nqb):
            dq_ref[m_ * WINDOW:(m_ + 1) * WINDOW, :] = dq[m_]
        for h in range(2):
            db_ref[h, :, 0:WINDOW] += db_acc[h][0]
            db_ref[h, :, WINDOW:2 * WINDOW] += db_acc[h][1]
            dsk_ref[h:h + 1, :] += jnp.broadcast_to(dsk_acc[h], (1, LANES))
        offp = pl.multiple_of(jnp.maximum(n * nqb - 1, 0) * WINDOW, WINDOW)
        dk_ref[pl.ds(offp, WINDOW), :] += dk_acc[0]
        dv_ref[pl.ds(offp, WINDOW), :] += dv_acc[0]
        for m_ in range(nqb):
            off = pl.multiple_of(n * rows + m_ * WINDOW, WINDOW)
            dk_ref[pl.ds(off, WINDOW), :] += dk_acc[m_ + 1]
            dv_ref[pl.ds(off, WINDOW), :] += dv_acc[m_ + 1]

    blk = pl.BlockSpec((None, rows, LANES), lambda b, p, n: (b, n, p))
    seq = pl.BlockSpec((None, S, LANES), lambda b, p, n: (b, 0, 0))
    return pl.pallas_call(
        body, name=name, grid=(Bl, 3, S // rows), in_specs=_swa_specs(P_SWV // LANES, nqb) + [blk, blk, blk],
        out_specs=[blk, seq, seq, pl.BlockSpec((None, None, 2, WINDOW, 2 * WINDOW), lambda b, p, n: (b, p, 0, 0, 0)),
                   pl.BlockSpec((None, None, 2, LANES), lambda b, p, n: (b, p, 0, 0))],
        out_shape=[jax.ShapeDtypeStruct((Bl, S, 3 * LANES), F32), jax.ShapeDtypeStruct((Bl, S, LANES), F32), jax.ShapeDtypeStruct((Bl, S, LANES), F32),
                   jax.ShapeDtypeStruct((Bl, 3, 2, WINDOW, 2 * WINDOW), F32), jax.ShapeDtypeStruct((Bl, 3, 2, LANES), F32)],
        compiler_params=_cp("arbitrary", "arbitrary", "arbitrary"),
    )(qn3, kn3, kn3, proj3, proj3, bias, sinks, o3, lse3, do3)


CONV_ROWS = 64
CONV_LANES = 128


def _conv_strip(x_ref, h_ref, w, b, r0, cols, first_blk):
    x = x_ref[r0:r0 + CONV_ROWS, cols]
    if r0 == 0:
        rows = lax.broadcasted_iota(jnp.int32, x.shape, 0)
        h6 = jnp.where(first_blk, 0.0, h_ref[6:7, cols])
        h7 = jnp.where(first_blk, 0.0, h_ref[7:8, cols])
        x1 = jnp.where(rows == 0, h7, pltpu.roll(x, 1, 0))
        x2 = jnp.where(rows == 0, h6, jnp.where(rows == 1, h7, pltpu.roll(x, 2, 0)))
    else:
        x1 = x_ref[r0 - 1:r0 - 1 + CONV_ROWS, cols]
        x2 = x_ref[r0 - 2:r0 - 2 + CONV_ROWS, cols]
    return w[0:1] * x2 + w[1:2] * x1 + w[2:3] * x + b, x, x1, x2


FF_BLK = D_FF // 2


def _up_perm(a):
    q = FF_BLK
    return _cat([a[..., 0:q], a[..., 2 * q:3 * q], a[..., q:2 * q], a[..., 3 * q:4 * q]])


def conv_gate_fwd(up3, cw, cb, *, tm=256, name):
    Bl, S, _ = up3.shape
    tm = min(tm, S)
    W = 2 * FF_BLK

    def body(x_ref, h_ref, w_ref, b_ref, o_ref):
        first = pl.program_id(1) == 0

        def chunk(c, carry):
            cg = pl.ds(pl.multiple_of(c * CONV_LANES, CONV_LANES), CONV_LANES)
            cv = pl.ds(pl.multiple_of(FF_BLK + c * CONV_LANES, CONV_LANES), CONV_LANES)
            wg, wv, bg, bv = w_ref[:, cg], w_ref[:, cv], b_ref[:, cg], b_ref[:, cv]
            for r0 in range(0, tm, CONV_ROWS):
                ug = _conv_strip(x_ref, h_ref, wg, bg, r0, cg, first)[0]
                uv = _conv_strip(x_ref, h_ref, wv, bv, r0, cv, first)[0]
                o_ref[r0:r0 + CONV_ROWS, cg] = (ug * jax.nn.sigmoid(ug) * uv).astype(BF16)
            return carry

        lax.fori_loop(0, FF_BLK // CONV_LANES, chunk, 0)

    hb = tm // 8
    return pl.pallas_call(
        body, name=name, grid=(Bl, S // tm, 2),
        in_specs=[pl.BlockSpec((None, tm, W), lambda b, s, c: (b, s, c)),
                  pl.BlockSpec((None, 8, W), lambda b, s, c: (b, jnp.maximum(s * hb - 1, 0), c)),
                  pl.BlockSpec((3, W), lambda b, s, c: (0, c)), pl.BlockSpec((1, W), lambda b, s, c: (0, c))],
        out_specs=pl.BlockSpec((None, tm, FF_BLK), lambda b, s, c: (b, s, c)),
        out_shape=jax.ShapeDtypeStruct((Bl, S, D_FF), BF16),
        compiler_params=_cp("parallel", "parallel", "parallel"),
    )(up3, up3, cw, cb)


def conv_gate_bwd(up3, cw, cb, da3, *, tm=256, name):
    Bl, S, _ = up3.shape
    tm = min(tm, S)
    ns = S // tm
    W = 2 * FF_BLK

    def body(x_ref, h_ref, w_ref, b_ref, da_ref, dup_ref, dw_ref, nxt_ref, du_scr):
        b, s = pl.program_id(1), pl.program_id(2)
        seq_end = s == 0
        first = s == ns - 1

        @pl.when((b == 0) & seq_end)
        def _():
            dw_ref[...] = jnp.zeros_like(dw_ref)

        def du_chunk(c, carry):
            cg = pl.ds(pl.multiple_of(c * CONV_LANES, CONV_LANES), CONV_LANES)
            cv = pl.ds(pl.multiple_of(FF_BLK + c * CONV_LANES, CONV_LANES), CONV_LANES)
            wg, wv, bg, bv = w_ref[:, cg], w_ref[:, cv], b_ref[:, cg], b_ref[:, cv]
            acc_g = [jnp.zeros((1, CONV_LANES), F32)] * 4
            acc_v = [jnp.zeros((1, CONV_LANES), F32)] * 4
            for r0 in range(0, tm, CONV_ROWS):
                ug, xg, xg1, xg2 = _conv_strip(x_ref, h_ref, wg, bg, r0, cg, first)
                uv, xv, xv1, xv2 = _conv_strip(x_ref, h_ref, wv, bv, r0, cv, first)
                da = da_ref[r0:r0 + CONV_ROWS, cg].astype(F32)
                sg = jax.nn.sigmoid(ug)
                dug = da * uv * sg * (1.0 + ug * (1.0 - sg))
                duv = da * ug * sg
                du_scr[r0:r0 + CONV_ROWS, cg] = dug
                du_scr[r0:r0 + CONV_ROWS, cv] = duv
                col = lambda t: jnp.sum(t, axis=0, keepdims=True)
                acc_g = [acc_g[0] + col(dug * xg2), acc_g[1] + col(dug * xg1), acc_g[2] + col(dug * xg), acc_g[3] + col(dug)]
                acc_v = [acc_v[0] + col(duv * xv2), acc_v[1] + col(duv * xv1), acc_v[2] + col(duv * xv), acc_v[3] + col(duv)]
            for t in range(4):
                dw_ref[t:t + 1, cg] += acc_g[t]
                dw_ref[t:t + 1, cv] += acc_v[t]
            return carry

        lax.fori_loop(0, FF_BLK // CONV_LANES, du_chunk, 0)
        du_scr[tm:tm + 8, :] = jnp.where(seq_end, 0.0, nxt_ref[...])

        def dup_chunk(c, carry):
            cols = pl.ds(pl.multiple_of(c * CONV_LANES, CONV_LANES), CONV_LANES)
            w = w_ref[:, cols]
            for r0 in range(0, tm, CONV_ROWS):
                d0 = du_scr[r0:r0 + CONV_ROWS, cols]
                d1 = du_scr[r0 + 1:r0 + 1 + CONV_ROWS, cols]
                d2 = du_scr[r0 + 2:r0 + 2 + CONV_ROWS, cols]
                dup_ref[r0:r0 + CONV_ROWS, cols] = (w[2:3] * d0 + w[1:2] * d1 + w[0:1] * d2).astype(BF16)
            return carry

        lax.fori_loop(0, W // CONV_LANES, dup_chunk, 0)
        nxt_ref[...] = du_scr[0:8, :]

    hb = tm // 8
    rb = lambda s: ns - 1 - s
    return pl.pallas_call(
        body, name=name, grid=(2, Bl, ns),
        in_specs=[pl.BlockSpec((None, tm, W), lambda c, b, s: (b, rb(s), c)),
                  pl.BlockSpec((None, 8, W), lambda c, b, s: (b, jnp.maximum(rb(s) * hb - 1, 0), c)),
                  pl.BlockSpec((3, W), lambda c, b, s: (0, c)), pl.BlockSpec((1, W), lambda c, b, s: (0, c)),
                  pl.BlockSpec((None, tm, FF_BLK), lambda c, b, s: (b, rb(s), c))],
        out_specs=[pl.BlockSpec((None, tm, W), lambda c, b, s: (b, rb(s), c)), pl.BlockSpec((8, W), lambda c, b, s: (0, c))],
        out_shape=[jax.ShapeDtypeStruct((Bl, S, 2 * D_FF), BF16), jax.ShapeDtypeStruct((8, 2 * D_FF), F32)],
        scratch_shapes=[pltpu.VMEM((8, W), F32), pltpu.VMEM((tm + 8, W), F32)],
        compiler_params=_cp("arbitrary", "arbitrary", "arbitrary"),
    )(up3, up3, cw, cb, da3)


def gate_bwd(dx3, y3, gate, *, tm=512, name):
    Bl, S, D = dx3.shape
    tm = min(tm, S)

    def body(dx_ref, y_ref, g_ref, o_ref, dg_ref):
        @pl.when(pl.program_id(1) == 0)
        def _():
            dg_ref[...] = jnp.zeros_like(dg_ref)

        dx = dx_ref[...]
        dg_ref[...] += jnp.sum(dx * y_ref[...], axis=0, keepdims=True)
        o_ref[...] = (dx * g_ref[...]).astype(BF16)

    blk = pl.BlockSpec((None, tm, D), lambda b, s: (b, s, 0))
    vec = pl.BlockSpec((None, 1, D), lambda b, s: (b, 0, 0))
    return pl.pallas_call(
        body, name=name, grid=(Bl, S // tm), in_specs=[blk, blk, vec], out_specs=[blk, vec],
        out_shape=[jax.ShapeDtypeStruct((Bl, S, D), BF16), jax.ShapeDtypeStruct((Bl, 1, D), F32)],
        compiler_params=_cp("parallel", "arbitrary"),
    )(dx3, y3, gate)


def loss_grad(y3, t3, *, tm=512, name):
    Bl, S, D = y3.shape
    tm = min(tm, S)
    last = (Bl - 1, S // tm - 1)

    def body(y_ref, t_ref, dy_ref, l_ref, acc_ref):
        b, s = pl.program_id(0), pl.program_id(1)

        @pl.when((b == 0) & (s == 0))
        def _():
            acc_ref[...] = jnp.zeros_like(acc_ref)

        e = y_ref[...] - t_ref[...]
        dy_ref[...] = e * (1.0 / D)
        acc_ref[...] += jnp.sum(e * e, axis=0, keepdims=True)

        @pl.when((b == last[0]) & (s == last[1]))
        def _():
            l_ref[...] = jnp.broadcast_to(jnp.sum(acc_ref[...], axis=1, keepdims=True) * (0.5 / D), (1, LANES))

    blk = pl.BlockSpec((None, tm, D), lambda b, s: (b, s, 0))
    return pl.pallas_call(
        body, name=name, grid=(Bl, S // tm), in_specs=[blk, blk],
        out_specs=[blk, pl.BlockSpec((1, LANES), lambda b, s: (0, 0))],
        out_shape=[jax.ShapeDtypeStruct((Bl, S, D), F32), jax.ShapeDtypeStruct((1, LANES), F32)],
        scratch_shapes=[pltpu.VMEM((1, D), F32)], compiler_params=_cp("arbitrary", "arbitrary"),
    )(y3, t3)


def adamw(w, g, m, v, *, name):
    L, R, C = w.shape
    tr = _tile(R, 512, 8)

    def body(w_ref, g_ref, m_ref, v_ref, d_ref, m2_ref, v2_ref):
        d_ref[...], m2_ref[...], v2_ref[...] = _adam_update(w_ref[...], g_ref[...], m_ref[...], v_ref[...])

    blk = pl.BlockSpec((None, tr, C), lambda l, i: (l, i, 0))
    shp = jax.ShapeDtypeStruct((L, R, C), F32)
    return pl.pallas_call(
        body, name=name, grid=(L, R // tr), in_specs=[blk] * 4, out_specs=[blk] * 3, out_shape=[shp] * 3,
        compiler_params=_cp("parallel", "parallel"),
    )(w, g, m, v)


def sum_leading(x, *, out_dtype=F32, tr=256, name):
    n, R, C = x.shape
    tr = _tile(R, tr, 16)

    def body(x_ref, o_ref):
        acc = x_ref[0].astype(F32)
        for k in range(1, n):
            acc = acc + x_ref[k].astype(F32)
        o_ref[...] = acc.astype(out_dtype)

    return pl.pallas_call(
        body, name=name, grid=(R // tr,), in_specs=[pl.BlockSpec((n, tr, C), lambda i: (0, i, 0))],
        out_specs=pl.BlockSpec((tr, C), lambda i: (i, 0)), out_shape=jax.ShapeDtypeStruct((R, C), out_dtype),
        compiler_params=_cp("parallel"),
    )(x)


def _adam_update(w, g, m, v):
    c1 = 1.0 / (1.0 - ADAM_B1 ** ADAM_STEP)
    c2 = 1.0 / (1.0 - ADAM_B2 ** ADAM_STEP)
    m2 = ADAM_B1 * m + (1.0 - ADAM_B1) * g
    v2 = ADAM_B2 * v + (1.0 - ADAM_B2) * (g * g)
    return -ADAM_LR * ((m2 * c1) / (jnp.sqrt(v2 * c2) + ADAM_EPS) + ADAM_WD * w), m2, v2


def adamw_small(ws, gs, ms, vs, *, name):
    na = len(ws)

    def body(*refs):
        w_r, g_r, m_r, v_r = (refs[i * na:(i + 1) * na] for i in range(4))
        d_r, m2_r, v2_r = (refs[(4 + i) * na:(5 + i) * na] for i in range(3))
        for a in range(na):
            d_r[a][...], m2_r[a][...], v2_r[a][...] = _adam_update(w_r[a][...], g_r[a][...], m_r[a][...], v_r[a][...])

    vm = pl.BlockSpec(memory_space=pltpu.VMEM)
    shp = [jax.ShapeDtypeStruct(w.shape, F32) for w in ws]
    out = pl.pallas_call(body, name=name, in_specs=[vm] * (4 * na), out_specs=[vm] * (3 * na), out_shape=shp * 3)(*ws, *gs, *ms, *vs)
    return out[:na], out[na:2 * na], out[2 * na:]


def sum_small(xs, *, name):
    na = len(xs)

    def body(*refs):
        for x_ref, o_ref in zip(refs[:na], refs[na:]):
            acc = x_ref[0]
            for k in range(1, x_ref.shape[0]):
                acc = acc + x_ref[k]
            o_ref[...] = acc

    vm = pl.BlockSpec(memory_space=pltpu.VMEM)
    return pl.pallas_call(body, name=name, in_specs=[vm] * na, out_specs=[vm] * na,
                          out_shape=[jax.ShapeDtypeStruct(x.shape[1:], x.dtype) for x in xs])(*xs)


def pair_add_half(g4, recv, c_arr, *, tr=512, name):
    _, R, C = g4.shape
    H = R // 2
    tr = _tile(H, tr, 16)
    nb = H // tr

    def body(c_ref, g_ref, r_ref, o_ref):
        o_ref[...] = (g_ref[...].astype(F32) + r_ref[...].astype(F32)).astype(BF16)

    grid_spec = pltpu.PrefetchScalarGridSpec(
        num_scalar_prefetch=1, grid=(4, nb),
        in_specs=[pl.BlockSpec((None, tr, C), lambda k, i, c_ref: (k, c_ref[0] * nb + i, 0)),
                  pl.BlockSpec((None, tr, C), lambda k, i, c_ref: (k, i, 0))],
        out_specs=pl.BlockSpec((None, tr, C), lambda k, i, c_ref: (k, i, 0)),
    )
    return pl.pallas_call(
        body, name=name, grid_spec=grid_spec, out_shape=jax.ShapeDtypeStruct((4, H, C), BF16),
        compiler_params=_cp("parallel", "parallel"),
    )(c_arr, g4, recv)


def chip_sum_into(landed, pair, sel, *, tr=512, name):
    _, H, C = landed.shape
    tr = _tile(H, tr, 16)
    nb = H // tr

    def body(s_ref, l0, l1, l2, l3, p_ref, o_ref):
        own = p_ref[...].astype(F32)
        acc = None
        for k, l_ref in enumerate((l0, l1, l2, l3)):
            part = jnp.where(s_ref[0] == k, own, l_ref[...].astype(F32))
            acc = part if acc is None else acc + part
        o_ref[...] = acc

    def slot(k):
        return pl.BlockSpec((None, tr, C), lambda i, s: (jnp.where(s[0] == k, (k + 1) % 4, k), i, 0))

    grid_spec = pltpu.PrefetchScalarGridSpec(
        num_scalar_prefetch=1, grid=(nb,),
        in_specs=[slot(0), slot(1), slot(2), slot(3), pl.BlockSpec((None, tr, C), lambda i, s: (s[0], i, 0))],
        out_specs=pl.BlockSpec((tr, C), lambda i, s: (s[1] * nb + i, 0)),
    )
    return pl.pallas_call(
        body, name=name, grid_spec=grid_spec, out_shape=jax.ShapeDtypeStruct((2 * H, C), F32), compiler_params=_cp("parallel"),
    )(sel, landed, landed, landed, landed, pair)


def mods_matmul(c_all, w_ada, b_ada_cols, *, tn=512, name):
    L, D, E = w_ada.shape
    nb = c_all.shape[0]
    tn = _tile(E, tn)

    def body(c_ref, w_ref, b_ref, o_ref):
        c = c_ref[...]
        a = c * jax.nn.sigmoid(c)
        o_ref[...] = jnp.dot(a, w_ref[...], preferred_element_type=F32, precision=lax.Precision.HIGHEST) + b_ref[...]

    return pl.pallas_call(
        body, name=name, grid=(L, E // tn),
        in_specs=[pl.BlockSpec((nb, D), lambda l, j: (0, 0)), pl.BlockSpec((None, D, tn), lambda l, j: (l, 0, j)),
                  pl.BlockSpec((None, 1, tn), lambda l, j: (l, 0, j))],
        out_specs=pl.BlockSpec((None, nb, tn), lambda l, j: (l, 0, j)),
        out_shape=jax.ShapeDtypeStruct((L, nb, E), F32), compiler_params=_cp("parallel", "parallel"),
    )(c_all, w_ada, b_ada_cols)


def ada_grad(c_all, dmods, *, tn=512, name):
    L, nb, E = dmods.shape
    D = c_all.shape[1]
    tn = _tile(E, tn)

    def body(c_ref, d_ref, o_ref):
        c = c_ref[...]
        a = c * jax.nn.sigmoid(c)
        o_ref[...] = lax.dot_general(a, d_ref[...], (((0,), (0,)), ((), ())), preferred_element_type=F32, precision=lax.Precision.HIGHEST)

    return pl.pallas_call(
        body, name=name, grid=(L, E // tn),
        in_specs=[pl.BlockSpec((nb, D), lambda l, j: (0, 0)), pl.BlockSpec((None, nb, tn), lambda l, j: (l, 0, j))],
        out_specs=pl.BlockSpec((None, D, tn), lambda l, j: (l, 0, j)),
        out_shape=jax.ShapeDtypeStruct((L, D, E), F32), compiler_params=_cp("parallel", "parallel"),
    )(c_all, dmods)


HBM = pl.BlockSpec(memory_space=pltpu.HBM)


def _me():
    return lax.axis_index("x"), lax.axis_index("y"), lax.axis_index("c")


def _flip(v, bit):
    return 1 - v if bit else v


def allgather8(xs, *, name):
    na = len(xs)

    def body(*refs):
        x_refs, out_refs = refs[:na], refs[na:2 * na]
        send_sems, recv_sems = refs[2 * na], refs[2 * na + 1]
        x, y, c = _me()
        me = 4 * x + 2 * y + c
        for x_ref, out_ref in zip(x_refs, out_refs):
            out_ref[me] = x_ref[...]
        sends = []
        for a, (x_ref, out_ref) in enumerate(zip(x_refs, out_refs)):
            for k in range(1, 8):
                peer = (_flip(x, k & 4), _flip(y, k & 2), _flip(c, k & 1))
                cp = pltpu.make_async_remote_copy(src_ref=x_ref, dst_ref=out_ref.at[me], send_sem=send_sems.at[a, k - 1],
                                                  recv_sem=recv_sems.at[a, k - 1], device_id=peer, device_id_type=MESH)
                cp.start()
                sends.append(cp)
        for a, (x_ref, out_ref) in enumerate(zip(x_refs, out_refs)):
            for k in range(1, 8):
                peer = (_flip(x, k & 4), _flip(y, k & 2), _flip(c, k & 1))
                src = 4 * peer[0] + 2 * peer[1] + peer[2]
                pltpu.make_async_remote_copy(src_ref=x_ref, dst_ref=out_ref.at[src], send_sem=send_sems.at[a, k - 1],
                                             recv_sem=recv_sems.at[a, k - 1], device_id=peer, device_id_type=MESH).wait_recv()
        for cp in sends:
            cp.wait_send()

    vm = pl.BlockSpec(memory_space=pltpu.VMEM)
    return pl.pallas_call(
        body, name=name, in_specs=[vm] * na, out_specs=[vm] * na,
        out_shape=[jax.ShapeDtypeStruct((8,) + a.shape, a.dtype) for a in xs],
        scratch_shapes=[pltpu.SemaphoreType.DMA((na, 7)), pltpu.SemaphoreType.DMA((na, 7))],
    )(*xs)


LOCAL_CHUNKS = 8


def _copy_via_vmem(src, dst_at, rows, buf, sem):
    ch = buf.shape[0]
    for i in range(rows // ch):
        load = pltpu.make_async_copy(src.at[pl.ds(i * ch, ch)], buf, sem)
        load.start()
        load.wait()
        store = pltpu.make_async_copy(buf, dst_at(i * ch, ch), sem)
        store.start()
        store.wait()


def _chunk_buf(rows, cols, dtype):
    align = 16 if dtype == BF16 else 8
    for n in range(LOCAL_CHUNKS, 0, -1):
        if rows % n == 0 and (rows // n) % align == 0:
            return pltpu.VMEM((rows // n, cols), dtype)
    return pltpu.VMEM((rows, cols), dtype)


def gather_weights(ws, *, name):
    na = len(ws)

    def body(*refs):
        x_refs, out_refs = refs[:na], refs[na:2 * na]
        send_sems, recv_sems, local_sem = refs[2 * na:2 * na + 3]
        bufs = refs[2 * na + 3:]
        x, y, c = _me()
        j = 2 * x + y
        chips = [(_flip(x, k & 2), _flip(y, k & 1)) for k in range(1, 4)]
        sends = []
        for a, (x_ref, out_ref) in enumerate(zip(x_refs, out_refs)):
            H = x_ref.shape[0] // 2
            for k, (px, py) in enumerate(chips):
                cp = pltpu.make_async_remote_copy(src_ref=x_ref.at[pl.ds(c * H, H)], dst_ref=out_ref.at[j, pl.ds(c * H, H)],
                                                  send_sem=send_sems.at[a, k], recv_sem=recv_sems.at[a, k],
                                                  device_id=(px, py, c), device_id_type=MESH)
                cp.start()
                sends.append(cp)
        for x_ref, out_ref, buf in zip(x_refs, out_refs, bufs):
            _copy_via_vmem(x_ref, lambda o, n, out_ref=out_ref: out_ref.at[j, pl.ds(o, n)], x_ref.shape[0], buf, local_sem)
        for a, out_ref in enumerate(out_refs):
            H = out_ref.shape[1] // 2
            for k, (px, py) in enumerate(chips):
                slot = out_ref.at[2 * px + py, pl.ds(c * H, H)]
                pltpu.make_async_remote_copy(src_ref=slot, dst_ref=slot, send_sem=send_sems.at[a, k], recv_sem=recv_sems.at[a, k],
                                             device_id=(px, py, c), device_id_type=MESH).wait_recv()
                cp = pltpu.make_async_remote_copy(src_ref=slot, dst_ref=slot, send_sem=send_sems.at[a, 3 + k],
                                                  recv_sem=recv_sems.at[a, 3 + k], device_id=(x, y, 1 - c), device_id_type=MESH)
                cp.start()
                sends.append(cp)
        for a, out_ref in enumerate(out_refs):
            H = out_ref.shape[1] // 2
            for k, (px, py) in enumerate(chips):
                slot = out_ref.at[2 * px + py, pl.ds((1 - c) * H, H)]
                pltpu.make_async_remote_copy(src_ref=slot, dst_ref=slot, send_sem=send_sems.at[a, 3 + k], recv_sem=recv_sems.at[a, 3 + k],
                                             device_id=(x, y, 1 - c), device_id_type=MESH).wait_recv()
        for cp in sends:
            cp.wait_send()

    return pl.pallas_call(
        body, name=name, in_specs=[HBM] * na, out_specs=[HBM] * na,
        out_shape=[jax.ShapeDtypeStruct((4,) + w.shape, w.dtype) for w in ws],
        scratch_shapes=[pltpu.SemaphoreType.DMA((na, 6)), pltpu.SemaphoreType.DMA((na, 6)), pltpu.SemaphoreType.DMA]
        + [_chunk_buf(w.shape[0], w.shape[1], w.dtype) for w in ws],
    )(*ws)


def swap_halves(gs, *, name):
    na = len(gs)

    def body(*refs):
        g_refs, out_refs = refs[:na], refs[na:2 * na]
        send_sems, recv_sems = refs[2 * na:]
        x, y, c = _me()
        sib = (x, y, 1 - c)
        sends = []
        for a, (g_ref, out_ref) in enumerate(zip(g_refs, out_refs)):
            H = g_ref.shape[1] // 2
            for k in range(4):
                cp = pltpu.make_async_remote_copy(src_ref=g_ref.at[k, pl.ds((1 - c) * H, H)], dst_ref=out_ref.at[k],
                                                  send_sem=send_sems.at[a, k], recv_sem=recv_sems.at[a, k], device_id=sib, device_id_type=MESH)
                cp.start()
                sends.append(cp)
        for a, (g_ref, out_ref) in enumerate(zip(g_refs, out_refs)):
            H = g_ref.shape[1] // 2
            for k in range(4):
                pltpu.make_async_remote_copy(src_ref=g_ref.at[k, pl.ds(c * H, H)], dst_ref=out_ref.at[k], send_sem=send_sems.at[a, k],
                                             recv_sem=recv_sems.at[a, k], device_id=sib, device_id_type=MESH).wait_recv()
        for cp in sends:
            cp.wait_send()

    return pl.pallas_call(
        body, name=name, in_specs=[HBM] * na, out_specs=[HBM] * na,
        out_shape=[jax.ShapeDtypeStruct((4, g.shape[1] // 2, g.shape[2]), g.dtype) for g in gs],
        scratch_shapes=[pltpu.SemaphoreType.DMA((na, 4)), pltpu.SemaphoreType.DMA((na, 4))],
    )(*gs)


def scatter_chips(ps, *, name):
    na = len(ps)

    def body(*refs):
        p_refs, out_refs = refs[:na], refs[na:2 * na]
        send_sems, recv_sems, local_sem = refs[2 * na:2 * na + 3]
        bufs = refs[2 * na + 3:]
        x, y, c = _me()
        j = 2 * x + y
        chips = [(_flip(x, k & 2), _flip(y, k & 1)) for k in range(1, 4)]
        sends = []
        for a, (p_ref, out_ref) in enumerate(zip(p_refs, out_refs)):
            for k, (px, py) in enumerate(chips):
                cp = pltpu.make_async_remote_copy(src_ref=p_ref.at[2 * px + py], dst_ref=out_ref.at[j], send_sem=send_sems.at[a, k],
                                                  recv_sem=recv_sems.at[a, k], device_id=(px, py, c), device_id_type=MESH)
                cp.start()
                sends.append(cp)
        for p_ref, out_ref, buf in zip(p_refs, out_refs, bufs):
            _copy_via_vmem(p_ref.at[j], lambda o, n, out_ref=out_ref: out_ref.at[j, pl.ds(o, n)], p_ref.shape[1], buf, local_sem)
        for a, out_ref in enumerate(out_refs):
            for k, (px, py) in enumerate(chips):
                slot = out_ref.at[2 * px + py]
                pltpu.make_async_remote_copy(src_ref=slot, dst_ref=slot, send_sem=send_sems.at[a, k], recv_sem=recv_sems.at[a, k],
                                             device_id=(px, py, c), device_id_type=MESH).wait_recv()
        for cp in sends:
            cp.wait_send()

    return pl.pallas_call(
        body, name=name, in_specs=[HBM] * na, out_specs=[HBM] * na, out_shape=[jax.ShapeDtypeStruct(p.shape, p.dtype) for p in ps],
        scratch_shapes=[pltpu.SemaphoreType.DMA((na, 3)), pltpu.SemaphoreType.DMA((na, 3)), pltpu.SemaphoreType.DMA]
        + [_chunk_buf(p.shape[1], p.shape[2], p.dtype) for p in ps],
    )(*ps)


def join_halves(halves, *, name):
    na = len(halves)

    def body(*refs):
        h_refs, out_refs = refs[:na], refs[na:2 * na]
        send_sems, recv_sems, local_sem = refs[2 * na:2 * na + 3]
        bufs = refs[2 * na + 3:]
        x, y, c = _me()
        sib = (x, y, 1 - c)
        sends = []
        for a, (h_ref, out_ref) in enumerate(zip(h_refs, out_refs)):
            H = h_ref.shape[0]
            cp = pltpu.make_async_remote_copy(src_ref=h_ref, dst_ref=out_ref.at[pl.ds(c * H, H)], send_sem=send_sems.at[a],
                                              recv_sem=recv_sems.at[a], device_id=sib, device_id_type=MESH)
            cp.start()
            sends.append(cp)
        for h_ref, out_ref, buf in zip(h_refs, out_refs, bufs):
            H = h_ref.shape[0]
            _copy_via_vmem(h_ref, lambda o, n, out_ref=out_ref, H=H: out_ref.at[pl.ds(c * H + o, n)], H, buf, local_sem)
        for a, (h_ref, out_ref) in enumerate(zip(h_refs, out_refs)):
            H = h_ref.shape[0]
            pltpu.make_async_remote_copy(src_ref=h_ref, dst_ref=out_ref.at[pl.ds((1 - c) * H, H)], send_sem=send_sems.at[a],
                                         recv_sem=recv_sems.at[a], device_id=sib, device_id_type=MESH).wait_recv()
        for cp in sends:
            cp.wait_send()

    return pl.pallas_call(
        body, name=name, in_specs=[HBM] * na, out_specs=[HBM] * na,
        out_shape=[jax.ShapeDtypeStruct((2 * h.shape[0], h.shape[1]), h.dtype) for h in halves],
        scratch_shapes=[pltpu.SemaphoreType.DMA((na,)), pltpu.SemaphoreType.DMA((na,)), pltpu.SemaphoreType.DMA]
        + [_chunk_buf(h.shape[0], h.shape[1], h.dtype) for h in halves],
    )(*halves)


class _Plan:
    def __init__(self, ins, out_shapes, ncopies, copies, aliased=False):
        self.ins, self.out_shapes, self.ncopies, self.copies, self.aliased = list(ins), list(out_shapes), ncopies, copies, aliased

    def start(self, in_refs, out_refs, send_sems, recv_sems):
        sends, _ = self.copies(in_refs, out_refs, send_sems, recv_sems)
        for cp in sends:
            cp.start()

    def finish(self, in_refs, out_refs, send_sems, recv_sems):
        sends, recvs = self.copies(in_refs, out_refs, send_sems, recv_sems)
        for cp in recvs:
            cp.wait_recv()
        for cp in sends:
            cp.wait_send()


def _rcopy(src, dst, send_sems, recv_sems, idx, dev):
    return pltpu.make_async_remote_copy(src_ref=src, dst_ref=dst, send_sem=send_sems.at[idx], recv_sem=recv_sems.at[idx],
                                        device_id=dev, device_id_type=MESH)


def _other_chips(x, y):
    return [(_flip(x, k & 2), _flip(y, k & 1)) for k in range(1, 4)]


def plan_gather_ici(ws):
    def copies(in_refs, out_refs, ss, rs):
        x, y, c = _me()
        j = 2 * x + y
        sends, recvs = [], []
        for a, (x_ref, out_ref) in enumerate(zip(in_refs, out_refs)):
            H = x_ref.shape[0] // 2
            for k, (px, py) in enumerate(_other_chips(x, y)):
                sends.append(_rcopy(x_ref.at[pl.ds(c * H, H)], out_ref.at[j, pl.ds(c * H, H)], ss, rs, 3 * a + k, (px, py, c)))
                slot = out_ref.at[2 * px + py, pl.ds(c * H, H)]
                recvs.append(_rcopy(slot, slot, ss, rs, 3 * a + k, (px, py, c)))
        return sends, recvs

    return _Plan(ws, [jax.ShapeDtypeStruct((4,) + w.shape, w.dtype) for w in ws], 3 * len(ws), copies)


def plan_gather_d2d(w4s):
    def copies(in_refs, out_refs, ss, rs):
        x, y, c = _me()
        sends, recvs = [], []
        for a, out_ref in enumerate(out_refs):
            H = out_ref.shape[1] // 2
            for k, (px, py) in enumerate(_other_chips(x, y)):
                mine = out_ref.at[2 * px + py, pl.ds(c * H, H)]
                theirs = out_ref.at[2 * px + py, pl.ds((1 - c) * H, H)]
                sends.append(_rcopy(mine, mine, ss, rs, 3 * a + k, (x, y, 1 - c)))
                recvs.append(_rcopy(theirs, theirs, ss, rs, 3 * a + k, (x, y, 1 - c)))
        return sends, recvs

    return _Plan(w4s, [jax.ShapeDtypeStruct(w.shape, w.dtype) for w in w4s], 3 * len(w4s), copies, aliased=True)


def plan_swap_halves(gs):
    def copies(in_refs, out_refs, ss, rs):
        x, y, c = _me()
        sends, recvs = [], []
        for a, (g_ref, out_ref) in enumerate(zip(in_refs, out_refs)):
            H = g_ref.shape[1] // 2
            for k in range(4):
                sends.append(_rcopy(g_ref.at[k, pl.ds((1 - c) * H, H)], out_ref.at[k], ss, rs, 4 * a + k, (x, y, 1 - c)))
                recvs.append(_rcopy(g_ref.at[k, pl.ds(c * H, H)], out_ref.at[k], ss, rs, 4 * a + k, (x, y, 1 - c)))
        return sends, recvs

    return _Plan(gs, [jax.ShapeDtypeStruct((4, g.shape[1] // 2, g.shape[2]), g.dtype) for g in gs], 4 * len(gs), copies)


def plan_scatter_ici(ps):
    def copies(in_refs, out_refs, ss, rs):
        x, y, c = _me()
        j = 2 * x + y
        sends, recvs = [], []
        for a, (p_ref, out_ref) in enumerate(zip(in_refs, out_refs)):
            for k, (px, py) in enumerate(_other_chips(x, y)):
                sends.append(_rcopy(p_ref.at[2 * px + py], out_ref.at[j], ss, rs, 3 * a + k, (px, py, c)))
                slot = out_ref.at[2 * px + py]
                recvs.append(_rcopy(slot, slot, ss, rs, 3 * a + k, (px, py, c)))
        return sends, recvs

    return _Plan(ps, [jax.ShapeDtypeStruct(p.shape, p.dtype) for p in ps], 3 * len(ps), copies)


def plan_join_halves(fulls):
    def copies(in_refs, out_refs, ss, rs):
        x, y, c = _me()
        sends, recvs = [], []
        for a, out_ref in enumerate(out_refs):
            H = out_ref.shape[0] // 2
            mine, theirs = out_ref.at[pl.ds(c * H, H)], out_ref.at[pl.ds((1 - c) * H, H)]
            sends.append(_rcopy(mine, mine, ss, rs, a, (x, y, 1 - c)))
            recvs.append(_rcopy(theirs, theirs, ss, rs, a, (x, y, 1 - c)))
        return sends, recvs

    return _Plan(fulls, [jax.ShapeDtypeStruct(f.shape, f.dtype) for f in fulls], len(fulls), copies, aliased=True)


def call_with_plans(body, plans, *, grid, in_specs, out_specs, out_shape, scratch_shapes, args, sem, name):
    plans = list(plans or [])
    n_in, n_out, n_scr = len(in_specs), len(out_specs), len(scratch_shapes)
    c_in = [len(p.ins) for p in plans]
    c_out = [len(p.out_shapes) for p in plans]
    steps = math.prod(grid) if grid else 1

    def wrapped(*refs):
        pos = 0

        def take(n):
            nonlocal pos
            out = refs[pos:pos + n]
            pos += n
            return out

        ins = take(n_in)
        cins = [take(n) for n in c_in]
        outs = take(n_out)
        couts = [take(n) for n in c_out]
        scr = take(n_scr)
        sems = [take(2) for _ in plans]
        def start_all():
            for p, ci, co, (ss, rs) in zip(plans, cins, couts, sems):
                p.start(ci, co, ss, rs)

        def finish_all():
            for p, ci, co, (ss, rs) in zip(plans, cins, couts, sems):
                p.finish(ci, co, ss, rs)

        if plans and grid:
            idx = 0
            for ax, g in enumerate(grid):
                idx = idx * g + pl.program_id(ax)
            pl.when(idx == 0)(start_all)
        elif plans:
            start_all()
        if body is not None:
            body(*ins, *outs, *scr)
        if plans and grid:
            pl.when(idx == steps - 1)(finish_all)
        elif plans:
            finish_all()

    aliases = {}
    i_pos, o_pos = n_in, n_out
    for p, ni, no in zip(plans, c_in, c_out):
        if p.aliased:
            aliases.update({i_pos + t: o_pos + t for t in range(ni)})
        i_pos += ni
        o_pos += no
    kwargs = dict(grid=grid) if grid else {}
    if aliases:
        kwargs["input_output_aliases"] = aliases
    res = pl.pallas_call(
        wrapped, name=name, in_specs=list(in_specs) + [HBM] * sum(c_in), out_specs=list(out_specs) + [HBM] * sum(c_out),
        out_shape=list(out_shape) + [s for p in plans for s in p.out_shapes],
        scratch_shapes=list(scratch_shapes) + [pltpu.SemaphoreType.DMA((p.ncopies,)) for p in plans for _ in range(2)],
        compiler_params=_cp(*sem) if grid else pltpu.CompilerParams(vmem_limit_bytes=VMEM_LIMIT), **kwargs,
    )(*args, *[a for p in plans for a in p.ins])
    res = list(res)
    comp, rest = res[:n_out], res[n_out:]
    pouts = []
    for no in c_out:
        pouts.append(rest[:no])
        rest = rest[no:]
    return comp, pouts


def run_plans(plans, *, name):
    return call_with_plans(None, plans, grid=(), in_specs=[], out_specs=[], out_shape=[], scratch_shapes=[], args=[], sem=(), name=name)[1]


def _cat(parts, axis=-1):
    return jnp.concatenate(parts, axis=axis)


def _pairs_of_heads(a, axis, inverse=False):
    lead, tail = a.shape[:axis], a.shape[axis + 1:]
    split = (3, 2) if inverse else (2, 3)
    a = a.reshape(lead + split + (HEAD,) + tail)
    return jnp.swapaxes(a, axis, axis + 1).reshape(lead + (6 * HEAD,) + tail)


def _prep_w_in(w):
    z = lambda n: jnp.zeros((w.shape[0], n), w.dtype)
    return _cat([w[:, 0:1152], z(64), w[:, 1152:1184], z(32), _pairs_of_heads(w[:, 1184:1568], 1), w[:, 1568:1824]])


def _unprep_w_in(g):
    return _cat([g[:, 0:1152], g[:, 1216:1248], _pairs_of_heads(g[:, P_SWQ:P_SWK], 1, inverse=True), g[:, P_SWK:P_END]])


def _prep_w_uq(w):
    r = w.shape[0]
    return jnp.pad(w.reshape(r, 6, MLA_QK), ((0, 0), (0, 0), (0, LANES - MLA_QK))).reshape(r, 6 * LANES)


def _unprep_w_uq(g):
    r = g.shape[0]
    return g.reshape(r, 6, LANES)[:, :, :MLA_QK].reshape(r, 6 * MLA_QK)


def _prep_w_ukv(w):
    r = w.shape[0]
    w3 = w.reshape(r, 6, LANES)
    k = jnp.pad(w3[:, :, :HEAD], ((0, 0), (0, 0), (0, LANES - HEAD))).reshape(r, 6 * LANES)
    return _cat([k, w3[:, :, HEAD:].reshape(r, 6 * HEAD)])


def _unprep_w_ukv(g):
    r = g.shape[0]
    k = g[:, :6 * LANES].reshape(r, 6, LANES)[:, :, :HEAD]
    return _cat([k, g[:, 6 * LANES:].reshape(r, 6, HEAD)], axis=2).reshape(r, 6 * LANES)


def _prep_w_out(w):
    return _cat([w[0:640], _pairs_of_heads(w[640:], 0)], axis=0)


def _unprep_w_out(g):
    return _cat([g[0:640], _pairs_of_heads(g[640:], 0, inverse=True)], axis=0)


def _rope_tables(positions):
    half = 16
    inv_freq = jnp.power(ROPE_THETA, -jnp.arange(half, dtype=F32) / half)
    ang = positions.astype(F32)[..., None] * inv_freq
    cos, sin = jnp.cos(ang), jnp.sin(ang)
    z = lambda n: jnp.zeros(ang.shape[:-1] + (n,), F32)
    return (_cat([jnp.ones(ang.shape[:-1] + (HEAD,), F32), cos, cos, z(32)]), _cat([z(HEAD), -sin, z(16), z(32)]), _cat([z(HEAD), z(16), sin, z(32)]))


def _small_params(p):
    pad96 = lambda g: _cat([g, jnp.zeros((32,), F32)]).reshape(1, LANES)
    two = lambda g: _cat([g, g]).reshape(1, LANES)
    sinks = jnp.broadcast_to(p["sw_sinks"].reshape(2, 3).T[:, :, None], (3, 2, LANES))
    return dict(n1=p["norm1_g"].reshape(1, -1), n2=p["norm2_g"].reshape(1, -1), cq_g=p["mla_cq_g"].reshape(1, -1),
                ckv_g=p["mla_ckv_g"].reshape(1, -1), qn_g=pad96(p["mla_qn_g"]), kn_g=pad96(p["mla_kn_g"]),
                swq_g=two(p["sw_qn_g"]), swk_g=two(p["sw_kn_g"]), sinks=sinks, conv_b=_up_perm(p["conv_b"]).reshape(1, -1))


class _NoFlow:
    def plans(self, tag):
        return []

    def done(self, tag, outs):
        pass

    def add(self, key, g):
        pass


def _layer_fwd(x3, md, W, tabs, bias, tag, flow=_NoFlow()):
    Bl, S, D = x3.shape
    T = Bl * S
    n = lambda s: f"{s}_{tag}"
    two = lambda a: a.reshape(T, a.shape[-1])
    three = lambda a: a.reshape(Bl, S, a.shape[-1])
    h = rms_fwd(x3, 0, D, W["n1"], md["scale1"], md["shift1"], name=n("norm1"))
    proj = three(matmul(two(h), W["w_in"], tn=1920, name=n("in_proj")))
    (o_a, rt_a), got = sb_attn_fwd(proj, plans=flow.plans(n("sb_fwd")), name=n("sb_fwd"))
    flow.done(n("sb_fwd"), got)
    cqn = rms_fwd(proj, P_CQ // 256, 256, W["cq_g"], name=n("cq_norm"))
    ckvn = rms_fwd(proj, P_CKV // LANES, LANES, W["ckv_g"], name=n("ckv_norm"))
    qb = three(matmul(two(cqn), W["w_uq"], tm=1024, tn=768, name=n("uq")))
    kvb = three(matmul(two(ckvn), W["w_ukv"], tm=1024, tn=1152, name=n("ukv")))
    q_m = rope_norm_fwd(qb, 6, W["qn_g"], tabs, name=n("q_rope"))
    k_m = rope_norm_fwd(kvb, 6, W["kn_g"], tabs, (proj, P_SLAB // LANES), name=n("k_rope"))
    (o_b, lse_b), got = mla_attn_fwd(q_m, k_m, kvb, 6, plans=flow.plans(n("mla_fwd")), name=n("mla_fwd"))
    flow.done(n("mla_fwd"), got)
    q_c = pair_rms_fwd(proj, P_SWQ // LANES, 3, W["swq_g"], name=n("swq_norm"))
    k_c = pair_rms_fwd(proj, P_SWK // LANES, 1, W["swk_g"], name=n("swk_norm"))
    (o_c, lse_c), got = swa_attn_fwd(q_c, k_c, proj, bias, W["sinks"], plans=flow.plans(n("swa_fwd")), name=n("swa_fwd"))
    flow.done(n("swa_fwd"), got)
    mix = _cat([o_a, o_b, o_c]).astype(BF16)
    att, x1 = matmul_res(two(mix), W["w_out"], two(x3), md["gate1"], S, name=n("out_proj"))
    x1 = three(x1)
    h2 = rms_fwd(x1, 0, D, W["n2"], md["scale2"], md["shift2"], name=n("norm2"))
    up = three(matmul(two(h2), W["w_up"], tm=1024, tn=1408, name=n("up_proj")))
    a = conv_gate_fwd(up, W["conv_w"], W["conv_b"], name=n("conv_gate"))
    yd, x2 = matmul_res(two(a), W["w_down"], two(x1), md["gate2"], S, name=n("down_proj"))
    saved = dict(x=x3, h=h, proj=proj, rt_a=rt_a, cqn=cqn, ckvn=ckvn, qb=qb, kvb=kvb, q_m=q_m, k_m=k_m, o_b=o_b, lse_b=lse_b,
                 q_c=q_c, k_c=k_c, o_c=o_c, lse_c=lse_c, mix=mix, att=three(att), x1=x1, h2=h2, up=up, a=a, yd=three(yd))
    return three(x2), saved


def _layer_bwd(dx2, sv, md, W, tabs, bias, tag, flow=_NoFlow()):
    Bl, S, D = dx2.shape
    T = Bl * S
    n = lambda s: f"{s}_{tag}"
    two = lambda a: a.reshape(T, a.shape[-1])
    three = lambda a: a.reshape(Bl, S, a.shape[-1])
    g = {}
    dyb, dgate2 = gate_bwd(dx2, sv["yd"], md["gate2"], name=n("gate2_bwd"))
    da = three(matmul(two(dyb), W["w_down"], tb=True, tm=1024, tn=1408, name=n("down_dx")))
    g["w_down"] = matmul(two(sv["a"]), two(dyb), ta=True, tm=256, tn=1024, name=n("down_dw"))
    dup, dcw = conv_gate_bwd(sv["up"], W["conv_w"], W["conv_b"], da, name=n("conv_gate_bwd"))
    dh2 = three(matmul(two(dup), W["w_up"], tb=True, tn=1024, name=n("up_dx")))
    g["w_up"] = matmul(two(sv["h2"]), two(dup), ta=True, tn=1408, name=n("up_dw"))
    dx1, dn2, dsc2, dsh2 = rms_bwd(sv["x1"], 0, D, dh2, W["n2"], md["scale2"], dx2, name=n("norm2_bwd"))
    dmo, dgate1 = gate_bwd(dx1, sv["att"], md["gate1"], name=n("gate1_bwd"))
    dmix = three(matmul(two(dmo), W["w_out"], tb=True, tn=1024, out_dtype=BF16, name=n("out_dx")))
    g["w_out"] = matmul(two(sv["mix"]), two(dmo), ta=True, tn=1024, name=n("out_dw"))
    proj = sv["proj"]
    for k in ("w_down", "w_up", "w_out"):
        flow.add((tag, k), g[k])
    (dq_a, dk_a, dv_a), got = sb_attn_bwd(proj, sv["rt_a"], dmix[:, :, 0:256], plans=flow.plans(n("sb_bwd")), name=n("sb_bwd"))
    flow.done(n("sb_bwd"), got)
    dq_m, dk_m, dv_b = mla_attn_bwd(sv["q_m"], sv["k_m"], sv["kvb"], 6, sv["o_b"], sv["lse_b"], dmix[:, :, 256:640], name=n("mla_bwd"))
    dqb, dqn = rope_norm_bwd(sv["qb"], 6, dq_m, W["qn_g"], tabs, name=n("q_rope_bwd"))
    dkn_x, dkn, dslab = rope_norm_bwd(sv["kvb"], 6, dk_m, W["kn_g"], tabs, (proj, P_SLAB // LANES), name=n("k_rope_bwd"))
    dkvb = _cat([dkn_x, dv_b]).astype(BF16)
    dckvn = three(matmul(two(dkvb), W["w_ukv"], tb=True, tm=1024, name=n("ukv_dx")))
    g["w_ukv"] = matmul(two(sv["ckvn"]), two(dkvb), ta=True, tn=1152, name=n("ukv_dw"))
    dcqn = three(matmul(two(dqb), W["w_uq"], tb=True, tm=1024, name=n("uq_dx")))
    g["w_uq"] = matmul(two(sv["cqn"]), two(dqb), ta=True, tn=768, name=n("uq_dw"))
    dcq, dcq_g = rms_bwd(proj, P_CQ // 256, 256, dcqn, W["cq_g"], name=n("cq_norm_bwd"))
    dckv, dckv_g = rms_bwd(proj, P_CKV // LANES, LANES, dckvn, W["ckv_g"], name=n("ckv_norm_bwd"))
    dq_c, dk_c, dv_c, dbias, dsink = swa_attn_bwd(sv["q_c"], sv["k_c"], proj, bias, W["sinks"], sv["o_c"], sv["lse_c"], dmix[:, :, 640:1024], name=n("swa_bwd"))
    dswq, dswq_g = pair_rms_bwd(proj, P_SWQ // LANES, 3, dq_c, W["swq_g"], name=n("swq_norm_bwd"))
    dswk, dswk_g = pair_rms_bwd(proj, P_SWK // LANES, 1, dk_c, W["swk_g"], name=n("swk_norm_bwd"))
    dproj = _cat([dq_a, dk_a, dv_a, dcq, dckv, dslab, dswq, dswk, dv_c]).astype(BF16)
    dh = three(matmul(two(dproj), W["w_in"], tb=True, tn=1024, name=n("in_dx")))
    g["w_in"] = matmul(two(sv["h"]), two(dproj), ta=True, tn=1920, tk=2048, name=n("in_dw"))
    dx, dn1, dsc1, dsh1 = rms_bwd(sv["x"], 0, D, dh, W["n1"], md["scale1"], dx1, name=n("norm1_bwd"))
    small = dict(n1=dn1, n2=dn2, cq_g=dcq_g, ckv_g=dckv_g, qn_g=dqn, kn_g=dkn, swq_g=dswq_g, swk_g=dswk_g, conv=dcw)
    dmods = _cat([dsh1, dsc1, dgate1, dsh2, dsc2, dgate2]).reshape(Bl, 6 * D)
    for k in ("w_ukv", "w_uq", "w_in"):
        flow.add((tag, k), g[k])
    return dx, g, small, dmods, dbias, dsink


BIG = ("w_in", "w_uq", "w_ukv", "w_out", "w_up", "w_down")
ROW_SHARDED = ("w_out", "w_down")
PREP = dict(w_in=_prep_w_in, w_uq=_prep_w_uq, w_ukv=_prep_w_ukv, w_out=_prep_w_out, w_up=_up_perm, w_down=lambda w: w)
UNPREP = dict(w_in=_unprep_w_in, w_uq=_unprep_w_uq, w_ukv=_unprep_w_ukv, w_out=_unprep_w_out, w_up=_up_perm, w_down=lambda w: w)
NCHIPS = 4


def _local_step(x, target, positions, mods, Wl, rel_flat, fwd_flow=_NoFlow(), bwd_flow=_NoFlow()):
    Bl, S, D = x.shape
    L = len(Wl)
    tabs = _rope_tables(positions)
    bucket = _bucket_table()
    bias = swa_bias(rel_flat, bucket, name="swa_bias")
    mds = []
    for l in range(L):
        parts = [mods[l, :, D * k:D * (k + 1)].reshape(Bl, 1, D) for k in range(6)]
        mds.append(dict(zip(("shift1", "scale1", "gate1", "shift2", "scale2", "gate2"), parts)))
    saved = []
    h = x
    for l in range(L):
        h, sv = _layer_fwd(h, mds[l], Wl[l], tabs, bias, f"l{l}", fwd_flow)
        saved.append(sv)
    dy, loss = loss_grad(h, target, name="loss")
    grads, smalls, dmods, dbiases, dsinks = [None] * L, [None] * L, [None] * L, [None] * L, [None] * L
    for l in reversed(range(L)):
        dy, grads[l], smalls[l], dmods[l], dbiases[l], dsinks[l] = _layer_bwd(dy, saved[l], mds[l], Wl[l], tabs, bias, f"l{l}", bwd_flow)
    drel = swa_bias_bwd(_cat(dbiases, axis=0), bucket, name="swa_bias_bwd")
    return loss, dy, grads, smalls, dmods, dsinks, drel


ATT = ("w_in", "w_uq", "w_ukv", "w_out")
FFN = ("w_up", "w_down")
GATHER_STAGES = {
    "sb_fwd_l0": ([("l0", k) for k in FFN], []),
    "mla_fwd_l0": ([("l1", k) for k in ATT + ("w_up",)], [("l0", k) for k in FFN]),
    "swa_fwd_l0": ([("l1", "w_down")], [("l1", k) for k in ATT + ("w_up",)]),
    "sb_fwd_l1": ([], [("l1", "w_down")]),
}
SCATTER_STAGES = {
    "sb_bwd_l1": [("l1", k) for k in FFN],
    "sb_bwd_l0": [("l1", k) for k in ATT] + [("l0", k) for k in FFN],
}


class _GatherFlow:
    def __init__(self, shards, chip):
        self.shards, self.chip, self.ici, self.d2d, self.pending = shards, chip, {}, {}, {}

    def early(self, keys):
        ici, = run_plans([plan_gather_ici([self.shards[k] for k in keys])], name="gather_early_ici")
        d2d, = run_plans([plan_gather_d2d(ici)], name="gather_early_d2d")
        self.d2d.update(zip(keys, d2d))

    def plans(self, tag):
        ici_keys, d2d_keys = GATHER_STAGES.get(tag, ([], []))
        plans = []
        if d2d_keys:
            plans.append(plan_gather_d2d([self.ici[k] for k in d2d_keys]))
        if ici_keys:
            plans.append(plan_gather_ici([self.shards[k] for k in ici_keys]))
        self.pending[tag] = (ici_keys, d2d_keys)
        return plans

    def done(self, tag, outs):
        ici_keys, d2d_keys = self.pending.pop(tag, ([], []))
        outs = list(outs)
        if d2d_keys:
            self.d2d.update(zip(d2d_keys, outs.pop(0)))
        if ici_keys:
            self.ici.update(zip(ici_keys, outs.pop(0)))

    def weight(self, key):
        k = key[1]
        own = self.shards[key]
        r, cc = own.shape
        w4 = lax.dynamic_update_slice(self.d2d[key], own[None], (self.chip, 0, 0))
        fw = w4.reshape(NCHIPS * r, cc) if k in ROW_SHARDED else jnp.transpose(w4, (1, 0, 2)).reshape(r, NCHIPS * cc)
        return PREP[k](fw)


class _LayerWeights(dict):
    def __init__(self, small, flow, tag):
        super().__init__(small)
        self.flow, self.tag = flow, tag

    def __missing__(self, k):
        self[k] = self.flow.weight((self.tag, k))
        return self[k]


class _ScatterFlow:
    def __init__(self, shapes, sel, c_arr):
        self.shapes, self.sel, self.c_arr = shapes, sel, c_arr
        self.g, self.pairs, self.landed, self.pending = {}, {}, {}, {}

    def add(self, key, g):
        self.g[key] = g

    def _pairs(self, keys, label):
        g4s = []
        for key in keys:
            k = key[1]
            r, cc = self.shapes[k]
            gk = UNPREP[k](self.g[key])
            g4 = gk.reshape(NCHIPS, r, cc) if k in ROW_SHARDED else jnp.transpose(gk.reshape(r, NCHIPS, cc), (1, 0, 2))
            g4s.append(g4.astype(BF16))
        theirs, = run_plans([plan_swap_halves(g4s)], name=f"rs_swap_{label}")
        pairs = [pair_add_half(g4, th, self.c_arr, name=f"rs_pair_add_{key[1]}_{key[0]}") for key, g4, th in zip(keys, g4s, theirs)]
        self.pairs.update(zip(keys, pairs))
        return pairs

    def plans(self, tag):
        keys = SCATTER_STAGES.get(tag, [])
        self.pending[tag] = keys
        return [plan_scatter_ici(self._pairs(keys, tag))] if keys else []

    def done(self, tag, outs):
        keys = self.pending.pop(tag, [])
        if keys:
            self.landed.update(zip(keys, outs[0]))

    def finish(self):
        rest = [key for key in self.g if key not in self.pairs]
        if rest:
            landed, = run_plans([plan_scatter_ici(self._pairs(rest, "rest"))], name="rs_scatter_rest")
            self.landed.update(zip(rest, landed))
        keys = list(self.pairs)
        fulls = [chip_sum_into(self.landed[key], self.pairs[key], self.sel, name=f"rs_chip_sum_{key[1]}_{key[0]}") for key in keys]
        joined, = run_plans([plan_join_halves(fulls)], name="rs_join_halves")
        return dict(zip(keys, joined))


WEIGHTS = ("rel_table", "norm1_g", "norm2_g", "w_ada", "b_ada", "w_in", "mla_cq_g", "w_uq", "mla_ckv_g", "w_ukv", "mla_qn_g", "mla_kn_g",
           "sw_qn_g", "sw_kn_g", "sw_sinks", "w_out", "w_up", "conv_w", "conv_b", "w_down")
SMALL = tuple(n for n in WEIGHTS if n not in BIG + ("w_ada",))


def kernel(x, c, positions, rel_table, norm1_g, norm2_g, w_ada, b_ada, w_in, mla_cq_g, w_uq, mla_ckv_g, w_ukv, mla_qn_g, mla_kn_g, sw_qn_g, sw_kn_g, sw_sinks, w_out, w_up, conv_w, conv_b, w_down, loss_target, m_rel_table, m_norm1_g, m_norm2_g, m_w_ada, m_b_ada, m_w_in, m_mla_cq_g, m_w_uq, m_mla_ckv_g, m_w_ukv, m_mla_qn_g, m_mla_kn_g, m_sw_qn_g, m_sw_kn_g, m_sw_sinks, m_w_out, m_w_up, m_conv_w, m_conv_b, m_w_down, v_rel_table, v_norm1_g, v_norm2_g, v_w_ada, v_b_ada, v_w_in, v_mla_cq_g, v_w_uq, v_mla_ckv_g, v_w_ukv, v_mla_qn_g, v_mla_kn_g, v_sw_qn_g, v_sw_kn_g, v_sw_sinks, v_w_out, v_w_up, v_conv_w, v_conv_b, v_w_down):
    w = dict(rel_table=rel_table, norm1_g=norm1_g, norm2_g=norm2_g, w_ada=w_ada, b_ada=b_ada, w_in=w_in, mla_cq_g=mla_cq_g, w_uq=w_uq,
             mla_ckv_g=mla_ckv_g, w_ukv=w_ukv, mla_qn_g=mla_qn_g, mla_kn_g=mla_kn_g, sw_qn_g=sw_qn_g, sw_kn_g=sw_kn_g, sw_sinks=sw_sinks,
             w_out=w_out, w_up=w_up, conv_w=conv_w, conv_b=conv_b, w_down=w_down)
    m = dict(rel_table=m_rel_table, norm1_g=m_norm1_g, norm2_g=m_norm2_g, w_ada=m_w_ada, b_ada=m_b_ada, w_in=m_w_in, mla_cq_g=m_mla_cq_g,
             w_uq=m_w_uq, mla_ckv_g=m_mla_ckv_g, w_ukv=m_w_ukv, mla_qn_g=m_mla_qn_g, mla_kn_g=m_mla_kn_g, sw_qn_g=m_sw_qn_g,
             sw_kn_g=m_sw_kn_g, sw_sinks=m_sw_sinks, w_out=m_w_out, w_up=m_w_up, conv_w=m_conv_w, conv_b=m_conv_b, w_down=m_w_down)
    v = dict(rel_table=v_rel_table, norm1_g=v_norm1_g, norm2_g=v_norm2_g, w_ada=v_w_ada, b_ada=v_b_ada, w_in=v_w_in, mla_cq_g=v_mla_cq_g,
             w_uq=v_w_uq, mla_ckv_g=v_mla_ckv_g, w_ukv=v_w_ukv, mla_qn_g=v_mla_qn_g, mla_kn_g=v_mla_kn_g, sw_qn_g=v_sw_qn_g,
             sw_kn_g=v_sw_kn_g, sw_sinks=v_sw_sinks, w_out=v_w_out, w_up=v_w_up, conv_w=v_conv_w, conv_b=v_conv_b, w_down=v_w_down)
    Bl, S, D = x.shape
    L = norm1_g.shape[0]
    xi, yi, ci = _me()
    chip = 2 * xi + yi
    dev = 4 * xi + 2 * yi + ci
    ndev = 2 * NCHIPS

    shapes = {k: w[k].shape[1:] for k in BIG}
    shards = {(f"l{l}", k): w[k][l].astype(BF16) for l in range(L) for k in BIG}
    gflow = _GatherFlow(shards, chip)
    gflow.early([("l0", k) for k in ATT])

    cw_cols = conv_w.shape[2]
    c_got, cw_got = allgather8([c, conv_w.reshape(L * 3, cw_cols)], name="gather_cond")
    c_all = c_got.reshape(ndev * Bl, D)
    conv_full = jnp.transpose(cw_got[0::2].reshape(NCHIPS, L, 3, cw_cols), (1, 2, 0, 3)).reshape(L, 3, NCHIPS * cw_cols)
    E = w_ada.shape[2]
    b_cols = lax.dynamic_slice(b_ada, (0, chip * E), (L, E)).reshape(L, 1, E)
    mods_cols = mods_matmul(c_all, w_ada, b_cols, name="mods")
    mods_all, = allgather8([mods_cols.reshape(L * ndev * Bl, E)], name="gather_mods")
    mods_all = jnp.transpose(mods_all[0::2].reshape(NCHIPS, L, ndev * Bl, E), (1, 2, 0, 3)).reshape(L, ndev * Bl, NCHIPS * E)
    mods = lax.dynamic_slice(mods_all, (0, dev * Bl, 0), (L, Bl, NCHIPS * E))

    Wl = []
    for l in range(L):
        Wd = _small_params({k: w[k][l] for k in SMALL if k not in ("rel_table", "b_ada", "conv_w")})
        Wd["conv_w"] = _up_perm(conv_full[l])
        Wl.append(_LayerWeights(Wd, gflow, f"l{l}"))

    sflow = _ScatterFlow(shapes, jnp.stack([chip, ci]).astype(jnp.int32), ci.reshape(1).astype(jnp.int32))
    loss, dx, _, smalls, dmods, dsinks, drel = _local_step(x, loss_target, positions, mods, Wl, rel_table.reshape(-1), gflow, sflow)
    reduced = sflow.finish()
    grad = {k: jnp.stack([reduced[(f"l{l}", k)] for l in range(L)]) for k in BIG}

    vec_names = ("n1", "n2", "cq_g", "ckv_g", "qn_g", "kn_g", "swq_g", "swk_g")
    vecs = _cat([_cat([smalls[l][k] for k in vec_names], axis=1) for l in range(L)], axis=0)
    convs = _cat([smalls[l]["conv"][0:4] for l in range(L)], axis=0)
    dm = jnp.stack(dmods, axis=1).reshape(Bl * L, 6 * D)
    dsk = jnp.stack(dsinks, axis=1).reshape(Bl * L * 6, LANES)
    got = allgather8([vecs, convs, drel, loss, dm, dsk], name="gather_small_grads")
    seq = lambda a, rows: a.reshape(ndev * Bl, rows, a.shape[-1])
    vec_s, conv_s, rel_s, loss_s, dm_s, dsk_s = sum_small(list(got[:4]) + [seq(got[4], L), seq(got[5], L * 6)], name="sum_small_grads")
    dm_all = jnp.transpose(seq(got[4], L), (1, 0, 2))
    grad["w_ada"] = ada_grad(c_all, lax.dynamic_slice(dm_all, (0, 0, chip * E), (L, ndev * Bl, E)), name="ada_grad")
    grad["b_ada"] = dm_s
    grad["sw_sinks"] = jnp.transpose(dsk_s.reshape(L, 3, 2, LANES)[:, :, :, 0], (0, 2, 1)).reshape(L, 6)
    grad["rel_table"] = rel_s[:6, :REL_BUCKETS].T
    off = 0
    for k, name_, keep in zip(vec_names, ("norm1_g", "norm2_g", "mla_cq_g", "mla_ckv_g", "mla_qn_g", "mla_kn_g", "sw_qn_g", "sw_kn_g"),
                              (D, D, 256, LANES, MLA_QK, MLA_QK, HEAD, HEAD)):
        grad[name_] = vec_s[:, off:off + keep]
        off += smalls[0][k].shape[1]
    conv = _up_perm(conv_s.reshape(L, 4, 2 * D_FF))
    grad["conv_w"] = lax.dynamic_slice(conv[:, 0:3], (0, 0, chip * cw_cols), (L, 3, cw_cols))
    grad["conv_b"] = conv[:, 3]
    loss_out = loss_s[0, 0]

    delta, new_m, new_v = {}, {}, {}
    for k in BIG + ("w_ada",):
        delta[k], new_m[k], new_v[k] = adamw(w[k], grad[k], m[k], v[k], name=f"adamw_{k}")
    outs = adamw_small(*[[src[k] for k in SMALL] for src in (w, grad, m, v)], name="adamw_small")
    for dst, o in zip((delta, new_m, new_v), outs):
        dst.update(dict(zip(SMALL, o)))
    return (loss_out, dx, *[grad[k] for k in WEIGHTS], *[delta[k] for k in WEIGHTS], *[new_m[k] for k in WEIGHTS], *[new_v[k] for k in WEIGHTS])
```

```python
import functools
import math

import jax
import jax.numpy as jnp
from jax import lax
from jax.experimental import pallas as pl
from jax.experimental.pallas import tpu as pltpu

F32 = jnp.float32
BF16 = jnp.bfloat16
MESH = pl.DeviceIdType.MESH

EPS = 1e-6
NEG = -1e30
HEAD = 64
LANES = 128
MLA_QK = 96
ROPE_THETA = 10000.0
REL_BUCKETS = 32
REL_MAX_DIST = 128
WINDOW = 128
D_FF = 2816
ADAM_LR, ADAM_B1, ADAM_B2, ADAM_EPS, ADAM_WD, ADAM_STEP = 0.001, 0.9, 0.999, 1e-08, 0.01, 10

VMEM_LIMIT = 56 * 1024 * 1024
STRIP = 32
P_SBQ, P_SBK, P_SBV, P_CQ, P_CKV, P_SLAB, P_SWQ, P_SWK, P_SWV, P_END = 0, 256, 512, 768, 1024, 1152, 1280, 1664, 1792, 1920
SW_PERM = (0, 3, 1, 4, 2, 5)


def _cp(*sem):
    return pltpu.CompilerParams(dimension_semantics=sem, vmem_limit_bytes=VMEM_LIMIT)


def _dot(a, b):
    return jnp.dot(a, b, preferred_element_type=F32)


def _dot_nt(a, b):
    return lax.dot_general(a, b, (((1,), (1,)), ((), ())), preferred_element_type=F32)


def _dot_tn(a, b):
    return lax.dot_general(a, b, (((0,), (0,)), ((), ())), preferred_element_type=F32)


def _split_dot(x, u):
    hi = x.astype(BF16)
    lo = (x - hi.astype(F32)).astype(BF16)
    return _dot(hi, u) + _dot(lo, u)


def _lane_masks():
    lane = lax.broadcasted_iota(jnp.int32, (1, LANES), 1)
    return (lane < HEAD, lane >= HEAD)


def _tile(n, cap, align=128):
    if n <= cap:
        return n
    t = cap - cap % align
    while t >= align:
        if n % t == 0:
            return t
        t -= align
    return n


def matmul(a, b, *, ta=False, tb=False, out_dtype=F32, tm=512, tn=512, tk=8192, name):
    M, K = (a.shape[1], a.shape[0]) if ta else a.shape
    N = b.shape[0] if tb else b.shape[1]
    tm, tn, tk = _tile(M, tm), _tile(N, tn), _tile(K, tk)
    nk = K // tk

    def body(a_ref, b_ref, o_ref, *scratch):
        av = a_ref[...].astype(BF16)
        bv = b_ref[...].astype(BF16)
        if ta:
            part = _dot_tn(av, bv)
        elif tb:
            part = _dot_nt(av, bv)
        else:
            part = _dot(av, bv)
        if nk == 1:
            o_ref[...] = part.astype(out_dtype)
        else:
            acc_ref, = scratch
            k = pl.program_id(2)

            @pl.when(k == 0)
            def _():
                acc_ref[...] = part

            @pl.when(k > 0)
            def _():
                acc_ref[...] += part

            @pl.when(k == nk - 1)
            def _():
                o_ref[...] = acc_ref[...].astype(out_dtype)

    n_outer = nk == 1 and tn * b.dtype.itemsize > tm * a.dtype.itemsize
    ij = (lambda p, q: (q, p)) if n_outer else (lambda p, q: (p, q))
    a_map = (lambda p, q, k: (k, ij(p, q)[0])) if ta else (lambda p, q, k: (ij(p, q)[0], k))
    b_map = (lambda p, q, k: (ij(p, q)[1], k)) if tb else (lambda p, q, k: (k, ij(p, q)[1]))
    grid = (N // tn, M // tm, nk) if n_outer else (M // tm, N // tn, nk)
    return pl.pallas_call(
        body, name=name, grid=grid,
        in_specs=[pl.BlockSpec((tk, tm) if ta else (tm, tk), a_map), pl.BlockSpec((tn, tk) if tb else (tk, tn), b_map)],
        out_specs=pl.BlockSpec((tm, tn), lambda p, q, k: ij(p, q)),
        out_shape=jax.ShapeDtypeStruct((M, N), out_dtype),
        scratch_shapes=[] if nk == 1 else [pltpu.VMEM((tm, tn), F32)],
        compiler_params=_cp("parallel", "parallel", "arbitrary"),
    )(a, b)


def matmul_res(a, b, res, gate, seq, *, tm=512, tn=1024, name):
    M, K = a.shape
    N = b.shape[1]
    tm, tn = _tile(min(M, seq), tm), _tile(N, tn)
    per_seq = seq // tm

    def body(a_ref, b_ref, r_ref, g_ref, y_ref, x_ref):
        y = _dot(a_ref[...].astype(BF16), b_ref[...].astype(BF16))
        y_ref[...] = y
        x_ref[...] = r_ref[...] + g_ref[...] * y

    out = jax.ShapeDtypeStruct((M, N), F32)
    return pl.pallas_call(
        body, name=name, grid=(M // tm, N // tn),
        in_specs=[pl.BlockSpec((tm, K), lambda i, j: (i, 0)), pl.BlockSpec((K, tn), lambda i, j: (0, j)),
                  pl.BlockSpec((tm, tn), lambda i, j: (i, j)), pl.BlockSpec((None, 1, tn), lambda i, j: (lax.div(i, jnp.int32(per_seq)), 0, j))],
        out_specs=[pl.BlockSpec((tm, tn), lambda i, j: (i, j))] * 2,
        out_shape=[out, out], compiler_params=_cp("parallel", "parallel"),
    )(a, b, res, gate)


def rms_fwd(x3, blk, W, g, sc=None, sh=None, *, tm=512, name):
    Bl, S, _ = x3.shape
    tm = min(tm, S)
    mod = sc is not None

    def body(x_ref, g_ref, *rest):
        o_ref = rest[-1]
        x = x_ref[...]
        r = lax.rsqrt(jnp.mean(x * x, axis=-1, keepdims=True) + EPS)
        y = x * r * g_ref[...]
        if mod:
            y = y * (1.0 + rest[0][...]) + rest[1][...]
        o_ref[...] = y.astype(BF16)

    vec = pl.BlockSpec((None, 1, W), lambda b, s: (b, 0, 0))
    return pl.pallas_call(
        body, name=name, grid=(Bl, S // tm),
        in_specs=[pl.BlockSpec((None, tm, W), lambda b, s: (b, s, blk)), pl.BlockSpec((1, W), lambda b, s: (0, 0))] + ([vec, vec] if mod else []),
        out_specs=pl.BlockSpec((None, tm, W), lambda b, s: (b, s, 0)),
        out_shape=jax.ShapeDtypeStruct((Bl, S, W), BF16),
        compiler_params=_cp("parallel", "parallel"),
    )(x3, g, *([sc, sh] if mod else []))


def rms_bwd(x3, blk, W, dy3, g, sc=None, dres3=None, *, tm=256, name):
    Bl, S, _ = x3.shape
    tm = min(tm, S)
    mod = sc is not None
    res = dres3 is not None

    def body(*refs):
        x_ref, dy_ref, g_ref = refs[:3]
        k = 3
        sc_ref = dr_ref = None
        if mod:
            sc_ref = refs[k]
            k += 1
        if res:
            dr_ref = refs[k]
            k += 1
        dx_ref, dg_ref = refs[k], refs[k + 1]
        b, s = pl.program_id(0), pl.program_id(1)
        x = x_ref[...]
        dy = dy_ref[...].astype(F32)
        g = g_ref[...]
        r = lax.rsqrt(jnp.mean(x * x, axis=-1, keepdims=True) + EPS)
        n = x * r
        if mod:
            dsc_ref, dsh_ref = refs[k + 2], refs[k + 3]
            one_sc = 1.0 + sc_ref[...]

            @pl.when(s == 0)
            def _():
                dsc_ref[...] = jnp.zeros_like(dsc_ref)
                dsh_ref[...] = jnp.zeros_like(dsh_ref)

            dsh_ref[...] += jnp.sum(dy, axis=0, keepdims=True)
            dsc_ref[...] += jnp.sum(dy * n * g, axis=0, keepdims=True)
            dyn = dy * one_sc
        else:
            dyn = dy

        @pl.when((b == 0) & (s == 0))
        def _():
            dg_ref[...] = jnp.zeros_like(dg_ref)

        dg_ref[...] += jnp.sum(dyn * n, axis=0, keepdims=True)
        dn = dyn * g
        dx = r * (dn - n * jnp.mean(dn * n, axis=-1, keepdims=True))
        if res:
            dx = dx + dr_ref[...]
        dx_ref[...] = dx

    blkspec = pl.BlockSpec((None, tm, W), lambda b, s: (b, s, 0))
    vec = pl.BlockSpec((None, 1, W), lambda b, s: (b, 0, 0))
    row = pl.BlockSpec((1, W), lambda b, s: (0, 0))
    in_specs = [pl.BlockSpec((None, tm, W), lambda b, s: (b, s, blk)), blkspec, row] + ([vec] if mod else []) + ([blkspec] if res else [])
    out_specs = [blkspec, row] + ([vec, vec] if mod else [])
    out_shape = [jax.ShapeDtypeStruct((Bl, S, W), F32), jax.ShapeDtypeStruct((1, W), F32)]
    if mod:
        out_shape += [jax.ShapeDtypeStruct((Bl, 1, W), F32)] * 2
    args = [x3, dy3, g] + ([sc] if mod else []) + ([dres3] if res else [])
    return pl.pallas_call(
        body, name=name, grid=(Bl, S // tm), in_specs=in_specs, out_specs=out_specs, out_shape=out_shape,
        compiler_params=_cp("arbitrary", "arbitrary"),
    )(*args)


def pair_rms_fwd(x3, blk0, npairs, g2, *, tm=1024, name):
    Bl, S, _ = x3.shape
    tm = min(tm, S)

    def body(x_ref, g_ref, o_ref):
        lo, hi = _lane_masks()
        x = x_ref[...]
        xx = x * x
        s0 = jnp.sum(jnp.where(lo, xx, 0.0), axis=-1, keepdims=True)
        s1 = jnp.sum(jnp.where(hi, xx, 0.0), axis=-1, keepdims=True)
        r = jnp.where(lo, lax.rsqrt(s0 / HEAD + EPS), lax.rsqrt(s1 / HEAD + EPS))
        o_ref[...] = (x * r * g_ref[...]).astype(BF16)

    return pl.pallas_call(
        body, name=name, grid=(Bl, S // tm, npairs),
        in_specs=[pl.BlockSpec((None, tm, LANES), lambda b, s, p: (b, s, blk0 + p)), pl.BlockSpec((1, LANES), lambda b, s, p: (0, 0))],
        out_specs=pl.BlockSpec((None, tm, LANES), lambda b, s, p: (b, s, p)),
        out_shape=jax.ShapeDtypeStruct((Bl, S, LANES * npairs), BF16),
        compiler_params=_cp("parallel", "parallel", "parallel"),
    )(x3, g2)


def pair_rms_bwd(x3, blk0, npairs, dy3, g2, *, tm=1024, name):
    Bl, S, _ = x3.shape
    tm = min(tm, S)

    def body(x_ref, dy_ref, g_ref, dx_ref, dg_ref):
        lo, hi = _lane_masks()
        first = (pl.program_id(0) == 0) & (pl.program_id(1) == 0) & (pl.program_id(2) == 0)
        x = x_ref[...]
        dy = dy_ref[...]
        xx = x * x
        s0 = jnp.sum(jnp.where(lo, xx, 0.0), axis=-1, keepdims=True)
        s1 = jnp.sum(jnp.where(hi, xx, 0.0), axis=-1, keepdims=True)
        r = jnp.where(lo, lax.rsqrt(s0 / HEAD + EPS), lax.rsqrt(s1 / HEAD + EPS))
        n = x * r

        @pl.when(first)
        def _():
            dg_ref[...] = jnp.zeros_like(dg_ref)

        part = jnp.sum(dy * n, axis=0, keepdims=True)
        dg_ref[...] += part + pltpu.roll(part, HEAD, 1)
        dn = dy * g_ref[...]
        t = dn * n
        m0 = jnp.sum(jnp.where(lo, t, 0.0), axis=-1, keepdims=True)
        m1 = jnp.sum(jnp.where(hi, t, 0.0), axis=-1, keepdims=True)
        dx_ref[...] = r * (dn - n * (jnp.where(lo, m0, m1) / HEAD))

    return pl.pallas_call(
        body, name=name, grid=(Bl, S // tm, npairs),
        in_specs=[pl.BlockSpec((None, tm, LANES), lambda b, s, p: (b, s, blk0 + p)), pl.BlockSpec((None, tm, LANES), lambda b, s, p: (b, s, p)),
                  pl.BlockSpec((1, LANES), lambda b, s, p: (0, 0))],
        out_specs=[pl.BlockSpec((None, tm, LANES), lambda b, s, p: (b, s, p)), pl.BlockSpec((1, LANES), lambda b, s, p: (0, 0))],
        out_shape=[jax.ShapeDtypeStruct((Bl, S, LANES * npairs), F32), jax.ShapeDtypeStruct((1, LANES), F32)],
        compiler_params=_cp("arbitrary", "arbitrary", "arbitrary"),
    )(x3, dy3, g2)


def _rot(y, cos_t, sin_a, sin_b):
    return y * cos_t + pltpu.roll(y, LANES - 16, 1) * sin_a + pltpu.roll(y, 16, 1) * sin_b


def _rot_t(d, cos_t, sin_a, sin_b):
    return d * cos_t + pltpu.roll(d * sin_a, 16, 1) + pltpu.roll(d * sin_b, LANES - 16, 1)


def rope_norm_fwd(x3, nheads, g, tabs, slab=None, *, tm=1024, name):
    Bl, S, _ = x3.shape
    tm = min(tm, S)
    has_slab = slab is not None

    def body(*refs):
        x_ref, g_ref, c_ref, sa_ref, sb_ref = refs[:5]
        o_ref = refs[-1]
        x = x_ref[...]
        if has_slab:
            x = x + refs[5][...]
        r = lax.rsqrt(jnp.sum(x * x, axis=-1, keepdims=True) / MLA_QK + EPS)
        o_ref[...] = _rot(x * r * g_ref[...], c_ref[...], sa_ref[...], sb_ref[...]).astype(BF16)

    head = pl.BlockSpec((None, tm, LANES), lambda b, s, h: (b, s, h))
    tab = pl.BlockSpec((None, tm, LANES), lambda b, s, h: (b, s, 0))
    in_specs = [head, pl.BlockSpec((1, LANES), lambda b, s, h: (0, 0)), tab, tab, tab]
    args = [x3, g, *tabs]
    if has_slab:
        sblk = slab[1]
        in_specs.append(pl.BlockSpec((None, tm, LANES), lambda b, s, h: (b, s, sblk)))
        args.append(slab[0])
    return pl.pallas_call(
        body, name=name, grid=(Bl, S // tm, nheads), in_specs=in_specs, out_specs=head,
        out_shape=jax.ShapeDtypeStruct((Bl, S, LANES * nheads), BF16),
        compiler_params=_cp("parallel", "parallel", "parallel"),
    )(*args)


def rope_norm_bwd(x3, nheads, dy3, g, tabs, slab=None, *, tm=1024, name):
    Bl, S, _ = x3.shape
    tm = min(tm, S)
    has_slab = slab is not None

    def body(*refs):
        x_ref, dy_ref, g_ref, c_ref, sa_ref, sb_ref = refs[:6]
        k = 7 if has_slab else 6
        dx_ref, dg_ref = refs[k], refs[k + 1]
        h = pl.program_id(2)
        first = (pl.program_id(0) == 0) & (pl.program_id(1) == 0) & (h == 0)
        x = x_ref[...]
        if has_slab:
            x = x + refs[6][...]
        g = g_ref[...]
        r = lax.rsqrt(jnp.sum(x * x, axis=-1, keepdims=True) / MLA_QK + EPS)
        n = x * r
        d = _rot_t(dy_ref[...], c_ref[...], sa_ref[...], sb_ref[...])

        @pl.when(first)
        def _():
            dg_ref[...] = jnp.zeros_like(dg_ref)

        dg_ref[...] += jnp.sum(d * n, axis=0, keepdims=True)
        dn = d * g
        dx = r * (dn - n * (jnp.sum(dn * n, axis=-1, keepdims=True) / MLA_QK))
        dx_ref[...] = dx
        if has_slab:
            ds_ref = refs[k + 2]

            @pl.when(h == 0)
            def _():
                ds_ref[...] = dx

            @pl.when(h > 0)
            def _():
                ds_ref[...] += dx

    head = pl.BlockSpec((None, tm, LANES), lambda b, s, h: (b, s, h))
    tab = pl.BlockSpec((None, tm, LANES), lambda b, s, h: (b, s, 0))
    row = pl.BlockSpec((1, LANES), lambda b, s, h: (0, 0))
    in_specs = [head, head, row, tab, tab, tab]
    args = [x3, dy3, g, *tabs]
    out_specs = [head, row]
    out_shape = [jax.ShapeDtypeStruct((Bl, S, LANES * nheads), F32), jax.ShapeDtypeStruct((1, LANES), F32)]
    if has_slab:
        sblk = slab[1]
        in_specs.append(pl.BlockSpec((None, tm, LANES), lambda b, s, h: (b, s, sblk)))
        args.append(slab[0])
        out_specs.append(tab)
        out_shape.append(jax.ShapeDtypeStruct((Bl, S, LANES), F32))
    return pl.pallas_call(
        body, name=name, grid=(Bl, S // tm, nheads), in_specs=in_specs, out_specs=out_specs, out_shape=out_shape,
        compiler_params=_cp("arbitrary", "arbitrary", "arbitrary"),
    )(*args)


def _softplus(z):
    return jnp.maximum(z, 0.0) + jnp.log(1.0 + jnp.exp(-jnp.abs(z)))


def _split_dots(xs, u):
    hi = [x.astype(BF16) for x in xs]
    lo = [(x - h.astype(F32)).astype(BF16) for x, h in zip(xs, hi)]
    top = [_dot(h, u) for h in hi]
    return [t + _dot(l, u) for t, l in zip(top, lo)]


SB_BLOCK = 256
SB_QBLOCK = 512


def sb_attn_fwd(proj3, *, plans=None, name):
    Bl, S, _ = proj3.shape
    tk = min(SB_BLOCK, S)
    tq = min(SB_QBLOCK, S)
    per_q = tq // tk
    scale = HEAD ** -0.5
    qb, kb0, vb0 = P_SBQ // LANES, P_SBK // LANES, P_SBV // LANES

    def body(q_ref, k_ref, v_ref, o_ref, rt_ref):
        i = pl.program_id(2)
        masks = _lane_masks()
        lane = lax.broadcasted_iota(jnp.int32, (1, LANES), 1)
        q = q_ref[...]
        qh = [jnp.where(m, q, 0.0).astype(BF16) for m in masks]
        rr = lax.broadcasted_iota(jnp.int32, (tq, tk), 0)
        cc = lax.broadcasted_iota(jnp.int32, (tq, tk), 1)
        u = (lax.broadcasted_iota(jnp.int32, (tk, tk), 0) > lax.broadcasted_iota(jnp.int32, (tk, tk), 1)).astype(BF16)

        rt_ref[...] = jnp.zeros_like(rt_ref)

        def step(t, carry):
            r0, r1, acc = carry
            j = (i + 1) * per_q - 1 - t
            off = pl.multiple_of(j * tk, tk)
            kb = k_ref[pl.ds(off, tk), :].astype(BF16)
            vb = v_ref[pl.ds(off, tk), :]
            strict = (cc + j * tk) < (rr + i * tq)
            rt_ref[...] = jnp.where(lane == j, r0, jnp.where(lane == j + HEAD, r1, rt_ref[...]))
            rs, two = [r0, r1], range(2)
            z = [_dot_nt(qh[h], kb) * scale for h in two]
            sp = [_softplus(z[h]) for h in two]
            keep = [jnp.where(strict, -sp[h], 0.0) for h in two]
            suf = _split_dots(keep, u)
            w = [jnp.where(strict, jnp.exp((z[h] - sp[h]) + suf[h] + rs[h]), 0.0) for h in two]
            pv = [_dot(w[h].astype(BF16), jnp.where(masks[h], vb, 0.0).astype(BF16)) for h in two]
            return rs[0] + jnp.sum(keep[0], axis=1, keepdims=True), rs[1] + jnp.sum(keep[1], axis=1, keepdims=True), acc + (pv[0] + pv[1])

        zero = jnp.zeros((tq, 1), F32)
        _, _, acc = lax.fori_loop(0, (i + 1) * per_q, step, (zero, zero, jnp.zeros((tq, LANES), F32)))
        o_ref[...] = acc

    seq = lambda blk0: pl.BlockSpec((None, S, LANES), lambda b, p, i: (b, 0, blk0 + p))
    out = pl.BlockSpec((None, tq, LANES), lambda b, p, i: (b, i, p))
    shp = jax.ShapeDtypeStruct((Bl, S, 2 * LANES), F32)
    return call_with_plans(
        body, plans, name=name, grid=(Bl, 2, S // tq),
        in_specs=[pl.BlockSpec((None, tq, LANES), lambda b, p, i: (b, i, qb + p)), seq(kb0), seq(vb0)],
        out_specs=[out, out], out_shape=[shp, shp], scratch_shapes=[], args=[proj3, proj3, proj3],
        sem=("arbitrary",) * 3 if plans else ("parallel", "parallel", "arbitrary"))


def sb_attn_bwd(proj3, rt3, do3, *, plans=None, name):
    Bl, S, _ = proj3.shape
    tk = min(SB_BLOCK, S)
    tq = min(SB_QBLOCK, S)
    per_q = tq // tk
    scale = HEAD ** -0.5
    qb, kb0, vb0 = P_SBQ // LANES, P_SBK // LANES, P_SBV // LANES

    def body(q_ref, k_ref, v_ref, rt_ref, do_ref, dq_ref, dk_ref, dv_ref):
        i = pl.program_id(2)

        @pl.when(i == 0)
        def _():
            dk_ref[...] = jnp.zeros_like(dk_ref)
            dv_ref[...] = jnp.zeros_like(dv_ref)

        masks = _lane_masks()
        lane = lax.broadcasted_iota(jnp.int32, (1, LANES), 1)
        q = q_ref[...]
        qh = [jnp.where(m, q, 0.0).astype(BF16) for m in masks]
        do_b = do_ref[...].astype(BF16)
        doh = [jnp.where(m, do_b, jnp.zeros_like(do_b)) for m in masks]
        rt = rt_ref[...]
        rr = lax.broadcasted_iota(jnp.int32, (tq, tk), 0)
        cc = lax.broadcasted_iota(jnp.int32, (tq, tk), 1)
        ur = lax.broadcasted_iota(jnp.int32, (tk, tk), 0)
        uc = lax.broadcasted_iota(jnp.int32, (tk, tk), 1)
        u_suffix = (ur > uc).astype(BF16)
        u_prefix = (ur < uc).astype(BF16)

        def step(j, carry):
            p0, p1, dq = carry
            off = pl.multiple_of(j * tk, tk)
            kf = k_ref[pl.ds(off, tk), :]
            kb = kf.astype(BF16)
            vb = v_ref[pl.ds(off, tk), :]
            strict = (cc + j * tk) < (rr + i * tq)
            ps, two = [p0, p1], range(2)
            r_j = [jnp.sum(jnp.where(lane == j + h * HEAD, rt, 0.0), axis=1, keepdims=True) for h in two]
            z = [_dot_nt(qh[h], kb) * scale for h in two]
            dw = [_dot_nt(doh[h], jnp.where(masks[h], vb, 0.0).astype(BF16)) for h in two]
            sp = [_softplus(z[h]) for h in two]
            keep = [jnp.where(strict, -sp[h], 0.0) for h in two]
            suf = _split_dots(keep, u_suffix)
            w = [jnp.where(strict, jnp.exp((z[h] - sp[h]) + suf[h] + r_j[h]), 0.0) for h in two]
            g = [dw[h] * w[h] for h in two]
            pre = _split_dots(g, u_prefix)
            dzb = [(jnp.where(strict, g[h] * jnp.exp(-sp[h]) - jnp.exp(z[h] - sp[h]) * (pre[h] + ps[h]), 0.0) * scale).astype(BF16) for h in two]
            dqs = [_dot(dzb[h], jnp.where(masks[h], kf, 0.0).astype(BF16)) for h in two]
            dks = [_dot_tn(dzb[h], qh[h]) for h in two]
            dvs = [_dot_tn(w[h].astype(BF16), doh[h]) for h in two]
            dk_ref[pl.ds(off, tk), :] += dks[0] + dks[1]
            dv_ref[pl.ds(off, tk), :] += dvs[0] + dvs[1]
            return ps[0] + jnp.sum(g[0], axis=1, keepdims=True), ps[1] + jnp.sum(g[1], axis=1, keepdims=True), dq + (dqs[0] + dqs[1])

        zero = jnp.zeros((tq, 1), F32)
        out = lax.fori_loop(0, (i + 1) * per_q, step, (zero, zero, jnp.zeros((tq, LANES), F32)))
        dq_ref[...] = out[2]

    seq_in = lambda blk0: pl.BlockSpec((None, S, LANES), lambda b, p, i: (b, 0, blk0 + p))
    blk = pl.BlockSpec((None, tq, LANES), lambda b, p, i: (b, i, p))
    seq_out = pl.BlockSpec((None, S, LANES), lambda b, p, i: (b, 0, p))
    shp = jax.ShapeDtypeStruct((Bl, S, 2 * LANES), F32)
    return call_with_plans(
        body, plans, name=name, grid=(Bl, 2, S // tq),
        in_specs=[pl.BlockSpec((None, tq, LANES), lambda b, p, i: (b, i, qb + p)), seq_in(kb0), seq_in(vb0), blk, blk],
        out_specs=[blk, seq_out, seq_out], out_shape=[shp, shp, shp], scratch_shapes=[], args=[proj3, proj3, proj3, rt3, do3],
        sem=("arbitrary",) * 3 if plans else ("parallel", "parallel", "arbitrary"))


def mla_attn_fwd(q3, k3, kv3, vblk0, *, tq=512, tk=256, plans=None, name):
    Bl, S, _ = q3.shape
    tq = min(tq, S)
    tk = min(tk, tq)
    per_q = tq // tk
    scale = MLA_QK ** -0.5

    def body(q_ref, k_ref, v_ref, o_ref, lse_ref):
        i = pl.program_id(2)
        masks = _lane_masks()
        rr = lax.broadcasted_iota(jnp.int32, (tq, tk), 0)
        cc = lax.broadcasted_iota(jnp.int32, (tq, tk), 1)
        qh = [q_ref[:, h * LANES:(h + 1) * LANES] for h in range(2)]

        def step(j, carry):
            m0, l0, m1, l1, acc = carry
            off = pl.multiple_of(j * tk, tk)
            vb = v_ref[pl.ds(off, tk), :]
            causal = (cc + j * tk) <= (rr + i * tq)
            ms, ls, two = [m0, m1], [l0, l1], range(2)
            kh = [k_ref[pl.ds(off, tk), h * LANES:(h + 1) * LANES] for h in two]
            s = [jnp.where(causal, _dot_nt(qh[h], kh[h]) * scale, NEG) for h in two]
            m_new = [jnp.maximum(ms[h], jnp.max(s[h], axis=1, keepdims=True)) for h in two]
            p = [jnp.exp(s[h] - m_new[h]) for h in two]
            alpha = [jnp.exp(ms[h] - m_new[h]) for h in two]
            ls = [alpha[h] * ls[h] + jnp.sum(p[h], axis=1, keepdims=True) for h in two]
            add = [_dot(p[h].astype(BF16), jnp.where(masks[h], vb, 0.0).astype(BF16)) for h in two]
            acc = acc * jnp.where(masks[0], alpha[0], alpha[1]) + (add[0] + add[1])
            return m_new[0], ls[0], m_new[1], ls[1], acc

        neg = jnp.full((tq, 1), NEG, F32)
        zero = jnp.zeros((tq, 1), F32)
        m0, l0, m1, l1, acc = lax.fori_loop(0, (i + 1) * per_q, step, (neg, zero, neg, zero, jnp.zeros((tq, LANES), F32)))
        o_ref[...] = acc / jnp.where(masks[0], l0, l1)
        lse_ref[...] = jnp.where(masks[0], m0 + jnp.log(l0), m1 + jnp.log(l1))

    out = pl.BlockSpec((None, tq, LANES), lambda b, p, i: (b, i, p))
    shp = jax.ShapeDtypeStruct((Bl, S, 3 * LANES), F32)
    return call_with_plans(
        body, plans, name=name, grid=(Bl, 3, S // tq),
        in_specs=[pl.BlockSpec((None, tq, 2 * LANES), lambda b, p, i: (b, i, p)), pl.BlockSpec((None, S, 2 * LANES), lambda b, p, i: (b, 0, p)),
                  pl.BlockSpec((None, S, LANES), lambda b, p, i: (b, 0, vblk0 + p))],
        out_specs=[out, out], out_shape=[shp, shp], scratch_shapes=[], args=[q3, k3, kv3],
        sem=("arbitrary",) * 3 if plans else ("parallel", "parallel", "arbitrary"))


def mla_attn_bwd(q3, k3, kv3, vblk0, o3, lse3, do3, *, tq=512, tk=256, name):
    Bl, S, _ = q3.shape
    tq = min(tq, S)
    tk = min(tk, tq)
    per_q = tq // tk
    nq = S // tq
    scale = MLA_QK ** -0.5

    def body(q_ref, k_ref, v_ref, o_ref, lse_ref, do_ref, dq_ref, dk_ref, dv_ref, s_scr, dp_scr, p_scr, ds_scr):
        j = pl.program_id(2)

        @pl.when(j == 0)
        def _():
            dq_ref[...] = jnp.zeros_like(dq_ref)

        masks = _lane_masks()
        vb = v_ref[...]
        vh = [jnp.where(m, vb, 0.0).astype(BF16) for m in masks]
        kh = [k_ref[:, h * LANES:(h + 1) * LANES] for h in range(2)]
        i0 = lax.div(j, jnp.int32(per_q))

        def step(i, carry, masked):
            dk0, dk1, dv = carry
            off = pl.multiple_of(i * tq, tq)
            do_b = do_ref[pl.ds(off, tq), :].astype(BF16)
            prod = do_b.astype(F32) * o_ref[pl.ds(off, tq), :]
            lse = lse_ref[pl.ds(off, tq), :]
            two = range(2)
            qh = [q_ref[pl.ds(off, tq), h * LANES:(h + 1) * LANES] for h in two]
            doh = [jnp.where(masks[h], do_b, jnp.zeros_like(do_b)) for h in two]
            delta = [jnp.sum(jnp.where(masks[h], prod, 0.0), axis=1, keepdims=True) for h in two]
            lse_h = [lse[:, h * HEAD:h * HEAD + 1] for h in two]
            for h in two:
                s_scr[h] = _dot_nt(qh[h], kh[h])
            for h in two:
                dp_scr[h] = _dot_nt(doh[h], vh[h])
            for r0 in range(0, tq, STRIP):
                rows = slice(r0, r0 + STRIP)
                for h in two:
                    s = s_scr[h, rows, :] * scale
                    if masked:
                        rr = lax.broadcasted_iota(jnp.int32, (STRIP, tk), 0) + (i * tq + r0)
                        cc = lax.broadcasted_iota(jnp.int32, (STRIP, tk), 1) + j * tk
                        s = jnp.where(cc <= rr, s, NEG)
                    p = jnp.exp(s - lse_h[h][rows])
                    p_scr[h, rows, :] = p.astype(BF16)
                    ds_scr[h, rows, :] = (p * (dp_scr[h, rows, :] - delta[h][rows])).astype(BF16)
            dqs = [_dot(ds_scr[h], kh[h]) * scale for h in two]
            dks = [dk0 + _dot_tn(ds_scr[0], qh[0]), dk1 + _dot_tn(ds_scr[1], qh[1])]
            dv = dv + _dot_tn(p_scr[0], doh[0]) + _dot_tn(p_scr[1], doh[1])
            for h in two:
                dq_ref[pl.ds(off, tq), h * LANES:(h + 1) * LANES] += dqs[h]
            return dks[0], dks[1], dv

        zero = jnp.zeros((tk, LANES), F32)
        carry = step(i0, (zero, zero, zero), True)
        dk0, dk1, dv = lax.fori_loop(i0 + 1, nq, lambda i, c: step(i, c, False), carry)
        dk_ref[:, 0:LANES] = dk0 * scale
        dk_ref[:, LANES:2 * LANES] = dk1 * scale
        dv_ref[...] = dv

    seq1 = pl.BlockSpec((None, S, LANES), lambda b, p, j: (b, 0, p))
    seq2 = pl.BlockSpec((None, S, 2 * LANES), lambda b, p, j: (b, 0, p))
    return pl.pallas_call(
        body, name=name, grid=(Bl, 3, S // tk),
        in_specs=[seq2, pl.BlockSpec((None, tk, 2 * LANES), lambda b, p, j: (b, j, p)),
                  pl.BlockSpec((None, tk, LANES), lambda b, p, j: (b, j, vblk0 + p)), seq1, seq1, seq1],
        out_specs=[seq2, pl.BlockSpec((None, tk, 2 * LANES), lambda b, p, j: (b, j, p)), pl.BlockSpec((None, tk, LANES), lambda b, p, j: (b, j, p))],
        out_shape=[jax.ShapeDtypeStruct((Bl, S, 6 * LANES), F32), jax.ShapeDtypeStruct((Bl, S, 6 * LANES), F32), jax.ShapeDtypeStruct((Bl, S, 3 * LANES), F32)],
        scratch_shapes=[pltpu.VMEM((2, tq, tk), F32), pltpu.VMEM((2, tq, tk), F32), pltpu.VMEM((2, tq, tk), BF16), pltpu.VMEM((2, tq, tk), BF16)],
        compiler_params=_cp("parallel", "parallel", "arbitrary"),
    )(q3, k3, kv3, o3, lse3, do3)


def _bucket_table():
    a = jnp.arange(WINDOW)[:, None]
    b = jnp.arange(2 * WINDOW)[None, :]
    dist = WINDOW + a - b
    max_exact = REL_BUCKETS // 2
    n = jnp.maximum(dist, 0)
    nf = jnp.maximum(n, 1).astype(F32)
    large = max_exact + (jnp.log(nf / max_exact) / math.log(REL_MAX_DIST / max_exact) * (REL_BUCKETS - max_exact)).astype(jnp.int32)
    large = jnp.minimum(large, REL_BUCKETS - 1)
    bucket = jnp.where(n < max_exact, n, large)
    return jnp.where((dist >= 0) & (dist < WINDOW), bucket, -1).astype(jnp.int32)


def swa_bias(rel_flat, bucket, *, name):
    def body(t_ref, b_ref, o_ref):
        bk = b_ref[...]
        for p in range(3):
            for hh in range(2):
                h = hh * 3 + p
                acc = jnp.full(bk.shape, NEG, F32)
                for b in range(REL_BUCKETS):
                    acc = jnp.where(bk == b, t_ref[b * 6 + h], acc)
                o_ref[p, hh] = acc

    return pl.pallas_call(
        body, name=name,
        in_specs=[pl.BlockSpec(memory_space=pltpu.SMEM), pl.BlockSpec(memory_space=pltpu.VMEM)],
        out_specs=pl.BlockSpec(memory_space=pltpu.VMEM),
        out_shape=jax.ShapeDtypeStruct((3, 2, WINDOW, 2 * WINDOW), F32),
    )(rel_flat, bucket)


def swa_bias_bwd(dbias, bucket, *, name):
    Bl = dbias.shape[0]

    def body(d_ref, b_ref, o_ref):
        bk = b_ref[...]
        lane = lax.broadcasted_iota(jnp.int32, (1, LANES), 1)
        rows = []
        for h in range(6):
            hh, p = divmod(h, 3)
            d = d_ref[0, p, hh]
            for bl in range(1, Bl):
                d = d + d_ref[bl, p, hh]
            row = jnp.zeros((1, LANES), F32)
            for b in range(REL_BUCKETS):
                s = jnp.sum(jnp.sum(jnp.where(bk == b, d, 0.0), axis=1, keepdims=True), axis=0, keepdims=True)
                row = row + jnp.where(lane == b, s, 0.0)
            rows.append(row)
        rows += [jnp.zeros((1, LANES), F32)] * 2
        o_ref[...] = jnp.concatenate(rows, axis=0)

    return pl.pallas_call(
        body, name=name,
        in_specs=[pl.BlockSpec(memory_space=pltpu.VMEM)] * 2, out_specs=pl.BlockSpec(memory_space=pltpu.VMEM),
        out_shape=jax.ShapeDtypeStruct((8, LANES), F32),
    )(dbias, bucket)


SWA_QBLOCKS = 4


def _swa_specs(vblk, nqb):
    rows = nqb * WINDOW
    cur = lambda blk: pl.BlockSpec((None, rows, LANES), lambda b, p, n: (b, n, blk))
    prev = lambda blk: pl.BlockSpec((None, WINDOW, LANES), lambda b, p, n: (b, jnp.maximum(n * nqb - 1, 0), blk))
    return [pl.BlockSpec((None, rows, LANES), lambda b, p, n: (b, n, p)), cur(0), prev(0), cur(vblk), prev(vblk),
            pl.BlockSpec((None, 2, WINDOW, 2 * WINDOW), lambda b, p, n: (p, 0, 0, 0)), pl.BlockSpec((None, 2, LANES), lambda b, p, n: (p, 0, 0))]


def _rows128(ref, m):
    return ref[m * WINDOW:(m + 1) * WINDOW, :]


def _swa_logits(qh, kp, kc, bias_h, first, scale):
    sp = jnp.where(first, NEG, _dot_nt(qh, kp) * scale + bias_h[:, :WINDOW])
    sc = _dot_nt(qh, kc) * scale + bias_h[:, WINDOW:]
    return sp, sc


def swa_attn_fwd(qn3, kn3, proj3, bias, sinks, *, plans=None, name):
    Bl, S, _ = qn3.shape
    scale = HEAD ** -0.5
    nqb = min(SWA_QBLOCKS, S // WINDOW)

    def body(q_ref, kc_ref, kp_ref, vc_ref, vp_ref, b_ref, s_ref, o_ref, lse_ref):
        seq_start = pl.program_id(2) == 0
        masks = _lane_masks()
        chains = [(m_, h) for m_ in range(nqb) for h in range(2)]
        kp = [kp_ref[...] if m_ == 0 else _rows128(kc_ref, m_ - 1) for m_ in range(nqb)]
        vp = [vp_ref[...] if m_ == 0 else _rows128(vc_ref, m_ - 1) for m_ in range(nqb)]
        kc = [_rows128(kc_ref, m_) for m_ in range(nqb)]
        vc = [_rows128(vc_ref, m_) for m_ in range(nqb)]
        sink = [s_ref[h:h + 1, 0:1] for h in range(2)]
        logits = {}
        for m_, h in chains:
            q = _rows128(q_ref, m_)
            qh = jnp.where(masks[h], q, jnp.zeros_like(q))
            logits[m_, h] = _swa_logits(qh, kp[m_], kc[m_], b_ref[h], seq_start if m_ == 0 else False, scale)
        mx = {c: jnp.maximum(jnp.maximum(jnp.max(logits[c][0], axis=1, keepdims=True), jnp.max(logits[c][1], axis=1, keepdims=True)), sink[c[1]])
              for c in chains}
        ex = {c: (jnp.exp(logits[c][0] - mx[c]), jnp.exp(logits[c][1] - mx[c])) for c in chains}
        den = {c: jnp.sum(ex[c][0], axis=1, keepdims=True) + jnp.sum(ex[c][1], axis=1, keepdims=True) + jnp.exp(sink[c[1]] - mx[c]) for c in chains}
        inv = {c: 1.0 / den[c] for c in chains}
        out = {}
        for m_, h in chains:
            c = (m_, h)
            out[c] = (_dot((ex[c][0] * inv[c]).astype(BF16), jnp.where(masks[h], vp[m_], 0.0).astype(BF16))
                      + _dot((ex[c][1] * inv[c]).astype(BF16), jnp.where(masks[h], vc[m_], 0.0).astype(BF16)))
        for m_ in range(nqb):
            o_ref[m_ * WINDOW:(m_ + 1) * WINDOW, :] = out[m_, 0] + out[m_, 1]
            lse_ref[m_ * WINDOW:(m_ + 1) * WINDOW, :] = jnp.where(masks[0], mx[m_, 0] + jnp.log(den[m_, 0]), mx[m_, 1] + jnp.log(den[m_, 1]))

    out = pl.BlockSpec((None, nqb * WINDOW, LANES), lambda b, p, n: (b, n, p))
    shp = jax.ShapeDtypeStruct((Bl, S, 3 * LANES), F32)
    return call_with_plans(
        body, plans, name=name, grid=(Bl, 3, S // (nqb * WINDOW)), in_specs=_swa_specs(P_SWV // LANES, nqb),
        out_specs=[out, out], out_shape=[shp, shp], scratch_shapes=[], args=[qn3, kn3, kn3, proj3, proj3, bias, sinks],
        sem=("arbitrary",) * 3 if plans else ("parallel", "parallel", "arbitrary"))


def swa_attn_bwd(qn3, kn3, proj3, bias, sinks, o3, lse3, do3, *, name):
    Bl, S, _ = qn3.shape
    scale = HEAD ** -0.5
    nqb = min(SWA_QBLOCKS, S // WINDOW)
    rows = nqb * WINDOW

    def body(q_ref, kc_ref, kp_ref, vc_ref, vp_ref, b_ref, s_ref, o_ref, lse_ref, do_ref,
             dq_ref, dk_ref, dv_ref, db_ref, dsk_ref):
        p_id, n = pl.program_id(1), pl.program_id(2)
        seq_start = n == 0

        @pl.when((p_id == 0) & seq_start)
        def _():
            dk_ref[...] = jnp.zeros_like(dk_ref)
            dv_ref[...] = jnp.zeros_like(dv_ref)

        @pl.when(seq_start)
        def _():
            db_ref[...] = jnp.zeros_like(db_ref)
            dsk_ref[...] = jnp.zeros_like(dsk_ref)

        masks = _lane_masks()
        zero = jnp.zeros((WINDOW, LANES), F32)
        chains = [(m_, h) for m_ in range(nqb) for h in range(2)]
        kp = [kp_ref[...] if m_ == 0 else _rows128(kc_ref, m_ - 1) for m_ in range(nqb)]
        vp = [vp_ref[...] if m_ == 0 else _rows128(vc_ref, m_ - 1) for m_ in range(nqb)]
        kc = [_rows128(kc_ref, m_) for m_ in range(nqb)]
        vc = [_rows128(vc_ref, m_) for m_ in range(nqb)]
        do_b = [_rows128(do_ref, m_).astype(BF16) for m_ in range(nqb)]
        prod = [do_b[m_].astype(F32) * _rows128(o_ref, m_) for m_ in range(nqb)]
        lse = [_rows128(lse_ref, m_) for m_ in range(nqb)]
        qh, doh, logits, lse_h, delta = {}, {}, {}, {}, {}
        for m_, h in chains:
            q = _rows128(q_ref, m_)
            qh[m_, h] = jnp.where(masks[h], q, jnp.zeros_like(q))
            doh[m_, h] = jnp.where(masks[h], do_b[m_], jnp.zeros_like(do_b[m_]))
            logits[m_, h] = _swa_logits(qh[m_, h], kp[m_], kc[m_], b_ref[h], seq_start if m_ == 0 else False, scale)
            lse_h[m_, h] = lse[m_][:, h * HEAD:h * HEAD + 1]
            delta[m_, h] = jnp.sum(jnp.where(masks[h], prod[m_], 0.0), axis=1, keepdims=True)
        pr = {c: (jnp.exp(logits[c][0] - lse_h[c]), jnp.exp(logits[c][1] - lse_h[c])) for c in chains}
        dp = {(m_, h): (_dot_nt(doh[m_, h], jnp.where(masks[h], vp[m_], 0.0).astype(BF16)),
                        _dot_nt(doh[m_, h], jnp.where(masks[h], vc[m_], 0.0).astype(BF16))) for m_, h in chains}
        ds = {c: (pr[c][0] * (dp[c][0] - delta[c]), pr[c][1] * (dp[c][1] - delta[c])) for c in chains}
        dsb = {c: ((ds[c][0] * scale).astype(BF16), (ds[c][1] * scale).astype(BF16)) for c in chains}
        dk_acc = [zero] * (nqb + 1)
        dv_acc = [zero] * (nqb + 1)
        db_acc = [[jnp.zeros((WINDOW, WINDOW), F32)] * 2 for _ in range(2)]
        dsk_acc = [jnp.zeros((1, 1), F32)] * 2
        dq = [zero] * nqb
        for m_, h in chains:
            c = (m_, h)
            db_acc[h] = [db_acc[h][0] + ds[c][0], db_acc[h][1] + ds[c][1]]
            dsk_acc[h] = dsk_acc[h] - jnp.sum(jnp.exp(s_ref[h:h + 1, 0:1] - lse_h[c]) * delta[c], axis=0, keepdims=True)
            dq[m_] = (dq[m_] + _dot(dsb[c][0], jnp.where(masks[h], kp[m_], jnp.zeros_like(kp[m_])))
                      + _dot(dsb[c][1], jnp.where(masks[h], kc[m_], jnp.zeros_like(kc[m_]))))
            dk_acc[m_] = dk_acc[m_] + _dot_tn(dsb[c][0], qh[c])
            dk_acc[m_ + 1] = dk_acc[m_ + 1] + _dot_tn(dsb[c][1], qh[c])
            dv_acc[m_] = dv_acc[m_] + _dot_tn(pr[c][0].astype(BF16), doh[c])
            dv_acc[m_ + 1] = dv_acc[m_ + 1] + _dot_tn(pr[c][1].astype(BF16), doh[c])
        for m_ in range(nqb):
            dq_ref[m_ * WINDOW:(m_ + 1) * WINDOW, :] = dq[m_]
        for h in range(2):
            db_ref[h, :, 0:WINDOW] += db_acc[h][0]
            db_ref[h, :, WINDOW:2 * WINDOW] += db_acc[h][1]
            dsk_ref[h:h + 1, :] += jnp.broadcast_to(dsk_acc[h], (1, LANES))
        offp = pl.multiple_of(jnp.maximum(n * nqb - 1, 0) * WINDOW, WINDOW)
        dk_ref[pl.ds(offp, WINDOW), :] += dk_acc[0]
        dv_ref[pl.ds(offp, WINDOW), :] += dv_acc[0]
        for m_ in range(nqb):
            off = pl.multiple_of(n * rows + m_ * WINDOW, WINDOW)
            dk_ref[pl.ds(off, WINDOW), :] += dk_acc[m_ + 1]
            dv_ref[pl.ds(off, WINDOW), :] += dv_acc[m_ + 1]

    blk = pl.BlockSpec((None, rows, LANES), lambda b, p, n: (b, n, p))
    seq = pl.BlockSpec((None, S, LANES), lambda b, p, n: (b, 0, 0))
    return pl.pallas_call(
        body, name=name, grid=(Bl, 3, S // rows), in_specs=_swa_specs(P_SWV // LANES, nqb) + [blk, blk, blk],
        out_specs=[blk, seq, seq, pl.BlockSpec((None, None, 2, WINDOW, 2 * WINDOW), lambda b, p, n: (b, p, 0, 0, 0)),
                   pl.BlockSpec((None, None, 2, LANES), lambda b, p, n: (b, p, 0, 0))],
        out_shape=[jax.ShapeDtypeStruct((Bl, S, 3 * LANES), F32), jax.ShapeDtypeStruct((Bl, S, LANES), F32), jax.ShapeDtypeStruct((Bl, S, LANES), F32),
                   jax.ShapeDtypeStruct((Bl, 3, 2, WINDOW, 2 * WINDOW), F32), jax.ShapeDtypeStruct((Bl, 3, 2, LANES), F32)],
        compiler_params=_cp("arbitrary", "arbitrary", "arbitrary"),
    )(qn3, kn3, kn3, proj3, proj3, bias, sinks, o3, lse3, do3)


CONV_ROWS = 64
CONV_LANES = 128


def _conv_strip(x_ref, h_ref, w, b, r0, cols, first_blk):
    x = x_ref[r0:r0 + CONV_ROWS, cols]
    if r0 == 0:
        rows = lax.broadcasted_iota(jnp.int32, x.shape, 0)
        h6 = jnp.where(first_blk, 0.0, h_ref[6:7, cols])
        h7 = jnp.where(first_blk, 0.0, h_ref[7:8, cols])
        x1 = jnp.where(rows == 0, h7, pltpu.roll(x, 1, 0))
        x2 = jnp.where(rows == 0, h6, jnp.where(rows == 1, h7, pltpu.roll(x, 2, 0)))
    else:
        x1 = x_ref[r0 - 1:r0 - 1 + CONV_ROWS, cols]
        x2 = x_ref[r0 - 2:r0 - 2 + CONV_ROWS, cols]
    return w[0:1] * x2 + w[1:2] * x1 + w[2:3] * x + b, x, x1, x2


FF_BLK = D_FF // 2


def _up_perm(a):
    q = FF_BLK
    return _cat([a[..., 0:q], a[..., 2 * q:3 * q], a[..., q:2 * q], a[..., 3 * q:4 * q]])


def conv_gate_fwd(up3, cw, cb, *, tm=256, name):
    Bl, S, _ = up3.shape
    tm = min(tm, S)
    W = 2 * FF_BLK

    def body(x_ref, h_ref, w_ref, b_ref, o_ref):
        first = pl.program_id(1) == 0

        def chunk(c, carry):
            cg = pl.ds(pl.multiple_of(c * CONV_LANES, CONV_LANES), CONV_LANES)
            cv = pl.ds(pl.multiple_of(FF_BLK + c * CONV_LANES, CONV_LANES), CONV_LANES)
            wg, wv, bg, bv = w_ref[:, cg], w_ref[:, cv], b_ref[:, cg], b_ref[:, cv]
            for r0 in range(0, tm, CONV_ROWS):
                ug = _conv_strip(x_ref, h_ref, wg, bg, r0, cg, first)[0]
                uv = _conv_strip(x_ref, h_ref, wv, bv, r0, cv, first)[0]
                o_ref[r0:r0 + CONV_ROWS, cg] = (ug * jax.nn.sigmoid(ug) * uv).astype(BF16)
            return carry

        lax.fori_loop(0, FF_BLK // CONV_LANES, chunk, 0)

    hb = tm // 8
    return pl.pallas_call(
        body, name=name, grid=(Bl, S // tm, 2),
        in_specs=[pl.BlockSpec((None, tm, W), lambda b, s, c: (b, s, c)),
                  pl.BlockSpec((None, 8, W), lambda b, s, c: (b, jnp.maximum(s * hb - 1, 0), c)),
                  pl.BlockSpec((3, W), lambda b, s, c: (0, c)), pl.BlockSpec((1, W), lambda b, s, c: (0, c))],
        out_specs=pl.BlockSpec((None, tm, FF_BLK), lambda b, s, c: (b, s, c)),
        out_shape=jax.ShapeDtypeStruct((Bl, S, D_FF), BF16),
        compiler_params=_cp("parallel", "parallel", "parallel"),
    )(up3, up3, cw, cb)


def conv_gate_bwd(up3, cw, cb, da3, *, tm=256, name):
    Bl, S, _ = up3.shape
    tm = min(tm, S)
    ns = S // tm
    W = 2 * FF_BLK

    def body(x_ref, h_ref, w_ref, b_ref, da_ref, dup_ref, dw_ref, nxt_ref, du_scr):
        b, s = pl.program_id(1), pl.program_id(2)
        seq_end = s == 0
        first = s == ns - 1

        @pl.when((b == 0) & seq_end)
        def _():
            dw_ref[...] = jnp.zeros_like(dw_ref)

        def du_chunk(c, carry):
            cg = pl.ds(pl.multiple_of(c * CONV_LANES, CONV_LANES), CONV_LANES)
            cv = pl.ds(pl.multiple_of(FF_BLK + c * CONV_LANES, CONV_LANES), CONV_LANES)
            wg, wv, bg, bv = w_ref[:, cg], w_ref[:, cv], b_ref[:, cg], b_ref[:, cv]
            acc_g = [jnp.zeros((1, CONV_LANES), F32)] * 4
            acc_v = [jnp.zeros((1, CONV_LANES), F32)] * 4
            for r0 in range(0, tm, CONV_ROWS):
                ug, xg, xg1, xg2 = _conv_strip(x_ref, h_ref, wg, bg, r0, cg, first)
                uv, xv, xv1, xv2 = _conv_strip(x_ref, h_ref, wv, bv, r0, cv, first)
                da = da_ref[r0:r0 + CONV_ROWS, cg].astype(F32)
                sg = jax.nn.sigmoid(ug)
                dug = da * uv * sg * (1.0 + ug * (1.0 - sg))
                duv = da * ug * sg
                du_scr[r0:r0 + CONV_ROWS, cg] = dug
                du_scr[r0:r0 + CONV_ROWS, cv] = duv
                col = lambda t: jnp.sum(t, axis=0, keepdims=True)
                acc_g = [acc_g[0] + col(dug * xg2), acc_g[1] + col(dug * xg1), acc_g[2] + col(dug * xg), acc_g[3] + col(dug)]
                acc_v = [acc_v[0] + col(duv * xv2), acc_v[1] + col(duv * xv1), acc_v[2] + col(duv * xv), acc_v[3] + col(duv)]
            for t in range(4):
                dw_ref[t:t + 1, cg] += acc_g[t]
                dw_ref[t:t + 1, cv] += acc_v[t]
            return carry

        lax.fori_loop(0, FF_BLK // CONV_LANES, du_chunk, 0)
        du_scr[tm:tm + 8, :] = jnp.where(seq_end, 0.0, nxt_ref[...])

        def dup_chunk(c, carry):
            cols = pl.ds(pl.multiple_of(c * CONV_LANES, CONV_LANES), CONV_LANES)
            w = w_ref[:, cols]
            for r0 in range(0, tm, CONV_ROWS):
                d0 = du_scr[r0:r0 + CONV_ROWS, cols]
                d1 = du_scr[r0 + 1:r0 + 1 + CONV_ROWS, cols]
                d2 = du_scr[r0 + 2:r0 + 2 + CONV_ROWS, cols]
                dup_ref[r0:r0 + CONV_ROWS, cols] = (w[2:3] * d0 + w[1:2] * d1 + w[0:1] * d2).astype(BF16)
            return carry

        lax.fori_loop(0, W // CONV_LANES, dup_chunk, 0)
        nxt_ref[...] = du_scr[0:8, :]

    hb = tm // 8
    rb = lambda s: ns - 1 - s
    return pl.pallas_call(
        body, name=name, grid=(2, Bl, ns),
        in_specs=[pl.BlockSpec((None, tm, W), lambda c, b, s: (b, rb(s), c)),
                  pl.BlockSpec((None, 8, W), lambda c, b, s: (b, jnp.maximum(rb(s) * hb - 1, 0), c)),
                  pl.BlockSpec((3, W), lambda c, b, s: (0, c)), pl.BlockSpec((1, W), lambda c, b, s: (0, c)),
                  pl.BlockSpec((None, tm, FF_BLK), lambda c, b, s: (b, rb(s), c))],
        out_specs=[pl.BlockSpec((None, tm, W), lambda c, b, s: (b, rb(s), c)), pl.BlockSpec((8, W), lambda c, b, s: (0, c))],
        out_shape=[jax.ShapeDtypeStruct((Bl, S, 2 * D_FF), BF16), jax.ShapeDtypeStruct((8, 2 * D_FF), F32)],
        scratch_shapes=[pltpu.VMEM((8, W), F32), pltpu.VMEM((tm + 8, W), F32)],
        compiler_params=_cp("arbitrary", "arbitrary", "arbitrary"),
    )(up3, up3, cw, cb, da3)


def gate_bwd(dx3, y3, gate, *, tm=512, name):
    Bl, S, D = dx3.shape
    tm = min(tm, S)

    def body(dx_ref, y_ref, g_ref, o_ref, dg_ref):
        @pl.when(pl.program_id(1) == 0)
        def _():
            dg_ref[...] = jnp.zeros_like(dg_ref)

        dx = dx_ref[...]
        dg_ref[...] += jnp.sum(dx * y_ref[...], axis=0, keepdims=True)
        o_ref[...] = (dx * g_ref[...]).astype(BF16)

    blk = pl.BlockSpec((None, tm, D), lambda b, s: (b, s, 0))
    vec = pl.BlockSpec((None, 1, D), lambda b, s: (b, 0, 0))
    return pl.pallas_call(
        body, name=name, grid=(Bl, S // tm), in_specs=[blk, blk, vec], out_specs=[blk, vec],
        out_shape=[jax.ShapeDtypeStruct((Bl, S, D), BF16), jax.ShapeDtypeStruct((Bl, 1, D), F32)],
        compiler_params=_cp("parallel", "arbitrary"),
    )(dx3, y3, gate)


def loss_grad(y3, t3, *, tm=512, name):
    Bl, S, D = y3.shape
    tm = min(tm, S)
    last = (Bl - 1, S // tm - 1)

    def body(y_ref, t_ref, dy_ref, l_ref, acc_ref):
        b, s = pl.program_id(0), pl.program_id(1)

        @pl.when((b == 0) & (s == 0))
        def _():
            acc_ref[...] = jnp.zeros_like(acc_ref)

        e = y_ref[...] - t_ref[...]
        dy_ref[...] = e * (1.0 / D)
        acc_ref[...] += jnp.sum(e * e, axis=0, keepdims=True)

        @pl.when((b == last[0]) & (s == last[1]))
        def _():
            l_ref[...] = jnp.broadcast_to(jnp.sum(acc_ref[...], axis=1, keepdims=True) * (0.5 / D), (1, LANES))

    blk = pl.BlockSpec((None, tm, D), lambda b, s: (b, s, 0))
    return pl.pallas_call(
        body, name=name, grid=(Bl, S // tm), in_specs=[blk, blk],
        out_specs=[blk, pl.BlockSpec((1, LANES), lambda b, s: (0, 0))],
        out_shape=[jax.ShapeDtypeStruct((Bl, S, D), F32), jax.ShapeDtypeStruct((1, LANES), F32)],
        scratch_shapes=[pltpu.VMEM((1, D), F32)], compiler_params=_cp("arbitrary", "arbitrary"),
    )(y3, t3)


def adamw(w, g, m, v, *, name):
    L, R, C = w.shape
    tr = _tile(R, 512, 8)

    def body(w_ref, g_ref, m_ref, v_ref, d_ref, m2_ref, v2_ref):
        d_ref[...], m2_ref[...], v2_ref[...] = _adam_update(w_ref[...], g_ref[...], m_ref[...], v_ref[...])

    blk = pl.BlockSpec((None, tr, C), lambda l, i: (l, i, 0))
    shp = jax.ShapeDtypeStruct((L, R, C), F32)
    return pl.pallas_call(
        body, name=name, grid=(L, R // tr), in_specs=[blk] * 4, out_specs=[blk] * 3, out_shape=[shp] * 3,
        compiler_params=_cp("parallel", "parallel"),
    )(w, g, m, v)


def sum_leading(x, *, out_dtype=F32, tr=256, name):
    n, R, C = x.shape
    tr = _tile(R, tr, 16)

    def body(x_ref, o_ref):
        acc = x_ref[0].astype(F32)
        for k in range(1, n):
            acc = acc + x_ref[k].astype(F32)
        o_ref[...] = acc.astype(out_dtype)

    return pl.pallas_call(
        body, name=name, grid=(R // tr,), in_specs=[pl.BlockSpec((n, tr, C), lambda i: (0, i, 0))],
        out_specs=pl.BlockSpec((tr, C), lambda i: (i, 0)), out_shape=jax.ShapeDtypeStruct((R, C), out_dtype),
        compiler_params=_cp("parallel"),
    )(x)


def _adam_update(w, g, m, v):
    c1 = 1.0 / (1.0 - ADAM_B1 ** ADAM_STEP)
    c2 = 1.0 / (1.0 - ADAM_B2 ** ADAM_STEP)
    m2 = ADAM_B1 * m + (1.0 - ADAM_B1) * g
    v2 = ADAM_B2 * v + (1.0 - ADAM_B2) * (g * g)
    return -ADAM_LR * ((m2 * c1) / (jnp.sqrt(v2 * c2) + ADAM_EPS) + ADAM_WD * w), m2, v2


def adamw_small(ws, gs, ms, vs, *, name):
    na = len(ws)

    def body(*refs):
        w_r, g_r, m_r, v_r = (refs[i * na:(i + 1) * na] for i in range(4))
        d_r, m2_r, v2_r = (refs[(4 + i) * na:(5 + i) * na] for i in range(3))
        for a in range(na):
            d_r[a][...], m2_r[a][...], v2_r[a][...] = _adam_update(w_r[a][...], g_r[a][...], m_r[a][...], v_r[a][...])

    vm = pl.BlockSpec(memory_space=pltpu.VMEM)
    shp = [jax.ShapeDtypeStruct(w.shape, F32) for w in ws]
    out = pl.pallas_call(body, name=name, in_specs=[vm] * (4 * na), out_specs=[vm] * (3 * na), out_shape=shp * 3)(*ws, *gs, *ms, *vs)
    return out[:na], out[na:2 * na], out[2 * na:]


def sum_small(xs, *, name):
    na = len(xs)

    def body(*refs):
        for x_ref, o_ref in zip(refs[:na], refs[na:]):
            acc = x_ref[0]
            for k in range(1, x_ref.shape[0]):
                acc = acc + x_ref[k]
            o_ref[...] = acc

    vm = pl.BlockSpec(memory_space=pltpu.VMEM)
    return pl.pallas_call(body, name=name, in_specs=[vm] * na, out_specs=[vm] * na,
                          out_shape=[jax.ShapeDtypeStruct(x.shape[1:], x.dtype) for x in xs])(*xs)


def pair_add_half(g4, recv, c_arr, *, tr=512, name):
    _, R, C = g4.shape
    H = R // 2
    tr = _tile(H, tr, 16)
    nb = H // tr

    def body(c_ref, g_ref, r_ref, o_ref):
        o_ref[...] = (g_ref[...].astype(F32) + r_ref[...].astype(F32)).astype(BF16)

    grid_spec = pltpu.PrefetchScalarGridSpec(
        num_scalar_prefetch=1, grid=(4, nb),
        in_specs=[pl.BlockSpec((None, tr, C), lambda k, i, c_ref: (k, c_ref[0] * nb + i, 0)),
                  pl.BlockSpec((None, tr, C), lambda k, i, c_ref: (k, i, 0))],
        out_specs=pl.BlockSpec((None, tr, C), lambda k, i, c_ref: (k, i, 0)),
    )
    return pl.pallas_call(
        body, name=name, grid_spec=grid_spec, out_shape=jax.ShapeDtypeStruct((4, H, C), BF16),
        compiler_params=_cp("parallel", "parallel"),
    )(c_arr, g4, recv)


def chip_sum_into(landed, pair, sel, *, tr=512, name):
    _, H, C = landed.shape
    tr = _tile(H, tr, 16)
    nb = H // tr

    def body(s_ref, l0, l1, l2, l3, p_ref, o_ref):
        own = p_ref[...].astype(F32)
        acc = None
        for k, l_ref in enumerate((l0, l1, l2, l3)):
            part = jnp.where(s_ref[0] == k, own, l_ref[...].astype(F32))
            acc = part if acc is None else acc + part
        o_ref[...] = acc

    def slot(k):
        return pl.BlockSpec((None, tr, C), lambda i, s: (jnp.where(s[0] == k, (k + 1) % 4, k), i, 0))

    grid_spec = pltpu.PrefetchScalarGridSpec(
        num_scalar_prefetch=1, grid=(nb,),
        in_specs=[slot(0), slot(1), slot(2), slot(3), pl.BlockSpec((None, tr, C), lambda i, s: (s[0], i, 0))],
        out_specs=pl.BlockSpec((tr, C), lambda i, s: (s[1] * nb + i, 0)),
    )
    return pl.pallas_call(
        body, name=name, grid_spec=grid_spec, out_shape=jax.ShapeDtypeStruct((2 * H, C), F32), compiler_params=_cp("parallel"),
    )(sel, landed, landed, landed, landed, pair)


def mods_matmul(c_all, w_ada, b_ada_cols, *, tn=512, name):
    L, D, E = w_ada.shape
    nb = c_all.shape[0]
    tn = _tile(E, tn)

    def body(c_ref, w_ref, b_ref, o_ref):
        c = c_ref[...]
        a = c * jax.nn.sigmoid(c)
        o_ref[...] = jnp.dot(a, w_ref[...], preferred_element_type=F32, precision=lax.Precision.HIGHEST) + b_ref[...]

    return pl.pallas_call(
        body, name=name, grid=(L, E // tn),
        in_specs=[pl.BlockSpec((nb, D), lambda l, j: (0, 0)), pl.BlockSpec((None, D, tn), lambda l, j: (l, 0, j)),
                  pl.BlockSpec((None, 1, tn), lambda l, j: (l, 0, j))],
        out_specs=pl.BlockSpec((None, nb, tn), lambda l, j: (l, 0, j)),
        out_shape=jax.ShapeDtypeStruct((L, nb, E), F32), compiler_params=_cp("parallel", "parallel"),
    )(c_all, w_ada, b_ada_cols)


def ada_grad(c_all, dmods, *, tn=512, name):
    L, nb, E = dmods.shape
    D = c_all.shape[1]
    tn = _tile(E, tn)

    def body(c_ref, d_ref, o_ref):
        c = c_ref[...]
        a = c * jax.nn.sigmoid(c)
        o_ref[...] = lax.dot_general(a, d_ref[...], (((0,), (0,)), ((), ())), preferred_element_type=F32, precision=lax.Precision.HIGHEST)

    return pl.pallas_call(
        body, name=name, grid=(L, E // tn),
        in_specs=[pl.BlockSpec((nb, D), lambda l, j: (0, 0)), pl.BlockSpec((None, nb, tn), lambda l, j: (l, 0, j))],
        out_specs=pl.BlockSpec((None, D, tn), lambda l, j: (l, 0, j)),
        out_shape=jax.ShapeDtypeStruct((L, D, E), F32), compiler_params=_cp("parallel", "parallel"),
    )(c_all, dmods)


HBM = pl.BlockSpec(memory_space=pltpu.HBM)


def _me():
    return lax.axis_index("x"), lax.axis_index("y"), lax.axis_index("c")


def _flip(v, bit):
    return 1 - v if bit else v


def allgather8(xs, *, name):
    na = len(xs)

    def body(*refs):
        x_refs, out_refs = refs[:na], refs[na:2 * na]
        send_sems, recv_sems = refs[2 * na], refs[2 * na + 1]
        x, y, c = _me()
        me = 4 * x + 2 * y + c
        for x_ref, out_ref in zip(x_refs, out_refs):
            out_ref[me] = x_ref[...]
        sends = []
        for a, (x_ref, out_ref) in enumerate(zip(x_refs, out_refs)):
            for k in range(1, 8):
                peer = (_flip(x, k & 4), _flip(y, k & 2), _flip(c, k & 1))
                cp = pltpu.make_async_remote_copy(src_ref=x_ref, dst_ref=out_ref.at[me], send_sem=send_sems.at[a, k - 1],
                                                  recv_sem=recv_sems.at[a, k - 1], device_id=peer, device_id_type=MESH)
                cp.start()
                sends.append(cp)
        for a, (x_ref, out_ref) in enumerate(zip(x_refs, out_refs)):
            for k in range(1, 8):
                peer = (_flip(x, k & 4), _flip(y, k & 2), _flip(c, k & 1))
                src = 4 * peer[0] + 2 * peer[1] + peer[2]
                pltpu.make_async_remote_copy(src_ref=x_ref, dst_ref=out_ref.at[src], send_sem=send_sems.at[a, k - 1],
                                             recv_sem=recv_sems.at[a, k - 1], device_id=peer, device_id_type=MESH).wait_recv()
        for cp in sends:
            cp.wait_send()

    vm = pl.BlockSpec(memory_space=pltpu.VMEM)
    return pl.pallas_call(
        body, name=name, in_specs=[vm] * na, out_specs=[vm] * na,
        out_shape=[jax.ShapeDtypeStruct((8,) + a.shape, a.dtype) for a in xs],
        scratch_shapes=[pltpu.SemaphoreType.DMA((na, 7)), pltpu.SemaphoreType.DMA((na, 7))],
    )(*xs)


LOCAL_CHUNKS = 8


def _copy_via_vmem(src, dst_at, rows, buf, sem):
    ch = buf.shape[0]
    for i in range(rows // ch):
        load = pltpu.make_async_copy(src.at[pl.ds(i * ch, ch)], buf, sem)
        load.start()
        load.wait()
        store = pltpu.make_async_copy(buf, dst_at(i * ch, ch), sem)
        store.start()
        store.wait()


def _chunk_buf(rows, cols, dtype):
    align = 16 if dtype == BF16 else 8
    for n in range(LOCAL_CHUNKS, 0, -1):
        if rows % n == 0 and (rows // n) % align == 0:
            return pltpu.VMEM((rows // n, cols), dtype)
    return pltpu.VMEM((rows, cols), dtype)


def gather_weights(ws, *, name):
    na = len(ws)

    def body(*refs):
        x_refs, out_refs = refs[:na], refs[na:2 * na]
        send_sems, recv_sems, local_sem = refs[2 * na:2 * na + 3]
        bufs = refs[2 * na + 3:]
        x, y, c = _me()
        j = 2 * x + y
        chips = [(_flip(x, k & 2), _flip(y, k & 1)) for k in range(1, 4)]
        sends = []
        for a, (x_ref, out_ref) in enumerate(zip(x_refs, out_refs)):
            H = x_ref.shape[0] // 2
            for k, (px, py) in enumerate(chips):
                cp = pltpu.make_async_remote_copy(src_ref=x_ref.at[pl.ds(c * H, H)], dst_ref=out_ref.at[j, pl.ds(c * H, H)],
                                                  send_sem=send_sems.at[a, k], recv_sem=recv_sems.at[a, k],
                                                  device_id=(px, py, c), device_id_type=MESH)
                cp.start()
                sends.append(cp)
        for x_ref, out_ref, buf in zip(x_refs, out_refs, bufs):
            _copy_via_vmem(x_ref, lambda o, n, out_ref=out_ref: out_ref.at[j, pl.ds(o, n)], x_ref.shape[0], buf, local_sem)
        for a, out_ref in enumerate(out_refs):
            H = out_ref.shape[1] // 2
            for k, (px, py) in enumerate(chips):
                slot = out_ref.at[2 * px + py, pl.ds(c * H, H)]
                pltpu.make_async_remote_copy(src_ref=slot, dst_ref=slot, send_sem=send_sems.at[a, k], recv_sem=recv_sems.at[a, k],
                                             device_id=(px, py, c), device_id_type=MESH).wait_recv()
                cp = pltpu.make_async_remote_copy(src_ref=slot, dst_ref=slot, send_sem=send_sems.at[a, 3 + k],
                                                  recv_sem=recv_sems.at[a, 3 + k], device_id=(x, y, 1 - c), device_id_type=MESH)
                cp.start()
                sends.append(cp)
        for a, out_ref in enumerate(out_refs):
            H = out_ref.shape[1] // 2
            for k, (px, py) in enumerate(chips):
                slot = out_ref.at[2 * px + py, pl.ds((1 - c) * H, H)]
                pltpu.make_async_remote_copy(src_ref=slot, dst_ref=slot, send_sem=send_sems.at[a, 3 + k], recv_sem=recv_sems.at[a, 3 + k],
                                             device_id=(x, y, 1 - c), device_id_type=MESH).wait_recv()
        for cp in sends:
            cp.wait_send()

    return pl.pallas_call(
        body, name=name, in_specs=[HBM] * na, out_specs=[HBM] * na,
        out_shape=[jax.ShapeDtypeStruct((4,) + w.shape, w.dtype) for w in ws],
        scratch_shapes=[pltpu.SemaphoreType.DMA((na, 6)), pltpu.SemaphoreType.DMA((na, 6)), pltpu.SemaphoreType.DMA]
        + [_chunk_buf(w.shape[0], w.shape[1], w.dtype) for w in ws],
    )(*ws)


def swap_halves(gs, *, name):
    na = len(gs)

    def body(*refs):
        g_refs, out_refs = refs[:na], refs[na:2 * na]
        send_sems, recv_sems = refs[2 * na:]
        x, y, c = _me()
        sib = (x, y, 1 - c)
        sends = []
        for a, (g_ref, out_ref) in enumerate(zip(g_refs, out_refs)):
            H = g_ref.shape[1] // 2
            for k in range(4):
                cp = pltpu.make_async_remote_copy(src_ref=g_ref.at[k, pl.ds((1 - c) * H, H)], dst_ref=out_ref.at[k],
                                                  send_sem=send_sems.at[a, k], recv_sem=recv_sems.at[a, k], device_id=sib, device_id_type=MESH)
                cp.start()
                sends.append(cp)
        for a, (g_ref, out_ref) in enumerate(zip(g_refs, out_refs)):
            H = g_ref.shape[1] // 2
            for k in range(4):
                pltpu.make_async_remote_copy(src_ref=g_ref.at[k, pl.ds(c * H, H)], dst_ref=out_ref.at[k], send_sem=send_sems.at[a, k],
                                             recv_sem=recv_sems.at[a, k], device_id=sib, device_id_type=MESH).wait_recv()
        for cp in sends:
            cp.wait_send()

    return pl.pallas_call(
        body, name=name, in_specs=[HBM] * na, out_specs=[HBM] * na,
        out_shape=[jax.ShapeDtypeStruct((4, g.shape[1] // 2, g.shape[2]), g.dtype) for g in gs],
        scratch_shapes=[pltpu.SemaphoreType.DMA((na, 4)), pltpu.SemaphoreType.DMA((na, 4))],
    )(*gs)


def scatter_chips(ps, *, name):
    na = len(ps)

    def body(*refs):
        p_refs, out_refs = refs[:na], refs[na:2 * na]
        send_sems, recv_sems, local_sem = refs[2 * na:2 * na + 3]
        bufs = refs[2 * na + 3:]
        x, y, c = _me()
        j = 2 * x + y
        chips = [(_flip(x, k & 2), _flip(y, k & 1)) for k in range(1, 4)]
        sends = []
        for a, (p_ref, out_ref) in enumerate(zip(p_refs, out_refs)):
            for k, (px, py) in enumerate(chips):
                cp = pltpu.make_async_remote_copy(src_ref=p_ref.at[2 * px + py], dst_ref=out_ref.at[j], send_sem=send_sems.at[a, k],
                                                  recv_sem=recv_sems.at[a, k], device_id=(px, py, c), device_id_type=MESH)
                cp.start()
                sends.append(cp)
        for p_ref, out_ref, buf in zip(p_refs, out_refs, bufs):
            _copy_via_vmem(p_ref.at[j], lambda o, n, out_ref=out_ref: out_ref.at[j, pl.ds(o, n)], p_ref.shape[1], buf, local_sem)
        for a, out_ref in enumerate(out_refs):
            for k, (px, py) in enumerate(chips):
                slot = out_ref.at[2 * px + py]
                pltpu.make_async_remote_copy(src_ref=slot, dst_ref=slot, send_sem=send_sems.at[a, k], recv_sem=recv_sems.at[a, k],
                                             device_id=(px, py, c), device_id_type=MESH).wait_recv()
        for cp in sends:
            cp.wait_send()

    return pl.pallas_call(
        body, name=name, in_specs=[HBM] * na, out_specs=[HBM] * na, out_shape=[jax.ShapeDtypeStruct(p.shape, p.dtype) for p in ps],
        scratch_shapes=[pltpu.SemaphoreType.DMA((na, 3)), pltpu.SemaphoreType.DMA((na, 3)), pltpu.SemaphoreType.DMA]
        + [_chunk_buf(p.shape[1], p.shape[2], p.dtype) for p in ps],
    )(*ps)


def join_halves(halves, *, name):
    na = len(halves)

    def body(*refs):
        h_refs, out_refs = refs[:na], refs[na:2 * na]
        send_sems, recv_sems, local_sem = refs[2 * na:2 * na + 3]
        bufs = refs[2 * na + 3:]
        x, y, c = _me()
        sib = (x, y, 1 - c)
        sends = []
        for a, (h_ref, out_ref) in enumerate(zip(h_refs, out_refs)):
            H = h_ref.shape[0]
            cp = pltpu.make_async_remote_copy(src_ref=h_ref, dst_ref=out_ref.at[pl.ds(c * H, H)], send_sem=send_sems.at[a],
                                              recv_sem=recv_sems.at[a], device_id=sib, device_id_type=MESH)
            cp.start()
            sends.append(cp)
        for h_ref, out_ref, buf in zip(h_refs, out_refs, bufs):
            H = h_ref.shape[0]
            _copy_via_vmem(h_ref, lambda o, n, out_ref=out_ref, H=H: out_ref.at[pl.ds(c * H + o, n)], H, buf, local_sem)
        for a, (h_ref, out_ref) in enumerate(zip(h_refs, out_refs)):
            H = h_ref.shape[0]
            pltpu.make_async_remote_copy(src_ref=h_ref, dst_ref=out_ref.at[pl.ds((1 - c) * H, H)], send_sem=send_sems.at[a],
                                         recv_sem=recv_sems.at[a], device_id=sib, device_id_type=MESH).wait_recv()
        for cp in sends:
            cp.wait_send()

    return pl.pallas_call(
        body, name=name, in_specs=[HBM] * na, out_specs=[HBM] * na,
        out_shape=[jax.ShapeDtypeStruct((2 * h.shape[0], h.shape[1]), h.dtype) for h in halves],
        scratch_shapes=[pltpu.SemaphoreType.DMA((na,)), pltpu.SemaphoreType.DMA((na,)), pltpu.SemaphoreType.DMA]
        + [_chunk_buf(h.shape[0], h.shape[1], h.dtype) for h in halves],
    )(*halves)


class _Plan:
    def __init__(self, ins, out_shapes, ncopies, copies, aliased=False):
        self.ins, self.out_shapes, self.ncopies, self.copies, self.aliased = list(ins), list(out_shapes), ncopies, copies, aliased

    def start(self, in_refs, out_refs, send_sems, recv_sems):
        sends, _ = self.copies(in_refs, out_refs, send_sems, recv_sems)
        for cp in sends:
            cp.start()

    def finish(self, in_refs, out_refs, send_sems, recv_sems):
        sends, recvs = self.copies(in_refs, out_refs, send_sems, recv_sems)
        for cp in recvs:
            cp.wait_recv()
        for cp in sends:
            cp.wait_send()


def _rcopy(src, dst, send_sems, recv_sems, idx, dev):
    return pltpu.make_async_remote_copy(src_ref=src, dst_ref=dst, send_sem=send_sems.at[idx], recv_sem=recv_sems.at[idx],
                                        device_id=dev, device_id_type=MESH)


def _other_chips(x, y):
    return [(_flip(x, k & 2), _flip(y, k & 1)) for k in range(1, 4)]


def plan_gather_ici(ws):
    def copies(in_refs, out_refs, ss, rs):
        x, y, c = _me()
        j = 2 * x + y
        sends, recvs = [], []
        for a, (x_ref, out_ref) in enumerate(zip(in_refs, out_refs)):
            H = x_ref.shape[0] // 2
            for k, (px, py) in enumerate(_other_chips(x, y)):
                sends.append(_rcopy(x_ref.at[pl.ds(c * H, H)], out_ref.at[j, pl.ds(c * H, H)], ss, rs, 3 * a + k, (px, py, c)))
                slot = out_ref.at[2 * px + py, pl.ds(c * H, H)]
                recvs.append(_rcopy(slot, slot, ss, rs, 3 * a + k, (px, py, c)))
        return sends, recvs

    return _Plan(ws, [jax.ShapeDtypeStruct((4,) + w.shape, w.dtype) for w in ws], 3 * len(ws), copies)


def plan_gather_d2d(w4s):
    def copies(in_refs, out_refs, ss, rs):
        x, y, c = _me()
        sends, recvs = [], []
        for a, out_ref in enumerate(out_refs):
            H = out_ref.shape[1] // 2
            for k, (px, py) in enumerate(_other_chips(x, y)):
                mine = out_ref.at[2 * px + py, pl.ds(c * H, H)]
                theirs = out_ref.at[2 * px + py, pl.ds((1 - c) * H, H)]
                sends.append(_rcopy(mine, mine, ss, rs, 3 * a + k, (x, y, 1 - c)))
                recvs.append(_rcopy(theirs, theirs, ss, rs, 3 * a + k, (x, y, 1 - c)))
        return sends, recvs

    return _Plan(w4s, [jax.ShapeDtypeStruct(w.shape, w.dtype) for w in w4s], 3 * len(w4s), copies, aliased=True)


def plan_swap_halves(gs):
    def copies(in_refs, out_refs, ss, rs):
        x, y, c = _me()
        sends, recvs = [], []
        for a, (g_ref, out_ref) in enumerate(zip(in_refs, out_refs)):
            H = g_ref.shape[1] // 2
            for k in range(4):
                sends.append(_rcopy(g_ref.at[k, pl.ds((1 - c) * H, H)], out_ref.at[k], ss, rs, 4 * a + k, (x, y, 1 - c)))
                recvs.append(_rcopy(g_ref.at[k, pl.ds(c * H, H)], out_ref.at[k], ss, rs, 4 * a + k, (x, y, 1 - c)))
        return sends, recvs

    return _Plan(gs, [jax.ShapeDtypeStruct((4, g.shape[1] // 2, g.shape[2]), g.dtype) for g in gs], 4 * len(gs), copies)


def plan_scatter_ici(ps):
    def copies(in_refs, out_refs, ss, rs):
        x, y, c = _me()
        j = 2 * x + y
        sends, recvs = [], []
        for a, (p_ref, out_ref) in enumerate(zip(in_refs, out_refs)):
            for k, (px, py) in enumerate(_other_chips(x, y)):
                sends.append(_rcopy(p_ref.at[2 * px + py], out_ref.at[j], ss, rs, 3 * a + k, (px, py, c)))
                slot = out_ref.at[2 * px + py]
                recvs.append(_rcopy(slot, slot, ss, rs, 3 * a + k, (px, py, c)))
        return sends, recvs

    return _Plan(ps, [jax.ShapeDtypeStruct(p.shape, p.dtype) for p in ps], 3 * len(ps), copies)


def plan_join_halves(fulls):
    def copies(in_refs, out_refs, ss, rs):
        x, y, c = _me()
        sends, recvs = [], []
        for a, out_ref in enumerate(out_refs):
            H = out_ref.shape[0] // 2
            mine, theirs = out_ref.at[pl.ds(c * H, H)], out_ref.at[pl.ds((1 - c) * H, H)]
            sends.append(_rcopy(mine, mine, ss, rs, a, (x, y, 1 - c)))
            recvs.append(_rcopy(theirs, theirs, ss, rs, a, (x, y, 1 - c)))
        return sends, recvs

    return _Plan(fulls, [jax.ShapeDtypeStruct(f.shape, f.dtype) for f in fulls], len(fulls), copies, aliased=True)


def call_with_plans(body, plans, *, grid, in_specs, out_specs, out_shape, scratch_shapes, args, sem, name):
    plans = list(plans or [])
    n_in, n_out, n_scr = len(in_specs), len(out_specs), len(scratch_shapes)
    c_in = [len(p.ins) for p in plans]
    c_out = [len(p.out_shapes) for p in plans]
    steps = math.prod(grid) if grid else 1

    def wrapped(*refs):
        pos = 0

        def take(n):
            nonlocal pos
            out = refs[pos:pos + n]
            pos += n
            return out

        ins = take(n_in)
        cins = [take(n) for n in c_in]
        outs = take(n_out)
        couts = [take(n) for n in c_out]
        scr = take(n_scr)
        sems = [take(2) for _ in plans]
        def start_all():
            for p, ci, co, (ss, rs) in zip(plans, cins, couts, sems):
                p.start(ci, co, ss, rs)

        def finish_all():
            for p, ci, co, (ss, rs) in zip(plans, cins, couts, sems):
                p.finish(ci, co, ss, rs)

        if plans and grid:
            idx = 0
            for ax, g in enumerate(grid):
                idx = idx * g + pl.program_id(ax)
            pl.when(idx == 0)(start_all)
        elif plans:
            start_all()
        if body is not None:
            body(*ins, *outs, *scr)
        if plans and grid:
            pl.when(idx == steps - 1)(finish_all)
        elif plans:
            finish_all()

    aliases = {}
    i_pos, o_pos = n_in, n_out
    for p, ni, no in zip(plans, c_in, c_out):
        if p.aliased:
            aliases.update({i_pos + t: o_pos + t for t in range(ni)})
        i_pos += ni
        o_pos += no
    kwargs = dict(grid=grid) if grid else {}
    if aliases:
        kwargs["input_output_aliases"] = aliases
    res = pl.pallas_call(
        wrapped, name=name, in_specs=list(in_specs) + [HBM] * sum(c_in), out_specs=list(out_specs) + [HBM] * sum(c_out),
        out_shape=list(out_shape) + [s for p in plans for s in p.out_shapes],
        scratch_shapes=list(scratch_shapes) + [pltpu.SemaphoreType.DMA((p.ncopies,)) for p in plans for _ in range(2)],
        compiler_params=_cp(*sem) if grid else pltpu.CompilerParams(vmem_limit_bytes=VMEM_LIMIT), **kwargs,
    )(*args, *[a for p in plans for a in p.ins])
    res = list(res)
    comp, rest = res[:n_out], res[n_out:]
    pouts = []
    for no in c_out:
        pouts.append(rest[:no])
        rest = rest[no:]
    return comp, pouts


def run_plans(plans, *, name):
    return call_with_plans(None, plans, grid=(), in_specs=[], out_specs=[], out_shape=[], scratch_shapes=[], args=[], sem=(), name=name)[1]


def _cat(parts, axis=-1):
    return jnp.concatenate(parts, axis=axis)


def _pairs_of_heads(a, axis, inverse=False):
    lead, tail = a.shape[:axis], a.shape[axis + 1:]
    split = (3, 2) if inverse else (2, 3)
    a = a.reshape(lead + split + (HEAD,) + tail)
    return jnp.swapaxes(a, axis, axis + 1).reshape(lead + (6 * HEAD,) + tail)


def _prep_w_in(w):
    z = lambda n: jnp.zeros((w.shape[0], n), w.dtype)
    return _cat([w[:, 0:1152], z(64), w[:, 1152:1184], z(32), _pairs_of_heads(w[:, 1184:1568], 1), w[:, 1568:1824]])


def _unprep_w_in(g):
    return _cat([g[:, 0:1152], g[:, 1216:1248], _pairs_of_heads(g[:, P_SWQ:P_SWK], 1, inverse=True), g[:, P_SWK:P_END]])


def _prep_w_uq(w):
    r = w.shape[0]
    return jnp.pad(w.reshape(r, 6, MLA_QK), ((0, 0), (0, 0), (0, LANES - MLA_QK))).reshape(r, 6 * LANES)


def _unprep_w_uq(g):
    r = g.shape[0]
    return g.reshape(r, 6, LANES)[:, :, :MLA_QK].reshape(r, 6 * MLA_QK)


def _prep_w_ukv(w):
    r = w.shape[0]
    w3 = w.reshape(r, 6, LANES)
    k = jnp.pad(w3[:, :, :HEAD], ((0, 0), (0, 0), (0, LANES - HEAD))).reshape(r, 6 * LANES)
    return _cat([k, w3[:, :, HEAD:].reshape(r, 6 * HEAD)])


def _unprep_w_ukv(g):
    r = g.shape[0]
    k = g[:, :6 * LANES].reshape(r, 6, LANES)[:, :, :HEAD]
    return _cat([k, g[:, 6 * LANES:].reshape(r, 6, HEAD)], axis=2).reshape(r, 6 * LANES)


def _prep_w_out(w):
    return _cat([w[0:640], _pairs_of_heads(w[640:], 0)], axis=0)


def _unprep_w_out(g):
    return _cat([g[0:640], _pairs_of_heads(g[640:], 0, inverse=True)], axis=0)


def _rope_tables(positions):
    half = 16
    inv_freq = jnp.power(ROPE_THETA, -jnp.arange(half, dtype=F32) / half)
    ang = positions.astype(F32)[..., None] * inv_freq
    cos, sin = jnp.cos(ang), jnp.sin(ang)
    z = lambda n: jnp.zeros(ang.shape[:-1] + (n,), F32)
    return (_cat([jnp.ones(ang.shape[:-1] + (HEAD,), F32), cos, cos, z(32)]), _cat([z(HEAD), -sin, z(16), z(32)]), _cat([z(HEAD), z(16), sin, z(32)]))


def _small_params(p):
    pad96 = lambda g: _cat([g, jnp.zeros((32,), F32)]).reshape(1, LANES)
    two = lambda g: _cat([g, g]).reshape(1, LANES)
    sinks = jnp.broadcast_to(p["sw_sinks"].reshape(2, 3).T[:, :, None], (3, 2, LANES))
    return dict(n1=p["norm1_g"].reshape(1, -1), n2=p["norm2_g"].reshape(1, -1), cq_g=p["mla_cq_g"].reshape(1, -1),
                ckv_g=p["mla_ckv_g"].reshape(1, -1), qn_g=pad96(p["mla_qn_g"]), kn_g=pad96(p["mla_kn_g"]),
                swq_g=two(p["sw_qn_g"]), swk_g=two(p["sw_kn_g"]), sinks=sinks, conv_b=_up_perm(p["conv_b"]).reshape(1, -1))


class _NoFlow:
    def plans(self, tag):
        return []

    def done(self, tag, outs):
        pass

    def add(self, key, g):
        pass


def _layer_fwd(x3, md, W, tabs, bias, tag, flow=_NoFlow()):
    Bl, S, D = x3.shape
    T = Bl * S
    n = lambda s: f"{s}_{tag}"
    two = lambda a: a.reshape(T, a.shape[-1])
    three = lambda a: a.reshape(Bl, S, a.shape[-1])
    h = rms_fwd(x3, 0, D, W["n1"], md["scale1"], md["shift1"], name=n("norm1"))
    proj = three(matmul(two(h), W["w_in"], tn=1920, name=n("in_proj")))
    (o_a, rt_a), got = sb_attn_fwd(proj, plans=flow.plans(n("sb_fwd")), name=n("sb_fwd"))
    flow.done(n("sb_fwd"), got)
    cqn = rms_fwd(proj, P_CQ // 256, 256, W["cq_g"], name=n("cq_norm"))
    ckvn = rms_fwd(proj, P_CKV // LANES, LANES, W["ckv_g"], name=n("ckv_norm"))
    qb = three(matmul(two(cqn), W["w_uq"], tm=1024, tn=768, name=n("uq")))
    kvb = three(matmul(two(ckvn), W["w_ukv"], tm=1024, tn=1152, name=n("ukv")))
    q_m = rope_norm_fwd(qb, 6, W["qn_g"], tabs, name=n("q_rope"))
    k_m = rope_norm_fwd(kvb, 6, W["kn_g"], tabs, (proj, P_SLAB // LANES), name=n("k_rope"))
    (o_b, lse_b), got = mla_attn_fwd(q_m, k_m, kvb, 6, plans=flow.plans(n("mla_fwd")), name=n("mla_fwd"))
    flow.done(n("mla_fwd"), got)
    q_c = pair_rms_fwd(proj, P_SWQ // LANES, 3, W["swq_g"], name=n("swq_norm"))
    k_c = pair_rms_fwd(proj, P_SWK // LANES, 1, W["swk_g"], name=n("swk_norm"))
    (o_c, lse_c), got = swa_attn_fwd(q_c, k_c, proj, bias, W["sinks"], plans=flow.plans(n("swa_fwd")), name=n("swa_fwd"))
    flow.done(n("swa_fwd"), got)
    mix = _cat([o_a, o_b, o_c]).astype(BF16)
    att, x1 = matmul_res(two(mix), W["w_out"], two(x3), md["gate1"], S, name=n("out_proj"))
    x1 = three(x1)
    h2 = rms_fwd(x1, 0, D, W["n2"], md["scale2"], md["shift2"], name=n("norm2"))
    up = three(matmul(two(h2), W["w_up"], tm=1024, tn=1408, name=n("up_proj")))
    a = conv_gate_fwd(up, W["conv_w"], W["conv_b"], name=n("conv_gate"))
    yd, x2 = matmul_res(two(a), W["w_down"], two(x1), md["gate2"], S, name=n("down_proj"))
    saved = dict(x=x3, h=h, proj=proj, rt_a=rt_a, cqn=cqn, ckvn=ckvn, qb=qb, kvb=kvb, q_m=q_m, k_m=k_m, o_b=o_b, lse_b=lse_b,
                 q_c=q_c, k_c=k_c, o_c=o_c, lse_c=lse_c, mix=mix, att=three(att), x1=x1, h2=h2, up=up, a=a, yd=three(yd))
    return three(x2), saved


def _layer_bwd(dx2, sv, md, W, tabs, bias, tag, flow=_NoFlow()):
    Bl, S, D = dx2.shape
    T = Bl * S
    n = lambda s: f"{s}_{tag}"
    two = lambda a: a.reshape(T, a.shape[-1])
    three = lambda a: a.reshape(Bl, S, a.shape[-1])
    g = {}
    dyb, dgate2 = gate_bwd(dx2, sv["yd"], md["gate2"], name=n("gate2_bwd"))
    da = three(matmul(two(dyb), W["w_down"], tb=True, tm=1024, tn=1408, name=n("down_dx")))
    g["w_down"] = matmul(two(sv["a"]), two(dyb), ta=True, tm=256, tn=1024, name=n("down_dw"))
    dup, dcw = conv_gate_bwd(sv["up"], W["conv_w"], W["conv_b"], da, name=n("conv_gate_bwd"))
    dh2 = three(matmul(two(dup), W["w_up"], tb=True, tn=1024, name=n("up_dx")))
    g["w_up"] = matmul(two(sv["h2"]), two(dup), ta=True, tn=1408, name=n("up_dw"))
    dx1, dn2, dsc2, dsh2 = rms_bwd(sv["x1"], 0, D, dh2, W["n2"], md["scale2"], dx2, name=n("norm2_bwd"))
    dmo, dgate1 = gate_bwd(dx1, sv["att"], md["gate1"], name=n("gate1_bwd"))
    dmix = three(matmul(two(dmo), W["w_out"], tb=True, tn=1024, out_dtype=BF16, name=n("out_dx")))
    g["w_out"] = matmul(two(sv["mix"]), two(dmo), ta=True, tn=1024, name=n("out_dw"))
    proj = sv["proj"]
    for k in ("w_down", "w_up", "w_out"):
        flow.add((tag, k), g[k])
    (dq_a, dk_a, dv_a), got = sb_attn_bwd(proj, sv["rt_a"], dmix[:, :, 0:256], plans=flow.plans(n("sb_bwd")), name=n("sb_bwd"))
    flow.done(n("sb_bwd"), got)
    dq_m, dk_m, dv_b = mla_attn_bwd(sv["q_m"], sv["k_m"], sv["kvb"], 6, sv["o_b"], sv["lse_b"], dmix[:, :, 256:640], name=n("mla_bwd"))
    dqb, dqn = rope_norm_bwd(sv["qb"], 6, dq_m, W["qn_g"], tabs, name=n("q_rope_bwd"))
    dkn_x, dkn, dslab = rope_norm_bwd(sv["kvb"], 6, dk_m, W["kn_g"], tabs, (proj, P_SLAB // LANES), name=n("k_rope_bwd"))
    dkvb = _cat([dkn_x, dv_b]).astype(BF16)
    dckvn = three(matmul(two(dkvb), W["w_ukv"], tb=True, tm=1024, name=n("ukv_dx")))
    g["w_ukv"] = matmul(two(sv["ckvn"]), two(dkvb), ta=True, tn=1152, name=n("ukv_dw"))
    dcqn = three(matmul(two(dqb), W["w_uq"], tb=True, tm=1024, name=n("uq_dx")))
    g["w_uq"] = matmul(two(sv["cqn"]), two(dqb), ta=True, tn=768, name=n("uq_dw"))
    dcq, dcq_g = rms_bwd(proj, P_CQ // 256, 256, dcqn, W["cq_g"], name=n("cq_norm_bwd"))
    dckv, dckv_g = rms_bwd(proj, P_CKV // LANES, LANES, dckvn, W["ckv_g"], name=n("ckv_norm_bwd"))
    dq_c, dk_c, dv_c, dbias, dsink = swa_attn_bwd(sv["q_c"], sv["k_c"], proj, bias, W["sinks"], sv["o_c"], sv["lse_c"], dmix[:, :, 640:1024], name=n("swa_bwd"))
    dswq, dswq_g = pair_rms_bwd(proj, P_SWQ // LANES, 3, dq_c, W["swq_g"], name=n("swq_norm_bwd"))
    dswk, dswk_g = pair_rms_bwd(proj, P_SWK // LANES, 1, dk_c, W["swk_g"], name=n("swk_norm_bwd"))
    dproj = _cat([dq_a, dk_a, dv_a, dcq, dckv, dslab, dswq, dswk, dv_c]).astype(BF16)
    dh = three(matmul(two(dproj), W["w_in"], tb=True, tn=1024, name=n("in_dx")))
    g["w_in"] = matmul(two(sv["h"]), two(dproj), ta=True, tn=1920, tk=2048, name=n("in_dw"))
    dx, dn1, dsc1, dsh1 = rms_bwd(sv["x"], 0, D, dh, W["n1"], md["scale1"], dx1, name=n("norm1_bwd"))
    small = dict(n1=dn1, n2=dn2, cq_g=dcq_g, ckv_g=dckv_g, qn_g=dqn, kn_g=dkn, swq_g=dswq_g, swk_g=dswk_g, conv=dcw)
    dmods = _cat([dsh1, dsc1, dgate1, dsh2, dsc2, dgate2]).reshape(Bl, 6 * D)
    for k in ("w_ukv", "w_uq", "w_in"):
        flow.add((tag, k), g[k])
    return dx, g, small, dmods, dbias, dsink


BIG = ("w_in", "w_uq", "w_ukv", "w_out", "w_up", "w_down")
ROW_SHARDED = ("w_out", "w_down")
PREP = dict(w_in=_prep_w_in, w_uq=_prep_w_uq, w_ukv=_prep_w_ukv, w_out=_prep_w_out, w_up=_up_perm, w_down=lambda w: w)
UNPREP = dict(w_in=_unprep_w_in, w_uq=_unprep_w_uq, w_ukv=_unprep_w_ukv, w_out=_unprep_w_out, w_up=_up_perm, w_down=lambda w: w)
NCHIPS = 4


def _local_step(x, target, positions, mods, Wl, rel_flat, fwd_flow=_NoFlow(), bwd_flow=_NoFlow()):
    Bl, S, D = x.shape
    L = len(Wl)
    tabs = _rope_tables(positions)
    bucket = _bucket_table()
    bias = swa_bias(rel_flat, bucket, name="swa_bias")
    mds = []
    for l in range(L):
        parts = [mods[l, :, D * k:D * (k + 1)].reshape(Bl, 1, D) for k in range(6)]
        mds.append(dict(zip(("shift1", "scale1", "gate1", "shift2", "scale2", "gate2"), parts)))
    saved = []
    h = x
    for l in range(L):
        h, sv = _layer_fwd(h, mds[l], Wl[l], tabs, bias, f"l{l}", fwd_flow)
        saved.append(sv)
    dy, loss = loss_grad(h, target, name="loss")
    grads, smalls, dmods, dbiases, dsinks = [None] * L, [None] * L, [None] * L, [None] * L, [None] * L
    for l in reversed(range(L)):
        dy, grads[l], smalls[l], dmods[l], dbiases[l], dsinks[l] = _layer_bwd(dy, saved[l], mds[l], Wl[l], tabs, bias, f"l{l}", bwd_flow)
    drel = swa_bias_bwd(_cat(dbiases, axis=0), bucket, name="swa_bias_bwd")
    return loss, dy, grads, smalls, dmods, dsinks, drel


ATT = ("w_in", "w_uq", "w_ukv", "w_out")
FFN = ("w_up", "w_down")
GATHER_STAGES = {
    "sb_fwd_l0": ([("l0", k) for k in FFN], []),
    "mla_fwd_l0": ([("l1", k) for k in ATT + ("w_up",)], [("l0", k) for k in FFN]),
    "swa_fwd_l0": ([("l1", "w_down")], [("l1", k) for k in ATT + ("w_up",)]),
    "sb_fwd_l1": ([], [("l1", "w_down")]),
}
SCATTER_STAGES = {
    "sb_bwd_l1": [("l1", k) for k in FFN],
    "sb_bwd_l0": [("l1", k) for k in ATT] + [("l0", k) for k in FFN],
}


class _GatherFlow:
    def __init__(self, shards, chip):
        self.shards, self.chip, self.ici, self.d2d, self.pending = shards, chip, {}, {}, {}

    def early(self, keys):
        ici, = run_plans([plan_gather_ici([self.shards[k] for k in keys])], name="gather_early_ici")
        d2d, = run_plans([plan_gather_d2d(ici)], name="gather_early_d2d")
        self.d2d.update(zip(keys, d2d))

    def plans(self, tag):
        ici_keys, d2d_keys = GATHER_STAGES.get(tag, ([], []))
        plans = []
        if d2d_keys:
            plans.append(plan_gather_d2d([self.ici[k] for k in d2d_keys]))
        if ici_keys:
            plans.append(plan_gather_ici([self.shards[k] for k in ici_keys]))
        self.pending[tag] = (ici_keys, d2d_keys)
        return plans

    def done(self, tag, outs):
        ici_keys, d2d_keys = self.pending.pop(tag, ([], []))
        outs = list(outs)
        if d2d_keys:
            self.d2d.update(zip(d2d_keys, outs.pop(0)))
        if ici_keys:
            self.ici.update(zip(ici_keys, outs.pop(0)))

    def weight(self, key):
        k = key[1]
        own = self.shards[key]
        r, cc = own.shape
        w4 = lax.dynamic_update_slice(self.d2d[key], own[None], (self.chip, 0, 0))
        fw = w4.reshape(NCHIPS * r, cc) if k in ROW_SHARDED else jnp.transpose(w4, (1, 0, 2)).reshape(r, NCHIPS * cc)
        return PREP[k](fw)


class _LayerWeights(dict):
    def __init__(self, small, flow, tag):
        super().__init__(small)
        self.flow, self.tag = flow, tag

    def __missing__(self, k):
        self[k] = self.flow.weight((self.tag, k))
        return self[k]


class _ScatterFlow:
    def __init__(self, shapes, sel, c_arr):
        self.shapes, self.sel, self.c_arr = shapes, sel, c_arr
        self.g, self.pairs, self.landed, self.pending = {}, {}, {}, {}

    def add(self, key, g):
        self.g[key] = g

    def _pairs(self, keys, label):
        g4s = []
        for key in keys:
            k = key[1]
            r, cc = self.shapes[k]
            gk = UNPREP[k](self.g[key])
            g4 = gk.reshape(NCHIPS, r, cc) if k in ROW_SHARDED else jnp.transpose(gk.reshape(r, NCHIPS, cc), (1, 0, 2))
            g4s.append(g4.astype(BF16))
        theirs, = run_plans([plan_swap_halves(g4s)], name=f"rs_swap_{label}")
        pairs = [pair_add_half(g4, th, self.c_arr, name=f"rs_pair_add_{key[1]}_{key[0]}") for key, g4, th in zip(keys, g4s, theirs)]
        self.pairs.update(zip(keys, pairs))
        return pairs

    def plans(self, tag):
        keys = SCATTER_STAGES.get(tag, [])
        self.pending[tag] = keys
        return [plan_scatter_ici(self._pairs(keys, tag))] if keys else []

    def done(self, tag, outs):
        keys = self.pending.pop(tag, [])
        if keys:
            self.landed.update(zip(keys, outs[0]))

    def finish(self):
        rest = [key for key in self.g if key not in self.pairs]
        if rest:
            landed, = run_plans([plan_scatter_ici(self._pairs(rest, "rest"))], name="rs_scatter_rest")
            self.landed.update(zip(rest, landed))
        keys = list(self.pairs)
        fulls = [chip_sum_into(self.landed[key], self.pairs[key], self.sel, name=f"rs_chip_sum_{key[1]}_{key[0]}") for key in keys]
        joined, = run_plans([plan_join_halves(fulls)], name="rs_join_halves")
        return dict(zip(keys, joined))


WEIGHTS = ("rel_table", "norm1_g", "norm2_g", "w_ada", "b_ada", "w_in", "mla_cq_g", "w_uq", "mla_ckv_g", "w_ukv", "mla_qn_g", "mla_kn_g",
           "sw_qn_g", "sw_kn_g", "sw_sinks", "w_out", "w_up", "conv_w", "conv_b", "w_down")
SMALL = tuple(n for n in WEIGHTS if n not in BIG + ("w_ada",))


def kernel(x, c, positions, rel_table, norm1_g, norm2_g, w_ada, b_ada, w_in, mla_cq_g, w_uq, mla_ckv_g, w_ukv, mla_qn_g, mla_kn_g, sw_qn_g, sw_kn_g, sw_sinks, w_out, w_up, conv_w, conv_b, w_down, loss_target, m_rel_table, m_norm1_g, m_norm2_g, m_w_ada, m_b_ada, m_w_in, m_mla_cq_g, m_w_uq, m_mla_ckv_g, m_w_ukv, m_mla_qn_g, m_mla_kn_g, m_sw_qn_g, m_sw_kn_g, m_sw_sinks, m_w_out, m_w_up, m_conv_w, m_conv_b, m_w_down, v_rel_table, v_norm1_g, v_norm2_g, v_w_ada, v_b_ada, v_w_in, v_mla_cq_g, v_w_uq, v_mla_ckv_g, v_w_ukv, v_mla_qn_g, v_mla_kn_g, v_sw_qn_g, v_sw_kn_g, v_sw_sinks, v_w_out, v_w_up, v_conv_w, v_conv_b, v_w_down):
    w = dict(rel_table=rel_table, norm1_g=norm1_g, norm2_g=norm2_g, w_ada=w_ada, b_ada=b_ada, w_in=w_in, mla_cq_g=mla_cq_g, w_uq=w_uq,
             mla_ckv_g=mla_ckv_g, w_ukv=w_ukv, mla_qn_g=mla_qn_g, mla_kn_g=mla_kn_g, sw_qn_g=sw_qn_g, sw_kn_g=sw_kn_g, sw_sinks=sw_sinks,
             w_out=w_out, w_up=w_up, conv_w=conv_w, conv_b=conv_b, w_down=w_down)
    m = dict(rel_table=m_rel_table, norm1_g=m_norm1_g, norm2_g=m_norm2_g, w_ada=m_w_ada, b_ada=m_b_ada, w_in=m_w_in, mla_cq_g=m_mla_cq_g,
             w_uq=m_w_uq, mla_ckv_g=m_mla_ckv_g, w_ukv=m_w_ukv, mla_qn_g=m_mla_qn_g, mla_kn_g=m_mla_kn_g, sw_qn_g=m_sw_qn_g,
             sw_kn_g=m_sw_kn_g, sw_sinks=m_sw_sinks, w_out=m_w_out, w_up=m_w_up, conv_w=m_conv_w, conv_b=m_conv_b, w_down=m_w_down)
    v = dict(rel_table=v_rel_table, norm1_g=v_norm1_g, norm2_g=v_norm2_g, w_ada=v_w_ada, b_ada=v_b_ada, w_in=v_w_in, mla_cq_g=v_mla_cq_g,
             w_uq=v_w_uq, mla_ckv_g=v_mla_ckv_g, w_ukv=v_w_ukv, mla_qn_g=v_mla_qn_g, mla_kn_g=v_mla_kn_g, sw_qn_g=v_sw_qn_g,
             sw_kn_g=v_sw_kn_g, sw_sinks=v_sw_sinks, w_out=v_w_out, w_up=v_w_up, conv_w=v_conv_w, conv_b=v_conv_b, w_down=v_w_down)
    Bl, S, D = x.shape
    L = norm1_g.shape[0]
    xi, yi, ci = _me()
    chip = 2 * xi + yi
    dev = 4 * xi + 2 * yi + ci
    ndev = 2 * NCHIPS

    shapes = {k: w[k].shape[1:] for k in BIG}
    shards = {(f"l{l}", k): w[k][l].astype(BF16) for l in range(L) for k in BIG}
    gflow = _GatherFlow(shards, chip)
    gflow.early([("l0", k) for k in ATT])

    cw_cols = conv_w.shape[2]
    c_got, cw_got = allgather8([c, conv_w.reshape(L * 3, cw_cols)], name="gather_cond")
    c_all = c_got.reshape(ndev * Bl, D)
    conv_full = jnp.transpose(cw_got[0::2].reshape(NCHIPS, L, 3, cw_cols), (1, 2, 0, 3)).reshape(L, 3, NCHIPS * cw_cols)
    E = w_ada.shape[2]
    b_cols = lax.dynamic_slice(b_ada, (0, chip * E), (L, E)).reshape(L, 1, E)
    mods_cols = mods_matmul(c_all, w_ada, b_cols, name="mods")
    mods_all, = allgather8([mods_cols.reshape(L * ndev * Bl, E)], name="gather_mods")
    mods_all = jnp.transpose(mods_all[0::2].reshape(NCHIPS, L, ndev * Bl, E), (1, 2, 0, 3)).reshape(L, ndev * Bl, NCHIPS * E)
    mods = lax.dynamic_slice(mods_all, (0, dev * Bl, 0), (L, Bl, NCHIPS * E))

    Wl = []
    for l in range(L):
        Wd = _small_params({k: w[k][l] for k in SMALL if k not in ("rel_table", "b_ada", "conv_w")})
        Wd["conv_w"] = _up_perm(conv_full[l])
        Wl.append(_LayerWeights(Wd, gflow, f"l{l}"))

    sflow = _ScatterFlow(shapes, jnp.stack([chip, ci]).astype(jnp.int32), ci.reshape(1).astype(jnp.int32))
    loss, dx, _, smalls, dmods, dsinks, drel = _local_step(x, loss_target, positions, mods, Wl, rel_table.reshape(-1), gflow, sflow)
    reduced = sflow.finish()
    grad = {k: jnp.stack([reduced[(f"l{l}", k)] for l in range(L)]) for k in BIG}

    vec_names = ("n1", "n2", "cq_g", "ckv_g", "qn_g", "kn_g", "swq_g", "swk_g")
    vecs = _cat([_cat([smalls[l][k] for k in vec_names], axis=1) for l in range(L)], axis=0)
    convs = _cat([smalls[l]["conv"][0:4] for l in range(L)], axis=0)
    dm = jnp.stack(dmods, axis=1).reshape(Bl * L, 6 * D)
    dsk = jnp.stack(dsinks, axis=1).reshape(Bl * L * 6, LANES)
    got = allgather8([vecs, convs, drel, loss, dm, dsk], name="gather_small_grads")
    seq = lambda a, rows: a.reshape(ndev * Bl, rows, a.shape[-1])
    vec_s, conv_s, rel_s, loss_s, dm_s, dsk_s = sum_small(list(got[:4]) + [seq(got[4], L), seq(got[5], L * 6)], name="sum_small_grads")
    dm_all = jnp.transpose(seq(got[4], L), (1, 0, 2))
    grad["w_ada"] = ada_grad(c_all, lax.dynamic_slice(dm_all, (0, 0, chip * E), (L, ndev * Bl, E)), name="ada_grad")
    grad["b_ada"] = dm_s
    grad["sw_sinks"] = jnp.transpose(dsk_s.reshape(L, 3, 2, LANES)[:, :, :, 0], (0, 2, 1)).reshape(L, 6)
    grad["rel_table"] = rel_s[:6, :REL_BUCKETS].T
    off = 0
    for k, name_, keep in zip(vec_names, ("norm1_g", "norm2_g", "mla_cq_g", "mla_ckv_g", "mla_qn_g", "mla_kn_g", "sw_qn_g", "sw_kn_g"),
                              (D, D, 256, LANES, MLA_QK, MLA_QK, HEAD, HEAD)):
        grad[name_] = vec_s[:, off:off + keep]
        off += smalls[0][k].shape[1]
    conv = _up_perm(conv_s.reshape(L, 4, 2 * D_FF))
    grad["conv_w"] = lax.dynamic_slice(conv[:, 0:3], (0, 0, chip * cw_cols), (L, 3, cw_cols))
    grad["conv_b"] = conv[:, 3]
    loss_out = loss_s[0, 0]

    delta, new_m, new_v = {}, {}, {}
    for k in BIG + ("w_ada",):
        delta[k], new_m[k], new_v[k] = adamw(w[k], grad[k], m[k], v[k], name=f"adamw_{k}")
    outs = adamw_small(*[[src[k] for k in SMALL] for src in (w, grad, m, v)], name="adamw_small")
    for dst, o in zip((delta, new_m, new_v), outs):
        dst.update(dict(zip(SMALL, o)))
    return (loss_out, dx, *[grad[k] for k in WEIGHTS], *[delta[k] for k in WEIGHTS], *[new_m[k] for k in WEIGHTS], *[new_v[k] for k in WEIGHTS])
```

```python
import functools
import math

import jax
import jax.numpy as jnp
from jax import lax
from jax.experimental import pallas as pl
from jax.experimental.pallas import tpu as pltpu

F32 = jnp.float32
BF16 = jnp.bfloat16
MESH = pl.DeviceIdType.MESH

EPS = 1e-6
NEG = -1e30
HEAD = 64
LANES = 128
MLA_QK = 96
ROPE_THETA = 10000.0
REL_BUCKETS = 32
REL_MAX_DIST = 128
WINDOW = 128
D_FF = 2816
ADAM_LR, ADAM_B1, ADAM_B2, ADAM_EPS, ADAM_WD, ADAM_STEP = 0.001, 0.9, 0.999, 1e-08, 0.01, 10

VMEM_LIMIT = 56 * 1024 * 1024
STRIP = 32
P_SBQ, P_SBK, P_SBV, P_CQ, P_CKV, P_SLAB, P_SWQ, P_SWK, P_SWV, P_END = 0, 256, 512, 768, 1024, 1152, 1280, 1664, 1792, 1920
SW_PERM = (0, 3, 1, 4, 2, 5)


def _cp(*sem):
    return pltpu.CompilerParams(dimension_semantics=sem, vmem_limit_bytes=VMEM_LIMIT)


def _dot(a, b):
    return jnp.dot(a, b, preferred_element_type=F32)


def _dot_nt(a, b):
    return lax.dot_general(a, b, (((1,), (1,)), ((), ())), preferred_element_type=F32)


def _dot_tn(a, b):
    return lax.dot_general(a, b, (((0,), (0,)), ((), ())), preferred_element_type=F32)


def _split_dot(x, u):
    hi = x.astype(BF16)
    lo = (x - hi.astype(F32)).astype(BF16)
    return _dot(hi, u) + _dot(lo, u)


def _lane_masks():
    lane = lax.broadcasted_iota(jnp.int32, (1, LANES), 1)
    return (lane < HEAD, lane >= HEAD)


def _tile(n, cap, align=128):
    if n <= cap:
        return n
    t = cap - cap % align
    while t >= align:
        if n % t == 0:
            return t
        t -= align
    return n


def matmul(a, b, *, ta=False, tb=False, out_dtype=F32, tm=512, tn=512, tk=8192, name):
    M, K = (a.shape[1], a.shape[0]) if ta else a.shape
    N = b.shape[0] if tb else b.shape[1]
    tm, tn, tk = _tile(M, tm), _tile(N, tn), _tile(K, tk)
    nk = K // tk

    def body(a_ref, b_ref, o_ref, *scratch):
        av = a_ref[...].astype(BF16)
        bv = b_ref[...].astype(BF16)
        if ta:
            part = _dot_tn(av, bv)
        elif tb:
            part = _dot_nt(av, bv)
        else:
            part = _dot(av, bv)
        if nk == 1:
            o_ref[...] = part.astype(out_dtype)
        else:
            acc_ref, = scratch
            k = pl.program_id(2)

            @pl.when(k == 0)
            def _():
                acc_ref[...] = part

            @pl.when(k > 0)
            def _():
                acc_ref[...] += part

            @pl.when(k == nk - 1)
            def _():
                o_ref[...] = acc_ref[...].astype(out_dtype)

    n_outer = nk == 1 and tn * b.dtype.itemsize > tm * a.dtype.itemsize
    ij = (lambda p, q: (q, p)) if n_outer else (lambda p, q: (p, q))
    a_map = (lambda p, q, k: (k, ij(p, q)[0])) if ta else (lambda p, q, k: (ij(p, q)[0], k))
    b_map = (lambda p, q, k: (ij(p, q)[1], k)) if tb else (lambda p, q, k: (k, ij(p, q)[1]))
    grid = (N // tn, M // tm, nk) if n_outer else (M // tm, N // tn, nk)
    return pl.pallas_call(
        body, name=name, grid=grid,
        in_specs=[pl.BlockSpec((tk, tm) if ta else (tm, tk), a_map), pl.BlockSpec((tn, tk) if tb else (tk, tn), b_map)],
        out_specs=pl.BlockSpec((tm, tn), lambda p, q, k: ij(p, q)),
        out_shape=jax.ShapeDtypeStruct((M, N), out_dtype),
        scratch_shapes=[] if nk == 1 else [pltpu.VMEM((tm, tn), F32)],
        compiler_params=_cp("parallel", "parallel", "arbitrary"),
    )(a, b)


def matmul_res(a, b, res, gate, seq, *, tm=512, tn=1024, name):
    M, K = a.shape
    N = b.shape[1]
    tm, tn = _tile(min(M, seq), tm), _tile(N, tn)
    per_seq = seq // tm

    def body(a_ref, b_ref, r_ref, g_ref, y_ref, x_ref):
        y = _dot(a_ref[...].astype(BF16), b_ref[...].astype(BF16))
        y_ref[...] = y
        x_ref[...] = r_ref[...] + g_ref[...] * y

    out = jax.ShapeDtypeStruct((M, N), F32)
    return pl.pallas_call(
        body, name=name, grid=(M // tm, N // tn),
        in_specs=[pl.BlockSpec((tm, K), lambda i, j: (i, 0)), pl.BlockSpec((K, tn), lambda i, j: (0, j)),
                  pl.BlockSpec((tm, tn), lambda i, j: (i, j)), pl.BlockSpec((None, 1, tn), lambda i, j: (lax.div(i, jnp.int32(per_seq)), 0, j))],
        out_specs=[pl.BlockSpec((tm, tn), lambda i, j: (i, j))] * 2,
        out_shape=[out, out], compiler_params=_cp("parallel", "parallel"),
    )(a, b, res, gate)


def rms_fwd(x3, blk, W, g, sc=None, sh=None, *, tm=512, name):
    Bl, S, _ = x3.shape
    tm = min(tm, S)
    mod = sc is not None

    def body(x_ref, g_ref, *rest):
        o_ref = rest[-1]
        x = x_ref[...]
        r = lax.rsqrt(jnp.mean(x * x, axis=-1, keepdims=True) + EPS)
        y = x * r * g_ref[...]
        if mod:
            y = y * (1.0 + rest[0][...]) + rest[1][...]
        o_ref[...] = y.astype(BF16)

    vec = pl.BlockSpec((None, 1, W), lambda b, s: (b, 0, 0))
    return pl.pallas_call(
        body, name=name, grid=(Bl, S // tm),
        in_specs=[pl.BlockSpec((None, tm, W), lambda b, s: (b, s, blk)), pl.BlockSpec((1, W), lambda b, s: (0, 0))] + ([vec, vec] if mod else []),
        out_specs=pl.BlockSpec((None, tm, W), lambda b, s: (b, s, 0)),
        out_shape=jax.ShapeDtypeStruct((Bl, S, W), BF16),
        compiler_params=_cp("parallel", "parallel"),
    )(x3, g, *([sc, sh] if mod else []))


def rms_bwd(x3, blk, W, dy3, g, sc=None, dres3=None, *, tm=256, name):
    Bl, S, _ = x3.shape
    tm = min(tm, S)
    mod = sc is not None
    res = dres3 is not None

    def body(*refs):
        x_ref, dy_ref, g_ref = refs[:3]
        k = 3
        sc_ref = dr_ref = None
        if mod:
            sc_ref = refs[k]
            k += 1
        if res:
            dr_ref = refs[k]
            k += 1
        dx_ref, dg_ref = refs[k], refs[k + 1]
        b, s = pl.program_id(0), pl.program_id(1)
        x = x_ref[...]
        dy = dy_ref[...].astype(F32)
        g = g_ref[...]
        r = lax.rsqrt(jnp.mean(x * x, axis=-1, keepdims=True) + EPS)
        n = x * r
        if mod:
            dsc_ref, dsh_ref = refs[k + 2], refs[k + 3]
            one_sc = 1.0 + sc_ref[...]

            @pl.when(s == 0)
            def _():
                dsc_ref[...] = jnp.zeros_like(dsc_ref)
                dsh_ref[...] = jnp.zeros_like(dsh_ref)

            dsh_ref[...] += jnp.sum(dy, axis=0, keepdims=True)
            dsc_ref[...] += jnp.sum(dy * n * g, axis=0, keepdims=True)
            dyn = dy * one_sc
        else:
            dyn = dy

        @pl.when((b == 0) & (s == 0))
        def _():
            dg_ref[...] = jnp.zeros_like(dg_ref)

        dg_ref[...] += jnp.sum(dyn * n, axis=0, keepdims=True)
        dn = dyn * g
        dx = r * (dn - n * jnp.mean(dn * n, axis=-1, keepdims=True))
        if res:
            dx = dx + dr_ref[...]
        dx_ref[...] = dx

    blkspec = pl.BlockSpec((None, tm, W), lambda b, s: (b, s, 0))
    vec = pl.BlockSpec((None, 1, W), lambda b, s: (b, 0, 0))
    row = pl.BlockSpec((1, W), lambda b, s: (0, 0))
    in_specs = [pl.BlockSpec((None, tm, W), lambda b, s: (b, s, blk)), blkspec, row] + ([vec] if mod else []) + ([blkspec] if res else [])
    out_specs = [blkspec, row] + ([vec, vec] if mod else [])
    out_shape = [jax.ShapeDtypeStruct((Bl, S, W), F32), jax.ShapeDtypeStruct((1, W), F32)]
    if mod:
        out_shape += [jax.ShapeDtypeStruct((Bl, 1, W), F32)] * 2
    args = [x3, dy3, g] + ([sc] if mod else []) + ([dres3] if res else [])
    return pl.pallas_call(
        body, name=name, grid=(Bl, S // tm), in_specs=in_specs, out_specs=out_specs, out_shape=out_shape,
        compiler_params=_cp("arbitrary", "arbitrary"),
    )(*args)


def pair_rms_fwd(x3, blk0, npairs, g2, *, tm=1024, name):
    Bl, S, _ = x3.shape
    tm = min(tm, S)

    def body(x_ref, g_ref, o_ref):
        lo, hi = _lane_masks()
        x = x_ref[...]
        xx = x * x
        s0 = jnp.sum(jnp.where(lo, xx, 0.0), axis=-1, keepdims=True)
        s1 = jnp.sum(jnp.where(hi, xx, 0.0), axis=-1, keepdims=True)
        r = jnp.where(lo, lax.rsqrt(s0 / HEAD + EPS), lax.rsqrt(s1 / HEAD + EPS))
        o_ref[...] = (x * r * g_ref[...]).astype(BF16)

    return pl.pallas_call(
        body, name=name, grid=(Bl, S // tm, npairs),
        in_specs=[pl.BlockSpec((None, tm, LANES), lambda b, s, p: (b, s, blk0 + p)), pl.BlockSpec((1, LANES), lambda b, s, p: (0, 0))],
        out_specs=pl.BlockSpec((None, tm, LANES), lambda b, s, p: (b, s, p)),
        out_shape=jax.ShapeDtypeStruct((Bl, S, LANES * npairs), BF16),
        compiler_params=_cp("parallel", "parallel", "parallel"),
    )(x3, g2)


def pair_rms_bwd(x3, blk0, npairs, dy3, g2, *, tm=1024, name):
    Bl, S, _ = x3.shape
    tm = min(tm, S)

    def body(x_ref, dy_ref, g_ref, dx_ref, dg_ref):
        lo, hi = _lane_masks()
        first = (pl.program_id(0) == 0) & (pl.program_id(1) == 0) & (pl.program_id(2) == 0)
        x = x_ref[...]
        dy = dy_ref[...]
        xx = x * x
        s0 = jnp.sum(jnp.where(lo, xx, 0.0), axis=-1, keepdims=True)
        s1 = jnp.sum(jnp.where(hi, xx, 0.0), axis=-1, keepdims=True)
        r = jnp.where(lo, lax.rsqrt(s0 / HEAD + EPS), lax.rsqrt(s1 / HEAD + EPS))
        n = x * r

        @pl.when(first)
        def _():
            dg_ref[...] = jnp.zeros_like(dg_ref)

        part = jnp.sum(dy * n, axis=0, keepdims=True)
        dg_ref[...] += part + pltpu.roll(part, HEAD, 1)
        dn = dy * g_ref[...]
        t = dn * n
        m0 = jnp.sum(jnp.where(lo, t, 0.0), axis=-1, keepdims=True)
        m1 = jnp.sum(jnp.where(hi, t, 0.0), axis=-1, keepdims=True)
        dx_ref[...] = r * (dn - n * (jnp.where(lo, m0, m1) / HEAD))

    return pl.pallas_call(
        body, name=name, grid=(Bl, S // tm, npairs),
        in_specs=[pl.BlockSpec((None, tm, LANES), lambda b, s, p: (b, s, blk0 + p)), pl.BlockSpec((None, tm, LANES), lambda b, s, p: (b, s, p)),
                  pl.BlockSpec((1, LANES), lambda b, s, p: (0, 0))],
        out_specs=[pl.BlockSpec((None, tm, LANES), lambda b, s, p: (b, s, p)), pl.BlockSpec((1, LANES), lambda b, s, p: (0, 0))],
        out_shape=[jax.ShapeDtypeStruct((Bl, S, LANES * npairs), F32), jax.ShapeDtypeStruct((1, LANES), F32)],
        compiler_params=_cp("arbitrary", "arbitrary", "arbitrary"),
    )(x3, dy3, g2)


def _rot(y, cos_t, sin_a, sin_b):
    return y * cos_t + pltpu.roll(y, LANES - 16, 1) * sin_a + pltpu.roll(y, 16, 1) * sin_b


def _rot_t(d, cos_t, sin_a, sin_b):
    return d * cos_t + pltpu.roll(d * sin_a, 16, 1) + pltpu.roll(d * sin_b, LANES - 16, 1)


def rope_norm_fwd(x3, nheads, g, tabs, slab=None, *, tm=1024, name):
    Bl, S, _ = x3.shape
    tm = min(tm, S)
    has_slab = slab is not None

    def body(*refs):
        x_ref, g_ref, c_ref, sa_ref, sb_ref = refs[:5]
        o_ref = refs[-1]
        x = x_ref[...]
        if has_slab:
            x = x + refs[5][...]
        r = lax.rsqrt(jnp.sum(x * x, axis=-1, keepdims=True) / MLA_QK + EPS)
        o_ref[...] = _rot(x * r * g_ref[...], c_ref[...], sa_ref[...], sb_ref[...]).astype(BF16)

    head = pl.BlockSpec((None, tm, LANES), lambda b, s, h: (b, s, h))
    tab = pl.BlockSpec((None, tm, LANES), lambda b, s, h: (b, s, 0))
    in_specs = [head, pl.BlockSpec((1, LANES), lambda b, s, h: (0, 0)), tab, tab, tab]
    args = [x3, g, *tabs]
    if has_slab:
        sblk = slab[1]
        in_specs.append(pl.BlockSpec((None, tm, LANES), lambda b, s, h: (b, s, sblk)))
        args.append(slab[0])
    return pl.pallas_call(
        body, name=name, grid=(Bl, S // tm, nheads), in_specs=in_specs, out_specs=head,
        out_shape=jax.ShapeDtypeStruct((Bl, S, LANES * nheads), BF16),
        compiler_params=_cp("parallel", "parallel", "parallel"),
    )(*args)


def rope_norm_bwd(x3, nheads, dy3, g, tabs, slab=None, *, tm=1024, name):
    Bl, S, _ = x3.shape
    tm = min(tm, S)
    has_slab = slab is not None

    def body(*refs):
        x_ref, dy_ref, g_ref, c_ref, sa_ref, sb_ref = refs[:6]
        k = 7 if has_slab else 6
        dx_ref, dg_ref = refs[k], refs[k + 1]
        h = pl.program_id(2)
        first = (pl.program_id(0) == 0) & (pl.program_id(1) == 0) & (h == 0)
        x = x_ref[...]
        if has_slab:
            x = x + refs[6][...]
        g = g_ref[...]
        r = lax.rsqrt(jnp.sum(x * x, axis=-1, keepdims=True) / MLA_QK + EPS)
        n = x * r
        d = _rot_t(dy_ref[...], c_ref[...], sa_ref[...], sb_ref[...])

        @pl.when(first)
        def _():
            dg_ref[...] = jnp.zeros_like(dg_ref)

        dg_ref[...] += jnp.sum(d * n, axis=0, keepdims=True)
        dn = d * g
        dx = r * (dn - n * (jnp.sum(dn * n, axis=-1, keepdims=True) / MLA_QK))
        dx_ref[...] = dx
        if has_slab:
            ds_ref = refs[k + 2]

            @pl.when(h == 0)
            def _():
                ds_ref[...] = dx

            @pl.when(h > 0)
            def _():
                ds_ref[...] += dx

    head = pl.BlockSpec((None, tm, LANES), lambda b, s, h: (b, s, h))
    tab = pl.BlockSpec((None, tm, LANES), lambda b, s, h: (b, s, 0))
    row = pl.BlockSpec((1, LANES), lambda b, s, h: (0, 0))
    in_specs = [head, head, row, tab, tab, tab]
    args = [x3, dy3, g, *tabs]
    out_specs = [head, row]
    out_shape = [jax.ShapeDtypeStruct((Bl, S, LANES * nheads), F32), jax.ShapeDtypeStruct((1, LANES), F32)]
    if has_slab:
        sblk = slab[1]
        in_specs.append(pl.BlockSpec((None, tm, LANES), lambda b, s, h: (b, s, sblk)))
        args.append(slab[0])
        out_specs.append(tab)
        out_shape.append(jax.ShapeDtypeStruct((Bl, S, LANES), F32))
    return pl.pallas_call(
        body, name=name, grid=(Bl, S // tm, nheads), in_specs=in_specs, out_specs=out_specs, out_shape=out_shape,
        compiler_params=_cp("arbitrary", "arbitrary", "arbitrary"),
    )(*args)


def _softplus(z):
    return jnp.maximum(z, 0.0) + jnp.log(1.0 + jnp.exp(-jnp.abs(z)))


def _split_dots(xs, u):
    hi = [x.astype(BF16) for x in xs]
    lo = [(x - h.astype(F32)).astype(BF16) for x, h in zip(xs, hi)]
    top = [_dot(h, u) for h in hi]
    return [t + _dot(l, u) for t, l in zip(top, lo)]


SB_BLOCK = 256
SB_QBLOCK = 512


def sb_attn_fwd(proj3, *, plans=None, name):
    Bl, S, _ = proj3.shape
    tk = min(SB_BLOCK, S)
    tq = min(SB_QBLOCK, S)
    per_q = tq // tk
    scale = HEAD ** -0.5
    qb, kb0, vb0 = P_SBQ // LANES, P_SBK // LANES, P_SBV // LANES

    def body(q_ref, k_ref, v_ref, o_ref, rt_ref):
        i = pl.program_id(2)
        masks = _lane_masks()
        lane = lax.broadcasted_iota(jnp.int32, (1, LANES), 1)
        q = q_ref[...]
        qh = [jnp.where(m, q, 0.0).astype(BF16) for m in masks]
        rr = lax.broadcasted_iota(jnp.int32, (tq, tk), 0)
        cc = lax.broadcasted_iota(jnp.int32, (tq, tk), 1)
        u = (lax.broadcasted_iota(jnp.int32, (tk, tk), 0) > lax.broadcasted_iota(jnp.int32, (tk, tk), 1)).astype(BF16)

        rt_ref[...] = jnp.zeros_like(rt_ref)

        def step(t, carry):
            r0, r1, acc = carry
            j = (i + 1) * per_q - 1 - t
            off = pl.multiple_of(j * tk, tk)
            kb = k_ref[pl.ds(off, tk), :].astype(BF16)
            vb = v_ref[pl.ds(off, tk), :]
            strict = (cc + j * tk) < (rr + i * tq)
            rt_ref[...] = jnp.where(lane == j, r0, jnp.where(lane == j + HEAD, r1, rt_ref[...]))
            rs, two = [r0, r1], range(2)
            z = [_dot_nt(qh[h], kb) * scale for h in two]
            sp = [_softplus(z[h]) for h in two]
            keep = [jnp.where(strict, -sp[h], 0.0) for h in two]
            suf = _split_dots(keep, u)
            w = [jnp.where(strict, jnp.exp((z[h] - sp[h]) + suf[h] + rs[h]), 0.0) for h in two]
            pv = [_dot(w[h].astype(BF16), jnp.where(masks[h], vb, 0.0).astype(BF16)) for h in two]
            return rs[0] + jnp.sum(keep[0], axis=1, keepdims=True), rs[1] + jnp.sum(keep[1], axis=1, keepdims=True), acc + (pv[0] + pv[1])

        zero = jnp.zeros((tq, 1), F32)
        _, _, acc = lax.fori_loop(0, (i + 1) * per_q, step, (zero, zero, jnp.zeros((tq, LANES), F32)))
        o_ref[...] = acc

    seq = lambda blk0: pl.BlockSpec((None, S, LANES), lambda b, p, i: (b, 0, blk0 + p))
    out = pl.BlockSpec((None, tq, LANES), lambda b, p, i: (b, i, p))
    shp = jax.ShapeDtypeStruct((Bl, S, 2 * LANES), F32)
    return call_with_plans(
        body, plans, name=name, grid=(Bl, 2, S // tq),
        in_specs=[pl.BlockSpec((None, tq, LANES), lambda b, p, i: (b, i, qb + p)), seq(kb0), seq(vb0)],
        out_specs=[out, out], out_shape=[shp, shp], scratch_shapes=[], args=[proj3, proj3, proj3],
        sem=("arbitrary",) * 3 if plans else ("parallel", "parallel", "arbitrary"))


def sb_attn_bwd(proj3, rt3, do3, *, plans=None, name):
    Bl, S, _ = proj3.shape
    tk = min(SB_BLOCK, S)
    tq = min(SB_QBLOCK, S)
    per_q = tq // tk
    scale = HEAD ** -0.5
    qb, kb0, vb0 = P_SBQ // LANES, P_SBK // LANES, P_SBV // LANES

    def body(q_ref, k_ref, v_ref, rt_ref, do_ref, dq_ref, dk_ref, dv_ref):
        i = pl.program_id(2)

        @pl.when(i == 0)
        def _():
            dk_ref[...] = jnp.zeros_like(dk_ref)
            dv_ref[...] = jnp.zeros_like(dv_ref)

        masks = _lane_masks()
        lane = lax.broadcasted_iota(jnp.int32, (1, LANES), 1)
        q = q_ref[...]
        qh = [jnp.where(m, q, 0.0).astype(BF16) for m in masks]
        do_b = do_ref[...].astype(BF16)
        doh = [jnp.where(m, do_b, jnp.zeros_like(do_b)) for m in masks]
        rt = rt_ref[...]
        rr = lax.broadcasted_iota(jnp.int32, (tq, tk), 0)
        cc = lax.broadcasted_iota(jnp.int32, (tq, tk), 1)
        ur = lax.broadcasted_iota(jnp.int32, (tk, tk), 0)
        uc = lax.broadcasted_iota(jnp.int32, (tk, tk), 1)
        u_suffix = (ur > uc).astype(BF16)
        u_prefix = (ur < uc).astype(BF16)

        def step(j, carry):
            p0, p1, dq = carry
            off = pl.multiple_of(j * tk, tk)
            kf = k_ref[pl.ds(off, tk), :]
            kb = kf.astype(BF16)
            vb = v_ref[pl.ds(off, tk), :]
            strict = (cc + j * tk) < (rr + i * tq)
            ps, two = [p0, p1], range(2)
            r_j = [jnp.sum(jnp.where(lane == j + h * HEAD, rt, 0.0), axis=1, keepdims=True) for h in two]
            z = [_dot_nt(qh[h], kb) * scale for h in two]
            dw = [_dot_nt(doh[h], jnp.where(masks[h], vb, 0.0).astype(BF16)) for h in two]
            sp = [_softplus(z[h]) for h in two]
            keep = [jnp.where(strict, -sp[h], 0.0) for h in two]
            suf = _split_dots(keep, u_suffix)
            w = [jnp.where(strict, jnp.exp((z[h] - sp[h]) + suf[h] + r_j[h]), 0.0) for h in two]
            g = [dw[h] * w[h] for h in two]
            pre = _split_dots(g, u_prefix)
            dzb = [(jnp.where(strict, g[h] * jnp.exp(-sp[h]) - jnp.exp(z[h] - sp[h]) * (pre[h] + ps[h]), 0.0) * scale).astype(BF16) for h in two]
            dqs = [_dot(dzb[h], jnp.where(masks[h], kf, 0.0).astype(BF16)) for h in two]
            dks = [_dot_tn(dzb[h], qh[h]) for h in two]
            dvs = [_dot_tn(w[h].astype(BF16), doh[h]) for h in two]
            dk_ref[pl.ds(off, tk), :] += dks[0] + dks[1]
            dv_ref[pl.ds(off, tk), :] += dvs[0] + dvs[1]
            return ps[0] + jnp.sum(g[0], axis=1, keepdims=True), ps[1] + jnp.sum(g[1], axis=1, keepdims=True), dq + (dqs[0] + dqs[1])

        zero = jnp.zeros((tq, 1), F32)
        out = lax.fori_loop(0, (i + 1) * per_q, step, (zero, zero, jnp.zeros((tq, LANES), F32)))
        dq_ref[...] = out[2]

    seq_in = lambda blk0: pl.BlockSpec((None, S, LANES), lambda b, p, i: (b, 0, blk0 + p))
    blk = pl.BlockSpec((None, tq, LANES), lambda b, p, i: (b, i, p))
    seq_out = pl.BlockSpec((None, S, LANES), lambda b, p, i: (b, 0, p))
    shp = jax.ShapeDtypeStruct((Bl, S, 2 * LANES), F32)
    return call_with_plans(
        body, plans, name=name, grid=(Bl, 2, S // tq),
        in_specs=[pl.BlockSpec((None, tq, LANES), lambda b, p, i: (b, i, qb + p)), seq_in(kb0), seq_in(vb0), blk, blk],
        out_specs=[blk, seq_out, seq_out], out_shape=[shp, shp, shp], scratch_shapes=[], args=[proj3, proj3, proj3, rt3, do3],
        sem=("arbitrary",) * 3 if plans else ("parallel", "parallel", "arbitrary"))


def mla_attn_fwd(q3, k3, kv3, vblk0, *, tq=512, tk=512, plans=None, name):
    Bl, S, _ = q3.shape
    tq = min(tq, S)
    tk = min(tk, tq)
    per_q = tq // tk
    scale = MLA_QK ** -0.5

    def body(q_ref, k_ref, v_ref, o_ref, lse_ref):
        i = pl.program_id(2)
        masks = _lane_masks()
        rr = lax.broadcasted_iota(jnp.int32, (tq, tk), 0)
        cc = lax.broadcasted_iota(jnp.int32, (tq, tk), 1)
        qh = [q_ref[:, h * LANES:(h + 1) * LANES] for h in range(2)]

        def step(j, carry):
            m0, l0, m1, l1, acc = carry
            off = pl.multiple_of(j * tk, tk)
            vb = v_ref[pl.ds(off, tk), :]
            causal = (cc + j * tk) <= (rr + i * tq)
            ms, ls, two = [m0, m1], [l0, l1], range(2)
            kh = [k_ref[pl.ds(off, tk), h * LANES:(h + 1) * LANES] for h in two]
            s = [jnp.where(causal, _dot_nt(qh[h], kh[h]) * scale, NEG) for h in two]
            m_new = [jnp.maximum(ms[h], jnp.max(s[h], axis=1, keepdims=True)) for h in two]
            p = [jnp.exp(s[h] - m_new[h]) for h in two]
            alpha = [jnp.exp(ms[h] - m_new[h]) for h in two]
            ls = [alpha[h] * ls[h] + jnp.sum(p[h], axis=1, keepdims=True) for h in two]
            add = [_dot(p[h].astype(BF16), jnp.where(masks[h], vb, 0.0).astype(BF16)) for h in two]
            acc = acc * jnp.where(masks[0], alpha[0], alpha[1]) + (add[0] + add[1])
            return m_new[0], ls[0], m_new[1], ls[1], acc

        neg = jnp.full((tq, 1), NEG, F32)
        zero = jnp.zeros((tq, 1), F32)
        m0, l0, m1, l1, acc = lax.fori_loop(0, (i + 1) * per_q, step, (neg, zero, neg, zero, jnp.zeros((tq, LANES), F32)))
        o_ref[...] = acc / jnp.where(masks[0], l0, l1)
        lse_ref[...] = jnp.where(masks[0], m0 + jnp.log(l0), m1 + jnp.log(l1))

    out = pl.BlockSpec((None, tq, LANES), lambda b, p, i: (b, i, p))
    shp = jax.ShapeDtypeStruct((Bl, S, 3 * LANES), F32)
    return call_with_plans(
        body, plans, name=name, grid=(Bl, 3, S // tq),
        in_specs=[pl.BlockSpec((None, tq, 2 * LANES), lambda b, p, i: (b, i, p)), pl.BlockSpec((None, S, 2 * LANES), lambda b, p, i: (b, 0, p)),
                  pl.BlockSpec((None, S, LANES), lambda b, p, i: (b, 0, vblk0 + p))],
        out_specs=[out, out], out_shape=[shp, shp], scratch_shapes=[], args=[q3, k3, kv3],
        sem=("arbitrary",) * 3 if plans else ("parallel", "parallel", "arbitrary"))


def mla_attn_bwd(q3, k3, kv3, vblk0, o3, lse3, do3, *, tq=512, tk=256, name):
    Bl, S, _ = q3.shape
    tq = min(tq, S)
    tk = min(tk, tq)
    per_q = tq // tk
    nq = S // tq
    scale = MLA_QK ** -0.5

    def body(q_ref, k_ref, v_ref, o_ref, lse_ref, do_ref, dq_ref, dk_ref, dv_ref, s_scr, dp_scr, p_scr, ds_scr):
        j = pl.program_id(2)

        @pl.when(j == 0)
        def _():
            dq_ref[...] = jnp.zeros_like(dq_ref)

        masks = _lane_masks()
        vb = v_ref[...]
        vh = [jnp.where(m, vb, 0.0).astype(BF16) for m in masks]
        kh = [k_ref[:, h * LANES:(h + 1) * LANES] for h in range(2)]
        i0 = lax.div(j, jnp.int32(per_q))

        def step(i, carry, masked):
            dk0, dk1, dv = carry
            off = pl.multiple_of(i * tq, tq)
            do_b = do_ref[pl.ds(off, tq), :].astype(BF16)
            prod = do_b.astype(F32) * o_ref[pl.ds(off, tq), :]
            lse = lse_ref[pl.ds(off, tq), :]
            two = range(2)
            qh = [q_ref[pl.ds(off, tq), h * LANES:(h + 1) * LANES] for h in two]
            doh = [jnp.where(masks[h], do_b, jnp.zeros_like(do_b)) for h in two]
            delta = [jnp.sum(jnp.where(masks[h], prod, 0.0), axis=1, keepdims=True) for h in two]
            lse_h = [lse[:, h * HEAD:h * HEAD + 1] for h in two]
            for h in two:
                s_scr[h] = _dot_nt(qh[h], kh[h])
            for h in two:
                dp_scr[h] = _dot_nt(doh[h], vh[h])
            for r0 in range(0, tq, STRIP):
                rows = slice(r0, r0 + STRIP)
                for h in two:
                    s = s_scr[h, rows, :] * scale
                    if masked:
                        rr = lax.broadcasted_iota(jnp.int32, (STRIP, tk), 0) + (i * tq + r0)
                        cc = lax.broadcasted_iota(jnp.int32, (STRIP, tk), 1) + j * tk
                        s = jnp.where(cc <= rr, s, NEG)
                    p = jnp.exp(s - lse_h[h][rows])
                    p_scr[h, rows, :] = p.astype(BF16)
                    ds_scr[h, rows, :] = (p * (dp_scr[h, rows, :] - delta[h][rows])).astype(BF16)
            dqs = [_dot(ds_scr[h], kh[h]) * scale for h in two]
            dks = [dk0 + _dot_tn(ds_scr[0], qh[0]), dk1 + _dot_tn(ds_scr[1], qh[1])]
            dv = dv + _dot_tn(p_scr[0], doh[0]) + _dot_tn(p_scr[1], doh[1])
            for h in two:
                dq_ref[pl.ds(off, tq), h * LANES:(h + 1) * LANES] += dqs[h]
            return dks[0], dks[1], dv

        zero = jnp.zeros((tk, LANES), F32)
        carry = step(i0, (zero, zero, zero), True)
        dk0, dk1, dv = lax.fori_loop(i0 + 1, nq, lambda i, c: step(i, c, False), carry)
        dk_ref[:, 0:LANES] = dk0 * scale
        dk_ref[:, LANES:2 * LANES] = dk1 * scale
        dv_ref[...] = dv

    seq1 = pl.BlockSpec((None, S, LANES), lambda b, p, j: (b, 0, p))
    seq2 = pl.BlockSpec((None, S, 2 * LANES), lambda b, p, j: (b, 0, p))
    return pl.pallas_call(
        body, name=name, grid=(Bl, 3, S // tk),
        in_specs=[seq2, pl.BlockSpec((None, tk, 2 * LANES), lambda b, p, j: (b, j, p)),
                  pl.BlockSpec((None, tk, LANES), lambda b, p, j: (b, j, vblk0 + p)), seq1, seq1, seq1],
        out_specs=[seq2, pl.BlockSpec((None, tk, 2 * LANES), lambda b, p, j: (b, j, p)), pl.BlockSpec((None, tk, LANES), lambda b, p, j: (b, j, p))],
        out_shape=[jax.ShapeDtypeStruct((Bl, S, 6 * LANES), F32), jax.ShapeDtypeStruct((Bl, S, 6 * LANES), F32), jax.ShapeDtypeStruct((Bl, S, 3 * LANES), F32)],
        scratch_shapes=[pltpu.VMEM((2, tq, tk), F32), pltpu.VMEM((2, tq, tk), F32), pltpu.VMEM((2, tq, tk), BF16), pltpu.VMEM((2, tq, tk), BF16)],
        compiler_params=_cp("parallel", "parallel", "arbitrary"),
    )(q3, k3, kv3, o3, lse3, do3)


def _bucket_table():
    a = jnp.arange(WINDOW)[:, None]
    b = jnp.arange(2 * WINDOW)[None, :]
    dist = WINDOW + a - b
    max_exact = REL_BUCKETS // 2
    n = jnp.maximum(dist, 0)
    nf = jnp.maximum(n, 1).astype(F32)
    large = max_exact + (jnp.log(nf / max_exact) / math.log(REL_MAX_DIST / max_exact) * (REL_BUCKETS - max_exact)).astype(jnp.int32)
    large = jnp.minimum(large, REL_BUCKETS - 1)
    bucket = jnp.where(n < max_exact, n, large)
    return jnp.where((dist >= 0) & (dist < WINDOW), bucket, -1).astype(jnp.int32)


def swa_bias(rel_flat, bucket, *, name):
    def body(t_ref, b_ref, o_ref):
        bk = b_ref[...]
        for p in range(3):
            for hh in range(2):
                h = hh * 3 + p
                acc = jnp.full(bk.shape, NEG, F32)
                for b in range(REL_BUCKETS):
                    acc = jnp.where(bk == b, t_ref[b * 6 + h], acc)
                o_ref[p, hh] = acc

    return pl.pallas_call(
        body, name=name,
        in_specs=[pl.BlockSpec(memory_space=pltpu.SMEM), pl.BlockSpec(memory_space=pltpu.VMEM)],
        out_specs=pl.BlockSpec(memory_space=pltpu.VMEM),
        out_shape=jax.ShapeDtypeStruct((3, 2, WINDOW, 2 * WINDOW), F32),
    )(rel_flat, bucket)


def swa_bias_bwd(dbias, bucket, *, name):
    Bl = dbias.shape[0]

    def body(d_ref, b_ref, o_ref):
        bk = b_ref[...]
        lane = lax.broadcasted_iota(jnp.int32, (1, LANES), 1)
        rows = []
        for h in range(6):
            hh, p = divmod(h, 3)
            d = d_ref[0, p, hh]
            for bl in range(1, Bl):
                d = d + d_ref[bl, p, hh]
            row = jnp.zeros((1, LANES), F32)
            for b in range(REL_BUCKETS):
                s = jnp.sum(jnp.sum(jnp.where(bk == b, d, 0.0), axis=1, keepdims=True), axis=0, keepdims=True)
                row = row + jnp.where(lane == b, s, 0.0)
            rows.append(row)
        rows += [jnp.zeros((1, LANES), F32)] * 2
        o_ref[...] = jnp.concatenate(rows, axis=0)

    return pl.pallas_call(
        body, name=name,
        in_specs=[pl.BlockSpec(memory_space=pltpu.VMEM)] * 2, out_specs=pl.BlockSpec(memory_space=pltpu.VMEM),
        out_shape=jax.ShapeDtypeStruct((8, LANES), F32),
    )(dbias, bucket)


SWA_QBLOCKS = 8


def _swa_specs(vblk, nqb):
    rows = nqb * WINDOW
    cur = lambda blk: pl.BlockSpec((None, rows, LANES), lambda b, p, n: (b, n, blk))
    prev = lambda blk: pl.BlockSpec((None, WINDOW, LANES), lambda b, p, n: (b, jnp.maximum(n * nqb - 1, 0), blk))
    return [pl.BlockSpec((None, rows, LANES), lambda b, p, n: (b, n, p)), cur(0), prev(0), cur(vblk), prev(vblk),
            pl.BlockSpec((None, 2, WINDOW, 2 * WINDOW), lambda b, p, n: (p, 0, 0, 0)), pl.BlockSpec((None, 2, LANES), lambda b, p, n: (p, 0, 0))]


def _rows128(ref, m):
    return ref[m * WINDOW:(m + 1) * WINDOW, :]


def _swa_logits(qh, kp, kc, bias_h, first, scale):
    sp = jnp.where(first, NEG, _dot_nt(qh, kp) * scale + bias_h[:, :WINDOW])
    sc = _dot_nt(qh, kc) * scale + bias_h[:, WINDOW:]
    return sp, sc


def swa_attn_fwd(qn3, kn3, proj3, bias, sinks, *, plans=None, name):
    Bl, S, _ = qn3.shape
    scale = HEAD ** -0.5
    nqb = min(SWA_QBLOCKS, S // WINDOW)

    def body(q_ref, kc_ref, kp_ref, vc_ref, vp_ref, b_ref, s_ref, o_ref, lse_ref):
        seq_start = pl.program_id(2) == 0
        masks = _lane_masks()
        chains = [(m_, h) for m_ in range(nqb) for h in range(2)]
        kp = [kp_ref[...] if m_ == 0 else _rows128(kc_ref, m_ - 1) for m_ in range(nqb)]
        vp = [vp_ref[...] if m_ == 0 else _rows128(vc_ref, m_ - 1) for m_ in range(nqb)]
        kc = [_rows128(kc_ref, m_) for m_ in range(nqb)]
        vc = [_rows128(vc_ref, m_) for m_ in range(nqb)]
        sink = [s_ref[h:h + 1, 0:1] for h in range(2)]
        logits = {}
        for m_, h in chains:
            q = _rows128(q_ref, m_)
            qh = jnp.where(masks[h], q, jnp.zeros_like(q))
            logits[m_, h] = _swa_logits(qh, kp[m_], kc[m_], b_ref[h], seq_start if m_ == 0 else False, scale)
        mx = {c: jnp.maximum(jnp.maximum(jnp.max(logits[c][0], axis=1, keepdims=True), jnp.max(logits[c][1], axis=1, keepdims=True)), sink[c[1]])
              for c in chains}
        ex = {c: (jnp.exp(logits[c][0] - mx[c]), jnp.exp(logits[c][1] - mx[c])) for c in chains}
        den = {c: jnp.sum(ex[c][0], axis=1, keepdims=True) + jnp.sum(ex[c][1], axis=1, keepdims=True) + jnp.exp(sink[c[1]] - mx[c]) for c in chains}
        inv = {c: 1.0 / den[c] for c in chains}
        out = {}
        for m_, h in chains:
            c = (m_, h)
            out[c] = (_dot((ex[c][0] * inv[c]).astype(BF16), jnp.where(masks[h], vp[m_], 0.0).astype(BF16))
                      + _dot((ex[c][1] * inv[c]).astype(BF16), jnp.where(masks[h], vc[m_], 0.0).astype(BF16)))
        for m_ in range(nqb):
            o_ref[m_ * WINDOW:(m_ + 1) * WINDOW, :] = out[m_, 0] + out[m_, 1]
            lse_ref[m_ * WINDOW:(m_ + 1) * WINDOW, :] = jnp.where(masks[0], mx[m_, 0] + jnp.log(den[m_, 0]), mx[m_, 1] + jnp.log(den[m_, 1]))

    out = pl.BlockSpec((None, nqb * WINDOW, LANES), lambda b, p, n: (b, n, p))
    shp = jax.ShapeDtypeStruct((Bl, S, 3 * LANES), F32)
    return call_with_plans(
        body, plans, name=name, grid=(Bl, 3, S // (nqb * WINDOW)), in_specs=_swa_specs(P_SWV // LANES, nqb),
        out_specs=[out, out], out_shape=[shp, shp], scratch_shapes=[], args=[qn3, kn3, kn3, proj3, proj3, bias, sinks],
        sem=("arbitrary",) * 3 if plans else ("parallel", "parallel", "arbitrary"))


def swa_attn_bwd(qn3, kn3, proj3, bias, sinks, o3, lse3, do3, *, name):
    Bl, S, _ = qn3.shape
    scale = HEAD ** -0.5
    nqb = min(SWA_QBLOCKS, S // WINDOW)
    rows = nqb * WINDOW

    def body(q_ref, kc_ref, kp_ref, vc_ref, vp_ref, b_ref, s_ref, o_ref, lse_ref, do_ref,
             dq_ref, dk_ref, dv_ref, db_ref, dsk_ref):
        p_id, n = pl.program_id(1), pl.program_id(2)
        seq_start = n == 0

        @pl.when((p_id == 0) & seq_start)
        def _():
            dk_ref[...] = jnp.zeros_like(dk_ref)
            dv_ref[...] = jnp.zeros_like(dv_ref)

        @pl.when(seq_start)
        def _():
            db_ref[...] = jnp.zeros_like(db_ref)
            dsk_ref[...] = jnp.zeros_like(dsk_ref)

        masks = _lane_masks()
        zero = jnp.zeros((WINDOW, LANES), F32)
        chains = [(m_, h) for m_ in range(nqb) for h in range(2)]
        kp = [kp_ref[...] if m_ == 0 else _rows128(kc_ref, m_ - 1) for m_ in range(nqb)]
        vp = [vp_ref[...] if m_ == 0 else _rows128(vc_ref, m_ - 1) for m_ in range(nqb)]
        kc = [_rows128(kc_ref, m_) for m_ in range(nqb)]
        vc = [_rows128(vc_ref, m_) for m_ in range(nqb)]
        do_b = [_rows128(do_ref, m_).astype(BF16) for m_ in range(nqb)]
        prod = [do_b[m_].astype(F32) * _rows128(o_ref, m_) for m_ in range(nqb)]
        lse = [_rows128(lse_ref, m_) for m_ in range(nqb)]
        qh, doh, logits, lse_h, delta = {}, {}, {}, {}, {}
        for m_, h in chains:
            q = _rows128(q_ref, m_)
            qh[m_, h] = jnp.where(masks[h], q, jnp.zeros_like(q))
            doh[m_, h] = jnp.where(masks[h], do_b[m_], jnp.zeros_like(do_b[m_]))
            logits[m_, h] = _swa_logits(qh[m_, h], kp[m_], kc[m_], b_ref[h], seq_start if m_ == 0 else False, scale)
            lse_h[m_, h] = lse[m_][:, h * HEAD:h * HEAD + 1]
            delta[m_, h] = jnp.sum(jnp.where(masks[h], prod[m_], 0.0), axis=1, keepdims=True)
        pr = {c: (jnp.exp(logits[c][0] - lse_h[c]), jnp.exp(logits[c][1] - lse_h[c])) for c in chains}
        dp = {(m_, h): (_dot_nt(doh[m_, h], jnp.where(masks[h], vp[m_], 0.0).astype(BF16)),
                        _dot_nt(doh[m_, h], jnp.where(masks[h], vc[m_], 0.0).astype(BF16))) for m_, h in chains}
        ds = {c: (pr[c][0] * (dp[c][0] - delta[c]), pr[c][1] * (dp[c][1] - delta[c])) for c in chains}
        dsb = {c: ((ds[c][0] * scale).astype(BF16), (ds[c][1] * scale).astype(BF16)) for c in chains}
        dk_acc = [zero] * (nqb + 1)
        dv_acc = [zero] * (nqb + 1)
        db_acc = [[jnp.zeros((WINDOW, WINDOW), F32)] * 2 for _ in range(2)]
        dsk_acc = [jnp.zeros((1, 1), F32)] * 2
        dq = [zero] * nqb
        for m_, h in chains:
            c = (m_, h)
            db_acc[h] = [db_acc[h][0] + ds[c][0], db_acc[h][1] + ds[c][1]]
            dsk_acc[h] = dsk_acc[h] - jnp.sum(jnp.exp(s_ref[h:h + 1, 0:1] - lse_h[c]) * delta[c], axis=0, keepdims=True)
            dq[m_] = (dq[m_] + _dot(dsb[c][0], jnp.where(masks[h], kp[m_], jnp.zeros_like(kp[m_])))
                      + _dot(dsb[c][1], jnp.where(masks[h], kc[m_], jnp.zeros_like(kc[m_]))))
            dk_acc[m_] = dk_acc[m_] + _dot_tn(dsb[c][0], qh[c])
            dk_acc[m_ + 1] = dk_acc[m_ + 1] + _dot_tn(dsb[c][1], qh[c])
            dv_acc[m_] = dv_acc[m_] + _dot_tn(pr[c][0].astype(BF16), doh[c])
            dv_acc[m_ + 1] = dv_acc[m_ + 1] + _dot_tn(pr[c][1].astype(BF16), doh[c])
        for m_ in range(nqb):
            dq_ref[m_ * WINDOW:(m_ + 1) * WINDOW, :] = dq[m_]
        for h in range(2):
            db_ref[h, :, 0:WINDOW] += db_acc[h][0]
            db_ref[h, :, WINDOW:2 * WINDOW] += db_acc[h][1]
            dsk_ref[h:h + 1, :] += jnp.broadcast_to(dsk_acc[h], (1, LANES))
        offp = pl.multiple_of(jnp.maximum(n * nqb - 1, 0) * WINDOW, WINDOW)
        dk_ref[pl.ds(offp, WINDOW), :] += dk_acc[0]
        dv_ref[pl.ds(offp, WINDOW), :] += dv_acc[0]
        for m_ in range(nqb):
            off = pl.multiple_of(n * rows + m_ * WINDOW, WINDOW)
            dk_ref[pl.ds(off, WINDOW), :] += dk_acc[m_ + 1]
            dv_ref[pl.ds(off, WINDOW), :] += dv_acc[m_ + 1]

    blk = pl.BlockSpec((None, rows, LANES), lambda b, p, n: (b, n, p))
    seq = pl.BlockSpec((None, S, LANES), lambda b, p, n: (b, 0, 0))
    return pl.pallas_call(
        body, name=name, grid=(Bl, 3, S // rows), in_specs=_swa_specs(P_SWV // LANES, nqb) + [blk, blk, blk],
        out_specs=[blk, seq, seq, pl.BlockSpec((None, None, 2, WINDOW, 2 * WINDOW), lambda b, p, n: (b, p, 0, 0, 0)),
                   pl.BlockSpec((None, None, 2, LANES), lambda b, p, n: (b, p, 0, 0))],
        out_shape=[jax.ShapeDtypeStruct((Bl, S, 3 * LANES), F32), jax.ShapeDtypeStruct((Bl, S, LANES), F32), jax.ShapeDtypeStruct((Bl, S, LANES), F32),
                   jax.ShapeDtypeStruct((Bl, 3, 2, WINDOW, 2 * WINDOW), F32), jax.ShapeDtypeStruct((Bl, 3, 2, LANES), F32)],
        compiler_params=_cp("arbitrary", "arbitrary", "arbitrary"),
    )(qn3, kn3, kn3, proj3, proj3, bias, sinks, o3, lse3, do3)


CONV_ROWS = 64
CONV_LANES = 128


def _conv_strip(x_ref, h_ref, w, b, r0, cols, first_blk):
    x = x_ref[r0:r0 + CONV_ROWS, cols]
    if r0 == 0:
        rows = lax.broadcasted_iota(jnp.int32, x.shape, 0)
        h6 = jnp.where(first_blk, 0.0, h_ref[6:7, cols])
        h7 = jnp.where(first_blk, 0.0, h_ref[7:8, cols])
        x1 = jnp.where(rows == 0, h7, pltpu.roll(x, 1, 0))
        x2 = jnp.where(rows == 0, h6, jnp.where(rows == 1, h7, pltpu.roll(x, 2, 0)))
    else:
        x1 = x_ref[r0 - 1:r0 - 1 + CONV_ROWS, cols]
        x2 = x_ref[r0 - 2:r0 - 2 + CONV_ROWS, cols]
    return w[0:1] * x2 + w[1:2] * x1 + w[2:3] * x + b, x, x1, x2


FF_BLK = D_FF // 2


def _up_perm(a):
    q = FF_BLK
    return _cat([a[..., 0:q], a[..., 2 * q:3 * q], a[..., q:2 * q], a[..., 3 * q:4 * q]])


def conv_gate_fwd(up3, cw, cb, *, tm=256, name):
    Bl, S, _ = up3.shape
    tm = min(tm, S)
    W = 2 * FF_BLK

    def body(x_ref, h_ref, w_ref, b_ref, o_ref):
        first = pl.program_id(1) == 0

        def chunk(c, carry):
            cg = pl.ds(pl.multiple_of(c * CONV_LANES, CONV_LANES), CONV_LANES)
            cv = pl.ds(pl.multiple_of(FF_BLK + c * CONV_LANES, CONV_LANES), CONV_LANES)
            wg, wv, bg, bv = w_ref[:, cg], w_ref[:, cv], b_ref[:, cg], b_ref[:, cv]
            for r0 in range(0, tm, CONV_ROWS):
                ug = _conv_strip(x_ref, h_ref, wg, bg, r0, cg, first)[0]
                uv = _conv_strip(x_ref, h_ref, wv, bv, r0, cv, first)[0]
                o_ref[r0:r0 + CONV_ROWS, cg] = (ug * jax.nn.sigmoid(ug) * uv).astype(BF16)
            return carry

        lax.fori_loop(0, FF_BLK // CONV_LANES, chunk, 0)

    hb = tm // 8
    return pl.pallas_call(
        body, name=name, grid=(Bl, S // tm, 2),
        in_specs=[pl.BlockSpec((None, tm, W), lambda b, s, c: (b, s, c)),
                  pl.BlockSpec((None, 8, W), lambda b, s, c: (b, jnp.maximum(s * hb - 1, 0), c)),
                  pl.BlockSpec((3, W), lambda b, s, c: (0, c)), pl.BlockSpec((1, W), lambda b, s, c: (0, c))],
        out_specs=pl.BlockSpec((None, tm, FF_BLK), lambda b, s, c: (b, s, c)),
        out_shape=jax.ShapeDtypeStruct((Bl, S, D_FF), BF16),
        compiler_params=_cp("parallel", "parallel", "parallel"),
    )(up3, up3, cw, cb)


def conv_gate_bwd(up3, cw, cb, da3, *, tm=256, name):
    Bl, S, _ = up3.shape
    tm = min(tm, S)
    ns = S // tm
    W = 2 * FF_BLK

    def body(x_ref, h_ref, w_ref, b_ref, da_ref, dup_ref, dw_ref, nxt_ref, du_scr):
        b, s = pl.program_id(1), pl.program_id(2)
        seq_end = s == 0
        first = s == ns - 1

        @pl.when((b == 0) & seq_end)
        def _():
            dw_ref[...] = jnp.zeros_like(dw_ref)

        def du_chunk(c, carry):
            cg = pl.ds(pl.multiple_of(c * CONV_LANES, CONV_LANES), CONV_LANES)
            cv = pl.ds(pl.multiple_of(FF_BLK + c * CONV_LANES, CONV_LANES), CONV_LANES)
            wg, wv, bg, bv = w_ref[:, cg], w_ref[:, cv], b_ref[:, cg], b_ref[:, cv]
            acc_g = [jnp.zeros((1, CONV_LANES), F32)] * 4
            acc_v = [jnp.zeros((1, CONV_LANES), F32)] * 4
            for r0 in range(0, tm, CONV_ROWS):
                ug, xg, xg1, xg2 = _conv_strip(x_ref, h_ref, wg, bg, r0, cg, first)
                uv, xv, xv1, xv2 = _conv_strip(x_ref, h_ref, wv, bv, r0, cv, first)
                da = da_ref[r0:r0 + CONV_ROWS, cg].astype(F32)
                sg = jax.nn.sigmoid(ug)
                dug = da * uv * sg * (1.0 + ug * (1.0 - sg))
                duv = da * ug * sg
                du_scr[r0:r0 + CONV_ROWS, cg] = dug
                du_scr[r0:r0 + CONV_ROWS, cv] = duv
                col = lambda t: jnp.sum(t, axis=0, keepdims=True)
                acc_g = [acc_g[0] + col(dug * xg2), acc_g[1] + col(dug * xg1), acc_g[2] + col(dug * xg), acc_g[3] + col(dug)]
                acc_v = [acc_v[0] + col(duv * xv2), acc_v[1] + col(duv * xv1), acc_v[2] + col(duv * xv), acc_v[3] + col(duv)]
            for t in range(4):
                dw_ref[t:t + 1, cg] += acc_g[t]
                dw_ref[t:t + 1, cv] += acc_v[t]
            return carry

        lax.fori_loop(0, FF_BLK // CONV_LANES, du_chunk, 0)
        du_scr[tm:tm + 8, :] = jnp.where(seq_end, 0.0, nxt_ref[...])

        def dup_chunk(c, carry):
            cols = pl.ds(pl.multiple_of(c * CONV_LANES, CONV_LANES), CONV_LANES)
            w = w_ref[:, cols]
            for r0 in range(0, tm, CONV_ROWS):
                d0 = du_scr[r0:r0 + CONV_ROWS, cols]
                d1 = du_scr[r0 + 1:r0 + 1 + CONV_ROWS, cols]
                d2 = du_scr[r0 + 2:r0 + 2 + CONV_ROWS, cols]
                dup_ref[r0:r0 + CONV_ROWS, cols] = (w[2:3] * d0 + w[1:2] * d1 + w[0:1] * d2).astype(BF16)
            return carry

        lax.fori_loop(0, W // CONV_LANES, dup_chunk, 0)
        nxt_ref[...] = du_scr[0:8, :]

    hb = tm // 8
    rb = lambda s: ns - 1 - s
    return pl.pallas_call(
        body, name=name, grid=(2, Bl, ns),
        in_specs=[pl.BlockSpec((None, tm, W), lambda c, b, s: (b, rb(s), c)),
                  pl.BlockSpec((None, 8, W), lambda c, b, s: (b, jnp.maximum(rb(s) * hb - 1, 0), c)),
                  pl.BlockSpec((3, W), lambda c, b, s: (0, c)), pl.BlockSpec((1, W), lambda c, b, s: (0, c)),
                  pl.BlockSpec((None, tm, FF_BLK), lambda c, b, s: (b, rb(s), c))],
        out_specs=[pl.BlockSpec((None, tm, W), lambda c, b, s: (b, rb(s), c)), pl.BlockSpec((8, W), lambda c, b, s: (0, c))],
        out_shape=[jax.ShapeDtypeStruct((Bl, S, 2 * D_FF), BF16), jax.ShapeDtypeStruct((8, 2 * D_FF), F32)],
        scratch_shapes=[pltpu.VMEM((8, W), F32), pltpu.VMEM((tm + 8, W), F32)],
        compiler_params=_cp("arbitrary", "arbitrary", "arbitrary"),
    )(up3, up3, cw, cb, da3)


def gate_bwd(dx3, y3, gate, *, tm=512, name):
    Bl, S, D = dx3.shape
    tm = min(tm, S)

    def body(dx_ref, y_ref, g_ref, o_ref, dg_ref):
        @pl.when(pl.program_id(1) == 0)
        def _():
            dg_ref[...] = jnp.zeros_like(dg_ref)

        dx = dx_ref[...]
        dg_ref[...] += jnp.sum(dx * y_ref[...], axis=0, keepdims=True)
        o_ref[...] = (dx * g_ref[...]).astype(BF16)

    blk = pl.BlockSpec((None, tm, D), lambda b, s: (b, s, 0))
    vec = pl.BlockSpec((None, 1, D), lambda b, s: (b, 0, 0))
    return pl.pallas_call(
        body, name=name, grid=(Bl, S // tm), in_specs=[blk, blk, vec], out_specs=[blk, vec],
        out_shape=[jax.ShapeDtypeStruct((Bl, S, D), BF16), jax.ShapeDtypeStruct((Bl, 1, D), F32)],
        compiler_params=_cp("parallel", "arbitrary"),
    )(dx3, y3, gate)


def loss_grad(y3, t3, *, tm=512, name):
    Bl, S, D = y3.shape
    tm = min(tm, S)
    last = (Bl - 1, S // tm - 1)

    def body(y_ref, t_ref, dy_ref, l_ref, acc_ref):
        b, s = pl.program_id(0), pl.program_id(1)

        @pl.when((b == 0) & (s == 0))
        def _():
            acc_ref[...] = jnp.zeros_like(acc_ref)

        e = y_ref[...] - t_ref[...]
        dy_ref[...] = e * (1.0 / D)
        acc_ref[...] += jnp.sum(e * e, axis=0, keepdims=True)

        @pl.when((b == last[0]) & (s == last[1]))
        def _():
            l_ref[...] = jnp.broadcast_to(jnp.sum(acc_ref[...], axis=1, keepdims=True) * (0.5 / D), (1, LANES))

    blk = pl.BlockSpec((None, tm, D), lambda b, s: (b, s, 0))
    return pl.pallas_call(
        body, name=name, grid=(Bl, S // tm), in_specs=[blk, blk],
        out_specs=[blk, pl.BlockSpec((1, LANES), lambda b, s: (0, 0))],
        out_shape=[jax.ShapeDtypeStruct((Bl, S, D), F32), jax.ShapeDtypeStruct((1, LANES), F32)],
        scratch_shapes=[pltpu.VMEM((1, D), F32)], compiler_params=_cp("arbitrary", "arbitrary"),
    )(y3, t3)


def adamw(w, g, m, v, *, name):
    L, R, C = w.shape
    tr = _tile(R, 512, 8)

    def body(w_ref, g_ref, m_ref, v_ref, d_ref, m2_ref, v2_ref):
        d_ref[...], m2_ref[...], v2_ref[...] = _adam_update(w_ref[...], g_ref[...], m_ref[...], v_ref[...])

    blk = pl.BlockSpec((None, tr, C), lambda l, i: (l, i, 0))
    shp = jax.ShapeDtypeStruct((L, R, C), F32)
    return pl.pallas_call(
        body, name=name, grid=(L, R // tr), in_specs=[blk] * 4, out_specs=[blk] * 3, out_shape=[shp] * 3,
        compiler_params=_cp("parallel", "parallel"),
    )(w, g, m, v)


def sum_leading(x, *, out_dtype=F32, tr=256, name):
    n, R, C = x.shape
    tr = _tile(R, tr, 16)

    def body(x_ref, o_ref):
        acc = x_ref[0].astype(F32)
        for k in range(1, n):
            acc = acc + x_ref[k].astype(F32)
        o_ref[...] = acc.astype(out_dtype)

    return pl.pallas_call(
        body, name=name, grid=(R // tr,), in_specs=[pl.BlockSpec((n, tr, C), lambda i: (0, i, 0))],
        out_specs=pl.BlockSpec((tr, C), lambda i: (i, 0)), out_shape=jax.ShapeDtypeStruct((R, C), out_dtype),
        compiler_params=_cp("parallel"),
    )(x)


def _adam_update(w, g, m, v):
    c1 = 1.0 / (1.0 - ADAM_B1 ** ADAM_STEP)
    c2 = 1.0 / (1.0 - ADAM_B2 ** ADAM_STEP)
    m2 = ADAM_B1 * m + (1.0 - ADAM_B1) * g
    v2 = ADAM_B2 * v + (1.0 - ADAM_B2) * (g * g)
    return -ADAM_LR * ((m2 * c1) / (jnp.sqrt(v2 * c2) + ADAM_EPS) + ADAM_WD * w), m2, v2


def adamw_small(ws, gs, ms, vs, *, name):
    na = len(ws)

    def body(*refs):
        w_r, g_r, m_r, v_r = (refs[i * na:(i + 1) * na] for i in range(4))
        d_r, m2_r, v2_r = (refs[(4 + i) * na:(5 + i) * na] for i in range(3))
        for a in range(na):
            d_r[a][...], m2_r[a][...], v2_r[a][...] = _adam_update(w_r[a][...], g_r[a][...], m_r[a][...], v_r[a][...])

    vm = pl.BlockSpec(memory_space=pltpu.VMEM)
    shp = [jax.ShapeDtypeStruct(w.shape, F32) for w in ws]
    out = pl.pallas_call(body, name=name, in_specs=[vm] * (4 * na), out_specs=[vm] * (3 * na), out_shape=shp * 3)(*ws, *gs, *ms, *vs)
    return out[:na], out[na:2 * na], out[2 * na:]


def sum_small(xs, *, name):
    na = len(xs)

    def body(*refs):
        for x_ref, o_ref in zip(refs[:na], refs[na:]):
            acc = x_ref[0]
            for k in range(1, x_ref.shape[0]):
                acc = acc + x_ref[k]
            o_ref[...] = acc

    vm = pl.BlockSpec(memory_space=pltpu.VMEM)
    return pl.pallas_call(body, name=name, in_specs=[vm] * na, out_specs=[vm] * na,
                          out_shape=[jax.ShapeDtypeStruct(x.shape[1:], x.dtype) for x in xs])(*xs)


def pair_add_half(g4, recv, c_arr, *, tr=512, name):
    _, R, C = g4.shape
    H = R // 2
    tr = _tile(H, tr, 16)
    nb = H // tr

    def body(c_ref, g_ref, r_ref, o_ref):
        o_ref[...] = (g_ref[...].astype(F32) + r_ref[...].astype(F32)).astype(BF16)

    grid_spec = pltpu.PrefetchScalarGridSpec(
        num_scalar_prefetch=1, grid=(4, nb),
        in_specs=[pl.BlockSpec((None, tr, C), lambda k, i, c_ref: (k, c_ref[0] * nb + i, 0)),
                  pl.BlockSpec((None, tr, C), lambda k, i, c_ref: (k, i, 0))],
        out_specs=pl.BlockSpec((None, tr, C), lambda k, i, c_ref: (k, i, 0)),
    )
    return pl.pallas_call(
        body, name=name, grid_spec=grid_spec, out_shape=jax.ShapeDtypeStruct((4, H, C), BF16),
        compiler_params=_cp("parallel", "parallel"),
    )(c_arr, g4, recv)


def chip_sum_into(landed, pair, sel, *, tr=512, name):
    _, H, C = landed.shape
    tr = _tile(H, tr, 16)
    nb = H // tr

    def body(s_ref, l0, l1, l2, l3, p_ref, o_ref):
        own = p_ref[...].astype(F32)
        acc = None
        for k, l_ref in enumerate((l0, l1, l2, l3)):
            part = jnp.where(s_ref[0] == k, own, l_ref[...].astype(F32))
            acc = part if acc is None else acc + part
        o_ref[...] = acc

    def slot(k):
        return pl.BlockSpec((None, tr, C), lambda i, s: (jnp.where(s[0] == k, (k + 1) % 4, k), i, 0))

    grid_spec = pltpu.PrefetchScalarGridSpec(
        num_scalar_prefetch=1, grid=(nb,),
        in_specs=[slot(0), slot(1), slot(2), slot(3), pl.BlockSpec((None, tr, C), lambda i, s: (s[0], i, 0))],
        out_specs=pl.BlockSpec((tr, C), lambda i, s: (s[1] * nb + i, 0)),
    )
    return pl.pallas_call(
        body, name=name, grid_spec=grid_spec, out_shape=jax.ShapeDtypeStruct((2 * H, C), F32), compiler_params=_cp("parallel"),
    )(sel, landed, landed, landed, landed, pair)


def mods_matmul(c_all, w_ada, b_ada_cols, *, tn=512, name):
    L, D, E = w_ada.shape
    nb = c_all.shape[0]
    tn = _tile(E, tn)

    def body(c_ref, w_ref, b_ref, o_ref):
        c = c_ref[...]
        a = c * jax.nn.sigmoid(c)
        o_ref[...] = jnp.dot(a, w_ref[...], preferred_element_type=F32, precision=lax.Precision.HIGHEST) + b_ref[...]

    return pl.pallas_call(
        body, name=name, grid=(L, E // tn),
        in_specs=[pl.BlockSpec((nb, D), lambda l, j: (0, 0)), pl.BlockSpec((None, D, tn), lambda l, j: (l, 0, j)),
                  pl.BlockSpec((None, 1, tn), lambda l, j: (l, 0, j))],
        out_specs=pl.BlockSpec((None, nb, tn), lambda l, j: (l, 0, j)),
        out_shape=jax.ShapeDtypeStruct((L, nb, E), F32), compiler_params=_cp("parallel", "parallel"),
    )(c_all, w_ada, b_ada_cols)


def ada_grad(c_all, dmods, *, tn=512, name):
    L, nb, E = dmods.shape
    D = c_all.shape[1]
    tn = _tile(E, tn)

    def body(c_ref, d_ref, o_ref):
        c = c_ref[...]
        a = c * jax.nn.sigmoid(c)
        o_ref[...] = lax.dot_general(a, d_ref[...], (((0,), (0,)), ((), ())), preferred_element_type=F32, precision=lax.Precision.HIGHEST)

    return pl.pallas_call(
        body, name=name, grid=(L, E // tn),
        in_specs=[pl.BlockSpec((nb, D), lambda l, j: (0, 0)), pl.BlockSpec((None, nb, tn), lambda l, j: (l, 0, j))],
        out_specs=pl.BlockSpec((None, D, tn), lambda l, j: (l, 0, j)),
        out_shape=jax.ShapeDtypeStruct((L, D, E), F32), compiler_params=_cp("parallel", "parallel"),
    )(c_all, dmods)


HBM = pl.BlockSpec(memory_space=pltpu.HBM)


def _me():
    return lax.axis_index("x"), lax.axis_index("y"), lax.axis_index("c")


def _flip(v, bit):
    return 1 - v if bit else v


def allgather8(xs, *, name):
    na = len(xs)

    def body(*refs):
        x_refs, out_refs = refs[:na], refs[na:2 * na]
        send_sems, recv_sems = refs[2 * na], refs[2 * na + 1]
        x, y, c = _me()
        me = 4 * x + 2 * y + c
        for x_ref, out_ref in zip(x_refs, out_refs):
            out_ref[me] = x_ref[...]
        sends = []
        for a, (x_ref, out_ref) in enumerate(zip(x_refs, out_refs)):
            for k in range(1, 8):
                peer = (_flip(x, k & 4), _flip(y, k & 2), _flip(c, k & 1))
                cp = pltpu.make_async_remote_copy(src_ref=x_ref, dst_ref=out_ref.at[me], send_sem=send_sems.at[a, k - 1],
                                                  recv_sem=recv_sems.at[a, k - 1], device_id=peer, device_id_type=MESH)
                cp.start()
                sends.append(cp)
        for a, (x_ref, out_ref) in enumerate(zip(x_refs, out_refs)):
            for k in range(1, 8):
                peer = (_flip(x, k & 4), _flip(y, k & 2), _flip(c, k & 1))
                src = 4 * peer[0] + 2 * peer[1] + peer[2]
                pltpu.make_async_remote_copy(src_ref=x_ref, dst_ref=out_ref.at[src], send_sem=send_sems.at[a, k - 1],
                                             recv_sem=recv_sems.at[a, k - 1], device_id=peer, device_id_type=MESH).wait_recv()
        for cp in sends:
            cp.wait_send()

    vm = pl.BlockSpec(memory_space=pltpu.VMEM)
    return pl.pallas_call(
        body, name=name, in_specs=[vm] * na, out_specs=[vm] * na,
        out_shape=[jax.ShapeDtypeStruct((8,) + a.shape, a.dtype) for a in xs],
        scratch_shapes=[pltpu.SemaphoreType.DMA((na, 7)), pltpu.SemaphoreType.DMA((na, 7))],
    )(*xs)


LOCAL_CHUNKS = 8


def _copy_via_vmem(src, dst_at, rows, buf, sem):
    ch = buf.shape[0]
    for i in range(rows // ch):
        load = pltpu.make_async_copy(src.at[pl.ds(i * ch, ch)], buf, sem)
        load.start()
        load.wait()
        store = pltpu.make_async_copy(buf, dst_at(i * ch, ch), sem)
        store.start()
        store.wait()


def _chunk_buf(rows, cols, dtype):
    align = 16 if dtype == BF16 else 8
    for n in range(LOCAL_CHUNKS, 0, -1):
        if rows % n == 0 and (rows // n) % align == 0:
            return pltpu.VMEM((rows // n, cols), dtype)
    return pltpu.VMEM((rows, cols), dtype)


def gather_weights(ws, *, name):
    na = len(ws)

    def body(*refs):
        x_refs, out_refs = refs[:na], refs[na:2 * na]
        send_sems, recv_sems, local_sem = refs[2 * na:2 * na + 3]
        bufs = refs[2 * na + 3:]
        x, y, c = _me()
        j = 2 * x + y
        chips = [(_flip(x, k & 2), _flip(y, k & 1)) for k in range(1, 4)]
        sends = []
        for a, (x_ref, out_ref) in enumerate(zip(x_refs, out_refs)):
            H = x_ref.shape[0] // 2
            for k, (px, py) in enumerate(chips):
                cp = pltpu.make_async_remote_copy(src_ref=x_ref.at[pl.ds(c * H, H)], dst_ref=out_ref.at[j, pl.ds(c * H, H)],
                                                  send_sem=send_sems.at[a, k], recv_sem=recv_sems.at[a, k],
                                                  device_id=(px, py, c), device_id_type=MESH)
                cp.start()
                sends.append(cp)
        for x_ref, out_ref, buf in zip(x_refs, out_refs, bufs):
            _copy_via_vmem(x_ref, lambda o, n, out_ref=out_ref: out_ref.at[j, pl.ds(o, n)], x_ref.shape[0], buf, local_sem)
        for a, out_ref in enumerate(out_refs):
            H = out_ref.shape[1] // 2
            for k, (px, py) in enumerate(chips):
                slot = out_ref.at[2 * px + py, pl.ds(c * H, H)]
                pltpu.make_async_remote_copy(src_ref=slot, dst_ref=slot, send_sem=send_sems.at[a, k], recv_sem=recv_sems.at[a, k],
                                             device_id=(px, py, c), device_id_type=MESH).wait_recv()
                cp = pltpu.make_async_remote_copy(src_ref=slot, dst_ref=slot, send_sem=send_sems.at[a, 3 + k],
                                                  recv_sem=recv_sems.at[a, 3 + k], device_id=(x, y, 1 - c), device_id_type=MESH)
                cp.start()
                sends.append(cp)
        for a, out_ref in enumerate(out_refs):
            H = out_ref.shape[1] // 2
            for k, (px, py) in enumerate(chips):
                slot = out_ref.at[2 * px + py, pl.ds((1 - c) * H, H)]
                pltpu.make_async_remote_copy(src_ref=slot, dst_ref=slot, send_sem=send_sems.at[a, 3 + k], recv_sem=recv_sems.at[a, 3 + k],
                                             device_id=(x, y, 1 - c), device_id_type=MESH).wait_recv()
        for cp in sends:
            cp.wait_send()

    return pl.pallas_call(
        body, name=name, in_specs=[HBM] * na, out_specs=[HBM] * na,
        out_shape=[jax.ShapeDtypeStruct((4,) + w.shape, w.dtype) for w in ws],
        scratch_shapes=[pltpu.SemaphoreType.DMA((na, 6)), pltpu.SemaphoreType.DMA((na, 6)), pltpu.SemaphoreType.DMA]
        + [_chunk_buf(w.shape[0], w.shape[1], w.dtype) for w in ws],
    )(*ws)


def swap_halves(gs, *, name):
    na = len(gs)

    def body(*refs):
        g_refs, out_refs = refs[:na], refs[na:2 * na]
        send_sems, recv_sems = refs[2 * na:]
        x, y, c = _me()
        sib = (x, y, 1 - c)
        sends = []
        for a, (g_ref, out_ref) in enumerate(zip(g_refs, out_refs)):
            H = g_ref.shape[1] // 2
            for k in range(4):
                cp = pltpu.make_async_remote_copy(src_ref=g_ref.at[k, pl.ds((1 - c) * H, H)], dst_ref=out_ref.at[k],
                                                  send_sem=send_sems.at[a, k], recv_sem=recv_sems.at[a, k], device_id=sib, device_id_type=MESH)
                cp.start()
                sends.append(cp)
        for a, (g_ref, out_ref) in enumerate(zip(g_refs, out_refs)):
            H = g_ref.shape[1] // 2
            for k in range(4):
                pltpu.make_async_remote_copy(src_ref=g_ref.at[k, pl.ds(c * H, H)], dst_ref=out_ref.at[k], send_sem=send_sems.at[a, k],
                                             recv_sem=recv_sems.at[a, k], device_id=sib, device_id_type=MESH).wait_recv()
        for cp in sends:
            cp.wait_send()

    return pl.pallas_call(
        body, name=name, in_specs=[HBM] * na, out_specs=[HBM] * na,
        out_shape=[jax.ShapeDtypeStruct((4, g.shape[1] // 2, g.shape[2]), g.dtype) for g in gs],
        scratch_shapes=[pltpu.SemaphoreType.DMA((na, 4)), pltpu.SemaphoreType.DMA((na, 4))],
    )(*gs)


def scatter_chips(ps, *, name):
    na = len(ps)

    def body(*refs):
        p_refs, out_refs = refs[:na], refs[na:2 * na]
        send_sems, recv_sems, local_sem = refs[2 * na:2 * na + 3]
        bufs = refs[2 * na + 3:]
        x, y, c = _me()
        j = 2 * x + y
        chips = [(_flip(x, k & 2), _flip(y, k & 1)) for k in range(1, 4)]
        sends = []
        for a, (p_ref, out_ref) in enumerate(zip(p_refs, out_refs)):
            for k, (px, py) in enumerate(chips):
                cp = pltpu.make_async_remote_copy(src_ref=p_ref.at[2 * px + py], dst_ref=out_ref.at[j], send_sem=send_sems.at[a, k],
                                                  recv_sem=recv_sems.at[a, k], device_id=(px, py, c), device_id_type=MESH)
                cp.start()
                sends.append(cp)
        for p_ref, out_ref, buf in zip(p_refs, out_refs, bufs):
            _copy_via_vmem(p_ref.at[j], lambda o, n, out_ref=out_ref: out_ref.at[j, pl.ds(o, n)], p_ref.shape[1], buf, local_sem)
        for a, out_ref in enumerate(out_refs):
            for k, (px, py) in enumerate(chips):
                slot = out_ref.at[2 * px + py]
                pltpu.make_async_remote_copy(src_ref=slot, dst_ref=slot, send_sem=send_sems.at[a, k], recv_sem=recv_sems.at[a, k],
                                             device_id=(px, py, c), device_id_type=MESH).wait_recv()
        for cp in sends:
            cp.wait_send()

    return pl.pallas_call(
        body, name=name, in_specs=[HBM] * na, out_specs=[HBM] * na, out_shape=[jax.ShapeDtypeStruct(p.shape, p.dtype) for p in ps],
        scratch_shapes=[pltpu.SemaphoreType.DMA((na, 3)), pltpu.SemaphoreType.DMA((na, 3)), pltpu.SemaphoreType.DMA]
        + [_chunk_buf(p.shape[1], p.shape[2], p.dtype) for p in ps],
    )(*ps)


def join_halves(halves, *, name):
    na = len(halves)

    def body(*refs):
        h_refs, out_refs = refs[:na], refs[na:2 * na]
        send_sems, recv_sems, local_sem = refs[2 * na:2 * na + 3]
        bufs = refs[2 * na + 3:]
        x, y, c = _me()
        sib = (x, y, 1 - c)
        sends = []
        for a, (h_ref, out_ref) in enumerate(zip(h_refs, out_refs)):
            H = h_ref.shape[0]
            cp = pltpu.make_async_remote_copy(src_ref=h_ref, dst_ref=out_ref.at[pl.ds(c * H, H)], send_sem=send_sems.at[a],
                                              recv_sem=recv_sems.at[a], device_id=sib, device_id_type=MESH)
            cp.start()
            sends.append(cp)
        for h_ref, out_ref, buf in zip(h_refs, out_refs, bufs):
            H = h_ref.shape[0]
            _copy_via_vmem(h_ref, lambda o, n, out_ref=out_ref, H=H: out_ref.at[pl.ds(c * H + o, n)], H, buf, local_sem)
        for a, (h_ref, out_ref) in enumerate(zip(h_refs, out_refs)):
            H = h_ref.shape[0]
            pltpu.make_async_remote_copy(src_ref=h_ref, dst_ref=out_ref.at[pl.ds((1 - c) * H, H)], send_sem=send_sems.at[a],
                                         recv_sem=recv_sems.at[a], device_id=sib, device_id_type=MESH).wait_recv()
        for cp in sends:
            cp.wait_send()

    return pl.pallas_call(
        body, name=name, in_specs=[HBM] * na, out_specs=[HBM] * na,
        out_shape=[jax.ShapeDtypeStruct((2 * h.shape[0], h.shape[1]), h.dtype) for h in halves],
        scratch_shapes=[pltpu.SemaphoreType.DMA((na,)), pltpu.SemaphoreType.DMA((na,)), pltpu.SemaphoreType.DMA]
        + [_chunk_buf(h.shape[0], h.shape[1], h.dtype) for h in halves],
    )(*halves)


class _Plan:
    def __init__(self, ins, out_shapes, ncopies, copies, aliased=False):
        self.ins, self.out_shapes, self.ncopies, self.copies, self.aliased = list(ins), list(out_shapes), ncopies, copies, aliased

    def start(self, in_refs, out_refs, send_sems, recv_sems):
        sends, _ = self.copies(in_refs, out_refs, send_sems, recv_sems)
        for cp in sends:
            cp.start()

    def finish(self, in_refs, out_refs, send_sems, recv_sems):
        sends, recvs = self.copies(in_refs, out_refs, send_sems, recv_sems)
        for cp in recvs:
            cp.wait_recv()
        for cp in sends:
            cp.wait_send()


def _rcopy(src, dst, send_sems, recv_sems, idx, dev):
    return pltpu.make_async_remote_copy(src_ref=src, dst_ref=dst, send_sem=send_sems.at[idx], recv_sem=recv_sems.at[idx],
                                        device_id=dev, device_id_type=MESH)


def _other_chips(x, y):
    return [(_flip(x, k & 2), _flip(y, k & 1)) for k in range(1, 4)]


def plan_gather_ici(ws):
    def copies(in_refs, out_refs, ss, rs):
        x, y, c = _me()
        j = 2 * x + y
        sends, recvs = [], []
        for a, (x_ref, out_ref) in enumerate(zip(in_refs, out_refs)):
            H = x_ref.shape[0] // 2
            for k, (px, py) in enumerate(_other_chips(x, y)):
                sends.append(_rcopy(x_ref.at[pl.ds(c * H, H)], out_ref.at[j, pl.ds(c * H, H)], ss, rs, 3 * a + k, (px, py, c)))
                slot = out_ref.at[2 * px + py, pl.ds(c * H, H)]
                recvs.append(_rcopy(slot, slot, ss, rs, 3 * a + k, (px, py, c)))
        return sends, recvs

    return _Plan(ws, [jax.ShapeDtypeStruct((4,) + w.shape, w.dtype) for w in ws], 3 * len(ws), copies)


def plan_gather_d2d(w4s):
    def copies(in_refs, out_refs, ss, rs):
        x, y, c = _me()
        sends, recvs = [], []
        for a, out_ref in enumerate(out_refs):
            H = out_ref.shape[1] // 2
            for k, (px, py) in enumerate(_other_chips(x, y)):
                mine = out_ref.at[2 * px + py, pl.ds(c * H, H)]
                theirs = out_ref.at[2 * px + py, pl.ds((1 - c) * H, H)]
                sends.append(_rcopy(mine, mine, ss, rs, 3 * a + k, (x, y, 1 - c)))
                recvs.append(_rcopy(theirs, theirs, ss, rs, 3 * a + k, (x, y, 1 - c)))
        return sends, recvs

    return _Plan(w4s, [jax.ShapeDtypeStruct(w.shape, w.dtype) for w in w4s], 3 * len(w4s), copies, aliased=True)


def plan_swap_halves(gs):
    def copies(in_refs, out_refs, ss, rs):
        x, y, c = _me()
        sends, recvs = [], []
        for a, (g_ref, out_ref) in enumerate(zip(in_refs, out_refs)):
            H = g_ref.shape[1] // 2
            for k in range(4):
                sends.append(_rcopy(g_ref.at[k, pl.ds((1 - c) * H, H)], out_ref.at[k], ss, rs, 4 * a + k, (x, y, 1 - c)))
                recvs.append(_rcopy(g_ref.at[k, pl.ds(c * H, H)], out_ref.at[k], ss, rs, 4 * a + k, (x, y, 1 - c)))
        return sends, recvs

    return _Plan(gs, [jax.ShapeDtypeStruct((4, g.shape[1] // 2, g.shape[2]), g.dtype) for g in gs], 4 * len(gs), copies)


def plan_scatter_ici(ps):
    def copies(in_refs, out_refs, ss, rs):
        x, y, c = _me()
        j = 2 * x + y
        sends, recvs = [], []
        for a, (p_ref, out_ref) in enumerate(zip(in_refs, out_refs)):
            for k, (px, py) in enumerate(_other_chips(x, y)):
                sends.append(_rcopy(p_ref.at[2 * px + py], out_ref.at[j], ss, rs, 3 * a + k, (px, py, c)))
                slot = out_ref.at[2 * px + py]
                recvs.append(_rcopy(slot, slot, ss, rs, 3 * a + k, (px, py, c)))
        return sends, recvs

    return _Plan(ps, [jax.ShapeDtypeStruct(p.shape, p.dtype) for p in ps], 3 * len(ps), copies)


def plan_join_halves(fulls):
    def copies(in_refs, out_refs, ss, rs):
        x, y, c = _me()
        sends, recvs = [], []
        for a, out_ref in enumerate(out_refs):
            H = out_ref.shape[0] // 2
            mine, theirs = out_ref.at[pl.ds(c * H, H)], out_ref.at[pl.ds((1 - c) * H, H)]
            sends.append(_rcopy(mine, mine, ss, rs, a, (x, y, 1 - c)))
            recvs.append(_rcopy(theirs, theirs, ss, rs, a, (x, y, 1 - c)))
        return sends, recvs

    return _Plan(fulls, [jax.ShapeDtypeStruct(f.shape, f.dtype) for f in fulls], len(fulls), copies, aliased=True)


def call_with_plans(body, plans, *, grid, in_specs, out_specs, out_shape, scratch_shapes, args, sem, name):
    plans = list(plans or [])
    n_in, n_out, n_scr = len(in_specs), len(out_specs), len(scratch_shapes)
    c_in = [len(p.ins) for p in plans]
    c_out = [len(p.out_shapes) for p in plans]
    steps = math.prod(grid) if grid else 1

    def wrapped(*refs):
        pos = 0

        def take(n):
            nonlocal pos
            out = refs[pos:pos + n]
            pos += n
            return out

        ins = take(n_in)
        cins = [take(n) for n in c_in]
        outs = take(n_out)
        couts = [take(n) for n in c_out]
        scr = take(n_scr)
        sems = [take(2) for _ in plans]
        def start_all():
            for p, ci, co, (ss, rs) in zip(plans, cins, couts, sems):
                p.start(ci, co, ss, rs)

        def finish_all():
            for p, ci, co, (ss, rs) in zip(plans, cins, couts, sems):
                p.finish(ci, co, ss, rs)

        if plans and grid:
            idx = 0
            for ax, g in enumerate(grid):
                idx = idx * g + pl.program_id(ax)
            pl.when(idx == 0)(start_all)
        elif plans:
            start_all()
        if body is not None:
            body(*ins, *outs, *scr)
        if plans and grid:
            pl.when(idx == steps - 1)(finish_all)
        elif plans:
            finish_all()

    aliases = {}
    i_pos, o_pos = n_in, n_out
    for p, ni, no in zip(plans, c_in, c_out):
        if p.aliased:
            aliases.update({i_pos + t: o_pos + t for t in range(ni)})
        i_pos += ni
        o_pos += no
    kwargs = dict(grid=grid) if grid else {}
    if aliases:
        kwargs["input_output_aliases"] = aliases
    res = pl.pallas_call(
        wrapped, name=name, in_specs=list(in_specs) + [HBM] * sum(c_in), out_specs=list(out_specs) + [HBM] * sum(c_out),
        out_shape=list(out_shape) + [s for p in plans for s in p.out_shapes],
        scratch_shapes=list(scratch_shapes) + [pltpu.SemaphoreType.DMA((p.ncopies,)) for p in plans for _ in range(2)],
        compiler_params=_cp(*sem) if grid else pltpu.CompilerParams(vmem_limit_bytes=VMEM_LIMIT), **kwargs,
    )(*args, *[a for p in plans for a in p.ins])
    res = list(res)
    comp, rest = res[:n_out], res[n_out:]
    pouts = []
    for no in c_out:
        pouts.append(rest[:no])
        rest = rest[no:]
    return comp, pouts


def run_plans(plans, *, name):
    return call_with_plans(None, plans, grid=(), in_specs=[], out_specs=[], out_shape=[], scratch_shapes=[], args=[], sem=(), name=name)[1]


def _cat(parts, axis=-1):
    return jnp.concatenate(parts, axis=axis)


def _pairs_of_heads(a, axis, inverse=False):
    lead, tail = a.shape[:axis], a.shape[axis + 1:]
    split = (3, 2) if inverse else (2, 3)
    a = a.reshape(lead + split + (HEAD,) + tail)
    return jnp.swapaxes(a, axis, axis + 1).reshape(lead + (6 * HEAD,) + tail)


def _prep_w_in(w):
    z = lambda n: jnp.zeros((w.shape[0], n), w.dtype)
    return _cat([w[:, 0:1152], z(64), w[:, 1152:1184], z(32), _pairs_of_heads(w[:, 1184:1568], 1), w[:, 1568:1824]])


def _unprep_w_in(g):
    return _cat([g[:, 0:1152], g[:, 1216:1248], _pairs_of_heads(g[:, P_SWQ:P_SWK], 1, inverse=True), g[:, P_SWK:P_END]])


def _prep_w_uq(w):
    r = w.shape[0]
    return jnp.pad(w.reshape(r, 6, MLA_QK), ((0, 0), (0, 0), (0, LANES - MLA_QK))).reshape(r, 6 * LANES)


def _unprep_w_uq(g):
    r = g.shape[0]
    return g.reshape(r, 6, LANES)[:, :, :MLA_QK].reshape(r, 6 * MLA_QK)


def _prep_w_ukv(w):
    r = w.shape[0]
    w3 = w.reshape(r, 6, LANES)
    k = jnp.pad(w3[:, :, :HEAD], ((0, 0), (0, 0), (0, LANES - HEAD))).reshape(r, 6 * LANES)
    return _cat([k, w3[:, :, HEAD:].reshape(r, 6 * HEAD)])


def _unprep_w_ukv(g):
    r = g.shape[0]
    k = g[:, :6 * LANES].reshape(r, 6, LANES)[:, :, :HEAD]
    return _cat([k, g[:, 6 * LANES:].reshape(r, 6, HEAD)], axis=2).reshape(r, 6 * LANES)


def _prep_w_out(w):
    return _cat([w[0:640], _pairs_of_heads(w[640:], 0)], axis=0)


def _unprep_w_out(g):
    return _cat([g[0:640], _pairs_of_heads(g[640:], 0, inverse=True)], axis=0)


def _rope_tables(positions):
    half = 16
    inv_freq = jnp.power(ROPE_THETA, -jnp.arange(half, dtype=F32) / half)
    ang = positions.astype(F32)[..., None] * inv_freq
    cos, sin = jnp.cos(ang), jnp.sin(ang)
    z = lambda n: jnp.zeros(ang.shape[:-1] + (n,), F32)
    return (_cat([jnp.ones(ang.shape[:-1] + (HEAD,), F32), cos, cos, z(32)]), _cat([z(HEAD), -sin, z(16), z(32)]), _cat([z(HEAD), z(16), sin, z(32)]))


def _small_params(p):
    pad96 = lambda g: _cat([g, jnp.zeros((32,), F32)]).reshape(1, LANES)
    two = lambda g: _cat([g, g]).reshape(1, LANES)
    sinks = jnp.broadcast_to(p["sw_sinks"].reshape(2, 3).T[:, :, None], (3, 2, LANES))
    return dict(n1=p["norm1_g"].reshape(1, -1), n2=p["norm2_g"].reshape(1, -1), cq_g=p["mla_cq_g"].reshape(1, -1),
                ckv_g=p["mla_ckv_g"].reshape(1, -1), qn_g=pad96(p["mla_qn_g"]), kn_g=pad96(p["mla_kn_g"]),
                swq_g=two(p["sw_qn_g"]), swk_g=two(p["sw_kn_g"]), sinks=sinks, conv_b=_up_perm(p["conv_b"]).reshape(1, -1))


class _NoFlow:
    def plans(self, tag):
        return []

    def done(self, tag, outs):
        pass

    def add(self, key, g):
        pass


def _layer_fwd(x3, md, W, tabs, bias, tag, flow=_NoFlow()):
    Bl, S, D = x3.shape
    T = Bl * S
    n = lambda s: f"{s}_{tag}"
    two = lambda a: a.reshape(T, a.shape[-1])
    three = lambda a: a.reshape(Bl, S, a.shape[-1])
    h = rms_fwd(x3, 0, D, W["n1"], md["scale1"], md["shift1"], name=n("norm1"))
    proj = three(matmul(two(h), W["w_in"], tn=1920, name=n("in_proj")))
    (o_a, rt_a), got = sb_attn_fwd(proj, plans=flow.plans(n("sb_fwd")), name=n("sb_fwd"))
    flow.done(n("sb_fwd"), got)
    cqn = rms_fwd(proj, P_CQ // 256, 256, W["cq_g"], name=n("cq_norm"))
    ckvn = rms_fwd(proj, P_CKV // LANES, LANES, W["ckv_g"], name=n("ckv_norm"))
    qb = three(matmul(two(cqn), W["w_uq"], tm=1024, tn=768, name=n("uq")))
    kvb = three(matmul(two(ckvn), W["w_ukv"], tm=1024, tn=1152, name=n("ukv")))
    q_m = rope_norm_fwd(qb, 6, W["qn_g"], tabs, name=n("q_rope"))
    k_m = rope_norm_fwd(kvb, 6, W["kn_g"], tabs, (proj, P_SLAB // LANES), name=n("k_rope"))
    (o_b, lse_b), got = mla_attn_fwd(q_m, k_m, kvb, 6, plans=flow.plans(n("mla_fwd")), name=n("mla_fwd"))
    flow.done(n("mla_fwd"), got)
    q_c = pair_rms_fwd(proj, P_SWQ // LANES, 3, W["swq_g"], name=n("swq_norm"))
    k_c = pair_rms_fwd(proj, P_SWK // LANES, 1, W["swk_g"], name=n("swk_norm"))
    (o_c, lse_c), got = swa_attn_fwd(q_c, k_c, proj, bias, W["sinks"], plans=flow.plans(n("swa_fwd")), name=n("swa_fwd"))
    flow.done(n("swa_fwd"), got)
    mix = _cat([o_a, o_b, o_c]).astype(BF16)
    att, x1 = matmul_res(two(mix), W["w_out"], two(x3), md["gate1"], S, name=n("out_proj"))
    x1 = three(x1)
    h2 = rms_fwd(x1, 0, D, W["n2"], md["scale2"], md["shift2"], name=n("norm2"))
    up = three(matmul(two(h2), W["w_up"], tm=1024, tn=1408, name=n("up_proj")))
    a = conv_gate_fwd(up, W["conv_w"], W["conv_b"], name=n("conv_gate"))
    yd, x2 = matmul_res(two(a), W["w_down"], two(x1), md["gate2"], S, name=n("down_proj"))
    saved = dict(x=x3, h=h, proj=proj, rt_a=rt_a, cqn=cqn, ckvn=ckvn, qb=qb, kvb=kvb, q_m=q_m, k_m=k_m, o_b=o_b, lse_b=lse_b,
                 q_c=q_c, k_c=k_c, o_c=o_c, lse_c=lse_c, mix=mix, att=three(att), x1=x1, h2=h2, up=up, a=a, yd=three(yd))
    return three(x2), saved


def _layer_bwd(dx2, sv, md, W, tabs, bias, tag, flow=_NoFlow()):
    Bl, S, D = dx2.shape
    T = Bl * S
    n = lambda s: f"{s}_{tag}"
    two = lambda a: a.reshape(T, a.shape[-1])
    three = lambda a: a.reshape(Bl, S, a.shape[-1])
    g = {}
    dyb, dgate2 = gate_bwd(dx2, sv["yd"], md["gate2"], name=n("gate2_bwd"))
    da = three(matmul(two(dyb), W["w_down"], tb=True, tm=1024, tn=1408, name=n("down_dx")))
    g["w_down"] = matmul(two(sv["a"]), two(dyb), ta=True, tm=256, tn=1024, name=n("down_dw"))
    dup, dcw = conv_gate_bwd(sv["up"], W["conv_w"], W["conv_b"], da, name=n("conv_gate_bwd"))
    dh2 = three(matmul(two(dup), W["w_up"], tb=True, tn=1024, name=n("up_dx")))
    g["w_up"] = matmul(two(sv["h2"]), two(dup), ta=True, tn=1408, name=n("up_dw"))
    dx1, dn2, dsc2, dsh2 = rms_bwd(sv["x1"], 0, D, dh2, W["n2"], md["scale2"], dx2, name=n("norm2_bwd"))
    dmo, dgate1 = gate_bwd(dx1, sv["att"], md["gate1"], name=n("gate1_bwd"))
    dmix = three(matmul(two(dmo), W["w_out"], tb=True, tn=1024, out_dtype=BF16, name=n("out_dx")))
    g["w_out"] = matmul(two(sv["mix"]), two(dmo), ta=True, tn=1024, name=n("out_dw"))
    proj = sv["proj"]
    for k in ("w_down", "w_up", "w_out"):
        flow.add((tag, k), g[k])
    (dq_a, dk_a, dv_a), got = sb_attn_bwd(proj, sv["rt_a"], dmix[:, :, 0:256], plans=flow.plans(n("sb_bwd")), name=n("sb_bwd"))
    flow.done(n("sb_bwd"), got)
    dq_m, dk_m, dv_b = mla_attn_bwd(sv["q_m"], sv["k_m"], sv["kvb"], 6, sv["o_b"], sv["lse_b"], dmix[:, :, 256:640], name=n("mla_bwd"))
    dqb, dqn = rope_norm_bwd(sv["qb"], 6, dq_m, W["qn_g"], tabs, name=n("q_rope_bwd"))
    dkn_x, dkn, dslab = rope_norm_bwd(sv["kvb"], 6, dk_m, W["kn_g"], tabs, (proj, P_SLAB // LANES), name=n("k_rope_bwd"))
    dkvb = _cat([dkn_x, dv_b]).astype(BF16)
    dckvn = three(matmul(two(dkvb), W["w_ukv"], tb=True, tm=1024, name=n("ukv_dx")))
    g["w_ukv"] = matmul(two(sv["ckvn"]), two(dkvb), ta=True, tn=1152, name=n("ukv_dw"))
    dcqn = three(matmul(two(dqb), W["w_uq"], tb=True, tm=1024, name=n("uq_dx")))
    g["w_uq"] = matmul(two(sv["cqn"]), two(dqb), ta=True, tn=768, name=n("uq_dw"))
    dcq, dcq_g = rms_bwd(proj, P_CQ // 256, 256, dcqn, W["cq_g"], name=n("cq_norm_bwd"))
    dckv, dckv_g = rms_bwd(proj, P_CKV // LANES, LANES, dckvn, W["ckv_g"], name=n("ckv_norm_bwd"))
    dq_c, dk_c, dv_c, dbias, dsink = swa_attn_bwd(sv["q_c"], sv["k_c"], proj, bias, W["sinks"], sv["o_c"], sv["lse_c"], dmix[:, :, 640:1024], name=n("swa_bwd"))
    dswq, dswq_g = pair_rms_bwd(proj, P_SWQ // LANES, 3, dq_c, W["swq_g"], name=n("swq_norm_bwd"))
    dswk, dswk_g = pair_rms_bwd(proj, P_SWK // LANES, 1, dk_c, W["swk_g"], name=n("swk_norm_bwd"))
    dproj = _cat([dq_a, dk_a, dv_a, dcq, dckv, dslab, dswq, dswk, dv_c]).astype(BF16)
    dh = three(matmul(two(dproj), W["w_in"], tb=True, tn=1024, name=n("in_dx")))
    g["w_in"] = matmul(two(sv["h"]), two(dproj), ta=True, tn=1920, tk=2048, name=n("in_dw"))
    dx, dn1, dsc1, dsh1 = rms_bwd(sv["x"], 0, D, dh, W["n1"], md["scale1"], dx1, name=n("norm1_bwd"))
    small = dict(n1=dn1, n2=dn2, cq_g=dcq_g, ckv_g=dckv_g, qn_g=dqn, kn_g=dkn, swq_g=dswq_g, swk_g=dswk_g, conv=dcw)
    dmods = _cat([dsh1, dsc1, dgate1, dsh2, dsc2, dgate2]).reshape(Bl, 6 * D)
    for k in ("w_ukv", "w_uq", "w_in"):
        flow.add((tag, k), g[k])
    return dx, g, small, dmods, dbias, dsink


BIG = ("w_in", "w_uq", "w_ukv", "w_out", "w_up", "w_down")
ROW_SHARDED = ("w_out", "w_down")
PREP = dict(w_in=_prep_w_in, w_uq=_prep_w_uq, w_ukv=_prep_w_ukv, w_out=_prep_w_out, w_up=_up_perm, w_down=lambda w: w)
UNPREP = dict(w_in=_unprep_w_in, w_uq=_unprep_w_uq, w_ukv=_unprep_w_ukv, w_out=_unprep_w_out, w_up=_up_perm, w_down=lambda w: w)
NCHIPS = 4


def _local_step(x, target, positions, mods, Wl, rel_flat, fwd_flow=_NoFlow(), bwd_flow=_NoFlow()):
    Bl, S, D = x.shape
    L = len(Wl)
    tabs = _rope_tables(positions)
    bucket = _bucket_table()
    bias = swa_bias(rel_flat, bucket, name="swa_bias")
    mds = []
    for l in range(L):
        parts = [mods[l, :, D * k:D * (k + 1)].reshape(Bl, 1, D) for k in range(6)]
        mds.append(dict(zip(("shift1", "scale1", "gate1", "shift2", "scale2", "gate2"), parts)))
    saved = []
    h = x
    for l in range(L):
        h, sv = _layer_fwd(h, mds[l], Wl[l], tabs, bias, f"l{l}", fwd_flow)
        saved.append(sv)
    dy, loss = loss_grad(h, target, name="loss")
    grads, smalls, dmods, dbiases, dsinks = [None] * L, [None] * L, [None] * L, [None] * L, [None] * L
    for l in reversed(range(L)):
        dy, grads[l], smalls[l], dmods[l], dbiases[l], dsinks[l] = _layer_bwd(dy, saved[l], mds[l], Wl[l], tabs, bias, f"l{l}", bwd_flow)
    drel = swa_bias_bwd(_cat(dbiases, axis=0), bucket, name="swa_bias_bwd")
    return loss, dy, grads, smalls, dmods, dsinks, drel


ATT = ("w_in", "w_uq", "w_ukv", "w_out")
FFN = ("w_up", "w_down")
GATHER_STAGES = {
    "sb_fwd_l0": ([("l0", k) for k in ("w_out",) + FFN], []),
    "mla_fwd_l0": ([("l1", k) for k in ATT + ("w_up",)], [("l0", k) for k in ("w_out",) + FFN]),
    "swa_fwd_l0": ([("l1", "w_down")], [("l1", k) for k in ATT + ("w_up",)]),
    "sb_fwd_l1": ([], [("l1", "w_down")]),
}
SCATTER_STAGES = {
    "sb_bwd_l1": [("l1", k) for k in FFN],
    "sb_bwd_l0": [("l1", k) for k in ATT] + [("l0", k) for k in FFN + ("w_out",)],
}


class _GatherFlow:
    def __init__(self, shards, chip):
        self.shards, self.chip, self.ici, self.d2d, self.pending = shards, chip, {}, {}, {}

    def early(self, keys):
        ici, = run_plans([plan_gather_ici([self.shards[k] for k in keys])], name="gather_early_ici")
        d2d, = run_plans([plan_gather_d2d(ici)], name="gather_early_d2d")
        self.d2d.update(zip(keys, d2d))

    def plans(self, tag):
        ici_keys, d2d_keys = GATHER_STAGES.get(tag, ([], []))
        plans = []
        if d2d_keys:
            plans.append(plan_gather_d2d([self.ici[k] for k in d2d_keys]))
        if ici_keys:
            plans.append(plan_gather_ici([self.shards[k] for k in ici_keys]))
        self.pending[tag] = (ici_keys, d2d_keys)
        return plans

    def done(self, tag, outs):
        ici_keys, d2d_keys = self.pending.pop(tag, ([], []))
        outs = list(outs)
        if d2d_keys:
            self.d2d.update(zip(d2d_keys, outs.pop(0)))
        if ici_keys:
            self.ici.update(zip(ici_keys, outs.pop(0)))

    def weight(self, key):
        k = key[1]
        own = self.shards[key]
        r, cc = own.shape
        w4 = lax.dynamic_update_slice(self.d2d[key], own[None], (self.chip, 0, 0))
        fw = w4.reshape(NCHIPS * r, cc) if k in ROW_SHARDED else jnp.transpose(w4, (1, 0, 2)).reshape(r, NCHIPS * cc)
        return PREP[k](fw)


class _LayerWeights(dict):
    def __init__(self, small, flow, tag):
        super().__init__(small)
        self.flow, self.tag = flow, tag

    def __missing__(self, k):
        self[k] = self.flow.weight((self.tag, k))
        return self[k]


class _ScatterFlow:
    def __init__(self, shapes, sel, c_arr):
        self.shapes, self.sel, self.c_arr = shapes, sel, c_arr
        self.g, self.pairs, self.landed, self.pending = {}, {}, {}, {}

    def add(self, key, g):
        self.g[key] = g

    def _pairs(self, keys, label):
        g4s = []
        for key in keys:
            k = key[1]
            r, cc = self.shapes[k]
            gk = UNPREP[k](self.g[key])
            g4 = gk.reshape(NCHIPS, r, cc) if k in ROW_SHARDED else jnp.transpose(gk.reshape(r, NCHIPS, cc), (1, 0, 2))
            g4s.append(g4.astype(BF16))
        theirs, = run_plans([plan_swap_halves(g4s)], name=f"rs_swap_{label}")
        pairs = [pair_add_half(g4, th, self.c_arr, name=f"rs_pair_add_{key[1]}_{key[0]}") for key, g4, th in zip(keys, g4s, theirs)]
        self.pairs.update(zip(keys, pairs))
        return pairs

    def plans(self, tag):
        keys = SCATTER_STAGES.get(tag, [])
        self.pending[tag] = keys
        return [plan_scatter_ici(self._pairs(keys, tag))] if keys else []

    def done(self, tag, outs):
        keys = self.pending.pop(tag, [])
        if keys:
            self.landed.update(zip(keys, outs[0]))

    def finish(self):
        rest = [key for key in self.g if key not in self.pairs]
        if rest:
            landed, = run_plans([plan_scatter_ici(self._pairs(rest, "rest"))], name="rs_scatter_rest")
            self.landed.update(zip(rest, landed))
        keys = list(self.pairs)
        fulls = [chip_sum_into(self.landed[key], self.pairs[key], self.sel, name=f"rs_chip_sum_{key[1]}_{key[0]}") for key in keys]
        joined, = run_plans([plan_join_halves(fulls)], name="rs_join_halves")
        return dict(zip(keys, joined))


WEIGHTS = ("rel_table", "norm1_g", "norm2_g", "w_ada", "b_ada", "w_in", "mla_cq_g", "w_uq", "mla_ckv_g", "w_ukv", "mla_qn_g", "mla_kn_g",
           "sw_qn_g", "sw_kn_g", "sw_sinks", "w_out", "w_up", "conv_w", "conv_b", "w_down")
SMALL = tuple(n for n in WEIGHTS if n not in BIG + ("w_ada",))


def kernel(x, c, positions, rel_table, norm1_g, norm2_g, w_ada, b_ada, w_in, mla_cq_g, w_uq, mla_ckv_g, w_ukv, mla_qn_g, mla_kn_g, sw_qn_g, sw_kn_g, sw_sinks, w_out, w_up, conv_w, conv_b, w_down, loss_target, m_rel_table, m_norm1_g, m_norm2_g, m_w_ada, m_b_ada, m_w_in, m_mla_cq_g, m_w_uq, m_mla_ckv_g, m_w_ukv, m_mla_qn_g, m_mla_kn_g, m_sw_qn_g, m_sw_kn_g, m_sw_sinks, m_w_out, m_w_up, m_conv_w, m_conv_b, m_w_down, v_rel_table, v_norm1_g, v_norm2_g, v_w_ada, v_b_ada, v_w_in, v_mla_cq_g, v_w_uq, v_mla_ckv_g, v_w_ukv, v_mla_qn_g, v_mla_kn_g, v_sw_qn_g, v_sw_kn_g, v_sw_sinks, v_w_out, v_w_up, v_conv_w, v_conv_b, v_w_down):
    w = dict(rel_table=rel_table, norm1_g=norm1_g, norm2_g=norm2_g, w_ada=w_ada, b_ada=b_ada, w_in=w_in, mla_cq_g=mla_cq_g, w_uq=w_uq,
             mla_ckv_g=mla_ckv_g, w_ukv=w_ukv, mla_qn_g=mla_qn_g, mla_kn_g=mla_kn_g, sw_qn_g=sw_qn_g, sw_kn_g=sw_kn_g, sw_sinks=sw_sinks,
             w_out=w_out, w_up=w_up, conv_w=conv_w, conv_b=conv_b, w_down=w_down)
    m = dict(rel_table=m_rel_table, norm1_g=m_norm1_g, norm2_g=m_norm2_g, w_ada=m_w_ada, b_ada=m_b_ada, w_in=m_w_in, mla_cq_g=m_mla_cq_g,
             w_uq=m_w_uq, mla_ckv_g=m_mla_ckv_g, w_ukv=m_w_ukv, mla_qn_g=m_mla_qn_g, mla_kn_g=m_mla_kn_g, sw_qn_g=m_sw_qn_g,
             sw_kn_g=m_sw_kn_g, sw_sinks=m_sw_sinks, w_out=m_w_out, w_up=m_w_up, conv_w=m_conv_w, conv_b=m_conv_b, w_down=m_w_down)
    v = dict(rel_table=v_rel_table, norm1_g=v_norm1_g, norm2_g=v_norm2_g, w_ada=v_w_ada, b_ada=v_b_ada, w_in=v_w_in, mla_cq_g=v_mla_cq_g,
             w_uq=v_w_uq, mla_ckv_g=v_mla_ckv_g, w_ukv=v_w_ukv, mla_qn_g=v_mla_qn_g, mla_kn_g=v_mla_kn_g, sw_qn_g=v_sw_qn_g,
             sw_kn_g=v_sw_kn_g, sw_sinks=v_sw_sinks, w_out=v_w_out, w_up=v_w_up, conv_w=v_conv_w, conv_b=v_conv_b, w_down=v_w_down)
    Bl, S, D = x.shape
    L = norm1_g.shape[0]
    xi, yi, ci = _me()
    chip = 2 * xi + yi
    dev = 4 * xi + 2 * yi + ci
    ndev = 2 * NCHIPS

    shapes = {k: w[k].shape[1:] for k in BIG}
    shards = {(f"l{l}", k): w[k][l].astype(BF16) for l in range(L) for k in BIG}
    gflow = _GatherFlow(shards, chip)
    gflow.early([("l0", k) for k in ("w_in", "w_uq", "w_ukv")])

    cw_cols = conv_w.shape[2]
    c_got, cw_got = allgather8([c, conv_w.reshape(L * 3, cw_cols)], name="gather_cond")
    c_all = c_got.reshape(ndev * Bl, D)
    conv_full = jnp.transpose(cw_got[0::2].reshape(NCHIPS, L, 3, cw_cols), (1, 2, 0, 3)).reshape(L, 3, NCHIPS * cw_cols)
    E = w_ada.shape[2]
    b_cols = lax.dynamic_slice(b_ada, (0, chip * E), (L, E)).reshape(L, 1, E)
    mods_cols = mods_matmul(c_all, w_ada, b_cols, name="mods")
    mods_all, = allgather8([mods_cols.reshape(L * ndev * Bl, E)], name="gather_mods")
    mods_all = jnp.transpose(mods_all[0::2].reshape(NCHIPS, L, ndev * Bl, E), (1, 2, 0, 3)).reshape(L, ndev * Bl, NCHIPS * E)
    mods = lax.dynamic_slice(mods_all, (0, dev * Bl, 0), (L, Bl, NCHIPS * E))

    Wl = []
    for l in range(L):
        Wd = _small_params({k: w[k][l] for k in SMALL if k not in ("rel_table", "b_ada", "conv_w")})
        Wd["conv_w"] = _up_perm(conv_full[l])
        Wl.append(_LayerWeights(Wd, gflow, f"l{l}"))

    sflow = _ScatterFlow(shapes, jnp.stack([chip, ci]).astype(jnp.int32), ci.reshape(1).astype(jnp.int32))
    loss, dx, _, smalls, dmods, dsinks, drel = _local_step(x, loss_target, positions, mods, Wl, rel_table.reshape(-1), gflow, sflow)
    reduced = sflow.finish()
    grad = {k: jnp.stack([reduced[(f"l{l}", k)] for l in range(L)]) for k in BIG}

    vec_names = ("n1", "n2", "cq_g", "ckv_g", "qn_g", "kn_g", "swq_g", "swk_g")
    vecs = _cat([_cat([smalls[l][k] for k in vec_names], axis=1) for l in range(L)], axis=0)
    convs = _cat([smalls[l]["conv"][0:4] for l in range(L)], axis=0)
    dm = jnp.stack(dmods, axis=1).reshape(Bl * L, 6 * D)
    dsk = jnp.stack(dsinks, axis=1).reshape(Bl * L * 6, LANES)
    got = allgather8([vecs, convs, drel, loss, dm, dsk], name="gather_small_grads")
    seq = lambda a, rows: a.reshape(ndev * Bl, rows, a.shape[-1])
    vec_s, conv_s, rel_s, loss_s, dm_s, dsk_s = sum_small(list(got[:4]) + [seq(got[4], L), seq(got[5], L * 6)], name="sum_small_grads")
    dm_all = jnp.transpose(seq(got[4], L), (1, 0, 2))
    grad["w_ada"] = ada_grad(c_all, lax.dynamic_slice(dm_all, (0, 0, chip * E), (L, ndev * Bl, E)), name="ada_grad")
    grad["b_ada"] = dm_s
    grad["sw_sinks"] = jnp.transpose(dsk_s.reshape(L, 3, 2, LANES)[:, :, :, 0], (0, 2, 1)).reshape(L, 6)
    grad["rel_table"] = rel_s[:6, :REL_BUCKETS].T
    off = 0
    for k, name_, keep in zip(vec_names, ("norm1_g", "norm2_g", "mla_cq_g", "mla_ckv_g", "mla_qn_g", "mla_kn_g", "sw_qn_g", "sw_kn_g"),
                              (D, D, 256, LANES, MLA_QK, MLA_QK, HEAD, HEAD)):
        grad[name_] = vec_s[:, off:off + keep]
        off += smalls[0][k].shape[1]
    conv = _up_perm(conv_s.reshape(L, 4, 2 * D_FF))
    grad["conv_w"] = lax.dynamic_slice(conv[:, 0:3], (0, 0, chip * cw_cols), (L, 3, cw_cols))
    grad["conv_b"] = conv[:, 3]
    loss_out = loss_s[0, 0]

    delta, new_m, new_v = {}, {}, {}
    for k in BIG + ("w_ada",):
        delta[k], new_m[k], new_v[k] = adamw(w[k], grad[k], m[k], v[k], name=f"adamw_{k}")
    outs = adamw_small(*[[src[k] for k in SMALL] for src in (w, grad, m, v)], name="adamw_small")
    for dst, o in zip((delta, new_m, new_v), outs):
        dst.update(dict(zip(SMALL, o)))
    return (loss_out, dx, *[grad[k] for k in WEIGHTS], *[delta[k] for k in WEIGHTS], *[new_m[k] for k in WEIGHTS], *[new_v[k] for k in WEIGHTS])
```

```python
import math

import jax
import jax.numpy as jnp
from jax import lax
from jax.experimental import pallas as pl
from jax.experimental.pallas import tpu as pltpu

F32 = jnp.float32
BF16 = jnp.bfloat16
MESH = pl.DeviceIdType.MESH

EPS = 1e-6
NEG = -1e30
HEAD = 64
LANES = 128
MLA_QK = 96
ROPE_THETA = 10000.0
REL_BUCKETS = 32
REL_MAX_DIST = 128
WINDOW = 128
D_FF = 2816
ADAM_LR, ADAM_B1, ADAM_B2, ADAM_EPS, ADAM_WD, ADAM_STEP = 0.001, 0.9, 0.999, 1e-08, 0.01, 10

VMEM_LIMIT = 56 * 1024 * 1024
STRIP = 32

P_SBQ, P_SBK, P_SBV, P_CQ, P_CKV, P_SLAB, P_SWQ, P_SWK, P_SWV, P_END = 0, 256, 512, 768, 1024, 1152, 1280, 1664, 1792, 1920


def _cp(*sem):
    return pltpu.CompilerParams(dimension_semantics=sem, vmem_limit_bytes=VMEM_LIMIT)


def _dot(a, b):
    return jnp.dot(a, b, preferred_element_type=F32)


def _dot_nt(a, b):
    return lax.dot_general(a, b, (((1,), (1,)), ((), ())), preferred_element_type=F32)


def _dot_tn(a, b):
    return lax.dot_general(a, b, (((0,), (0,)), ((), ())), preferred_element_type=F32)


def _lane_masks():
    lane = lax.broadcasted_iota(jnp.int32, (1, LANES), 1)
    return (lane < HEAD, lane >= HEAD)


def _tile(n, cap, align=128):
    if n <= cap:
        return n
    t = cap - cap % align
    while t >= align:
        if n % t == 0:
            return t
        t -= align
    return n


def matmul(a, b, *, ta=False, tb=False, out_dtype=F32, tm=512, tn=512, tk=8192, name):
    M, K = (a.shape[1], a.shape[0]) if ta else a.shape
    N = b.shape[0] if tb else b.shape[1]
    tm, tn, tk = _tile(M, tm), _tile(N, tn), _tile(K, tk)
    nk = K // tk

    def body(a_ref, b_ref, o_ref, *scratch):
        av = a_ref[...].astype(BF16)
        bv = b_ref[...].astype(BF16)
        if ta:
            part = _dot_tn(av, bv)
        elif tb:
            part = _dot_nt(av, bv)
        else:
            part = _dot(av, bv)
        if nk == 1:
            o_ref[...] = part.astype(out_dtype)
        else:
            acc_ref, = scratch
            k = pl.program_id(2)

            @pl.when(k == 0)
            def _():
                acc_ref[...] = part

            @pl.when(k > 0)
            def _():
                acc_ref[...] += part

            @pl.when(k == nk - 1)
            def _():
                o_ref[...] = acc_ref[...].astype(out_dtype)

    n_outer = nk == 1 and tn * b.dtype.itemsize > tm * a.dtype.itemsize
    ij = (lambda p, q: (q, p)) if n_outer else (lambda p, q: (p, q))
    a_map = (lambda p, q, k: (k, ij(p, q)[0])) if ta else (lambda p, q, k: (ij(p, q)[0], k))
    b_map = (lambda p, q, k: (ij(p, q)[1], k)) if tb else (lambda p, q, k: (k, ij(p, q)[1]))
    grid = (N // tn, M // tm, nk) if n_outer else (M // tm, N // tn, nk)
    return pl.pallas_call(
        body, name=name, grid=grid,
        in_specs=[pl.BlockSpec((tk, tm) if ta else (tm, tk), a_map), pl.BlockSpec((tn, tk) if tb else (tk, tn), b_map)],
        out_specs=pl.BlockSpec((tm, tn), lambda p, q, k: ij(p, q)),
        out_shape=jax.ShapeDtypeStruct((M, N), out_dtype),
        scratch_shapes=[] if nk == 1 else [pltpu.VMEM((tm, tn), F32)],
        compiler_params=_cp("parallel", "parallel", "arbitrary"),
    )(a, b)


def matmul_res(a, b, res, gate, seq, *, tm=512, tn=1024, name):
    M, K = a.shape
    N = b.shape[1]
    tm, tn = _tile(min(M, seq), tm), _tile(N, tn)
    per_seq = seq // tm

    def body(a_ref, b_ref, r_ref, g_ref, y_ref, x_ref):
        y = _dot(a_ref[...].astype(BF16), b_ref[...].astype(BF16))
        y_ref[...] = y
        x_ref[...] = r_ref[...] + g_ref[...] * y

    out = jax.ShapeDtypeStruct((M, N), F32)
    return pl.pallas_call(
        body, name=name, grid=(M // tm, N // tn),
        in_specs=[pl.BlockSpec((tm, K), lambda i, j: (i, 0)), pl.BlockSpec((K, tn), lambda i, j: (0, j)),
                  pl.BlockSpec((tm, tn), lambda i, j: (i, j)), pl.BlockSpec((None, 1, tn), lambda i, j: (lax.div(i, jnp.int32(per_seq)), 0, j))],
        out_specs=[pl.BlockSpec((tm, tn), lambda i, j: (i, j))] * 2,
        out_shape=[out, out], compiler_params=_cp("parallel", "parallel"),
    )(a, b, res, gate)


def rms_fwd(x3, blk, W, g, sc=None, sh=None, *, tm=512, name):
    Bl, S, _ = x3.shape
    tm = min(tm, S)
    mod = sc is not None

    def body(x_ref, g_ref, *rest):
        o_ref = rest[-1]
        x = x_ref[...]
        r = lax.rsqrt(jnp.mean(x * x, axis=-1, keepdims=True) + EPS)
        y = x * r * g_ref[...]
        if mod:
            y = y * (1.0 + rest[0][...]) + rest[1][...]
        o_ref[...] = y.astype(BF16)

    vec = pl.BlockSpec((None, 1, W), lambda b, s: (b, 0, 0))
    return pl.pallas_call(
        body, name=name, grid=(Bl, S // tm),
        in_specs=[pl.BlockSpec((None, tm, W), lambda b, s: (b, s, blk)), pl.BlockSpec((1, W), lambda b, s: (0, 0))] + ([vec, vec] if mod else []),
        out_specs=pl.BlockSpec((None, tm, W), lambda b, s: (b, s, 0)),
        out_shape=jax.ShapeDtypeStruct((Bl, S, W), BF16),
        compiler_params=_cp("parallel", "parallel"),
    )(x3, g, *([sc, sh] if mod else []))


def rms_bwd(x3, blk, W, dy3, g, sc=None, dres3=None, *, tm=256, name):
    Bl, S, _ = x3.shape
    tm = min(tm, S)
    mod = sc is not None
    res = dres3 is not None

    def body(*refs):
        x_ref, dy_ref, g_ref = refs[:3]
        k = 3
        sc_ref = dr_ref = None
        if mod:
            sc_ref = refs[k]
            k += 1
        if res:
            dr_ref = refs[k]
            k += 1
        dx_ref, dg_ref = refs[k], refs[k + 1]
        b, s = pl.program_id(0), pl.program_id(1)
        x = x_ref[...]
        dy = dy_ref[...].astype(F32)
        g = g_ref[...]
        r = lax.rsqrt(jnp.mean(x * x, axis=-1, keepdims=True) + EPS)
        n = x * r
        if mod:
            dsc_ref, dsh_ref = refs[k + 2], refs[k + 3]
            one_sc = 1.0 + sc_ref[...]

            @pl.when(s == 0)
            def _():
                dsc_ref[...] = jnp.zeros_like(dsc_ref)
                dsh_ref[...] = jnp.zeros_like(dsh_ref)

            dsh_ref[...] += jnp.sum(dy, axis=0, keepdims=True)
            dsc_ref[...] += jnp.sum(dy * n * g, axis=0, keepdims=True)
            dyn = dy * one_sc
        else:
            dyn = dy

        @pl.when((b == 0) & (s == 0))
        def _():
            dg_ref[...] = jnp.zeros_like(dg_ref)

        dg_ref[...] += jnp.sum(dyn * n, axis=0, keepdims=True)
        dn = dyn * g
        dx = r * (dn - n * jnp.mean(dn * n, axis=-1, keepdims=True))
        if res:
            dx = dx + dr_ref[...]
        dx_ref[...] = dx

    blkspec = pl.BlockSpec((None, tm, W), lambda b, s: (b, s, 0))
    vec = pl.BlockSpec((None, 1, W), lambda b, s: (b, 0, 0))
    row = pl.BlockSpec((1, W), lambda b, s: (0, 0))
    in_specs = [pl.BlockSpec((None, tm, W), lambda b, s: (b, s, blk)), blkspec, row] + ([vec] if mod else []) + ([blkspec] if res else [])
    out_specs = [blkspec, row] + ([vec, vec] if mod else [])
    out_shape = [jax.ShapeDtypeStruct((Bl, S, W), F32), jax.ShapeDtypeStruct((1, W), F32)]
    if mod:
        out_shape += [jax.ShapeDtypeStruct((Bl, 1, W), F32)] * 2
    args = [x3, dy3, g] + ([sc] if mod else []) + ([dres3] if res else [])
    return pl.pallas_call(
        body, name=name, grid=(Bl, S // tm), in_specs=in_specs, out_specs=out_specs, out_shape=out_shape,
        compiler_params=_cp("arbitrary", "arbitrary"),
    )(*args)


def pair_rms_fwd(x3, blk0, npairs, g2, *, tm=1024, name):
    Bl, S, _ = x3.shape
    tm = min(tm, S)

    def body(x_ref, g_ref, o_ref):
        lo, hi = _lane_masks()
        x = x_ref[...]
        xx = x * x
        s0 = jnp.sum(jnp.where(lo, xx, 0.0), axis=-1, keepdims=True)
        s1 = jnp.sum(jnp.where(hi, xx, 0.0), axis=-1, keepdims=True)
        r = jnp.where(lo, lax.rsqrt(s0 / HEAD + EPS), lax.rsqrt(s1 / HEAD + EPS))
        o_ref[...] = (x * r * g_ref[...]).astype(BF16)

    return pl.pallas_call(
        body, name=name, grid=(Bl, S // tm, npairs),
        in_specs=[pl.BlockSpec((None, tm, LANES), lambda b, s, p: (b, s, blk0 + p)), pl.BlockSpec((1, LANES), lambda b, s, p: (0, 0))],
        out_specs=pl.BlockSpec((None, tm, LANES), lambda b, s, p: (b, s, p)),
        out_shape=jax.ShapeDtypeStruct((Bl, S, LANES * npairs), BF16),
        compiler_params=_cp("parallel", "parallel", "parallel"),
    )(x3, g2)


def pair_rms_bwd(x3, blk0, npairs, dy3, g2, *, tm=1024, name):
    Bl, S, _ = x3.shape
    tm = min(tm, S)

    def body(x_ref, dy_ref, g_ref, dx_ref, dg_ref):
        lo, hi = _lane_masks()
        first = (pl.program_id(0) == 0) & (pl.program_id(1) == 0) & (pl.program_id(2) == 0)
        x = x_ref[...]
        dy = dy_ref[...]
        xx = x * x
        s0 = jnp.sum(jnp.where(lo, xx, 0.0), axis=-1, keepdims=True)
        s1 = jnp.sum(jnp.where(hi, xx, 0.0), axis=-1, keepdims=True)
        r = jnp.where(lo, lax.rsqrt(s0 / HEAD + EPS), lax.rsqrt(s1 / HEAD + EPS))
        n = x * r

        @pl.when(first)
        def _():
            dg_ref[...] = jnp.zeros_like(dg_ref)

        part = jnp.sum(dy * n, axis=0, keepdims=True)
        dg_ref[...] += part + pltpu.roll(part, HEAD, 1)
        dn = dy * g_ref[...]
        t = dn * n
        m0 = jnp.sum(jnp.where(lo, t, 0.0), axis=-1, keepdims=True)
        m1 = jnp.sum(jnp.where(hi, t, 0.0), axis=-1, keepdims=True)
        dx_ref[...] = r * (dn - n * (jnp.where(lo, m0, m1) / HEAD))

    return pl.pallas_call(
        body, name=name, grid=(Bl, S // tm, npairs),
        in_specs=[pl.BlockSpec((None, tm, LANES), lambda b, s, p: (b, s, blk0 + p)), pl.BlockSpec((None, tm, LANES), lambda b, s, p: (b, s, p)),
                  pl.BlockSpec((1, LANES), lambda b, s, p: (0, 0))],
        out_specs=[pl.BlockSpec((None, tm, LANES), lambda b, s, p: (b, s, p)), pl.BlockSpec((1, LANES), lambda b, s, p: (0, 0))],
        out_shape=[jax.ShapeDtypeStruct((Bl, S, LANES * npairs), F32), jax.ShapeDtypeStruct((1, LANES), F32)],
        compiler_params=_cp("arbitrary", "arbitrary", "arbitrary"),
    )(x3, dy3, g2)


def _rot(y, cos_t, sin_a, sin_b):
    return y * cos_t + pltpu.roll(y, LANES - 16, 1) * sin_a + pltpu.roll(y, 16, 1) * sin_b


def _rot_t(d, cos_t, sin_a, sin_b):
    return d * cos_t + pltpu.roll(d * sin_a, 16, 1) + pltpu.roll(d * sin_b, LANES - 16, 1)


def rope_norm_fwd(x3, nheads, g, tabs, slab=None, *, tm=1024, name):
    Bl, S, _ = x3.shape
    tm = min(tm, S)
    has_slab = slab is not None

    def body(*refs):
        x_ref, g_ref, c_ref, sa_ref, sb_ref = refs[:5]
        o_ref = refs[-1]
        x = x_ref[...]
        if has_slab:
            x = x + refs[5][...]
        r = lax.rsqrt(jnp.sum(x * x, axis=-1, keepdims=True) / MLA_QK + EPS)
        o_ref[...] = _rot(x * r * g_ref[...], c_ref[...], sa_ref[...], sb_ref[...]).astype(BF16)

    head = pl.BlockSpec((None, tm, LANES), lambda b, s, h: (b, s, h))
    tab = pl.BlockSpec((None, tm, LANES), lambda b, s, h: (b, s, 0))
    in_specs = [head, pl.BlockSpec((1, LANES), lambda b, s, h: (0, 0)), tab, tab, tab]
    args = [x3, g, *tabs]
    if has_slab:
        sblk = slab[1]
        in_specs.append(pl.BlockSpec((None, tm, LANES), lambda b, s, h: (b, s, sblk)))
        args.append(slab[0])
    return pl.pallas_call(
        body, name=name, grid=(Bl, S // tm, nheads), in_specs=in_specs, out_specs=head,
        out_shape=jax.ShapeDtypeStruct((Bl, S, LANES * nheads), BF16),
        compiler_params=_cp("parallel", "parallel", "parallel"),
    )(*args)


def rope_norm_bwd(x3, nheads, dy3, g, tabs, slab=None, *, tm=1024, name):
    Bl, S, _ = x3.shape
    tm = min(tm, S)
    has_slab = slab is not None

    def body(*refs):
        x_ref, dy_ref, g_ref, c_ref, sa_ref, sb_ref = refs[:6]
        k = 7 if has_slab else 6
        dx_ref, dg_ref = refs[k], refs[k + 1]
        h = pl.program_id(2)
        first = (pl.program_id(0) == 0) & (pl.program_id(1) == 0) & (h == 0)
        x = x_ref[...]
        if has_slab:
            x = x + refs[6][...]
        g = g_ref[...]
        r = lax.rsqrt(jnp.sum(x * x, axis=-1, keepdims=True) / MLA_QK + EPS)
        n = x * r
        d = _rot_t(dy_ref[...], c_ref[...], sa_ref[...], sb_ref[...])

        @pl.when(first)
        def _():
            dg_ref[...] = jnp.zeros_like(dg_ref)

        dg_ref[...] += jnp.sum(d * n, axis=0, keepdims=True)
        dn = d * g
        dx = r * (dn - n * (jnp.sum(dn * n, axis=-1, keepdims=True) / MLA_QK))
        dx_ref[...] = dx
        if has_slab:
            ds_ref = refs[k + 2]

            @pl.when(h == 0)
            def _():
                ds_ref[...] = dx

            @pl.when(h > 0)
            def _():
                ds_ref[...] += dx

    head = pl.BlockSpec((None, tm, LANES), lambda b, s, h: (b, s, h))
    tab = pl.BlockSpec((None, tm, LANES), lambda b, s, h: (b, s, 0))
    row = pl.BlockSpec((1, LANES), lambda b, s, h: (0, 0))
    in_specs = [head, head, row, tab, tab, tab]
    args = [x3, dy3, g, *tabs]
    out_specs = [head, row]
    out_shape = [jax.ShapeDtypeStruct((Bl, S, LANES * nheads), F32), jax.ShapeDtypeStruct((1, LANES), F32)]
    if has_slab:
        sblk = slab[1]
        in_specs.append(pl.BlockSpec((None, tm, LANES), lambda b, s, h: (b, s, sblk)))
        args.append(slab[0])
        out_specs.append(tab)
        out_shape.append(jax.ShapeDtypeStruct((Bl, S, LANES), F32))
    return pl.pallas_call(
        body, name=name, grid=(Bl, S // tm, nheads), in_specs=in_specs, out_specs=out_specs, out_shape=out_shape,
        compiler_params=_cp("arbitrary", "arbitrary", "arbitrary"),
    )(*args)


def _softplus(z):
    return jnp.maximum(z, 0.0) + jnp.log(1.0 + jnp.exp(-jnp.abs(z)))


def _split_dots(xs, u):
    hi = [x.astype(BF16) for x in xs]
    lo = [(x - h.astype(F32)).astype(BF16) for x, h in zip(xs, hi)]
    top = [_dot(h, u) for h in hi]
    return [t + _dot(l, u) for t, l in zip(top, lo)]


SB_BLOCK = 256
SB_QBLOCK = 512


def sb_attn_fwd(proj3, *, plans=None, name):
    Bl, S, _ = proj3.shape
    tk = min(SB_BLOCK, S)
    tq = min(SB_QBLOCK, S)
    per_q = tq // tk
    scale = HEAD ** -0.5
    qb, kb0, vb0 = P_SBQ // LANES, P_SBK // LANES, P_SBV // LANES

    def body(q_ref, k_ref, v_ref, o_ref, rt_ref):
        i = pl.program_id(2)
        masks = _lane_masks()
        lane = lax.broadcasted_iota(jnp.int32, (1, LANES), 1)
        q = q_ref[...]
        qh = [jnp.where(m, q, 0.0).astype(BF16) for m in masks]
        rr = lax.broadcasted_iota(jnp.int32, (tq, tk), 0)
        cc = lax.broadcasted_iota(jnp.int32, (tq, tk), 1)
        u = (lax.broadcasted_iota(jnp.int32, (tk, tk), 0) > lax.broadcasted_iota(jnp.int32, (tk, tk), 1)).astype(BF16)

        rt_ref[...] = jnp.zeros_like(rt_ref)

        def step(t, carry):
            r0, r1, acc = carry
            j = (i + 1) * per_q - 1 - t
            off = pl.multiple_of(j * tk, tk)
            kb = k_ref[pl.ds(off, tk), :].astype(BF16)
            vb = v_ref[pl.ds(off, tk), :]
            strict = (cc + j * tk) < (rr + i * tq)
            rt_ref[...] = jnp.where(lane == j, r0, jnp.where(lane == j + HEAD, r1, rt_ref[...]))
            rs, two = [r0, r1], range(2)
            z = [_dot_nt(qh[h], kb) * scale for h in two]
            sp = [_softplus(z[h]) for h in two]
            keep = [jnp.where(strict, -sp[h], 0.0) for h in two]
            suf = _split_dots(keep, u)
            w = [jnp.where(strict, jnp.exp((z[h] - sp[h]) + suf[h] + rs[h]), 0.0) for h in two]
            pv = [_dot(w[h].astype(BF16), jnp.where(masks[h], vb, 0.0).astype(BF16)) for h in two]
            return rs[0] + jnp.sum(keep[0], axis=1, keepdims=True), rs[1] + jnp.sum(keep[1], axis=1, keepdims=True), acc + (pv[0] + pv[1])

        zero = jnp.zeros((tq, 1), F32)
        _, _, acc = lax.fori_loop(0, (i + 1) * per_q, step, (zero, zero, jnp.zeros((tq, LANES), F32)))
        o_ref[...] = acc

    seq = lambda blk0: pl.BlockSpec((None, S, LANES), lambda b, p, i: (b, 0, blk0 + p))
    out = pl.BlockSpec((None, tq, LANES), lambda b, p, i: (b, i, p))
    shp = jax.ShapeDtypeStruct((Bl, S, 2 * LANES), F32)
    return call_with_plans(
        body, plans, name=name, grid=(Bl, 2, S // tq),
        in_specs=[pl.BlockSpec((None, tq, LANES), lambda b, p, i: (b, i, qb + p)), seq(kb0), seq(vb0)],
        out_specs=[out, out], out_shape=[shp, shp], scratch_shapes=[], args=[proj3, proj3, proj3],
        sem=("arbitrary",) * 3 if plans else ("parallel", "parallel", "arbitrary"))


def sb_attn_bwd(proj3, rt3, do3, *, plans=None, name):
    Bl, S, _ = proj3.shape
    tk = min(SB_BLOCK, S)
    tq = min(SB_QBLOCK, S)
    per_q = tq // tk
    scale = HEAD ** -0.5
    qb, kb0, vb0 = P_SBQ // LANES, P_SBK // LANES, P_SBV // LANES

    def body(q_ref, k_ref, v_ref, rt_ref, do_ref, dq_ref, dk_ref, dv_ref):
        i = pl.program_id(2)

        @pl.when(i == 0)
        def _():
            dk_ref[...] = jnp.zeros_like(dk_ref)
            dv_ref[...] = jnp.zeros_like(dv_ref)

        masks = _lane_masks()
        lane = lax.broadcasted_iota(jnp.int32, (1, LANES), 1)
        q = q_ref[...]
        qh = [jnp.where(m, q, 0.0).astype(BF16) for m in masks]
        do_b = do_ref[...].astype(BF16)
        doh = [jnp.where(m, do_b, jnp.zeros_like(do_b)) for m in masks]
        rt = rt_ref[...]
        rr = lax.broadcasted_iota(jnp.int32, (tq, tk), 0)
        cc = lax.broadcasted_iota(jnp.int32, (tq, tk), 1)
        ur = lax.broadcasted_iota(jnp.int32, (tk, tk), 0)
        uc = lax.broadcasted_iota(jnp.int32, (tk, tk), 1)
        u_suffix = (ur > uc).astype(BF16)
        u_prefix = (ur < uc).astype(BF16)

        def step(j, carry):
            p0, p1, dq = carry
            off = pl.multiple_of(j * tk, tk)
            kf = k_ref[pl.ds(off, tk), :]
            kb = kf.astype(BF16)
            vb = v_ref[pl.ds(off, tk), :]
            strict = (cc + j * tk) < (rr + i * tq)
            ps, two = [p0, p1], range(2)
            r_j = [jnp.sum(jnp.where(lane == j + h * HEAD, rt, 0.0), axis=1, keepdims=True) for h in two]
            z = [_dot_nt(qh[h], kb) * scale for h in two]
            dw = [_dot_nt(doh[h], jnp.where(masks[h], vb, 0.0).astype(BF16)) for h in two]
            sp = [_softplus(z[h]) for h in two]
            keep = [jnp.where(strict, -sp[h], 0.0) for h in two]
            suf = _split_dots(keep, u_suffix)
            w = [jnp.where(strict, jnp.exp((z[h] - sp[h]) + suf[h] + r_j[h]), 0.0) for h in two]
            g = [dw[h] * w[h] for h in two]
            pre = _split_dots(g, u_prefix)
            dzb = [(jnp.where(strict, g[h] * jnp.exp(-sp[h]) - jnp.exp(z[h] - sp[h]) * (pre[h] + ps[h]), 0.0) * scale).astype(BF16) for h in two]
            dqs = [_dot(dzb[h], jnp.where(masks[h], kf, 0.0).astype(BF16)) for h in two]
            dks = [_dot_tn(dzb[h], qh[h]) for h in two]
            dvs = [_dot_tn(w[h].astype(BF16), doh[h]) for h in two]
            dk_ref[pl.ds(off, tk), :] += dks[0] + dks[1]
            dv_ref[pl.ds(off, tk), :] += dvs[0] + dvs[1]
            return ps[0] + jnp.sum(g[0], axis=1, keepdims=True), ps[1] + jnp.sum(g[1], axis=1, keepdims=True), dq + (dqs[0] + dqs[1])

        zero = jnp.zeros((tq, 1), F32)
        out = lax.fori_loop(0, (i + 1) * per_q, step, (zero, zero, jnp.zeros((tq, LANES), F32)))
        dq_ref[...] = out[2]

    seq_in = lambda blk0: pl.BlockSpec((None, S, LANES), lambda b, p, i: (b, 0, blk0 + p))
    blk = pl.BlockSpec((None, tq, LANES), lambda b, p, i: (b, i, p))
    seq_out = pl.BlockSpec((None, S, LANES), lambda b, p, i: (b, 0, p))
    shp = jax.ShapeDtypeStruct((Bl, S, 2 * LANES), F32)
    return call_with_plans(
        body, plans, name=name, grid=(Bl, 2, S // tq),
        in_specs=[pl.BlockSpec((None, tq, LANES), lambda b, p, i: (b, i, qb + p)), seq_in(kb0), seq_in(vb0), blk, blk],
        out_specs=[blk, seq_out, seq_out], out_shape=[shp, shp, shp], scratch_shapes=[], args=[proj3, proj3, proj3, rt3, do3],
        sem=("arbitrary",) * 3 if plans else ("parallel", "parallel", "arbitrary"))


def mla_attn_fwd(q3, k3, kv3, vblk0, *, tq=512, tk=512, plans=None, name):
    Bl, S, _ = q3.shape
    tq = min(tq, S)
    tk = min(tk, tq)
    per_q = tq // tk
    scale = MLA_QK ** -0.5

    def body(q_ref, k_ref, v_ref, o_ref, lse_ref):
        i = pl.program_id(2)
        masks = _lane_masks()
        rr = lax.broadcasted_iota(jnp.int32, (tq, tk), 0)
        cc = lax.broadcasted_iota(jnp.int32, (tq, tk), 1)
        qh = [q_ref[:, h * LANES:(h + 1) * LANES] for h in range(2)]

        def step(j, carry):
            m0, l0, m1, l1, acc = carry
            off = pl.multiple_of(j * tk, tk)
            vb = v_ref[pl.ds(off, tk), :]
            causal = (cc + j * tk) <= (rr + i * tq)
            ms, ls, two = [m0, m1], [l0, l1], range(2)
            kh = [k_ref[pl.ds(off, tk), h * LANES:(h + 1) * LANES] for h in two]
            s = [jnp.where(causal, _dot_nt(qh[h], kh[h]) * scale, NEG) for h in two]
            m_new = [jnp.maximum(ms[h], jnp.max(s[h], axis=1, keepdims=True)) for h in two]
            p = [jnp.exp(s[h] - m_new[h]) for h in two]
            alpha = [jnp.exp(ms[h] - m_new[h]) for h in two]
            ls = [alpha[h] * ls[h] + jnp.sum(p[h], axis=1, keepdims=True) for h in two]
            add = [_dot(p[h].astype(BF16), jnp.where(masks[h], vb, 0.0).astype(BF16)) for h in two]
            acc = acc * jnp.where(masks[0], alpha[0], alpha[1]) + (add[0] + add[1])
            return m_new[0], ls[0], m_new[1], ls[1], acc

        neg = jnp.full((tq, 1), NEG, F32)
        zero = jnp.zeros((tq, 1), F32)
        m0, l0, m1, l1, acc = lax.fori_loop(0, (i + 1) * per_q, step, (neg, zero, neg, zero, jnp.zeros((tq, LANES), F32)))
        o_ref[...] = acc / jnp.where(masks[0], l0, l1)
        lse_ref[...] = jnp.where(masks[0], m0 + jnp.log(l0), m1 + jnp.log(l1))

    out = pl.BlockSpec((None, tq, LANES), lambda b, p, i: (b, i, p))
    shp = jax.ShapeDtypeStruct((Bl, S, 3 * LANES), F32)
    return call_with_plans(
        body, plans, name=name, grid=(Bl, 3, S // tq),
        in_specs=[pl.BlockSpec((None, tq, 2 * LANES), lambda b, p, i: (b, i, p)), pl.BlockSpec((None, S, 2 * LANES), lambda b, p, i: (b, 0, p)),
                  pl.BlockSpec((None, S, LANES), lambda b, p, i: (b, 0, vblk0 + p))],
        out_specs=[out, out], out_shape=[shp, shp], scratch_shapes=[], args=[q3, k3, kv3],
        sem=("arbitrary",) * 3 if plans else ("parallel", "parallel", "arbitrary"))


def mla_attn_bwd(q3, k3, kv3, vblk0, o3, lse3, do3, *, tq=512, tk=512, name):
    Bl, S, _ = q3.shape
    tq = min(tq, S)
    tk = min(tk, tq)
    per_q = tq // tk
    nq = S // tq
    scale = MLA_QK ** -0.5

    def body(q_ref, k_ref, v_ref, o_ref, lse_ref, do_ref, dq_ref, dk_ref, dv_ref, s_scr, dp_scr, p_scr, ds_scr):
        j = pl.program_id(2)

        @pl.when(j == 0)
        def _():
            dq_ref[...] = jnp.zeros_like(dq_ref)

        masks = _lane_masks()
        vb = v_ref[...]
        vh = [jnp.where(m, vb, 0.0).astype(BF16) for m in masks]
        kh = [k_ref[:, h * LANES:(h + 1) * LANES] for h in range(2)]
        i0 = lax.div(j, jnp.int32(per_q))

        def step(i, carry, masked):
            dk0, dk1, dv = carry
            off = pl.multiple_of(i * tq, tq)
            do_b = do_ref[pl.ds(off, tq), :].astype(BF16)
            prod = do_b.astype(F32) * o_ref[pl.ds(off, tq), :]
            lse = lse_ref[pl.ds(off, tq), :]
            two = range(2)
            qh = [q_ref[pl.ds(off, tq), h * LANES:(h + 1) * LANES] for h in two]
            doh = [jnp.where(masks[h], do_b, jnp.zeros_like(do_b)) for h in two]
            delta = [jnp.sum(jnp.where(masks[h], prod, 0.0), axis=1, keepdims=True) for h in two]
            lse_h = [lse[:, h * HEAD:h * HEAD + 1] for h in two]
            for h in two:
                s_scr[h] = _dot_nt(qh[h], kh[h])
            for h in two:
                dp_scr[h] = _dot_nt(doh[h], vh[h])
            for r0 in range(0, tq, STRIP):
                rows = slice(r0, r0 + STRIP)
                for h in two:
                    s = s_scr[h, rows, :] * scale
                    if masked:
                        rr = lax.broadcasted_iota(jnp.int32, (STRIP, tk), 0) + (i * tq + r0)
                        cc = lax.broadcasted_iota(jnp.int32, (STRIP, tk), 1) + j * tk
                        s = jnp.where(cc <= rr, s, NEG)
                    p = jnp.exp(s - lse_h[h][rows])
                    p_scr[h, rows, :] = p.astype(BF16)
                    ds_scr[h, rows, :] = (p * (dp_scr[h, rows, :] - delta[h][rows])).astype(BF16)
            dqs = [_dot(ds_scr[h], kh[h]) * scale for h in two]
            dks = [dk0 + _dot_tn(ds_scr[0], qh[0]), dk1 + _dot_tn(ds_scr[1], qh[1])]
            dv = dv + _dot_tn(p_scr[0], doh[0]) + _dot_tn(p_scr[1], doh[1])
            for h in two:
                dq_ref[pl.ds(off, tq), h * LANES:(h + 1) * LANES] += dqs[h]
            return dks[0], dks[1], dv

        zero = jnp.zeros((tk, LANES), F32)
        carry = step(i0, (zero, zero, zero), True)
        dk0, dk1, dv = lax.fori_loop(i0 + 1, nq, lambda i, c: step(i, c, False), carry)
        dk_ref[:, 0:LANES] = dk0 * scale
        dk_ref[:, LANES:2 * LANES] = dk1 * scale
        dv_ref[...] = dv

    seq1 = pl.BlockSpec((None, S, LANES), lambda b, p, j: (b, 0, p))
    seq2 = pl.BlockSpec((None, S, 2 * LANES), lambda b, p, j: (b, 0, p))
    return pl.pallas_call(
        body, name=name, grid=(Bl, 3, S // tk),
        in_specs=[seq2, pl.BlockSpec((None, tk, 2 * LANES), lambda b, p, j: (b, j, p)),
                  pl.BlockSpec((None, tk, LANES), lambda b, p, j: (b, j, vblk0 + p)), seq1, seq1, seq1],
        out_specs=[seq2, pl.BlockSpec((None, tk, 2 * LANES), lambda b, p, j: (b, j, p)), pl.BlockSpec((None, tk, LANES), lambda b, p, j: (b, j, p))],
        out_shape=[jax.ShapeDtypeStruct((Bl, S, 6 * LANES), F32), jax.ShapeDtypeStruct((Bl, S, 6 * LANES), F32), jax.ShapeDtypeStruct((Bl, S, 3 * LANES), F32)],
        scratch_shapes=[pltpu.VMEM((2, tq, tk), F32), pltpu.VMEM((2, tq, tk), F32), pltpu.VMEM((2, tq, tk), BF16), pltpu.VMEM((2, tq, tk), BF16)],
        compiler_params=_cp("parallel", "parallel", "arbitrary"),
    )(q3, k3, kv3, o3, lse3, do3)


def _bucket_table():
    a = jnp.arange(WINDOW)[:, None]
    b = jnp.arange(2 * WINDOW)[None, :]
    dist = WINDOW + a - b
    max_exact = REL_BUCKETS // 2
    n = jnp.maximum(dist, 0)
    nf = jnp.maximum(n, 1).astype(F32)
    large = max_exact + (jnp.log(nf / max_exact) / math.log(REL_MAX_DIST / max_exact) * (REL_BUCKETS - max_exact)).astype(jnp.int32)
    large = jnp.minimum(large, REL_BUCKETS - 1)
    bucket = jnp.where(n < max_exact, n, large)
    return jnp.where((dist >= 0) & (dist < WINDOW), bucket, -1).astype(jnp.int32)


def swa_bias(rel_flat, bucket, *, name):
    def body(t_ref, b_ref, o_ref):
        bk = b_ref[...]
        for p in range(3):
            for hh in range(2):
                h = hh * 3 + p
                acc = jnp.full(bk.shape, NEG, F32)
                for b in range(REL_BUCKETS):
                    acc = jnp.where(bk == b, t_ref[b * 6 + h], acc)
                o_ref[p, hh] = acc

    return pl.pallas_call(
        body, name=name,
        in_specs=[pl.BlockSpec(memory_space=pltpu.SMEM), pl.BlockSpec(memory_space=pltpu.VMEM)],
        out_specs=pl.BlockSpec(memory_space=pltpu.VMEM),
        out_shape=jax.ShapeDtypeStruct((3, 2, WINDOW, 2 * WINDOW), F32),
    )(rel_flat, bucket)


def swa_bias_bwd(dbias, bucket, *, name):
    Bl = dbias.shape[0]

    def body(d_ref, b_ref, o_ref):
        bk = b_ref[...]
        lane = lax.broadcasted_iota(jnp.int32, (1, LANES), 1)
        rows = []
        for h in range(6):
            hh, p = divmod(h, 3)
            d = d_ref[0, p, hh]
            for bl in range(1, Bl):
                d = d + d_ref[bl, p, hh]
            row = jnp.zeros((1, LANES), F32)
            for b in range(REL_BUCKETS):
                s = jnp.sum(jnp.sum(jnp.where(bk == b, d, 0.0), axis=1, keepdims=True), axis=0, keepdims=True)
                row = row + jnp.where(lane == b, s, 0.0)
            rows.append(row)
        rows += [jnp.zeros((1, LANES), F32)] * 2
        o_ref[...] = jnp.concatenate(rows, axis=0)

    return pl.pallas_call(
        body, name=name,
        in_specs=[pl.BlockSpec(memory_space=pltpu.VMEM)] * 2, out_specs=pl.BlockSpec(memory_space=pltpu.VMEM),
        out_shape=jax.ShapeDtypeStruct((8, LANES), F32),
    )(dbias, bucket)


SWA_QBLOCKS = 8


def _swa_specs(vblk, nqb):
    rows = nqb * WINDOW
    cur = lambda blk: pl.BlockSpec((None, rows, LANES), lambda b, p, n: (b, n, blk))
    prev = lambda blk: pl.BlockSpec((None, WINDOW, LANES), lambda b, p, n: (b, jnp.maximum(n * nqb - 1, 0), blk))
    return [pl.BlockSpec((None, rows, LANES), lambda b, p, n: (b, n, p)), cur(0), prev(0), cur(vblk), prev(vblk),
            pl.BlockSpec((None, 2, WINDOW, 2 * WINDOW), lambda b, p, n: (p, 0, 0, 0)), pl.BlockSpec((None, 2, LANES), lambda b, p, n: (p, 0, 0))]


def _rows128(ref, m):
    return ref[m * WINDOW:(m + 1) * WINDOW, :]


def _swa_logits(qh, kp, kc, bias_h, first, scale):
    sp = jnp.where(first, NEG, _dot_nt(qh, kp) * scale + bias_h[:, :WINDOW])
    sc = _dot_nt(qh, kc) * scale + bias_h[:, WINDOW:]
    return sp, sc


def swa_attn_fwd(qn3, kn3, proj3, bias, sinks, *, plans=None, name):
    Bl, S, _ = qn3.shape
    scale = HEAD ** -0.5
    nqb = min(SWA_QBLOCKS, S // WINDOW)

    def body(q_ref, kc_ref, kp_ref, vc_ref, vp_ref, b_ref, s_ref, o_ref, lse_ref):
        seq_start = pl.program_id(2) == 0
        masks = _lane_masks()
        chains = [(m_, h) for m_ in range(nqb) for h in range(2)]
        kp = [kp_ref[...] if m_ == 0 else _rows128(kc_ref, m_ - 1) for m_ in range(nqb)]
        vp = [vp_ref[...] if m_ == 0 else _rows128(vc_ref, m_ - 1) for m_ in range(nqb)]
        kc = [_rows128(kc_ref, m_) for m_ in range(nqb)]
        vc = [_rows128(vc_ref, m_) for m_ in range(nqb)]
        sink = [s_ref[h:h + 1, 0:1] for h in range(2)]
        logits = {}
        for m_, h in chains:
            q = _rows128(q_ref, m_)
            qh = jnp.where(masks[h], q, jnp.zeros_like(q))
            logits[m_, h] = _swa_logits(qh, kp[m_], kc[m_], b_ref[h], seq_start if m_ == 0 else False, scale)
        mx = {c: jnp.maximum(jnp.maximum(jnp.max(logits[c][0], axis=1, keepdims=True), jnp.max(logits[c][1], axis=1, keepdims=True)), sink[c[1]])
              for c in chains}
        ex = {c: (jnp.exp(logits[c][0] - mx[c]), jnp.exp(logits[c][1] - mx[c])) for c in chains}
        den = {c: jnp.sum(ex[c][0], axis=1, keepdims=True) + jnp.sum(ex[c][1], axis=1, keepdims=True) + jnp.exp(sink[c[1]] - mx[c]) for c in chains}
        inv = {c: 1.0 / den[c] for c in chains}
        out = {}
        for m_, h in chains:
            c = (m_, h)
            out[c] = (_dot((ex[c][0] * inv[c]).astype(BF16), jnp.where(masks[h], vp[m_], 0.0).astype(BF16))
                      + _dot((ex[c][1] * inv[c]).astype(BF16), jnp.where(masks[h], vc[m_], 0.0).astype(BF16)))
        for m_ in range(nqb):
            o_ref[m_ * WINDOW:(m_ + 1) * WINDOW, :] = out[m_, 0] + out[m_, 1]
            lse_ref[m_ * WINDOW:(m_ + 1) * WINDOW, :] = jnp.where(masks[0], mx[m_, 0] + jnp.log(den[m_, 0]), mx[m_, 1] + jnp.log(den[m_, 1]))

    out = pl.BlockSpec((None, nqb * WINDOW, LANES), lambda b, p, n: (b, n, p))
    shp = jax.ShapeDtypeStruct((Bl, S, 3 * LANES), F32)
    return call_with_plans(
        body, plans, name=name, grid=(Bl, 3, S // (nqb * WINDOW)), in_specs=_swa_specs(P_SWV // LANES, nqb),
        out_specs=[out, out], out_shape=[shp, shp], scratch_shapes=[], args=[qn3, kn3, kn3, proj3, proj3, bias, sinks],
        sem=("arbitrary",) * 3 if plans else ("parallel", "parallel", "arbitrary"))


def swa_attn_bwd(qn3, kn3, proj3, bias, sinks, o3, lse3, do3, *, name):
    Bl, S, _ = qn3.shape
    scale = HEAD ** -0.5
    nqb = min(SWA_QBLOCKS, S // WINDOW)
    rows = nqb * WINDOW

    def body(q_ref, kc_ref, kp_ref, vc_ref, vp_ref, b_ref, s_ref, o_ref, lse_ref, do_ref,
             dq_ref, dk_ref, dv_ref, db_ref, dsk_ref):
        p_id, n = pl.program_id(1), pl.program_id(2)
        seq_start = n == 0

        @pl.when((p_id == 0) & seq_start)
        def _():
            dk_ref[...] = jnp.zeros_like(dk_ref)
            dv_ref[...] = jnp.zeros_like(dv_ref)

        @pl.when(seq_start)
        def _():
            db_ref[...] = jnp.zeros_like(db_ref)
            dsk_ref[...] = jnp.zeros_like(dsk_ref)

        masks = _lane_masks()
        zero = jnp.zeros((WINDOW, LANES), F32)
        chains = [(m_, h) for m_ in range(nqb) for h in range(2)]
        kp = [kp_ref[...] if m_ == 0 else _rows128(kc_ref, m_ - 1) for m_ in range(nqb)]
        vp = [vp_ref[...] if m_ == 0 else _rows128(vc_ref, m_ - 1) for m_ in range(nqb)]
        kc = [_rows128(kc_ref, m_) for m_ in range(nqb)]
        vc = [_rows128(vc_ref, m_) for m_ in range(nqb)]
        do_b = [_rows128(do_ref, m_).astype(BF16) for m_ in range(nqb)]
        prod = [do_b[m_].astype(F32) * _rows128(o_ref, m_) for m_ in range(nqb)]
        lse = [_rows128(lse_ref, m_) for m_ in range(nqb)]
        qh, doh, logits, lse_h, delta = {}, {}, {}, {}, {}
        for m_, h in chains:
            q = _rows128(q_ref, m_)
            qh[m_, h] = jnp.where(masks[h], q, jnp.zeros_like(q))
            doh[m_, h] = jnp.where(masks[h], do_b[m_], jnp.zeros_like(do_b[m_]))
            logits[m_, h] = _swa_logits(qh[m_, h], kp[m_], kc[m_], b_ref[h], seq_start if m_ == 0 else False, scale)
            lse_h[m_, h] = lse[m_][:, h * HEAD:h * HEAD + 1]
            delta[m_, h] = jnp.sum(jnp.where(masks[h], prod[m_], 0.0), axis=1, keepdims=True)
        pr = {c: (jnp.exp(logits[c][0] - lse_h[c]), jnp.exp(logits[c][1] - lse_h[c])) for c in chains}
        dp = {(m_, h): (_dot_nt(doh[m_, h], jnp.where(masks[h], vp[m_], 0.0).astype(BF16)),
                        _dot_nt(doh[m_, h], jnp.where(masks[h], vc[m_], 0.0).astype(BF16))) for m_, h in chains}
        ds = {c: (pr[c][0] * (dp[c][0] - delta[c]), pr[c][1] * (dp[c][1] - delta[c])) for c in chains}
        dsb = {c: ((ds[c][0] * scale).astype(BF16), (ds[c][1] * scale).astype(BF16)) for c in chains}
        dk_acc = [zero] * (nqb + 1)
        dv_acc = [zero] * (nqb + 1)
        db_acc = [[jnp.zeros((WINDOW, WINDOW), F32)] * 2 for _ in range(2)]
        dsk_acc = [jnp.zeros((1, 1), F32)] * 2
        dq = [zero] * nqb
        for m_, h in chains:
            c = (m_, h)
            db_acc[h] = [db_acc[h][0] + ds[c][0], db_acc[h][1] + ds[c][1]]
            dsk_acc[h] = dsk_acc[h] - jnp.sum(jnp.exp(s_ref[h:h + 1, 0:1] - lse_h[c]) * delta[c], axis=0, keepdims=True)
            dq[m_] = (dq[m_] + _dot(dsb[c][0], jnp.where(masks[h], kp[m_], jnp.zeros_like(kp[m_])))
                      + _dot(dsb[c][1], jnp.where(masks[h], kc[m_], jnp.zeros_like(kc[m_]))))
            dk_acc[m_] = dk_acc[m_] + _dot_tn(dsb[c][0], qh[c])
            dk_acc[m_ + 1] = dk_acc[m_ + 1] + _dot_tn(dsb[c][1], qh[c])
            dv_acc[m_] = dv_acc[m_] + _dot_tn(pr[c][0].astype(BF16), doh[c])
            dv_acc[m_ + 1] = dv_acc[m_ + 1] + _dot_tn(pr[c][1].astype(BF16), doh[c])
        for m_ in range(nqb):
            dq_ref[m_ * WINDOW:(m_ + 1) * WINDOW, :] = dq[m_]
        for h in range(2):
            db_ref[h, :, 0:WINDOW] += db_acc[h][0]
            db_ref[h, :, WINDOW:2 * WINDOW] += db_acc[h][1]
            dsk_ref[h:h + 1, :] += jnp.broadcast_to(dsk_acc[h], (1, LANES))
        offp = pl.multiple_of(jnp.maximum(n * nqb - 1, 0) * WINDOW, WINDOW)
        dk_ref[pl.ds(offp, WINDOW), :] += dk_acc[0]
        dv_ref[pl.ds(offp, WINDOW), :] += dv_acc[0]
        for m_ in range(nqb):
            off = pl.multiple_of(n * rows + m_ * WINDOW, WINDOW)
            dk_ref[pl.ds(off, WINDOW), :] += dk_acc[m_ + 1]
            dv_ref[pl.ds(off, WINDOW), :] += dv_acc[m_ + 1]

    blk = pl.BlockSpec((None, rows, LANES), lambda b, p, n: (b, n, p))
    seq = pl.BlockSpec((None, S, LANES), lambda b, p, n: (b, 0, 0))
    return pl.pallas_call(
        body, name=name, grid=(Bl, 3, S // rows), in_specs=_swa_specs(P_SWV // LANES, nqb) + [blk, blk, blk],
        out_specs=[blk, seq, seq, pl.BlockSpec((None, None, 2, WINDOW, 2 * WINDOW), lambda b, p, n: (b, p, 0, 0, 0)),
                   pl.BlockSpec((None, None, 2, LANES), lambda b, p, n: (b, p, 0, 0))],
        out_shape=[jax.ShapeDtypeStruct((Bl, S, 3 * LANES), F32), jax.ShapeDtypeStruct((Bl, S, LANES), F32), jax.ShapeDtypeStruct((Bl, S, LANES), F32),
                   jax.ShapeDtypeStruct((Bl, 3, 2, WINDOW, 2 * WINDOW), F32), jax.ShapeDtypeStruct((Bl, 3, 2, LANES), F32)],
        compiler_params=_cp("arbitrary", "arbitrary", "arbitrary"),
    )(qn3, kn3, kn3, proj3, proj3, bias, sinks, o3, lse3, do3)


CONV_ROWS = 64
CONV_LANES = 128


def _conv_strip(x_ref, h_ref, w, b, r0, cols, first_blk):
    x = x_ref[r0:r0 + CONV_ROWS, cols]
    if r0 == 0:
        rows = lax.broadcasted_iota(jnp.int32, x.shape, 0)
        h6 = jnp.where(first_blk, 0.0, h_ref[6:7, cols])
        h7 = jnp.where(first_blk, 0.0, h_ref[7:8, cols])
        x1 = jnp.where(rows == 0, h7, pltpu.roll(x, 1, 0))
        x2 = jnp.where(rows == 0, h6, jnp.where(rows == 1, h7, pltpu.roll(x, 2, 0)))
    else:
        x1 = x_ref[r0 - 1:r0 - 1 + CONV_ROWS, cols]
        x2 = x_ref[r0 - 2:r0 - 2 + CONV_ROWS, cols]
    return w[0:1] * x2 + w[1:2] * x1 + w[2:3] * x + b, x, x1, x2


FF_BLK = D_FF // 2


def _up_perm(a):
    q = FF_BLK
    return _cat([a[..., 0:q], a[..., 2 * q:3 * q], a[..., q:2 * q], a[..., 3 * q:4 * q]])


def conv_gate_fwd(up3, cw, cb, *, tm=256, name):
    Bl, S, _ = up3.shape
    tm = min(tm, S)
    W = 2 * FF_BLK

    def body(x_ref, h_ref, w_ref, b_ref, o_ref):
        first = pl.program_id(1) == 0

        def chunk(c, carry):
            cg = pl.ds(pl.multiple_of(c * CONV_LANES, CONV_LANES), CONV_LANES)
            cv = pl.ds(pl.multiple_of(FF_BLK + c * CONV_LANES, CONV_LANES), CONV_LANES)
            wg, wv, bg, bv = w_ref[:, cg], w_ref[:, cv], b_ref[:, cg], b_ref[:, cv]
            for r0 in range(0, tm, CONV_ROWS):
                ug = _conv_strip(x_ref, h_ref, wg, bg, r0, cg, first)[0]
                uv = _conv_strip(x_ref, h_ref, wv, bv, r0, cv, first)[0]
                o_ref[r0:r0 + CONV_ROWS, cg] = (ug * jax.nn.sigmoid(ug) * uv).astype(BF16)
            return carry

        lax.fori_loop(0, FF_BLK // CONV_LANES, chunk, 0)

    hb = tm // 8
    return pl.pallas_call(
        body, name=name, grid=(Bl, S // tm, 2),
        in_specs=[pl.BlockSpec((None, tm, W), lambda b, s, c: (b, s, c)),
                  pl.BlockSpec((None, 8, W), lambda b, s, c: (b, jnp.maximum(s * hb - 1, 0), c)),
                  pl.BlockSpec((3, W), lambda b, s, c: (0, c)), pl.BlockSpec((1, W), lambda b, s, c: (0, c))],
        out_specs=pl.BlockSpec((None, tm, FF_BLK), lambda b, s, c: (b, s, c)),
        out_shape=jax.ShapeDtypeStruct((Bl, S, D_FF), BF16),
        compiler_params=_cp("parallel", "parallel", "parallel"),
    )(up3, up3, cw, cb)


def conv_gate_bwd(up3, cw, cb, da3, *, tm=256, name):
    Bl, S, _ = up3.shape
    tm = min(tm, S)
    ns = S // tm
    W = 2 * FF_BLK

    def body(x_ref, h_ref, w_ref, b_ref, da_ref, dup_ref, dw_ref, nxt_ref, du_scr):
        b, s = pl.program_id(1), pl.program_id(2)
        seq_end = s == 0
        first = s == ns - 1

        @pl.when((b == 0) & seq_end)
        def _():
            dw_ref[...] = jnp.zeros_like(dw_ref)

        def du_chunk(c, carry):
            cg = pl.ds(pl.multiple_of(c * CONV_LANES, CONV_LANES), CONV_LANES)
            cv = pl.ds(pl.multiple_of(FF_BLK + c * CONV_LANES, CONV_LANES), CONV_LANES)
            wg, wv, bg, bv = w_ref[:, cg], w_ref[:, cv], b_ref[:, cg], b_ref[:, cv]
            acc_g = [jnp.zeros((1, CONV_LANES), F32)] * 4
            acc_v = [jnp.zeros((1, CONV_LANES), F32)] * 4
            for r0 in range(0, tm, CONV_ROWS):
                ug, xg, xg1, xg2 = _conv_strip(x_ref, h_ref, wg, bg, r0, cg, first)
                uv, xv, xv1, xv2 = _conv_strip(x_ref, h_ref, wv, bv, r0, cv, first)
                da = da_ref[r0:r0 + CONV_ROWS, cg].astype(F32)
                sg = jax.nn.sigmoid(ug)
                dug = da * uv * sg * (1.0 + ug * (1.0 - sg))
                duv = da * ug * sg
                du_scr[r0:r0 + CONV_ROWS, cg] = dug
                du_scr[r0:r0 + CONV_ROWS, cv] = duv
                col = lambda t: jnp.sum(t, axis=0, keepdims=True)
                acc_g = [acc_g[0] + col(dug * xg2), acc_g[1] + col(dug * xg1), acc_g[2] + col(dug * xg), acc_g[3] + col(dug)]
                acc_v = [acc_v[0] + col(duv * xv2), acc_v[1] + col(duv * xv1), acc_v[2] + col(duv * xv), acc_v[3] + col(duv)]
            for t in range(4):
                dw_ref[t:t + 1, cg] += acc_g[t]
                dw_ref[t:t + 1, cv] += acc_v[t]
            return carry

        lax.fori_loop(0, FF_BLK // CONV_LANES, du_chunk, 0)
        du_scr[tm:tm + 8, :] = jnp.where(seq_end, 0.0, nxt_ref[...])

        def dup_chunk(c, carry):
            cols = pl.ds(pl.multiple_of(c * CONV_LANES, CONV_LANES), CONV_LANES)
            w = w_ref[:, cols]
            for r0 in range(0, tm, CONV_ROWS):
                d0 = du_scr[r0:r0 + CONV_ROWS, cols]
                d1 = du_scr[r0 + 1:r0 + 1 + CONV_ROWS, cols]
                d2 = du_scr[r0 + 2:r0 + 2 + CONV_ROWS, cols]
                dup_ref[r0:r0 + CONV_ROWS, cols] = (w[2:3] * d0 + w[1:2] * d1 + w[0:1] * d2).astype(BF16)
            return carry

        lax.fori_loop(0, W // CONV_LANES, dup_chunk, 0)
        nxt_ref[...] = du_scr[0:8, :]

    hb = tm // 8
    rb = lambda s: ns - 1 - s
    return pl.pallas_call(
        body, name=name, grid=(2, Bl, ns),
        in_specs=[pl.BlockSpec((None, tm, W), lambda c, b, s: (b, rb(s), c)),
                  pl.BlockSpec((None, 8, W), lambda c, b, s: (b, jnp.maximum(rb(s) * hb - 1, 0), c)),
                  pl.BlockSpec((3, W), lambda c, b, s: (0, c)), pl.BlockSpec((1, W), lambda c, b, s: (0, c)),
                  pl.BlockSpec((None, tm, FF_BLK), lambda c, b, s: (b, rb(s), c))],
        out_specs=[pl.BlockSpec((None, tm, W), lambda c, b, s: (b, rb(s), c)), pl.BlockSpec((8, W), lambda c, b, s: (0, c))],
        out_shape=[jax.ShapeDtypeStruct((Bl, S, 2 * D_FF), BF16), jax.ShapeDtypeStruct((8, 2 * D_FF), F32)],
        scratch_shapes=[pltpu.VMEM((8, W), F32), pltpu.VMEM((tm + 8, W), F32)],
        compiler_params=_cp("arbitrary", "arbitrary", "arbitrary"),
    )(up3, up3, cw, cb, da3)


def gate_bwd(dx3, y3, gate, *, tm=512, name):
    Bl, S, D = dx3.shape
    tm = min(tm, S)

    def body(dx_ref, y_ref, g_ref, o_ref, dg_ref):
        @pl.when(pl.program_id(1) == 0)
        def _():
            dg_ref[...] = jnp.zeros_like(dg_ref)

        dx = dx_ref[...]
        dg_ref[...] += jnp.sum(dx * y_ref[...], axis=0, keepdims=True)
        o_ref[...] = (dx * g_ref[...]).astype(BF16)

    blk = pl.BlockSpec((None, tm, D), lambda b, s: (b, s, 0))
    vec = pl.BlockSpec((None, 1, D), lambda b, s: (b, 0, 0))
    return pl.pallas_call(
        body, name=name, grid=(Bl, S // tm), in_specs=[blk, blk, vec], out_specs=[blk, vec],
        out_shape=[jax.ShapeDtypeStruct((Bl, S, D), BF16), jax.ShapeDtypeStruct((Bl, 1, D), F32)],
        compiler_params=_cp("parallel", "arbitrary"),
    )(dx3, y3, gate)


def loss_grad(y3, t3, *, tm=512, name):
    Bl, S, D = y3.shape
    tm = min(tm, S)
    last = (Bl - 1, S // tm - 1)

    def body(y_ref, t_ref, dy_ref, l_ref, acc_ref):
        b, s = pl.program_id(0), pl.program_id(1)

        @pl.when((b == 0) & (s == 0))
        def _():
            acc_ref[...] = jnp.zeros_like(acc_ref)

        e = y_ref[...] - t_ref[...]
        dy_ref[...] = e * (1.0 / D)
        acc_ref[...] += jnp.sum(e * e, axis=0, keepdims=True)

        @pl.when((b == last[0]) & (s == last[1]))
        def _():
            l_ref[...] = jnp.broadcast_to(jnp.sum(acc_ref[...], axis=1, keepdims=True) * (0.5 / D), (1, LANES))

    blk = pl.BlockSpec((None, tm, D), lambda b, s: (b, s, 0))
    return pl.pallas_call(
        body, name=name, grid=(Bl, S // tm), in_specs=[blk, blk],
        out_specs=[blk, pl.BlockSpec((1, LANES), lambda b, s: (0, 0))],
        out_shape=[jax.ShapeDtypeStruct((Bl, S, D), F32), jax.ShapeDtypeStruct((1, LANES), F32)],
        scratch_shapes=[pltpu.VMEM((1, D), F32)], compiler_params=_cp("arbitrary", "arbitrary"),
    )(y3, t3)


def adamw(w, g, m, v, *, name):
    L, R, C = w.shape
    tr = _tile(R, 512, 8)

    def body(w_ref, g_ref, m_ref, v_ref, d_ref, m2_ref, v2_ref):
        d_ref[...], m2_ref[...], v2_ref[...] = _adam_update(w_ref[...], g_ref[...], m_ref[...], v_ref[...])

    blk = pl.BlockSpec((None, tr, C), lambda l, i: (l, i, 0))
    shp = jax.ShapeDtypeStruct((L, R, C), F32)
    return pl.pallas_call(
        body, name=name, grid=(L, R // tr), in_specs=[blk] * 4, out_specs=[blk] * 3, out_shape=[shp] * 3,
        compiler_params=_cp("parallel", "parallel"),
    )(w, g, m, v)


def _adam_update(w, g, m, v):
    c1 = 1.0 / (1.0 - ADAM_B1 ** ADAM_STEP)
    c2 = 1.0 / (1.0 - ADAM_B2 ** ADAM_STEP)
    m2 = ADAM_B1 * m + (1.0 - ADAM_B1) * g
    v2 = ADAM_B2 * v + (1.0 - ADAM_B2) * (g * g)
    return -ADAM_LR * ((m2 * c1) / (jnp.sqrt(v2 * c2) + ADAM_EPS) + ADAM_WD * w), m2, v2


def adamw_small(ws, gs, ms, vs, *, name):
    na = len(ws)

    def body(*refs):
        w_r, g_r, m_r, v_r = (refs[i * na:(i + 1) * na] for i in range(4))
        d_r, m2_r, v2_r = (refs[(4 + i) * na:(5 + i) * na] for i in range(3))
        for a in range(na):
            d_r[a][...], m2_r[a][...], v2_r[a][...] = _adam_update(w_r[a][...], g_r[a][...], m_r[a][...], v_r[a][...])

    vm = pl.BlockSpec(memory_space=pltpu.VMEM)
    shp = [jax.ShapeDtypeStruct(w.shape, F32) for w in ws]
    out = pl.pallas_call(body, name=name, in_specs=[vm] * (4 * na), out_specs=[vm] * (3 * na), out_shape=shp * 3)(*ws, *gs, *ms, *vs)
    return out[:na], out[na:2 * na], out[2 * na:]


def sum_small(xs, *, name):
    na = len(xs)

    def body(*refs):
        for x_ref, o_ref in zip(refs[:na], refs[na:]):
            acc = x_ref[0]
            for k in range(1, x_ref.shape[0]):
                acc = acc + x_ref[k]
            o_ref[...] = acc

    vm = pl.BlockSpec(memory_space=pltpu.VMEM)
    return pl.pallas_call(body, name=name, in_specs=[vm] * na, out_specs=[vm] * na,
                          out_shape=[jax.ShapeDtypeStruct(x.shape[1:], x.dtype) for x in xs])(*xs)


def pair_add_half(g4, recv, c_arr, *, tr=512, name):
    _, R, C = g4.shape
    H = R // 2
    tr = _tile(H, tr, 16)
    nb = H // tr

    def body(c_ref, g_ref, r_ref, o_ref):
        o_ref[...] = (g_ref[...].astype(F32) + r_ref[...].astype(F32)).astype(BF16)

    grid_spec = pltpu.PrefetchScalarGridSpec(
        num_scalar_prefetch=1, grid=(4, nb),
        in_specs=[pl.BlockSpec((None, tr, C), lambda k, i, c_ref: (k, c_ref[0] * nb + i, 0)),
                  pl.BlockSpec((None, tr, C), lambda k, i, c_ref: (k, i, 0))],
        out_specs=pl.BlockSpec((None, tr, C), lambda k, i, c_ref: (k, i, 0)),
    )
    return pl.pallas_call(
        body, name=name, grid_spec=grid_spec, out_shape=jax.ShapeDtypeStruct((4, H, C), BF16),
        compiler_params=_cp("parallel", "parallel"),
    )(c_arr, g4, recv)


def chip_sum_into(landed, pair, sel, *, tr=512, name):
    _, H, C = landed.shape
    tr = _tile(H, tr, 16)
    nb = H // tr

    def body(s_ref, l0, l1, l2, l3, p_ref, o_ref):
        own = p_ref[...].astype(F32)
        acc = None
        for k, l_ref in enumerate((l0, l1, l2, l3)):
            part = jnp.where(s_ref[0] == k, own, l_ref[...].astype(F32))
            acc = part if acc is None else acc + part
        o_ref[...] = acc

    def slot(k):
        return pl.BlockSpec((None, tr, C), lambda i, s: (jnp.where(s[0] == k, (k + 1) % 4, k), i, 0))

    grid_spec = pltpu.PrefetchScalarGridSpec(
        num_scalar_prefetch=1, grid=(nb,),
        in_specs=[slot(0), slot(1), slot(2), slot(3), pl.BlockSpec((None, tr, C), lambda i, s: (s[0], i, 0))],
        out_specs=pl.BlockSpec((tr, C), lambda i, s: (s[1] * nb + i, 0)),
    )
    return pl.pallas_call(
        body, name=name, grid_spec=grid_spec, out_shape=jax.ShapeDtypeStruct((2 * H, C), F32), compiler_params=_cp("parallel"),
    )(sel, landed, landed, landed, landed, pair)


def mods_matmul(c_all, w_ada, b_ada_cols, *, tn=512, name):
    L, D, E = w_ada.shape
    nb = c_all.shape[0]
    tn = _tile(E, tn)

    def body(c_ref, w_ref, b_ref, o_ref):
        c = c_ref[...]
        a = c * jax.nn.sigmoid(c)
        o_ref[...] = jnp.dot(a, w_ref[...], preferred_element_type=F32, precision=lax.Precision.HIGHEST) + b_ref[...]

    return pl.pallas_call(
        body, name=name, grid=(L, E // tn),
        in_specs=[pl.BlockSpec((nb, D), lambda l, j: (0, 0)), pl.BlockSpec((None, D, tn), lambda l, j: (l, 0, j)),
                  pl.BlockSpec((None, 1, tn), lambda l, j: (l, 0, j))],
        out_specs=pl.BlockSpec((None, nb, tn), lambda l, j: (l, 0, j)),
        out_shape=jax.ShapeDtypeStruct((L, nb, E), F32), compiler_params=_cp("parallel", "parallel"),
    )(c_all, w_ada, b_ada_cols)


def ada_grad(c_all, dmods, *, tn=512, name):
    L, nb, E = dmods.shape
    D = c_all.shape[1]
    tn = _tile(E, tn)

    def body(c_ref, d_ref, o_ref):
        c = c_ref[...]
        a = c * jax.nn.sigmoid(c)
        o_ref[...] = lax.dot_general(a, d_ref[...], (((0,), (0,)), ((), ())), preferred_element_type=F32, precision=lax.Precision.HIGHEST)

    return pl.pallas_call(
        body, name=name, grid=(L, E // tn),
        in_specs=[pl.BlockSpec((nb, D), lambda l, j: (0, 0)), pl.BlockSpec((None, nb, tn), lambda l, j: (l, 0, j))],
        out_specs=pl.BlockSpec((None, D, tn), lambda l, j: (l, 0, j)),
        out_shape=jax.ShapeDtypeStruct((L, D, E), F32), compiler_params=_cp("parallel", "parallel"),
    )(c_all, dmods)


HBM = pl.BlockSpec(memory_space=pltpu.HBM)


def _me():
    return lax.axis_index("x"), lax.axis_index("y"), lax.axis_index("c")


def _flip(v, bit):
    return 1 - v if bit else v


def allgather8(xs, *, name):
    na = len(xs)

    def body(*refs):
        x_refs, out_refs = refs[:na], refs[na:2 * na]
        send_sems, recv_sems = refs[2 * na], refs[2 * na + 1]
        x, y, c = _me()
        me = 4 * x + 2 * y + c
        for x_ref, out_ref in zip(x_refs, out_refs):
            out_ref[me] = x_ref[...]
        sends = []
        for a, (x_ref, out_ref) in enumerate(zip(x_refs, out_refs)):
            for k in range(1, 8):
                peer = (_flip(x, k & 4), _flip(y, k & 2), _flip(c, k & 1))
                cp = pltpu.make_async_remote_copy(src_ref=x_ref, dst_ref=out_ref.at[me], send_sem=send_sems.at[a, k - 1],
                                                  recv_sem=recv_sems.at[a, k - 1], device_id=peer, device_id_type=MESH)
                cp.start()
                sends.append(cp)
        for a, (x_ref, out_ref) in enumerate(zip(x_refs, out_refs)):
            for k in range(1, 8):
                peer = (_flip(x, k & 4), _flip(y, k & 2), _flip(c, k & 1))
                src = 4 * peer[0] + 2 * peer[1] + peer[2]
                pltpu.make_async_remote_copy(src_ref=x_ref, dst_ref=out_ref.at[src], send_sem=send_sems.at[a, k - 1],
                                             recv_sem=recv_sems.at[a, k - 1], device_id=peer, device_id_type=MESH).wait_recv()
        for cp in sends:
            cp.wait_send()

    vm = pl.BlockSpec(memory_space=pltpu.VMEM)
    return pl.pallas_call(
        body, name=name, in_specs=[vm] * na, out_specs=[vm] * na,
        out_shape=[jax.ShapeDtypeStruct((8,) + a.shape, a.dtype) for a in xs],
        scratch_shapes=[pltpu.SemaphoreType.DMA((na, 7)), pltpu.SemaphoreType.DMA((na, 7))],
    )(*xs)


class _Plan:
    def __init__(self, ins, out_shapes, ncopies, copies, aliased=False):
        self.ins, self.out_shapes, self.ncopies, self.copies, self.aliased = list(ins), list(out_shapes), ncopies, copies, aliased

    def start(self, in_refs, out_refs, send_sems, recv_sems):
        sends, _ = self.copies(in_refs, out_refs, send_sems, recv_sems)
        for cp in sends:
            cp.start()

    def finish(self, in_refs, out_refs, send_sems, recv_sems):
        sends, recvs = self.copies(in_refs, out_refs, send_sems, recv_sems)
        for cp in recvs:
            cp.wait_recv()
        for cp in sends:
            cp.wait_send()


def _rcopy(src, dst, send_sems, recv_sems, idx, dev):
    return pltpu.make_async_remote_copy(src_ref=src, dst_ref=dst, send_sem=send_sems.at[idx], recv_sem=recv_sems.at[idx],
                                        device_id=dev, device_id_type=MESH)


def _other_chips(x, y):
    return [(_flip(x, k & 2), _flip(y, k & 1)) for k in range(1, 4)]


def plan_gather_ici(ws):
    def copies(in_refs, out_refs, ss, rs):
        x, y, c = _me()
        j = 2 * x + y
        sends, recvs = [], []
        for a, (x_ref, out_ref) in enumerate(zip(in_refs, out_refs)):
            H = x_ref.shape[0] // 2
            for k, (px, py) in enumerate(_other_chips(x, y)):
                sends.append(_rcopy(x_ref.at[pl.ds(c * H, H)], out_ref.at[j, pl.ds(c * H, H)], ss, rs, 3 * a + k, (px, py, c)))
                slot = out_ref.at[2 * px + py, pl.ds(c * H, H)]
                recvs.append(_rcopy(slot, slot, ss, rs, 3 * a + k, (px, py, c)))
        return sends, recvs

    return _Plan(ws, [jax.ShapeDtypeStruct((4,) + w.shape, w.dtype) for w in ws], 3 * len(ws), copies)


def plan_gather_d2d(w4s):
    def copies(in_refs, out_refs, ss, rs):
        x, y, c = _me()
        sends, recvs = [], []
        for a, out_ref in enumerate(out_refs):
            H = out_ref.shape[1] // 2
            for k, (px, py) in enumerate(_other_chips(x, y)):
                mine = out_ref.at[2 * px + py, pl.ds(c * H, H)]
                theirs = out_ref.at[2 * px + py, pl.ds((1 - c) * H, H)]
                sends.append(_rcopy(mine, mine, ss, rs, 3 * a + k, (x, y, 1 - c)))
                recvs.append(_rcopy(theirs, theirs, ss, rs, 3 * a + k, (x, y, 1 - c)))
        return sends, recvs

    return _Plan(w4s, [jax.ShapeDtypeStruct(w.shape, w.dtype) for w in w4s], 3 * len(w4s), copies, aliased=True)


def plan_swap_halves(gs):
    def copies(in_refs, out_refs, ss, rs):
        x, y, c = _me()
        sends, recvs = [], []
        for a, (g_ref, out_ref) in enumerate(zip(in_refs, out_refs)):
            H = g_ref.shape[1] // 2
            for k in range(4):
                sends.append(_rcopy(g_ref.at[k, pl.ds((1 - c) * H, H)], out_ref.at[k], ss, rs, 4 * a + k, (x, y, 1 - c)))
                recvs.append(_rcopy(g_ref.at[k, pl.ds(c * H, H)], out_ref.at[k], ss, rs, 4 * a + k, (x, y, 1 - c)))
        return sends, recvs

    return _Plan(gs, [jax.ShapeDtypeStruct((4, g.shape[1] // 2, g.shape[2]), g.dtype) for g in gs], 4 * len(gs), copies)


def plan_scatter_ici(ps):
    def copies(in_refs, out_refs, ss, rs):
        x, y, c = _me()
        j = 2 * x + y
        sends, recvs = [], []
        for a, (p_ref, out_ref) in enumerate(zip(in_refs, out_refs)):
            for k, (px, py) in enumerate(_other_chips(x, y)):
                sends.append(_rcopy(p_ref.at[2 * px + py], out_ref.at[j], ss, rs, 3 * a + k, (px, py, c)))
                slot = out_ref.at[2 * px + py]
                recvs.append(_rcopy(slot, slot, ss, rs, 3 * a + k, (px, py, c)))
        return sends, recvs

    return _Plan(ps, [jax.ShapeDtypeStruct(p.shape, p.dtype) for p in ps], 3 * len(ps), copies)


def plan_join_halves(fulls):
    def copies(in_refs, out_refs, ss, rs):
        x, y, c = _me()
        sends, recvs = [], []
        for a, out_ref in enumerate(out_refs):
            H = out_ref.shape[0] // 2
            mine, theirs = out_ref.at[pl.ds(c * H, H)], out_ref.at[pl.ds((1 - c) * H, H)]
            sends.append(_rcopy(mine, mine, ss, rs, a, (x, y, 1 - c)))
            recvs.append(_rcopy(theirs, theirs, ss, rs, a, (x, y, 1 - c)))
        return sends, recvs

    return _Plan(fulls, [jax.ShapeDtypeStruct(f.shape, f.dtype) for f in fulls], len(fulls), copies, aliased=True)


def call_with_plans(body, plans, *, grid, in_specs, out_specs, out_shape, scratch_shapes, args, sem, name):
    plans = list(plans or [])
    n_in, n_out, n_scr = len(in_specs), len(out_specs), len(scratch_shapes)
    c_in = [len(p.ins) for p in plans]
    c_out = [len(p.out_shapes) for p in plans]
    steps = math.prod(grid) if grid else 1

    def wrapped(*refs):
        pos = 0

        def take(n):
            nonlocal pos
            out = refs[pos:pos + n]
            pos += n
            return out

        ins = take(n_in)
        cins = [take(n) for n in c_in]
        outs = take(n_out)
        couts = [take(n) for n in c_out]
        scr = take(n_scr)
        sems = [take(2) for _ in plans]
        def start_all():
            for p, ci, co, (ss, rs) in zip(plans, cins, couts, sems):
                p.start(ci, co, ss, rs)

        def finish_all():
            for p, ci, co, (ss, rs) in zip(plans, cins, couts, sems):
                p.finish(ci, co, ss, rs)

        if plans and grid:
            idx = 0
            for ax, g in enumerate(grid):
                idx = idx * g + pl.program_id(ax)
            pl.when(idx == 0)(start_all)
        elif plans:
            start_all()
        if body is not None:
            body(*ins, *outs, *scr)
        if plans and grid:
            pl.when(idx == steps - 1)(finish_all)
        elif plans:
            finish_all()

    aliases = {}
    i_pos, o_pos = n_in, n_out
    for p, ni, no in zip(plans, c_in, c_out):
        if p.aliased:
            aliases.update({i_pos + t: o_pos + t for t in range(ni)})
        i_pos += ni
        o_pos += no
    kwargs = dict(grid=grid) if grid else {}
    if aliases:
        kwargs["input_output_aliases"] = aliases
    res = pl.pallas_call(
        wrapped, name=name, in_specs=list(in_specs) + [HBM] * sum(c_in), out_specs=list(out_specs) + [HBM] * sum(c_out),
        out_shape=list(out_shape) + [s for p in plans for s in p.out_shapes],
        scratch_shapes=list(scratch_shapes) + [pltpu.SemaphoreType.DMA((p.ncopies,)) for p in plans for _ in range(2)],
        compiler_params=_cp(*sem) if grid else pltpu.CompilerParams(vmem_limit_bytes=VMEM_LIMIT), **kwargs,
    )(*args, *[a for p in plans for a in p.ins])
    res = list(res)
    comp, rest = res[:n_out], res[n_out:]
    pouts = []
    for no in c_out:
        pouts.append(rest[:no])
        rest = rest[no:]
    return comp, pouts


def run_plans(plans, *, name):
    return call_with_plans(None, plans, grid=(), in_specs=[], out_specs=[], out_shape=[], scratch_shapes=[], args=[], sem=(), name=name)[1]


def _cat(parts, axis=-1):
    return jnp.concatenate(parts, axis=axis)


def _pairs_of_heads(a, axis, inverse=False):
    lead, tail = a.shape[:axis], a.shape[axis + 1:]
    split = (3, 2) if inverse else (2, 3)
    a = a.reshape(lead + split + (HEAD,) + tail)
    return jnp.swapaxes(a, axis, axis + 1).reshape(lead + (6 * HEAD,) + tail)


def _prep_w_in(w):
    z = lambda n: jnp.zeros((w.shape[0], n), w.dtype)
    return _cat([w[:, 0:1152], z(64), w[:, 1152:1184], z(32), _pairs_of_heads(w[:, 1184:1568], 1), w[:, 1568:1824]])


def _unprep_w_in(g):
    return _cat([g[:, 0:1152], g[:, 1216:1248], _pairs_of_heads(g[:, P_SWQ:P_SWK], 1, inverse=True), g[:, P_SWK:P_END]])


def _prep_w_uq(w):
    r = w.shape[0]
    return jnp.pad(w.reshape(r, 6, MLA_QK), ((0, 0), (0, 0), (0, LANES - MLA_QK))).reshape(r, 6 * LANES)


def _unprep_w_uq(g):
    r = g.shape[0]
    return g.reshape(r, 6, LANES)[:, :, :MLA_QK].reshape(r, 6 * MLA_QK)


def _prep_w_ukv(w):
    r = w.shape[0]
    w3 = w.reshape(r, 6, LANES)
    k = jnp.pad(w3[:, :, :HEAD], ((0, 0), (0, 0), (0, LANES - HEAD))).reshape(r, 6 * LANES)
    return _cat([k, w3[:, :, HEAD:].reshape(r, 6 * HEAD)])


def _unprep_w_ukv(g):
    r = g.shape[0]
    k = g[:, :6 * LANES].reshape(r, 6, LANES)[:, :, :HEAD]
    return _cat([k, g[:, 6 * LANES:].reshape(r, 6, HEAD)], axis=2).reshape(r, 6 * LANES)


def _prep_w_out(w):
    return _cat([w[0:640], _pairs_of_heads(w[640:], 0)], axis=0)


def _unprep_w_out(g):
    return _cat([g[0:640], _pairs_of_heads(g[640:], 0, inverse=True)], axis=0)


def _rope_tables(positions):
    half = 16
    inv_freq = jnp.power(ROPE_THETA, -jnp.arange(half, dtype=F32) / half)
    ang = positions.astype(F32)[..., None] * inv_freq
    cos, sin = jnp.cos(ang), jnp.sin(ang)
    z = lambda n: jnp.zeros(ang.shape[:-1] + (n,), F32)
    return (_cat([jnp.ones(ang.shape[:-1] + (HEAD,), F32), cos, cos, z(32)]), _cat([z(HEAD), -sin, z(16), z(32)]), _cat([z(HEAD), z(16), sin, z(32)]))


def _small_params(p):
    pad96 = lambda g: _cat([g, jnp.zeros((32,), F32)]).reshape(1, LANES)
    two = lambda g: _cat([g, g]).reshape(1, LANES)
    sinks = jnp.broadcast_to(p["sw_sinks"].reshape(2, 3).T[:, :, None], (3, 2, LANES))
    return dict(n1=p["norm1_g"].reshape(1, -1), n2=p["norm2_g"].reshape(1, -1), cq_g=p["mla_cq_g"].reshape(1, -1),
                ckv_g=p["mla_ckv_g"].reshape(1, -1), qn_g=pad96(p["mla_qn_g"]), kn_g=pad96(p["mla_kn_g"]),
                swq_g=two(p["sw_qn_g"]), swk_g=two(p["sw_kn_g"]), sinks=sinks, conv_b=_up_perm(p["conv_b"]).reshape(1, -1))


class _NoFlow:
    def plans(self, tag):
        return []

    def done(self, tag, outs):
        pass

    def add(self, key, g):
        pass


def _layer_fwd(x3, md, W, tabs, bias, tag, flow=_NoFlow()):
    Bl, S, D = x3.shape
    T = Bl * S
    n = lambda s: f"{s}_{tag}"
    two = lambda a: a.reshape(T, a.shape[-1])
    three = lambda a: a.reshape(Bl, S, a.shape[-1])
    h = rms_fwd(x3, 0, D, W["n1"], md["scale1"], md["shift1"], name=n("norm1"))
    proj = three(matmul(two(h), W["w_in"], tn=1920, name=n("in_proj")))
    (o_a, rt_a), got = sb_attn_fwd(proj, plans=flow.plans(n("sb_fwd")), name=n("sb_fwd"))
    flow.done(n("sb_fwd"), got)
    cqn = rms_fwd(proj, P_CQ // 256, 256, W["cq_g"], name=n("cq_norm"))
    ckvn = rms_fwd(proj, P_CKV // LANES, LANES, W["ckv_g"], name=n("ckv_norm"))
    qb = three(matmul(two(cqn), W["w_uq"], tm=1024, tn=768, name=n("uq")))
    kvb = three(matmul(two(ckvn), W["w_ukv"], tm=1024, tn=1152, name=n("ukv")))
    q_m = rope_norm_fwd(qb, 6, W["qn_g"], tabs, name=n("q_rope"))
    k_m = rope_norm_fwd(kvb, 6, W["kn_g"], tabs, (proj, P_SLAB // LANES), name=n("k_rope"))
    (o_b, lse_b), got = mla_attn_fwd(q_m, k_m, kvb, 6, plans=flow.plans(n("mla_fwd")), name=n("mla_fwd"))
    flow.done(n("mla_fwd"), got)
    q_c = pair_rms_fwd(proj, P_SWQ // LANES, 3, W["swq_g"], name=n("swq_norm"))
    k_c = pair_rms_fwd(proj, P_SWK // LANES, 1, W["swk_g"], name=n("swk_norm"))
    (o_c, lse_c), got = swa_attn_fwd(q_c, k_c, proj, bias, W["sinks"], plans=flow.plans(n("swa_fwd")), name=n("swa_fwd"))
    flow.done(n("swa_fwd"), got)
    mix = _cat([o_a, o_b, o_c]).astype(BF16)
    att, x1 = matmul_res(two(mix), W["w_out"], two(x3), md["gate1"], S, name=n("out_proj"))
    x1 = three(x1)
    h2 = rms_fwd(x1, 0, D, W["n2"], md["scale2"], md["shift2"], name=n("norm2"))
    up = three(matmul(two(h2), W["w_up"], tm=1024, tn=1408, name=n("up_proj")))
    a = conv_gate_fwd(up, W["conv_w"], W["conv_b"], name=n("conv_gate"))
    yd, x2 = matmul_res(two(a), W["w_down"], two(x1), md["gate2"], S, name=n("down_proj"))
    saved = dict(x=x3, h=h, proj=proj, rt_a=rt_a, cqn=cqn, ckvn=ckvn, qb=qb, kvb=kvb, q_m=q_m, k_m=k_m, o_b=o_b, lse_b=lse_b,
                 q_c=q_c, k_c=k_c, o_c=o_c, lse_c=lse_c, mix=mix, att=three(att), x1=x1, h2=h2, up=up, a=a, yd=three(yd))
    return three(x2), saved


def _layer_bwd(dx2, sv, md, W, tabs, bias, tag, flow=_NoFlow()):
    Bl, S, D = dx2.shape
    T = Bl * S
    n = lambda s: f"{s}_{tag}"
    two = lambda a: a.reshape(T, a.shape[-1])
    three = lambda a: a.reshape(Bl, S, a.shape[-1])
    g = {}
    dyb, dgate2 = gate_bwd(dx2, sv["yd"], md["gate2"], name=n("gate2_bwd"))
    da = three(matmul(two(dyb), W["w_down"], tb=True, tm=1024, tn=1408, name=n("down_dx")))
    g["w_down"] = matmul(two(sv["a"]), two(dyb), ta=True, tm=256, tn=1024, name=n("down_dw"))
    dup, dcw = conv_gate_bwd(sv["up"], W["conv_w"], W["conv_b"], da, name=n("conv_gate_bwd"))
    dh2 = three(matmul(two(dup), W["w_up"], tb=True, tn=1024, name=n("up_dx")))
    g["w_up"] = matmul(two(sv["h2"]), two(dup), ta=True, tn=1408, name=n("up_dw"))
    dx1, dn2, dsc2, dsh2 = rms_bwd(sv["x1"], 0, D, dh2, W["n2"], md["scale2"], dx2, name=n("norm2_bwd"))
    dmo, dgate1 = gate_bwd(dx1, sv["att"], md["gate1"], name=n("gate1_bwd"))
    dmix = three(matmul(two(dmo), W["w_out"], tb=True, tn=1024, out_dtype=BF16, name=n("out_dx")))
    g["w_out"] = matmul(two(sv["mix"]), two(dmo), ta=True, tn=1024, name=n("out_dw"))
    proj = sv["proj"]
    for k in ("w_down", "w_up", "w_out"):
        flow.add((tag, k), g[k])
    (dq_a, dk_a, dv_a), got = sb_attn_bwd(proj, sv["rt_a"], dmix[:, :, 0:256], plans=flow.plans(n("sb_bwd")), name=n("sb_bwd"))
    flow.done(n("sb_bwd"), got)
    dq_m, dk_m, dv_b = mla_attn_bwd(sv["q_m"], sv["k_m"], sv["kvb"], 6, sv["o_b"], sv["lse_b"], dmix[:, :, 256:640], name=n("mla_bwd"))
    dqb, dqn = rope_norm_bwd(sv["qb"], 6, dq_m, W["qn_g"], tabs, name=n("q_rope_bwd"))
    dkn_x, dkn, dslab = rope_norm_bwd(sv["kvb"], 6, dk_m, W["kn_g"], tabs, (proj, P_SLAB // LANES), name=n("k_rope_bwd"))
    dkvb = _cat([dkn_x, dv_b]).astype(BF16)
    dckvn = three(matmul(two(dkvb), W["w_ukv"], tb=True, tm=1024, name=n("ukv_dx")))
    g["w_ukv"] = matmul(two(sv["ckvn"]), two(dkvb), ta=True, tn=1152, name=n("ukv_dw"))
    dcqn = three(matmul(two(dqb), W["w_uq"], tb=True, tm=1024, name=n("uq_dx")))
    g["w_uq"] = matmul(two(sv["cqn"]), two(dqb), ta=True, tn=768, name=n("uq_dw"))
    dcq, dcq_g = rms_bwd(proj, P_CQ // 256, 256, dcqn, W["cq_g"], name=n("cq_norm_bwd"))
    dckv, dckv_g = rms_bwd(proj, P_CKV // LANES, LANES, dckvn, W["ckv_g"], name=n("ckv_norm_bwd"))
    dq_c, dk_c, dv_c, dbias, dsink = swa_attn_bwd(sv["q_c"], sv["k_c"], proj, bias, W["sinks"], sv["o_c"], sv["lse_c"], dmix[:, :, 640:1024], name=n("swa_bwd"))
    dswq, dswq_g = pair_rms_bwd(proj, P_SWQ // LANES, 3, dq_c, W["swq_g"], name=n("swq_norm_bwd"))
    dswk, dswk_g = pair_rms_bwd(proj, P_SWK // LANES, 1, dk_c, W["swk_g"], name=n("swk_norm_bwd"))
    dproj = _cat([dq_a, dk_a, dv_a, dcq, dckv, dslab, dswq, dswk, dv_c]).astype(BF16)
    dh = three(matmul(two(dproj), W["w_in"], tb=True, tn=1024, name=n("in_dx")))
    g["w_in"] = matmul(two(sv["h"]), two(dproj), ta=True, tn=1920, tk=2048, name=n("in_dw"))
    dx, dn1, dsc1, dsh1 = rms_bwd(sv["x"], 0, D, dh, W["n1"], md["scale1"], dx1, name=n("norm1_bwd"))
    small = dict(n1=dn1, n2=dn2, cq_g=dcq_g, ckv_g=dckv_g, qn_g=dqn, kn_g=dkn, swq_g=dswq_g, swk_g=dswk_g, conv=dcw)
    dmods = _cat([dsh1, dsc1, dgate1, dsh2, dsc2, dgate2]).reshape(Bl, 6 * D)
    for k in ("w_ukv", "w_uq", "w_in"):
        flow.add((tag, k), g[k])
    return dx, g, small, dmods, dbias, dsink


BIG = ("w_in", "w_uq", "w_ukv", "w_out", "w_up", "w_down")
ROW_SHARDED = ("w_out", "w_down")
PREP = dict(w_in=_prep_w_in, w_uq=_prep_w_uq, w_ukv=_prep_w_ukv, w_out=_prep_w_out, w_up=_up_perm, w_down=lambda w: w)
UNPREP = dict(w_in=_unprep_w_in, w_uq=_unprep_w_uq, w_ukv=_unprep_w_ukv, w_out=_unprep_w_out, w_up=_up_perm, w_down=lambda w: w)
NCHIPS = 4


def _local_step(x, target, positions, mods, Wl, rel_flat, fwd_flow=_NoFlow(), bwd_flow=_NoFlow()):
    Bl, S, D = x.shape
    L = len(Wl)
    tabs = _rope_tables(positions)
    bucket = _bucket_table()
    bias = swa_bias(rel_flat, bucket, name="swa_bias")
    mds = []
    for l in range(L):
        parts = [mods[l, :, D * k:D * (k + 1)].reshape(Bl, 1, D) for k in range(6)]
        mds.append(dict(zip(("shift1", "scale1", "gate1", "shift2", "scale2", "gate2"), parts)))
    saved = []
    h = x
    for l in range(L):
        h, sv = _layer_fwd(h, mds[l], Wl[l], tabs, bias, f"l{l}", fwd_flow)
        saved.append(sv)
    dy, loss = loss_grad(h, target, name="loss")
    grads, smalls, dmods, dbiases, dsinks = [None] * L, [None] * L, [None] * L, [None] * L, [None] * L
    for l in reversed(range(L)):
        dy, grads[l], smalls[l], dmods[l], dbiases[l], dsinks[l] = _layer_bwd(dy, saved[l], mds[l], Wl[l], tabs, bias, f"l{l}", bwd_flow)
    drel = swa_bias_bwd(_cat(dbiases, axis=0), bucket, name="swa_bias_bwd")
    return loss, dy, grads, smalls, dmods, dsinks, drel


ATT = ("w_in", "w_uq", "w_ukv", "w_out")
FFN = ("w_up", "w_down")
GATHER_STAGES = {
    "sb_fwd_l0": ([("l0", k) for k in ("w_out",) + FFN], []),
    "mla_fwd_l0": ([("l1", k) for k in ATT + ("w_up",)], [("l0", k) for k in ("w_out",) + FFN]),
    "swa_fwd_l0": ([("l1", "w_down")], [("l1", k) for k in ATT + ("w_up",)]),
    "sb_fwd_l1": ([], [("l1", "w_down")]),
}
SCATTER_STAGES = {
    "sb_bwd_l1": [("l1", k) for k in FFN],
    "sb_bwd_l0": [("l1", k) for k in ATT] + [("l0", k) for k in FFN + ("w_out",)],
}


class _GatherFlow:
    def __init__(self, shards, chip):
        self.shards, self.chip, self.ici, self.d2d, self.pending = shards, chip, {}, {}, {}

    def early(self, keys):
        ici, = run_plans([plan_gather_ici([self.shards[k] for k in keys])], name="gather_early_ici")
        d2d, = run_plans([plan_gather_d2d(ici)], name="gather_early_d2d")
        self.d2d.update(zip(keys, d2d))

    def plans(self, tag):
        ici_keys, d2d_keys = GATHER_STAGES.get(tag, ([], []))
        plans = []
        if d2d_keys:
            plans.append(plan_gather_d2d([self.ici[k] for k in d2d_keys]))
        if ici_keys:
            plans.append(plan_gather_ici([self.shards[k] for k in ici_keys]))
        self.pending[tag] = (ici_keys, d2d_keys)
        return plans

    def done(self, tag, outs):
        ici_keys, d2d_keys = self.pending.pop(tag, ([], []))
        outs = list(outs)
        if d2d_keys:
            self.d2d.update(zip(d2d_keys, outs.pop(0)))
        if ici_keys:
            self.ici.update(zip(ici_keys, outs.pop(0)))

    def weight(self, key):
        k = key[1]
        own = self.shards[key]
        r, cc = own.shape
        w4 = lax.dynamic_update_slice(self.d2d[key], own[None], (self.chip, 0, 0))
        fw = w4.reshape(NCHIPS * r, cc) if k in ROW_SHARDED else jnp.transpose(w4, (1, 0, 2)).reshape(r, NCHIPS * cc)
        return PREP[k](fw)


class _LayerWeights(dict):
    def __init__(self, small, flow, tag):
        super().__init__(small)
        self.flow, self.tag = flow, tag

    def __missing__(self, k):
        self[k] = self.flow.weight((self.tag, k))
        return self[k]


class _ScatterFlow:
    def __init__(self, shapes, sel, c_arr):
        self.shapes, self.sel, self.c_arr = shapes, sel, c_arr
        self.g, self.pairs, self.landed, self.pending = {}, {}, {}, {}

    def add(self, key, g):
        self.g[key] = g

    def _pairs(self, keys, label):
        g4s = []
        for key in keys:
            k = key[1]
            r, cc = self.shapes[k]
            gk = UNPREP[k](self.g[key])
            g4 = gk.reshape(NCHIPS, r, cc) if k in ROW_SHARDED else jnp.transpose(gk.reshape(r, NCHIPS, cc), (1, 0, 2))
            g4s.append(g4.astype(BF16))
        theirs, = run_plans([plan_swap_halves(g4s)], name=f"rs_swap_{label}")
        pairs = [pair_add_half(g4, th, self.c_arr, name=f"rs_pair_add_{key[1]}_{key[0]}") for key, g4, th in zip(keys, g4s, theirs)]
        self.pairs.update(zip(keys, pairs))
        return pairs

    def plans(self, tag):
        keys = SCATTER_STAGES.get(tag, [])
        self.pending[tag] = keys
        return [plan_scatter_ici(self._pairs(keys, tag))] if keys else []

    def done(self, tag, outs):
        keys = self.pending.pop(tag, [])
        if keys:
            self.landed.update(zip(keys, outs[0]))

    def finish(self):
        rest = [key for key in self.g if key not in self.pairs]
        if rest:
            landed, = run_plans([plan_scatter_ici(self._pairs(rest, "rest"))], name="rs_scatter_rest")
            self.landed.update(zip(rest, landed))
        keys = list(self.pairs)
        fulls = [chip_sum_into(self.landed[key], self.pairs[key], self.sel, name=f"rs_chip_sum_{key[1]}_{key[0]}") for key in keys]
        joined, = run_plans([plan_join_halves(fulls)], name="rs_join_halves")
        return dict(zip(keys, joined))


WEIGHTS = ("rel_table", "norm1_g", "norm2_g", "w_ada", "b_ada", "w_in", "mla_cq_g", "w_uq", "mla_ckv_g", "w_ukv", "mla_qn_g", "mla_kn_g",
           "sw_qn_g", "sw_kn_g", "sw_sinks", "w_out", "w_up", "conv_w", "conv_b", "w_down")
SMALL = tuple(n for n in WEIGHTS if n not in BIG + ("w_ada",))


def kernel(x, c, positions, rel_table, norm1_g, norm2_g, w_ada, b_ada, w_in, mla_cq_g, w_uq, mla_ckv_g, w_ukv, mla_qn_g, mla_kn_g, sw_qn_g, sw_kn_g, sw_sinks, w_out, w_up, conv_w, conv_b, w_down, loss_target, m_rel_table, m_norm1_g, m_norm2_g, m_w_ada, m_b_ada, m_w_in, m_mla_cq_g, m_w_uq, m_mla_ckv_g, m_w_ukv, m_mla_qn_g, m_mla_kn_g, m_sw_qn_g, m_sw_kn_g, m_sw_sinks, m_w_out, m_w_up, m_conv_w, m_conv_b, m_w_down, v_rel_table, v_norm1_g, v_norm2_g, v_w_ada, v_b_ada, v_w_in, v_mla_cq_g, v_w_uq, v_mla_ckv_g, v_w_ukv, v_mla_qn_g, v_mla_kn_g, v_sw_qn_g, v_sw_kn_g, v_sw_sinks, v_w_out, v_w_up, v_conv_w, v_conv_b, v_w_down):
    w = dict(rel_table=rel_table, norm1_g=norm1_g, norm2_g=norm2_g, w_ada=w_ada, b_ada=b_ada, w_in=w_in, mla_cq_g=mla_cq_g, w_uq=w_uq,
             mla_ckv_g=mla_ckv_g, w_ukv=w_ukv, mla_qn_g=mla_qn_g, mla_kn_g=mla_kn_g, sw_qn_g=sw_qn_g, sw_kn_g=sw_kn_g, sw_sinks=sw_sinks,
             w_out=w_out, w_up=w_up, conv_w=conv_w, conv_b=conv_b, w_down=w_down)
    m = dict(rel_table=m_rel_table, norm1_g=m_norm1_g, norm2_g=m_norm2_g, w_ada=m_w_ada, b_ada=m_b_ada, w_in=m_w_in, mla_cq_g=m_mla_cq_g,
             w_uq=m_w_uq, mla_ckv_g=m_mla_ckv_g, w_ukv=m_w_ukv, mla_qn_g=m_mla_qn_g, mla_kn_g=m_mla_kn_g, sw_qn_g=m_sw_qn_g,
             sw_kn_g=m_sw_kn_g, sw_sinks=m_sw_sinks, w_out=m_w_out, w_up=m_w_up, conv_w=m_conv_w, conv_b=m_conv_b, w_down=m_w_down)
    v = dict(rel_table=v_rel_table, norm1_g=v_norm1_g, norm2_g=v_norm2_g, w_ada=v_w_ada, b_ada=v_b_ada, w_in=v_w_in, mla_cq_g=v_mla_cq_g,
             w_uq=v_w_uq, mla_ckv_g=v_mla_ckv_g, w_ukv=v_w_ukv, mla_qn_g=v_mla_qn_g, mla_kn_g=v_mla_kn_g, sw_qn_g=v_sw_qn_g,
             sw_kn_g=v_sw_kn_g, sw_sinks=v_sw_sinks, w_out=v_w_out, w_up=v_w_up, conv_w=v_conv_w, conv_b=v_conv_b, w_down=v_w_down)
    Bl, S, D = x.shape
    L = norm1_g.shape[0]
    xi, yi, ci = _me()
    chip = 2 * xi + yi
    dev = 4 * xi + 2 * yi + ci
    ndev = 2 * NCHIPS

    shapes = {k: w[k].shape[1:] for k in BIG}
    shards = {(f"l{l}", k): w[k][l].astype(BF16) for l in range(L) for k in BIG}
    gflow = _GatherFlow(shards, chip)
    gflow.early([("l0", k) for k in ("w_in", "w_uq", "w_ukv")])

    cw_cols = conv_w.shape[2]
    c_got, cw_got = allgather8([c, conv_w.reshape(L * 3, cw_cols)], name="gather_cond")
    c_all = c_got.reshape(ndev * Bl, D)
    conv_full = jnp.transpose(cw_got[0::2].reshape(NCHIPS, L, 3, cw_cols), (1, 2, 0, 3)).reshape(L, 3, NCHIPS * cw_cols)
    E = w_ada.shape[2]
    b_cols = lax.dynamic_slice(b_ada, (0, chip * E), (L, E)).reshape(L, 1, E)
    mods_cols = mods_matmul(c_all, w_ada, b_cols, name="mods")
    mods_all, = allgather8([mods_cols.reshape(L * ndev * Bl, E)], name="gather_mods")
    mods_all = jnp.transpose(mods_all[0::2].reshape(NCHIPS, L, ndev * Bl, E), (1, 2, 0, 3)).reshape(L, ndev * Bl, NCHIPS * E)
    mods = lax.dynamic_slice(mods_all, (0, dev * Bl, 0), (L, Bl, NCHIPS * E))

    Wl = []
    for l in range(L):
        Wd = _small_params({k: w[k][l] for k in SMALL if k not in ("rel_table", "b_ada", "conv_w")})
        Wd["conv_w"] = _up_perm(conv_full[l])
        Wl.append(_LayerWeights(Wd, gflow, f"l{l}"))

    sflow = _ScatterFlow(shapes, jnp.stack([chip, ci]).astype(jnp.int32), ci.reshape(1).astype(jnp.int32))
    loss, dx, _, smalls, dmods, dsinks, drel = _local_step(x, loss_target, positions, mods, Wl, rel_table.reshape(-1), gflow, sflow)
    reduced = sflow.finish()
    grad = {k: jnp.stack([reduced[(f"l{l}", k)] for l in range(L)]) for k in BIG}

    vec_names = ("n1", "n2", "cq_g", "ckv_g", "qn_g", "kn_g", "swq_g", "swk_g")
    vecs = _cat([_cat([smalls[l][k] for k in vec_names], axis=1) for l in range(L)], axis=0)
    convs = _cat([smalls[l]["conv"][0:4] for l in range(L)], axis=0)
    dm = jnp.stack(dmods, axis=1).reshape(Bl * L, 6 * D)
    dsk = jnp.stack(dsinks, axis=1).reshape(Bl * L * 6, LANES)
    got = allgather8([vecs, convs, drel, loss, dm, dsk], name="gather_small_grads")
    seq = lambda a, rows: a.reshape(ndev * Bl, rows, a.shape[-1])
    vec_s, conv_s, rel_s, loss_s, dm_s, dsk_s = sum_small(list(got[:4]) + [seq(got[4], L), seq(got[5], L * 6)], name="sum_small_grads")
    dm_all = jnp.transpose(seq(got[4], L), (1, 0, 2))
    grad["w_ada"] = ada_grad(c_all, lax.dynamic_slice(dm_all, (0, 0, chip * E), (L, ndev * Bl, E)), name="ada_grad")
    grad["b_ada"] = dm_s
    grad["sw_sinks"] = jnp.transpose(dsk_s.reshape(L, 3, 2, LANES)[:, :, :, 0], (0, 2, 1)).reshape(L, 6)
    grad["rel_table"] = rel_s[:6, :REL_BUCKETS].T
    off = 0
    for k, name_, keep in zip(vec_names, ("norm1_g", "norm2_g", "mla_cq_g", "mla_ckv_g", "mla_qn_g", "mla_kn_g", "sw_qn_g", "sw_kn_g"),
                              (D, D, 256, LANES, MLA_QK, MLA_QK, HEAD, HEAD)):
        grad[name_] = vec_s[:, off:off + keep]
        off += smalls[0][k].shape[1]
    conv = _up_perm(conv_s.reshape(L, 4, 2 * D_FF))
    grad["conv_w"] = lax.dynamic_slice(conv[:, 0:3], (0, 0, chip * cw_cols), (L, 3, cw_cols))
    grad["conv_b"] = conv[:, 3]
    loss_out = loss_s[0, 0]

    delta, new_m, new_v = {}, {}, {}
    for k in BIG + ("w_ada",):
        delta[k], new_m[k], new_v[k] = adamw(w[k], grad[k], m[k], v[k], name=f"adamw_{k}")
    outs = adamw_small(*[[src[k] for k in SMALL] for src in (w, grad, m, v)], name="adamw_small")
    for dst, o in zip((delta, new_m, new_v), outs):
        dst.update(dict(zip(SMALL, o)))
    return (loss_out, dx, *[grad[k] for k in WEIGHTS], *[delta[k] for k in WEIGHTS], *[new_m[k] for k in WEIGHTS], *[new_v[k] for k in WEIGHTS])
```

```python
import math

import jax
import jax.numpy as jnp
from jax import lax
from jax.experimental import pallas as pl
from jax.experimental.pallas import tpu as pltpu

F32 = jnp.float32
BF16 = jnp.bfloat16
MESH = pl.DeviceIdType.MESH

EPS = 1e-6
NEG = -1e30
HEAD = 64
LANES = 128
MLA_QK = 96
ROPE_THETA = 10000.0
REL_BUCKETS = 32
REL_MAX_DIST = 128
WINDOW = 128
D_FF = 2816
ADAM_LR, ADAM_B1, ADAM_B2, ADAM_EPS, ADAM_WD, ADAM_STEP = 0.001, 0.9, 0.999, 1e-08, 0.01, 10

VMEM_LIMIT = 56 * 1024 * 1024
STRIP = 32

P_SBQ, P_SBK, P_SBV, P_CQ, P_CKV, P_SLAB, P_SWQ, P_SWK, P_SWV, P_END = 0, 256, 512, 768, 1024, 1152, 1280, 1664, 1792, 1920


def _cp(*sem):
    return pltpu.CompilerParams(dimension_semantics=sem, vmem_limit_bytes=VMEM_LIMIT)


def _dot(a, b):
    return jnp.dot(a, b, preferred_element_type=F32)


def _dot_nt(a, b):
    return lax.dot_general(a, b, (((1,), (1,)), ((), ())), preferred_element_type=F32)


def _dot_tn(a, b):
    return lax.dot_general(a, b, (((0,), (0,)), ((), ())), preferred_element_type=F32)


def _lane_masks():
    lane = lax.broadcasted_iota(jnp.int32, (1, LANES), 1)
    return (lane < HEAD, lane >= HEAD)


def _tile(n, cap, align=128):
    if n <= cap:
        return n
    t = cap - cap % align
    while t >= align:
        if n % t == 0:
            return t
        t -= align
    return n


def matmul(a, b, *, ta=False, tb=False, out_dtype=F32, tm=512, tn=512, tk=8192, name):
    M, K = (a.shape[1], a.shape[0]) if ta else a.shape
    N = b.shape[0] if tb else b.shape[1]
    tm, tn, tk = _tile(M, tm), _tile(N, tn), _tile(K, tk)
    nk = K // tk

    def body(a_ref, b_ref, o_ref, *scratch):
        av = a_ref[...].astype(BF16)
        bv = b_ref[...].astype(BF16)
        if ta:
            part = _dot_tn(av, bv)
        elif tb:
            part = _dot_nt(av, bv)
        else:
            part = _dot(av, bv)
        if nk == 1:
            o_ref[...] = part.astype(out_dtype)
        else:
            acc_ref, = scratch
            k = pl.program_id(2)

            @pl.when(k == 0)
            def _():
                acc_ref[...] = part

            @pl.when(k > 0)
            def _():
                acc_ref[...] += part

            @pl.when(k == nk - 1)
            def _():
                o_ref[...] = acc_ref[...].astype(out_dtype)

    n_outer = nk == 1 and tn * b.dtype.itemsize > tm * a.dtype.itemsize
    ij = (lambda p, q: (q, p)) if n_outer else (lambda p, q: (p, q))
    a_map = (lambda p, q, k: (k, ij(p, q)[0])) if ta else (lambda p, q, k: (ij(p, q)[0], k))
    b_map = (lambda p, q, k: (ij(p, q)[1], k)) if tb else (lambda p, q, k: (k, ij(p, q)[1]))
    grid = (N // tn, M // tm, nk) if n_outer else (M // tm, N // tn, nk)
    return pl.pallas_call(
        body, name=name, grid=grid,
        in_specs=[pl.BlockSpec((tk, tm) if ta else (tm, tk), a_map), pl.BlockSpec((tn, tk) if tb else (tk, tn), b_map)],
        out_specs=pl.BlockSpec((tm, tn), lambda p, q, k: ij(p, q)),
        out_shape=jax.ShapeDtypeStruct((M, N), out_dtype),
        scratch_shapes=[] if nk == 1 else [pltpu.VMEM((tm, tn), F32)],
        compiler_params=_cp("parallel", "parallel", "arbitrary"),
    )(a, b)


def matmul_res(a, b, res, gate, seq, *, tm=512, tn=1024, name):
    M, K = a.shape
    N = b.shape[1]
    tm, tn = _tile(min(M, seq), tm), _tile(N, tn)
    per_seq = seq // tm

    def body(a_ref, b_ref, r_ref, g_ref, y_ref, x_ref):
        y = _dot(a_ref[...].astype(BF16), b_ref[...].astype(BF16))
        y_ref[...] = y
        x_ref[...] = r_ref[...] + g_ref[...] * y

    out = jax.ShapeDtypeStruct((M, N), F32)
    return pl.pallas_call(
        body, name=name, grid=(M // tm, N // tn),
        in_specs=[pl.BlockSpec((tm, K), lambda i, j: (i, 0)), pl.BlockSpec((K, tn), lambda i, j: (0, j)),
                  pl.BlockSpec((tm, tn), lambda i, j: (i, j)), pl.BlockSpec((None, 1, tn), lambda i, j: (lax.div(i, jnp.int32(per_seq)), 0, j))],
        out_specs=[pl.BlockSpec((tm, tn), lambda i, j: (i, j))] * 2,
        out_shape=[out, out], compiler_params=_cp("parallel", "parallel"),
    )(a, b, res, gate)


def rms_fwd(x3, blk, W, g, sc=None, sh=None, *, tm=512, name):
    Bl, S, _ = x3.shape
    tm = min(tm, S)
    mod = sc is not None

    def body(x_ref, g_ref, *rest):
        o_ref = rest[-1]
        x = x_ref[...]
        r = lax.rsqrt(jnp.mean(x * x, axis=-1, keepdims=True) + EPS)
        y = x * r * g_ref[...]
        if mod:
            y = y * (1.0 + rest[0][...]) + rest[1][...]
        o_ref[...] = y.astype(BF16)

    vec = pl.BlockSpec((None, 1, W), lambda b, s: (b, 0, 0))
    return pl.pallas_call(
        body, name=name, grid=(Bl, S // tm),
        in_specs=[pl.BlockSpec((None, tm, W), lambda b, s: (b, s, blk)), pl.BlockSpec((1, W), lambda b, s: (0, 0))] + ([vec, vec] if mod else []),
        out_specs=pl.BlockSpec((None, tm, W), lambda b, s: (b, s, 0)),
        out_shape=jax.ShapeDtypeStruct((Bl, S, W), BF16),
        compiler_params=_cp("parallel", "parallel"),
    )(x3, g, *([sc, sh] if mod else []))


def rms_bwd(x3, blk, W, dy3, g, sc=None, dres3=None, *, tm=256, name):
    Bl, S, _ = x3.shape
    tm = min(tm, S)
    mod = sc is not None
    res = dres3 is not None

    def body(*refs):
        x_ref, dy_ref, g_ref = refs[:3]
        k = 3
        sc_ref = dr_ref = None
        if mod:
            sc_ref = refs[k]
            k += 1
        if res:
            dr_ref = refs[k]
            k += 1
        dx_ref, dg_ref = refs[k], refs[k + 1]
        b, s = pl.program_id(0), pl.program_id(1)
        x = x_ref[...]
        dy = dy_ref[...].astype(F32)
        g = g_ref[...]
        r = lax.rsqrt(jnp.mean(x * x, axis=-1, keepdims=True) + EPS)
        n = x * r
        if mod:
            dsc_ref, dsh_ref = refs[k + 2], refs[k + 3]
            one_sc = 1.0 + sc_ref[...]

            @pl.when(s == 0)
            def _():
                dsc_ref[...] = jnp.zeros_like(dsc_ref)
                dsh_ref[...] = jnp.zeros_like(dsh_ref)

            dsh_ref[...] += jnp.sum(dy, axis=0, keepdims=True)
            dsc_ref[...] += jnp.sum(dy * n * g, axis=0, keepdims=True)
            dyn = dy * one_sc
        else:
            dyn = dy

        @pl.when((b == 0) & (s == 0))
        def _():
            dg_ref[...] = jnp.zeros_like(dg_ref)

        dg_ref[...] += jnp.sum(dyn * n, axis=0, keepdims=True)
        dn = dyn * g
        dx = r * (dn - n * jnp.mean(dn * n, axis=-1, keepdims=True))
        if res:
            dx = dx + dr_ref[...]
        dx_ref[...] = dx

    blkspec = pl.BlockSpec((None, tm, W), lambda b, s: (b, s, 0))
    vec = pl.BlockSpec((None, 1, W), lambda b, s: (b, 0, 0))
    row = pl.BlockSpec((1, W), lambda b, s: (0, 0))
    in_specs = [pl.BlockSpec((None, tm, W), lambda b, s: (b, s, blk)), blkspec, row] + ([vec] if mod else []) + ([blkspec] if res else [])
    out_specs = [blkspec, row] + ([vec, vec] if mod else [])
    out_shape = [jax.ShapeDtypeStruct((Bl, S, W), F32), jax.ShapeDtypeStruct((1, W), F32)]
    if mod:
        out_shape += [jax.ShapeDtypeStruct((Bl, 1, W), F32)] * 2
    args = [x3, dy3, g] + ([sc] if mod else []) + ([dres3] if res else [])
    return pl.pallas_call(
        body, name=name, grid=(Bl, S // tm), in_specs=in_specs, out_specs=out_specs, out_shape=out_shape,
        compiler_params=_cp("arbitrary", "arbitrary"),
    )(*args)


def pair_rms_fwd(x3, blk0, npairs, g2, *, tm=1024, name):
    Bl, S, _ = x3.shape
    tm = min(tm, S)

    def body(x_ref, g_ref, o_ref):
        lo, hi = _lane_masks()
        x = x_ref[...]
        xx = x * x
        s0 = jnp.sum(jnp.where(lo, xx, 0.0), axis=-1, keepdims=True)
        s1 = jnp.sum(jnp.where(hi, xx, 0.0), axis=-1, keepdims=True)
        r = jnp.where(lo, lax.rsqrt(s0 / HEAD + EPS), lax.rsqrt(s1 / HEAD + EPS))
        o_ref[...] = (x * r * g_ref[...]).astype(BF16)

    return pl.pallas_call(
        body, name=name, grid=(Bl, S // tm, npairs),
        in_specs=[pl.BlockSpec((None, tm, LANES), lambda b, s, p: (b, s, blk0 + p)), pl.BlockSpec((1, LANES), lambda b, s, p: (0, 0))],
        out_specs=pl.BlockSpec((None, tm, LANES), lambda b, s, p: (b, s, p)),
        out_shape=jax.ShapeDtypeStruct((Bl, S, LANES * npairs), BF16),
        compiler_params=_cp("parallel", "parallel", "parallel"),
    )(x3, g2)


def pair_rms_bwd(x3, blk0, npairs, dy3, g2, *, tm=1024, name):
    Bl, S, _ = x3.shape
    tm = min(tm, S)

    def body(x_ref, dy_ref, g_ref, dx_ref, dg_ref):
        lo, hi = _lane_masks()
        first = (pl.program_id(0) == 0) & (pl.program_id(1) == 0) & (pl.program_id(2) == 0)
        x = x_ref[...]
        dy = dy_ref[...]
        xx = x * x
        s0 = jnp.sum(jnp.where(lo, xx, 0.0), axis=-1, keepdims=True)
        s1 = jnp.sum(jnp.where(hi, xx, 0.0), axis=-1, keepdims=True)
        r = jnp.where(lo, lax.rsqrt(s0 / HEAD + EPS), lax.rsqrt(s1 / HEAD + EPS))
        n = x * r

        @pl.when(first)
        def _():
            dg_ref[...] = jnp.zeros_like(dg_ref)

        part = jnp.sum(dy * n, axis=0, keepdims=True)
        dg_ref[...] += part + pltpu.roll(part, HEAD, 1)
        dn = dy * g_ref[...]
        t = dn * n
        m0 = jnp.sum(jnp.where(lo, t, 0.0), axis=-1, keepdims=True)
        m1 = jnp.sum(jnp.where(hi, t, 0.0), axis=-1, keepdims=True)
        dx_ref[...] = r * (dn - n * (jnp.where(lo, m0, m1) / HEAD))

    return pl.pallas_call(
        body, name=name, grid=(Bl, S // tm, npairs),
        in_specs=[pl.BlockSpec((None, tm, LANES), lambda b, s, p: (b, s, blk0 + p)), pl.BlockSpec((None, tm, LANES), lambda b, s, p: (b, s, p)),
                  pl.BlockSpec((1, LANES), lambda b, s, p: (0, 0))],
        out_specs=[pl.BlockSpec((None, tm, LANES), lambda b, s, p: (b, s, p)), pl.BlockSpec((1, LANES), lambda b, s, p: (0, 0))],
        out_shape=[jax.ShapeDtypeStruct((Bl, S, LANES * npairs), F32), jax.ShapeDtypeStruct((1, LANES), F32)],
        compiler_params=_cp("arbitrary", "arbitrary", "arbitrary"),
    )(x3, dy3, g2)


def _rot(y, cos_t, sin_a, sin_b):
    return y * cos_t + pltpu.roll(y, LANES - 16, 1) * sin_a + pltpu.roll(y, 16, 1) * sin_b


def _rot_t(d, cos_t, sin_a, sin_b):
    return d * cos_t + pltpu.roll(d * sin_a, 16, 1) + pltpu.roll(d * sin_b, LANES - 16, 1)


def rope_norm_fwd(x3, nheads, g, tabs, slab=None, *, tm=1024, name):
    Bl, S, _ = x3.shape
    tm = min(tm, S)
    has_slab = slab is not None

    def body(*refs):
        x_ref, g_ref, c_ref, sa_ref, sb_ref = refs[:5]
        o_ref = refs[-1]
        x = x_ref[...]
        if has_slab:
            x = x + refs[5][...]
        r = lax.rsqrt(jnp.sum(x * x, axis=-1, keepdims=True) / MLA_QK + EPS)
        o_ref[...] = _rot(x * r * g_ref[...], c_ref[...], sa_ref[...], sb_ref[...]).astype(BF16)

    head = pl.BlockSpec((None, tm, LANES), lambda b, s, h: (b, s, h))
    tab = pl.BlockSpec((None, tm, LANES), lambda b, s, h: (b, s, 0))
    in_specs = [head, pl.BlockSpec((1, LANES), lambda b, s, h: (0, 0)), tab, tab, tab]
    args = [x3, g, *tabs]
    if has_slab:
        sblk = slab[1]
        in_specs.append(pl.BlockSpec((None, tm, LANES), lambda b, s, h: (b, s, sblk)))
        args.append(slab[0])
    return pl.pallas_call(
        body, name=name, grid=(Bl, S // tm, nheads), in_specs=in_specs, out_specs=head,
        out_shape=jax.ShapeDtypeStruct((Bl, S, LANES * nheads), BF16),
        compiler_params=_cp("parallel", "parallel", "parallel"),
    )(*args)


def rope_norm_bwd(x3, nheads, dy3, g, tabs, slab=None, *, tm=1024, name):
    Bl, S, _ = x3.shape
    tm = min(tm, S)
    has_slab = slab is not None

    def body(*refs):
        x_ref, dy_ref, g_ref, c_ref, sa_ref, sb_ref = refs[:6]
        k = 7 if has_slab else 6
        dx_ref, dg_ref = refs[k], refs[k + 1]
        h = pl.program_id(2)
        first = (pl.program_id(0) == 0) & (pl.program_id(1) == 0) & (h == 0)
        x = x_ref[...]
        if has_slab:
            x = x + refs[6][...]
        g = g_ref[...]
        r = lax.rsqrt(jnp.sum(x * x, axis=-1, keepdims=True) / MLA_QK + EPS)
        n = x * r
        d = _rot_t(dy_ref[...], c_ref[...], sa_ref[...], sb_ref[...])

        @pl.when(first)
        def _():
            dg_ref[...] = jnp.zeros_like(dg_ref)

        dg_ref[...] += jnp.sum(d * n, axis=0, keepdims=True)
        dn = d * g
        dx = r * (dn - n * (jnp.sum(dn * n, axis=-1, keepdims=True) / MLA_QK))
        dx_ref[...] = dx
        if has_slab:
            ds_ref = refs[k + 2]

            @pl.when(h == 0)
            def _():
                ds_ref[...] = dx

            @pl.when(h > 0)
            def _():
                ds_ref[...] += dx

    head = pl.BlockSpec((None, tm, LANES), lambda b, s, h: (b, s, h))
    tab = pl.BlockSpec((None, tm, LANES), lambda b, s, h: (b, s, 0))
    row = pl.BlockSpec((1, LANES), lambda b, s, h: (0, 0))
    in_specs = [head, head, row, tab, tab, tab]
    args = [x3, dy3, g, *tabs]
    out_specs = [head, row]
    out_shape = [jax.ShapeDtypeStruct((Bl, S, LANES * nheads), F32), jax.ShapeDtypeStruct((1, LANES), F32)]
    if has_slab:
        sblk = slab[1]
        in_specs.append(pl.BlockSpec((None, tm, LANES), lambda b, s, h: (b, s, sblk)))
        args.append(slab[0])
        out_specs.append(tab)
        out_shape.append(jax.ShapeDtypeStruct((Bl, S, LANES), F32))
    return pl.pallas_call(
        body, name=name, grid=(Bl, S // tm, nheads), in_specs=in_specs, out_specs=out_specs, out_shape=out_shape,
        compiler_params=_cp("arbitrary", "arbitrary", "arbitrary"),
    )(*args)


def _softplus(z):
    return jnp.maximum(z, 0.0) + jnp.log(1.0 + jnp.exp(-jnp.abs(z)))


def _split_dots(xs, u):
    hi = [x.astype(BF16) for x in xs]
    lo = [(x - h.astype(F32)).astype(BF16) for x, h in zip(xs, hi)]
    top = [_dot(h, u) for h in hi]
    return [t + _dot(l, u) for t, l in zip(top, lo)]


SB_BLOCK = 256
SB_QBLOCK = 512


def sb_attn_fwd(proj3, *, plans=None, name):
    Bl, S, _ = proj3.shape
    tk = min(SB_BLOCK, S)
    tq = min(SB_QBLOCK, S)
    per_q = tq // tk
    scale = HEAD ** -0.5
    qb, kb0, vb0 = P_SBQ // LANES, P_SBK // LANES, P_SBV // LANES

    def body(q_ref, k_ref, v_ref, o_ref, rt_ref):
        i = pl.program_id(2)
        masks = _lane_masks()
        lane = lax.broadcasted_iota(jnp.int32, (1, LANES), 1)
        q = q_ref[...]
        qh = [jnp.where(m, q, 0.0).astype(BF16) for m in masks]
        rr = lax.broadcasted_iota(jnp.int32, (tq, tk), 0)
        cc = lax.broadcasted_iota(jnp.int32, (tq, tk), 1)
        u = (lax.broadcasted_iota(jnp.int32, (tk, tk), 0) > lax.broadcasted_iota(jnp.int32, (tk, tk), 1)).astype(BF16)

        rt_ref[...] = jnp.zeros_like(rt_ref)

        def step(j, carry, masked):
            r0, r1, acc = carry
            off = pl.multiple_of(j * tk, tk)
            kb = k_ref[pl.ds(off, tk), :].astype(BF16)
            vb = v_ref[pl.ds(off, tk), :]
            strict = (cc + j * tk) < (rr + i * tq) if masked else None
            only = (lambda t: jnp.where(strict, t, 0.0)) if masked else (lambda t: t)
            rt_ref[...] = jnp.where(lane == j, r0, jnp.where(lane == j + HEAD, r1, rt_ref[...]))
            rs, two = [r0, r1], range(2)
            z = [_dot_nt(qh[h], kb) * scale for h in two]
            sp = [_softplus(z[h]) for h in two]
            keep = [only(-sp[h]) for h in two]
            suf = _split_dots(keep, u)
            w = [only(jnp.exp((z[h] - sp[h]) + suf[h] + rs[h])) for h in two]
            pv = [_dot(w[h].astype(BF16), jnp.where(masks[h], vb, 0.0).astype(BF16)) for h in two]
            return rs[0] + jnp.sum(keep[0], axis=1, keepdims=True), rs[1] + jnp.sum(keep[1], axis=1, keepdims=True), acc + (pv[0] + pv[1])

        zero = jnp.zeros((tq, 1), F32)
        carry = (zero, zero, jnp.zeros((tq, LANES), F32))
        for t in range(per_q):
            carry = step((i + 1) * per_q - 1 - t, carry, True)
        _, _, acc = lax.fori_loop(0, i * per_q, lambda t, c: step(i * per_q - 1 - t, c, False), carry)
        o_ref[...] = acc

    seq = lambda blk0: pl.BlockSpec((None, S, LANES), lambda b, p, i: (b, 0, blk0 + p))
    out = pl.BlockSpec((None, tq, LANES), lambda b, p, i: (b, i, p))
    shp = jax.ShapeDtypeStruct((Bl, S, 2 * LANES), F32)
    return call_with_plans(
        body, plans, name=name, grid=(Bl, 2, S // tq),
        in_specs=[pl.BlockSpec((None, tq, LANES), lambda b, p, i: (b, i, qb + p)), seq(kb0), seq(vb0)],
        out_specs=[out, out], out_shape=[shp, shp], scratch_shapes=[], args=[proj3, proj3, proj3],
        sem=("arbitrary",) * 3 if plans else ("parallel", "parallel", "arbitrary"))


def sb_attn_bwd(proj3, rt3, do3, *, plans=None, name):
    Bl, S, _ = proj3.shape
    tk = min(SB_BLOCK, S)
    tq = min(SB_QBLOCK, S)
    per_q = tq // tk
    scale = HEAD ** -0.5
    qb, kb0, vb0 = P_SBQ // LANES, P_SBK // LANES, P_SBV // LANES

    def body(q_ref, k_ref, v_ref, rt_ref, do_ref, dq_ref, dk_ref, dv_ref):
        i = pl.program_id(2)

        @pl.when(i == 0)
        def _():
            dk_ref[...] = jnp.zeros_like(dk_ref)
            dv_ref[...] = jnp.zeros_like(dv_ref)

        masks = _lane_masks()
        lane = lax.broadcasted_iota(jnp.int32, (1, LANES), 1)
        q = q_ref[...]
        qh = [jnp.where(m, q, 0.0).astype(BF16) for m in masks]
        do_b = do_ref[...].astype(BF16)
        doh = [jnp.where(m, do_b, jnp.zeros_like(do_b)) for m in masks]
        rt = rt_ref[...]
        rr = lax.broadcasted_iota(jnp.int32, (tq, tk), 0)
        cc = lax.broadcasted_iota(jnp.int32, (tq, tk), 1)
        ur = lax.broadcasted_iota(jnp.int32, (tk, tk), 0)
        uc = lax.broadcasted_iota(jnp.int32, (tk, tk), 1)
        u_suffix = (ur > uc).astype(BF16)
        u_prefix = (ur < uc).astype(BF16)

        def step(j, carry, masked):
            p0, p1, dq = carry
            off = pl.multiple_of(j * tk, tk)
            kf = k_ref[pl.ds(off, tk), :]
            kb = kf.astype(BF16)
            vb = v_ref[pl.ds(off, tk), :]
            strict = (cc + j * tk) < (rr + i * tq) if masked else None
            only = (lambda t: jnp.where(strict, t, 0.0)) if masked else (lambda t: t)
            ps, two = [p0, p1], range(2)
            r_j = [jnp.sum(jnp.where(lane == j + h * HEAD, rt, 0.0), axis=1, keepdims=True) for h in two]
            z = [_dot_nt(qh[h], kb) * scale for h in two]
            dw = [_dot_nt(doh[h], jnp.where(masks[h], vb, 0.0).astype(BF16)) for h in two]
            sp = [_softplus(z[h]) for h in two]
            keep = [only(-sp[h]) for h in two]
            suf = _split_dots(keep, u_suffix)
            w = [only(jnp.exp((z[h] - sp[h]) + suf[h] + r_j[h])) for h in two]
            g = [dw[h] * w[h] for h in two]
            pre = _split_dots(g, u_prefix)
            dzb = [(only(g[h] * jnp.exp(-sp[h]) - jnp.exp(z[h] - sp[h]) * (pre[h] + ps[h])) * scale).astype(BF16) for h in two]
            dqs = [_dot(dzb[h], jnp.where(masks[h], kf, 0.0).astype(BF16)) for h in two]
            dks = [_dot_tn(dzb[h], qh[h]) for h in two]
            dvs = [_dot_tn(w[h].astype(BF16), doh[h]) for h in two]
            dk_ref[pl.ds(off, tk), :] += dks[0] + dks[1]
            dv_ref[pl.ds(off, tk), :] += dvs[0] + dvs[1]
            return ps[0] + jnp.sum(g[0], axis=1, keepdims=True), ps[1] + jnp.sum(g[1], axis=1, keepdims=True), dq + (dqs[0] + dqs[1])

        zero = jnp.zeros((tq, 1), F32)
        carry = lax.fori_loop(0, i * per_q, lambda j, c: step(j, c, False), (zero, zero, jnp.zeros((tq, LANES), F32)))
        for t in range(per_q):
            carry = step(i * per_q + t, carry, True)
        dq_ref[...] = carry[2]

    seq_in = lambda blk0: pl.BlockSpec((None, S, LANES), lambda b, p, i: (b, 0, blk0 + p))
    blk = pl.BlockSpec((None, tq, LANES), lambda b, p, i: (b, i, p))
    seq_out = pl.BlockSpec((None, S, LANES), lambda b, p, i: (b, 0, p))
    shp = jax.ShapeDtypeStruct((Bl, S, 2 * LANES), F32)
    return call_with_plans(
        body, plans, name=name, grid=(Bl, 2, S // tq),
        in_specs=[pl.BlockSpec((None, tq, LANES), lambda b, p, i: (b, i, qb + p)), seq_in(kb0), seq_in(vb0), blk, blk],
        out_specs=[blk, seq_out, seq_out], out_shape=[shp, shp, shp], scratch_shapes=[], args=[proj3, proj3, proj3, rt3, do3],
        sem=("arbitrary",) * 3 if plans else ("parallel", "parallel", "arbitrary"))


def mla_attn_fwd(q3, k3, kv3, vblk0, *, tq=512, tk=512, plans=None, name):
    Bl, S, _ = q3.shape
    tq = min(tq, S)
    tk = min(tk, tq)
    per_q = tq // tk
    scale = MLA_QK ** -0.5

    def body(q_ref, k_ref, v_ref, o_ref, lse_ref):
        i = pl.program_id(2)
        masks = _lane_masks()
        rr = lax.broadcasted_iota(jnp.int32, (tq, tk), 0)
        cc = lax.broadcasted_iota(jnp.int32, (tq, tk), 1)
        qh = [q_ref[:, h * LANES:(h + 1) * LANES] for h in range(2)]

        def step(j, carry):
            m0, l0, m1, l1, acc = carry
            off = pl.multiple_of(j * tk, tk)
            vb = v_ref[pl.ds(off, tk), :]
            causal = (cc + j * tk) <= (rr + i * tq)
            ms, ls, two = [m0, m1], [l0, l1], range(2)
            kh = [k_ref[pl.ds(off, tk), h * LANES:(h + 1) * LANES] for h in two]
            s = [jnp.where(causal, _dot_nt(qh[h], kh[h]) * scale, NEG) for h in two]
            m_new = [jnp.maximum(ms[h], jnp.max(s[h], axis=1, keepdims=True)) for h in two]
            p = [jnp.exp(s[h] - m_new[h]) for h in two]
            alpha = [jnp.exp(ms[h] - m_new[h]) for h in two]
            ls = [alpha[h] * ls[h] + jnp.sum(p[h], axis=1, keepdims=True) for h in two]
            add = [_dot(p[h].astype(BF16), jnp.where(masks[h], vb, 0.0).astype(BF16)) for h in two]
            acc = acc * jnp.where(masks[0], alpha[0], alpha[1]) + (add[0] + add[1])
            return m_new[0], ls[0], m_new[1], ls[1], acc

        neg = jnp.full((tq, 1), NEG, F32)
        zero = jnp.zeros((tq, 1), F32)
        m0, l0, m1, l1, acc = lax.fori_loop(0, (i + 1) * per_q, step, (neg, zero, neg, zero, jnp.zeros((tq, LANES), F32)))
        o_ref[...] = acc / jnp.where(masks[0], l0, l1)
        lse_ref[...] = jnp.where(masks[0], m0 + jnp.log(l0), m1 + jnp.log(l1))

    out = pl.BlockSpec((None, tq, LANES), lambda b, p, i: (b, i, p))
    shp = jax.ShapeDtypeStruct((Bl, S, 3 * LANES), F32)
    return call_with_plans(
        body, plans, name=name, grid=(Bl, 3, S // tq),
        in_specs=[pl.BlockSpec((None, tq, 2 * LANES), lambda b, p, i: (b, i, p)), pl.BlockSpec((None, S, 2 * LANES), lambda b, p, i: (b, 0, p)),
                  pl.BlockSpec((None, S, LANES), lambda b, p, i: (b, 0, vblk0 + p))],
        out_specs=[out, out], out_shape=[shp, shp], scratch_shapes=[], args=[q3, k3, kv3],
        sem=("arbitrary",) * 3 if plans else ("parallel", "parallel", "arbitrary"))


def mla_attn_bwd(q3, k3, kv3, vblk0, o3, lse3, do3, *, tq=512, tk=512, name):
    Bl, S, _ = q3.shape
    tq = min(tq, S)
    tk = min(tk, tq)
    per_q = tq // tk
    nq = S // tq
    scale = MLA_QK ** -0.5

    def body(q_ref, k_ref, v_ref, o_ref, lse_ref, do_ref, dq_ref, dk_ref, dv_ref, s_scr, dp_scr, p_scr, ds_scr):
        j = pl.program_id(2)

        @pl.when(j == 0)
        def _():
            dq_ref[...] = jnp.zeros_like(dq_ref)

        masks = _lane_masks()
        vb = v_ref[...]
        vh = [jnp.where(m, vb, 0.0).astype(BF16) for m in masks]
        kh = [k_ref[:, h * LANES:(h + 1) * LANES] for h in range(2)]
        i0 = lax.div(j, jnp.int32(per_q))

        def step(i, carry, masked):
            dk0, dk1, dv = carry
            off = pl.multiple_of(i * tq, tq)
            do_b = do_ref[pl.ds(off, tq), :].astype(BF16)
            prod = do_b.astype(F32) * o_ref[pl.ds(off, tq), :]
            lse = lse_ref[pl.ds(off, tq), :]
            two = range(2)
            qh = [q_ref[pl.ds(off, tq), h * LANES:(h + 1) * LANES] for h in two]
            doh = [jnp.where(masks[h], do_b, jnp.zeros_like(do_b)) for h in two]
            delta = [jnp.sum(jnp.where(masks[h], prod, 0.0), axis=1, keepdims=True) for h in two]
            lse_h = [lse[:, h * HEAD:h * HEAD + 1] for h in two]
            for h in two:
                s_scr[h] = _dot_nt(qh[h], kh[h])
            for h in two:
                dp_scr[h] = _dot_nt(doh[h], vh[h])
            for r0 in range(0, tq, STRIP):
                rows = slice(r0, r0 + STRIP)
                for h in two:
                    s = s_scr[h, rows, :] * scale
                    if masked:
                        rr = lax.broadcasted_iota(jnp.int32, (STRIP, tk), 0) + (i * tq + r0)
                        cc = lax.broadcasted_iota(jnp.int32, (STRIP, tk), 1) + j * tk
                        s = jnp.where(cc <= rr, s, NEG)
                    p = jnp.exp(s - lse_h[h][rows])
                    p_scr[h, rows, :] = p.astype(BF16)
                    ds_scr[h, rows, :] = (p * (dp_scr[h, rows, :] - delta[h][rows])).astype(BF16)
            dqs = [_dot(ds_scr[h], kh[h]) * scale for h in two]
            dks = [dk0 + _dot_tn(ds_scr[0], qh[0]), dk1 + _dot_tn(ds_scr[1], qh[1])]
            dv = dv + _dot_tn(p_scr[0], doh[0]) + _dot_tn(p_scr[1], doh[1])
            for h in two:
                dq_ref[pl.ds(off, tq), h * LANES:(h + 1) * LANES] += dqs[h]
            return dks[0], dks[1], dv

        zero = jnp.zeros((tk, LANES), F32)
        carry = step(i0, (zero, zero, zero), True)
        dk0, dk1, dv = lax.fori_loop(i0 + 1, nq, lambda i, c: step(i, c, False), carry)
        dk_ref[:, 0:LANES] = dk0 * scale
        dk_ref[:, LANES:2 * LANES] = dk1 * scale
        dv_ref[...] = dv

    seq1 = pl.BlockSpec((None, S, LANES), lambda b, p, j: (b, 0, p))
    seq2 = pl.BlockSpec((None, S, 2 * LANES), lambda b, p, j: (b, 0, p))
    return pl.pallas_call(
        body, name=name, grid=(Bl, 3, S // tk),
        in_specs=[seq2, pl.BlockSpec((None, tk, 2 * LANES), lambda b, p, j: (b, j, p)),
                  pl.BlockSpec((None, tk, LANES), lambda b, p, j: (b, j, vblk0 + p)), seq1, seq1, seq1],
        out_specs=[seq2, pl.BlockSpec((None, tk, 2 * LANES), lambda b, p, j: (b, j, p)), pl.BlockSpec((None, tk, LANES), lambda b, p, j: (b, j, p))],
        out_shape=[jax.ShapeDtypeStruct((Bl, S, 6 * LANES), F32), jax.ShapeDtypeStruct((Bl, S, 6 * LANES), F32), jax.ShapeDtypeStruct((Bl, S, 3 * LANES), F32)],
        scratch_shapes=[pltpu.VMEM((2, tq, tk), F32), pltpu.VMEM((2, tq, tk), F32), pltpu.VMEM((2, tq, tk), BF16), pltpu.VMEM((2, tq, tk), BF16)],
        compiler_params=_cp("parallel", "parallel", "arbitrary"),
    )(q3, k3, kv3, o3, lse3, do3)


def _bucket_table():
    a = jnp.arange(WINDOW)[:, None]
    b = jnp.arange(2 * WINDOW)[None, :]
    dist = WINDOW + a - b
    max_exact = REL_BUCKETS // 2
    n = jnp.maximum(dist, 0)
    nf = jnp.maximum(n, 1).astype(F32)
    large = max_exact + (jnp.log(nf / max_exact) / math.log(REL_MAX_DIST / max_exact) * (REL_BUCKETS - max_exact)).astype(jnp.int32)
    large = jnp.minimum(large, REL_BUCKETS - 1)
    bucket = jnp.where(n < max_exact, n, large)
    return jnp.where((dist >= 0) & (dist < WINDOW), bucket, -1).astype(jnp.int32)


def swa_bias(rel_flat, bucket, *, name):
    def body(t_ref, b_ref, o_ref):
        bk = b_ref[...]
        for p in range(3):
            for hh in range(2):
                h = hh * 3 + p
                acc = jnp.full(bk.shape, NEG, F32)
                for b in range(REL_BUCKETS):
                    acc = jnp.where(bk == b, t_ref[b * 6 + h], acc)
                o_ref[p, hh] = acc

    return pl.pallas_call(
        body, name=name,
        in_specs=[pl.BlockSpec(memory_space=pltpu.SMEM), pl.BlockSpec(memory_space=pltpu.VMEM)],
        out_specs=pl.BlockSpec(memory_space=pltpu.VMEM),
        out_shape=jax.ShapeDtypeStruct((3, 2, WINDOW, 2 * WINDOW), F32),
    )(rel_flat, bucket)


def swa_bias_bwd(dbias, bucket, *, name):
    Bl = dbias.shape[0]

    def body(d_ref, b_ref, o_ref):
        bk = b_ref[...]
        lane = lax.broadcasted_iota(jnp.int32, (1, LANES), 1)
        rows = []
        for h in range(6):
            hh, p = divmod(h, 3)
            d = d_ref[0, p, hh]
            for bl in range(1, Bl):
                d = d + d_ref[bl, p, hh]
            row = jnp.zeros((1, LANES), F32)
            for b in range(REL_BUCKETS):
                s = jnp.sum(jnp.sum(jnp.where(bk == b, d, 0.0), axis=1, keepdims=True), axis=0, keepdims=True)
                row = row + jnp.where(lane == b, s, 0.0)
            rows.append(row)
        rows += [jnp.zeros((1, LANES), F32)] * 2
        o_ref[...] = jnp.concatenate(rows, axis=0)

    return pl.pallas_call(
        body, name=name,
        in_specs=[pl.BlockSpec(memory_space=pltpu.VMEM)] * 2, out_specs=pl.BlockSpec(memory_space=pltpu.VMEM),
        out_shape=jax.ShapeDtypeStruct((8, LANES), F32),
    )(dbias, bucket)


SWA_QBLOCKS = 8


def _swa_specs(vblk, nqb):
    rows = nqb * WINDOW
    cur = lambda blk: pl.BlockSpec((None, rows, LANES), lambda b, p, n: (b, n, blk))
    prev = lambda blk: pl.BlockSpec((None, WINDOW, LANES), lambda b, p, n: (b, jnp.maximum(n * nqb - 1, 0), blk))
    return [pl.BlockSpec((None, rows, LANES), lambda b, p, n: (b, n, p)), cur(0), prev(0), cur(vblk), prev(vblk),
            pl.BlockSpec((None, 2, WINDOW, 2 * WINDOW), lambda b, p, n: (p, 0, 0, 0)), pl.BlockSpec((None, 2, LANES), lambda b, p, n: (p, 0, 0))]


def _rows128(ref, m):
    return ref[m * WINDOW:(m + 1) * WINDOW, :]


def _swa_logits(qh, kp, kc, bias_h, first, scale):
    sp = jnp.where(first, NEG, _dot_nt(qh, kp) * scale + bias_h[:, :WINDOW])
    sc = _dot_nt(qh, kc) * scale + bias_h[:, WINDOW:]
    return sp, sc


def swa_attn_fwd(qn3, kn3, proj3, bias, sinks, *, plans=None, name):
    Bl, S, _ = qn3.shape
    scale = HEAD ** -0.5
    nqb = min(SWA_QBLOCKS, S // WINDOW)

    def body(q_ref, kc_ref, kp_ref, vc_ref, vp_ref, b_ref, s_ref, o_ref, lse_ref):
        seq_start = pl.program_id(2) == 0
        masks = _lane_masks()
        chains = [(m_, h) for m_ in range(nqb) for h in range(2)]
        kp = [kp_ref[...] if m_ == 0 else _rows128(kc_ref, m_ - 1) for m_ in range(nqb)]
        vp = [vp_ref[...] if m_ == 0 else _rows128(vc_ref, m_ - 1) for m_ in range(nqb)]
        kc = [_rows128(kc_ref, m_) for m_ in range(nqb)]
        vc = [_rows128(vc_ref, m_) for m_ in range(nqb)]
        sink = [s_ref[h:h + 1, 0:1] for h in range(2)]
        logits = {}
        for m_, h in chains:
            q = _rows128(q_ref, m_)
            qh = jnp.where(masks[h], q, jnp.zeros_like(q))
            logits[m_, h] = _swa_logits(qh, kp[m_], kc[m_], b_ref[h], seq_start if m_ == 0 else False, scale)
        mx = {c: jnp.maximum(jnp.maximum(jnp.max(logits[c][0], axis=1, keepdims=True), jnp.max(logits[c][1], axis=1, keepdims=True)), sink[c[1]])
              for c in chains}
        ex = {c: (jnp.exp(logits[c][0] - mx[c]), jnp.exp(logits[c][1] - mx[c])) for c in chains}
        den = {c: jnp.sum(ex[c][0], axis=1, keepdims=True) + jnp.sum(ex[c][1], axis=1, keepdims=True) + jnp.exp(sink[c[1]] - mx[c]) for c in chains}
        inv = {c: 1.0 / den[c] for c in chains}
        out = {}
        for m_, h in chains:
            c = (m_, h)
            out[c] = (_dot((ex[c][0] * inv[c]).astype(BF16), jnp.where(masks[h], vp[m_], 0.0).astype(BF16))
                      + _dot((ex[c][1] * inv[c]).astype(BF16), jnp.where(masks[h], vc[m_], 0.0).astype(BF16)))
        for m_ in range(nqb):
            o_ref[m_ * WINDOW:(m_ + 1) * WINDOW, :] = out[m_, 0] + out[m_, 1]
            lse_ref[m_ * WINDOW:(m_ + 1) * WINDOW, :] = jnp.where(masks[0], mx[m_, 0] + jnp.log(den[m_, 0]), mx[m_, 1] + jnp.log(den[m_, 1]))

    out = pl.BlockSpec((None, nqb * WINDOW, LANES), lambda b, p, n: (b, n, p))
    shp = jax.ShapeDtypeStruct((Bl, S, 3 * LANES), F32)
    return call_with_plans(
        body, plans, name=name, grid=(Bl, 3, S // (nqb * WINDOW)), in_specs=_swa_specs(P_SWV // LANES, nqb),
        out_specs=[out, out], out_shape=[shp, shp], scratch_shapes=[], args=[qn3, kn3, kn3, proj3, proj3, bias, sinks],
        sem=("arbitrary",) * 3 if plans else ("parallel", "parallel", "arbitrary"))


def swa_attn_bwd(qn3, kn3, proj3, bias, sinks, o3, lse3, do3, *, name):
    Bl, S, _ = qn3.shape
    scale = HEAD ** -0.5
    nqb = min(SWA_QBLOCKS, S // WINDOW)
    rows = nqb * WINDOW

    def body(q_ref, kc_ref, kp_ref, vc_ref, vp_ref, b_ref, s_ref, o_ref, lse_ref, do_ref,
             dq_ref, dk_ref, dv_ref, db_ref, dsk_ref):
        p_id, n = pl.program_id(1), pl.program_id(2)
        seq_start = n == 0

        @pl.when((p_id == 0) & seq_start)
        def _():
            dk_ref[...] = jnp.zeros_like(dk_ref)
            dv_ref[...] = jnp.zeros_like(dv_ref)

        @pl.when(seq_start)
        def _():
            db_ref[...] = jnp.zeros_like(db_ref)
            dsk_ref[...] = jnp.zeros_like(dsk_ref)

        masks = _lane_masks()
        zero = jnp.zeros((WINDOW, LANES), F32)
        chains = [(m_, h) for m_ in range(nqb) for h in range(2)]
        kp = [kp_ref[...] if m_ == 0 else _rows128(kc_ref, m_ - 1) for m_ in range(nqb)]
        vp = [vp_ref[...] if m_ == 0 else _rows128(vc_ref, m_ - 1) for m_ in range(nqb)]
        kc = [_rows128(kc_ref, m_) for m_ in range(nqb)]
        vc = [_rows128(vc_ref, m_) for m_ in range(nqb)]
        do_b = [_rows128(do_ref, m_).astype(BF16) for m_ in range(nqb)]
        prod = [do_b[m_].astype(F32) * _rows128(o_ref, m_) for m_ in range(nqb)]
        lse = [_rows128(lse_ref, m_) for m_ in range(nqb)]
        qh, doh, logits, lse_h, delta = {}, {}, {}, {}, {}
        for m_, h in chains:
            q = _rows128(q_ref, m_)
            qh[m_, h] = jnp.where(masks[h], q, jnp.zeros_like(q))
            doh[m_, h] = jnp.where(masks[h], do_b[m_], jnp.zeros_like(do_b[m_]))
            logits[m_, h] = _swa_logits(qh[m_, h], kp[m_], kc[m_], b_ref[h], seq_start if m_ == 0 else False, scale)
            lse_h[m_, h] = lse[m_][:, h * HEAD:h * HEAD + 1]
            delta[m_, h] = jnp.sum(jnp.where(masks[h], prod[m_], 0.0), axis=1, keepdims=True)
        pr = {c: (jnp.exp(logits[c][0] - lse_h[c]), jnp.exp(logits[c][1] - lse_h[c])) for c in chains}
        dp = {(m_, h): (_dot_nt(doh[m_, h], jnp.where(masks[h], vp[m_], 0.0).astype(BF16)),
                        _dot_nt(doh[m_, h], jnp.where(masks[h], vc[m_], 0.0).astype(BF16))) for m_, h in chains}
        ds = {c: (pr[c][0] * (dp[c][0] - delta[c]), pr[c][1] * (dp[c][1] - delta[c])) for c in chains}
        dsb = {c: ((ds[c][0] * scale).astype(BF16), (ds[c][1] * scale).astype(BF16)) for c in chains}
        dk_acc = [zero] * (nqb + 1)
        dv_acc = [zero] * (nqb + 1)
        db_acc = [[jnp.zeros((WINDOW, WINDOW), F32)] * 2 for _ in range(2)]
        dsk_acc = [jnp.zeros((1, 1), F32)] * 2
        dq = [zero] * nqb
        for m_, h in chains:
            c = (m_, h)
            db_acc[h] = [db_acc[h][0] + ds[c][0], db_acc[h][1] + ds[c][1]]
            dsk_acc[h] = dsk_acc[h] - jnp.sum(jnp.exp(s_ref[h:h + 1, 0:1] - lse_h[c]) * delta[c], axis=0, keepdims=True)
            dq[m_] = (dq[m_] + _dot(dsb[c][0], jnp.where(masks[h], kp[m_], jnp.zeros_like(kp[m_])))
                      + _dot(dsb[c][1], jnp.where(masks[h], kc[m_], jnp.zeros_like(kc[m_]))))
            dk_acc[m_] = dk_acc[m_] + _dot_tn(dsb[c][0], qh[c])
            dk_acc[m_ + 1] = dk_acc[m_ + 1] + _dot_tn(dsb[c][1], qh[c])
            dv_acc[m_] = dv_acc[m_] + _dot_tn(pr[c][0].astype(BF16), doh[c])
            dv_acc[m_ + 1] = dv_acc[m_ + 1] + _dot_tn(pr[c][1].astype(BF16), doh[c])
        for m_ in range(nqb):
            dq_ref[m_ * WINDOW:(m_ + 1) * WINDOW, :] = dq[m_]
        for h in range(2):
            db_ref[h, :, 0:WINDOW] += db_acc[h][0]
            db_ref[h, :, WINDOW:2 * WINDOW] += db_acc[h][1]
            dsk_ref[h:h + 1, :] += jnp.broadcast_to(dsk_acc[h], (1, LANES))
        offp = pl.multiple_of(jnp.maximum(n * nqb - 1, 0) * WINDOW, WINDOW)
        dk_ref[pl.ds(offp, WINDOW), :] += dk_acc[0]
        dv_ref[pl.ds(offp, WINDOW), :] += dv_acc[0]
        for m_ in range(nqb):
            off = pl.multiple_of(n * rows + m_ * WINDOW, WINDOW)
            dk_ref[pl.ds(off, WINDOW), :] += dk_acc[m_ + 1]
            dv_ref[pl.ds(off, WINDOW), :] += dv_acc[m_ + 1]

    blk = pl.BlockSpec((None, rows, LANES), lambda b, p, n: (b, n, p))
    seq = pl.BlockSpec((None, S, LANES), lambda b, p, n: (b, 0, 0))
    return pl.pallas_call(
        body, name=name, grid=(Bl, 3, S // rows), in_specs=_swa_specs(P_SWV // LANES, nqb) + [blk, blk, blk],
        out_specs=[blk, seq, seq, pl.BlockSpec((None, None, 2, WINDOW, 2 * WINDOW), lambda b, p, n: (b, p, 0, 0, 0)),
                   pl.BlockSpec((None, None, 2, LANES), lambda b, p, n: (b, p, 0, 0))],
        out_shape=[jax.ShapeDtypeStruct((Bl, S, 3 * LANES), F32), jax.ShapeDtypeStruct((Bl, S, LANES), F32), jax.ShapeDtypeStruct((Bl, S, LANES), F32),
                   jax.ShapeDtypeStruct((Bl, 3, 2, WINDOW, 2 * WINDOW), F32), jax.ShapeDtypeStruct((Bl, 3, 2, LANES), F32)],
        compiler_params=_cp("arbitrary", "arbitrary", "arbitrary"),
    )(qn3, kn3, kn3, proj3, proj3, bias, sinks, o3, lse3, do3)


CONV_ROWS = 64
CONV_LANES = 128


def _conv_strip(x_ref, h_ref, w, b, r0, cols, first_blk):
    x = x_ref[r0:r0 + CONV_ROWS, cols]
    if r0 == 0:
        rows = lax.broadcasted_iota(jnp.int32, x.shape, 0)
        h6 = jnp.where(first_blk, 0.0, h_ref[6:7, cols])
        h7 = jnp.where(first_blk, 0.0, h_ref[7:8, cols])
        x1 = jnp.where(rows == 0, h7, pltpu.roll(x, 1, 0))
        x2 = jnp.where(rows == 0, h6, jnp.where(rows == 1, h7, pltpu.roll(x, 2, 0)))
    else:
        x1 = x_ref[r0 - 1:r0 - 1 + CONV_ROWS, cols]
        x2 = x_ref[r0 - 2:r0 - 2 + CONV_ROWS, cols]
    return w[0:1] * x2 + w[1:2] * x1 + w[2:3] * x + b, x, x1, x2


FF_BLK = D_FF // 2


def _up_perm(a):
    q = FF_BLK
    return _cat([a[..., 0:q], a[..., 2 * q:3 * q], a[..., q:2 * q], a[..., 3 * q:4 * q]])


def conv_gate_fwd(up3, cw, cb, *, tm=256, name):
    Bl, S, _ = up3.shape
    tm = min(tm, S)
    W = 2 * FF_BLK

    def body(x_ref, h_ref, w_ref, b_ref, o_ref):
        first = pl.program_id(1) == 0

        def chunk(c, carry):
            cg = pl.ds(pl.multiple_of(c * CONV_LANES, CONV_LANES), CONV_LANES)
            cv = pl.ds(pl.multiple_of(FF_BLK + c * CONV_LANES, CONV_LANES), CONV_LANES)
            wg, wv, bg, bv = w_ref[:, cg], w_ref[:, cv], b_ref[:, cg], b_ref[:, cv]
            for r0 in range(0, tm, CONV_ROWS):
                ug = _conv_strip(x_ref, h_ref, wg, bg, r0, cg, first)[0]
                uv = _conv_strip(x_ref, h_ref, wv, bv, r0, cv, first)[0]
                o_ref[r0:r0 + CONV_ROWS, cg] = (ug * jax.nn.sigmoid(ug) * uv).astype(BF16)
            return carry

        lax.fori_loop(0, FF_BLK // CONV_LANES, chunk, 0)

    hb = tm // 8
    return pl.pallas_call(
        body, name=name, grid=(Bl, S // tm, 2),
        in_specs=[pl.BlockSpec((None, tm, W), lambda b, s, c: (b, s, c)),
                  pl.BlockSpec((None, 8, W), lambda b, s, c: (b, jnp.maximum(s * hb - 1, 0), c)),
                  pl.BlockSpec((3, W), lambda b, s, c: (0, c)), pl.BlockSpec((1, W), lambda b, s, c: (0, c))],
        out_specs=pl.BlockSpec((None, tm, FF_BLK), lambda b, s, c: (b, s, c)),
        out_shape=jax.ShapeDtypeStruct((Bl, S, D_FF), BF16),
        compiler_params=_cp("parallel", "parallel", "parallel"),
    )(up3, up3, cw, cb)


def conv_gate_bwd(up3, cw, cb, da3, *, tm=256, name):
    Bl, S, _ = up3.shape
    tm = min(tm, S)
    ns = S // tm
    W = 2 * FF_BLK

    def body(x_ref, h_ref, w_ref, b_ref, da_ref, dup_ref, dw_ref, nxt_ref, du_scr):
        b, s = pl.program_id(1), pl.program_id(2)
        seq_end = s == 0
        first = s == ns - 1

        @pl.when((b == 0) & seq_end)
        def _():
            dw_ref[...] = jnp.zeros_like(dw_ref)

        def du_chunk(c, carry):
            cg = pl.ds(pl.multiple_of(c * CONV_LANES, CONV_LANES), CONV_LANES)
            cv = pl.ds(pl.multiple_of(FF_BLK + c * CONV_LANES, CONV_LANES), CONV_LANES)
            wg, wv, bg, bv = w_ref[:, cg], w_ref[:, cv], b_ref[:, cg], b_ref[:, cv]
            acc_g = [jnp.zeros((1, CONV_LANES), F32)] * 4
            acc_v = [jnp.zeros((1, CONV_LANES), F32)] * 4
            for r0 in range(0, tm, CONV_ROWS):
                ug, xg, xg1, xg2 = _conv_strip(x_ref, h_ref, wg, bg, r0, cg, first)
                uv, xv, xv1, xv2 = _conv_strip(x_ref, h_ref, wv, bv, r0, cv, first)
                da = da_ref[r0:r0 + CONV_ROWS, cg].astype(F32)
                sg = jax.nn.sigmoid(ug)
                dug = da * uv * sg * (1.0 + ug * (1.0 - sg))
                duv = da * ug * sg
                du_scr[r0:r0 + CONV_ROWS, cg] = dug
                du_scr[r0:r0 + CONV_ROWS, cv] = duv
                col = lambda t: jnp.sum(t, axis=0, keepdims=True)
                acc_g = [acc_g[0] + col(dug * xg2), acc_g[1] + col(dug * xg1), acc_g[2] + col(dug * xg), acc_g[3] + col(dug)]
                acc_v = [acc_v[0] + col(duv * xv2), acc_v[1] + col(duv * xv1), acc_v[2] + col(duv * xv), acc_v[3] + col(duv)]
            for t in range(4):
                dw_ref[t:t + 1, cg] += acc_g[t]
                dw_ref[t:t + 1, cv] += acc_v[t]
            return carry

        lax.fori_loop(0, FF_BLK // CONV_LANES, du_chunk, 0)
        du_scr[tm:tm + 8, :] = jnp.where(seq_end, 0.0, nxt_ref[...])

        def dup_chunk(c, carry):
            cols = pl.ds(pl.multiple_of(c * CONV_LANES, CONV_LANES), CONV_LANES)
            w = w_ref[:, cols]
            for r0 in range(0, tm, CONV_ROWS):
                d0 = du_scr[r0:r0 + CONV_ROWS, cols]
                d1 = du_scr[r0 + 1:r0 + 1 + CONV_ROWS, cols]
                d2 = du_scr[r0 + 2:r0 + 2 + CONV_ROWS, cols]
                dup_ref[r0:r0 + CONV_ROWS, cols] = (w[2:3] * d0 + w[1:2] * d1 + w[0:1] * d2).astype(BF16)
            return carry

        lax.fori_loop(0, W // CONV_LANES, dup_chunk, 0)
        nxt_ref[...] = du_scr[0:8, :]

    hb = tm // 8
    rb = lambda s: ns - 1 - s
    return pl.pallas_call(
        body, name=name, grid=(2, Bl, ns),
        in_specs=[pl.BlockSpec((None, tm, W), lambda c, b, s: (b, rb(s), c)),
                  pl.BlockSpec((None, 8, W), lambda c, b, s: (b, jnp.maximum(rb(s) * hb - 1, 0), c)),
                  pl.BlockSpec((3, W), lambda c, b, s: (0, c)), pl.BlockSpec((1, W), lambda c, b, s: (0, c)),
                  pl.BlockSpec((None, tm, FF_BLK), lambda c, b, s: (b, rb(s), c))],
        out_specs=[pl.BlockSpec((None, tm, W), lambda c, b, s: (b, rb(s), c)), pl.BlockSpec((8, W), lambda c, b, s: (0, c))],
        out_shape=[jax.ShapeDtypeStruct((Bl, S, 2 * D_FF), BF16), jax.ShapeDtypeStruct((8, 2 * D_FF), F32)],
        scratch_shapes=[pltpu.VMEM((8, W), F32), pltpu.VMEM((tm + 8, W), F32)],
        compiler_params=_cp("arbitrary", "arbitrary", "arbitrary"),
    )(up3, up3, cw, cb, da3)


def cast_layer(w3, l, *, name):
    _, R, C = w3.shape
    tr = _tile(R, 512, 16)

    def body(w_ref, o_ref):
        o_ref[...] = w_ref[...].astype(BF16)

    return pl.pallas_call(
        body, name=name, grid=(R // tr,), in_specs=[pl.BlockSpec((None, tr, C), lambda i: (l, i, 0))],
        out_specs=pl.BlockSpec((tr, C), lambda i: (i, 0)), out_shape=jax.ShapeDtypeStruct((R, C), BF16),
        compiler_params=_cp("parallel"),
    )(w3)


def gate_bwd(dx3, y3, gate, *, tm=512, name):
    Bl, S, D = dx3.shape
    tm = min(tm, S)

    def body(dx_ref, y_ref, g_ref, o_ref, dg_ref):
        @pl.when(pl.program_id(1) == 0)
        def _():
            dg_ref[...] = jnp.zeros_like(dg_ref)

        dx = dx_ref[...]
        dg_ref[...] += jnp.sum(dx * y_ref[...], axis=0, keepdims=True)
        o_ref[...] = (dx * g_ref[...]).astype(BF16)

    blk = pl.BlockSpec((None, tm, D), lambda b, s: (b, s, 0))
    vec = pl.BlockSpec((None, 1, D), lambda b, s: (b, 0, 0))
    return pl.pallas_call(
        body, name=name, grid=(Bl, S // tm), in_specs=[blk, blk, vec], out_specs=[blk, vec],
        out_shape=[jax.ShapeDtypeStruct((Bl, S, D), BF16), jax.ShapeDtypeStruct((Bl, 1, D), F32)],
        compiler_params=_cp("parallel", "arbitrary"),
    )(dx3, y3, gate)


def loss_grad(y3, t3, *, tm=512, name):
    Bl, S, D = y3.shape
    tm = min(tm, S)
    last = (Bl - 1, S // tm - 1)

    def body(y_ref, t_ref, dy_ref, l_ref, acc_ref):
        b, s = pl.program_id(0), pl.program_id(1)

        @pl.when((b == 0) & (s == 0))
        def _():
            acc_ref[...] = jnp.zeros_like(acc_ref)

        e = y_ref[...] - t_ref[...]
        dy_ref[...] = e * (1.0 / D)
        acc_ref[...] += jnp.sum(e * e, axis=0, keepdims=True)

        @pl.when((b == last[0]) & (s == last[1]))
        def _():
            l_ref[...] = jnp.broadcast_to(jnp.sum(acc_ref[...], axis=1, keepdims=True) * (0.5 / D), (1, LANES))

    blk = pl.BlockSpec((None, tm, D), lambda b, s: (b, s, 0))
    return pl.pallas_call(
        body, name=name, grid=(Bl, S // tm), in_specs=[blk, blk],
        out_specs=[blk, pl.BlockSpec((1, LANES), lambda b, s: (0, 0))],
        out_shape=[jax.ShapeDtypeStruct((Bl, S, D), F32), jax.ShapeDtypeStruct((1, LANES), F32)],
        scratch_shapes=[pltpu.VMEM((1, D), F32)], compiler_params=_cp("arbitrary", "arbitrary"),
    )(y3, t3)


def adamw(w, g, m, v, *, name):
    L, R, C = w.shape
    tr = _tile(R, 512, 8)

    def body(w_ref, g_ref, m_ref, v_ref, d_ref, m2_ref, v2_ref):
        d_ref[...], m2_ref[...], v2_ref[...] = _adam_update(w_ref[...], g_ref[...], m_ref[...], v_ref[...])

    blk = pl.BlockSpec((None, tr, C), lambda l, i: (l, i, 0))
    shp = jax.ShapeDtypeStruct((L, R, C), F32)
    return pl.pallas_call(
        body, name=name, grid=(L, R // tr), in_specs=[blk] * 4, out_specs=[blk] * 3, out_shape=[shp] * 3,
        compiler_params=_cp("parallel", "parallel"),
    )(w, g, m, v)


def _adam_update(w, g, m, v):
    c1 = 1.0 / (1.0 - ADAM_B1 ** ADAM_STEP)
    c2 = 1.0 / (1.0 - ADAM_B2 ** ADAM_STEP)
    m2 = ADAM_B1 * m + (1.0 - ADAM_B1) * g
    v2 = ADAM_B2 * v + (1.0 - ADAM_B2) * (g * g)
    return -ADAM_LR * ((m2 * c1) / (jnp.sqrt(v2 * c2) + ADAM_EPS) + ADAM_WD * w), m2, v2


def adamw_small(ws, gs, ms, vs, *, name):
    na = len(ws)

    def body(*refs):
        w_r, g_r, m_r, v_r = (refs[i * na:(i + 1) * na] for i in range(4))
        d_r, m2_r, v2_r = (refs[(4 + i) * na:(5 + i) * na] for i in range(3))
        for a in range(na):
            d_r[a][...], m2_r[a][...], v2_r[a][...] = _adam_update(w_r[a][...], g_r[a][...], m_r[a][...], v_r[a][...])

    vm = pl.BlockSpec(memory_space=pltpu.VMEM)
    shp = [jax.ShapeDtypeStruct(w.shape, F32) for w in ws]
    out = pl.pallas_call(body, name=name, in_specs=[vm] * (4 * na), out_specs=[vm] * (3 * na), out_shape=shp * 3)(*ws, *gs, *ms, *vs)
    return out[:na], out[na:2 * na], out[2 * na:]


def sum_small(xs, *, name):
    na = len(xs)

    def body(*refs):
        for x_ref, o_ref in zip(refs[:na], refs[na:]):
            acc = x_ref[0]
            for k in range(1, x_ref.shape[0]):
                acc = acc + x_ref[k]
            o_ref[...] = acc

    vm = pl.BlockSpec(memory_space=pltpu.VMEM)
    return pl.pallas_call(body, name=name, in_specs=[vm] * na, out_specs=[vm] * na,
                          out_shape=[jax.ShapeDtypeStruct(x.shape[1:], x.dtype) for x in xs])(*xs)


def pair_add_half(g4, recv, c_arr, *, tr=512, name):
    _, R, C = g4.shape
    H = R // 2
    tr = _tile(H, tr, 16)
    nb = H // tr

    def body(c_ref, g_ref, r_ref, o_ref):
        o_ref[...] = (g_ref[...].astype(F32) + r_ref[...].astype(F32)).astype(BF16)

    grid_spec = pltpu.PrefetchScalarGridSpec(
        num_scalar_prefetch=1, grid=(4, nb),
        in_specs=[pl.BlockSpec((None, tr, C), lambda k, i, c_ref: (k, c_ref[0] * nb + i, 0)),
                  pl.BlockSpec((None, tr, C), lambda k, i, c_ref: (k, i, 0))],
        out_specs=pl.BlockSpec((None, tr, C), lambda k, i, c_ref: (k, i, 0)),
    )
    return pl.pallas_call(
        body, name=name, grid_spec=grid_spec, out_shape=jax.ShapeDtypeStruct((4, H, C), BF16),
        compiler_params=_cp("parallel", "parallel"),
    )(c_arr, g4, recv)


def chip_sum_into(landed, pair, sel, *, tr=512, name):
    _, H, C = landed.shape
    tr = _tile(H, tr, 16)
    nb = H // tr

    def body(s_ref, l0, l1, l2, l3, p_ref, o_ref):
        own = p_ref[...].astype(F32)
        acc = None
        for k, l_ref in enumerate((l0, l1, l2, l3)):
            part = jnp.where(s_ref[0] == k, own, l_ref[...].astype(F32))
            acc = part if acc is None else acc + part
        o_ref[...] = acc

    def slot(k):
        return pl.BlockSpec((None, tr, C), lambda i, s: (jnp.where(s[0] == k, (k + 1) % 4, k), i, 0))

    grid_spec = pltpu.PrefetchScalarGridSpec(
        num_scalar_prefetch=1, grid=(nb,),
        in_specs=[slot(0), slot(1), slot(2), slot(3), pl.BlockSpec((None, tr, C), lambda i, s: (s[0], i, 0))],
        out_specs=pl.BlockSpec((tr, C), lambda i, s: (s[1] * nb + i, 0)),
    )
    return pl.pallas_call(
        body, name=name, grid_spec=grid_spec, out_shape=jax.ShapeDtypeStruct((2 * H, C), F32), compiler_params=_cp("parallel"),
    )(sel, landed, landed, landed, landed, pair)


def mods_matmul(c_all, w_ada, b_ada_cols, *, tn=512, name):
    L, D, E = w_ada.shape
    nb = c_all.shape[0]
    tn = _tile(E, tn)

    def body(c_ref, w_ref, b_ref, o_ref):
        c = c_ref[...]
        a = c * jax.nn.sigmoid(c)
        o_ref[...] = jnp.dot(a, w_ref[...], preferred_element_type=F32, precision=lax.Precision.HIGHEST) + b_ref[...]

    return pl.pallas_call(
        body, name=name, grid=(L, E // tn),
        in_specs=[pl.BlockSpec((nb, D), lambda l, j: (0, 0)), pl.BlockSpec((None, D, tn), lambda l, j: (l, 0, j)),
                  pl.BlockSpec((None, 1, tn), lambda l, j: (l, 0, j))],
        out_specs=pl.BlockSpec((None, nb, tn), lambda l, j: (l, 0, j)),
        out_shape=jax.ShapeDtypeStruct((L, nb, E), F32), compiler_params=_cp("parallel", "parallel"),
    )(c_all, w_ada, b_ada_cols)


def ada_grad(c_all, dmods, *, tn=512, name):
    L, nb, E = dmods.shape
    D = c_all.shape[1]
    tn = _tile(E, tn)

    def body(c_ref, d_ref, o_ref):
        c = c_ref[...]
        a = c * jax.nn.sigmoid(c)
        o_ref[...] = lax.dot_general(a, d_ref[...], (((0,), (0,)), ((), ())), preferred_element_type=F32, precision=lax.Precision.HIGHEST)

    return pl.pallas_call(
        body, name=name, grid=(L, E // tn),
        in_specs=[pl.BlockSpec((nb, D), lambda l, j: (0, 0)), pl.BlockSpec((None, nb, tn), lambda l, j: (l, 0, j))],
        out_specs=pl.BlockSpec((None, D, tn), lambda l, j: (l, 0, j)),
        out_shape=jax.ShapeDtypeStruct((L, D, E), F32), compiler_params=_cp("parallel", "parallel"),
    )(c_all, dmods)


HBM = pl.BlockSpec(memory_space=pltpu.HBM)


def _me():
    return lax.axis_index("x"), lax.axis_index("y"), lax.axis_index("c")


def _flip(v, bit):
    return 1 - v if bit else v


def allgather8(xs, *, name):
    na = len(xs)

    def body(*refs):
        x_refs, out_refs = refs[:na], refs[na:2 * na]
        send_sems, recv_sems = refs[2 * na], refs[2 * na + 1]
        x, y, c = _me()
        me = 4 * x + 2 * y + c
        for x_ref, out_ref in zip(x_refs, out_refs):
            out_ref[me] = x_ref[...]
        sends = []
        for a, (x_ref, out_ref) in enumerate(zip(x_refs, out_refs)):
            for k in range(1, 8):
                peer = (_flip(x, k & 4), _flip(y, k & 2), _flip(c, k & 1))
                cp = pltpu.make_async_remote_copy(src_ref=x_ref, dst_ref=out_ref.at[me], send_sem=send_sems.at[a, k - 1],
                                                  recv_sem=recv_sems.at[a, k - 1], device_id=peer, device_id_type=MESH)
                cp.start()
                sends.append(cp)
        for a, (x_ref, out_ref) in enumerate(zip(x_refs, out_refs)):
            for k in range(1, 8):
                peer = (_flip(x, k & 4), _flip(y, k & 2), _flip(c, k & 1))
                src = 4 * peer[0] + 2 * peer[1] + peer[2]
                pltpu.make_async_remote_copy(src_ref=x_ref, dst_ref=out_ref.at[src], send_sem=send_sems.at[a, k - 1],
                                             recv_sem=recv_sems.at[a, k - 1], device_id=peer, device_id_type=MESH).wait_recv()
        for cp in sends:
            cp.wait_send()

    vm = pl.BlockSpec(memory_space=pltpu.VMEM)
    return pl.pallas_call(
        body, name=name, in_specs=[vm] * na, out_specs=[vm] * na,
        out_shape=[jax.ShapeDtypeStruct((8,) + a.shape, a.dtype) for a in xs],
        scratch_shapes=[pltpu.SemaphoreType.DMA((na, 7)), pltpu.SemaphoreType.DMA((na, 7))],
    )(*xs)


class _Plan:
    def __init__(self, ins, out_shapes, ncopies, copies, aliased=False):
        self.ins, self.out_shapes, self.ncopies, self.copies, self.aliased = list(ins), list(out_shapes), ncopies, copies, aliased

    def start(self, in_refs, out_refs, send_sems, recv_sems):
        sends, _ = self.copies(in_refs, out_refs, send_sems, recv_sems)
        for cp in sends:
            cp.start()

    def finish(self, in_refs, out_refs, send_sems, recv_sems):
        sends, recvs = self.copies(in_refs, out_refs, send_sems, recv_sems)
        for cp in recvs:
            cp.wait_recv()
        for cp in sends:
            cp.wait_send()


def _rcopy(src, dst, send_sems, recv_sems, idx, dev):
    return pltpu.make_async_remote_copy(src_ref=src, dst_ref=dst, send_sem=send_sems.at[idx], recv_sem=recv_sems.at[idx],
                                        device_id=dev, device_id_type=MESH)


def _other_chips(x, y):
    return [(_flip(x, k & 2), _flip(y, k & 1)) for k in range(1, 4)]


def plan_gather_ici(ws):
    def copies(in_refs, out_refs, ss, rs):
        x, y, c = _me()
        j = 2 * x + y
        sends, recvs = [], []
        for a, (x_ref, out_ref) in enumerate(zip(in_refs, out_refs)):
            H = x_ref.shape[0] // 2
            for k, (px, py) in enumerate(_other_chips(x, y)):
                sends.append(_rcopy(x_ref.at[pl.ds(c * H, H)], out_ref.at[j, pl.ds(c * H, H)], ss, rs, 3 * a + k, (px, py, c)))
                slot = out_ref.at[2 * px + py, pl.ds(c * H, H)]
                recvs.append(_rcopy(slot, slot, ss, rs, 3 * a + k, (px, py, c)))
        return sends, recvs

    return _Plan(ws, [jax.ShapeDtypeStruct((4,) + w.shape, w.dtype) for w in ws], 3 * len(ws), copies)


def plan_gather_d2d(w4s):
    def copies(in_refs, out_refs, ss, rs):
        x, y, c = _me()
        sends, recvs = [], []
        for a, out_ref in enumerate(out_refs):
            H = out_ref.shape[1] // 2
            for k, (px, py) in enumerate(_other_chips(x, y)):
                mine = out_ref.at[2 * px + py, pl.ds(c * H, H)]
                theirs = out_ref.at[2 * px + py, pl.ds((1 - c) * H, H)]
                sends.append(_rcopy(mine, mine, ss, rs, 3 * a + k, (x, y, 1 - c)))
                recvs.append(_rcopy(theirs, theirs, ss, rs, 3 * a + k, (x, y, 1 - c)))
        return sends, recvs

    return _Plan(w4s, [jax.ShapeDtypeStruct(w.shape, w.dtype) for w in w4s], 3 * len(w4s), copies, aliased=True)


def plan_swap_halves(gs):
    def copies(in_refs, out_refs, ss, rs):
        x, y, c = _me()
        sends, recvs = [], []
        for a, (g_ref, out_ref) in enumerate(zip(in_refs, out_refs)):
            H = g_ref.shape[1] // 2
            for k in range(4):
                sends.append(_rcopy(g_ref.at[k, pl.ds((1 - c) * H, H)], out_ref.at[k], ss, rs, 4 * a + k, (x, y, 1 - c)))
                recvs.append(_rcopy(g_ref.at[k, pl.ds(c * H, H)], out_ref.at[k], ss, rs, 4 * a + k, (x, y, 1 - c)))
        return sends, recvs

    return _Plan(gs, [jax.ShapeDtypeStruct((4, g.shape[1] // 2, g.shape[2]), g.dtype) for g in gs], 4 * len(gs), copies)


def plan_scatter_ici(ps):
    def copies(in_refs, out_refs, ss, rs):
        x, y, c = _me()
        j = 2 * x + y
        sends, recvs = [], []
        for a, (p_ref, out_ref) in enumerate(zip(in_refs, out_refs)):
            for k, (px, py) in enumerate(_other_chips(x, y)):
                sends.append(_rcopy(p_ref.at[2 * px + py], out_ref.at[j], ss, rs, 3 * a + k, (px, py, c)))
                slot = out_ref.at[2 * px + py]
                recvs.append(_rcopy(slot, slot, ss, rs, 3 * a + k, (px, py, c)))
        return sends, recvs

    return _Plan(ps, [jax.ShapeDtypeStruct(p.shape, p.dtype) for p in ps], 3 * len(ps), copies)


def plan_join_halves(fulls):
    def copies(in_refs, out_refs, ss, rs):
        x, y, c = _me()
        sends, recvs = [], []
        for a, out_ref in enumerate(out_refs):
            H = out_ref.shape[0] // 2
            mine, theirs = out_ref.at[pl.ds(c * H, H)], out_ref.at[pl.ds((1 - c) * H, H)]
            sends.append(_rcopy(mine, mine, ss, rs, a, (x, y, 1 - c)))
            recvs.append(_rcopy(theirs, theirs, ss, rs, a, (x, y, 1 - c)))
        return sends, recvs

    return _Plan(fulls, [jax.ShapeDtypeStruct(f.shape, f.dtype) for f in fulls], len(fulls), copies, aliased=True)


def call_with_plans(body, plans, *, grid, in_specs, out_specs, out_shape, scratch_shapes, args, sem, name):
    plans = list(plans or [])
    n_in, n_out, n_scr = len(in_specs), len(out_specs), len(scratch_shapes)
    c_in = [len(p.ins) for p in plans]
    c_out = [len(p.out_shapes) for p in plans]
    steps = math.prod(grid) if grid else 1

    def wrapped(*refs):
        pos = 0

        def take(n):
            nonlocal pos
            out = refs[pos:pos + n]
            pos += n
            return out

        ins = take(n_in)
        cins = [take(n) for n in c_in]
        outs = take(n_out)
        couts = [take(n) for n in c_out]
        scr = take(n_scr)
        sems = [take(2) for _ in plans]
        def start_all():
            for p, ci, co, (ss, rs) in zip(plans, cins, couts, sems):
                p.start(ci, co, ss, rs)

        def finish_all():
            for p, ci, co, (ss, rs) in zip(plans, cins, couts, sems):
                p.finish(ci, co, ss, rs)

        if plans and grid:
            idx = 0
            for ax, g in enumerate(grid):
                idx = idx * g + pl.program_id(ax)
            pl.when(idx == 0)(start_all)
        elif plans:
            start_all()
        if body is not None:
            body(*ins, *outs, *scr)
        if plans and grid:
            pl.when(idx == steps - 1)(finish_all)
        elif plans:
            finish_all()

    aliases = {}
    i_pos, o_pos = n_in, n_out
    for p, ni, no in zip(plans, c_in, c_out):
        if p.aliased:
            aliases.update({i_pos + t: o_pos + t for t in range(ni)})
        i_pos += ni
        o_pos += no
    kwargs = dict(grid=grid) if grid else {}
    if aliases:
        kwargs["input_output_aliases"] = aliases
    res = pl.pallas_call(
        wrapped, name=name, in_specs=list(in_specs) + [HBM] * sum(c_in), out_specs=list(out_specs) + [HBM] * sum(c_out),
        out_shape=list(out_shape) + [s for p in plans for s in p.out_shapes],
        scratch_shapes=list(scratch_shapes) + [pltpu.SemaphoreType.DMA((p.ncopies,)) for p in plans for _ in range(2)],
        compiler_params=_cp(*sem) if grid else pltpu.CompilerParams(vmem_limit_bytes=VMEM_LIMIT), **kwargs,
    )(*args, *[a for p in plans for a in p.ins])
    res = list(res)
    comp, rest = res[:n_out], res[n_out:]
    pouts = []
    for no in c_out:
        pouts.append(rest[:no])
        rest = rest[no:]
    return comp, pouts


def run_plans(plans, *, name):
    return call_with_plans(None, plans, grid=(), in_specs=[], out_specs=[], out_shape=[], scratch_shapes=[], args=[], sem=(), name=name)[1]


def _cat(parts, axis=-1):
    return jnp.concatenate(parts, axis=axis)


def _pairs_of_heads(a, axis, inverse=False):
    lead, tail = a.shape[:axis], a.shape[axis + 1:]
    split = (3, 2) if inverse else (2, 3)
    a = a.reshape(lead + split + (HEAD,) + tail)
    return jnp.swapaxes(a, axis, axis + 1).reshape(lead + (6 * HEAD,) + tail)


def _prep_w_in(w):
    z = lambda n: jnp.zeros((w.shape[0], n), w.dtype)
    return _cat([w[:, 0:1152], z(64), w[:, 1152:1184], z(32), _pairs_of_heads(w[:, 1184:1568], 1), w[:, 1568:1824]])


def _unprep_w_in(g):
    return _cat([g[:, 0:1152], g[:, 1216:1248], _pairs_of_heads(g[:, P_SWQ:P_SWK], 1, inverse=True), g[:, P_SWK:P_END]])


def _prep_w_uq(w):
    r = w.shape[0]
    return jnp.pad(w.reshape(r, 6, MLA_QK), ((0, 0), (0, 0), (0, LANES - MLA_QK))).reshape(r, 6 * LANES)


def _unprep_w_uq(g):
    r = g.shape[0]
    return g.reshape(r, 6, LANES)[:, :, :MLA_QK].reshape(r, 6 * MLA_QK)


def _prep_w_ukv(w):
    r = w.shape[0]
    w3 = w.reshape(r, 6, LANES)
    k = jnp.pad(w3[:, :, :HEAD], ((0, 0), (0, 0), (0, LANES - HEAD))).reshape(r, 6 * LANES)
    return _cat([k, w3[:, :, HEAD:].reshape(r, 6 * HEAD)])


def _unprep_w_ukv(g):
    r = g.shape[0]
    k = g[:, :6 * LANES].reshape(r, 6, LANES)[:, :, :HEAD]
    return _cat([k, g[:, 6 * LANES:].reshape(r, 6, HEAD)], axis=2).reshape(r, 6 * LANES)


def _prep_w_out(w):
    return _cat([w[0:640], _pairs_of_heads(w[640:], 0)], axis=0)


def _unprep_w_out(g):
    return _cat([g[0:640], _pairs_of_heads(g[640:], 0, inverse=True)], axis=0)


def _rope_tables(positions):
    half = 16
    inv_freq = jnp.power(ROPE_THETA, -jnp.arange(half, dtype=F32) / half)
    ang = positions.astype(F32)[..., None] * inv_freq
    cos, sin = jnp.cos(ang), jnp.sin(ang)
    z = lambda n: jnp.zeros(ang.shape[:-1] + (n,), F32)
    return (_cat([jnp.ones(ang.shape[:-1] + (HEAD,), F32), cos, cos, z(32)]), _cat([z(HEAD), -sin, z(16), z(32)]), _cat([z(HEAD), z(16), sin, z(32)]))


def _small_params(p):
    pad96 = lambda g: _cat([g, jnp.zeros((32,), F32)]).reshape(1, LANES)
    two = lambda g: _cat([g, g]).reshape(1, LANES)
    sinks = jnp.broadcast_to(p["sw_sinks"].reshape(2, 3).T[:, :, None], (3, 2, LANES))
    return dict(n1=p["norm1_g"].reshape(1, -1), n2=p["norm2_g"].reshape(1, -1), cq_g=p["mla_cq_g"].reshape(1, -1),
                ckv_g=p["mla_ckv_g"].reshape(1, -1), qn_g=pad96(p["mla_qn_g"]), kn_g=pad96(p["mla_kn_g"]),
                swq_g=two(p["sw_qn_g"]), swk_g=two(p["sw_kn_g"]), sinks=sinks, conv_b=_up_perm(p["conv_b"]).reshape(1, -1))


class _NoFlow:
    def plans(self, tag):
        return []

    def done(self, tag, outs):
        pass

    def add(self, key, g):
        pass


def _layer_fwd(x3, md, W, tabs, bias, tag, flow=_NoFlow()):
    Bl, S, D = x3.shape
    T = Bl * S
    n = lambda s: f"{s}_{tag}"
    two = lambda a: a.reshape(T, a.shape[-1])
    three = lambda a: a.reshape(Bl, S, a.shape[-1])
    h = rms_fwd(x3, 0, D, W["n1"], md["scale1"], md["shift1"], name=n("norm1"))
    proj = three(matmul(two(h), W["w_in"], tn=1920, name=n("in_proj")))
    (o_a, rt_a), got = sb_attn_fwd(proj, plans=flow.plans(n("sb_fwd")), name=n("sb_fwd"))
    flow.done(n("sb_fwd"), got)
    cqn = rms_fwd(proj, P_CQ // 256, 256, W["cq_g"], name=n("cq_norm"))
    ckvn = rms_fwd(proj, P_CKV // LANES, LANES, W["ckv_g"], name=n("ckv_norm"))
    qb = three(matmul(two(cqn), W["w_uq"], tm=1024, tn=768, name=n("uq")))
    kvb = three(matmul(two(ckvn), W["w_ukv"], tm=1024, tn=1152, name=n("ukv")))
    q_m = rope_norm_fwd(qb, 6, W["qn_g"], tabs, name=n("q_rope"))
    k_m = rope_norm_fwd(kvb, 6, W["kn_g"], tabs, (proj, P_SLAB // LANES), name=n("k_rope"))
    (o_b, lse_b), got = mla_attn_fwd(q_m, k_m, kvb, 6, plans=flow.plans(n("mla_fwd")), name=n("mla_fwd"))
    flow.done(n("mla_fwd"), got)
    q_c = pair_rms_fwd(proj, P_SWQ // LANES, 3, W["swq_g"], name=n("swq_norm"))
    k_c = pair_rms_fwd(proj, P_SWK // LANES, 1, W["swk_g"], name=n("swk_norm"))
    (o_c, lse_c), got = swa_attn_fwd(q_c, k_c, proj, bias, W["sinks"], plans=flow.plans(n("swa_fwd")), name=n("swa_fwd"))
    flow.done(n("swa_fwd"), got)
    mix = _cat([o_a, o_b, o_c]).astype(BF16)
    att, x1 = matmul_res(two(mix), W["w_out"], two(x3), md["gate1"], S, name=n("out_proj"))
    x1 = three(x1)
    h2 = rms_fwd(x1, 0, D, W["n2"], md["scale2"], md["shift2"], name=n("norm2"))
    up = three(matmul(two(h2), W["w_up"], tm=1024, tn=1408, name=n("up_proj")))
    a = conv_gate_fwd(up, W["conv_w"], W["conv_b"], name=n("conv_gate"))
    yd, x2 = matmul_res(two(a), W["w_down"], two(x1), md["gate2"], S, name=n("down_proj"))
    saved = dict(x=x3, h=h, proj=proj, rt_a=rt_a, cqn=cqn, ckvn=ckvn, qb=qb, kvb=kvb, q_m=q_m, k_m=k_m, o_b=o_b, lse_b=lse_b,
                 q_c=q_c, k_c=k_c, o_c=o_c, lse_c=lse_c, mix=mix, att=three(att), x1=x1, h2=h2, up=up, a=a, yd=three(yd))
    return three(x2), saved


def _layer_bwd(dx2, sv, md, W, tabs, bias, tag, flow=_NoFlow()):
    Bl, S, D = dx2.shape
    T = Bl * S
    n = lambda s: f"{s}_{tag}"
    two = lambda a: a.reshape(T, a.shape[-1])
    three = lambda a: a.reshape(Bl, S, a.shape[-1])
    g = {}
    dyb, dgate2 = gate_bwd(dx2, sv["yd"], md["gate2"], name=n("gate2_bwd"))
    da = three(matmul(two(dyb), W["w_down"], tb=True, tm=1024, tn=1408, name=n("down_dx")))
    g["w_down"] = matmul(two(sv["a"]), two(dyb), ta=True, tm=256, tn=1024, name=n("down_dw"))
    dup, dcw = conv_gate_bwd(sv["up"], W["conv_w"], W["conv_b"], da, name=n("conv_gate_bwd"))
    dh2 = three(matmul(two(dup), W["w_up"], tb=True, tn=1024, name=n("up_dx")))
    g["w_up"] = matmul(two(sv["h2"]), two(dup), ta=True, tn=1408, name=n("up_dw"))
    dx1, dn2, dsc2, dsh2 = rms_bwd(sv["x1"], 0, D, dh2, W["n2"], md["scale2"], dx2, name=n("norm2_bwd"))
    dmo, dgate1 = gate_bwd(dx1, sv["att"], md["gate1"], name=n("gate1_bwd"))
    dmix = three(matmul(two(dmo), W["w_out"], tb=True, tn=1024, out_dtype=BF16, name=n("out_dx")))
    g["w_out"] = matmul(two(sv["mix"]), two(dmo), ta=True, tn=1024, name=n("out_dw"))
    proj = sv["proj"]
    for k in ("w_down", "w_up", "w_out"):
        flow.add((tag, k), g[k])
    (dq_a, dk_a, dv_a), got = sb_attn_bwd(proj, sv["rt_a"], dmix[:, :, 0:256], plans=flow.plans(n("sb_bwd")), name=n("sb_bwd"))
    flow.done(n("sb_bwd"), got)
    dq_m, dk_m, dv_b = mla_attn_bwd(sv["q_m"], sv["k_m"], sv["kvb"], 6, sv["o_b"], sv["lse_b"], dmix[:, :, 256:640], name=n("mla_bwd"))
    dqb, dqn = rope_norm_bwd(sv["qb"], 6, dq_m, W["qn_g"], tabs, name=n("q_rope_bwd"))
    dkn_x, dkn, dslab = rope_norm_bwd(sv["kvb"], 6, dk_m, W["kn_g"], tabs, (proj, P_SLAB // LANES), name=n("k_rope_bwd"))
    dkvb = _cat([dkn_x, dv_b]).astype(BF16)
    dckvn = three(matmul(two(dkvb), W["w_ukv"], tb=True, tm=1024, name=n("ukv_dx")))
    g["w_ukv"] = matmul(two(sv["ckvn"]), two(dkvb), ta=True, tn=1152, name=n("ukv_dw"))
    dcqn = three(matmul(two(dqb), W["w_uq"], tb=True, tm=1024, name=n("uq_dx")))
    g["w_uq"] = matmul(two(sv["cqn"]), two(dqb), ta=True, tn=768, name=n("uq_dw"))
    dcq, dcq_g = rms_bwd(proj, P_CQ // 256, 256, dcqn, W["cq_g"], name=n("cq_norm_bwd"))
    dckv, dckv_g = rms_bwd(proj, P_CKV // LANES, LANES, dckvn, W["ckv_g"], name=n("ckv_norm_bwd"))
    dq_c, dk_c, dv_c, dbias, dsink = swa_attn_bwd(sv["q_c"], sv["k_c"], proj, bias, W["sinks"], sv["o_c"], sv["lse_c"], dmix[:, :, 640:1024], name=n("swa_bwd"))
    dswq, dswq_g = pair_rms_bwd(proj, P_SWQ // LANES, 3, dq_c, W["swq_g"], name=n("swq_norm_bwd"))
    dswk, dswk_g = pair_rms_bwd(proj, P_SWK // LANES, 1, dk_c, W["swk_g"], name=n("swk_norm_bwd"))
    dproj = _cat([dq_a, dk_a, dv_a, dcq, dckv, dslab, dswq, dswk, dv_c]).astype(BF16)
    dh = three(matmul(two(dproj), W["w_in"], tb=True, tn=1024, name=n("in_dx")))
    g["w_in"] = matmul(two(sv["h"]), two(dproj), ta=True, tn=1920, tk=2048, name=n("in_dw"))
    dx, dn1, dsc1, dsh1 = rms_bwd(sv["x"], 0, D, dh, W["n1"], md["scale1"], dx1, name=n("norm1_bwd"))
    small = dict(n1=dn1, n2=dn2, cq_g=dcq_g, ckv_g=dckv_g, qn_g=dqn, kn_g=dkn, swq_g=dswq_g, swk_g=dswk_g, conv=dcw)
    dmods = _cat([dsh1, dsc1, dgate1, dsh2, dsc2, dgate2]).reshape(Bl, 6 * D)
    for k in ("w_ukv", "w_uq", "w_in"):
        flow.add((tag, k), g[k])
    return dx, g, small, dmods, dbias, dsink


BIG = ("w_in", "w_uq", "w_ukv", "w_out", "w_up", "w_down")
ROW_SHARDED = ("w_out", "w_down")
PREP = dict(w_in=_prep_w_in, w_uq=_prep_w_uq, w_ukv=_prep_w_ukv, w_out=_prep_w_out, w_up=_up_perm, w_down=lambda w: w)
UNPREP = dict(w_in=_unprep_w_in, w_uq=_unprep_w_uq, w_ukv=_unprep_w_ukv, w_out=_unprep_w_out, w_up=_up_perm, w_down=lambda w: w)
NCHIPS = 4


def _local_step(x, target, positions, mods, Wl, rel_flat, fwd_flow=_NoFlow(), bwd_flow=_NoFlow()):
    Bl, S, D = x.shape
    L = len(Wl)
    tabs = _rope_tables(positions)
    bucket = _bucket_table()
    bias = swa_bias(rel_flat, bucket, name="swa_bias")
    mds = []
    for l in range(L):
        parts = [mods[l, :, D * k:D * (k + 1)].reshape(Bl, 1, D) for k in range(6)]
        mds.append(dict(zip(("shift1", "scale1", "gate1", "shift2", "scale2", "gate2"), parts)))
    saved = []
    h = x
    for l in range(L):
        h, sv = _layer_fwd(h, mds[l], Wl[l], tabs, bias, f"l{l}", fwd_flow)
        saved.append(sv)
    dy, loss = loss_grad(h, target, name="loss")
    grads, smalls, dmods, dbiases, dsinks = [None] * L, [None] * L, [None] * L, [None] * L, [None] * L
    for l in reversed(range(L)):
        dy, grads[l], smalls[l], dmods[l], dbiases[l], dsinks[l] = _layer_bwd(dy, saved[l], mds[l], Wl[l], tabs, bias, f"l{l}", bwd_flow)
    drel = swa_bias_bwd(_cat(dbiases, axis=0), bucket, name="swa_bias_bwd")
    return loss, dy, grads, smalls, dmods, dsinks, drel


ATT = ("w_in", "w_uq", "w_ukv", "w_out")
FFN = ("w_up", "w_down")
GATHER_STAGES = {
    "sb_fwd_l0": ([("l0", k) for k in ("w_out",) + FFN], []),
    "mla_fwd_l0": ([("l1", k) for k in ATT + ("w_up",)], [("l0", k) for k in ("w_out",) + FFN]),
    "swa_fwd_l0": ([("l1", "w_down")], [("l1", k) for k in ATT + ("w_up",)]),
    "sb_fwd_l1": ([], [("l1", "w_down")]),
}
SCATTER_STAGES = {
    "sb_bwd_l1": [("l1", k) for k in FFN],
    "sb_bwd_l0": [("l1", k) for k in ATT] + [("l0", k) for k in FFN + ("w_out",)],
}


class _GatherFlow:
    def __init__(self, shards, chip):
        self.shards, self.chip, self.ici, self.d2d, self.pending = shards, chip, {}, {}, {}

    def early(self, keys):
        ici, = run_plans([plan_gather_ici([self.shards[k] for k in keys])], name="gather_early_ici")
        d2d, = run_plans([plan_gather_d2d(ici)], name="gather_early_d2d")
        self.d2d.update(zip(keys, d2d))

    def plans(self, tag):
        ici_keys, d2d_keys = GATHER_STAGES.get(tag, ([], []))
        plans = []
        if d2d_keys:
            plans.append(plan_gather_d2d([self.ici[k] for k in d2d_keys]))
        if ici_keys:
            plans.append(plan_gather_ici([self.shards[k] for k in ici_keys]))
        self.pending[tag] = (ici_keys, d2d_keys)
        return plans

    def done(self, tag, outs):
        ici_keys, d2d_keys = self.pending.pop(tag, ([], []))
        outs = list(outs)
        if d2d_keys:
            self.d2d.update(zip(d2d_keys, outs.pop(0)))
        if ici_keys:
            self.ici.update(zip(ici_keys, outs.pop(0)))

    def weight(self, key):
        k = key[1]
        own = self.shards[key]
        r, cc = own.shape
        w4 = lax.dynamic_update_slice(self.d2d[key], own[None], (self.chip, 0, 0))
        fw = w4.reshape(NCHIPS * r, cc) if k in ROW_SHARDED else jnp.transpose(w4, (1, 0, 2)).reshape(r, NCHIPS * cc)
        return PREP[k](fw)


class _LayerWeights(dict):
    def __init__(self, small, flow, tag):
        super().__init__(small)
        self.flow, self.tag = flow, tag

    def __missing__(self, k):
        self[k] = self.flow.weight((self.tag, k))
        return self[k]


class _ScatterFlow:
    def __init__(self, shapes, sel, c_arr):
        self.shapes, self.sel, self.c_arr = shapes, sel, c_arr
        self.g, self.pairs, self.landed, self.pending = {}, {}, {}, {}

    def add(self, key, g):
        self.g[key] = g

    def _pairs(self, keys, label):
        g4s = []
        for key in keys:
            k = key[1]
            r, cc = self.shapes[k]
            gk = UNPREP[k](self.g[key])
            g4 = gk.reshape(NCHIPS, r, cc) if k in ROW_SHARDED else jnp.transpose(gk.reshape(r, NCHIPS, cc), (1, 0, 2))
            g4s.append(g4.astype(BF16))
        theirs, = run_plans([plan_swap_halves(g4s)], name=f"rs_swap_{label}")
        pairs = [pair_add_half(g4, th, self.c_arr, name=f"rs_pair_add_{key[1]}_{key[0]}") for key, g4, th in zip(keys, g4s, theirs)]
        self.pairs.update(zip(keys, pairs))
        return pairs

    def plans(self, tag):
        keys = SCATTER_STAGES.get(tag, [])
        self.pending[tag] = keys
        return [plan_scatter_ici(self._pairs(keys, tag))] if keys else []

    def done(self, tag, outs):
        keys = self.pending.pop(tag, [])
        if keys:
            self.landed.update(zip(keys, outs[0]))

    def finish(self):
        rest = [key for key in self.g if key not in self.pairs]
        if rest:
            landed, = run_plans([plan_scatter_ici(self._pairs(rest, "rest"))], name="rs_scatter_rest")
            self.landed.update(zip(rest, landed))
        keys = list(self.pairs)
        fulls = [chip_sum_into(self.landed[key], self.pairs[key], self.sel, name=f"rs_chip_sum_{key[1]}_{key[0]}") for key in keys]
        joined, = run_plans([plan_join_halves(fulls)], name="rs_join_halves")
        return dict(zip(keys, joined))


WEIGHTS = ("rel_table", "norm1_g", "norm2_g", "w_ada", "b_ada", "w_in", "mla_cq_g", "w_uq", "mla_ckv_g", "w_ukv", "mla_qn_g", "mla_kn_g",
           "sw_qn_g", "sw_kn_g", "sw_sinks", "w_out", "w_up", "conv_w", "conv_b", "w_down")
SMALL = tuple(n for n in WEIGHTS if n not in BIG + ("w_ada",))


def kernel(x, c, positions, rel_table, norm1_g, norm2_g, w_ada, b_ada, w_in, mla_cq_g, w_uq, mla_ckv_g, w_ukv, mla_qn_g, mla_kn_g, sw_qn_g, sw_kn_g, sw_sinks, w_out, w_up, conv_w, conv_b, w_down, loss_target, m_rel_table, m_norm1_g, m_norm2_g, m_w_ada, m_b_ada, m_w_in, m_mla_cq_g, m_w_uq, m_mla_ckv_g, m_w_ukv, m_mla_qn_g, m_mla_kn_g, m_sw_qn_g, m_sw_kn_g, m_sw_sinks, m_w_out, m_w_up, m_conv_w, m_conv_b, m_w_down, v_rel_table, v_norm1_g, v_norm2_g, v_w_ada, v_b_ada, v_w_in, v_mla_cq_g, v_w_uq, v_mla_ckv_g, v_w_ukv, v_mla_qn_g, v_mla_kn_g, v_sw_qn_g, v_sw_kn_g, v_sw_sinks, v_w_out, v_w_up, v_conv_w, v_conv_b, v_w_down):
    w = dict(rel_table=rel_table, norm1_g=norm1_g, norm2_g=norm2_g, w_ada=w_ada, b_ada=b_ada, w_in=w_in, mla_cq_g=mla_cq_g, w_uq=w_uq,
             mla_ckv_g=mla_ckv_g, w_ukv=w_ukv, mla_qn_g=mla_qn_g, mla_kn_g=mla_kn_g, sw_qn_g=sw_qn_g, sw_kn_g=sw_kn_g, sw_sinks=sw_sinks,
             w_out=w_out, w_up=w_up, conv_w=conv_w, conv_b=conv_b, w_down=w_down)
    m = dict(rel_table=m_rel_table, norm1_g=m_norm1_g, norm2_g=m_norm2_g, w_ada=m_w_ada, b_ada=m_b_ada, w_in=m_w_in, mla_cq_g=m_mla_cq_g,
             w_uq=m_w_uq, mla_ckv_g=m_mla_ckv_g, w_ukv=m_w_ukv, mla_qn_g=m_mla_qn_g, mla_kn_g=m_mla_kn_g, sw_qn_g=m_sw_qn_g,
             sw_kn_g=m_sw_kn_g, sw_sinks=m_sw_sinks, w_out=m_w_out, w_up=m_w_up, conv_w=m_conv_w, conv_b=m_conv_b, w_down=m_w_down)
    v = dict(rel_table=v_rel_table, norm1_g=v_norm1_g, norm2_g=v_norm2_g, w_ada=v_w_ada, b_ada=v_b_ada, w_in=v_w_in, mla_cq_g=v_mla_cq_g,
             w_uq=v_w_uq, mla_ckv_g=v_mla_ckv_g, w_ukv=v_w_ukv, mla_qn_g=v_mla_qn_g, mla_kn_g=v_mla_kn_g, sw_qn_g=v_sw_qn_g,
             sw_kn_g=v_sw_kn_g, sw_sinks=v_sw_sinks, w_out=v_w_out, w_up=v_w_up, conv_w=v_conv_w, conv_b=v_conv_b, w_down=v_w_down)
    Bl, S, D = x.shape
    L = norm1_g.shape[0]
    xi, yi, ci = _me()
    chip = 2 * xi + yi
    dev = 4 * xi + 2 * yi + ci
    ndev = 2 * NCHIPS

    shapes = {k: w[k].shape[1:] for k in BIG}
    shards = {(f"l{l}", k): cast_layer(w[k], l, name=f"cast_{k}_l{l}") for l in range(L) for k in BIG}
    gflow = _GatherFlow(shards, chip)
    gflow.early([("l0", k) for k in ("w_in", "w_uq", "w_ukv")])

    cw_cols = conv_w.shape[2]
    c_got, cw_got = allgather8([c, conv_w.reshape(L * 3, cw_cols)], name="gather_cond")
    c_all = c_got.reshape(ndev * Bl, D)
    conv_full = jnp.transpose(cw_got[0::2].reshape(NCHIPS, L, 3, cw_cols), (1, 2, 0, 3)).reshape(L, 3, NCHIPS * cw_cols)
    E = w_ada.shape[2]
    b_cols = lax.dynamic_slice(b_ada, (0, chip * E), (L, E)).reshape(L, 1, E)
    mods_cols = mods_matmul(c_all, w_ada, b_cols, name="mods")
    mods_all, = allgather8([mods_cols.reshape(L * ndev * Bl, E)], name="gather_mods")
    mods_all = jnp.transpose(mods_all[0::2].reshape(NCHIPS, L, ndev * Bl, E), (1, 2, 0, 3)).reshape(L, ndev * Bl, NCHIPS * E)
    mods = lax.dynamic_slice(mods_all, (0, dev * Bl, 0), (L, Bl, NCHIPS * E))

    Wl = []
    for l in range(L):
        Wd = _small_params({k: w[k][l] for k in SMALL if k not in ("rel_table", "b_ada", "conv_w")})
        Wd["conv_w"] = _up_perm(conv_full[l])
        Wl.append(_LayerWeights(Wd, gflow, f"l{l}"))

    sflow = _ScatterFlow(shapes, jnp.stack([chip, ci]).astype(jnp.int32), ci.reshape(1).astype(jnp.int32))
    loss, dx, _, smalls, dmods, dsinks, drel = _local_step(x, loss_target, positions, mods, Wl, rel_table.reshape(-1), gflow, sflow)
    reduced = sflow.finish()
    grad = {k: jnp.stack([reduced[(f"l{l}", k)] for l in range(L)]) for k in BIG}

    vec_names = ("n1", "n2", "cq_g", "ckv_g", "qn_g", "kn_g", "swq_g", "swk_g")
    vecs = _cat([_cat([smalls[l][k] for k in vec_names], axis=1) for l in range(L)], axis=0)
    convs = _cat([smalls[l]["conv"][0:4] for l in range(L)], axis=0)
    dm = jnp.stack(dmods, axis=1).reshape(Bl * L, 6 * D)
    dsk = jnp.stack(dsinks, axis=1).reshape(Bl * L * 6, LANES)
    got = allgather8([vecs, convs, drel, loss, dm, dsk], name="gather_small_grads")
    seq = lambda a, rows: a.reshape(ndev * Bl, rows, a.shape[-1])
    vec_s, conv_s, rel_s, loss_s, dm_s, dsk_s = sum_small(list(got[:4]) + [seq(got[4], L), seq(got[5], L * 6)], name="sum_small_grads")
    dm_all = jnp.transpose(seq(got[4], L), (1, 0, 2))
    grad["w_ada"] = ada_grad(c_all, lax.dynamic_slice(dm_all, (0, 0, chip * E), (L, ndev * Bl, E)), name="ada_grad")
    grad["b_ada"] = dm_s
    grad["sw_sinks"] = jnp.transpose(dsk_s.reshape(L, 3, 2, LANES)[:, :, :, 0], (0, 2, 1)).reshape(L, 6)
    grad["rel_table"] = rel_s[:6, :REL_BUCKETS].T
    off = 0
    for k, name_, keep in zip(vec_names, ("norm1_g", "norm2_g", "mla_cq_g", "mla_ckv_g", "mla_qn_g", "mla_kn_g", "sw_qn_g", "sw_kn_g"),
                              (D, D, 256, LANES, MLA_QK, MLA_QK, HEAD, HEAD)):
        grad[name_] = vec_s[:, off:off + keep]
        off += smalls[0][k].shape[1]
    conv = _up_perm(conv_s.reshape(L, 4, 2 * D_FF))
    grad["conv_w"] = lax.dynamic_slice(conv[:, 0:3], (0, 0, chip * cw_cols), (L, 3, cw_cols))
    grad["conv_b"] = conv[:, 3]
    loss_out = loss_s[0, 0]

    delta, new_m, new_v = {}, {}, {}
    for k in BIG + ("w_ada",):
        delta[k], new_m[k], new_v[k] = adamw(w[k], grad[k], m[k], v[k], name=f"adamw_{k}")
    outs = adamw_small(*[[src[k] for k in SMALL] for src in (w, grad, m, v)], name="adamw_small")
    for dst, o in zip((delta, new_m, new_v), outs):
        dst.update(dict(zip(SMALL, o)))
    return (loss_out, dx, *[grad[k] for k in WEIGHTS], *[delta[k] for k in WEIGHTS], *[new_m[k] for k in WEIGHTS], *[new_v[k] for k in WEIGHTS])
```

```python
import math

import jax
import jax.numpy as jnp
from jax import lax
from jax.experimental import pallas as pl
from jax.experimental.pallas import tpu as pltpu

F32 = jnp.float32
BF16 = jnp.bfloat16
MESH = pl.DeviceIdType.MESH

EPS = 1e-6
NEG = -1e30
HEAD = 64
LANES = 128
MLA_QK = 96
ROPE_THETA = 10000.0
REL_BUCKETS = 32
REL_MAX_DIST = 128
WINDOW = 128
D_FF = 2816
ADAM_LR, ADAM_B1, ADAM_B2, ADAM_EPS, ADAM_WD, ADAM_STEP = 0.001, 0.9, 0.999, 1e-08, 0.01, 10

VMEM_LIMIT = 56 * 1024 * 1024
STRIP = 32

P_SBQ, P_SBK, P_SBV, P_CQ, P_CKV, P_SLAB, P_SWQ, P_SWK, P_SWV, P_END = 0, 256, 512, 768, 1024, 1152, 1280, 1664, 1792, 1920


def _cp(*sem):
    return pltpu.CompilerParams(dimension_semantics=sem, vmem_limit_bytes=VMEM_LIMIT)


def _dot(a, b):
    return jnp.dot(a, b, preferred_element_type=F32)


def _dot_nt(a, b):
    return lax.dot_general(a, b, (((1,), (1,)), ((), ())), preferred_element_type=F32)


def _dot_tn(a, b):
    return lax.dot_general(a, b, (((0,), (0,)), ((), ())), preferred_element_type=F32)


def _lane_masks():
    lane = lax.broadcasted_iota(jnp.int32, (1, LANES), 1)
    return (lane < HEAD, lane >= HEAD)


def _tile(n, cap, align=128):
    if n <= cap:
        return n
    t = cap - cap % align
    while t >= align:
        if n % t == 0:
            return t
        t -= align
    return n


def matmul(a, b, *, ta=False, tb=False, out_dtype=F32, tm=512, tn=512, tk=8192, name):
    M, K = (a.shape[1], a.shape[0]) if ta else a.shape
    N = b.shape[0] if tb else b.shape[1]
    tm, tn, tk = _tile(M, tm), _tile(N, tn), _tile(K, tk)
    nk = K // tk

    def body(a_ref, b_ref, o_ref, *scratch):
        av = a_ref[...].astype(BF16)
        bv = b_ref[...].astype(BF16)
        if ta:
            part = _dot_tn(av, bv)
        elif tb:
            part = _dot_nt(av, bv)
        else:
            part = _dot(av, bv)
        if nk == 1:
            o_ref[...] = part.astype(out_dtype)
        else:
            acc_ref, = scratch
            k = pl.program_id(2)

            @pl.when(k == 0)
            def _():
                acc_ref[...] = part

            @pl.when(k > 0)
            def _():
                acc_ref[...] += part

            @pl.when(k == nk - 1)
            def _():
                o_ref[...] = acc_ref[...].astype(out_dtype)

    n_outer = nk == 1 and tn * b.dtype.itemsize > tm * a.dtype.itemsize
    ij = (lambda p, q: (q, p)) if n_outer else (lambda p, q: (p, q))
    a_map = (lambda p, q, k: (k, ij(p, q)[0])) if ta else (lambda p, q, k: (ij(p, q)[0], k))
    b_map = (lambda p, q, k: (ij(p, q)[1], k)) if tb else (lambda p, q, k: (k, ij(p, q)[1]))
    grid = (N // tn, M // tm, nk) if n_outer else (M // tm, N // tn, nk)
    return pl.pallas_call(
        body, name=name, grid=grid,
        in_specs=[pl.BlockSpec((tk, tm) if ta else (tm, tk), a_map), pl.BlockSpec((tn, tk) if tb else (tk, tn), b_map)],
        out_specs=pl.BlockSpec((tm, tn), lambda p, q, k: ij(p, q)),
        out_shape=jax.ShapeDtypeStruct((M, N), out_dtype),
        scratch_shapes=[] if nk == 1 else [pltpu.VMEM((tm, tn), F32)],
        compiler_params=_cp("parallel", "parallel", "arbitrary"),
    )(a, b)


def matmul_res(a, b, res, gate, seq, *, tm=512, tn=1024, name):
    M, K = a.shape
    N = b.shape[1]
    tm, tn = _tile(min(M, seq), tm), _tile(N, tn)
    per_seq = seq // tm

    def body(a_ref, b_ref, r_ref, g_ref, y_ref, x_ref):
        y = _dot(a_ref[...].astype(BF16), b_ref[...].astype(BF16))
        y_ref[...] = y
        x_ref[...] = r_ref[...] + g_ref[...] * y

    out = jax.ShapeDtypeStruct((M, N), F32)
    return pl.pallas_call(
        body, name=name, grid=(M // tm, N // tn),
        in_specs=[pl.BlockSpec((tm, K), lambda i, j: (i, 0)), pl.BlockSpec((K, tn), lambda i, j: (0, j)),
                  pl.BlockSpec((tm, tn), lambda i, j: (i, j)), pl.BlockSpec((None, 1, tn), lambda i, j: (lax.div(i, jnp.int32(per_seq)), 0, j))],
        out_specs=[pl.BlockSpec((tm, tn), lambda i, j: (i, j))] * 2,
        out_shape=[out, out], compiler_params=_cp("parallel", "parallel"),
    )(a, b, res, gate)


def rms_fwd(x3, blk, W, g, sc=None, sh=None, *, tm=512, name):
    Bl, S, _ = x3.shape
    tm = min(tm, S)
    mod = sc is not None

    def body(x_ref, g_ref, *rest):
        o_ref = rest[-1]
        x = x_ref[...]
        r = lax.rsqrt(jnp.mean(x * x, axis=-1, keepdims=True) + EPS)
        y = x * r * g_ref[...]
        if mod:
            y = y * (1.0 + rest[0][...]) + rest[1][...]
        o_ref[...] = y.astype(BF16)

    vec = pl.BlockSpec((None, 1, W), lambda b, s: (b, 0, 0))
    return pl.pallas_call(
        body, name=name, grid=(Bl, S // tm),
        in_specs=[pl.BlockSpec((None, tm, W), lambda b, s: (b, s, blk)), pl.BlockSpec((1, W), lambda b, s: (0, 0))] + ([vec, vec] if mod else []),
        out_specs=pl.BlockSpec((None, tm, W), lambda b, s: (b, s, 0)),
        out_shape=jax.ShapeDtypeStruct((Bl, S, W), BF16),
        compiler_params=_cp("parallel", "parallel"),
    )(x3, g, *([sc, sh] if mod else []))


def rms_bwd(x3, blk, W, dy3, g, sc=None, dres3=None, *, tm=256, name):
    Bl, S, _ = x3.shape
    tm = min(tm, S)
    mod = sc is not None
    res = dres3 is not None

    def body(*refs):
        x_ref, dy_ref, g_ref = refs[:3]
        k = 3
        sc_ref = dr_ref = None
        if mod:
            sc_ref = refs[k]
            k += 1
        if res:
            dr_ref = refs[k]
            k += 1
        dx_ref, dg_ref = refs[k], refs[k + 1]
        b, s = pl.program_id(0), pl.program_id(1)
        x = x_ref[...]
        dy = dy_ref[...].astype(F32)
        g = g_ref[...]
        r = lax.rsqrt(jnp.mean(x * x, axis=-1, keepdims=True) + EPS)
        n = x * r
        if mod:
            dsc_ref, dsh_ref = refs[k + 2], refs[k + 3]
            one_sc = 1.0 + sc_ref[...]

            @pl.when(s == 0)
            def _():
                dsc_ref[...] = jnp.zeros_like(dsc_ref)
                dsh_ref[...] = jnp.zeros_like(dsh_ref)

            dsh_ref[...] += jnp.sum(dy, axis=0, keepdims=True)
            dsc_ref[...] += jnp.sum(dy * n * g, axis=0, keepdims=True)
            dyn = dy * one_sc
        else:
            dyn = dy

        @pl.when((b == 0) & (s == 0))
        def _():
            dg_ref[...] = jnp.zeros_like(dg_ref)

        dg_ref[...] += jnp.sum(dyn * n, axis=0, keepdims=True)
        dn = dyn * g
        dx = r * (dn - n * jnp.mean(dn * n, axis=-1, keepdims=True))
        if res:
            dx = dx + dr_ref[...]
        dx_ref[...] = dx

    blkspec = pl.BlockSpec((None, tm, W), lambda b, s: (b, s, 0))
    vec = pl.BlockSpec((None, 1, W), lambda b, s: (b, 0, 0))
    row = pl.BlockSpec((1, W), lambda b, s: (0, 0))
    in_specs = [pl.BlockSpec((None, tm, W), lambda b, s: (b, s, blk)), blkspec, row] + ([vec] if mod else []) + ([blkspec] if res else [])
    out_specs = [blkspec, row] + ([vec, vec] if mod else [])
    out_shape = [jax.ShapeDtypeStruct((Bl, S, W), F32), jax.ShapeDtypeStruct((1, W), F32)]
    if mod:
        out_shape += [jax.ShapeDtypeStruct((Bl, 1, W), F32)] * 2
    args = [x3, dy3, g] + ([sc] if mod else []) + ([dres3] if res else [])
    return pl.pallas_call(
        body, name=name, grid=(Bl, S // tm), in_specs=in_specs, out_specs=out_specs, out_shape=out_shape,
        compiler_params=_cp("arbitrary", "arbitrary"),
    )(*args)


def pair_rms_fwd(x3, blk0, npairs, g2, *, tm=1024, name):
    Bl, S, _ = x3.shape
    tm = min(tm, S)

    def body(x_ref, g_ref, o_ref):
        lo, hi = _lane_masks()
        x = x_ref[...]
        xx = x * x
        s0 = jnp.sum(jnp.where(lo, xx, 0.0), axis=-1, keepdims=True)
        s1 = jnp.sum(jnp.where(hi, xx, 0.0), axis=-1, keepdims=True)
        r = jnp.where(lo, lax.rsqrt(s0 / HEAD + EPS), lax.rsqrt(s1 / HEAD + EPS))
        o_ref[...] = (x * r * g_ref[...]).astype(BF16)

    return pl.pallas_call(
        body, name=name, grid=(Bl, S // tm, npairs),
        in_specs=[pl.BlockSpec((None, tm, LANES), lambda b, s, p: (b, s, blk0 + p)), pl.BlockSpec((1, LANES), lambda b, s, p: (0, 0))],
        out_specs=pl.BlockSpec((None, tm, LANES), lambda b, s, p: (b, s, p)),
        out_shape=jax.ShapeDtypeStruct((Bl, S, LANES * npairs), BF16),
        compiler_params=_cp("parallel", "parallel", "parallel"),
    )(x3, g2)


def pair_rms_bwd(x3, blk0, npairs, dy3, g2, *, tm=1024, name):
    Bl, S, _ = x3.shape
    tm = min(tm, S)

    def body(x_ref, dy_ref, g_ref, dx_ref, dg_ref):
        lo, hi = _lane_masks()
        first = (pl.program_id(0) == 0) & (pl.program_id(1) == 0) & (pl.program_id(2) == 0)
        x = x_ref[...]
        dy = dy_ref[...]
        xx = x * x
        s0 = jnp.sum(jnp.where(lo, xx, 0.0), axis=-1, keepdims=True)
        s1 = jnp.sum(jnp.where(hi, xx, 0.0), axis=-1, keepdims=True)
        r = jnp.where(lo, lax.rsqrt(s0 / HEAD + EPS), lax.rsqrt(s1 / HEAD + EPS))
        n = x * r

        @pl.when(first)
        def _():
            dg_ref[...] = jnp.zeros_like(dg_ref)

        part = jnp.sum(dy * n, axis=0, keepdims=True)
        dg_ref[...] += part + pltpu.roll(part, HEAD, 1)
        dn = dy * g_ref[...]
        t = dn * n
        m0 = jnp.sum(jnp.where(lo, t, 0.0), axis=-1, keepdims=True)
        m1 = jnp.sum(jnp.where(hi, t, 0.0), axis=-1, keepdims=True)
        dx_ref[...] = r * (dn - n * (jnp.where(lo, m0, m1) / HEAD))

    return pl.pallas_call(
        body, name=name, grid=(Bl, S // tm, npairs),
        in_specs=[pl.BlockSpec((None, tm, LANES), lambda b, s, p: (b, s, blk0 + p)), pl.BlockSpec((None, tm, LANES), lambda b, s, p: (b, s, p)),
                  pl.BlockSpec((1, LANES), lambda b, s, p: (0, 0))],
        out_specs=[pl.BlockSpec((None, tm, LANES), lambda b, s, p: (b, s, p)), pl.BlockSpec((1, LANES), lambda b, s, p: (0, 0))],
        out_shape=[jax.ShapeDtypeStruct((Bl, S, LANES * npairs), F32), jax.ShapeDtypeStruct((1, LANES), F32)],
        compiler_params=_cp("arbitrary", "arbitrary", "arbitrary"),
    )(x3, dy3, g2)


def _rot(y, cos_t, sin_a, sin_b):
    return y * cos_t + pltpu.roll(y, LANES - 16, 1) * sin_a + pltpu.roll(y, 16, 1) * sin_b


def _rot_t(d, cos_t, sin_a, sin_b):
    return d * cos_t + pltpu.roll(d * sin_a, 16, 1) + pltpu.roll(d * sin_b, LANES - 16, 1)


def rope_norm_fwd(x3, nheads, g, tabs, slab=None, *, tm=1024, name):
    Bl, S, _ = x3.shape
    tm = min(tm, S)
    has_slab = slab is not None

    def body(*refs):
        x_ref, g_ref, c_ref, sa_ref, sb_ref = refs[:5]
        o_ref = refs[-1]
        x = x_ref[...]
        if has_slab:
            x = x + refs[5][...]
        r = lax.rsqrt(jnp.sum(x * x, axis=-1, keepdims=True) / MLA_QK + EPS)
        o_ref[...] = _rot(x * r * g_ref[...], c_ref[...], sa_ref[...], sb_ref[...]).astype(BF16)

    head = pl.BlockSpec((None, tm, LANES), lambda b, s, h: (b, s, h))
    tab = pl.BlockSpec((None, tm, LANES), lambda b, s, h: (b, s, 0))
    in_specs = [head, pl.BlockSpec((1, LANES), lambda b, s, h: (0, 0)), tab, tab, tab]
    args = [x3, g, *tabs]
    if has_slab:
        sblk = slab[1]
        in_specs.append(pl.BlockSpec((None, tm, LANES), lambda b, s, h: (b, s, sblk)))
        args.append(slab[0])
    return pl.pallas_call(
        body, name=name, grid=(Bl, S // tm, nheads), in_specs=in_specs, out_specs=head,
        out_shape=jax.ShapeDtypeStruct((Bl, S, LANES * nheads), BF16),
        compiler_params=_cp("parallel", "parallel", "parallel"),
    )(*args)


def rope_norm_bwd(x3, nheads, dy3, g, tabs, slab=None, *, tm=1024, name):
    Bl, S, _ = x3.shape
    tm = min(tm, S)
    has_slab = slab is not None

    def body(*refs):
        x_ref, dy_ref, g_ref, c_ref, sa_ref, sb_ref = refs[:6]
        k = 7 if has_slab else 6
        dx_ref, dg_ref = refs[k], refs[k + 1]
        h = pl.program_id(2)
        first = (pl.program_id(0) == 0) & (pl.program_id(1) == 0) & (h == 0)
        x = x_ref[...]
        if has_slab:
            x = x + refs[6][...]
        g = g_ref[...]
        r = lax.rsqrt(jnp.sum(x * x, axis=-1, keepdims=True) / MLA_QK + EPS)
        n = x * r
        d = _rot_t(dy_ref[...], c_ref[...], sa_ref[...], sb_ref[...])

        @pl.when(first)
        def _():
            dg_ref[...] = jnp.zeros_like(dg_ref)

        dg_ref[...] += jnp.sum(d * n, axis=0, keepdims=True)
        dn = d * g
        dx = r * (dn - n * (jnp.sum(dn * n, axis=-1, keepdims=True) / MLA_QK))
        dx_ref[...] = dx.astype(BF16)
        if has_slab:
            ds_ref = refs[k + 2]

            @pl.when(h == 0)
            def _():
                ds_ref[...] = dx

            @pl.when(h > 0)
            def _():
                ds_ref[...] += dx

    head = pl.BlockSpec((None, tm, LANES), lambda b, s, h: (b, s, h))
    tab = pl.BlockSpec((None, tm, LANES), lambda b, s, h: (b, s, 0))
    row = pl.BlockSpec((1, LANES), lambda b, s, h: (0, 0))
    in_specs = [head, head, row, tab, tab, tab]
    args = [x3, dy3, g, *tabs]
    out_specs = [head, row]
    out_shape = [jax.ShapeDtypeStruct((Bl, S, LANES * nheads), BF16), jax.ShapeDtypeStruct((1, LANES), F32)]
    if has_slab:
        sblk = slab[1]
        in_specs.append(pl.BlockSpec((None, tm, LANES), lambda b, s, h: (b, s, sblk)))
        args.append(slab[0])
        out_specs.append(tab)
        out_shape.append(jax.ShapeDtypeStruct((Bl, S, LANES), F32))
    return pl.pallas_call(
        body, name=name, grid=(Bl, S // tm, nheads), in_specs=in_specs, out_specs=out_specs, out_shape=out_shape,
        compiler_params=_cp("arbitrary", "arbitrary", "arbitrary"),
    )(*args)


def _softplus(z):
    return jnp.maximum(z, 0.0) + jnp.log(1.0 + jnp.exp(-jnp.abs(z)))


def _split_dots(xs, u):
    hi = [x.astype(BF16) for x in xs]
    lo = [(x - h.astype(F32)).astype(BF16) for x, h in zip(xs, hi)]
    top = [_dot(h, u) for h in hi]
    return [t + _dot(l, u) for t, l in zip(top, lo)]


SB_BLOCK = 256
SB_QBLOCK = 512


def sb_attn_fwd(proj3, *, plans=None, name):
    Bl, S, _ = proj3.shape
    tk = min(SB_BLOCK, S)
    tq = min(SB_QBLOCK, S)
    per_q = tq // tk
    scale = HEAD ** -0.5
    qb, kb0, vb0 = P_SBQ // LANES, P_SBK // LANES, P_SBV // LANES

    def body(q_ref, k_ref, v_ref, o_ref, rt_ref):
        i = pl.program_id(2)
        masks = _lane_masks()
        lane = lax.broadcasted_iota(jnp.int32, (1, LANES), 1)
        q = q_ref[...]
        qh = [jnp.where(m, q, 0.0).astype(BF16) for m in masks]
        rr = lax.broadcasted_iota(jnp.int32, (tq, tk), 0)
        cc = lax.broadcasted_iota(jnp.int32, (tq, tk), 1)
        u = (lax.broadcasted_iota(jnp.int32, (tk, tk), 0) > lax.broadcasted_iota(jnp.int32, (tk, tk), 1)).astype(BF16)

        rt_ref[...] = jnp.zeros_like(rt_ref)

        def step(j, carry, masked):
            r0, r1, acc = carry
            off = pl.multiple_of(j * tk, tk)
            kb = k_ref[pl.ds(off, tk), :].astype(BF16)
            vb = v_ref[pl.ds(off, tk), :]
            strict = (cc + j * tk) < (rr + i * tq) if masked else None
            only = (lambda t: jnp.where(strict, t, 0.0)) if masked else (lambda t: t)
            rt_ref[...] = jnp.where(lane == j, r0, jnp.where(lane == j + HEAD, r1, rt_ref[...]))
            rs, two = [r0, r1], range(2)
            z = [_dot_nt(qh[h], kb) * scale for h in two]
            sp = [_softplus(z[h]) for h in two]
            keep = [only(-sp[h]) for h in two]
            suf = _split_dots(keep, u)
            w = [only(jnp.exp((z[h] - sp[h]) + suf[h] + rs[h])) for h in two]
            pv = [_dot(w[h].astype(BF16), jnp.where(masks[h], vb, 0.0).astype(BF16)) for h in two]
            return rs[0] + jnp.sum(keep[0], axis=1, keepdims=True), rs[1] + jnp.sum(keep[1], axis=1, keepdims=True), acc + (pv[0] + pv[1])

        zero = jnp.zeros((tq, 1), F32)
        carry = (zero, zero, jnp.zeros((tq, LANES), F32))
        for t in range(per_q):
            carry = step((i + 1) * per_q - 1 - t, carry, True)
        _, _, acc = lax.fori_loop(0, i * per_q, lambda t, c: step(i * per_q - 1 - t, c, False), carry)
        o_ref[...] = acc

    seq = lambda blk0: pl.BlockSpec((None, S, LANES), lambda b, p, i: (b, 0, blk0 + p))
    out = pl.BlockSpec((None, tq, LANES), lambda b, p, i: (b, i, p))
    shp = jax.ShapeDtypeStruct((Bl, S, 2 * LANES), F32)
    return call_with_plans(
        body, plans, name=name, grid=(Bl, 2, S // tq),
        in_specs=[pl.BlockSpec((None, tq, LANES), lambda b, p, i: (b, i, qb + p)), seq(kb0), seq(vb0)],
        out_specs=[out, out], out_shape=[shp, shp], scratch_shapes=[], args=[proj3, proj3, proj3],
        sem=("arbitrary",) * 3 if plans else ("parallel", "parallel", "arbitrary"))


def sb_attn_bwd(proj3, rt3, do3, *, plans=None, name):
    Bl, S, _ = proj3.shape
    tk = min(SB_BLOCK, S)
    tq = min(SB_QBLOCK, S)
    per_q = tq // tk
    scale = HEAD ** -0.5
    qb, kb0, vb0 = P_SBQ // LANES, P_SBK // LANES, P_SBV // LANES

    def body(q_ref, k_ref, v_ref, rt_ref, do_ref, dq_ref, dk_ref, dv_ref):
        i = pl.program_id(2)

        @pl.when(i == 0)
        def _():
            dk_ref[...] = jnp.zeros_like(dk_ref)
            dv_ref[...] = jnp.zeros_like(dv_ref)

        masks = _lane_masks()
        lane = lax.broadcasted_iota(jnp.int32, (1, LANES), 1)
        q = q_ref[...]
        qh = [jnp.where(m, q, 0.0).astype(BF16) for m in masks]
        do_b = do_ref[...].astype(BF16)
        doh = [jnp.where(m, do_b, jnp.zeros_like(do_b)) for m in masks]
        rt = rt_ref[...]
        rr = lax.broadcasted_iota(jnp.int32, (tq, tk), 0)
        cc = lax.broadcasted_iota(jnp.int32, (tq, tk), 1)
        ur = lax.broadcasted_iota(jnp.int32, (tk, tk), 0)
        uc = lax.broadcasted_iota(jnp.int32, (tk, tk), 1)
        u_suffix = (ur > uc).astype(BF16)
        u_prefix = (ur < uc).astype(BF16)

        def step(j, carry, masked):
            p0, p1, dq = carry
            off = pl.multiple_of(j * tk, tk)
            kf = k_ref[pl.ds(off, tk), :]
            kb = kf.astype(BF16)
            vb = v_ref[pl.ds(off, tk), :]
            strict = (cc + j * tk) < (rr + i * tq) if masked else None
            only = (lambda t: jnp.where(strict, t, 0.0)) if masked else (lambda t: t)
            ps, two = [p0, p1], range(2)
            r_j = [jnp.sum(jnp.where(lane == j + h * HEAD, rt, 0.0), axis=1, keepdims=True) for h in two]
            z = [_dot_nt(qh[h], kb) * scale for h in two]
            dw = [_dot_nt(doh[h], jnp.where(masks[h], vb, 0.0).astype(BF16)) for h in two]
            sp = [_softplus(z[h]) for h in two]
            keep = [only(-sp[h]) for h in two]
            suf = _split_dots(keep, u_suffix)
            w = [only(jnp.exp((z[h] - sp[h]) + suf[h] + r_j[h])) for h in two]
            g = [dw[h] * w[h] for h in two]
            pre = _split_dots(g, u_prefix)
            dzb = [(only(g[h] * jnp.exp(-sp[h]) - jnp.exp(z[h] - sp[h]) * (pre[h] + ps[h])) * scale).astype(BF16) for h in two]
            dqs = [_dot(dzb[h], jnp.where(masks[h], kf, 0.0).astype(BF16)) for h in two]
            dks = [_dot_tn(dzb[h], qh[h]) for h in two]
            dvs = [_dot_tn(w[h].astype(BF16), doh[h]) for h in two]
            dk_ref[pl.ds(off, tk), :] += dks[0] + dks[1]
            dv_ref[pl.ds(off, tk), :] += dvs[0] + dvs[1]
            return ps[0] + jnp.sum(g[0], axis=1, keepdims=True), ps[1] + jnp.sum(g[1], axis=1, keepdims=True), dq + (dqs[0] + dqs[1])

        zero = jnp.zeros((tq, 1), F32)
        carry = lax.fori_loop(0, i * per_q, lambda j, c: step(j, c, False), (zero, zero, jnp.zeros((tq, LANES), F32)))
        for t in range(per_q):
            carry = step(i * per_q + t, carry, True)
        dq_ref[...] = carry[2]

    seq_in = lambda blk0: pl.BlockSpec((None, S, LANES), lambda b, p, i: (b, 0, blk0 + p))
    blk = pl.BlockSpec((None, tq, LANES), lambda b, p, i: (b, i, p))
    seq_out = pl.BlockSpec((None, S, LANES), lambda b, p, i: (b, 0, p))
    shp = jax.ShapeDtypeStruct((Bl, S, 2 * LANES), F32)
    return call_with_plans(
        body, plans, name=name, grid=(Bl, 2, S // tq),
        in_specs=[pl.BlockSpec((None, tq, LANES), lambda b, p, i: (b, i, qb + p)), seq_in(kb0), seq_in(vb0), blk, blk],
        out_specs=[blk, seq_out, seq_out], out_shape=[shp, shp, shp], scratch_shapes=[], args=[proj3, proj3, proj3, rt3, do3],
        sem=("arbitrary",) * 3 if plans else ("parallel", "parallel", "arbitrary"))


def mla_attn_fwd(q3, k3, kv3, vblk0, *, tq=512, tk=512, plans=None, name):
    Bl, S, _ = q3.shape
    tq = min(tq, S)
    tk = min(tk, tq)
    per_q = tq // tk
    scale = MLA_QK ** -0.5

    def body(q_ref, k_ref, v_ref, o_ref, lse_ref):
        i = pl.program_id(2)
        masks = _lane_masks()
        rr = lax.broadcasted_iota(jnp.int32, (tq, tk), 0)
        cc = lax.broadcasted_iota(jnp.int32, (tq, tk), 1)
        qh = [q_ref[:, h * LANES:(h + 1) * LANES] for h in range(2)]

        def step(j, carry):
            m0, l0, m1, l1, acc = carry
            off = pl.multiple_of(j * tk, tk)
            vb = v_ref[pl.ds(off, tk), :]
            causal = (cc + j * tk) <= (rr + i * tq)
            ms, ls, two = [m0, m1], [l0, l1], range(2)
            kh = [k_ref[pl.ds(off, tk), h * LANES:(h + 1) * LANES] for h in two]
            s = [jnp.where(causal, _dot_nt(qh[h], kh[h]) * scale, NEG) for h in two]
            m_new = [jnp.maximum(ms[h], jnp.max(s[h], axis=1, keepdims=True)) for h in two]
            p = [jnp.exp(s[h] - m_new[h]) for h in two]
            alpha = [jnp.exp(ms[h] - m_new[h]) for h in two]
            ls = [alpha[h] * ls[h] + jnp.sum(p[h], axis=1, keepdims=True) for h in two]
            add = [_dot(p[h].astype(BF16), jnp.where(masks[h], vb, 0.0).astype(BF16)) for h in two]
            acc = acc * jnp.where(masks[0], alpha[0], alpha[1]) + (add[0] + add[1])
            return m_new[0], ls[0], m_new[1], ls[1], acc

        neg = jnp.full((tq, 1), NEG, F32)
        zero = jnp.zeros((tq, 1), F32)
        m0, l0, m1, l1, acc = lax.fori_loop(0, (i + 1) * per_q, step, (neg, zero, neg, zero, jnp.zeros((tq, LANES), F32)))
        o_ref[...] = acc / jnp.where(masks[0], l0, l1)
        lse_ref[...] = jnp.where(masks[0], m0 + jnp.log(l0), m1 + jnp.log(l1))

    out = pl.BlockSpec((None, tq, LANES), lambda b, p, i: (b, i, p))
    shp = jax.ShapeDtypeStruct((Bl, S, 3 * LANES), F32)
    return call_with_plans(
        body, plans, name=name, grid=(Bl, 3, S // tq),
        in_specs=[pl.BlockSpec((None, tq, 2 * LANES), lambda b, p, i: (b, i, p)), pl.BlockSpec((None, S, 2 * LANES), lambda b, p, i: (b, 0, p)),
                  pl.BlockSpec((None, S, LANES), lambda b, p, i: (b, 0, vblk0 + p))],
        out_specs=[out, out], out_shape=[shp, shp], scratch_shapes=[], args=[q3, k3, kv3],
        sem=("arbitrary",) * 3 if plans else ("parallel", "parallel", "arbitrary"))


def mla_attn_bwd(q3, k3, kv3, vblk0, o3, lse3, do3, *, tq=512, tk=512, name):
    Bl, S, _ = q3.shape
    tq = min(tq, S)
    tk = min(tk, tq)
    per_q = tq // tk
    nq = S // tq
    scale = MLA_QK ** -0.5

    def body(q_ref, k_ref, v_ref, o_ref, lse_ref, do_ref, dq_ref, dk_ref, dv_ref, s_scr, dp_scr, p_scr, ds_scr):
        j = pl.program_id(2)

        @pl.when(j == 0)
        def _():
            dq_ref[...] = jnp.zeros_like(dq_ref)

        masks = _lane_masks()
        vb = v_ref[...]
        vh = [jnp.where(m, vb, 0.0).astype(BF16) for m in masks]
        kh = [k_ref[:, h * LANES:(h + 1) * LANES] for h in range(2)]
        i0 = lax.div(j, jnp.int32(per_q))

        def step(i, carry, masked):
            dk0, dk1, dv = carry
            off = pl.multiple_of(i * tq, tq)
            do_b = do_ref[pl.ds(off, tq), :].astype(BF16)
            prod = do_b.astype(F32) * o_ref[pl.ds(off, tq), :]
            lse = lse_ref[pl.ds(off, tq), :]
            two = range(2)
            qh = [q_ref[pl.ds(off, tq), h * LANES:(h + 1) * LANES] for h in two]
            doh = [jnp.where(masks[h], do_b, jnp.zeros_like(do_b)) for h in two]
            delta = [jnp.sum(jnp.where(masks[h], prod, 0.0), axis=1, keepdims=True) for h in two]
            lse_h = [lse[:, h * HEAD:h * HEAD + 1] for h in two]
            for h in two:
                s_scr[h] = _dot_nt(qh[h], kh[h])
            for h in two:
                dp_scr[h] = _dot_nt(doh[h], vh[h])
            for r0 in range(0, tq, STRIP):
                rows = slice(r0, r0 + STRIP)
                for h in two:
                    s = s_scr[h, rows, :] * scale
                    if masked:
                        rr = lax.broadcasted_iota(jnp.int32, (STRIP, tk), 0) + (i * tq + r0)
                        cc = lax.broadcasted_iota(jnp.int32, (STRIP, tk), 1) + j * tk
                        s = jnp.where(cc <= rr, s, NEG)
                    p = jnp.exp(s - lse_h[h][rows])
                    p_scr[h, rows, :] = p.astype(BF16)
                    ds_scr[h, rows, :] = (p * (dp_scr[h, rows, :] - delta[h][rows])).astype(BF16)
            dqs = [_dot(ds_scr[h], kh[h]) * scale for h in two]
            dks = [dk0 + _dot_tn(ds_scr[0], qh[0]), dk1 + _dot_tn(ds_scr[1], qh[1])]
            dv = dv + _dot_tn(p_scr[0], doh[0]) + _dot_tn(p_scr[1], doh[1])
            for h in two:
                dq_ref[pl.ds(off, tq), h * LANES:(h + 1) * LANES] += dqs[h]
            return dks[0], dks[1], dv

        zero = jnp.zeros((tk, LANES), F32)
        carry = step(i0, (zero, zero, zero), True)
        dk0, dk1, dv = lax.fori_loop(i0 + 1, nq, lambda i, c: step(i, c, False), carry)
        dk_ref[:, 0:LANES] = dk0 * scale
        dk_ref[:, LANES:2 * LANES] = dk1 * scale
        dv_ref[...] = dv.astype(BF16)

    seq1 = pl.BlockSpec((None, S, LANES), lambda b, p, j: (b, 0, p))
    seq2 = pl.BlockSpec((None, S, 2 * LANES), lambda b, p, j: (b, 0, p))
    return pl.pallas_call(
        body, name=name, grid=(Bl, 3, S // tk),
        in_specs=[seq2, pl.BlockSpec((None, tk, 2 * LANES), lambda b, p, j: (b, j, p)),
                  pl.BlockSpec((None, tk, LANES), lambda b, p, j: (b, j, vblk0 + p)), seq1, seq1, seq1],
        out_specs=[seq2, pl.BlockSpec((None, tk, 2 * LANES), lambda b, p, j: (b, j, p)), pl.BlockSpec((None, tk, LANES), lambda b, p, j: (b, j, p))],
        out_shape=[jax.ShapeDtypeStruct((Bl, S, 6 * LANES), F32), jax.ShapeDtypeStruct((Bl, S, 6 * LANES), F32), jax.ShapeDtypeStruct((Bl, S, 3 * LANES), BF16)],
        scratch_shapes=[pltpu.VMEM((2, tq, tk), F32), pltpu.VMEM((2, tq, tk), F32), pltpu.VMEM((2, tq, tk), BF16), pltpu.VMEM((2, tq, tk), BF16)],
        compiler_params=_cp("parallel", "parallel", "arbitrary"),
    )(q3, k3, kv3, o3, lse3, do3)


def _bucket_table():
    a = jnp.arange(WINDOW)[:, None]
    b = jnp.arange(2 * WINDOW)[None, :]
    dist = WINDOW + a - b
    max_exact = REL_BUCKETS // 2
    n = jnp.maximum(dist, 0)
    nf = jnp.maximum(n, 1).astype(F32)
    large = max_exact + (jnp.log(nf / max_exact) / math.log(REL_MAX_DIST / max_exact) * (REL_BUCKETS - max_exact)).astype(jnp.int32)
    large = jnp.minimum(large, REL_BUCKETS - 1)
    bucket = jnp.where(n < max_exact, n, large)
    return jnp.where((dist >= 0) & (dist < WINDOW), bucket, -1).astype(jnp.int32)


def swa_bias(rel_flat, bucket, *, name):
    def body(t_ref, b_ref, o_ref):
        bk = b_ref[...]
        for p in range(3):
            for hh in range(2):
                h = hh * 3 + p
                acc = jnp.full(bk.shape, NEG, F32)
                for b in range(REL_BUCKETS):
                    acc = jnp.where(bk == b, t_ref[b * 6 + h], acc)
                o_ref[p, hh] = acc

    return pl.pallas_call(
        body, name=name,
        in_specs=[pl.BlockSpec(memory_space=pltpu.SMEM), pl.BlockSpec(memory_space=pltpu.VMEM)],
        out_specs=pl.BlockSpec(memory_space=pltpu.VMEM),
        out_shape=jax.ShapeDtypeStruct((3, 2, WINDOW, 2 * WINDOW), F32),
    )(rel_flat, bucket)


def swa_bias_bwd(dbias, bucket, *, name):
    Bl = dbias.shape[0]

    def body(d_ref, b_ref, o_ref):
        bk = b_ref[...]
        lane = lax.broadcasted_iota(jnp.int32, (1, LANES), 1)
        rows = []
        for h in range(6):
            hh, p = divmod(h, 3)
            d = d_ref[0, p, hh]
            for bl in range(1, Bl):
                d = d + d_ref[bl, p, hh]
            row = jnp.zeros((1, LANES), F32)
            for b in range(REL_BUCKETS):
                s = jnp.sum(jnp.sum(jnp.where(bk == b, d, 0.0), axis=1, keepdims=True), axis=0, keepdims=True)
                row = row + jnp.where(lane == b, s, 0.0)
            rows.append(row)
        rows += [jnp.zeros((1, LANES), F32)] * 2
        o_ref[...] = jnp.concatenate(rows, axis=0)

    return pl.pallas_call(
        body, name=name,
        in_specs=[pl.BlockSpec(memory_space=pltpu.VMEM)] * 2, out_specs=pl.BlockSpec(memory_space=pltpu.VMEM),
        out_shape=jax.ShapeDtypeStruct((8, LANES), F32),
    )(dbias, bucket)


SWA_QBLOCKS = 8


def _swa_specs(vblk, nqb):
    rows = nqb * WINDOW
    cur = lambda blk: pl.BlockSpec((None, rows, LANES), lambda b, p, n: (b, n, blk))
    prev = lambda blk: pl.BlockSpec((None, WINDOW, LANES), lambda b, p, n: (b, jnp.maximum(n * nqb - 1, 0), blk))
    return [pl.BlockSpec((None, rows, LANES), lambda b, p, n: (b, n, p)), cur(0), prev(0), cur(vblk), prev(vblk),
            pl.BlockSpec((None, 2, WINDOW, 2 * WINDOW), lambda b, p, n: (p, 0, 0, 0)), pl.BlockSpec((None, 2, LANES), lambda b, p, n: (p, 0, 0))]


def _rows128(ref, m):
    return ref[m * WINDOW:(m + 1) * WINDOW, :]


def _swa_logits(qh, kp, kc, bias_h, first, scale):
    sp = jnp.where(first, NEG, _dot_nt(qh, kp) * scale + bias_h[:, :WINDOW])
    sc = _dot_nt(qh, kc) * scale + bias_h[:, WINDOW:]
    return sp, sc


def swa_attn_fwd(qn3, kn3, proj3, bias, sinks, *, plans=None, name):
    Bl, S, _ = qn3.shape
    scale = HEAD ** -0.5
    nqb = min(SWA_QBLOCKS, S // WINDOW)

    def body(q_ref, kc_ref, kp_ref, vc_ref, vp_ref, b_ref, s_ref, o_ref, lse_ref):
        seq_start = pl.program_id(2) == 0
        masks = _lane_masks()
        chains = [(m_, h) for m_ in range(nqb) for h in range(2)]
        kp = [kp_ref[...] if m_ == 0 else _rows128(kc_ref, m_ - 1) for m_ in range(nqb)]
        vp = [vp_ref[...] if m_ == 0 else _rows128(vc_ref, m_ - 1) for m_ in range(nqb)]
        kc = [_rows128(kc_ref, m_) for m_ in range(nqb)]
        vc = [_rows128(vc_ref, m_) for m_ in range(nqb)]
        sink = [s_ref[h:h + 1, 0:1] for h in range(2)]
        logits = {}
        for m_, h in chains:
            q = _rows128(q_ref, m_)
            qh = jnp.where(masks[h], q, jnp.zeros_like(q))
            logits[m_, h] = _swa_logits(qh, kp[m_], kc[m_], b_ref[h], seq_start if m_ == 0 else False, scale)
        mx = {c: jnp.maximum(jnp.maximum(jnp.max(logits[c][0], axis=1, keepdims=True), jnp.max(logits[c][1], axis=1, keepdims=True)), sink[c[1]])
              for c in chains}
        ex = {c: (jnp.exp(logits[c][0] - mx[c]), jnp.exp(logits[c][1] - mx[c])) for c in chains}
        den = {c: jnp.sum(ex[c][0], axis=1, keepdims=True) + jnp.sum(ex[c][1], axis=1, keepdims=True) + jnp.exp(sink[c[1]] - mx[c]) for c in chains}
        inv = {c: 1.0 / den[c] for c in chains}
        out = {}
        for m_, h in chains:
            c = (m_, h)
            out[c] = (_dot((ex[c][0] * inv[c]).astype(BF16), jnp.where(masks[h], vp[m_], 0.0).astype(BF16))
                      + _dot((ex[c][1] * inv[c]).astype(BF16), jnp.where(masks[h], vc[m_], 0.0).astype(BF16)))
        for m_ in range(nqb):
            o_ref[m_ * WINDOW:(m_ + 1) * WINDOW, :] = out[m_, 0] + out[m_, 1]
            lse_ref[m_ * WINDOW:(m_ + 1) * WINDOW, :] = jnp.where(masks[0], mx[m_, 0] + jnp.log(den[m_, 0]), mx[m_, 1] + jnp.log(den[m_, 1]))

    out = pl.BlockSpec((None, nqb * WINDOW, LANES), lambda b, p, n: (b, n, p))
    shp = jax.ShapeDtypeStruct((Bl, S, 3 * LANES), F32)
    return call_with_plans(
        body, plans, name=name, grid=(Bl, 3, S // (nqb * WINDOW)), in_specs=_swa_specs(P_SWV // LANES, nqb),
        out_specs=[out, out], out_shape=[shp, shp], scratch_shapes=[], args=[qn3, kn3, kn3, proj3, proj3, bias, sinks],
        sem=("arbitrary",) * 3 if plans else ("parallel", "parallel", "arbitrary"))


def swa_attn_bwd(qn3, kn3, proj3, bias, sinks, o3, lse3, do3, *, name):
    Bl, S, _ = qn3.shape
    scale = HEAD ** -0.5
    nqb = min(SWA_QBLOCKS, S // WINDOW)
    rows = nqb * WINDOW

    def body(q_ref, kc_ref, kp_ref, vc_ref, vp_ref, b_ref, s_ref, o_ref, lse_ref, do_ref,
             dq_ref, dk_ref, dv_ref, db_ref, dsk_ref):
        p_id, n = pl.program_id(1), pl.program_id(2)
        seq_start = n == 0

        @pl.when((p_id == 0) & seq_start)
        def _():
            dk_ref[...] = jnp.zeros_like(dk_ref)
            dv_ref[...] = jnp.zeros_like(dv_ref)

        @pl.when(seq_start)
        def _():
            db_ref[...] = jnp.zeros_like(db_ref)
            dsk_ref[...] = jnp.zeros_like(dsk_ref)

        masks = _lane_masks()
        zero = jnp.zeros((WINDOW, LANES), F32)
        chains = [(m_, h) for m_ in range(nqb) for h in range(2)]
        kp = [kp_ref[...] if m_ == 0 else _rows128(kc_ref, m_ - 1) for m_ in range(nqb)]
        vp = [vp_ref[...] if m_ == 0 else _rows128(vc_ref, m_ - 1) for m_ in range(nqb)]
        kc = [_rows128(kc_ref, m_) for m_ in range(nqb)]
        vc = [_rows128(vc_ref, m_) for m_ in range(nqb)]
        do_b = [_rows128(do_ref, m_).astype(BF16) for m_ in range(nqb)]
        prod = [do_b[m_].astype(F32) * _rows128(o_ref, m_) for m_ in range(nqb)]
        lse = [_rows128(lse_ref, m_) for m_ in range(nqb)]
        qh, doh, logits, lse_h, delta = {}, {}, {}, {}, {}
        for m_, h in chains:
            q = _rows128(q_ref, m_)
            qh[m_, h] = jnp.where(masks[h], q, jnp.zeros_like(q))
            doh[m_, h] = jnp.where(masks[h], do_b[m_], jnp.zeros_like(do_b[m_]))
            logits[m_, h] = _swa_logits(qh[m_, h], kp[m_], kc[m_], b_ref[h], seq_start if m_ == 0 else False, scale)
            lse_h[m_, h] = lse[m_][:, h * HEAD:h * HEAD + 1]
            delta[m_, h] = jnp.sum(jnp.where(masks[h], prod[m_], 0.0), axis=1, keepdims=True)
        pr = {c: (jnp.exp(logits[c][0] - lse_h[c]), jnp.exp(logits[c][1] - lse_h[c])) for c in chains}
        dp = {(m_, h): (_dot_nt(doh[m_, h], jnp.where(masks[h], vp[m_], 0.0).astype(BF16)),
                        _dot_nt(doh[m_, h], jnp.where(masks[h], vc[m_], 0.0).astype(BF16))) for m_, h in chains}
        ds = {c: (pr[c][0] * (dp[c][0] - delta[c]), pr[c][1] * (dp[c][1] - delta[c])) for c in chains}
        dsb = {c: ((ds[c][0] * scale).astype(BF16), (ds[c][1] * scale).astype(BF16)) for c in chains}
        dk_acc = [zero] * (nqb + 1)
        dv_acc = [zero] * (nqb + 1)
        db_acc = [[jnp.zeros((WINDOW, WINDOW), F32)] * 2 for _ in range(2)]
        dsk_acc = [jnp.zeros((1, 1), F32)] * 2
        dq = [zero] * nqb
        for m_, h in chains:
            c = (m_, h)
            db_acc[h] = [db_acc[h][0] + ds[c][0], db_acc[h][1] + ds[c][1]]
            dsk_acc[h] = dsk_acc[h] - jnp.sum(jnp.exp(s_ref[h:h + 1, 0:1] - lse_h[c]) * delta[c], axis=0, keepdims=True)
            dq[m_] = (dq[m_] + _dot(dsb[c][0], jnp.where(masks[h], kp[m_], jnp.zeros_like(kp[m_])))
                      + _dot(dsb[c][1], jnp.where(masks[h], kc[m_], jnp.zeros_like(kc[m_]))))
            dk_acc[m_] = dk_acc[m_] + _dot_tn(dsb[c][0], qh[c])
            dk_acc[m_ + 1] = dk_acc[m_ + 1] + _dot_tn(dsb[c][1], qh[c])
            dv_acc[m_] = dv_acc[m_] + _dot_tn(pr[c][0].astype(BF16), doh[c])
            dv_acc[m_ + 1] = dv_acc[m_ + 1] + _dot_tn(pr[c][1].astype(BF16), doh[c])
        for m_ in range(nqb):
            dq_ref[m_ * WINDOW:(m_ + 1) * WINDOW, :] = dq[m_]
        for h in range(2):
            db_ref[h, :, 0:WINDOW] += db_acc[h][0]
            db_ref[h, :, WINDOW:2 * WINDOW] += db_acc[h][1]
            dsk_ref[h:h + 1, :] += jnp.broadcast_to(dsk_acc[h], (1, LANES))
        offp = pl.multiple_of(jnp.maximum(n * nqb - 1, 0) * WINDOW, WINDOW)
        dk_ref[pl.ds(offp, WINDOW), :] += dk_acc[0]
        dv_ref[pl.ds(offp, WINDOW), :] += dv_acc[0]
        for m_ in range(nqb):
            off = pl.multiple_of(n * rows + m_ * WINDOW, WINDOW)
            dk_ref[pl.ds(off, WINDOW), :] += dk_acc[m_ + 1]
            dv_ref[pl.ds(off, WINDOW), :] += dv_acc[m_ + 1]

    blk = pl.BlockSpec((None, rows, LANES), lambda b, p, n: (b, n, p))
    seq = pl.BlockSpec((None, S, LANES), lambda b, p, n: (b, 0, 0))
    return pl.pallas_call(
        body, name=name, grid=(Bl, 3, S // rows), in_specs=_swa_specs(P_SWV // LANES, nqb) + [blk, blk, blk],
        out_specs=[blk, seq, seq, pl.BlockSpec((None, None, 2, WINDOW, 2 * WINDOW), lambda b, p, n: (b, p, 0, 0, 0)),
                   pl.BlockSpec((None, None, 2, LANES), lambda b, p, n: (b, p, 0, 0))],
        out_shape=[jax.ShapeDtypeStruct((Bl, S, 3 * LANES), F32), jax.ShapeDtypeStruct((Bl, S, LANES), F32), jax.ShapeDtypeStruct((Bl, S, LANES), F32),
                   jax.ShapeDtypeStruct((Bl, 3, 2, WINDOW, 2 * WINDOW), F32), jax.ShapeDtypeStruct((Bl, 3, 2, LANES), F32)],
        compiler_params=_cp("arbitrary", "arbitrary", "arbitrary"),
    )(qn3, kn3, kn3, proj3, proj3, bias, sinks, o3, lse3, do3)


CONV_ROWS = 64
CONV_LANES = 128


def _conv_strip(x_ref, h_ref, w, b, r0, cols, first_blk):
    x = x_ref[r0:r0 + CONV_ROWS, cols]
    if r0 == 0:
        rows = lax.broadcasted_iota(jnp.int32, x.shape, 0)
        h6 = jnp.where(first_blk, 0.0, h_ref[6:7, cols])
        h7 = jnp.where(first_blk, 0.0, h_ref[7:8, cols])
        x1 = jnp.where(rows == 0, h7, pltpu.roll(x, 1, 0))
        x2 = jnp.where(rows == 0, h6, jnp.where(rows == 1, h7, pltpu.roll(x, 2, 0)))
    else:
        x1 = x_ref[r0 - 1:r0 - 1 + CONV_ROWS, cols]
        x2 = x_ref[r0 - 2:r0 - 2 + CONV_ROWS, cols]
    return w[0:1] * x2 + w[1:2] * x1 + w[2:3] * x + b, x, x1, x2


FF_BLK = D_FF // 2


def _up_perm(a):
    q = FF_BLK
    return _cat([a[..., 0:q], a[..., 2 * q:3 * q], a[..., q:2 * q], a[..., 3 * q:4 * q]])


def conv_gate_fwd(up3, cw, cb, *, tm=256, name):
    Bl, S, _ = up3.shape
    tm = min(tm, S)
    W = 2 * FF_BLK

    def body(x_ref, h_ref, w_ref, b_ref, o_ref):
        first = pl.program_id(1) == 0

        def chunk(c, carry):
            cg = pl.ds(pl.multiple_of(c * CONV_LANES, CONV_LANES), CONV_LANES)
            cv = pl.ds(pl.multiple_of(FF_BLK + c * CONV_LANES, CONV_LANES), CONV_LANES)
            wg, wv, bg, bv = w_ref[:, cg], w_ref[:, cv], b_ref[:, cg], b_ref[:, cv]
            for r0 in range(0, tm, CONV_ROWS):
                ug = _conv_strip(x_ref, h_ref, wg, bg, r0, cg, first)[0]
                uv = _conv_strip(x_ref, h_ref, wv, bv, r0, cv, first)[0]
                o_ref[r0:r0 + CONV_ROWS, cg] = (ug * jax.nn.sigmoid(ug) * uv).astype(BF16)
            return carry

        lax.fori_loop(0, FF_BLK // CONV_LANES, chunk, 0)

    hb = tm // 8
    return pl.pallas_call(
        body, name=name, grid=(Bl, S // tm, 2),
        in_specs=[pl.BlockSpec((None, tm, W), lambda b, s, c: (b, s, c)),
                  pl.BlockSpec((None, 8, W), lambda b, s, c: (b, jnp.maximum(s * hb - 1, 0), c)),
                  pl.BlockSpec((3, W), lambda b, s, c: (0, c)), pl.BlockSpec((1, W), lambda b, s, c: (0, c))],
        out_specs=pl.BlockSpec((None, tm, FF_BLK), lambda b, s, c: (b, s, c)),
        out_shape=jax.ShapeDtypeStruct((Bl, S, D_FF), BF16),
        compiler_params=_cp("parallel", "parallel", "parallel"),
    )(up3, up3, cw, cb)


def conv_gate_bwd(up3, cw, cb, da3, *, tm=256, name):
    Bl, S, _ = up3.shape
    tm = min(tm, S)
    ns = S // tm
    W = 2 * FF_BLK

    def body(x_ref, h_ref, w_ref, b_ref, da_ref, dup_ref, dw_ref, nxt_ref, du_scr):
        b, s = pl.program_id(1), pl.program_id(2)
        seq_end = s == 0
        first = s == ns - 1

        @pl.when((b == 0) & seq_end)
        def _():
            dw_ref[...] = jnp.zeros_like(dw_ref)

        def du_chunk(c, carry):
            cg = pl.ds(pl.multiple_of(c * CONV_LANES, CONV_LANES), CONV_LANES)
            cv = pl.ds(pl.multiple_of(FF_BLK + c * CONV_LANES, CONV_LANES), CONV_LANES)
            wg, wv, bg, bv = w_ref[:, cg], w_ref[:, cv], b_ref[:, cg], b_ref[:, cv]
            acc_g = [jnp.zeros((1, CONV_LANES), F32)] * 4
            acc_v = [jnp.zeros((1, CONV_LANES), F32)] * 4
            for r0 in range(0, tm, CONV_ROWS):
                ug, xg, xg1, xg2 = _conv_strip(x_ref, h_ref, wg, bg, r0, cg, first)
                uv, xv, xv1, xv2 = _conv_strip(x_ref, h_ref, wv, bv, r0, cv, first)
                da = da_ref[r0:r0 + CONV_ROWS, cg].astype(F32)
                sg = jax.nn.sigmoid(ug)
                dug = da * uv * sg * (1.0 + ug * (1.0 - sg))
                duv = da * ug * sg
                du_scr[r0:r0 + CONV_ROWS, cg] = dug
                du_scr[r0:r0 + CONV_ROWS, cv] = duv
                col = lambda t: jnp.sum(t, axis=0, keepdims=True)
                acc_g = [acc_g[0] + col(dug * xg2), acc_g[1] + col(dug * xg1), acc_g[2] + col(dug * xg), acc_g[3] + col(dug)]
                acc_v = [acc_v[0] + col(duv * xv2), acc_v[1] + col(duv * xv1), acc_v[2] + col(duv * xv), acc_v[3] + col(duv)]
            for t in range(4):
                dw_ref[t:t + 1, cg] += acc_g[t]
                dw_ref[t:t + 1, cv] += acc_v[t]
            return carry

        lax.fori_loop(0, FF_BLK // CONV_LANES, du_chunk, 0)
        du_scr[tm:tm + 8, :] = jnp.where(seq_end, 0.0, nxt_ref[...])

        def dup_chunk(c, carry):
            cols = pl.ds(pl.multiple_of(c * CONV_LANES, CONV_LANES), CONV_LANES)
            w = w_ref[:, cols]
            for r0 in range(0, tm, CONV_ROWS):
                d0 = du_scr[r0:r0 + CONV_ROWS, cols]
                d1 = du_scr[r0 + 1:r0 + 1 + CONV_ROWS, cols]
                d2 = du_scr[r0 + 2:r0 + 2 + CONV_ROWS, cols]
                dup_ref[r0:r0 + CONV_ROWS, cols] = (w[2:3] * d0 + w[1:2] * d1 + w[0:1] * d2).astype(BF16)
            return carry

        lax.fori_loop(0, W // CONV_LANES, dup_chunk, 0)
        nxt_ref[...] = du_scr[0:8, :]

    hb = tm // 8
    rb = lambda s: ns - 1 - s
    return pl.pallas_call(
        body, name=name, grid=(2, Bl, ns),
        in_specs=[pl.BlockSpec((None, tm, W), lambda c, b, s: (b, rb(s), c)),
                  pl.BlockSpec((None, 8, W), lambda c, b, s: (b, jnp.maximum(rb(s) * hb - 1, 0), c)),
                  pl.BlockSpec((3, W), lambda c, b, s: (0, c)), pl.BlockSpec((1, W), lambda c, b, s: (0, c)),
                  pl.BlockSpec((None, tm, FF_BLK), lambda c, b, s: (b, rb(s), c))],
        out_specs=[pl.BlockSpec((None, tm, W), lambda c, b, s: (b, rb(s), c)), pl.BlockSpec((8, W), lambda c, b, s: (0, c))],
        out_shape=[jax.ShapeDtypeStruct((Bl, S, 2 * D_FF), BF16), jax.ShapeDtypeStruct((8, 2 * D_FF), F32)],
        scratch_shapes=[pltpu.VMEM((8, W), F32), pltpu.VMEM((tm + 8, W), F32)],
        compiler_params=_cp("arbitrary", "arbitrary", "arbitrary"),
    )(up3, up3, cw, cb, da3)


def cast_layer(w3, l, *, name):
    _, R, C = w3.shape
    tr = _tile(R, 512, 16)

    def body(w_ref, o_ref):
        o_ref[...] = w_ref[...].astype(BF16)

    return pl.pallas_call(
        body, name=name, grid=(R // tr,), in_specs=[pl.BlockSpec((None, tr, C), lambda i: (l, i, 0))],
        out_specs=pl.BlockSpec((tr, C), lambda i: (i, 0)), out_shape=jax.ShapeDtypeStruct((R, C), BF16),
        compiler_params=_cp("parallel"),
    )(w3)


def gate_bwd(dx3, y3, gate, *, tm=512, name):
    Bl, S, D = dx3.shape
    tm = min(tm, S)

    def body(dx_ref, y_ref, g_ref, o_ref, dg_ref):
        @pl.when(pl.program_id(1) == 0)
        def _():
            dg_ref[...] = jnp.zeros_like(dg_ref)

        dx = dx_ref[...]
        dg_ref[...] += jnp.sum(dx * y_ref[...], axis=0, keepdims=True)
        o_ref[...] = (dx * g_ref[...]).astype(BF16)

    blk = pl.BlockSpec((None, tm, D), lambda b, s: (b, s, 0))
    vec = pl.BlockSpec((None, 1, D), lambda b, s: (b, 0, 0))
    return pl.pallas_call(
        body, name=name, grid=(Bl, S // tm), in_specs=[blk, blk, vec], out_specs=[blk, vec],
        out_shape=[jax.ShapeDtypeStruct((Bl, S, D), BF16), jax.ShapeDtypeStruct((Bl, 1, D), F32)],
        compiler_params=_cp("parallel", "arbitrary"),
    )(dx3, y3, gate)


def loss_grad(y3, t3, *, tm=512, name):
    Bl, S, D = y3.shape
    tm = min(tm, S)
    last = (Bl - 1, S // tm - 1)

    def body(y_ref, t_ref, dy_ref, l_ref, acc_ref):
        b, s = pl.program_id(0), pl.program_id(1)

        @pl.when((b == 0) & (s == 0))
        def _():
            acc_ref[...] = jnp.zeros_like(acc_ref)

        e = y_ref[...] - t_ref[...]
        dy_ref[...] = e * (1.0 / D)
        acc_ref[...] += jnp.sum(e * e, axis=0, keepdims=True)

        @pl.when((b == last[0]) & (s == last[1]))
        def _():
            l_ref[...] = jnp.broadcast_to(jnp.sum(acc_ref[...], axis=1, keepdims=True) * (0.5 / D), (1, LANES))

    blk = pl.BlockSpec((None, tm, D), lambda b, s: (b, s, 0))
    return pl.pallas_call(
        body, name=name, grid=(Bl, S // tm), in_specs=[blk, blk],
        out_specs=[blk, pl.BlockSpec((1, LANES), lambda b, s: (0, 0))],
        out_shape=[jax.ShapeDtypeStruct((Bl, S, D), F32), jax.ShapeDtypeStruct((1, LANES), F32)],
        scratch_shapes=[pltpu.VMEM((1, D), F32)], compiler_params=_cp("arbitrary", "arbitrary"),
    )(y3, t3)


def adamw(w, g, m, v, *, name):
    L, R, C = w.shape
    tr = _tile(R, 512, 8)

    def body(w_ref, g_ref, m_ref, v_ref, d_ref, m2_ref, v2_ref):
        d_ref[...], m2_ref[...], v2_ref[...] = _adam_update(w_ref[...], g_ref[...], m_ref[...], v_ref[...])

    blk = pl.BlockSpec((None, tr, C), lambda l, i: (l, i, 0))
    shp = jax.ShapeDtypeStruct((L, R, C), F32)
    return pl.pallas_call(
        body, name=name, grid=(L, R // tr), in_specs=[blk] * 4, out_specs=[blk] * 3, out_shape=[shp] * 3,
        compiler_params=_cp("parallel", "parallel"),
    )(w, g, m, v)


def _adam_update(w, g, m, v):
    c1 = 1.0 / (1.0 - ADAM_B1 ** ADAM_STEP)
    c2 = 1.0 / (1.0 - ADAM_B2 ** ADAM_STEP)
    m2 = ADAM_B1 * m + (1.0 - ADAM_B1) * g
    v2 = ADAM_B2 * v + (1.0 - ADAM_B2) * (g * g)
    return -ADAM_LR * ((m2 * c1) / (jnp.sqrt(v2 * c2) + ADAM_EPS) + ADAM_WD * w), m2, v2


def adamw_small(ws, gs, ms, vs, *, name):
    na = len(ws)

    def body(*refs):
        w_r, g_r, m_r, v_r = (refs[i * na:(i + 1) * na] for i in range(4))
        d_r, m2_r, v2_r = (refs[(4 + i) * na:(5 + i) * na] for i in range(3))
        for a in range(na):
            d_r[a][...], m2_r[a][...], v2_r[a][...] = _adam_update(w_r[a][...], g_r[a][...], m_r[a][...], v_r[a][...])

    vm = pl.BlockSpec(memory_space=pltpu.VMEM)
    shp = [jax.ShapeDtypeStruct(w.shape, F32) for w in ws]
    out = pl.pallas_call(body, name=name, in_specs=[vm] * (4 * na), out_specs=[vm] * (3 * na), out_shape=shp * 3)(*ws, *gs, *ms, *vs)
    return out[:na], out[na:2 * na], out[2 * na:]


def sum_small(xs, *, name):
    na = len(xs)

    def body(*refs):
        for x_ref, o_ref in zip(refs[:na], refs[na:]):
            acc = x_ref[0]
            for k in range(1, x_ref.shape[0]):
                acc = acc + x_ref[k]
            o_ref[...] = acc

    vm = pl.BlockSpec(memory_space=pltpu.VMEM)
    return pl.pallas_call(body, name=name, in_specs=[vm] * na, out_specs=[vm] * na,
                          out_shape=[jax.ShapeDtypeStruct(x.shape[1:], x.dtype) for x in xs])(*xs)


def pair_add_half(g4, recv, c_arr, *, tr=512, name):
    _, R, C = g4.shape
    H = R // 2
    tr = _tile(H, tr, 16)
    nb = H // tr

    def body(c_ref, g_ref, r_ref, o_ref):
        o_ref[...] = (g_ref[...].astype(F32) + r_ref[...].astype(F32)).astype(BF16)

    grid_spec = pltpu.PrefetchScalarGridSpec(
        num_scalar_prefetch=1, grid=(4, nb),
        in_specs=[pl.BlockSpec((None, tr, C), lambda k, i, c_ref: (k, c_ref[0] * nb + i, 0)),
                  pl.BlockSpec((None, tr, C), lambda k, i, c_ref: (k, i, 0))],
        out_specs=pl.BlockSpec((None, tr, C), lambda k, i, c_ref: (k, i, 0)),
    )
    return pl.pallas_call(
        body, name=name, grid_spec=grid_spec, out_shape=jax.ShapeDtypeStruct((4, H, C), BF16),
        compiler_params=_cp("parallel", "parallel"),
    )(c_arr, g4, recv)


def chip_sum_into(landed, pair, sel, *, tr=512, name):
    _, H, C = landed.shape
    tr = _tile(H, tr, 16)
    nb = H // tr

    def body(s_ref, l0, l1, l2, l3, p_ref, o_ref):
        own = p_ref[...].astype(F32)
        acc = None
        for k, l_ref in enumerate((l0, l1, l2, l3)):
            part = jnp.where(s_ref[0] == k, own, l_ref[...].astype(F32))
            acc = part if acc is None else acc + part
        o_ref[...] = acc

    def slot(k):
        return pl.BlockSpec((None, tr, C), lambda i, s: (jnp.where(s[0] == k, (k + 1) % 4, k), i, 0))

    grid_spec = pltpu.PrefetchScalarGridSpec(
        num_scalar_prefetch=1, grid=(nb,),
        in_specs=[slot(0), slot(1), slot(2), slot(3), pl.BlockSpec((None, tr, C), lambda i, s: (s[0], i, 0))],
        out_specs=pl.BlockSpec((tr, C), lambda i, s: (s[1] * nb + i, 0)),
    )
    return pl.pallas_call(
        body, name=name, grid_spec=grid_spec, out_shape=jax.ShapeDtypeStruct((2 * H, C), F32), compiler_params=_cp("parallel"),
    )(sel, landed, landed, landed, landed, pair)


def mods_matmul(c_all, w_ada, b_ada_cols, *, tn=512, name):
    L, D, E = w_ada.shape
    nb = c_all.shape[0]
    tn = _tile(E, tn)

    def body(c_ref, w_ref, b_ref, o_ref):
        c = c_ref[...]
        a = c * jax.nn.sigmoid(c)
        o_ref[...] = jnp.dot(a, w_ref[...], preferred_element_type=F32, precision=lax.Precision.HIGHEST) + b_ref[...]

    return pl.pallas_call(
        body, name=name, grid=(L, E // tn),
        in_specs=[pl.BlockSpec((nb, D), lambda l, j: (0, 0)), pl.BlockSpec((None, D, tn), lambda l, j: (l, 0, j)),
                  pl.BlockSpec((None, 1, tn), lambda l, j: (l, 0, j))],
        out_specs=pl.BlockSpec((None, nb, tn), lambda l, j: (l, 0, j)),
        out_shape=jax.ShapeDtypeStruct((L, nb, E), F32), compiler_params=_cp("parallel", "parallel"),
    )(c_all, w_ada, b_ada_cols)


def ada_grad(c_all, dmods, *, tn=512, name):
    L, nb, E = dmods.shape
    D = c_all.shape[1]
    tn = _tile(E, tn)

    def body(c_ref, d_ref, o_ref):
        c = c_ref[...]
        a = c * jax.nn.sigmoid(c)
        o_ref[...] = lax.dot_general(a, d_ref[...], (((0,), (0,)), ((), ())), preferred_element_type=F32, precision=lax.Precision.HIGHEST)

    return pl.pallas_call(
        body, name=name, grid=(L, E // tn),
        in_specs=[pl.BlockSpec((nb, D), lambda l, j: (0, 0)), pl.BlockSpec((None, nb, tn), lambda l, j: (l, 0, j))],
        out_specs=pl.BlockSpec((None, D, tn), lambda l, j: (l, 0, j)),
        out_shape=jax.ShapeDtypeStruct((L, D, E), F32), compiler_params=_cp("parallel", "parallel"),
    )(c_all, dmods)


HBM = pl.BlockSpec(memory_space=pltpu.HBM)


def _me():
    return lax.axis_index("x"), lax.axis_index("y"), lax.axis_index("c")


def _flip(v, bit):
    return 1 - v if bit else v


def allgather8(xs, *, name):
    na = len(xs)

    def body(*refs):
        x_refs, out_refs = refs[:na], refs[na:2 * na]
        send_sems, recv_sems = refs[2 * na], refs[2 * na + 1]
        x, y, c = _me()
        me = 4 * x + 2 * y + c
        for x_ref, out_ref in zip(x_refs, out_refs):
            out_ref[me] = x_ref[...]
        sends = []
        for a, (x_ref, out_ref) in enumerate(zip(x_refs, out_refs)):
            for k in range(1, 8):
                peer = (_flip(x, k & 4), _flip(y, k & 2), _flip(c, k & 1))
                cp = pltpu.make_async_remote_copy(src_ref=x_ref, dst_ref=out_ref.at[me], send_sem=send_sems.at[a, k - 1],
                                                  recv_sem=recv_sems.at[a, k - 1], device_id=peer, device_id_type=MESH)
                cp.start()
                sends.append(cp)
        for a, (x_ref, out_ref) in enumerate(zip(x_refs, out_refs)):
            for k in range(1, 8):
                peer = (_flip(x, k & 4), _flip(y, k & 2), _flip(c, k & 1))
                src = 4 * peer[0] + 2 * peer[1] + peer[2]
                pltpu.make_async_remote_copy(src_ref=x_ref, dst_ref=out_ref.at[src], send_sem=send_sems.at[a, k - 1],
                                             recv_sem=recv_sems.at[a, k - 1], device_id=peer, device_id_type=MESH).wait_recv()
        for cp in sends:
            cp.wait_send()

    vm = pl.BlockSpec(memory_space=pltpu.VMEM)
    return pl.pallas_call(
        body, name=name, in_specs=[vm] * na, out_specs=[vm] * na,
        out_shape=[jax.ShapeDtypeStruct((8,) + a.shape, a.dtype) for a in xs],
        scratch_shapes=[pltpu.SemaphoreType.DMA((na, 7)), pltpu.SemaphoreType.DMA((na, 7))],
    )(*xs)


class _Plan:
    def __init__(self, ins, out_shapes, ncopies, copies, aliased=False):
        self.ins, self.out_shapes, self.ncopies, self.copies, self.aliased = list(ins), list(out_shapes), ncopies, copies, aliased

    def start(self, in_refs, out_refs, send_sems, recv_sems):
        sends, _ = self.copies(in_refs, out_refs, send_sems, recv_sems)
        for cp in sends:
            cp.start()

    def finish(self, in_refs, out_refs, send_sems, recv_sems):
        sends, recvs = self.copies(in_refs, out_refs, send_sems, recv_sems)
        for cp in recvs:
            cp.wait_recv()
        for cp in sends:
            cp.wait_send()


def _rcopy(src, dst, send_sems, recv_sems, idx, dev):
    return pltpu.make_async_remote_copy(src_ref=src, dst_ref=dst, send_sem=send_sems.at[idx], recv_sem=recv_sems.at[idx],
                                        device_id=dev, device_id_type=MESH)


def _other_chips(x, y):
    return [(_flip(x, k & 2), _flip(y, k & 1)) for k in range(1, 4)]


def plan_gather_ici(ws):
    def copies(in_refs, out_refs, ss, rs):
        x, y, c = _me()
        j = 2 * x + y
        sends, recvs = [], []
        for a, (x_ref, out_ref) in enumerate(zip(in_refs, out_refs)):
            H = x_ref.shape[0] // 2
            for k, (px, py) in enumerate(_other_chips(x, y)):
                sends.append(_rcopy(x_ref.at[pl.ds(c * H, H)], out_ref.at[j, pl.ds(c * H, H)], ss, rs, 3 * a + k, (px, py, c)))
                slot = out_ref.at[2 * px + py, pl.ds(c * H, H)]
                recvs.append(_rcopy(slot, slot, ss, rs, 3 * a + k, (px, py, c)))
        return sends, recvs

    return _Plan(ws, [jax.ShapeDtypeStruct((4,) + w.shape, w.dtype) for w in ws], 3 * len(ws), copies)


def plan_gather_d2d(w4s):
    def copies(in_refs, out_refs, ss, rs):
        x, y, c = _me()
        sends, recvs = [], []
        for a, out_ref in enumerate(out_refs):
            H = out_ref.shape[1] // 2
            for k, (px, py) in enumerate(_other_chips(x, y)):
                mine = out_ref.at[2 * px + py, pl.ds(c * H, H)]
                theirs = out_ref.at[2 * px + py, pl.ds((1 - c) * H, H)]
                sends.append(_rcopy(mine, mine, ss, rs, 3 * a + k, (x, y, 1 - c)))
                recvs.append(_rcopy(theirs, theirs, ss, rs, 3 * a + k, (x, y, 1 - c)))
        return sends, recvs

    return _Plan(w4s, [jax.ShapeDtypeStruct(w.shape, w.dtype) for w in w4s], 3 * len(w4s), copies, aliased=True)


def plan_swap_halves(gs):
    def copies(in_refs, out_refs, ss, rs):
        x, y, c = _me()
        sends, recvs = [], []
        for a, (g_ref, out_ref) in enumerate(zip(in_refs, out_refs)):
            H = g_ref.shape[1] // 2
            for k in range(4):
                sends.append(_rcopy(g_ref.at[k, pl.ds((1 - c) * H, H)], out_ref.at[k], ss, rs, 4 * a + k, (x, y, 1 - c)))
                recvs.append(_rcopy(g_ref.at[k, pl.ds(c * H, H)], out_ref.at[k], ss, rs, 4 * a + k, (x, y, 1 - c)))
        return sends, recvs

    return _Plan(gs, [jax.ShapeDtypeStruct((4, g.shape[1] // 2, g.shape[2]), g.dtype) for g in gs], 4 * len(gs), copies)


def plan_scatter_ici(ps):
    def copies(in_refs, out_refs, ss, rs):
        x, y, c = _me()
        j = 2 * x + y
        sends, recvs = [], []
        for a, (p_ref, out_ref) in enumerate(zip(in_refs, out_refs)):
            for k, (px, py) in enumerate(_other_chips(x, y)):
                sends.append(_rcopy(p_ref.at[2 * px + py], out_ref.at[j], ss, rs, 3 * a + k, (px, py, c)))
                slot = out_ref.at[2 * px + py]
                recvs.append(_rcopy(slot, slot, ss, rs, 3 * a + k, (px, py, c)))
        return sends, recvs

    return _Plan(ps, [jax.ShapeDtypeStruct(p.shape, p.dtype) for p in ps], 3 * len(ps), copies)


def plan_join_halves(fulls):
    def copies(in_refs, out_refs, ss, rs):
        x, y, c = _me()
        sends, recvs = [], []
        for a, out_ref in enumerate(out_refs):
            H = out_ref.shape[0] // 2
            mine, theirs = out_ref.at[pl.ds(c * H, H)], out_ref.at[pl.ds((1 - c) * H, H)]
            sends.append(_rcopy(mine, mine, ss, rs, a, (x, y, 1 - c)))
            recvs.append(_rcopy(theirs, theirs, ss, rs, a, (x, y, 1 - c)))
        return sends, recvs

    return _Plan(fulls, [jax.ShapeDtypeStruct(f.shape, f.dtype) for f in fulls], len(fulls), copies, aliased=True)


def call_with_plans(body, plans, *, grid, in_specs, out_specs, out_shape, scratch_shapes, args, sem, name):
    plans = list(plans or [])
    n_in, n_out, n_scr = len(in_specs), len(out_specs), len(scratch_shapes)
    c_in = [len(p.ins) for p in plans]
    c_out = [len(p.out_shapes) for p in plans]
    steps = math.prod(grid) if grid else 1

    def wrapped(*refs):
        pos = 0

        def take(n):
            nonlocal pos
            out = refs[pos:pos + n]
            pos += n
            return out

        ins = take(n_in)
        cins = [take(n) for n in c_in]
        outs = take(n_out)
        couts = [take(n) for n in c_out]
        scr = take(n_scr)
        sems = [take(2) for _ in plans]
        def start_all():
            for p, ci, co, (ss, rs) in zip(plans, cins, couts, sems):
                p.start(ci, co, ss, rs)

        def finish_all():
            for p, ci, co, (ss, rs) in zip(plans, cins, couts, sems):
                p.finish(ci, co, ss, rs)

        if plans and grid:
            idx = 0
            for ax, g in enumerate(grid):
                idx = idx * g + pl.program_id(ax)
            pl.when(idx == 0)(start_all)
        elif plans:
            start_all()
        if body is not None:
            body(*ins, *outs, *scr)
        if plans and grid:
            pl.when(idx == steps - 1)(finish_all)
        elif plans:
            finish_all()

    aliases = {}
    i_pos, o_pos = n_in, n_out
    for p, ni, no in zip(plans, c_in, c_out):
        if p.aliased:
            aliases.update({i_pos + t: o_pos + t for t in range(ni)})
        i_pos += ni
        o_pos += no
    kwargs = dict(grid=grid) if grid else {}
    if aliases:
        kwargs["input_output_aliases"] = aliases
    res = pl.pallas_call(
        wrapped, name=name, in_specs=list(in_specs) + [HBM] * sum(c_in), out_specs=list(out_specs) + [HBM] * sum(c_out),
        out_shape=list(out_shape) + [s for p in plans for s in p.out_shapes],
        scratch_shapes=list(scratch_shapes) + [pltpu.SemaphoreType.DMA((p.ncopies,)) for p in plans for _ in range(2)],
        compiler_params=_cp(*sem) if grid else pltpu.CompilerParams(vmem_limit_bytes=VMEM_LIMIT), **kwargs,
    )(*args, *[a for p in plans for a in p.ins])
    res = list(res)
    comp, rest = res[:n_out], res[n_out:]
    pouts = []
    for no in c_out:
        pouts.append(rest[:no])
        rest = rest[no:]
    return comp, pouts


def run_plans(plans, *, name):
    return call_with_plans(None, plans, grid=(), in_specs=[], out_specs=[], out_shape=[], scratch_shapes=[], args=[], sem=(), name=name)[1]


def _cat(parts, axis=-1):
    return jnp.concatenate(parts, axis=axis)


def _pairs_of_heads(a, axis, inverse=False):
    lead, tail = a.shape[:axis], a.shape[axis + 1:]
    split = (3, 2) if inverse else (2, 3)
    a = a.reshape(lead + split + (HEAD,) + tail)
    return jnp.swapaxes(a, axis, axis + 1).reshape(lead + (6 * HEAD,) + tail)


def _prep_w_in(w):
    z = lambda n: jnp.zeros((w.shape[0], n), w.dtype)
    return _cat([w[:, 0:1152], z(64), w[:, 1152:1184], z(32), _pairs_of_heads(w[:, 1184:1568], 1), w[:, 1568:1824]])


def _unprep_w_in(g):
    return _cat([g[:, 0:1152], g[:, 1216:1248], _pairs_of_heads(g[:, P_SWQ:P_SWK], 1, inverse=True), g[:, P_SWK:P_END]])


def _prep_w_uq(w):
    r = w.shape[0]
    return jnp.pad(w.reshape(r, 6, MLA_QK), ((0, 0), (0, 0), (0, LANES - MLA_QK))).reshape(r, 6 * LANES)


def _unprep_w_uq(g):
    r = g.shape[0]
    return g.reshape(r, 6, LANES)[:, :, :MLA_QK].reshape(r, 6 * MLA_QK)


def _prep_w_ukv(w):
    r = w.shape[0]
    w3 = w.reshape(r, 6, LANES)
    k = jnp.pad(w3[:, :, :HEAD], ((0, 0), (0, 0), (0, LANES - HEAD))).reshape(r, 6 * LANES)
    return _cat([k, w3[:, :, HEAD:].reshape(r, 6 * HEAD)])


def _unprep_w_ukv(g):
    r = g.shape[0]
    k = g[:, :6 * LANES].reshape(r, 6, LANES)[:, :, :HEAD]
    return _cat([k, g[:, 6 * LANES:].reshape(r, 6, HEAD)], axis=2).reshape(r, 6 * LANES)


def _prep_w_out(w):
    return _cat([w[0:640], _pairs_of_heads(w[640:], 0)], axis=0)


def _unprep_w_out(g):
    return _cat([g[0:640], _pairs_of_heads(g[640:], 0, inverse=True)], axis=0)


def _rope_tables(positions):
    half = 16
    inv_freq = jnp.power(ROPE_THETA, -jnp.arange(half, dtype=F32) / half)
    ang = positions.astype(F32)[..., None] * inv_freq
    cos, sin = jnp.cos(ang), jnp.sin(ang)
    z = lambda n: jnp.zeros(ang.shape[:-1] + (n,), F32)
    return (_cat([jnp.ones(ang.shape[:-1] + (HEAD,), F32), cos, cos, z(32)]), _cat([z(HEAD), -sin, z(16), z(32)]), _cat([z(HEAD), z(16), sin, z(32)]))


def _small_params(p):
    pad96 = lambda g: _cat([g, jnp.zeros((32,), F32)]).reshape(1, LANES)
    two = lambda g: _cat([g, g]).reshape(1, LANES)
    sinks = jnp.broadcast_to(p["sw_sinks"].reshape(2, 3).T[:, :, None], (3, 2, LANES))
    return dict(n1=p["norm1_g"].reshape(1, -1), n2=p["norm2_g"].reshape(1, -1), cq_g=p["mla_cq_g"].reshape(1, -1),
                ckv_g=p["mla_ckv_g"].reshape(1, -1), qn_g=pad96(p["mla_qn_g"]), kn_g=pad96(p["mla_kn_g"]),
                swq_g=two(p["sw_qn_g"]), swk_g=two(p["sw_kn_g"]), sinks=sinks, conv_b=_up_perm(p["conv_b"]).reshape(1, -1))


class _NoFlow:
    def plans(self, tag):
        return []

    def done(self, tag, outs):
        pass

    def add(self, key, g):
        pass


def _layer_fwd(x3, md, W, tabs, bias, tag, flow=_NoFlow()):
    Bl, S, D = x3.shape
    T = Bl * S
    n = lambda s: f"{s}_{tag}"
    two = lambda a: a.reshape(T, a.shape[-1])
    three = lambda a: a.reshape(Bl, S, a.shape[-1])
    h = rms_fwd(x3, 0, D, W["n1"], md["scale1"], md["shift1"], name=n("norm1"))
    proj = three(matmul(two(h), W["w_in"], tn=1920, name=n("in_proj")))
    (o_a, rt_a), got = sb_attn_fwd(proj, plans=flow.plans(n("sb_fwd")), name=n("sb_fwd"))
    flow.done(n("sb_fwd"), got)
    cqn = rms_fwd(proj, P_CQ // 256, 256, W["cq_g"], name=n("cq_norm"))
    ckvn = rms_fwd(proj, P_CKV // LANES, LANES, W["ckv_g"], name=n("ckv_norm"))
    qb = three(matmul(two(cqn), W["w_uq"], tm=1024, tn=768, name=n("uq")))
    kvb = three(matmul(two(ckvn), W["w_ukv"], tm=1024, tn=1152, name=n("ukv")))
    q_m = rope_norm_fwd(qb, 6, W["qn_g"], tabs, name=n("q_rope"))
    k_m = rope_norm_fwd(kvb, 6, W["kn_g"], tabs, (proj, P_SLAB // LANES), name=n("k_rope"))
    (o_b, lse_b), got = mla_attn_fwd(q_m, k_m, kvb, 6, plans=flow.plans(n("mla_fwd")), name=n("mla_fwd"))
    flow.done(n("mla_fwd"), got)
    q_c = pair_rms_fwd(proj, P_SWQ // LANES, 3, W["swq_g"], name=n("swq_norm"))
    k_c = pair_rms_fwd(proj, P_SWK // LANES, 1, W["swk_g"], name=n("swk_norm"))
    (o_c, lse_c), got = swa_attn_fwd(q_c, k_c, proj, bias, W["sinks"], plans=flow.plans(n("swa_fwd")), name=n("swa_fwd"))
    flow.done(n("swa_fwd"), got)
    mix = _cat([o_a, o_b, o_c]).astype(BF16)
    att, x1 = matmul_res(two(mix), W["w_out"], two(x3), md["gate1"], S, name=n("out_proj"))
    x1 = three(x1)
    h2 = rms_fwd(x1, 0, D, W["n2"], md["scale2"], md["shift2"], name=n("norm2"))
    up = three(matmul(two(h2), W["w_up"], tm=1024, tn=1408, name=n("up_proj")))
    a = conv_gate_fwd(up, W["conv_w"], W["conv_b"], name=n("conv_gate"))
    yd, x2 = matmul_res(two(a), W["w_down"], two(x1), md["gate2"], S, name=n("down_proj"))
    saved = dict(x=x3, h=h, proj=proj, rt_a=rt_a, cqn=cqn, ckvn=ckvn, qb=qb, kvb=kvb, q_m=q_m, k_m=k_m, o_b=o_b, lse_b=lse_b,
                 q_c=q_c, k_c=k_c, o_c=o_c, lse_c=lse_c, mix=mix, att=three(att), x1=x1, h2=h2, up=up, a=a, yd=three(yd))
    return three(x2), saved


def _layer_bwd(dx2, sv, md, W, tabs, bias, tag, flow=_NoFlow()):
    Bl, S, D = dx2.shape
    T = Bl * S
    n = lambda s: f"{s}_{tag}"
    two = lambda a: a.reshape(T, a.shape[-1])
    three = lambda a: a.reshape(Bl, S, a.shape[-1])
    g = {}
    dyb, dgate2 = gate_bwd(dx2, sv["yd"], md["gate2"], name=n("gate2_bwd"))
    da = three(matmul(two(dyb), W["w_down"], tb=True, tm=1024, tn=1408, name=n("down_dx")))
    g["w_down"] = matmul(two(sv["a"]), two(dyb), ta=True, tm=256, tn=1024, out_dtype=BF16, name=n("down_dw"))
    dup, dcw = conv_gate_bwd(sv["up"], W["conv_w"], W["conv_b"], da, name=n("conv_gate_bwd"))
    dh2 = three(matmul(two(dup), W["w_up"], tb=True, tn=1024, name=n("up_dx")))
    g["w_up"] = matmul(two(sv["h2"]), two(dup), ta=True, tn=1408, out_dtype=BF16, name=n("up_dw"))
    dx1, dn2, dsc2, dsh2 = rms_bwd(sv["x1"], 0, D, dh2, W["n2"], md["scale2"], dx2, name=n("norm2_bwd"))
    dmo, dgate1 = gate_bwd(dx1, sv["att"], md["gate1"], name=n("gate1_bwd"))
    dmix = three(matmul(two(dmo), W["w_out"], tb=True, tn=1024, out_dtype=BF16, name=n("out_dx")))
    g["w_out"] = matmul(two(sv["mix"]), two(dmo), ta=True, tn=1024, out_dtype=BF16, name=n("out_dw"))
    proj = sv["proj"]
    for k in ("w_down", "w_up", "w_out"):
        flow.add((tag, k), g[k])
    (dq_a, dk_a, dv_a), got = sb_attn_bwd(proj, sv["rt_a"], dmix[:, :, 0:256], plans=flow.plans(n("sb_bwd")), name=n("sb_bwd"))
    flow.done(n("sb_bwd"), got)
    dq_m, dk_m, dv_b = mla_attn_bwd(sv["q_m"], sv["k_m"], sv["kvb"], 6, sv["o_b"], sv["lse_b"], dmix[:, :, 256:640], name=n("mla_bwd"))
    dqb, dqn = rope_norm_bwd(sv["qb"], 6, dq_m, W["qn_g"], tabs, name=n("q_rope_bwd"))
    dkn_x, dkn, dslab = rope_norm_bwd(sv["kvb"], 6, dk_m, W["kn_g"], tabs, (proj, P_SLAB // LANES), name=n("k_rope_bwd"))
    dkvb = _cat([dkn_x, dv_b]).astype(BF16)
    dckvn = three(matmul(two(dkvb), W["w_ukv"], tb=True, tm=1024, name=n("ukv_dx")))
    g["w_ukv"] = matmul(two(sv["ckvn"]), two(dkvb), ta=True, tn=1152, out_dtype=BF16, name=n("ukv_dw"))
    dcqn = three(matmul(two(dqb), W["w_uq"], tb=True, tm=1024, name=n("uq_dx")))
    g["w_uq"] = matmul(two(sv["cqn"]), two(dqb), ta=True, tn=768, out_dtype=BF16, name=n("uq_dw"))
    dcq, dcq_g = rms_bwd(proj, P_CQ // 256, 256, dcqn, W["cq_g"], name=n("cq_norm_bwd"))
    dckv, dckv_g = rms_bwd(proj, P_CKV // LANES, LANES, dckvn, W["ckv_g"], name=n("ckv_norm_bwd"))
    dq_c, dk_c, dv_c, dbias, dsink = swa_attn_bwd(sv["q_c"], sv["k_c"], proj, bias, W["sinks"], sv["o_c"], sv["lse_c"], dmix[:, :, 640:1024], name=n("swa_bwd"))
    dswq, dswq_g = pair_rms_bwd(proj, P_SWQ // LANES, 3, dq_c, W["swq_g"], name=n("swq_norm_bwd"))
    dswk, dswk_g = pair_rms_bwd(proj, P_SWK // LANES, 1, dk_c, W["swk_g"], name=n("swk_norm_bwd"))
    dproj = _cat([dq_a, dk_a, dv_a, dcq, dckv, dslab, dswq, dswk, dv_c]).astype(BF16)
    dh = three(matmul(two(dproj), W["w_in"], tb=True, tn=1024, name=n("in_dx")))
    g["w_in"] = matmul(two(sv["h"]), two(dproj), ta=True, tn=1920, tk=2048, out_dtype=BF16, name=n("in_dw"))
    dx, dn1, dsc1, dsh1 = rms_bwd(sv["x"], 0, D, dh, W["n1"], md["scale1"], dx1, name=n("norm1_bwd"))
    small = dict(n1=dn1, n2=dn2, cq_g=dcq_g, ckv_g=dckv_g, qn_g=dqn, kn_g=dkn, swq_g=dswq_g, swk_g=dswk_g, conv=dcw)
    dmods = _cat([dsh1, dsc1, dgate1, dsh2, dsc2, dgate2]).reshape(Bl, 6 * D)
    for k in ("w_ukv", "w_uq", "w_in"):
        flow.add((tag, k), g[k])
    return dx, g, small, dmods, dbias, dsink


BIG = ("w_in", "w_uq", "w_ukv", "w_out", "w_up", "w_down")
ROW_SHARDED = ("w_out", "w_down")
PREP = dict(w_in=_prep_w_in, w_uq=_prep_w_uq, w_ukv=_prep_w_ukv, w_out=_prep_w_out, w_up=_up_perm, w_down=lambda w: w)
UNPREP = dict(w_in=_unprep_w_in, w_uq=_unprep_w_uq, w_ukv=_unprep_w_ukv, w_out=_unprep_w_out, w_up=_up_perm, w_down=lambda w: w)
NCHIPS = 4


def _local_step(x, target, positions, mods, Wl, rel_flat, fwd_flow=_NoFlow(), bwd_flow=_NoFlow()):
    Bl, S, D = x.shape
    L = len(Wl)
    tabs = _rope_tables(positions)
    bucket = _bucket_table()
    bias = swa_bias(rel_flat, bucket, name="swa_bias")
    mds = []
    for l in range(L):
        parts = [mods[l, :, D * k:D * (k + 1)].reshape(Bl, 1, D) for k in range(6)]
        mds.append(dict(zip(("shift1", "scale1", "gate1", "shift2", "scale2", "gate2"), parts)))
    saved = []
    h = x
    for l in range(L):
        h, sv = _layer_fwd(h, mds[l], Wl[l], tabs, bias, f"l{l}", fwd_flow)
        saved.append(sv)
    dy, loss = loss_grad(h, target, name="loss")
    grads, smalls, dmods, dbiases, dsinks = [None] * L, [None] * L, [None] * L, [None] * L, [None] * L
    for l in reversed(range(L)):
        dy, grads[l], smalls[l], dmods[l], dbiases[l], dsinks[l] = _layer_bwd(dy, saved[l], mds[l], Wl[l], tabs, bias, f"l{l}", bwd_flow)
    drel = swa_bias_bwd(_cat(dbiases, axis=0), bucket, name="swa_bias_bwd")
    return loss, dy, grads, smalls, dmods, dsinks, drel


ATT = ("w_in", "w_uq", "w_ukv", "w_out")
FFN = ("w_up", "w_down")
GATHER_STAGES = {
    "sb_fwd_l0": ([("l0", k) for k in ("w_out",) + FFN], []),
    "mla_fwd_l0": ([("l1", k) for k in ATT + ("w_up",)], [("l0", k) for k in ("w_out",) + FFN]),
    "swa_fwd_l0": ([("l1", "w_down")], [("l1", k) for k in ATT + ("w_up",)]),
    "sb_fwd_l1": ([], [("l1", "w_down")]),
}
SCATTER_STAGES = {
    "sb_bwd_l1": [("l1", k) for k in FFN],
    "sb_bwd_l0": [("l1", k) for k in ATT] + [("l0", k) for k in FFN + ("w_out",)],
}


class _GatherFlow:
    def __init__(self, shards, chip):
        self.shards, self.chip, self.ici, self.d2d, self.pending = shards, chip, {}, {}, {}

    def early(self, keys):
        ici, = run_plans([plan_gather_ici([self.shards[k] for k in keys])], name="gather_early_ici")
        d2d, = run_plans([plan_gather_d2d(ici)], name="gather_early_d2d")
        self.d2d.update(zip(keys, d2d))

    def plans(self, tag):
        ici_keys, d2d_keys = GATHER_STAGES.get(tag, ([], []))
        plans = []
        if d2d_keys:
            plans.append(plan_gather_d2d([self.ici[k] for k in d2d_keys]))
        if ici_keys:
            plans.append(plan_gather_ici([self.shards[k] for k in ici_keys]))
        self.pending[tag] = (ici_keys, d2d_keys)
        return plans

    def done(self, tag, outs):
        ici_keys, d2d_keys = self.pending.pop(tag, ([], []))
        outs = list(outs)
        if d2d_keys:
            self.d2d.update(zip(d2d_keys, outs.pop(0)))
        if ici_keys:
            self.ici.update(zip(ici_keys, outs.pop(0)))

    def weight(self, key):
        k = key[1]
        own = self.shards[key]
        r, cc = own.shape
        w4 = lax.dynamic_update_slice(self.d2d[key], own[None], (self.chip, 0, 0))
        fw = w4.reshape(NCHIPS * r, cc) if k in ROW_SHARDED else jnp.transpose(w4, (1, 0, 2)).reshape(r, NCHIPS * cc)
        return PREP[k](fw)


class _LayerWeights(dict):
    def __init__(self, small, flow, tag):
        super().__init__(small)
        self.flow, self.tag = flow, tag

    def __missing__(self, k):
        self[k] = self.flow.weight((self.tag, k))
        return self[k]


class _ScatterFlow:
    def __init__(self, shapes, sel, c_arr):
        self.shapes, self.sel, self.c_arr = shapes, sel, c_arr
        self.g, self.pairs, self.landed, self.pending = {}, {}, {}, {}

    def add(self, key, g):
        self.g[key] = g

    def _pairs(self, keys, label):
        g4s = []
        for key in keys:
            k = key[1]
            r, cc = self.shapes[k]
            gk = UNPREP[k](self.g[key])
            g4 = gk.reshape(NCHIPS, r, cc) if k in ROW_SHARDED else jnp.transpose(gk.reshape(r, NCHIPS, cc), (1, 0, 2))
            g4s.append(g4.astype(BF16))
        theirs, = run_plans([plan_swap_halves(g4s)], name=f"rs_swap_{label}")
        pairs = [pair_add_half(g4, th, self.c_arr, name=f"rs_pair_add_{key[1]}_{key[0]}") for key, g4, th in zip(keys, g4s, theirs)]
        self.pairs.update(zip(keys, pairs))
        return pairs

    def plans(self, tag):
        keys = SCATTER_STAGES.get(tag, [])
        self.pending[tag] = keys
        return [plan_scatter_ici(self._pairs(keys, tag))] if keys else []

    def done(self, tag, outs):
        keys = self.pending.pop(tag, [])
        if keys:
            self.landed.update(zip(keys, outs[0]))

    def finish(self):
        rest = [key for key in self.g if key not in self.pairs]
        if rest:
            landed, = run_plans([plan_scatter_ici(self._pairs(rest, "rest"))], name="rs_scatter_rest")
            self.landed.update(zip(rest, landed))
        keys = list(self.pairs)
        fulls = [chip_sum_into(self.landed[key], self.pairs[key], self.sel, name=f"rs_chip_sum_{key[1]}_{key[0]}") for key in keys]
        joined, = run_plans([plan_join_halves(fulls)], name="rs_join_halves")
        return dict(zip(keys, joined))


WEIGHTS = ("rel_table", "norm1_g", "norm2_g", "w_ada", "b_ada", "w_in", "mla_cq_g", "w_uq", "mla_ckv_g", "w_ukv", "mla_qn_g", "mla_kn_g",
           "sw_qn_g", "sw_kn_g", "sw_sinks", "w_out", "w_up", "conv_w", "conv_b", "w_down")
SMALL = tuple(n for n in WEIGHTS if n not in BIG + ("w_ada",))


def kernel(x, c, positions, rel_table, norm1_g, norm2_g, w_ada, b_ada, w_in, mla_cq_g, w_uq, mla_ckv_g, w_ukv, mla_qn_g, mla_kn_g, sw_qn_g, sw_kn_g, sw_sinks, w_out, w_up, conv_w, conv_b, w_down, loss_target, m_rel_table, m_norm1_g, m_norm2_g, m_w_ada, m_b_ada, m_w_in, m_mla_cq_g, m_w_uq, m_mla_ckv_g, m_w_ukv, m_mla_qn_g, m_mla_kn_g, m_sw_qn_g, m_sw_kn_g, m_sw_sinks, m_w_out, m_w_up, m_conv_w, m_conv_b, m_w_down, v_rel_table, v_norm1_g, v_norm2_g, v_w_ada, v_b_ada, v_w_in, v_mla_cq_g, v_w_uq, v_mla_ckv_g, v_w_ukv, v_mla_qn_g, v_mla_kn_g, v_sw_qn_g, v_sw_kn_g, v_sw_sinks, v_w_out, v_w_up, v_conv_w, v_conv_b, v_w_down):
    w = dict(rel_table=rel_table, norm1_g=norm1_g, norm2_g=norm2_g, w_ada=w_ada, b_ada=b_ada, w_in=w_in, mla_cq_g=mla_cq_g, w_uq=w_uq,
             mla_ckv_g=mla_ckv_g, w_ukv=w_ukv, mla_qn_g=mla_qn_g, mla_kn_g=mla_kn_g, sw_qn_g=sw_qn_g, sw_kn_g=sw_kn_g, sw_sinks=sw_sinks,
             w_out=w_out, w_up=w_up, conv_w=conv_w, conv_b=conv_b, w_down=w_down)
    m = dict(rel_table=m_rel_table, norm1_g=m_norm1_g, norm2_g=m_norm2_g, w_ada=m_w_ada, b_ada=m_b_ada, w_in=m_w_in, mla_cq_g=m_mla_cq_g,
             w_uq=m_w_uq, mla_ckv_g=m_mla_ckv_g, w_ukv=m_w_ukv, mla_qn_g=m_mla_qn_g, mla_kn_g=m_mla_kn_g, sw_qn_g=m_sw_qn_g,
             sw_kn_g=m_sw_kn_g, sw_sinks=m_sw_sinks, w_out=m_w_out, w_up=m_w_up, conv_w=m_conv_w, conv_b=m_conv_b, w_down=m_w_down)
    v = dict(rel_table=v_rel_table, norm1_g=v_norm1_g, norm2_g=v_norm2_g, w_ada=v_w_ada, b_ada=v_b_ada, w_in=v_w_in, mla_cq_g=v_mla_cq_g,
             w_uq=v_w_uq, mla_ckv_g=v_mla_ckv_g, w_ukv=v_w_ukv, mla_qn_g=v_mla_qn_g, mla_kn_g=v_mla_kn_g, sw_qn_g=v_sw_qn_g,
             sw_kn_g=v_sw_kn_g, sw_sinks=v_sw_sinks, w_out=v_w_out, w_up=v_w_up, conv_w=v_conv_w, conv_b=v_conv_b, w_down=v_w_down)
    Bl, S, D = x.shape
    L = norm1_g.shape[0]
    xi, yi, ci = _me()
    chip = 2 * xi + yi
    dev = 4 * xi + 2 * yi + ci
    ndev = 2 * NCHIPS

    shapes = {k: w[k].shape[1:] for k in BIG}
    shards = {(f"l{l}", k): cast_layer(w[k], l, name=f"cast_{k}_l{l}") for l in range(L) for k in BIG}
    gflow = _GatherFlow(shards, chip)
    gflow.early([("l0", k) for k in ("w_in", "w_uq", "w_ukv")])

    cw_cols = conv_w.shape[2]
    c_got, cw_got = allgather8([c, conv_w.reshape(L * 3, cw_cols)], name="gather_cond")
    c_all = c_got.reshape(ndev * Bl, D)
    conv_full = jnp.transpose(cw_got[0::2].reshape(NCHIPS, L, 3, cw_cols), (1, 2, 0, 3)).reshape(L, 3, NCHIPS * cw_cols)
    E = w_ada.shape[2]
    b_cols = lax.dynamic_slice(b_ada, (0, chip * E), (L, E)).reshape(L, 1, E)
    mods_cols = mods_matmul(c_all, w_ada, b_cols, name="mods")
    mods_all, = allgather8([mods_cols.reshape(L * ndev * Bl, E)], name="gather_mods")
    mods_all = jnp.transpose(mods_all[0::2].reshape(NCHIPS, L, ndev * Bl, E), (1, 2, 0, 3)).reshape(L, ndev * Bl, NCHIPS * E)
    mods = lax.dynamic_slice(mods_all, (0, dev * Bl, 0), (L, Bl, NCHIPS * E))

    Wl = []
    for l in range(L):
        Wd = _small_params({k: w[k][l] for k in SMALL if k not in ("rel_table", "b_ada", "conv_w")})
        Wd["conv_w"] = _up_perm(conv_full[l])
        Wl.append(_LayerWeights(Wd, gflow, f"l{l}"))

    sflow = _ScatterFlow(shapes, jnp.stack([chip, ci]).astype(jnp.int32), ci.reshape(1).astype(jnp.int32))
    loss, dx, _, smalls, dmods, dsinks, drel = _local_step(x, loss_target, positions, mods, Wl, rel_table.reshape(-1), gflow, sflow)
    reduced = sflow.finish()
    grad = {k: jnp.stack([reduced[(f"l{l}", k)] for l in range(L)]) for k in BIG}

    vec_names = ("n1", "n2", "cq_g", "ckv_g", "qn_g", "kn_g", "swq_g", "swk_g")
    vecs = _cat([_cat([smalls[l][k] for k in vec_names], axis=1) for l in range(L)], axis=0)
    convs = _cat([smalls[l]["conv"][0:4] for l in range(L)], axis=0)
    dm = jnp.stack(dmods, axis=1).reshape(Bl * L, 6 * D)
    dsk = jnp.stack(dsinks, axis=1).reshape(Bl * L * 6, LANES)
    got = allgather8([vecs, convs, drel, loss, dm, dsk], name="gather_small_grads")
    seq = lambda a, rows: a.reshape(ndev * Bl, rows, a.shape[-1])
    vec_s, conv_s, rel_s, loss_s, dm_s, dsk_s = sum_small(list(got[:4]) + [seq(got[4], L), seq(got[5], L * 6)], name="sum_small_grads")
    dm_all = jnp.transpose(seq(got[4], L), (1, 0, 2))
    grad["w_ada"] = ada_grad(c_all, lax.dynamic_slice(dm_all, (0, 0, chip * E), (L, ndev * Bl, E)), name="ada_grad")
    grad["b_ada"] = dm_s
    grad["sw_sinks"] = jnp.transpose(dsk_s.reshape(L, 3, 2, LANES)[:, :, :, 0], (0, 2, 1)).reshape(L, 6)
    grad["rel_table"] = rel_s[:6, :REL_BUCKETS].T
    off = 0
    for k, name_, keep in zip(vec_names, ("norm1_g", "norm2_g", "mla_cq_g", "mla_ckv_g", "mla_qn_g", "mla_kn_g", "sw_qn_g", "sw_kn_g"),
                              (D, D, 256, LANES, MLA_QK, MLA_QK, HEAD, HEAD)):
        grad[name_] = vec_s[:, off:off + keep]
        off += smalls[0][k].shape[1]
    conv = _up_perm(conv_s.reshape(L, 4, 2 * D_FF))
    grad["conv_w"] = lax.dynamic_slice(conv[:, 0:3], (0, 0, chip * cw_cols), (L, 3, cw_cols))
    grad["conv_b"] = conv[:, 3]
    loss_out = loss_s[0, 0]

    delta, new_m, new_v = {}, {}, {}
    for k in BIG + ("w_ada",):
        delta[k], new_m[k], new_v[k] = adamw(w[k], grad[k], m[k], v[k], name=f"adamw_{k}")
    outs = adamw_small(*[[src[k] for k in SMALL] for src in (w, grad, m, v)], name="adamw_small")
    for dst, o in zip((delta, new_m, new_v), outs):
        dst.update(dict(zip(SMALL, o)))
    return (loss_out, dx, *[grad[k] for k in WEIGHTS], *[delta[k] for k in WEIGHTS], *[new_m[k] for k in WEIGHTS], *[new_v[k] for k in WEIGHTS])
```

```python
import math

import jax
import jax.numpy as jnp
from jax import lax
from jax.experimental import pallas as pl
from jax.experimental.pallas import tpu as pltpu

F32 = jnp.float32
BF16 = jnp.bfloat16
MESH = pl.DeviceIdType.MESH

EPS = 1e-6
NEG = -1e30
HEAD = 64
LANES = 128
MLA_QK = 96
ROPE_THETA = 10000.0
REL_BUCKETS = 32
REL_MAX_DIST = 128
WINDOW = 128
D_FF = 2816
ADAM_LR, ADAM_B1, ADAM_B2, ADAM_EPS, ADAM_WD, ADAM_STEP = 0.001, 0.9, 0.999, 1e-08, 0.01, 10

VMEM_LIMIT = 56 * 1024 * 1024
STRIP = 32

P_SBQ, P_SBK, P_SBV, P_CQ, P_CKV, P_SLAB, P_SWQ, P_SWK, P_SWV, P_END = 0, 256, 512, 768, 1024, 1152, 1280, 1664, 1792, 1920


def _cp(*sem):
    return pltpu.CompilerParams(dimension_semantics=sem, vmem_limit_bytes=VMEM_LIMIT)


def _dot(a, b):
    return jnp.dot(a, b, preferred_element_type=F32)


def _dot_nt(a, b):
    return lax.dot_general(a, b, (((1,), (1,)), ((), ())), preferred_element_type=F32)


def _dot_tn(a, b):
    return lax.dot_general(a, b, (((0,), (0,)), ((), ())), preferred_element_type=F32)


def _lane_masks():
    lane = lax.broadcasted_iota(jnp.int32, (1, LANES), 1)
    return (lane < HEAD, lane >= HEAD)


def _tile(n, cap, align=128):
    if n <= cap:
        return n
    t = cap - cap % align
    while t >= align:
        if n % t == 0:
            return t
        t -= align
    return n


def matmul(a, b, *, ta=False, tb=False, out_dtype=F32, tm=512, tn=512, tk=8192, name):
    M, K = (a.shape[1], a.shape[0]) if ta else a.shape
    N = b.shape[0] if tb else b.shape[1]
    tm, tn, tk = _tile(M, tm), _tile(N, tn), _tile(K, tk)
    nk = K // tk

    def body(a_ref, b_ref, o_ref, *scratch):
        av = a_ref[...].astype(BF16)
        bv = b_ref[...].astype(BF16)
        if ta:
            part = _dot_tn(av, bv)
        elif tb:
            part = _dot_nt(av, bv)
        else:
            part = _dot(av, bv)
        if nk == 1:
            o_ref[...] = part.astype(out_dtype)
        else:
            acc_ref, = scratch
            k = pl.program_id(2)

            @pl.when(k == 0)
            def _():
                acc_ref[...] = part

            @pl.when(k > 0)
            def _():
                acc_ref[...] += part

            @pl.when(k == nk - 1)
            def _():
                o_ref[...] = acc_ref[...].astype(out_dtype)

    n_outer = nk == 1 and tn * b.dtype.itemsize > tm * a.dtype.itemsize
    ij = (lambda p, q: (q, p)) if n_outer else (lambda p, q: (p, q))
    a_map = (lambda p, q, k: (k, ij(p, q)[0])) if ta else (lambda p, q, k: (ij(p, q)[0], k))
    b_map = (lambda p, q, k: (ij(p, q)[1], k)) if tb else (lambda p, q, k: (k, ij(p, q)[1]))
    grid = (N // tn, M // tm, nk) if n_outer else (M // tm, N // tn, nk)
    return pl.pallas_call(
        body, name=name, grid=grid,
        in_specs=[pl.BlockSpec((tk, tm) if ta else (tm, tk), a_map), pl.BlockSpec((tn, tk) if tb else (tk, tn), b_map)],
        out_specs=pl.BlockSpec((tm, tn), lambda p, q, k: ij(p, q)),
        out_shape=jax.ShapeDtypeStruct((M, N), out_dtype),
        scratch_shapes=[] if nk == 1 else [pltpu.VMEM((tm, tn), F32)],
        compiler_params=_cp("parallel", "parallel", "arbitrary"),
    )(a, b)


def matmul_res(a, b, res, gate, seq, *, tm=512, tn=1024, name):
    M, K = a.shape
    N = b.shape[1]
    tm, tn = _tile(min(M, seq), tm), _tile(N, tn)
    per_seq = seq // tm

    def body(a_ref, b_ref, r_ref, g_ref, y_ref, x_ref):
        y = _dot(a_ref[...].astype(BF16), b_ref[...].astype(BF16))
        y_ref[...] = y
        x_ref[...] = r_ref[...] + g_ref[...] * y

    out = jax.ShapeDtypeStruct((M, N), F32)
    return pl.pallas_call(
        body, name=name, grid=(M // tm, N // tn),
        in_specs=[pl.BlockSpec((tm, K), lambda i, j: (i, 0)), pl.BlockSpec((K, tn), lambda i, j: (0, j)),
                  pl.BlockSpec((tm, tn), lambda i, j: (i, j)), pl.BlockSpec((None, 1, tn), lambda i, j: (lax.div(i, jnp.int32(per_seq)), 0, j))],
        out_specs=[pl.BlockSpec((tm, tn), lambda i, j: (i, j))] * 2,
        out_shape=[out, out], compiler_params=_cp("parallel", "parallel"),
    )(a, b, res, gate)


def rms_fwd(x3, blk, W, g, sc=None, sh=None, *, tm=512, name):
    Bl, S, _ = x3.shape
    tm = min(tm, S)
    mod = sc is not None

    def body(x_ref, g_ref, *rest):
        o_ref = rest[-1]
        x = x_ref[...]
        r = lax.rsqrt(jnp.mean(x * x, axis=-1, keepdims=True) + EPS)
        y = x * r * g_ref[...]
        if mod:
            y = y * (1.0 + rest[0][...]) + rest[1][...]
        o_ref[...] = y.astype(BF16)

    vec = pl.BlockSpec((None, 1, W), lambda b, s: (b, 0, 0))
    return pl.pallas_call(
        body, name=name, grid=(Bl, S // tm),
        in_specs=[pl.BlockSpec((None, tm, W), lambda b, s: (b, s, blk)), pl.BlockSpec((1, W), lambda b, s: (0, 0))] + ([vec, vec] if mod else []),
        out_specs=pl.BlockSpec((None, tm, W), lambda b, s: (b, s, 0)),
        out_shape=jax.ShapeDtypeStruct((Bl, S, W), BF16),
        compiler_params=_cp("parallel", "parallel"),
    )(x3, g, *([sc, sh] if mod else []))


def rms_bwd(x3, blk, W, dy3, g, sc=None, dres3=None, *, tm=256, name):
    Bl, S, _ = x3.shape
    tm = min(tm, S)
    mod = sc is not None
    res = dres3 is not None

    def body(*refs):
        x_ref, dy_ref, g_ref = refs[:3]
        k = 3
        sc_ref = dr_ref = None
        if mod:
            sc_ref = refs[k]
            k += 1
        if res:
            dr_ref = refs[k]
            k += 1
        dx_ref, dg_ref = refs[k], refs[k + 1]
        b, s = pl.program_id(0), pl.program_id(1)
        x = x_ref[...]
        dy = dy_ref[...].astype(F32)
        g = g_ref[...]
        r = lax.rsqrt(jnp.mean(x * x, axis=-1, keepdims=True) + EPS)
        n = x * r
        if mod:
            dsc_ref, dsh_ref = refs[k + 2], refs[k + 3]
            one_sc = 1.0 + sc_ref[...]

            @pl.when(s == 0)
            def _():
                dsc_ref[...] = jnp.zeros_like(dsc_ref)
                dsh_ref[...] = jnp.zeros_like(dsh_ref)

            dsh_ref[...] += jnp.sum(dy, axis=0, keepdims=True)
            dsc_ref[...] += jnp.sum(dy * n * g, axis=0, keepdims=True)
            dyn = dy * one_sc
        else:
            dyn = dy

        @pl.when((b == 0) & (s == 0))
        def _():
            dg_ref[...] = jnp.zeros_like(dg_ref)

        dg_ref[...] += jnp.sum(dyn * n, axis=0, keepdims=True)
        dn = dyn * g
        dx = r * (dn - n * jnp.mean(dn * n, axis=-1, keepdims=True))
        if res:
            dx = dx + dr_ref[...]
        dx_ref[...] = dx

    blkspec = pl.BlockSpec((None, tm, W), lambda b, s: (b, s, 0))
    vec = pl.BlockSpec((None, 1, W), lambda b, s: (b, 0, 0))
    row = pl.BlockSpec((1, W), lambda b, s: (0, 0))
    in_specs = [pl.BlockSpec((None, tm, W), lambda b, s: (b, s, blk)), blkspec, row] + ([vec] if mod else []) + ([blkspec] if res else [])
    out_specs = [blkspec, row] + ([vec, vec] if mod else [])
    out_shape = [jax.ShapeDtypeStruct((Bl, S, W), F32), jax.ShapeDtypeStruct((1, W), F32)]
    if mod:
        out_shape += [jax.ShapeDtypeStruct((Bl, 1, W), F32)] * 2
    args = [x3, dy3, g] + ([sc] if mod else []) + ([dres3] if res else [])
    return pl.pallas_call(
        body, name=name, grid=(Bl, S // tm), in_specs=in_specs, out_specs=out_specs, out_shape=out_shape,
        compiler_params=_cp("arbitrary", "arbitrary"),
    )(*args)


def pair_rms_fwd(x3, blk0, npairs, g2, *, tm=1024, name):
    Bl, S, _ = x3.shape
    tm = min(tm, S)

    def body(x_ref, g_ref, o_ref):
        lo, hi = _lane_masks()
        x = x_ref[...]
        xx = x * x
        s0 = jnp.sum(jnp.where(lo, xx, 0.0), axis=-1, keepdims=True)
        s1 = jnp.sum(jnp.where(hi, xx, 0.0), axis=-1, keepdims=True)
        r = jnp.where(lo, lax.rsqrt(s0 / HEAD + EPS), lax.rsqrt(s1 / HEAD + EPS))
        o_ref[...] = (x * r * g_ref[...]).astype(BF16)

    return pl.pallas_call(
        body, name=name, grid=(Bl, S // tm, npairs),
        in_specs=[pl.BlockSpec((None, tm, LANES), lambda b, s, p: (b, s, blk0 + p)), pl.BlockSpec((1, LANES), lambda b, s, p: (0, 0))],
        out_specs=pl.BlockSpec((None, tm, LANES), lambda b, s, p: (b, s, p)),
        out_shape=jax.ShapeDtypeStruct((Bl, S, LANES * npairs), BF16),
        compiler_params=_cp("parallel", "parallel", "parallel"),
    )(x3, g2)


def pair_rms_bwd(x3, blk0, npairs, dy3, g2, *, tm=1024, name):
    Bl, S, _ = x3.shape
    tm = min(tm, S)

    def body(x_ref, dy_ref, g_ref, dx_ref, dg_ref):
        lo, hi = _lane_masks()
        first = (pl.program_id(0) == 0) & (pl.program_id(1) == 0) & (pl.program_id(2) == 0)
        x = x_ref[...]
        dy = dy_ref[...]
        xx = x * x
        s0 = jnp.sum(jnp.where(lo, xx, 0.0), axis=-1, keepdims=True)
        s1 = jnp.sum(jnp.where(hi, xx, 0.0), axis=-1, keepdims=True)
        r = jnp.where(lo, lax.rsqrt(s0 / HEAD + EPS), lax.rsqrt(s1 / HEAD + EPS))
        n = x * r

        @pl.when(first)
        def _():
            dg_ref[...] = jnp.zeros_like(dg_ref)

        part = jnp.sum(dy * n, axis=0, keepdims=True)
        dg_ref[...] += part + pltpu.roll(part, HEAD, 1)
        dn = dy * g_ref[...]
        t = dn * n
        m0 = jnp.sum(jnp.where(lo, t, 0.0), axis=-1, keepdims=True)
        m1 = jnp.sum(jnp.where(hi, t, 0.0), axis=-1, keepdims=True)
        dx_ref[...] = r * (dn - n * (jnp.where(lo, m0, m1) / HEAD))

    return pl.pallas_call(
        body, name=name, grid=(Bl, S // tm, npairs),
        in_specs=[pl.BlockSpec((None, tm, LANES), lambda b, s, p: (b, s, blk0 + p)), pl.BlockSpec((None, tm, LANES), lambda b, s, p: (b, s, p)),
                  pl.BlockSpec((1, LANES), lambda b, s, p: (0, 0))],
        out_specs=[pl.BlockSpec((None, tm, LANES), lambda b, s, p: (b, s, p)), pl.BlockSpec((1, LANES), lambda b, s, p: (0, 0))],
        out_shape=[jax.ShapeDtypeStruct((Bl, S, LANES * npairs), F32), jax.ShapeDtypeStruct((1, LANES), F32)],
        compiler_params=_cp("arbitrary", "arbitrary", "arbitrary"),
    )(x3, dy3, g2)


def _rot(y, cos_t, sin_a, sin_b):
    return y * cos_t + pltpu.roll(y, LANES - 16, 1) * sin_a + pltpu.roll(y, 16, 1) * sin_b


def _rot_t(d, cos_t, sin_a, sin_b):
    return d * cos_t + pltpu.roll(d * sin_a, 16, 1) + pltpu.roll(d * sin_b, LANES - 16, 1)


def rope_norm_fwd(x3, nheads, g, tabs, slab=None, *, tm=1024, name):
    Bl, S, _ = x3.shape
    tm = min(tm, S)
    has_slab = slab is not None

    def body(*refs):
        x_ref, g_ref, c_ref, sa_ref, sb_ref = refs[:5]
        o_ref = refs[-1]
        x = x_ref[...]
        if has_slab:
            x = x + refs[5][...]
        r = lax.rsqrt(jnp.sum(x * x, axis=-1, keepdims=True) / MLA_QK + EPS)
        o_ref[...] = _rot(x * r * g_ref[...], c_ref[...], sa_ref[...], sb_ref[...]).astype(BF16)

    head = pl.BlockSpec((None, tm, LANES), lambda b, s, h: (b, s, h))
    tab = pl.BlockSpec((None, tm, LANES), lambda b, s, h: (b, s, 0))
    in_specs = [head, pl.BlockSpec((1, LANES), lambda b, s, h: (0, 0)), tab, tab, tab]
    args = [x3, g, *tabs]
    if has_slab:
        sblk = slab[1]
        in_specs.append(pl.BlockSpec((None, tm, LANES), lambda b, s, h: (b, s, sblk)))
        args.append(slab[0])
    return pl.pallas_call(
        body, name=name, grid=(Bl, S // tm, nheads), in_specs=in_specs, out_specs=head,
        out_shape=jax.ShapeDtypeStruct((Bl, S, LANES * nheads), BF16),
        compiler_params=_cp("parallel", "parallel", "parallel"),
    )(*args)


def rope_norm_bwd(x3, nheads, dy3, g, tabs, slab=None, *, tm=1024, name):
    Bl, S, _ = x3.shape
    tm = min(tm, S)
    has_slab = slab is not None

    def body(*refs):
        x_ref, dy_ref, g_ref, c_ref, sa_ref, sb_ref = refs[:6]
        k = 7 if has_slab else 6
        dx_ref, dg_ref = refs[k], refs[k + 1]
        h = pl.program_id(2)
        first = (pl.program_id(0) == 0) & (pl.program_id(1) == 0) & (h == 0)
        x = x_ref[...]
        if has_slab:
            x = x + refs[6][...]
        g = g_ref[...]
        r = lax.rsqrt(jnp.sum(x * x, axis=-1, keepdims=True) / MLA_QK + EPS)
        n = x * r
        d = _rot_t(dy_ref[...], c_ref[...], sa_ref[...], sb_ref[...])

        @pl.when(first)
        def _():
            dg_ref[...] = jnp.zeros_like(dg_ref)

        dg_ref[...] += jnp.sum(d * n, axis=0, keepdims=True)
        dn = d * g
        dx = r * (dn - n * (jnp.sum(dn * n, axis=-1, keepdims=True) / MLA_QK))
        dx_ref[...] = dx.astype(BF16)
        if has_slab:
            ds_ref = refs[k + 2]

            @pl.when(h == 0)
            def _():
                ds_ref[...] = dx

            @pl.when(h > 0)
            def _():
                ds_ref[...] += dx

    head = pl.BlockSpec((None, tm, LANES), lambda b, s, h: (b, s, h))
    tab = pl.BlockSpec((None, tm, LANES), lambda b, s, h: (b, s, 0))
    row = pl.BlockSpec((1, LANES), lambda b, s, h: (0, 0))
    in_specs = [head, head, row, tab, tab, tab]
    args = [x3, dy3, g, *tabs]
    out_specs = [head, row]
    out_shape = [jax.ShapeDtypeStruct((Bl, S, LANES * nheads), BF16), jax.ShapeDtypeStruct((1, LANES), F32)]
    if has_slab:
        sblk = slab[1]
        in_specs.append(pl.BlockSpec((None, tm, LANES), lambda b, s, h: (b, s, sblk)))
        args.append(slab[0])
        out_specs.append(tab)
        out_shape.append(jax.ShapeDtypeStruct((Bl, S, LANES), F32))
    return pl.pallas_call(
        body, name=name, grid=(Bl, S // tm, nheads), in_specs=in_specs, out_specs=out_specs, out_shape=out_shape,
        compiler_params=_cp("arbitrary", "arbitrary", "arbitrary"),
    )(*args)


def _softplus(z):
    return jnp.maximum(z, 0.0) + jnp.log(1.0 + jnp.exp(-jnp.abs(z)))


def _split_dots(xs, u):
    hi = [x.astype(BF16) for x in xs]
    lo = [(x - h.astype(F32)).astype(BF16) for x, h in zip(xs, hi)]
    top = [_dot(h, u) for h in hi]
    return [t + _dot(l, u) for t, l in zip(top, lo)]


SB_BLOCK = 256
SB_QBLOCK = 512


def sb_attn_fwd(proj3, *, plans=None, name):
    Bl, S, _ = proj3.shape
    tk = min(SB_BLOCK, S)
    tq = min(SB_QBLOCK, S)
    per_q = tq // tk
    scale = HEAD ** -0.5
    qb, kb0, vb0 = P_SBQ // LANES, P_SBK // LANES, P_SBV // LANES

    def body(q_ref, k_ref, v_ref, o_ref, rt_ref):
        i = pl.program_id(2)
        masks = _lane_masks()
        lane = lax.broadcasted_iota(jnp.int32, (1, LANES), 1)
        q = q_ref[...]
        qh = [jnp.where(m, q, 0.0).astype(BF16) for m in masks]
        rr = lax.broadcasted_iota(jnp.int32, (tq, tk), 0)
        cc = lax.broadcasted_iota(jnp.int32, (tq, tk), 1)
        u = (lax.broadcasted_iota(jnp.int32, (tk, tk), 0) > lax.broadcasted_iota(jnp.int32, (tk, tk), 1)).astype(BF16)

        rt_ref[...] = jnp.zeros_like(rt_ref)

        def step(j, carry, masked):
            r0, r1, acc = carry
            off = pl.multiple_of(j * tk, tk)
            kb = k_ref[pl.ds(off, tk), :].astype(BF16)
            vb = v_ref[pl.ds(off, tk), :]
            strict = (cc + j * tk) < (rr + i * tq) if masked else None
            only = (lambda t: jnp.where(strict, t, 0.0)) if masked else (lambda t: t)
            rt_ref[...] = jnp.where(lane == j, r0, jnp.where(lane == j + HEAD, r1, rt_ref[...]))
            rs, two = [r0, r1], range(2)
            z = [_dot_nt(qh[h], kb) * scale for h in two]
            sp = [_softplus(z[h]) for h in two]
            keep = [only(-sp[h]) for h in two]
            suf = _split_dots(keep, u)
            w = [only(jnp.exp((z[h] - sp[h]) + suf[h] + rs[h])) for h in two]
            pv = [_dot(w[h].astype(BF16), jnp.where(masks[h], vb, 0.0).astype(BF16)) for h in two]
            return rs[0] + jnp.sum(keep[0], axis=1, keepdims=True), rs[1] + jnp.sum(keep[1], axis=1, keepdims=True), acc + (pv[0] + pv[1])

        zero = jnp.zeros((tq, 1), F32)
        carry = (zero, zero, jnp.zeros((tq, LANES), F32))
        for t in range(per_q):
            carry = step((i + 1) * per_q - 1 - t, carry, True)
        _, _, acc = lax.fori_loop(0, i * per_q, lambda t, c: step(i * per_q - 1 - t, c, False), carry)
        o_ref[...] = acc

    seq = lambda blk0: pl.BlockSpec((None, S, LANES), lambda b, p, i: (b, 0, blk0 + p))
    out = pl.BlockSpec((None, tq, LANES), lambda b, p, i: (b, i, p))
    shp = jax.ShapeDtypeStruct((Bl, S, 2 * LANES), F32)
    return call_with_plans(
        body, plans, name=name, grid=(Bl, 2, S // tq),
        in_specs=[pl.BlockSpec((None, tq, LANES), lambda b, p, i: (b, i, qb + p)), seq(kb0), seq(vb0)],
        out_specs=[out, out], out_shape=[shp, shp], scratch_shapes=[], args=[proj3, proj3, proj3],
        sem=("arbitrary",) * 3 if plans else ("parallel", "parallel", "arbitrary"))


def sb_attn_bwd(proj3, rt3, do3, *, do_blk0=0, plans=None, name):
    Bl, S, _ = proj3.shape
    tk = min(SB_BLOCK, S)
    tq = min(SB_QBLOCK, S)
    per_q = tq // tk
    scale = HEAD ** -0.5
    qb, kb0, vb0 = P_SBQ // LANES, P_SBK // LANES, P_SBV // LANES

    def body(q_ref, k_ref, v_ref, rt_ref, do_ref, dq_ref, dk_ref, dv_ref):
        i = pl.program_id(2)

        @pl.when(i == 0)
        def _():
            dk_ref[...] = jnp.zeros_like(dk_ref)
            dv_ref[...] = jnp.zeros_like(dv_ref)

        masks = _lane_masks()
        lane = lax.broadcasted_iota(jnp.int32, (1, LANES), 1)
        q = q_ref[...]
        qh = [jnp.where(m, q, 0.0).astype(BF16) for m in masks]
        do_b = do_ref[...].astype(BF16)
        doh = [jnp.where(m, do_b, jnp.zeros_like(do_b)) for m in masks]
        rt = rt_ref[...]
        rr = lax.broadcasted_iota(jnp.int32, (tq, tk), 0)
        cc = lax.broadcasted_iota(jnp.int32, (tq, tk), 1)
        ur = lax.broadcasted_iota(jnp.int32, (tk, tk), 0)
        uc = lax.broadcasted_iota(jnp.int32, (tk, tk), 1)
        u_suffix = (ur > uc).astype(BF16)
        u_prefix = (ur < uc).astype(BF16)

        def step(j, carry, masked):
            p0, p1, dq = carry
            off = pl.multiple_of(j * tk, tk)
            kf = k_ref[pl.ds(off, tk), :]
            kb = kf.astype(BF16)
            vb = v_ref[pl.ds(off, tk), :]
            strict = (cc + j * tk) < (rr + i * tq) if masked else None
            only = (lambda t: jnp.where(strict, t, 0.0)) if masked else (lambda t: t)
            ps, two = [p0, p1], range(2)
            r_j = [jnp.sum(jnp.where(lane == j + h * HEAD, rt, 0.0), axis=1, keepdims=True) for h in two]
            z = [_dot_nt(qh[h], kb) * scale for h in two]
            dw = [_dot_nt(doh[h], jnp.where(masks[h], vb, 0.0).astype(BF16)) for h in two]
            sp = [_softplus(z[h]) for h in two]
            keep = [only(-sp[h]) for h in two]
            suf = _split_dots(keep, u_suffix)
            w = [only(jnp.exp((z[h] - sp[h]) + suf[h] + r_j[h])) for h in two]
            g = [dw[h] * w[h] for h in two]
            pre = _split_dots(g, u_prefix)
            dzb = [(only(g[h] * jnp.exp(-sp[h]) - jnp.exp(z[h] - sp[h]) * (pre[h] + ps[h])) * scale).astype(BF16) for h in two]
            dqs = [_dot(dzb[h], jnp.where(masks[h], kf, 0.0).astype(BF16)) for h in two]
            dks = [_dot_tn(dzb[h], qh[h]) for h in two]
            dvs = [_dot_tn(w[h].astype(BF16), doh[h]) for h in two]
            dk_ref[pl.ds(off, tk), :] += dks[0] + dks[1]
            dv_ref[pl.ds(off, tk), :] += dvs[0] + dvs[1]
            return ps[0] + jnp.sum(g[0], axis=1, keepdims=True), ps[1] + jnp.sum(g[1], axis=1, keepdims=True), dq + (dqs[0] + dqs[1])

        zero = jnp.zeros((tq, 1), F32)
        carry = lax.fori_loop(0, i * per_q, lambda j, c: step(j, c, False), (zero, zero, jnp.zeros((tq, LANES), F32)))
        for t in range(per_q):
            carry = step(i * per_q + t, carry, True)
        dq_ref[...] = carry[2]

    seq_in = lambda blk0: pl.BlockSpec((None, S, LANES), lambda b, p, i: (b, 0, blk0 + p))
    blk = pl.BlockSpec((None, tq, LANES), lambda b, p, i: (b, i, p))
    seq_out = pl.BlockSpec((None, S, LANES), lambda b, p, i: (b, 0, p))
    shp = jax.ShapeDtypeStruct((Bl, S, 2 * LANES), F32)
    return call_with_plans(
        body, plans, name=name, grid=(Bl, 2, S // tq),
        in_specs=[pl.BlockSpec((None, tq, LANES), lambda b, p, i: (b, i, qb + p)), seq_in(kb0), seq_in(vb0), blk,
                  pl.BlockSpec((None, tq, LANES), lambda b, p, i: (b, i, do_blk0 + p))],
        out_specs=[blk, seq_out, seq_out], out_shape=[shp, shp, shp], scratch_shapes=[], args=[proj3, proj3, proj3, rt3, do3],
        sem=("arbitrary",) * 3 if plans else ("parallel", "parallel", "arbitrary"))


def mla_attn_fwd(q3, k3, kv3, vblk0, *, tq=512, tk=512, plans=None, name):
    Bl, S, _ = q3.shape
    tq = min(tq, S)
    tk = min(tk, tq)
    per_q = tq // tk
    scale = MLA_QK ** -0.5

    def body(q_ref, k_ref, v_ref, o_ref, lse_ref):
        i = pl.program_id(2)
        masks = _lane_masks()
        rr = lax.broadcasted_iota(jnp.int32, (tq, tk), 0)
        cc = lax.broadcasted_iota(jnp.int32, (tq, tk), 1)
        qh = [q_ref[:, h * LANES:(h + 1) * LANES] for h in range(2)]

        def step(j, carry):
            m0, l0, m1, l1, acc = carry
            off = pl.multiple_of(j * tk, tk)
            vb = v_ref[pl.ds(off, tk), :]
            causal = (cc + j * tk) <= (rr + i * tq)
            ms, ls, two = [m0, m1], [l0, l1], range(2)
            kh = [k_ref[pl.ds(off, tk), h * LANES:(h + 1) * LANES] for h in two]
            s = [jnp.where(causal, _dot_nt(qh[h], kh[h]) * scale, NEG) for h in two]
            m_new = [jnp.maximum(ms[h], jnp.max(s[h], axis=1, keepdims=True)) for h in two]
            p = [jnp.exp(s[h] - m_new[h]) for h in two]
            alpha = [jnp.exp(ms[h] - m_new[h]) for h in two]
            ls = [alpha[h] * ls[h] + jnp.sum(p[h], axis=1, keepdims=True) for h in two]
            add = [_dot(p[h].astype(BF16), jnp.where(masks[h], vb, 0.0).astype(BF16)) for h in two]
            acc = acc * jnp.where(masks[0], alpha[0], alpha[1]) + (add[0] + add[1])
            return m_new[0], ls[0], m_new[1], ls[1], acc

        neg = jnp.full((tq, 1), NEG, F32)
        zero = jnp.zeros((tq, 1), F32)
        m0, l0, m1, l1, acc = lax.fori_loop(0, (i + 1) * per_q, step, (neg, zero, neg, zero, jnp.zeros((tq, LANES), F32)))
        o_ref[...] = acc / jnp.where(masks[0], l0, l1)
        lse_ref[...] = jnp.where(masks[0], m0 + jnp.log(l0), m1 + jnp.log(l1))

    out = pl.BlockSpec((None, tq, LANES), lambda b, p, i: (b, i, p))
    shp = jax.ShapeDtypeStruct((Bl, S, 3 * LANES), F32)
    return call_with_plans(
        body, plans, name=name, grid=(Bl, 3, S // tq),
        in_specs=[pl.BlockSpec((None, tq, 2 * LANES), lambda b, p, i: (b, i, p)), pl.BlockSpec((None, S, 2 * LANES), lambda b, p, i: (b, 0, p)),
                  pl.BlockSpec((None, S, LANES), lambda b, p, i: (b, 0, vblk0 + p))],
        out_specs=[out, out], out_shape=[shp, shp], scratch_shapes=[], args=[q3, k3, kv3],
        sem=("arbitrary",) * 3 if plans else ("parallel", "parallel", "arbitrary"))


def mla_attn_bwd(q3, k3, kv3, vblk0, o3, lse3, do3, *, do_blk0=0, tq=512, tk=512, name):
    Bl, S, _ = q3.shape
    tq = min(tq, S)
    tk = min(tk, tq)
    per_q = tq // tk
    nq = S // tq
    scale = MLA_QK ** -0.5

    def body(q_ref, k_ref, v_ref, o_ref, lse_ref, do_ref, dq_ref, dk_ref, dv_ref, s_scr, dp_scr, p_scr, ds_scr):
        j = pl.program_id(2)

        @pl.when(j == 0)
        def _():
            dq_ref[...] = jnp.zeros_like(dq_ref)

        masks = _lane_masks()
        vb = v_ref[...]
        vh = [jnp.where(m, vb, 0.0).astype(BF16) for m in masks]
        kh = [k_ref[:, h * LANES:(h + 1) * LANES] for h in range(2)]
        i0 = lax.div(j, jnp.int32(per_q))

        def step(i, carry, masked):
            dk0, dk1, dv = carry
            off = pl.multiple_of(i * tq, tq)
            do_b = do_ref[pl.ds(off, tq), :].astype(BF16)
            prod = do_b.astype(F32) * o_ref[pl.ds(off, tq), :]
            lse = lse_ref[pl.ds(off, tq), :]
            two = range(2)
            qh = [q_ref[pl.ds(off, tq), h * LANES:(h + 1) * LANES] for h in two]
            doh = [jnp.where(masks[h], do_b, jnp.zeros_like(do_b)) for h in two]
            delta = [jnp.sum(jnp.where(masks[h], prod, 0.0), axis=1, keepdims=True) for h in two]
            lse_h = [lse[:, h * HEAD:h * HEAD + 1] for h in two]
            for h in two:
                s_scr[h] = _dot_nt(qh[h], kh[h])
            for h in two:
                dp_scr[h] = _dot_nt(doh[h], vh[h])
            for r0 in range(0, tq, STRIP):
                rows = slice(r0, r0 + STRIP)
                for h in two:
                    s = s_scr[h, rows, :] * scale
                    if masked:
                        rr = lax.broadcasted_iota(jnp.int32, (STRIP, tk), 0) + (i * tq + r0)
                        cc = lax.broadcasted_iota(jnp.int32, (STRIP, tk), 1) + j * tk
                        s = jnp.where(cc <= rr, s, NEG)
                    p = jnp.exp(s - lse_h[h][rows])
                    p_scr[h, rows, :] = p.astype(BF16)
                    ds_scr[h, rows, :] = (p * (dp_scr[h, rows, :] - delta[h][rows])).astype(BF16)
            dqs = [_dot(ds_scr[h], kh[h]) * scale for h in two]
            dks = [dk0 + _dot_tn(ds_scr[0], qh[0]), dk1 + _dot_tn(ds_scr[1], qh[1])]
            dv = dv + _dot_tn(p_scr[0], doh[0]) + _dot_tn(p_scr[1], doh[1])
            for h in two:
                dq_ref[pl.ds(off, tq), h * LANES:(h + 1) * LANES] += dqs[h]
            return dks[0], dks[1], dv

        zero = jnp.zeros((tk, LANES), F32)
        carry = step(i0, (zero, zero, zero), True)
        dk0, dk1, dv = lax.fori_loop(i0 + 1, nq, lambda i, c: step(i, c, False), carry)
        dk_ref[:, 0:LANES] = dk0 * scale
        dk_ref[:, LANES:2 * LANES] = dk1 * scale
        dv_ref[...] = dv.astype(BF16)

    seq1 = pl.BlockSpec((None, S, LANES), lambda b, p, j: (b, 0, p))
    seq2 = pl.BlockSpec((None, S, 2 * LANES), lambda b, p, j: (b, 0, p))
    return pl.pallas_call(
        body, name=name, grid=(Bl, 3, S // tk),
        in_specs=[seq2, pl.BlockSpec((None, tk, 2 * LANES), lambda b, p, j: (b, j, p)),
                  pl.BlockSpec((None, tk, LANES), lambda b, p, j: (b, j, vblk0 + p)), seq1, seq1,
                  pl.BlockSpec((None, S, LANES), lambda b, p, j: (b, 0, do_blk0 + p))],
        out_specs=[seq2, pl.BlockSpec((None, tk, 2 * LANES), lambda b, p, j: (b, j, p)), pl.BlockSpec((None, tk, LANES), lambda b, p, j: (b, j, p))],
        out_shape=[jax.ShapeDtypeStruct((Bl, S, 6 * LANES), F32), jax.ShapeDtypeStruct((Bl, S, 6 * LANES), F32), jax.ShapeDtypeStruct((Bl, S, 3 * LANES), BF16)],
        scratch_shapes=[pltpu.VMEM((2, tq, tk), F32), pltpu.VMEM((2, tq, tk), F32), pltpu.VMEM((2, tq, tk), BF16), pltpu.VMEM((2, tq, tk), BF16)],
        compiler_params=_cp("parallel", "parallel", "arbitrary"),
    )(q3, k3, kv3, o3, lse3, do3)


def _bucket_table():
    a = jnp.arange(WINDOW)[:, None]
    b = jnp.arange(2 * WINDOW)[None, :]
    dist = WINDOW + a - b
    max_exact = REL_BUCKETS // 2
    n = jnp.maximum(dist, 0)
    nf = jnp.maximum(n, 1).astype(F32)
    large = max_exact + (jnp.log(nf / max_exact) / math.log(REL_MAX_DIST / max_exact) * (REL_BUCKETS - max_exact)).astype(jnp.int32)
    large = jnp.minimum(large, REL_BUCKETS - 1)
    bucket = jnp.where(n < max_exact, n, large)
    return jnp.where((dist >= 0) & (dist < WINDOW), bucket, -1).astype(jnp.int32)


def swa_bias(rel_flat, bucket, *, name):
    def body(t_ref, b_ref, o_ref):
        bk = b_ref[...]
        for p in range(3):
            for hh in range(2):
                h = hh * 3 + p
                acc = jnp.full(bk.shape, NEG, F32)
                for b in range(REL_BUCKETS):
                    acc = jnp.where(bk == b, t_ref[b * 6 + h], acc)
                o_ref[p, hh] = acc

    return pl.pallas_call(
        body, name=name,
        in_specs=[pl.BlockSpec(memory_space=pltpu.SMEM), pl.BlockSpec(memory_space=pltpu.VMEM)],
        out_specs=pl.BlockSpec(memory_space=pltpu.VMEM),
        out_shape=jax.ShapeDtypeStruct((3, 2, WINDOW, 2 * WINDOW), F32),
    )(rel_flat, bucket)


def swa_bias_bwd(dbias, bucket, *, name):
    Bl = dbias.shape[0]

    def body(d_ref, b_ref, o_ref):
        bk = b_ref[...]
        lane = lax.broadcasted_iota(jnp.int32, (1, LANES), 1)
        rows = []
        for h in range(6):
            hh, p = divmod(h, 3)
            d = d_ref[0, p, hh]
            for bl in range(1, Bl):
                d = d + d_ref[bl, p, hh]
            row = jnp.zeros((1, LANES), F32)
            for b in range(REL_BUCKETS):
                s = jnp.sum(jnp.sum(jnp.where(bk == b, d, 0.0), axis=1, keepdims=True), axis=0, keepdims=True)
                row = row + jnp.where(lane == b, s, 0.0)
            rows.append(row)
        rows += [jnp.zeros((1, LANES), F32)] * 2
        o_ref[...] = jnp.concatenate(rows, axis=0)

    return pl.pallas_call(
        body, name=name,
        in_specs=[pl.BlockSpec(memory_space=pltpu.VMEM)] * 2, out_specs=pl.BlockSpec(memory_space=pltpu.VMEM),
        out_shape=jax.ShapeDtypeStruct((8, LANES), F32),
    )(dbias, bucket)


SWA_QBLOCKS = 8


def _swa_specs(vblk, nqb):
    rows = nqb * WINDOW
    cur = lambda blk: pl.BlockSpec((None, rows, LANES), lambda b, p, n: (b, n, blk))
    prev = lambda blk: pl.BlockSpec((None, WINDOW, LANES), lambda b, p, n: (b, jnp.maximum(n * nqb - 1, 0), blk))
    return [pl.BlockSpec((None, rows, LANES), lambda b, p, n: (b, n, p)), cur(0), prev(0), cur(vblk), prev(vblk),
            pl.BlockSpec((None, 2, WINDOW, 2 * WINDOW), lambda b, p, n: (p, 0, 0, 0)), pl.BlockSpec((None, 2, LANES), lambda b, p, n: (p, 0, 0))]


def _rows128(ref, m):
    return ref[m * WINDOW:(m + 1) * WINDOW, :]


def _swa_logits(qh, kp, kc, bias_h, first, scale):
    sp = jnp.where(first, NEG, _dot_nt(qh, kp) * scale + bias_h[:, :WINDOW])
    sc = _dot_nt(qh, kc) * scale + bias_h[:, WINDOW:]
    return sp, sc


def swa_attn_fwd(qn3, kn3, proj3, bias, sinks, *, plans=None, name):
    Bl, S, _ = qn3.shape
    scale = HEAD ** -0.5
    nqb = min(SWA_QBLOCKS, S // WINDOW)

    def body(q_ref, kc_ref, kp_ref, vc_ref, vp_ref, b_ref, s_ref, o_ref, lse_ref):
        seq_start = pl.program_id(2) == 0
        masks = _lane_masks()
        chains = [(m_, h) for m_ in range(nqb) for h in range(2)]
        kp = [kp_ref[...] if m_ == 0 else _rows128(kc_ref, m_ - 1) for m_ in range(nqb)]
        vp = [vp_ref[...] if m_ == 0 else _rows128(vc_ref, m_ - 1) for m_ in range(nqb)]
        kc = [_rows128(kc_ref, m_) for m_ in range(nqb)]
        vc = [_rows128(vc_ref, m_) for m_ in range(nqb)]
        sink = [s_ref[h:h + 1, 0:1] for h in range(2)]
        logits = {}
        for m_, h in chains:
            q = _rows128(q_ref, m_)
            qh = jnp.where(masks[h], q, jnp.zeros_like(q))
            logits[m_, h] = _swa_logits(qh, kp[m_], kc[m_], b_ref[h], seq_start if m_ == 0 else False, scale)
        mx = {c: jnp.maximum(jnp.maximum(jnp.max(logits[c][0], axis=1, keepdims=True), jnp.max(logits[c][1], axis=1, keepdims=True)), sink[c[1]])
              for c in chains}
        ex = {c: (jnp.exp(logits[c][0] - mx[c]), jnp.exp(logits[c][1] - mx[c])) for c in chains}
        den = {c: jnp.sum(ex[c][0], axis=1, keepdims=True) + jnp.sum(ex[c][1], axis=1, keepdims=True) + jnp.exp(sink[c[1]] - mx[c]) for c in chains}
        inv = {c: 1.0 / den[c] for c in chains}
        out = {}
        for m_, h in chains:
            c = (m_, h)
            out[c] = (_dot((ex[c][0] * inv[c]).astype(BF16), jnp.where(masks[h], vp[m_], 0.0).astype(BF16))
                      + _dot((ex[c][1] * inv[c]).astype(BF16), jnp.where(masks[h], vc[m_], 0.0).astype(BF16)))
        for m_ in range(nqb):
            o_ref[m_ * WINDOW:(m_ + 1) * WINDOW, :] = out[m_, 0] + out[m_, 1]
            lse_ref[m_ * WINDOW:(m_ + 1) * WINDOW, :] = jnp.where(masks[0], mx[m_, 0] + jnp.log(den[m_, 0]), mx[m_, 1] + jnp.log(den[m_, 1]))

    out = pl.BlockSpec((None, nqb * WINDOW, LANES), lambda b, p, n: (b, n, p))
    shp = jax.ShapeDtypeStruct((Bl, S, 3 * LANES), F32)
    return call_with_plans(
        body, plans, name=name, grid=(Bl, 3, S // (nqb * WINDOW)), in_specs=_swa_specs(P_SWV // LANES, nqb),
        out_specs=[out, out], out_shape=[shp, shp], scratch_shapes=[], args=[qn3, kn3, kn3, proj3, proj3, bias, sinks],
        sem=("arbitrary",) * 3 if plans else ("parallel", "parallel", "arbitrary"))


def swa_attn_bwd(qn3, kn3, proj3, bias, sinks, o3, lse3, do3, *, do_blk0=0, name):
    Bl, S, _ = qn3.shape
    scale = HEAD ** -0.5
    nqb = min(SWA_QBLOCKS, S // WINDOW)
    rows = nqb * WINDOW

    def body(q_ref, kc_ref, kp_ref, vc_ref, vp_ref, b_ref, s_ref, o_ref, lse_ref, do_ref,
             dq_ref, dk_ref, dv_ref, db_ref, dsk_ref):
        p_id, n = pl.program_id(1), pl.program_id(2)
        seq_start = n == 0

        @pl.when((p_id == 0) & seq_start)
        def _():
            dk_ref[...] = jnp.zeros_like(dk_ref)
            dv_ref[...] = jnp.zeros_like(dv_ref)

        @pl.when(seq_start)
        def _():
            db_ref[...] = jnp.zeros_like(db_ref)
            dsk_ref[...] = jnp.zeros_like(dsk_ref)

        masks = _lane_masks()
        zero = jnp.zeros((WINDOW, LANES), F32)
        chains = [(m_, h) for m_ in range(nqb) for h in range(2)]
        kp = [kp_ref[...] if m_ == 0 else _rows128(kc_ref, m_ - 1) for m_ in range(nqb)]
        vp = [vp_ref[...] if m_ == 0 else _rows128(vc_ref, m_ - 1) for m_ in range(nqb)]
        kc = [_rows128(kc_ref, m_) for m_ in range(nqb)]
        vc = [_rows128(vc_ref, m_) for m_ in range(nqb)]
        do_b = [_rows128(do_ref, m_).astype(BF16) for m_ in range(nqb)]
        prod = [do_b[m_].astype(F32) * _rows128(o_ref, m_) for m_ in range(nqb)]
        lse = [_rows128(lse_ref, m_) for m_ in range(nqb)]
        qh, doh, logits, lse_h, delta = {}, {}, {}, {}, {}
        for m_, h in chains:
            q = _rows128(q_ref, m_)
            qh[m_, h] = jnp.where(masks[h], q, jnp.zeros_like(q))
            doh[m_, h] = jnp.where(masks[h], do_b[m_], jnp.zeros_like(do_b[m_]))
            logits[m_, h] = _swa_logits(qh[m_, h], kp[m_], kc[m_], b_ref[h], seq_start if m_ == 0 else False, scale)
            lse_h[m_, h] = lse[m_][:, h * HEAD:h * HEAD + 1]
            delta[m_, h] = jnp.sum(jnp.where(masks[h], prod[m_], 0.0), axis=1, keepdims=True)
        pr = {c: (jnp.exp(logits[c][0] - lse_h[c]), jnp.exp(logits[c][1] - lse_h[c])) for c in chains}
        dp = {(m_, h): (_dot_nt(doh[m_, h], jnp.where(masks[h], vp[m_], 0.0).astype(BF16)),
                        _dot_nt(doh[m_, h], jnp.where(masks[h], vc[m_], 0.0).astype(BF16))) for m_, h in chains}
        ds = {c: (pr[c][0] * (dp[c][0] - delta[c]), pr[c][1] * (dp[c][1] - delta[c])) for c in chains}
        dsb = {c: ((ds[c][0] * scale).astype(BF16), (ds[c][1] * scale).astype(BF16)) for c in chains}
        dk_acc = [zero] * (nqb + 1)
        dv_acc = [zero] * (nqb + 1)
        db_acc = [[jnp.zeros((WINDOW, WINDOW), F32)] * 2 for _ in range(2)]
        dsk_acc = [jnp.zeros((1, 1), F32)] * 2
        dq = [zero] * nqb
        for m_, h in chains:
            c = (m_, h)
            db_acc[h] = [db_acc[h][0] + ds[c][0], db_acc[h][1] + ds[c][1]]
            dsk_acc[h] = dsk_acc[h] - jnp.sum(jnp.exp(s_ref[h:h + 1, 0:1] - lse_h[c]) * delta[c], axis=0, keepdims=True)
            dq[m_] = (dq[m_] + _dot(dsb[c][0], jnp.where(masks[h], kp[m_], jnp.zeros_like(kp[m_])))
                      + _dot(dsb[c][1], jnp.where(masks[h], kc[m_], jnp.zeros_like(kc[m_]))))
            dk_acc[m_] = dk_acc[m_] + _dot_tn(dsb[c][0], qh[c])
            dk_acc[m_ + 1] = dk_acc[m_ + 1] + _dot_tn(dsb[c][1], qh[c])
            dv_acc[m_] = dv_acc[m_] + _dot_tn(pr[c][0].astype(BF16), doh[c])
            dv_acc[m_ + 1] = dv_acc[m_ + 1] + _dot_tn(pr[c][1].astype(BF16), doh[c])
        for m_ in range(nqb):
            dq_ref[m_ * WINDOW:(m_ + 1) * WINDOW, :] = dq[m_]
        for h in range(2):
            db_ref[h, :, 0:WINDOW] += db_acc[h][0]
            db_ref[h, :, WINDOW:2 * WINDOW] += db_acc[h][1]
            dsk_ref[h:h + 1, :] += jnp.broadcast_to(dsk_acc[h], (1, LANES))
        offp = pl.multiple_of(jnp.maximum(n * nqb - 1, 0) * WINDOW, WINDOW)
        dk_ref[pl.ds(offp, WINDOW), :] += dk_acc[0]
        dv_ref[pl.ds(offp, WINDOW), :] += dv_acc[0]
        for m_ in range(nqb):
            off = pl.multiple_of(n * rows + m_ * WINDOW, WINDOW)
            dk_ref[pl.ds(off, WINDOW), :] += dk_acc[m_ + 1]
            dv_ref[pl.ds(off, WINDOW), :] += dv_acc[m_ + 1]

    blk = pl.BlockSpec((None, rows, LANES), lambda b, p, n: (b, n, p))
    seq = pl.BlockSpec((None, S, LANES), lambda b, p, n: (b, 0, 0))
    return pl.pallas_call(
        body, name=name, grid=(Bl, 3, S // rows),
        in_specs=_swa_specs(P_SWV // LANES, nqb) + [blk, blk, pl.BlockSpec((None, rows, LANES), lambda b, p, n: (b, n, do_blk0 + p))],
        out_specs=[blk, seq, seq, pl.BlockSpec((None, None, 2, WINDOW, 2 * WINDOW), lambda b, p, n: (b, p, 0, 0, 0)),
                   pl.BlockSpec((None, None, 2, LANES), lambda b, p, n: (b, p, 0, 0))],
        out_shape=[jax.ShapeDtypeStruct((Bl, S, 3 * LANES), F32), jax.ShapeDtypeStruct((Bl, S, LANES), F32), jax.ShapeDtypeStruct((Bl, S, LANES), F32),
                   jax.ShapeDtypeStruct((Bl, 3, 2, WINDOW, 2 * WINDOW), F32), jax.ShapeDtypeStruct((Bl, 3, 2, LANES), F32)],
        compiler_params=_cp("arbitrary", "arbitrary", "arbitrary"),
    )(qn3, kn3, kn3, proj3, proj3, bias, sinks, o3, lse3, do3)


CONV_ROWS = 64
CONV_LANES = 128


def _conv_strip(x_ref, h_ref, w, b, r0, cols, first_blk):
    x = x_ref[r0:r0 + CONV_ROWS, cols]
    if r0 == 0:
        rows = lax.broadcasted_iota(jnp.int32, x.shape, 0)
        h6 = jnp.where(first_blk, 0.0, h_ref[6:7, cols])
        h7 = jnp.where(first_blk, 0.0, h_ref[7:8, cols])
        x1 = jnp.where(rows == 0, h7, pltpu.roll(x, 1, 0))
        x2 = jnp.where(rows == 0, h6, jnp.where(rows == 1, h7, pltpu.roll(x, 2, 0)))
    else:
        x1 = x_ref[r0 - 1:r0 - 1 + CONV_ROWS, cols]
        x2 = x_ref[r0 - 2:r0 - 2 + CONV_ROWS, cols]
    return w[0:1] * x2 + w[1:2] * x1 + w[2:3] * x + b, x, x1, x2


FF_BLK = D_FF // 2


def _up_perm(a):
    q = FF_BLK
    return _cat([a[..., 0:q], a[..., 2 * q:3 * q], a[..., q:2 * q], a[..., 3 * q:4 * q]])


def conv_gate_fwd(up3, cw, cb, *, tm=256, name):
    Bl, S, _ = up3.shape
    tm = min(tm, S)
    W = 2 * FF_BLK

    def body(x_ref, h_ref, w_ref, b_ref, o_ref):
        first = pl.program_id(1) == 0

        def chunk(c, carry):
            cg = pl.ds(pl.multiple_of(c * CONV_LANES, CONV_LANES), CONV_LANES)
            cv = pl.ds(pl.multiple_of(FF_BLK + c * CONV_LANES, CONV_LANES), CONV_LANES)
            wg, wv, bg, bv = w_ref[:, cg], w_ref[:, cv], b_ref[:, cg], b_ref[:, cv]
            for r0 in range(0, tm, CONV_ROWS):
                ug = _conv_strip(x_ref, h_ref, wg, bg, r0, cg, first)[0]
                uv = _conv_strip(x_ref, h_ref, wv, bv, r0, cv, first)[0]
                o_ref[r0:r0 + CONV_ROWS, cg] = (ug * jax.nn.sigmoid(ug) * uv).astype(BF16)
            return carry

        lax.fori_loop(0, FF_BLK // CONV_LANES, chunk, 0)

    hb = tm // 8
    return pl.pallas_call(
        body, name=name, grid=(Bl, S // tm, 2),
        in_specs=[pl.BlockSpec((None, tm, W), lambda b, s, c: (b, s, c)),
                  pl.BlockSpec((None, 8, W), lambda b, s, c: (b, jnp.maximum(s * hb - 1, 0), c)),
                  pl.BlockSpec((3, W), lambda b, s, c: (0, c)), pl.BlockSpec((1, W), lambda b, s, c: (0, c))],
        out_specs=pl.BlockSpec((None, tm, FF_BLK), lambda b, s, c: (b, s, c)),
        out_shape=jax.ShapeDtypeStruct((Bl, S, D_FF), BF16),
        compiler_params=_cp("parallel", "parallel", "parallel"),
    )(up3, up3, cw, cb)


def conv_gate_bwd(up3, cw, cb, da3, *, tm=256, name):
    Bl, S, _ = up3.shape
    tm = min(tm, S)
    ns = S // tm
    W = 2 * FF_BLK

    def body(x_ref, h_ref, w_ref, b_ref, da_ref, dup_ref, dw_ref, nxt_ref, du_scr):
        b, s = pl.program_id(1), pl.program_id(2)
        seq_end = s == 0
        first = s == ns - 1

        @pl.when((b == 0) & seq_end)
        def _():
            dw_ref[...] = jnp.zeros_like(dw_ref)

        def du_chunk(c, carry):
            cg = pl.ds(pl.multiple_of(c * CONV_LANES, CONV_LANES), CONV_LANES)
            cv = pl.ds(pl.multiple_of(FF_BLK + c * CONV_LANES, CONV_LANES), CONV_LANES)
            wg, wv, bg, bv = w_ref[:, cg], w_ref[:, cv], b_ref[:, cg], b_ref[:, cv]
            acc_g = [jnp.zeros((1, CONV_LANES), F32)] * 4
            acc_v = [jnp.zeros((1, CONV_LANES), F32)] * 4
            for r0 in range(0, tm, CONV_ROWS):
                ug, xg, xg1, xg2 = _conv_strip(x_ref, h_ref, wg, bg, r0, cg, first)
                uv, xv, xv1, xv2 = _conv_strip(x_ref, h_ref, wv, bv, r0, cv, first)
                da = da_ref[r0:r0 + CONV_ROWS, cg].astype(F32)
                sg = jax.nn.sigmoid(ug)
                dug = da * uv * sg * (1.0 + ug * (1.0 - sg))
                duv = da * ug * sg
                du_scr[r0:r0 + CONV_ROWS, cg] = dug
                du_scr[r0:r0 + CONV_ROWS, cv] = duv
                col = lambda t: jnp.sum(t, axis=0, keepdims=True)
                acc_g = [acc_g[0] + col(dug * xg2), acc_g[1] + col(dug * xg1), acc_g[2] + col(dug * xg), acc_g[3] + col(dug)]
                acc_v = [acc_v[0] + col(duv * xv2), acc_v[1] + col(duv * xv1), acc_v[2] + col(duv * xv), acc_v[3] + col(duv)]
            for t in range(4):
                dw_ref[t:t + 1, cg] += acc_g[t]
                dw_ref[t:t + 1, cv] += acc_v[t]
            return carry

        lax.fori_loop(0, FF_BLK // CONV_LANES, du_chunk, 0)
        du_scr[tm:tm + 8, :] = jnp.where(seq_end, 0.0, nxt_ref[...])

        def dup_chunk(c, carry):
            cols = pl.ds(pl.multiple_of(c * CONV_LANES, CONV_LANES), CONV_LANES)
            w = w_ref[:, cols]
            for r0 in range(0, tm, CONV_ROWS):
                d0 = du_scr[r0:r0 + CONV_ROWS, cols]
                d1 = du_scr[r0 + 1:r0 + 1 + CONV_ROWS, cols]
                d2 = du_scr[r0 + 2:r0 + 2 + CONV_ROWS, cols]
                dup_ref[r0:r0 + CONV_ROWS, cols] = (w[2:3] * d0 + w[1:2] * d1 + w[0:1] * d2).astype(BF16)
            return carry

        lax.fori_loop(0, W // CONV_LANES, dup_chunk, 0)
        nxt_ref[...] = du_scr[0:8, :]

    hb = tm // 8
    rb = lambda s: ns - 1 - s
    return pl.pallas_call(
        body, name=name, grid=(2, Bl, ns),
        in_specs=[pl.BlockSpec((None, tm, W), lambda c, b, s: (b, rb(s), c)),
                  pl.BlockSpec((None, 8, W), lambda c, b, s: (b, jnp.maximum(rb(s) * hb - 1, 0), c)),
                  pl.BlockSpec((3, W), lambda c, b, s: (0, c)), pl.BlockSpec((1, W), lambda c, b, s: (0, c)),
                  pl.BlockSpec((None, tm, FF_BLK), lambda c, b, s: (b, rb(s), c))],
        out_specs=[pl.BlockSpec((None, tm, W), lambda c, b, s: (b, rb(s), c)), pl.BlockSpec((8, W), lambda c, b, s: (0, c))],
        out_shape=[jax.ShapeDtypeStruct((Bl, S, 2 * D_FF), BF16), jax.ShapeDtypeStruct((8, 2 * D_FF), F32)],
        scratch_shapes=[pltpu.VMEM((8, W), F32), pltpu.VMEM((tm + 8, W), F32)],
        compiler_params=_cp("arbitrary", "arbitrary", "arbitrary"),
    )(up3, up3, cw, cb, da3)


def cast_layer(w3, l, *, name):
    _, R, C = w3.shape
    tr = _tile(R, 512, 16)

    def body(w_ref, o_ref):
        o_ref[...] = w_ref[...].astype(BF16)

    return pl.pallas_call(
        body, name=name, grid=(R // tr,), in_specs=[pl.BlockSpec((None, tr, C), lambda i: (l, i, 0))],
        out_specs=pl.BlockSpec((tr, C), lambda i: (i, 0)), out_shape=jax.ShapeDtypeStruct((R, C), BF16),
        compiler_params=_cp("parallel"),
    )(w3)


def gate_bwd(dx3, y3, gate, *, tm=512, name):
    Bl, S, D = dx3.shape
    tm = min(tm, S)

    def body(dx_ref, y_ref, g_ref, o_ref, dg_ref):
        @pl.when(pl.program_id(1) == 0)
        def _():
            dg_ref[...] = jnp.zeros_like(dg_ref)

        dx = dx_ref[...]
        dg_ref[...] += jnp.sum(dx * y_ref[...], axis=0, keepdims=True)
        o_ref[...] = (dx * g_ref[...]).astype(BF16)

    blk = pl.BlockSpec((None, tm, D), lambda b, s: (b, s, 0))
    vec = pl.BlockSpec((None, 1, D), lambda b, s: (b, 0, 0))
    return pl.pallas_call(
        body, name=name, grid=(Bl, S // tm), in_specs=[blk, blk, vec], out_specs=[blk, vec],
        out_shape=[jax.ShapeDtypeStruct((Bl, S, D), BF16), jax.ShapeDtypeStruct((Bl, 1, D), F32)],
        compiler_params=_cp("parallel", "arbitrary"),
    )(dx3, y3, gate)


def loss_grad(y3, t3, *, tm=512, name):
    Bl, S, D = y3.shape
    tm = min(tm, S)
    last = (Bl - 1, S // tm - 1)

    def body(y_ref, t_ref, dy_ref, l_ref, acc_ref):
        b, s = pl.program_id(0), pl.program_id(1)

        @pl.when((b == 0) & (s == 0))
        def _():
            acc_ref[...] = jnp.zeros_like(acc_ref)

        e = y_ref[...] - t_ref[...]
        dy_ref[...] = e * (1.0 / D)
        acc_ref[...] += jnp.sum(e * e, axis=0, keepdims=True)

        @pl.when((b == last[0]) & (s == last[1]))
        def _():
            l_ref[...] = jnp.broadcast_to(jnp.sum(acc_ref[...], axis=1, keepdims=True) * (0.5 / D), (1, LANES))

    blk = pl.BlockSpec((None, tm, D), lambda b, s: (b, s, 0))
    return pl.pallas_call(
        body, name=name, grid=(Bl, S // tm), in_specs=[blk, blk],
        out_specs=[blk, pl.BlockSpec((1, LANES), lambda b, s: (0, 0))],
        out_shape=[jax.ShapeDtypeStruct((Bl, S, D), F32), jax.ShapeDtypeStruct((1, LANES), F32)],
        scratch_shapes=[pltpu.VMEM((1, D), F32)], compiler_params=_cp("arbitrary", "arbitrary"),
    )(y3, t3)


def adamw(w, g, m, v, *, name):
    L, R, C = w.shape
    tr = _tile(R, 512, 8)

    def body(w_ref, g_ref, m_ref, v_ref, d_ref, m2_ref, v2_ref):
        d_ref[...], m2_ref[...], v2_ref[...] = _adam_update(w_ref[...], g_ref[...], m_ref[...], v_ref[...])

    blk = pl.BlockSpec((None, tr, C), lambda l, i: (l, i, 0))
    shp = jax.ShapeDtypeStruct((L, R, C), F32)
    return pl.pallas_call(
        body, name=name, grid=(L, R // tr), in_specs=[blk] * 4, out_specs=[blk] * 3, out_shape=[shp] * 3,
        compiler_params=_cp("parallel", "parallel"),
    )(w, g, m, v)


def _adam_update(w, g, m, v):
    c1 = 1.0 / (1.0 - ADAM_B1 ** ADAM_STEP)
    c2 = 1.0 / (1.0 - ADAM_B2 ** ADAM_STEP)
    m2 = ADAM_B1 * m + (1.0 - ADAM_B1) * g
    v2 = ADAM_B2 * v + (1.0 - ADAM_B2) * (g * g)
    return -ADAM_LR * ((m2 * c1) / (jnp.sqrt(v2 * c2) + ADAM_EPS) + ADAM_WD * w), m2, v2


def adamw_small(ws, gs, ms, vs, *, name):
    na = len(ws)

    def body(*refs):
        w_r, g_r, m_r, v_r = (refs[i * na:(i + 1) * na] for i in range(4))
        d_r, m2_r, v2_r = (refs[(4 + i) * na:(5 + i) * na] for i in range(3))
        for a in range(na):
            d_r[a][...], m2_r[a][...], v2_r[a][...] = _adam_update(w_r[a][...], g_r[a][...], m_r[a][...], v_r[a][...])

    vm = pl.BlockSpec(memory_space=pltpu.VMEM)
    shp = [jax.ShapeDtypeStruct(w.shape, F32) for w in ws]
    out = pl.pallas_call(body, name=name, in_specs=[vm] * (4 * na), out_specs=[vm] * (3 * na), out_shape=shp * 3)(*ws, *gs, *ms, *vs)
    return out[:na], out[na:2 * na], out[2 * na:]


def sum_small(xs, *, name):
    na = len(xs)

    def body(*refs):
        for x_ref, o_ref in zip(refs[:na], refs[na:]):
            acc = x_ref[0]
            for k in range(1, x_ref.shape[0]):
                acc = acc + x_ref[k]
            o_ref[...] = acc

    vm = pl.BlockSpec(memory_space=pltpu.VMEM)
    return pl.pallas_call(body, name=name, in_specs=[vm] * na, out_specs=[vm] * na,
                          out_shape=[jax.ShapeDtypeStruct(x.shape[1:], x.dtype) for x in xs])(*xs)


def pair_add_half(g4, recv, c_arr, *, tr=512, name):
    _, R, C = g4.shape
    H = R // 2
    tr = _tile(H, tr, 16)
    nb = H // tr

    def body(c_ref, g_ref, r_ref, o_ref):
        o_ref[...] = (g_ref[...].astype(F32) + r_ref[...].astype(F32)).astype(BF16)

    grid_spec = pltpu.PrefetchScalarGridSpec(
        num_scalar_prefetch=1, grid=(4, nb),
        in_specs=[pl.BlockSpec((None, tr, C), lambda k, i, c_ref: (k, c_ref[0] * nb + i, 0)),
                  pl.BlockSpec((None, tr, C), lambda k, i, c_ref: (k, i, 0))],
        out_specs=pl.BlockSpec((None, tr, C), lambda k, i, c_ref: (k, i, 0)),
    )
    return pl.pallas_call(
        body, name=name, grid_spec=grid_spec, out_shape=jax.ShapeDtypeStruct((4, H, C), BF16),
        compiler_params=_cp("parallel", "parallel"),
    )(c_arr, g4, recv)


def chip_sum_into(landed, pair, sel, *, tr=512, name):
    _, H, C = landed.shape
    tr = _tile(H, tr, 16)
    nb = H // tr

    def body(s_ref, l0, l1, l2, l3, p_ref, o_ref):
        own = p_ref[...].astype(F32)
        acc = None
        for k, l_ref in enumerate((l0, l1, l2, l3)):
            part = jnp.where(s_ref[0] == k, own, l_ref[...].astype(F32))
            acc = part if acc is None else acc + part
        o_ref[...] = acc

    def slot(k):
        return pl.BlockSpec((None, tr, C), lambda i, s: (jnp.where(s[0] == k, (k + 1) % 4, k), i, 0))

    grid_spec = pltpu.PrefetchScalarGridSpec(
        num_scalar_prefetch=1, grid=(nb,),
        in_specs=[slot(0), slot(1), slot(2), slot(3), pl.BlockSpec((None, tr, C), lambda i, s: (s[0], i, 0))],
        out_specs=pl.BlockSpec((tr, C), lambda i, s: (s[1] * nb + i, 0)),
    )
    return pl.pallas_call(
        body, name=name, grid_spec=grid_spec, out_shape=jax.ShapeDtypeStruct((2 * H, C), F32), compiler_params=_cp("parallel"),
    )(sel, landed, landed, landed, landed, pair)


def mods_matmul(c_all, w_ada, b_ada_cols, *, tn=512, name):
    L, D, E = w_ada.shape
    nb = c_all.shape[0]
    tn = _tile(E, tn)

    def body(c_ref, w_ref, b_ref, o_ref):
        c = c_ref[...]
        a = c * jax.nn.sigmoid(c)
        o_ref[...] = jnp.dot(a, w_ref[...], preferred_element_type=F32, precision=lax.Precision.HIGHEST) + b_ref[...]

    return pl.pallas_call(
        body, name=name, grid=(L, E // tn),
        in_specs=[pl.BlockSpec((nb, D), lambda l, j: (0, 0)), pl.BlockSpec((None, D, tn), lambda l, j: (l, 0, j)),
                  pl.BlockSpec((None, 1, tn), lambda l, j: (l, 0, j))],
        out_specs=pl.BlockSpec((None, nb, tn), lambda l, j: (l, 0, j)),
        out_shape=jax.ShapeDtypeStruct((L, nb, E), F32), compiler_params=_cp("parallel", "parallel"),
    )(c_all, w_ada, b_ada_cols)


def ada_grad(c_all, dmods, *, tn=512, name):
    L, nb, E = dmods.shape
    D = c_all.shape[1]
    tn = _tile(E, tn)

    def body(c_ref, d_ref, o_ref):
        c = c_ref[...]
        a = c * jax.nn.sigmoid(c)
        o_ref[...] = lax.dot_general(a, d_ref[...], (((0,), (0,)), ((), ())), preferred_element_type=F32, precision=lax.Precision.HIGHEST)

    return pl.pallas_call(
        body, name=name, grid=(L, E // tn),
        in_specs=[pl.BlockSpec((nb, D), lambda l, j: (0, 0)), pl.BlockSpec((None, nb, tn), lambda l, j: (l, 0, j))],
        out_specs=pl.BlockSpec((None, D, tn), lambda l, j: (l, 0, j)),
        out_shape=jax.ShapeDtypeStruct((L, D, E), F32), compiler_params=_cp("parallel", "parallel"),
    )(c_all, dmods)


HBM = pl.BlockSpec(memory_space=pltpu.HBM)


def _me():
    return lax.axis_index("x"), lax.axis_index("y"), lax.axis_index("c")


def _flip(v, bit):
    return 1 - v if bit else v


def allgather8(xs, *, name):
    na = len(xs)

    def body(*refs):
        x_refs, out_refs = refs[:na], refs[na:2 * na]
        send_sems, recv_sems = refs[2 * na], refs[2 * na + 1]
        x, y, c = _me()
        me = 4 * x + 2 * y + c
        for x_ref, out_ref in zip(x_refs, out_refs):
            out_ref[me] = x_ref[...]
        sends = []
        for a, (x_ref, out_ref) in enumerate(zip(x_refs, out_refs)):
            for k in range(1, 8):
                peer = (_flip(x, k & 4), _flip(y, k & 2), _flip(c, k & 1))
                cp = pltpu.make_async_remote_copy(src_ref=x_ref, dst_ref=out_ref.at[me], send_sem=send_sems.at[a, k - 1],
                                                  recv_sem=recv_sems.at[a, k - 1], device_id=peer, device_id_type=MESH)
                cp.start()
                sends.append(cp)
        for a, (x_ref, out_ref) in enumerate(zip(x_refs, out_refs)):
            for k in range(1, 8):
                peer = (_flip(x, k & 4), _flip(y, k & 2), _flip(c, k & 1))
                src = 4 * peer[0] + 2 * peer[1] + peer[2]
                pltpu.make_async_remote_copy(src_ref=x_ref, dst_ref=out_ref.at[src], send_sem=send_sems.at[a, k - 1],
                                             recv_sem=recv_sems.at[a, k - 1], device_id=peer, device_id_type=MESH).wait_recv()
        for cp in sends:
            cp.wait_send()

    vm = pl.BlockSpec(memory_space=pltpu.VMEM)
    return pl.pallas_call(
        body, name=name, in_specs=[vm] * na, out_specs=[vm] * na,
        out_shape=[jax.ShapeDtypeStruct((8,) + a.shape, a.dtype) for a in xs],
        scratch_shapes=[pltpu.SemaphoreType.DMA((na, 7)), pltpu.SemaphoreType.DMA((na, 7))],
    )(*xs)


class _Plan:
    def __init__(self, ins, out_shapes, ncopies, copies, aliased=False):
        self.ins, self.out_shapes, self.ncopies, self.copies, self.aliased = list(ins), list(out_shapes), ncopies, copies, aliased

    def start(self, in_refs, out_refs, send_sems, recv_sems):
        sends, _ = self.copies(in_refs, out_refs, send_sems, recv_sems)
        for cp in sends:
            cp.start()

    def finish(self, in_refs, out_refs, send_sems, recv_sems):
        sends, recvs = self.copies(in_refs, out_refs, send_sems, recv_sems)
        for cp in recvs:
            cp.wait_recv()
        for cp in sends:
            cp.wait_send()


def _rcopy(src, dst, send_sems, recv_sems, idx, dev):
    return pltpu.make_async_remote_copy(src_ref=src, dst_ref=dst, send_sem=send_sems.at[idx], recv_sem=recv_sems.at[idx],
                                        device_id=dev, device_id_type=MESH)


def _other_chips(x, y):
    return [(_flip(x, k & 2), _flip(y, k & 1)) for k in range(1, 4)]


def plan_gather_ici(ws):
    def copies(in_refs, out_refs, ss, rs):
        x, y, c = _me()
        j = 2 * x + y
        sends, recvs = [], []
        for a, (x_ref, out_ref) in enumerate(zip(in_refs, out_refs)):
            H = x_ref.shape[0] // 2
            for k, (px, py) in enumerate(_other_chips(x, y)):
                sends.append(_rcopy(x_ref.at[pl.ds(c * H, H)], out_ref.at[j, pl.ds(c * H, H)], ss, rs, 3 * a + k, (px, py, c)))
                slot = out_ref.at[2 * px + py, pl.ds(c * H, H)]
                recvs.append(_rcopy(slot, slot, ss, rs, 3 * a + k, (px, py, c)))
        return sends, recvs

    return _Plan(ws, [jax.ShapeDtypeStruct((4,) + w.shape, w.dtype) for w in ws], 3 * len(ws), copies)


def plan_gather_d2d(w4s):
    def copies(in_refs, out_refs, ss, rs):
        x, y, c = _me()
        sends, recvs = [], []
        for a, out_ref in enumerate(out_refs):
            H = out_ref.shape[1] // 2
            for k, (px, py) in enumerate(_other_chips(x, y)):
                mine = out_ref.at[2 * px + py, pl.ds(c * H, H)]
                theirs = out_ref.at[2 * px + py, pl.ds((1 - c) * H, H)]
                sends.append(_rcopy(mine, mine, ss, rs, 3 * a + k, (x, y, 1 - c)))
                recvs.append(_rcopy(theirs, theirs, ss, rs, 3 * a + k, (x, y, 1 - c)))
        return sends, recvs

    return _Plan(w4s, [jax.ShapeDtypeStruct(w.shape, w.dtype) for w in w4s], 3 * len(w4s), copies, aliased=True)


def plan_swap_halves(gs):
    def copies(in_refs, out_refs, ss, rs):
        x, y, c = _me()
        sends, recvs = [], []
        for a, (g_ref, out_ref) in enumerate(zip(in_refs, out_refs)):
            H = g_ref.shape[1] // 2
            for k in range(4):
                sends.append(_rcopy(g_ref.at[k, pl.ds((1 - c) * H, H)], out_ref.at[k], ss, rs, 4 * a + k, (x, y, 1 - c)))
                recvs.append(_rcopy(g_ref.at[k, pl.ds(c * H, H)], out_ref.at[k], ss, rs, 4 * a + k, (x, y, 1 - c)))
        return sends, recvs

    return _Plan(gs, [jax.ShapeDtypeStruct((4, g.shape[1] // 2, g.shape[2]), g.dtype) for g in gs], 4 * len(gs), copies)


def plan_scatter_ici(ps):
    def copies(in_refs, out_refs, ss, rs):
        x, y, c = _me()
        j = 2 * x + y
        sends, recvs = [], []
        for a, (p_ref, out_ref) in enumerate(zip(in_refs, out_refs)):
            for k, (px, py) in enumerate(_other_chips(x, y)):
                sends.append(_rcopy(p_ref.at[2 * px + py], out_ref.at[j], ss, rs, 3 * a + k, (px, py, c)))
                slot = out_ref.at[2 * px + py]
                recvs.append(_rcopy(slot, slot, ss, rs, 3 * a + k, (px, py, c)))
        return sends, recvs

    return _Plan(ps, [jax.ShapeDtypeStruct(p.shape, p.dtype) for p in ps], 3 * len(ps), copies)


def plan_join_halves(fulls):
    def copies(in_refs, out_refs, ss, rs):
        x, y, c = _me()
        sends, recvs = [], []
        for a, out_ref in enumerate(out_refs):
            H = out_ref.shape[0] // 2
            mine, theirs = out_ref.at[pl.ds(c * H, H)], out_ref.at[pl.ds((1 - c) * H, H)]
            sends.append(_rcopy(mine, mine, ss, rs, a, (x, y, 1 - c)))
            recvs.append(_rcopy(theirs, theirs, ss, rs, a, (x, y, 1 - c)))
        return sends, recvs

    return _Plan(fulls, [jax.ShapeDtypeStruct(f.shape, f.dtype) for f in fulls], len(fulls), copies, aliased=True)


def call_with_plans(body, plans, *, grid, in_specs, out_specs, out_shape, scratch_shapes, args, sem, name):
    plans = list(plans or [])
    n_in, n_out, n_scr = len(in_specs), len(out_specs), len(scratch_shapes)
    c_in = [len(p.ins) for p in plans]
    c_out = [len(p.out_shapes) for p in plans]
    steps = math.prod(grid) if grid else 1

    def wrapped(*refs):
        pos = 0

        def take(n):
            nonlocal pos
            out = refs[pos:pos + n]
            pos += n
            return out

        ins = take(n_in)
        cins = [take(n) for n in c_in]
        outs = take(n_out)
        couts = [take(n) for n in c_out]
        scr = take(n_scr)
        sems = [take(2) for _ in plans]
        def start_all():
            for p, ci, co, (ss, rs) in zip(plans, cins, couts, sems):
                p.start(ci, co, ss, rs)

        def finish_all():
            for p, ci, co, (ss, rs) in zip(plans, cins, couts, sems):
                p.finish(ci, co, ss, rs)

        if plans and grid:
            idx = 0
            for ax, g in enumerate(grid):
                idx = idx * g + pl.program_id(ax)
            pl.when(idx == 0)(start_all)
        elif plans:
            start_all()
        if body is not None:
            body(*ins, *outs, *scr)
        if plans and grid:
            pl.when(idx == steps - 1)(finish_all)
        elif plans:
            finish_all()

    aliases = {}
    i_pos, o_pos = n_in, n_out
    for p, ni, no in zip(plans, c_in, c_out):
        if p.aliased:
            aliases.update({i_pos + t: o_pos + t for t in range(ni)})
        i_pos += ni
        o_pos += no
    kwargs = dict(grid=grid) if grid else {}
    if aliases:
        kwargs["input_output_aliases"] = aliases
    res = pl.pallas_call(
        wrapped, name=name, in_specs=list(in_specs) + [HBM] * sum(c_in), out_specs=list(out_specs) + [HBM] * sum(c_out),
        out_shape=list(out_shape) + [s for p in plans for s in p.out_shapes],
        scratch_shapes=list(scratch_shapes) + [pltpu.SemaphoreType.DMA((p.ncopies,)) for p in plans for _ in range(2)],
        compiler_params=_cp(*sem) if grid else pltpu.CompilerParams(vmem_limit_bytes=VMEM_LIMIT), **kwargs,
    )(*args, *[a for p in plans for a in p.ins])
    res = list(res)
    comp, rest = res[:n_out], res[n_out:]
    pouts = []
    for no in c_out:
        pouts.append(rest[:no])
        rest = rest[no:]
    return comp, pouts


def run_plans(plans, *, name):
    return call_with_plans(None, plans, grid=(), in_specs=[], out_specs=[], out_shape=[], scratch_shapes=[], args=[], sem=(), name=name)[1]


def _cat(parts, axis=-1):
    return jnp.concatenate(parts, axis=axis)


def _pairs_of_heads(a, axis, inverse=False):
    lead, tail = a.shape[:axis], a.shape[axis + 1:]
    split = (3, 2) if inverse else (2, 3)
    a = a.reshape(lead + split + (HEAD,) + tail)
    return jnp.swapaxes(a, axis, axis + 1).reshape(lead + (6 * HEAD,) + tail)


def _prep_w_in(w):
    z = lambda n: jnp.zeros((w.shape[0], n), w.dtype)
    return _cat([w[:, 0:1152], z(64), w[:, 1152:1184], z(32), _pairs_of_heads(w[:, 1184:1568], 1), w[:, 1568:1824]])


def _unprep_w_in(g):
    return _cat([g[:, 0:1152], g[:, 1216:1248], _pairs_of_heads(g[:, P_SWQ:P_SWK], 1, inverse=True), g[:, P_SWK:P_END]])


def _prep_w_uq(w):
    r = w.shape[0]
    return jnp.pad(w.reshape(r, 6, MLA_QK), ((0, 0), (0, 0), (0, LANES - MLA_QK))).reshape(r, 6 * LANES)


def _unprep_w_uq(g):
    r = g.shape[0]
    return g.reshape(r, 6, LANES)[:, :, :MLA_QK].reshape(r, 6 * MLA_QK)


def _prep_w_ukv(w):
    r = w.shape[0]
    w3 = w.reshape(r, 6, LANES)
    k = jnp.pad(w3[:, :, :HEAD], ((0, 0), (0, 0), (0, LANES - HEAD))).reshape(r, 6 * LANES)
    return _cat([k, w3[:, :, HEAD:].reshape(r, 6 * HEAD)])


def _unprep_w_ukv(g):
    r = g.shape[0]
    k = g[:, :6 * LANES].reshape(r, 6, LANES)[:, :, :HEAD]
    return _cat([k, g[:, 6 * LANES:].reshape(r, 6, HEAD)], axis=2).reshape(r, 6 * LANES)


def _prep_w_out(w):
    return _cat([w[0:640], _pairs_of_heads(w[640:], 0)], axis=0)


def _unprep_w_out(g):
    return _cat([g[0:640], _pairs_of_heads(g[640:], 0, inverse=True)], axis=0)


def _rope_tables(positions):
    half = 16
    inv_freq = jnp.power(ROPE_THETA, -jnp.arange(half, dtype=F32) / half)
    ang = positions.astype(F32)[..., None] * inv_freq
    cos, sin = jnp.cos(ang), jnp.sin(ang)
    z = lambda n: jnp.zeros(ang.shape[:-1] + (n,), F32)
    return (_cat([jnp.ones(ang.shape[:-1] + (HEAD,), F32), cos, cos, z(32)]), _cat([z(HEAD), -sin, z(16), z(32)]), _cat([z(HEAD), z(16), sin, z(32)]))


def _small_params(p):
    pad96 = lambda g: _cat([g, jnp.zeros((32,), F32)]).reshape(1, LANES)
    two = lambda g: _cat([g, g]).reshape(1, LANES)
    sinks = jnp.broadcast_to(p["sw_sinks"].reshape(2, 3).T[:, :, None], (3, 2, LANES))
    return dict(n1=p["norm1_g"].reshape(1, -1), n2=p["norm2_g"].reshape(1, -1), cq_g=p["mla_cq_g"].reshape(1, -1),
                ckv_g=p["mla_ckv_g"].reshape(1, -1), qn_g=pad96(p["mla_qn_g"]), kn_g=pad96(p["mla_kn_g"]),
                swq_g=two(p["sw_qn_g"]), swk_g=two(p["sw_kn_g"]), sinks=sinks, conv_b=_up_perm(p["conv_b"]).reshape(1, -1))


class _NoFlow:
    def plans(self, tag):
        return []

    def done(self, tag, outs):
        pass

    def add(self, key, g):
        pass


def _layer_fwd(x3, md, W, tabs, bias, tag, flow=_NoFlow()):
    Bl, S, D = x3.shape
    T = Bl * S
    n = lambda s: f"{s}_{tag}"
    two = lambda a: a.reshape(T, a.shape[-1])
    three = lambda a: a.reshape(Bl, S, a.shape[-1])
    h = rms_fwd(x3, 0, D, W["n1"], md["scale1"], md["shift1"], name=n("norm1"))
    proj = three(matmul(two(h), W["w_in"], tn=1920, name=n("in_proj")))
    (o_a, rt_a), got = sb_attn_fwd(proj, plans=flow.plans(n("sb_fwd")), name=n("sb_fwd"))
    flow.done(n("sb_fwd"), got)
    cqn = rms_fwd(proj, P_CQ // 256, 256, W["cq_g"], name=n("cq_norm"))
    ckvn = rms_fwd(proj, P_CKV // LANES, LANES, W["ckv_g"], name=n("ckv_norm"))
    qb = three(matmul(two(cqn), W["w_uq"], tm=1024, tn=768, name=n("uq")))
    kvb = three(matmul(two(ckvn), W["w_ukv"], tm=1024, tn=1152, name=n("ukv")))
    q_m = rope_norm_fwd(qb, 6, W["qn_g"], tabs, name=n("q_rope"))
    k_m = rope_norm_fwd(kvb, 6, W["kn_g"], tabs, (proj, P_SLAB // LANES), name=n("k_rope"))
    (o_b, lse_b), got = mla_attn_fwd(q_m, k_m, kvb, 6, plans=flow.plans(n("mla_fwd")), name=n("mla_fwd"))
    flow.done(n("mla_fwd"), got)
    q_c = pair_rms_fwd(proj, P_SWQ // LANES, 3, W["swq_g"], name=n("swq_norm"))
    k_c = pair_rms_fwd(proj, P_SWK // LANES, 1, W["swk_g"], name=n("swk_norm"))
    (o_c, lse_c), got = swa_attn_fwd(q_c, k_c, proj, bias, W["sinks"], plans=flow.plans(n("swa_fwd")), name=n("swa_fwd"))
    flow.done(n("swa_fwd"), got)
    mix = _cat([o_a, o_b, o_c]).astype(BF16)
    att, x1 = matmul_res(two(mix), W["w_out"], two(x3), md["gate1"], S, name=n("out_proj"))
    x1 = three(x1)
    h2 = rms_fwd(x1, 0, D, W["n2"], md["scale2"], md["shift2"], name=n("norm2"))
    up = three(matmul(two(h2), W["w_up"], tm=1024, tn=1408, name=n("up_proj")))
    a = conv_gate_fwd(up, W["conv_w"], W["conv_b"], name=n("conv_gate"))
    yd, x2 = matmul_res(two(a), W["w_down"], two(x1), md["gate2"], S, name=n("down_proj"))
    saved = dict(x=x3, h=h, proj=proj, rt_a=rt_a, cqn=cqn, ckvn=ckvn, qb=qb, kvb=kvb, q_m=q_m, k_m=k_m, o_b=o_b, lse_b=lse_b,
                 q_c=q_c, k_c=k_c, o_c=o_c, lse_c=lse_c, mix=mix, att=three(att), x1=x1, h2=h2, up=up, a=a, yd=three(yd))
    return three(x2), saved


def _layer_bwd(dx2, sv, md, W, tabs, bias, tag, flow=_NoFlow()):
    Bl, S, D = dx2.shape
    T = Bl * S
    n = lambda s: f"{s}_{tag}"
    two = lambda a: a.reshape(T, a.shape[-1])
    three = lambda a: a.reshape(Bl, S, a.shape[-1])
    g = {}
    dyb, dgate2 = gate_bwd(dx2, sv["yd"], md["gate2"], name=n("gate2_bwd"))
    da = three(matmul(two(dyb), W["w_down"], tb=True, tm=1024, tn=1408, name=n("down_dx")))
    g["w_down"] = matmul(two(sv["a"]), two(dyb), ta=True, tm=256, tn=1024, out_dtype=BF16, name=n("down_dw"))
    dup, dcw = conv_gate_bwd(sv["up"], W["conv_w"], W["conv_b"], da, name=n("conv_gate_bwd"))
    dh2 = three(matmul(two(dup), W["w_up"], tb=True, tn=1024, name=n("up_dx")))
    g["w_up"] = matmul(two(sv["h2"]), two(dup), ta=True, tn=1408, out_dtype=BF16, name=n("up_dw"))
    dx1, dn2, dsc2, dsh2 = rms_bwd(sv["x1"], 0, D, dh2, W["n2"], md["scale2"], dx2, name=n("norm2_bwd"))
    dmo, dgate1 = gate_bwd(dx1, sv["att"], md["gate1"], name=n("gate1_bwd"))
    dmix = three(matmul(two(dmo), W["w_out"], tb=True, tn=1024, out_dtype=BF16, name=n("out_dx")))
    g["w_out"] = matmul(two(sv["mix"]), two(dmo), ta=True, tn=1024, out_dtype=BF16, name=n("out_dw"))
    proj = sv["proj"]
    for k in ("w_down", "w_up", "w_out"):
        flow.add((tag, k), g[k])
    (dq_a, dk_a, dv_a), got = sb_attn_bwd(proj, sv["rt_a"], dmix, do_blk0=0, plans=flow.plans(n("sb_bwd")), name=n("sb_bwd"))
    flow.done(n("sb_bwd"), got)
    dq_m, dk_m, dv_b = mla_attn_bwd(sv["q_m"], sv["k_m"], sv["kvb"], 6, sv["o_b"], sv["lse_b"], dmix, do_blk0=2, name=n("mla_bwd"))
    dqb, dqn = rope_norm_bwd(sv["qb"], 6, dq_m, W["qn_g"], tabs, name=n("q_rope_bwd"))
    dkn_x, dkn, dslab = rope_norm_bwd(sv["kvb"], 6, dk_m, W["kn_g"], tabs, (proj, P_SLAB // LANES), name=n("k_rope_bwd"))
    dkvb = _cat([dkn_x, dv_b]).astype(BF16)
    dckvn = three(matmul(two(dkvb), W["w_ukv"], tb=True, tm=1024, name=n("ukv_dx")))
    g["w_ukv"] = matmul(two(sv["ckvn"]), two(dkvb), ta=True, tn=1152, out_dtype=BF16, name=n("ukv_dw"))
    dcqn = three(matmul(two(dqb), W["w_uq"], tb=True, tm=1024, name=n("uq_dx")))
    g["w_uq"] = matmul(two(sv["cqn"]), two(dqb), ta=True, tn=768, out_dtype=BF16, name=n("uq_dw"))
    dcq, dcq_g = rms_bwd(proj, P_CQ // 256, 256, dcqn, W["cq_g"], name=n("cq_norm_bwd"))
    dckv, dckv_g = rms_bwd(proj, P_CKV // LANES, LANES, dckvn, W["ckv_g"], name=n("ckv_norm_bwd"))
    dq_c, dk_c, dv_c, dbias, dsink = swa_attn_bwd(sv["q_c"], sv["k_c"], proj, bias, W["sinks"], sv["o_c"], sv["lse_c"], dmix, do_blk0=5, name=n("swa_bwd"))
    dswq, dswq_g = pair_rms_bwd(proj, P_SWQ // LANES, 3, dq_c, W["swq_g"], name=n("swq_norm_bwd"))
    dswk, dswk_g = pair_rms_bwd(proj, P_SWK // LANES, 1, dk_c, W["swk_g"], name=n("swk_norm_bwd"))
    dproj = _cat([dq_a, dk_a, dv_a, dcq, dckv, dslab, dswq, dswk, dv_c]).astype(BF16)
    dh = three(matmul(two(dproj), W["w_in"], tb=True, tn=1024, name=n("in_dx")))
    g["w_in"] = matmul(two(sv["h"]), two(dproj), ta=True, tn=1920, tk=2048, out_dtype=BF16, name=n("in_dw"))
    dx, dn1, dsc1, dsh1 = rms_bwd(sv["x"], 0, D, dh, W["n1"], md["scale1"], dx1, name=n("norm1_bwd"))
    small = dict(n1=dn1, n2=dn2, cq_g=dcq_g, ckv_g=dckv_g, qn_g=dqn, kn_g=dkn, swq_g=dswq_g, swk_g=dswk_g, conv=dcw)
    dmods = _cat([dsh1, dsc1, dgate1, dsh2, dsc2, dgate2]).reshape(Bl, 6 * D)
    for k in ("w_ukv", "w_uq", "w_in"):
        flow.add((tag, k), g[k])
    return dx, g, small, dmods, dbias, dsink


BIG = ("w_in", "w_uq", "w_ukv", "w_out", "w_up", "w_down")
ROW_SHARDED = ("w_out", "w_down")
PREP = dict(w_in=_prep_w_in, w_uq=_prep_w_uq, w_ukv=_prep_w_ukv, w_out=_prep_w_out, w_up=_up_perm, w_down=lambda w: w)
UNPREP = dict(w_in=_unprep_w_in, w_uq=_unprep_w_uq, w_ukv=_unprep_w_ukv, w_out=_unprep_w_out, w_up=_up_perm, w_down=lambda w: w)
NCHIPS = 4


def _local_step(x, target, positions, mods, Wl, rel_flat, fwd_flow=_NoFlow(), bwd_flow=_NoFlow()):
    Bl, S, D = x.shape
    L = len(Wl)
    tabs = _rope_tables(positions)
    bucket = _bucket_table()
    bias = swa_bias(rel_flat, bucket, name="swa_bias")
    mds = []
    for l in range(L):
        parts = [mods[l, :, D * k:D * (k + 1)].reshape(Bl, 1, D) for k in range(6)]
        mds.append(dict(zip(("shift1", "scale1", "gate1", "shift2", "scale2", "gate2"), parts)))
    saved = []
    h = x
    for l in range(L):
        h, sv = _layer_fwd(h, mds[l], Wl[l], tabs, bias, f"l{l}", fwd_flow)
        saved.append(sv)
    dy, loss = loss_grad(h, target, name="loss")
    grads, smalls, dmods, dbiases, dsinks = [None] * L, [None] * L, [None] * L, [None] * L, [None] * L
    for l in reversed(range(L)):
        dy, grads[l], smalls[l], dmods[l], dbiases[l], dsinks[l] = _layer_bwd(dy, saved[l], mds[l], Wl[l], tabs, bias, f"l{l}", bwd_flow)
    drel = swa_bias_bwd(_cat(dbiases, axis=0), bucket, name="swa_bias_bwd")
    return loss, dy, grads, smalls, dmods, dsinks, drel


ATT = ("w_in", "w_uq", "w_ukv", "w_out")
FFN = ("w_up", "w_down")
GATHER_STAGES = {
    "sb_fwd_l0": ([("l0", k) for k in ("w_out",) + FFN], []),
    "mla_fwd_l0": ([("l1", k) for k in ATT + ("w_up",)], [("l0", k) for k in ("w_out",) + FFN]),
    "swa_fwd_l0": ([("l1", "w_down")], [("l1", k) for k in ATT + ("w_up",)]),
    "sb_fwd_l1": ([], [("l1", "w_down")]),
}
SCATTER_STAGES = {
    "sb_bwd_l1": [("l1", k) for k in FFN],
    "sb_bwd_l0": [("l1", k) for k in ATT] + [("l0", k) for k in FFN + ("w_out",)],
}


class _GatherFlow:
    def __init__(self, shards, chip):
        self.shards, self.chip, self.ici, self.d2d, self.pending = shards, chip, {}, {}, {}

    def early(self, keys):
        ici, = run_plans([plan_gather_ici([self.shards[k] for k in keys])], name="gather_early_ici")
        d2d, = run_plans([plan_gather_d2d(ici)], name="gather_early_d2d")
        self.d2d.update(zip(keys, d2d))

    def plans(self, tag):
        ici_keys, d2d_keys = GATHER_STAGES.get(tag, ([], []))
        plans = []
        if d2d_keys:
            plans.append(plan_gather_d2d([self.ici[k] for k in d2d_keys]))
        if ici_keys:
            plans.append(plan_gather_ici([self.shards[k] for k in ici_keys]))
        self.pending[tag] = (ici_keys, d2d_keys)
        return plans

    def done(self, tag, outs):
        ici_keys, d2d_keys = self.pending.pop(tag, ([], []))
        outs = list(outs)
        if d2d_keys:
            self.d2d.update(zip(d2d_keys, outs.pop(0)))
        if ici_keys:
            self.ici.update(zip(ici_keys, outs.pop(0)))

    def weight(self, key):
        k = key[1]
        own = self.shards[key]
        r, cc = own.shape
        w4 = lax.dynamic_update_slice(self.d2d[key], own[None], (self.chip, 0, 0))
        fw = w4.reshape(NCHIPS * r, cc) if k in ROW_SHARDED else jnp.transpose(w4, (1, 0, 2)).reshape(r, NCHIPS * cc)
        return PREP[k](fw)


class _LayerWeights(dict):
    def __init__(self, small, flow, tag):
        super().__init__(small)
        self.flow, self.tag = flow, tag

    def __missing__(self, k):
        self[k] = self.flow.weight((self.tag, k))
        return self[k]


class _ScatterFlow:
    def __init__(self, shapes, sel, c_arr):
        self.shapes, self.sel, self.c_arr = shapes, sel, c_arr
        self.g, self.pairs, self.landed, self.pending = {}, {}, {}, {}

    def add(self, key, g):
        self.g[key] = g

    def _pairs(self, keys, label):
        g4s = []
        for key in keys:
            k = key[1]
            r, cc = self.shapes[k]
            gk = UNPREP[k](self.g[key])
            g4 = gk.reshape(NCHIPS, r, cc) if k in ROW_SHARDED else jnp.transpose(gk.reshape(r, NCHIPS, cc), (1, 0, 2))
            g4s.append(g4.astype(BF16))
        theirs, = run_plans([plan_swap_halves(g4s)], name=f"rs_swap_{label}")
        pairs = [pair_add_half(g4, th, self.c_arr, name=f"rs_pair_add_{key[1]}_{key[0]}") for key, g4, th in zip(keys, g4s, theirs)]
        self.pairs.update(zip(keys, pairs))
        return pairs

    def plans(self, tag):
        keys = SCATTER_STAGES.get(tag, [])
        self.pending[tag] = keys
        return [plan_scatter_ici(self._pairs(keys, tag))] if keys else []

    def done(self, tag, outs):
        keys = self.pending.pop(tag, [])
        if keys:
            self.landed.update(zip(keys, outs[0]))

    def finish(self):
        rest = [key for key in self.g if key not in self.pairs]
        if rest:
            landed, = run_plans([plan_scatter_ici(self._pairs(rest, "rest"))], name="rs_scatter_rest")
            self.landed.update(zip(rest, landed))
        keys = list(self.pairs)
        fulls = [chip_sum_into(self.landed[key], self.pairs[key], self.sel, name=f"rs_chip_sum_{key[1]}_{key[0]}") for key in keys]
        joined, = run_plans([plan_join_halves(fulls)], name="rs_join_halves")
        return dict(zip(keys, joined))


WEIGHTS = ("rel_table", "norm1_g", "norm2_g", "w_ada", "b_ada", "w_in", "mla_cq_g", "w_uq", "mla_ckv_g", "w_ukv", "mla_qn_g", "mla_kn_g",
           "sw_qn_g", "sw_kn_g", "sw_sinks", "w_out", "w_up", "conv_w", "conv_b", "w_down")
SMALL = tuple(n for n in WEIGHTS if n not in BIG + ("w_ada",))


def kernel(x, c, positions, rel_table, norm1_g, norm2_g, w_ada, b_ada, w_in, mla_cq_g, w_uq, mla_ckv_g, w_ukv, mla_qn_g, mla_kn_g, sw_qn_g, sw_kn_g, sw_sinks, w_out, w_up, conv_w, conv_b, w_down, loss_target, m_rel_table, m_norm1_g, m_norm2_g, m_w_ada, m_b_ada, m_w_in, m_mla_cq_g, m_w_uq, m_mla_ckv_g, m_w_ukv, m_mla_qn_g, m_mla_kn_g, m_sw_qn_g, m_sw_kn_g, m_sw_sinks, m_w_out, m_w_up, m_conv_w, m_conv_b, m_w_down, v_rel_table, v_norm1_g, v_norm2_g, v_w_ada, v_b_ada, v_w_in, v_mla_cq_g, v_w_uq, v_mla_ckv_g, v_w_ukv, v_mla_qn_g, v_mla_kn_g, v_sw_qn_g, v_sw_kn_g, v_sw_sinks, v_w_out, v_w_up, v_conv_w, v_conv_b, v_w_down):
    w = dict(rel_table=rel_table, norm1_g=norm1_g, norm2_g=norm2_g, w_ada=w_ada, b_ada=b_ada, w_in=w_in, mla_cq_g=mla_cq_g, w_uq=w_uq,
             mla_ckv_g=mla_ckv_g, w_ukv=w_ukv, mla_qn_g=mla_qn_g, mla_kn_g=mla_kn_g, sw_qn_g=sw_qn_g, sw_kn_g=sw_kn_g, sw_sinks=sw_sinks,
             w_out=w_out, w_up=w_up, conv_w=conv_w, conv_b=conv_b, w_down=w_down)
    m = dict(rel_table=m_rel_table, norm1_g=m_norm1_g, norm2_g=m_norm2_g, w_ada=m_w_ada, b_ada=m_b_ada, w_in=m_w_in, mla_cq_g=m_mla_cq_g,
             w_uq=m_w_uq, mla_ckv_g=m_mla_ckv_g, w_ukv=m_w_ukv, mla_qn_g=m_mla_qn_g, mla_kn_g=m_mla_kn_g, sw_qn_g=m_sw_qn_g,
             sw_kn_g=m_sw_kn_g, sw_sinks=m_sw_sinks, w_out=m_w_out, w_up=m_w_up, conv_w=m_conv_w, conv_b=m_conv_b, w_down=m_w_down)
    v = dict(rel_table=v_rel_table, norm1_g=v_norm1_g, norm2_g=v_norm2_g, w_ada=v_w_ada, b_ada=v_b_ada, w_in=v_w_in, mla_cq_g=v_mla_cq_g,
             w_uq=v_w_uq, mla_ckv_g=v_mla_ckv_g, w_ukv=v_w_ukv, mla_qn_g=v_mla_qn_g, mla_kn_g=v_mla_kn_g, sw_qn_g=v_sw_qn_g,
             sw_kn_g=v_sw_kn_g, sw_sinks=v_sw_sinks, w_out=v_w_out, w_up=v_w_up, conv_w=v_conv_w, conv_b=v_conv_b, w_down=v_w_down)
    Bl, S, D = x.shape
    L = norm1_g.shape[0]
    xi, yi, ci = _me()
    chip = 2 * xi + yi
    dev = 4 * xi + 2 * yi + ci
    ndev = 2 * NCHIPS

    shapes = {k: w[k].shape[1:] for k in BIG}
    shards = {(f"l{l}", k): cast_layer(w[k], l, name=f"cast_{k}_l{l}") for l in range(L) for k in BIG}
    gflow = _GatherFlow(shards, chip)
    gflow.early([("l0", k) for k in ("w_in", "w_uq", "w_ukv")])

    cw_cols = conv_w.shape[2]
    c_got, cw_got = allgather8([c, conv_w.reshape(L * 3, cw_cols)], name="gather_cond")
    c_all = c_got.reshape(ndev * Bl, D)
    conv_full = jnp.transpose(cw_got[0::2].reshape(NCHIPS, L, 3, cw_cols), (1, 2, 0, 3)).reshape(L, 3, NCHIPS * cw_cols)
    E = w_ada.shape[2]
    b_cols = lax.dynamic_slice(b_ada, (0, chip * E), (L, E)).reshape(L, 1, E)
    mods_cols = mods_matmul(c_all, w_ada, b_cols, name="mods")
    mods_all, = allgather8([mods_cols.reshape(L * ndev * Bl, E)], name="gather_mods")
    mods_all = jnp.transpose(mods_all[0::2].reshape(NCHIPS, L, ndev * Bl, E), (1, 2, 0, 3)).reshape(L, ndev * Bl, NCHIPS * E)
    mods = lax.dynamic_slice(mods_all, (0, dev * Bl, 0), (L, Bl, NCHIPS * E))

    Wl = []
    for l in range(L):
        Wd = _small_params({k: w[k][l] for k in SMALL if k not in ("rel_table", "b_ada", "conv_w")})
        Wd["conv_w"] = _up_perm(conv_full[l])
        Wl.append(_LayerWeights(Wd, gflow, f"l{l}"))

    sflow = _ScatterFlow(shapes, jnp.stack([chip, ci]).astype(jnp.int32), ci.reshape(1).astype(jnp.int32))
    loss, dx, _, smalls, dmods, dsinks, drel = _local_step(x, loss_target, positions, mods, Wl, rel_table.reshape(-1), gflow, sflow)
    reduced = sflow.finish()
    grad = {k: jnp.stack([reduced[(f"l{l}", k)] for l in range(L)]) for k in BIG}

    vec_names = ("n1", "n2", "cq_g", "ckv_g", "qn_g", "kn_g", "swq_g", "swk_g")
    vecs = _cat([_cat([smalls[l][k] for k in vec_names], axis=1) for l in range(L)], axis=0)
    convs = _cat([smalls[l]["conv"][0:4] for l in range(L)], axis=0)
    dm = jnp.stack(dmods, axis=1).reshape(Bl * L, 6 * D)
    dsk = jnp.stack(dsinks, axis=1).reshape(Bl * L * 6, LANES)
    got = allgather8([vecs, convs, drel, loss, dm, dsk], name="gather_small_grads")
    seq = lambda a, rows: a.reshape(ndev * Bl, rows, a.shape[-1])
    vec_s, conv_s, rel_s, loss_s, dm_s, dsk_s = sum_small(list(got[:4]) + [seq(got[4], L), seq(got[5], L * 6)], name="sum_small_grads")
    dm_all = jnp.transpose(seq(got[4], L), (1, 0, 2))
    grad["w_ada"] = ada_grad(c_all, lax.dynamic_slice(dm_all, (0, 0, chip * E), (L, ndev * Bl, E)), name="ada_grad")
    grad["b_ada"] = dm_s
    grad["sw_sinks"] = jnp.transpose(dsk_s.reshape(L, 3, 2, LANES)[:, :, :, 0], (0, 2, 1)).reshape(L, 6)
    grad["rel_table"] = rel_s[:6, :REL_BUCKETS].T
    off = 0
    for k, name_, keep in zip(vec_names, ("norm1_g", "norm2_g", "mla_cq_g", "mla_ckv_g", "mla_qn_g", "mla_kn_g", "sw_qn_g", "sw_kn_g"),
                              (D, D, 256, LANES, MLA_QK, MLA_QK, HEAD, HEAD)):
        grad[name_] = vec_s[:, off:off + keep]
        off += smalls[0][k].shape[1]
    conv = _up_perm(conv_s.reshape(L, 4, 2 * D_FF))
    grad["conv_w"] = lax.dynamic_slice(conv[:, 0:3], (0, 0, chip * cw_cols), (L, 3, cw_cols))
    grad["conv_b"] = conv[:, 3]
    loss_out = loss_s[0, 0]

    delta, new_m, new_v = {}, {}, {}
    for k in BIG + ("w_ada",):
        delta[k], new_m[k], new_v[k] = adamw(w[k], grad[k], m[k], v[k], name=f"adamw_{k}")
    outs = adamw_small(*[[src[k] for k in SMALL] for src in (w, grad, m, v)], name="adamw_small")
    for dst, o in zip((delta, new_m, new_v), outs):
        dst.update(dict(zip(SMALL, o)))
    return (loss_out, dx, *[grad[k] for k in WEIGHTS], *[delta[k] for k in WEIGHTS], *[new_m[k] for k in WEIGHTS], *[new_v[k] for k in WEIGHTS])
```

```python
import math

import jax
import jax.numpy as jnp
from jax import lax
from jax.experimental import pallas as pl
from jax.experimental.pallas import tpu as pltpu

F32 = jnp.float32
BF16 = jnp.bfloat16
MESH = pl.DeviceIdType.MESH

EPS = 1e-6
NEG = -1e30
HEAD = 64
LANES = 128
MLA_QK = 96
ROPE_THETA = 10000.0
REL_BUCKETS = 32
REL_MAX_DIST = 128
WINDOW = 128
D_FF = 2816
ADAM_LR, ADAM_B1, ADAM_B2, ADAM_EPS, ADAM_WD, ADAM_STEP = 0.001, 0.9, 0.999, 1e-08, 0.01, 10

VMEM_LIMIT = 56 * 1024 * 1024
STRIP = 32

P_SBQ, P_SBK, P_SBV, P_CQ, P_CKV, P_SLAB, P_SWQ, P_SWK, P_SWV, P_END = 0, 256, 512, 768, 1024, 1152, 1280, 1664, 1792, 1920


def _cp(*sem):
    return pltpu.CompilerParams(dimension_semantics=sem, vmem_limit_bytes=VMEM_LIMIT)


def _dot(a, b):
    return jnp.dot(a, b, preferred_element_type=F32)


def _dot_nt(a, b):
    return lax.dot_general(a, b, (((1,), (1,)), ((), ())), preferred_element_type=F32)


def _dot_tn(a, b):
    return lax.dot_general(a, b, (((0,), (0,)), ((), ())), preferred_element_type=F32)


def _lane_masks():
    lane = lax.broadcasted_iota(jnp.int32, (1, LANES), 1)
    return (lane < HEAD, lane >= HEAD)


def _tile(n, cap, align=128):
    if n <= cap:
        return n
    t = cap - cap % align
    while t >= align:
        if n % t == 0:
            return t
        t -= align
    return n


def matmul(a, b, *, ta=False, tb=False, out_dtype=F32, tm=512, tn=512, tk=8192, name):
    M, K = (a.shape[1], a.shape[0]) if ta else a.shape
    N = b.shape[0] if tb else b.shape[1]
    tm, tn, tk = _tile(M, tm), _tile(N, tn), _tile(K, tk)
    nk = K // tk

    def body(a_ref, b_ref, o_ref, *scratch):
        av = a_ref[...].astype(BF16)
        bv = b_ref[...].astype(BF16)
        if ta:
            part = _dot_tn(av, bv)
        elif tb:
            part = _dot_nt(av, bv)
        else:
            part = _dot(av, bv)
        if nk == 1:
            o_ref[...] = part.astype(out_dtype)
        else:
            acc_ref, = scratch
            k = pl.program_id(2)

            @pl.when(k == 0)
            def _():
                acc_ref[...] = part

            @pl.when(k > 0)
            def _():
                acc_ref[...] += part

            @pl.when(k == nk - 1)
            def _():
                o_ref[...] = acc_ref[...].astype(out_dtype)

    n_outer = nk == 1 and tn * b.dtype.itemsize > tm * a.dtype.itemsize
    ij = (lambda p, q: (q, p)) if n_outer else (lambda p, q: (p, q))
    a_map = (lambda p, q, k: (k, ij(p, q)[0])) if ta else (lambda p, q, k: (ij(p, q)[0], k))
    b_map = (lambda p, q, k: (ij(p, q)[1], k)) if tb else (lambda p, q, k: (k, ij(p, q)[1]))
    grid = (N // tn, M // tm, nk) if n_outer else (M // tm, N // tn, nk)
    return pl.pallas_call(
        body, name=name, grid=grid,
        in_specs=[pl.BlockSpec((tk, tm) if ta else (tm, tk), a_map), pl.BlockSpec((tn, tk) if tb else (tk, tn), b_map)],
        out_specs=pl.BlockSpec((tm, tn), lambda p, q, k: ij(p, q)),
        out_shape=jax.ShapeDtypeStruct((M, N), out_dtype),
        scratch_shapes=[] if nk == 1 else [pltpu.VMEM((tm, tn), F32)],
        compiler_params=_cp("parallel", "parallel", "arbitrary"),
    )(a, b)


def matmul_res(a, b, res, gate, seq, *, tm=512, tn=1024, name):
    M, K = a.shape
    N = b.shape[1]
    tm, tn = _tile(min(M, seq), tm), _tile(N, tn)
    per_seq = seq // tm

    def body(a_ref, b_ref, r_ref, g_ref, y_ref, x_ref):
        y = _dot(a_ref[...].astype(BF16), b_ref[...].astype(BF16))
        y_ref[...] = y
        x_ref[...] = r_ref[...] + g_ref[...] * y

    out = jax.ShapeDtypeStruct((M, N), F32)
    return pl.pallas_call(
        body, name=name, grid=(M // tm, N // tn),
        in_specs=[pl.BlockSpec((tm, K), lambda i, j: (i, 0)), pl.BlockSpec((K, tn), lambda i, j: (0, j)),
                  pl.BlockSpec((tm, tn), lambda i, j: (i, j)), pl.BlockSpec((None, 1, tn), lambda i, j: (lax.div(i, jnp.int32(per_seq)), 0, j))],
        out_specs=[pl.BlockSpec((tm, tn), lambda i, j: (i, j))] * 2,
        out_shape=[out, out], compiler_params=_cp("parallel", "parallel"),
    )(a, b, res, gate)


def rms_fwd(x3, blk, W, g, sc=None, sh=None, *, tm=512, name):
    Bl, S, _ = x3.shape
    tm = min(tm, S)
    mod = sc is not None

    def body(x_ref, g_ref, *rest):
        o_ref = rest[-1]
        x = x_ref[...]
        r = lax.rsqrt(jnp.mean(x * x, axis=-1, keepdims=True) + EPS)
        y = x * r * g_ref[...]
        if mod:
            y = y * (1.0 + rest[0][...]) + rest[1][...]
        o_ref[...] = y.astype(BF16)

    vec = pl.BlockSpec((None, 1, W), lambda b, s: (b, 0, 0))
    return pl.pallas_call(
        body, name=name, grid=(Bl, S // tm),
        in_specs=[pl.BlockSpec((None, tm, W), lambda b, s: (b, s, blk)), pl.BlockSpec((1, W), lambda b, s: (0, 0))] + ([vec, vec] if mod else []),
        out_specs=pl.BlockSpec((None, tm, W), lambda b, s: (b, s, 0)),
        out_shape=jax.ShapeDtypeStruct((Bl, S, W), BF16),
        compiler_params=_cp("parallel", "parallel"),
    )(x3, g, *([sc, sh] if mod else []))


def rms_bwd(x3, blk, W, dy3, g, sc=None, dres3=None, *, tm=256, name):
    Bl, S, _ = x3.shape
    tm = min(tm, S)
    mod = sc is not None
    res = dres3 is not None

    def body(*refs):
        x_ref, dy_ref, g_ref = refs[:3]
        k = 3
        sc_ref = dr_ref = None
        if mod:
            sc_ref = refs[k]
            k += 1
        if res:
            dr_ref = refs[k]
            k += 1
        dx_ref, dg_ref = refs[k], refs[k + 1]
        b, s = pl.program_id(0), pl.program_id(1)
        x = x_ref[...]
        dy = dy_ref[...].astype(F32)
        g = g_ref[...]
        r = lax.rsqrt(jnp.mean(x * x, axis=-1, keepdims=True) + EPS)
        n = x * r
        if mod:
            dsc_ref, dsh_ref = refs[k + 2], refs[k + 3]
            one_sc = 1.0 + sc_ref[...]

            @pl.when(s == 0)
            def _():
                dsc_ref[...] = jnp.zeros_like(dsc_ref)
                dsh_ref[...] = jnp.zeros_like(dsh_ref)

            dsh_ref[...] += jnp.sum(dy, axis=0, keepdims=True)
            dsc_ref[...] += jnp.sum(dy * n * g, axis=0, keepdims=True)
            dyn = dy * one_sc
        else:
            dyn = dy

        @pl.when((b == 0) & (s == 0))
        def _():
            dg_ref[...] = jnp.zeros_like(dg_ref)

        dg_ref[...] += jnp.sum(dyn * n, axis=0, keepdims=True)
        dn = dyn * g
        dx = r * (dn - n * jnp.mean(dn * n, axis=-1, keepdims=True))
        if res:
            dx = dx + dr_ref[...]
        dx_ref[...] = dx

    blkspec = pl.BlockSpec((None, tm, W), lambda b, s: (b, s, 0))
    vec = pl.BlockSpec((None, 1, W), lambda b, s: (b, 0, 0))
    row = pl.BlockSpec((1, W), lambda b, s: (0, 0))
    in_specs = [pl.BlockSpec((None, tm, W), lambda b, s: (b, s, blk)), blkspec, row] + ([vec] if mod else []) + ([blkspec] if res else [])
    out_specs = [blkspec, row] + ([vec, vec] if mod else [])
    out_shape = [jax.ShapeDtypeStruct((Bl, S, W), F32), jax.ShapeDtypeStruct((1, W), F32)]
    if mod:
        out_shape += [jax.ShapeDtypeStruct((Bl, 1, W), F32)] * 2
    args = [x3, dy3, g] + ([sc] if mod else []) + ([dres3] if res else [])
    return pl.pallas_call(
        body, name=name, grid=(Bl, S // tm), in_specs=in_specs, out_specs=out_specs, out_shape=out_shape,
        compiler_params=_cp("arbitrary", "arbitrary"),
    )(*args)


def pair_rms_fwd(x3, blk0, npairs, g2, *, tm=1024, name):
    Bl, S, _ = x3.shape
    tm = min(tm, S)

    def body(x_ref, g_ref, o_ref):
        lo, hi = _lane_masks()
        x = x_ref[...]
        xx = x * x
        s0 = jnp.sum(jnp.where(lo, xx, 0.0), axis=-1, keepdims=True)
        s1 = jnp.sum(jnp.where(hi, xx, 0.0), axis=-1, keepdims=True)
        r = jnp.where(lo, lax.rsqrt(s0 / HEAD + EPS), lax.rsqrt(s1 / HEAD + EPS))
        o_ref[...] = (x * r * g_ref[...]).astype(BF16)

    return pl.pallas_call(
        body, name=name, grid=(Bl, S // tm, npairs),
        in_specs=[pl.BlockSpec((None, tm, LANES), lambda b, s, p: (b, s, blk0 + p)), pl.BlockSpec((1, LANES), lambda b, s, p: (0, 0))],
        out_specs=pl.BlockSpec((None, tm, LANES), lambda b, s, p: (b, s, p)),
        out_shape=jax.ShapeDtypeStruct((Bl, S, LANES * npairs), BF16),
        compiler_params=_cp("parallel", "parallel", "parallel"),
    )(x3, g2)


def pair_rms_bwd(x3, blk0, npairs, dy3, g2, *, tm=1024, name):
    Bl, S, _ = x3.shape
    tm = min(tm, S)

    def body(x_ref, dy_ref, g_ref, dx_ref, dg_ref):
        lo, hi = _lane_masks()
        first = (pl.program_id(0) == 0) & (pl.program_id(1) == 0) & (pl.program_id(2) == 0)
        x = x_ref[...]
        dy = dy_ref[...]
        xx = x * x
        s0 = jnp.sum(jnp.where(lo, xx, 0.0), axis=-1, keepdims=True)
        s1 = jnp.sum(jnp.where(hi, xx, 0.0), axis=-1, keepdims=True)
        r = jnp.where(lo, lax.rsqrt(s0 / HEAD + EPS), lax.rsqrt(s1 / HEAD + EPS))
        n = x * r

        @pl.when(first)
        def _():
            dg_ref[...] = jnp.zeros_like(dg_ref)

        part = jnp.sum(dy * n, axis=0, keepdims=True)
        dg_ref[...] += part + pltpu.roll(part, HEAD, 1)
        dn = dy * g_ref[...]
        t = dn * n
        m0 = jnp.sum(jnp.where(lo, t, 0.0), axis=-1, keepdims=True)
        m1 = jnp.sum(jnp.where(hi, t, 0.0), axis=-1, keepdims=True)
        dx_ref[...] = r * (dn - n * (jnp.where(lo, m0, m1) / HEAD))

    return pl.pallas_call(
        body, name=name, grid=(Bl, S // tm, npairs),
        in_specs=[pl.BlockSpec((None, tm, LANES), lambda b, s, p: (b, s, blk0 + p)), pl.BlockSpec((None, tm, LANES), lambda b, s, p: (b, s, p)),
                  pl.BlockSpec((1, LANES), lambda b, s, p: (0, 0))],
        out_specs=[pl.BlockSpec((None, tm, LANES), lambda b, s, p: (b, s, p)), pl.BlockSpec((1, LANES), lambda b, s, p: (0, 0))],
        out_shape=[jax.ShapeDtypeStruct((Bl, S, LANES * npairs), F32), jax.ShapeDtypeStruct((1, LANES), F32)],
        compiler_params=_cp("arbitrary", "arbitrary", "arbitrary"),
    )(x3, dy3, g2)


def _rot(y, cos_t, sin_a, sin_b):
    return y * cos_t + pltpu.roll(y, LANES - 16, 1) * sin_a + pltpu.roll(y, 16, 1) * sin_b


def _rot_t(d, cos_t, sin_a, sin_b):
    return d * cos_t + pltpu.roll(d * sin_a, 16, 1) + pltpu.roll(d * sin_b, LANES - 16, 1)


def rope_norm_fwd(x3, nheads, g, tabs, slab=None, *, tm=1024, name):
    Bl, S, _ = x3.shape
    tm = min(tm, S)
    has_slab = slab is not None

    def body(*refs):
        x_ref, g_ref, c_ref, sa_ref, sb_ref = refs[:5]
        o_ref = refs[-1]
        x = x_ref[...]
        if has_slab:
            x = x + refs[5][...]
        r = lax.rsqrt(jnp.sum(x * x, axis=-1, keepdims=True) / MLA_QK + EPS)
        o_ref[...] = _rot(x * r * g_ref[...], c_ref[...], sa_ref[...], sb_ref[...]).astype(BF16)

    head = pl.BlockSpec((None, tm, LANES), lambda b, s, h: (b, s, h))
    tab = pl.BlockSpec((None, tm, LANES), lambda b, s, h: (b, s, 0))
    in_specs = [head, pl.BlockSpec((1, LANES), lambda b, s, h: (0, 0)), tab, tab, tab]
    args = [x3, g, *tabs]
    if has_slab:
        sblk = slab[1]
        in_specs.append(pl.BlockSpec((None, tm, LANES), lambda b, s, h: (b, s, sblk)))
        args.append(slab[0])
    return pl.pallas_call(
        body, name=name, grid=(Bl, S // tm, nheads), in_specs=in_specs, out_specs=head,
        out_shape=jax.ShapeDtypeStruct((Bl, S, LANES * nheads), BF16),
        compiler_params=_cp("parallel", "parallel", "parallel"),
    )(*args)


def rope_norm_bwd(x3, nheads, dy3, g, tabs, slab=None, *, tm=1024, name):
    Bl, S, _ = x3.shape
    tm = min(tm, S)
    has_slab = slab is not None

    def body(*refs):
        x_ref, dy_ref, g_ref, c_ref, sa_ref, sb_ref = refs[:6]
        k = 7 if has_slab else 6
        dx_ref, dg_ref = refs[k], refs[k + 1]
        h = pl.program_id(2)
        first = (pl.program_id(0) == 0) & (pl.program_id(1) == 0) & (h == 0)
        x = x_ref[...]
        if has_slab:
            x = x + refs[6][...]
        g = g_ref[...]
        r = lax.rsqrt(jnp.sum(x * x, axis=-1, keepdims=True) / MLA_QK + EPS)
        n = x * r
        d = _rot_t(dy_ref[...], c_ref[...], sa_ref[...], sb_ref[...])

        @pl.when(first)
        def _():
            dg_ref[...] = jnp.zeros_like(dg_ref)

        dg_ref[...] += jnp.sum(d * n, axis=0, keepdims=True)
        dn = d * g
        dx = r * (dn - n * (jnp.sum(dn * n, axis=-1, keepdims=True) / MLA_QK))
        dx_ref[...] = dx.astype(BF16)
        if has_slab:
            ds_ref = refs[k + 2]

            @pl.when(h == 0)
            def _():
                ds_ref[...] = dx

            @pl.when(h > 0)
            def _():
                ds_ref[...] += dx

    head = pl.BlockSpec((None, tm, LANES), lambda b, s, h: (b, s, h))
    tab = pl.BlockSpec((None, tm, LANES), lambda b, s, h: (b, s, 0))
    row = pl.BlockSpec((1, LANES), lambda b, s, h: (0, 0))
    in_specs = [head, head, row, tab, tab, tab]
    args = [x3, dy3, g, *tabs]
    out_specs = [head, row]
    out_shape = [jax.ShapeDtypeStruct((Bl, S, LANES * nheads), BF16), jax.ShapeDtypeStruct((1, LANES), F32)]
    if has_slab:
        sblk = slab[1]
        in_specs.append(pl.BlockSpec((None, tm, LANES), lambda b, s, h: (b, s, sblk)))
        args.append(slab[0])
        out_specs.append(tab)
        out_shape.append(jax.ShapeDtypeStruct((Bl, S, LANES), F32))
    return pl.pallas_call(
        body, name=name, grid=(Bl, S // tm, nheads), in_specs=in_specs, out_specs=out_specs, out_shape=out_shape,
        compiler_params=_cp("arbitrary", "arbitrary", "arbitrary"),
    )(*args)


def _softplus(z):
    return jnp.maximum(z, 0.0) + jnp.log(1.0 + jnp.exp(-jnp.abs(z)))


def _split_dots(xs, u):
    hi = [x.astype(BF16) for x in xs]
    lo = [(x - h.astype(F32)).astype(BF16) for x, h in zip(xs, hi)]
    top = [_dot(h, u) for h in hi]
    return [t + _dot(l, u) for t, l in zip(top, lo)]


SB_BLOCK = 256
SB_QBLOCK = 512


def sb_attn_fwd(proj3, *, plans=None, name):
    Bl, S, _ = proj3.shape
    tk = min(SB_BLOCK, S)
    tq = min(SB_QBLOCK, S)
    per_q = tq // tk
    scale = HEAD ** -0.5
    qb, kb0, vb0 = P_SBQ // LANES, P_SBK // LANES, P_SBV // LANES

    def body(q_ref, k_ref, v_ref, o_ref, rt_ref):
        i = pl.program_id(2)
        masks = _lane_masks()
        lane = lax.broadcasted_iota(jnp.int32, (1, LANES), 1)
        q = q_ref[...]
        qh = [jnp.where(m, q, 0.0).astype(BF16) for m in masks]
        rr = lax.broadcasted_iota(jnp.int32, (tq, tk), 0)
        cc = lax.broadcasted_iota(jnp.int32, (tq, tk), 1)
        u = (lax.broadcasted_iota(jnp.int32, (tk, tk), 0) > lax.broadcasted_iota(jnp.int32, (tk, tk), 1)).astype(BF16)

        rt_ref[...] = jnp.zeros_like(rt_ref)

        def step(j, carry, masked):
            r0, r1, acc = carry
            off = pl.multiple_of(j * tk, tk)
            kb = k_ref[pl.ds(off, tk), :].astype(BF16)
            vb = v_ref[pl.ds(off, tk), :]
            strict = (cc + j * tk) < (rr + i * tq) if masked else None
            only = (lambda t: jnp.where(strict, t, 0.0)) if masked else (lambda t: t)
            rt_ref[...] = jnp.where(lane == j, r0, jnp.where(lane == j + HEAD, r1, rt_ref[...]))
            rs, two = [r0, r1], range(2)
            z = [_dot_nt(qh[h], kb) * scale for h in two]
            sp = [_softplus(z[h]) for h in two]
            keep = [only(-sp[h]) for h in two]
            suf = _split_dots(keep, u)
            w = [only(jnp.exp((z[h] - sp[h]) + suf[h] + rs[h])) for h in two]
            pv = [_dot(w[h].astype(BF16), jnp.where(masks[h], vb, 0.0).astype(BF16)) for h in two]
            return rs[0] + jnp.sum(keep[0], axis=1, keepdims=True), rs[1] + jnp.sum(keep[1], axis=1, keepdims=True), acc + (pv[0] + pv[1])

        zero = jnp.zeros((tq, 1), F32)
        carry = (zero, zero, jnp.zeros((tq, LANES), F32))
        for t in range(per_q):
            carry = step((i + 1) * per_q - 1 - t, carry, True)
        _, _, acc = lax.fori_loop(0, i * per_q, lambda t, c: step(i * per_q - 1 - t, c, False), carry)
        o_ref[...] = acc

    seq = lambda blk0: pl.BlockSpec((None, S, LANES), lambda b, p, i: (b, 0, blk0 + p))
    out = pl.BlockSpec((None, tq, LANES), lambda b, p, i: (b, i, p))
    shp = jax.ShapeDtypeStruct((Bl, S, 2 * LANES), F32)
    return call_with_plans(
        body, plans, name=name, grid=(Bl, 2, S // tq),
        in_specs=[pl.BlockSpec((None, tq, LANES), lambda b, p, i: (b, i, qb + p)), seq(kb0), seq(vb0)],
        out_specs=[out, out], out_shape=[shp, shp], scratch_shapes=[], args=[proj3, proj3, proj3],
        sem=("arbitrary",) * 3 if plans else ("parallel", "parallel", "arbitrary"))


def sb_attn_bwd(proj3, rt3, do3, *, do_blk0=0, plans=None, name):
    Bl, S, _ = proj3.shape
    tk = min(SB_BLOCK, S)
    tq = min(SB_QBLOCK, S)
    per_q = tq // tk
    scale = HEAD ** -0.5
    qb, kb0, vb0 = P_SBQ // LANES, P_SBK // LANES, P_SBV // LANES

    def body(q_ref, k_ref, v_ref, rt_ref, do_ref, dq_ref, dk_ref, dv_ref):
        i = pl.program_id(2)

        @pl.when(i == 0)
        def _():
            dk_ref[...] = jnp.zeros_like(dk_ref)
            dv_ref[...] = jnp.zeros_like(dv_ref)

        masks = _lane_masks()
        lane = lax.broadcasted_iota(jnp.int32, (1, LANES), 1)
        q = q_ref[...]
        qh = [jnp.where(m, q, 0.0).astype(BF16) for m in masks]
        do_b = do_ref[...].astype(BF16)
        doh = [jnp.where(m, do_b, jnp.zeros_like(do_b)) for m in masks]
        rt = rt_ref[...]
        rr = lax.broadcasted_iota(jnp.int32, (tq, tk), 0)
        cc = lax.broadcasted_iota(jnp.int32, (tq, tk), 1)
        ur = lax.broadcasted_iota(jnp.int32, (tk, tk), 0)
        uc = lax.broadcasted_iota(jnp.int32, (tk, tk), 1)
        u_suffix = (ur > uc).astype(BF16)
        u_prefix = (ur < uc).astype(BF16)

        def step(j, carry, masked):
            p0, p1, dq = carry
            off = pl.multiple_of(j * tk, tk)
            kf = k_ref[pl.ds(off, tk), :]
            kb = kf.astype(BF16)
            vb = v_ref[pl.ds(off, tk), :]
            strict = (cc + j * tk) < (rr + i * tq) if masked else None
            only = (lambda t: jnp.where(strict, t, 0.0)) if masked else (lambda t: t)
            ps, two = [p0, p1], range(2)
            r_j = [jnp.sum(jnp.where(lane == j + h * HEAD, rt, 0.0), axis=1, keepdims=True) for h in two]
            z = [_dot_nt(qh[h], kb) * scale for h in two]
            dw = [_dot_nt(doh[h], jnp.where(masks[h], vb, 0.0).astype(BF16)) for h in two]
            sp = [_softplus(z[h]) for h in two]
            keep = [only(-sp[h]) for h in two]
            suf = _split_dots(keep, u_suffix)
            w = [only(jnp.exp((z[h] - sp[h]) + suf[h] + r_j[h])) for h in two]
            g = [dw[h] * w[h] for h in two]
            pre = _split_dots(g, u_prefix)
            dzb = [(only(g[h] * jnp.exp(-sp[h]) - jnp.exp(z[h] - sp[h]) * (pre[h] + ps[h])) * scale).astype(BF16) for h in two]
            dqs = [_dot(dzb[h], jnp.where(masks[h], kf, 0.0).astype(BF16)) for h in two]
            dks = [_dot_tn(dzb[h], qh[h]) for h in two]
            dvs = [_dot_tn(w[h].astype(BF16), doh[h]) for h in two]
            dk_ref[pl.ds(off, tk), :] += dks[0] + dks[1]
            dv_ref[pl.ds(off, tk), :] += dvs[0] + dvs[1]
            return ps[0] + jnp.sum(g[0], axis=1, keepdims=True), ps[1] + jnp.sum(g[1], axis=1, keepdims=True), dq + (dqs[0] + dqs[1])

        zero = jnp.zeros((tq, 1), F32)
        carry = lax.fori_loop(0, i * per_q, lambda j, c: step(j, c, False), (zero, zero, jnp.zeros((tq, LANES), F32)))
        for t in range(per_q):
            carry = step(i * per_q + t, carry, True)
        dq_ref[...] = carry[2]

    seq_in = lambda blk0: pl.BlockSpec((None, S, LANES), lambda b, p, i: (b, 0, blk0 + p))
    blk = pl.BlockSpec((None, tq, LANES), lambda b, p, i: (b, i, p))
    seq_out = pl.BlockSpec((None, S, LANES), lambda b, p, i: (b, 0, p))
    shp = jax.ShapeDtypeStruct((Bl, S, 2 * LANES), F32)
    return call_with_plans(
        body, plans, name=name, grid=(Bl, 2, S // tq),
        in_specs=[pl.BlockSpec((None, tq, LANES), lambda b, p, i: (b, i, qb + p)), seq_in(kb0), seq_in(vb0), blk,
                  pl.BlockSpec((None, tq, LANES), lambda b, p, i: (b, i, do_blk0 + p))],
        out_specs=[blk, seq_out, seq_out], out_shape=[shp, shp, shp], scratch_shapes=[], args=[proj3, proj3, proj3, rt3, do3],
        sem=("arbitrary",) * 3 if plans else ("parallel", "parallel", "arbitrary"))


def mla_attn_fwd(q3, k3, kv3, vblk0, *, tq=512, tk=512, plans=None, name):
    Bl, S, _ = q3.shape
    tq = min(tq, S)
    tk = min(tk, tq)
    per_q = tq // tk
    scale = MLA_QK ** -0.5

    def body(q_ref, k_ref, v_ref, o_ref, lse_ref):
        i = pl.program_id(2)
        masks = _lane_masks()
        rr = lax.broadcasted_iota(jnp.int32, (tq, tk), 0)
        cc = lax.broadcasted_iota(jnp.int32, (tq, tk), 1)
        qh = [q_ref[:, h * LANES:(h + 1) * LANES] for h in range(2)]

        def step(j, carry):
            m0, l0, m1, l1, acc = carry
            off = pl.multiple_of(j * tk, tk)
            vb = v_ref[pl.ds(off, tk), :]
            causal = (cc + j * tk) <= (rr + i * tq)
            ms, ls, two = [m0, m1], [l0, l1], range(2)
            kh = [k_ref[pl.ds(off, tk), h * LANES:(h + 1) * LANES] for h in two]
            s = [jnp.where(causal, _dot_nt(qh[h], kh[h]) * scale, NEG) for h in two]
            m_new = [jnp.maximum(ms[h], jnp.max(s[h], axis=1, keepdims=True)) for h in two]
            p = [jnp.exp(s[h] - m_new[h]) for h in two]
            alpha = [jnp.exp(ms[h] - m_new[h]) for h in two]
            ls = [alpha[h] * ls[h] + jnp.sum(p[h], axis=1, keepdims=True) for h in two]
            add = [_dot(p[h].astype(BF16), jnp.where(masks[h], vb, 0.0).astype(BF16)) for h in two]
            acc = acc * jnp.where(masks[0], alpha[0], alpha[1]) + (add[0] + add[1])
            return m_new[0], ls[0], m_new[1], ls[1], acc

        neg = jnp.full((tq, 1), NEG, F32)
        zero = jnp.zeros((tq, 1), F32)
        m0, l0, m1, l1, acc = lax.fori_loop(0, (i + 1) * per_q, step, (neg, zero, neg, zero, jnp.zeros((tq, LANES), F32)))
        o_ref[...] = acc / jnp.where(masks[0], l0, l1)
        lse_ref[...] = jnp.where(masks[0], m0 + jnp.log(l0), m1 + jnp.log(l1))

    out = pl.BlockSpec((None, tq, LANES), lambda b, p, i: (b, i, p))
    shp = jax.ShapeDtypeStruct((Bl, S, 3 * LANES), F32)
    return call_with_plans(
        body, plans, name=name, grid=(Bl, 3, S // tq),
        in_specs=[pl.BlockSpec((None, tq, 2 * LANES), lambda b, p, i: (b, i, p)), pl.BlockSpec((None, S, 2 * LANES), lambda b, p, i: (b, 0, p)),
                  pl.BlockSpec((None, S, LANES), lambda b, p, i: (b, 0, vblk0 + p))],
        out_specs=[out, out], out_shape=[shp, shp], scratch_shapes=[], args=[q3, k3, kv3],
        sem=("arbitrary",) * 3 if plans else ("parallel", "parallel", "arbitrary"))


def mla_attn_bwd(q3, k3, kv3, vblk0, o3, lse3, do3, *, do_blk0=0, tq=512, tk=512, name):
    Bl, S, _ = q3.shape
    tq = min(tq, S)
    tk = min(tk, tq)
    per_q = tq // tk
    nq = S // tq
    scale = MLA_QK ** -0.5

    def body(q_ref, k_ref, v_ref, o_ref, lse_ref, do_ref, dq_ref, dk_ref, dv_ref, s_scr, dp_scr, p_scr, ds_scr):
        j = pl.program_id(2)

        @pl.when(j == 0)
        def _():
            dq_ref[...] = jnp.zeros_like(dq_ref)

        masks = _lane_masks()
        vb = v_ref[...]
        vh = [jnp.where(m, vb, 0.0).astype(BF16) for m in masks]
        kh = [k_ref[:, h * LANES:(h + 1) * LANES] for h in range(2)]
        i0 = lax.div(j, jnp.int32(per_q))

        def step(i, carry, masked):
            dk0, dk1, dv = carry
            off = pl.multiple_of(i * tq, tq)
            do_b = do_ref[pl.ds(off, tq), :].astype(BF16)
            prod = do_b.astype(F32) * o_ref[pl.ds(off, tq), :]
            lse = lse_ref[pl.ds(off, tq), :]
            two = range(2)
            qh = [q_ref[pl.ds(off, tq), h * LANES:(h + 1) * LANES] for h in two]
            doh = [jnp.where(masks[h], do_b, jnp.zeros_like(do_b)) for h in two]
            delta = [jnp.sum(jnp.where(masks[h], prod, 0.0), axis=1, keepdims=True) for h in two]
            lse_h = [lse[:, h * HEAD:h * HEAD + 1] for h in two]
            for h in two:
                s_scr[h] = _dot_nt(qh[h], kh[h])
            for h in two:
                dp_scr[h] = _dot_nt(doh[h], vh[h])
            for r0 in range(0, tq, STRIP):
                rows = slice(r0, r0 + STRIP)
                for h in two:
                    s = s_scr[h, rows, :] * scale
                    if masked:
                        rr = lax.broadcasted_iota(jnp.int32, (STRIP, tk), 0) + (i * tq + r0)
                        cc = lax.broadcasted_iota(jnp.int32, (STRIP, tk), 1) + j * tk
                        s = jnp.where(cc <= rr, s, NEG)
                    p = jnp.exp(s - lse_h[h][rows])
                    p_scr[h, rows, :] = p.astype(BF16)
                    ds_scr[h, rows, :] = (p * (dp_scr[h, rows, :] - delta[h][rows])).astype(BF16)
            dqs = [_dot(ds_scr[h], kh[h]) * scale for h in two]
            dks = [dk0 + _dot_tn(ds_scr[0], qh[0]), dk1 + _dot_tn(ds_scr[1], qh[1])]
            dv = dv + _dot_tn(p_scr[0], doh[0]) + _dot_tn(p_scr[1], doh[1])
            for h in two:
                dq_ref[pl.ds(off, tq), h * LANES:(h + 1) * LANES] += dqs[h]
            return dks[0], dks[1], dv

        zero = jnp.zeros((tk, LANES), F32)
        carry = step(i0, (zero, zero, zero), True)
        dk0, dk1, dv = lax.fori_loop(i0 + 1, nq, lambda i, c: step(i, c, False), carry)
        dk_ref[:, 0:LANES] = dk0 * scale
        dk_ref[:, LANES:2 * LANES] = dk1 * scale
        dv_ref[...] = dv.astype(BF16)

    seq1 = pl.BlockSpec((None, S, LANES), lambda b, p, j: (b, 0, p))
    seq2 = pl.BlockSpec((None, S, 2 * LANES), lambda b, p, j: (b, 0, p))
    return pl.pallas_call(
        body, name=name, grid=(Bl, 3, S // tk),
        in_specs=[seq2, pl.BlockSpec((None, tk, 2 * LANES), lambda b, p, j: (b, j, p)),
                  pl.BlockSpec((None, tk, LANES), lambda b, p, j: (b, j, vblk0 + p)), seq1, seq1,
                  pl.BlockSpec((None, S, LANES), lambda b, p, j: (b, 0, do_blk0 + p))],
        out_specs=[seq2, pl.BlockSpec((None, tk, 2 * LANES), lambda b, p, j: (b, j, p)), pl.BlockSpec((None, tk, LANES), lambda b, p, j: (b, j, p))],
        out_shape=[jax.ShapeDtypeStruct((Bl, S, 6 * LANES), F32), jax.ShapeDtypeStruct((Bl, S, 6 * LANES), F32), jax.ShapeDtypeStruct((Bl, S, 3 * LANES), BF16)],
        scratch_shapes=[pltpu.VMEM((2, tq, tk), F32), pltpu.VMEM((2, tq, tk), F32), pltpu.VMEM((2, tq, tk), BF16), pltpu.VMEM((2, tq, tk), BF16)],
        compiler_params=_cp("parallel", "parallel", "arbitrary"),
    )(q3, k3, kv3, o3, lse3, do3)


def _bucket_table():
    a = jnp.arange(WINDOW)[:, None]
    b = jnp.arange(2 * WINDOW)[None, :]
    dist = WINDOW + a - b
    max_exact = REL_BUCKETS // 2
    n = jnp.maximum(dist, 0)
    nf = jnp.maximum(n, 1).astype(F32)
    large = max_exact + (jnp.log(nf / max_exact) / math.log(REL_MAX_DIST / max_exact) * (REL_BUCKETS - max_exact)).astype(jnp.int32)
    large = jnp.minimum(large, REL_BUCKETS - 1)
    bucket = jnp.where(n < max_exact, n, large)
    return jnp.where((dist >= 0) & (dist < WINDOW), bucket, -1).astype(jnp.int32)


def swa_bias(rel_flat, bucket, *, name):
    def body(t_ref, b_ref, o_ref):
        bk = b_ref[...]
        for p in range(3):
            for hh in range(2):
                h = hh * 3 + p
                acc = jnp.full(bk.shape, NEG, F32)
                for b in range(REL_BUCKETS):
                    acc = jnp.where(bk == b, t_ref[b * 6 + h], acc)
                o_ref[p, hh] = acc

    return pl.pallas_call(
        body, name=name,
        in_specs=[pl.BlockSpec(memory_space=pltpu.SMEM), pl.BlockSpec(memory_space=pltpu.VMEM)],
        out_specs=pl.BlockSpec(memory_space=pltpu.VMEM),
        out_shape=jax.ShapeDtypeStruct((3, 2, WINDOW, 2 * WINDOW), F32),
    )(rel_flat, bucket)


def swa_bias_bwd(dbias, bucket, *, name):
    Bl = dbias.shape[0]

    def body(d_ref, b_ref, o_ref):
        bk = b_ref[...]
        lane = lax.broadcasted_iota(jnp.int32, (1, LANES), 1)
        rows = []
        for h in range(6):
            hh, p = divmod(h, 3)
            d = d_ref[0, p, hh]
            for bl in range(1, Bl):
                d = d + d_ref[bl, p, hh]
            row = jnp.zeros((1, LANES), F32)
            for b in range(REL_BUCKETS):
                s = jnp.sum(jnp.sum(jnp.where(bk == b, d, 0.0), axis=1, keepdims=True), axis=0, keepdims=True)
                row = row + jnp.where(lane == b, s, 0.0)
            rows.append(row)
        rows += [jnp.zeros((1, LANES), F32)] * 2
        o_ref[...] = jnp.concatenate(rows, axis=0)

    return pl.pallas_call(
        body, name=name,
        in_specs=[pl.BlockSpec(memory_space=pltpu.VMEM)] * 2, out_specs=pl.BlockSpec(memory_space=pltpu.VMEM),
        out_shape=jax.ShapeDtypeStruct((8, LANES), F32),
    )(dbias, bucket)


SWA_QBLOCKS = 8


def _swa_specs(vblk, nqb):
    rows = nqb * WINDOW
    cur = lambda blk: pl.BlockSpec((None, rows, LANES), lambda b, p, n: (b, n, blk))
    prev = lambda blk: pl.BlockSpec((None, WINDOW, LANES), lambda b, p, n: (b, jnp.maximum(n * nqb - 1, 0), blk))
    return [pl.BlockSpec((None, rows, LANES), lambda b, p, n: (b, n, p)), cur(0), prev(0), cur(vblk), prev(vblk),
            pl.BlockSpec((None, 2, WINDOW, 2 * WINDOW), lambda b, p, n: (p, 0, 0, 0)), pl.BlockSpec((None, 2, LANES), lambda b, p, n: (p, 0, 0))]


def _rows128(ref, m):
    return ref[m * WINDOW:(m + 1) * WINDOW, :]


def _swa_logits(qh, kp, kc, bias_h, first, scale):
    sp = jnp.where(first, NEG, _dot_nt(qh, kp) * scale + bias_h[:, :WINDOW])
    sc = _dot_nt(qh, kc) * scale + bias_h[:, WINDOW:]
    return sp, sc


def swa_attn_fwd(qn3, kn3, proj3, bias, sinks, *, plans=None, name):
    Bl, S, _ = qn3.shape
    scale = HEAD ** -0.5
    nqb = min(SWA_QBLOCKS, S // WINDOW)

    def body(q_ref, kc_ref, kp_ref, vc_ref, vp_ref, b_ref, s_ref, o_ref, lse_ref):
        seq_start = pl.program_id(2) == 0
        masks = _lane_masks()
        chains = [(m_, h) for m_ in range(nqb) for h in range(2)]
        kp = [kp_ref[...] if m_ == 0 else _rows128(kc_ref, m_ - 1) for m_ in range(nqb)]
        vp = [vp_ref[...] if m_ == 0 else _rows128(vc_ref, m_ - 1) for m_ in range(nqb)]
        kc = [_rows128(kc_ref, m_) for m_ in range(nqb)]
        vc = [_rows128(vc_ref, m_) for m_ in range(nqb)]
        sink = [s_ref[h:h + 1, 0:1] for h in range(2)]
        logits = {}
        for m_, h in chains:
            q = _rows128(q_ref, m_)
            qh = jnp.where(masks[h], q, jnp.zeros_like(q))
            logits[m_, h] = _swa_logits(qh, kp[m_], kc[m_], b_ref[h], seq_start if m_ == 0 else False, scale)
        mx = {c: jnp.maximum(jnp.maximum(jnp.max(logits[c][0], axis=1, keepdims=True), jnp.max(logits[c][1], axis=1, keepdims=True)), sink[c[1]])
              for c in chains}
        ex = {c: (jnp.exp(logits[c][0] - mx[c]), jnp.exp(logits[c][1] - mx[c])) for c in chains}
        den = {c: jnp.sum(ex[c][0], axis=1, keepdims=True) + jnp.sum(ex[c][1], axis=1, keepdims=True) + jnp.exp(sink[c[1]] - mx[c]) for c in chains}
        inv = {c: 1.0 / den[c] for c in chains}
        out = {}
        for m_, h in chains:
            c = (m_, h)
            out[c] = (_dot((ex[c][0] * inv[c]).astype(BF16), jnp.where(masks[h], vp[m_], 0.0).astype(BF16))
                      + _dot((ex[c][1] * inv[c]).astype(BF16), jnp.where(masks[h], vc[m_], 0.0).astype(BF16)))
        for m_ in range(nqb):
            o_ref[m_ * WINDOW:(m_ + 1) * WINDOW, :] = out[m_, 0] + out[m_, 1]
            lse_ref[m_ * WINDOW:(m_ + 1) * WINDOW, :] = jnp.where(masks[0], mx[m_, 0] + jnp.log(den[m_, 0]), mx[m_, 1] + jnp.log(den[m_, 1]))

    out = pl.BlockSpec((None, nqb * WINDOW, LANES), lambda b, p, n: (b, n, p))
    shp = jax.ShapeDtypeStruct((Bl, S, 3 * LANES), F32)
    return call_with_plans(
        body, plans, name=name, grid=(Bl, 3, S // (nqb * WINDOW)), in_specs=_swa_specs(P_SWV // LANES, nqb),
        out_specs=[out, out], out_shape=[shp, shp], scratch_shapes=[], args=[qn3, kn3, kn3, proj3, proj3, bias, sinks],
        sem=("arbitrary",) * 3 if plans else ("parallel", "parallel", "arbitrary"))


def swa_attn_bwd(qn3, kn3, proj3, bias, sinks, o3, lse3, do3, *, do_blk0=0, name):
    Bl, S, _ = qn3.shape
    scale = HEAD ** -0.5
    nqb = min(SWA_QBLOCKS, S // WINDOW)
    rows = nqb * WINDOW

    def body(q_ref, kc_ref, kp_ref, vc_ref, vp_ref, b_ref, s_ref, o_ref, lse_ref, do_ref,
             dq_ref, dk_ref, dv_ref, db_ref, dsk_ref):
        p_id, n = pl.program_id(1), pl.program_id(2)
        seq_start = n == 0

        @pl.when((p_id == 0) & seq_start)
        def _():
            dk_ref[...] = jnp.zeros_like(dk_ref)
            dv_ref[...] = jnp.zeros_like(dv_ref)

        @pl.when(seq_start)
        def _():
            db_ref[...] = jnp.zeros_like(db_ref)
            dsk_ref[...] = jnp.zeros_like(dsk_ref)

        masks = _lane_masks()
        zero = jnp.zeros((WINDOW, LANES), F32)
        chains = [(m_, h) for m_ in range(nqb) for h in range(2)]
        kp = [kp_ref[...] if m_ == 0 else _rows128(kc_ref, m_ - 1) for m_ in range(nqb)]
        vp = [vp_ref[...] if m_ == 0 else _rows128(vc_ref, m_ - 1) for m_ in range(nqb)]
        kc = [_rows128(kc_ref, m_) for m_ in range(nqb)]
        vc = [_rows128(vc_ref, m_) for m_ in range(nqb)]
        do_b = [_rows128(do_ref, m_).astype(BF16) for m_ in range(nqb)]
        prod = [do_b[m_].astype(F32) * _rows128(o_ref, m_) for m_ in range(nqb)]
        lse = [_rows128(lse_ref, m_) for m_ in range(nqb)]
        qh, doh, logits, lse_h, delta = {}, {}, {}, {}, {}
        for m_, h in chains:
            q = _rows128(q_ref, m_)
            qh[m_, h] = jnp.where(masks[h], q, jnp.zeros_like(q))
            doh[m_, h] = jnp.where(masks[h], do_b[m_], jnp.zeros_like(do_b[m_]))
            logits[m_, h] = _swa_logits(qh[m_, h], kp[m_], kc[m_], b_ref[h], seq_start if m_ == 0 else False, scale)
            lse_h[m_, h] = lse[m_][:, h * HEAD:h * HEAD + 1]
            delta[m_, h] = jnp.sum(jnp.where(masks[h], prod[m_], 0.0), axis=1, keepdims=True)
        pr = {c: (jnp.exp(logits[c][0] - lse_h[c]), jnp.exp(logits[c][1] - lse_h[c])) for c in chains}
        dp = {(m_, h): (_dot_nt(doh[m_, h], jnp.where(masks[h], vp[m_], 0.0).astype(BF16)),
                        _dot_nt(doh[m_, h], jnp.where(masks[h], vc[m_], 0.0).astype(BF16))) for m_, h in chains}
        ds = {c: (pr[c][0] * (dp[c][0] - delta[c]), pr[c][1] * (dp[c][1] - delta[c])) for c in chains}
        dsb = {c: ((ds[c][0] * scale).astype(BF16), (ds[c][1] * scale).astype(BF16)) for c in chains}
        dk_acc = [zero] * (nqb + 1)
        dv_acc = [zero] * (nqb + 1)
        db_acc = [[jnp.zeros((WINDOW, WINDOW), F32)] * 2 for _ in range(2)]
        dsk_acc = [jnp.zeros((1, 1), F32)] * 2
        dq = [zero] * nqb
        for m_, h in chains:
            c = (m_, h)
            db_acc[h] = [db_acc[h][0] + ds[c][0], db_acc[h][1] + ds[c][1]]
            dsk_acc[h] = dsk_acc[h] - jnp.sum(jnp.exp(s_ref[h:h + 1, 0:1] - lse_h[c]) * delta[c], axis=0, keepdims=True)
            dq[m_] = (dq[m_] + _dot(dsb[c][0], jnp.where(masks[h], kp[m_], jnp.zeros_like(kp[m_])))
                      + _dot(dsb[c][1], jnp.where(masks[h], kc[m_], jnp.zeros_like(kc[m_]))))
            dk_acc[m_] = dk_acc[m_] + _dot_tn(dsb[c][0], qh[c])
            dk_acc[m_ + 1] = dk_acc[m_ + 1] + _dot_tn(dsb[c][1], qh[c])
            dv_acc[m_] = dv_acc[m_] + _dot_tn(pr[c][0].astype(BF16), doh[c])
            dv_acc[m_ + 1] = dv_acc[m_ + 1] + _dot_tn(pr[c][1].astype(BF16), doh[c])
        for m_ in range(nqb):
            dq_ref[m_ * WINDOW:(m_ + 1) * WINDOW, :] = dq[m_]
        for h in range(2):
            db_ref[h, :, 0:WINDOW] += db_acc[h][0]
            db_ref[h, :, WINDOW:2 * WINDOW] += db_acc[h][1]
            dsk_ref[h:h + 1, :] += jnp.broadcast_to(dsk_acc[h], (1, LANES))
        offp = pl.multiple_of(jnp.maximum(n * nqb - 1, 0) * WINDOW, WINDOW)
        dk_ref[pl.ds(offp, WINDOW), :] += dk_acc[0]
        dv_ref[pl.ds(offp, WINDOW), :] += dv_acc[0]
        for m_ in range(nqb):
            off = pl.multiple_of(n * rows + m_ * WINDOW, WINDOW)
            dk_ref[pl.ds(off, WINDOW), :] += dk_acc[m_ + 1]
            dv_ref[pl.ds(off, WINDOW), :] += dv_acc[m_ + 1]

    blk = pl.BlockSpec((None, rows, LANES), lambda b, p, n: (b, n, p))
    seq = pl.BlockSpec((None, S, LANES), lambda b, p, n: (b, 0, 0))
    return pl.pallas_call(
        body, name=name, grid=(Bl, 3, S // rows),
        in_specs=_swa_specs(P_SWV // LANES, nqb) + [blk, blk, pl.BlockSpec((None, rows, LANES), lambda b, p, n: (b, n, do_blk0 + p))],
        out_specs=[blk, seq, seq, pl.BlockSpec((None, None, 2, WINDOW, 2 * WINDOW), lambda b, p, n: (b, p, 0, 0, 0)),
                   pl.BlockSpec((None, None, 2, LANES), lambda b, p, n: (b, p, 0, 0))],
        out_shape=[jax.ShapeDtypeStruct((Bl, S, 3 * LANES), F32), jax.ShapeDtypeStruct((Bl, S, LANES), F32), jax.ShapeDtypeStruct((Bl, S, LANES), F32),
                   jax.ShapeDtypeStruct((Bl, 3, 2, WINDOW, 2 * WINDOW), F32), jax.ShapeDtypeStruct((Bl, 3, 2, LANES), F32)],
        compiler_params=_cp("arbitrary", "arbitrary", "arbitrary"),
    )(qn3, kn3, kn3, proj3, proj3, bias, sinks, o3, lse3, do3)


CONV_ROWS = 64
CONV_LANES = 128


def _conv_strip(x_ref, h_ref, w, b, r0, cols, first_blk):
    x = x_ref[r0:r0 + CONV_ROWS, cols]
    if r0 == 0:
        rows = lax.broadcasted_iota(jnp.int32, x.shape, 0)
        h6 = jnp.where(first_blk, 0.0, h_ref[6:7, cols])
        h7 = jnp.where(first_blk, 0.0, h_ref[7:8, cols])
        x1 = jnp.where(rows == 0, h7, pltpu.roll(x, 1, 0))
        x2 = jnp.where(rows == 0, h6, jnp.where(rows == 1, h7, pltpu.roll(x, 2, 0)))
    else:
        x1 = x_ref[r0 - 1:r0 - 1 + CONV_ROWS, cols]
        x2 = x_ref[r0 - 2:r0 - 2 + CONV_ROWS, cols]
    return w[0:1] * x2 + w[1:2] * x1 + w[2:3] * x + b, x, x1, x2


FF_BLK = D_FF // 2


def _up_perm(a):
    q = FF_BLK
    return _cat([a[..., 0:q], a[..., 2 * q:3 * q], a[..., q:2 * q], a[..., 3 * q:4 * q]])


def conv_gate_fwd(up3, cw, cb, *, tm=512, name):
    Bl, S, _ = up3.shape
    tm = min(tm, S)
    W = 2 * FF_BLK

    def body(x_ref, h_ref, w_ref, b_ref, o_ref):
        first = pl.program_id(1) == 0

        def chunk(c, carry):
            cg = pl.ds(pl.multiple_of(c * CONV_LANES, CONV_LANES), CONV_LANES)
            cv = pl.ds(pl.multiple_of(FF_BLK + c * CONV_LANES, CONV_LANES), CONV_LANES)
            wg, wv, bg, bv = w_ref[:, cg], w_ref[:, cv], b_ref[:, cg], b_ref[:, cv]
            for r0 in range(0, tm, CONV_ROWS):
                ug = _conv_strip(x_ref, h_ref, wg, bg, r0, cg, first)[0]
                uv = _conv_strip(x_ref, h_ref, wv, bv, r0, cv, first)[0]
                o_ref[r0:r0 + CONV_ROWS, cg] = (ug * jax.nn.sigmoid(ug) * uv).astype(BF16)
            return carry

        lax.fori_loop(0, FF_BLK // CONV_LANES, chunk, 0)

    hb = tm // 8
    return pl.pallas_call(
        body, name=name, grid=(Bl, S // tm, 2),
        in_specs=[pl.BlockSpec((None, tm, W), lambda b, s, c: (b, s, c)),
                  pl.BlockSpec((None, 8, W), lambda b, s, c: (b, jnp.maximum(s * hb - 1, 0), c)),
                  pl.BlockSpec((3, W), lambda b, s, c: (0, c)), pl.BlockSpec((1, W), lambda b, s, c: (0, c))],
        out_specs=pl.BlockSpec((None, tm, FF_BLK), lambda b, s, c: (b, s, c)),
        out_shape=jax.ShapeDtypeStruct((Bl, S, D_FF), BF16),
        compiler_params=_cp("parallel", "parallel", "parallel"),
    )(up3, up3, cw, cb)


def conv_gate_bwd(up3, cw, cb, da3, *, tm=512, name):
    Bl, S, _ = up3.shape
    tm = min(tm, S)
    ns = S // tm
    W = 2 * FF_BLK

    def body(x_ref, h_ref, w_ref, b_ref, da_ref, dup_ref, dw_ref, nxt_ref, du_scr):
        b, s = pl.program_id(1), pl.program_id(2)
        seq_end = s == 0
        first = s == ns - 1

        @pl.when((b == 0) & seq_end)
        def _():
            dw_ref[...] = jnp.zeros_like(dw_ref)

        def du_chunk(c, carry):
            cg = pl.ds(pl.multiple_of(c * CONV_LANES, CONV_LANES), CONV_LANES)
            cv = pl.ds(pl.multiple_of(FF_BLK + c * CONV_LANES, CONV_LANES), CONV_LANES)
            wg, wv, bg, bv = w_ref[:, cg], w_ref[:, cv], b_ref[:, cg], b_ref[:, cv]
            acc_g = [jnp.zeros((1, CONV_LANES), F32)] * 4
            acc_v = [jnp.zeros((1, CONV_LANES), F32)] * 4
            for r0 in range(0, tm, CONV_ROWS):
                ug, xg, xg1, xg2 = _conv_strip(x_ref, h_ref, wg, bg, r0, cg, first)
                uv, xv, xv1, xv2 = _conv_strip(x_ref, h_ref, wv, bv, r0, cv, first)
                da = da_ref[r0:r0 + CONV_ROWS, cg].astype(F32)
                sg = jax.nn.sigmoid(ug)
                dug = da * uv * sg * (1.0 + ug * (1.0 - sg))
                duv = da * ug * sg
                du_scr[r0:r0 + CONV_ROWS, cg] = dug
                du_scr[r0:r0 + CONV_ROWS, cv] = duv
                col = lambda t: jnp.sum(t, axis=0, keepdims=True)
                acc_g = [acc_g[0] + col(dug * xg2), acc_g[1] + col(dug * xg1), acc_g[2] + col(dug * xg), acc_g[3] + col(dug)]
                acc_v = [acc_v[0] + col(duv * xv2), acc_v[1] + col(duv * xv1), acc_v[2] + col(duv * xv), acc_v[3] + col(duv)]
            for t in range(4):
                dw_ref[t:t + 1, cg] += acc_g[t]
                dw_ref[t:t + 1, cv] += acc_v[t]
            return carry

        lax.fori_loop(0, FF_BLK // CONV_LANES, du_chunk, 0)
        du_scr[tm:tm + 8, :] = jnp.where(seq_end, 0.0, nxt_ref[...])

        def dup_chunk(c, carry):
            cols = pl.ds(pl.multiple_of(c * CONV_LANES, CONV_LANES), CONV_LANES)
            w = w_ref[:, cols]
            for r0 in range(0, tm, CONV_ROWS):
                d0 = du_scr[r0:r0 + CONV_ROWS, cols]
                d1 = du_scr[r0 + 1:r0 + 1 + CONV_ROWS, cols]
                d2 = du_scr[r0 + 2:r0 + 2 + CONV_ROWS, cols]
                dup_ref[r0:r0 + CONV_ROWS, cols] = (w[2:3] * d0 + w[1:2] * d1 + w[0:1] * d2).astype(BF16)
            return carry

        lax.fori_loop(0, W // CONV_LANES, dup_chunk, 0)
        nxt_ref[...] = du_scr[0:8, :]

    hb = tm // 8
    rb = lambda s: ns - 1 - s
    return pl.pallas_call(
        body, name=name, grid=(2, Bl, ns),
        in_specs=[pl.BlockSpec((None, tm, W), lambda c, b, s: (b, rb(s), c)),
                  pl.BlockSpec((None, 8, W), lambda c, b, s: (b, jnp.maximum(rb(s) * hb - 1, 0), c)),
                  pl.BlockSpec((3, W), lambda c, b, s: (0, c)), pl.BlockSpec((1, W), lambda c, b, s: (0, c)),
                  pl.BlockSpec((None, tm, FF_BLK), lambda c, b, s: (b, rb(s), c))],
        out_specs=[pl.BlockSpec((None, tm, W), lambda c, b, s: (b, rb(s), c)), pl.BlockSpec((8, W), lambda c, b, s: (0, c))],
        out_shape=[jax.ShapeDtypeStruct((Bl, S, 2 * D_FF), BF16), jax.ShapeDtypeStruct((8, 2 * D_FF), F32)],
        scratch_shapes=[pltpu.VMEM((8, W), F32), pltpu.VMEM((tm + 8, W), F32)],
        compiler_params=_cp("arbitrary", "arbitrary", "arbitrary"),
    )(up3, up3, cw, cb, da3)


def cast_layer(w3, l, *, name):
    _, R, C = w3.shape
    tr = _tile(R, 512, 16)

    def body(w_ref, o_ref):
        o_ref[...] = w_ref[...].astype(BF16)

    return pl.pallas_call(
        body, name=name, grid=(R // tr,), in_specs=[pl.BlockSpec((None, tr, C), lambda i: (l, i, 0))],
        out_specs=pl.BlockSpec((tr, C), lambda i: (i, 0)), out_shape=jax.ShapeDtypeStruct((R, C), BF16),
        compiler_params=_cp("parallel"),
    )(w3)


def gate_bwd(dx3, y3, gate, *, tm=512, name):
    Bl, S, D = dx3.shape
    tm = min(tm, S)

    def body(dx_ref, y_ref, g_ref, o_ref, dg_ref):
        @pl.when(pl.program_id(1) == 0)
        def _():
            dg_ref[...] = jnp.zeros_like(dg_ref)

        dx = dx_ref[...]
        dg_ref[...] += jnp.sum(dx * y_ref[...], axis=0, keepdims=True)
        o_ref[...] = (dx * g_ref[...]).astype(BF16)

    blk = pl.BlockSpec((None, tm, D), lambda b, s: (b, s, 0))
    vec = pl.BlockSpec((None, 1, D), lambda b, s: (b, 0, 0))
    return pl.pallas_call(
        body, name=name, grid=(Bl, S // tm), in_specs=[blk, blk, vec], out_specs=[blk, vec],
        out_shape=[jax.ShapeDtypeStruct((Bl, S, D), BF16), jax.ShapeDtypeStruct((Bl, 1, D), F32)],
        compiler_params=_cp("parallel", "arbitrary"),
    )(dx3, y3, gate)


def loss_grad(y3, t3, *, tm=512, name):
    Bl, S, D = y3.shape
    tm = min(tm, S)
    last = (Bl - 1, S // tm - 1)

    def body(y_ref, t_ref, dy_ref, l_ref, acc_ref):
        b, s = pl.program_id(0), pl.program_id(1)

        @pl.when((b == 0) & (s == 0))
        def _():
            acc_ref[...] = jnp.zeros_like(acc_ref)

        e = y_ref[...] - t_ref[...]
        dy_ref[...] = e * (1.0 / D)
        acc_ref[...] += jnp.sum(e * e, axis=0, keepdims=True)

        @pl.when((b == last[0]) & (s == last[1]))
        def _():
            l_ref[...] = jnp.broadcast_to(jnp.sum(acc_ref[...], axis=1, keepdims=True) * (0.5 / D), (1, LANES))

    blk = pl.BlockSpec((None, tm, D), lambda b, s: (b, s, 0))
    return pl.pallas_call(
        body, name=name, grid=(Bl, S // tm), in_specs=[blk, blk],
        out_specs=[blk, pl.BlockSpec((1, LANES), lambda b, s: (0, 0))],
        out_shape=[jax.ShapeDtypeStruct((Bl, S, D), F32), jax.ShapeDtypeStruct((1, LANES), F32)],
        scratch_shapes=[pltpu.VMEM((1, D), F32)], compiler_params=_cp("arbitrary", "arbitrary"),
    )(y3, t3)


def adamw(w, g, m, v, *, name):
    L, R, C = w.shape
    tr = _tile(R, 512, 8)

    def body(w_ref, g_ref, m_ref, v_ref, d_ref, m2_ref, v2_ref):
        d_ref[...], m2_ref[...], v2_ref[...] = _adam_update(w_ref[...], g_ref[...], m_ref[...], v_ref[...])

    blk = pl.BlockSpec((None, tr, C), lambda l, i: (l, i, 0))
    shp = jax.ShapeDtypeStruct((L, R, C), F32)
    return pl.pallas_call(
        body, name=name, grid=(L, R // tr), in_specs=[blk] * 4, out_specs=[blk] * 3, out_shape=[shp] * 3,
        compiler_params=_cp("parallel", "parallel"),
    )(w, g, m, v)


def _adam_update(w, g, m, v):
    c1 = 1.0 / (1.0 - ADAM_B1 ** ADAM_STEP)
    c2 = 1.0 / (1.0 - ADAM_B2 ** ADAM_STEP)
    m2 = ADAM_B1 * m + (1.0 - ADAM_B1) * g
    v2 = ADAM_B2 * v + (1.0 - ADAM_B2) * (g * g)
    return -ADAM_LR * ((m2 * c1) / (jnp.sqrt(v2 * c2) + ADAM_EPS) + ADAM_WD * w), m2, v2


def adamw_small(ws, gs, ms, vs, *, name):
    na = len(ws)

    def body(*refs):
        w_r, g_r, m_r, v_r = (refs[i * na:(i + 1) * na] for i in range(4))
        d_r, m2_r, v2_r = (refs[(4 + i) * na:(5 + i) * na] for i in range(3))
        for a in range(na):
            d_r[a][...], m2_r[a][...], v2_r[a][...] = _adam_update(w_r[a][...], g_r[a][...], m_r[a][...], v_r[a][...])

    vm = pl.BlockSpec(memory_space=pltpu.VMEM)
    shp = [jax.ShapeDtypeStruct(w.shape, F32) for w in ws]
    out = pl.pallas_call(body, name=name, in_specs=[vm] * (4 * na), out_specs=[vm] * (3 * na), out_shape=shp * 3)(*ws, *gs, *ms, *vs)
    return out[:na], out[na:2 * na], out[2 * na:]


def sum_small(xs, *, name):
    na = len(xs)

    def body(*refs):
        for x_ref, o_ref in zip(refs[:na], refs[na:]):
            acc = x_ref[0]
            for k in range(1, x_ref.shape[0]):
                acc = acc + x_ref[k]
            o_ref[...] = acc

    vm = pl.BlockSpec(memory_space=pltpu.VMEM)
    return pl.pallas_call(body, name=name, in_specs=[vm] * na, out_specs=[vm] * na,
                          out_shape=[jax.ShapeDtypeStruct(x.shape[1:], x.dtype) for x in xs])(*xs)


def pair_add_half(g4, recv, c_arr, *, tr=512, name):
    _, R, C = g4.shape
    H = R // 2
    tr = _tile(H, tr, 16)
    nb = H // tr

    def body(c_ref, g_ref, r_ref, o_ref):
        o_ref[...] = (g_ref[...].astype(F32) + r_ref[...].astype(F32)).astype(BF16)

    grid_spec = pltpu.PrefetchScalarGridSpec(
        num_scalar_prefetch=1, grid=(4, nb),
        in_specs=[pl.BlockSpec((None, tr, C), lambda k, i, c_ref: (k, c_ref[0] * nb + i, 0)),
                  pl.BlockSpec((None, tr, C), lambda k, i, c_ref: (k, i, 0))],
        out_specs=pl.BlockSpec((None, tr, C), lambda k, i, c_ref: (k, i, 0)),
    )
    return pl.pallas_call(
        body, name=name, grid_spec=grid_spec, out_shape=jax.ShapeDtypeStruct((4, H, C), BF16),
        compiler_params=_cp("parallel", "parallel"),
    )(c_arr, g4, recv)


def chip_sum_into(landed, pair, sel, *, tr=512, name):
    _, H, C = landed.shape
    tr = _tile(H, tr, 16)
    nb = H // tr

    def body(s_ref, l0, l1, l2, l3, p_ref, o_ref):
        own = p_ref[...].astype(F32)
        acc = None
        for k, l_ref in enumerate((l0, l1, l2, l3)):
            part = jnp.where(s_ref[0] == k, own, l_ref[...].astype(F32))
            acc = part if acc is None else acc + part
        o_ref[...] = acc

    def slot(k):
        return pl.BlockSpec((None, tr, C), lambda i, s: (jnp.where(s[0] == k, (k + 1) % 4, k), i, 0))

    grid_spec = pltpu.PrefetchScalarGridSpec(
        num_scalar_prefetch=1, grid=(nb,),
        in_specs=[slot(0), slot(1), slot(2), slot(3), pl.BlockSpec((None, tr, C), lambda i, s: (s[0], i, 0))],
        out_specs=pl.BlockSpec((tr, C), lambda i, s: (s[1] * nb + i, 0)),
    )
    return pl.pallas_call(
        body, name=name, grid_spec=grid_spec, out_shape=jax.ShapeDtypeStruct((2 * H, C), F32), compiler_params=_cp("parallel"),
    )(sel, landed, landed, landed, landed, pair)


def mods_matmul(c_all, w_ada, b_ada_cols, *, tn=512, name):
    L, D, E = w_ada.shape
    nb = c_all.shape[0]
    tn = _tile(E, tn)

    def body(c_ref, w_ref, b_ref, o_ref):
        c = c_ref[...]
        a = c * jax.nn.sigmoid(c)
        o_ref[...] = jnp.dot(a, w_ref[...], preferred_element_type=F32, precision=lax.Precision.HIGHEST) + b_ref[...]

    return pl.pallas_call(
        body, name=name, grid=(L, E // tn),
        in_specs=[pl.BlockSpec((nb, D), lambda l, j: (0, 0)), pl.BlockSpec((None, D, tn), lambda l, j: (l, 0, j)),
                  pl.BlockSpec((None, 1, tn), lambda l, j: (l, 0, j))],
        out_specs=pl.BlockSpec((None, nb, tn), lambda l, j: (l, 0, j)),
        out_shape=jax.ShapeDtypeStruct((L, nb, E), F32), compiler_params=_cp("parallel", "parallel"),
    )(c_all, w_ada, b_ada_cols)


def ada_grad(c_all, dmods, *, tn=512, name):
    L, nb, E = dmods.shape
    D = c_all.shape[1]
    tn = _tile(E, tn)

    def body(c_ref, d_ref, o_ref):
        c = c_ref[...]
        a = c * jax.nn.sigmoid(c)
        o_ref[...] = lax.dot_general(a, d_ref[...], (((0,), (0,)), ((), ())), preferred_element_type=F32, precision=lax.Precision.HIGHEST)

    return pl.pallas_call(
        body, name=name, grid=(L, E // tn),
        in_specs=[pl.BlockSpec((nb, D), lambda l, j: (0, 0)), pl.BlockSpec((None, nb, tn), lambda l, j: (l, 0, j))],
        out_specs=pl.BlockSpec((None, D, tn), lambda l, j: (l, 0, j)),
        out_shape=jax.ShapeDtypeStruct((L, D, E), F32), compiler_params=_cp("parallel", "parallel"),
    )(c_all, dmods)


HBM = pl.BlockSpec(memory_space=pltpu.HBM)


def _me():
    return lax.axis_index("x"), lax.axis_index("y"), lax.axis_index("c")


def _flip(v, bit):
    return 1 - v if bit else v


def allgather8(xs, *, name):
    na = len(xs)

    def body(*refs):
        x_refs, out_refs = refs[:na], refs[na:2 * na]
        send_sems, recv_sems = refs[2 * na], refs[2 * na + 1]
        x, y, c = _me()
        me = 4 * x + 2 * y + c
        for x_ref, out_ref in zip(x_refs, out_refs):
            out_ref[me] = x_ref[...]
        sends = []
        for a, (x_ref, out_ref) in enumerate(zip(x_refs, out_refs)):
            for k in range(1, 8):
                peer = (_flip(x, k & 4), _flip(y, k & 2), _flip(c, k & 1))
                cp = pltpu.make_async_remote_copy(src_ref=x_ref, dst_ref=out_ref.at[me], send_sem=send_sems.at[a, k - 1],
                                                  recv_sem=recv_sems.at[a, k - 1], device_id=peer, device_id_type=MESH)
                cp.start()
                sends.append(cp)
        for a, (x_ref, out_ref) in enumerate(zip(x_refs, out_refs)):
            for k in range(1, 8):
                peer = (_flip(x, k & 4), _flip(y, k & 2), _flip(c, k & 1))
                src = 4 * peer[0] + 2 * peer[1] + peer[2]
                pltpu.make_async_remote_copy(src_ref=x_ref, dst_ref=out_ref.at[src], send_sem=send_sems.at[a, k - 1],
                                             recv_sem=recv_sems.at[a, k - 1], device_id=peer, device_id_type=MESH).wait_recv()
        for cp in sends:
            cp.wait_send()

    vm = pl.BlockSpec(memory_space=pltpu.VMEM)
    return pl.pallas_call(
        body, name=name, in_specs=[vm] * na, out_specs=[vm] * na,
        out_shape=[jax.ShapeDtypeStruct((8,) + a.shape, a.dtype) for a in xs],
        scratch_shapes=[pltpu.SemaphoreType.DMA((na, 7)), pltpu.SemaphoreType.DMA((na, 7))],
    )(*xs)


class _Plan:
    def __init__(self, ins, out_shapes, ncopies, copies, aliased=False):
        self.ins, self.out_shapes, self.ncopies, self.copies, self.aliased = list(ins), list(out_shapes), ncopies, copies, aliased

    def start(self, in_refs, out_refs, send_sems, recv_sems):
        sends, _ = self.copies(in_refs, out_refs, send_sems, recv_sems)
        for cp in sends:
            cp.start()

    def finish(self, in_refs, out_refs, send_sems, recv_sems):
        sends, recvs = self.copies(in_refs, out_refs, send_sems, recv_sems)
        for cp in recvs:
            cp.wait_recv()
        for cp in sends:
            cp.wait_send()


def _rcopy(src, dst, send_sems, recv_sems, idx, dev):
    return pltpu.make_async_remote_copy(src_ref=src, dst_ref=dst, send_sem=send_sems.at[idx], recv_sem=recv_sems.at[idx],
                                        device_id=dev, device_id_type=MESH)


def _other_chips(x, y):
    return [(_flip(x, k & 2), _flip(y, k & 1)) for k in range(1, 4)]


def plan_gather_ici(ws):
    def copies(in_refs, out_refs, ss, rs):
        x, y, c = _me()
        j = 2 * x + y
        sends, recvs = [], []
        for a, (x_ref, out_ref) in enumerate(zip(in_refs, out_refs)):
            H = x_ref.shape[0] // 2
            for k, (px, py) in enumerate(_other_chips(x, y)):
                sends.append(_rcopy(x_ref.at[pl.ds(c * H, H)], out_ref.at[j, pl.ds(c * H, H)], ss, rs, 3 * a + k, (px, py, c)))
                slot = out_ref.at[2 * px + py, pl.ds(c * H, H)]
                recvs.append(_rcopy(slot, slot, ss, rs, 3 * a + k, (px, py, c)))
        return sends, recvs

    return _Plan(ws, [jax.ShapeDtypeStruct((4,) + w.shape, w.dtype) for w in ws], 3 * len(ws), copies)


def plan_gather_d2d(w4s):
    def copies(in_refs, out_refs, ss, rs):
        x, y, c = _me()
        sends, recvs = [], []
        for a, out_ref in enumerate(out_refs):
            H = out_ref.shape[1] // 2
            for k, (px, py) in enumerate(_other_chips(x, y)):
                mine = out_ref.at[2 * px + py, pl.ds(c * H, H)]
                theirs = out_ref.at[2 * px + py, pl.ds((1 - c) * H, H)]
                sends.append(_rcopy(mine, mine, ss, rs, 3 * a + k, (x, y, 1 - c)))
                recvs.append(_rcopy(theirs, theirs, ss, rs, 3 * a + k, (x, y, 1 - c)))
        return sends, recvs

    return _Plan(w4s, [jax.ShapeDtypeStruct(w.shape, w.dtype) for w in w4s], 3 * len(w4s), copies, aliased=True)


def plan_swap_halves(gs):
    def copies(in_refs, out_refs, ss, rs):
        x, y, c = _me()
        sends, recvs = [], []
        for a, (g_ref, out_ref) in enumerate(zip(in_refs, out_refs)):
            H = g_ref.shape[1] // 2
            for k in range(4):
                sends.append(_rcopy(g_ref.at[k, pl.ds((1 - c) * H, H)], out_ref.at[k], ss, rs, 4 * a + k, (x, y, 1 - c)))
                recvs.append(_rcopy(g_ref.at[k, pl.ds(c * H, H)], out_ref.at[k], ss, rs, 4 * a + k, (x, y, 1 - c)))
        return sends, recvs

    return _Plan(gs, [jax.ShapeDtypeStruct((4, g.shape[1] // 2, g.shape[2]), g.dtype) for g in gs], 4 * len(gs), copies)


def plan_scatter_ici(ps):
    def copies(in_refs, out_refs, ss, rs):
        x, y, c = _me()
        j = 2 * x + y
        sends, recvs = [], []
        for a, (p_ref, out_ref) in enumerate(zip(in_refs, out_refs)):
            for k, (px, py) in enumerate(_other_chips(x, y)):
                sends.append(_rcopy(p_ref.at[2 * px + py], out_ref.at[j], ss, rs, 3 * a + k, (px, py, c)))
                slot = out_ref.at[2 * px + py]
                recvs.append(_rcopy(slot, slot, ss, rs, 3 * a + k, (px, py, c)))
        return sends, recvs

    return _Plan(ps, [jax.ShapeDtypeStruct(p.shape, p.dtype) for p in ps], 3 * len(ps), copies)


def plan_join_halves(fulls):
    def copies(in_refs, out_refs, ss, rs):
        x, y, c = _me()
        sends, recvs = [], []
        for a, out_ref in enumerate(out_refs):
            H = out_ref.shape[0] // 2
            mine, theirs = out_ref.at[pl.ds(c * H, H)], out_ref.at[pl.ds((1 - c) * H, H)]
            sends.append(_rcopy(mine, mine, ss, rs, a, (x, y, 1 - c)))
            recvs.append(_rcopy(theirs, theirs, ss, rs, a, (x, y, 1 - c)))
        return sends, recvs

    return _Plan(fulls, [jax.ShapeDtypeStruct(f.shape, f.dtype) for f in fulls], len(fulls), copies, aliased=True)


def call_with_plans(body, plans, *, grid, in_specs, out_specs, out_shape, scratch_shapes, args, sem, name):
    plans = list(plans or [])
    n_in, n_out, n_scr = len(in_specs), len(out_specs), len(scratch_shapes)
    c_in = [len(p.ins) for p in plans]
    c_out = [len(p.out_shapes) for p in plans]
    steps = math.prod(grid) if grid else 1

    def wrapped(*refs):
        pos = 0

        def take(n):
            nonlocal pos
            out = refs[pos:pos + n]
            pos += n
            return out

        ins = take(n_in)
        cins = [take(n) for n in c_in]
        outs = take(n_out)
        couts = [take(n) for n in c_out]
        scr = take(n_scr)
        sems = [take(2) for _ in plans]
        def start_all():
            for p, ci, co, (ss, rs) in zip(plans, cins, couts, sems):
                p.start(ci, co, ss, rs)

        def finish_all():
            for p, ci, co, (ss, rs) in zip(plans, cins, couts, sems):
                p.finish(ci, co, ss, rs)

        if plans and grid:
            idx = 0
            for ax, g in enumerate(grid):
                idx = idx * g + pl.program_id(ax)
            pl.when(idx == 0)(start_all)
        elif plans:
            start_all()
        if body is not None:
            body(*ins, *outs, *scr)
        if plans and grid:
            pl.when(idx == steps - 1)(finish_all)
        elif plans:
            finish_all()

    aliases = {}
    i_pos, o_pos = n_in, n_out
    for p, ni, no in zip(plans, c_in, c_out):
        if p.aliased:
            aliases.update({i_pos + t: o_pos + t for t in range(ni)})
        i_pos += ni
        o_pos += no
    kwargs = dict(grid=grid) if grid else {}
    if aliases:
        kwargs["input_output_aliases"] = aliases
    res = pl.pallas_call(
        wrapped, name=name, in_specs=list(in_specs) + [HBM] * sum(c_in), out_specs=list(out_specs) + [HBM] * sum(c_out),
        out_shape=list(out_shape) + [s for p in plans for s in p.out_shapes],
        scratch_shapes=list(scratch_shapes) + [pltpu.SemaphoreType.DMA((p.ncopies,)) for p in plans for _ in range(2)],
        compiler_params=_cp(*sem) if grid else pltpu.CompilerParams(vmem_limit_bytes=VMEM_LIMIT), **kwargs,
    )(*args, *[a for p in plans for a in p.ins])
    res = list(res)
    comp, rest = res[:n_out], res[n_out:]
    pouts = []
    for no in c_out:
        pouts.append(rest[:no])
        rest = rest[no:]
    return comp, pouts


def run_plans(plans, *, name):
    return call_with_plans(None, plans, grid=(), in_specs=[], out_specs=[], out_shape=[], scratch_shapes=[], args=[], sem=(), name=name)[1]


def _cat(parts, axis=-1):
    return jnp.concatenate(parts, axis=axis)


def _pairs_of_heads(a, axis, inverse=False):
    lead, tail = a.shape[:axis], a.shape[axis + 1:]
    split = (3, 2) if inverse else (2, 3)
    a = a.reshape(lead + split + (HEAD,) + tail)
    return jnp.swapaxes(a, axis, axis + 1).reshape(lead + (6 * HEAD,) + tail)


def _prep_w_in(w):
    z = lambda n: jnp.zeros((w.shape[0], n), w.dtype)
    return _cat([w[:, 0:1152], z(64), w[:, 1152:1184], z(32), _pairs_of_heads(w[:, 1184:1568], 1), w[:, 1568:1824]])


def _unprep_w_in(g):
    return _cat([g[:, 0:1152], g[:, 1216:1248], _pairs_of_heads(g[:, P_SWQ:P_SWK], 1, inverse=True), g[:, P_SWK:P_END]])


def _prep_w_uq(w):
    r = w.shape[0]
    return jnp.pad(w.reshape(r, 6, MLA_QK), ((0, 0), (0, 0), (0, LANES - MLA_QK))).reshape(r, 6 * LANES)


def _unprep_w_uq(g):
    r = g.shape[0]
    return g.reshape(r, 6, LANES)[:, :, :MLA_QK].reshape(r, 6 * MLA_QK)


def _prep_w_ukv(w):
    r = w.shape[0]
    w3 = w.reshape(r, 6, LANES)
    k = jnp.pad(w3[:, :, :HEAD], ((0, 0), (0, 0), (0, LANES - HEAD))).reshape(r, 6 * LANES)
    return _cat([k, w3[:, :, HEAD:].reshape(r, 6 * HEAD)])


def _unprep_w_ukv(g):
    r = g.shape[0]
    k = g[:, :6 * LANES].reshape(r, 6, LANES)[:, :, :HEAD]
    return _cat([k, g[:, 6 * LANES:].reshape(r, 6, HEAD)], axis=2).reshape(r, 6 * LANES)


def _prep_w_out(w):
    return _cat([w[0:640], _pairs_of_heads(w[640:], 0)], axis=0)


def _unprep_w_out(g):
    return _cat([g[0:640], _pairs_of_heads(g[640:], 0, inverse=True)], axis=0)


def _rope_tables(positions):
    half = 16
    inv_freq = jnp.power(ROPE_THETA, -jnp.arange(half, dtype=F32) / half)
    ang = positions.astype(F32)[..., None] * inv_freq
    cos, sin = jnp.cos(ang), jnp.sin(ang)
    z = lambda n: jnp.zeros(ang.shape[:-1] + (n,), F32)
    return (_cat([jnp.ones(ang.shape[:-1] + (HEAD,), F32), cos, cos, z(32)]), _cat([z(HEAD), -sin, z(16), z(32)]), _cat([z(HEAD), z(16), sin, z(32)]))


def _small_params(p):
    pad96 = lambda g: _cat([g, jnp.zeros((32,), F32)]).reshape(1, LANES)
    two = lambda g: _cat([g, g]).reshape(1, LANES)
    sinks = jnp.broadcast_to(p["sw_sinks"].reshape(2, 3).T[:, :, None], (3, 2, LANES))
    return dict(n1=p["norm1_g"].reshape(1, -1), n2=p["norm2_g"].reshape(1, -1), cq_g=p["mla_cq_g"].reshape(1, -1),
                ckv_g=p["mla_ckv_g"].reshape(1, -1), qn_g=pad96(p["mla_qn_g"]), kn_g=pad96(p["mla_kn_g"]),
                swq_g=two(p["sw_qn_g"]), swk_g=two(p["sw_kn_g"]), sinks=sinks, conv_b=_up_perm(p["conv_b"]).reshape(1, -1))


class _NoFlow:
    def plans(self, tag):
        return []

    def done(self, tag, outs):
        pass

    def add(self, key, g):
        pass


def _layer_fwd(x3, md, W, tabs, bias, tag, flow=_NoFlow()):
    Bl, S, D = x3.shape
    T = Bl * S
    n = lambda s: f"{s}_{tag}"
    two = lambda a: a.reshape(T, a.shape[-1])
    three = lambda a: a.reshape(Bl, S, a.shape[-1])
    h = rms_fwd(x3, 0, D, W["n1"], md["scale1"], md["shift1"], name=n("norm1"))
    proj = three(matmul(two(h), W["w_in"], tn=1920, name=n("in_proj")))
    (o_a, rt_a), got = sb_attn_fwd(proj, plans=flow.plans(n("sb_fwd")), name=n("sb_fwd"))
    flow.done(n("sb_fwd"), got)
    cqn = rms_fwd(proj, P_CQ // 256, 256, W["cq_g"], name=n("cq_norm"))
    ckvn = rms_fwd(proj, P_CKV // LANES, LANES, W["ckv_g"], name=n("ckv_norm"))
    qb = three(matmul(two(cqn), W["w_uq"], tm=1024, tn=768, name=n("uq")))
    kvb = three(matmul(two(ckvn), W["w_ukv"], tm=1024, tn=1152, name=n("ukv")))
    q_m = rope_norm_fwd(qb, 6, W["qn_g"], tabs, name=n("q_rope"))
    k_m = rope_norm_fwd(kvb, 6, W["kn_g"], tabs, (proj, P_SLAB // LANES), name=n("k_rope"))
    (o_b, lse_b), got = mla_attn_fwd(q_m, k_m, kvb, 6, plans=flow.plans(n("mla_fwd")), name=n("mla_fwd"))
    flow.done(n("mla_fwd"), got)
    q_c = pair_rms_fwd(proj, P_SWQ // LANES, 3, W["swq_g"], name=n("swq_norm"))
    k_c = pair_rms_fwd(proj, P_SWK // LANES, 1, W["swk_g"], name=n("swk_norm"))
    (o_c, lse_c), got = swa_attn_fwd(q_c, k_c, proj, bias, W["sinks"], plans=flow.plans(n("swa_fwd")), name=n("swa_fwd"))
    flow.done(n("swa_fwd"), got)
    mix = _cat([o_a, o_b, o_c]).astype(BF16)
    att, x1 = matmul_res(two(mix), W["w_out"], two(x3), md["gate1"], S, name=n("out_proj"))
    x1 = three(x1)
    h2 = rms_fwd(x1, 0, D, W["n2"], md["scale2"], md["shift2"], name=n("norm2"))
    up = three(matmul(two(h2), W["w_up"], tm=1024, tn=1408, name=n("up_proj")))
    a = conv_gate_fwd(up, W["conv_w"], W["conv_b"], name=n("conv_gate"))
    yd, x2 = matmul_res(two(a), W["w_down"], two(x1), md["gate2"], S, name=n("down_proj"))
    saved = dict(x=x3, h=h, proj=proj, rt_a=rt_a, cqn=cqn, ckvn=ckvn, qb=qb, kvb=kvb, q_m=q_m, k_m=k_m, o_b=o_b, lse_b=lse_b,
                 q_c=q_c, k_c=k_c, o_c=o_c, lse_c=lse_c, mix=mix, att=three(att), x1=x1, h2=h2, up=up, a=a, yd=three(yd))
    return three(x2), saved


def _layer_bwd(dx2, sv, md, W, tabs, bias, tag, flow=_NoFlow()):
    Bl, S, D = dx2.shape
    T = Bl * S
    n = lambda s: f"{s}_{tag}"
    two = lambda a: a.reshape(T, a.shape[-1])
    three = lambda a: a.reshape(Bl, S, a.shape[-1])
    g = {}
    dyb, dgate2 = gate_bwd(dx2, sv["yd"], md["gate2"], name=n("gate2_bwd"))
    da = three(matmul(two(dyb), W["w_down"], tb=True, tm=1024, tn=1408, name=n("down_dx")))
    g["w_down"] = matmul(two(sv["a"]), two(dyb), ta=True, tm=256, tn=1024, out_dtype=BF16, name=n("down_dw"))
    dup, dcw = conv_gate_bwd(sv["up"], W["conv_w"], W["conv_b"], da, name=n("conv_gate_bwd"))
    dh2 = three(matmul(two(dup), W["w_up"], tb=True, tn=1024, name=n("up_dx")))
    g["w_up"] = matmul(two(sv["h2"]), two(dup), ta=True, tn=1408, out_dtype=BF16, name=n("up_dw"))
    dx1, dn2, dsc2, dsh2 = rms_bwd(sv["x1"], 0, D, dh2, W["n2"], md["scale2"], dx2, name=n("norm2_bwd"))
    dmo, dgate1 = gate_bwd(dx1, sv["att"], md["gate1"], name=n("gate1_bwd"))
    dmix = three(matmul(two(dmo), W["w_out"], tb=True, tn=1024, out_dtype=BF16, name=n("out_dx")))
    g["w_out"] = matmul(two(sv["mix"]), two(dmo), ta=True, tn=1024, out_dtype=BF16, name=n("out_dw"))
    proj = sv["proj"]
    for k in ("w_down", "w_up", "w_out"):
        flow.add((tag, k), g[k])
    (dq_a, dk_a, dv_a), got = sb_attn_bwd(proj, sv["rt_a"], dmix, do_blk0=0, plans=flow.plans(n("sb_bwd")), name=n("sb_bwd"))
    flow.done(n("sb_bwd"), got)
    dq_m, dk_m, dv_b = mla_attn_bwd(sv["q_m"], sv["k_m"], sv["kvb"], 6, sv["o_b"], sv["lse_b"], dmix, do_blk0=2, name=n("mla_bwd"))
    dqb, dqn = rope_norm_bwd(sv["qb"], 6, dq_m, W["qn_g"], tabs, name=n("q_rope_bwd"))
    dkn_x, dkn, dslab = rope_norm_bwd(sv["kvb"], 6, dk_m, W["kn_g"], tabs, (proj, P_SLAB // LANES), name=n("k_rope_bwd"))
    dkvb = _cat([dkn_x, dv_b]).astype(BF16)
    dckvn = three(matmul(two(dkvb), W["w_ukv"], tb=True, tm=1024, name=n("ukv_dx")))
    g["w_ukv"] = matmul(two(sv["ckvn"]), two(dkvb), ta=True, tn=1152, out_dtype=BF16, name=n("ukv_dw"))
    dcqn = three(matmul(two(dqb), W["w_uq"], tb=True, tm=1024, name=n("uq_dx")))
    g["w_uq"] = matmul(two(sv["cqn"]), two(dqb), ta=True, tn=768, out_dtype=BF16, name=n("uq_dw"))
    dcq, dcq_g = rms_bwd(proj, P_CQ // 256, 256, dcqn, W["cq_g"], name=n("cq_norm_bwd"))
    dckv, dckv_g = rms_bwd(proj, P_CKV // LANES, LANES, dckvn, W["ckv_g"], name=n("ckv_norm_bwd"))
    dq_c, dk_c, dv_c, dbias, dsink = swa_attn_bwd(sv["q_c"], sv["k_c"], proj, bias, W["sinks"], sv["o_c"], sv["lse_c"], dmix, do_blk0=5, name=n("swa_bwd"))
    dswq, dswq_g = pair_rms_bwd(proj, P_SWQ // LANES, 3, dq_c, W["swq_g"], name=n("swq_norm_bwd"))
    dswk, dswk_g = pair_rms_bwd(proj, P_SWK // LANES, 1, dk_c, W["swk_g"], name=n("swk_norm_bwd"))
    dproj = _cat([dq_a, dk_a, dv_a, dcq, dckv, dslab, dswq, dswk, dv_c]).astype(BF16)
    dh = three(matmul(two(dproj), W["w_in"], tb=True, tn=1024, name=n("in_dx")))
    g["w_in"] = matmul(two(sv["h"]), two(dproj), ta=True, tn=1920, tk=2048, out_dtype=BF16, name=n("in_dw"))
    dx, dn1, dsc1, dsh1 = rms_bwd(sv["x"], 0, D, dh, W["n1"], md["scale1"], dx1, name=n("norm1_bwd"))
    small = dict(n1=dn1, n2=dn2, cq_g=dcq_g, ckv_g=dckv_g, qn_g=dqn, kn_g=dkn, swq_g=dswq_g, swk_g=dswk_g, conv=dcw)
    dmods = _cat([dsh1, dsc1, dgate1, dsh2, dsc2, dgate2]).reshape(Bl, 6 * D)
    for k in ("w_ukv", "w_uq", "w_in"):
        flow.add((tag, k), g[k])
    return dx, g, small, dmods, dbias, dsink


BIG = ("w_in", "w_uq", "w_ukv", "w_out", "w_up", "w_down")
ROW_SHARDED = ("w_out", "w_down")
PREP = dict(w_in=_prep_w_in, w_uq=_prep_w_uq, w_ukv=_prep_w_ukv, w_out=_prep_w_out, w_up=_up_perm, w_down=lambda w: w)
UNPREP = dict(w_in=_unprep_w_in, w_uq=_unprep_w_uq, w_ukv=_unprep_w_ukv, w_out=_unprep_w_out, w_up=_up_perm, w_down=lambda w: w)
NCHIPS = 4


def _local_step(x, target, positions, mods, Wl, rel_flat, fwd_flow=_NoFlow(), bwd_flow=_NoFlow()):
    Bl, S, D = x.shape
    L = len(Wl)
    tabs = _rope_tables(positions)
    bucket = _bucket_table()
    bias = swa_bias(rel_flat, bucket, name="swa_bias")
    mds = []
    for l in range(L):
        parts = [mods[l, :, D * k:D * (k + 1)].reshape(Bl, 1, D) for k in range(6)]
        mds.append(dict(zip(("shift1", "scale1", "gate1", "shift2", "scale2", "gate2"), parts)))
    saved = []
    h = x
    for l in range(L):
        h, sv = _layer_fwd(h, mds[l], Wl[l], tabs, bias, f"l{l}", fwd_flow)
        saved.append(sv)
    dy, loss = loss_grad(h, target, name="loss")
    grads, smalls, dmods, dbiases, dsinks = [None] * L, [None] * L, [None] * L, [None] * L, [None] * L
    for l in reversed(range(L)):
        dy, grads[l], smalls[l], dmods[l], dbiases[l], dsinks[l] = _layer_bwd(dy, saved[l], mds[l], Wl[l], tabs, bias, f"l{l}", bwd_flow)
    drel = swa_bias_bwd(_cat(dbiases, axis=0), bucket, name="swa_bias_bwd")
    return loss, dy, grads, smalls, dmods, dsinks, drel


ATT = ("w_in", "w_uq", "w_ukv", "w_out")
FFN = ("w_up", "w_down")
GATHER_STAGES = {
    "sb_fwd_l0": ([("l0", k) for k in ("w_out",) + FFN], []),
    "mla_fwd_l0": ([("l1", k) for k in ATT + ("w_up",)], [("l0", k) for k in ("w_out",) + FFN]),
    "swa_fwd_l0": ([("l1", "w_down")], [("l1", k) for k in ATT + ("w_up",)]),
    "sb_fwd_l1": ([], [("l1", "w_down")]),
}
SCATTER_STAGES = {
    "sb_bwd_l1": [("l1", k) for k in FFN],
    "sb_bwd_l0": [("l1", k) for k in ATT] + [("l0", k) for k in FFN + ("w_out",)],
}


class _GatherFlow:
    def __init__(self, shards, chip):
        self.shards, self.chip, self.ici, self.d2d, self.pending = shards, chip, {}, {}, {}

    def early(self, keys):
        ici, = run_plans([plan_gather_ici([self.shards[k] for k in keys])], name="gather_early_ici")
        d2d, = run_plans([plan_gather_d2d(ici)], name="gather_early_d2d")
        self.d2d.update(zip(keys, d2d))

    def plans(self, tag):
        ici_keys, d2d_keys = GATHER_STAGES.get(tag, ([], []))
        plans = []
        if d2d_keys:
            plans.append(plan_gather_d2d([self.ici[k] for k in d2d_keys]))
        if ici_keys:
            plans.append(plan_gather_ici([self.shards[k] for k in ici_keys]))
        self.pending[tag] = (ici_keys, d2d_keys)
        return plans

    def done(self, tag, outs):
        ici_keys, d2d_keys = self.pending.pop(tag, ([], []))
        outs = list(outs)
        if d2d_keys:
            self.d2d.update(zip(d2d_keys, outs.pop(0)))
        if ici_keys:
            self.ici.update(zip(ici_keys, outs.pop(0)))

    def weight(self, key):
        k = key[1]
        own = self.shards[key]
        r, cc = own.shape
        w4 = lax.dynamic_update_slice(self.d2d[key], own[None], (self.chip, 0, 0))
        fw = w4.reshape(NCHIPS * r, cc) if k in ROW_SHARDED else jnp.transpose(w4, (1, 0, 2)).reshape(r, NCHIPS * cc)
        return PREP[k](fw)


class _LayerWeights(dict):
    def __init__(self, small, flow, tag):
        super().__init__(small)
        self.flow, self.tag = flow, tag

    def __missing__(self, k):
        self[k] = self.flow.weight((self.tag, k))
        return self[k]


class _ScatterFlow:
    def __init__(self, shapes, sel, c_arr):
        self.shapes, self.sel, self.c_arr = shapes, sel, c_arr
        self.g, self.pairs, self.landed, self.pending = {}, {}, {}, {}

    def add(self, key, g):
        self.g[key] = g

    def _pairs(self, keys, label):
        g4s = []
        for key in keys:
            k = key[1]
            r, cc = self.shapes[k]
            gk = UNPREP[k](self.g[key])
            g4 = gk.reshape(NCHIPS, r, cc) if k in ROW_SHARDED else jnp.transpose(gk.reshape(r, NCHIPS, cc), (1, 0, 2))
            g4s.append(g4.astype(BF16))
        theirs, = run_plans([plan_swap_halves(g4s)], name=f"rs_swap_{label}")
        pairs = [pair_add_half(g4, th, self.c_arr, name=f"rs_pair_add_{key[1]}_{key[0]}") for key, g4, th in zip(keys, g4s, theirs)]
        self.pairs.update(zip(keys, pairs))
        return pairs

    def plans(self, tag):
        keys = SCATTER_STAGES.get(tag, [])
        self.pending[tag] = keys
        return [plan_scatter_ici(self._pairs(keys, tag))] if keys else []

    def done(self, tag, outs):
        keys = self.pending.pop(tag, [])
        if keys:
            self.landed.update(zip(keys, outs[0]))

    def finish(self):
        rest = [key for key in self.g if key not in self.pairs]
        if rest:
            landed, = run_plans([plan_scatter_ici(self._pairs(rest, "rest"))], name="rs_scatter_rest")
            self.landed.update(zip(rest, landed))
        keys = list(self.pairs)
        fulls = [chip_sum_into(self.landed[key], self.pairs[key], self.sel, name=f"rs_chip_sum_{key[1]}_{key[0]}") for key in keys]
        joined, = run_plans([plan_join_halves(fulls)], name="rs_join_halves")
        return dict(zip(keys, joined))


WEIGHTS = ("rel_table", "norm1_g", "norm2_g", "w_ada", "b_ada", "w_in", "mla_cq_g", "w_uq", "mla_ckv_g", "w_ukv", "mla_qn_g", "mla_kn_g",
           "sw_qn_g", "sw_kn_g", "sw_sinks", "w_out", "w_up", "conv_w", "conv_b", "w_down")
SMALL = tuple(n for n in WEIGHTS if n not in BIG + ("w_ada",))


def kernel(x, c, positions, rel_table, norm1_g, norm2_g, w_ada, b_ada, w_in, mla_cq_g, w_uq, mla_ckv_g, w_ukv, mla_qn_g, mla_kn_g, sw_qn_g, sw_kn_g, sw_sinks, w_out, w_up, conv_w, conv_b, w_down, loss_target, m_rel_table, m_norm1_g, m_norm2_g, m_w_ada, m_b_ada, m_w_in, m_mla_cq_g, m_w_uq, m_mla_ckv_g, m_w_ukv, m_mla_qn_g, m_mla_kn_g, m_sw_qn_g, m_sw_kn_g, m_sw_sinks, m_w_out, m_w_up, m_conv_w, m_conv_b, m_w_down, v_rel_table, v_norm1_g, v_norm2_g, v_w_ada, v_b_ada, v_w_in, v_mla_cq_g, v_w_uq, v_mla_ckv_g, v_w_ukv, v_mla_qn_g, v_mla_kn_g, v_sw_qn_g, v_sw_kn_g, v_sw_sinks, v_w_out, v_w_up, v_conv_w, v_conv_b, v_w_down):
    w = dict(rel_table=rel_table, norm1_g=norm1_g, norm2_g=norm2_g, w_ada=w_ada, b_ada=b_ada, w_in=w_in, mla_cq_g=mla_cq_g, w_uq=w_uq,
             mla_ckv_g=mla_ckv_g, w_ukv=w_ukv, mla_qn_g=mla_qn_g, mla_kn_g=mla_kn_g, sw_qn_g=sw_qn_g, sw_kn_g=sw_kn_g, sw_sinks=sw_sinks,
             w_out=w_out, w_up=w_up, conv_w=conv_w, conv_b=conv_b, w_down=w_down)
    m = dict(rel_table=m_rel_table, norm1_g=m_norm1_g, norm2_g=m_norm2_g, w_ada=m_w_ada, b_ada=m_b_ada, w_in=m_w_in, mla_cq_g=m_mla_cq_g,
             w_uq=m_w_uq, mla_ckv_g=m_mla_ckv_g, w_ukv=m_w_ukv, mla_qn_g=m_mla_qn_g, mla_kn_g=m_mla_kn_g, sw_qn_g=m_sw_qn_g,
             sw_kn_g=m_sw_kn_g, sw_sinks=m_sw_sinks, w_out=m_w_out, w_up=m_w_up, conv_w=m_conv_w, conv_b=m_conv_b, w_down=m_w_down)
    v = dict(rel_table=v_rel_table, norm1_g=v_norm1_g, norm2_g=v_norm2_g, w_ada=v_w_ada, b_ada=v_b_ada, w_in=v_w_in, mla_cq_g=v_mla_cq_g,
             w_uq=v_w_uq, mla_ckv_g=v_mla_ckv_g, w_ukv=v_w_ukv, mla_qn_g=v_mla_qn_g, mla_kn_g=v_mla_kn_g, sw_qn_g=v_sw_qn_g,
             sw_kn_g=v_sw_kn_g, sw_sinks=v_sw_sinks, w_out=v_w_out, w_up=v_w_up, conv_w=v_conv_w, conv_b=v_conv_b, w_down=v_w_down)
    Bl, S, D = x.shape
    L = norm1_g.shape[0]
    xi, yi, ci = _me()
    chip = 2 * xi + yi
    dev = 4 * xi + 2 * yi + ci
    ndev = 2 * NCHIPS

    shapes = {k: w[k].shape[1:] for k in BIG}
    shards = {(f"l{l}", k): cast_layer(w[k], l, name=f"cast_{k}_l{l}") for l in range(L) for k in BIG}
    gflow = _GatherFlow(shards, chip)
    gflow.early([("l0", k) for k in ("w_in", "w_uq", "w_ukv")])

    cw_cols = conv_w.shape[2]
    c_got, cw_got = allgather8([c, conv_w.reshape(L * 3, cw_cols)], name="gather_cond")
    c_all = c_got.reshape(ndev * Bl, D)
    conv_full = jnp.transpose(cw_got[0::2].reshape(NCHIPS, L, 3, cw_cols), (1, 2, 0, 3)).reshape(L, 3, NCHIPS * cw_cols)
    E = w_ada.shape[2]
    b_cols = lax.dynamic_slice(b_ada, (0, chip * E), (L, E)).reshape(L, 1, E)
    mods_cols = mods_matmul(c_all, w_ada, b_cols, name="mods")
    mods_all, = allgather8([mods_cols.reshape(L * ndev * Bl, E)], name="gather_mods")
    mods_all = jnp.transpose(mods_all[0::2].reshape(NCHIPS, L, ndev * Bl, E), (1, 2, 0, 3)).reshape(L, ndev * Bl, NCHIPS * E)
    mods = lax.dynamic_slice(mods_all, (0, dev * Bl, 0), (L, Bl, NCHIPS * E))

    Wl = []
    for l in range(L):
        Wd = _small_params({k: w[k][l] for k in SMALL if k not in ("rel_table", "b_ada", "conv_w")})
        Wd["conv_w"] = _up_perm(conv_full[l])
        Wl.append(_LayerWeights(Wd, gflow, f"l{l}"))

    sflow = _ScatterFlow(shapes, jnp.stack([chip, ci]).astype(jnp.int32), ci.reshape(1).astype(jnp.int32))
    loss, dx, _, smalls, dmods, dsinks, drel = _local_step(x, loss_target, positions, mods, Wl, rel_table.reshape(-1), gflow, sflow)
    reduced = sflow.finish()
    grad = {k: jnp.stack([reduced[(f"l{l}", k)] for l in range(L)]) for k in BIG}

    vec_names = ("n1", "n2", "cq_g", "ckv_g", "qn_g", "kn_g", "swq_g", "swk_g")
    vecs = _cat([_cat([smalls[l][k] for k in vec_names], axis=1) for l in range(L)], axis=0)
    convs = _cat([smalls[l]["conv"][0:4] for l in range(L)], axis=0)
    dm = jnp.stack(dmods, axis=1).reshape(Bl * L, 6 * D)
    dsk = jnp.stack(dsinks, axis=1).reshape(Bl * L * 6, LANES)
    got = allgather8([vecs, convs, drel, loss, dm, dsk], name="gather_small_grads")
    seq = lambda a, rows: a.reshape(ndev * Bl, rows, a.shape[-1])
    vec_s, conv_s, rel_s, loss_s, dm_s, dsk_s = sum_small(list(got[:4]) + [seq(got[4], L), seq(got[5], L * 6)], name="sum_small_grads")
    dm_all = jnp.transpose(seq(got[4], L), (1, 0, 2))
    grad["w_ada"] = ada_grad(c_all, lax.dynamic_slice(dm_all, (0, 0, chip * E), (L, ndev * Bl, E)), name="ada_grad")
    grad["b_ada"] = dm_s
    grad["sw_sinks"] = jnp.transpose(dsk_s.reshape(L, 3, 2, LANES)[:, :, :, 0], (0, 2, 1)).reshape(L, 6)
    grad["rel_table"] = rel_s[:6, :REL_BUCKETS].T
    off = 0
    for k, name_, keep in zip(vec_names, ("norm1_g", "norm2_g", "mla_cq_g", "mla_ckv_g", "mla_qn_g", "mla_kn_g", "sw_qn_g", "sw_kn_g"),
                              (D, D, 256, LANES, MLA_QK, MLA_QK, HEAD, HEAD)):
        grad[name_] = vec_s[:, off:off + keep]
        off += smalls[0][k].shape[1]
    conv = _up_perm(conv_s.reshape(L, 4, 2 * D_FF))
    grad["conv_w"] = lax.dynamic_slice(conv[:, 0:3], (0, 0, chip * cw_cols), (L, 3, cw_cols))
    grad["conv_b"] = conv[:, 3]
    loss_out = loss_s[0, 0]

    delta, new_m, new_v = {}, {}, {}
    for k in BIG + ("w_ada",):
        delta[k], new_m[k], new_v[k] = adamw(w[k], grad[k], m[k], v[k], name=f"adamw_{k}")
    outs = adamw_small(*[[src[k] for k in SMALL] for src in (w, grad, m, v)], name="adamw_small")
    for dst, o in zip((delta, new_m, new_v), outs):
        dst.update(dict(zip(SMALL, o)))
    return (loss_out, dx, *[grad[k] for k in WEIGHTS], *[delta[k] for k in WEIGHTS], *[new_m[k] for k in WEIGHTS], *[new_v[k] for k in WEIGHTS])
```

```python
import math

import jax
import jax.numpy as jnp
from jax import lax
from jax.experimental import pallas as pl
from jax.experimental.pallas import tpu as pltpu

F32 = jnp.float32
BF16 = jnp.bfloat16
MESH = pl.DeviceIdType.MESH

EPS = 1e-6
NEG = -1e30
HEAD = 64
LANES = 128
MLA_QK = 96
ROPE_THETA = 10000.0
REL_BUCKETS = 32
REL_MAX_DIST = 128
WINDOW = 128
D_FF = 2816
ADAM_LR, ADAM_B1, ADAM_B2, ADAM_EPS, ADAM_WD, ADAM_STEP = 0.001, 0.9, 0.999, 1e-08, 0.01, 10

VMEM_LIMIT = 56 * 1024 * 1024
STRIP = 32

P_SBQ, P_SBK, P_SBV, P_CQ, P_CKV, P_SLAB, P_SWQ, P_SWK, P_SWV, P_END = 0, 256, 512, 768, 1024, 1152, 1280, 1664, 1792, 1920


def _cp(*sem):
    return pltpu.CompilerParams(dimension_semantics=sem, vmem_limit_bytes=VMEM_LIMIT)


def _dot(a, b):
    return jnp.dot(a, b, preferred_element_type=F32)


def _dot_nt(a, b):
    return lax.dot_general(a, b, (((1,), (1,)), ((), ())), preferred_element_type=F32)


def _dot_tn(a, b):
    return lax.dot_general(a, b, (((0,), (0,)), ((), ())), preferred_element_type=F32)


def _lane_masks():
    lane = lax.broadcasted_iota(jnp.int32, (1, LANES), 1)
    return (lane < HEAD, lane >= HEAD)


def _tile(n, cap, align=128):
    if n <= cap:
        return n
    t = cap - cap % align
    while t >= align:
        if n % t == 0:
            return t
        t -= align
    return n


def matmul(a, b, *, ta=False, tb=False, out_dtype=F32, tm=512, tn=512, tk=8192, name):
    M, K = (a.shape[1], a.shape[0]) if ta else a.shape
    N = b.shape[0] if tb else b.shape[1]
    tm, tn, tk = _tile(M, tm), _tile(N, tn), _tile(K, tk)
    nk = K // tk

    def body(a_ref, b_ref, o_ref, *scratch):
        av = a_ref[...].astype(BF16)
        bv = b_ref[...].astype(BF16)
        if ta:
            part = _dot_tn(av, bv)
        elif tb:
            part = _dot_nt(av, bv)
        else:
            part = _dot(av, bv)
        if nk == 1:
            o_ref[...] = part.astype(out_dtype)
        else:
            acc_ref, = scratch
            k = pl.program_id(2)

            @pl.when(k == 0)
            def _():
                acc_ref[...] = part

            @pl.when(k > 0)
            def _():
                acc_ref[...] += part

            @pl.when(k == nk - 1)
            def _():
                o_ref[...] = acc_ref[...].astype(out_dtype)

    n_outer = nk == 1 and tn * b.dtype.itemsize > tm * a.dtype.itemsize
    ij = (lambda p, q: (q, p)) if n_outer else (lambda p, q: (p, q))
    a_map = (lambda p, q, k: (k, ij(p, q)[0])) if ta else (lambda p, q, k: (ij(p, q)[0], k))
    b_map = (lambda p, q, k: (ij(p, q)[1], k)) if tb else (lambda p, q, k: (k, ij(p, q)[1]))
    grid = (N // tn, M // tm, nk) if n_outer else (M // tm, N // tn, nk)
    return pl.pallas_call(
        body, name=name, grid=grid,
        in_specs=[pl.BlockSpec((tk, tm) if ta else (tm, tk), a_map), pl.BlockSpec((tn, tk) if tb else (tk, tn), b_map)],
        out_specs=pl.BlockSpec((tm, tn), lambda p, q, k: ij(p, q)),
        out_shape=jax.ShapeDtypeStruct((M, N), out_dtype),
        scratch_shapes=[] if nk == 1 else [pltpu.VMEM((tm, tn), F32)],
        compiler_params=_cp("parallel", "parallel", "arbitrary"),
    )(a, b)


def matmul_res(a, b, res, gate, seq, *, tm=512, tn=1024, name):
    M, K = a.shape
    N = b.shape[1]
    tm, tn = _tile(min(M, seq), tm), _tile(N, tn)
    per_seq = seq // tm

    def body(a_ref, b_ref, r_ref, g_ref, y_ref, x_ref):
        y = _dot(a_ref[...].astype(BF16), b_ref[...].astype(BF16))
        y_ref[...] = y
        x_ref[...] = r_ref[...] + g_ref[...] * y

    out = jax.ShapeDtypeStruct((M, N), F32)
    return pl.pallas_call(
        body, name=name, grid=(M // tm, N // tn),
        in_specs=[pl.BlockSpec((tm, K), lambda i, j: (i, 0)), pl.BlockSpec((K, tn), lambda i, j: (0, j)),
                  pl.BlockSpec((tm, tn), lambda i, j: (i, j)), pl.BlockSpec((None, 1, tn), lambda i, j: (lax.div(i, jnp.int32(per_seq)), 0, j))],
        out_specs=[pl.BlockSpec((tm, tn), lambda i, j: (i, j))] * 2,
        out_shape=[out, out], compiler_params=_cp("parallel", "parallel"),
    )(a, b, res, gate)


def rms_fwd(x3, blk, W, g, sc=None, sh=None, *, tm=512, name):
    Bl, S, _ = x3.shape
    tm = min(tm, S)
    mod = sc is not None

    def body(x_ref, g_ref, *rest):
        o_ref = rest[-1]
        x = x_ref[...]
        r = lax.rsqrt(jnp.mean(x * x, axis=-1, keepdims=True) + EPS)
        y = x * r * g_ref[...]
        if mod:
            y = y * (1.0 + rest[0][...]) + rest[1][...]
        o_ref[...] = y.astype(BF16)

    vec = pl.BlockSpec((None, 1, W), lambda b, s: (b, 0, 0))
    return pl.pallas_call(
        body, name=name, grid=(Bl, S // tm),
        in_specs=[pl.BlockSpec((None, tm, W), lambda b, s: (b, s, blk)), pl.BlockSpec((1, W), lambda b, s: (0, 0))] + ([vec, vec] if mod else []),
        out_specs=pl.BlockSpec((None, tm, W), lambda b, s: (b, s, 0)),
        out_shape=jax.ShapeDtypeStruct((Bl, S, W), BF16),
        compiler_params=_cp("parallel", "parallel"),
    )(x3, g, *([sc, sh] if mod else []))


def rms_bwd(x3, blk, W, dy3, g, sc=None, dres3=None, *, tm=256, name):
    Bl, S, _ = x3.shape
    tm = min(tm, S)
    mod = sc is not None
    res = dres3 is not None

    def body(*refs):
        x_ref, dy_ref, g_ref = refs[:3]
        k = 3
        sc_ref = dr_ref = None
        if mod:
            sc_ref = refs[k]
            k += 1
        if res:
            dr_ref = refs[k]
            k += 1
        dx_ref, dg_ref = refs[k], refs[k + 1]
        b, s = pl.program_id(0), pl.program_id(1)
        x = x_ref[...]
        dy = dy_ref[...].astype(F32)
        g = g_ref[...]
        r = lax.rsqrt(jnp.mean(x * x, axis=-1, keepdims=True) + EPS)
        n = x * r
        if mod:
            dsc_ref, dsh_ref = refs[k + 2], refs[k + 3]
            one_sc = 1.0 + sc_ref[...]

            @pl.when(s == 0)
            def _():
                dsc_ref[...] = jnp.zeros_like(dsc_ref)
                dsh_ref[...] = jnp.zeros_like(dsh_ref)

            dsh_ref[...] += jnp.sum(dy, axis=0, keepdims=True)
            dsc_ref[...] += jnp.sum(dy * n * g, axis=0, keepdims=True)
            dyn = dy * one_sc
        else:
            dyn = dy

        @pl.when((b == 0) & (s == 0))
        def _():
            dg_ref[...] = jnp.zeros_like(dg_ref)

        dg_ref[...] += jnp.sum(dyn * n, axis=0, keepdims=True)
        dn = dyn * g
        dx = r * (dn - n * jnp.mean(dn * n, axis=-1, keepdims=True))
        if res:
            dx = dx + dr_ref[...]
        dx_ref[...] = dx

    blkspec = pl.BlockSpec((None, tm, W), lambda b, s: (b, s, 0))
    vec = pl.BlockSpec((None, 1, W), lambda b, s: (b, 0, 0))
    row = pl.BlockSpec((1, W), lambda b, s: (0, 0))
    in_specs = [pl.BlockSpec((None, tm, W), lambda b, s: (b, s, blk)), blkspec, row] + ([vec] if mod else []) + ([blkspec] if res else [])
    out_specs = [blkspec, row] + ([vec, vec] if mod else [])
    out_shape = [jax.ShapeDtypeStruct((Bl, S, W), F32), jax.ShapeDtypeStruct((1, W), F32)]
    if mod:
        out_shape += [jax.ShapeDtypeStruct((Bl, 1, W), F32)] * 2
    args = [x3, dy3, g] + ([sc] if mod else []) + ([dres3] if res else [])
    return pl.pallas_call(
        body, name=name, grid=(Bl, S // tm), in_specs=in_specs, out_specs=out_specs, out_shape=out_shape,
        compiler_params=_cp("arbitrary", "arbitrary"),
    )(*args)


def pair_rms_fwd(x3, blk0, npairs, g2, *, tm=1024, name):
    Bl, S, _ = x3.shape
    tm = min(tm, S)

    def body(x_ref, g_ref, o_ref):
        lo, hi = _lane_masks()
        x = x_ref[...]
        xx = x * x
        s0 = jnp.sum(jnp.where(lo, xx, 0.0), axis=-1, keepdims=True)
        s1 = jnp.sum(jnp.where(hi, xx, 0.0), axis=-1, keepdims=True)
        r = jnp.where(lo, lax.rsqrt(s0 / HEAD + EPS), lax.rsqrt(s1 / HEAD + EPS))
        o_ref[...] = (x * r * g_ref[...]).astype(BF16)

    return pl.pallas_call(
        body, name=name, grid=(Bl, S // tm, npairs),
        in_specs=[pl.BlockSpec((None, tm, LANES), lambda b, s, p: (b, s, blk0 + p)), pl.BlockSpec((1, LANES), lambda b, s, p: (0, 0))],
        out_specs=pl.BlockSpec((None, tm, LANES), lambda b, s, p: (b, s, p)),
        out_shape=jax.ShapeDtypeStruct((Bl, S, LANES * npairs), BF16),
        compiler_params=_cp("parallel", "parallel", "parallel"),
    )(x3, g2)


def pair_rms_bwd(x3, blk0, npairs, dy3, g2, *, tm=1024, name):
    Bl, S, _ = x3.shape
    tm = min(tm, S)

    def body(x_ref, dy_ref, g_ref, dx_ref, dg_ref):
        lo, hi = _lane_masks()
        first = (pl.program_id(0) == 0) & (pl.program_id(1) == 0) & (pl.program_id(2) == 0)
        x = x_ref[...]
        dy = dy_ref[...]
        xx = x * x
        s0 = jnp.sum(jnp.where(lo, xx, 0.0), axis=-1, keepdims=True)
        s1 = jnp.sum(jnp.where(hi, xx, 0.0), axis=-1, keepdims=True)
        r = jnp.where(lo, lax.rsqrt(s0 / HEAD + EPS), lax.rsqrt(s1 / HEAD + EPS))
        n = x * r

        @pl.when(first)
        def _():
            dg_ref[...] = jnp.zeros_like(dg_ref)

        part = jnp.sum(dy * n, axis=0, keepdims=True)
        dg_ref[...] += part + pltpu.roll(part, HEAD, 1)
        dn = dy * g_ref[...]
        t = dn * n
        m0 = jnp.sum(jnp.where(lo, t, 0.0), axis=-1, keepdims=True)
        m1 = jnp.sum(jnp.where(hi, t, 0.0), axis=-1, keepdims=True)
        dx_ref[...] = r * (dn - n * (jnp.where(lo, m0, m1) / HEAD))

    return pl.pallas_call(
        body, name=name, grid=(Bl, S // tm, npairs),
        in_specs=[pl.BlockSpec((None, tm, LANES), lambda b, s, p: (b, s, blk0 + p)), pl.BlockSpec((None, tm, LANES), lambda b, s, p: (b, s, p)),
                  pl.BlockSpec((1, LANES), lambda b, s, p: (0, 0))],
        out_specs=[pl.BlockSpec((None, tm, LANES), lambda b, s, p: (b, s, p)), pl.BlockSpec((1, LANES), lambda b, s, p: (0, 0))],
        out_shape=[jax.ShapeDtypeStruct((Bl, S, LANES * npairs), F32), jax.ShapeDtypeStruct((1, LANES), F32)],
        compiler_params=_cp("arbitrary", "arbitrary", "arbitrary"),
    )(x3, dy3, g2)


def _rot(y, cos_t, sin_a, sin_b):
    return y * cos_t + pltpu.roll(y, LANES - 16, 1) * sin_a + pltpu.roll(y, 16, 1) * sin_b


def _rot_t(d, cos_t, sin_a, sin_b):
    return d * cos_t + pltpu.roll(d * sin_a, 16, 1) + pltpu.roll(d * sin_b, LANES - 16, 1)


def rope_norm_fwd(x3, nheads, g, tabs, slab=None, *, tm=1024, name):
    Bl, S, _ = x3.shape
    tm = min(tm, S)
    has_slab = slab is not None

    def body(*refs):
        x_ref, g_ref, c_ref, sa_ref, sb_ref = refs[:5]
        o_ref = refs[-1]
        x = x_ref[...]
        if has_slab:
            x = x + refs[5][...]
        r = lax.rsqrt(jnp.sum(x * x, axis=-1, keepdims=True) / MLA_QK + EPS)
        o_ref[...] = _rot(x * r * g_ref[...], c_ref[...], sa_ref[...], sb_ref[...]).astype(BF16)

    head = pl.BlockSpec((None, tm, LANES), lambda b, s, h: (b, s, h))
    tab = pl.BlockSpec((None, tm, LANES), lambda b, s, h: (b, s, 0))
    in_specs = [head, pl.BlockSpec((1, LANES), lambda b, s, h: (0, 0)), tab, tab, tab]
    args = [x3, g, *tabs]
    if has_slab:
        sblk = slab[1]
        in_specs.append(pl.BlockSpec((None, tm, LANES), lambda b, s, h: (b, s, sblk)))
        args.append(slab[0])
    return pl.pallas_call(
        body, name=name, grid=(Bl, S // tm, nheads), in_specs=in_specs, out_specs=head,
        out_shape=jax.ShapeDtypeStruct((Bl, S, LANES * nheads), BF16),
        compiler_params=_cp("parallel", "parallel", "parallel"),
    )(*args)


def rope_norm_bwd(x3, nheads, dy3, g, tabs, slab=None, *, tm=1024, name):
    Bl, S, _ = x3.shape
    tm = min(tm, S)
    has_slab = slab is not None

    def body(*refs):
        x_ref, dy_ref, g_ref, c_ref, sa_ref, sb_ref = refs[:6]
        k = 7 if has_slab else 6
        dx_ref, dg_ref = refs[k], refs[k + 1]
        h = pl.program_id(2)
        first = (pl.program_id(0) == 0) & (pl.program_id(1) == 0) & (h == 0)
        x = x_ref[...]
        if has_slab:
            x = x + refs[6][...]
        g = g_ref[...]
        r = lax.rsqrt(jnp.sum(x * x, axis=-1, keepdims=True) / MLA_QK + EPS)
        n = x * r
        d = _rot_t(dy_ref[...], c_ref[...], sa_ref[...], sb_ref[...])

        @pl.when(first)
        def _():
            dg_ref[...] = jnp.zeros_like(dg_ref)

        dg_ref[...] += jnp.sum(d * n, axis=0, keepdims=True)
        dn = d * g
        dx = r * (dn - n * (jnp.sum(dn * n, axis=-1, keepdims=True) / MLA_QK))
        dx_ref[...] = dx.astype(BF16)
        if has_slab:
            ds_ref = refs[k + 2]

            @pl.when(h == 0)
            def _():
                ds_ref[...] = dx

            @pl.when(h > 0)
            def _():
                ds_ref[...] += dx

    head = pl.BlockSpec((None, tm, LANES), lambda b, s, h: (b, s, h))
    tab = pl.BlockSpec((None, tm, LANES), lambda b, s, h: (b, s, 0))
    row = pl.BlockSpec((1, LANES), lambda b, s, h: (0, 0))
    in_specs = [head, head, row, tab, tab, tab]
    args = [x3, dy3, g, *tabs]
    out_specs = [head, row]
    out_shape = [jax.ShapeDtypeStruct((Bl, S, LANES * nheads), BF16), jax.ShapeDtypeStruct((1, LANES), F32)]
    if has_slab:
        sblk = slab[1]
        in_specs.append(pl.BlockSpec((None, tm, LANES), lambda b, s, h: (b, s, sblk)))
        args.append(slab[0])
        out_specs.append(tab)
        out_shape.append(jax.ShapeDtypeStruct((Bl, S, LANES), F32))
    return pl.pallas_call(
        body, name=name, grid=(Bl, S // tm, nheads), in_specs=in_specs, out_specs=out_specs, out_shape=out_shape,
        compiler_params=_cp("arbitrary", "arbitrary", "arbitrary"),
    )(*args)


def _softplus(z):
    return jnp.maximum(z, 0.0) + jnp.log(1.0 + jnp.exp(-jnp.abs(z)))


def _split_dots(xs, u):
    hi = [x.astype(BF16) for x in xs]
    lo = [(x - h.astype(F32)).astype(BF16) for x, h in zip(xs, hi)]
    top = [_dot(h, u) for h in hi]
    return [t + _dot(l, u) for t, l in zip(top, lo)]


SB_BLOCK = 256
SB_QBLOCK = 512


def sb_attn_fwd(proj3, *, plans=None, name):
    Bl, S, _ = proj3.shape
    tk = min(SB_BLOCK, S)
    tq = min(SB_QBLOCK, S)
    per_q = tq // tk
    scale = HEAD ** -0.5
    qb, kb0, vb0 = P_SBQ // LANES, P_SBK // LANES, P_SBV // LANES

    def body(q_ref, k_ref, v_ref, o_ref, rt_ref):
        i = pl.program_id(2)
        masks = _lane_masks()
        lane = lax.broadcasted_iota(jnp.int32, (1, LANES), 1)
        q = q_ref[...]
        qh = [jnp.where(m, q, 0.0).astype(BF16) for m in masks]
        rr = lax.broadcasted_iota(jnp.int32, (tq, tk), 0)
        cc = lax.broadcasted_iota(jnp.int32, (tq, tk), 1)
        u = (lax.broadcasted_iota(jnp.int32, (tk, tk), 0) > lax.broadcasted_iota(jnp.int32, (tk, tk), 1)).astype(BF16)

        rt_ref[...] = jnp.zeros_like(rt_ref)

        def step(j, carry, masked):
            r0, r1, acc = carry
            off = pl.multiple_of(j * tk, tk)
            kb = k_ref[pl.ds(off, tk), :].astype(BF16)
            vb = v_ref[pl.ds(off, tk), :]
            strict = (cc + j * tk) < (rr + i * tq) if masked else None
            only = (lambda t: jnp.where(strict, t, 0.0)) if masked else (lambda t: t)
            rt_ref[...] = jnp.where(lane == j, r0, jnp.where(lane == j + HEAD, r1, rt_ref[...]))
            rs, two = [r0, r1], range(2)
            z = [_dot_nt(qh[h], kb) * scale for h in two]
            sp = [_softplus(z[h]) for h in two]
            keep = [only(-sp[h]) for h in two]
            suf = _split_dots(keep, u)
            w = [only(jnp.exp((z[h] - sp[h]) + suf[h] + rs[h])) for h in two]
            pv = [_dot(w[h].astype(BF16), jnp.where(masks[h], vb, 0.0).astype(BF16)) for h in two]
            return rs[0] + jnp.sum(keep[0], axis=1, keepdims=True), rs[1] + jnp.sum(keep[1], axis=1, keepdims=True), acc + (pv[0] + pv[1])

        zero = jnp.zeros((tq, 1), F32)
        carry = (zero, zero, jnp.zeros((tq, LANES), F32))
        for t in range(per_q):
            carry = step((i + 1) * per_q - 1 - t, carry, True)
        _, _, acc = lax.fori_loop(0, i * per_q, lambda t, c: step(i * per_q - 1 - t, c, False), carry)
        o_ref[...] = acc

    seq = lambda blk0: pl.BlockSpec((None, S, LANES), lambda b, p, i: (b, 0, blk0 + p))
    out = pl.BlockSpec((None, tq, LANES), lambda b, p, i: (b, i, p))
    shp = jax.ShapeDtypeStruct((Bl, S, 2 * LANES), F32)
    return call_with_plans(
        body, plans, name=name, grid=(Bl, 2, S // tq),
        in_specs=[pl.BlockSpec((None, tq, LANES), lambda b, p, i: (b, i, qb + p)), seq(kb0), seq(vb0)],
        out_specs=[out, out], out_shape=[shp, shp], scratch_shapes=[], args=[proj3, proj3, proj3],
        sem=("arbitrary",) * 3 if plans else ("parallel", "parallel", "arbitrary"))


def sb_attn_bwd(proj3, rt3, do3, *, do_blk0=0, plans=None, name):
    Bl, S, _ = proj3.shape
    tk = min(SB_BLOCK, S)
    tq = min(SB_QBLOCK, S)
    per_q = tq // tk
    scale = HEAD ** -0.5
    qb, kb0, vb0 = P_SBQ // LANES, P_SBK // LANES, P_SBV // LANES

    def body(q_ref, k_ref, v_ref, rt_ref, do_ref, dq_ref, dk_ref, dv_ref):
        i = pl.program_id(2)

        @pl.when(i == 0)
        def _():
            dk_ref[...] = jnp.zeros_like(dk_ref)
            dv_ref[...] = jnp.zeros_like(dv_ref)

        masks = _lane_masks()
        lane = lax.broadcasted_iota(jnp.int32, (1, LANES), 1)
        q = q_ref[...]
        qh = [jnp.where(m, q, 0.0).astype(BF16) for m in masks]
        do_b = do_ref[...].astype(BF16)
        doh = [jnp.where(m, do_b, jnp.zeros_like(do_b)) for m in masks]
        rt = rt_ref[...]
        rr = lax.broadcasted_iota(jnp.int32, (tq, tk), 0)
        cc = lax.broadcasted_iota(jnp.int32, (tq, tk), 1)
        ur = lax.broadcasted_iota(jnp.int32, (tk, tk), 0)
        uc = lax.broadcasted_iota(jnp.int32, (tk, tk), 1)
        u_suffix = (ur > uc).astype(BF16)
        u_prefix = (ur < uc).astype(BF16)

        def step(j, carry, masked):
            p0, p1, dq = carry
            off = pl.multiple_of(j * tk, tk)
            kf = k_ref[pl.ds(off, tk), :]
            kb = kf.astype(BF16)
            vb = v_ref[pl.ds(off, tk), :]
            strict = (cc + j * tk) < (rr + i * tq) if masked else None
            only = (lambda t: jnp.where(strict, t, 0.0)) if masked else (lambda t: t)
            ps, two = [p0, p1], range(2)
            r_j = [jnp.sum(jnp.where(lane == j + h * HEAD, rt, 0.0), axis=1, keepdims=True) for h in two]
            z = [_dot_nt(qh[h], kb) * scale for h in two]
            dw = [_dot_nt(doh[h], jnp.where(masks[h], vb, 0.0).astype(BF16)) for h in two]
            sp = [_softplus(z[h]) for h in two]
            keep = [only(-sp[h]) for h in two]
            suf = _split_dots(keep, u_suffix)
            w = [only(jnp.exp((z[h] - sp[h]) + suf[h] + r_j[h])) for h in two]
            g = [dw[h] * w[h] for h in two]
            pre = _split_dots(g, u_prefix)
            dzb = [(only(g[h] * jnp.exp(-sp[h]) - jnp.exp(z[h] - sp[h]) * (pre[h] + ps[h])) * scale).astype(BF16) for h in two]
            dqs = [_dot(dzb[h], jnp.where(masks[h], kf, 0.0).astype(BF16)) for h in two]
            dks = [_dot_tn(dzb[h], qh[h]) for h in two]
            dvs = [_dot_tn(w[h].astype(BF16), doh[h]) for h in two]
            dk_ref[pl.ds(off, tk), :] += dks[0] + dks[1]
            dv_ref[pl.ds(off, tk), :] += dvs[0] + dvs[1]
            return ps[0] + jnp.sum(g[0], axis=1, keepdims=True), ps[1] + jnp.sum(g[1], axis=1, keepdims=True), dq + (dqs[0] + dqs[1])

        zero = jnp.zeros((tq, 1), F32)
        carry = lax.fori_loop(0, i * per_q, lambda j, c: step(j, c, False), (zero, zero, jnp.zeros((tq, LANES), F32)))
        for t in range(per_q):
            carry = step(i * per_q + t, carry, True)
        dq_ref[...] = carry[2]

    seq_in = lambda blk0: pl.BlockSpec((None, S, LANES), lambda b, p, i: (b, 0, blk0 + p))
    blk = pl.BlockSpec((None, tq, LANES), lambda b, p, i: (b, i, p))
    seq_out = pl.BlockSpec((None, S, LANES), lambda b, p, i: (b, 0, p))
    shp = jax.ShapeDtypeStruct((Bl, S, 2 * LANES), F32)
    return call_with_plans(
        body, plans, name=name, grid=(Bl, 2, S // tq),
        in_specs=[pl.BlockSpec((None, tq, LANES), lambda b, p, i: (b, i, qb + p)), seq_in(kb0), seq_in(vb0), blk,
                  pl.BlockSpec((None, tq, LANES), lambda b, p, i: (b, i, do_blk0 + p))],
        out_specs=[blk, seq_out, seq_out], out_shape=[shp, shp, shp], scratch_shapes=[], args=[proj3, proj3, proj3, rt3, do3],
        sem=("arbitrary",) * 3 if plans else ("parallel", "parallel", "arbitrary"))


def mla_attn_fwd(q3, k3, kv3, vblk0, *, tq=512, tk=512, plans=None, name):
    Bl, S, _ = q3.shape
    tq = min(tq, S)
    tk = min(tk, tq)
    per_q = tq // tk
    scale = MLA_QK ** -0.5

    def body(q_ref, k_ref, v_ref, o_ref, lse_ref):
        i = pl.program_id(2)
        masks = _lane_masks()
        rr = lax.broadcasted_iota(jnp.int32, (tq, tk), 0)
        cc = lax.broadcasted_iota(jnp.int32, (tq, tk), 1)
        qh = [q_ref[:, h * LANES:(h + 1) * LANES] for h in range(2)]

        def step(j, carry):
            m0, l0, m1, l1, acc = carry
            off = pl.multiple_of(j * tk, tk)
            vb = v_ref[pl.ds(off, tk), :]
            causal = (cc + j * tk) <= (rr + i * tq)
            ms, ls, two = [m0, m1], [l0, l1], range(2)
            kh = [k_ref[pl.ds(off, tk), h * LANES:(h + 1) * LANES] for h in two]
            s = [jnp.where(causal, _dot_nt(qh[h], kh[h]) * scale, NEG) for h in two]
            m_new = [jnp.maximum(ms[h], jnp.max(s[h], axis=1, keepdims=True)) for h in two]
            p = [jnp.exp(s[h] - m_new[h]) for h in two]
            alpha = [jnp.exp(ms[h] - m_new[h]) for h in two]
            ls = [alpha[h] * ls[h] + jnp.sum(p[h], axis=1, keepdims=True) for h in two]
            add = [_dot(p[h].astype(BF16), jnp.where(masks[h], vb, 0.0).astype(BF16)) for h in two]
            acc = acc * jnp.where(masks[0], alpha[0], alpha[1]) + (add[0] + add[1])
            return m_new[0], ls[0], m_new[1], ls[1], acc

        neg = jnp.full((tq, 1), NEG, F32)
        zero = jnp.zeros((tq, 1), F32)
        m0, l0, m1, l1, acc = lax.fori_loop(0, (i + 1) * per_q, step, (neg, zero, neg, zero, jnp.zeros((tq, LANES), F32)))
        o_ref[...] = acc / jnp.where(masks[0], l0, l1)
        lse_ref[...] = jnp.where(masks[0], m0 + jnp.log(l0), m1 + jnp.log(l1))

    out = pl.BlockSpec((None, tq, LANES), lambda b, p, i: (b, i, p))
    shp = jax.ShapeDtypeStruct((Bl, S, 3 * LANES), F32)
    return call_with_plans(
        body, plans, name=name, grid=(Bl, 3, S // tq),
        in_specs=[pl.BlockSpec((None, tq, 2 * LANES), lambda b, p, i: (b, i, p)), pl.BlockSpec((None, S, 2 * LANES), lambda b, p, i: (b, 0, p)),
                  pl.BlockSpec((None, S, LANES), lambda b, p, i: (b, 0, vblk0 + p))],
        out_specs=[out, out], out_shape=[shp, shp], scratch_shapes=[], args=[q3, k3, kv3],
        sem=("arbitrary",) * 3 if plans else ("parallel", "parallel", "arbitrary"))


def mla_attn_bwd(q3, k3, kv3, vblk0, o3, lse3, do3, *, do_blk0=0, tq=512, tk=512, name):
    Bl, S, _ = q3.shape
    tq = min(tq, S)
    tk = min(tk, tq)
    per_q = tq // tk
    nq = S // tq
    scale = MLA_QK ** -0.5

    def body(q_ref, k_ref, v_ref, o_ref, lse_ref, do_ref, dq_ref, dk_ref, dv_ref, s_scr, dp_scr, p_scr, ds_scr):
        j = pl.program_id(2)

        @pl.when(j == 0)
        def _():
            dq_ref[...] = jnp.zeros_like(dq_ref)

        masks = _lane_masks()
        vb = v_ref[...]
        vh = [jnp.where(m, vb, 0.0).astype(BF16) for m in masks]
        kh = [k_ref[:, h * LANES:(h + 1) * LANES] for h in range(2)]
        i0 = lax.div(j, jnp.int32(per_q))

        def step(i, carry, masked):
            dk0, dk1, dv = carry
            off = pl.multiple_of(i * tq, tq)
            do_b = do_ref[pl.ds(off, tq), :].astype(BF16)
            prod = do_b.astype(F32) * o_ref[pl.ds(off, tq), :]
            lse = lse_ref[pl.ds(off, tq), :]
            two = range(2)
            qh = [q_ref[pl.ds(off, tq), h * LANES:(h + 1) * LANES] for h in two]
            doh = [jnp.where(masks[h], do_b, jnp.zeros_like(do_b)) for h in two]
            delta = [jnp.sum(jnp.where(masks[h], prod, 0.0), axis=1, keepdims=True) for h in two]
            lse_h = [lse[:, h * HEAD:h * HEAD + 1] for h in two]
            for h in two:
                s_scr[h] = _dot_nt(qh[h], kh[h])
            for h in two:
                dp_scr[h] = _dot_nt(doh[h], vh[h])
            for r0 in range(0, tq, STRIP):
                rows = slice(r0, r0 + STRIP)
                for h in two:
                    s = s_scr[h, rows, :] * scale
                    if masked:
                        rr = lax.broadcasted_iota(jnp.int32, (STRIP, tk), 0) + (i * tq + r0)
                        cc = lax.broadcasted_iota(jnp.int32, (STRIP, tk), 1) + j * tk
                        s = jnp.where(cc <= rr, s, NEG)
                    p = jnp.exp(s - lse_h[h][rows])
                    p_scr[h, rows, :] = p.astype(BF16)
                    ds_scr[h, rows, :] = (p * (dp_scr[h, rows, :] - delta[h][rows])).astype(BF16)
            dqs = [_dot(ds_scr[h], kh[h]) * scale for h in two]
            dks = [dk0 + _dot_tn(ds_scr[0], qh[0]), dk1 + _dot_tn(ds_scr[1], qh[1])]
            dv = dv + _dot_tn(p_scr[0], doh[0]) + _dot_tn(p_scr[1], doh[1])
            for h in two:
                dq_ref[pl.ds(off, tq), h * LANES:(h + 1) * LANES] += dqs[h]
            return dks[0], dks[1], dv

        zero = jnp.zeros((tk, LANES), F32)
        carry = step(i0, (zero, zero, zero), True)
        dk0, dk1, dv = lax.fori_loop(i0 + 1, nq, lambda i, c: step(i, c, False), carry)
        dk_ref[:, 0:LANES] = dk0 * scale
        dk_ref[:, LANES:2 * LANES] = dk1 * scale
        dv_ref[...] = dv.astype(BF16)

    seq1 = pl.BlockSpec((None, S, LANES), lambda b, p, j: (b, 0, p))
    seq2 = pl.BlockSpec((None, S, 2 * LANES), lambda b, p, j: (b, 0, p))
    return pl.pallas_call(
        body, name=name, grid=(Bl, 3, S // tk),
        in_specs=[seq2, pl.BlockSpec((None, tk, 2 * LANES), lambda b, p, j: (b, j, p)),
                  pl.BlockSpec((None, tk, LANES), lambda b, p, j: (b, j, vblk0 + p)), seq1, seq1,
                  pl.BlockSpec((None, S, LANES), lambda b, p, j: (b, 0, do_blk0 + p))],
        out_specs=[seq2, pl.BlockSpec((None, tk, 2 * LANES), lambda b, p, j: (b, j, p)), pl.BlockSpec((None, tk, LANES), lambda b, p, j: (b, j, p))],
        out_shape=[jax.ShapeDtypeStruct((Bl, S, 6 * LANES), F32), jax.ShapeDtypeStruct((Bl, S, 6 * LANES), F32), jax.ShapeDtypeStruct((Bl, S, 3 * LANES), BF16)],
        scratch_shapes=[pltpu.VMEM((2, tq, tk), F32), pltpu.VMEM((2, tq, tk), F32), pltpu.VMEM((2, tq, tk), BF16), pltpu.VMEM((2, tq, tk), BF16)],
        compiler_params=_cp("parallel", "parallel", "arbitrary"),
    )(q3, k3, kv3, o3, lse3, do3)


def _bucket_table():
    a = jnp.arange(WINDOW)[:, None]
    b = jnp.arange(2 * WINDOW)[None, :]
    dist = WINDOW + a - b
    max_exact = REL_BUCKETS // 2
    n = jnp.maximum(dist, 0)
    nf = jnp.maximum(n, 1).astype(F32)
    large = max_exact + (jnp.log(nf / max_exact) / math.log(REL_MAX_DIST / max_exact) * (REL_BUCKETS - max_exact)).astype(jnp.int32)
    large = jnp.minimum(large, REL_BUCKETS - 1)
    bucket = jnp.where(n < max_exact, n, large)
    return jnp.where((dist >= 0) & (dist < WINDOW), bucket, -1).astype(jnp.int32)


def swa_bias(rel_flat, bucket, *, name):
    def body(t_ref, b_ref, o_ref):
        bk = b_ref[...]
        for p in range(3):
            for hh in range(2):
                h = hh * 3 + p
                acc = jnp.full(bk.shape, NEG, F32)
                for b in range(REL_BUCKETS):
                    acc = jnp.where(bk == b, t_ref[b * 6 + h], acc)
                o_ref[p, hh] = acc

    return pl.pallas_call(
        body, name=name,
        in_specs=[pl.BlockSpec(memory_space=pltpu.SMEM), pl.BlockSpec(memory_space=pltpu.VMEM)],
        out_specs=pl.BlockSpec(memory_space=pltpu.VMEM),
        out_shape=jax.ShapeDtypeStruct((3, 2, WINDOW, 2 * WINDOW), F32),
    )(rel_flat, bucket)


def swa_bias_bwd(dbias, bucket, *, name):
    Bl = dbias.shape[0]

    def body(d_ref, b_ref, o_ref):
        bk = b_ref[...]
        lane = lax.broadcasted_iota(jnp.int32, (1, LANES), 1)
        rows = []
        for h in range(6):
            hh, p = divmod(h, 3)
            d = d_ref[0, p, hh]
            for bl in range(1, Bl):
                d = d + d_ref[bl, p, hh]
            row = jnp.zeros((1, LANES), F32)
            for b in range(REL_BUCKETS):
                s = jnp.sum(jnp.sum(jnp.where(bk == b, d, 0.0), axis=1, keepdims=True), axis=0, keepdims=True)
                row = row + jnp.where(lane == b, s, 0.0)
            rows.append(row)
        rows += [jnp.zeros((1, LANES), F32)] * 2
        o_ref[...] = jnp.concatenate(rows, axis=0)

    return pl.pallas_call(
        body, name=name,
        in_specs=[pl.BlockSpec(memory_space=pltpu.VMEM)] * 2, out_specs=pl.BlockSpec(memory_space=pltpu.VMEM),
        out_shape=jax.ShapeDtypeStruct((8, LANES), F32),
    )(dbias, bucket)


SWA_QBLOCKS = 16


def _swa_specs(vblk, nqb):
    rows = nqb * WINDOW
    cur = lambda blk: pl.BlockSpec((None, rows, LANES), lambda b, p, n: (b, n, blk))
    prev = lambda blk: pl.BlockSpec((None, WINDOW, LANES), lambda b, p, n: (b, jnp.maximum(n * nqb - 1, 0), blk))
    return [pl.BlockSpec((None, rows, LANES), lambda b, p, n: (b, n, p)), cur(0), prev(0), cur(vblk), prev(vblk),
            pl.BlockSpec((None, 2, WINDOW, 2 * WINDOW), lambda b, p, n: (p, 0, 0, 0)), pl.BlockSpec((None, 2, LANES), lambda b, p, n: (p, 0, 0))]


def _rows128(ref, m):
    return ref[m * WINDOW:(m + 1) * WINDOW, :]


def _swa_logits(qh, kp, kc, bias_h, first, scale):
    sp = jnp.where(first, NEG, _dot_nt(qh, kp) * scale + bias_h[:, :WINDOW])
    sc = _dot_nt(qh, kc) * scale + bias_h[:, WINDOW:]
    return sp, sc


def swa_attn_fwd(qn3, kn3, proj3, bias, sinks, *, plans=None, name):
    Bl, S, _ = qn3.shape
    scale = HEAD ** -0.5
    nqb = min(SWA_QBLOCKS, S // WINDOW)

    def body(q_ref, kc_ref, kp_ref, vc_ref, vp_ref, b_ref, s_ref, o_ref, lse_ref):
        seq_start = pl.program_id(2) == 0
        masks = _lane_masks()
        chains = [(m_, h) for m_ in range(nqb) for h in range(2)]
        kp = [kp_ref[...] if m_ == 0 else _rows128(kc_ref, m_ - 1) for m_ in range(nqb)]
        vp = [vp_ref[...] if m_ == 0 else _rows128(vc_ref, m_ - 1) for m_ in range(nqb)]
        kc = [_rows128(kc_ref, m_) for m_ in range(nqb)]
        vc = [_rows128(vc_ref, m_) for m_ in range(nqb)]
        sink = [s_ref[h:h + 1, 0:1] for h in range(2)]
        logits = {}
        for m_, h in chains:
            q = _rows128(q_ref, m_)
            qh = jnp.where(masks[h], q, jnp.zeros_like(q))
            logits[m_, h] = _swa_logits(qh, kp[m_], kc[m_], b_ref[h], seq_start if m_ == 0 else False, scale)
        mx = {c: jnp.maximum(jnp.maximum(jnp.max(logits[c][0], axis=1, keepdims=True), jnp.max(logits[c][1], axis=1, keepdims=True)), sink[c[1]])
              for c in chains}
        ex = {c: (jnp.exp(logits[c][0] - mx[c]), jnp.exp(logits[c][1] - mx[c])) for c in chains}
        den = {c: jnp.sum(ex[c][0], axis=1, keepdims=True) + jnp.sum(ex[c][1], axis=1, keepdims=True) + jnp.exp(sink[c[1]] - mx[c]) for c in chains}
        inv = {c: 1.0 / den[c] for c in chains}
        out = {}
        for m_, h in chains:
            c = (m_, h)
            out[c] = (_dot((ex[c][0] * inv[c]).astype(BF16), jnp.where(masks[h], vp[m_], 0.0).astype(BF16))
                      + _dot((ex[c][1] * inv[c]).astype(BF16), jnp.where(masks[h], vc[m_], 0.0).astype(BF16)))
        for m_ in range(nqb):
            o_ref[m_ * WINDOW:(m_ + 1) * WINDOW, :] = out[m_, 0] + out[m_, 1]
            lse_ref[m_ * WINDOW:(m_ + 1) * WINDOW, :] = jnp.where(masks[0], mx[m_, 0] + jnp.log(den[m_, 0]), mx[m_, 1] + jnp.log(den[m_, 1]))

    out = pl.BlockSpec((None, nqb * WINDOW, LANES), lambda b, p, n: (b, n, p))
    shp = jax.ShapeDtypeStruct((Bl, S, 3 * LANES), F32)
    return call_with_plans(
        body, plans, name=name, grid=(Bl, 3, S // (nqb * WINDOW)), in_specs=_swa_specs(P_SWV // LANES, nqb),
        out_specs=[out, out], out_shape=[shp, shp], scratch_shapes=[], args=[qn3, kn3, kn3, proj3, proj3, bias, sinks],
        sem=("arbitrary",) * 3 if plans else ("parallel", "parallel", "arbitrary"))


def swa_attn_bwd(qn3, kn3, proj3, bias, sinks, o3, lse3, do3, *, do_blk0=0, name):
    Bl, S, _ = qn3.shape
    scale = HEAD ** -0.5
    nqb = min(SWA_QBLOCKS, S // WINDOW)
    rows = nqb * WINDOW

    def body(q_ref, kc_ref, kp_ref, vc_ref, vp_ref, b_ref, s_ref, o_ref, lse_ref, do_ref,
             dq_ref, dk_ref, dv_ref, db_ref, dsk_ref):
        p_id, n = pl.program_id(1), pl.program_id(2)
        seq_start = n == 0

        @pl.when((p_id == 0) & seq_start)
        def _():
            dk_ref[...] = jnp.zeros_like(dk_ref)
            dv_ref[...] = jnp.zeros_like(dv_ref)

        @pl.when(seq_start)
        def _():
            db_ref[...] = jnp.zeros_like(db_ref)
            dsk_ref[...] = jnp.zeros_like(dsk_ref)

        masks = _lane_masks()
        zero = jnp.zeros((WINDOW, LANES), F32)
        chains = [(m_, h) for m_ in range(nqb) for h in range(2)]
        kp = [kp_ref[...] if m_ == 0 else _rows128(kc_ref, m_ - 1) for m_ in range(nqb)]
        vp = [vp_ref[...] if m_ == 0 else _rows128(vc_ref, m_ - 1) for m_ in range(nqb)]
        kc = [_rows128(kc_ref, m_) for m_ in range(nqb)]
        vc = [_rows128(vc_ref, m_) for m_ in range(nqb)]
        do_b = [_rows128(do_ref, m_).astype(BF16) for m_ in range(nqb)]
        prod = [do_b[m_].astype(F32) * _rows128(o_ref, m_) for m_ in range(nqb)]
        lse = [_rows128(lse_ref, m_) for m_ in range(nqb)]
        qh, doh, logits, lse_h, delta = {}, {}, {}, {}, {}
        for m_, h in chains:
            q = _rows128(q_ref, m_)
            qh[m_, h] = jnp.where(masks[h], q, jnp.zeros_like(q))
            doh[m_, h] = jnp.where(masks[h], do_b[m_], jnp.zeros_like(do_b[m_]))
            logits[m_, h] = _swa_logits(qh[m_, h], kp[m_], kc[m_], b_ref[h], seq_start if m_ == 0 else False, scale)
            lse_h[m_, h] = lse[m_][:, h * HEAD:h * HEAD + 1]
            delta[m_, h] = jnp.sum(jnp.where(masks[h], prod[m_], 0.0), axis=1, keepdims=True)
        pr = {c: (jnp.exp(logits[c][0] - lse_h[c]), jnp.exp(logits[c][1] - lse_h[c])) for c in chains}
        dp = {(m_, h): (_dot_nt(doh[m_, h], jnp.where(masks[h], vp[m_], 0.0).astype(BF16)),
                        _dot_nt(doh[m_, h], jnp.where(masks[h], vc[m_], 0.0).astype(BF16))) for m_, h in chains}
        ds = {c: (pr[c][0] * (dp[c][0] - delta[c]), pr[c][1] * (dp[c][1] - delta[c])) for c in chains}
        dsb = {c: ((ds[c][0] * scale).astype(BF16), (ds[c][1] * scale).astype(BF16)) for c in chains}
        dk_acc = [zero] * (nqb + 1)
        dv_acc = [zero] * (nqb + 1)
        db_acc = [[jnp.zeros((WINDOW, WINDOW), F32)] * 2 for _ in range(2)]
        dsk_acc = [jnp.zeros((1, 1), F32)] * 2
        dq = [zero] * nqb
        for m_, h in chains:
            c = (m_, h)
            db_acc[h] = [db_acc[h][0] + ds[c][0], db_acc[h][1] + ds[c][1]]
            dsk_acc[h] = dsk_acc[h] - jnp.sum(jnp.exp(s_ref[h:h + 1, 0:1] - lse_h[c]) * delta[c], axis=0, keepdims=True)
            dq[m_] = (dq[m_] + _dot(dsb[c][0], jnp.where(masks[h], kp[m_], jnp.zeros_like(kp[m_])))
                      + _dot(dsb[c][1], jnp.where(masks[h], kc[m_], jnp.zeros_like(kc[m_]))))
            dk_acc[m_] = dk_acc[m_] + _dot_tn(dsb[c][0], qh[c])
            dk_acc[m_ + 1] = dk_acc[m_ + 1] + _dot_tn(dsb[c][1], qh[c])
            dv_acc[m_] = dv_acc[m_] + _dot_tn(pr[c][0].astype(BF16), doh[c])
            dv_acc[m_ + 1] = dv_acc[m_ + 1] + _dot_tn(pr[c][1].astype(BF16), doh[c])
        for m_ in range(nqb):
            dq_ref[m_ * WINDOW:(m_ + 1) * WINDOW, :] = dq[m_]
        for h in range(2):
            db_ref[h, :, 0:WINDOW] += db_acc[h][0]
            db_ref[h, :, WINDOW:2 * WINDOW] += db_acc[h][1]
            dsk_ref[h:h + 1, :] += jnp.broadcast_to(dsk_acc[h], (1, LANES))
        offp = pl.multiple_of(jnp.maximum(n * nqb - 1, 0) * WINDOW, WINDOW)
        dk_ref[pl.ds(offp, WINDOW), :] += dk_acc[0]
        dv_ref[pl.ds(offp, WINDOW), :] += dv_acc[0]
        for m_ in range(nqb):
            off = pl.multiple_of(n * rows + m_ * WINDOW, WINDOW)
            dk_ref[pl.ds(off, WINDOW), :] += dk_acc[m_ + 1]
            dv_ref[pl.ds(off, WINDOW), :] += dv_acc[m_ + 1]

    blk = pl.BlockSpec((None, rows, LANES), lambda b, p, n: (b, n, p))
    seq = pl.BlockSpec((None, S, LANES), lambda b, p, n: (b, 0, 0))
    return pl.pallas_call(
        body, name=name, grid=(Bl, 3, S // rows),
        in_specs=_swa_specs(P_SWV // LANES, nqb) + [blk, blk, pl.BlockSpec((None, rows, LANES), lambda b, p, n: (b, n, do_blk0 + p))],
        out_specs=[blk, seq, seq, pl.BlockSpec((None, None, 2, WINDOW, 2 * WINDOW), lambda b, p, n: (b, p, 0, 0, 0)),
                   pl.BlockSpec((None, None, 2, LANES), lambda b, p, n: (b, p, 0, 0))],
        out_shape=[jax.ShapeDtypeStruct((Bl, S, 3 * LANES), F32), jax.ShapeDtypeStruct((Bl, S, LANES), F32), jax.ShapeDtypeStruct((Bl, S, LANES), F32),
                   jax.ShapeDtypeStruct((Bl, 3, 2, WINDOW, 2 * WINDOW), F32), jax.ShapeDtypeStruct((Bl, 3, 2, LANES), F32)],
        compiler_params=_cp("arbitrary", "arbitrary", "arbitrary"),
    )(qn3, kn3, kn3, proj3, proj3, bias, sinks, o3, lse3, do3)


CONV_ROWS = 64
CONV_LANES = 128


def _conv_strip(x_ref, h_ref, w, b, r0, cols, first_blk):
    x = x_ref[r0:r0 + CONV_ROWS, cols]
    if r0 == 0:
        rows = lax.broadcasted_iota(jnp.int32, x.shape, 0)
        h6 = jnp.where(first_blk, 0.0, h_ref[6:7, cols])
        h7 = jnp.where(first_blk, 0.0, h_ref[7:8, cols])
        x1 = jnp.where(rows == 0, h7, pltpu.roll(x, 1, 0))
        x2 = jnp.where(rows == 0, h6, jnp.where(rows == 1, h7, pltpu.roll(x, 2, 0)))
    else:
        x1 = x_ref[r0 - 1:r0 - 1 + CONV_ROWS, cols]
        x2 = x_ref[r0 - 2:r0 - 2 + CONV_ROWS, cols]
    return w[0:1] * x2 + w[1:2] * x1 + w[2:3] * x + b, x, x1, x2


FF_BLK = D_FF // 2


def _up_perm(a):
    q = FF_BLK
    return _cat([a[..., 0:q], a[..., 2 * q:3 * q], a[..., q:2 * q], a[..., 3 * q:4 * q]])


def conv_gate_fwd(up3, cw, cb, *, tm=512, name):
    Bl, S, _ = up3.shape
    tm = min(tm, S)
    W = 2 * FF_BLK

    def body(x_ref, h_ref, w_ref, b_ref, o_ref):
        first = pl.program_id(1) == 0

        def chunk(c, carry):
            cg = pl.ds(pl.multiple_of(c * CONV_LANES, CONV_LANES), CONV_LANES)
            cv = pl.ds(pl.multiple_of(FF_BLK + c * CONV_LANES, CONV_LANES), CONV_LANES)
            wg, wv, bg, bv = w_ref[:, cg], w_ref[:, cv], b_ref[:, cg], b_ref[:, cv]
            for r0 in range(0, tm, CONV_ROWS):
                ug = _conv_strip(x_ref, h_ref, wg, bg, r0, cg, first)[0]
                uv = _conv_strip(x_ref, h_ref, wv, bv, r0, cv, first)[0]
                o_ref[r0:r0 + CONV_ROWS, cg] = (ug * jax.nn.sigmoid(ug) * uv).astype(BF16)
            return carry

        lax.fori_loop(0, FF_BLK // CONV_LANES, chunk, 0)

    hb = tm // 8
    return pl.pallas_call(
        body, name=name, grid=(Bl, S // tm, 2),
        in_specs=[pl.BlockSpec((None, tm, W), lambda b, s, c: (b, s, c)),
                  pl.BlockSpec((None, 8, W), lambda b, s, c: (b, jnp.maximum(s * hb - 1, 0), c)),
                  pl.BlockSpec((3, W), lambda b, s, c: (0, c)), pl.BlockSpec((1, W), lambda b, s, c: (0, c))],
        out_specs=pl.BlockSpec((None, tm, FF_BLK), lambda b, s, c: (b, s, c)),
        out_shape=jax.ShapeDtypeStruct((Bl, S, D_FF), BF16),
        compiler_params=_cp("parallel", "parallel", "parallel"),
    )(up3, up3, cw, cb)


def conv_gate_bwd(up3, cw, cb, da3, *, tm=512, name):
    Bl, S, _ = up3.shape
    tm = min(tm, S)
    ns = S // tm
    W = 2 * FF_BLK

    def body(x_ref, h_ref, w_ref, b_ref, da_ref, dup_ref, dw_ref, nxt_ref, du_scr):
        b, s = pl.program_id(1), pl.program_id(2)
        seq_end = s == 0
        first = s == ns - 1

        @pl.when((b == 0) & seq_end)
        def _():
            dw_ref[...] = jnp.zeros_like(dw_ref)

        def du_chunk(c, carry):
            cg = pl.ds(pl.multiple_of(c * CONV_LANES, CONV_LANES), CONV_LANES)
            cv = pl.ds(pl.multiple_of(FF_BLK + c * CONV_LANES, CONV_LANES), CONV_LANES)
            wg, wv, bg, bv = w_ref[:, cg], w_ref[:, cv], b_ref[:, cg], b_ref[:, cv]
            acc_g = [jnp.zeros((1, CONV_LANES), F32)] * 4
            acc_v = [jnp.zeros((1, CONV_LANES), F32)] * 4
            for r0 in range(0, tm, CONV_ROWS):
                ug, xg, xg1, xg2 = _conv_strip(x_ref, h_ref, wg, bg, r0, cg, first)
                uv, xv, xv1, xv2 = _conv_strip(x_ref, h_ref, wv, bv, r0, cv, first)
                da = da_ref[r0:r0 + CONV_ROWS, cg].astype(F32)
                sg = jax.nn.sigmoid(ug)
                dug = da * uv * sg * (1.0 + ug * (1.0 - sg))
                duv = da * ug * sg
                du_scr[r0:r0 + CONV_ROWS, cg] = dug
                du_scr[r0:r0 + CONV_ROWS, cv] = duv
                col = lambda t: jnp.sum(t, axis=0, keepdims=True)
                acc_g = [acc_g[0] + col(dug * xg2), acc_g[1] + col(dug * xg1), acc_g[2] + col(dug * xg), acc_g[3] + col(dug)]
                acc_v = [acc_v[0] + col(duv * xv2), acc_v[1] + col(duv * xv1), acc_v[2] + col(duv * xv), acc_v[3] + col(duv)]
            for t in range(4):
                dw_ref[t:t + 1, cg] += acc_g[t]
                dw_ref[t:t + 1, cv] += acc_v[t]
            return carry

        lax.fori_loop(0, FF_BLK // CONV_LANES, du_chunk, 0)
        du_scr[tm:tm + 8, :] = jnp.where(seq_end, 0.0, nxt_ref[...])

        def dup_chunk(c, carry):
            cols = pl.ds(pl.multiple_of(c * CONV_LANES, CONV_LANES), CONV_LANES)
            w = w_ref[:, cols]
            for r0 in range(0, tm, CONV_ROWS):
                d0 = du_scr[r0:r0 + CONV_ROWS, cols]
                d1 = du_scr[r0 + 1:r0 + 1 + CONV_ROWS, cols]
                d2 = du_scr[r0 + 2:r0 + 2 + CONV_ROWS, cols]
                dup_ref[r0:r0 + CONV_ROWS, cols] = (w[2:3] * d0 + w[1:2] * d1 + w[0:1] * d2).astype(BF16)
            return carry

        lax.fori_loop(0, W // CONV_LANES, dup_chunk, 0)
        nxt_ref[...] = du_scr[0:8, :]

    hb = tm // 8
    rb = lambda s: ns - 1 - s
    return pl.pallas_call(
        body, name=name, grid=(2, Bl, ns),
        in_specs=[pl.BlockSpec((None, tm, W), lambda c, b, s: (b, rb(s), c)),
                  pl.BlockSpec((None, 8, W), lambda c, b, s: (b, jnp.maximum(rb(s) * hb - 1, 0), c)),
                  pl.BlockSpec((3, W), lambda c, b, s: (0, c)), pl.BlockSpec((1, W), lambda c, b, s: (0, c)),
                  pl.BlockSpec((None, tm, FF_BLK), lambda c, b, s: (b, rb(s), c))],
        out_specs=[pl.BlockSpec((None, tm, W), lambda c, b, s: (b, rb(s), c)), pl.BlockSpec((8, W), lambda c, b, s: (0, c))],
        out_shape=[jax.ShapeDtypeStruct((Bl, S, 2 * D_FF), BF16), jax.ShapeDtypeStruct((8, 2 * D_FF), F32)],
        scratch_shapes=[pltpu.VMEM((8, W), F32), pltpu.VMEM((tm + 8, W), F32)],
        compiler_params=_cp("arbitrary", "arbitrary", "arbitrary"),
    )(up3, up3, cw, cb, da3)


def cast_layer(w3, l, *, name):
    _, R, C = w3.shape
    tr = _tile(R, 512, 16)

    def body(w_ref, o_ref):
        o_ref[...] = w_ref[...].astype(BF16)

    return pl.pallas_call(
        body, name=name, grid=(R // tr,), in_specs=[pl.BlockSpec((None, tr, C), lambda i: (l, i, 0))],
        out_specs=pl.BlockSpec((tr, C), lambda i: (i, 0)), out_shape=jax.ShapeDtypeStruct((R, C), BF16),
        compiler_params=_cp("parallel"),
    )(w3)


def gate_bwd(dx3, y3, gate, *, tm=512, name):
    Bl, S, D = dx3.shape
    tm = min(tm, S)

    def body(dx_ref, y_ref, g_ref, o_ref, dg_ref):
        @pl.when(pl.program_id(1) == 0)
        def _():
            dg_ref[...] = jnp.zeros_like(dg_ref)

        dx = dx_ref[...]
        dg_ref[...] += jnp.sum(dx * y_ref[...], axis=0, keepdims=True)
        o_ref[...] = (dx * g_ref[...]).astype(BF16)

    blk = pl.BlockSpec((None, tm, D), lambda b, s: (b, s, 0))
    vec = pl.BlockSpec((None, 1, D), lambda b, s: (b, 0, 0))
    return pl.pallas_call(
        body, name=name, grid=(Bl, S // tm), in_specs=[blk, blk, vec], out_specs=[blk, vec],
        out_shape=[jax.ShapeDtypeStruct((Bl, S, D), BF16), jax.ShapeDtypeStruct((Bl, 1, D), F32)],
        compiler_params=_cp("parallel", "arbitrary"),
    )(dx3, y3, gate)


def loss_grad(y3, t3, *, tm=512, name):
    Bl, S, D = y3.shape
    tm = min(tm, S)
    last = (Bl - 1, S // tm - 1)

    def body(y_ref, t_ref, dy_ref, l_ref, acc_ref):
        b, s = pl.program_id(0), pl.program_id(1)

        @pl.when((b == 0) & (s == 0))
        def _():
            acc_ref[...] = jnp.zeros_like(acc_ref)

        e = y_ref[...] - t_ref[...]
        dy_ref[...] = e * (1.0 / D)
        acc_ref[...] += jnp.sum(e * e, axis=0, keepdims=True)

        @pl.when((b == last[0]) & (s == last[1]))
        def _():
            l_ref[...] = jnp.broadcast_to(jnp.sum(acc_ref[...], axis=1, keepdims=True) * (0.5 / D), (1, LANES))

    blk = pl.BlockSpec((None, tm, D), lambda b, s: (b, s, 0))
    return pl.pallas_call(
        body, name=name, grid=(Bl, S // tm), in_specs=[blk, blk],
        out_specs=[blk, pl.BlockSpec((1, LANES), lambda b, s: (0, 0))],
        out_shape=[jax.ShapeDtypeStruct((Bl, S, D), F32), jax.ShapeDtypeStruct((1, LANES), F32)],
        scratch_shapes=[pltpu.VMEM((1, D), F32)], compiler_params=_cp("arbitrary", "arbitrary"),
    )(y3, t3)


def adamw(w, g, m, v, *, name):
    L, R, C = w.shape
    tr = _tile(R, 512, 8)

    def body(w_ref, g_ref, m_ref, v_ref, d_ref, m2_ref, v2_ref):
        d_ref[...], m2_ref[...], v2_ref[...] = _adam_update(w_ref[...], g_ref[...], m_ref[...], v_ref[...])

    blk = pl.BlockSpec((None, tr, C), lambda l, i: (l, i, 0))
    shp = jax.ShapeDtypeStruct((L, R, C), F32)
    return pl.pallas_call(
        body, name=name, grid=(L, R // tr), in_specs=[blk] * 4, out_specs=[blk] * 3, out_shape=[shp] * 3,
        compiler_params=_cp("parallel", "parallel"),
    )(w, g, m, v)


def _adam_update(w, g, m, v):
    c1 = 1.0 / (1.0 - ADAM_B1 ** ADAM_STEP)
    c2 = 1.0 / (1.0 - ADAM_B2 ** ADAM_STEP)
    m2 = ADAM_B1 * m + (1.0 - ADAM_B1) * g
    v2 = ADAM_B2 * v + (1.0 - ADAM_B2) * (g * g)
    return -ADAM_LR * ((m2 * c1) / (jnp.sqrt(v2 * c2) + ADAM_EPS) + ADAM_WD * w), m2, v2


def adamw_small(ws, gs, ms, vs, *, name):
    na = len(ws)

    def body(*refs):
        w_r, g_r, m_r, v_r = (refs[i * na:(i + 1) * na] for i in range(4))
        d_r, m2_r, v2_r = (refs[(4 + i) * na:(5 + i) * na] for i in range(3))
        for a in range(na):
            d_r[a][...], m2_r[a][...], v2_r[a][...] = _adam_update(w_r[a][...], g_r[a][...], m_r[a][...], v_r[a][...])

    vm = pl.BlockSpec(memory_space=pltpu.VMEM)
    shp = [jax.ShapeDtypeStruct(w.shape, F32) for w in ws]
    out = pl.pallas_call(body, name=name, in_specs=[vm] * (4 * na), out_specs=[vm] * (3 * na), out_shape=shp * 3)(*ws, *gs, *ms, *vs)
    return out[:na], out[na:2 * na], out[2 * na:]


def sum_small(xs, *, name):
    na = len(xs)

    def body(*refs):
        for x_ref, o_ref in zip(refs[:na], refs[na:]):
            acc = x_ref[0]
            for k in range(1, x_ref.shape[0]):
                acc = acc + x_ref[k]
            o_ref[...] = acc

    vm = pl.BlockSpec(memory_space=pltpu.VMEM)
    return pl.pallas_call(body, name=name, in_specs=[vm] * na, out_specs=[vm] * na,
                          out_shape=[jax.ShapeDtypeStruct(x.shape[1:], x.dtype) for x in xs])(*xs)


def pair_add_half(g4, recv, c_arr, *, tr=512, name):
    _, R, C = g4.shape
    H = R // 2
    tr = _tile(H, tr, 16)
    nb = H // tr

    def body(c_ref, g_ref, r_ref, o_ref):
        o_ref[...] = (g_ref[...].astype(F32) + r_ref[...].astype(F32)).astype(BF16)

    grid_spec = pltpu.PrefetchScalarGridSpec(
        num_scalar_prefetch=1, grid=(4, nb),
        in_specs=[pl.BlockSpec((None, tr, C), lambda k, i, c_ref: (k, c_ref[0] * nb + i, 0)),
                  pl.BlockSpec((None, tr, C), lambda k, i, c_ref: (k, i, 0))],
        out_specs=pl.BlockSpec((None, tr, C), lambda k, i, c_ref: (k, i, 0)),
    )
    return pl.pallas_call(
        body, name=name, grid_spec=grid_spec, out_shape=jax.ShapeDtypeStruct((4, H, C), BF16),
        compiler_params=_cp("parallel", "parallel"),
    )(c_arr, g4, recv)


def chip_sum_into(landed, pair, sel, *, tr=512, name):
    _, H, C = landed.shape
    tr = _tile(H, tr, 16)
    nb = H // tr

    def body(s_ref, l0, l1, l2, l3, p_ref, o_ref):
        own = p_ref[...].astype(F32)
        acc = None
        for k, l_ref in enumerate((l0, l1, l2, l3)):
            part = jnp.where(s_ref[0] == k, own, l_ref[...].astype(F32))
            acc = part if acc is None else acc + part
        o_ref[...] = acc

    def slot(k):
        return pl.BlockSpec((None, tr, C), lambda i, s: (jnp.where(s[0] == k, (k + 1) % 4, k), i, 0))

    grid_spec = pltpu.PrefetchScalarGridSpec(
        num_scalar_prefetch=1, grid=(nb,),
        in_specs=[slot(0), slot(1), slot(2), slot(3), pl.BlockSpec((None, tr, C), lambda i, s: (s[0], i, 0))],
        out_specs=pl.BlockSpec((tr, C), lambda i, s: (s[1] * nb + i, 0)),
    )
    return pl.pallas_call(
        body, name=name, grid_spec=grid_spec, out_shape=jax.ShapeDtypeStruct((2 * H, C), F32), compiler_params=_cp("parallel"),
    )(sel, landed, landed, landed, landed, pair)


def mods_matmul(c_all, w_ada, b_ada_cols, *, tn=512, name):
    L, D, E = w_ada.shape
    nb = c_all.shape[0]
    tn = _tile(E, tn)

    def body(c_ref, w_ref, b_ref, o_ref):
        c = c_ref[...]
        a = c * jax.nn.sigmoid(c)
        o_ref[...] = jnp.dot(a, w_ref[...], preferred_element_type=F32, precision=lax.Precision.HIGHEST) + b_ref[...]

    return pl.pallas_call(
        body, name=name, grid=(L, E // tn),
        in_specs=[pl.BlockSpec((nb, D), lambda l, j: (0, 0)), pl.BlockSpec((None, D, tn), lambda l, j: (l, 0, j)),
                  pl.BlockSpec((None, 1, tn), lambda l, j: (l, 0, j))],
        out_specs=pl.BlockSpec((None, nb, tn), lambda l, j: (l, 0, j)),
        out_shape=jax.ShapeDtypeStruct((L, nb, E), F32), compiler_params=_cp("parallel", "parallel"),
    )(c_all, w_ada, b_ada_cols)


def ada_grad(c_all, dmods, *, tn=512, name):
    L, nb, E = dmods.shape
    D = c_all.shape[1]
    tn = _tile(E, tn)

    def body(c_ref, d_ref, o_ref):
        c = c_ref[...]
        a = c * jax.nn.sigmoid(c)
        o_ref[...] = lax.dot_general(a, d_ref[...], (((0,), (0,)), ((), ())), preferred_element_type=F32, precision=lax.Precision.HIGHEST)

    return pl.pallas_call(
        body, name=name, grid=(L, E // tn),
        in_specs=[pl.BlockSpec((nb, D), lambda l, j: (0, 0)), pl.BlockSpec((None, nb, tn), lambda l, j: (l, 0, j))],
        out_specs=pl.BlockSpec((None, D, tn), lambda l, j: (l, 0, j)),
        out_shape=jax.ShapeDtypeStruct((L, D, E), F32), compiler_params=_cp("parallel", "parallel"),
    )(c_all, dmods)


HBM = pl.BlockSpec(memory_space=pltpu.HBM)


def _me():
    return lax.axis_index("x"), lax.axis_index("y"), lax.axis_index("c")


def _flip(v, bit):
    return 1 - v if bit else v


def allgather8(xs, *, name):
    na = len(xs)

    def body(*refs):
        x_refs, out_refs = refs[:na], refs[na:2 * na]
        send_sems, recv_sems = refs[2 * na], refs[2 * na + 1]
        x, y, c = _me()
        me = 4 * x + 2 * y + c
        for x_ref, out_ref in zip(x_refs, out_refs):
            out_ref[me] = x_ref[...]
        sends = []
        for a, (x_ref, out_ref) in enumerate(zip(x_refs, out_refs)):
            for k in range(1, 8):
                peer = (_flip(x, k & 4), _flip(y, k & 2), _flip(c, k & 1))
                cp = pltpu.make_async_remote_copy(src_ref=x_ref, dst_ref=out_ref.at[me], send_sem=send_sems.at[a, k - 1],
                                                  recv_sem=recv_sems.at[a, k - 1], device_id=peer, device_id_type=MESH)
                cp.start()
                sends.append(cp)
        for a, (x_ref, out_ref) in enumerate(zip(x_refs, out_refs)):
            for k in range(1, 8):
                peer = (_flip(x, k & 4), _flip(y, k & 2), _flip(c, k & 1))
                src = 4 * peer[0] + 2 * peer[1] + peer[2]
                pltpu.make_async_remote_copy(src_ref=x_ref, dst_ref=out_ref.at[src], send_sem=send_sems.at[a, k - 1],
                                             recv_sem=recv_sems.at[a, k - 1], device_id=peer, device_id_type=MESH).wait_recv()
        for cp in sends:
            cp.wait_send()

    vm = pl.BlockSpec(memory_space=pltpu.VMEM)
    return pl.pallas_call(
        body, name=name, in_specs=[vm] * na, out_specs=[vm] * na,
        out_shape=[jax.ShapeDtypeStruct((8,) + a.shape, a.dtype) for a in xs],
        scratch_shapes=[pltpu.SemaphoreType.DMA((na, 7)), pltpu.SemaphoreType.DMA((na, 7))],
    )(*xs)


class _Plan:
    def __init__(self, ins, out_shapes, ncopies, copies, aliased=False):
        self.ins, self.out_shapes, self.ncopies, self.copies, self.aliased = list(ins), list(out_shapes), ncopies, copies, aliased

    def start(self, in_refs, out_refs, send_sems, recv_sems):
        sends, _ = self.copies(in_refs, out_refs, send_sems, recv_sems)
        for cp in sends:
            cp.start()

    def finish(self, in_refs, out_refs, send_sems, recv_sems):
        sends, recvs = self.copies(in_refs, out_refs, send_sems, recv_sems)
        for cp in recvs:
            cp.wait_recv()
        for cp in sends:
            cp.wait_send()


def _rcopy(src, dst, send_sems, recv_sems, idx, dev):
    return pltpu.make_async_remote_copy(src_ref=src, dst_ref=dst, send_sem=send_sems.at[idx], recv_sem=recv_sems.at[idx],
                                        device_id=dev, device_id_type=MESH)


def _other_chips(x, y):
    return [(_flip(x, k & 2), _flip(y, k & 1)) for k in range(1, 4)]


def plan_gather_ici(ws):
    def copies(in_refs, out_refs, ss, rs):
        x, y, c = _me()
        j = 2 * x + y
        sends, recvs = [], []
        for a, (x_ref, out_ref) in enumerate(zip(in_refs, out_refs)):
            H = x_ref.shape[0] // 2
            for k, (px, py) in enumerate(_other_chips(x, y)):
                sends.append(_rcopy(x_ref.at[pl.ds(c * H, H)], out_ref.at[j, pl.ds(c * H, H)], ss, rs, 3 * a + k, (px, py, c)))
                slot = out_ref.at[2 * px + py, pl.ds(c * H, H)]
                recvs.append(_rcopy(slot, slot, ss, rs, 3 * a + k, (px, py, c)))
        return sends, recvs

    return _Plan(ws, [jax.ShapeDtypeStruct((4,) + w.shape, w.dtype) for w in ws], 3 * len(ws), copies)


def plan_gather_d2d(w4s):
    def copies(in_refs, out_refs, ss, rs):
        x, y, c = _me()
        sends, recvs = [], []
        for a, out_ref in enumerate(out_refs):
            H = out_ref.shape[1] // 2
            for k, (px, py) in enumerate(_other_chips(x, y)):
                mine = out_ref.at[2 * px + py, pl.ds(c * H, H)]
                theirs = out_ref.at[2 * px + py, pl.ds((1 - c) * H, H)]
                sends.append(_rcopy(mine, mine, ss, rs, 3 * a + k, (x, y, 1 - c)))
                recvs.append(_rcopy(theirs, theirs, ss, rs, 3 * a + k, (x, y, 1 - c)))
        return sends, recvs

    return _Plan(w4s, [jax.ShapeDtypeStruct(w.shape, w.dtype) for w in w4s], 3 * len(w4s), copies, aliased=True)


def plan_swap_halves(gs):
    def copies(in_refs, out_refs, ss, rs):
        x, y, c = _me()
        sends, recvs = [], []
        for a, (g_ref, out_ref) in enumerate(zip(in_refs, out_refs)):
            H = g_ref.shape[1] // 2
            for k in range(4):
                sends.append(_rcopy(g_ref.at[k, pl.ds((1 - c) * H, H)], out_ref.at[k], ss, rs, 4 * a + k, (x, y, 1 - c)))
                recvs.append(_rcopy(g_ref.at[k, pl.ds(c * H, H)], out_ref.at[k], ss, rs, 4 * a + k, (x, y, 1 - c)))
        return sends, recvs

    return _Plan(gs, [jax.ShapeDtypeStruct((4, g.shape[1] // 2, g.shape[2]), g.dtype) for g in gs], 4 * len(gs), copies)


def plan_scatter_ici(ps):
    def copies(in_refs, out_refs, ss, rs):
        x, y, c = _me()
        j = 2 * x + y
        sends, recvs = [], []
        for a, (p_ref, out_ref) in enumerate(zip(in_refs, out_refs)):
            for k, (px, py) in enumerate(_other_chips(x, y)):
                sends.append(_rcopy(p_ref.at[2 * px + py], out_ref.at[j], ss, rs, 3 * a + k, (px, py, c)))
                slot = out_ref.at[2 * px + py]
                recvs.append(_rcopy(slot, slot, ss, rs, 3 * a + k, (px, py, c)))
        return sends, recvs

    return _Plan(ps, [jax.ShapeDtypeStruct(p.shape, p.dtype) for p in ps], 3 * len(ps), copies)


def plan_join_halves(fulls):
    def copies(in_refs, out_refs, ss, rs):
        x, y, c = _me()
        sends, recvs = [], []
        for a, out_ref in enumerate(out_refs):
            H = out_ref.shape[0] // 2
            mine, theirs = out_ref.at[pl.ds(c * H, H)], out_ref.at[pl.ds((1 - c) * H, H)]
            sends.append(_rcopy(mine, mine, ss, rs, a, (x, y, 1 - c)))
            recvs.append(_rcopy(theirs, theirs, ss, rs, a, (x, y, 1 - c)))
        return sends, recvs

    return _Plan(fulls, [jax.ShapeDtypeStruct(f.shape, f.dtype) for f in fulls], len(fulls), copies, aliased=True)


def call_with_plans(body, plans, *, grid, in_specs, out_specs, out_shape, scratch_shapes, args, sem, name):
    plans = list(plans or [])
    n_in, n_out, n_scr = len(in_specs), len(out_specs), len(scratch_shapes)
    c_in = [len(p.ins) for p in plans]
    c_out = [len(p.out_shapes) for p in plans]
    steps = math.prod(grid) if grid else 1

    def wrapped(*refs):
        pos = 0

        def take(n):
            nonlocal pos
            out = refs[pos:pos + n]
            pos += n
            return out

        ins = take(n_in)
        cins = [take(n) for n in c_in]
        outs = take(n_out)
        couts = [take(n) for n in c_out]
        scr = take(n_scr)
        sems = [take(2) for _ in plans]
        def start_all():
            for p, ci, co, (ss, rs) in zip(plans, cins, couts, sems):
                p.start(ci, co, ss, rs)

        def finish_all():
            for p, ci, co, (ss, rs) in zip(plans, cins, couts, sems):
                p.finish(ci, co, ss, rs)

        if plans and grid:
            idx = 0
            for ax, g in enumerate(grid):
                idx = idx * g + pl.program_id(ax)
            pl.when(idx == 0)(start_all)
        elif plans:
            start_all()
        if body is not None:
            body(*ins, *outs, *scr)
        if plans and grid:
            pl.when(idx == steps - 1)(finish_all)
        elif plans:
            finish_all()

    aliases = {}
    i_pos, o_pos = n_in, n_out
    for p, ni, no in zip(plans, c_in, c_out):
        if p.aliased:
            aliases.update({i_pos + t: o_pos + t for t in range(ni)})
        i_pos += ni
        o_pos += no
    kwargs = dict(grid=grid) if grid else {}
    if aliases:
        kwargs["input_output_aliases"] = aliases
    res = pl.pallas_call(
        wrapped, name=name, in_specs=list(in_specs) + [HBM] * sum(c_in), out_specs=list(out_specs) + [HBM] * sum(c_out),
        out_shape=list(out_shape) + [s for p in plans for s in p.out_shapes],
        scratch_shapes=list(scratch_shapes) + [pltpu.SemaphoreType.DMA((p.ncopies,)) for p in plans for _ in range(2)],
        compiler_params=_cp(*sem) if grid else pltpu.CompilerParams(vmem_limit_bytes=VMEM_LIMIT), **kwargs,
    )(*args, *[a for p in plans for a in p.ins])
    res = list(res)
    comp, rest = res[:n_out], res[n_out:]
    pouts = []
    for no in c_out:
        pouts.append(rest[:no])
        rest = rest[no:]
    return comp, pouts


def run_plans(plans, *, name):
    return call_with_plans(None, plans, grid=(), in_specs=[], out_specs=[], out_shape=[], scratch_shapes=[], args=[], sem=(), name=name)[1]


def _cat(parts, axis=-1):
    return jnp.concatenate(parts, axis=axis)


def _pairs_of_heads(a, axis, inverse=False):
    lead, tail = a.shape[:axis], a.shape[axis + 1:]
    split = (3, 2) if inverse else (2, 3)
    a = a.reshape(lead + split + (HEAD,) + tail)
    return jnp.swapaxes(a, axis, axis + 1).reshape(lead + (6 * HEAD,) + tail)


def _prep_w_in(w):
    z = lambda n: jnp.zeros((w.shape[0], n), w.dtype)
    return _cat([w[:, 0:1152], z(64), w[:, 1152:1184], z(32), _pairs_of_heads(w[:, 1184:1568], 1), w[:, 1568:1824]])


def _unprep_w_in(g):
    return _cat([g[:, 0:1152], g[:, 1216:1248], _pairs_of_heads(g[:, P_SWQ:P_SWK], 1, inverse=True), g[:, P_SWK:P_END]])


def _prep_w_uq(w):
    r = w.shape[0]
    return jnp.pad(w.reshape(r, 6, MLA_QK), ((0, 0), (0, 0), (0, LANES - MLA_QK))).reshape(r, 6 * LANES)


def _unprep_w_uq(g):
    r = g.shape[0]
    return g.reshape(r, 6, LANES)[:, :, :MLA_QK].reshape(r, 6 * MLA_QK)


def _prep_w_ukv(w):
    r = w.shape[0]
    w3 = w.reshape(r, 6, LANES)
    k = jnp.pad(w3[:, :, :HEAD], ((0, 0), (0, 0), (0, LANES - HEAD))).reshape(r, 6 * LANES)
    return _cat([k, w3[:, :, HEAD:].reshape(r, 6 * HEAD)])


def _unprep_w_ukv(g):
    r = g.shape[0]
    k = g[:, :6 * LANES].reshape(r, 6, LANES)[:, :, :HEAD]
    return _cat([k, g[:, 6 * LANES:].reshape(r, 6, HEAD)], axis=2).reshape(r, 6 * LANES)


def _prep_w_out(w):
    return _cat([w[0:640], _pairs_of_heads(w[640:], 0)], axis=0)


def _unprep_w_out(g):
    return _cat([g[0:640], _pairs_of_heads(g[640:], 0, inverse=True)], axis=0)


def _rope_tables(positions):
    half = 16
    inv_freq = jnp.power(ROPE_THETA, -jnp.arange(half, dtype=F32) / half)
    ang = positions.astype(F32)[..., None] * inv_freq
    cos, sin = jnp.cos(ang), jnp.sin(ang)
    z = lambda n: jnp.zeros(ang.shape[:-1] + (n,), F32)
    return (_cat([jnp.ones(ang.shape[:-1] + (HEAD,), F32), cos, cos, z(32)]), _cat([z(HEAD), -sin, z(16), z(32)]), _cat([z(HEAD), z(16), sin, z(32)]))


def _small_params(p):
    pad96 = lambda g: _cat([g, jnp.zeros((32,), F32)]).reshape(1, LANES)
    two = lambda g: _cat([g, g]).reshape(1, LANES)
    sinks = jnp.broadcast_to(p["sw_sinks"].reshape(2, 3).T[:, :, None], (3, 2, LANES))
    return dict(n1=p["norm1_g"].reshape(1, -1), n2=p["norm2_g"].reshape(1, -1), cq_g=p["mla_cq_g"].reshape(1, -1),
                ckv_g=p["mla_ckv_g"].reshape(1, -1), qn_g=pad96(p["mla_qn_g"]), kn_g=pad96(p["mla_kn_g"]),
                swq_g=two(p["sw_qn_g"]), swk_g=two(p["sw_kn_g"]), sinks=sinks, conv_b=_up_perm(p["conv_b"]).reshape(1, -1))


class _NoFlow:
    def plans(self, tag):
        return []

    def done(self, tag, outs):
        pass

    def add(self, key, g):
        pass


def _layer_fwd(x3, md, W, tabs, bias, tag, flow=_NoFlow()):
    Bl, S, D = x3.shape
    T = Bl * S
    n = lambda s: f"{s}_{tag}"
    two = lambda a: a.reshape(T, a.shape[-1])
    three = lambda a: a.reshape(Bl, S, a.shape[-1])
    h = rms_fwd(x3, 0, D, W["n1"], md["scale1"], md["shift1"], name=n("norm1"))
    proj = three(matmul(two(h), W["w_in"], tn=1920, name=n("in_proj")))
    (o_a, rt_a), got = sb_attn_fwd(proj, plans=flow.plans(n("sb_fwd")), name=n("sb_fwd"))
    flow.done(n("sb_fwd"), got)
    cqn = rms_fwd(proj, P_CQ // 256, 256, W["cq_g"], name=n("cq_norm"))
    ckvn = rms_fwd(proj, P_CKV // LANES, LANES, W["ckv_g"], name=n("ckv_norm"))
    qb = three(matmul(two(cqn), W["w_uq"], tm=1024, tn=768, name=n("uq")))
    kvb = three(matmul(two(ckvn), W["w_ukv"], tm=1024, tn=1152, name=n("ukv")))
    q_m = rope_norm_fwd(qb, 6, W["qn_g"], tabs, name=n("q_rope"))
    k_m = rope_norm_fwd(kvb, 6, W["kn_g"], tabs, (proj, P_SLAB // LANES), name=n("k_rope"))
    (o_b, lse_b), got = mla_attn_fwd(q_m, k_m, kvb, 6, plans=flow.plans(n("mla_fwd")), name=n("mla_fwd"))
    flow.done(n("mla_fwd"), got)
    q_c = pair_rms_fwd(proj, P_SWQ // LANES, 3, W["swq_g"], name=n("swq_norm"))
    k_c = pair_rms_fwd(proj, P_SWK // LANES, 1, W["swk_g"], name=n("swk_norm"))
    (o_c, lse_c), got = swa_attn_fwd(q_c, k_c, proj, bias, W["sinks"], plans=flow.plans(n("swa_fwd")), name=n("swa_fwd"))
    flow.done(n("swa_fwd"), got)
    mix = _cat([o_a, o_b, o_c]).astype(BF16)
    att, x1 = matmul_res(two(mix), W["w_out"], two(x3), md["gate1"], S, name=n("out_proj"))
    x1 = three(x1)
    h2 = rms_fwd(x1, 0, D, W["n2"], md["scale2"], md["shift2"], name=n("norm2"))
    up = three(matmul(two(h2), W["w_up"], tm=1024, tn=1408, name=n("up_proj")))
    a = conv_gate_fwd(up, W["conv_w"], W["conv_b"], name=n("conv_gate"))
    yd, x2 = matmul_res(two(a), W["w_down"], two(x1), md["gate2"], S, name=n("down_proj"))
    saved = dict(x=x3, h=h, proj=proj, rt_a=rt_a, cqn=cqn, ckvn=ckvn, qb=qb, kvb=kvb, q_m=q_m, k_m=k_m, o_b=o_b, lse_b=lse_b,
                 q_c=q_c, k_c=k_c, o_c=o_c, lse_c=lse_c, mix=mix, att=three(att), x1=x1, h2=h2, up=up, a=a, yd=three(yd))
    return three(x2), saved


def _layer_bwd(dx2, sv, md, W, tabs, bias, tag, flow=_NoFlow()):
    Bl, S, D = dx2.shape
    T = Bl * S
    n = lambda s: f"{s}_{tag}"
    two = lambda a: a.reshape(T, a.shape[-1])
    three = lambda a: a.reshape(Bl, S, a.shape[-1])
    g = {}
    dyb, dgate2 = gate_bwd(dx2, sv["yd"], md["gate2"], name=n("gate2_bwd"))
    da = three(matmul(two(dyb), W["w_down"], tb=True, tm=1024, tn=1408, name=n("down_dx")))
    g["w_down"] = matmul(two(sv["a"]), two(dyb), ta=True, tm=256, tn=1024, out_dtype=BF16, name=n("down_dw"))
    dup, dcw = conv_gate_bwd(sv["up"], W["conv_w"], W["conv_b"], da, name=n("conv_gate_bwd"))
    dh2 = three(matmul(two(dup), W["w_up"], tb=True, tn=1024, name=n("up_dx")))
    g["w_up"] = matmul(two(sv["h2"]), two(dup), ta=True, tn=1408, out_dtype=BF16, name=n("up_dw"))
    dx1, dn2, dsc2, dsh2 = rms_bwd(sv["x1"], 0, D, dh2, W["n2"], md["scale2"], dx2, name=n("norm2_bwd"))
    dmo, dgate1 = gate_bwd(dx1, sv["att"], md["gate1"], name=n("gate1_bwd"))
    dmix = three(matmul(two(dmo), W["w_out"], tb=True, tn=1024, out_dtype=BF16, name=n("out_dx")))
    g["w_out"] = matmul(two(sv["mix"]), two(dmo), ta=True, tn=1024, out_dtype=BF16, name=n("out_dw"))
    proj = sv["proj"]
    for k in ("w_down", "w_up", "w_out"):
        flow.add((tag, k), g[k])
    (dq_a, dk_a, dv_a), got = sb_attn_bwd(proj, sv["rt_a"], dmix, do_blk0=0, plans=flow.plans(n("sb_bwd")), name=n("sb_bwd"))
    flow.done(n("sb_bwd"), got)
    dq_m, dk_m, dv_b = mla_attn_bwd(sv["q_m"], sv["k_m"], sv["kvb"], 6, sv["o_b"], sv["lse_b"], dmix, do_blk0=2, name=n("mla_bwd"))
    dqb, dqn = rope_norm_bwd(sv["qb"], 6, dq_m, W["qn_g"], tabs, name=n("q_rope_bwd"))
    dkn_x, dkn, dslab = rope_norm_bwd(sv["kvb"], 6, dk_m, W["kn_g"], tabs, (proj, P_SLAB // LANES), name=n("k_rope_bwd"))
    dkvb = _cat([dkn_x, dv_b]).astype(BF16)
    dckvn = three(matmul(two(dkvb), W["w_ukv"], tb=True, tm=1024, name=n("ukv_dx")))
    g["w_ukv"] = matmul(two(sv["ckvn"]), two(dkvb), ta=True, tn=1152, out_dtype=BF16, name=n("ukv_dw"))
    dcqn = three(matmul(two(dqb), W["w_uq"], tb=True, tm=1024, name=n("uq_dx")))
    g["w_uq"] = matmul(two(sv["cqn"]), two(dqb), ta=True, tn=768, out_dtype=BF16, name=n("uq_dw"))
    dcq, dcq_g = rms_bwd(proj, P_CQ // 256, 256, dcqn, W["cq_g"], name=n("cq_norm_bwd"))
    dckv, dckv_g = rms_bwd(proj, P_CKV // LANES, LANES, dckvn, W["ckv_g"], name=n("ckv_norm_bwd"))
    dq_c, dk_c, dv_c, dbias, dsink = swa_attn_bwd(sv["q_c"], sv["k_c"], proj, bias, W["sinks"], sv["o_c"], sv["lse_c"], dmix, do_blk0=5, name=n("swa_bwd"))
    dswq, dswq_g = pair_rms_bwd(proj, P_SWQ // LANES, 3, dq_c, W["swq_g"], name=n("swq_norm_bwd"))
    dswk, dswk_g = pair_rms_bwd(proj, P_SWK // LANES, 1, dk_c, W["swk_g"], name=n("swk_norm_bwd"))
    dproj = _cat([dq_a, dk_a, dv_a, dcq, dckv, dslab, dswq, dswk, dv_c]).astype(BF16)
    dh = three(matmul(two(dproj), W["w_in"], tb=True, tn=1024, name=n("in_dx")))
    g["w_in"] = matmul(two(sv["h"]), two(dproj), ta=True, tn=1920, tk=2048, out_dtype=BF16, name=n("in_dw"))
    dx, dn1, dsc1, dsh1 = rms_bwd(sv["x"], 0, D, dh, W["n1"], md["scale1"], dx1, name=n("norm1_bwd"))
    small = dict(n1=dn1, n2=dn2, cq_g=dcq_g, ckv_g=dckv_g, qn_g=dqn, kn_g=dkn, swq_g=dswq_g, swk_g=dswk_g, conv=dcw)
    dmods = _cat([dsh1, dsc1, dgate1, dsh2, dsc2, dgate2]).reshape(Bl, 6 * D)
    for k in ("w_ukv", "w_uq", "w_in"):
        flow.add((tag, k), g[k])
    return dx, g, small, dmods, dbias, dsink


BIG = ("w_in", "w_uq", "w_ukv", "w_out", "w_up", "w_down")
ROW_SHARDED = ("w_out", "w_down")
PREP = dict(w_in=_prep_w_in, w_uq=_prep_w_uq, w_ukv=_prep_w_ukv, w_out=_prep_w_out, w_up=_up_perm, w_down=lambda w: w)
UNPREP = dict(w_in=_unprep_w_in, w_uq=_unprep_w_uq, w_ukv=_unprep_w_ukv, w_out=_unprep_w_out, w_up=_up_perm, w_down=lambda w: w)
NCHIPS = 4


def _local_step(x, target, positions, mods, Wl, rel_flat, fwd_flow=_NoFlow(), bwd_flow=_NoFlow()):
    Bl, S, D = x.shape
    L = len(Wl)
    tabs = _rope_tables(positions)
    bucket = _bucket_table()
    bias = swa_bias(rel_flat, bucket, name="swa_bias")
    mds = []
    for l in range(L):
        parts = [mods[l, :, D * k:D * (k + 1)].reshape(Bl, 1, D) for k in range(6)]
        mds.append(dict(zip(("shift1", "scale1", "gate1", "shift2", "scale2", "gate2"), parts)))
    saved = []
    h = x
    for l in range(L):
        h, sv = _layer_fwd(h, mds[l], Wl[l], tabs, bias, f"l{l}", fwd_flow)
        saved.append(sv)
    dy, loss = loss_grad(h, target, name="loss")
    grads, smalls, dmods, dbiases, dsinks = [None] * L, [None] * L, [None] * L, [None] * L, [None] * L
    for l in reversed(range(L)):
        dy, grads[l], smalls[l], dmods[l], dbiases[l], dsinks[l] = _layer_bwd(dy, saved[l], mds[l], Wl[l], tabs, bias, f"l{l}", bwd_flow)
    drel = swa_bias_bwd(_cat(dbiases, axis=0), bucket, name="swa_bias_bwd")
    return loss, dy, grads, smalls, dmods, dsinks, drel


ATT = ("w_in", "w_uq", "w_ukv", "w_out")
FFN = ("w_up", "w_down")
GATHER_STAGES = {
    "sb_fwd_l0": ([("l0", k) for k in ("w_out",) + FFN], []),
    "mla_fwd_l0": ([("l1", k) for k in ATT + ("w_up",)], [("l0", k) for k in ("w_out",) + FFN]),
    "swa_fwd_l0": ([("l1", "w_down")], [("l1", k) for k in ATT + ("w_up",)]),
    "sb_fwd_l1": ([], [("l1", "w_down")]),
}
SCATTER_STAGES = {
    "sb_bwd_l1": [("l1", k) for k in FFN],
    "sb_bwd_l0": [("l1", k) for k in ATT] + [("l0", k) for k in FFN + ("w_out",)],
}


class _GatherFlow:
    def __init__(self, shards, chip):
        self.shards, self.chip, self.ici, self.d2d, self.pending = shards, chip, {}, {}, {}

    def early(self, keys):
        ici, = run_plans([plan_gather_ici([self.shards[k] for k in keys])], name="gather_early_ici")
        d2d, = run_plans([plan_gather_d2d(ici)], name="gather_early_d2d")
        self.d2d.update(zip(keys, d2d))

    def plans(self, tag):
        ici_keys, d2d_keys = GATHER_STAGES.get(tag, ([], []))
        plans = []
        if d2d_keys:
            plans.append(plan_gather_d2d([self.ici[k] for k in d2d_keys]))
        if ici_keys:
            plans.append(plan_gather_ici([self.shards[k] for k in ici_keys]))
        self.pending[tag] = (ici_keys, d2d_keys)
        return plans

    def done(self, tag, outs):
        ici_keys, d2d_keys = self.pending.pop(tag, ([], []))
        outs = list(outs)
        if d2d_keys:
            self.d2d.update(zip(d2d_keys, outs.pop(0)))
        if ici_keys:
            self.ici.update(zip(ici_keys, outs.pop(0)))

    def weight(self, key):
        k = key[1]
        own = self.shards[key]
        r, cc = own.shape
        w4 = lax.dynamic_update_slice(self.d2d[key], own[None], (self.chip, 0, 0))
        fw = w4.reshape(NCHIPS * r, cc) if k in ROW_SHARDED else jnp.transpose(w4, (1, 0, 2)).reshape(r, NCHIPS * cc)
        return PREP[k](fw)


class _LayerWeights(dict):
    def __init__(self, small, flow, tag):
        super().__init__(small)
        self.flow, self.tag = flow, tag

    def __missing__(self, k):
        self[k] = self.flow.weight((self.tag, k))
        return self[k]


class _ScatterFlow:
    def __init__(self, shapes, sel, c_arr):
        self.shapes, self.sel, self.c_arr = shapes, sel, c_arr
        self.g, self.pairs, self.landed, self.pending = {}, {}, {}, {}

    def add(self, key, g):
        self.g[key] = g

    def _pairs(self, keys, label):
        g4s = []
        for key in keys:
            k = key[1]
            r, cc = self.shapes[k]
            gk = UNPREP[k](self.g[key])
            g4 = gk.reshape(NCHIPS, r, cc) if k in ROW_SHARDED else jnp.transpose(gk.reshape(r, NCHIPS, cc), (1, 0, 2))
            g4s.append(g4.astype(BF16))
        theirs, = run_plans([plan_swap_halves(g4s)], name=f"rs_swap_{label}")
        pairs = [pair_add_half(g4, th, self.c_arr, name=f"rs_pair_add_{key[1]}_{key[0]}") for key, g4, th in zip(keys, g4s, theirs)]
        self.pairs.update(zip(keys, pairs))
        return pairs

    def plans(self, tag):
        keys = SCATTER_STAGES.get(tag, [])
        self.pending[tag] = keys
        return [plan_scatter_ici(self._pairs(keys, tag))] if keys else []

    def done(self, tag, outs):
        keys = self.pending.pop(tag, [])
        if keys:
            self.landed.update(zip(keys, outs[0]))

    def finish(self):
        rest = [key for key in self.g if key not in self.pairs]
        if rest:
            landed, = run_plans([plan_scatter_ici(self._pairs(rest, "rest"))], name="rs_scatter_rest")
            self.landed.update(zip(rest, landed))
        keys = list(self.pairs)
        fulls = [chip_sum_into(self.landed[key], self.pairs[key], self.sel, name=f"rs_chip_sum_{key[1]}_{key[0]}") for key in keys]
        joined, = run_plans([plan_join_halves(fulls)], name="rs_join_halves")
        return dict(zip(keys, joined))


WEIGHTS = ("rel_table", "norm1_g", "norm2_g", "w_ada", "b_ada", "w_in", "mla_cq_g", "w_uq", "mla_ckv_g", "w_ukv", "mla_qn_g", "mla_kn_g",
           "sw_qn_g", "sw_kn_g", "sw_sinks", "w_out", "w_up", "conv_w", "conv_b", "w_down")
SMALL = tuple(n for n in WEIGHTS if n not in BIG + ("w_ada",))


def kernel(x, c, positions, rel_table, norm1_g, norm2_g, w_ada, b_ada, w_in, mla_cq_g, w_uq, mla_ckv_g, w_ukv, mla_qn_g, mla_kn_g, sw_qn_g, sw_kn_g, sw_sinks, w_out, w_up, conv_w, conv_b, w_down, loss_target, m_rel_table, m_norm1_g, m_norm2_g, m_w_ada, m_b_ada, m_w_in, m_mla_cq_g, m_w_uq, m_mla_ckv_g, m_w_ukv, m_mla_qn_g, m_mla_kn_g, m_sw_qn_g, m_sw_kn_g, m_sw_sinks, m_w_out, m_w_up, m_conv_w, m_conv_b, m_w_down, v_rel_table, v_norm1_g, v_norm2_g, v_w_ada, v_b_ada, v_w_in, v_mla_cq_g, v_w_uq, v_mla_ckv_g, v_w_ukv, v_mla_qn_g, v_mla_kn_g, v_sw_qn_g, v_sw_kn_g, v_sw_sinks, v_w_out, v_w_up, v_conv_w, v_conv_b, v_w_down):
    w = dict(rel_table=rel_table, norm1_g=norm1_g, norm2_g=norm2_g, w_ada=w_ada, b_ada=b_ada, w_in=w_in, mla_cq_g=mla_cq_g, w_uq=w_uq,
             mla_ckv_g=mla_ckv_g, w_ukv=w_ukv, mla_qn_g=mla_qn_g, mla_kn_g=mla_kn_g, sw_qn_g=sw_qn_g, sw_kn_g=sw_kn_g, sw_sinks=sw_sinks,
             w_out=w_out, w_up=w_up, conv_w=conv_w, conv_b=conv_b, w_down=w_down)
    m = dict(rel_table=m_rel_table, norm1_g=m_norm1_g, norm2_g=m_norm2_g, w_ada=m_w_ada, b_ada=m_b_ada, w_in=m_w_in, mla_cq_g=m_mla_cq_g,
             w_uq=m_w_uq, mla_ckv_g=m_mla_ckv_g, w_ukv=m_w_ukv, mla_qn_g=m_mla_qn_g, mla_kn_g=m_mla_kn_g, sw_qn_g=m_sw_qn_g,
             sw_kn_g=m_sw_kn_g, sw_sinks=m_sw_sinks, w_out=m_w_out, w_up=m_w_up, conv_w=m_conv_w, conv_b=m_conv_b, w_down=m_w_down)
    v = dict(rel_table=v_rel_table, norm1_g=v_norm1_g, norm2_g=v_norm2_g, w_ada=v_w_ada, b_ada=v_b_ada, w_in=v_w_in, mla_cq_g=v_mla_cq_g,
             w_uq=v_w_uq, mla_ckv_g=v_mla_ckv_g, w_ukv=v_w_ukv, mla_qn_g=v_mla_qn_g, mla_kn_g=v_mla_kn_g, sw_qn_g=v_sw_qn_g,
             sw_kn_g=v_sw_kn_g, sw_sinks=v_sw_sinks, w_out=v_w_out, w_up=v_w_up, conv_w=v_conv_w, conv_b=v_conv_b, w_down=v_w_down)
    Bl, S, D = x.shape
    L = norm1_g.shape[0]
    xi, yi, ci = _me()
    chip = 2 * xi + yi
    dev = 4 * xi + 2 * yi + ci
    ndev = 2 * NCHIPS

    shapes = {k: w[k].shape[1:] for k in BIG}
    shards = {(f"l{l}", k): cast_layer(w[k], l, name=f"cast_{k}_l{l}") for l in range(L) for k in BIG}
    gflow = _GatherFlow(shards, chip)
    gflow.early([("l0", k) for k in ("w_in", "w_uq", "w_ukv")])

    cw_cols = conv_w.shape[2]
    c_got, cw_got = allgather8([c, conv_w.reshape(L * 3, cw_cols)], name="gather_cond")
    c_all = c_got.reshape(ndev * Bl, D)
    conv_full = jnp.transpose(cw_got[0::2].reshape(NCHIPS, L, 3, cw_cols), (1, 2, 0, 3)).reshape(L, 3, NCHIPS * cw_cols)
    E = w_ada.shape[2]
    b_cols = lax.dynamic_slice(b_ada, (0, chip * E), (L, E)).reshape(L, 1, E)
    mods_cols = mods_matmul(c_all, w_ada, b_cols, name="mods")
    mods_all, = allgather8([mods_cols.reshape(L * ndev * Bl, E)], name="gather_mods")
    mods_all = jnp.transpose(mods_all[0::2].reshape(NCHIPS, L, ndev * Bl, E), (1, 2, 0, 3)).reshape(L, ndev * Bl, NCHIPS * E)
    mods = lax.dynamic_slice(mods_all, (0, dev * Bl, 0), (L, Bl, NCHIPS * E))

    Wl = []
    for l in range(L):
        Wd = _small_params({k: w[k][l] for k in SMALL if k not in ("rel_table", "b_ada", "conv_w")})
        Wd["conv_w"] = _up_perm(conv_full[l])
        Wl.append(_LayerWeights(Wd, gflow, f"l{l}"))

    sflow = _ScatterFlow(shapes, jnp.stack([chip, ci]).astype(jnp.int32), ci.reshape(1).astype(jnp.int32))
    loss, dx, _, smalls, dmods, dsinks, drel = _local_step(x, loss_target, positions, mods, Wl, rel_table.reshape(-1), gflow, sflow)
    reduced = sflow.finish()
    grad = {k: jnp.stack([reduced[(f"l{l}", k)] for l in range(L)]) for k in BIG}

    vec_names = ("n1", "n2", "cq_g", "ckv_g", "qn_g", "kn_g", "swq_g", "swk_g")
    vecs = _cat([_cat([smalls[l][k] for k in vec_names], axis=1) for l in range(L)], axis=0)
    convs = _cat([smalls[l]["conv"][0:4] for l in range(L)], axis=0)
    dm = jnp.stack(dmods, axis=1).reshape(Bl * L, 6 * D)
    dsk = jnp.stack(dsinks, axis=1).reshape(Bl * L * 6, LANES)
    got = allgather8([vecs, convs, drel, loss, dm, dsk], name="gather_small_grads")
    seq = lambda a, rows: a.reshape(ndev * Bl, rows, a.shape[-1])
    vec_s, conv_s, rel_s, loss_s, dm_s, dsk_s = sum_small(list(got[:4]) + [seq(got[4], L), seq(got[5], L * 6)], name="sum_small_grads")
    dm_all = jnp.transpose(seq(got[4], L), (1, 0, 2))
    grad["w_ada"] = ada_grad(c_all, lax.dynamic_slice(dm_all, (0, 0, chip * E), (L, ndev * Bl, E)), name="ada_grad")
    grad["b_ada"] = dm_s
    grad["sw_sinks"] = jnp.transpose(dsk_s.reshape(L, 3, 2, LANES)[:, :, :, 0], (0, 2, 1)).reshape(L, 6)
    grad["rel_table"] = rel_s[:6, :REL_BUCKETS].T
    off = 0
    for k, name_, keep in zip(vec_names, ("norm1_g", "norm2_g", "mla_cq_g", "mla_ckv_g", "mla_qn_g", "mla_kn_g", "sw_qn_g", "sw_kn_g"),
                              (D, D, 256, LANES, MLA_QK, MLA_QK, HEAD, HEAD)):
        grad[name_] = vec_s[:, off:off + keep]
        off += smalls[0][k].shape[1]
    conv = _up_perm(conv_s.reshape(L, 4, 2 * D_FF))
    grad["conv_w"] = lax.dynamic_slice(conv[:, 0:3], (0, 0, chip * cw_cols), (L, 3, cw_cols))
    grad["conv_b"] = conv[:, 3]
    loss_out = loss_s[0, 0]

    delta, new_m, new_v = {}, {}, {}
    for k in BIG + ("w_ada",):
        delta[k], new_m[k], new_v[k] = adamw(w[k], grad[k], m[k], v[k], name=f"adamw_{k}")
    outs = adamw_small(*[[src[k] for k in SMALL] for src in (w, grad, m, v)], name="adamw_small")
    for dst, o in zip((delta, new_m, new_v), outs):
        dst.update(dict(zip(SMALL, o)))
    return (loss_out, dx, *[grad[k] for k in WEIGHTS], *[delta[k] for k in WEIGHTS], *[new_m[k] for k in WEIGHTS], *[new_v[k] for k in WEIGHTS])
```

```python
import math

import jax
import jax.numpy as jnp
from jax import lax
from jax.experimental import pallas as pl
from jax.experimental.pallas import tpu as pltpu

F32 = jnp.float32
BF16 = jnp.bfloat16
MESH = pl.DeviceIdType.MESH

EPS = 1e-6
NEG = -1e30
HEAD = 64
LANES = 128
MLA_QK = 96
ROPE_THETA = 10000.0
REL_BUCKETS = 32
REL_MAX_DIST = 128
WINDOW = 128
D_FF = 2816
ADAM_LR, ADAM_B1, ADAM_B2, ADAM_EPS, ADAM_WD, ADAM_STEP = 0.001, 0.9, 0.999, 1e-08, 0.01, 10

VMEM_LIMIT = 56 * 1024 * 1024
STRIP = 32

P_SBQ, P_SBK, P_SBV, P_CQ, P_CKV, P_SLAB, P_SWQ, P_SWK, P_SWV, P_END = 0, 256, 512, 768, 1024, 1152, 1280, 1664, 1792, 1920


def _cp(*sem):
    return pltpu.CompilerParams(dimension_semantics=sem, vmem_limit_bytes=VMEM_LIMIT)


def _dot(a, b):
    return jnp.dot(a, b, preferred_element_type=F32)


def _dot_nt(a, b):
    return lax.dot_general(a, b, (((1,), (1,)), ((), ())), preferred_element_type=F32)


def _dot_tn(a, b):
    return lax.dot_general(a, b, (((0,), (0,)), ((), ())), preferred_element_type=F32)


def _lane_masks():
    lane = lax.broadcasted_iota(jnp.int32, (1, LANES), 1)
    return (lane < HEAD, lane >= HEAD)


def _tile(n, cap, align=128):
    if n <= cap:
        return n
    t = cap - cap % align
    while t >= align:
        if n % t == 0:
            return t
        t -= align
    return n


def matmul(a, b, *, ta=False, tb=False, out_dtype=F32, tm=512, tn=512, tk=8192, name):
    M, K = (a.shape[1], a.shape[0]) if ta else a.shape
    N = b.shape[0] if tb else b.shape[1]
    tm, tn, tk = _tile(M, tm), _tile(N, tn), _tile(K, tk)
    nk = K // tk

    def body(a_ref, b_ref, o_ref, *scratch):
        av = a_ref[...].astype(BF16)
        bv = b_ref[...].astype(BF16)
        if ta:
            part = _dot_tn(av, bv)
        elif tb:
            part = _dot_nt(av, bv)
        else:
            part = _dot(av, bv)
        if nk == 1:
            o_ref[...] = part.astype(out_dtype)
        else:
            acc_ref, = scratch
            k = pl.program_id(2)

            @pl.when(k == 0)
            def _():
                acc_ref[...] = part

            @pl.when(k > 0)
            def _():
                acc_ref[...] += part

            @pl.when(k == nk - 1)
            def _():
                o_ref[...] = acc_ref[...].astype(out_dtype)

    n_outer = nk == 1 and tn * b.dtype.itemsize > tm * a.dtype.itemsize
    ij = (lambda p, q: (q, p)) if n_outer else (lambda p, q: (p, q))
    a_map = (lambda p, q, k: (k, ij(p, q)[0])) if ta else (lambda p, q, k: (ij(p, q)[0], k))
    b_map = (lambda p, q, k: (ij(p, q)[1], k)) if tb else (lambda p, q, k: (k, ij(p, q)[1]))
    grid = (N // tn, M // tm, nk) if n_outer else (M // tm, N // tn, nk)
    return pl.pallas_call(
        body, name=name, grid=grid,
        in_specs=[pl.BlockSpec((tk, tm) if ta else (tm, tk), a_map), pl.BlockSpec((tn, tk) if tb else (tk, tn), b_map)],
        out_specs=pl.BlockSpec((tm, tn), lambda p, q, k: ij(p, q)),
        out_shape=jax.ShapeDtypeStruct((M, N), out_dtype),
        scratch_shapes=[] if nk == 1 else [pltpu.VMEM((tm, tn), F32)],
        compiler_params=_cp("parallel", "parallel", "arbitrary"),
    )(a, b)


def matmul_res(a, b, res, gate, seq, *, tm=512, tn=1024, name):
    M, K = a.shape
    N = b.shape[1]
    tm, tn = _tile(min(M, seq), tm), _tile(N, tn)
    per_seq = seq // tm

    def body(a_ref, b_ref, r_ref, g_ref, y_ref, x_ref):
        y = _dot(a_ref[...].astype(BF16), b_ref[...].astype(BF16))
        y_ref[...] = y
        x_ref[...] = r_ref[...] + g_ref[...] * y

    out = jax.ShapeDtypeStruct((M, N), F32)
    return pl.pallas_call(
        body, name=name, grid=(M // tm, N // tn),
        in_specs=[pl.BlockSpec((tm, K), lambda i, j: (i, 0)), pl.BlockSpec((K, tn), lambda i, j: (0, j)),
                  pl.BlockSpec((tm, tn), lambda i, j: (i, j)), pl.BlockSpec((None, 1, tn), lambda i, j: (lax.div(i, jnp.int32(per_seq)), 0, j))],
        out_specs=[pl.BlockSpec((tm, tn), lambda i, j: (i, j))] * 2,
        out_shape=[out, out], compiler_params=_cp("parallel", "parallel"),
    )(a, b, res, gate)


def rms_fwd(x3, blk, W, g, sc=None, sh=None, *, tm=512, name):
    Bl, S, _ = x3.shape
    tm = min(tm, S)
    mod = sc is not None

    def body(x_ref, g_ref, *rest):
        o_ref = rest[-1]
        x = x_ref[...]
        r = lax.rsqrt(jnp.mean(x * x, axis=-1, keepdims=True) + EPS)
        y = x * r * g_ref[...]
        if mod:
            y = y * (1.0 + rest[0][...]) + rest[1][...]
        o_ref[...] = y.astype(BF16)

    vec = pl.BlockSpec((None, 1, W), lambda b, s: (b, 0, 0))
    return pl.pallas_call(
        body, name=name, grid=(Bl, S // tm),
        in_specs=[pl.BlockSpec((None, tm, W), lambda b, s: (b, s, blk)), pl.BlockSpec((1, W), lambda b, s: (0, 0))] + ([vec, vec] if mod else []),
        out_specs=pl.BlockSpec((None, tm, W), lambda b, s: (b, s, 0)),
        out_shape=jax.ShapeDtypeStruct((Bl, S, W), BF16),
        compiler_params=_cp("parallel", "parallel"),
    )(x3, g, *([sc, sh] if mod else []))


def rms_bwd(x3, blk, W, dy3, g, sc=None, dres3=None, *, tm=128, name):
    Bl, S, _ = x3.shape
    tm = min(tm, S)
    mod = sc is not None
    res = dres3 is not None

    def body(*refs):
        x_ref, dy_ref, g_ref = refs[:3]
        k = 3
        sc_ref = dr_ref = None
        if mod:
            sc_ref = refs[k]
            k += 1
        if res:
            dr_ref = refs[k]
            k += 1
        dx_ref, dg_ref = refs[k], refs[k + 1]
        b, s = pl.program_id(0), pl.program_id(1)
        x = x_ref[...]
        dy = dy_ref[...].astype(F32)
        g = g_ref[...]
        r = lax.rsqrt(jnp.mean(x * x, axis=-1, keepdims=True) + EPS)
        n = x * r
        if mod:
            dsc_ref, dsh_ref = refs[k + 2], refs[k + 3]
            one_sc = 1.0 + sc_ref[...]

            @pl.when(s == 0)
            def _():
                dsc_ref[...] = jnp.zeros_like(dsc_ref)
                dsh_ref[...] = jnp.zeros_like(dsh_ref)

            dsh_ref[...] += jnp.sum(dy, axis=0, keepdims=True)
            dsc_ref[...] += jnp.sum(dy * n * g, axis=0, keepdims=True)
            dyn = dy * one_sc
        else:
            dyn = dy

        @pl.when((b == 0) & (s == 0))
        def _():
            dg_ref[...] = jnp.zeros_like(dg_ref)

        dg_ref[...] += jnp.sum(dyn * n, axis=0, keepdims=True)
        dn = dyn * g
        dx = r * (dn - n * jnp.mean(dn * n, axis=-1, keepdims=True))
        if res:
            dx = dx + dr_ref[...]
        dx_ref[...] = dx

    blkspec = pl.BlockSpec((None, tm, W), lambda b, s: (b, s, 0))
    vec = pl.BlockSpec((None, 1, W), lambda b, s: (b, 0, 0))
    row = pl.BlockSpec((1, W), lambda b, s: (0, 0))
    in_specs = [pl.BlockSpec((None, tm, W), lambda b, s: (b, s, blk)), blkspec, row] + ([vec] if mod else []) + ([blkspec] if res else [])
    out_specs = [blkspec, row] + ([vec, vec] if mod else [])
    out_shape = [jax.ShapeDtypeStruct((Bl, S, W), F32), jax.ShapeDtypeStruct((1, W), F32)]
    if mod:
        out_shape += [jax.ShapeDtypeStruct((Bl, 1, W), F32)] * 2
    args = [x3, dy3, g] + ([sc] if mod else []) + ([dres3] if res else [])
    return pl.pallas_call(
        body, name=name, grid=(Bl, S // tm), in_specs=in_specs, out_specs=out_specs, out_shape=out_shape,
        compiler_params=_cp("arbitrary", "arbitrary"),
    )(*args)


def pair_rms_fwd(x3, blk0, npairs, g2, *, tm=1024, name):
    Bl, S, _ = x3.shape
    tm = min(tm, S)

    def body(x_ref, g_ref, o_ref):
        lo, hi = _lane_masks()
        x = x_ref[...]
        xx = x * x
        s0 = jnp.sum(jnp.where(lo, xx, 0.0), axis=-1, keepdims=True)
        s1 = jnp.sum(jnp.where(hi, xx, 0.0), axis=-1, keepdims=True)
        r = jnp.where(lo, lax.rsqrt(s0 / HEAD + EPS), lax.rsqrt(s1 / HEAD + EPS))
        o_ref[...] = (x * r * g_ref[...]).astype(BF16)

    return pl.pallas_call(
        body, name=name, grid=(Bl, S // tm, npairs),
        in_specs=[pl.BlockSpec((None, tm, LANES), lambda b, s, p: (b, s, blk0 + p)), pl.BlockSpec((1, LANES), lambda b, s, p: (0, 0))],
        out_specs=pl.BlockSpec((None, tm, LANES), lambda b, s, p: (b, s, p)),
        out_shape=jax.ShapeDtypeStruct((Bl, S, LANES * npairs), BF16),
        compiler_params=_cp("parallel", "parallel", "parallel"),
    )(x3, g2)


def pair_rms_bwd(x3, blk0, npairs, dy3, g2, *, tm=1024, name):
    Bl, S, _ = x3.shape
    tm = min(tm, S)

    def body(x_ref, dy_ref, g_ref, dx_ref, dg_ref):
        lo, hi = _lane_masks()
        first = (pl.program_id(0) == 0) & (pl.program_id(1) == 0) & (pl.program_id(2) == 0)
        x = x_ref[...]
        dy = dy_ref[...]
        xx = x * x
        s0 = jnp.sum(jnp.where(lo, xx, 0.0), axis=-1, keepdims=True)
        s1 = jnp.sum(jnp.where(hi, xx, 0.0), axis=-1, keepdims=True)
        r = jnp.where(lo, lax.rsqrt(s0 / HEAD + EPS), lax.rsqrt(s1 / HEAD + EPS))
        n = x * r

        @pl.when(first)
        def _():
            dg_ref[...] = jnp.zeros_like(dg_ref)

        part = jnp.sum(dy * n, axis=0, keepdims=True)
        dg_ref[...] += part + pltpu.roll(part, HEAD, 1)
        dn = dy * g_ref[...]
        t = dn * n
        m0 = jnp.sum(jnp.where(lo, t, 0.0), axis=-1, keepdims=True)
        m1 = jnp.sum(jnp.where(hi, t, 0.0), axis=-1, keepdims=True)
        dx_ref[...] = r * (dn - n * (jnp.where(lo, m0, m1) / HEAD))

    return pl.pallas_call(
        body, name=name, grid=(Bl, S // tm, npairs),
        in_specs=[pl.BlockSpec((None, tm, LANES), lambda b, s, p: (b, s, blk0 + p)), pl.BlockSpec((None, tm, LANES), lambda b, s, p: (b, s, p)),
                  pl.BlockSpec((1, LANES), lambda b, s, p: (0, 0))],
        out_specs=[pl.BlockSpec((None, tm, LANES), lambda b, s, p: (b, s, p)), pl.BlockSpec((1, LANES), lambda b, s, p: (0, 0))],
        out_shape=[jax.ShapeDtypeStruct((Bl, S, LANES * npairs), F32), jax.ShapeDtypeStruct((1, LANES), F32)],
        compiler_params=_cp("arbitrary", "arbitrary", "arbitrary"),
    )(x3, dy3, g2)


def _rot(y, cos_t, sin_a, sin_b):
    return y * cos_t + pltpu.roll(y, LANES - 16, 1) * sin_a + pltpu.roll(y, 16, 1) * sin_b


def _rot_t(d, cos_t, sin_a, sin_b):
    return d * cos_t + pltpu.roll(d * sin_a, 16, 1) + pltpu.roll(d * sin_b, LANES - 16, 1)


def rope_norm_fwd(x3, nheads, g, tabs, slab=None, *, tm=512, name):
    Bl, S, _ = x3.shape
    tm = min(tm, S)
    has_slab = slab is not None

    def body(*refs):
        x_ref, g_ref, c_ref, sa_ref, sb_ref = refs[:5]
        o_ref = refs[-1]
        x = x_ref[...]
        if has_slab:
            x = x + refs[5][...]
        r = lax.rsqrt(jnp.sum(x * x, axis=-1, keepdims=True) / MLA_QK + EPS)
        o_ref[...] = _rot(x * r * g_ref[...], c_ref[...], sa_ref[...], sb_ref[...]).astype(BF16)

    head = pl.BlockSpec((None, tm, LANES), lambda b, s, h: (b, s, h))
    tab = pl.BlockSpec((None, tm, LANES), lambda b, s, h: (b, s, 0))
    in_specs = [head, pl.BlockSpec((1, LANES), lambda b, s, h: (0, 0)), tab, tab, tab]
    args = [x3, g, *tabs]
    if has_slab:
        sblk = slab[1]
        in_specs.append(pl.BlockSpec((None, tm, LANES), lambda b, s, h: (b, s, sblk)))
        args.append(slab[0])
    return pl.pallas_call(
        body, name=name, grid=(Bl, S // tm, nheads), in_specs=in_specs, out_specs=head,
        out_shape=jax.ShapeDtypeStruct((Bl, S, LANES * nheads), BF16),
        compiler_params=_cp("parallel", "parallel", "parallel"),
    )(*args)


def rope_norm_bwd(x3, nheads, dy3, g, tabs, slab=None, *, tm=512, name):
    Bl, S, _ = x3.shape
    tm = min(tm, S)
    has_slab = slab is not None

    def body(*refs):
        x_ref, dy_ref, g_ref, c_ref, sa_ref, sb_ref = refs[:6]
        k = 7 if has_slab else 6
        dx_ref, dg_ref = refs[k], refs[k + 1]
        h = pl.program_id(2)
        first = (pl.program_id(0) == 0) & (pl.program_id(1) == 0) & (h == 0)
        x = x_ref[...]
        if has_slab:
            x = x + refs[6][...]
        g = g_ref[...]
        r = lax.rsqrt(jnp.sum(x * x, axis=-1, keepdims=True) / MLA_QK + EPS)
        n = x * r
        d = _rot_t(dy_ref[...], c_ref[...], sa_ref[...], sb_ref[...])

        @pl.when(first)
        def _():
            dg_ref[...] = jnp.zeros_like(dg_ref)

        dg_ref[...] += jnp.sum(d * n, axis=0, keepdims=True)
        dn = d * g
        dx = r * (dn - n * (jnp.sum(dn * n, axis=-1, keepdims=True) / MLA_QK))
        dx_ref[...] = dx.astype(BF16)
        if has_slab:
            ds_ref = refs[k + 2]

            @pl.when(h == 0)
            def _():
                ds_ref[...] = dx

            @pl.when(h > 0)
            def _():
                ds_ref[...] += dx

    head = pl.BlockSpec((None, tm, LANES), lambda b, s, h: (b, s, h))
    tab = pl.BlockSpec((None, tm, LANES), lambda b, s, h: (b, s, 0))
    row = pl.BlockSpec((1, LANES), lambda b, s, h: (0, 0))
    in_specs = [head, head, row, tab, tab, tab]
    args = [x3, dy3, g, *tabs]
    out_specs = [head, row]
    out_shape = [jax.ShapeDtypeStruct((Bl, S, LANES * nheads), BF16), jax.ShapeDtypeStruct((1, LANES), F32)]
    if has_slab:
        sblk = slab[1]
        in_specs.append(pl.BlockSpec((None, tm, LANES), lambda b, s, h: (b, s, sblk)))
        args.append(slab[0])
        out_specs.append(tab)
        out_shape.append(jax.ShapeDtypeStruct((Bl, S, LANES), F32))
    return pl.pallas_call(
        body, name=name, grid=(Bl, S // tm, nheads), in_specs=in_specs, out_specs=out_specs, out_shape=out_shape,
        compiler_params=_cp("arbitrary", "arbitrary", "arbitrary"),
    )(*args)


def _softplus(z):
    return jnp.maximum(z, 0.0) + jnp.log(1.0 + jnp.exp(-jnp.abs(z)))


def _split_dots(xs, u):
    hi = [x.astype(BF16) for x in xs]
    lo = [(x - h.astype(F32)).astype(BF16) for x, h in zip(xs, hi)]
    top = [_dot(h, u) for h in hi]
    return [t + _dot(l, u) for t, l in zip(top, lo)]


SB_BLOCK = 256
SB_QBLOCK = 512


def sb_attn_fwd(proj3, *, plans=None, name):
    Bl, S, _ = proj3.shape
    tk = min(SB_BLOCK, S)
    tq = min(SB_QBLOCK, S)
    per_q = tq // tk
    scale = HEAD ** -0.5
    qb, kb0, vb0 = P_SBQ // LANES, P_SBK // LANES, P_SBV // LANES

    def body(q_ref, k_ref, v_ref, o_ref, rt_ref):
        i = pl.program_id(2)
        masks = _lane_masks()
        lane = lax.broadcasted_iota(jnp.int32, (1, LANES), 1)
        q = q_ref[...]
        qh = [jnp.where(m, q, 0.0).astype(BF16) for m in masks]
        rr = lax.broadcasted_iota(jnp.int32, (tq, tk), 0)
        cc = lax.broadcasted_iota(jnp.int32, (tq, tk), 1)
        u = (lax.broadcasted_iota(jnp.int32, (tk, tk), 0) > lax.broadcasted_iota(jnp.int32, (tk, tk), 1)).astype(BF16)

        rt_ref[...] = jnp.zeros_like(rt_ref)

        def step(j, carry, masked):
            r0, r1, acc = carry
            off = pl.multiple_of(j * tk, tk)
            kb = k_ref[pl.ds(off, tk), :].astype(BF16)
            vb = v_ref[pl.ds(off, tk), :]
            strict = (cc + j * tk) < (rr + i * tq) if masked else None
            only = (lambda t: jnp.where(strict, t, 0.0)) if masked else (lambda t: t)
            rt_ref[...] = jnp.where(lane == j, r0, jnp.where(lane == j + HEAD, r1, rt_ref[...]))
            rs, two = [r0, r1], range(2)
            z = [_dot_nt(qh[h], kb) * scale for h in two]
            sp = [_softplus(z[h]) for h in two]
            keep = [only(-sp[h]) for h in two]
            suf = _split_dots(keep, u)
            w = [only(jnp.exp((z[h] - sp[h]) + suf[h] + rs[h])) for h in two]
            pv = [_dot(w[h].astype(BF16), jnp.where(masks[h], vb, 0.0).astype(BF16)) for h in two]
            return rs[0] + jnp.sum(keep[0], axis=1, keepdims=True), rs[1] + jnp.sum(keep[1], axis=1, keepdims=True), acc + (pv[0] + pv[1])

        zero = jnp.zeros((tq, 1), F32)
        carry = (zero, zero, jnp.zeros((tq, LANES), F32))
        for t in range(per_q):
            carry = step((i + 1) * per_q - 1 - t, carry, True)
        _, _, acc = lax.fori_loop(0, i * per_q, lambda t, c: step(i * per_q - 1 - t, c, False), carry)
        o_ref[...] = acc

    seq = lambda blk0: pl.BlockSpec((None, S, LANES), lambda b, p, i: (b, 0, blk0 + p))
    out = pl.BlockSpec((None, tq, LANES), lambda b, p, i: (b, i, p))
    shp = jax.ShapeDtypeStruct((Bl, S, 2 * LANES), F32)
    return call_with_plans(
        body, plans, name=name, grid=(Bl, 2, S // tq),
        in_specs=[pl.BlockSpec((None, tq, LANES), lambda b, p, i: (b, i, qb + p)), seq(kb0), seq(vb0)],
        out_specs=[out, out], out_shape=[shp, shp], scratch_shapes=[], args=[proj3, proj3, proj3],
        sem=("arbitrary",) * 3 if plans else ("parallel", "parallel", "arbitrary"))


def sb_attn_bwd(proj3, rt3, do3, *, do_blk0=0, plans=None, name):
    Bl, S, _ = proj3.shape
    tk = min(SB_BLOCK, S)
    tq = min(SB_QBLOCK, S)
    per_q = tq // tk
    scale = HEAD ** -0.5
    qb, kb0, vb0 = P_SBQ // LANES, P_SBK // LANES, P_SBV // LANES

    def body(q_ref, k_ref, v_ref, rt_ref, do_ref, dq_ref, dk_ref, dv_ref):
        i = pl.program_id(2)

        @pl.when(i == 0)
        def _():
            dk_ref[...] = jnp.zeros_like(dk_ref)
            dv_ref[...] = jnp.zeros_like(dv_ref)

        masks = _lane_masks()
        lane = lax.broadcasted_iota(jnp.int32, (1, LANES), 1)
        q = q_ref[...]
        qh = [jnp.where(m, q, 0.0).astype(BF16) for m in masks]
        do_b = do_ref[...].astype(BF16)
        doh = [jnp.where(m, do_b, jnp.zeros_like(do_b)) for m in masks]
        rt = rt_ref[...]
        rr = lax.broadcasted_iota(jnp.int32, (tq, tk), 0)
        cc = lax.broadcasted_iota(jnp.int32, (tq, tk), 1)
        ur = lax.broadcasted_iota(jnp.int32, (tk, tk), 0)
        uc = lax.broadcasted_iota(jnp.int32, (tk, tk), 1)
        u_suffix = (ur > uc).astype(BF16)
        u_prefix = (ur < uc).astype(BF16)

        def step(j, carry, masked):
            p0, p1, dq = carry
            off = pl.multiple_of(j * tk, tk)
            kf = k_ref[pl.ds(off, tk), :]
            kb = kf.astype(BF16)
            vb = v_ref[pl.ds(off, tk), :]
            strict = (cc + j * tk) < (rr + i * tq) if masked else None
            only = (lambda t: jnp.where(strict, t, 0.0)) if masked else (lambda t: t)
            ps, two = [p0, p1], range(2)
            r_j = [jnp.sum(jnp.where(lane == j + h * HEAD, rt, 0.0), axis=1, keepdims=True) for h in two]
            z = [_dot_nt(qh[h], kb) * scale for h in two]
            dw = [_dot_nt(doh[h], jnp.where(masks[h], vb, 0.0).astype(BF16)) for h in two]
            sp = [_softplus(z[h]) for h in two]
            keep = [only(-sp[h]) for h in two]
            suf = _split_dots(keep, u_suffix)
            w = [only(jnp.exp((z[h] - sp[h]) + suf[h] + r_j[h])) for h in two]
            g = [dw[h] * w[h] for h in two]
            pre = _split_dots(g, u_prefix)
            dzb = [(only(g[h] * jnp.exp(-sp[h]) - jnp.exp(z[h] - sp[h]) * (pre[h] + ps[h])) * scale).astype(BF16) for h in two]
            dqs = [_dot(dzb[h], jnp.where(masks[h], kf, 0.0).astype(BF16)) for h in two]
            dks = [_dot_tn(dzb[h], qh[h]) for h in two]
            dvs = [_dot_tn(w[h].astype(BF16), doh[h]) for h in two]
            dk_ref[pl.ds(off, tk), :] += dks[0] + dks[1]
            dv_ref[pl.ds(off, tk), :] += dvs[0] + dvs[1]
            return ps[0] + jnp.sum(g[0], axis=1, keepdims=True), ps[1] + jnp.sum(g[1], axis=1, keepdims=True), dq + (dqs[0] + dqs[1])

        zero = jnp.zeros((tq, 1), F32)
        carry = lax.fori_loop(0, i * per_q, lambda j, c: step(j, c, False), (zero, zero, jnp.zeros((tq, LANES), F32)))
        for t in range(per_q):
            carry = step(i * per_q + t, carry, True)
        dq_ref[...] = carry[2]

    seq_in = lambda blk0: pl.BlockSpec((None, S, LANES), lambda b, p, i: (b, 0, blk0 + p))
    blk = pl.BlockSpec((None, tq, LANES), lambda b, p, i: (b, i, p))
    seq_out = pl.BlockSpec((None, S, LANES), lambda b, p, i: (b, 0, p))
    shp = jax.ShapeDtypeStruct((Bl, S, 2 * LANES), F32)
    return call_with_plans(
        body, plans, name=name, grid=(Bl, 2, S // tq),
        in_specs=[pl.BlockSpec((None, tq, LANES), lambda b, p, i: (b, i, qb + p)), seq_in(kb0), seq_in(vb0), blk,
                  pl.BlockSpec((None, tq, LANES), lambda b, p, i: (b, i, do_blk0 + p))],
        out_specs=[blk, seq_out, seq_out], out_shape=[shp, shp, shp], scratch_shapes=[], args=[proj3, proj3, proj3, rt3, do3],
        sem=("arbitrary",) * 3 if plans else ("parallel", "parallel", "arbitrary"))


def mla_attn_fwd(q3, k3, kv3, vblk0, *, tq=512, tk=512, plans=None, name):
    Bl, S, _ = q3.shape
    tq = min(tq, S)
    tk = min(tk, tq)
    per_q = tq // tk
    scale = MLA_QK ** -0.5

    def body(q_ref, k_ref, v_ref, o_ref, lse_ref):
        i = pl.program_id(2)
        masks = _lane_masks()
        rr = lax.broadcasted_iota(jnp.int32, (tq, tk), 0)
        cc = lax.broadcasted_iota(jnp.int32, (tq, tk), 1)
        qh = [q_ref[:, h * LANES:(h + 1) * LANES] for h in range(2)]

        def step(j, carry):
            m0, l0, m1, l1, acc = carry
            off = pl.multiple_of(j * tk, tk)
            vb = v_ref[pl.ds(off, tk), :]
            causal = (cc + j * tk) <= (rr + i * tq)
            ms, ls, two = [m0, m1], [l0, l1], range(2)
            kh = [k_ref[pl.ds(off, tk), h * LANES:(h + 1) * LANES] for h in two]
            s = [jnp.where(causal, _dot_nt(qh[h], kh[h]) * scale, NEG) for h in two]
            m_new = [jnp.maximum(ms[h], jnp.max(s[h], axis=1, keepdims=True)) for h in two]
            p = [jnp.exp(s[h] - m_new[h]) for h in two]
            alpha = [jnp.exp(ms[h] - m_new[h]) for h in two]
            ls = [alpha[h] * ls[h] + jnp.sum(p[h], axis=1, keepdims=True) for h in two]
            add = [_dot(p[h].astype(BF16), jnp.where(masks[h], vb, 0.0).astype(BF16)) for h in two]
            acc = acc * jnp.where(masks[0], alpha[0], alpha[1]) + (add[0] + add[1])
            return m_new[0], ls[0], m_new[1], ls[1], acc

        neg = jnp.full((tq, 1), NEG, F32)
        zero = jnp.zeros((tq, 1), F32)
        m0, l0, m1, l1, acc = lax.fori_loop(0, (i + 1) * per_q, step, (neg, zero, neg, zero, jnp.zeros((tq, LANES), F32)))
        o_ref[...] = acc / jnp.where(masks[0], l0, l1)
        lse_ref[...] = jnp.where(masks[0], m0 + jnp.log(l0), m1 + jnp.log(l1))

    out = pl.BlockSpec((None, tq, LANES), lambda b, p, i: (b, i, p))
    shp = jax.ShapeDtypeStruct((Bl, S, 3 * LANES), F32)
    return call_with_plans(
        body, plans, name=name, grid=(Bl, 3, S // tq),
        in_specs=[pl.BlockSpec((None, tq, 2 * LANES), lambda b, p, i: (b, i, p)), pl.BlockSpec((None, S, 2 * LANES), lambda b, p, i: (b, 0, p)),
                  pl.BlockSpec((None, S, LANES), lambda b, p, i: (b, 0, vblk0 + p))],
        out_specs=[out, out], out_shape=[shp, shp], scratch_shapes=[], args=[q3, k3, kv3],
        sem=("arbitrary",) * 3 if plans else ("parallel", "parallel", "arbitrary"))


def mla_attn_bwd(q3, k3, kv3, vblk0, o3, lse3, do3, *, do_blk0=0, tq=512, tk=512, name):
    Bl, S, _ = q3.shape
    tq = min(tq, S)
    tk = min(tk, tq)
    per_q = tq // tk
    nq = S // tq
    scale = MLA_QK ** -0.5

    def body(q_ref, k_ref, v_ref, o_ref, lse_ref, do_ref, dq_ref, dk_ref, dv_ref, s_scr, dp_scr, p_scr, ds_scr):
        j = pl.program_id(2)

        @pl.when(j == 0)
        def _():
            dq_ref[...] = jnp.zeros_like(dq_ref)

        masks = _lane_masks()
        vb = v_ref[...]
        vh = [jnp.where(m, vb, 0.0).astype(BF16) for m in masks]
        kh = [k_ref[:, h * LANES:(h + 1) * LANES] for h in range(2)]
        i0 = lax.div(j, jnp.int32(per_q))

        def step(i, carry, masked):
            dk0, dk1, dv = carry
            off = pl.multiple_of(i * tq, tq)
            do_b = do_ref[pl.ds(off, tq), :].astype(BF16)
            prod = do_b.astype(F32) * o_ref[pl.ds(off, tq), :]
            lse = lse_ref[pl.ds(off, tq), :]
            two = range(2)
            qh = [q_ref[pl.ds(off, tq), h * LANES:(h + 1) * LANES] for h in two]
            doh = [jnp.where(masks[h], do_b, jnp.zeros_like(do_b)) for h in two]
            delta = [jnp.sum(jnp.where(masks[h], prod, 0.0), axis=1, keepdims=True) for h in two]
            lse_h = [lse[:, h * HEAD:h * HEAD + 1] for h in two]
            for h in two:
                s_scr[h] = _dot_nt(qh[h], kh[h])
            for h in two:
                dp_scr[h] = _dot_nt(doh[h], vh[h])
            for r0 in range(0, tq, STRIP):
                rows = slice(r0, r0 + STRIP)
                for h in two:
                    s = s_scr[h, rows, :] * scale
                    if masked:
                        rr = lax.broadcasted_iota(jnp.int32, (STRIP, tk), 0) + (i * tq + r0)
                        cc = lax.broadcasted_iota(jnp.int32, (STRIP, tk), 1) + j * tk
                        s = jnp.where(cc <= rr, s, NEG)
                    p = jnp.exp(s - lse_h[h][rows])
                    p_scr[h, rows, :] = p.astype(BF16)
                    ds_scr[h, rows, :] = (p * (dp_scr[h, rows, :] - delta[h][rows])).astype(BF16)
            dqs = [_dot(ds_scr[h], kh[h]) * scale for h in two]
            dks = [dk0 + _dot_tn(ds_scr[0], qh[0]), dk1 + _dot_tn(ds_scr[1], qh[1])]
            dv = dv + _dot_tn(p_scr[0], doh[0]) + _dot_tn(p_scr[1], doh[1])
            for h in two:
                dq_ref[pl.ds(off, tq), h * LANES:(h + 1) * LANES] += dqs[h]
            return dks[0], dks[1], dv

        zero = jnp.zeros((tk, LANES), F32)
        carry = step(i0, (zero, zero, zero), True)
        dk0, dk1, dv = lax.fori_loop(i0 + 1, nq, lambda i, c: step(i, c, False), carry)
        dk_ref[:, 0:LANES] = dk0 * scale
        dk_ref[:, LANES:2 * LANES] = dk1 * scale
        dv_ref[...] = dv.astype(BF16)

    seq1 = pl.BlockSpec((None, S, LANES), lambda b, p, j: (b, 0, p))
    seq2 = pl.BlockSpec((None, S, 2 * LANES), lambda b, p, j: (b, 0, p))
    return pl.pallas_call(
        body, name=name, grid=(Bl, 3, S // tk),
        in_specs=[seq2, pl.BlockSpec((None, tk, 2 * LANES), lambda b, p, j: (b, j, p)),
                  pl.BlockSpec((None, tk, LANES), lambda b, p, j: (b, j, vblk0 + p)), seq1, seq1,
                  pl.BlockSpec((None, S, LANES), lambda b, p, j: (b, 0, do_blk0 + p))],
        out_specs=[seq2, pl.BlockSpec((None, tk, 2 * LANES), lambda b, p, j: (b, j, p)), pl.BlockSpec((None, tk, LANES), lambda b, p, j: (b, j, p))],
        out_shape=[jax.ShapeDtypeStruct((Bl, S, 6 * LANES), F32), jax.ShapeDtypeStruct((Bl, S, 6 * LANES), F32), jax.ShapeDtypeStruct((Bl, S, 3 * LANES), BF16)],
        scratch_shapes=[pltpu.VMEM((2, tq, tk), F32), pltpu.VMEM((2, tq, tk), F32), pltpu.VMEM((2, tq, tk), BF16), pltpu.VMEM((2, tq, tk), BF16)],
        compiler_params=_cp("parallel", "parallel", "arbitrary"),
    )(q3, k3, kv3, o3, lse3, do3)


def _bucket_table():
    a = jnp.arange(WINDOW)[:, None]
    b = jnp.arange(2 * WINDOW)[None, :]
    dist = WINDOW + a - b
    max_exact = REL_BUCKETS // 2
    n = jnp.maximum(dist, 0)
    nf = jnp.maximum(n, 1).astype(F32)
    large = max_exact + (jnp.log(nf / max_exact) / math.log(REL_MAX_DIST / max_exact) * (REL_BUCKETS - max_exact)).astype(jnp.int32)
    large = jnp.minimum(large, REL_BUCKETS - 1)
    bucket = jnp.where(n < max_exact, n, large)
    return jnp.where((dist >= 0) & (dist < WINDOW), bucket, -1).astype(jnp.int32)


def swa_bias(rel_flat, bucket, *, name):
    def body(t_ref, b_ref, o_ref):
        bk = b_ref[...]
        for p in range(3):
            for hh in range(2):
                h = hh * 3 + p
                acc = jnp.full(bk.shape, NEG, F32)
                for b in range(REL_BUCKETS):
                    acc = jnp.where(bk == b, t_ref[b * 6 + h], acc)
                o_ref[p, hh] = acc

    return pl.pallas_call(
        body, name=name,
        in_specs=[pl.BlockSpec(memory_space=pltpu.SMEM), pl.BlockSpec(memory_space=pltpu.VMEM)],
        out_specs=pl.BlockSpec(memory_space=pltpu.VMEM),
        out_shape=jax.ShapeDtypeStruct((3, 2, WINDOW, 2 * WINDOW), F32),
    )(rel_flat, bucket)


def swa_bias_bwd(dbias, bucket, *, name):
    Bl = dbias.shape[0]

    def body(d_ref, b_ref, o_ref):
        bk = b_ref[...]
        lane = lax.broadcasted_iota(jnp.int32, (1, LANES), 1)
        rows = []
        for h in range(6):
            hh, p = divmod(h, 3)
            d = d_ref[0, p, hh]
            for bl in range(1, Bl):
                d = d + d_ref[bl, p, hh]
            row = jnp.zeros((1, LANES), F32)
            for b in range(REL_BUCKETS):
                s = jnp.sum(jnp.sum(jnp.where(bk == b, d, 0.0), axis=1, keepdims=True), axis=0, keepdims=True)
                row = row + jnp.where(lane == b, s, 0.0)
            rows.append(row)
        rows += [jnp.zeros((1, LANES), F32)] * 2
        o_ref[...] = jnp.concatenate(rows, axis=0)

    return pl.pallas_call(
        body, name=name,
        in_specs=[pl.BlockSpec(memory_space=pltpu.VMEM)] * 2, out_specs=pl.BlockSpec(memory_space=pltpu.VMEM),
        out_shape=jax.ShapeDtypeStruct((8, LANES), F32),
    )(dbias, bucket)


SWA_QBLOCKS = 8


def _swa_specs(vblk, nqb):
    rows = nqb * WINDOW
    cur = lambda blk: pl.BlockSpec((None, rows, LANES), lambda b, p, n: (b, n, blk))
    prev = lambda blk: pl.BlockSpec((None, WINDOW, LANES), lambda b, p, n: (b, jnp.maximum(n * nqb - 1, 0), blk))
    return [pl.BlockSpec((None, rows, LANES), lambda b, p, n: (b, n, p)), cur(0), prev(0), cur(vblk), prev(vblk),
            pl.BlockSpec((None, 2, WINDOW, 2 * WINDOW), lambda b, p, n: (p, 0, 0, 0)), pl.BlockSpec((None, 2, LANES), lambda b, p, n: (p, 0, 0))]


def _rows128(ref, m):
    return ref[m * WINDOW:(m + 1) * WINDOW, :]


def _swa_logits(qh, kp, kc, bias_h, first, scale):
    sp = jnp.where(first, NEG, _dot_nt(qh, kp) * scale + bias_h[:, :WINDOW])
    sc = _dot_nt(qh, kc) * scale + bias_h[:, WINDOW:]
    return sp, sc


def swa_attn_fwd(qn3, kn3, proj3, bias, sinks, *, plans=None, name):
    Bl, S, _ = qn3.shape
    scale = HEAD ** -0.5
    nqb = min(SWA_QBLOCKS, S // WINDOW)

    def body(q_ref, kc_ref, kp_ref, vc_ref, vp_ref, b_ref, s_ref, o_ref, lse_ref):
        seq_start = pl.program_id(2) == 0
        masks = _lane_masks()
        chains = [(m_, h) for m_ in range(nqb) for h in range(2)]
        kp = [kp_ref[...] if m_ == 0 else _rows128(kc_ref, m_ - 1) for m_ in range(nqb)]
        vp = [vp_ref[...] if m_ == 0 else _rows128(vc_ref, m_ - 1) for m_ in range(nqb)]
        kc = [_rows128(kc_ref, m_) for m_ in range(nqb)]
        vc = [_rows128(vc_ref, m_) for m_ in range(nqb)]
        sink = [s_ref[h:h + 1, 0:1] for h in range(2)]
        logits = {}
        for m_, h in chains:
            q = _rows128(q_ref, m_)
            qh = jnp.where(masks[h], q, jnp.zeros_like(q))
            logits[m_, h] = _swa_logits(qh, kp[m_], kc[m_], b_ref[h], seq_start if m_ == 0 else False, scale)
        mx = {c: jnp.maximum(jnp.maximum(jnp.max(logits[c][0], axis=1, keepdims=True), jnp.max(logits[c][1], axis=1, keepdims=True)), sink[c[1]])
              for c in chains}
        ex = {c: (jnp.exp(logits[c][0] - mx[c]), jnp.exp(logits[c][1] - mx[c])) for c in chains}
        den = {c: jnp.sum(ex[c][0], axis=1, keepdims=True) + jnp.sum(ex[c][1], axis=1, keepdims=True) + jnp.exp(sink[c[1]] - mx[c]) for c in chains}
        inv = {c: 1.0 / den[c] for c in chains}
        out = {}
        for m_, h in chains:
            c = (m_, h)
            out[c] = (_dot((ex[c][0] * inv[c]).astype(BF16), jnp.where(masks[h], vp[m_], 0.0).astype(BF16))
                      + _dot((ex[c][1] * inv[c]).astype(BF16), jnp.where(masks[h], vc[m_], 0.0).astype(BF16)))
        for m_ in range(nqb):
            o_ref[m_ * WINDOW:(m_ + 1) * WINDOW, :] = out[m_, 0] + out[m_, 1]
            lse_ref[m_ * WINDOW:(m_ + 1) * WINDOW, :] = jnp.where(masks[0], mx[m_, 0] + jnp.log(den[m_, 0]), mx[m_, 1] + jnp.log(den[m_, 1]))

    out = pl.BlockSpec((None, nqb * WINDOW, LANES), lambda b, p, n: (b, n, p))
    shp = jax.ShapeDtypeStruct((Bl, S, 3 * LANES), F32)
    return call_with_plans(
        body, plans, name=name, grid=(Bl, 3, S // (nqb * WINDOW)), in_specs=_swa_specs(P_SWV // LANES, nqb),
        out_specs=[out, out], out_shape=[shp, shp], scratch_shapes=[], args=[qn3, kn3, kn3, proj3, proj3, bias, sinks],
        sem=("arbitrary",) * 3 if plans else ("parallel", "parallel", "arbitrary"))


def swa_attn_bwd(qn3, kn3, proj3, bias, sinks, o3, lse3, do3, *, do_blk0=0, name):
    Bl, S, _ = qn3.shape
    scale = HEAD ** -0.5
    nqb = min(SWA_QBLOCKS, S // WINDOW)
    rows = nqb * WINDOW

    def body(q_ref, kc_ref, kp_ref, vc_ref, vp_ref, b_ref, s_ref, o_ref, lse_ref, do_ref,
             dq_ref, dk_ref, dv_ref, db_ref, dsk_ref):
        p_id, n = pl.program_id(1), pl.program_id(2)
        seq_start = n == 0

        @pl.when((p_id == 0) & seq_start)
        def _():
            dk_ref[...] = jnp.zeros_like(dk_ref)
            dv_ref[...] = jnp.zeros_like(dv_ref)

        @pl.when(seq_start)
        def _():
            db_ref[...] = jnp.zeros_like(db_ref)
            dsk_ref[...] = jnp.zeros_like(dsk_ref)

        masks = _lane_masks()
        zero = jnp.zeros((WINDOW, LANES), F32)
        chains = [(m_, h) for m_ in range(nqb) for h in range(2)]
        kp = [kp_ref[...] if m_ == 0 else _rows128(kc_ref, m_ - 1) for m_ in range(nqb)]
        vp = [vp_ref[...] if m_ == 0 else _rows128(vc_ref, m_ - 1) for m_ in range(nqb)]
        kc = [_rows128(kc_ref, m_) for m_ in range(nqb)]
        vc = [_rows128(vc_ref, m_) for m_ in range(nqb)]
        do_b = [_rows128(do_ref, m_).astype(BF16) for m_ in range(nqb)]
        prod = [do_b[m_].astype(F32) * _rows128(o_ref, m_) for m_ in range(nqb)]
        lse = [_rows128(lse_ref, m_) for m_ in range(nqb)]
        qh, doh, logits, lse_h, delta = {}, {}, {}, {}, {}
        for m_, h in chains:
            q = _rows128(q_ref, m_)
            qh[m_, h] = jnp.where(masks[h], q, jnp.zeros_like(q))
            doh[m_, h] = jnp.where(masks[h], do_b[m_], jnp.zeros_like(do_b[m_]))
            logits[m_, h] = _swa_logits(qh[m_, h], kp[m_], kc[m_], b_ref[h], seq_start if m_ == 0 else False, scale)
            lse_h[m_, h] = lse[m_][:, h * HEAD:h * HEAD + 1]
            delta[m_, h] = jnp.sum(jnp.where(masks[h], prod[m_], 0.0), axis=1, keepdims=True)
        pr = {c: (jnp.exp(logits[c][0] - lse_h[c]), jnp.exp(logits[c][1] - lse_h[c])) for c in chains}
        dp = {(m_, h): (_dot_nt(doh[m_, h], jnp.where(masks[h], vp[m_], 0.0).astype(BF16)),
                        _dot_nt(doh[m_, h], jnp.where(masks[h], vc[m_], 0.0).astype(BF16))) for m_, h in chains}
        ds = {c: (pr[c][0] * (dp[c][0] - delta[c]), pr[c][1] * (dp[c][1] - delta[c])) for c in chains}
        dsb = {c: ((ds[c][0] * scale).astype(BF16), (ds[c][1] * scale).astype(BF16)) for c in chains}
        dk_acc = [zero] * (nqb + 1)
        dv_acc = [zero] * (nqb + 1)
        db_acc = [[jnp.zeros((WINDOW, WINDOW), F32)] * 2 for _ in range(2)]
        dsk_acc = [jnp.zeros((1, 1), F32)] * 2
        dq = [zero] * nqb
        for m_, h in chains:
            c = (m_, h)
            db_acc[h] = [db_acc[h][0] + ds[c][0], db_acc[h][1] + ds[c][1]]
            dsk_acc[h] = dsk_acc[h] - jnp.sum(jnp.exp(s_ref[h:h + 1, 0:1] - lse_h[c]) * delta[c], axis=0, keepdims=True)
            dq[m_] = (dq[m_] + _dot(dsb[c][0], jnp.where(masks[h], kp[m_], jnp.zeros_like(kp[m_])))
                      + _dot(dsb[c][1], jnp.where(masks[h], kc[m_], jnp.zeros_like(kc[m_]))))
            dk_acc[m_] = dk_acc[m_] + _dot_tn(dsb[c][0], qh[c])
            dk_acc[m_ + 1] = dk_acc[m_ + 1] + _dot_tn(dsb[c][1], qh[c])
            dv_acc[m_] = dv_acc[m_] + _dot_tn(pr[c][0].astype(BF16), doh[c])
            dv_acc[m_ + 1] = dv_acc[m_ + 1] + _dot_tn(pr[c][1].astype(BF16), doh[c])
        for m_ in range(nqb):
            dq_ref[m_ * WINDOW:(m_ + 1) * WINDOW, :] = dq[m_]
        for h in range(2):
            db_ref[h, :, 0:WINDOW] += db_acc[h][0]
            db_ref[h, :, WINDOW:2 * WINDOW] += db_acc[h][1]
            dsk_ref[h:h + 1, :] += jnp.broadcast_to(dsk_acc[h], (1, LANES))
        offp = pl.multiple_of(jnp.maximum(n * nqb - 1, 0) * WINDOW, WINDOW)
        dk_ref[pl.ds(offp, WINDOW), :] += dk_acc[0]
        dv_ref[pl.ds(offp, WINDOW), :] += dv_acc[0]
        for m_ in range(nqb):
            off = pl.multiple_of(n * rows + m_ * WINDOW, WINDOW)
            dk_ref[pl.ds(off, WINDOW), :] += dk_acc[m_ + 1]
            dv_ref[pl.ds(off, WINDOW), :] += dv_acc[m_ + 1]

    blk = pl.BlockSpec((None, rows, LANES), lambda b, p, n: (b, n, p))
    seq = pl.BlockSpec((None, S, LANES), lambda b, p, n: (b, 0, 0))
    return pl.pallas_call(
        body, name=name, grid=(Bl, 3, S // rows),
        in_specs=_swa_specs(P_SWV // LANES, nqb) + [blk, blk, pl.BlockSpec((None, rows, LANES), lambda b, p, n: (b, n, do_blk0 + p))],
        out_specs=[blk, seq, seq, pl.BlockSpec((None, None, 2, WINDOW, 2 * WINDOW), lambda b, p, n: (b, p, 0, 0, 0)),
                   pl.BlockSpec((None, None, 2, LANES), lambda b, p, n: (b, p, 0, 0))],
        out_shape=[jax.ShapeDtypeStruct((Bl, S, 3 * LANES), F32), jax.ShapeDtypeStruct((Bl, S, LANES), F32), jax.ShapeDtypeStruct((Bl, S, LANES), F32),
                   jax.ShapeDtypeStruct((Bl, 3, 2, WINDOW, 2 * WINDOW), F32), jax.ShapeDtypeStruct((Bl, 3, 2, LANES), F32)],
        compiler_params=_cp("arbitrary", "arbitrary", "arbitrary"),
    )(qn3, kn3, kn3, proj3, proj3, bias, sinks, o3, lse3, do3)


CONV_ROWS = 64
CONV_LANES = 128


def _conv_strip(x_ref, h_ref, w, b, r0, cols, first_blk):
    x = x_ref[r0:r0 + CONV_ROWS, cols]
    if r0 == 0:
        rows = lax.broadcasted_iota(jnp.int32, x.shape, 0)
        h6 = jnp.where(first_blk, 0.0, h_ref[6:7, cols])
        h7 = jnp.where(first_blk, 0.0, h_ref[7:8, cols])
        x1 = jnp.where(rows == 0, h7, pltpu.roll(x, 1, 0))
        x2 = jnp.where(rows == 0, h6, jnp.where(rows == 1, h7, pltpu.roll(x, 2, 0)))
    else:
        x1 = x_ref[r0 - 1:r0 - 1 + CONV_ROWS, cols]
        x2 = x_ref[r0 - 2:r0 - 2 + CONV_ROWS, cols]
    return w[0:1] * x2 + w[1:2] * x1 + w[2:3] * x + b, x, x1, x2


FF_BLK = D_FF // 2


def _up_perm(a):
    q = FF_BLK
    return _cat([a[..., 0:q], a[..., 2 * q:3 * q], a[..., q:2 * q], a[..., 3 * q:4 * q]])


def conv_gate_fwd(up3, cw, cb, *, tm=512, name):
    Bl, S, _ = up3.shape
    tm = min(tm, S)
    W = 2 * FF_BLK

    def body(x_ref, h_ref, w_ref, b_ref, o_ref):
        first = pl.program_id(1) == 0

        def chunk(c, carry):
            cg = pl.ds(pl.multiple_of(c * CONV_LANES, CONV_LANES), CONV_LANES)
            cv = pl.ds(pl.multiple_of(FF_BLK + c * CONV_LANES, CONV_LANES), CONV_LANES)
            wg, wv, bg, bv = w_ref[:, cg], w_ref[:, cv], b_ref[:, cg], b_ref[:, cv]
            for r0 in range(0, tm, CONV_ROWS):
                ug = _conv_strip(x_ref, h_ref, wg, bg, r0, cg, first)[0]
                uv = _conv_strip(x_ref, h_ref, wv, bv, r0, cv, first)[0]
                o_ref[r0:r0 + CONV_ROWS, cg] = (ug * jax.nn.sigmoid(ug) * uv).astype(BF16)
            return carry

        lax.fori_loop(0, FF_BLK // CONV_LANES, chunk, 0)

    hb = tm // 8
    return pl.pallas_call(
        body, name=name, grid=(Bl, S // tm, 2),
        in_specs=[pl.BlockSpec((None, tm, W), lambda b, s, c: (b, s, c)),
                  pl.BlockSpec((None, 8, W), lambda b, s, c: (b, jnp.maximum(s * hb - 1, 0), c)),
                  pl.BlockSpec((3, W), lambda b, s, c: (0, c)), pl.BlockSpec((1, W), lambda b, s, c: (0, c))],
        out_specs=pl.BlockSpec((None, tm, FF_BLK), lambda b, s, c: (b, s, c)),
        out_shape=jax.ShapeDtypeStruct((Bl, S, D_FF), BF16),
        compiler_params=_cp("parallel", "parallel", "parallel"),
    )(up3, up3, cw, cb)


def conv_gate_bwd(up3, cw, cb, da3, *, tm=512, name):
    Bl, S, _ = up3.shape
    tm = min(tm, S)
    ns = S // tm
    W = 2 * FF_BLK

    def body(x_ref, h_ref, w_ref, b_ref, da_ref, dup_ref, dw_ref, nxt_ref, du_scr):
        b, s = pl.program_id(1), pl.program_id(2)
        seq_end = s == 0
        first = s == ns - 1

        @pl.when((b == 0) & seq_end)
        def _():
            dw_ref[...] = jnp.zeros_like(dw_ref)

        def du_chunk(c, carry):
            cg = pl.ds(pl.multiple_of(c * CONV_LANES, CONV_LANES), CONV_LANES)
            cv = pl.ds(pl.multiple_of(FF_BLK + c * CONV_LANES, CONV_LANES), CONV_LANES)
            wg, wv, bg, bv = w_ref[:, cg], w_ref[:, cv], b_ref[:, cg], b_ref[:, cv]
            acc_g = [jnp.zeros((1, CONV_LANES), F32)] * 4
            acc_v = [jnp.zeros((1, CONV_LANES), F32)] * 4
            for r0 in range(0, tm, CONV_ROWS):
                ug, xg, xg1, xg2 = _conv_strip(x_ref, h_ref, wg, bg, r0, cg, first)
                uv, xv, xv1, xv2 = _conv_strip(x_ref, h_ref, wv, bv, r0, cv, first)
                da = da_ref[r0:r0 + CONV_ROWS, cg].astype(F32)
                sg = jax.nn.sigmoid(ug)
                dug = da * uv * sg * (1.0 + ug * (1.0 - sg))
                duv = da * ug * sg
                du_scr[r0:r0 + CONV_ROWS, cg] = dug
                du_scr[r0:r0 + CONV_ROWS, cv] = duv
                col = lambda t: jnp.sum(t, axis=0, keepdims=True)
                acc_g = [acc_g[0] + col(dug * xg2), acc_g[1] + col(dug * xg1), acc_g[2] + col(dug * xg), acc_g[3] + col(dug)]
                acc_v = [acc_v[0] + col(duv * xv2), acc_v[1] + col(duv * xv1), acc_v[2] + col(duv * xv), acc_v[3] + col(duv)]
            for t in range(4):
                dw_ref[t:t + 1, cg] += acc_g[t]
                dw_ref[t:t + 1, cv] += acc_v[t]
            return carry

        lax.fori_loop(0, FF_BLK // CONV_LANES, du_chunk, 0)
        du_scr[tm:tm + 8, :] = jnp.where(seq_end, 0.0, nxt_ref[...])

        def dup_chunk(c, carry):
            cols = pl.ds(pl.multiple_of(c * CONV_LANES, CONV_LANES), CONV_LANES)
            w = w_ref[:, cols]
            for r0 in range(0, tm, CONV_ROWS):
                d0 = du_scr[r0:r0 + CONV_ROWS, cols]
                d1 = du_scr[r0 + 1:r0 + 1 + CONV_ROWS, cols]
                d2 = du_scr[r0 + 2:r0 + 2 + CONV_ROWS, cols]
                dup_ref[r0:r0 + CONV_ROWS, cols] = (w[2:3] * d0 + w[1:2] * d1 + w[0:1] * d2).astype(BF16)
            return carry

        lax.fori_loop(0, W // CONV_LANES, dup_chunk, 0)
        nxt_ref[...] = du_scr[0:8, :]

    hb = tm // 8
    rb = lambda s: ns - 1 - s
    return pl.pallas_call(
        body, name=name, grid=(2, Bl, ns),
        in_specs=[pl.BlockSpec((None, tm, W), lambda c, b, s: (b, rb(s), c)),
                  pl.BlockSpec((None, 8, W), lambda c, b, s: (b, jnp.maximum(rb(s) * hb - 1, 0), c)),
                  pl.BlockSpec((3, W), lambda c, b, s: (0, c)), pl.BlockSpec((1, W), lambda c, b, s: (0, c)),
                  pl.BlockSpec((None, tm, FF_BLK), lambda c, b, s: (b, rb(s), c))],
        out_specs=[pl.BlockSpec((None, tm, W), lambda c, b, s: (b, rb(s), c)), pl.BlockSpec((8, W), lambda c, b, s: (0, c))],
        out_shape=[jax.ShapeDtypeStruct((Bl, S, 2 * D_FF), BF16), jax.ShapeDtypeStruct((8, 2 * D_FF), F32)],
        scratch_shapes=[pltpu.VMEM((8, W), F32), pltpu.VMEM((tm + 8, W), F32)],
        compiler_params=_cp("arbitrary", "arbitrary", "arbitrary"),
    )(up3, up3, cw, cb, da3)


def cast_layer(w3, l, *, name):
    _, R, C = w3.shape
    tr = _tile(R, 512, 16)

    def body(w_ref, o_ref):
        o_ref[...] = w_ref[...].astype(BF16)

    return pl.pallas_call(
        body, name=name, grid=(R // tr,), in_specs=[pl.BlockSpec((None, tr, C), lambda i: (l, i, 0))],
        out_specs=pl.BlockSpec((tr, C), lambda i: (i, 0)), out_shape=jax.ShapeDtypeStruct((R, C), BF16),
        compiler_params=_cp("parallel"),
    )(w3)


def gate_bwd(dx3, y3, gate, *, tm=512, name):
    Bl, S, D = dx3.shape
    tm = min(tm, S)

    def body(dx_ref, y_ref, g_ref, o_ref, dg_ref):
        @pl.when(pl.program_id(1) == 0)
        def _():
            dg_ref[...] = jnp.zeros_like(dg_ref)

        dx = dx_ref[...]
        dg_ref[...] += jnp.sum(dx * y_ref[...], axis=0, keepdims=True)
        o_ref[...] = (dx * g_ref[...]).astype(BF16)

    blk = pl.BlockSpec((None, tm, D), lambda b, s: (b, s, 0))
    vec = pl.BlockSpec((None, 1, D), lambda b, s: (b, 0, 0))
    return pl.pallas_call(
        body, name=name, grid=(Bl, S // tm), in_specs=[blk, blk, vec], out_specs=[blk, vec],
        out_shape=[jax.ShapeDtypeStruct((Bl, S, D), BF16), jax.ShapeDtypeStruct((Bl, 1, D), F32)],
        compiler_params=_cp("parallel", "arbitrary"),
    )(dx3, y3, gate)


def loss_grad(y3, t3, *, tm=512, name):
    Bl, S, D = y3.shape
    tm = min(tm, S)
    last = (Bl - 1, S // tm - 1)

    def body(y_ref, t_ref, dy_ref, l_ref, acc_ref):
        b, s = pl.program_id(0), pl.program_id(1)

        @pl.when((b == 0) & (s == 0))
        def _():
            acc_ref[...] = jnp.zeros_like(acc_ref)

        e = y_ref[...] - t_ref[...]
        dy_ref[...] = e * (1.0 / D)
        acc_ref[...] += jnp.sum(e * e, axis=0, keepdims=True)

        @pl.when((b == last[0]) & (s == last[1]))
        def _():
            l_ref[...] = jnp.broadcast_to(jnp.sum(acc_ref[...], axis=1, keepdims=True) * (0.5 / D), (1, LANES))

    blk = pl.BlockSpec((None, tm, D), lambda b, s: (b, s, 0))
    return pl.pallas_call(
        body, name=name, grid=(Bl, S // tm), in_specs=[blk, blk],
        out_specs=[blk, pl.BlockSpec((1, LANES), lambda b, s: (0, 0))],
        out_shape=[jax.ShapeDtypeStruct((Bl, S, D), F32), jax.ShapeDtypeStruct((1, LANES), F32)],
        scratch_shapes=[pltpu.VMEM((1, D), F32)], compiler_params=_cp("arbitrary", "arbitrary"),
    )(y3, t3)


def adamw(w, g, m, v, *, name):
    L, R, C = w.shape
    tr = _tile(R, 512, 8)

    def body(w_ref, g_ref, m_ref, v_ref, d_ref, m2_ref, v2_ref):
        d_ref[...], m2_ref[...], v2_ref[...] = _adam_update(w_ref[...], g_ref[...], m_ref[...], v_ref[...])

    blk = pl.BlockSpec((None, tr, C), lambda l, i: (l, i, 0))
    shp = jax.ShapeDtypeStruct((L, R, C), F32)
    return pl.pallas_call(
        body, name=name, grid=(L, R // tr), in_specs=[blk] * 4, out_specs=[blk] * 3, out_shape=[shp] * 3,
        compiler_params=_cp("parallel", "parallel"),
    )(w, g, m, v)


def _adam_update(w, g, m, v):
    c1 = 1.0 / (1.0 - ADAM_B1 ** ADAM_STEP)
    c2 = 1.0 / (1.0 - ADAM_B2 ** ADAM_STEP)
    m2 = ADAM_B1 * m + (1.0 - ADAM_B1) * g
    v2 = ADAM_B2 * v + (1.0 - ADAM_B2) * (g * g)
    return -ADAM_LR * ((m2 * c1) / (jnp.sqrt(v2 * c2) + ADAM_EPS) + ADAM_WD * w), m2, v2


def adamw_small(ws, gs, ms, vs, *, name):
    na = len(ws)

    def body(*refs):
        w_r, g_r, m_r, v_r = (refs[i * na:(i + 1) * na] for i in range(4))
        d_r, m2_r, v2_r = (refs[(4 + i) * na:(5 + i) * na] for i in range(3))
        for a in range(na):
            d_r[a][...], m2_r[a][...], v2_r[a][...] = _adam_update(w_r[a][...], g_r[a][...], m_r[a][...], v_r[a][...])

    vm = pl.BlockSpec(memory_space=pltpu.VMEM)
    shp = [jax.ShapeDtypeStruct(w.shape, F32) for w in ws]
    out = pl.pallas_call(body, name=name, in_specs=[vm] * (4 * na), out_specs=[vm] * (3 * na), out_shape=shp * 3)(*ws, *gs, *ms, *vs)
    return out[:na], out[na:2 * na], out[2 * na:]


def sum_small(xs, *, name):
    na = len(xs)

    def body(*refs):
        for x_ref, o_ref in zip(refs[:na], refs[na:]):
            acc = x_ref[0]
            for k in range(1, x_ref.shape[0]):
                acc = acc + x_ref[k]
            o_ref[...] = acc

    vm = pl.BlockSpec(memory_space=pltpu.VMEM)
    return pl.pallas_call(body, name=name, in_specs=[vm] * na, out_specs=[vm] * na,
                          out_shape=[jax.ShapeDtypeStruct(x.shape[1:], x.dtype) for x in xs])(*xs)


def pair_add_half(g4, recv, c_arr, *, tr=512, name):
    _, R, C = g4.shape
    H = R // 2
    tr = _tile(H, tr, 16)
    nb = H // tr

    def body(c_ref, g_ref, r_ref, o_ref):
        o_ref[...] = (g_ref[...].astype(F32) + r_ref[...].astype(F32)).astype(BF16)

    grid_spec = pltpu.PrefetchScalarGridSpec(
        num_scalar_prefetch=1, grid=(4, nb),
        in_specs=[pl.BlockSpec((None, tr, C), lambda k, i, c_ref: (k, c_ref[0] * nb + i, 0)),
                  pl.BlockSpec((None, tr, C), lambda k, i, c_ref: (k, i, 0))],
        out_specs=pl.BlockSpec((None, tr, C), lambda k, i, c_ref: (k, i, 0)),
    )
    return pl.pallas_call(
        body, name=name, grid_spec=grid_spec, out_shape=jax.ShapeDtypeStruct((4, H, C), BF16),
        compiler_params=_cp("parallel", "parallel"),
    )(c_arr, g4, recv)


def chip_sum_into(landed, pair, sel, *, tr=512, name):
    _, H, C = landed.shape
    tr = _tile(H, tr, 16)
    nb = H // tr

    def body(s_ref, l0, l1, l2, l3, p_ref, o_ref):
        own = p_ref[...].astype(F32)
        acc = None
        for k, l_ref in enumerate((l0, l1, l2, l3)):
            part = jnp.where(s_ref[0] == k, own, l_ref[...].astype(F32))
            acc = part if acc is None else acc + part
        o_ref[...] = acc

    def slot(k):
        return pl.BlockSpec((None, tr, C), lambda i, s: (jnp.where(s[0] == k, (k + 1) % 4, k), i, 0))

    grid_spec = pltpu.PrefetchScalarGridSpec(
        num_scalar_prefetch=1, grid=(nb,),
        in_specs=[slot(0), slot(1), slot(2), slot(3), pl.BlockSpec((None, tr, C), lambda i, s: (s[0], i, 0))],
        out_specs=pl.BlockSpec((tr, C), lambda i, s: (s[1] * nb + i, 0)),
    )
    return pl.pallas_call(
        body, name=name, grid_spec=grid_spec, out_shape=jax.ShapeDtypeStruct((2 * H, C), F32), compiler_params=_cp("parallel"),
    )(sel, landed, landed, landed, landed, pair)


def mods_matmul(c_all, w_ada, b_ada_cols, *, tn=512, name):
    L, D, E = w_ada.shape
    nb = c_all.shape[0]
    tn = _tile(E, tn)

    def body(c_ref, w_ref, b_ref, o_ref):
        c = c_ref[...]
        a = c * jax.nn.sigmoid(c)
        o_ref[...] = jnp.dot(a, w_ref[...], preferred_element_type=F32, precision=lax.Precision.HIGHEST) + b_ref[...]

    return pl.pallas_call(
        body, name=name, grid=(L, E // tn),
        in_specs=[pl.BlockSpec((nb, D), lambda l, j: (0, 0)), pl.BlockSpec((None, D, tn), lambda l, j: (l, 0, j)),
                  pl.BlockSpec((None, 1, tn), lambda l, j: (l, 0, j))],
        out_specs=pl.BlockSpec((None, nb, tn), lambda l, j: (l, 0, j)),
        out_shape=jax.ShapeDtypeStruct((L, nb, E), F32), compiler_params=_cp("parallel", "parallel"),
    )(c_all, w_ada, b_ada_cols)


def ada_grad(c_all, dmods, *, tn=512, name):
    L, nb, E = dmods.shape
    D = c_all.shape[1]
    tn = _tile(E, tn)

    def body(c_ref, d_ref, o_ref):
        c = c_ref[...]
        a = c * jax.nn.sigmoid(c)
        o_ref[...] = lax.dot_general(a, d_ref[...], (((0,), (0,)), ((), ())), preferred_element_type=F32, precision=lax.Precision.HIGHEST)

    return pl.pallas_call(
        body, name=name, grid=(L, E // tn),
        in_specs=[pl.BlockSpec((nb, D), lambda l, j: (0, 0)), pl.BlockSpec((None, nb, tn), lambda l, j: (l, 0, j))],
        out_specs=pl.BlockSpec((None, D, tn), lambda l, j: (l, 0, j)),
        out_shape=jax.ShapeDtypeStruct((L, D, E), F32), compiler_params=_cp("parallel", "parallel"),
    )(c_all, dmods)


HBM = pl.BlockSpec(memory_space=pltpu.HBM)


def _me():
    return lax.axis_index("x"), lax.axis_index("y"), lax.axis_index("c")


def _flip(v, bit):
    return 1 - v if bit else v


def allgather8(xs, *, name):
    na = len(xs)

    def body(*refs):
        x_refs, out_refs = refs[:na], refs[na:2 * na]
        send_sems, recv_sems = refs[2 * na], refs[2 * na + 1]
        x, y, c = _me()
        me = 4 * x + 2 * y + c
        for x_ref, out_ref in zip(x_refs, out_refs):
            out_ref[me] = x_ref[...]
        sends = []
        for a, (x_ref, out_ref) in enumerate(zip(x_refs, out_refs)):
            for k in range(1, 8):
                peer = (_flip(x, k & 4), _flip(y, k & 2), _flip(c, k & 1))
                cp = pltpu.make_async_remote_copy(src_ref=x_ref, dst_ref=out_ref.at[me], send_sem=send_sems.at[a, k - 1],
                                                  recv_sem=recv_sems.at[a, k - 1], device_id=peer, device_id_type=MESH)
                cp.start()
                sends.append(cp)
        for a, (x_ref, out_ref) in enumerate(zip(x_refs, out_refs)):
            for k in range(1, 8):
                peer = (_flip(x, k & 4), _flip(y, k & 2), _flip(c, k & 1))
                src = 4 * peer[0] + 2 * peer[1] + peer[2]
                pltpu.make_async_remote_copy(src_ref=x_ref, dst_ref=out_ref.at[src], send_sem=send_sems.at[a, k - 1],
                                             recv_sem=recv_sems.at[a, k - 1], device_id=peer, device_id_type=MESH).wait_recv()
        for cp in sends:
            cp.wait_send()

    vm = pl.BlockSpec(memory_space=pltpu.VMEM)
    return pl.pallas_call(
        body, name=name, in_specs=[vm] * na, out_specs=[vm] * na,
        out_shape=[jax.ShapeDtypeStruct((8,) + a.shape, a.dtype) for a in xs],
        scratch_shapes=[pltpu.SemaphoreType.DMA((na, 7)), pltpu.SemaphoreType.DMA((na, 7))],
    )(*xs)


class _Plan:
    def __init__(self, ins, out_shapes, ncopies, copies, aliased=False):
        self.ins, self.out_shapes, self.ncopies, self.copies, self.aliased = list(ins), list(out_shapes), ncopies, copies, aliased

    def start(self, in_refs, out_refs, send_sems, recv_sems):
        sends, _ = self.copies(in_refs, out_refs, send_sems, recv_sems)
        for cp in sends:
            cp.start()

    def finish(self, in_refs, out_refs, send_sems, recv_sems):
        sends, recvs = self.copies(in_refs, out_refs, send_sems, recv_sems)
        for cp in recvs:
            cp.wait_recv()
        for cp in sends:
            cp.wait_send()


def _rcopy(src, dst, send_sems, recv_sems, idx, dev):
    return pltpu.make_async_remote_copy(src_ref=src, dst_ref=dst, send_sem=send_sems.at[idx], recv_sem=recv_sems.at[idx],
                                        device_id=dev, device_id_type=MESH)


def _other_chips(x, y):
    return [(_flip(x, k & 2), _flip(y, k & 1)) for k in range(1, 4)]


def plan_gather_ici(ws):
    def copies(in_refs, out_refs, ss, rs):
        x, y, c = _me()
        j = 2 * x + y
        sends, recvs = [], []
        for a, (x_ref, out_ref) in enumerate(zip(in_refs, out_refs)):
            H = x_ref.shape[0] // 2
            for k, (px, py) in enumerate(_other_chips(x, y)):
                sends.append(_rcopy(x_ref.at[pl.ds(c * H, H)], out_ref.at[j, pl.ds(c * H, H)], ss, rs, 3 * a + k, (px, py, c)))
                slot = out_ref.at[2 * px + py, pl.ds(c * H, H)]
                recvs.append(_rcopy(slot, slot, ss, rs, 3 * a + k, (px, py, c)))
        return sends, recvs

    return _Plan(ws, [jax.ShapeDtypeStruct((4,) + w.shape, w.dtype) for w in ws], 3 * len(ws), copies)


def plan_gather_d2d(w4s):
    def copies(in_refs, out_refs, ss, rs):
        x, y, c = _me()
        sends, recvs = [], []
        for a, out_ref in enumerate(out_refs):
            H = out_ref.shape[1] // 2
            for k, (px, py) in enumerate(_other_chips(x, y)):
                mine = out_ref.at[2 * px + py, pl.ds(c * H, H)]
                theirs = out_ref.at[2 * px + py, pl.ds((1 - c) * H, H)]
                sends.append(_rcopy(mine, mine, ss, rs, 3 * a + k, (x, y, 1 - c)))
                recvs.append(_rcopy(theirs, theirs, ss, rs, 3 * a + k, (x, y, 1 - c)))
        return sends, recvs

    return _Plan(w4s, [jax.ShapeDtypeStruct(w.shape, w.dtype) for w in w4s], 3 * len(w4s), copies, aliased=True)


def plan_swap_halves(gs):
    def copies(in_refs, out_refs, ss, rs):
        x, y, c = _me()
        sends, recvs = [], []
        for a, (g_ref, out_ref) in enumerate(zip(in_refs, out_refs)):
            H = g_ref.shape[1] // 2
            for k in range(4):
                sends.append(_rcopy(g_ref.at[k, pl.ds((1 - c) * H, H)], out_ref.at[k], ss, rs, 4 * a + k, (x, y, 1 - c)))
                recvs.append(_rcopy(g_ref.at[k, pl.ds(c * H, H)], out_ref.at[k], ss, rs, 4 * a + k, (x, y, 1 - c)))
        return sends, recvs

    return _Plan(gs, [jax.ShapeDtypeStruct((4, g.shape[1] // 2, g.shape[2]), g.dtype) for g in gs], 4 * len(gs), copies)


def plan_scatter_ici(ps):
    def copies(in_refs, out_refs, ss, rs):
        x, y, c = _me()
        j = 2 * x + y
        sends, recvs = [], []
        for a, (p_ref, out_ref) in enumerate(zip(in_refs, out_refs)):
            for k, (px, py) in enumerate(_other_chips(x, y)):
                sends.append(_rcopy(p_ref.at[2 * px + py], out_ref.at[j], ss, rs, 3 * a + k, (px, py, c)))
                slot = out_ref.at[2 * px + py]
                recvs.append(_rcopy(slot, slot, ss, rs, 3 * a + k, (px, py, c)))
        return sends, recvs

    return _Plan(ps, [jax.ShapeDtypeStruct(p.shape, p.dtype) for p in ps], 3 * len(ps), copies)


def plan_join_halves(fulls):
    def copies(in_refs, out_refs, ss, rs):
        x, y, c = _me()
        sends, recvs = [], []
        for a, out_ref in enumerate(out_refs):
            H = out_ref.shape[0] // 2
            mine, theirs = out_ref.at[pl.ds(c * H, H)], out_ref.at[pl.ds((1 - c) * H, H)]
            sends.append(_rcopy(mine, mine, ss, rs, a, (x, y, 1 - c)))
            recvs.append(_rcopy(theirs, theirs, ss, rs, a, (x, y, 1 - c)))
        return sends, recvs

    return _Plan(fulls, [jax.ShapeDtypeStruct(f.shape, f.dtype) for f in fulls], len(fulls), copies, aliased=True)


def call_with_plans(body, plans, *, grid, in_specs, out_specs, out_shape, scratch_shapes, args, sem, name):
    plans = list(plans or [])
    n_in, n_out, n_scr = len(in_specs), len(out_specs), len(scratch_shapes)
    c_in = [len(p.ins) for p in plans]
    c_out = [len(p.out_shapes) for p in plans]
    steps = math.prod(grid) if grid else 1

    def wrapped(*refs):
        pos = 0

        def take(n):
            nonlocal pos
            out = refs[pos:pos + n]
            pos += n
            return out

        ins = take(n_in)
        cins = [take(n) for n in c_in]
        outs = take(n_out)
        couts = [take(n) for n in c_out]
        scr = take(n_scr)
        sems = [take(2) for _ in plans]
        def start_all():
            for p, ci, co, (ss, rs) in zip(plans, cins, couts, sems):
                p.start(ci, co, ss, rs)

        def finish_all():
            for p, ci, co, (ss, rs) in zip(plans, cins, couts, sems):
                p.finish(ci, co, ss, rs)

        if plans and grid:
            idx = 0
            for ax, g in enumerate(grid):
                idx = idx * g + pl.program_id(ax)
            pl.when(idx == 0)(start_all)
        elif plans:
            start_all()
        if body is not None:
            body(*ins, *outs, *scr)
        if plans and grid:
            pl.when(idx == steps - 1)(finish_all)
        elif plans:
            finish_all()

    aliases = {}
    i_pos, o_pos = n_in, n_out
    for p, ni, no in zip(plans, c_in, c_out):
        if p.aliased:
            aliases.update({i_pos + t: o_pos + t for t in range(ni)})
        i_pos += ni
        o_pos += no
    kwargs = dict(grid=grid) if grid else {}
    if aliases:
        kwargs["input_output_aliases"] = aliases
    res = pl.pallas_call(
        wrapped, name=name, in_specs=list(in_specs) + [HBM] * sum(c_in), out_specs=list(out_specs) + [HBM] * sum(c_out),
        out_shape=list(out_shape) + [s for p in plans for s in p.out_shapes],
        scratch_shapes=list(scratch_shapes) + [pltpu.SemaphoreType.DMA((p.ncopies,)) for p in plans for _ in range(2)],
        compiler_params=_cp(*sem) if grid else pltpu.CompilerParams(vmem_limit_bytes=VMEM_LIMIT), **kwargs,
    )(*args, *[a for p in plans for a in p.ins])
    res = list(res)
    comp, rest = res[:n_out], res[n_out:]
    pouts = []
    for no in c_out:
        pouts.append(rest[:no])
        rest = rest[no:]
    return comp, pouts


def run_plans(plans, *, name):
    return call_with_plans(None, plans, grid=(), in_specs=[], out_specs=[], out_shape=[], scratch_shapes=[], args=[], sem=(), name=name)[1]


def _cat(parts, axis=-1):
    return jnp.concatenate(parts, axis=axis)


def _pairs_of_heads(a, axis, inverse=False):
    lead, tail = a.shape[:axis], a.shape[axis + 1:]
    split = (3, 2) if inverse else (2, 3)
    a = a.reshape(lead + split + (HEAD,) + tail)
    return jnp.swapaxes(a, axis, axis + 1).reshape(lead + (6 * HEAD,) + tail)


def _prep_w_in(w):
    z = lambda n: jnp.zeros((w.shape[0], n), w.dtype)
    return _cat([w[:, 0:1152], z(64), w[:, 1152:1184], z(32), _pairs_of_heads(w[:, 1184:1568], 1), w[:, 1568:1824]])


def _unprep_w_in(g):
    return _cat([g[:, 0:1152], g[:, 1216:1248], _pairs_of_heads(g[:, P_SWQ:P_SWK], 1, inverse=True), g[:, P_SWK:P_END]])


def _prep_w_uq(w):
    r = w.shape[0]
    return jnp.pad(w.reshape(r, 6, MLA_QK), ((0, 0), (0, 0), (0, LANES - MLA_QK))).reshape(r, 6 * LANES)


def _unprep_w_uq(g):
    r = g.shape[0]
    return g.reshape(r, 6, LANES)[:, :, :MLA_QK].reshape(r, 6 * MLA_QK)


def _prep_w_ukv(w):
    r = w.shape[0]
    w3 = w.reshape(r, 6, LANES)
    k = jnp.pad(w3[:, :, :HEAD], ((0, 0), (0, 0), (0, LANES - HEAD))).reshape(r, 6 * LANES)
    return _cat([k, w3[:, :, HEAD:].reshape(r, 6 * HEAD)])


def _unprep_w_ukv(g):
    r = g.shape[0]
    k = g[:, :6 * LANES].reshape(r, 6, LANES)[:, :, :HEAD]
    return _cat([k, g[:, 6 * LANES:].reshape(r, 6, HEAD)], axis=2).reshape(r, 6 * LANES)


def _prep_w_out(w):
    return _cat([w[0:640], _pairs_of_heads(w[640:], 0)], axis=0)


def _unprep_w_out(g):
    return _cat([g[0:640], _pairs_of_heads(g[640:], 0, inverse=True)], axis=0)


def _rope_tables(positions):
    half = 16
    inv_freq = jnp.power(ROPE_THETA, -jnp.arange(half, dtype=F32) / half)
    ang = positions.astype(F32)[..., None] * inv_freq
    cos, sin = jnp.cos(ang), jnp.sin(ang)
    z = lambda n: jnp.zeros(ang.shape[:-1] + (n,), F32)
    return (_cat([jnp.ones(ang.shape[:-1] + (HEAD,), F32), cos, cos, z(32)]), _cat([z(HEAD), -sin, z(16), z(32)]), _cat([z(HEAD), z(16), sin, z(32)]))


def _small_params(p):
    pad96 = lambda g: _cat([g, jnp.zeros((32,), F32)]).reshape(1, LANES)
    two = lambda g: _cat([g, g]).reshape(1, LANES)
    sinks = jnp.broadcast_to(p["sw_sinks"].reshape(2, 3).T[:, :, None], (3, 2, LANES))
    return dict(n1=p["norm1_g"].reshape(1, -1), n2=p["norm2_g"].reshape(1, -1), cq_g=p["mla_cq_g"].reshape(1, -1),
                ckv_g=p["mla_ckv_g"].reshape(1, -1), qn_g=pad96(p["mla_qn_g"]), kn_g=pad96(p["mla_kn_g"]),
                swq_g=two(p["sw_qn_g"]), swk_g=two(p["sw_kn_g"]), sinks=sinks, conv_b=_up_perm(p["conv_b"]).reshape(1, -1))


class _NoFlow:
    def plans(self, tag):
        return []

    def done(self, tag, outs):
        pass

    def add(self, key, g):
        pass


def _layer_fwd(x3, md, W, tabs, bias, tag, flow=_NoFlow()):
    Bl, S, D = x3.shape
    T = Bl * S
    n = lambda s: f"{s}_{tag}"
    two = lambda a: a.reshape(T, a.shape[-1])
    three = lambda a: a.reshape(Bl, S, a.shape[-1])
    h = rms_fwd(x3, 0, D, W["n1"], md["scale1"], md["shift1"], name=n("norm1"))
    proj = three(matmul(two(h), W["w_in"], tn=1920, name=n("in_proj")))
    (o_a, rt_a), got = sb_attn_fwd(proj, plans=flow.plans(n("sb_fwd")), name=n("sb_fwd"))
    flow.done(n("sb_fwd"), got)
    cqn = rms_fwd(proj, P_CQ // 256, 256, W["cq_g"], name=n("cq_norm"))
    ckvn = rms_fwd(proj, P_CKV // LANES, LANES, W["ckv_g"], name=n("ckv_norm"))
    qb = three(matmul(two(cqn), W["w_uq"], tm=1024, tn=768, name=n("uq")))
    kvb = three(matmul(two(ckvn), W["w_ukv"], tm=1024, tn=1152, name=n("ukv")))
    q_m = rope_norm_fwd(qb, 6, W["qn_g"], tabs, name=n("q_rope"))
    k_m = rope_norm_fwd(kvb, 6, W["kn_g"], tabs, (proj, P_SLAB // LANES), name=n("k_rope"))
    (o_b, lse_b), got = mla_attn_fwd(q_m, k_m, kvb, 6, plans=flow.plans(n("mla_fwd")), name=n("mla_fwd"))
    flow.done(n("mla_fwd"), got)
    q_c = pair_rms_fwd(proj, P_SWQ // LANES, 3, W["swq_g"], name=n("swq_norm"))
    k_c = pair_rms_fwd(proj, P_SWK // LANES, 1, W["swk_g"], name=n("swk_norm"))
    (o_c, lse_c), got = swa_attn_fwd(q_c, k_c, proj, bias, W["sinks"], plans=flow.plans(n("swa_fwd")), name=n("swa_fwd"))
    flow.done(n("swa_fwd"), got)
    mix = _cat([o_a, o_b, o_c]).astype(BF16)
    att, x1 = matmul_res(two(mix), W["w_out"], two(x3), md["gate1"], S, name=n("out_proj"))
    x1 = three(x1)
    h2 = rms_fwd(x1, 0, D, W["n2"], md["scale2"], md["shift2"], name=n("norm2"))
    up = three(matmul(two(h2), W["w_up"], tm=1024, tn=1408, name=n("up_proj")))
    a = conv_gate_fwd(up, W["conv_w"], W["conv_b"], name=n("conv_gate"))
    yd, x2 = matmul_res(two(a), W["w_down"], two(x1), md["gate2"], S, name=n("down_proj"))
    saved = dict(x=x3, h=h, proj=proj, rt_a=rt_a, cqn=cqn, ckvn=ckvn, qb=qb, kvb=kvb, q_m=q_m, k_m=k_m, o_b=o_b, lse_b=lse_b,
                 q_c=q_c, k_c=k_c, o_c=o_c, lse_c=lse_c, mix=mix, att=three(att), x1=x1, h2=h2, up=up, a=a, yd=three(yd))
    return three(x2), saved


def _layer_bwd(dx2, sv, md, W, tabs, bias, tag, flow=_NoFlow()):
    Bl, S, D = dx2.shape
    T = Bl * S
    n = lambda s: f"{s}_{tag}"
    two = lambda a: a.reshape(T, a.shape[-1])
    three = lambda a: a.reshape(Bl, S, a.shape[-1])
    g = {}
    dyb, dgate2 = gate_bwd(dx2, sv["yd"], md["gate2"], name=n("gate2_bwd"))
    da = three(matmul(two(dyb), W["w_down"], tb=True, tm=1024, tn=1408, name=n("down_dx")))
    g["w_down"] = matmul(two(sv["a"]), two(dyb), ta=True, tm=256, tn=1024, out_dtype=BF16, name=n("down_dw"))
    dup, dcw = conv_gate_bwd(sv["up"], W["conv_w"], W["conv_b"], da, name=n("conv_gate_bwd"))
    dh2 = three(matmul(two(dup), W["w_up"], tb=True, tn=1024, name=n("up_dx")))
    g["w_up"] = matmul(two(sv["h2"]), two(dup), ta=True, tn=1408, out_dtype=BF16, name=n("up_dw"))
    dx1, dn2, dsc2, dsh2 = rms_bwd(sv["x1"], 0, D, dh2, W["n2"], md["scale2"], dx2, name=n("norm2_bwd"))
    dmo, dgate1 = gate_bwd(dx1, sv["att"], md["gate1"], name=n("gate1_bwd"))
    dmix = three(matmul(two(dmo), W["w_out"], tb=True, tn=1024, out_dtype=BF16, name=n("out_dx")))
    g["w_out"] = matmul(two(sv["mix"]), two(dmo), ta=True, tn=1024, out_dtype=BF16, name=n("out_dw"))
    proj = sv["proj"]
    for k in ("w_down", "w_up", "w_out"):
        flow.add((tag, k), g[k])
    (dq_a, dk_a, dv_a), got = sb_attn_bwd(proj, sv["rt_a"], dmix, do_blk0=0, plans=flow.plans(n("sb_bwd")), name=n("sb_bwd"))
    flow.done(n("sb_bwd"), got)
    dq_m, dk_m, dv_b = mla_attn_bwd(sv["q_m"], sv["k_m"], sv["kvb"], 6, sv["o_b"], sv["lse_b"], dmix, do_blk0=2, name=n("mla_bwd"))
    dqb, dqn = rope_norm_bwd(sv["qb"], 6, dq_m, W["qn_g"], tabs, name=n("q_rope_bwd"))
    dkn_x, dkn, dslab = rope_norm_bwd(sv["kvb"], 6, dk_m, W["kn_g"], tabs, (proj, P_SLAB // LANES), name=n("k_rope_bwd"))
    dkvb = _cat([dkn_x, dv_b]).astype(BF16)
    dckvn = three(matmul(two(dkvb), W["w_ukv"], tb=True, tm=1024, name=n("ukv_dx")))
    g["w_ukv"] = matmul(two(sv["ckvn"]), two(dkvb), ta=True, tn=1152, out_dtype=BF16, name=n("ukv_dw"))
    dcqn = three(matmul(two(dqb), W["w_uq"], tb=True, tm=1024, name=n("uq_dx")))
    g["w_uq"] = matmul(two(sv["cqn"]), two(dqb), ta=True, tn=768, out_dtype=BF16, name=n("uq_dw"))
    dcq, dcq_g = rms_bwd(proj, P_CQ // 256, 256, dcqn, W["cq_g"], name=n("cq_norm_bwd"))
    dckv, dckv_g = rms_bwd(proj, P_CKV // LANES, LANES, dckvn, W["ckv_g"], name=n("ckv_norm_bwd"))
    dq_c, dk_c, dv_c, dbias, dsink = swa_attn_bwd(sv["q_c"], sv["k_c"], proj, bias, W["sinks"], sv["o_c"], sv["lse_c"], dmix, do_blk0=5, name=n("swa_bwd"))
    dswq, dswq_g = pair_rms_bwd(proj, P_SWQ // LANES, 3, dq_c, W["swq_g"], name=n("swq_norm_bwd"))
    dswk, dswk_g = pair_rms_bwd(proj, P_SWK // LANES, 1, dk_c, W["swk_g"], name=n("swk_norm_bwd"))
    dproj = _cat([dq_a, dk_a, dv_a, dcq, dckv, dslab, dswq, dswk, dv_c]).astype(BF16)
    dh = three(matmul(two(dproj), W["w_in"], tb=True, tn=1024, name=n("in_dx")))
    g["w_in"] = matmul(two(sv["h"]), two(dproj), ta=True, tn=1920, tk=2048, out_dtype=BF16, name=n("in_dw"))
    dx, dn1, dsc1, dsh1 = rms_bwd(sv["x"], 0, D, dh, W["n1"], md["scale1"], dx1, name=n("norm1_bwd"))
    small = dict(n1=dn1, n2=dn2, cq_g=dcq_g, ckv_g=dckv_g, qn_g=dqn, kn_g=dkn, swq_g=dswq_g, swk_g=dswk_g, conv=dcw)
    dmods = _cat([dsh1, dsc1, dgate1, dsh2, dsc2, dgate2]).reshape(Bl, 6 * D)
    for k in ("w_ukv", "w_uq", "w_in"):
        flow.add((tag, k), g[k])
    return dx, g, small, dmods, dbias, dsink


BIG = ("w_in", "w_uq", "w_ukv", "w_out", "w_up", "w_down")
ROW_SHARDED = ("w_out", "w_down")
PREP = dict(w_in=_prep_w_in, w_uq=_prep_w_uq, w_ukv=_prep_w_ukv, w_out=_prep_w_out, w_up=_up_perm, w_down=lambda w: w)
UNPREP = dict(w_in=_unprep_w_in, w_uq=_unprep_w_uq, w_ukv=_unprep_w_ukv, w_out=_unprep_w_out, w_up=_up_perm, w_down=lambda w: w)
NCHIPS = 4


def _local_step(x, target, positions, mods, Wl, rel_flat, fwd_flow=_NoFlow(), bwd_flow=_NoFlow()):
    Bl, S, D = x.shape
    L = len(Wl)
    tabs = _rope_tables(positions)
    bucket = _bucket_table()
    bias = swa_bias(rel_flat, bucket, name="swa_bias")
    mds = []
    for l in range(L):
        parts = [mods[l, :, D * k:D * (k + 1)].reshape(Bl, 1, D) for k in range(6)]
        mds.append(dict(zip(("shift1", "scale1", "gate1", "shift2", "scale2", "gate2"), parts)))
    saved = []
    h = x
    for l in range(L):
        h, sv = _layer_fwd(h, mds[l], Wl[l], tabs, bias, f"l{l}", fwd_flow)
        saved.append(sv)
    dy, loss = loss_grad(h, target, name="loss")
    grads, smalls, dmods, dbiases, dsinks = [None] * L, [None] * L, [None] * L, [None] * L, [None] * L
    for l in reversed(range(L)):
        dy, grads[l], smalls[l], dmods[l], dbiases[l], dsinks[l] = _layer_bwd(dy, saved[l], mds[l], Wl[l], tabs, bias, f"l{l}", bwd_flow)
    drel = swa_bias_bwd(_cat(dbiases, axis=0), bucket, name="swa_bias_bwd")
    return loss, dy, grads, smalls, dmods, dsinks, drel


ATT = ("w_in", "w_uq", "w_ukv", "w_out")
FFN = ("w_up", "w_down")
GATHER_STAGES = {
    "sb_fwd_l0": ([("l0", k) for k in ("w_out",) + FFN], []),
    "mla_fwd_l0": ([("l1", k) for k in ATT + ("w_up",)], [("l0", k) for k in ("w_out",) + FFN]),
    "swa_fwd_l0": ([("l1", "w_down")], [("l1", k) for k in ATT + ("w_up",)]),
    "sb_fwd_l1": ([], [("l1", "w_down")]),
}
SCATTER_STAGES = {
    "sb_bwd_l1": [("l1", k) for k in FFN],
    "sb_bwd_l0": [("l1", k) for k in ATT] + [("l0", k) for k in FFN + ("w_out",)],
}


class _GatherFlow:
    def __init__(self, shards, chip):
        self.shards, self.chip, self.ici, self.d2d, self.pending = shards, chip, {}, {}, {}

    def early(self, keys):
        ici, = run_plans([plan_gather_ici([self.shards[k] for k in keys])], name="gather_early_ici")
        d2d, = run_plans([plan_gather_d2d(ici)], name="gather_early_d2d")
        self.d2d.update(zip(keys, d2d))

    def plans(self, tag):
        ici_keys, d2d_keys = GATHER_STAGES.get(tag, ([], []))
        plans = []
        if d2d_keys:
            plans.append(plan_gather_d2d([self.ici[k] for k in d2d_keys]))
        if ici_keys:
            plans.append(plan_gather_ici([self.shards[k] for k in ici_keys]))
        self.pending[tag] = (ici_keys, d2d_keys)
        return plans

    def done(self, tag, outs):
        ici_keys, d2d_keys = self.pending.pop(tag, ([], []))
        outs = list(outs)
        if d2d_keys:
            self.d2d.update(zip(d2d_keys, outs.pop(0)))
        if ici_keys:
            self.ici.update(zip(ici_keys, outs.pop(0)))

    def weight(self, key):
        k = key[1]
        own = self.shards[key]
        r, cc = own.shape
        w4 = lax.dynamic_update_slice(self.d2d[key], own[None], (self.chip, 0, 0))
        fw = w4.reshape(NCHIPS * r, cc) if k in ROW_SHARDED else jnp.transpose(w4, (1, 0, 2)).reshape(r, NCHIPS * cc)
        return PREP[k](fw)


class _LayerWeights(dict):
    def __init__(self, small, flow, tag):
        super().__init__(small)
        self.flow, self.tag = flow, tag

    def __missing__(self, k):
        self[k] = self.flow.weight((self.tag, k))
        return self[k]


class _ScatterFlow:
    def __init__(self, shapes, sel, c_arr):
        self.shapes, self.sel, self.c_arr = shapes, sel, c_arr
        self.g, self.pairs, self.landed, self.pending = {}, {}, {}, {}

    def add(self, key, g):
        self.g[key] = g

    def _pairs(self, keys, label):
        g4s = []
        for key in keys:
            k = key[1]
            r, cc = self.shapes[k]
            gk = UNPREP[k](self.g[key])
            g4 = gk.reshape(NCHIPS, r, cc) if k in ROW_SHARDED else jnp.transpose(gk.reshape(r, NCHIPS, cc), (1, 0, 2))
            g4s.append(g4.astype(BF16))
        theirs, = run_plans([plan_swap_halves(g4s)], name=f"rs_swap_{label}")
        pairs = [pair_add_half(g4, th, self.c_arr, name=f"rs_pair_add_{key[1]}_{key[0]}") for key, g4, th in zip(keys, g4s, theirs)]
        self.pairs.update(zip(keys, pairs))
        return pairs

    def plans(self, tag):
        keys = SCATTER_STAGES.get(tag, [])
        self.pending[tag] = keys
        return [plan_scatter_ici(self._pairs(keys, tag))] if keys else []

    def done(self, tag, outs):
        keys = self.pending.pop(tag, [])
        if keys:
            self.landed.update(zip(keys, outs[0]))

    def finish(self):
        rest = [key for key in self.g if key not in self.pairs]
        if rest:
            landed, = run_plans([plan_scatter_ici(self._pairs(rest, "rest"))], name="rs_scatter_rest")
            self.landed.update(zip(rest, landed))
        keys = list(self.pairs)
        fulls = [chip_sum_into(self.landed[key], self.pairs[key], self.sel, name=f"rs_chip_sum_{key[1]}_{key[0]}") for key in keys]
        joined, = run_plans([plan_join_halves(fulls)], name="rs_join_halves")
        return dict(zip(keys, joined))


WEIGHTS = ("rel_table", "norm1_g", "norm2_g", "w_ada", "b_ada", "w_in", "mla_cq_g", "w_uq", "mla_ckv_g", "w_ukv", "mla_qn_g", "mla_kn_g",
           "sw_qn_g", "sw_kn_g", "sw_sinks", "w_out", "w_up", "conv_w", "conv_b", "w_down")
SMALL = tuple(n for n in WEIGHTS if n not in BIG + ("w_ada",))


def kernel(x, c, positions, rel_table, norm1_g, norm2_g, w_ada, b_ada, w_in, mla_cq_g, w_uq, mla_ckv_g, w_ukv, mla_qn_g, mla_kn_g, sw_qn_g, sw_kn_g, sw_sinks, w_out, w_up, conv_w, conv_b, w_down, loss_target, m_rel_table, m_norm1_g, m_norm2_g, m_w_ada, m_b_ada, m_w_in, m_mla_cq_g, m_w_uq, m_mla_ckv_g, m_w_ukv, m_mla_qn_g, m_mla_kn_g, m_sw_qn_g, m_sw_kn_g, m_sw_sinks, m_w_out, m_w_up, m_conv_w, m_conv_b, m_w_down, v_rel_table, v_norm1_g, v_norm2_g, v_w_ada, v_b_ada, v_w_in, v_mla_cq_g, v_w_uq, v_mla_ckv_g, v_w_ukv, v_mla_qn_g, v_mla_kn_g, v_sw_qn_g, v_sw_kn_g, v_sw_sinks, v_w_out, v_w_up, v_conv_w, v_conv_b, v_w_down):
    w = dict(rel_table=rel_table, norm1_g=norm1_g, norm2_g=norm2_g, w_ada=w_ada, b_ada=b_ada, w_in=w_in, mla_cq_g=mla_cq_g, w_uq=w_uq,
             mla_ckv_g=mla_ckv_g, w_ukv=w_ukv, mla_qn_g=mla_qn_g, mla_kn_g=mla_kn_g, sw_qn_g=sw_qn_g, sw_kn_g=sw_kn_g, sw_sinks=sw_sinks,
             w_out=w_out, w_up=w_up, conv_w=conv_w, conv_b=conv_b, w_down=w_down)
    m = dict(rel_table=m_rel_table, norm1_g=m_norm1_g, norm2_g=m_norm2_g, w_ada=m_w_ada, b_ada=m_b_ada, w_in=m_w_in, mla_cq_g=m_mla_cq_g,
             w_uq=m_w_uq, mla_ckv_g=m_mla_ckv_g, w_ukv=m_w_ukv, mla_qn_g=m_mla_qn_g, mla_kn_g=m_mla_kn_g, sw_qn_g=m_sw_qn_g,
             sw_kn_g=m_sw_kn_g, sw_sinks=m_sw_sinks, w_out=m_w_out, w_up=m_w_up, conv_w=m_conv_w, conv_b=m_conv_b, w_down=m_w_down)
    v = dict(rel_table=v_rel_table, norm1_g=v_norm1_g, norm2_g=v_norm2_g, w_ada=v_w_ada, b_ada=v_b_ada, w_in=v_w_in, mla_cq_g=v_mla_cq_g,
             w_uq=v_w_uq, mla_ckv_g=v_mla_ckv_g, w_ukv=v_w_ukv, mla_qn_g=v_mla_qn_g, mla_kn_g=v_mla_kn_g, sw_qn_g=v_sw_qn_g,
             sw_kn_g=v_sw_kn_g, sw_sinks=v_sw_sinks, w_out=v_w_out, w_up=v_w_up, conv_w=v_conv_w, conv_b=v_conv_b, w_down=v_w_down)
    Bl, S, D = x.shape
    L = norm1_g.shape[0]
    xi, yi, ci = _me()
    chip = 2 * xi + yi
    dev = 4 * xi + 2 * yi + ci
    ndev = 2 * NCHIPS

    shapes = {k: w[k].shape[1:] for k in BIG}
    shards = {(f"l{l}", k): cast_layer(w[k], l, name=f"cast_{k}_l{l}") for l in range(L) for k in BIG}
    gflow = _GatherFlow(shards, chip)
    gflow.early([("l0", k) for k in ("w_in", "w_uq", "w_ukv")])

    cw_cols = conv_w.shape[2]
    c_got, cw_got = allgather8([c, conv_w.reshape(L * 3, cw_cols)], name="gather_cond")
    c_all = c_got.reshape(ndev * Bl, D)
    conv_full = jnp.transpose(cw_got[0::2].reshape(NCHIPS, L, 3, cw_cols), (1, 2, 0, 3)).reshape(L, 3, NCHIPS * cw_cols)
    E = w_ada.shape[2]
    b_cols = lax.dynamic_slice(b_ada, (0, chip * E), (L, E)).reshape(L, 1, E)
    mods_cols = mods_matmul(c_all, w_ada, b_cols, name="mods")
    mods_all, = allgather8([mods_cols.reshape(L * ndev * Bl, E)], name="gather_mods")
    mods_all = jnp.transpose(mods_all[0::2].reshape(NCHIPS, L, ndev * Bl, E), (1, 2, 0, 3)).reshape(L, ndev * Bl, NCHIPS * E)
    mods = lax.dynamic_slice(mods_all, (0, dev * Bl, 0), (L, Bl, NCHIPS * E))

    Wl = []
    for l in range(L):
        Wd = _small_params({k: w[k][l] for k in SMALL if k not in ("rel_table", "b_ada", "conv_w")})
        Wd["conv_w"] = _up_perm(conv_full[l])
        Wl.append(_LayerWeights(Wd, gflow, f"l{l}"))

    sflow = _ScatterFlow(shapes, jnp.stack([chip, ci]).astype(jnp.int32), ci.reshape(1).astype(jnp.int32))
    loss, dx, _, smalls, dmods, dsinks, drel = _local_step(x, loss_target, positions, mods, Wl, rel_table.reshape(-1), gflow, sflow)
    reduced = sflow.finish()
    grad = {k: jnp.stack([reduced[(f"l{l}", k)] for l in range(L)]) for k in BIG}

    vec_names = ("n1", "n2", "cq_g", "ckv_g", "qn_g", "kn_g", "swq_g", "swk_g")
    vecs = _cat([_cat([smalls[l][k] for k in vec_names], axis=1) for l in range(L)], axis=0)
    convs = _cat([smalls[l]["conv"][0:4] for l in range(L)], axis=0)
    dm = jnp.stack(dmods, axis=1).reshape(Bl * L, 6 * D)
    dsk = jnp.stack(dsinks, axis=1).reshape(Bl * L * 6, LANES)
    got = allgather8([vecs, convs, drel, loss, dm, dsk], name="gather_small_grads")
    seq = lambda a, rows: a.reshape(ndev * Bl, rows, a.shape[-1])
    vec_s, conv_s, rel_s, loss_s, dm_s, dsk_s = sum_small(list(got[:4]) + [seq(got[4], L), seq(got[5], L * 6)], name="sum_small_grads")
    dm_all = jnp.transpose(seq(got[4], L), (1, 0, 2))
    grad["w_ada"] = ada_grad(c_all, lax.dynamic_slice(dm_all, (0, 0, chip * E), (L, ndev * Bl, E)), name="ada_grad")
    grad["b_ada"] = dm_s
    grad["sw_sinks"] = jnp.transpose(dsk_s.reshape(L, 3, 2, LANES)[:, :, :, 0], (0, 2, 1)).reshape(L, 6)
    grad["rel_table"] = rel_s[:6, :REL_BUCKETS].T
    off = 0
    for k, name_, keep in zip(vec_names, ("norm1_g", "norm2_g", "mla_cq_g", "mla_ckv_g", "mla_qn_g", "mla_kn_g", "sw_qn_g", "sw_kn_g"),
                              (D, D, 256, LANES, MLA_QK, MLA_QK, HEAD, HEAD)):
        grad[name_] = vec_s[:, off:off + keep]
        off += smalls[0][k].shape[1]
    conv = _up_perm(conv_s.reshape(L, 4, 2 * D_FF))
    grad["conv_w"] = lax.dynamic_slice(conv[:, 0:3], (0, 0, chip * cw_cols), (L, 3, cw_cols))
    grad["conv_b"] = conv[:, 3]
    loss_out = loss_s[0, 0]

    delta, new_m, new_v = {}, {}, {}
    for k in BIG + ("w_ada",):
        delta[k], new_m[k], new_v[k] = adamw(w[k], grad[k], m[k], v[k], name=f"adamw_{k}")
    outs = adamw_small(*[[src[k] for k in SMALL] for src in (w, grad, m, v)], name="adamw_small")
    for dst, o in zip((delta, new_m, new_v), outs):
        dst.update(dict(zip(SMALL, o)))
    return (loss_out, dx, *[grad[k] for k in WEIGHTS], *[delta[k] for k in WEIGHTS], *[new_m[k] for k in WEIGHTS], *[new_v[k] for k in WEIGHTS])
```
